```python
import math
import jax, jax.numpy as jnp
from jax import lax
import numpy as np

D_MODEL = 1024
BATCH = 8
SEQ = 8192
DEPTH = 1

MEM_LEN = 256

SB_HEADS = 8
SB_HEAD_DIM = D_MODEL // 16
SB_WIDTH = SB_HEADS * SB_HEAD_DIM
SB_BLOCK = 128

RET_HEADS = 4
RET_QK_DIM = D_MODEL // 8
RET_V_DIM = 2 * RET_QK_DIM
RET_QK_WIDTH = RET_HEADS * RET_QK_DIM
RET_V_WIDTH = RET_HEADS * RET_V_DIM
RET_CHUNK = 128
ROPE_BASE = 10000.0

N_BRANCHES = 2

OFF_SB_Q = 0
OFF_SB_K = OFF_SB_Q + SB_WIDTH
OFF_SB_V = OFF_SB_K + SB_WIDTH
OFF_RET_Q = OFF_SB_V + SB_WIDTH
OFF_RET_K = OFF_RET_Q + RET_QK_WIDTH
OFF_RET_V = OFF_RET_K + RET_QK_WIDTH
OFF_RET_G = OFF_RET_V + RET_V_WIDTH
OFF_GATE = OFF_RET_G + RET_V_WIDTH
IN_WIDTH = OFF_GATE + N_BRANCHES * D_MODEL

MEM_HEADS = 4
MEM_HEAD_DIM = D_MODEL // MEM_HEADS

FFN_HIDDEN = -(-8 * D_MODEL // (3 * 256)) * 256

DN_ALPHA = (2.0 * DEPTH) ** 0.25
DN_BETA = (8.0 * DEPTH) ** -0.25
LN_EPS = 1e-5

kernel_name = "hybrid_stickbreak_retention_deepnorm"


def layer_norm(x, g, b):
    xf = x.astype(jnp.float32)
    mu = jnp.mean(xf, -1, keepdims=True)
    var = jnp.mean(jnp.square(xf - mu), -1, keepdims=True)
    return ((xf - mu) * lax.rsqrt(var + LN_EPS)).astype(x.dtype) * g + b


def head_norm(x):
    xf = x.astype(jnp.float32)
    mu = jnp.mean(xf, -1, keepdims=True)
    var = jnp.mean(jnp.square(xf - mu), -1, keepdims=True)
    return (xf - mu) * lax.rsqrt(var + LN_EPS)


def rotary(x):
    S, d = x.shape[1], x.shape[-1]
    half = d // 2
    inv = 1.0 / (ROPE_BASE ** (jnp.arange(half, dtype=jnp.float32) / half))
    ang = jnp.arange(S, dtype=jnp.float32)[:, None] * inv[None, :]
    cos = jnp.cos(ang)[None, :, None, :]
    sin = jnp.sin(ang)[None, :, None, :]
    xf = x.astype(jnp.float32)
    x1, x2 = xf[..., :half], xf[..., half:]
    return jnp.concatenate([x1 * cos - x2 * sin, x1 * sin + x2 * cos], -1).astype(x.dtype)


def stick_breaking_attention(q, k, v):
    B, S, H, d = q.shape
    nb = S // SB_BLOCK
    qb = (q.astype(jnp.float32) * (d ** -0.5)).reshape(B, nb, SB_BLOCK, H, d).transpose(1, 0, 3, 2, 4)
    kf = k.astype(jnp.float32).transpose(0, 2, 1, 3)
    vf = v.astype(jnp.float32).transpose(0, 2, 1, 3)
    key_pos = jnp.arange(S)

    def one_block(args):
        q_blk, blk = args
        q_pos = blk * SB_BLOCK + jnp.arange(SB_BLOCK)
        mask = key_pos[None, :] < q_pos[:, None]
        z = jnp.einsum('bhqd,bhkd->bhqk', q_blk, kf)
        log_beta = jax.nn.log_sigmoid(z)
        log_rem = jnp.where(mask, jax.nn.log_sigmoid(-z), 0.0)
        later = lax.cumsum(log_rem, axis=3, reverse=True) - log_rem
        w = jnp.where(mask, jnp.exp(log_beta + later), 0.0)
        return jnp.einsum('bhqk,bhkd->bhqd', w, vf)

    out = lax.map(one_block, (qb, jnp.arange(nb)))
    return out.transpose(1, 0, 3, 2, 4).reshape(B, S, H * d)


def retention(q, k, v):
    B, S, H, dk = q.shape
    dv = v.shape[-1]
    nc = S // RET_CHUNK
    log_gamma = jnp.log1p(-jnp.exp2(-5.0 - jnp.arange(H, dtype=jnp.float32)))
    qc = (q.astype(jnp.float32) * (dk ** -0.5)).reshape(B, nc, RET_CHUNK, H, dk)
    kc = k.astype(jnp.float32).reshape(B, nc, RET_CHUNK, H, dk)
    vc = v.astype(jnp.float32).reshape(B, nc, RET_CHUNK, H, dv)
    idx = jnp.arange(RET_CHUNK, dtype=jnp.float32)
    rel = idx[:, None] - idx[None, :]
    decay = jnp.where(rel[None] >= 0,
                      jnp.exp(log_gamma[:, None, None] * jnp.maximum(rel, 0.0)[None]), 0.0)
    scores = jnp.einsum('bnihd,bnjhd->bnhij', qc, kc) * decay
    inner = jnp.einsum('bnhij,bnjhe->bnihe', scores, vc)
    k_decay = jnp.exp(log_gamma[None, :] * (RET_CHUNK - 1.0 - idx)[:, None])
    chunk_kv = jnp.einsum('bnjhd,jh,bnjhe->nbhde', kc, k_decay, vc)
    chunk_decay = jnp.exp(log_gamma * RET_CHUNK)[None, :, None, None]

    def step(state, kv):
        return state * chunk_decay + kv, state

    _, states = lax.scan(step, jnp.zeros((B, H, dk, dv), jnp.float32), chunk_kv)
    q_decay = jnp.exp(log_gamma[None, :] * (idx + 1.0)[:, None])
    cross = jnp.einsum('bnihd,ih,nbhde->bnihe', qc, q_decay, states)
    return (inner + cross).reshape(B, S, H, dv)


def memory_cross_attention(x, mem, w_q, w_kv, w_o):
    B, S, _ = x.shape
    q = (x @ w_q).reshape(B, S, MEM_HEADS, MEM_HEAD_DIM)
    kv = (mem @ w_kv).reshape(B, MEM_LEN, 2, MEM_HEADS, MEM_HEAD_DIM)
    k, v = kv[:, :, 0], kv[:, :, 1]
    s = jnp.einsum('bshd,bmhd->bhsm', q.astype(jnp.float32), k.astype(jnp.float32)) * (MEM_HEAD_DIM ** -0.5)
    p = jax.nn.softmax(s, axis=-1)
    o = jnp.einsum('bhsm,bmhd->bshd', p, v.astype(jnp.float32)).reshape(B, S, D_MODEL)
    return o.astype(x.dtype) @ w_o


def _fwd_setup_inputs(seed: int = 0) -> dict:
    key = jax.random.key(seed)
    ks = jax.random.split(key, 24)
    nrm = lambda k, shape: jax.random.normal(k, shape, jnp.float32)

    def dense(k, fan_in, fan_out, scale=1.0):
        return nrm(k, (DEPTH, fan_in, fan_out)) * (fan_in ** -0.5) * scale

    gain = lambda k: 1.0 + 0.02 * nrm(k, (DEPTH, D_MODEL))
    bias = lambda k, n: 0.02 * nrm(k, (DEPTH, n))

    col_scale = jnp.concatenate([
        jnp.ones((OFF_SB_V,), jnp.float32),
        jnp.full((SB_WIDTH,), DN_BETA, jnp.float32),
        jnp.ones((2 * RET_QK_WIDTH,), jnp.float32),
        jnp.full((RET_V_WIDTH,), DN_BETA, jnp.float32),
        jnp.ones((RET_V_WIDTH + N_BRANCHES * D_MODEL,), jnp.float32)])
    mem_kv_scale = jnp.concatenate([jnp.ones((D_MODEL,), jnp.float32),
                                    jnp.full((D_MODEL,), DN_BETA, jnp.float32)])
    return {
        "x": nrm(ks[0], (BATCH, SEQ, D_MODEL)),
        "mem": nrm(ks[1], (BATCH, MEM_LEN, D_MODEL)),
        "w_in": dense(ks[2], D_MODEL, IN_WIDTH) * col_scale,
        "b_gate": bias(ks[3], N_BRANCHES * D_MODEL),
        "w_sb_o": dense(ks[4], SB_WIDTH, D_MODEL),
        "w_ret_o": dense(ks[5], RET_V_WIDTH, D_MODEL),
        "w_mix_o": dense(ks[6], D_MODEL, D_MODEL, DN_BETA),
        "ln1_g": gain(ks[7]),
        "ln1_b": bias(ks[8], D_MODEL),
        "w_mem_q": dense(ks[9], D_MODEL, D_MODEL),
        "w_mem_kv": dense(ks[10], D_MODEL, 2 * D_MODEL) * mem_kv_scale,
        "w_mem_o": dense(ks[11], D_MODEL, D_MODEL, DN_BETA),
        "ln2_g": gain(ks[12]),
        "ln2_b": bias(ks[13], D_MODEL),
        "w_ffn_in": dense(ks[14], D_MODEL, 2 * FFN_HIDDEN, DN_BETA),
        "w_ffn_out": dense(ks[15], FFN_HIDDEN, D_MODEL, DN_BETA),
        "ln3_g": gain(ks[16]),
        "ln3_b": bias(ks[17], D_MODEL),
    }


def _fwd_reference(x, mem, w_in, b_gate, w_sb_o, w_ret_o, w_mix_o, ln1_g, ln1_b,
              w_mem_q, w_mem_kv, w_mem_o, ln2_g, ln2_b, w_ffn_in, w_ffn_out, ln3_g, ln3_b):
    B, S, _ = x.shape
    for l in range(DEPTH):
        h = x @ w_in[l]
        sb_q = h[..., OFF_SB_Q:OFF_SB_K].reshape(B, S, SB_HEADS, SB_HEAD_DIM)
        sb_k = h[..., OFF_SB_K:OFF_SB_V].reshape(B, S, SB_HEADS, SB_HEAD_DIM)
        sb_v = h[..., OFF_SB_V:OFF_RET_Q].reshape(B, S, SB_HEADS, SB_HEAD_DIM)
        r_q = rotary(h[..., OFF_RET_Q:OFF_RET_K].reshape(B, S, RET_HEADS, RET_QK_DIM))
        r_k = rotary(h[..., OFF_RET_K:OFF_RET_V].reshape(B, S, RET_HEADS, RET_QK_DIM))
        r_v = h[..., OFF_RET_V:OFF_RET_G].reshape(B, S, RET_HEADS, RET_V_DIM)
        r_g = h[..., OFF_RET_G:OFF_GATE]
        gates = jax.nn.sigmoid(h[..., OFF_GATE:] + b_gate[l]).reshape(B, S, N_BRANCHES, D_MODEL)

        y_sb = stick_breaking_attention(sb_q, sb_k, sb_v).astype(x.dtype) @ w_sb_o[l]
        ret = head_norm(retention(r_q, r_k, r_v)).reshape(B, S, RET_V_WIDTH)
        y_ret = (jax.nn.silu(r_g) * ret.astype(x.dtype)) @ w_ret_o[l]
        mix = (gates[:, :, 0] * y_sb + gates[:, :, 1] * y_ret) @ w_mix_o[l]
        x = layer_norm(DN_ALPHA * x + mix, ln1_g[l], ln1_b[l])

        xa = memory_cross_attention(x, mem, w_mem_q[l], w_mem_kv[l], w_mem_o[l])
        x = layer_norm(DN_ALPHA * x + xa, ln2_g[l], ln2_b[l])

        f = x @ w_ffn_in[l]
        ff = (jax.nn.silu(f[..., :FFN_HIDDEN]) * f[..., FFN_HIDDEN:]) @ w_ffn_out[l]
        x = layer_norm(DN_ALPHA * x + ff, ln3_g[l], ln3_b[l])
    return x


import jax as _jax
import jax.numpy as _jnp

TWIN_FORMAT = 'train_step'
FWD_PARAMS = ['x', 'mem', 'w_in', 'b_gate', 'w_sb_o', 'w_ret_o', 'w_mix_o', 'ln1_g', 'ln1_b', 'w_mem_q', 'w_mem_kv', 'w_mem_o', 'ln2_g', 'ln2_b', 'w_ffn_in', 'w_ffn_out', 'ln3_g', 'ln3_b']
TWIN_WEIGHTS = ['w_in', 'b_gate', 'w_sb_o', 'w_ret_o', 'w_mix_o', 'ln1_g', 'ln1_b', 'w_mem_q', 'w_mem_kv', 'w_mem_o', 'ln2_g', 'ln2_b', 'w_ffn_in', 'w_ffn_out', 'ln3_g', 'ln3_b']
TWIN_DIFF_INPUT = 'x'
TWIN_INPUTS = ['x', 'mem', 'w_in', 'b_gate', 'w_sb_o', 'w_ret_o', 'w_mix_o', 'ln1_g', 'ln1_b', 'w_mem_q', 'w_mem_kv', 'w_mem_o', 'ln2_g', 'ln2_b', 'w_ffn_in', 'w_ffn_out', 'ln3_g', 'ln3_b', 'loss_target', 'm_w_in', 'm_b_gate', 'm_w_sb_o', 'm_w_ret_o', 'm_w_mix_o', 'm_ln1_g', 'm_ln1_b', 'm_w_mem_q', 'm_w_mem_kv', 'm_w_mem_o', 'm_ln2_g', 'm_ln2_b', 'm_w_ffn_in', 'm_w_ffn_out', 'm_ln3_g', 'm_ln3_b', 'v_w_in', 'v_b_gate', 'v_w_sb_o', 'v_w_ret_o', 'v_w_mix_o', 'v_ln1_g', 'v_ln1_b', 'v_w_mem_q', 'v_w_mem_kv', 'v_w_mem_o', 'v_ln2_g', 'v_ln2_b', 'v_w_ffn_in', 'v_w_ffn_out', 'v_ln3_g', 'v_ln3_b']
TWIN_OUTPUTS = ['loss', 'grad_x', 'grad_w_in', 'grad_b_gate', 'grad_w_sb_o', 'grad_w_ret_o', 'grad_w_mix_o', 'grad_ln1_g', 'grad_ln1_b', 'grad_w_mem_q', 'grad_w_mem_kv', 'grad_w_mem_o', 'grad_ln2_g', 'grad_ln2_b', 'grad_w_ffn_in', 'grad_w_ffn_out', 'grad_ln3_g', 'grad_ln3_b', 'delta_w_in', 'delta_b_gate', 'delta_w_sb_o', 'delta_w_ret_o', 'delta_w_mix_o', 'delta_ln1_g', 'delta_ln1_b', 'delta_w_mem_q', 'delta_w_mem_kv', 'delta_w_mem_o', 'delta_ln2_g', 'delta_ln2_b', 'delta_w_ffn_in', 'delta_w_ffn_out', 'delta_ln3_g', 'delta_ln3_b', 'new_m_w_in', 'new_m_b_gate', 'new_m_w_sb_o', 'new_m_w_ret_o', 'new_m_w_mix_o', 'new_m_ln1_g', 'new_m_ln1_b', 'new_m_w_mem_q', 'new_m_w_mem_kv', 'new_m_w_mem_o', 'new_m_ln2_g', 'new_m_ln2_b', 'new_m_w_ffn_in', 'new_m_w_ffn_out', 'new_m_ln3_g', 'new_m_ln3_b', 'new_v_w_in', 'new_v_b_gate', 'new_v_w_sb_o', 'new_v_w_ret_o', 'new_v_w_mix_o', 'new_v_ln1_g', 'new_v_ln1_b', 'new_v_w_mem_q', 'new_v_w_mem_kv', 'new_v_w_mem_o', 'new_v_ln2_g', 'new_v_ln2_b', 'new_v_w_ffn_in', 'new_v_w_ffn_out', 'new_v_ln3_g', 'new_v_ln3_b']
TWIN_LEAF_KINDS = {'loss': 'loss', 'grad_x': 'grad_x', 'grad_w_in': 'grad_w', 'grad_b_gate': 'grad_w', 'grad_w_sb_o': 'grad_w', 'grad_w_ret_o': 'grad_w', 'grad_w_mix_o': 'grad_w', 'grad_ln1_g': 'grad_w', 'grad_ln1_b': 'grad_w', 'grad_w_mem_q': 'grad_w', 'grad_w_mem_kv': 'grad_w', 'grad_w_mem_o': 'grad_w', 'grad_ln2_g': 'grad_w', 'grad_ln2_b': 'grad_w', 'grad_w_ffn_in': 'grad_w', 'grad_w_ffn_out': 'grad_w', 'grad_ln3_g': 'grad_w', 'grad_ln3_b': 'grad_w', 'delta_w_in': 'delta_w', 'delta_b_gate': 'delta_w', 'delta_w_sb_o': 'delta_w', 'delta_w_ret_o': 'delta_w', 'delta_w_mix_o': 'delta_w', 'delta_ln1_g': 'delta_w', 'delta_ln1_b': 'delta_w', 'delta_w_mem_q': 'delta_w', 'delta_w_mem_kv': 'delta_w', 'delta_w_mem_o': 'delta_w', 'delta_ln2_g': 'delta_w', 'delta_ln2_b': 'delta_w', 'delta_w_ffn_in': 'delta_w', 'delta_w_ffn_out': 'delta_w', 'delta_ln3_g': 'delta_w', 'delta_ln3_b': 'delta_w', 'new_m_w_in': 'new_m', 'new_m_b_gate': 'new_m', 'new_m_w_sb_o': 'new_m', 'new_m_w_ret_o': 'new_m', 'new_m_w_mix_o': 'new_m', 'new_m_ln1_g': 'new_m', 'new_m_ln1_b': 'new_m', 'new_m_w_mem_q': 'new_m', 'new_m_w_mem_kv': 'new_m', 'new_m_w_mem_o': 'new_m', 'new_m_ln2_g': 'new_m', 'new_m_ln2_b': 'new_m', 'new_m_w_ffn_in': 'new_m', 'new_m_w_ffn_out': 'new_m', 'new_m_ln3_g': 'new_m', 'new_m_ln3_b': 'new_m', 'new_v_w_in': 'new_v', 'new_v_b_gate': 'new_v', 'new_v_w_sb_o': 'new_v', 'new_v_w_ret_o': 'new_v', 'new_v_w_mix_o': 'new_v', 'new_v_ln1_g': 'new_v', 'new_v_ln1_b': 'new_v', 'new_v_w_mem_q': 'new_v', 'new_v_w_mem_kv': 'new_v', 'new_v_w_mem_o': 'new_v', 'new_v_ln2_g': 'new_v', 'new_v_ln2_b': 'new_v', 'new_v_w_ffn_in': 'new_v', 'new_v_w_ffn_out': 'new_v', 'new_v_ln3_g': 'new_v', 'new_v_ln3_b': 'new_v'}


def _forward(args):
    return _fwd_reference(*[args[k] for k in FWD_PARAMS])


def _output_shape():
    def fwd():
        inp = _fwd_setup_inputs(0)
        return _fwd_reference(*[inp[k] for k in FWD_PARAMS])
    out = _jax.eval_shape(fwd)
    return out.shape, out.dtype

N_MICROBATCH = 1
ADAM_LR = 0.001
ADAM_B1 = 0.9
ADAM_B2 = 0.999
ADAM_EPS = 1e-08
ADAM_WD = 0.01
ADAM_STEP = 10
PER_EXAMPLE_BATCH_AXIS = {'x': 0, 'mem': 0, 'loss_target': 0}
SHARED_INPUTS = []
_WEIGHT_DTYPES = {'w_in': _jnp.float32, 'b_gate': _jnp.float32, 'w_sb_o': _jnp.float32, 'w_ret_o': _jnp.float32, 'w_mix_o': _jnp.float32, 'ln1_g': _jnp.float32, 'ln1_b': _jnp.float32, 'w_mem_q': _jnp.float32, 'w_mem_kv': _jnp.float32, 'w_mem_o': _jnp.float32, 'ln2_g': _jnp.float32, 'ln2_b': _jnp.float32, 'w_ffn_in': _jnp.float32, 'w_ffn_out': _jnp.float32, 'ln3_g': _jnp.float32, 'ln3_b': _jnp.float32}
MOMENT_SCALE = {'w_in': 4.342598e-02, 'b_gate': 1.485520e-02, 'w_sb_o': 2.578735e-02, 'w_ret_o': 4.000361e-02, 'w_mix_o': 7.976976e-02, 'ln1_g': 2.256044e+00, 'ln1_b': 1.050846e+00, 'w_mem_q': 7.881317e-03, 'w_mem_kv': 1.216771e-02, 'w_mem_o': 1.514410e-02, 'ln2_g': 2.256756e+00, 'ln2_b': 1.044120e+00, 'w_ffn_in': 2.500902e-02, 'w_ffn_out': 4.092385e-02, 'ln3_g': 6.412045e+01, 'ln3_b': 1.751872e+00}


def _to_microbatches(a, axis):
    t = _jnp.moveaxis(a, axis, 0)
    t = t.reshape((N_MICROBATCH, t.shape[0] // N_MICROBATCH) + t.shape[1:])
    return _jnp.moveaxis(t, 1, axis + 1)


def setup_inputs(seed: int = 0) -> dict:
    inp = _fwd_setup_inputs(seed)
    key = _jax.random.fold_in(_jax.random.key(seed), 7919)
    shape, _ = _output_shape()
    out = dict(inp)
    out["loss_target"] = _jax.random.normal(_jax.random.fold_in(key, 0), shape, _jnp.float32)
    for i, name in enumerate(TWIN_WEIGHTS):
        w = inp[name].astype(_jnp.float32)
        if MOMENT_SCALE is None:
            s = _jnp.sqrt(_jnp.mean(_jnp.square(w)) + 1e-30)
        else:
            s = MOMENT_SCALE[name]
        km, kv = _jax.random.split(_jax.random.fold_in(key, i + 1))
        out[name] = w
        out["m_" + name] = s * _jax.random.normal(km, w.shape, _jnp.float32)
        out["v_" + name] = (s * s) * _jax.random.uniform(kv, w.shape, _jnp.float32, 0.5, 1.5)
    if N_MICROBATCH > 1:
        for name, axis in PER_EXAMPLE_BATCH_AXIS.items():
            out[name] = _to_microbatches(out[name], axis)
    return {'x': out['x'], 'mem': out['mem'], 'w_in': out['w_in'], 'b_gate': out['b_gate'], 'w_sb_o': out['w_sb_o'], 'w_ret_o': out['w_ret_o'], 'w_mix_o': out['w_mix_o'], 'ln1_g': out['ln1_g'], 'ln1_b': out['ln1_b'], 'w_mem_q': out['w_mem_q'], 'w_mem_kv': out['w_mem_kv'], 'w_mem_o': out['w_mem_o'], 'ln2_g': out['ln2_g'], 'ln2_b': out['ln2_b'], 'w_ffn_in': out['w_ffn_in'], 'w_ffn_out': out['w_ffn_out'], 'ln3_g': out['ln3_g'], 'ln3_b': out['ln3_b'], 'loss_target': out['loss_target'], 'm_w_in': out['m_w_in'], 'm_b_gate': out['m_b_gate'], 'm_w_sb_o': out['m_w_sb_o'], 'm_w_ret_o': out['m_w_ret_o'], 'm_w_mix_o': out['m_w_mix_o'], 'm_ln1_g': out['m_ln1_g'], 'm_ln1_b': out['m_ln1_b'], 'm_w_mem_q': out['m_w_mem_q'], 'm_w_mem_kv': out['m_w_mem_kv'], 'm_w_mem_o': out['m_w_mem_o'], 'm_ln2_g': out['m_ln2_g'], 'm_ln2_b': out['m_ln2_b'], 'm_w_ffn_in': out['m_w_ffn_in'], 'm_w_ffn_out': out['m_w_ffn_out'], 'm_ln3_g': out['m_ln3_g'], 'm_ln3_b': out['m_ln3_b'], 'v_w_in': out['v_w_in'], 'v_b_gate': out['v_b_gate'], 'v_w_sb_o': out['v_w_sb_o'], 'v_w_ret_o': out['v_w_ret_o'], 'v_w_mix_o': out['v_w_mix_o'], 'v_ln1_g': out['v_ln1_g'], 'v_ln1_b': out['v_ln1_b'], 'v_w_mem_q': out['v_w_mem_q'], 'v_w_mem_kv': out['v_w_mem_kv'], 'v_w_mem_o': out['v_w_mem_o'], 'v_ln2_g': out['v_ln2_g'], 'v_ln2_b': out['v_ln2_b'], 'v_w_ffn_in': out['v_w_ffn_in'], 'v_w_ffn_out': out['v_w_ffn_out'], 'v_ln3_g': out['v_ln3_g'], 'v_ln3_b': out['v_ln3_b']}


def _loss(weights, diff, rest, loss_target):
    with _jax.named_scope("forward"):
        args = {**rest, TWIN_DIFF_INPUT: diff, **{k: w.astype(_WEIGHT_DTYPES[k]) for k, w in weights.items()}}
        y = _forward(args)
    with _jax.named_scope("loss_head"):
        err = _jnp.square(y.astype(_jnp.float32) - loss_target)
        return 0.5 * _jnp.sum(_jnp.mean(err, axis=-1)) if err.ndim else 0.5 * err


def _adamw(w, g, m, v):
    m = ADAM_B1 * m + (1.0 - ADAM_B1) * g
    v = ADAM_B2 * v + (1.0 - ADAM_B2) * _jnp.square(g)
    m_hat = m / (1.0 - ADAM_B1 ** ADAM_STEP)
    v_hat = v / (1.0 - ADAM_B2 ** ADAM_STEP)
    delta = -ADAM_LR * (m_hat / (_jnp.sqrt(v_hat) + ADAM_EPS) + ADAM_WD * w)
    return delta, m, v


def reference(x, mem, w_in, b_gate, w_sb_o, w_ret_o, w_mix_o, ln1_g, ln1_b, w_mem_q, w_mem_kv, w_mem_o, ln2_g, ln2_b, w_ffn_in, w_ffn_out, ln3_g, ln3_b, loss_target, m_w_in, m_b_gate, m_w_sb_o, m_w_ret_o, m_w_mix_o, m_ln1_g, m_ln1_b, m_w_mem_q, m_w_mem_kv, m_w_mem_o, m_ln2_g, m_ln2_b, m_w_ffn_in, m_w_ffn_out, m_ln3_g, m_ln3_b, v_w_in, v_b_gate, v_w_sb_o, v_w_ret_o, v_w_mix_o, v_ln1_g, v_ln1_b, v_w_mem_q, v_w_mem_kv, v_w_mem_o, v_ln2_g, v_ln2_b, v_w_ffn_in, v_w_ffn_out, v_ln3_g, v_ln3_b):
    given = dict(x=x, mem=mem, w_in=w_in, b_gate=b_gate, w_sb_o=w_sb_o, w_ret_o=w_ret_o, w_mix_o=w_mix_o, ln1_g=ln1_g, ln1_b=ln1_b, w_mem_q=w_mem_q, w_mem_kv=w_mem_kv, w_mem_o=w_mem_o, ln2_g=ln2_g, ln2_b=ln2_b, w_ffn_in=w_ffn_in, w_ffn_out=w_ffn_out, ln3_g=ln3_g, ln3_b=ln3_b, loss_target=loss_target, m_w_in=m_w_in, m_b_gate=m_b_gate, m_w_sb_o=m_w_sb_o, m_w_ret_o=m_w_ret_o, m_w_mix_o=m_w_mix_o, m_ln1_g=m_ln1_g, m_ln1_b=m_ln1_b, m_w_mem_q=m_w_mem_q, m_w_mem_kv=m_w_mem_kv, m_w_mem_o=m_w_mem_o, m_ln2_g=m_ln2_g, m_ln2_b=m_ln2_b, m_w_ffn_in=m_w_ffn_in, m_w_ffn_out=m_w_ffn_out, m_ln3_g=m_ln3_g, m_ln3_b=m_ln3_b, v_w_in=v_w_in, v_b_gate=v_b_gate, v_w_sb_o=v_w_sb_o, v_w_ret_o=v_w_ret_o, v_w_mix_o=v_w_mix_o, v_ln1_g=v_ln1_g, v_ln1_b=v_ln1_b, v_w_mem_q=v_w_mem_q, v_w_mem_kv=v_w_mem_kv, v_w_mem_o=v_w_mem_o, v_ln2_g=v_ln2_g, v_ln2_b=v_ln2_b, v_w_ffn_in=v_w_ffn_in, v_w_ffn_out=v_w_ffn_out, v_ln3_g=v_ln3_g, v_ln3_b=v_ln3_b)
    weights = {n: given[n] for n in TWIN_WEIGHTS}
    shared = {n: given[n] for n in SHARED_INPUTS}
    per_example = {n: given[n] for n in ['x', 'mem']}
    grad_fn = _jax.value_and_grad(_loss, argnums=(0, 1))

    def one_microbatch(ex, loss_target):
        ex = dict(ex)
        diff = ex.pop(TWIN_DIFF_INPUT)
        return grad_fn(weights, diff, {**shared, **ex}, loss_target)

    if N_MICROBATCH == 1:
        loss, (grad_w, grad_x) = one_microbatch(per_example, given["loss_target"])
    else:
        def body(carry, xs):
            loss_sum, grad_sum = carry
            l_k, (gw_k, gx_k) = one_microbatch(xs[0], xs[1])
            with _jax.named_scope("update"):
                return (loss_sum + l_k, _jax.tree.map(_jnp.add, grad_sum, gw_k)), gx_k

        init = (_jnp.zeros((), _jnp.float32), _jax.tree.map(_jnp.zeros_like, weights))
        (loss, grad_w), grad_x = _jax.lax.scan(body, init, (per_example, given["loss_target"]))
    with _jax.named_scope("update"):
        delta_w, new_m, new_v = {}, {}, {}
        for n in TWIN_WEIGHTS:
            delta_w[n], new_m[n], new_v[n] = _adamw(weights[n], grad_w[n], given["m_" + n], given["v_" + n])
    return (loss, grad_x, *[grad_w[n] for n in TWIN_WEIGHTS], *[delta_w[n] for n in TWIN_WEIGHTS],
            *[new_m[n] for n in TWIN_WEIGHTS], *[new_v[n] for n in TWIN_WEIGHTS])
```

```python
import functools

import jax
import jax.numpy as jnp
import numpy as np
from jax import lax
from jax.experimental import pallas as pl
from jax.experimental.pallas import tpu as pltpu

F32, BF16 = jnp.float32, jnp.bfloat16
MESH = pl.DeviceIdType.MESH

D_MODEL = 1024
MEM_LEN = 256
SB_HEADS, SB_DIM, SB_WIDTH = 8, 64, 512
RET_HEADS, RET_QK, RET_V = 4, 128, 256
RET_QK_WIDTH, RET_V_WIDTH = 512, 1024
ROPE_BASE = 10000.0
MEM_HEADS, MEM_DIM = 4, 256
FFN_HIDDEN = 2816
IN_WIDTH = 6656
OFF_RET_Q, OFF_RET_V, OFF_RET_G, OFF_GATE = 1536, 2560, 3584, 4608
DN_ALPHA = 2.0 ** 0.25
LN_EPS = 1e-5
SB_SCALE = SB_DIM ** -0.5
RET_SCALE = RET_QK ** -0.5
MEM_SCALE = MEM_DIM ** -0.5
ADAM_LR, ADAM_B1, ADAM_B2, ADAM_EPS, ADAM_WD, ADAM_STEP = 0.001, 0.9, 0.999, 1e-08, 0.01, 10

N_DEV, N_CHIPS = 8, 4

LANES = 128
VMEM_LIMIT_BYTES = 52 * 2 ** 20
ROW_TILE = 512
SEQ_TILE = 512
SB_BLOCK = 256
RET_BLOCK = 256
XATTN_ROWS = 512

BIG = (
    ("w_in", (D_MODEL, IN_WIDTH), 1),
    ("w_sb_o", (SB_WIDTH, D_MODEL), 1),
    ("w_ret_o", (RET_V_WIDTH, D_MODEL), 0),
    ("w_mix_o", (D_MODEL, D_MODEL), 0),
    ("w_mem_q", (D_MODEL, D_MODEL), 0),
    ("w_mem_kv", (D_MODEL, 2 * D_MODEL), 1),
    ("w_mem_o", (D_MODEL, D_MODEL), 0),
    ("w_ffn_in", (D_MODEL, 2 * FFN_HIDDEN), 1),
    ("w_ffn_out", (FFN_HIDDEN, D_MODEL), 0),
)
SMALL = ("b_gate", "ln1_g", "ln1_b", "ln2_g", "ln2_b", "ln3_g", "ln3_b")
SMALL_LEN = 2 * D_MODEL + 6 * D_MODEL
SMALL_ROWS = SMALL_LEN // LANES
PACK_ROWS = SMALL_ROWS + D_MODEL // LANES
WEIGHT_ORDER = ("w_in", "b_gate", "w_sb_o", "w_ret_o", "w_mix_o", "ln1_g", "ln1_b", "w_mem_q", "w_mem_kv",
                "w_mem_o", "ln2_g", "ln2_b", "w_ffn_in", "w_ffn_out", "ln3_g", "ln3_b")


def _cparams():
    return pltpu.CompilerParams(vmem_limit_bytes=VMEM_LIMIT_BYTES)


def _dot(a, b, ca, cb):
    return lax.dot_general(a, b, (((ca,), (cb,)), ((), ())), preferred_element_type=F32)


def _sigmoid(x):
    return 1.0 / (1.0 + jnp.exp(-x))


def _mm(name, a, b, m, n, k, *, tm, tn, tk, epi, outs, ins=(), accs=(), ta=False, tb=False,
        a_off=(0, 0), b_off=(0, 0)):
    assert m % tm == 0 and n % tn == 0 and k % tk == 0, (name, m, n, k, tm, tn, tk)
    ni, nj, nk = m // tm, n // tn, k // tk
    assert not accs or nj == 1, name
    if ta:
        a_spec = pl.BlockSpec((tk, tm), lambda i, j, kk: (kk + a_off[0], i + a_off[1]))
    else:
        a_spec = pl.BlockSpec((tm, tk), lambda i, j, kk: (i + a_off[0], kk + a_off[1]))
    if tb:
        b_spec = pl.BlockSpec((tn, tk), lambda i, j, kk: (j + b_off[0], kk + b_off[1]))
    else:
        b_spec = pl.BlockSpec((tk, tn), lambda i, j, kk: (kk + b_off[0], j + b_off[1]))
    in_specs = [a_spec, b_spec]
    for _, bs, im in ins:
        in_specs.append(pl.BlockSpec(bs, lambda i, j, kk, im=im: im(i, j)))
    out_specs, out_shape = [], []
    for shape, dtype, bs, im in outs:
        out_specs.append(pl.BlockSpec(bs, lambda i, j, kk, im=im: im(i, j)))
        out_shape.append(jax.ShapeDtypeStruct(shape, dtype))
    for shape, dtype in accs:
        out_specs.append(pl.BlockSpec(shape, lambda i, j, kk, nd=len(shape): (0,) * nd))
        out_shape.append(jax.ShapeDtypeStruct(shape, dtype))
    n_in, n_out, n_acc = len(ins), len(outs), len(accs)
    ca, cb = (0 if ta else 1), (1 if tb else 0)

    def body(*refs):
        a_ref, b_ref = refs[:2]
        in_refs = refs[2:2 + n_in]
        out_refs = refs[2 + n_in:2 + n_in + n_out]
        acc_refs = refs[2 + n_in + n_out:2 + n_in + n_out + n_acc]
        scratch = refs[2 + n_in + n_out + n_acc:]
        i, j, kk = pl.program_id(0), pl.program_id(1), pl.program_id(2)
        part = _dot(a_ref[...].astype(BF16), b_ref[...].astype(BF16), ca, cb)

        def finish(acc):
            o_tiles, a_tiles = epi(acc, [r[...] for r in in_refs], i, j)
            for r, t in zip(out_refs, o_tiles, strict=True):
                r[...] = t.astype(r.dtype)
            if n_acc:
                @pl.when(i == 0)
                def _():
                    for r, t in zip(acc_refs, a_tiles, strict=True):
                        r[...] = t

                @pl.when(i > 0)
                def _():
                    for r, t in zip(acc_refs, a_tiles, strict=True):
                        r[...] += t

        if nk == 1:
            finish(part)
        else:
            acc_ref = scratch[0]

            @pl.when(kk == 0)
            def _():
                acc_ref[...] = part

            @pl.when(kk > 0)
            def _():
                acc_ref[...] += part

            @pl.when(kk == nk - 1)
            def _():
                finish(acc_ref[...])

    res = pl.pallas_call(
        body, name=name, grid=(ni, nj, nk), in_specs=in_specs, out_specs=out_specs, out_shape=out_shape,
        scratch_shapes=[pltpu.VMEM((tm, tn), F32)] if nk > 1 else [],
        compiler_params=_cparams(),
    )(a, b, *[x for x, _, _ in ins])
    return res


def _tile(tm, tn, dj=0):
    return (tm, tn), (lambda i, j: (i, j + dj))


def _rowvec(tn, dj=0):
    return (1, tn), (lambda i, j: (0, j + dj))


def _plain(acc, tiles, i, j):
    return [acc], []


def _ew(name, fn, ins, outs, rows, tr):
    assert rows % tr == 0, (name, rows, tr)
    in_specs = []
    for x in ins:
        if x.shape[0] == rows:
            in_specs.append(pl.BlockSpec((tr, x.shape[1]), lambda i: (i, 0)))
        else:
            in_specs.append(pl.BlockSpec(x.shape, lambda i: (0, 0)))
    n_in = len(ins)

    def body(*refs):
        res = fn(*[r[...] for r in refs[:n_in]])
        for r, t in zip(refs[n_in:], res, strict=True):
            r[...] = t.astype(r.dtype)

    return pl.pallas_call(
        body, name=name, grid=(rows // tr,), in_specs=in_specs,
        out_specs=[pl.BlockSpec((tr, w), lambda i: (i, 0)) for w, _ in outs],
        out_shape=[jax.ShapeDtypeStruct((rows, w), dt) for w, dt in outs],
        compiler_params=_cparams(),
    )(*ins)


def _cast_bf16(name, x):
    rows = x.shape[0]
    tr = next(t for t in (512, 256, 64) if rows % t == 0)
    return _ew(name, lambda v: (v,), [x], [(x.shape[1], BF16)], rows, tr)[0]


def _rope_tables(s):
    half = RET_QK // 2
    inv = 1.0 / (ROPE_BASE ** (jnp.arange(half, dtype=F32) / half))
    inv2 = jnp.concatenate([inv, inv]).reshape(1, RET_QK)
    sign = jnp.concatenate([-jnp.ones((half,), F32), jnp.ones((half,), F32)]).reshape(1, RET_QK)
    tr = ROW_TILE

    def body(inv_ref, sign_ref, cos_ref, sin_ref):
        i = pl.program_id(0)
        pos = (lax.broadcasted_iota(jnp.int32, (tr, RET_QK), 0) + i * tr).astype(F32)
        ang = pos * inv_ref[...]
        cos_ref[...] = jnp.cos(ang)
        sin_ref[...] = jnp.sin(ang) * sign_ref[...]

    vec = pl.BlockSpec((1, RET_QK), lambda i: (0, 0))
    blk = pl.BlockSpec((tr, RET_QK), lambda i: (i, 0))
    return pl.pallas_call(
        body, name="rope_tables", grid=(s // tr,), in_specs=[vec, vec], out_specs=[blk, blk],
        out_shape=[jax.ShapeDtypeStruct((s, RET_QK), F32)] * 2, compiler_params=_cparams(),
    )(inv2, sign)


def _swap_halves(x):
    return pltpu.roll(x, RET_QK // 2, 1)


def _norm(u):
    mu = jnp.mean(u, axis=-1, keepdims=True)
    d = u - mu
    var = jnp.mean(d * d, axis=-1, keepdims=True)
    rstd = lax.rsqrt(var + LN_EPS)
    return d * rstd, rstd


def _norm_bwd(dxh, xhat, rstd):
    m1 = jnp.mean(dxh, axis=-1, keepdims=True)
    m2 = jnp.mean(dxh * xhat, axis=-1, keepdims=True)
    return rstd * (dxh - m1 - xhat * m2)


def _colsum(t):
    return jnp.sum(t, axis=0, keepdims=True)


def _split_mm(t, tri):
    hi = t.astype(BF16)
    lo = (t - hi.astype(F32)).astype(BF16)
    return _dot(hi, tri, 1, 0) + _dot(lo, tri, 1, 0)


def _sb_masks():
    t = SB_BLOCK
    lane = lax.broadcasted_iota(jnp.int32, (1, LANES), 1)
    first = lane < SB_DIM
    m0 = jnp.where(first, 1.0, 0.0).astype(BF16)
    m1 = jnp.where(first, 0.0, 1.0).astype(BF16)
    row = lax.broadcasted_iota(jnp.int32, (t, t), 0)
    col = lax.broadcasted_iota(jnp.int32, (t, t), 1)
    return first, (m0, m1), row, col


def _sb_logits(qh, k, causal):
    z = _dot(qh, k, 1, 1)
    lp = jnp.log1p(jnp.exp(-jnp.abs(z)))
    a = jnp.minimum(z, 0.0) - lp
    r = jnp.minimum(-z, 0.0) - lp
    if causal is not None:
        r = jnp.where(causal, r, 0.0)
    return a, r


def _sb_fwd(qkv, s):
    t = SB_BLOCK
    nq = s // t

    def body(q_ref, k_ref, v_ref, o_ref, of_ref, l_ref, acc_ref):
        i = pl.program_id(1)
        first, hmask, row, col = _sb_masks()
        after = jnp.where(row > col, 1.0, 0.0).astype(BF16)
        causal = col < row
        q = q_ref[...]
        qh = (q * hmask[0], q * hmask[1])
        l_ref[...] = jnp.zeros_like(l_ref)
        acc_ref[...] = jnp.zeros_like(acc_ref)

        def block(kb, mask):
            start = pl.multiple_of(kb * t, t)
            k = k_ref[pl.ds(start, t), :]
            v = v_ref[pl.ds(start, t), :]
            pv = []
            for h in range(2):
                a, r = _sb_logits(qh[h], k, mask)
                w = jnp.exp(a + _split_mm(r, after) + l_ref[h])
                if mask is not None:
                    w = jnp.where(mask, w, 0.0)
                pv.append(_dot(w.astype(BF16), v, 1, 0))
                l_ref[h] = l_ref[h] + jnp.sum(r, axis=1, keepdims=True)
            acc_ref[...] += jnp.where(first, pv[0], pv[1])

        block(i, causal)

        def step(jj, carry):
            block(i - 1 - jj, None)
            return carry

        lax.fori_loop(0, i, step, 0)
        o_ref[...] = acc_ref[...].astype(o_ref.dtype)
        of_ref[...] = acc_ref[...]

    nk_off = SB_WIDTH // LANES
    blk = pl.BlockSpec((t, LANES), lambda p, i: (i, p))
    return pl.pallas_call(
        body, name="sb_fwd", grid=(SB_HEADS // 2, nq),
        in_specs=[blk,
                  pl.BlockSpec((s, LANES), lambda p, i: (0, nk_off + p)),
                  pl.BlockSpec((s, LANES), lambda p, i: (0, 2 * nk_off + p))],
        out_specs=[blk, blk],
        out_shape=[jax.ShapeDtypeStruct((s, SB_WIDTH), BF16), jax.ShapeDtypeStruct((s, SB_WIDTH), F32)],
        scratch_shapes=[pltpu.VMEM((2, t, 1), F32), pltpu.VMEM((t, LANES), F32)],
        compiler_params=_cparams(),
    )(qkv, qkv, qkv)


def _sb_bwd(qkv, o, do, s):
    t = SB_BLOCK
    nq = s // t

    def body(q_ref, k_ref, v_ref, o_ref, do_ref, dq_ref, dk_ref, dv_ref, l_ref, e_ref, dq_acc, dk_acc, dv_acc):
        i = pl.program_id(1)
        first, hmask, row, col = _sb_masks()
        after = jnp.where(row > col, 1.0, 0.0).astype(BF16)
        from_here = jnp.where(row >= col, 1.0, 0.0).astype(BF16)
        causal = col < row

        @pl.when(i == 0)
        def _():
            dk_acc[...] = jnp.zeros_like(dk_acc)
            dv_acc[...] = jnp.zeros_like(dv_acc)

        q = q_ref[...]
        do_ = do_ref[...]
        qh = (q * hmask[0], q * hmask[1])
        doh = (do_ * hmask[0], do_ * hmask[1])
        prod = do_.astype(F32) * o_ref[...]
        total = (jnp.sum(jnp.where(first, prod, 0.0), axis=1, keepdims=True),
                 jnp.sum(jnp.where(first, 0.0, prod), axis=1, keepdims=True))
        l_ref[...] = jnp.zeros_like(l_ref)
        e_ref[...] = jnp.zeros_like(e_ref)
        dq_acc[...] = jnp.zeros_like(dq_acc)

        def block(kb, mask):
            start = pl.multiple_of(kb * t, t)
            k = k_ref[pl.ds(start, t), :]
            v = v_ref[pl.ds(start, t), :]
            dqs, dks, dvs = [], [], []
            for h in range(2):
                a, r = _sb_logits(qh[h], k, mask)
                w = jnp.exp(a + _split_mm(r, after) + l_ref[h])
                if mask is not None:
                    w = jnp.where(mask, w, 0.0)
                wb = w.astype(BF16)
                e = _dot(doh[h], v, 1, 1) * wb.astype(F32)
                before = total[h] - (_split_mm(e, from_here) + e_ref[h])
                dz = e - jnp.exp(a) * (e + before)
                if mask is not None:
                    dz = jnp.where(mask, dz, 0.0)
                dzb = dz.astype(BF16)
                dqs.append(_dot(dzb, k, 1, 0))
                dks.append(_dot(dzb, q, 0, 0))
                dvs.append(_dot(wb, do_, 0, 0))
                l_ref[h] = l_ref[h] + jnp.sum(r, axis=1, keepdims=True)
                e_ref[h] = e_ref[h] + jnp.sum(e, axis=1, keepdims=True)
            dq_acc[...] += jnp.where(first, dqs[0], dqs[1])
            dk_acc[pl.ds(start, t), :] += jnp.where(first, dks[0], dks[1])
            dv_acc[pl.ds(start, t), :] += jnp.where(first, dvs[0], dvs[1])

        block(i, causal)

        def step(jj, carry):
            block(i - 1 - jj, None)
            return carry

        lax.fori_loop(0, i, step, 0)
        dq_ref[...] = (dq_acc[...] * SB_SCALE).astype(dq_ref.dtype)

        @pl.when(i == nq - 1)
        def _():
            dk_ref[...] = dk_acc[...].astype(dk_ref.dtype)
            dv_ref[...] = dv_acc[...].astype(dv_ref.dtype)

    nk_off = SB_WIDTH // LANES
    blk = pl.BlockSpec((t, LANES), lambda p, i: (i, p))
    col_blk = pl.BlockSpec((s, LANES), lambda p, i: (0, p))
    sds = jax.ShapeDtypeStruct((s, SB_WIDTH), BF16)
    return pl.pallas_call(
        body, name="sb_bwd", grid=(SB_HEADS // 2, nq),
        in_specs=[blk,
                  pl.BlockSpec((s, LANES), lambda p, i: (0, nk_off + p)),
                  pl.BlockSpec((s, LANES), lambda p, i: (0, 2 * nk_off + p)),
                  blk, blk],
        out_specs=[blk, col_blk, col_blk],
        out_shape=[sds, sds, sds],
        scratch_shapes=[pltpu.VMEM((2, t, 1), F32), pltpu.VMEM((2, t, 1), F32), pltpu.VMEM((t, LANES), F32),
                        pltpu.VMEM((s, LANES), F32), pltpu.VMEM((s, LANES), F32)],
        compiler_params=_cparams(),
    )(qkv, qkv, qkv, o, do)


def _ret_log_gamma():
    lg = np.log1p(-np.exp2(-5.0 - np.arange(RET_HEADS, dtype=np.float32))).astype(np.float32)
    return jnp.asarray(np.broadcast_to(lg[:, None, None], (RET_HEADS, 8, LANES)).copy())


def _ret_decays(lg_ref):
    c = RET_BLOCK
    lg = lg_ref[0, 0:1, 0:1]
    row = lax.broadcasted_iota(jnp.int32, (c, c), 0)
    col = lax.broadcasted_iota(jnp.int32, (c, c), 1)
    rel = (row - col).astype(F32)
    within = jnp.where(row >= col, jnp.exp(lg * jnp.maximum(rel, 0.0)), 0.0)
    idx = lax.broadcasted_iota(jnp.int32, (c, 1), 0).astype(F32)
    q_dec = jnp.exp(lg * (idx + 1.0))
    k_dec = jnp.exp(lg * (c - 1.0 - idx))
    chunk_dec = jnp.exp(lg * float(c))
    return within, q_dec, k_dec, chunk_dec


def _ret_specs(s, reverse=False):
    c = RET_BLOCK
    nc = s // c
    pos = (lambda n: nc - 1 - n) if reverse else (lambda n: n)
    qk_heads = RET_QK_WIDTH // RET_QK
    q_spec = pl.BlockSpec((c, RET_QK), lambda h, n: (pos(n), h))
    k_spec = pl.BlockSpec((c, RET_QK), lambda h, n: (pos(n), qk_heads + h))
    v_spec = pl.BlockSpec((c, RET_V), lambda h, n: (pos(n), h))
    lg_spec = pl.BlockSpec((1, 8, LANES), lambda h, n: (h, 0, 0))
    rope_spec = pl.BlockSpec((c, RET_QK), lambda h, n: (pos(n), 0))
    return nc, q_spec, k_spec, v_spec, lg_spec, rope_spec


def _ret_fwd(rqk, rv, rg, s):
    nc, q_spec, k_spec, v_spec, lg_spec, _ = _ret_specs(s)

    def body(q_ref, k_ref, v_ref, g_ref, lg_ref, r_ref, y_ref, state):
        n = pl.program_id(1)

        @pl.when(n == 0)
        def _():
            state[...] = jnp.zeros_like(state)

        within, q_dec, k_dec, chunk_dec = _ret_decays(lg_ref)
        q, k, v = q_ref[...], k_ref[...], v_ref[...]
        scores = _dot(q.astype(BF16), k.astype(BF16), 1, 1) * within
        out = _dot(scores.astype(BF16), v, 1, 0)
        out += _dot((q * q_dec).astype(BF16), state[...].astype(BF16), 1, 0)
        r_ref[...] = out
        xhat, _ = _norm(out)
        g = g_ref[...]
        y_ref[...] = (g * _sigmoid(g) * xhat).astype(y_ref.dtype)
        state[...] = state[...] * chunk_dec + _dot((k * k_dec).astype(BF16), v, 0, 0)

    return pl.pallas_call(
        body, name="ret_fwd", grid=(RET_HEADS, nc),
        in_specs=[q_spec, k_spec, v_spec, v_spec, lg_spec],
        out_specs=[v_spec, v_spec],
        out_shape=[jax.ShapeDtypeStruct((s, RET_V_WIDTH), F32), jax.ShapeDtypeStruct((s, RET_V_WIDTH), BF16)],
        scratch_shapes=[pltpu.VMEM((RET_QK, RET_V), F32)],
        compiler_params=_cparams(),
    )(rqk, rqk, rv, rg, _ret_log_gamma())


def _rope_bwd(d, cos, sin):
    return d * cos + _swap_halves(d * sin)


def _ret_bwd_q(rqk, rv, d_out, cos2, sin2, s):
    nc, q_spec, k_spec, v_spec, lg_spec, rope_spec = _ret_specs(s)

    def body(q_ref, k_ref, v_ref, d_ref, lg_ref, cos_ref, sin_ref, dq_ref, state):
        n = pl.program_id(1)

        @pl.when(n == 0)
        def _():
            state[...] = jnp.zeros_like(state)

        within, q_dec, k_dec, chunk_dec = _ret_decays(lg_ref)
        k, v, d = k_ref[...], v_ref[...], d_ref[...]
        st = state[...].astype(BF16)
        d_scores = _dot(d, v, 1, 1) * within
        dq = _dot(d_scores.astype(BF16), k.astype(BF16), 1, 0) + q_dec * _dot(d, st, 1, 1)
        dq_ref[...] = (_rope_bwd(dq, cos_ref[...], sin_ref[...]) * RET_SCALE).astype(dq_ref.dtype)
        state[...] = state[...] * chunk_dec + _dot((k * k_dec).astype(BF16), v, 0, 0)

    return pl.pallas_call(
        body, name="ret_bwd_q", grid=(RET_HEADS, nc),
        in_specs=[q_spec, k_spec, v_spec, v_spec, lg_spec, rope_spec, rope_spec],
        out_specs=q_spec,
        out_shape=jax.ShapeDtypeStruct((s, RET_QK_WIDTH), BF16),
        scratch_shapes=[pltpu.VMEM((RET_QK, RET_V), F32)],
        compiler_params=_cparams(),
    )(rqk, rqk, rv, d_out, _ret_log_gamma(), cos2, sin2)


def _ret_bwd_kv(rqk, rv, d_out, cos2, sin2, s):
    nc, q_spec, k_spec, v_spec, lg_spec, rope_spec = _ret_specs(s, reverse=True)

    def body(q_ref, k_ref, v_ref, d_ref, lg_ref, cos_ref, sin_ref, dk_ref, dv_ref, state):
        n = pl.program_id(1)

        @pl.when(n == 0)
        def _():
            state[...] = jnp.zeros_like(state)

        within, q_dec, k_dec, chunk_dec = _ret_decays(lg_ref)
        q, k, v, d = q_ref[...], k_ref[...], v_ref[...], d_ref[...]
        qb, kb = q.astype(BF16), k.astype(BF16)
        st = state[...].astype(BF16)
        scores = _dot(qb, kb, 1, 1) * within
        d_scores = _dot(d, v, 1, 1) * within
        dk = _dot(d_scores.astype(BF16), qb, 0, 0) + k_dec * _dot(v, st, 1, 1)
        dv = _dot(scores.astype(BF16), d, 0, 0) + k_dec * _dot(kb, st, 1, 0)
        dk_ref[...] = _rope_bwd(dk, cos_ref[...], sin_ref[...]).astype(dk_ref.dtype)
        dv_ref[...] = dv.astype(dv_ref.dtype)
        state[...] = state[...] * chunk_dec + _dot((q * q_dec).astype(BF16), d, 0, 0)

    c = RET_BLOCK
    dk_spec = pl.BlockSpec((c, RET_QK), lambda h, n: (nc - 1 - n, h))
    return pl.pallas_call(
        body, name="ret_bwd_kv", grid=(RET_HEADS, nc),
        in_specs=[q_spec, k_spec, v_spec, v_spec, lg_spec, rope_spec, rope_spec],
        out_specs=[dk_spec, v_spec],
        out_shape=[jax.ShapeDtypeStruct((s, RET_QK_WIDTH), BF16), jax.ShapeDtypeStruct((s, RET_V_WIDTH), BF16)],
        scratch_shapes=[pltpu.VMEM((RET_QK, RET_V), F32)],
        compiler_params=_cparams(),
    )(rqk, rqk, rv, d_out, _ret_log_gamma(), cos2, sin2)


def _xattn_probs(q, k):
    sc = _dot(q, k, 1, 1)
    sc = sc - jnp.max(sc, axis=-1, keepdims=True)
    p = jnp.exp(sc)
    return p / jnp.sum(p, axis=-1, keepdims=True)


def _xattn_fwd(qm, kv, s):
    tq = XATTN_ROWS

    def body(q_ref, kv_ref, o_ref):
        for h in range(MEM_HEADS):
            sl = slice(h * MEM_DIM, (h + 1) * MEM_DIM)
            sv = slice(D_MODEL + h * MEM_DIM, D_MODEL + (h + 1) * MEM_DIM)
            p = _xattn_probs(q_ref[:, sl], kv_ref[:, sl])
            o_ref[:, sl] = _dot(p.astype(BF16), kv_ref[:, sv], 1, 0).astype(o_ref.dtype)

    return pl.pallas_call(
        body, name="xattn_fwd", grid=(s // tq,),
        in_specs=[pl.BlockSpec((tq, D_MODEL), lambda i: (i, 0)),
                  pl.BlockSpec((MEM_LEN, 2 * D_MODEL), lambda i: (0, 0))],
        out_specs=pl.BlockSpec((tq, D_MODEL), lambda i: (i, 0)),
        out_shape=jax.ShapeDtypeStruct((s, D_MODEL), BF16),
        compiler_params=_cparams(),
    )(qm, kv)


def _xattn_bwd(qm, kv, do, s):
    tq = XATTN_ROWS

    def body(q_ref, kv_ref, do_ref, dq_ref, dkv_ref):
        i = pl.program_id(0)

        @pl.when(i == 0)
        def _():
            dkv_ref[...] = jnp.zeros_like(dkv_ref)

        for h in range(MEM_HEADS):
            sl = slice(h * MEM_DIM, (h + 1) * MEM_DIM)
            sv = slice(D_MODEL + h * MEM_DIM, D_MODEL + (h + 1) * MEM_DIM)
            q, k, v, d = q_ref[:, sl], kv_ref[:, sl], kv_ref[:, sv], do_ref[:, sl]
            p = _xattn_probs(q, k)
            dp = _dot(d, v, 1, 1)
            ds = (p * (dp - jnp.sum(p * dp, axis=-1, keepdims=True))).astype(BF16)
            dq_ref[:, sl] = (_dot(ds, k, 1, 0) * MEM_SCALE).astype(dq_ref.dtype)
            dkv_ref[:, sl] += _dot(ds, q, 0, 0)
            dkv_ref[:, sv] += _dot(p.astype(BF16), d, 0, 0)

    row_blk = pl.BlockSpec((tq, D_MODEL), lambda i: (i, 0))
    kv_blk = pl.BlockSpec((MEM_LEN, 2 * D_MODEL), lambda i: (0, 0))
    return pl.pallas_call(
        body, name="xattn_bwd", grid=(s // tq,),
        in_specs=[row_blk, kv_blk, row_blk],
        out_specs=[row_blk, kv_blk],
        out_shape=[jax.ShapeDtypeStruct((s, D_MODEL), BF16), jax.ShapeDtypeStruct((MEM_LEN, 2 * D_MODEL), F32)],
        compiler_params=_cparams(),
    )(qm, kv, do)


def _place():
    x, y, c = lax.axis_index("x"), lax.axis_index("y"), lax.axis_index("c")
    others = [(1 - x, y), (x, 1 - y), (1 - x, 1 - y)]
    return x, y, c, others


def _slab(ref, axis, chip, size):
    start = pl.multiple_of(chip * size, LANES if axis == 1 else 16)
    if axis == 0:
        return ref.at[pl.ds(start, size), :]
    return ref.at[:, pl.ds(start, size)]


def _weight_gather(shards):
    nw = len(BIG)
    any_spec = pl.BlockSpec(memory_space=pl.ANY)

    def body(*refs):
        shard, full = refs[:nw], refs[nw:2 * nw]
        send_sems, recv_sems, local_sems = refs[2 * nw:]
        x, y, c, others = _place()
        mine = 2 * x + y
        sent, local = [], []
        for w, (_, shape, axis) in enumerate(BIG):
            size = shape[axis] // N_CHIPS
            cp = pltpu.make_async_copy(shard[w], _slab(full[w], axis, mine, size), local_sems.at[w])
            cp.start()
            local.append(cp)
            for t, (qx, qy) in enumerate(others):
                cp = pltpu.make_async_remote_copy(
                    src_ref=shard[w], dst_ref=_slab(full[w], axis, mine, size),
                    send_sem=send_sems.at[3 * w + t], recv_sem=recv_sems.at[3 * w + t],
                    device_id=(qx, qy, c), device_id_type=MESH)
                cp.start()
                sent.append(cp)
        for w, (_, shape, axis) in enumerate(BIG):
            size = shape[axis] // N_CHIPS
            for t, (qx, qy) in enumerate(others):
                pltpu.make_async_remote_copy(
                    src_ref=shard[w], dst_ref=_slab(full[w], axis, 2 * qx + qy, size),
                    send_sem=send_sems.at[3 * w + t], recv_sem=recv_sems.at[3 * w + t],
                    device_id=(qx, qy, c), device_id_type=MESH).wait_recv()
        for cp in sent:
            cp.wait_send()
        for cp in local:
            cp.wait()

    return pl.pallas_call(
        body, name="weight_gather",
        in_specs=[any_spec] * nw, out_specs=[any_spec] * nw,
        out_shape=[jax.ShapeDtypeStruct(shape, BF16) for _, shape, _ in BIG],
        scratch_shapes=[pltpu.SemaphoreType.DMA((3 * nw,)), pltpu.SemaphoreType.DMA((3 * nw,)),
                        pltpu.SemaphoreType.DMA((nw,))],
    )(*shards)


def _shard_shape(shape, axis):
    return tuple(d // N_CHIPS if a == axis else d for a, d in enumerate(shape))


def _grad_exchange(grads):
    nw = len(BIG)
    any_spec = pl.BlockSpec(memory_space=pl.ANY)

    def body(*refs):
        grad, stack = refs[:nw], refs[nw:2 * nw]
        send_sems, recv_sems, local_sems = refs[2 * nw:]
        x, y, c, others = _place()
        mine = 2 * x + y
        me, sibling = (x, y, c), (x, y, 1 - c)

        def dev(px, py, pc):
            return 4 * px + 2 * py + pc

        def copy(w, n, src, slot, to):
            return pltpu.make_async_remote_copy(
                src_ref=src, dst_ref=stack[w].at[slot], send_sem=send_sems.at[7 * w + n],
                recv_sem=recv_sems.at[7 * w + n], device_id=to, device_id_type=MESH)

        sent, local = [], []
        for w, (_, shape, axis) in enumerate(BIG):
            size = shape[axis] // N_CHIPS
            own = _slab(grad[w], axis, mine, size)
            cp = pltpu.make_async_copy(own, stack[w].at[dev(*me)], local_sems.at[w])
            cp.start()
            local.append(cp)
            first = [copy(w, 0, own, dev(*me), sibling)]
            first += [copy(w, 1 + t, _slab(grad[w], axis, 2 * qx + qy, size), dev(*me), (qx, qy, c))
                      for t, (qx, qy) in enumerate(others)]
            for cp in first:
                cp.start()
            sent += first
        for w in range(nw):
            for t, (qx, qy) in enumerate(others):
                got = stack[w].at[dev(qx, qy, c)]
                copy(w, 1 + t, got, dev(qx, qy, c), me).wait_recv()
                cp = copy(w, 4 + t, got, dev(qx, qy, c), sibling)
                cp.start()
                sent.append(cp)
        for w in range(nw):
            copy(w, 0, stack[w].at[dev(*sibling)], dev(*sibling), me).wait_recv()
            for t, (qx, qy) in enumerate(others):
                copy(w, 4 + t, stack[w].at[dev(qx, qy, 1 - c)], dev(qx, qy, 1 - c), me).wait_recv()
        for cp in sent:
            cp.wait_send()
        for cp in local:
            cp.wait()

    return pl.pallas_call(
        body, name="grad_exchange",
        in_specs=[any_spec] * nw, out_specs=[any_spec] * nw,
        out_shape=[jax.ShapeDtypeStruct((N_DEV,) + _shard_shape(shape, axis), BF16) for _, shape, axis in BIG],
        scratch_shapes=[pltpu.SemaphoreType.DMA((7 * nw,)), pltpu.SemaphoreType.DMA((7 * nw,)),
                        pltpu.SemaphoreType.DMA((nw,))],
    )(*grads)


def _adamw(w, g, m, v):
    m = ADAM_B1 * m + (1.0 - ADAM_B1) * g
    v = ADAM_B2 * v + (1.0 - ADAM_B2) * (g * g)
    m_hat = m / (1.0 - ADAM_B1 ** ADAM_STEP)
    v_hat = v / (1.0 - ADAM_B2 ** ADAM_STEP)
    delta = -ADAM_LR * (m_hat / (jnp.sqrt(v_hat) + ADAM_EPS) + ADAM_WD * w)
    return delta, m, v


def _reduce_adamw(name, stack, w, m, v):
    rows, cols = w.shape
    tr = next(t for t in (256, 128, 64) if rows % t == 0)

    def body(s_ref, w_ref, m_ref, v_ref, g_ref, d_ref, nm_ref, nv_ref):
        g = s_ref[0].astype(F32)
        for d in range(1, N_DEV):
            g = g + s_ref[d].astype(F32)
        g_ref[...] = g
        d_ref[...], nm_ref[...], nv_ref[...] = _adamw(w_ref[...], g, m_ref[...], v_ref[...])

    blk = pl.BlockSpec((tr, cols), lambda i: (i, 0))
    return pl.pallas_call(
        body, name=name, grid=(rows // tr,),
        in_specs=[pl.BlockSpec((N_DEV, tr, cols), lambda i: (0, i, 0)), blk, blk, blk],
        out_specs=[blk] * 4, out_shape=[jax.ShapeDtypeStruct((rows, cols), F32)] * 4,
        compiler_params=_cparams(),
    )(stack, w, m, v)


def _small_step(pack, w, m, v):
    def body(p_ref, w_ref, m_ref, v_ref, g_ref, d_ref, nm_ref, nv_ref, loss_ref, all_ref, send_sems, recv_sems):
        x, y, c, _ = _place()
        me = 4 * x + 2 * y + c
        all_ref[me] = p_ref[...]
        sent = []
        for n in range(1, N_DEV):
            peer = me ^ n
            cp = pltpu.make_async_remote_copy(
                src_ref=p_ref, dst_ref=all_ref.at[me], send_sem=send_sems.at[n - 1], recv_sem=recv_sems.at[n - 1],
                device_id=(peer // 4, (peer // 2) % 2, peer % 2), device_id_type=MESH)
            cp.start()
            sent.append(cp)
        for n in range(1, N_DEV):
            peer = me ^ n
            pltpu.make_async_remote_copy(
                src_ref=p_ref, dst_ref=all_ref.at[peer], send_sem=send_sems.at[n - 1], recv_sem=recv_sems.at[n - 1],
                device_id=(peer // 4, (peer // 2) % 2, peer % 2), device_id_type=MESH).wait_recv()
        for cp in sent:
            cp.wait_send()
        tot = all_ref[0]
        for d in range(1, N_DEV):
            tot = tot + all_ref[d]
        g = tot[:SMALL_ROWS]
        g_ref[...] = g
        d_ref[...], nm_ref[...], nv_ref[...] = _adamw(w_ref[...], g, m_ref[...], v_ref[...])
        loss_ref[...] = jnp.sum(jnp.sum(tot[SMALL_ROWS:], axis=1, keepdims=True), axis=0, keepdims=True)

    vm = pl.BlockSpec(memory_space=pltpu.VMEM)
    small = jax.ShapeDtypeStruct((SMALL_ROWS, LANES), F32)
    return pl.pallas_call(
        body, name="small_step",
        in_specs=[vm] * 4, out_specs=[vm] * 5,
        out_shape=[small] * 4 + [jax.ShapeDtypeStruct((1, 1), F32)],
        scratch_shapes=[pltpu.VMEM((N_DEV, PACK_ROWS, LANES), F32),
                        pltpu.SemaphoreType.DMA((N_DEV - 1,)), pltpu.SemaphoreType.DMA((N_DEV - 1,))],
    )(pack, w, m, v)


def _layer_step(x, mem, tgt, wt, vec):
    s = x.shape[0]
    d = D_MODEL
    tm = ROW_TILE
    w_in = wt["w_in"]
    cos2, sin2 = _rope_tables(s)
    xb = _cast_bf16("cast_x", x)
    bf = lambda w: ((s, w), BF16)
    f32 = lambda w: ((s, w), F32)

    (sb_qkv,) = _mm(
        "in_sb", xb, w_in, s, 3 * SB_WIDTH, d, tm=tm, tn=512, tk=d,
        epi=lambda acc, t, i, j: ([acc * jnp.where(j == 0, SB_SCALE, 1.0)], []),
        outs=[(*bf(3 * SB_WIDTH), *_tile(tm, 512))])

    def rope_epi(acc, t, i, j):
        cos, sin = t
        scale = jnp.where(j == 0, RET_SCALE, 1.0)
        parts = []
        for g in range(512 // RET_QK):
            xg = acc[:, g * RET_QK:(g + 1) * RET_QK]
            parts.append((xg * cos + _swap_halves(xg) * sin) * scale)
        return [jnp.concatenate(parts, axis=1)], []

    rope_in = ((tm, RET_QK), lambda i, j: (i, 0))
    (rqk,) = _mm("in_rqk", xb, w_in, s, 2 * RET_QK_WIDTH, d, tm=tm, tn=512, tk=d, b_off=(0, OFF_RET_Q // 512),
                 epi=rope_epi, ins=[(cos2, *rope_in), (sin2, *rope_in)],
                 outs=[(*f32(2 * RET_QK_WIDTH), *_tile(tm, 512))])
    (rv,) = _mm("in_rv", xb, w_in, s, RET_V_WIDTH, d, tm=tm, tn=512, tk=d, b_off=(0, OFF_RET_V // 512),
                epi=_plain, outs=[(*bf(RET_V_WIDTH), *_tile(tm, 512))])
    (rg,) = _mm("in_rg", xb, w_in, s, RET_V_WIDTH, d, tm=tm, tn=512, tk=d, b_off=(0, OFF_RET_G // 512),
                epi=_plain, outs=[(*f32(RET_V_WIDTH), *_tile(tm, 512))])
    (gates,) = _mm("in_gate", xb, w_in, s, 2 * d, d, tm=tm, tn=512, tk=d, b_off=(0, OFF_GATE // 512),
                   epi=lambda acc, t, i, j: ([_sigmoid(acc + t[0])], []),
                   ins=[(vec["b_gate"], *_rowvec(512))], outs=[(*f32(2 * d), *_tile(tm, 512))])

    sb_out, sb_out_f32 = _sb_fwd(sb_qkv, s)
    ret, gated = _ret_fwd(rqk, rv, rg, s)
    (y_sb,) = _mm("sb_o", sb_out, wt["w_sb_o"], s, d, SB_WIDTH, tm=tm, tn=512, tk=SB_WIDTH, epi=_plain,
                  outs=[(*f32(d), *_tile(tm, 512))])
    y_ret, mixin = _mm(
        "ret_o", gated, wt["w_ret_o"], s, d, RET_V_WIDTH, tm=tm, tn=512, tk=RET_V_WIDTH,
        epi=lambda acc, t, i, j: ([acc, t[0] * t[2] + t[1] * acc], []),
        ins=[(gates, *_tile(tm, 512)), (gates, *_tile(tm, 512, d // 512)), (y_sb, *_tile(tm, 512))],
        outs=[(*f32(d), *_tile(tm, 512)), (*bf(d), *_tile(tm, 512))])

    def ln_epi(acc, t, i, j):
        res, g, b = t
        xhat, rstd = _norm(DN_ALPHA * res + acc)
        y = xhat * g + b
        return [y, y, xhat, rstd], []

    full = _tile(tm, d)
    col1 = ((tm, 1), lambda i, j: (i, 0))
    ln_outs = [(*f32(d), *full), (*bf(d), *full), (*f32(d), *full), ((s, 1), F32, *col1)]
    x1, x1b, xhat1, rstd1 = _mm(
        "mix_o", mixin, wt["w_mix_o"], s, d, d, tm=tm, tn=d, tk=d, epi=ln_epi,
        ins=[(x, *full), (vec["ln1_g"], *_rowvec(d)), (vec["ln1_b"], *_rowvec(d))], outs=ln_outs)

    (qm,) = _mm("mem_q", x1b, wt["w_mem_q"], s, d, d, tm=tm, tn=512, tk=d,
                epi=lambda acc, t, i, j: ([acc * MEM_SCALE], []), outs=[(*bf(d), *_tile(tm, 512))])
    (kv,) = _mm("mem_kv", mem, wt["w_mem_kv"], MEM_LEN, 2 * d, d, tm=MEM_LEN, tn=512, tk=d, epi=_plain,
                outs=[((MEM_LEN, 2 * d), BF16, *_tile(MEM_LEN, 512))])
    att = _xattn_fwd(qm, kv, s)
    x2, x2b, xhat2, rstd2 = _mm(
        "mem_o", att, wt["w_mem_o"], s, d, d, tm=tm, tn=d, tk=d, epi=ln_epi,
        ins=[(x1, *full), (vec["ln2_g"], *_rowvec(d)), (vec["ln2_b"], *_rowvec(d))], outs=ln_outs)

    fh = FFN_HIDDEN
    tf = fh // 2
    (f1,) = _mm("ffn_in1", x2b, wt["w_ffn_in"], s, fh, d, tm=tm, tn=tf, tk=d, epi=_plain,
                outs=[(*f32(fh), *_tile(tm, tf))])
    f2, act = _mm(
        "ffn_in2", x2b, wt["w_ffn_in"], s, fh, d, tm=tm, tn=tf, tk=d, b_off=(0, 2),
        epi=lambda acc, t, i, j: ([acc, t[0] * _sigmoid(t[0]) * acc], []),
        ins=[(f1, *_tile(tm, tf))], outs=[(*f32(fh), *_tile(tm, tf)), (*bf(fh), *_tile(tm, tf))])

    def head_epi(acc, t, i, j):
        res, g, b, target = t
        xhat, rstd = _norm(DN_ALPHA * res + acc)
        err = xhat * g + b - target
        dy = err * (1.0 / d)
        du = _norm_bwd(dy * g, xhat, rstd)
        return [du, du], [_colsum(dy * xhat), _colsum(dy), _colsum(err * err) * (0.5 / d)]

    vec_acc = ((1, d), F32)
    du3, du3b, dg3, db3, loss_cols = _mm(
        "ffn_out", act, wt["w_ffn_out"], s, d, fh, tm=tm, tn=d, tk=tf, epi=head_epi,
        ins=[(x2, *full), (vec["ln3_g"], *_rowvec(d)), (vec["ln3_b"], *_rowvec(d)), (tgt, *full)],
        outs=[(*f32(d), *full), (*bf(d), *full)], accs=[vec_acc] * 3)

    grads = {}
    ts = min(SEQ_TILE, s)

    def wgrad(name, a, b, m, n, tm_, tn_, tk_=None):
        (g,) = _mm(name, a, b, m, n, a.shape[0], tm=tm_, tn=tn_, tk=tk_ or ts, ta=True, epi=_plain,
                   outs=[((m, n), BF16, *_tile(tm_, tn_))])
        return g

    def ffn_bwd_epi(acc, t, i, j):
        a, b = t
        sg = _sigmoid(a)
        return [acc * b * (sg * (1.0 + a * (1.0 - sg))), acc * (a * sg)], []

    df1, df2 = _mm(
        "ffn_out_t", du3b, wt["w_ffn_out"], s, fh, d, tm=tm, tn=tf, tk=d, tb=True, epi=ffn_bwd_epi,
        ins=[(f1, *_tile(tm, tf)), (f2, *_tile(tm, tf))],
        outs=[(*bf(fh), *_tile(tm, tf)), (*bf(fh), *_tile(tm, tf))])
    grads["w_ffn_out"] = wgrad("g_ffn_out", act, du3b, fh, d, tf, d)
    grads["w_ffn_in"] = jnp.concatenate(
        [wgrad("g_ffn_in1", x2b, df1, d, fh, d, tf), wgrad("g_ffn_in2", x2b, df2, d, fh, d, tf)], axis=1)
    (dx2a,) = _mm("ffn_in1_t", df1, wt["w_ffn_in"], s, d, fh, tm=tm, tn=d, tk=tf, tb=True, epi=_plain,
                  outs=[(*f32(d), *full)])

    def ln_bwd(name, a, b, k, tk, b_off, more, scales, xhat, rstd, g):
        def epi(acc, t, i, j):
            *extra, xh, rs, gg = t
            dy = acc
            for e, sc in zip(extra, scales, strict=True):
                dy = dy + e * sc
            du = _norm_bwd(dy * gg, xh, rs)
            return [du, du], [_colsum(dy * xh), _colsum(dy)]

        return _mm(name, a, b, s, d, k, tm=tm, tn=d, tk=tk, tb=True, b_off=b_off, epi=epi,
                   ins=[(e, *full) for e in more] + [(xhat, *full), (rstd, *col1), (g, *_rowvec(d))],
                   outs=[(*f32(d), *full), (*bf(d), *full)], accs=[vec_acc] * 2)

    du2, du2b, dg2, db2 = ln_bwd("ffn_in2_t", df2, wt["w_ffn_in"], fh, tf, (0, 2), [dx2a, du3], [1.0, DN_ALPHA],
                                 xhat2, rstd2, vec["ln2_g"])

    (datt,) = _mm("mem_o_t", du2b, wt["w_mem_o"], s, d, d, tm=tm, tn=512, tk=d, tb=True, epi=_plain,
                  outs=[(*bf(d), *_tile(tm, 512))])
    grads["w_mem_o"] = wgrad("g_mem_o", att, du2b, d, d, d, d)
    dqm, dkv = _xattn_bwd(qm, kv, datt, s)
    grads["w_mem_q"] = wgrad("g_mem_q", x1b, dqm, d, d, d, d)
    grads["w_mem_kv"] = wgrad("g_mem_kv", mem, dkv, d, 2 * d, d, d, MEM_LEN)
    du1, du1b, dg1, db1 = ln_bwd("mem_q_t", dqm, wt["w_mem_q"], d, d, (0, 0), [du2], [DN_ALPHA],
                                 xhat1, rstd1, vec["ln1_g"])

    def merge_bwd_epi(acc, t, i, j):
        g0, g1, ysb, yret = t
        dgate0 = acc * ysb * (g0 * (1.0 - g0))
        dgate1 = acc * yret * (g1 * (1.0 - g1))
        return [dgate0, dgate1, acc * g0, acc * g1], [_colsum(dgate0), _colsum(dgate1)]

    tm_merge = 256
    mfull = _tile(tm_merge, d)
    dgate0, dgate1, dy_sb, dy_ret, dbg0, dbg1 = _mm(
        "mix_o_t", du1b, wt["w_mix_o"], s, d, d, tm=tm_merge, tn=d, tk=d, tb=True, epi=merge_bwd_epi,
        ins=[(gates, *mfull), (gates, *_tile(tm_merge, d, 1)), (y_sb, *mfull), (y_ret, *mfull)],
        outs=[(*bf(d), *mfull)] * 4, accs=[vec_acc] * 2)
    grads["w_mix_o"] = wgrad("g_mix_o", mixin, du1b, d, d, d, d)
    grads["w_sb_o"] = wgrad("g_sb_o", sb_out, dy_sb, SB_WIDTH, d, SB_WIDTH, d)
    grads["w_ret_o"] = wgrad("g_ret_o", gated, dy_ret, RET_V_WIDTH, d, RET_V_WIDTH, d)
    (dsb_out,) = _mm("sb_o_t", dy_sb, wt["w_sb_o"], s, SB_WIDTH, d, tm=tm, tn=SB_WIDTH, tk=d, tb=True, epi=_plain,
                     outs=[(*bf(SB_WIDTH), *_tile(tm, SB_WIDTH))])

    def gate_norm_bwd_epi(acc, t, i, j):
        r, g = t
        drg, dret = [], []
        for h in range(512 // RET_V):
            sl = slice(h * RET_V, (h + 1) * RET_V)
            xhat, rstd = _norm(r[:, sl])
            gg, dd = g[:, sl], acc[:, sl]
            sg = _sigmoid(gg)
            drg.append(dd * xhat * (sg * (1.0 + gg * (1.0 - sg))))
            dret.append(_norm_bwd(dd * (gg * sg), xhat, rstd))
        return [jnp.concatenate(drg, axis=1), jnp.concatenate(dret, axis=1)], []

    drg, dret = _mm(
        "ret_o_t", dy_ret, wt["w_ret_o"], s, RET_V_WIDTH, d, tm=tm, tn=512, tk=d, tb=True, epi=gate_norm_bwd_epi,
        ins=[(ret, *_tile(tm, 512)), (rg, *_tile(tm, 512))],
        outs=[(*bf(RET_V_WIDTH), *_tile(tm, 512))] * 2)

    drq = _ret_bwd_q(rqk, rv, dret, cos2, sin2, s)
    drk, drv = _ret_bwd_kv(rqk, rv, dret, cos2, sin2, s)
    dsq, dsk, dsv = _sb_bwd(sb_qkv, sb_out_f32, dsb_out, s)

    dh = jnp.concatenate([dsq, dsk, dsv, drq, drk, drv, drg, dgate0, dgate1], axis=1)
    grads["w_in"] = wgrad("g_in", xb, dh, d, IN_WIDTH, d, IN_WIDTH // N_CHIPS)
    (grad_x,) = _mm("in_t", dh, w_in, s, d, IN_WIDTH, tm=tm, tn=d, tk=512, tb=True,
                    epi=lambda acc, t, i, j: ([acc + DN_ALPHA * t[0]], []),
                    ins=[(du1, *full)], outs=[(*f32(d), *full)])

    small = {"b_gate": jnp.concatenate([dbg0, dbg1], axis=1), "ln1_g": dg1, "ln1_b": db1, "ln2_g": dg2,
             "ln2_b": db2, "ln3_g": dg3, "ln3_b": db3}
    return grad_x, grads, small, loss_cols


def kernel(x, mem, w_in, b_gate, w_sb_o, w_ret_o, w_mix_o, ln1_g, ln1_b, w_mem_q, w_mem_kv, w_mem_o, ln2_g, ln2_b, w_ffn_in, w_ffn_out, ln3_g, ln3_b, loss_target, m_w_in, m_b_gate, m_w_sb_o, m_w_ret_o, m_w_mix_o, m_ln1_g, m_ln1_b, m_w_mem_q, m_w_mem_kv, m_w_mem_o, m_ln2_g, m_ln2_b, m_w_ffn_in, m_w_ffn_out, m_ln3_g, m_ln3_b, v_w_in, v_b_gate, v_w_sb_o, v_w_ret_o, v_w_mix_o, v_ln1_g, v_ln1_b, v_w_mem_q, v_w_mem_kv, v_w_mem_o, v_ln2_g, v_ln2_b, v_w_ffn_in, v_w_ffn_out, v_ln3_g, v_ln3_b):
    given = dict(locals())
    s = x.shape[1]
    x2d = x.reshape(s, D_MODEL)
    tgt = loss_target.reshape(s, D_MODEL)
    mem2d = mem.reshape(MEM_LEN, D_MODEL)
    shard = {name: given[name].reshape(_shard_shape(shape, axis)) for name, shape, axis in BIG}
    vec = {name: given[name] for name in SMALL}

    shards_bf = [_cast_bf16("cast_" + name, shard[name]) for name, _, _ in BIG]
    wt = dict(zip([name for name, _, _ in BIG], _weight_gather(shards_bf), strict=True))

    grad_x, grads, small, loss_cols = _layer_step(x2d, mem2d, tgt, wt, vec)

    stacks = _grad_exchange([grads[name] for name, _, _ in BIG])
    out = {}
    for (name, shape, axis), stack in zip(BIG, stacks, strict=True):
        shp = given[name].shape
        res = _reduce_adamw("adamw_" + name, stack, shard[name], given["m_" + name].reshape(stack.shape[1:]),
                            given["v_" + name].reshape(stack.shape[1:]))
        out[name] = [r.reshape(shp) for r in res]

    pack = jnp.concatenate([small[name] for name in SMALL] + [loss_cols], axis=1).reshape(PACK_ROWS, LANES)
    cat = lambda pre: jnp.concatenate([given[pre + name] for name in SMALL], axis=1).reshape(SMALL_ROWS, LANES)
    *res, loss = _small_step(pack, cat(""), cat("m_"), cat("v_"))
    flat = [r.reshape(1, SMALL_LEN) for r in res]
    off = 0
    for name in SMALL:
        n = given[name].shape[1]
        out[name] = [r[:, off:off + n] for r in flat]
        off += n

    return (loss.reshape(()), grad_x.reshape(x.shape),
            *[out[name][0] for name in WEIGHT_ORDER], *[out[name][1] for name in WEIGHT_ORDER],
            *[out[name][2] for name in WEIGHT_ORDER], *[out[name][3] for name in WEIGHT_ORDER])
```

```python
import functools

import jax
import jax.numpy as jnp
import numpy as np
from jax import lax
from jax.experimental import pallas as pl
from jax.experimental.pallas import tpu as pltpu

F32, BF16 = jnp.float32, jnp.bfloat16
MESH = pl.DeviceIdType.MESH

D_MODEL = 1024
MEM_LEN = 256
SB_HEADS, SB_DIM, SB_WIDTH = 8, 64, 512
RET_HEADS, RET_QK, RET_V = 4, 128, 256
RET_QK_WIDTH, RET_V_WIDTH = 512, 1024
ROPE_BASE = 10000.0
MEM_HEADS, MEM_DIM = 4, 256
FFN_HIDDEN = 2816
IN_WIDTH = 6656
OFF_RET_Q, OFF_RET_V, OFF_RET_G, OFF_GATE = 1536, 2560, 3584, 4608
DN_ALPHA = 2.0 ** 0.25
LN_EPS = 1e-5
SB_SCALE = SB_DIM ** -0.5
SB_DEAD = -110.0
RET_SCALE = RET_QK ** -0.5
MEM_SCALE = MEM_DIM ** -0.5
ADAM_LR, ADAM_B1, ADAM_B2, ADAM_EPS, ADAM_WD, ADAM_STEP = 0.001, 0.9, 0.999, 1e-08, 0.01, 10

N_DEV, N_CHIPS = 8, 4

LANES = 128
VMEM_LIMIT_BYTES = 52 * 2 ** 20
ROW_TILE = 512
SEQ_TILE = 512
SB_BLOCK = 256
RET_BLOCK = 256
XATTN_ROWS = 512

BIG = (
    ("w_in", (D_MODEL, IN_WIDTH), 1),
    ("w_sb_o", (SB_WIDTH, D_MODEL), 1),
    ("w_ret_o", (RET_V_WIDTH, D_MODEL), 0),
    ("w_mix_o", (D_MODEL, D_MODEL), 0),
    ("w_mem_q", (D_MODEL, D_MODEL), 0),
    ("w_mem_kv", (D_MODEL, 2 * D_MODEL), 1),
    ("w_mem_o", (D_MODEL, D_MODEL), 0),
    ("w_ffn_in", (D_MODEL, 2 * FFN_HIDDEN), 1),
    ("w_ffn_out", (FFN_HIDDEN, D_MODEL), 0),
)
SMALL = ("b_gate", "ln1_g", "ln1_b", "ln2_g", "ln2_b", "ln3_g", "ln3_b")
SMALL_LEN = 2 * D_MODEL + 6 * D_MODEL
SMALL_ROWS = SMALL_LEN // LANES
PACK_ROWS = SMALL_ROWS + D_MODEL // LANES
WEIGHT_ORDER = ("w_in", "b_gate", "w_sb_o", "w_ret_o", "w_mix_o", "ln1_g", "ln1_b", "w_mem_q", "w_mem_kv",
                "w_mem_o", "ln2_g", "ln2_b", "w_ffn_in", "w_ffn_out", "ln3_g", "ln3_b")


def _cparams():
    return pltpu.CompilerParams(vmem_limit_bytes=VMEM_LIMIT_BYTES)


def _dot(a, b, ca, cb):
    return lax.dot_general(a, b, (((ca,), (cb,)), ((), ())), preferred_element_type=F32)


def _sigmoid(x):
    return 1.0 / (1.0 + jnp.exp(-x))


def _mm(name, a, b, m, n, k, *, tm, tn, tk, epi, outs, ins=(), accs=(), ta=False, tb=False,
        a_off=(0, 0), b_off=(0, 0)):
    assert m % tm == 0 and n % tn == 0 and k % tk == 0, (name, m, n, k, tm, tn, tk)
    ni, nj, nk = m // tm, n // tn, k // tk
    assert not accs or nj == 1, name
    if ta:
        a_spec = pl.BlockSpec((tk, tm), lambda i, j, kk: (kk + a_off[0], i + a_off[1]))
    else:
        a_spec = pl.BlockSpec((tm, tk), lambda i, j, kk: (i + a_off[0], kk + a_off[1]))
    if tb:
        b_spec = pl.BlockSpec((tn, tk), lambda i, j, kk: (j + b_off[0], kk + b_off[1]))
    else:
        b_spec = pl.BlockSpec((tk, tn), lambda i, j, kk: (kk + b_off[0], j + b_off[1]))
    in_specs = [a_spec, b_spec]
    for _, bs, im in ins:
        in_specs.append(pl.BlockSpec(bs, lambda i, j, kk, im=im: im(i, j)))
    out_specs, out_shape = [], []
    for shape, dtype, bs, im in outs:
        out_specs.append(pl.BlockSpec(bs, lambda i, j, kk, im=im: im(i, j)))
        out_shape.append(jax.ShapeDtypeStruct(shape, dtype))
    for shape, dtype in accs:
        out_specs.append(pl.BlockSpec(shape, lambda i, j, kk, nd=len(shape): (0,) * nd))
        out_shape.append(jax.ShapeDtypeStruct(shape, dtype))
    n_in, n_out, n_acc = len(ins), len(outs), len(accs)
    ca, cb = (0 if ta else 1), (1 if tb else 0)

    def body(*refs):
        a_ref, b_ref = refs[:2]
        in_refs = refs[2:2 + n_in]
        out_refs = refs[2 + n_in:2 + n_in + n_out]
        acc_refs = refs[2 + n_in + n_out:2 + n_in + n_out + n_acc]
        scratch = refs[2 + n_in + n_out + n_acc:]
        i, j, kk = pl.program_id(0), pl.program_id(1), pl.program_id(2)
        part = _dot(a_ref[...].astype(BF16), b_ref[...].astype(BF16), ca, cb)

        def finish(acc):
            o_tiles, a_tiles = epi(acc, [r[...] for r in in_refs], i, j)
            for r, t in zip(out_refs, o_tiles, strict=True):
                r[...] = t.astype(r.dtype)
            if n_acc:
                @pl.when(i == 0)
                def _():
                    for r, t in zip(acc_refs, a_tiles, strict=True):
                        r[...] = t

                @pl.when(i > 0)
                def _():
                    for r, t in zip(acc_refs, a_tiles, strict=True):
                        r[...] += t

        if nk == 1:
            finish(part)
        else:
            acc_ref = scratch[0]

            @pl.when(kk == 0)
            def _():
                acc_ref[...] = part

            @pl.when(kk > 0)
            def _():
                acc_ref[...] += part

            @pl.when(kk == nk - 1)
            def _():
                finish(acc_ref[...])

    res = pl.pallas_call(
        body, name=name, grid=(ni, nj, nk), in_specs=in_specs, out_specs=out_specs, out_shape=out_shape,
        scratch_shapes=[pltpu.VMEM((tm, tn), F32)] if nk > 1 else [],
        compiler_params=_cparams(),
    )(a, b, *[x for x, _, _ in ins])
    return res


def _tile(tm, tn, dj=0):
    return (tm, tn), (lambda i, j: (i, j + dj))


def _rowvec(tn, dj=0):
    return (1, tn), (lambda i, j: (0, j + dj))


def _plain(acc, tiles, i, j):
    return [acc], []


def _ew(name, fn, ins, outs, rows, tr):
    assert rows % tr == 0, (name, rows, tr)
    in_specs = []
    for x in ins:
        if x.shape[0] == rows:
            in_specs.append(pl.BlockSpec((tr, x.shape[1]), lambda i: (i, 0)))
        else:
            in_specs.append(pl.BlockSpec(x.shape, lambda i: (0, 0)))
    n_in = len(ins)

    def body(*refs):
        res = fn(*[r[...] for r in refs[:n_in]])
        for r, t in zip(refs[n_in:], res, strict=True):
            r[...] = t.astype(r.dtype)

    return pl.pallas_call(
        body, name=name, grid=(rows // tr,), in_specs=in_specs,
        out_specs=[pl.BlockSpec((tr, w), lambda i: (i, 0)) for w, _ in outs],
        out_shape=[jax.ShapeDtypeStruct((rows, w), dt) for w, dt in outs],
        compiler_params=_cparams(),
    )(*ins)


def _cast_bf16(name, x):
    rows = x.shape[0]
    tr = next(t for t in (512, 256, 64) if rows % t == 0)
    return _ew(name, lambda v: (v,), [x], [(x.shape[1], BF16)], rows, tr)[0]


def _rope_tables(s):
    half = RET_QK // 2
    inv = 1.0 / (ROPE_BASE ** (jnp.arange(half, dtype=F32) / half))
    inv2 = jnp.concatenate([inv, inv]).reshape(1, RET_QK)
    sign = jnp.concatenate([-jnp.ones((half,), F32), jnp.ones((half,), F32)]).reshape(1, RET_QK)
    tr = ROW_TILE

    def body(inv_ref, sign_ref, cos_ref, sin_ref):
        i = pl.program_id(0)
        pos = (lax.broadcasted_iota(jnp.int32, (tr, RET_QK), 0) + i * tr).astype(F32)
        ang = pos * inv_ref[...]
        cos_ref[...] = jnp.cos(ang)
        sin_ref[...] = jnp.sin(ang) * sign_ref[...]

    vec = pl.BlockSpec((1, RET_QK), lambda i: (0, 0))
    blk = pl.BlockSpec((tr, RET_QK), lambda i: (i, 0))
    return pl.pallas_call(
        body, name="rope_tables", grid=(s // tr,), in_specs=[vec, vec], out_specs=[blk, blk],
        out_shape=[jax.ShapeDtypeStruct((s, RET_QK), F32)] * 2, compiler_params=_cparams(),
    )(inv2, sign)


def _swap_halves(x):
    return pltpu.roll(x, RET_QK // 2, 1)


def _norm(u):
    mu = jnp.mean(u, axis=-1, keepdims=True)
    d = u - mu
    var = jnp.mean(d * d, axis=-1, keepdims=True)
    rstd = lax.rsqrt(var + LN_EPS)
    return d * rstd, rstd


def _norm_bwd(dxh, xhat, rstd):
    m1 = jnp.mean(dxh, axis=-1, keepdims=True)
    m2 = jnp.mean(dxh * xhat, axis=-1, keepdims=True)
    return rstd * (dxh - m1 - xhat * m2)


def _colsum(t):
    return jnp.sum(t, axis=0, keepdims=True)


def _split_mm(t, tri):
    hi = t.astype(BF16)
    lo = (t - hi.astype(F32)).astype(BF16)
    return _dot(hi, tri, 1, 0) + _dot(lo, tri, 1, 0)


def _sb_masks():
    t = SB_BLOCK
    lane = lax.broadcasted_iota(jnp.int32, (1, LANES), 1)
    first = lane < SB_DIM
    m0 = jnp.where(first, 1.0, 0.0).astype(BF16)
    m1 = jnp.where(first, 0.0, 1.0).astype(BF16)
    row = lax.broadcasted_iota(jnp.int32, (t, t), 0)
    col = lax.broadcasted_iota(jnp.int32, (t, t), 1)
    return first, (m0, m1), row, col


def _sb_logits(qh, k, causal):
    z = _dot(qh, k, 1, 1)
    lp = jnp.log(1.0 + jnp.exp(-jnp.abs(z)))
    a = jnp.minimum(z, 0.0) - lp
    r = jnp.minimum(-z, 0.0) - lp
    if causal is not None:
        r = jnp.where(causal, r, 0.0)
    return a, r


def _sb_walk(i, block, l_ref, causal):
    block(i, causal)

    def alive():
        top = jnp.max(jnp.maximum(l_ref[0], l_ref[1]))
        return jnp.where(top > SB_DEAD, 1, 0)

    def cond(c):
        return jnp.logical_and(c[0] < i, c[1] > 0)

    def step(c):
        block(i - 1 - c[0], None)
        return c[0] + 1, alive()

    lax.while_loop(cond, step, (jnp.int32(0), alive()))


def _sb_fwd(qkv, s):
    t = SB_BLOCK
    nq = s // t

    def body(q_ref, k_ref, v_ref, o_ref, of_ref, l_ref, acc_ref):
        i = pl.program_id(1)
        first, hmask, row, col = _sb_masks()
        after = jnp.where(row > col, 1.0, 0.0).astype(BF16)
        causal = col < row
        q = q_ref[...]
        qh = (q * hmask[0], q * hmask[1])
        l_ref[...] = jnp.zeros_like(l_ref)
        acc_ref[...] = jnp.zeros_like(acc_ref)

        def block(kb, mask):
            start = pl.multiple_of(kb * t, t)
            k = k_ref[pl.ds(start, t), :]
            v = v_ref[pl.ds(start, t), :]
            pv = []
            for h in range(2):
                a, r = _sb_logits(qh[h], k, mask)
                w = jnp.exp(a + _split_mm(r, after) + l_ref[h])
                if mask is not None:
                    w = jnp.where(mask, w, 0.0)
                pv.append(_dot(w.astype(BF16), v, 1, 0))
                l_ref[h] = l_ref[h] + jnp.sum(r, axis=1, keepdims=True)
            acc_ref[...] += jnp.where(first, pv[0], pv[1])

        _sb_walk(i, block, l_ref, causal)
        o_ref[...] = acc_ref[...].astype(o_ref.dtype)
        of_ref[...] = acc_ref[...]

    nk_off = SB_WIDTH // LANES
    blk = pl.BlockSpec((t, LANES), lambda p, i: (i, p))
    return pl.pallas_call(
        body, name="sb_fwd", grid=(SB_HEADS // 2, nq),
        in_specs=[blk,
                  pl.BlockSpec((s, LANES), lambda p, i: (0, nk_off + p)),
                  pl.BlockSpec((s, LANES), lambda p, i: (0, 2 * nk_off + p))],
        out_specs=[blk, blk],
        out_shape=[jax.ShapeDtypeStruct((s, SB_WIDTH), BF16), jax.ShapeDtypeStruct((s, SB_WIDTH), F32)],
        scratch_shapes=[pltpu.VMEM((2, t, 1), F32), pltpu.VMEM((t, LANES), F32)],
        compiler_params=_cparams(),
    )(qkv, qkv, qkv)


def _sb_bwd(qkv, o, do, s):
    t = SB_BLOCK
    nq = s // t

    def body(q_ref, k_ref, v_ref, o_ref, do_ref, dq_ref, dk_ref, dv_ref, l_ref, e_ref, dq_acc, dk_acc, dv_acc):
        i = pl.program_id(1)
        first, hmask, row, col = _sb_masks()
        after = jnp.where(row > col, 1.0, 0.0).astype(BF16)
        from_here = jnp.where(row >= col, 1.0, 0.0).astype(BF16)
        causal = col < row

        @pl.when(i == 0)
        def _():
            dk_acc[...] = jnp.zeros_like(dk_acc)
            dv_acc[...] = jnp.zeros_like(dv_acc)

        q = q_ref[...]
        do_ = do_ref[...]
        qh = (q * hmask[0], q * hmask[1])
        doh = (do_ * hmask[0], do_ * hmask[1])
        prod = do_.astype(F32) * o_ref[...]
        total = (jnp.sum(jnp.where(first, prod, 0.0), axis=1, keepdims=True),
                 jnp.sum(jnp.where(first, 0.0, prod), axis=1, keepdims=True))
        l_ref[...] = jnp.zeros_like(l_ref)
        e_ref[...] = jnp.zeros_like(e_ref)
        dq_acc[...] = jnp.zeros_like(dq_acc)

        def block(kb, mask):
            start = pl.multiple_of(kb * t, t)
            k = k_ref[pl.ds(start, t), :]
            v = v_ref[pl.ds(start, t), :]
            dqs, dks, dvs = [], [], []
            for h in range(2):
                a, r = _sb_logits(qh[h], k, mask)
                w = jnp.exp(a + _split_mm(r, after) + l_ref[h])
                if mask is not None:
                    w = jnp.where(mask, w, 0.0)
                wb = w.astype(BF16)
                e = _dot(doh[h], v, 1, 1) * wb.astype(F32)
                before = total[h] - (_split_mm(e, from_here) + e_ref[h])
                dz = e - jnp.exp(a) * (e + before)
                if mask is not None:
                    dz = jnp.where(mask, dz, 0.0)
                dzb = dz.astype(BF16)
                dqs.append(_dot(dzb, k, 1, 0))
                dks.append(_dot(dzb, q, 0, 0))
                dvs.append(_dot(wb, do_, 0, 0))
                l_ref[h] = l_ref[h] + jnp.sum(r, axis=1, keepdims=True)
                e_ref[h] = e_ref[h] + jnp.sum(e, axis=1, keepdims=True)
            dq_acc[...] += jnp.where(first, dqs[0], dqs[1])
            dk_acc[pl.ds(start, t), :] += jnp.where(first, dks[0], dks[1])
            dv_acc[pl.ds(start, t), :] += jnp.where(first, dvs[0], dvs[1])

        _sb_walk(i, block, l_ref, causal)
        dq_ref[...] = (dq_acc[...] * SB_SCALE).astype(dq_ref.dtype)

        @pl.when(i == nq - 1)
        def _():
            dk_ref[...] = dk_acc[...].astype(dk_ref.dtype)
            dv_ref[...] = dv_acc[...].astype(dv_ref.dtype)

    nk_off = SB_WIDTH // LANES
    blk = pl.BlockSpec((t, LANES), lambda p, i: (i, p))
    col_blk = pl.BlockSpec((s, LANES), lambda p, i: (0, p))
    sds = jax.ShapeDtypeStruct((s, SB_WIDTH), BF16)
    return pl.pallas_call(
        body, name="sb_bwd", grid=(SB_HEADS // 2, nq),
        in_specs=[blk,
                  pl.BlockSpec((s, LANES), lambda p, i: (0, nk_off + p)),
                  pl.BlockSpec((s, LANES), lambda p, i: (0, 2 * nk_off + p)),
                  blk, blk],
        out_specs=[blk, col_blk, col_blk],
        out_shape=[sds, sds, sds],
        scratch_shapes=[pltpu.VMEM((2, t, 1), F32), pltpu.VMEM((2, t, 1), F32), pltpu.VMEM((t, LANES), F32),
                        pltpu.VMEM((s, LANES), F32), pltpu.VMEM((s, LANES), F32)],
        compiler_params=_cparams(),
    )(qkv, qkv, qkv, o, do)


def _ret_log_gamma():
    lg = np.log1p(-np.exp2(-5.0 - np.arange(RET_HEADS, dtype=np.float32))).astype(np.float32)
    return jnp.asarray(np.broadcast_to(lg[:, None, None], (RET_HEADS, 8, LANES)).copy())


def _ret_decays(lg_ref):
    c = RET_BLOCK
    lg = lg_ref[0, 0:1, 0:1]
    row = lax.broadcasted_iota(jnp.int32, (c, c), 0)
    col = lax.broadcasted_iota(jnp.int32, (c, c), 1)
    rel = (row - col).astype(F32)
    within = jnp.where(row >= col, jnp.exp(lg * jnp.maximum(rel, 0.0)), 0.0)
    idx = lax.broadcasted_iota(jnp.int32, (c, 1), 0).astype(F32)
    q_dec = jnp.exp(lg * (idx + 1.0))
    k_dec = jnp.exp(lg * (c - 1.0 - idx))
    chunk_dec = jnp.exp(lg * float(c))
    return within, q_dec, k_dec, chunk_dec


def _ret_specs(s, reverse=False):
    c = RET_BLOCK
    nc = s // c
    pos = (lambda n: nc - 1 - n) if reverse else (lambda n: n)
    qk_heads = RET_QK_WIDTH // RET_QK
    q_spec = pl.BlockSpec((c, RET_QK), lambda h, n: (pos(n), h))
    k_spec = pl.BlockSpec((c, RET_QK), lambda h, n: (pos(n), qk_heads + h))
    v_spec = pl.BlockSpec((c, RET_V), lambda h, n: (pos(n), h))
    lg_spec = pl.BlockSpec((1, 8, LANES), lambda h, n: (h, 0, 0))
    rope_spec = pl.BlockSpec((c, RET_QK), lambda h, n: (pos(n), 0))
    return nc, q_spec, k_spec, v_spec, lg_spec, rope_spec


def _ret_fwd(rqk, rv, rg, s):
    nc, q_spec, k_spec, v_spec, lg_spec, _ = _ret_specs(s)

    def body(q_ref, k_ref, v_ref, g_ref, lg_ref, r_ref, y_ref, state):
        n = pl.program_id(1)

        @pl.when(n == 0)
        def _():
            state[...] = jnp.zeros_like(state)

        within, q_dec, k_dec, chunk_dec = _ret_decays(lg_ref)
        q, k, v = q_ref[...], k_ref[...], v_ref[...]
        scores = _dot(q.astype(BF16), k.astype(BF16), 1, 1) * within
        out = _dot(scores.astype(BF16), v, 1, 0)
        out += _dot((q * q_dec).astype(BF16), state[...].astype(BF16), 1, 0)
        r_ref[...] = out
        xhat, _ = _norm(out)
        g = g_ref[...]
        y_ref[...] = (g * _sigmoid(g) * xhat).astype(y_ref.dtype)
        state[...] = state[...] * chunk_dec + _dot((k * k_dec).astype(BF16), v, 0, 0)

    return pl.pallas_call(
        body, name="ret_fwd", grid=(RET_HEADS, nc),
        in_specs=[q_spec, k_spec, v_spec, v_spec, lg_spec],
        out_specs=[v_spec, v_spec],
        out_shape=[jax.ShapeDtypeStruct((s, RET_V_WIDTH), F32), jax.ShapeDtypeStruct((s, RET_V_WIDTH), BF16)],
        scratch_shapes=[pltpu.VMEM((RET_QK, RET_V), F32)],
        compiler_params=_cparams(),
    )(rqk, rqk, rv, rg, _ret_log_gamma())


def _rope_bwd(d, cos, sin):
    return d * cos + _swap_halves(d * sin)


def _ret_bwd_q(rqk, rv, d_out, cos2, sin2, s):
    nc, q_spec, k_spec, v_spec, lg_spec, rope_spec = _ret_specs(s)

    def body(q_ref, k_ref, v_ref, d_ref, lg_ref, cos_ref, sin_ref, dq_ref, state):
        n = pl.program_id(1)

        @pl.when(n == 0)
        def _():
            state[...] = jnp.zeros_like(state)

        within, q_dec, k_dec, chunk_dec = _ret_decays(lg_ref)
        k, v, d = k_ref[...], v_ref[...], d_ref[...]
        st = state[...].astype(BF16)
        d_scores = _dot(d, v, 1, 1) * within
        dq = _dot(d_scores.astype(BF16), k.astype(BF16), 1, 0) + q_dec * _dot(d, st, 1, 1)
        dq_ref[...] = (_rope_bwd(dq, cos_ref[...], sin_ref[...]) * RET_SCALE).astype(dq_ref.dtype)
        state[...] = state[...] * chunk_dec + _dot((k * k_dec).astype(BF16), v, 0, 0)

    return pl.pallas_call(
        body, name="ret_bwd_q", grid=(RET_HEADS, nc),
        in_specs=[q_spec, k_spec, v_spec, v_spec, lg_spec, rope_spec, rope_spec],
        out_specs=q_spec,
        out_shape=jax.ShapeDtypeStruct((s, RET_QK_WIDTH), BF16),
        scratch_shapes=[pltpu.VMEM((RET_QK, RET_V), F32)],
        compiler_params=_cparams(),
    )(rqk, rqk, rv, d_out, _ret_log_gamma(), cos2, sin2)


def _ret_bwd_kv(rqk, rv, d_out, cos2, sin2, s):
    nc, q_spec, k_spec, v_spec, lg_spec, rope_spec = _ret_specs(s, reverse=True)

    def body(q_ref, k_ref, v_ref, d_ref, lg_ref, cos_ref, sin_ref, dk_ref, dv_ref, state):
        n = pl.program_id(1)

        @pl.when(n == 0)
        def _():
            state[...] = jnp.zeros_like(state)

        within, q_dec, k_dec, chunk_dec = _ret_decays(lg_ref)
        q, k, v, d = q_ref[...], k_ref[...], v_ref[...], d_ref[...]
        qb, kb = q.astype(BF16), k.astype(BF16)
        st = state[...].astype(BF16)
        scores = _dot(qb, kb, 1, 1) * within
        d_scores = _dot(d, v, 1, 1) * within
        dk = _dot(d_scores.astype(BF16), qb, 0, 0) + k_dec * _dot(v, st, 1, 1)
        dv = _dot(scores.astype(BF16), d, 0, 0) + k_dec * _dot(kb, st, 1, 0)
        dk_ref[...] = _rope_bwd(dk, cos_ref[...], sin_ref[...]).astype(dk_ref.dtype)
        dv_ref[...] = dv.astype(dv_ref.dtype)
        state[...] = state[...] * chunk_dec + _dot((q * q_dec).astype(BF16), d, 0, 0)

    c = RET_BLOCK
    dk_spec = pl.BlockSpec((c, RET_QK), lambda h, n: (nc - 1 - n, h))
    return pl.pallas_call(
        body, name="ret_bwd_kv", grid=(RET_HEADS, nc),
        in_specs=[q_spec, k_spec, v_spec, v_spec, lg_spec, rope_spec, rope_spec],
        out_specs=[dk_spec, v_spec],
        out_shape=[jax.ShapeDtypeStruct((s, RET_QK_WIDTH), BF16), jax.ShapeDtypeStruct((s, RET_V_WIDTH), BF16)],
        scratch_shapes=[pltpu.VMEM((RET_QK, RET_V), F32)],
        compiler_params=_cparams(),
    )(rqk, rqk, rv, d_out, _ret_log_gamma(), cos2, sin2)


def _xattn_probs(q, k):
    sc = _dot(q, k, 1, 1)
    sc = sc - jnp.max(sc, axis=-1, keepdims=True)
    p = jnp.exp(sc)
    return p / jnp.sum(p, axis=-1, keepdims=True)


def _xattn_fwd(qm, kv, s):
    tq = XATTN_ROWS

    def body(q_ref, kv_ref, o_ref):
        for h in range(MEM_HEADS):
            sl = slice(h * MEM_DIM, (h + 1) * MEM_DIM)
            sv = slice(D_MODEL + h * MEM_DIM, D_MODEL + (h + 1) * MEM_DIM)
            p = _xattn_probs(q_ref[:, sl], kv_ref[:, sl])
            o_ref[:, sl] = _dot(p.astype(BF16), kv_ref[:, sv], 1, 0).astype(o_ref.dtype)

    return pl.pallas_call(
        body, name="xattn_fwd", grid=(s // tq,),
        in_specs=[pl.BlockSpec((tq, D_MODEL), lambda i: (i, 0)),
                  pl.BlockSpec((MEM_LEN, 2 * D_MODEL), lambda i: (0, 0))],
        out_specs=pl.BlockSpec((tq, D_MODEL), lambda i: (i, 0)),
        out_shape=jax.ShapeDtypeStruct((s, D_MODEL), BF16),
        compiler_params=_cparams(),
    )(qm, kv)


def _xattn_bwd(qm, kv, do, s):
    tq = XATTN_ROWS

    def body(q_ref, kv_ref, do_ref, dq_ref, dkv_ref):
        i = pl.program_id(0)

        @pl.when(i == 0)
        def _():
            dkv_ref[...] = jnp.zeros_like(dkv_ref)

        for h in range(MEM_HEADS):
            sl = slice(h * MEM_DIM, (h + 1) * MEM_DIM)
            sv = slice(D_MODEL + h * MEM_DIM, D_MODEL + (h + 1) * MEM_DIM)
            q, k, v, d = q_ref[:, sl], kv_ref[:, sl], kv_ref[:, sv], do_ref[:, sl]
            p = _xattn_probs(q, k)
            dp = _dot(d, v, 1, 1)
            ds = (p * (dp - jnp.sum(p * dp, axis=-1, keepdims=True))).astype(BF16)
            dq_ref[:, sl] = (_dot(ds, k, 1, 0) * MEM_SCALE).astype(dq_ref.dtype)
            dkv_ref[:, sl] += _dot(ds, q, 0, 0)
            dkv_ref[:, sv] += _dot(p.astype(BF16), d, 0, 0)

    row_blk = pl.BlockSpec((tq, D_MODEL), lambda i: (i, 0))
    kv_blk = pl.BlockSpec((MEM_LEN, 2 * D_MODEL), lambda i: (0, 0))
    return pl.pallas_call(
        body, name="xattn_bwd", grid=(s // tq,),
        in_specs=[row_blk, kv_blk, row_blk],
        out_specs=[row_blk, kv_blk],
        out_shape=[jax.ShapeDtypeStruct((s, D_MODEL), BF16), jax.ShapeDtypeStruct((MEM_LEN, 2 * D_MODEL), F32)],
        compiler_params=_cparams(),
    )(qm, kv, do)


def _place():
    x, y, c = lax.axis_index("x"), lax.axis_index("y"), lax.axis_index("c")
    others = [(1 - x, y), (x, 1 - y), (1 - x, 1 - y)]
    return x, y, c, others


def _slab(ref, axis, chip, size):
    start = pl.multiple_of(chip * size, LANES if axis == 1 else 16)
    if axis == 0:
        return ref.at[pl.ds(start, size), :]
    return ref.at[:, pl.ds(start, size)]


def _weight_gather(shards):
    nw = len(BIG)
    any_spec = pl.BlockSpec(memory_space=pl.ANY)

    def body(*refs):
        shard, full = refs[:nw], refs[nw:2 * nw]
        send_sems, recv_sems, local_sems = refs[2 * nw:]
        x, y, c, others = _place()
        mine = 2 * x + y
        sent, local = [], []
        for w, (_, shape, axis) in enumerate(BIG):
            size = shape[axis] // N_CHIPS
            cp = pltpu.make_async_copy(shard[w], _slab(full[w], axis, mine, size), local_sems.at[w])
            cp.start()
            local.append(cp)
            for t, (qx, qy) in enumerate(others):
                cp = pltpu.make_async_remote_copy(
                    src_ref=shard[w], dst_ref=_slab(full[w], axis, mine, size),
                    send_sem=send_sems.at[3 * w + t], recv_sem=recv_sems.at[3 * w + t],
                    device_id=(qx, qy, c), device_id_type=MESH)
                cp.start()
                sent.append(cp)
        for w, (_, shape, axis) in enumerate(BIG):
            size = shape[axis] // N_CHIPS
            for t, (qx, qy) in enumerate(others):
                pltpu.make_async_remote_copy(
                    src_ref=shard[w], dst_ref=_slab(full[w], axis, 2 * qx + qy, size),
                    send_sem=send_sems.at[3 * w + t], recv_sem=recv_sems.at[3 * w + t],
                    device_id=(qx, qy, c), device_id_type=MESH).wait_recv()
        for cp in sent:
            cp.wait_send()
        for cp in local:
            cp.wait()

    return pl.pallas_call(
        body, name="weight_gather",
        in_specs=[any_spec] * nw, out_specs=[any_spec] * nw,
        out_shape=[jax.ShapeDtypeStruct(shape, BF16) for _, shape, _ in BIG],
        scratch_shapes=[pltpu.SemaphoreType.DMA((3 * nw,)), pltpu.SemaphoreType.DMA((3 * nw,)),
                        pltpu.SemaphoreType.DMA((nw,))],
    )(*shards)


def _shard_shape(shape, axis):
    return tuple(d // N_CHIPS if a == axis else d for a, d in enumerate(shape))


def _grad_exchange(grads):
    nw = len(BIG)
    any_spec = pl.BlockSpec(memory_space=pl.ANY)

    def body(*refs):
        grad, stack = refs[:nw], refs[nw:2 * nw]
        send_sems, recv_sems, local_sems = refs[2 * nw:]
        x, y, c, others = _place()
        mine = 2 * x + y
        me, sibling = (x, y, c), (x, y, 1 - c)

        def dev(px, py, pc):
            return 4 * px + 2 * py + pc

        def copy(w, n, src, slot, to):
            return pltpu.make_async_remote_copy(
                src_ref=src, dst_ref=stack[w].at[slot], send_sem=send_sems.at[7 * w + n],
                recv_sem=recv_sems.at[7 * w + n], device_id=to, device_id_type=MESH)

        sent, local = [], []
        for w, (_, shape, axis) in enumerate(BIG):
            size = shape[axis] // N_CHIPS
            own = _slab(grad[w], axis, mine, size)
            cp = pltpu.make_async_copy(own, stack[w].at[dev(*me)], local_sems.at[w])
            cp.start()
            local.append(cp)
            first = [copy(w, 0, own, dev(*me), sibling)]
            first += [copy(w, 1 + t, _slab(grad[w], axis, 2 * qx + qy, size), dev(*me), (qx, qy, c))
                      for t, (qx, qy) in enumerate(others)]
            for cp in first:
                cp.start()
            sent += first
        for w in range(nw):
            for t, (qx, qy) in enumerate(others):
                got = stack[w].at[dev(qx, qy, c)]
                copy(w, 1 + t, got, dev(qx, qy, c), me).wait_recv()
                cp = copy(w, 4 + t, got, dev(qx, qy, c), sibling)
                cp.start()
                sent.append(cp)
        for w in range(nw):
            copy(w, 0, stack[w].at[dev(*sibling)], dev(*sibling), me).wait_recv()
            for t, (qx, qy) in enumerate(others):
                copy(w, 4 + t, stack[w].at[dev(qx, qy, 1 - c)], dev(qx, qy, 1 - c), me).wait_recv()
        for cp in sent:
            cp.wait_send()
        for cp in local:
            cp.wait()

    return pl.pallas_call(
        body, name="grad_exchange",
        in_specs=[any_spec] * nw, out_specs=[any_spec] * nw,
        out_shape=[jax.ShapeDtypeStruct((N_DEV,) + _shard_shape(shape, axis), BF16) for _, shape, axis in BIG],
        scratch_shapes=[pltpu.SemaphoreType.DMA((7 * nw,)), pltpu.SemaphoreType.DMA((7 * nw,)),
                        pltpu.SemaphoreType.DMA((nw,))],
    )(*grads)


def _adamw(w, g, m, v):
    m = ADAM_B1 * m + (1.0 - ADAM_B1) * g
    v = ADAM_B2 * v + (1.0 - ADAM_B2) * (g * g)
    m_hat = m / (1.0 - ADAM_B1 ** ADAM_STEP)
    v_hat = v / (1.0 - ADAM_B2 ** ADAM_STEP)
    delta = -ADAM_LR * (m_hat / (jnp.sqrt(v_hat) + ADAM_EPS) + ADAM_WD * w)
    return delta, m, v


def _reduce_adamw(name, stack, w, m, v):
    rows, cols = w.shape
    tr = next(t for t in (256, 128, 64) if rows % t == 0)

    def body(s_ref, w_ref, m_ref, v_ref, g_ref, d_ref, nm_ref, nv_ref):
        g = s_ref[0].astype(F32)
        for d in range(1, N_DEV):
            g = g + s_ref[d].astype(F32)
        g_ref[...] = g
        d_ref[...], nm_ref[...], nv_ref[...] = _adamw(w_ref[...], g, m_ref[...], v_ref[...])

    blk = pl.BlockSpec((tr, cols), lambda i: (i, 0))
    return pl.pallas_call(
        body, name=name, grid=(rows // tr,),
        in_specs=[pl.BlockSpec((N_DEV, tr, cols), lambda i: (0, i, 0)), blk, blk, blk],
        out_specs=[blk] * 4, out_shape=[jax.ShapeDtypeStruct((rows, cols), F32)] * 4,
        compiler_params=_cparams(),
    )(stack, w, m, v)


def _small_step(pack, w, m, v):
    def body(p_ref, w_ref, m_ref, v_ref, g_ref, d_ref, nm_ref, nv_ref, loss_ref, all_ref, send_sems, recv_sems):
        x, y, c, _ = _place()
        me = 4 * x + 2 * y + c
        all_ref[me] = p_ref[...]
        sent = []
        for n in range(1, N_DEV):
            peer = me ^ n
            cp = pltpu.make_async_remote_copy(
                src_ref=p_ref, dst_ref=all_ref.at[me], send_sem=send_sems.at[n - 1], recv_sem=recv_sems.at[n - 1],
                device_id=(peer // 4, (peer // 2) % 2, peer % 2), device_id_type=MESH)
            cp.start()
            sent.append(cp)
        for n in range(1, N_DEV):
            peer = me ^ n
            pltpu.make_async_remote_copy(
                src_ref=p_ref, dst_ref=all_ref.at[peer], send_sem=send_sems.at[n - 1], recv_sem=recv_sems.at[n - 1],
                device_id=(peer // 4, (peer // 2) % 2, peer % 2), device_id_type=MESH).wait_recv()
        for cp in sent:
            cp.wait_send()
        tot = all_ref[0]
        for d in range(1, N_DEV):
            tot = tot + all_ref[d]
        g = tot[:SMALL_ROWS]
        g_ref[...] = g
        d_ref[...], nm_ref[...], nv_ref[...] = _adamw(w_ref[...], g, m_ref[...], v_ref[...])
        loss_ref[...] = jnp.sum(jnp.sum(tot[SMALL_ROWS:], axis=1, keepdims=True), axis=0, keepdims=True)

    vm = pl.BlockSpec(memory_space=pltpu.VMEM)
    small = jax.ShapeDtypeStruct((SMALL_ROWS, LANES), F32)
    return pl.pallas_call(
        body, name="small_step",
        in_specs=[vm] * 4, out_specs=[vm] * 5,
        out_shape=[small] * 4 + [jax.ShapeDtypeStruct((1, 1), F32)],
        scratch_shapes=[pltpu.VMEM((N_DEV, PACK_ROWS, LANES), F32),
                        pltpu.SemaphoreType.DMA((N_DEV - 1,)), pltpu.SemaphoreType.DMA((N_DEV - 1,))],
    )(pack, w, m, v)


def _layer_step(x, mem, tgt, wt, vec):
    s = x.shape[0]
    d = D_MODEL
    tm = ROW_TILE
    w_in = wt["w_in"]
    cos2, sin2 = _rope_tables(s)
    xb = _cast_bf16("cast_x", x)
    bf = lambda w: ((s, w), BF16)
    f32 = lambda w: ((s, w), F32)

    (sb_qkv,) = _mm(
        "in_sb", xb, w_in, s, 3 * SB_WIDTH, d, tm=tm, tn=512, tk=d,
        epi=lambda acc, t, i, j: ([acc * jnp.where(j == 0, SB_SCALE, 1.0)], []),
        outs=[(*bf(3 * SB_WIDTH), *_tile(tm, 512))])

    def rope_epi(acc, t, i, j):
        cos, sin = t
        scale = jnp.where(j == 0, RET_SCALE, 1.0)
        parts = []
        for g in range(512 // RET_QK):
            xg = acc[:, g * RET_QK:(g + 1) * RET_QK]
            parts.append((xg * cos + _swap_halves(xg) * sin) * scale)
        return [jnp.concatenate(parts, axis=1)], []

    rope_in = ((tm, RET_QK), lambda i, j: (i, 0))
    (rqk,) = _mm("in_rqk", xb, w_in, s, 2 * RET_QK_WIDTH, d, tm=tm, tn=512, tk=d, b_off=(0, OFF_RET_Q // 512),
                 epi=rope_epi, ins=[(cos2, *rope_in), (sin2, *rope_in)],
                 outs=[(*f32(2 * RET_QK_WIDTH), *_tile(tm, 512))])
    (rv,) = _mm("in_rv", xb, w_in, s, RET_V_WIDTH, d, tm=tm, tn=512, tk=d, b_off=(0, OFF_RET_V // 512),
                epi=_plain, outs=[(*bf(RET_V_WIDTH), *_tile(tm, 512))])
    (rg,) = _mm("in_rg", xb, w_in, s, RET_V_WIDTH, d, tm=tm, tn=512, tk=d, b_off=(0, OFF_RET_G // 512),
                epi=_plain, outs=[(*f32(RET_V_WIDTH), *_tile(tm, 512))])
    (gates,) = _mm("in_gate", xb, w_in, s, 2 * d, d, tm=tm, tn=512, tk=d, b_off=(0, OFF_GATE // 512),
                   epi=lambda acc, t, i, j: ([_sigmoid(acc + t[0])], []),
                   ins=[(vec["b_gate"], *_rowvec(512))], outs=[(*f32(2 * d), *_tile(tm, 512))])

    sb_out, sb_out_f32 = _sb_fwd(sb_qkv, s)
    ret, gated = _ret_fwd(rqk, rv, rg, s)
    (y_sb,) = _mm("sb_o", sb_out, wt["w_sb_o"], s, d, SB_WIDTH, tm=tm, tn=512, tk=SB_WIDTH, epi=_plain,
                  outs=[(*f32(d), *_tile(tm, 512))])
    y_ret, mixin = _mm(
        "ret_o", gated, wt["w_ret_o"], s, d, RET_V_WIDTH, tm=tm, tn=512, tk=RET_V_WIDTH,
        epi=lambda acc, t, i, j: ([acc, t[0] * t[2] + t[1] * acc], []),
        ins=[(gates, *_tile(tm, 512)), (gates, *_tile(tm, 512, d // 512)), (y_sb, *_tile(tm, 512))],
        outs=[(*f32(d), *_tile(tm, 512)), (*bf(d), *_tile(tm, 512))])

    def ln_epi(acc, t, i, j):
        res, g, b = t
        xhat, rstd = _norm(DN_ALPHA * res + acc)
        y = xhat * g + b
        return [y, y, xhat, rstd], []

    full = _tile(tm, d)
    col1 = ((tm, 1), lambda i, j: (i, 0))
    ln_outs = [(*f32(d), *full), (*bf(d), *full), (*f32(d), *full), ((s, 1), F32, *col1)]
    x1, x1b, xhat1, rstd1 = _mm(
        "mix_o", mixin, wt["w_mix_o"], s, d, d, tm=tm, tn=d, tk=d, epi=ln_epi,
        ins=[(x, *full), (vec["ln1_g"], *_rowvec(d)), (vec["ln1_b"], *_rowvec(d))], outs=ln_outs)

    (qm,) = _mm("mem_q", x1b, wt["w_mem_q"], s, d, d, tm=tm, tn=512, tk=d,
                epi=lambda acc, t, i, j: ([acc * MEM_SCALE], []), outs=[(*bf(d), *_tile(tm, 512))])
    (kv,) = _mm("mem_kv", mem, wt["w_mem_kv"], MEM_LEN, 2 * d, d, tm=MEM_LEN, tn=512, tk=d, epi=_plain,
                outs=[((MEM_LEN, 2 * d), BF16, *_tile(MEM_LEN, 512))])
    att = _xattn_fwd(qm, kv, s)
    x2, x2b, xhat2, rstd2 = _mm(
        "mem_o", att, wt["w_mem_o"], s, d, d, tm=tm, tn=d, tk=d, epi=ln_epi,
        ins=[(x1, *full), (vec["ln2_g"], *_rowvec(d)), (vec["ln2_b"], *_rowvec(d))], outs=ln_outs)

    fh = FFN_HIDDEN
    tf = fh // 2
    (f1,) = _mm("ffn_in1", x2b, wt["w_ffn_in"], s, fh, d, tm=tm, tn=tf, tk=d, epi=_plain,
                outs=[(*f32(fh), *_tile(tm, tf))])
    f2, act = _mm(
        "ffn_in2", x2b, wt["w_ffn_in"], s, fh, d, tm=tm, tn=tf, tk=d, b_off=(0, 2),
        epi=lambda acc, t, i, j: ([acc, t[0] * _sigmoid(t[0]) * acc], []),
        ins=[(f1, *_tile(tm, tf))], outs=[(*f32(fh), *_tile(tm, tf)), (*bf(fh), *_tile(tm, tf))])

    def head_epi(acc, t, i, j):
        res, g, b, target = t
        xhat, rstd = _norm(DN_ALPHA * res + acc)
        err = xhat * g + b - target
        dy = err * (1.0 / d)
        du = _norm_bwd(dy * g, xhat, rstd)
        return [du, du], [_colsum(dy * xhat), _colsum(dy), _colsum(err * err) * (0.5 / d)]

    vec_acc = ((1, d), F32)
    du3, du3b, dg3, db3, loss_cols = _mm(
        "ffn_out", act, wt["w_ffn_out"], s, d, fh, tm=tm, tn=d, tk=tf, epi=head_epi,
        ins=[(x2, *full), (vec["ln3_g"], *_rowvec(d)), (vec["ln3_b"], *_rowvec(d)), (tgt, *full)],
        outs=[(*f32(d), *full), (*bf(d), *full)], accs=[vec_acc] * 3)

    grads = {}
    ts = min(SEQ_TILE, s)

    def wgrad(name, a, b, m, n, tm_, tn_, tk_=None):
        (g,) = _mm(name, a, b, m, n, a.shape[0], tm=tm_, tn=tn_, tk=tk_ or ts, ta=True, epi=_plain,
                   outs=[((m, n), BF16, *_tile(tm_, tn_))])
        return g

    def ffn_bwd_epi(acc, t, i, j):
        a, b = t
        sg = _sigmoid(a)
        return [acc * b * (sg * (1.0 + a * (1.0 - sg))), acc * (a * sg)], []

    df1, df2 = _mm(
        "ffn_out_t", du3b, wt["w_ffn_out"], s, fh, d, tm=tm, tn=tf, tk=d, tb=True, epi=ffn_bwd_epi,
        ins=[(f1, *_tile(tm, tf)), (f2, *_tile(tm, tf))],
        outs=[(*bf(fh), *_tile(tm, tf)), (*bf(fh), *_tile(tm, tf))])
    grads["w_ffn_out"] = wgrad("g_ffn_out", act, du3b, fh, d, tf, d)
    grads["w_ffn_in"] = jnp.concatenate(
        [wgrad("g_ffn_in1", x2b, df1, d, fh, d, tf), wgrad("g_ffn_in2", x2b, df2, d, fh, d, tf)], axis=1)
    (dx2a,) = _mm("ffn_in1_t", df1, wt["w_ffn_in"], s, d, fh, tm=tm, tn=d, tk=tf, tb=True, epi=_plain,
                  outs=[(*f32(d), *full)])

    def ln_bwd(name, a, b, k, tk, b_off, more, scales, xhat, rstd, g):
        def epi(acc, t, i, j):
            *extra, xh, rs, gg = t
            dy = acc
            for e, sc in zip(extra, scales, strict=True):
                dy = dy + e * sc
            du = _norm_bwd(dy * gg, xh, rs)
            return [du, du], [_colsum(dy * xh), _colsum(dy)]

        return _mm(name, a, b, s, d, k, tm=tm, tn=d, tk=tk, tb=True, b_off=b_off, epi=epi,
                   ins=[(e, *full) for e in more] + [(xhat, *full), (rstd, *col1), (g, *_rowvec(d))],
                   outs=[(*f32(d), *full), (*bf(d), *full)], accs=[vec_acc] * 2)

    du2, du2b, dg2, db2 = ln_bwd("ffn_in2_t", df2, wt["w_ffn_in"], fh, tf, (0, 2), [dx2a, du3], [1.0, DN_ALPHA],
                                 xhat2, rstd2, vec["ln2_g"])

    (datt,) = _mm("mem_o_t", du2b, wt["w_mem_o"], s, d, d, tm=tm, tn=512, tk=d, tb=True, epi=_plain,
                  outs=[(*bf(d), *_tile(tm, 512))])
    grads["w_mem_o"] = wgrad("g_mem_o", att, du2b, d, d, d, d)
    dqm, dkv = _xattn_bwd(qm, kv, datt, s)
    grads["w_mem_q"] = wgrad("g_mem_q", x1b, dqm, d, d, d, d)
    grads["w_mem_kv"] = wgrad("g_mem_kv", mem, dkv, d, 2 * d, d, d, MEM_LEN)
    du1, du1b, dg1, db1 = ln_bwd("mem_q_t", dqm, wt["w_mem_q"], d, d, (0, 0), [du2], [DN_ALPHA],
                                 xhat1, rstd1, vec["ln1_g"])

    def merge_bwd_epi(acc, t, i, j):
        g0, g1, ysb, yret = t
        dgate0 = acc * ysb * (g0 * (1.0 - g0))
        dgate1 = acc * yret * (g1 * (1.0 - g1))
        return [dgate0, dgate1, acc * g0, acc * g1], [_colsum(dgate0), _colsum(dgate1)]

    tm_merge = 256
    mfull = _tile(tm_merge, d)
    dgate0, dgate1, dy_sb, dy_ret, dbg0, dbg1 = _mm(
        "mix_o_t", du1b, wt["w_mix_o"], s, d, d, tm=tm_merge, tn=d, tk=d, tb=True, epi=merge_bwd_epi,
        ins=[(gates, *mfull), (gates, *_tile(tm_merge, d, 1)), (y_sb, *mfull), (y_ret, *mfull)],
        outs=[(*bf(d), *mfull)] * 4, accs=[vec_acc] * 2)
    grads["w_mix_o"] = wgrad("g_mix_o", mixin, du1b, d, d, d, d)
    grads["w_sb_o"] = wgrad("g_sb_o", sb_out, dy_sb, SB_WIDTH, d, SB_WIDTH, d)
    grads["w_ret_o"] = wgrad("g_ret_o", gated, dy_ret, RET_V_WIDTH, d, RET_V_WIDTH, d)
    (dsb_out,) = _mm("sb_o_t", dy_sb, wt["w_sb_o"], s, SB_WIDTH, d, tm=tm, tn=SB_WIDTH, tk=d, tb=True, epi=_plain,
                     outs=[(*bf(SB_WIDTH), *_tile(tm, SB_WIDTH))])

    def gate_norm_bwd_epi(acc, t, i, j):
        r, g = t
        drg, dret = [], []
        for h in range(512 // RET_V):
            sl = slice(h * RET_V, (h + 1) * RET_V)
            xhat, rstd = _norm(r[:, sl])
            gg, dd = g[:, sl], acc[:, sl]
            sg = _sigmoid(gg)
            drg.append(dd * xhat * (sg * (1.0 + gg * (1.0 - sg))))
            dret.append(_norm_bwd(dd * (gg * sg), xhat, rstd))
        return [jnp.concatenate(drg, axis=1), jnp.concatenate(dret, axis=1)], []

    drg, dret = _mm(
        "ret_o_t", dy_ret, wt["w_ret_o"], s, RET_V_WIDTH, d, tm=tm, tn=512, tk=d, tb=True, epi=gate_norm_bwd_epi,
        ins=[(ret, *_tile(tm, 512)), (rg, *_tile(tm, 512))],
        outs=[(*bf(RET_V_WIDTH), *_tile(tm, 512))] * 2)

    drq = _ret_bwd_q(rqk, rv, dret, cos2, sin2, s)
    drk, drv = _ret_bwd_kv(rqk, rv, dret, cos2, sin2, s)
    dsq, dsk, dsv = _sb_bwd(sb_qkv, sb_out_f32, dsb_out, s)

    dh = jnp.concatenate([dsq, dsk, dsv, drq, drk, drv, drg, dgate0, dgate1], axis=1)
    grads["w_in"] = wgrad("g_in", xb, dh, d, IN_WIDTH, d, IN_WIDTH // N_CHIPS)
    (grad_x,) = _mm("in_t", dh, w_in, s, d, IN_WIDTH, tm=tm, tn=d, tk=512, tb=True,
                    epi=lambda acc, t, i, j: ([acc + DN_ALPHA * t[0]], []),
                    ins=[(du1, *full)], outs=[(*f32(d), *full)])

    small = {"b_gate": jnp.concatenate([dbg0, dbg1], axis=1), "ln1_g": dg1, "ln1_b": db1, "ln2_g": dg2,
             "ln2_b": db2, "ln3_g": dg3, "ln3_b": db3}
    return grad_x, grads, small, loss_cols


def kernel(x, mem, w_in, b_gate, w_sb_o, w_ret_o, w_mix_o, ln1_g, ln1_b, w_mem_q, w_mem_kv, w_mem_o, ln2_g, ln2_b, w_ffn_in, w_ffn_out, ln3_g, ln3_b, loss_target, m_w_in, m_b_gate, m_w_sb_o, m_w_ret_o, m_w_mix_o, m_ln1_g, m_ln1_b, m_w_mem_q, m_w_mem_kv, m_w_mem_o, m_ln2_g, m_ln2_b, m_w_ffn_in, m_w_ffn_out, m_ln3_g, m_ln3_b, v_w_in, v_b_gate, v_w_sb_o, v_w_ret_o, v_w_mix_o, v_ln1_g, v_ln1_b, v_w_mem_q, v_w_mem_kv, v_w_mem_o, v_ln2_g, v_ln2_b, v_w_ffn_in, v_w_ffn_out, v_ln3_g, v_ln3_b):
    given = dict(locals())
    s = x.shape[1]
    x2d = x.reshape(s, D_MODEL)
    tgt = loss_target.reshape(s, D_MODEL)
    mem2d = mem.reshape(MEM_LEN, D_MODEL)
    shard = {name: given[name].reshape(_shard_shape(shape, axis)) for name, shape, axis in BIG}
    vec = {name: given[name] for name in SMALL}

    shards_bf = [_cast_bf16("cast_" + name, shard[name]) for name, _, _ in BIG]
    wt = dict(zip([name for name, _, _ in BIG], _weight_gather(shards_bf), strict=True))

    grad_x, grads, small, loss_cols = _layer_step(x2d, mem2d, tgt, wt, vec)

    stacks = _grad_exchange([grads[name] for name, _, _ in BIG])
    out = {}
    for (name, shape, axis), stack in zip(BIG, stacks, strict=True):
        shp = given[name].shape
        res = _reduce_adamw("adamw_" + name, stack, shard[name], given["m_" + name].reshape(stack.shape[1:]),
                            given["v_" + name].reshape(stack.shape[1:]))
        out[name] = [r.reshape(shp) for r in res]

    pack = jnp.concatenate([small[name] for name in SMALL] + [loss_cols], axis=1).reshape(PACK_ROWS, LANES)
    cat = lambda pre: jnp.concatenate([given[pre + name] for name in SMALL], axis=1).reshape(SMALL_ROWS, LANES)
    *res, loss = _small_step(pack, cat(""), cat("m_"), cat("v_"))
    flat = [r.reshape(1, SMALL_LEN) for r in res]
    off = 0
    for name in SMALL:
        n = given[name].shape[1]
        out[name] = [r[:, off:off + n] for r in flat]
        off += n

    return (loss.reshape(()), grad_x.reshape(x.shape),
            *[out[name][0] for name in WEIGHT_ORDER], *[out[name][1] for name in WEIGHT_ORDER],
            *[out[name][2] for name in WEIGHT_ORDER], *[out[name][3] for name in WEIGHT_ORDER])
```

```python
import functools

import jax
import jax.numpy as jnp
import numpy as np
from jax import lax
from jax.experimental import pallas as pl
from jax.experimental.pallas import tpu as pltpu

F32, BF16 = jnp.float32, jnp.bfloat16
MESH = pl.DeviceIdType.MESH

D_MODEL = 1024
MEM_LEN = 256
SB_HEADS, SB_DIM, SB_WIDTH = 8, 64, 512
RET_HEADS, RET_QK, RET_V = 4, 128, 256
RET_QK_WIDTH, RET_V_WIDTH = 512, 1024
ROPE_BASE = 10000.0
MEM_HEADS, MEM_DIM = 4, 256
FFN_HIDDEN = 2816
IN_WIDTH = 6656
OFF_RET_Q, OFF_RET_V, OFF_RET_G, OFF_GATE = 1536, 2560, 3584, 4608
DN_ALPHA = 2.0 ** 0.25
LN_EPS = 1e-5
SB_SCALE = SB_DIM ** -0.5
SB_DEAD = -110.0
RET_SCALE = RET_QK ** -0.5
MEM_SCALE = MEM_DIM ** -0.5
ADAM_LR, ADAM_B1, ADAM_B2, ADAM_EPS, ADAM_WD, ADAM_STEP = 0.001, 0.9, 0.999, 1e-08, 0.01, 10

N_DEV, N_CHIPS = 8, 4

LANES = 128
VMEM_LIMIT_BYTES = 52 * 2 ** 20
ROW_TILE = 512
WIDE_TILE = 1024
SEQ_TILE = 1024
SB_BLOCK = 256
RET_BLOCK = 256
XATTN_ROWS = 512

BIG = (
    ("w_in", (D_MODEL, IN_WIDTH), 1),
    ("w_sb_o", (SB_WIDTH, D_MODEL), 1),
    ("w_ret_o", (RET_V_WIDTH, D_MODEL), 0),
    ("w_mix_o", (D_MODEL, D_MODEL), 0),
    ("w_mem_q", (D_MODEL, D_MODEL), 0),
    ("w_mem_kv", (D_MODEL, 2 * D_MODEL), 1),
    ("w_mem_o", (D_MODEL, D_MODEL), 0),
    ("w_ffn_in", (D_MODEL, 2 * FFN_HIDDEN), 1),
    ("w_ffn_out", (FFN_HIDDEN, D_MODEL), 0),
)
SMALL = ("b_gate", "ln1_g", "ln1_b", "ln2_g", "ln2_b", "ln3_g", "ln3_b")
SMALL_LEN = 2 * D_MODEL + 6 * D_MODEL
SMALL_ROWS = SMALL_LEN // LANES
PACK_ROWS = SMALL_ROWS + D_MODEL // LANES
WEIGHT_ORDER = ("w_in", "b_gate", "w_sb_o", "w_ret_o", "w_mix_o", "ln1_g", "ln1_b", "w_mem_q", "w_mem_kv",
                "w_mem_o", "ln2_g", "ln2_b", "w_ffn_in", "w_ffn_out", "ln3_g", "ln3_b")


def _cparams():
    return pltpu.CompilerParams(vmem_limit_bytes=VMEM_LIMIT_BYTES)


def _dot(a, b, ca, cb):
    return lax.dot_general(a, b, (((ca,), (cb,)), ((), ())), preferred_element_type=F32)


def _sigmoid(x):
    return 1.0 / (1.0 + jnp.exp(-x))


def _mm(name, a, b, m, n, k, *, tm, tn, tk, epi, outs, ins=(), accs=(), ta=False, tb=False,
        a_off=(0, 0), b_off=(0, 0), j_outer=False):
    assert m % tm == 0 and n % tn == 0 and k % tk == 0, (name, m, n, k, tm, tn, tk)
    ni, nj, nk = m // tm, n // tn, k // tk
    assert not accs or nj == 1, name
    ij = (lambda g0, g1: (g1, g0)) if j_outer else (lambda g0, g1: (g0, g1))

    def spec(block, index):
        return pl.BlockSpec(block, lambda g0, g1, kk: index(*ij(g0, g1), kk))

    if ta:
        a_spec = spec((tk, tm), lambda i, j, kk: (kk + a_off[0], i + a_off[1]))
    else:
        a_spec = spec((tm, tk), lambda i, j, kk: (i + a_off[0], kk + a_off[1]))
    if tb:
        b_spec = spec((tn, tk), lambda i, j, kk: (j + b_off[0], kk + b_off[1]))
    else:
        b_spec = spec((tk, tn), lambda i, j, kk: (kk + b_off[0], j + b_off[1]))
    in_specs = [a_spec, b_spec]
    for _, bs, im in ins:
        in_specs.append(spec(bs, lambda i, j, kk, im=im: im(i, j)))
    out_specs, out_shape = [], []
    for shape, dtype, bs, im in outs:
        out_specs.append(spec(bs, lambda i, j, kk, im=im: im(i, j)))
        out_shape.append(jax.ShapeDtypeStruct(shape, dtype))
    for shape, dtype in accs:
        out_specs.append(spec(shape, lambda i, j, kk, nd=len(shape): (0,) * nd))
        out_shape.append(jax.ShapeDtypeStruct(shape, dtype))
    n_in, n_out, n_acc = len(ins), len(outs), len(accs)
    ca, cb = (0 if ta else 1), (1 if tb else 0)

    def body(*refs):
        a_ref, b_ref = refs[:2]
        in_refs = refs[2:2 + n_in]
        out_refs = refs[2 + n_in:2 + n_in + n_out]
        acc_refs = refs[2 + n_in + n_out:2 + n_in + n_out + n_acc]
        scratch = refs[2 + n_in + n_out + n_acc:]
        (i, j), kk = ij(pl.program_id(0), pl.program_id(1)), pl.program_id(2)
        part = _dot(a_ref[...].astype(BF16), b_ref[...].astype(BF16), ca, cb)

        def finish(acc):
            o_tiles, a_tiles = epi(acc, [r[...] for r in in_refs], i, j)
            for r, t in zip(out_refs, o_tiles, strict=True):
                r[...] = t.astype(r.dtype)
            if n_acc:
                @pl.when(i == 0)
                def _():
                    for r, t in zip(acc_refs, a_tiles, strict=True):
                        r[...] = t

                @pl.when(i > 0)
                def _():
                    for r, t in zip(acc_refs, a_tiles, strict=True):
                        r[...] += t

        if nk == 1:
            finish(part)
        else:
            acc_ref = scratch[0]

            @pl.when(kk == 0)
            def _():
                acc_ref[...] = part

            @pl.when(kk > 0)
            def _():
                acc_ref[...] += part

            @pl.when(kk == nk - 1)
            def _():
                finish(acc_ref[...])

    res = pl.pallas_call(
        body, name=name, grid=(*ij(ni, nj), nk), in_specs=in_specs, out_specs=out_specs, out_shape=out_shape,
        scratch_shapes=[pltpu.VMEM((tm, tn), F32)] if nk > 1 else [],
        compiler_params=_cparams(),
    )(a, b, *[x for x, _, _ in ins])
    return res


def _tile(tm, tn, dj=0):
    return (tm, tn), (lambda i, j: (i, j + dj))


def _rowvec(tn, dj=0):
    return (1, tn), (lambda i, j: (0, j + dj))


def _plain(acc, tiles, i, j):
    return [acc], []


def _ew(name, fn, ins, outs, rows, tr):
    assert rows % tr == 0, (name, rows, tr)
    in_specs = []
    for x in ins:
        if x.shape[0] == rows:
            in_specs.append(pl.BlockSpec((tr, x.shape[1]), lambda i: (i, 0)))
        else:
            in_specs.append(pl.BlockSpec(x.shape, lambda i: (0, 0)))
    n_in = len(ins)

    def body(*refs):
        res = fn(*[r[...] for r in refs[:n_in]])
        for r, t in zip(refs[n_in:], res, strict=True):
            r[...] = t.astype(r.dtype)

    return pl.pallas_call(
        body, name=name, grid=(rows // tr,), in_specs=in_specs,
        out_specs=[pl.BlockSpec((tr, w), lambda i: (i, 0)) for w, _ in outs],
        out_shape=[jax.ShapeDtypeStruct((rows, w), dt) for w, dt in outs],
        compiler_params=_cparams(),
    )(*ins)


def _cast_bf16(name, x):
    rows = x.shape[0]
    tr = next(t for t in (512, 256, 64) if rows % t == 0)
    return _ew(name, lambda v: (v,), [x], [(x.shape[1], BF16)], rows, tr)[0]


def _rope_tables(s):
    half = RET_QK // 2
    inv = 1.0 / (ROPE_BASE ** (jnp.arange(half, dtype=F32) / half))
    inv2 = jnp.concatenate([inv, inv]).reshape(1, RET_QK)
    sign = jnp.concatenate([-jnp.ones((half,), F32), jnp.ones((half,), F32)]).reshape(1, RET_QK)
    tr = ROW_TILE

    def body(inv_ref, sign_ref, cos_ref, sin_ref):
        i = pl.program_id(0)
        pos = (lax.broadcasted_iota(jnp.int32, (tr, RET_QK), 0) + i * tr).astype(F32)
        ang = pos * inv_ref[...]
        cos_ref[...] = jnp.cos(ang)
        sin_ref[...] = jnp.sin(ang) * sign_ref[...]

    vec = pl.BlockSpec((1, RET_QK), lambda i: (0, 0))
    blk = pl.BlockSpec((tr, RET_QK), lambda i: (i, 0))
    return pl.pallas_call(
        body, name="rope_tables", grid=(s // tr,), in_specs=[vec, vec], out_specs=[blk, blk],
        out_shape=[jax.ShapeDtypeStruct((s, RET_QK), F32)] * 2, compiler_params=_cparams(),
    )(inv2, sign)


def _swap_halves(x):
    return pltpu.roll(x, RET_QK // 2, 1)


def _norm(u):
    mu = jnp.mean(u, axis=-1, keepdims=True)
    d = u - mu
    var = jnp.mean(d * d, axis=-1, keepdims=True)
    rstd = lax.rsqrt(var + LN_EPS)
    return d * rstd, rstd


def _norm_bwd(dxh, xhat, rstd):
    m1 = jnp.mean(dxh, axis=-1, keepdims=True)
    m2 = jnp.mean(dxh * xhat, axis=-1, keepdims=True)
    return rstd * (dxh - m1 - xhat * m2)


def _colsum(t):
    return jnp.sum(t, axis=0, keepdims=True)


def _split_mm(t, tri):
    hi = t.astype(BF16)
    lo = (t - hi.astype(F32)).astype(BF16)
    return _dot(hi, tri, 1, 0) + _dot(lo, tri, 1, 0)


def _sb_masks():
    t = SB_BLOCK
    lane = lax.broadcasted_iota(jnp.int32, (1, LANES), 1)
    first = lane < SB_DIM
    m0 = jnp.where(first, 1.0, 0.0).astype(BF16)
    m1 = jnp.where(first, 0.0, 1.0).astype(BF16)
    row = lax.broadcasted_iota(jnp.int32, (t, t), 0)
    col = lax.broadcasted_iota(jnp.int32, (t, t), 1)
    return first, (m0, m1), row, col


def _sb_logits(qh, k, causal):
    z = _dot(qh, k, 1, 1)
    lp = jnp.log(1.0 + jnp.exp(-jnp.abs(z)))
    a = jnp.minimum(z, 0.0) - lp
    r = jnp.minimum(-z, 0.0) - lp
    if causal is not None:
        r = jnp.where(causal, r, 0.0)
    return a, r


def _sb_walk(i, block, l_ref, causal):
    block(i, causal)

    def alive():
        top = jnp.max(jnp.maximum(l_ref[0], l_ref[1]))
        return jnp.where(top > SB_DEAD, 1, 0)

    def cond(c):
        return jnp.logical_and(c[0] < i, c[1] > 0)

    def step(c):
        block(i - 1 - c[0], None)
        return c[0] + 1, alive()

    lax.while_loop(cond, step, (jnp.int32(0), alive()))


def _sb_fwd(qkv, s):
    t = SB_BLOCK
    nq = s // t

    def body(q_ref, k_ref, v_ref, o_ref, of_ref, l_ref, acc_ref):
        i = pl.program_id(1)
        first, hmask, row, col = _sb_masks()
        after = jnp.where(row > col, 1.0, 0.0).astype(BF16)
        causal = col < row
        q = q_ref[...]
        qh = (q * hmask[0], q * hmask[1])
        l_ref[...] = jnp.zeros_like(l_ref)
        acc_ref[...] = jnp.zeros_like(acc_ref)

        def block(kb, mask):
            start = pl.multiple_of(kb * t, t)
            k = k_ref[pl.ds(start, t), :]
            v = v_ref[pl.ds(start, t), :]
            pv = []
            for h in range(2):
                a, r = _sb_logits(qh[h], k, mask)
                w = jnp.exp(a + _split_mm(r, after) + l_ref[h])
                if mask is not None:
                    w = jnp.where(mask, w, 0.0)
                pv.append(_dot(w.astype(BF16), v, 1, 0))
                l_ref[h] = l_ref[h] + jnp.sum(r, axis=1, keepdims=True)
            acc_ref[...] += jnp.where(first, pv[0], pv[1])

        _sb_walk(i, block, l_ref, causal)
        o_ref[...] = acc_ref[...].astype(o_ref.dtype)
        of_ref[...] = acc_ref[...]

    nk_off = SB_WIDTH // LANES
    blk = pl.BlockSpec((t, LANES), lambda p, i: (i, p))
    return pl.pallas_call(
        body, name="sb_fwd", grid=(SB_HEADS // 2, nq),
        in_specs=[blk,
                  pl.BlockSpec((s, LANES), lambda p, i: (0, nk_off + p)),
                  pl.BlockSpec((s, LANES), lambda p, i: (0, 2 * nk_off + p))],
        out_specs=[blk, blk],
        out_shape=[jax.ShapeDtypeStruct((s, SB_WIDTH), BF16), jax.ShapeDtypeStruct((s, SB_WIDTH), F32)],
        scratch_shapes=[pltpu.VMEM((2, t, 1), F32), pltpu.VMEM((t, LANES), F32)],
        compiler_params=_cparams(),
    )(qkv, qkv, qkv)


def _sb_bwd(qkv, o, do, s):
    t = SB_BLOCK
    nq = s // t

    def body(q_ref, k_ref, v_ref, o_ref, do_ref, dq_ref, dk_ref, dv_ref, l_ref, e_ref, dq_acc, dk_acc, dv_acc):
        i = pl.program_id(1)
        first, hmask, row, col = _sb_masks()
        after = jnp.where(row > col, 1.0, 0.0).astype(BF16)
        from_here = jnp.where(row >= col, 1.0, 0.0).astype(BF16)
        causal = col < row

        @pl.when(i == 0)
        def _():
            dk_acc[...] = jnp.zeros_like(dk_acc)
            dv_acc[...] = jnp.zeros_like(dv_acc)

        q = q_ref[...]
        do_ = do_ref[...]
        qh = (q * hmask[0], q * hmask[1])
        doh = (do_ * hmask[0], do_ * hmask[1])
        prod = do_.astype(F32) * o_ref[...]
        total = (jnp.sum(jnp.where(first, prod, 0.0), axis=1, keepdims=True),
                 jnp.sum(jnp.where(first, 0.0, prod), axis=1, keepdims=True))
        l_ref[...] = jnp.zeros_like(l_ref)
        e_ref[...] = jnp.zeros_like(e_ref)
        dq_acc[...] = jnp.zeros_like(dq_acc)

        def block(kb, mask):
            start = pl.multiple_of(kb * t, t)
            k = k_ref[pl.ds(start, t), :]
            v = v_ref[pl.ds(start, t), :]
            dqs, dks, dvs = [], [], []
            for h in range(2):
                a, r = _sb_logits(qh[h], k, mask)
                w = jnp.exp(a + _split_mm(r, after) + l_ref[h])
                if mask is not None:
                    w = jnp.where(mask, w, 0.0)
                wb = w.astype(BF16)
                e = _dot(doh[h], v, 1, 1) * wb.astype(F32)
                before = total[h] - (_split_mm(e, from_here) + e_ref[h])
                dz = e - jnp.exp(a) * (e + before)
                if mask is not None:
                    dz = jnp.where(mask, dz, 0.0)
                dzb = dz.astype(BF16)
                dqs.append(_dot(dzb, k, 1, 0))
                dks.append(_dot(dzb, q, 0, 0))
                dvs.append(_dot(wb, do_, 0, 0))
                l_ref[h] = l_ref[h] + jnp.sum(r, axis=1, keepdims=True)
                e_ref[h] = e_ref[h] + jnp.sum(e, axis=1, keepdims=True)
            dq_acc[...] += jnp.where(first, dqs[0], dqs[1])
            dk_acc[pl.ds(start, t), :] += jnp.where(first, dks[0], dks[1])
            dv_acc[pl.ds(start, t), :] += jnp.where(first, dvs[0], dvs[1])

        _sb_walk(i, block, l_ref, causal)
        dq_ref[...] = (dq_acc[...] * SB_SCALE).astype(dq_ref.dtype)

        @pl.when(i == nq - 1)
        def _():
            dk_ref[...] = dk_acc[...].astype(dk_ref.dtype)
            dv_ref[...] = dv_acc[...].astype(dv_ref.dtype)

    nk_off = SB_WIDTH // LANES
    blk = pl.BlockSpec((t, LANES), lambda p, i: (i, p))
    col_blk = pl.BlockSpec((s, LANES), lambda p, i: (0, p))
    sds = jax.ShapeDtypeStruct((s, SB_WIDTH), BF16)
    return pl.pallas_call(
        body, name="sb_bwd", grid=(SB_HEADS // 2, nq),
        in_specs=[blk,
                  pl.BlockSpec((s, LANES), lambda p, i: (0, nk_off + p)),
                  pl.BlockSpec((s, LANES), lambda p, i: (0, 2 * nk_off + p)),
                  blk, blk],
        out_specs=[blk, col_blk, col_blk],
        out_shape=[sds, sds, sds],
        scratch_shapes=[pltpu.VMEM((2, t, 1), F32), pltpu.VMEM((2, t, 1), F32), pltpu.VMEM((t, LANES), F32),
                        pltpu.VMEM((s, LANES), F32), pltpu.VMEM((s, LANES), F32)],
        compiler_params=_cparams(),
    )(qkv, qkv, qkv, o, do)


def _ret_log_gamma():
    lg = np.log1p(-np.exp2(-5.0 - np.arange(RET_HEADS, dtype=np.float32))).astype(np.float32)
    return jnp.asarray(np.broadcast_to(lg[:, None, None], (RET_HEADS, 8, LANES)).copy())


def _ret_decays(lg_ref):
    c = RET_BLOCK
    lg = lg_ref[0, 0:1, 0:1]
    row = lax.broadcasted_iota(jnp.int32, (c, c), 0)
    col = lax.broadcasted_iota(jnp.int32, (c, c), 1)
    rel = (row - col).astype(F32)
    within = jnp.where(row >= col, jnp.exp(lg * jnp.maximum(rel, 0.0)), 0.0)
    idx = lax.broadcasted_iota(jnp.int32, (c, 1), 0).astype(F32)
    q_dec = jnp.exp(lg * (idx + 1.0))
    k_dec = jnp.exp(lg * (c - 1.0 - idx))
    chunk_dec = jnp.exp(lg * float(c))
    return within, q_dec, k_dec, chunk_dec


def _ret_specs(s, reverse=False):
    c = RET_BLOCK
    nc = s // c
    pos = (lambda n: nc - 1 - n) if reverse else (lambda n: n)
    qk_heads = RET_QK_WIDTH // RET_QK
    q_spec = pl.BlockSpec((c, RET_QK), lambda h, n: (pos(n), h))
    k_spec = pl.BlockSpec((c, RET_QK), lambda h, n: (pos(n), qk_heads + h))
    v_spec = pl.BlockSpec((c, RET_V), lambda h, n: (pos(n), h))
    lg_spec = pl.BlockSpec((1, 8, LANES), lambda h, n: (h, 0, 0))
    rope_spec = pl.BlockSpec((c, RET_QK), lambda h, n: (pos(n), 0))
    return nc, q_spec, k_spec, v_spec, lg_spec, rope_spec


def _ret_fwd(rqk, rvg, s):
    nc, q_spec, k_spec, v_spec, lg_spec, _ = _ret_specs(s)
    g_spec = pl.BlockSpec((RET_BLOCK, RET_V), lambda h, n: (n, RET_HEADS + h))

    def body(q_ref, k_ref, v_ref, g_ref, lg_ref, r_ref, y_ref, state):
        n = pl.program_id(1)

        @pl.when(n == 0)
        def _():
            state[...] = jnp.zeros_like(state)

        within, q_dec, k_dec, chunk_dec = _ret_decays(lg_ref)
        q, k, v = q_ref[...], k_ref[...], v_ref[...]
        scores = _dot(q.astype(BF16), k.astype(BF16), 1, 1) * within
        out = _dot(scores.astype(BF16), v, 1, 0)
        out += _dot((q * q_dec).astype(BF16), state[...].astype(BF16), 1, 0)
        r_ref[...] = out
        xhat, _ = _norm(out)
        g = g_ref[...].astype(F32)
        y_ref[...] = (g * _sigmoid(g) * xhat).astype(y_ref.dtype)
        state[...] = state[...] * chunk_dec + _dot((k * k_dec).astype(BF16), v, 0, 0)

    return pl.pallas_call(
        body, name="ret_fwd", grid=(RET_HEADS, nc),
        in_specs=[q_spec, k_spec, v_spec, g_spec, lg_spec],
        out_specs=[v_spec, v_spec],
        out_shape=[jax.ShapeDtypeStruct((s, RET_V_WIDTH), F32), jax.ShapeDtypeStruct((s, RET_V_WIDTH), BF16)],
        scratch_shapes=[pltpu.VMEM((RET_QK, RET_V), F32)],
        compiler_params=_cparams(),
    )(rqk, rqk, rvg, rvg, _ret_log_gamma())


def _rope_bwd(d, cos, sin):
    return d * cos + _swap_halves(d * sin)


def _ret_bwd_q(rqk, rv, d_out, cos2, sin2, s):
    nc, q_spec, k_spec, v_spec, lg_spec, rope_spec = _ret_specs(s)

    def body(q_ref, k_ref, v_ref, d_ref, lg_ref, cos_ref, sin_ref, dq_ref, state):
        n = pl.program_id(1)

        @pl.when(n == 0)
        def _():
            state[...] = jnp.zeros_like(state)

        within, q_dec, k_dec, chunk_dec = _ret_decays(lg_ref)
        k, v, d = k_ref[...], v_ref[...], d_ref[...]
        st = state[...].astype(BF16)
        d_scores = _dot(d, v, 1, 1) * within
        dq = _dot(d_scores.astype(BF16), k.astype(BF16), 1, 0) + q_dec * _dot(d, st, 1, 1)
        dq_ref[...] = (_rope_bwd(dq, cos_ref[...], sin_ref[...]) * RET_SCALE).astype(dq_ref.dtype)
        state[...] = state[...] * chunk_dec + _dot((k * k_dec).astype(BF16), v, 0, 0)

    return pl.pallas_call(
        body, name="ret_bwd_q", grid=(RET_HEADS, nc),
        in_specs=[q_spec, k_spec, v_spec, v_spec, lg_spec, rope_spec, rope_spec],
        out_specs=q_spec,
        out_shape=jax.ShapeDtypeStruct((s, RET_QK_WIDTH), BF16),
        scratch_shapes=[pltpu.VMEM((RET_QK, RET_V), F32)],
        compiler_params=_cparams(),
    )(rqk, rqk, rv, d_out, _ret_log_gamma(), cos2, sin2)


def _ret_bwd_kv(rqk, rv, d_out, cos2, sin2, s):
    nc, q_spec, k_spec, v_spec, lg_spec, rope_spec = _ret_specs(s, reverse=True)

    def body(q_ref, k_ref, v_ref, d_ref, lg_ref, cos_ref, sin_ref, dk_ref, dv_ref, state):
        n = pl.program_id(1)

        @pl.when(n == 0)
        def _():
            state[...] = jnp.zeros_like(state)

        within, q_dec, k_dec, chunk_dec = _ret_decays(lg_ref)
        q, k, v, d = q_ref[...], k_ref[...], v_ref[...], d_ref[...]
        qb, kb = q.astype(BF16), k.astype(BF16)
        st = state[...].astype(BF16)
        scores = _dot(qb, kb, 1, 1) * within
        d_scores = _dot(d, v, 1, 1) * within
        dk = _dot(d_scores.astype(BF16), qb, 0, 0) + k_dec * _dot(v, st, 1, 1)
        dv = _dot(scores.astype(BF16), d, 0, 0) + k_dec * _dot(kb, st, 1, 0)
        dk_ref[...] = _rope_bwd(dk, cos_ref[...], sin_ref[...]).astype(dk_ref.dtype)
        dv_ref[...] = dv.astype(dv_ref.dtype)
        state[...] = state[...] * chunk_dec + _dot((q * q_dec).astype(BF16), d, 0, 0)

    c = RET_BLOCK
    dk_spec = pl.BlockSpec((c, RET_QK), lambda h, n: (nc - 1 - n, h))
    return pl.pallas_call(
        body, name="ret_bwd_kv", grid=(RET_HEADS, nc),
        in_specs=[q_spec, k_spec, v_spec, v_spec, lg_spec, rope_spec, rope_spec],
        out_specs=[dk_spec, v_spec],
        out_shape=[jax.ShapeDtypeStruct((s, RET_QK_WIDTH), BF16), jax.ShapeDtypeStruct((s, RET_V_WIDTH), BF16)],
        scratch_shapes=[pltpu.VMEM((RET_QK, RET_V), F32)],
        compiler_params=_cparams(),
    )(rqk, rqk, rv, d_out, _ret_log_gamma(), cos2, sin2)


def _xattn_probs(q, k):
    sc = _dot(q, k, 1, 1)
    sc = sc - jnp.max(sc, axis=-1, keepdims=True)
    p = jnp.exp(sc)
    return p / jnp.sum(p, axis=-1, keepdims=True)


def _xattn_fwd(qm, kv, s):
    tq = XATTN_ROWS

    def body(q_ref, kv_ref, o_ref):
        for h in range(MEM_HEADS):
            sl = slice(h * MEM_DIM, (h + 1) * MEM_DIM)
            sv = slice(D_MODEL + h * MEM_DIM, D_MODEL + (h + 1) * MEM_DIM)
            p = _xattn_probs(q_ref[:, sl], kv_ref[:, sl])
            o_ref[:, sl] = _dot(p.astype(BF16), kv_ref[:, sv], 1, 0).astype(o_ref.dtype)

    return pl.pallas_call(
        body, name="xattn_fwd", grid=(s // tq,),
        in_specs=[pl.BlockSpec((tq, D_MODEL), lambda i: (i, 0)),
                  pl.BlockSpec((MEM_LEN, 2 * D_MODEL), lambda i: (0, 0))],
        out_specs=pl.BlockSpec((tq, D_MODEL), lambda i: (i, 0)),
        out_shape=jax.ShapeDtypeStruct((s, D_MODEL), BF16),
        compiler_params=_cparams(),
    )(qm, kv)


def _xattn_bwd(qm, kv, do, s):
    tq = XATTN_ROWS

    def body(q_ref, kv_ref, do_ref, dq_ref, dkv_ref):
        i = pl.program_id(0)

        @pl.when(i == 0)
        def _():
            dkv_ref[...] = jnp.zeros_like(dkv_ref)

        for h in range(MEM_HEADS):
            sl = slice(h * MEM_DIM, (h + 1) * MEM_DIM)
            sv = slice(D_MODEL + h * MEM_DIM, D_MODEL + (h + 1) * MEM_DIM)
            q, k, v, d = q_ref[:, sl], kv_ref[:, sl], kv_ref[:, sv], do_ref[:, sl]
            p = _xattn_probs(q, k)
            dp = _dot(d, v, 1, 1)
            ds = (p * (dp - jnp.sum(p * dp, axis=-1, keepdims=True))).astype(BF16)
            dq_ref[:, sl] = (_dot(ds, k, 1, 0) * MEM_SCALE).astype(dq_ref.dtype)
            dkv_ref[:, sl] += _dot(ds, q, 0, 0)
            dkv_ref[:, sv] += _dot(p.astype(BF16), d, 0, 0)

    row_blk = pl.BlockSpec((tq, D_MODEL), lambda i: (i, 0))
    kv_blk = pl.BlockSpec((MEM_LEN, 2 * D_MODEL), lambda i: (0, 0))
    return pl.pallas_call(
        body, name="xattn_bwd", grid=(s // tq,),
        in_specs=[row_blk, kv_blk, row_blk],
        out_specs=[row_blk, kv_blk],
        out_shape=[jax.ShapeDtypeStruct((s, D_MODEL), BF16), jax.ShapeDtypeStruct((MEM_LEN, 2 * D_MODEL), F32)],
        compiler_params=_cparams(),
    )(qm, kv, do)


def _place():
    x, y, c = lax.axis_index("x"), lax.axis_index("y"), lax.axis_index("c")
    others = [(1 - x, y), (x, 1 - y), (1 - x, 1 - y)]
    return x, y, c, others


def _slab(ref, axis, chip, size):
    start = pl.multiple_of(chip * size, LANES if axis == 1 else 16)
    if axis == 0:
        return ref.at[pl.ds(start, size), :]
    return ref.at[:, pl.ds(start, size)]


def _weight_gather(shards):
    nw = len(BIG)
    any_spec = pl.BlockSpec(memory_space=pl.ANY)

    def body(*refs):
        shard, full = refs[:nw], refs[nw:2 * nw]
        send_sems, recv_sems, local_sems = refs[2 * nw:]
        x, y, c, others = _place()
        mine = 2 * x + y
        sent, local = [], []
        for w, (_, shape, axis) in enumerate(BIG):
            size = shape[axis] // N_CHIPS
            cp = pltpu.make_async_copy(shard[w], _slab(full[w], axis, mine, size), local_sems.at[w])
            cp.start()
            local.append(cp)
            for t, (qx, qy) in enumerate(others):
                cp = pltpu.make_async_remote_copy(
                    src_ref=shard[w], dst_ref=_slab(full[w], axis, mine, size),
                    send_sem=send_sems.at[3 * w + t], recv_sem=recv_sems.at[3 * w + t],
                    device_id=(qx, qy, c), device_id_type=MESH)
                cp.start()
                sent.append(cp)
        for w, (_, shape, axis) in enumerate(BIG):
            size = shape[axis] // N_CHIPS
            for t, (qx, qy) in enumerate(others):
                pltpu.make_async_remote_copy(
                    src_ref=shard[w], dst_ref=_slab(full[w], axis, 2 * qx + qy, size),
                    send_sem=send_sems.at[3 * w + t], recv_sem=recv_sems.at[3 * w + t],
                    device_id=(qx, qy, c), device_id_type=MESH).wait_recv()
        for cp in sent:
            cp.wait_send()
        for cp in local:
            cp.wait()

    return pl.pallas_call(
        body, name="weight_gather",
        in_specs=[any_spec] * nw, out_specs=[any_spec] * nw,
        out_shape=[jax.ShapeDtypeStruct(shape, BF16) for _, shape, _ in BIG],
        scratch_shapes=[pltpu.SemaphoreType.DMA((3 * nw,)), pltpu.SemaphoreType.DMA((3 * nw,)),
                        pltpu.SemaphoreType.DMA((nw,))],
    )(*shards)


def _shard_shape(shape, axis):
    return tuple(d // N_CHIPS if a == axis else d for a, d in enumerate(shape))


def _grad_exchange(grads):
    nw = len(BIG)
    any_spec = pl.BlockSpec(memory_space=pl.ANY)

    def body(*refs):
        grad, stack = refs[:nw], refs[nw:2 * nw]
        send_sems, recv_sems, local_sems = refs[2 * nw:]
        x, y, c, others = _place()
        mine = 2 * x + y
        me, sibling = (x, y, c), (x, y, 1 - c)

        def dev(px, py, pc):
            return 4 * px + 2 * py + pc

        def copy(w, n, src, slot, to):
            return pltpu.make_async_remote_copy(
                src_ref=src, dst_ref=stack[w].at[slot], send_sem=send_sems.at[7 * w + n],
                recv_sem=recv_sems.at[7 * w + n], device_id=to, device_id_type=MESH)

        sent, local = [], []
        for w, (_, shape, axis) in enumerate(BIG):
            size = shape[axis] // N_CHIPS
            own = _slab(grad[w], axis, mine, size)
            cp = pltpu.make_async_copy(own, stack[w].at[dev(*me)], local_sems.at[w])
            cp.start()
            local.append(cp)
            first = [copy(w, 0, own, dev(*me), sibling)]
            first += [copy(w, 1 + t, _slab(grad[w], axis, 2 * qx + qy, size), dev(*me), (qx, qy, c))
                      for t, (qx, qy) in enumerate(others)]
            for cp in first:
                cp.start()
            sent += first
        for w in range(nw):
            for t, (qx, qy) in enumerate(others):
                got = stack[w].at[dev(qx, qy, c)]
                copy(w, 1 + t, got, dev(qx, qy, c), me).wait_recv()
                cp = copy(w, 4 + t, got, dev(qx, qy, c), sibling)
                cp.start()
                sent.append(cp)
        for w in range(nw):
            copy(w, 0, stack[w].at[dev(*sibling)], dev(*sibling), me).wait_recv()
            for t, (qx, qy) in enumerate(others):
                copy(w, 4 + t, stack[w].at[dev(qx, qy, 1 - c)], dev(qx, qy, 1 - c), me).wait_recv()
        for cp in sent:
            cp.wait_send()
        for cp in local:
            cp.wait()

    return pl.pallas_call(
        body, name="grad_exchange",
        in_specs=[any_spec] * nw, out_specs=[any_spec] * nw,
        out_shape=[jax.ShapeDtypeStruct((N_DEV,) + _shard_shape(shape, axis), BF16) for _, shape, axis in BIG],
        scratch_shapes=[pltpu.SemaphoreType.DMA((7 * nw,)), pltpu.SemaphoreType.DMA((7 * nw,)),
                        pltpu.SemaphoreType.DMA((nw,))],
    )(*grads)


def _adamw(w, g, m, v):
    m = ADAM_B1 * m + (1.0 - ADAM_B1) * g
    v = ADAM_B2 * v + (1.0 - ADAM_B2) * (g * g)
    m_hat = m / (1.0 - ADAM_B1 ** ADAM_STEP)
    v_hat = v / (1.0 - ADAM_B2 ** ADAM_STEP)
    delta = -ADAM_LR * (m_hat / (jnp.sqrt(v_hat) + ADAM_EPS) + ADAM_WD * w)
    return delta, m, v


def _reduce_adamw(name, stack, w, m, v):
    rows, cols = w.shape
    tr = next(t for t in (256, 128, 64) if rows % t == 0)

    def body(s_ref, w_ref, m_ref, v_ref, g_ref, d_ref, nm_ref, nv_ref):
        g = s_ref[0].astype(F32)
        for d in range(1, N_DEV):
            g = g + s_ref[d].astype(F32)
        g_ref[...] = g
        d_ref[...], nm_ref[...], nv_ref[...] = _adamw(w_ref[...], g, m_ref[...], v_ref[...])

    blk = pl.BlockSpec((tr, cols), lambda i: (i, 0))
    return pl.pallas_call(
        body, name=name, grid=(rows // tr,),
        in_specs=[pl.BlockSpec((N_DEV, tr, cols), lambda i: (0, i, 0)), blk, blk, blk],
        out_specs=[blk] * 4, out_shape=[jax.ShapeDtypeStruct((rows, cols), F32)] * 4,
        compiler_params=_cparams(),
    )(stack, w, m, v)


def _small_step(pack, w, m, v):
    def body(p_ref, w_ref, m_ref, v_ref, g_ref, d_ref, nm_ref, nv_ref, loss_ref, all_ref, send_sems, recv_sems):
        x, y, c, _ = _place()
        me = 4 * x + 2 * y + c
        all_ref[me] = p_ref[...]
        sent = []
        for n in range(1, N_DEV):
            peer = me ^ n
            cp = pltpu.make_async_remote_copy(
                src_ref=p_ref, dst_ref=all_ref.at[me], send_sem=send_sems.at[n - 1], recv_sem=recv_sems.at[n - 1],
                device_id=(peer // 4, (peer // 2) % 2, peer % 2), device_id_type=MESH)
            cp.start()
            sent.append(cp)
        for n in range(1, N_DEV):
            peer = me ^ n
            pltpu.make_async_remote_copy(
                src_ref=p_ref, dst_ref=all_ref.at[peer], send_sem=send_sems.at[n - 1], recv_sem=recv_sems.at[n - 1],
                device_id=(peer // 4, (peer // 2) % 2, peer % 2), device_id_type=MESH).wait_recv()
        for cp in sent:
            cp.wait_send()
        tot = all_ref[0]
        for d in range(1, N_DEV):
            tot = tot + all_ref[d]
        g = tot[:SMALL_ROWS]
        g_ref[...] = g
        d_ref[...], nm_ref[...], nv_ref[...] = _adamw(w_ref[...], g, m_ref[...], v_ref[...])
        loss_ref[...] = jnp.sum(jnp.sum(tot[SMALL_ROWS:], axis=1, keepdims=True), axis=0, keepdims=True)

    vm = pl.BlockSpec(memory_space=pltpu.VMEM)
    small = jax.ShapeDtypeStruct((SMALL_ROWS, LANES), F32)
    return pl.pallas_call(
        body, name="small_step",
        in_specs=[vm] * 4, out_specs=[vm] * 5,
        out_shape=[small] * 4 + [jax.ShapeDtypeStruct((1, 1), F32)],
        scratch_shapes=[pltpu.VMEM((N_DEV, PACK_ROWS, LANES), F32),
                        pltpu.SemaphoreType.DMA((N_DEV - 1,)), pltpu.SemaphoreType.DMA((N_DEV - 1,))],
    )(pack, w, m, v)


def _layer_step(x, mem, tgt, wt, vec):
    s = x.shape[0]
    d = D_MODEL
    tm = min(ROW_TILE, s)
    tl = min(WIDE_TILE, s)
    w_in = wt["w_in"]
    cos2, sin2 = _rope_tables(s)
    xb = _cast_bf16("cast_x", x)
    bf = lambda w: ((s, w), BF16)
    f32 = lambda w: ((s, w), F32)

    (sb_qkv,) = _mm(
        "in_sb", xb, w_in, s, 3 * SB_WIDTH, d, tm=tl, tn=512, tk=d,
        epi=lambda acc, t, i, j: ([acc * jnp.where(j == 0, SB_SCALE, 1.0)], []),
        outs=[(*bf(3 * SB_WIDTH), *_tile(tl, 512))])

    def rope_epi(acc, t, i, j):
        cos, sin = t
        scale = jnp.where(j == 0, RET_SCALE, 1.0)
        parts = []
        for g in range(512 // RET_QK):
            xg = acc[:, g * RET_QK:(g + 1) * RET_QK]
            parts.append((xg * cos + _swap_halves(xg) * sin) * scale)
        return [jnp.concatenate(parts, axis=1)], []

    rope_in = ((tl, RET_QK), lambda i, j: (i, 0))
    (rqk,) = _mm("in_rqk", xb, w_in, s, 2 * RET_QK_WIDTH, d, tm=tl, tn=512, tk=d, b_off=(0, OFF_RET_Q // 512),
                 epi=rope_epi, ins=[(cos2, *rope_in), (sin2, *rope_in)],
                 outs=[(*f32(2 * RET_QK_WIDTH), *_tile(tl, 512))])
    (rvg,) = _mm("in_rvg", xb, w_in, s, 2 * RET_V_WIDTH, d, tm=tl, tn=512, tk=d, b_off=(0, OFF_RET_V // 512),
                 epi=_plain, outs=[(*bf(2 * RET_V_WIDTH), *_tile(tl, 512))])
    (gates,) = _mm("in_gate", xb, w_in, s, 2 * d, d, tm=tl, tn=512, tk=d, b_off=(0, OFF_GATE // 512),
                   epi=lambda acc, t, i, j: ([_sigmoid(acc + t[0])], []),
                   ins=[(vec["b_gate"], *_rowvec(512))], outs=[(*bf(2 * d), *_tile(tl, 512))])

    sb_out, sb_out_f32 = _sb_fwd(sb_qkv, s)
    ret, gated = _ret_fwd(rqk, rvg, s)
    (y_sb,) = _mm("sb_o", sb_out, wt["w_sb_o"], s, d, SB_WIDTH, tm=tl, tn=d, tk=SB_WIDTH, epi=_plain,
                  outs=[(*bf(d), *_tile(tl, d))])
    y_ret, mixin = _mm(
        "ret_o", gated, wt["w_ret_o"], s, d, RET_V_WIDTH, tm=tl, tn=d, tk=RET_V_WIDTH,
        epi=lambda acc, t, i, j: ([acc, t[0].astype(F32) * t[2].astype(F32) + t[1].astype(F32) * acc], []),
        ins=[(gates, *_tile(tl, d)), (gates, *_tile(tl, d, 1)), (y_sb, *_tile(tl, d))],
        outs=[(*bf(d), *_tile(tl, d)), (*bf(d), *_tile(tl, d))])

    def ln_epi(acc, t, i, j):
        res, g, b = t
        xhat, rstd = _norm(DN_ALPHA * res + acc)
        y = xhat * g + b
        return [y, y, xhat, rstd], []

    full = _tile(tm, d)
    col1 = ((tm, 1), lambda i, j: (i, 0))
    ln_outs = [(*f32(d), *full), (*bf(d), *full), (*f32(d), *full), ((s, 1), F32, *col1)]
    x1, x1b, xhat1, rstd1 = _mm(
        "mix_o", mixin, wt["w_mix_o"], s, d, d, tm=tm, tn=d, tk=d, epi=ln_epi,
        ins=[(x, *full), (vec["ln1_g"], *_rowvec(d)), (vec["ln1_b"], *_rowvec(d))], outs=ln_outs)

    (qm,) = _mm("mem_q", x1b, wt["w_mem_q"], s, d, d, tm=tl, tn=d, tk=d,
                epi=lambda acc, t, i, j: ([acc * MEM_SCALE], []), outs=[(*bf(d), *_tile(tl, d))])
    (kv,) = _mm("mem_kv", mem, wt["w_mem_kv"], MEM_LEN, 2 * d, d, tm=MEM_LEN, tn=d, tk=d, epi=_plain,
                outs=[((MEM_LEN, 2 * d), BF16, *_tile(MEM_LEN, d))])
    att = _xattn_fwd(qm, kv, s)
    x2, x2b, xhat2, rstd2 = _mm(
        "mem_o", att, wt["w_mem_o"], s, d, d, tm=tm, tn=d, tk=d, epi=ln_epi,
        ins=[(x1, *full), (vec["ln2_g"], *_rowvec(d)), (vec["ln2_b"], *_rowvec(d))], outs=ln_outs)

    fh = FFN_HIDDEN
    tf = fh // 2
    (f1,) = _mm("ffn_in1", x2b, wt["w_ffn_in"], s, fh, d, tm=tl, tn=tf, tk=d, epi=_plain, j_outer=True,
                outs=[(*bf(fh), *_tile(tl, tf))])

    def swiglu_epi(acc, t, i, j):
        a = t[0].astype(F32)
        return [acc, a * _sigmoid(a) * acc], []

    f2, act = _mm(
        "ffn_in2", x2b, wt["w_ffn_in"], s, fh, d, tm=tm, tn=tf, tk=d, b_off=(0, 2), epi=swiglu_epi, j_outer=True,
        ins=[(f1, *_tile(tm, tf))], outs=[(*bf(fh), *_tile(tm, tf)), (*bf(fh), *_tile(tm, tf))])

    def head_epi(acc, t, i, j):
        res, g, b, target = t
        xhat, rstd = _norm(DN_ALPHA * res + acc)
        err = xhat * g + b - target
        dy = err * (1.0 / d)
        du = _norm_bwd(dy * g, xhat, rstd)
        return [du, du], [_colsum(dy * xhat), _colsum(dy), _colsum(err * err) * (0.5 / d)]

    vec_acc = ((1, d), F32)
    du3, du3b, dg3, db3, loss_cols = _mm(
        "ffn_out", act, wt["w_ffn_out"], s, d, fh, tm=tm, tn=d, tk=fh, epi=head_epi,
        ins=[(x2, *full), (vec["ln3_g"], *_rowvec(d)), (vec["ln3_b"], *_rowvec(d)), (tgt, *full)],
        outs=[(*f32(d), *full), (*bf(d), *full)], accs=[vec_acc] * 3)

    grads = {}
    ts = min(SEQ_TILE, s)

    def wgrad(name, a, b, m, n, tm_, tn_, tk_=None):
        (g,) = _mm(name, a, b, m, n, a.shape[0], tm=tm_, tn=tn_, tk=tk_ or ts, ta=True, epi=_plain,
                   outs=[((m, n), BF16, *_tile(tm_, tn_))])
        return g

    def ffn_bwd_epi(acc, t, i, j):
        a, b = t[0].astype(F32), t[1].astype(F32)
        sg = _sigmoid(a)
        return [acc * b * (sg * (1.0 + a * (1.0 - sg))), acc * (a * sg)], []

    df1, df2 = _mm(
        "ffn_out_t", du3b, wt["w_ffn_out"], s, fh, d, tm=tm, tn=tf, tk=d, tb=True, epi=ffn_bwd_epi, j_outer=True,
        ins=[(f1, *_tile(tm, tf)), (f2, *_tile(tm, tf))],
        outs=[(*bf(fh), *_tile(tm, tf)), (*bf(fh), *_tile(tm, tf))])
    grads["w_ffn_out"] = wgrad("g_ffn_out", act, du3b, fh, d, tf, d)
    grads["w_ffn_in"] = jnp.concatenate(
        [wgrad("g_ffn_in1", x2b, df1, d, fh, d, tf), wgrad("g_ffn_in2", x2b, df2, d, fh, d, tf)], axis=1)
    (dx2a,) = _mm("ffn_in1_t", df1, wt["w_ffn_in"], s, d, fh, tm=tm, tn=d, tk=fh, tb=True, epi=_plain,
                  outs=[(*f32(d), *full)])

    def ln_bwd(name, a, b, k, tk, b_off, more, scales, xhat, rstd, g):
        def epi(acc, t, i, j):
            *extra, xh, rs, gg = t
            dy = acc
            for e, sc in zip(extra, scales, strict=True):
                dy = dy + e * sc
            du = _norm_bwd(dy * gg, xh, rs)
            return [du, du], [_colsum(dy * xh), _colsum(dy)]

        return _mm(name, a, b, s, d, k, tm=tm, tn=d, tk=tk, tb=True, b_off=b_off, epi=epi,
                   ins=[(e, *full) for e in more] + [(xhat, *full), (rstd, *col1), (g, *_rowvec(d))],
                   outs=[(*f32(d), *full), (*bf(d), *full)], accs=[vec_acc] * 2)

    du2, du2b, dg2, db2 = ln_bwd("ffn_in2_t", df2, wt["w_ffn_in"], fh, fh, (0, 1), [dx2a, du3], [1.0, DN_ALPHA],
                                 xhat2, rstd2, vec["ln2_g"])

    (datt,) = _mm("mem_o_t", du2b, wt["w_mem_o"], s, d, d, tm=tl, tn=d, tk=d, tb=True, epi=_plain,
                  outs=[(*bf(d), *_tile(tl, d))])
    grads["w_mem_o"] = wgrad("g_mem_o", att, du2b, d, d, d, d)
    dqm, dkv = _xattn_bwd(qm, kv, datt, s)
    grads["w_mem_q"] = wgrad("g_mem_q", x1b, dqm, d, d, d, d)
    grads["w_mem_kv"] = wgrad("g_mem_kv", mem, dkv, d, 2 * d, d, d, MEM_LEN)
    du1, du1b, dg1, db1 = ln_bwd("mem_q_t", dqm, wt["w_mem_q"], d, d, (0, 0), [du2], [DN_ALPHA],
                                 xhat1, rstd1, vec["ln1_g"])

    def merge_bwd_epi(acc, t, i, j):
        g0, g1, ysb, yret = (v.astype(F32) for v in t)
        dgate0 = acc * ysb * (g0 * (1.0 - g0))
        dgate1 = acc * yret * (g1 * (1.0 - g1))
        return [dgate0, dgate1, acc * g0, acc * g1], [_colsum(dgate0), _colsum(dgate1)]

    dgate0, dgate1, dy_sb, dy_ret, dbg0, dbg1 = _mm(
        "mix_o_t", du1b, wt["w_mix_o"], s, d, d, tm=tm, tn=d, tk=d, tb=True, epi=merge_bwd_epi,
        ins=[(gates, *full), (gates, *_tile(tm, d, 1)), (y_sb, *full), (y_ret, *full)],
        outs=[(*bf(d), *full)] * 4, accs=[vec_acc] * 2)
    grads["w_mix_o"] = wgrad("g_mix_o", mixin, du1b, d, d, d, d)
    grads["w_sb_o"] = wgrad("g_sb_o", sb_out, dy_sb, SB_WIDTH, d, SB_WIDTH, d)
    grads["w_ret_o"] = wgrad("g_ret_o", gated, dy_ret, RET_V_WIDTH, d, RET_V_WIDTH, d)
    (dsb_out,) = _mm("sb_o_t", dy_sb, wt["w_sb_o"], s, SB_WIDTH, d, tm=tl, tn=SB_WIDTH, tk=d, tb=True, epi=_plain,
                     outs=[(*bf(SB_WIDTH), *_tile(tl, SB_WIDTH))])

    def gate_norm_bwd_epi(acc, t, i, j):
        r, g = t[0], t[1].astype(F32)
        drg, dret = [], []
        for h in range(d // RET_V):
            sl = slice(h * RET_V, (h + 1) * RET_V)
            xhat, rstd = _norm(r[:, sl])
            gg, dd = g[:, sl], acc[:, sl]
            sg = _sigmoid(gg)
            drg.append(dd * xhat * (sg * (1.0 + gg * (1.0 - sg))))
            dret.append(_norm_bwd(dd * (gg * sg), xhat, rstd))
        return [jnp.concatenate(drg, axis=1), jnp.concatenate(dret, axis=1)], []

    drg, dret = _mm(
        "ret_o_t", dy_ret, wt["w_ret_o"], s, RET_V_WIDTH, d, tm=tm, tn=d, tk=d, tb=True, epi=gate_norm_bwd_epi,
        ins=[(ret, *full), (rvg, *_tile(tm, d, 1))],
        outs=[(*bf(RET_V_WIDTH), *full)] * 2)

    drq = _ret_bwd_q(rqk, rvg, dret, cos2, sin2, s)
    drk, drv = _ret_bwd_kv(rqk, rvg, dret, cos2, sin2, s)
    dsq, dsk, dsv = _sb_bwd(sb_qkv, sb_out_f32, dsb_out, s)

    dh = jnp.concatenate([dsq, dsk, dsv, drq, drk, drv, drg, dgate0, dgate1], axis=1)
    grads["w_in"] = wgrad("g_in", xb, dh, d, IN_WIDTH, d, IN_WIDTH // N_CHIPS)
    (grad_x,) = _mm("in_t", dh, w_in, s, d, IN_WIDTH, tm=tl, tn=d, tk=IN_WIDTH // N_CHIPS, tb=True,
                    epi=lambda acc, t, i, j: ([acc + DN_ALPHA * t[0]], []),
                    ins=[(du1, *_tile(tl, d))], outs=[(*f32(d), *_tile(tl, d))])

    small = {"b_gate": jnp.concatenate([dbg0, dbg1], axis=1), "ln1_g": dg1, "ln1_b": db1, "ln2_g": dg2,
             "ln2_b": db2, "ln3_g": dg3, "ln3_b": db3}
    return grad_x, grads, small, loss_cols


def kernel(x, mem, w_in, b_gate, w_sb_o, w_ret_o, w_mix_o, ln1_g, ln1_b, w_mem_q, w_mem_kv, w_mem_o, ln2_g, ln2_b, w_ffn_in, w_ffn_out, ln3_g, ln3_b, loss_target, m_w_in, m_b_gate, m_w_sb_o, m_w_ret_o, m_w_mix_o, m_ln1_g, m_ln1_b, m_w_mem_q, m_w_mem_kv, m_w_mem_o, m_ln2_g, m_ln2_b, m_w_ffn_in, m_w_ffn_out, m_ln3_g, m_ln3_b, v_w_in, v_b_gate, v_w_sb_o, v_w_ret_o, v_w_mix_o, v_ln1_g, v_ln1_b, v_w_mem_q, v_w_mem_kv, v_w_mem_o, v_ln2_g, v_ln2_b, v_w_ffn_in, v_w_ffn_out, v_ln3_g, v_ln3_b):
    given = dict(locals())
    s = x.shape[1]
    x2d = x.reshape(s, D_MODEL)
    tgt = loss_target.reshape(s, D_MODEL)
    mem2d = mem.reshape(MEM_LEN, D_MODEL)
    shard = {name: given[name].reshape(_shard_shape(shape, axis)) for name, shape, axis in BIG}
    vec = {name: given[name] for name in SMALL}

    shards_bf = [_cast_bf16("cast_" + name, shard[name]) for name, _, _ in BIG]
    wt = dict(zip([name for name, _, _ in BIG], _weight_gather(shards_bf), strict=True))

    grad_x, grads, small, loss_cols = _layer_step(x2d, mem2d, tgt, wt, vec)

    stacks = _grad_exchange([grads[name] for name, _, _ in BIG])
    out = {}
    for (name, shape, axis), stack in zip(BIG, stacks, strict=True):
        shp = given[name].shape
        res = _reduce_adamw("adamw_" + name, stack, shard[name], given["m_" + name].reshape(stack.shape[1:]),
                            given["v_" + name].reshape(stack.shape[1:]))
        out[name] = [r.reshape(shp) for r in res]

    pack = jnp.concatenate([small[name] for name in SMALL] + [loss_cols], axis=1).reshape(PACK_ROWS, LANES)
    cat = lambda pre: jnp.concatenate([given[pre + name] for name in SMALL], axis=1).reshape(SMALL_ROWS, LANES)
    *res, loss = _small_step(pack, cat(""), cat("m_"), cat("v_"))
    flat = [r.reshape(1, SMALL_LEN) for r in res]
    off = 0
    for name in SMALL:
        n = given[name].shape[1]
        out[name] = [r[:, off:off + n] for r in flat]
        off += n

    return (loss.reshape(()), grad_x.reshape(x.shape),
            *[out[name][0] for name in WEIGHT_ORDER], *[out[name][1] for name in WEIGHT_ORDER],
            *[out[name][2] for name in WEIGHT_ORDER], *[out[name][3] for name in WEIGHT_ORDER])
```

```python
import functools

import jax
import jax.numpy as jnp
import numpy as np
from jax import lax
from jax.experimental import pallas as pl
from jax.experimental.pallas import tpu as pltpu

F32, BF16 = jnp.float32, jnp.bfloat16
MESH = pl.DeviceIdType.MESH

D_MODEL = 1024
MEM_LEN = 256
SB_HEADS, SB_DIM, SB_WIDTH = 8, 64, 512
RET_HEADS, RET_QK, RET_V = 4, 128, 256
RET_QK_WIDTH, RET_V_WIDTH = 512, 1024
ROPE_BASE = 10000.0
MEM_HEADS, MEM_DIM = 4, 256
FFN_HIDDEN = 2816
IN_WIDTH = 6656
OFF_RET_Q, OFF_RET_V, OFF_RET_G, OFF_GATE = 1536, 2560, 3584, 4608
DN_ALPHA = 2.0 ** 0.25
LN_EPS = 1e-5
SB_SCALE = SB_DIM ** -0.5
SB_DEAD = -110.0
RET_SCALE = RET_QK ** -0.5
MEM_SCALE = MEM_DIM ** -0.5
ADAM_LR, ADAM_B1, ADAM_B2, ADAM_EPS, ADAM_WD, ADAM_STEP = 0.001, 0.9, 0.999, 1e-08, 0.01, 10

N_DEV, N_CHIPS = 8, 4

LANES = 128
VMEM_LIMIT_BYTES = 52 * 2 ** 20
ROW_TILE = 512
WIDE_TILE = 1024
SEQ_TILE = 1024
SB_BLOCK = 256
RET_BLOCK = 256
XATTN_ROWS = 512

BIG = (
    ("w_in", (D_MODEL, IN_WIDTH), 1),
    ("w_sb_o", (SB_WIDTH, D_MODEL), 1),
    ("w_ret_o", (RET_V_WIDTH, D_MODEL), 0),
    ("w_mix_o", (D_MODEL, D_MODEL), 0),
    ("w_mem_q", (D_MODEL, D_MODEL), 0),
    ("w_mem_kv", (D_MODEL, 2 * D_MODEL), 1),
    ("w_mem_o", (D_MODEL, D_MODEL), 0),
    ("w_ffn_in", (D_MODEL, 2 * FFN_HIDDEN), 1),
    ("w_ffn_out", (FFN_HIDDEN, D_MODEL), 0),
)
SMALL = ("b_gate", "ln1_g", "ln1_b", "ln2_g", "ln2_b", "ln3_g", "ln3_b")
SMALL_LEN = 2 * D_MODEL + 6 * D_MODEL
SMALL_ROWS = SMALL_LEN // LANES
PACK_ROWS = SMALL_ROWS + D_MODEL // LANES
WEIGHT_ORDER = ("w_in", "b_gate", "w_sb_o", "w_ret_o", "w_mix_o", "ln1_g", "ln1_b", "w_mem_q", "w_mem_kv",
                "w_mem_o", "ln2_g", "ln2_b", "w_ffn_in", "w_ffn_out", "ln3_g", "ln3_b")


def _cparams():
    return pltpu.CompilerParams(vmem_limit_bytes=VMEM_LIMIT_BYTES)


def _dot(a, b, ca, cb):
    return lax.dot_general(a, b, (((ca,), (cb,)), ((), ())), preferred_element_type=F32)


def _sigmoid(x):
    return 1.0 / (1.0 + jnp.exp(-x))


def _mm(name, a, b, m, n, k, *, tm, tn, tk, epi, outs, ins=(), accs=(), ta=False, tb=False,
        a_off=(0, 0), b_off=(0, 0), j_outer=False, comm=None):
    assert m % tm == 0 and n % tn == 0 and k % tk == 0, (name, m, n, k, tm, tn, tk)
    ni, nj, nk = m // tm, n // tn, k // tk
    assert not accs or nj == 1, name
    ij = (lambda g0, g1: (g1, g0)) if j_outer else (lambda g0, g1: (g0, g1))

    def spec(block, index):
        return pl.BlockSpec(block, lambda g0, g1, kk: index(*ij(g0, g1), kk))

    if ta:
        a_spec = spec((tk, tm), lambda i, j, kk: (kk + a_off[0], i + a_off[1]))
    else:
        a_spec = spec((tm, tk), lambda i, j, kk: (i + a_off[0], kk + a_off[1]))
    if tb:
        b_spec = spec((tn, tk), lambda i, j, kk: (j + b_off[0], kk + b_off[1]))
    else:
        b_spec = spec((tk, tn), lambda i, j, kk: (kk + b_off[0], j + b_off[1]))
    in_specs = [a_spec, b_spec]
    for _, bs, im in ins:
        in_specs.append(spec(bs, lambda i, j, kk, im=im: im(i, j)))
    out_specs, out_shape = [], []
    for shape, dtype, bs, im in outs:
        out_specs.append(spec(bs, lambda i, j, kk, im=im: im(i, j)))
        out_shape.append(jax.ShapeDtypeStruct(shape, dtype))
    for shape, dtype in accs:
        out_specs.append(spec(shape, lambda i, j, kk, nd=len(shape): (0,) * nd))
        out_shape.append(jax.ShapeDtypeStruct(shape, dtype))
    n_in, n_out, n_acc = len(ins), len(outs), len(accs)
    ca, cb = (0 if ta else 1), (1 if tb else 0)
    grid = (*ij(ni, nj), nk)
    comm_ins, comm_outs, comm_scratch = [], [], []
    if comm is not None:
        comm_in_specs, comm_out_specs = comm.specs
        comm_ins, comm_outs, comm_scratch = list(comm.ins), list(comm.out_shape), list(comm.scratch)
        in_specs += comm_in_specs
        out_specs += comm_out_specs
        out_shape += comm_outs
    n_ci, n_co = len(comm_ins), len(comm_outs)

    def body(*refs):
        a_ref, b_ref = refs[:2]
        in_refs = refs[2:2 + n_in]
        ci_refs = refs[2 + n_in:2 + n_in + n_ci]
        rest = refs[2 + n_in + n_ci:]
        out_refs, acc_refs = rest[:n_out], rest[n_out:n_out + n_acc]
        co_refs = rest[n_out + n_acc:n_out + n_acc + n_co]
        scratch = rest[n_out + n_acc + n_co:]
        sem_refs, scratch = scratch[:len(comm_scratch)], scratch[len(comm_scratch):]
        (i, j), kk = ij(pl.program_id(0), pl.program_id(1)), pl.program_id(2)
        if comm is not None:
            first_step, last_step = _grid_ends(grid)
            pl.when(first_step)(lambda: comm.start(ci_refs, co_refs, sem_refs))
        part = _dot(a_ref[...].astype(BF16), b_ref[...].astype(BF16), ca, cb)

        def finish(acc):
            o_tiles, a_tiles = epi(acc, [r[...] for r in in_refs], i, j)
            for r, t in zip(out_refs, o_tiles, strict=True):
                r[...] = t.astype(r.dtype)
            if n_acc:
                @pl.when(i == 0)
                def _():
                    for r, t in zip(acc_refs, a_tiles, strict=True):
                        r[...] = t

                @pl.when(i > 0)
                def _():
                    for r, t in zip(acc_refs, a_tiles, strict=True):
                        r[...] += t

        if nk == 1:
            finish(part)
        else:
            acc_ref = scratch[0]

            @pl.when(kk == 0)
            def _():
                acc_ref[...] = part

            @pl.when(kk > 0)
            def _():
                acc_ref[...] += part

            @pl.when(kk == nk - 1)
            def _():
                finish(acc_ref[...])

        if comm is not None:
            pl.when(last_step)(lambda: comm.finish(ci_refs, co_refs, sem_refs))

    res = pl.pallas_call(
        body, name=name, grid=grid, in_specs=in_specs, out_specs=out_specs, out_shape=out_shape,
        scratch_shapes=comm_scratch + ([pltpu.VMEM((tm, tn), F32)] if nk > 1 else []),
        compiler_params=_cparams(),
    )(a, b, *[x for x, _, _ in ins], *comm_ins)
    return res


def _grid_ends(grid):
    ids = [pl.program_id(ax) for ax in range(len(grid))]
    first = functools.reduce(jnp.logical_and, [p == 0 for p in ids])
    last = functools.reduce(jnp.logical_and, [p == n - 1 for p, n in zip(ids, grid, strict=True)])
    return first, last


def _tile(tm, tn, dj=0):
    return (tm, tn), (lambda i, j: (i, j + dj))


def _rowvec(tn, dj=0):
    return (1, tn), (lambda i, j: (0, j + dj))


def _plain(acc, tiles, i, j):
    return [acc], []


def _ew(name, fn, ins, outs, rows, tr):
    assert rows % tr == 0, (name, rows, tr)
    in_specs = []
    for x in ins:
        if x.shape[0] == rows:
            in_specs.append(pl.BlockSpec((tr, x.shape[1]), lambda i: (i, 0)))
        else:
            in_specs.append(pl.BlockSpec(x.shape, lambda i: (0, 0)))
    n_in = len(ins)

    def body(*refs):
        res = fn(*[r[...] for r in refs[:n_in]])
        for r, t in zip(refs[n_in:], res, strict=True):
            r[...] = t.astype(r.dtype)

    return pl.pallas_call(
        body, name=name, grid=(rows // tr,), in_specs=in_specs,
        out_specs=[pl.BlockSpec((tr, w), lambda i: (i, 0)) for w, _ in outs],
        out_shape=[jax.ShapeDtypeStruct((rows, w), dt) for w, dt in outs],
        compiler_params=_cparams(),
    )(*ins)


def _cast_bf16(name, x):
    rows = x.shape[0]
    tr = next(t for t in (512, 256, 64) if rows % t == 0)
    return _ew(name, lambda v: (v,), [x], [(x.shape[1], BF16)], rows, tr)[0]


def _rope_tables(s):
    half = RET_QK // 2
    inv = 1.0 / (ROPE_BASE ** (jnp.arange(half, dtype=F32) / half))
    inv2 = jnp.concatenate([inv, inv]).reshape(1, RET_QK)
    sign = jnp.concatenate([-jnp.ones((half,), F32), jnp.ones((half,), F32)]).reshape(1, RET_QK)
    tr = ROW_TILE

    def body(inv_ref, sign_ref, cos_ref, sin_ref):
        i = pl.program_id(0)
        pos = (lax.broadcasted_iota(jnp.int32, (tr, RET_QK), 0) + i * tr).astype(F32)
        ang = pos * inv_ref[...]
        cos_ref[...] = jnp.cos(ang)
        sin_ref[...] = jnp.sin(ang) * sign_ref[...]

    vec = pl.BlockSpec((1, RET_QK), lambda i: (0, 0))
    blk = pl.BlockSpec((tr, RET_QK), lambda i: (i, 0))
    return pl.pallas_call(
        body, name="rope_tables", grid=(s // tr,), in_specs=[vec, vec], out_specs=[blk, blk],
        out_shape=[jax.ShapeDtypeStruct((s, RET_QK), F32)] * 2, compiler_params=_cparams(),
    )(inv2, sign)


def _swap_halves(x):
    return pltpu.roll(x, RET_QK // 2, 1)


def _norm(u):
    mu = jnp.mean(u, axis=-1, keepdims=True)
    d = u - mu
    var = jnp.mean(d * d, axis=-1, keepdims=True)
    rstd = lax.rsqrt(var + LN_EPS)
    return d * rstd, rstd


def _norm_bwd(dxh, xhat, rstd):
    m1 = jnp.mean(dxh, axis=-1, keepdims=True)
    m2 = jnp.mean(dxh * xhat, axis=-1, keepdims=True)
    return rstd * (dxh - m1 - xhat * m2)


def _colsum(t):
    return jnp.sum(t, axis=0, keepdims=True)


def _split_mm(t, tri):
    hi = t.astype(BF16)
    lo = (t - hi.astype(F32)).astype(BF16)
    return _dot(hi, tri, 1, 0) + _dot(lo, tri, 1, 0)


def _sb_masks():
    t = SB_BLOCK
    lane = lax.broadcasted_iota(jnp.int32, (1, LANES), 1)
    first = lane < SB_DIM
    m0 = jnp.where(first, 1.0, 0.0).astype(BF16)
    m1 = jnp.where(first, 0.0, 1.0).astype(BF16)
    row = lax.broadcasted_iota(jnp.int32, (t, t), 0)
    col = lax.broadcasted_iota(jnp.int32, (t, t), 1)
    return first, (m0, m1), row, col


def _sb_logits(qh, k, causal):
    z = _dot(qh, k, 1, 1)
    lp = jnp.log(1.0 + jnp.exp(-jnp.abs(z)))
    a = jnp.minimum(z, 0.0) - lp
    r = jnp.minimum(-z, 0.0) - lp
    if causal is not None:
        r = jnp.where(causal, r, 0.0)
    return a, r


def _sb_walk(i, block, l_ref, causal):
    block(i, causal)

    def alive():
        top = jnp.max(jnp.maximum(l_ref[0], l_ref[1]))
        return jnp.where(top > SB_DEAD, 1, 0)

    def cond(c):
        return jnp.logical_and(c[0] < i, c[1] > 0)

    def step(c):
        block(i - 1 - c[0], None)
        return c[0] + 1, alive()

    lax.while_loop(cond, step, (jnp.int32(0), alive()))


def _host(comm, n_in, n_out):
    if comm is None:
        return [], [], [], [], [], lambda refs: (refs[:n_in], refs[n_in:n_in + n_out], refs[n_in + n_out:], None)
    in_specs, out_specs = comm.specs
    n_ci, n_co, n_sem = len(comm.ins), len(comm.out_shape), len(comm.scratch)

    def split(refs):
        ins, ci = refs[:n_in], refs[n_in:n_in + n_ci]
        rest = refs[n_in + n_ci:]
        outs, co = rest[:n_out], rest[n_out:n_out + n_co]
        sems, scratch = rest[n_out + n_co:n_out + n_co + n_sem], rest[n_out + n_co + n_sem:]
        return ins, outs, scratch, (ci, co, sems)

    return in_specs, out_specs, list(comm.out_shape), list(comm.scratch), list(comm.ins), split


def _sb_fwd(qkv, s, comm=None):
    t = SB_BLOCK
    nq = s // t
    grid = (SB_HEADS // 2, nq)
    c_in_specs, c_out_specs, c_out_shape, c_scratch, c_ins, split = _host(comm, 3, 2)

    def body(*refs):
        (q_ref, k_ref, v_ref), (o_ref, of_ref), (l_ref, acc_ref), riding = split(refs)
        i = pl.program_id(1)
        if comm is not None:
            first_step, last_step = _grid_ends(grid)
            pl.when(first_step)(lambda: comm.start(*riding))
        first, hmask, row, col = _sb_masks()
        after = jnp.where(row > col, 1.0, 0.0).astype(BF16)
        causal = col < row
        q = q_ref[...]
        qh = (q * hmask[0], q * hmask[1])
        l_ref[...] = jnp.zeros_like(l_ref)
        acc_ref[...] = jnp.zeros_like(acc_ref)

        def block(kb, mask):
            start = pl.multiple_of(kb * t, t)
            k = k_ref[pl.ds(start, t), :]
            v = v_ref[pl.ds(start, t), :]
            half = t // 2
            parts = [(h, slice(p * half, (p + 1) * half)) for p in range(2) for h in range(2)]
            carry = [l_ref[h, rows] for h, rows in parts]
            ar = [_sb_logits(qh[h][rows], k, None if mask is None else mask[rows]) for h, rows in parts]
            later = [_split_mm(ar[c][1], after) for c in range(4)]
            w = [jnp.exp(ar[c][0] + later[c] + carry[c]) for c in range(4)]
            if mask is not None:
                w = [jnp.where(mask[rows], w[c], 0.0) for c, (_, rows) in enumerate(parts)]
            pv = [_dot(w[c].astype(BF16), v, 1, 0) for c in range(4)]
            for p in range(2):
                rows = parts[2 * p][1]
                acc_ref[rows, :] += jnp.where(first, pv[2 * p], pv[2 * p + 1])
            for c, (h, rows) in enumerate(parts):
                l_ref[h, rows] = carry[c] + jnp.sum(ar[c][1], axis=1, keepdims=True)

        _sb_walk(i, block, l_ref, causal)
        o_ref[...] = acc_ref[...].astype(o_ref.dtype)
        of_ref[...] = acc_ref[...]
        if comm is not None:
            pl.when(last_step)(lambda: comm.finish(*riding))

    nk_off = SB_WIDTH // LANES
    blk = pl.BlockSpec((t, LANES), lambda p, i: (i, p))
    return pl.pallas_call(
        body, name="sb_fwd", grid=grid,
        in_specs=[blk,
                  pl.BlockSpec((s, LANES), lambda p, i: (0, nk_off + p)),
                  pl.BlockSpec((s, LANES), lambda p, i: (0, 2 * nk_off + p))] + c_in_specs,
        out_specs=[blk, blk] + c_out_specs,
        out_shape=[jax.ShapeDtypeStruct((s, SB_WIDTH), BF16), jax.ShapeDtypeStruct((s, SB_WIDTH), F32)] + c_out_shape,
        scratch_shapes=c_scratch + [pltpu.VMEM((2, t, 1), F32), pltpu.VMEM((t, LANES), F32)],
        compiler_params=_cparams(),
    )(qkv, qkv, qkv, *c_ins)


def _sb_bwd(qkv, o, do, s, comm=None):
    t = SB_BLOCK
    nq = s // t
    grid = (SB_HEADS // 2, nq)
    c_in_specs, c_out_specs, c_out_shape, c_scratch, c_ins, split = _host(comm, 5, 3)

    def body(*refs):
        ((q_ref, k_ref, v_ref, o_ref, do_ref), (dq_ref, dk_ref, dv_ref),
         (l_ref, e_ref, dq_acc, dk_acc, dv_acc), riding) = split(refs)
        i = pl.program_id(1)
        if comm is not None:
            first_step, last_step = _grid_ends(grid)
            pl.when(first_step)(lambda: comm.start(*riding))
        first, hmask, row, col = _sb_masks()
        after = jnp.where(row > col, 1.0, 0.0).astype(BF16)
        from_here = jnp.where(row >= col, 1.0, 0.0).astype(BF16)
        causal = col < row

        @pl.when(i == 0)
        def _():
            dk_acc[...] = jnp.zeros_like(dk_acc)
            dv_acc[...] = jnp.zeros_like(dv_acc)

        q = q_ref[...]
        do_ = do_ref[...]
        qh = (q * hmask[0], q * hmask[1])
        doh = (do_ * hmask[0], do_ * hmask[1])
        prod = do_.astype(F32) * o_ref[...]
        total = (jnp.sum(jnp.where(first, prod, 0.0), axis=1, keepdims=True),
                 jnp.sum(jnp.where(first, 0.0, prod), axis=1, keepdims=True))
        l_ref[...] = jnp.zeros_like(l_ref)
        e_ref[...] = jnp.zeros_like(e_ref)
        dq_acc[...] = jnp.zeros_like(dq_acc)

        def block(kb, mask):
            start = pl.multiple_of(kb * t, t)
            k = k_ref[pl.ds(start, t), :]
            v = v_ref[pl.ds(start, t), :]
            heads = range(2)
            carry = [l_ref[h] for h in heads]
            e_carry = [e_ref[h] for h in heads]
            ar = [_sb_logits(qh[h], k, mask) for h in heads]
            dw = [_dot(doh[h], v, 1, 1) for h in heads]
            later = [_split_mm(ar[h][1], after) for h in heads]
            w = [jnp.exp(ar[h][0] + later[h] + carry[h]) for h in heads]
            if mask is not None:
                w = [jnp.where(mask, w[h], 0.0) for h in heads]
            wb = [w[h].astype(BF16) for h in heads]
            dvs = [_dot(wb[h], do_, 0, 0) for h in heads]
            e = [dw[h] * wb[h].astype(F32) for h in heads]
            suffix = [_split_mm(e[h], from_here) for h in heads]
            dz = []
            for h in heads:
                before = total[h] - (suffix[h] + e_carry[h])
                dzh = e[h] - jnp.exp(ar[h][0]) * (e[h] + before)
                if mask is not None:
                    dzh = jnp.where(mask, dzh, 0.0)
                dz.append(dzh.astype(BF16))
            dqs = [_dot(dz[h], k, 1, 0) for h in heads]
            dks = [_dot(dz[h], q, 0, 0) for h in heads]
            dq_acc[...] += jnp.where(first, dqs[0], dqs[1])
            dk_acc[pl.ds(start, t), :] += jnp.where(first, dks[0], dks[1])
            dv_acc[pl.ds(start, t), :] += jnp.where(first, dvs[0], dvs[1])
            for h in heads:
                l_ref[h] = carry[h] + jnp.sum(ar[h][1], axis=1, keepdims=True)
                e_ref[h] = e_carry[h] + jnp.sum(e[h], axis=1, keepdims=True)

        _sb_walk(i, block, l_ref, causal)
        dq_ref[...] = (dq_acc[...] * SB_SCALE).astype(dq_ref.dtype)

        @pl.when(i == nq - 1)
        def _():
            dk_ref[...] = dk_acc[...].astype(dk_ref.dtype)
            dv_ref[...] = dv_acc[...].astype(dv_ref.dtype)

        if comm is not None:
            pl.when(last_step)(lambda: comm.finish(*riding))

    nk_off = SB_WIDTH // LANES
    blk = pl.BlockSpec((t, LANES), lambda p, i: (i, p))
    col_blk = pl.BlockSpec((s, LANES), lambda p, i: (0, p))
    sds = jax.ShapeDtypeStruct((s, SB_WIDTH), BF16)
    return pl.pallas_call(
        body, name="sb_bwd", grid=grid,
        in_specs=[blk,
                  pl.BlockSpec((s, LANES), lambda p, i: (0, nk_off + p)),
                  pl.BlockSpec((s, LANES), lambda p, i: (0, 2 * nk_off + p)),
                  blk, blk] + c_in_specs,
        out_specs=[blk, col_blk, col_blk] + c_out_specs,
        out_shape=[sds, sds, sds] + c_out_shape,
        scratch_shapes=c_scratch + [pltpu.VMEM((2, t, 1), F32), pltpu.VMEM((2, t, 1), F32),
                                    pltpu.VMEM((t, LANES), F32), pltpu.VMEM((s, LANES), F32),
                                    pltpu.VMEM((s, LANES), F32)],
        compiler_params=_cparams(),
    )(qkv, qkv, qkv, o, do, *c_ins)


def _ret_log_gamma():
    lg = np.log1p(-np.exp2(-5.0 - np.arange(RET_HEADS, dtype=np.float32))).astype(np.float32)
    return jnp.asarray(np.broadcast_to(lg[:, None, None], (RET_HEADS, 8, LANES)).copy())


def _ret_decays(lg_ref):
    c = RET_BLOCK
    lg = lg_ref[0, 0:1, 0:1]
    row = lax.broadcasted_iota(jnp.int32, (c, c), 0)
    col = lax.broadcasted_iota(jnp.int32, (c, c), 1)
    rel = (row - col).astype(F32)
    within = jnp.where(row >= col, jnp.exp(lg * jnp.maximum(rel, 0.0)), 0.0)
    idx = lax.broadcasted_iota(jnp.int32, (c, 1), 0).astype(F32)
    q_dec = jnp.exp(lg * (idx + 1.0))
    k_dec = jnp.exp(lg * (c - 1.0 - idx))
    chunk_dec = jnp.exp(lg * float(c))
    return within, q_dec, k_dec, chunk_dec


def _ret_specs(s, reverse=False):
    c = RET_BLOCK
    nc = s // c
    pos = (lambda n: nc - 1 - n) if reverse else (lambda n: n)
    qk_heads = RET_QK_WIDTH // RET_QK
    q_spec = pl.BlockSpec((c, RET_QK), lambda h, n: (pos(n), h))
    k_spec = pl.BlockSpec((c, RET_QK), lambda h, n: (pos(n), qk_heads + h))
    v_spec = pl.BlockSpec((c, RET_V), lambda h, n: (pos(n), h))
    lg_spec = pl.BlockSpec((1, 8, LANES), lambda h, n: (h, 0, 0))
    rope_spec = pl.BlockSpec((c, RET_QK), lambda h, n: (pos(n), 0))
    return nc, q_spec, k_spec, v_spec, lg_spec, rope_spec


def _ret_fwd(rqk, rvg, s):
    nc, q_spec, k_spec, v_spec, lg_spec, _ = _ret_specs(s)
    g_spec = pl.BlockSpec((RET_BLOCK, RET_V), lambda h, n: (n, RET_HEADS + h))

    def body(q_ref, k_ref, v_ref, g_ref, lg_ref, r_ref, y_ref, state):
        n = pl.program_id(1)

        @pl.when(n == 0)
        def _():
            state[...] = jnp.zeros_like(state)

        within, q_dec, k_dec, chunk_dec = _ret_decays(lg_ref)
        q, k, v = q_ref[...], k_ref[...], v_ref[...]
        scores = _dot(q.astype(BF16), k.astype(BF16), 1, 1) * within
        out = _dot(scores.astype(BF16), v, 1, 0)
        out += _dot((q * q_dec).astype(BF16), state[...].astype(BF16), 1, 0)
        r_ref[...] = out
        xhat, _ = _norm(out)
        g = g_ref[...].astype(F32)
        y_ref[...] = (g * _sigmoid(g) * xhat).astype(y_ref.dtype)
        state[...] = state[...] * chunk_dec + _dot((k * k_dec).astype(BF16), v, 0, 0)

    return pl.pallas_call(
        body, name="ret_fwd", grid=(RET_HEADS, nc),
        in_specs=[q_spec, k_spec, v_spec, g_spec, lg_spec],
        out_specs=[v_spec, v_spec],
        out_shape=[jax.ShapeDtypeStruct((s, RET_V_WIDTH), F32), jax.ShapeDtypeStruct((s, RET_V_WIDTH), BF16)],
        scratch_shapes=[pltpu.VMEM((RET_QK, RET_V), F32)],
        compiler_params=_cparams(),
    )(rqk, rqk, rvg, rvg, _ret_log_gamma())


def _rope_bwd(d, cos, sin):
    return d * cos + _swap_halves(d * sin)


def _ret_bwd_q(rqk, rv, d_out, cos2, sin2, s):
    nc, q_spec, k_spec, v_spec, lg_spec, rope_spec = _ret_specs(s)

    def body(q_ref, k_ref, v_ref, d_ref, lg_ref, cos_ref, sin_ref, dq_ref, state):
        n = pl.program_id(1)

        @pl.when(n == 0)
        def _():
            state[...] = jnp.zeros_like(state)

        within, q_dec, k_dec, chunk_dec = _ret_decays(lg_ref)
        k, v, d = k_ref[...], v_ref[...], d_ref[...]
        st = state[...].astype(BF16)
        d_scores = _dot(d, v, 1, 1) * within
        dq = _dot(d_scores.astype(BF16), k.astype(BF16), 1, 0) + q_dec * _dot(d, st, 1, 1)
        dq_ref[...] = (_rope_bwd(dq, cos_ref[...], sin_ref[...]) * RET_SCALE).astype(dq_ref.dtype)
        state[...] = state[...] * chunk_dec + _dot((k * k_dec).astype(BF16), v, 0, 0)

    return pl.pallas_call(
        body, name="ret_bwd_q", grid=(RET_HEADS, nc),
        in_specs=[q_spec, k_spec, v_spec, v_spec, lg_spec, rope_spec, rope_spec],
        out_specs=q_spec,
        out_shape=jax.ShapeDtypeStruct((s, RET_QK_WIDTH), BF16),
        scratch_shapes=[pltpu.VMEM((RET_QK, RET_V), F32)],
        compiler_params=_cparams(),
    )(rqk, rqk, rv, d_out, _ret_log_gamma(), cos2, sin2)


def _ret_bwd_kv(rqk, rv, d_out, cos2, sin2, s):
    nc, q_spec, k_spec, v_spec, lg_spec, rope_spec = _ret_specs(s, reverse=True)

    def body(q_ref, k_ref, v_ref, d_ref, lg_ref, cos_ref, sin_ref, dk_ref, dv_ref, state):
        n = pl.program_id(1)

        @pl.when(n == 0)
        def _():
            state[...] = jnp.zeros_like(state)

        within, q_dec, k_dec, chunk_dec = _ret_decays(lg_ref)
        q, k, v, d = q_ref[...], k_ref[...], v_ref[...], d_ref[...]
        qb, kb = q.astype(BF16), k.astype(BF16)
        st = state[...].astype(BF16)
        scores = _dot(qb, kb, 1, 1) * within
        d_scores = _dot(d, v, 1, 1) * within
        dk = _dot(d_scores.astype(BF16), qb, 0, 0) + k_dec * _dot(v, st, 1, 1)
        dv = _dot(scores.astype(BF16), d, 0, 0) + k_dec * _dot(kb, st, 1, 0)
        dk_ref[...] = _rope_bwd(dk, cos_ref[...], sin_ref[...]).astype(dk_ref.dtype)
        dv_ref[...] = dv.astype(dv_ref.dtype)
        state[...] = state[...] * chunk_dec + _dot((q * q_dec).astype(BF16), d, 0, 0)

    c = RET_BLOCK
    dk_spec = pl.BlockSpec((c, RET_QK), lambda h, n: (nc - 1 - n, h))
    return pl.pallas_call(
        body, name="ret_bwd_kv", grid=(RET_HEADS, nc),
        in_specs=[q_spec, k_spec, v_spec, v_spec, lg_spec, rope_spec, rope_spec],
        out_specs=[dk_spec, v_spec],
        out_shape=[jax.ShapeDtypeStruct((s, RET_QK_WIDTH), BF16), jax.ShapeDtypeStruct((s, RET_V_WIDTH), BF16)],
        scratch_shapes=[pltpu.VMEM((RET_QK, RET_V), F32)],
        compiler_params=_cparams(),
    )(rqk, rqk, rv, d_out, _ret_log_gamma(), cos2, sin2)


def _xattn_probs(q, k):
    sc = _dot(q, k, 1, 1)
    sc = sc - jnp.max(sc, axis=-1, keepdims=True)
    p = jnp.exp(sc)
    return p / jnp.sum(p, axis=-1, keepdims=True)


def _xattn_fwd(qm, kv, s):
    tq = XATTN_ROWS

    def body(q_ref, kv_ref, o_ref):
        for h in range(MEM_HEADS):
            sl = slice(h * MEM_DIM, (h + 1) * MEM_DIM)
            sv = slice(D_MODEL + h * MEM_DIM, D_MODEL + (h + 1) * MEM_DIM)
            p = _xattn_probs(q_ref[:, sl], kv_ref[:, sl])
            o_ref[:, sl] = _dot(p.astype(BF16), kv_ref[:, sv], 1, 0).astype(o_ref.dtype)

    return pl.pallas_call(
        body, name="xattn_fwd", grid=(s // tq,),
        in_specs=[pl.BlockSpec((tq, D_MODEL), lambda i: (i, 0)),
                  pl.BlockSpec((MEM_LEN, 2 * D_MODEL), lambda i: (0, 0))],
        out_specs=pl.BlockSpec((tq, D_MODEL), lambda i: (i, 0)),
        out_shape=jax.ShapeDtypeStruct((s, D_MODEL), BF16),
        compiler_params=_cparams(),
    )(qm, kv)


def _xattn_bwd(qm, kv, do, s):
    tq = XATTN_ROWS

    def body(q_ref, kv_ref, do_ref, dq_ref, dkv_ref):
        i = pl.program_id(0)

        @pl.when(i == 0)
        def _():
            dkv_ref[...] = jnp.zeros_like(dkv_ref)

        for h in range(MEM_HEADS):
            sl = slice(h * MEM_DIM, (h + 1) * MEM_DIM)
            sv = slice(D_MODEL + h * MEM_DIM, D_MODEL + (h + 1) * MEM_DIM)
            q, k, v, d = q_ref[:, sl], kv_ref[:, sl], kv_ref[:, sv], do_ref[:, sl]
            p = _xattn_probs(q, k)
            dp = _dot(d, v, 1, 1)
            ds = (p * (dp - jnp.sum(p * dp, axis=-1, keepdims=True))).astype(BF16)
            dq_ref[:, sl] = (_dot(ds, k, 1, 0) * MEM_SCALE).astype(dq_ref.dtype)
            dkv_ref[:, sl] += _dot(ds, q, 0, 0)
            dkv_ref[:, sv] += _dot(p.astype(BF16), d, 0, 0)

    row_blk = pl.BlockSpec((tq, D_MODEL), lambda i: (i, 0))
    kv_blk = pl.BlockSpec((MEM_LEN, 2 * D_MODEL), lambda i: (0, 0))
    return pl.pallas_call(
        body, name="xattn_bwd", grid=(s // tq,),
        in_specs=[row_blk, kv_blk, row_blk],
        out_specs=[row_blk, kv_blk],
        out_shape=[jax.ShapeDtypeStruct((s, D_MODEL), BF16), jax.ShapeDtypeStruct((MEM_LEN, 2 * D_MODEL), F32)],
        compiler_params=_cparams(),
    )(qm, kv, do)


def _place():
    x, y, c = lax.axis_index("x"), lax.axis_index("y"), lax.axis_index("c")
    others = [(1 - x, y), (x, 1 - y), (1 - x, 1 - y)]
    return x, y, c, others


def _slab(ref, axis, chip, size):
    start = pl.multiple_of(chip * size, LANES if axis == 1 else 16)
    if axis == 0:
        return ref.at[pl.ds(start, size), :]
    return ref.at[:, pl.ds(start, size)]


class _CommPlan:
    def __init__(self, ins, out_shape, scratch, start, finish):
        self.ins, self.out_shape, self.scratch, self.start, self.finish = ins, out_shape, scratch, start, finish

    @property
    def specs(self):
        any_spec = pl.BlockSpec(memory_space=pl.ANY)
        return [any_spec] * len(self.ins), [any_spec] * len(self.out_shape)

    def split(self, refs):
        n_in, n_out = len(self.ins), len(self.out_shape)
        return refs[:n_in], refs[n_in:n_in + n_out], refs[n_in + n_out:]


def _run_plan(name, plan):
    def body(*refs):
        plan.start(*plan.split(refs))
        plan.finish(*plan.split(refs))

    in_specs, out_specs = plan.specs
    return pl.pallas_call(body, name=name, in_specs=in_specs, out_specs=out_specs, out_shape=plan.out_shape,
                          scratch_shapes=plan.scratch)(*plan.ins)


def _gather_plan(names, shards):
    spec = {name: (shape, axis) for name, shape, axis in BIG}
    nw = len(names)

    def shard_half(ref, c):
        rows = ref.shape[0] // 2
        return ref.at[pl.ds(pl.multiple_of(c * rows, 16), rows), :]

    def region(ref, w, chip, c):
        shape, axis = spec[names[w]]
        size = shape[axis] // N_CHIPS
        if axis == 0:
            rows = size // 2
            return ref.at[pl.ds(pl.multiple_of(chip * size + c * rows, 16), rows), :]
        rows = shape[0] // 2
        return ref.at[pl.ds(pl.multiple_of(c * rows, 16), rows), pl.ds(pl.multiple_of(chip * size, LANES), size)]

    def ops(shard, full, sems):
        ici_send, ici_recv, d2d_send, d2d_recv, local_sems = sems
        x, y, c, others = _place()
        mine, sibling = 2 * x + y, (x, y, 1 - c)
        local, over_ici, arrived, passed_on, from_sibling = [], [], [], [], []
        for w in range(nw):
            shape, axis = spec[names[w]]
            local.append(pltpu.make_async_copy(shard[w], _slab(full[w], axis, mine, shape[axis] // N_CHIPS),
                                               local_sems.at[w]))
            for t, (qx, qy) in enumerate(others):
                n, theirs = 3 * w + t, 2 * qx + qy
                over_ici.append(pltpu.make_async_remote_copy(
                    src_ref=shard_half(shard[w], c), dst_ref=region(full[w], w, mine, c),
                    send_sem=ici_send.at[n], recv_sem=ici_recv.at[n], device_id=(qx, qy, c), device_id_type=MESH))
                arrived.append(pltpu.make_async_remote_copy(
                    src_ref=shard_half(shard[w], c), dst_ref=region(full[w], w, theirs, c),
                    send_sem=ici_send.at[n], recv_sem=ici_recv.at[n], device_id=(qx, qy, c), device_id_type=MESH))
                passed_on.append(pltpu.make_async_remote_copy(
                    src_ref=region(full[w], w, theirs, c), dst_ref=region(full[w], w, theirs, c),
                    send_sem=d2d_send.at[n], recv_sem=d2d_recv.at[n], device_id=sibling, device_id_type=MESH))
                from_sibling.append(pltpu.make_async_remote_copy(
                    src_ref=region(full[w], w, theirs, c), dst_ref=region(full[w], w, theirs, 1 - c),
                    send_sem=d2d_send.at[n], recv_sem=d2d_recv.at[n], device_id=sibling, device_id_type=MESH))
        return local, over_ici, arrived, passed_on, from_sibling

    def start(shard, full, sems):
        local, over_ici, _, _, _ = ops(shard, full, sems)
        for cp in local + over_ici:
            cp.start()

    def finish(shard, full, sems):
        local, over_ici, arrived, passed_on, from_sibling = ops(shard, full, sems)
        for got, onward in zip(arrived, passed_on, strict=True):
            got.wait_recv()
            onward.start()
        for got in from_sibling:
            got.wait_recv()
        for cp in over_ici + passed_on:
            cp.wait_send()
        for cp in local:
            cp.wait()

    dma = pltpu.SemaphoreType.DMA
    return _CommPlan(
        ins=[shards[name] for name in names],
        out_shape=[jax.ShapeDtypeStruct(spec[name][0], BF16) for name in names],
        scratch=[dma((3 * nw,)), dma((3 * nw,)), dma((3 * nw,)), dma((3 * nw,)), dma((nw,))],
        start=start, finish=finish)


def _shard_shape(shape, axis):
    return tuple(d // N_CHIPS if a == axis else d for a, d in enumerate(shape))


def _exchange_plan(names, grads):
    spec = {name: (shape, axis) for name, shape, axis in BIG}
    nw = len(names)

    def ops(grad, stack, sems):
        send_sems, recv_sems, local_sems = sems
        x, y, c, others = _place()
        mine = 2 * x + y
        me, sibling = (x, y, c), (x, y, 1 - c)

        def dev(px, py, pc):
            return 4 * px + 2 * py + pc

        def copy(w, n, src, slot, to):
            return pltpu.make_async_remote_copy(
                src_ref=src, dst_ref=stack[w].at[slot], send_sem=send_sems.at[7 * w + n],
                recv_sem=recv_sems.at[7 * w + n], device_id=to, device_id_type=MESH)

        local, first, arrived, passed_on, from_sibling = [], [], [], [], []
        for w in range(nw):
            shape, axis = spec[names[w]]
            size = shape[axis] // N_CHIPS
            own = _slab(grad[w], axis, mine, size)
            local.append(pltpu.make_async_copy(own, stack[w].at[dev(*me)], local_sems.at[w]))
            first.append(copy(w, 0, own, dev(*me), sibling))
            from_sibling.append(copy(w, 0, own, dev(*sibling), me))
            for t, (qx, qy) in enumerate(others):
                got = stack[w].at[dev(qx, qy, c)]
                first.append(copy(w, 1 + t, _slab(grad[w], axis, 2 * qx + qy, size), dev(*me), (qx, qy, c)))
                arrived.append(copy(w, 1 + t, got, dev(qx, qy, c), me))
                passed_on.append(copy(w, 4 + t, got, dev(qx, qy, c), sibling))
                from_sibling.append(copy(w, 4 + t, got, dev(qx, qy, 1 - c), me))
        return local, first, arrived, passed_on, from_sibling

    def start(grad, stack, sems):
        local, first, _, _, _ = ops(grad, stack, sems)
        for cp in local + first:
            cp.start()

    def finish(grad, stack, sems):
        local, first, arrived, passed_on, from_sibling = ops(grad, stack, sems)
        for got, onward in zip(arrived, passed_on, strict=True):
            got.wait_recv()
            onward.start()
        for got in from_sibling:
            got.wait_recv()
        for cp in first + passed_on:
            cp.wait_send()
        for cp in local:
            cp.wait()

    dma = pltpu.SemaphoreType.DMA
    return _CommPlan(
        ins=[grads[name] for name in names],
        out_shape=[jax.ShapeDtypeStruct((N_DEV,) + _shard_shape(*spec[name]), BF16) for name in names],
        scratch=[dma((7 * nw,)), dma((7 * nw,)), dma((nw,))],
        start=start, finish=finish)


def _adamw(w, g, m, v):
    m = ADAM_B1 * m + (1.0 - ADAM_B1) * g
    v = ADAM_B2 * v + (1.0 - ADAM_B2) * (g * g)
    m_hat = m / (1.0 - ADAM_B1 ** ADAM_STEP)
    v_hat = v / (1.0 - ADAM_B2 ** ADAM_STEP)
    delta = -ADAM_LR * (m_hat / (jnp.sqrt(v_hat) + ADAM_EPS) + ADAM_WD * w)
    return delta, m, v


def _reduce_adamw(name, stack, w, m, v):
    rows, cols = w.shape
    tr = next(t for t in (256, 128, 64) if rows % t == 0)

    def body(s_ref, w_ref, m_ref, v_ref, g_ref, d_ref, nm_ref, nv_ref):
        g = s_ref[0].astype(F32)
        for d in range(1, N_DEV):
            g = g + s_ref[d].astype(F32)
        g_ref[...] = g
        d_ref[...], nm_ref[...], nv_ref[...] = _adamw(w_ref[...], g, m_ref[...], v_ref[...])

    blk = pl.BlockSpec((tr, cols), lambda i: (i, 0))
    return pl.pallas_call(
        body, name=name, grid=(rows // tr,),
        in_specs=[pl.BlockSpec((N_DEV, tr, cols), lambda i: (0, i, 0)), blk, blk, blk],
        out_specs=[blk] * 4, out_shape=[jax.ShapeDtypeStruct((rows, cols), F32)] * 4,
        compiler_params=_cparams(),
    )(stack, w, m, v)


def _small_step(pack, w, m, v):
    def body(p_ref, w_ref, m_ref, v_ref, g_ref, d_ref, nm_ref, nv_ref, loss_ref, all_ref, send_sems, recv_sems):
        x, y, c, _ = _place()
        me = 4 * x + 2 * y + c
        all_ref[me] = p_ref[...]
        sent = []
        for n in range(1, N_DEV):
            peer = me ^ n
            cp = pltpu.make_async_remote_copy(
                src_ref=p_ref, dst_ref=all_ref.at[me], send_sem=send_sems.at[n - 1], recv_sem=recv_sems.at[n - 1],
                device_id=(peer // 4, (peer // 2) % 2, peer % 2), device_id_type=MESH)
            cp.start()
            sent.append(cp)
        for n in range(1, N_DEV):
            peer = me ^ n
            pltpu.make_async_remote_copy(
                src_ref=p_ref, dst_ref=all_ref.at[peer], send_sem=send_sems.at[n - 1], recv_sem=recv_sems.at[n - 1],
                device_id=(peer // 4, (peer // 2) % 2, peer % 2), device_id_type=MESH).wait_recv()
        for cp in sent:
            cp.wait_send()
        tot = all_ref[0]
        for d in range(1, N_DEV):
            tot = tot + all_ref[d]
        g = tot[:SMALL_ROWS]
        g_ref[...] = g
        d_ref[...], nm_ref[...], nv_ref[...] = _adamw(w_ref[...], g, m_ref[...], v_ref[...])
        loss_ref[...] = jnp.sum(jnp.sum(tot[SMALL_ROWS:], axis=1, keepdims=True), axis=0, keepdims=True)

    vm = pl.BlockSpec(memory_space=pltpu.VMEM)
    small = jax.ShapeDtypeStruct((SMALL_ROWS, LANES), F32)
    return pl.pallas_call(
        body, name="small_step",
        in_specs=[vm] * 4, out_specs=[vm] * 5,
        out_shape=[small] * 4 + [jax.ShapeDtypeStruct((1, 1), F32)],
        scratch_shapes=[pltpu.VMEM((N_DEV, PACK_ROWS, LANES), F32),
                        pltpu.SemaphoreType.DMA((N_DEV - 1,)), pltpu.SemaphoreType.DMA((N_DEV - 1,))],
    )(pack, w, m, v)


LATER_WEIGHTS = tuple(name for name, _, _ in BIG if name != "w_in")


def _layer_step(x, mem, tgt, w_in, shards, vec):
    s = x.shape[0]
    d = D_MODEL
    tm = min(ROW_TILE, s)
    tl = min(WIDE_TILE, s)
    cos2, sin2 = _rope_tables(s)
    xb = _cast_bf16("cast_x", x)
    bf = lambda w: ((s, w), BF16)
    f32 = lambda w: ((s, w), F32)

    (sb_qkv,) = _mm(
        "in_sb", xb, w_in, s, 3 * SB_WIDTH, d, tm=tl, tn=512, tk=d,
        epi=lambda acc, t, i, j: ([acc * jnp.where(j == 0, SB_SCALE, 1.0)], []),
        outs=[(*bf(3 * SB_WIDTH), *_tile(tl, 512))])

    def rope_epi(acc, t, i, j):
        cos, sin = t
        scale = jnp.where(j == 0, RET_SCALE, 1.0)
        parts = []
        for g in range(512 // RET_QK):
            xg = acc[:, g * RET_QK:(g + 1) * RET_QK]
            parts.append((xg * cos + _swap_halves(xg) * sin) * scale)
        return [jnp.concatenate(parts, axis=1)], []

    rope_in = ((tl, RET_QK), lambda i, j: (i, 0))
    (rqk,) = _mm("in_rqk", xb, w_in, s, 2 * RET_QK_WIDTH, d, tm=tl, tn=512, tk=d, b_off=(0, OFF_RET_Q // 512),
                 epi=rope_epi, ins=[(cos2, *rope_in), (sin2, *rope_in)],
                 outs=[(*f32(2 * RET_QK_WIDTH), *_tile(tl, 512))])
    (rvg,) = _mm("in_rvg", xb, w_in, s, 2 * RET_V_WIDTH, d, tm=tl, tn=512, tk=d, b_off=(0, OFF_RET_V // 512),
                 epi=_plain, outs=[(*bf(2 * RET_V_WIDTH), *_tile(tl, 512))])
    (gates,) = _mm("in_gate", xb, w_in, s, 2 * d, d, tm=tl, tn=512, tk=d, b_off=(0, OFF_GATE // 512),
                   epi=lambda acc, t, i, j: ([_sigmoid(acc + t[0])], []),
                   ins=[(vec["b_gate"], *_rowvec(512))], outs=[(*bf(2 * d), *_tile(tl, 512))])

    sb_out, sb_out_f32, *gathered = _sb_fwd(sb_qkv, s, comm=_gather_plan(LATER_WEIGHTS, shards))
    wt = dict(zip(LATER_WEIGHTS, gathered, strict=True))
    ret, gated = _ret_fwd(rqk, rvg, s)
    (y_sb,) = _mm("sb_o", sb_out, wt["w_sb_o"], s, d, SB_WIDTH, tm=tl, tn=d, tk=SB_WIDTH, epi=_plain,
                  outs=[(*bf(d), *_tile(tl, d))])
    y_ret, mixin = _mm(
        "ret_o", gated, wt["w_ret_o"], s, d, RET_V_WIDTH, tm=tl, tn=d, tk=RET_V_WIDTH,
        epi=lambda acc, t, i, j: ([acc, t[0].astype(F32) * t[2].astype(F32) + t[1].astype(F32) * acc], []),
        ins=[(gates, *_tile(tl, d)), (gates, *_tile(tl, d, 1)), (y_sb, *_tile(tl, d))],
        outs=[(*bf(d), *_tile(tl, d)), (*bf(d), *_tile(tl, d))])

    def ln_epi(acc, t, i, j):
        res, g, b = t
        xhat, rstd = _norm(DN_ALPHA * res + acc)
        y = xhat * g + b
        return [y, y, xhat, rstd], []

    full = _tile(tm, d)
    col1 = ((tm, 1), lambda i, j: (i, 0))
    ln_outs = [(*f32(d), *full), (*bf(d), *full), (*f32(d), *full), ((s, 1), F32, *col1)]
    x1, x1b, xhat1, rstd1 = _mm(
        "mix_o", mixin, wt["w_mix_o"], s, d, d, tm=tm, tn=d, tk=d, epi=ln_epi,
        ins=[(x, *full), (vec["ln1_g"], *_rowvec(d)), (vec["ln1_b"], *_rowvec(d))], outs=ln_outs)

    (qm,) = _mm("mem_q", x1b, wt["w_mem_q"], s, d, d, tm=tl, tn=d, tk=d,
                epi=lambda acc, t, i, j: ([acc * MEM_SCALE], []), outs=[(*bf(d), *_tile(tl, d))])
    (kv,) = _mm("mem_kv", mem, wt["w_mem_kv"], MEM_LEN, 2 * d, d, tm=MEM_LEN, tn=d, tk=d, epi=_plain,
                outs=[((MEM_LEN, 2 * d), BF16, *_tile(MEM_LEN, d))])
    att = _xattn_fwd(qm, kv, s)
    x2, x2b, xhat2, rstd2 = _mm(
        "mem_o", att, wt["w_mem_o"], s, d, d, tm=tm, tn=d, tk=d, epi=ln_epi,
        ins=[(x1, *full), (vec["ln2_g"], *_rowvec(d)), (vec["ln2_b"], *_rowvec(d))], outs=ln_outs)

    fh = FFN_HIDDEN
    tf = fh // 2
    (f1,) = _mm("ffn_in1", x2b, wt["w_ffn_in"], s, fh, d, tm=tl, tn=tf, tk=d, epi=_plain, j_outer=True,
                outs=[(*bf(fh), *_tile(tl, tf))])

    def swiglu_epi(acc, t, i, j):
        a = t[0].astype(F32)
        return [acc, a * _sigmoid(a) * acc], []

    f2, act = _mm(
        "ffn_in2", x2b, wt["w_ffn_in"], s, fh, d, tm=tm, tn=tf, tk=d, b_off=(0, 2), epi=swiglu_epi, j_outer=True,
        ins=[(f1, *_tile(tm, tf))], outs=[(*bf(fh), *_tile(tm, tf)), (*bf(fh), *_tile(tm, tf))])

    def head_epi(acc, t, i, j):
        res, g, b, target = t
        xhat, rstd = _norm(DN_ALPHA * res + acc)
        err = xhat * g + b - target
        dy = err * (1.0 / d)
        du = _norm_bwd(dy * g, xhat, rstd)
        return [du, du], [_colsum(dy * xhat), _colsum(dy), _colsum(err * err) * (0.5 / d)]

    vec_acc = ((1, d), F32)
    du3, du3b, dg3, db3, loss_cols = _mm(
        "ffn_out", act, wt["w_ffn_out"], s, d, fh, tm=tm, tn=d, tk=fh, epi=head_epi,
        ins=[(x2, *full), (vec["ln3_g"], *_rowvec(d)), (vec["ln3_b"], *_rowvec(d)), (tgt, *full)],
        outs=[(*f32(d), *full), (*bf(d), *full)], accs=[vec_acc] * 3)

    grads = {}
    ts = min(SEQ_TILE, s)

    def wgrad(name, a, b, m, n, tm_, tn_, tk_=None):
        (g,) = _mm(name, a, b, m, n, a.shape[0], tm=tm_, tn=tn_, tk=tk_ or ts, ta=True, epi=_plain,
                   outs=[((m, n), BF16, *_tile(tm_, tn_))])
        return g

    def ffn_bwd_epi(acc, t, i, j):
        a, b = t[0].astype(F32), t[1].astype(F32)
        sg = _sigmoid(a)
        return [acc * b * (sg * (1.0 + a * (1.0 - sg))), acc * (a * sg)], []

    df1, df2 = _mm(
        "ffn_out_t", du3b, wt["w_ffn_out"], s, fh, d, tm=tm, tn=tf, tk=d, tb=True, epi=ffn_bwd_epi, j_outer=True,
        ins=[(f1, *_tile(tm, tf)), (f2, *_tile(tm, tf))],
        outs=[(*bf(fh), *_tile(tm, tf)), (*bf(fh), *_tile(tm, tf))])
    grads["w_ffn_out"] = wgrad("g_ffn_out", act, du3b, fh, d, tf, d)
    grads["w_ffn_in"] = jnp.concatenate(
        [wgrad("g_ffn_in1", x2b, df1, d, fh, d, tf), wgrad("g_ffn_in2", x2b, df2, d, fh, d, tf)], axis=1)
    (dx2a,) = _mm("ffn_in1_t", df1, wt["w_ffn_in"], s, d, fh, tm=tm, tn=d, tk=fh, tb=True, epi=_plain,
                  outs=[(*f32(d), *full)])

    def ln_bwd(name, a, b, k, tk, b_off, more, scales, xhat, rstd, g):
        def epi(acc, t, i, j):
            *extra, xh, rs, gg = t
            dy = acc
            for e, sc in zip(extra, scales, strict=True):
                dy = dy + e * sc
            du = _norm_bwd(dy * gg, xh, rs)
            return [du, du], [_colsum(dy * xh), _colsum(dy)]

        return _mm(name, a, b, s, d, k, tm=tm, tn=d, tk=tk, tb=True, b_off=b_off, epi=epi,
                   ins=[(e, *full) for e in more] + [(xhat, *full), (rstd, *col1), (g, *_rowvec(d))],
                   outs=[(*f32(d), *full), (*bf(d), *full)], accs=[vec_acc] * 2)

    du2, du2b, dg2, db2 = ln_bwd("ffn_in2_t", df2, wt["w_ffn_in"], fh, fh, (0, 1), [dx2a, du3], [1.0, DN_ALPHA],
                                 xhat2, rstd2, vec["ln2_g"])

    (datt,) = _mm("mem_o_t", du2b, wt["w_mem_o"], s, d, d, tm=tl, tn=d, tk=d, tb=True, epi=_plain,
                  outs=[(*bf(d), *_tile(tl, d))])
    grads["w_mem_o"] = wgrad("g_mem_o", att, du2b, d, d, d, d)
    dqm, dkv = _xattn_bwd(qm, kv, datt, s)
    grads["w_mem_q"] = wgrad("g_mem_q", x1b, dqm, d, d, d, d)
    grads["w_mem_kv"] = wgrad("g_mem_kv", mem, dkv, d, 2 * d, d, d, MEM_LEN)
    du1, du1b, dg1, db1 = ln_bwd("mem_q_t", dqm, wt["w_mem_q"], d, d, (0, 0), [du2], [DN_ALPHA],
                                 xhat1, rstd1, vec["ln1_g"])

    def merge_bwd_epi(acc, t, i, j):
        g0, g1, ysb, yret = (v.astype(F32) for v in t)
        dgate0 = acc * ysb * (g0 * (1.0 - g0))
        dgate1 = acc * yret * (g1 * (1.0 - g1))
        return [dgate0, dgate1, acc * g0, acc * g1], [_colsum(dgate0), _colsum(dgate1)]

    dgate0, dgate1, dy_sb, dy_ret, dbg0, dbg1 = _mm(
        "mix_o_t", du1b, wt["w_mix_o"], s, d, d, tm=tm, tn=d, tk=d, tb=True, epi=merge_bwd_epi,
        ins=[(gates, *full), (gates, *_tile(tm, d, 1)), (y_sb, *full), (y_ret, *full)],
        outs=[(*bf(d), *full)] * 4, accs=[vec_acc] * 2)
    grads["w_mix_o"] = wgrad("g_mix_o", mixin, du1b, d, d, d, d)
    grads["w_sb_o"] = wgrad("g_sb_o", sb_out, dy_sb, SB_WIDTH, d, SB_WIDTH, d)
    grads["w_ret_o"] = wgrad("g_ret_o", gated, dy_ret, RET_V_WIDTH, d, RET_V_WIDTH, d)
    (dsb_out,) = _mm("sb_o_t", dy_sb, wt["w_sb_o"], s, SB_WIDTH, d, tm=tl, tn=SB_WIDTH, tk=d, tb=True, epi=_plain,
                     outs=[(*bf(SB_WIDTH), *_tile(tl, SB_WIDTH))])

    def gate_norm_bwd_epi(acc, t, i, j):
        r, g = t[0], t[1].astype(F32)
        drg, dret = [], []
        for h in range(d // RET_V):
            sl = slice(h * RET_V, (h + 1) * RET_V)
            xhat, rstd = _norm(r[:, sl])
            gg, dd = g[:, sl], acc[:, sl]
            sg = _sigmoid(gg)
            drg.append(dd * xhat * (sg * (1.0 + gg * (1.0 - sg))))
            dret.append(_norm_bwd(dd * (gg * sg), xhat, rstd))
        return [jnp.concatenate(drg, axis=1), jnp.concatenate(dret, axis=1)], []

    drg, dret = _mm(
        "ret_o_t", dy_ret, wt["w_ret_o"], s, RET_V_WIDTH, d, tm=tm, tn=d, tk=d, tb=True, epi=gate_norm_bwd_epi,
        ins=[(ret, *full), (rvg, *_tile(tm, d, 1))],
        outs=[(*bf(RET_V_WIDTH), *full)] * 2)

    drq = _ret_bwd_q(rqk, rvg, dret, cos2, sin2, s)
    drk, drv = _ret_bwd_kv(rqk, rvg, dret, cos2, sin2, s)
    dsq, dsk, dsv, *stacked = _sb_bwd(sb_qkv, sb_out_f32, dsb_out, s, comm=_exchange_plan(LATER_WEIGHTS, grads))
    stacks = dict(zip(LATER_WEIGHTS, stacked, strict=True))

    dh = jnp.concatenate([dsq, dsk, dsv, drq, drk, drv, drg, dgate0, dgate1], axis=1)
    grads["w_in"] = wgrad("g_in", xb, dh, d, IN_WIDTH, d, IN_WIDTH // N_CHIPS)
    grad_x, stacks["w_in"] = _mm(
        "in_t", dh, w_in, s, d, IN_WIDTH, tm=tl, tn=d, tk=IN_WIDTH // N_CHIPS, tb=True,
        epi=lambda acc, t, i, j: ([acc + DN_ALPHA * t[0]], []),
        ins=[(du1, *_tile(tl, d))], outs=[(*f32(d), *_tile(tl, d))], comm=_exchange_plan(("w_in",), grads))

    small = {"b_gate": jnp.concatenate([dbg0, dbg1], axis=1), "ln1_g": dg1, "ln1_b": db1, "ln2_g": dg2,
             "ln2_b": db2, "ln3_g": dg3, "ln3_b": db3}
    return grad_x, stacks, small, loss_cols


def kernel(x, mem, w_in, b_gate, w_sb_o, w_ret_o, w_mix_o, ln1_g, ln1_b, w_mem_q, w_mem_kv, w_mem_o, ln2_g, ln2_b, w_ffn_in, w_ffn_out, ln3_g, ln3_b, loss_target, m_w_in, m_b_gate, m_w_sb_o, m_w_ret_o, m_w_mix_o, m_ln1_g, m_ln1_b, m_w_mem_q, m_w_mem_kv, m_w_mem_o, m_ln2_g, m_ln2_b, m_w_ffn_in, m_w_ffn_out, m_ln3_g, m_ln3_b, v_w_in, v_b_gate, v_w_sb_o, v_w_ret_o, v_w_mix_o, v_ln1_g, v_ln1_b, v_w_mem_q, v_w_mem_kv, v_w_mem_o, v_ln2_g, v_ln2_b, v_w_ffn_in, v_w_ffn_out, v_ln3_g, v_ln3_b):
    given = dict(locals())
    s = x.shape[1]
    x2d = x.reshape(s, D_MODEL)
    tgt = loss_target.reshape(s, D_MODEL)
    mem2d = mem.reshape(MEM_LEN, D_MODEL)
    shard = {name: given[name].reshape(_shard_shape(shape, axis)) for name, shape, axis in BIG}
    vec = {name: given[name] for name in SMALL}

    shards_bf = {name: _cast_bf16("cast_" + name, shard[name]) for name, _, _ in BIG}
    (w_in_full,) = _run_plan("gather_w_in", _gather_plan(("w_in",), shards_bf))

    grad_x, stacks, small, loss_cols = _layer_step(x2d, mem2d, tgt, w_in_full, shards_bf, vec)

    out = {}
    for name, shape, axis in BIG:
        stack = stacks[name]
        shp = given[name].shape
        res = _reduce_adamw("adamw_" + name, stack, shard[name], given["m_" + name].reshape(stack.shape[1:]),
                            given["v_" + name].reshape(stack.shape[1:]))
        out[name] = [r.reshape(shp) for r in res]

    pack = jnp.concatenate([small[name] for name in SMALL] + [loss_cols], axis=1).reshape(PACK_ROWS, LANES)
    cat = lambda pre: jnp.concatenate([given[pre + name] for name in SMALL], axis=1).reshape(SMALL_ROWS, LANES)
    *res, loss = _small_step(pack, cat(""), cat("m_"), cat("v_"))
    flat = [r.reshape(1, SMALL_LEN) for r in res]
    off = 0
    for name in SMALL:
        n = given[name].shape[1]
        out[name] = [r[:, off:off + n] for r in flat]
        off += n

    return (loss.reshape(()), grad_x.reshape(x.shape),
            *[out[name][0] for name in WEIGHT_ORDER], *[out[name][1] for name in WEIGHT_ORDER],
            *[out[name][2] for name in WEIGHT_ORDER], *[out[name][3] for name in WEIGHT_ORDER])
```

```python
import functools

import jax
import jax.numpy as jnp
import numpy as np
from jax import lax
from jax.experimental import pallas as pl
from jax.experimental.pallas import tpu as pltpu

F32, BF16 = jnp.float32, jnp.bfloat16
MESH = pl.DeviceIdType.MESH

D_MODEL = 1024
MEM_LEN = 256
SB_HEADS, SB_DIM, SB_WIDTH = 8, 64, 512
RET_HEADS, RET_QK, RET_V = 4, 128, 256
RET_QK_WIDTH, RET_V_WIDTH = 512, 1024
ROPE_BASE = 10000.0
MEM_HEADS, MEM_DIM = 4, 256
FFN_HIDDEN = 2816
IN_WIDTH = 6656
OFF_RET_Q, OFF_RET_V, OFF_RET_G, OFF_GATE = 1536, 2560, 3584, 4608
DN_ALPHA = 2.0 ** 0.25
LN_EPS = 1e-5
SB_SCALE = SB_DIM ** -0.5
SB_DEAD = -110.0
RET_SCALE = RET_QK ** -0.5
MEM_SCALE = MEM_DIM ** -0.5
ADAM_LR, ADAM_B1, ADAM_B2, ADAM_EPS, ADAM_WD, ADAM_STEP = 0.001, 0.9, 0.999, 1e-08, 0.01, 10

N_DEV, N_CHIPS = 8, 4

LANES = 128
VMEM_LIMIT_BYTES = 52 * 2 ** 20
ROW_TILE = 512
WIDE_TILE = 1024
SEQ_TILE = 1024
SB_BLOCK = 256
RET_BLOCK = 256
XATTN_ROWS = 512

BIG = (
    ("w_in", (D_MODEL, IN_WIDTH), 1),
    ("w_sb_o", (SB_WIDTH, D_MODEL), 1),
    ("w_ret_o", (RET_V_WIDTH, D_MODEL), 0),
    ("w_mix_o", (D_MODEL, D_MODEL), 0),
    ("w_mem_q", (D_MODEL, D_MODEL), 0),
    ("w_mem_kv", (D_MODEL, 2 * D_MODEL), 1),
    ("w_mem_o", (D_MODEL, D_MODEL), 0),
    ("w_ffn_in", (D_MODEL, 2 * FFN_HIDDEN), 1),
    ("w_ffn_out", (FFN_HIDDEN, D_MODEL), 0),
)
SMALL = ("b_gate", "ln1_g", "ln1_b", "ln2_g", "ln2_b", "ln3_g", "ln3_b")
SMALL_LEN = 2 * D_MODEL + 6 * D_MODEL
SMALL_ROWS = SMALL_LEN // LANES
PACK_ROWS = SMALL_ROWS + D_MODEL // LANES
WEIGHT_ORDER = ("w_in", "b_gate", "w_sb_o", "w_ret_o", "w_mix_o", "ln1_g", "ln1_b", "w_mem_q", "w_mem_kv",
                "w_mem_o", "ln2_g", "ln2_b", "w_ffn_in", "w_ffn_out", "ln3_g", "ln3_b")


def _cparams():
    return pltpu.CompilerParams(vmem_limit_bytes=VMEM_LIMIT_BYTES)


def _dot(a, b, ca, cb):
    return lax.dot_general(a, b, (((ca,), (cb,)), ((), ())), preferred_element_type=F32)


def _sigmoid(x):
    return 1.0 / (1.0 + jnp.exp(-x))


def _mm(name, a, b, m, n, k, *, tm, tn, tk, epi, outs, ins=(), accs=(), ta=False, tb=False,
        a_off=(0, 0), b_off=(0, 0), j_outer=False, comm=None):
    assert m % tm == 0 and n % tn == 0 and k % tk == 0, (name, m, n, k, tm, tn, tk)
    ni, nj, nk = m // tm, n // tn, k // tk
    assert not accs or nj == 1, name
    ij = (lambda g0, g1: (g1, g0)) if j_outer else (lambda g0, g1: (g0, g1))

    def spec(block, index):
        return pl.BlockSpec(block, lambda g0, g1, kk: index(*ij(g0, g1), kk))

    if ta:
        a_spec = spec((tk, tm), lambda i, j, kk: (kk + a_off[0], i + a_off[1]))
    else:
        a_spec = spec((tm, tk), lambda i, j, kk: (i + a_off[0], kk + a_off[1]))
    if tb:
        b_spec = spec((tn, tk), lambda i, j, kk: (j + b_off[0], kk + b_off[1]))
    else:
        b_spec = spec((tk, tn), lambda i, j, kk: (kk + b_off[0], j + b_off[1]))
    in_specs = [a_spec, b_spec]
    for _, bs, im in ins:
        in_specs.append(spec(bs, lambda i, j, kk, im=im: im(i, j)))
    out_specs, out_shape = [], []
    for shape, dtype, bs, im in outs:
        out_specs.append(spec(bs, lambda i, j, kk, im=im: im(i, j)))
        out_shape.append(jax.ShapeDtypeStruct(shape, dtype))
    for shape, dtype in accs:
        out_specs.append(spec(shape, lambda i, j, kk, nd=len(shape): (0,) * nd))
        out_shape.append(jax.ShapeDtypeStruct(shape, dtype))
    n_in, n_out, n_acc = len(ins), len(outs), len(accs)
    ca, cb = (0 if ta else 1), (1 if tb else 0)
    grid = (*ij(ni, nj), nk)
    comm_ins, comm_outs, comm_scratch = [], [], []
    if comm is not None:
        comm_in_specs, comm_out_specs = comm.specs
        comm_ins, comm_outs, comm_scratch = list(comm.ins), list(comm.out_shape), list(comm.scratch)
        in_specs += comm_in_specs
        out_specs += comm_out_specs
        out_shape += comm_outs
    n_ci, n_co = len(comm_ins), len(comm_outs)

    def body(*refs):
        a_ref, b_ref = refs[:2]
        in_refs = refs[2:2 + n_in]
        ci_refs = refs[2 + n_in:2 + n_in + n_ci]
        rest = refs[2 + n_in + n_ci:]
        out_refs, acc_refs = rest[:n_out], rest[n_out:n_out + n_acc]
        co_refs = rest[n_out + n_acc:n_out + n_acc + n_co]
        scratch = rest[n_out + n_acc + n_co:]
        sem_refs, scratch = scratch[:len(comm_scratch)], scratch[len(comm_scratch):]
        (i, j), kk = ij(pl.program_id(0), pl.program_id(1)), pl.program_id(2)
        if comm is not None:
            first_step, last_step = _grid_ends(grid)
            pl.when(first_step)(lambda: comm.start(ci_refs, co_refs, sem_refs))
        part = _dot(a_ref[...].astype(BF16), b_ref[...].astype(BF16), ca, cb)

        def finish(acc):
            o_tiles, a_tiles = epi(acc, [r[...] for r in in_refs], i, j)
            for r, t in zip(out_refs, o_tiles, strict=True):
                r[...] = t.astype(r.dtype)
            if n_acc:
                @pl.when(i == 0)
                def _():
                    for r, t in zip(acc_refs, a_tiles, strict=True):
                        r[...] = t

                @pl.when(i > 0)
                def _():
                    for r, t in zip(acc_refs, a_tiles, strict=True):
                        r[...] += t

        if nk == 1:
            finish(part)
        else:
            acc_ref = scratch[0]

            @pl.when(kk == 0)
            def _():
                acc_ref[...] = part

            @pl.when(kk > 0)
            def _():
                acc_ref[...] += part

            @pl.when(kk == nk - 1)
            def _():
                finish(acc_ref[...])

        if comm is not None:
            pl.when(last_step)(lambda: comm.finish(ci_refs, co_refs, sem_refs))

    res = pl.pallas_call(
        body, name=name, grid=grid, in_specs=in_specs, out_specs=out_specs, out_shape=out_shape,
        scratch_shapes=comm_scratch + ([pltpu.VMEM((tm, tn), F32)] if nk > 1 else []),
        compiler_params=_cparams(),
    )(a, b, *[x for x, _, _ in ins], *comm_ins)
    return res


def _grid_ends(grid):
    ids = [pl.program_id(ax) for ax in range(len(grid))]
    first = functools.reduce(jnp.logical_and, [p == 0 for p in ids])
    last = functools.reduce(jnp.logical_and, [p == n - 1 for p, n in zip(ids, grid, strict=True)])
    return first, last


def _tile(tm, tn, dj=0):
    return (tm, tn), (lambda i, j: (i, j + dj))


def _rowvec(tn, dj=0):
    return (1, tn), (lambda i, j: (0, j + dj))


def _plain(acc, tiles, i, j):
    return [acc], []


def _ew(name, fn, ins, outs, rows, tr):
    assert rows % tr == 0, (name, rows, tr)
    in_specs = []
    for x in ins:
        if x.shape[0] == rows:
            in_specs.append(pl.BlockSpec((tr, x.shape[1]), lambda i: (i, 0)))
        else:
            in_specs.append(pl.BlockSpec(x.shape, lambda i: (0, 0)))
    n_in = len(ins)

    def body(*refs):
        res = fn(*[r[...] for r in refs[:n_in]])
        for r, t in zip(refs[n_in:], res, strict=True):
            r[...] = t.astype(r.dtype)

    return pl.pallas_call(
        body, name=name, grid=(rows // tr,), in_specs=in_specs,
        out_specs=[pl.BlockSpec((tr, w), lambda i: (i, 0)) for w, _ in outs],
        out_shape=[jax.ShapeDtypeStruct((rows, w), dt) for w, dt in outs],
        compiler_params=_cparams(),
    )(*ins)


def _cast_bf16(name, x):
    rows = x.shape[0]
    tr = next(t for t in (512, 256, 64) if rows % t == 0)
    return _ew(name, lambda v: (v,), [x], [(x.shape[1], BF16)], rows, tr)[0]


def _rope_tables(s):
    half = RET_QK // 2
    inv = 1.0 / (ROPE_BASE ** (jnp.arange(half, dtype=F32) / half))
    inv2 = jnp.concatenate([inv, inv]).reshape(1, RET_QK)
    sign = jnp.concatenate([-jnp.ones((half,), F32), jnp.ones((half,), F32)]).reshape(1, RET_QK)
    tr = ROW_TILE

    def body(inv_ref, sign_ref, cos_ref, sin_ref):
        i = pl.program_id(0)
        pos = (lax.broadcasted_iota(jnp.int32, (tr, RET_QK), 0) + i * tr).astype(F32)
        ang = pos * inv_ref[...]
        cos_ref[...] = jnp.cos(ang)
        sin_ref[...] = jnp.sin(ang) * sign_ref[...]

    vec = pl.BlockSpec((1, RET_QK), lambda i: (0, 0))
    blk = pl.BlockSpec((tr, RET_QK), lambda i: (i, 0))
    return pl.pallas_call(
        body, name="rope_tables", grid=(s // tr,), in_specs=[vec, vec], out_specs=[blk, blk],
        out_shape=[jax.ShapeDtypeStruct((s, RET_QK), F32)] * 2, compiler_params=_cparams(),
    )(inv2, sign)


def _swap_halves(x):
    return pltpu.roll(x, RET_QK // 2, 1)


def _norm(u):
    mu = jnp.mean(u, axis=-1, keepdims=True)
    d = u - mu
    var = jnp.mean(d * d, axis=-1, keepdims=True)
    rstd = lax.rsqrt(var + LN_EPS)
    return d * rstd, rstd


def _norm_bwd(dxh, xhat, rstd):
    m1 = jnp.mean(dxh, axis=-1, keepdims=True)
    m2 = jnp.mean(dxh * xhat, axis=-1, keepdims=True)
    return rstd * (dxh - m1 - xhat * m2)


def _colsum(t):
    return jnp.sum(t, axis=0, keepdims=True)


def _split_mm(t, tri):
    hi = t.astype(BF16)
    lo = (t - hi.astype(F32)).astype(BF16)
    return _dot(hi, tri, 1, 0) + _dot(lo, tri, 1, 0)


def _sb_masks():
    t = SB_BLOCK
    lane = lax.broadcasted_iota(jnp.int32, (1, LANES), 1)
    first = lane < SB_DIM
    m0 = jnp.where(first, 1.0, 0.0).astype(BF16)
    m1 = jnp.where(first, 0.0, 1.0).astype(BF16)
    row = lax.broadcasted_iota(jnp.int32, (t, t), 0)
    col = lax.broadcasted_iota(jnp.int32, (t, t), 1)
    return first, (m0, m1), row, col


def _sb_logits(qh, k, causal):
    z = _dot(qh, k, 1, 1)
    lp = jnp.log(1.0 + jnp.exp(-jnp.abs(z)))
    a = jnp.minimum(z, 0.0) - lp
    r = jnp.minimum(-z, 0.0) - lp
    if causal is not None:
        r = jnp.where(causal, r, 0.0)
    return a, r


def _sb_walk(i, blocks, l_ref, causal):
    pl.when(i == 0)(lambda: blocks([(i, causal)]))
    pl.when(i > 0)(lambda: blocks([(i, causal), (i - 1, None)]))

    def alive():
        top = jnp.max(jnp.maximum(l_ref[0], l_ref[1]))
        return jnp.where(top > SB_DEAD, 1, 0)

    def cond(c):
        return jnp.logical_and(c[0] < i, c[1] > 0)

    def step(c):
        blocks([(i - 1 - c[0], None)])
        return c[0] + 1, alive()

    lax.while_loop(cond, step, (jnp.int32(1), alive()))


def _host(comm, n_in, n_out):
    if comm is None:
        return [], [], [], [], [], lambda refs: (refs[:n_in], refs[n_in:n_in + n_out], refs[n_in + n_out:], None)
    in_specs, out_specs = comm.specs
    n_ci, n_co, n_sem = len(comm.ins), len(comm.out_shape), len(comm.scratch)

    def split(refs):
        ins, ci = refs[:n_in], refs[n_in:n_in + n_ci]
        rest = refs[n_in + n_ci:]
        outs, co = rest[:n_out], rest[n_out:n_out + n_co]
        sems, scratch = rest[n_out + n_co:n_out + n_co + n_sem], rest[n_out + n_co + n_sem:]
        return ins, outs, scratch, (ci, co, sems)

    return in_specs, out_specs, list(comm.out_shape), list(comm.scratch), list(comm.ins), split


def _sb_fwd(qkv, s, comm=None):
    t = SB_BLOCK
    nq = s // t
    grid = (SB_HEADS // 2, nq)
    c_in_specs, c_out_specs, c_out_shape, c_scratch, c_ins, split = _host(comm, 3, 2)

    def body(*refs):
        (q_ref, k_ref, v_ref), (o_ref, of_ref), (l_ref, acc_ref), riding = split(refs)
        i = pl.program_id(1)
        if comm is not None:
            first_step, last_step = _grid_ends(grid)
            pl.when(first_step)(lambda: comm.start(*riding))
        first, hmask, row, col = _sb_masks()
        after = jnp.where(row > col, 1.0, 0.0).astype(BF16)
        causal = col < row
        q = q_ref[...]
        qh = (q * hmask[0], q * hmask[1])
        l_ref[...] = jnp.zeros_like(l_ref)
        acc_ref[...] = jnp.zeros_like(acc_ref)

        def blocks(todo):
            chains = [(b, h) for b in range(len(todo)) for h in range(2)]
            starts = [pl.multiple_of(kb * t, t) for kb, _ in todo]
            ks = [k_ref[pl.ds(st, t), :] for st in starts]
            vs = [v_ref[pl.ds(st, t), :] for st in starts]
            ar = {(b, h): _sb_logits(qh[h], ks[b], todo[b][1]) for b, h in chains}
            later = {bh: _split_mm(ar[bh][1], after) for bh in chains}
            carry = [l_ref[0], l_ref[1]]
            w = {}
            for b, (_, mask) in enumerate(todo):
                for h in range(2):
                    wbh = jnp.exp(ar[b, h][0] + later[b, h] + carry[h])
                    w[b, h] = wbh if mask is None else jnp.where(mask, wbh, 0.0)
                carry = [carry[h] + jnp.sum(ar[b, h][1], axis=1, keepdims=True) for h in range(2)]
            pv = {(b, h): _dot(w[b, h].astype(BF16), vs[b], 1, 0) for b, h in chains}
            acc = acc_ref[...]
            for b in range(len(todo)):
                acc = acc + jnp.where(first, pv[b, 0], pv[b, 1])
            acc_ref[...] = acc
            l_ref[0], l_ref[1] = carry

        _sb_walk(i, blocks, l_ref, causal)
        o_ref[...] = acc_ref[...].astype(o_ref.dtype)
        of_ref[...] = acc_ref[...]
        if comm is not None:
            pl.when(last_step)(lambda: comm.finish(*riding))

    nk_off = SB_WIDTH // LANES
    blk = pl.BlockSpec((t, LANES), lambda p, i: (i, p))
    return pl.pallas_call(
        body, name="sb_fwd", grid=grid,
        in_specs=[blk,
                  pl.BlockSpec((s, LANES), lambda p, i: (0, nk_off + p)),
                  pl.BlockSpec((s, LANES), lambda p, i: (0, 2 * nk_off + p))] + c_in_specs,
        out_specs=[blk, blk] + c_out_specs,
        out_shape=[jax.ShapeDtypeStruct((s, SB_WIDTH), BF16), jax.ShapeDtypeStruct((s, SB_WIDTH), F32)] + c_out_shape,
        scratch_shapes=c_scratch + [pltpu.VMEM((2, t, 1), F32), pltpu.VMEM((t, LANES), F32)],
        compiler_params=_cparams(),
    )(qkv, qkv, qkv, *c_ins)


def _sb_bwd(qkv, o, do, s, comm=None):
    t = SB_BLOCK
    nq = s // t
    grid = (SB_HEADS // 2, nq)
    c_in_specs, c_out_specs, c_out_shape, c_scratch, c_ins, split = _host(comm, 5, 3)

    def body(*refs):
        ((q_ref, k_ref, v_ref, o_ref, do_ref), (dq_ref, dk_ref, dv_ref),
         (l_ref, e_ref, dq_acc, dk_acc, dv_acc), riding) = split(refs)
        i = pl.program_id(1)
        if comm is not None:
            first_step, last_step = _grid_ends(grid)
            pl.when(first_step)(lambda: comm.start(*riding))
        first, hmask, row, col = _sb_masks()
        after = jnp.where(row > col, 1.0, 0.0).astype(BF16)
        from_here = jnp.where(row >= col, 1.0, 0.0).astype(BF16)
        causal = col < row

        @pl.when(i == 0)
        def _():
            dk_acc[...] = jnp.zeros_like(dk_acc)
            dv_acc[...] = jnp.zeros_like(dv_acc)

        q = q_ref[...]
        do_ = do_ref[...]
        qh = (q * hmask[0], q * hmask[1])
        doh = (do_ * hmask[0], do_ * hmask[1])
        prod = do_.astype(F32) * o_ref[...]
        total = (jnp.sum(jnp.where(first, prod, 0.0), axis=1, keepdims=True),
                 jnp.sum(jnp.where(first, 0.0, prod), axis=1, keepdims=True))
        l_ref[...] = jnp.zeros_like(l_ref)
        e_ref[...] = jnp.zeros_like(e_ref)
        dq_acc[...] = jnp.zeros_like(dq_acc)

        def blocks(todo):
            chains = [(b, h) for b in range(len(todo)) for h in range(2)]
            starts = [pl.multiple_of(kb * t, t) for kb, _ in todo]
            ks = [k_ref[pl.ds(st, t), :] for st in starts]
            vs = [v_ref[pl.ds(st, t), :] for st in starts]
            ar = {(b, h): _sb_logits(qh[h], ks[b], todo[b][1]) for b, h in chains}
            dw = {(b, h): _dot(doh[h], vs[b], 1, 1) for b, h in chains}
            later = {bh: _split_mm(ar[bh][1], after) for bh in chains}
            carry = [l_ref[0], l_ref[1]]
            wb = {}
            for b, (_, mask) in enumerate(todo):
                for h in range(2):
                    wbh = jnp.exp(ar[b, h][0] + later[b, h] + carry[h])
                    wb[b, h] = (wbh if mask is None else jnp.where(mask, wbh, 0.0)).astype(BF16)
                carry = [carry[h] + jnp.sum(ar[b, h][1], axis=1, keepdims=True) for h in range(2)]
            dvs = {(b, h): _dot(wb[b, h], do_, 0, 0) for b, h in chains}
            e = {bh: dw[bh] * wb[bh].astype(F32) for bh in chains}
            suffix = {bh: _split_mm(e[bh], from_here) for bh in chains}
            e_carry = [e_ref[0], e_ref[1]]
            dz = {}
            for b, (_, mask) in enumerate(todo):
                for h in range(2):
                    before = total[h] - (suffix[b, h] + e_carry[h])
                    dzh = e[b, h] - jnp.exp(ar[b, h][0]) * (e[b, h] + before)
                    dz[b, h] = (dzh if mask is None else jnp.where(mask, dzh, 0.0)).astype(BF16)
                e_carry = [e_carry[h] + jnp.sum(e[b, h], axis=1, keepdims=True) for h in range(2)]
            dqs = {(b, h): _dot(dz[b, h], ks[b], 1, 0) for b, h in chains}
            dks = {(b, h): _dot(dz[b, h], q, 0, 0) for b, h in chains}
            dq = dq_acc[...]
            for b, st in enumerate(starts):
                dq = dq + jnp.where(first, dqs[b, 0], dqs[b, 1])
                dk_acc[pl.ds(st, t), :] += jnp.where(first, dks[b, 0], dks[b, 1])
                dv_acc[pl.ds(st, t), :] += jnp.where(first, dvs[b, 0], dvs[b, 1])
            dq_acc[...] = dq
            l_ref[0], l_ref[1] = carry
            e_ref[0], e_ref[1] = e_carry

        _sb_walk(i, blocks, l_ref, causal)
        dq_ref[...] = (dq_acc[...] * SB_SCALE).astype(dq_ref.dtype)

        @pl.when(i == nq - 1)
        def _():
            dk_ref[...] = dk_acc[...].astype(dk_ref.dtype)
            dv_ref[...] = dv_acc[...].astype(dv_ref.dtype)

        if comm is not None:
            pl.when(last_step)(lambda: comm.finish(*riding))

    nk_off = SB_WIDTH // LANES
    blk = pl.BlockSpec((t, LANES), lambda p, i: (i, p))
    col_blk = pl.BlockSpec((s, LANES), lambda p, i: (0, p))
    sds = jax.ShapeDtypeStruct((s, SB_WIDTH), BF16)
    return pl.pallas_call(
        body, name="sb_bwd", grid=grid,
        in_specs=[blk,
                  pl.BlockSpec((s, LANES), lambda p, i: (0, nk_off + p)),
                  pl.BlockSpec((s, LANES), lambda p, i: (0, 2 * nk_off + p)),
                  blk, blk] + c_in_specs,
        out_specs=[blk, col_blk, col_blk] + c_out_specs,
        out_shape=[sds, sds, sds] + c_out_shape,
        scratch_shapes=c_scratch + [pltpu.VMEM((2, t, 1), F32), pltpu.VMEM((2, t, 1), F32),
                                    pltpu.VMEM((t, LANES), F32), pltpu.VMEM((s, LANES), F32),
                                    pltpu.VMEM((s, LANES), F32)],
        compiler_params=_cparams(),
    )(qkv, qkv, qkv, o, do, *c_ins)


def _ret_log_gamma():
    lg = np.log1p(-np.exp2(-5.0 - np.arange(RET_HEADS, dtype=np.float32))).astype(np.float32)
    return jnp.asarray(np.broadcast_to(lg[:, None, None], (RET_HEADS, 8, LANES)).copy())


RET_SCRATCH = [pltpu.VMEM((RET_HEADS, RET_QK, RET_V), F32),
               pltpu.VMEM((RET_HEADS, RET_BLOCK, RET_BLOCK), F32),
               pltpu.VMEM((RET_HEADS, RET_BLOCK, 1), F32),
               pltpu.VMEM((RET_HEADS, RET_BLOCK, 1), F32)]


def _ret_begin(n, lg_ref, state, within, q_dec, k_dec):
    @pl.when(n == 0)
    def _():
        c = RET_BLOCK
        state[...] = jnp.zeros_like(state)
        row = lax.broadcasted_iota(jnp.int32, (c, c), 0)
        col = lax.broadcasted_iota(jnp.int32, (c, c), 1)
        rel = jnp.maximum(row - col, 0).astype(F32)
        idx = lax.broadcasted_iota(jnp.int32, (c, 1), 0).astype(F32)
        for h in range(RET_HEADS):
            lg = lg_ref[h, 0:1, 0:1]
            within[h] = jnp.where(row >= col, jnp.exp(lg * rel), 0.0)
            q_dec[h] = jnp.exp(lg * (idx + 1.0))
            k_dec[h] = jnp.exp(lg * (c - 1.0 - idx))


def _chunk_decay(lg_ref, h):
    return jnp.exp(lg_ref[h, 0:1, 0:1] * float(RET_BLOCK))


def _ret_heads(x, width):
    return [x[:, h * width:(h + 1) * width] for h in range(RET_HEADS)]


def _ret_specs(s, reverse=False):
    c = RET_BLOCK
    nc = s // c
    pos = (lambda n: nc - 1 - n) if reverse else (lambda n: n)
    q_spec = pl.BlockSpec((c, RET_QK_WIDTH), lambda n: (pos(n), 0))
    k_spec = pl.BlockSpec((c, RET_QK_WIDTH), lambda n: (pos(n), 1))
    v_spec = pl.BlockSpec((c, RET_V_WIDTH), lambda n: (pos(n), 0))
    lg_spec = pl.BlockSpec((RET_HEADS, 8, LANES), lambda n: (0, 0, 0))
    rope_spec = pl.BlockSpec((c, RET_QK), lambda n: (pos(n), 0))
    return nc, q_spec, k_spec, v_spec, lg_spec, rope_spec


def _ret_fwd(rqk, rvg, s):
    nc, q_spec, k_spec, v_spec, lg_spec, _ = _ret_specs(s)
    g_spec = pl.BlockSpec((RET_BLOCK, RET_V_WIDTH), lambda n: (n, 1))
    heads = range(RET_HEADS)

    def body(q_ref, k_ref, v_ref, g_ref, lg_ref, r_ref, y_ref, state, within, q_dec, k_dec):
        n = pl.program_id(0)
        _ret_begin(n, lg_ref, state, within, q_dec, k_dec)
        q, k = _ret_heads(q_ref[...], RET_QK), _ret_heads(k_ref[...], RET_QK)
        v, g = _ret_heads(v_ref[...], RET_V), _ret_heads(g_ref[...], RET_V)
        scores = [_dot(q[h].astype(BF16), k[h].astype(BF16), 1, 1) * within[h] for h in heads]
        cross = [_dot((q[h] * q_dec[h]).astype(BF16), state[h].astype(BF16), 1, 0) for h in heads]
        out = [_dot(scores[h].astype(BF16), v[h], 1, 0) + cross[h] for h in heads]
        grown = [_dot((k[h] * k_dec[h]).astype(BF16), v[h], 0, 0) for h in heads]
        for h in heads:
            sl = slice(h * RET_V, (h + 1) * RET_V)
            r_ref[:, sl] = out[h]
            xhat, _ = _norm(out[h])
            gh = g[h].astype(F32)
            y_ref[:, sl] = (gh * _sigmoid(gh) * xhat).astype(y_ref.dtype)
            state[h] = state[h] * _chunk_decay(lg_ref, h) + grown[h]

    return pl.pallas_call(
        body, name="ret_fwd", grid=(nc,),
        in_specs=[q_spec, k_spec, v_spec, g_spec, lg_spec],
        out_specs=[v_spec, v_spec],
        out_shape=[jax.ShapeDtypeStruct((s, RET_V_WIDTH), F32), jax.ShapeDtypeStruct((s, RET_V_WIDTH), BF16)],
        scratch_shapes=RET_SCRATCH,
        compiler_params=_cparams(),
    )(rqk, rqk, rvg, rvg, _ret_log_gamma())


def _rope_bwd(d, cos, sin):
    return d * cos + _swap_halves(d * sin)


def _ret_bwd_q(rqk, rv, d_out, cos2, sin2, s):
    nc, q_spec, k_spec, v_spec, lg_spec, rope_spec = _ret_specs(s)
    heads = range(RET_HEADS)

    def body(k_ref, v_ref, d_ref, lg_ref, cos_ref, sin_ref, dq_ref, state, within, q_dec, k_dec):
        n = pl.program_id(0)
        _ret_begin(n, lg_ref, state, within, q_dec, k_dec)
        k = _ret_heads(k_ref[...], RET_QK)
        v, d = _ret_heads(v_ref[...], RET_V), _ret_heads(d_ref[...], RET_V)
        cos, sin = cos_ref[...], sin_ref[...]
        d_scores = [_dot(d[h], v[h], 1, 1) * within[h] for h in heads]
        cross = [q_dec[h] * _dot(d[h], state[h].astype(BF16), 1, 1) for h in heads]
        dq = [_dot(d_scores[h].astype(BF16), k[h].astype(BF16), 1, 0) + cross[h] for h in heads]
        grown = [_dot((k[h] * k_dec[h]).astype(BF16), v[h], 0, 0) for h in heads]
        for h in heads:
            sl = slice(h * RET_QK, (h + 1) * RET_QK)
            dq_ref[:, sl] = (_rope_bwd(dq[h], cos, sin) * RET_SCALE).astype(dq_ref.dtype)
            state[h] = state[h] * _chunk_decay(lg_ref, h) + grown[h]

    return pl.pallas_call(
        body, name="ret_bwd_q", grid=(nc,),
        in_specs=[k_spec, v_spec, v_spec, lg_spec, rope_spec, rope_spec],
        out_specs=q_spec,
        out_shape=jax.ShapeDtypeStruct((s, RET_QK_WIDTH), BF16),
        scratch_shapes=RET_SCRATCH,
        compiler_params=_cparams(),
    )(rqk, rv, d_out, _ret_log_gamma(), cos2, sin2)


def _ret_bwd_kv(rqk, rv, d_out, cos2, sin2, s):
    nc, q_spec, k_spec, v_spec, lg_spec, rope_spec = _ret_specs(s, reverse=True)
    heads = range(RET_HEADS)

    def body(q_ref, k_ref, v_ref, d_ref, lg_ref, cos_ref, sin_ref, dk_ref, dv_ref, state, within, q_dec, k_dec):
        n = pl.program_id(0)
        _ret_begin(n, lg_ref, state, within, q_dec, k_dec)
        q, k = _ret_heads(q_ref[...], RET_QK), _ret_heads(k_ref[...], RET_QK)
        v, d = _ret_heads(v_ref[...], RET_V), _ret_heads(d_ref[...], RET_V)
        cos, sin = cos_ref[...], sin_ref[...]
        qb, kb = [q[h].astype(BF16) for h in heads], [k[h].astype(BF16) for h in heads]
        st = [state[h].astype(BF16) for h in heads]
        scores = [_dot(qb[h], kb[h], 1, 1) * within[h] for h in heads]
        d_scores = [_dot(d[h], v[h], 1, 1) * within[h] for h in heads]
        dk = [_dot(d_scores[h].astype(BF16), qb[h], 0, 0) + k_dec[h] * _dot(v[h], st[h], 1, 1) for h in heads]
        dv = [_dot(scores[h].astype(BF16), d[h], 0, 0) + k_dec[h] * _dot(kb[h], st[h], 1, 0) for h in heads]
        grown = [_dot((q[h] * q_dec[h]).astype(BF16), d[h], 0, 0) for h in heads]
        for h in heads:
            dk_ref[:, h * RET_QK:(h + 1) * RET_QK] = _rope_bwd(dk[h], cos, sin).astype(dk_ref.dtype)
            dv_ref[:, h * RET_V:(h + 1) * RET_V] = dv[h].astype(dv_ref.dtype)
            state[h] = state[h] * _chunk_decay(lg_ref, h) + grown[h]

    return pl.pallas_call(
        body, name="ret_bwd_kv", grid=(nc,),
        in_specs=[q_spec, k_spec, v_spec, v_spec, lg_spec, rope_spec, rope_spec],
        out_specs=[q_spec, v_spec],
        out_shape=[jax.ShapeDtypeStruct((s, RET_QK_WIDTH), BF16), jax.ShapeDtypeStruct((s, RET_V_WIDTH), BF16)],
        scratch_shapes=RET_SCRATCH,
        compiler_params=_cparams(),
    )(rqk, rqk, rv, d_out, _ret_log_gamma(), cos2, sin2)


def _xattn_probs(q, k):
    sc = _dot(q, k, 1, 1)
    sc = sc - jnp.max(sc, axis=-1, keepdims=True)
    p = jnp.exp(sc)
    return p / jnp.sum(p, axis=-1, keepdims=True)


def _xattn_fwd(qm, kv, s):
    tq = XATTN_ROWS

    def body(q_ref, kv_ref, o_ref):
        for h in range(MEM_HEADS):
            sl = slice(h * MEM_DIM, (h + 1) * MEM_DIM)
            sv = slice(D_MODEL + h * MEM_DIM, D_MODEL + (h + 1) * MEM_DIM)
            p = _xattn_probs(q_ref[:, sl], kv_ref[:, sl])
            o_ref[:, sl] = _dot(p.astype(BF16), kv_ref[:, sv], 1, 0).astype(o_ref.dtype)

    return pl.pallas_call(
        body, name="xattn_fwd", grid=(s // tq,),
        in_specs=[pl.BlockSpec((tq, D_MODEL), lambda i: (i, 0)),
                  pl.BlockSpec((MEM_LEN, 2 * D_MODEL), lambda i: (0, 0))],
        out_specs=pl.BlockSpec((tq, D_MODEL), lambda i: (i, 0)),
        out_shape=jax.ShapeDtypeStruct((s, D_MODEL), BF16),
        compiler_params=_cparams(),
    )(qm, kv)


def _xattn_bwd(qm, kv, do, s):
    tq = XATTN_ROWS

    def body(q_ref, kv_ref, do_ref, dq_ref, dkv_ref):
        i = pl.program_id(0)

        @pl.when(i == 0)
        def _():
            dkv_ref[...] = jnp.zeros_like(dkv_ref)

        for h in range(MEM_HEADS):
            sl = slice(h * MEM_DIM, (h + 1) * MEM_DIM)
            sv = slice(D_MODEL + h * MEM_DIM, D_MODEL + (h + 1) * MEM_DIM)
            q, k, v, d = q_ref[:, sl], kv_ref[:, sl], kv_ref[:, sv], do_ref[:, sl]
            p = _xattn_probs(q, k)
            dp = _dot(d, v, 1, 1)
            ds = (p * (dp - jnp.sum(p * dp, axis=-1, keepdims=True))).astype(BF16)
            dq_ref[:, sl] = (_dot(ds, k, 1, 0) * MEM_SCALE).astype(dq_ref.dtype)
            dkv_ref[:, sl] += _dot(ds, q, 0, 0)
            dkv_ref[:, sv] += _dot(p.astype(BF16), d, 0, 0)

    row_blk = pl.BlockSpec((tq, D_MODEL), lambda i: (i, 0))
    kv_blk = pl.BlockSpec((MEM_LEN, 2 * D_MODEL), lambda i: (0, 0))
    return pl.pallas_call(
        body, name="xattn_bwd", grid=(s // tq,),
        in_specs=[row_blk, kv_blk, row_blk],
        out_specs=[row_blk, kv_blk],
        out_shape=[jax.ShapeDtypeStruct((s, D_MODEL), BF16), jax.ShapeDtypeStruct((MEM_LEN, 2 * D_MODEL), F32)],
        compiler_params=_cparams(),
    )(qm, kv, do)


def _place():
    x, y, c = lax.axis_index("x"), lax.axis_index("y"), lax.axis_index("c")
    others = [(1 - x, y), (x, 1 - y), (1 - x, 1 - y)]
    return x, y, c, others


def _slab(ref, axis, chip, size):
    start = pl.multiple_of(chip * size, LANES if axis == 1 else 16)
    if axis == 0:
        return ref.at[pl.ds(start, size), :]
    return ref.at[:, pl.ds(start, size)]


class _CommPlan:
    def __init__(self, ins, out_shape, scratch, start, finish):
        self.ins, self.out_shape, self.scratch, self.start, self.finish = ins, out_shape, scratch, start, finish

    @property
    def specs(self):
        any_spec = pl.BlockSpec(memory_space=pl.ANY)
        return [any_spec] * len(self.ins), [any_spec] * len(self.out_shape)

    def split(self, refs):
        n_in, n_out = len(self.ins), len(self.out_shape)
        return refs[:n_in], refs[n_in:n_in + n_out], refs[n_in + n_out:]


def _run_plan(name, plan):
    def body(*refs):
        plan.start(*plan.split(refs))
        plan.finish(*plan.split(refs))

    in_specs, out_specs = plan.specs
    return pl.pallas_call(body, name=name, in_specs=in_specs, out_specs=out_specs, out_shape=plan.out_shape,
                          scratch_shapes=plan.scratch)(*plan.ins)


def _gather_plan(names, shards):
    spec = {name: (shape, axis) for name, shape, axis in BIG}
    nw = len(names)

    def shard_half(ref, c):
        rows = ref.shape[0] // 2
        return ref.at[pl.ds(pl.multiple_of(c * rows, 16), rows), :]

    def region(ref, w, chip, c):
        shape, axis = spec[names[w]]
        size = shape[axis] // N_CHIPS
        if axis == 0:
            rows = size // 2
            return ref.at[pl.ds(pl.multiple_of(chip * size + c * rows, 16), rows), :]
        rows = shape[0] // 2
        return ref.at[pl.ds(pl.multiple_of(c * rows, 16), rows), pl.ds(pl.multiple_of(chip * size, LANES), size)]

    def ops(shard, full, sems):
        ici_send, ici_recv, d2d_send, d2d_recv, local_sems = sems
        x, y, c, others = _place()
        mine, sibling = 2 * x + y, (x, y, 1 - c)
        local, over_ici, arrived, passed_on, from_sibling = [], [], [], [], []
        for w in range(nw):
            shape, axis = spec[names[w]]
            local.append(pltpu.make_async_copy(shard[w], _slab(full[w], axis, mine, shape[axis] // N_CHIPS),
                                               local_sems.at[w]))
            for t, (qx, qy) in enumerate(others):
                n, theirs = 3 * w + t, 2 * qx + qy
                over_ici.append(pltpu.make_async_remote_copy(
                    src_ref=shard_half(shard[w], c), dst_ref=region(full[w], w, mine, c),
                    send_sem=ici_send.at[n], recv_sem=ici_recv.at[n], device_id=(qx, qy, c), device_id_type=MESH))
                arrived.append(pltpu.make_async_remote_copy(
                    src_ref=shard_half(shard[w], c), dst_ref=region(full[w], w, theirs, c),
                    send_sem=ici_send.at[n], recv_sem=ici_recv.at[n], device_id=(qx, qy, c), device_id_type=MESH))
                passed_on.append(pltpu.make_async_remote_copy(
                    src_ref=region(full[w], w, theirs, c), dst_ref=region(full[w], w, theirs, c),
                    send_sem=d2d_send.at[n], recv_sem=d2d_recv.at[n], device_id=sibling, device_id_type=MESH))
                from_sibling.append(pltpu.make_async_remote_copy(
                    src_ref=region(full[w], w, theirs, c), dst_ref=region(full[w], w, theirs, 1 - c),
                    send_sem=d2d_send.at[n], recv_sem=d2d_recv.at[n], device_id=sibling, device_id_type=MESH))
        return local, over_ici, arrived, passed_on, from_sibling

    def start(shard, full, sems):
        local, over_ici, _, _, _ = ops(shard, full, sems)
        for cp in local + over_ici:
            cp.start()

    def finish(shard, full, sems):
        local, over_ici, arrived, passed_on, from_sibling = ops(shard, full, sems)
        for got, onward in zip(arrived, passed_on, strict=True):
            got.wait_recv()
            onward.start()
        for got in from_sibling:
            got.wait_recv()
        for cp in over_ici + passed_on:
            cp.wait_send()
        for cp in local:
            cp.wait()

    dma = pltpu.SemaphoreType.DMA
    return _CommPlan(
        ins=[shards[name] for name in names],
        out_shape=[jax.ShapeDtypeStruct(spec[name][0], BF16) for name in names],
        scratch=[dma((3 * nw,)), dma((3 * nw,)), dma((3 * nw,)), dma((3 * nw,)), dma((nw,))],
        start=start, finish=finish)


def _shard_shape(shape, axis):
    return tuple(d // N_CHIPS if a == axis else d for a, d in enumerate(shape))


def _exchange_plan(names, grads):
    spec = {name: (shape, axis) for name, shape, axis in BIG}
    nw = len(names)

    def ops(grad, stack, sems):
        send_sems, recv_sems, local_sems = sems
        x, y, c, others = _place()
        mine = 2 * x + y
        me, sibling = (x, y, c), (x, y, 1 - c)

        def dev(px, py, pc):
            return 4 * px + 2 * py + pc

        def copy(w, n, src, slot, to):
            return pltpu.make_async_remote_copy(
                src_ref=src, dst_ref=stack[w].at[slot], send_sem=send_sems.at[7 * w + n],
                recv_sem=recv_sems.at[7 * w + n], device_id=to, device_id_type=MESH)

        local, first, arrived, passed_on, from_sibling = [], [], [], [], []
        for w in range(nw):
            shape, axis = spec[names[w]]
            size = shape[axis] // N_CHIPS
            own = _slab(grad[w], axis, mine, size)
            local.append(pltpu.make_async_copy(own, stack[w].at[dev(*me)], local_sems.at[w]))
            first.append(copy(w, 0, own, dev(*me), sibling))
            from_sibling.append(copy(w, 0, own, dev(*sibling), me))
            for t, (qx, qy) in enumerate(others):
                got = stack[w].at[dev(qx, qy, c)]
                first.append(copy(w, 1 + t, _slab(grad[w], axis, 2 * qx + qy, size), dev(*me), (qx, qy, c)))
                arrived.append(copy(w, 1 + t, got, dev(qx, qy, c), me))
                passed_on.append(copy(w, 4 + t, got, dev(qx, qy, c), sibling))
                from_sibling.append(copy(w, 4 + t, got, dev(qx, qy, 1 - c), me))
        return local, first, arrived, passed_on, from_sibling

    def start(grad, stack, sems):
        local, first, _, _, _ = ops(grad, stack, sems)
        for cp in local + first:
            cp.start()

    def finish(grad, stack, sems):
        local, first, arrived, passed_on, from_sibling = ops(grad, stack, sems)
        for got, onward in zip(arrived, passed_on, strict=True):
            got.wait_recv()
            onward.start()
        for got in from_sibling:
            got.wait_recv()
        for cp in first + passed_on:
            cp.wait_send()
        for cp in local:
            cp.wait()

    dma = pltpu.SemaphoreType.DMA
    return _CommPlan(
        ins=[grads[name] for name in names],
        out_shape=[jax.ShapeDtypeStruct((N_DEV,) + _shard_shape(*spec[name]), BF16) for name in names],
        scratch=[dma((7 * nw,)), dma((7 * nw,)), dma((nw,))],
        start=start, finish=finish)


def _adamw(w, g, m, v):
    m = ADAM_B1 * m + (1.0 - ADAM_B1) * g
    v = ADAM_B2 * v + (1.0 - ADAM_B2) * (g * g)
    m_hat = m / (1.0 - ADAM_B1 ** ADAM_STEP)
    v_hat = v / (1.0 - ADAM_B2 ** ADAM_STEP)
    delta = -ADAM_LR * (m_hat / (jnp.sqrt(v_hat) + ADAM_EPS) + ADAM_WD * w)
    return delta, m, v


def _reduce_adamw(name, stack, w, m, v):
    rows, cols = w.shape
    tr = next(t for t in (256, 128, 64) if rows % t == 0)

    def body(s_ref, w_ref, m_ref, v_ref, g_ref, d_ref, nm_ref, nv_ref):
        g = s_ref[0].astype(F32)
        for d in range(1, N_DEV):
            g = g + s_ref[d].astype(F32)
        g_ref[...] = g
        d_ref[...], nm_ref[...], nv_ref[...] = _adamw(w_ref[...], g, m_ref[...], v_ref[...])

    blk = pl.BlockSpec((tr, cols), lambda i: (i, 0))
    return pl.pallas_call(
        body, name=name, grid=(rows // tr,),
        in_specs=[pl.BlockSpec((N_DEV, tr, cols), lambda i: (0, i, 0)), blk, blk, blk],
        out_specs=[blk] * 4, out_shape=[jax.ShapeDtypeStruct((rows, cols), F32)] * 4,
        compiler_params=_cparams(),
    )(stack, w, m, v)


def _small_step(pack, w, m, v):
    def body(p_ref, w_ref, m_ref, v_ref, g_ref, d_ref, nm_ref, nv_ref, loss_ref, all_ref, send_sems, recv_sems):
        x, y, c, _ = _place()
        me = 4 * x + 2 * y + c
        all_ref[me] = p_ref[...]
        sent = []
        for n in range(1, N_DEV):
            peer = me ^ n
            cp = pltpu.make_async_remote_copy(
                src_ref=p_ref, dst_ref=all_ref.at[me], send_sem=send_sems.at[n - 1], recv_sem=recv_sems.at[n - 1],
                device_id=(peer // 4, (peer // 2) % 2, peer % 2), device_id_type=MESH)
            cp.start()
            sent.append(cp)
        for n in range(1, N_DEV):
            peer = me ^ n
            pltpu.make_async_remote_copy(
                src_ref=p_ref, dst_ref=all_ref.at[peer], send_sem=send_sems.at[n - 1], recv_sem=recv_sems.at[n - 1],
                device_id=(peer // 4, (peer // 2) % 2, peer % 2), device_id_type=MESH).wait_recv()
        for cp in sent:
            cp.wait_send()
        tot = all_ref[0]
        for d in range(1, N_DEV):
            tot = tot + all_ref[d]
        g = tot[:SMALL_ROWS]
        g_ref[...] = g
        d_ref[...], nm_ref[...], nv_ref[...] = _adamw(w_ref[...], g, m_ref[...], v_ref[...])
        loss_ref[...] = jnp.sum(jnp.sum(tot[SMALL_ROWS:], axis=1, keepdims=True), axis=0, keepdims=True)

    vm = pl.BlockSpec(memory_space=pltpu.VMEM)
    small = jax.ShapeDtypeStruct((SMALL_ROWS, LANES), F32)
    return pl.pallas_call(
        body, name="small_step",
        in_specs=[vm] * 4, out_specs=[vm] * 5,
        out_shape=[small] * 4 + [jax.ShapeDtypeStruct((1, 1), F32)],
        scratch_shapes=[pltpu.VMEM((N_DEV, PACK_ROWS, LANES), F32),
                        pltpu.SemaphoreType.DMA((N_DEV - 1,)), pltpu.SemaphoreType.DMA((N_DEV - 1,))],
    )(pack, w, m, v)


LATER_WEIGHTS = tuple(name for name, _, _ in BIG if name != "w_in")


def _layer_step(x, mem, tgt, w_in, shards, vec):
    s = x.shape[0]
    d = D_MODEL
    tm = min(ROW_TILE, s)
    tl = min(WIDE_TILE, s)
    cos2, sin2 = _rope_tables(s)
    xb = _cast_bf16("cast_x", x)
    bf = lambda w: ((s, w), BF16)
    f32 = lambda w: ((s, w), F32)

    (sb_qkv,) = _mm(
        "in_sb", xb, w_in, s, 3 * SB_WIDTH, d, tm=tl, tn=512, tk=d,
        epi=lambda acc, t, i, j: ([acc * jnp.where(j == 0, SB_SCALE, 1.0)], []),
        outs=[(*bf(3 * SB_WIDTH), *_tile(tl, 512))])

    def rope_epi(acc, t, i, j):
        cos, sin = t
        scale = jnp.where(j == 0, RET_SCALE, 1.0)
        parts = []
        for g in range(512 // RET_QK):
            xg = acc[:, g * RET_QK:(g + 1) * RET_QK]
            parts.append((xg * cos + _swap_halves(xg) * sin) * scale)
        return [jnp.concatenate(parts, axis=1)], []

    rope_in = ((tl, RET_QK), lambda i, j: (i, 0))
    (rqk,) = _mm("in_rqk", xb, w_in, s, 2 * RET_QK_WIDTH, d, tm=tl, tn=512, tk=d, b_off=(0, OFF_RET_Q // 512),
                 epi=rope_epi, ins=[(cos2, *rope_in), (sin2, *rope_in)],
                 outs=[(*f32(2 * RET_QK_WIDTH), *_tile(tl, 512))])
    (rvg,) = _mm("in_rvg", xb, w_in, s, 2 * RET_V_WIDTH, d, tm=tl, tn=512, tk=d, b_off=(0, OFF_RET_V // 512),
                 epi=_plain, outs=[(*bf(2 * RET_V_WIDTH), *_tile(tl, 512))])
    (gates,) = _mm("in_gate", xb, w_in, s, 2 * d, d, tm=tl, tn=512, tk=d, b_off=(0, OFF_GATE // 512),
                   epi=lambda acc, t, i, j: ([_sigmoid(acc + t[0])], []),
                   ins=[(vec["b_gate"], *_rowvec(512))], outs=[(*bf(2 * d), *_tile(tl, 512))])

    sb_out, sb_out_f32, *gathered = _sb_fwd(sb_qkv, s, comm=_gather_plan(LATER_WEIGHTS, shards))
    wt = dict(zip(LATER_WEIGHTS, gathered, strict=True))
    ret, gated = _ret_fwd(rqk, rvg, s)
    (y_sb,) = _mm("sb_o", sb_out, wt["w_sb_o"], s, d, SB_WIDTH, tm=tl, tn=d, tk=SB_WIDTH, epi=_plain,
                  outs=[(*bf(d), *_tile(tl, d))])
    y_ret, mixin = _mm(
        "ret_o", gated, wt["w_ret_o"], s, d, RET_V_WIDTH, tm=tl, tn=d, tk=RET_V_WIDTH,
        epi=lambda acc, t, i, j: ([acc, t[0].astype(F32) * t[2].astype(F32) + t[1].astype(F32) * acc], []),
        ins=[(gates, *_tile(tl, d)), (gates, *_tile(tl, d, 1)), (y_sb, *_tile(tl, d))],
        outs=[(*bf(d), *_tile(tl, d)), (*bf(d), *_tile(tl, d))])

    def ln_epi(acc, t, i, j):
        res, g, b = t
        xhat, rstd = _norm(DN_ALPHA * res + acc)
        y = xhat * g + b
        return [y, y, xhat, rstd], []

    full = _tile(tm, d)
    col1 = ((tm, 1), lambda i, j: (i, 0))
    ln_outs = [(*f32(d), *full), (*bf(d), *full), (*f32(d), *full), ((s, 1), F32, *col1)]
    x1, x1b, xhat1, rstd1 = _mm(
        "mix_o", mixin, wt["w_mix_o"], s, d, d, tm=tm, tn=d, tk=d, epi=ln_epi,
        ins=[(x, *full), (vec["ln1_g"], *_rowvec(d)), (vec["ln1_b"], *_rowvec(d))], outs=ln_outs)

    (qm,) = _mm("mem_q", x1b, wt["w_mem_q"], s, d, d, tm=tl, tn=d, tk=d,
                epi=lambda acc, t, i, j: ([acc * MEM_SCALE], []), outs=[(*bf(d), *_tile(tl, d))])
    (kv,) = _mm("mem_kv", mem, wt["w_mem_kv"], MEM_LEN, 2 * d, d, tm=MEM_LEN, tn=d, tk=d, epi=_plain,
                outs=[((MEM_LEN, 2 * d), BF16, *_tile(MEM_LEN, d))])
    att = _xattn_fwd(qm, kv, s)
    x2, x2b, xhat2, rstd2 = _mm(
        "mem_o", att, wt["w_mem_o"], s, d, d, tm=tm, tn=d, tk=d, epi=ln_epi,
        ins=[(x1, *full), (vec["ln2_g"], *_rowvec(d)), (vec["ln2_b"], *_rowvec(d))], outs=ln_outs)

    fh = FFN_HIDDEN
    tf = fh // 2
    (f1,) = _mm("ffn_in1", x2b, wt["w_ffn_in"], s, fh, d, tm=tl, tn=tf, tk=d, epi=_plain, j_outer=True,
                outs=[(*bf(fh), *_tile(tl, tf))])

    def swiglu_epi(acc, t, i, j):
        a = t[0].astype(F32)
        return [acc, a * _sigmoid(a) * acc], []

    f2, act = _mm(
        "ffn_in2", x2b, wt["w_ffn_in"], s, fh, d, tm=tm, tn=tf, tk=d, b_off=(0, 2), epi=swiglu_epi, j_outer=True,
        ins=[(f1, *_tile(tm, tf))], outs=[(*bf(fh), *_tile(tm, tf)), (*bf(fh), *_tile(tm, tf))])

    def head_epi(acc, t, i, j):
        res, g, b, target = t
        xhat, rstd = _norm(DN_ALPHA * res + acc)
        err = xhat * g + b - target
        dy = err * (1.0 / d)
        du = _norm_bwd(dy * g, xhat, rstd)
        return [du, du], [_colsum(dy * xhat), _colsum(dy), _colsum(err * err) * (0.5 / d)]

    vec_acc = ((1, d), F32)
    du3, du3b, dg3, db3, loss_cols = _mm(
        "ffn_out", act, wt["w_ffn_out"], s, d, fh, tm=tm, tn=d, tk=fh, epi=head_epi,
        ins=[(x2, *full), (vec["ln3_g"], *_rowvec(d)), (vec["ln3_b"], *_rowvec(d)), (tgt, *full)],
        outs=[(*f32(d), *full), (*bf(d), *full)], accs=[vec_acc] * 3)

    grads = {}
    ts = min(SEQ_TILE, s)

    def wgrad(name, a, b, m, n, tm_, tn_, tk_=None):
        (g,) = _mm(name, a, b, m, n, a.shape[0], tm=tm_, tn=tn_, tk=tk_ or ts, ta=True, epi=_plain,
                   outs=[((m, n), BF16, *_tile(tm_, tn_))])
        return g

    def ffn_bwd_epi(acc, t, i, j):
        a, b = t[0].astype(F32), t[1].astype(F32)
        sg = _sigmoid(a)
        return [acc * b * (sg * (1.0 + a * (1.0 - sg))), acc * (a * sg)], []

    df1, df2 = _mm(
        "ffn_out_t", du3b, wt["w_ffn_out"], s, fh, d, tm=tm, tn=tf, tk=d, tb=True, epi=ffn_bwd_epi, j_outer=True,
        ins=[(f1, *_tile(tm, tf)), (f2, *_tile(tm, tf))],
        outs=[(*bf(fh), *_tile(tm, tf)), (*bf(fh), *_tile(tm, tf))])
    grads["w_ffn_out"] = wgrad("g_ffn_out", act, du3b, fh, d, tf, d)
    grads["w_ffn_in"] = jnp.concatenate(
        [wgrad("g_ffn_in1", x2b, df1, d, fh, d, tf), wgrad("g_ffn_in2", x2b, df2, d, fh, d, tf)], axis=1)
    (dx2a,) = _mm("ffn_in1_t", df1, wt["w_ffn_in"], s, d, fh, tm=tm, tn=d, tk=fh, tb=True, epi=_plain,
                  outs=[(*f32(d), *full)])

    def ln_bwd(name, a, b, k, tk, b_off, more, scales, xhat, rstd, g):
        def epi(acc, t, i, j):
            *extra, xh, rs, gg = t
            dy = acc
            for e, sc in zip(extra, scales, strict=True):
                dy = dy + e * sc
            du = _norm_bwd(dy * gg, xh, rs)
            return [du, du], [_colsum(dy * xh), _colsum(dy)]

        return _mm(name, a, b, s, d, k, tm=tm, tn=d, tk=tk, tb=True, b_off=b_off, epi=epi,
                   ins=[(e, *full) for e in more] + [(xhat, *full), (rstd, *col1), (g, *_rowvec(d))],
                   outs=[(*f32(d), *full), (*bf(d), *full)], accs=[vec_acc] * 2)

    du2, du2b, dg2, db2 = ln_bwd("ffn_in2_t", df2, wt["w_ffn_in"], fh, fh, (0, 1), [dx2a, du3], [1.0, DN_ALPHA],
                                 xhat2, rstd2, vec["ln2_g"])

    (datt,) = _mm("mem_o_t", du2b, wt["w_mem_o"], s, d, d, tm=tl, tn=d, tk=d, tb=True, epi=_plain,
                  outs=[(*bf(d), *_tile(tl, d))])
    grads["w_mem_o"] = wgrad("g_mem_o", att, du2b, d, d, d, d)
    dqm, dkv = _xattn_bwd(qm, kv, datt, s)
    grads["w_mem_q"] = wgrad("g_mem_q", x1b, dqm, d, d, d, d)
    grads["w_mem_kv"] = wgrad("g_mem_kv", mem, dkv, d, 2 * d, d, d, MEM_LEN)
    du1, du1b, dg1, db1 = ln_bwd("mem_q_t", dqm, wt["w_mem_q"], d, d, (0, 0), [du2], [DN_ALPHA],
                                 xhat1, rstd1, vec["ln1_g"])

    def merge_bwd_epi(acc, t, i, j):
        g0, g1, ysb, yret = (v.astype(F32) for v in t)
        dgate0 = acc * ysb * (g0 * (1.0 - g0))
        dgate1 = acc * yret * (g1 * (1.0 - g1))
        return [dgate0, dgate1, acc * g0, acc * g1], [_colsum(dgate0), _colsum(dgate1)]

    dgate0, dgate1, dy_sb, dy_ret, dbg0, dbg1 = _mm(
        "mix_o_t", du1b, wt["w_mix_o"], s, d, d, tm=tm, tn=d, tk=d, tb=True, epi=merge_bwd_epi,
        ins=[(gates, *full), (gates, *_tile(tm, d, 1)), (y_sb, *full), (y_ret, *full)],
        outs=[(*bf(d), *full)] * 4, accs=[vec_acc] * 2)
    grads["w_mix_o"] = wgrad("g_mix_o", mixin, du1b, d, d, d, d)
    grads["w_sb_o"] = wgrad("g_sb_o", sb_out, dy_sb, SB_WIDTH, d, SB_WIDTH, d)
    grads["w_ret_o"] = wgrad("g_ret_o", gated, dy_ret, RET_V_WIDTH, d, RET_V_WIDTH, d)
    (dsb_out,) = _mm("sb_o_t", dy_sb, wt["w_sb_o"], s, SB_WIDTH, d, tm=tl, tn=SB_WIDTH, tk=d, tb=True, epi=_plain,
                     outs=[(*bf(SB_WIDTH), *_tile(tl, SB_WIDTH))])

    def gate_norm_bwd_epi(acc, t, i, j):
        r, g = t[0], t[1].astype(F32)
        drg, dret = [], []
        for h in range(d // RET_V):
            sl = slice(h * RET_V, (h + 1) * RET_V)
            xhat, rstd = _norm(r[:, sl])
            gg, dd = g[:, sl], acc[:, sl]
            sg = _sigmoid(gg)
            drg.append(dd * xhat * (sg * (1.0 + gg * (1.0 - sg))))
            dret.append(_norm_bwd(dd * (gg * sg), xhat, rstd))
        return [jnp.concatenate(drg, axis=1), jnp.concatenate(dret, axis=1)], []

    drg, dret = _mm(
        "ret_o_t", dy_ret, wt["w_ret_o"], s, RET_V_WIDTH, d, tm=tm, tn=d, tk=d, tb=True, epi=gate_norm_bwd_epi,
        ins=[(ret, *full), (rvg, *_tile(tm, d, 1))],
        outs=[(*bf(RET_V_WIDTH), *full)] * 2)

    drq = _ret_bwd_q(rqk, rvg, dret, cos2, sin2, s)
    drk, drv = _ret_bwd_kv(rqk, rvg, dret, cos2, sin2, s)
    dsq, dsk, dsv, *stacked = _sb_bwd(sb_qkv, sb_out_f32, dsb_out, s, comm=_exchange_plan(LATER_WEIGHTS, grads))
    stacks = dict(zip(LATER_WEIGHTS, stacked, strict=True))

    dh = jnp.concatenate([dsq, dsk, dsv, drq, drk, drv, drg, dgate0, dgate1], axis=1)
    grads["w_in"] = wgrad("g_in", xb, dh, d, IN_WIDTH, d, IN_WIDTH // N_CHIPS)
    grad_x, stacks["w_in"] = _mm(
        "in_t", dh, w_in, s, d, IN_WIDTH, tm=tl, tn=d, tk=IN_WIDTH // N_CHIPS, tb=True,
        epi=lambda acc, t, i, j: ([acc + DN_ALPHA * t[0]], []),
        ins=[(du1, *_tile(tl, d))], outs=[(*f32(d), *_tile(tl, d))], comm=_exchange_plan(("w_in",), grads))

    small = {"b_gate": jnp.concatenate([dbg0, dbg1], axis=1), "ln1_g": dg1, "ln1_b": db1, "ln2_g": dg2,
             "ln2_b": db2, "ln3_g": dg3, "ln3_b": db3}
    return grad_x, stacks, small, loss_cols


def kernel(x, mem, w_in, b_gate, w_sb_o, w_ret_o, w_mix_o, ln1_g, ln1_b, w_mem_q, w_mem_kv, w_mem_o, ln2_g, ln2_b, w_ffn_in, w_ffn_out, ln3_g, ln3_b, loss_target, m_w_in, m_b_gate, m_w_sb_o, m_w_ret_o, m_w_mix_o, m_ln1_g, m_ln1_b, m_w_mem_q, m_w_mem_kv, m_w_mem_o, m_ln2_g, m_ln2_b, m_w_ffn_in, m_w_ffn_out, m_ln3_g, m_ln3_b, v_w_in, v_b_gate, v_w_sb_o, v_w_ret_o, v_w_mix_o, v_ln1_g, v_ln1_b, v_w_mem_q, v_w_mem_kv, v_w_mem_o, v_ln2_g, v_ln2_b, v_w_ffn_in, v_w_ffn_out, v_ln3_g, v_ln3_b):
    given = dict(locals())
    s = x.shape[1]
    x2d = x.reshape(s, D_MODEL)
    tgt = loss_target.reshape(s, D_MODEL)
    mem2d = mem.reshape(MEM_LEN, D_MODEL)
    shard = {name: given[name].reshape(_shard_shape(shape, axis)) for name, shape, axis in BIG}
    vec = {name: given[name] for name in SMALL}

    shards_bf = {name: _cast_bf16("cast_" + name, shard[name]) for name, _, _ in BIG}
    (w_in_full,) = _run_plan("gather_w_in", _gather_plan(("w_in",), shards_bf))

    grad_x, stacks, small, loss_cols = _layer_step(x2d, mem2d, tgt, w_in_full, shards_bf, vec)

    out = {}
    for name, shape, axis in BIG:
        stack = stacks[name]
        shp = given[name].shape
        res = _reduce_adamw("adamw_" + name, stack, shard[name], given["m_" + name].reshape(stack.shape[1:]),
                            given["v_" + name].reshape(stack.shape[1:]))
        out[name] = [r.reshape(shp) for r in res]

    pack = jnp.concatenate([small[name] for name in SMALL] + [loss_cols], axis=1).reshape(PACK_ROWS, LANES)
    cat = lambda pre: jnp.concatenate([given[pre + name] for name in SMALL], axis=1).reshape(SMALL_ROWS, LANES)
    *res, loss = _small_step(pack, cat(""), cat("m_"), cat("v_"))
    flat = [r.reshape(1, SMALL_LEN) for r in res]
    off = 0
    for name in SMALL:
        n = given[name].shape[1]
        out[name] = [r[:, off:off + n] for r in flat]
        off += n

    return (loss.reshape(()), grad_x.reshape(x.shape),
            *[out[name][0] for name in WEIGHT_ORDER], *[out[name][1] for name in WEIGHT_ORDER],
            *[out[name][2] for name in WEIGHT_ORDER], *[out[name][3] for name in WEIGHT_ORDER])
```

```python
import functools

import jax
import jax.numpy as jnp
import numpy as np
from jax import lax
from jax.experimental import pallas as pl
from jax.experimental.pallas import tpu as pltpu

F32, BF16 = jnp.float32, jnp.bfloat16
MESH = pl.DeviceIdType.MESH

D_MODEL = 1024
MEM_LEN = 256
SB_HEADS, SB_DIM, SB_WIDTH = 8, 64, 512
RET_HEADS, RET_QK, RET_V = 4, 128, 256
RET_QK_WIDTH, RET_V_WIDTH = 512, 1024
ROPE_BASE = 10000.0
MEM_HEADS, MEM_DIM = 4, 256
FFN_HIDDEN = 2816
IN_WIDTH = 6656
OFF_RET_Q, OFF_RET_V, OFF_RET_G, OFF_GATE = 1536, 2560, 3584, 4608
DN_ALPHA = 2.0 ** 0.25
LN_EPS = 1e-5
SB_SCALE = SB_DIM ** -0.5
SB_DEAD = -110.0
RET_SCALE = RET_QK ** -0.5
MEM_SCALE = MEM_DIM ** -0.5
ADAM_LR, ADAM_B1, ADAM_B2, ADAM_EPS, ADAM_WD, ADAM_STEP = 0.001, 0.9, 0.999, 1e-08, 0.01, 10

N_DEV, N_CHIPS = 8, 4

LANES = 128
VMEM_LIMIT_BYTES = 52 * 2 ** 20
ROW_TILE = 512
WIDE_TILE = 1024
SEQ_TILE = 1024
SB_BLOCK = 256
RET_BLOCK = 256
XATTN_ROWS = 512

BIG = (
    ("w_in", (D_MODEL, IN_WIDTH), 1),
    ("w_sb_o", (SB_WIDTH, D_MODEL), 1),
    ("w_ret_o", (RET_V_WIDTH, D_MODEL), 0),
    ("w_mix_o", (D_MODEL, D_MODEL), 0),
    ("w_mem_q", (D_MODEL, D_MODEL), 0),
    ("w_mem_kv", (D_MODEL, 2 * D_MODEL), 1),
    ("w_mem_o", (D_MODEL, D_MODEL), 0),
    ("w_ffn_in", (D_MODEL, 2 * FFN_HIDDEN), 1),
    ("w_ffn_out", (FFN_HIDDEN, D_MODEL), 0),
)
SMALL = ("b_gate", "ln1_g", "ln1_b", "ln2_g", "ln2_b", "ln3_g", "ln3_b")
SMALL_LEN = 2 * D_MODEL + 6 * D_MODEL
SMALL_ROWS = SMALL_LEN // LANES
PACK_ROWS = SMALL_ROWS + D_MODEL // LANES
WEIGHT_ORDER = ("w_in", "b_gate", "w_sb_o", "w_ret_o", "w_mix_o", "ln1_g", "ln1_b", "w_mem_q", "w_mem_kv",
                "w_mem_o", "ln2_g", "ln2_b", "w_ffn_in", "w_ffn_out", "ln3_g", "ln3_b")


def _cparams():
    return pltpu.CompilerParams(vmem_limit_bytes=VMEM_LIMIT_BYTES)


def _dot(a, b, ca, cb):
    return lax.dot_general(a, b, (((ca,), (cb,)), ((), ())), preferred_element_type=F32)


def _sigmoid(x):
    return 1.0 / (1.0 + jnp.exp(-x))


def _mm(name, a, b, m, n, k, *, tm, tn, tk, epi, outs, ins=(), accs=(), ta=False, tb=False,
        a_off=(0, 0), b_off=(0, 0), j_outer=False, comm=None):
    assert m % tm == 0 and n % tn == 0 and k % tk == 0, (name, m, n, k, tm, tn, tk)
    ni, nj, nk = m // tm, n // tn, k // tk
    assert not accs or nj == 1, name
    ij = (lambda g0, g1: (g1, g0)) if j_outer else (lambda g0, g1: (g0, g1))

    def spec(block, index):
        return pl.BlockSpec(block, lambda g0, g1, kk: index(*ij(g0, g1), kk))

    if ta:
        a_spec = spec((tk, tm), lambda i, j, kk: (kk + a_off[0], i + a_off[1]))
    else:
        a_spec = spec((tm, tk), lambda i, j, kk: (i + a_off[0], kk + a_off[1]))
    if tb:
        b_spec = spec((tn, tk), lambda i, j, kk: (j + b_off[0], kk + b_off[1]))
    else:
        b_spec = spec((tk, tn), lambda i, j, kk: (kk + b_off[0], j + b_off[1]))
    in_specs = [a_spec, b_spec]
    for _, bs, im in ins:
        in_specs.append(spec(bs, lambda i, j, kk, im=im: im(i, j)))
    out_specs, out_shape = [], []
    for shape, dtype, bs, im in outs:
        out_specs.append(spec(bs, lambda i, j, kk, im=im: im(i, j)))
        out_shape.append(jax.ShapeDtypeStruct(shape, dtype))
    for shape, dtype in accs:
        out_specs.append(spec(shape, lambda i, j, kk, nd=len(shape): (0,) * nd))
        out_shape.append(jax.ShapeDtypeStruct(shape, dtype))
    n_in, n_out, n_acc = len(ins), len(outs), len(accs)
    ca, cb = (0 if ta else 1), (1 if tb else 0)
    grid = (*ij(ni, nj), nk)
    comm_ins, comm_outs, comm_scratch = [], [], []
    if comm is not None:
        comm_in_specs, comm_out_specs = comm.specs
        comm_ins, comm_outs, comm_scratch = list(comm.ins), list(comm.out_shape), list(comm.scratch)
        in_specs += comm_in_specs
        out_specs += comm_out_specs
        out_shape += comm_outs
    n_ci, n_co = len(comm_ins), len(comm_outs)

    def body(*refs):
        a_ref, b_ref = refs[:2]
        in_refs = refs[2:2 + n_in]
        ci_refs = refs[2 + n_in:2 + n_in + n_ci]
        rest = refs[2 + n_in + n_ci:]
        out_refs, acc_refs = rest[:n_out], rest[n_out:n_out + n_acc]
        co_refs = rest[n_out + n_acc:n_out + n_acc + n_co]
        scratch = rest[n_out + n_acc + n_co:]
        sem_refs, scratch = scratch[:len(comm_scratch)], scratch[len(comm_scratch):]
        (i, j), kk = ij(pl.program_id(0), pl.program_id(1)), pl.program_id(2)
        if comm is not None:
            first_step, last_step = _grid_ends(grid)
            pl.when(first_step)(lambda: comm.start(ci_refs, co_refs, sem_refs))
        part = _dot(a_ref[...].astype(BF16), b_ref[...].astype(BF16), ca, cb)

        def finish(acc):
            o_tiles, a_tiles = epi(acc, [r[...] for r in in_refs], i, j)
            for r, t in zip(out_refs, o_tiles, strict=True):
                r[...] = t.astype(r.dtype)
            if n_acc:
                @pl.when(i == 0)
                def _():
                    for r, t in zip(acc_refs, a_tiles, strict=True):
                        r[...] = t

                @pl.when(i > 0)
                def _():
                    for r, t in zip(acc_refs, a_tiles, strict=True):
                        r[...] += t

        if nk == 1:
            finish(part)
        else:
            acc_ref = scratch[0]

            @pl.when(kk == 0)
            def _():
                acc_ref[...] = part

            @pl.when(kk > 0)
            def _():
                acc_ref[...] += part

            @pl.when(kk == nk - 1)
            def _():
                finish(acc_ref[...])

        if comm is not None:
            pl.when(last_step)(lambda: comm.finish(ci_refs, co_refs, sem_refs))

    res = pl.pallas_call(
        body, name=name, grid=grid, in_specs=in_specs, out_specs=out_specs, out_shape=out_shape,
        scratch_shapes=comm_scratch + ([pltpu.VMEM((tm, tn), F32)] if nk > 1 else []),
        compiler_params=_cparams(),
    )(a, b, *[x for x, _, _ in ins], *comm_ins)
    return res


def _grid_ends(grid):
    ids = [pl.program_id(ax) for ax in range(len(grid))]
    first = functools.reduce(jnp.logical_and, [p == 0 for p in ids])
    last = functools.reduce(jnp.logical_and, [p == n - 1 for p, n in zip(ids, grid, strict=True)])
    return first, last


def _tile(tm, tn, dj=0):
    return (tm, tn), (lambda i, j: (i, j + dj))


def _rowvec(tn, dj=0):
    return (1, tn), (lambda i, j: (0, j + dj))


def _plain(acc, tiles, i, j):
    return [acc], []


def _ew(name, fn, ins, outs, rows, tr):
    assert rows % tr == 0, (name, rows, tr)
    in_specs = []
    for x in ins:
        if x.shape[0] == rows:
            in_specs.append(pl.BlockSpec((tr, x.shape[1]), lambda i: (i, 0)))
        else:
            in_specs.append(pl.BlockSpec(x.shape, lambda i: (0, 0)))
    n_in = len(ins)

    def body(*refs):
        res = fn(*[r[...] for r in refs[:n_in]])
        for r, t in zip(refs[n_in:], res, strict=True):
            r[...] = t.astype(r.dtype)

    return pl.pallas_call(
        body, name=name, grid=(rows // tr,), in_specs=in_specs,
        out_specs=[pl.BlockSpec((tr, w), lambda i: (i, 0)) for w, _ in outs],
        out_shape=[jax.ShapeDtypeStruct((rows, w), dt) for w, dt in outs],
        compiler_params=_cparams(),
    )(*ins)


def _cast_bf16(name, x):
    rows = x.shape[0]
    tr = next(t for t in (512, 256, 64) if rows % t == 0)
    return _ew(name, lambda v: (v,), [x], [(x.shape[1], BF16)], rows, tr)[0]


def _rope_tables(s):
    half = RET_QK // 2
    inv = 1.0 / (ROPE_BASE ** (jnp.arange(half, dtype=F32) / half))
    inv2 = jnp.concatenate([inv, inv]).reshape(1, RET_QK)
    sign = jnp.concatenate([-jnp.ones((half,), F32), jnp.ones((half,), F32)]).reshape(1, RET_QK)
    tr = ROW_TILE

    def body(inv_ref, sign_ref, cos_ref, sin_ref):
        i = pl.program_id(0)
        pos = (lax.broadcasted_iota(jnp.int32, (tr, RET_QK), 0) + i * tr).astype(F32)
        ang = pos * inv_ref[...]
        cos_ref[...] = jnp.cos(ang)
        sin_ref[...] = jnp.sin(ang) * sign_ref[...]

    vec = pl.BlockSpec((1, RET_QK), lambda i: (0, 0))
    blk = pl.BlockSpec((tr, RET_QK), lambda i: (i, 0))
    return pl.pallas_call(
        body, name="rope_tables", grid=(s // tr,), in_specs=[vec, vec], out_specs=[blk, blk],
        out_shape=[jax.ShapeDtypeStruct((s, RET_QK), F32)] * 2, compiler_params=_cparams(),
    )(inv2, sign)


def _swap_halves(x):
    return pltpu.roll(x, RET_QK // 2, 1)


def _norm(u):
    mu = jnp.mean(u, axis=-1, keepdims=True)
    d = u - mu
    var = jnp.mean(d * d, axis=-1, keepdims=True)
    rstd = lax.rsqrt(var + LN_EPS)
    return d * rstd, rstd


def _norm_bwd(dxh, xhat, rstd):
    m1 = jnp.mean(dxh, axis=-1, keepdims=True)
    m2 = jnp.mean(dxh * xhat, axis=-1, keepdims=True)
    return rstd * (dxh - m1 - xhat * m2)


def _colsum(t):
    return jnp.sum(t, axis=0, keepdims=True)


def _split_mm(t, tri):
    hi = t.astype(BF16)
    lo = (t - hi.astype(F32)).astype(BF16)
    return _dot(hi, tri, 1, 0) + _dot(lo, tri, 1, 0)


def _sb_masks():
    t = SB_BLOCK
    lane = lax.broadcasted_iota(jnp.int32, (1, LANES), 1)
    first = lane < SB_DIM
    m0 = jnp.where(first, 1.0, 0.0).astype(BF16)
    m1 = jnp.where(first, 0.0, 1.0).astype(BF16)
    row = lax.broadcasted_iota(jnp.int32, (t, t), 0)
    col = lax.broadcasted_iota(jnp.int32, (t, t), 1)
    return first, (m0, m1), row, col


def _sb_logits(qh, k, causal):
    z = _dot(qh, k, 1, 1)
    lp = jnp.log(1.0 + jnp.exp(-jnp.abs(z)))
    a = jnp.minimum(z, 0.0) - lp
    r = jnp.minimum(-z, 0.0) - lp
    if causal is not None:
        r = jnp.where(causal, r, 0.0)
    return a, r


def _sb_walk(i, blocks, l_ref, causal):
    pl.when(i == 0)(lambda: blocks([(i, causal)]))
    pl.when(i > 0)(lambda: blocks([(i, causal), (i - 1, None)]))

    def alive():
        top = jnp.max(jnp.maximum(l_ref[0], l_ref[1]))
        return jnp.where(top > SB_DEAD, 1, 0)

    def cond(c):
        return jnp.logical_and(c[0] < i, c[1] > 0)

    def step(c):
        blocks([(i - 1 - c[0], None)])
        return c[0] + 1, alive()

    lax.while_loop(cond, step, (jnp.int32(1), alive()))


def _host(comm, n_in, n_out):
    if comm is None:
        return [], [], [], [], [], lambda refs: (refs[:n_in], refs[n_in:n_in + n_out], refs[n_in + n_out:], None)
    in_specs, out_specs = comm.specs
    n_ci, n_co, n_sem = len(comm.ins), len(comm.out_shape), len(comm.scratch)

    def split(refs):
        ins, ci = refs[:n_in], refs[n_in:n_in + n_ci]
        rest = refs[n_in + n_ci:]
        outs, co = rest[:n_out], rest[n_out:n_out + n_co]
        sems, scratch = rest[n_out + n_co:n_out + n_co + n_sem], rest[n_out + n_co + n_sem:]
        return ins, outs, scratch, (ci, co, sems)

    return in_specs, out_specs, list(comm.out_shape), list(comm.scratch), list(comm.ins), split


def _sb_qkv_specs(s):
    pairs = SB_HEADS // 2
    return [pl.BlockSpec((None, SB_BLOCK, LANES), lambda p, i: (p, i, 0)),
            pl.BlockSpec((None, s, LANES), lambda p, i: (pairs + p, 0, 0)),
            pl.BlockSpec((None, s, LANES), lambda p, i: (2 * pairs + p, 0, 0))]


def _sb_fwd(qkv, s, comm=None):
    t = SB_BLOCK
    nq = s // t
    grid = (SB_HEADS // 2, nq)
    c_in_specs, c_out_specs, c_out_shape, c_scratch, c_ins, split = _host(comm, 3, 2)

    def body(*refs):
        (q_ref, k_ref, v_ref), (o_ref, of_ref), (l_ref, acc_ref), riding = split(refs)
        i = pl.program_id(1)
        if comm is not None:
            first_step, last_step = _grid_ends(grid)
            pl.when(first_step)(lambda: comm.start(*riding))
        first, hmask, row, col = _sb_masks()
        after = jnp.where(row > col, 1.0, 0.0).astype(BF16)
        causal = col < row
        q = q_ref[...]
        qh = (q * hmask[0], q * hmask[1])
        l_ref[...] = jnp.zeros_like(l_ref)
        acc_ref[...] = jnp.zeros_like(acc_ref)

        def blocks(todo):
            chains = [(b, h) for b in range(len(todo)) for h in range(2)]
            starts = [pl.multiple_of(kb * t, t) for kb, _ in todo]
            ks = [k_ref[pl.ds(st, t), :] for st in starts]
            vs = [v_ref[pl.ds(st, t), :] for st in starts]
            ar = {(b, h): _sb_logits(qh[h], ks[b], todo[b][1]) for b, h in chains}
            later = {bh: _split_mm(ar[bh][1], after) for bh in chains}
            carry = [l_ref[0], l_ref[1]]
            w = {}
            for b, (_, mask) in enumerate(todo):
                for h in range(2):
                    wbh = jnp.exp(ar[b, h][0] + later[b, h] + carry[h])
                    w[b, h] = wbh if mask is None else jnp.where(mask, wbh, 0.0)
                carry = [carry[h] + jnp.sum(ar[b, h][1], axis=1, keepdims=True) for h in range(2)]
            pv = {(b, h): _dot(w[b, h].astype(BF16), vs[b], 1, 0) for b, h in chains}
            acc = acc_ref[...]
            for b in range(len(todo)):
                acc = acc + jnp.where(first, pv[b, 0], pv[b, 1])
            acc_ref[...] = acc
            l_ref[0], l_ref[1] = carry

        _sb_walk(i, blocks, l_ref, causal)
        o_ref[...] = acc_ref[...].astype(o_ref.dtype)
        of_ref[...] = acc_ref[...]
        if comm is not None:
            pl.when(last_step)(lambda: comm.finish(*riding))

    blk = pl.BlockSpec((t, LANES), lambda p, i: (i, p))
    return pl.pallas_call(
        body, name="sb_fwd", grid=grid,
        in_specs=_sb_qkv_specs(s) + c_in_specs,
        out_specs=[blk, blk] + c_out_specs,
        out_shape=[jax.ShapeDtypeStruct((s, SB_WIDTH), BF16), jax.ShapeDtypeStruct((s, SB_WIDTH), F32)] + c_out_shape,
        scratch_shapes=c_scratch + [pltpu.VMEM((2, t, 1), F32), pltpu.VMEM((t, LANES), F32)],
        compiler_params=_cparams(),
    )(qkv, qkv, qkv, *c_ins)


def _sb_bwd(qkv, o, do, s, comm=None):
    t = SB_BLOCK
    nq = s // t
    grid = (SB_HEADS // 2, nq)
    c_in_specs, c_out_specs, c_out_shape, c_scratch, c_ins, split = _host(comm, 5, 3)

    def body(*refs):
        ((q_ref, k_ref, v_ref, o_ref, do_ref), (dq_ref, dk_ref, dv_ref),
         (l_ref, e_ref, dq_acc, dk_acc, dv_acc), riding) = split(refs)
        i = pl.program_id(1)
        if comm is not None:
            first_step, last_step = _grid_ends(grid)
            pl.when(first_step)(lambda: comm.start(*riding))
        first, hmask, row, col = _sb_masks()
        after = jnp.where(row > col, 1.0, 0.0).astype(BF16)
        from_here = jnp.where(row >= col, 1.0, 0.0).astype(BF16)
        causal = col < row

        @pl.when(i == 0)
        def _():
            dk_acc[...] = jnp.zeros_like(dk_acc)
            dv_acc[...] = jnp.zeros_like(dv_acc)

        q = q_ref[...]
        do_ = do_ref[...]
        qh = (q * hmask[0], q * hmask[1])
        doh = (do_ * hmask[0], do_ * hmask[1])
        prod = do_.astype(F32) * o_ref[...]
        total = (jnp.sum(jnp.where(first, prod, 0.0), axis=1, keepdims=True),
                 jnp.sum(jnp.where(first, 0.0, prod), axis=1, keepdims=True))
        l_ref[...] = jnp.zeros_like(l_ref)
        e_ref[...] = jnp.zeros_like(e_ref)
        dq_acc[...] = jnp.zeros_like(dq_acc)

        def blocks(todo):
            chains = [(b, h) for b in range(len(todo)) for h in range(2)]
            starts = [pl.multiple_of(kb * t, t) for kb, _ in todo]
            ks = [k_ref[pl.ds(st, t), :] for st in starts]
            vs = [v_ref[pl.ds(st, t), :] for st in starts]
            ar = {(b, h): _sb_logits(qh[h], ks[b], todo[b][1]) for b, h in chains}
            dw = {(b, h): _dot(doh[h], vs[b], 1, 1) for b, h in chains}
            later = {bh: _split_mm(ar[bh][1], after) for bh in chains}
            carry = [l_ref[0], l_ref[1]]
            wb = {}
            for b, (_, mask) in enumerate(todo):
                for h in range(2):
                    wbh = jnp.exp(ar[b, h][0] + later[b, h] + carry[h])
                    wb[b, h] = (wbh if mask is None else jnp.where(mask, wbh, 0.0)).astype(BF16)
                carry = [carry[h] + jnp.sum(ar[b, h][1], axis=1, keepdims=True) for h in range(2)]
            dvs = {(b, h): _dot(wb[b, h], do_, 0, 0) for b, h in chains}
            e = {bh: dw[bh] * wb[bh].astype(F32) for bh in chains}
            suffix = {bh: _split_mm(e[bh], from_here) for bh in chains}
            e_carry = [e_ref[0], e_ref[1]]
            dz = {}
            for b, (_, mask) in enumerate(todo):
                for h in range(2):
                    before = total[h] - (suffix[b, h] + e_carry[h])
                    dzh = e[b, h] - jnp.exp(ar[b, h][0]) * (e[b, h] + before)
                    dz[b, h] = (dzh if mask is None else jnp.where(mask, dzh, 0.0)).astype(BF16)
                e_carry = [e_carry[h] + jnp.sum(e[b, h], axis=1, keepdims=True) for h in range(2)]
            dqs = {(b, h): _dot(dz[b, h], ks[b], 1, 0) for b, h in chains}
            dks = {(b, h): _dot(dz[b, h], q, 0, 0) for b, h in chains}
            dq = dq_acc[...]
            for b, st in enumerate(starts):
                dq = dq + jnp.where(first, dqs[b, 0], dqs[b, 1])
                dk_acc[pl.ds(st, t), :] += jnp.where(first, dks[b, 0], dks[b, 1])
                dv_acc[pl.ds(st, t), :] += jnp.where(first, dvs[b, 0], dvs[b, 1])
            dq_acc[...] = dq
            l_ref[0], l_ref[1] = carry
            e_ref[0], e_ref[1] = e_carry

        _sb_walk(i, blocks, l_ref, causal)
        dq_ref[...] = (dq_acc[...] * SB_SCALE).astype(dq_ref.dtype)

        @pl.when(i == nq - 1)
        def _():
            dk_ref[...] = dk_acc[...].astype(dk_ref.dtype)
            dv_ref[...] = dv_acc[...].astype(dv_ref.dtype)

        if comm is not None:
            pl.when(last_step)(lambda: comm.finish(*riding))

    blk = pl.BlockSpec((t, LANES), lambda p, i: (i, p))
    col_blk = pl.BlockSpec((s, LANES), lambda p, i: (0, p))
    sds = jax.ShapeDtypeStruct((s, SB_WIDTH), BF16)
    return pl.pallas_call(
        body, name="sb_bwd", grid=grid,
        in_specs=_sb_qkv_specs(s) + [blk, blk] + c_in_specs,
        out_specs=[blk, col_blk, col_blk] + c_out_specs,
        out_shape=[sds, sds, sds] + c_out_shape,
        scratch_shapes=c_scratch + [pltpu.VMEM((2, t, 1), F32), pltpu.VMEM((2, t, 1), F32),
                                    pltpu.VMEM((t, LANES), F32), pltpu.VMEM((s, LANES), F32),
                                    pltpu.VMEM((s, LANES), F32)],
        compiler_params=_cparams(),
    )(qkv, qkv, qkv, o, do, *c_ins)


def _ret_log_gamma():
    lg = np.log1p(-np.exp2(-5.0 - np.arange(RET_HEADS, dtype=np.float32))).astype(np.float32)
    return jnp.asarray(np.broadcast_to(lg[:, None, None], (RET_HEADS, 8, LANES)).copy())


RET_SCRATCH = [pltpu.VMEM((RET_HEADS, RET_QK, RET_V), F32),
               pltpu.VMEM((RET_HEADS, RET_BLOCK, RET_BLOCK), F32),
               pltpu.VMEM((RET_HEADS, RET_BLOCK, 1), F32),
               pltpu.VMEM((RET_HEADS, RET_BLOCK, 1), F32)]


def _ret_begin(n, lg_ref, state, within, q_dec, k_dec):
    @pl.when(n == 0)
    def _():
        c = RET_BLOCK
        state[...] = jnp.zeros_like(state)
        row = lax.broadcasted_iota(jnp.int32, (c, c), 0)
        col = lax.broadcasted_iota(jnp.int32, (c, c), 1)
        rel = jnp.maximum(row - col, 0).astype(F32)
        idx = lax.broadcasted_iota(jnp.int32, (c, 1), 0).astype(F32)
        for h in range(RET_HEADS):
            lg = lg_ref[h, 0:1, 0:1]
            within[h] = jnp.where(row >= col, jnp.exp(lg * rel), 0.0)
            q_dec[h] = jnp.exp(lg * (idx + 1.0))
            k_dec[h] = jnp.exp(lg * (c - 1.0 - idx))


def _chunk_decay(lg_ref, h):
    return jnp.exp(lg_ref[h, 0:1, 0:1] * float(RET_BLOCK))


def _ret_heads(x, width):
    return [x[:, h * width:(h + 1) * width] for h in range(RET_HEADS)]


def _ret_specs(s, reverse=False):
    c = RET_BLOCK
    nc = s // c
    pos = (lambda n: nc - 1 - n) if reverse else (lambda n: n)
    q_spec = pl.BlockSpec((c, RET_QK_WIDTH), lambda n: (pos(n), 0))
    k_spec = pl.BlockSpec((c, RET_QK_WIDTH), lambda n: (pos(n), 1))
    v_spec = pl.BlockSpec((c, RET_V_WIDTH), lambda n: (pos(n), 0))
    lg_spec = pl.BlockSpec((RET_HEADS, 8, LANES), lambda n: (0, 0, 0))
    rope_spec = pl.BlockSpec((c, RET_QK), lambda n: (pos(n), 0))
    return nc, q_spec, k_spec, v_spec, lg_spec, rope_spec


def _ret_fwd(rqk, rvg, s):
    nc, q_spec, k_spec, v_spec, lg_spec, _ = _ret_specs(s)
    g_spec = pl.BlockSpec((RET_BLOCK, RET_V_WIDTH), lambda n: (n, 1))
    heads = range(RET_HEADS)

    def body(q_ref, k_ref, v_ref, g_ref, lg_ref, r_ref, y_ref, state, within, q_dec, k_dec):
        n = pl.program_id(0)
        _ret_begin(n, lg_ref, state, within, q_dec, k_dec)
        q, k = _ret_heads(q_ref[...], RET_QK), _ret_heads(k_ref[...], RET_QK)
        v, g = _ret_heads(v_ref[...], RET_V), _ret_heads(g_ref[...], RET_V)
        scores = [_dot(q[h].astype(BF16), k[h].astype(BF16), 1, 1) * within[h] for h in heads]
        cross = [_dot((q[h] * q_dec[h]).astype(BF16), state[h].astype(BF16), 1, 0) for h in heads]
        out = [_dot(scores[h].astype(BF16), v[h], 1, 0) + cross[h] for h in heads]
        grown = [_dot((k[h] * k_dec[h]).astype(BF16), v[h], 0, 0) for h in heads]
        for h in heads:
            sl = slice(h * RET_V, (h + 1) * RET_V)
            r_ref[:, sl] = out[h]
            xhat, _ = _norm(out[h])
            gh = g[h].astype(F32)
            y_ref[:, sl] = (gh * _sigmoid(gh) * xhat).astype(y_ref.dtype)
            state[h] = state[h] * _chunk_decay(lg_ref, h) + grown[h]

    return pl.pallas_call(
        body, name="ret_fwd", grid=(nc,),
        in_specs=[q_spec, k_spec, v_spec, g_spec, lg_spec],
        out_specs=[v_spec, v_spec],
        out_shape=[jax.ShapeDtypeStruct((s, RET_V_WIDTH), F32), jax.ShapeDtypeStruct((s, RET_V_WIDTH), BF16)],
        scratch_shapes=RET_SCRATCH,
        compiler_params=_cparams(),
    )(rqk, rqk, rvg, rvg, _ret_log_gamma())


def _rope_bwd(d, cos, sin):
    return d * cos + _swap_halves(d * sin)


def _ret_bwd_q(rqk, rv, d_out, cos2, sin2, s):
    nc, q_spec, k_spec, v_spec, lg_spec, rope_spec = _ret_specs(s)
    heads = range(RET_HEADS)

    def body(k_ref, v_ref, d_ref, lg_ref, cos_ref, sin_ref, dq_ref, state, within, q_dec, k_dec):
        n = pl.program_id(0)
        _ret_begin(n, lg_ref, state, within, q_dec, k_dec)
        k = _ret_heads(k_ref[...], RET_QK)
        v, d = _ret_heads(v_ref[...], RET_V), _ret_heads(d_ref[...], RET_V)
        cos, sin = cos_ref[...], sin_ref[...]
        d_scores = [_dot(d[h], v[h], 1, 1) * within[h] for h in heads]
        cross = [q_dec[h] * _dot(d[h], state[h].astype(BF16), 1, 1) for h in heads]
        dq = [_dot(d_scores[h].astype(BF16), k[h].astype(BF16), 1, 0) + cross[h] for h in heads]
        grown = [_dot((k[h] * k_dec[h]).astype(BF16), v[h], 0, 0) for h in heads]
        for h in heads:
            sl = slice(h * RET_QK, (h + 1) * RET_QK)
            dq_ref[:, sl] = (_rope_bwd(dq[h], cos, sin) * RET_SCALE).astype(dq_ref.dtype)
            state[h] = state[h] * _chunk_decay(lg_ref, h) + grown[h]

    return pl.pallas_call(
        body, name="ret_bwd_q", grid=(nc,),
        in_specs=[k_spec, v_spec, v_spec, lg_spec, rope_spec, rope_spec],
        out_specs=q_spec,
        out_shape=jax.ShapeDtypeStruct((s, RET_QK_WIDTH), BF16),
        scratch_shapes=RET_SCRATCH,
        compiler_params=_cparams(),
    )(rqk, rv, d_out, _ret_log_gamma(), cos2, sin2)


def _ret_bwd_kv(rqk, rv, d_out, cos2, sin2, s):
    nc, q_spec, k_spec, v_spec, lg_spec, rope_spec = _ret_specs(s, reverse=True)
    heads = range(RET_HEADS)

    def body(q_ref, k_ref, v_ref, d_ref, lg_ref, cos_ref, sin_ref, dk_ref, dv_ref, state, within, q_dec, k_dec):
        n = pl.program_id(0)
        _ret_begin(n, lg_ref, state, within, q_dec, k_dec)
        q, k = _ret_heads(q_ref[...], RET_QK), _ret_heads(k_ref[...], RET_QK)
        v, d = _ret_heads(v_ref[...], RET_V), _ret_heads(d_ref[...], RET_V)
        cos, sin = cos_ref[...], sin_ref[...]
        qb, kb = [q[h].astype(BF16) for h in heads], [k[h].astype(BF16) for h in heads]
        st = [state[h].astype(BF16) for h in heads]
        scores = [_dot(qb[h], kb[h], 1, 1) * within[h] for h in heads]
        d_scores = [_dot(d[h], v[h], 1, 1) * within[h] for h in heads]
        dk = [_dot(d_scores[h].astype(BF16), qb[h], 0, 0) + k_dec[h] * _dot(v[h], st[h], 1, 1) for h in heads]
        dv = [_dot(scores[h].astype(BF16), d[h], 0, 0) + k_dec[h] * _dot(kb[h], st[h], 1, 0) for h in heads]
        grown = [_dot((q[h] * q_dec[h]).astype(BF16), d[h], 0, 0) for h in heads]
        for h in heads:
            dk_ref[:, h * RET_QK:(h + 1) * RET_QK] = _rope_bwd(dk[h], cos, sin).astype(dk_ref.dtype)
            dv_ref[:, h * RET_V:(h + 1) * RET_V] = dv[h].astype(dv_ref.dtype)
            state[h] = state[h] * _chunk_decay(lg_ref, h) + grown[h]

    return pl.pallas_call(
        body, name="ret_bwd_kv", grid=(nc,),
        in_specs=[q_spec, k_spec, v_spec, v_spec, lg_spec, rope_spec, rope_spec],
        out_specs=[q_spec, v_spec],
        out_shape=[jax.ShapeDtypeStruct((s, RET_QK_WIDTH), BF16), jax.ShapeDtypeStruct((s, RET_V_WIDTH), BF16)],
        scratch_shapes=RET_SCRATCH,
        compiler_params=_cparams(),
    )(rqk, rqk, rv, d_out, _ret_log_gamma(), cos2, sin2)


def _xattn_probs(q, k):
    sc = _dot(q, k, 1, 1)
    sc = sc - jnp.max(sc, axis=-1, keepdims=True)
    p = jnp.exp(sc)
    return p / jnp.sum(p, axis=-1, keepdims=True)


def _xattn_fwd(qm, kv, s):
    tq = XATTN_ROWS

    def body(q_ref, kv_ref, o_ref):
        for h in range(MEM_HEADS):
            sl = slice(h * MEM_DIM, (h + 1) * MEM_DIM)
            sv = slice(D_MODEL + h * MEM_DIM, D_MODEL + (h + 1) * MEM_DIM)
            p = _xattn_probs(q_ref[:, sl], kv_ref[:, sl])
            o_ref[:, sl] = _dot(p.astype(BF16), kv_ref[:, sv], 1, 0).astype(o_ref.dtype)

    return pl.pallas_call(
        body, name="xattn_fwd", grid=(s // tq,),
        in_specs=[pl.BlockSpec((tq, D_MODEL), lambda i: (i, 0)),
                  pl.BlockSpec((MEM_LEN, 2 * D_MODEL), lambda i: (0, 0))],
        out_specs=pl.BlockSpec((tq, D_MODEL), lambda i: (i, 0)),
        out_shape=jax.ShapeDtypeStruct((s, D_MODEL), BF16),
        compiler_params=_cparams(),
    )(qm, kv)


def _xattn_bwd(qm, kv, do, s):
    tq = XATTN_ROWS

    def body(q_ref, kv_ref, do_ref, dq_ref, dkv_ref):
        i = pl.program_id(0)

        @pl.when(i == 0)
        def _():
            dkv_ref[...] = jnp.zeros_like(dkv_ref)

        for h in range(MEM_HEADS):
            sl = slice(h * MEM_DIM, (h + 1) * MEM_DIM)
            sv = slice(D_MODEL + h * MEM_DIM, D_MODEL + (h + 1) * MEM_DIM)
            q, k, v, d = q_ref[:, sl], kv_ref[:, sl], kv_ref[:, sv], do_ref[:, sl]
            p = _xattn_probs(q, k)
            dp = _dot(d, v, 1, 1)
            ds = (p * (dp - jnp.sum(p * dp, axis=-1, keepdims=True))).astype(BF16)
            dq_ref[:, sl] = (_dot(ds, k, 1, 0) * MEM_SCALE).astype(dq_ref.dtype)
            dkv_ref[:, sl] += _dot(ds, q, 0, 0)
            dkv_ref[:, sv] += _dot(p.astype(BF16), d, 0, 0)

    row_blk = pl.BlockSpec((tq, D_MODEL), lambda i: (i, 0))
    kv_blk = pl.BlockSpec((MEM_LEN, 2 * D_MODEL), lambda i: (0, 0))
    return pl.pallas_call(
        body, name="xattn_bwd", grid=(s // tq,),
        in_specs=[row_blk, kv_blk, row_blk],
        out_specs=[row_blk, kv_blk],
        out_shape=[jax.ShapeDtypeStruct((s, D_MODEL), BF16), jax.ShapeDtypeStruct((MEM_LEN, 2 * D_MODEL), F32)],
        compiler_params=_cparams(),
    )(qm, kv, do)


def _place():
    x, y, c = lax.axis_index("x"), lax.axis_index("y"), lax.axis_index("c")
    others = [(1 - x, y), (x, 1 - y), (1 - x, 1 - y)]
    return x, y, c, others


def _slab(ref, axis, chip, size):
    start = pl.multiple_of(chip * size, LANES if axis == 1 else 16)
    if axis == 0:
        return ref.at[pl.ds(start, size), :]
    return ref.at[:, pl.ds(start, size)]


class _CommPlan:
    def __init__(self, ins, out_shape, scratch, start, finish):
        self.ins, self.out_shape, self.scratch, self.start, self.finish = ins, out_shape, scratch, start, finish

    @property
    def specs(self):
        any_spec = pl.BlockSpec(memory_space=pl.ANY)
        return [any_spec] * len(self.ins), [any_spec] * len(self.out_shape)

    def split(self, refs):
        n_in, n_out = len(self.ins), len(self.out_shape)
        return refs[:n_in], refs[n_in:n_in + n_out], refs[n_in + n_out:]


def _run_plan(name, plan):
    def body(*refs):
        plan.start(*plan.split(refs))
        plan.finish(*plan.split(refs))

    in_specs, out_specs = plan.specs
    return pl.pallas_call(body, name=name, in_specs=in_specs, out_specs=out_specs, out_shape=plan.out_shape,
                          scratch_shapes=plan.scratch)(*plan.ins)


def _gather_plan(names, shards):
    spec = {name: (shape, axis) for name, shape, axis in BIG}
    nw = len(names)

    def shard_half(ref, c):
        rows = ref.shape[0] // 2
        return ref.at[pl.ds(pl.multiple_of(c * rows, 16), rows), :]

    def region(ref, w, chip, c):
        shape, axis = spec[names[w]]
        size = shape[axis] // N_CHIPS
        if axis == 0:
            rows = size // 2
            return ref.at[pl.ds(pl.multiple_of(chip * size + c * rows, 16), rows), :]
        rows = shape[0] // 2
        return ref.at[pl.ds(pl.multiple_of(c * rows, 16), rows), pl.ds(pl.multiple_of(chip * size, LANES), size)]

    def ops(shard, full, sems):
        ici_send, ici_recv, d2d_send, d2d_recv, local_sems = sems
        x, y, c, others = _place()
        mine, sibling = 2 * x + y, (x, y, 1 - c)
        local, over_ici, arrived, passed_on, from_sibling = [], [], [], [], []
        for w in range(nw):
            shape, axis = spec[names[w]]
            local.append(pltpu.make_async_copy(shard[w], _slab(full[w], axis, mine, shape[axis] // N_CHIPS),
                                               local_sems.at[w]))
            for t, (qx, qy) in enumerate(others):
                n, theirs = 3 * w + t, 2 * qx + qy
                over_ici.append(pltpu.make_async_remote_copy(
                    src_ref=shard_half(shard[w], c), dst_ref=region(full[w], w, mine, c),
                    send_sem=ici_send.at[n], recv_sem=ici_recv.at[n], device_id=(qx, qy, c), device_id_type=MESH))
                arrived.append(pltpu.make_async_remote_copy(
                    src_ref=shard_half(shard[w], c), dst_ref=region(full[w], w, theirs, c),
                    send_sem=ici_send.at[n], recv_sem=ici_recv.at[n], device_id=(qx, qy, c), device_id_type=MESH))
                passed_on.append(pltpu.make_async_remote_copy(
                    src_ref=region(full[w], w, theirs, c), dst_ref=region(full[w], w, theirs, c),
                    send_sem=d2d_send.at[n], recv_sem=d2d_recv.at[n], device_id=sibling, device_id_type=MESH))
                from_sibling.append(pltpu.make_async_remote_copy(
                    src_ref=region(full[w], w, theirs, c), dst_ref=region(full[w], w, theirs, 1 - c),
                    send_sem=d2d_send.at[n], recv_sem=d2d_recv.at[n], device_id=sibling, device_id_type=MESH))
        return local, over_ici, arrived, passed_on, from_sibling

    def start(shard, full, sems):
        local, over_ici, _, _, _ = ops(shard, full, sems)
        for cp in local + over_ici:
            cp.start()

    def finish(shard, full, sems):
        local, over_ici, arrived, passed_on, from_sibling = ops(shard, full, sems)
        for got, onward in zip(arrived, passed_on, strict=True):
            got.wait_recv()
            onward.start()
        for got in from_sibling:
            got.wait_recv()
        for cp in over_ici + passed_on:
            cp.wait_send()
        for cp in local:
            cp.wait()

    dma = pltpu.SemaphoreType.DMA
    return _CommPlan(
        ins=[shards[name] for name in names],
        out_shape=[jax.ShapeDtypeStruct(spec[name][0], BF16) for name in names],
        scratch=[dma((3 * nw,)), dma((3 * nw,)), dma((3 * nw,)), dma((3 * nw,)), dma((nw,))],
        start=start, finish=finish)


def _shard_shape(shape, axis):
    return tuple(d // N_CHIPS if a == axis else d for a, d in enumerate(shape))


def _exchange_plan(names, grads):
    spec = {name: (shape, axis) for name, shape, axis in BIG}
    nw = len(names)

    def ops(grad, stack, sems):
        send_sems, recv_sems, local_sems = sems
        x, y, c, others = _place()
        mine = 2 * x + y
        me, sibling = (x, y, c), (x, y, 1 - c)

        def dev(px, py, pc):
            return 4 * px + 2 * py + pc

        def copy(w, n, src, slot, to):
            return pltpu.make_async_remote_copy(
                src_ref=src, dst_ref=stack[w].at[slot], send_sem=send_sems.at[7 * w + n],
                recv_sem=recv_sems.at[7 * w + n], device_id=to, device_id_type=MESH)

        local, first, arrived, passed_on, from_sibling = [], [], [], [], []
        for w in range(nw):
            shape, axis = spec[names[w]]
            size = shape[axis] // N_CHIPS
            own = _slab(grad[w], axis, mine, size)
            local.append(pltpu.make_async_copy(own, stack[w].at[dev(*me)], local_sems.at[w]))
            first.append(copy(w, 0, own, dev(*me), sibling))
            from_sibling.append(copy(w, 0, own, dev(*sibling), me))
            for t, (qx, qy) in enumerate(others):
                got = stack[w].at[dev(qx, qy, c)]
                first.append(copy(w, 1 + t, _slab(grad[w], axis, 2 * qx + qy, size), dev(*me), (qx, qy, c)))
                arrived.append(copy(w, 1 + t, got, dev(qx, qy, c), me))
                passed_on.append(copy(w, 4 + t, got, dev(qx, qy, c), sibling))
                from_sibling.append(copy(w, 4 + t, got, dev(qx, qy, 1 - c), me))
        return local, first, arrived, passed_on, from_sibling

    def start(grad, stack, sems):
        local, first, _, _, _ = ops(grad, stack, sems)
        for cp in local + first:
            cp.start()

    def finish(grad, stack, sems):
        local, first, arrived, passed_on, from_sibling = ops(grad, stack, sems)
        for got, onward in zip(arrived, passed_on, strict=True):
            got.wait_recv()
            onward.start()
        for got in from_sibling:
            got.wait_recv()
        for cp in first + passed_on:
            cp.wait_send()
        for cp in local:
            cp.wait()

    dma = pltpu.SemaphoreType.DMA
    return _CommPlan(
        ins=[grads[name] for name in names],
        out_shape=[jax.ShapeDtypeStruct((N_DEV,) + _shard_shape(*spec[name]), BF16) for name in names],
        scratch=[dma((7 * nw,)), dma((7 * nw,)), dma((nw,))],
        start=start, finish=finish)


def _adamw(w, g, m, v):
    m = ADAM_B1 * m + (1.0 - ADAM_B1) * g
    v = ADAM_B2 * v + (1.0 - ADAM_B2) * (g * g)
    m_hat = m / (1.0 - ADAM_B1 ** ADAM_STEP)
    v_hat = v / (1.0 - ADAM_B2 ** ADAM_STEP)
    delta = -ADAM_LR * (m_hat / (jnp.sqrt(v_hat) + ADAM_EPS) + ADAM_WD * w)
    return delta, m, v


def _reduce_adamw(name, stack, w, m, v):
    rows, cols = w.shape
    tr = next(t for t in (256, 128, 64) if rows % t == 0)

    def body(s_ref, w_ref, m_ref, v_ref, g_ref, d_ref, nm_ref, nv_ref):
        g = s_ref[0].astype(F32)
        for d in range(1, N_DEV):
            g = g + s_ref[d].astype(F32)
        g_ref[...] = g
        d_ref[...], nm_ref[...], nv_ref[...] = _adamw(w_ref[...], g, m_ref[...], v_ref[...])

    blk = pl.BlockSpec((tr, cols), lambda i: (i, 0))
    return pl.pallas_call(
        body, name=name, grid=(rows // tr,),
        in_specs=[pl.BlockSpec((N_DEV, tr, cols), lambda i: (0, i, 0)), blk, blk, blk],
        out_specs=[blk] * 4, out_shape=[jax.ShapeDtypeStruct((rows, cols), F32)] * 4,
        compiler_params=_cparams(),
    )(stack, w, m, v)


def _small_step(pack, w, m, v):
    def body(p_ref, w_ref, m_ref, v_ref, g_ref, d_ref, nm_ref, nv_ref, loss_ref, all_ref, send_sems, recv_sems):
        x, y, c, _ = _place()
        me = 4 * x + 2 * y + c
        all_ref[me] = p_ref[...]
        sent = []
        for n in range(1, N_DEV):
            peer = me ^ n
            cp = pltpu.make_async_remote_copy(
                src_ref=p_ref, dst_ref=all_ref.at[me], send_sem=send_sems.at[n - 1], recv_sem=recv_sems.at[n - 1],
                device_id=(peer // 4, (peer // 2) % 2, peer % 2), device_id_type=MESH)
            cp.start()
            sent.append(cp)
        for n in range(1, N_DEV):
            peer = me ^ n
            pltpu.make_async_remote_copy(
                src_ref=p_ref, dst_ref=all_ref.at[peer], send_sem=send_sems.at[n - 1], recv_sem=recv_sems.at[n - 1],
                device_id=(peer // 4, (peer // 2) % 2, peer % 2), device_id_type=MESH).wait_recv()
        for cp in sent:
            cp.wait_send()
        tot = all_ref[0]
        for d in range(1, N_DEV):
            tot = tot + all_ref[d]
        g = tot[:SMALL_ROWS]
        g_ref[...] = g
        d_ref[...], nm_ref[...], nv_ref[...] = _adamw(w_ref[...], g, m_ref[...], v_ref[...])
        loss_ref[...] = jnp.sum(jnp.sum(tot[SMALL_ROWS:], axis=1, keepdims=True), axis=0, keepdims=True)

    vm = pl.BlockSpec(memory_space=pltpu.VMEM)
    small = jax.ShapeDtypeStruct((SMALL_ROWS, LANES), F32)
    return pl.pallas_call(
        body, name="small_step",
        in_specs=[vm] * 4, out_specs=[vm] * 5,
        out_shape=[small] * 4 + [jax.ShapeDtypeStruct((1, 1), F32)],
        scratch_shapes=[pltpu.VMEM((N_DEV, PACK_ROWS, LANES), F32),
                        pltpu.SemaphoreType.DMA((N_DEV - 1,)), pltpu.SemaphoreType.DMA((N_DEV - 1,))],
    )(pack, w, m, v)


LATER_WEIGHTS = tuple(name for name, _, _ in BIG if name != "w_in")


def _layer_step(x, mem, tgt, w_in, shards, vec):
    s = x.shape[0]
    d = D_MODEL
    tm = min(ROW_TILE, s)
    tl = min(WIDE_TILE, s)
    cos2, sin2 = _rope_tables(s)
    xb = _cast_bf16("cast_x", x)
    bf = lambda w: ((s, w), BF16)
    f32 = lambda w: ((s, w), F32)

    w_sb, w_rqk = w_in[:, :OFF_RET_Q], w_in[:, OFF_RET_Q:OFF_RET_V]
    w_rvg, w_gate = w_in[:, OFF_RET_V:OFF_GATE], w_in[:, OFF_GATE:]
    q_scale = lambda width, q_width, scale: jnp.concatenate(
        [jnp.full((1, q_width), scale, F32), jnp.ones((1, width - q_width), F32)], axis=1)
    n_groups = 3 * SB_WIDTH // LANES

    def sb_epi(acc, t, i, j):
        scaled = acc * t[0]
        return [jnp.stack([scaled[:, g * LANES:(g + 1) * LANES] for g in range(n_groups)])], []

    (sb_qkv,) = _mm(
        "in_sb", xb, w_sb, s, 3 * SB_WIDTH, d, tm=tm, tn=3 * SB_WIDTH, tk=d, epi=sb_epi,
        ins=[(q_scale(3 * SB_WIDTH, SB_WIDTH, SB_SCALE), *_rowvec(3 * SB_WIDTH))],
        outs=[((n_groups, s, LANES), BF16, (n_groups, tm, LANES), lambda i, j: (0, i, 0))])

    def rope_epi(acc, t, i, j):
        cos, sin, scale = t
        parts = []
        for g in range(2 * RET_QK_WIDTH // RET_QK):
            xg = acc[:, g * RET_QK:(g + 1) * RET_QK]
            parts.append(xg * cos + _swap_halves(xg) * sin)
        return [jnp.concatenate(parts, axis=1) * scale], []

    rope_in = ((tm, RET_QK), lambda i, j: (i, 0))
    (rqk,) = _mm("in_rqk", xb, w_rqk, s, 2 * RET_QK_WIDTH, d, tm=tm, tn=2 * RET_QK_WIDTH, tk=d, epi=rope_epi,
                 ins=[(cos2, *rope_in), (sin2, *rope_in),
                      (q_scale(2 * RET_QK_WIDTH, RET_QK_WIDTH, RET_SCALE), *_rowvec(2 * RET_QK_WIDTH))],
                 outs=[(*f32(2 * RET_QK_WIDTH), *_tile(tm, 2 * RET_QK_WIDTH))])
    (rvg,) = _mm("in_rvg", xb, w_rvg, s, 2 * RET_V_WIDTH, d, tm=tm, tn=2 * RET_V_WIDTH, tk=d,
                 epi=_plain, outs=[(*bf(2 * RET_V_WIDTH), *_tile(tm, 2 * RET_V_WIDTH))])
    (gates,) = _mm("in_gate", xb, w_gate, s, 2 * d, d, tm=tm, tn=2 * d, tk=d,
                   epi=lambda acc, t, i, j: ([_sigmoid(acc + t[0])], []),
                   ins=[(vec["b_gate"], *_rowvec(2 * d))], outs=[(*bf(2 * d), *_tile(tm, 2 * d))])

    sb_out, sb_out_f32, *gathered = _sb_fwd(sb_qkv, s, comm=_gather_plan(LATER_WEIGHTS, shards))
    wt = dict(zip(LATER_WEIGHTS, gathered, strict=True))
    ret, gated = _ret_fwd(rqk, rvg, s)
    (y_sb,) = _mm("sb_o", sb_out, wt["w_sb_o"], s, d, SB_WIDTH, tm=tl, tn=d, tk=SB_WIDTH, epi=_plain,
                  outs=[(*bf(d), *_tile(tl, d))])
    y_ret, mixin = _mm(
        "ret_o", gated, wt["w_ret_o"], s, d, RET_V_WIDTH, tm=tl, tn=d, tk=RET_V_WIDTH,
        epi=lambda acc, t, i, j: ([acc, t[0].astype(F32) * t[2].astype(F32) + t[1].astype(F32) * acc], []),
        ins=[(gates, *_tile(tl, d)), (gates, *_tile(tl, d, 1)), (y_sb, *_tile(tl, d))],
        outs=[(*bf(d), *_tile(tl, d)), (*bf(d), *_tile(tl, d))])

    def ln_epi(acc, t, i, j):
        res, g, b = t
        xhat, rstd = _norm(DN_ALPHA * res + acc)
        y = xhat * g + b
        return [y, y, xhat, rstd], []

    full = _tile(tm, d)
    col1 = ((tm, 1), lambda i, j: (i, 0))
    ln_outs = [(*f32(d), *full), (*bf(d), *full), (*f32(d), *full), ((s, 1), F32, *col1)]
    x1, x1b, xhat1, rstd1 = _mm(
        "mix_o", mixin, wt["w_mix_o"], s, d, d, tm=tm, tn=d, tk=d, epi=ln_epi,
        ins=[(x, *full), (vec["ln1_g"], *_rowvec(d)), (vec["ln1_b"], *_rowvec(d))], outs=ln_outs)

    (qm,) = _mm("mem_q", x1b, wt["w_mem_q"], s, d, d, tm=tl, tn=d, tk=d,
                epi=lambda acc, t, i, j: ([acc * MEM_SCALE], []), outs=[(*bf(d), *_tile(tl, d))])
    (kv,) = _mm("mem_kv", mem, wt["w_mem_kv"], MEM_LEN, 2 * d, d, tm=MEM_LEN, tn=d, tk=d, epi=_plain,
                outs=[((MEM_LEN, 2 * d), BF16, *_tile(MEM_LEN, d))])
    att = _xattn_fwd(qm, kv, s)
    x2, x2b, xhat2, rstd2 = _mm(
        "mem_o", att, wt["w_mem_o"], s, d, d, tm=tm, tn=d, tk=d, epi=ln_epi,
        ins=[(x1, *full), (vec["ln2_g"], *_rowvec(d)), (vec["ln2_b"], *_rowvec(d))], outs=ln_outs)

    fh = FFN_HIDDEN
    tf = fh // 2
    (f1,) = _mm("ffn_in1", x2b, wt["w_ffn_in"], s, fh, d, tm=tl, tn=tf, tk=d, epi=_plain, j_outer=True,
                outs=[(*bf(fh), *_tile(tl, tf))])

    def swiglu_epi(acc, t, i, j):
        a = t[0].astype(F32)
        return [acc, a * _sigmoid(a) * acc], []

    f2, act = _mm(
        "ffn_in2", x2b, wt["w_ffn_in"], s, fh, d, tm=tm, tn=tf, tk=d, b_off=(0, 2), epi=swiglu_epi, j_outer=True,
        ins=[(f1, *_tile(tm, tf))], outs=[(*bf(fh), *_tile(tm, tf)), (*bf(fh), *_tile(tm, tf))])

    def head_epi(acc, t, i, j):
        res, g, b, target = t
        xhat, rstd = _norm(DN_ALPHA * res + acc)
        err = xhat * g + b - target
        dy = err * (1.0 / d)
        du = _norm_bwd(dy * g, xhat, rstd)
        return [du, du], [_colsum(dy * xhat), _colsum(dy), _colsum(err * err) * (0.5 / d)]

    vec_acc = ((1, d), F32)
    du3, du3b, dg3, db3, loss_cols = _mm(
        "ffn_out", act, wt["w_ffn_out"], s, d, fh, tm=tm, tn=d, tk=fh, epi=head_epi,
        ins=[(x2, *full), (vec["ln3_g"], *_rowvec(d)), (vec["ln3_b"], *_rowvec(d)), (tgt, *full)],
        outs=[(*f32(d), *full), (*bf(d), *full)], accs=[vec_acc] * 3)

    grads = {}
    ts = min(SEQ_TILE, s)

    def wgrad(name, a, b, m, n, tm_, tn_, tk_=None):
        (g,) = _mm(name, a, b, m, n, a.shape[0], tm=tm_, tn=tn_, tk=tk_ or ts, ta=True, epi=_plain,
                   outs=[((m, n), BF16, *_tile(tm_, tn_))])
        return g

    def ffn_bwd_epi(acc, t, i, j):
        a, b = t[0].astype(F32), t[1].astype(F32)
        sg = _sigmoid(a)
        return [acc * b * (sg * (1.0 + a * (1.0 - sg))), acc * (a * sg)], []

    df1, df2 = _mm(
        "ffn_out_t", du3b, wt["w_ffn_out"], s, fh, d, tm=tm, tn=tf, tk=d, tb=True, epi=ffn_bwd_epi, j_outer=True,
        ins=[(f1, *_tile(tm, tf)), (f2, *_tile(tm, tf))],
        outs=[(*bf(fh), *_tile(tm, tf)), (*bf(fh), *_tile(tm, tf))])
    grads["w_ffn_out"] = wgrad("g_ffn_out", act, du3b, fh, d, tf, d)
    grads["w_ffn_in"] = jnp.concatenate(
        [wgrad("g_ffn_in1", x2b, df1, d, fh, d, tf), wgrad("g_ffn_in2", x2b, df2, d, fh, d, tf)], axis=1)
    (dx2a,) = _mm("ffn_in1_t", df1, wt["w_ffn_in"], s, d, fh, tm=tm, tn=d, tk=fh, tb=True, epi=_plain,
                  outs=[(*f32(d), *full)])

    def ln_bwd(name, a, b, k, tk, b_off, more, scales, xhat, rstd, g):
        def epi(acc, t, i, j):
            *extra, xh, rs, gg = t
            dy = acc
            for e, sc in zip(extra, scales, strict=True):
                dy = dy + e * sc
            du = _norm_bwd(dy * gg, xh, rs)
            return [du, du], [_colsum(dy * xh), _colsum(dy)]

        return _mm(name, a, b, s, d, k, tm=tm, tn=d, tk=tk, tb=True, b_off=b_off, epi=epi,
                   ins=[(e, *full) for e in more] + [(xhat, *full), (rstd, *col1), (g, *_rowvec(d))],
                   outs=[(*f32(d), *full), (*bf(d), *full)], accs=[vec_acc] * 2)

    du2, du2b, dg2, db2 = ln_bwd("ffn_in2_t", df2, wt["w_ffn_in"], fh, fh, (0, 1), [dx2a, du3], [1.0, DN_ALPHA],
                                 xhat2, rstd2, vec["ln2_g"])

    (datt,) = _mm("mem_o_t", du2b, wt["w_mem_o"], s, d, d, tm=tl, tn=d, tk=d, tb=True, epi=_plain,
                  outs=[(*bf(d), *_tile(tl, d))])
    grads["w_mem_o"] = wgrad("g_mem_o", att, du2b, d, d, d, d)
    dqm, dkv = _xattn_bwd(qm, kv, datt, s)
    grads["w_mem_q"] = wgrad("g_mem_q", x1b, dqm, d, d, d, d)
    grads["w_mem_kv"] = wgrad("g_mem_kv", mem, dkv, d, 2 * d, d, d, MEM_LEN)
    du1, du1b, dg1, db1 = ln_bwd("mem_q_t", dqm, wt["w_mem_q"], d, d, (0, 0), [du2], [DN_ALPHA],
                                 xhat1, rstd1, vec["ln1_g"])

    def merge_bwd_epi(acc, t, i, j):
        g0, g1, ysb, yret = (v.astype(F32) for v in t)
        dgate0 = acc * ysb * (g0 * (1.0 - g0))
        dgate1 = acc * yret * (g1 * (1.0 - g1))
        return [dgate0, dgate1, acc * g0, acc * g1], [_colsum(dgate0), _colsum(dgate1)]

    dgate0, dgate1, dy_sb, dy_ret, dbg0, dbg1 = _mm(
        "mix_o_t", du1b, wt["w_mix_o"], s, d, d, tm=tm, tn=d, tk=d, tb=True, epi=merge_bwd_epi,
        ins=[(gates, *full), (gates, *_tile(tm, d, 1)), (y_sb, *full), (y_ret, *full)],
        outs=[(*bf(d), *full)] * 4, accs=[vec_acc] * 2)
    grads["w_mix_o"] = wgrad("g_mix_o", mixin, du1b, d, d, d, d)
    grads["w_sb_o"] = wgrad("g_sb_o", sb_out, dy_sb, SB_WIDTH, d, SB_WIDTH, d)
    grads["w_ret_o"] = wgrad("g_ret_o", gated, dy_ret, RET_V_WIDTH, d, RET_V_WIDTH, d)
    (dsb_out,) = _mm("sb_o_t", dy_sb, wt["w_sb_o"], s, SB_WIDTH, d, tm=tl, tn=SB_WIDTH, tk=d, tb=True, epi=_plain,
                     outs=[(*bf(SB_WIDTH), *_tile(tl, SB_WIDTH))])

    def gate_norm_bwd_epi(acc, t, i, j):
        r, g = t[0], t[1].astype(F32)
        drg, dret = [], []
        for h in range(d // RET_V):
            sl = slice(h * RET_V, (h + 1) * RET_V)
            xhat, rstd = _norm(r[:, sl])
            gg, dd = g[:, sl], acc[:, sl]
            sg = _sigmoid(gg)
            drg.append(dd * xhat * (sg * (1.0 + gg * (1.0 - sg))))
            dret.append(_norm_bwd(dd * (gg * sg), xhat, rstd))
        return [jnp.concatenate(drg, axis=1), jnp.concatenate(dret, axis=1)], []

    drg, dret = _mm(
        "ret_o_t", dy_ret, wt["w_ret_o"], s, RET_V_WIDTH, d, tm=tm, tn=d, tk=d, tb=True, epi=gate_norm_bwd_epi,
        ins=[(ret, *full), (rvg, *_tile(tm, d, 1))],
        outs=[(*bf(RET_V_WIDTH), *full)] * 2)

    drq = _ret_bwd_q(rqk, rvg, dret, cos2, sin2, s)
    drk, drv = _ret_bwd_kv(rqk, rvg, dret, cos2, sin2, s)
    dsq, dsk, dsv, *stacked = _sb_bwd(sb_qkv, sb_out_f32, dsb_out, s, comm=_exchange_plan(LATER_WEIGHTS, grads))
    stacks = dict(zip(LATER_WEIGHTS, stacked, strict=True))

    dh = jnp.concatenate([dsq, dsk, dsv, drq, drk, drv, drg, dgate0, dgate1], axis=1)
    grads["w_in"] = wgrad("g_in", xb, dh, d, IN_WIDTH, d, IN_WIDTH // N_CHIPS)
    grad_x, stacks["w_in"] = _mm(
        "in_t", dh, w_in, s, d, IN_WIDTH, tm=tl, tn=d, tk=IN_WIDTH // N_CHIPS, tb=True,
        epi=lambda acc, t, i, j: ([acc + DN_ALPHA * t[0]], []),
        ins=[(du1, *_tile(tl, d))], outs=[(*f32(d), *_tile(tl, d))], comm=_exchange_plan(("w_in",), grads))

    small = {"b_gate": jnp.concatenate([dbg0, dbg1], axis=1), "ln1_g": dg1, "ln1_b": db1, "ln2_g": dg2,
             "ln2_b": db2, "ln3_g": dg3, "ln3_b": db3}
    return grad_x, stacks, small, loss_cols


def kernel(x, mem, w_in, b_gate, w_sb_o, w_ret_o, w_mix_o, ln1_g, ln1_b, w_mem_q, w_mem_kv, w_mem_o, ln2_g, ln2_b, w_ffn_in, w_ffn_out, ln3_g, ln3_b, loss_target, m_w_in, m_b_gate, m_w_sb_o, m_w_ret_o, m_w_mix_o, m_ln1_g, m_ln1_b, m_w_mem_q, m_w_mem_kv, m_w_mem_o, m_ln2_g, m_ln2_b, m_w_ffn_in, m_w_ffn_out, m_ln3_g, m_ln3_b, v_w_in, v_b_gate, v_w_sb_o, v_w_ret_o, v_w_mix_o, v_ln1_g, v_ln1_b, v_w_mem_q, v_w_mem_kv, v_w_mem_o, v_ln2_g, v_ln2_b, v_w_ffn_in, v_w_ffn_out, v_ln3_g, v_ln3_b):
    given = dict(locals())
    s = x.shape[1]
    x2d = x.reshape(s, D_MODEL)
    tgt = loss_target.reshape(s, D_MODEL)
    mem2d = mem.reshape(MEM_LEN, D_MODEL)
    shard = {name: given[name].reshape(_shard_shape(shape, axis)) for name, shape, axis in BIG}
    vec = {name: given[name] for name in SMALL}

    shards_bf = {name: _cast_bf16("cast_" + name, shard[name]) for name, _, _ in BIG}
    (w_in_full,) = _run_plan("gather_w_in", _gather_plan(("w_in",), shards_bf))

    grad_x, stacks, small, loss_cols = _layer_step(x2d, mem2d, tgt, w_in_full, shards_bf, vec)

    out = {}
    for name, shape, axis in BIG:
        stack = stacks[name]
        shp = given[name].shape
        res = _reduce_adamw("adamw_" + name, stack, shard[name], given["m_" + name].reshape(stack.shape[1:]),
                            given["v_" + name].reshape(stack.shape[1:]))
        out[name] = [r.reshape(shp) for r in res]

    pack = jnp.concatenate([small[name] for name in SMALL] + [loss_cols], axis=1).reshape(PACK_ROWS, LANES)
    cat = lambda pre: jnp.concatenate([given[pre + name] for name in SMALL], axis=1).reshape(SMALL_ROWS, LANES)
    *res, loss = _small_step(pack, cat(""), cat("m_"), cat("v_"))
    flat = [r.reshape(1, SMALL_LEN) for r in res]
    off = 0
    for name in SMALL:
        n = given[name].shape[1]
        out[name] = [r[:, off:off + n] for r in flat]
        off += n

    return (loss.reshape(()), grad_x.reshape(x.shape),
            *[out[name][0] for name in WEIGHT_ORDER], *[out[name][1] for name in WEIGHT_ORDER],
            *[out[name][2] for name in WEIGHT_ORDER], *[out[name][3] for name in WEIGHT_ORDER])
```

```python
import functools

import jax
import jax.numpy as jnp
import numpy as np
from jax import lax
from jax.experimental import pallas as pl
from jax.experimental.pallas import tpu as pltpu

F32, BF16 = jnp.float32, jnp.bfloat16
MESH = pl.DeviceIdType.MESH

D_MODEL = 1024
MEM_LEN = 256
SB_HEADS, SB_DIM, SB_WIDTH = 8, 64, 512
RET_HEADS, RET_QK, RET_V = 4, 128, 256
RET_QK_WIDTH, RET_V_WIDTH = 512, 1024
ROPE_BASE = 10000.0
MEM_HEADS, MEM_DIM = 4, 256
FFN_HIDDEN = 2816
IN_WIDTH = 6656
OFF_RET_Q, OFF_RET_V, OFF_RET_G, OFF_GATE = 1536, 2560, 3584, 4608
DN_ALPHA = 2.0 ** 0.25
LN_EPS = 1e-5
SB_SCALE = SB_DIM ** -0.5
SB_DEAD = -110.0
RET_SCALE = RET_QK ** -0.5
MEM_SCALE = MEM_DIM ** -0.5
ADAM_LR, ADAM_B1, ADAM_B2, ADAM_EPS, ADAM_WD, ADAM_STEP = 0.001, 0.9, 0.999, 1e-08, 0.01, 10

N_DEV, N_CHIPS = 8, 4

LANES = 128
MXU_COLS = 256
VMEM_LIMIT_BYTES = 52 * 2 ** 20
ROW_TILE = 512
WIDE_TILE = 1024
SEQ_TILE = 1024
SB_BLOCK = 256
RET_BLOCK = 256
XATTN_ROWS = 512

BIG = (
    ("w_in", (D_MODEL, IN_WIDTH), 1),
    ("w_sb_o", (SB_WIDTH, D_MODEL), 1),
    ("w_ret_o", (RET_V_WIDTH, D_MODEL), 0),
    ("w_mix_o", (D_MODEL, D_MODEL), 0),
    ("w_mem_q", (D_MODEL, D_MODEL), 0),
    ("w_mem_kv", (D_MODEL, 2 * D_MODEL), 1),
    ("w_mem_o", (D_MODEL, D_MODEL), 0),
    ("w_ffn_in", (D_MODEL, 2 * FFN_HIDDEN), 1),
    ("w_ffn_out", (FFN_HIDDEN, D_MODEL), 0),
)
SMALL = ("b_gate", "ln1_g", "ln1_b", "ln2_g", "ln2_b", "ln3_g", "ln3_b")
SMALL_LEN = 2 * D_MODEL + 6 * D_MODEL
SMALL_ROWS = SMALL_LEN // LANES
PACK_ROWS = SMALL_ROWS + D_MODEL // LANES
WEIGHT_ORDER = ("w_in", "b_gate", "w_sb_o", "w_ret_o", "w_mix_o", "ln1_g", "ln1_b", "w_mem_q", "w_mem_kv",
                "w_mem_o", "ln2_g", "ln2_b", "w_ffn_in", "w_ffn_out", "ln3_g", "ln3_b")


def _cparams():
    return pltpu.CompilerParams(vmem_limit_bytes=VMEM_LIMIT_BYTES)


def _dot(a, b, ca, cb):
    return lax.dot_general(a, b, (((ca,), (cb,)), ((), ())), preferred_element_type=F32)


def _sigmoid(x):
    return 1.0 / (1.0 + jnp.exp(-x))


def _mm(name, a, b, m, n, k, *, tm, tn, tk, epi, outs, ins=(), accs=(), ta=False, tb=False,
        a_off=(0, 0), b_off=(0, 0), j_outer=False, comm=None, chunk=None):
    assert m % tm == 0 and n % tn == 0 and k % tk == 0, (name, m, n, k, tm, tn, tk)
    assert chunk is None or (k == tk and tn % chunk == 0), name
    ni, nj, nk = m // tm, n // tn, k // tk
    assert not accs or nj == 1, name
    ij = (lambda g0, g1: (g1, g0)) if j_outer else (lambda g0, g1: (g0, g1))

    def spec(block, index):
        return pl.BlockSpec(block, lambda g0, g1, kk: index(*ij(g0, g1), kk))

    if ta:
        a_spec = spec((tk, tm), lambda i, j, kk: (kk + a_off[0], i + a_off[1]))
    else:
        a_spec = spec((tm, tk), lambda i, j, kk: (i + a_off[0], kk + a_off[1]))
    if tb:
        b_spec = spec((tn, tk), lambda i, j, kk: (j + b_off[0], kk + b_off[1]))
    else:
        b_spec = spec((tk, tn), lambda i, j, kk: (kk + b_off[0], j + b_off[1]))
    in_specs = [a_spec, b_spec]
    for _, bs, im in ins:
        in_specs.append(spec(bs, lambda i, j, kk, im=im: im(i, j)))
    out_specs, out_shape = [], []
    for shape, dtype, bs, im in outs:
        out_specs.append(spec(bs, lambda i, j, kk, im=im: im(i, j)))
        out_shape.append(jax.ShapeDtypeStruct(shape, dtype))
    for shape, dtype in accs:
        out_specs.append(spec(shape, lambda i, j, kk, nd=len(shape): (0,) * nd))
        out_shape.append(jax.ShapeDtypeStruct(shape, dtype))
    n_in, n_out, n_acc = len(ins), len(outs), len(accs)
    ca, cb = (0 if ta else 1), (1 if tb else 0)
    grid = (*ij(ni, nj), nk)
    comm_ins, comm_outs, comm_scratch = [], [], []
    if comm is not None:
        comm_in_specs, comm_out_specs = comm.specs
        comm_ins, comm_outs, comm_scratch = list(comm.ins), list(comm.out_shape), list(comm.scratch)
        in_specs += comm_in_specs
        out_specs += comm_out_specs
        out_shape += comm_outs
    n_ci, n_co = len(comm_ins), len(comm_outs)

    def body(*refs):
        a_ref, b_ref = refs[:2]
        in_refs = refs[2:2 + n_in]
        ci_refs = refs[2 + n_in:2 + n_in + n_ci]
        rest = refs[2 + n_in + n_ci:]
        out_refs, acc_refs = rest[:n_out], rest[n_out:n_out + n_acc]
        co_refs = rest[n_out + n_acc:n_out + n_acc + n_co]
        scratch = rest[n_out + n_acc + n_co:]
        sem_refs, scratch = scratch[:len(comm_scratch)], scratch[len(comm_scratch):]
        (i, j), kk = ij(pl.program_id(0), pl.program_id(1)), pl.program_id(2)
        if comm is not None:
            first_step, last_step = _grid_ends(grid)
            pl.when(first_step)(lambda: comm.start(ci_refs, co_refs, sem_refs))
        def finish(acc, cols=slice(None)):
            def of(r):
                return r[..., cols] if r.shape[-1] == tn else r[...]

            o_tiles, a_tiles = epi(acc, [of(r) for r in in_refs], i, j)
            for r, t in zip(out_refs, o_tiles, strict=True):
                r[..., cols] = t.astype(r.dtype)
            if n_acc:
                @pl.when(i == 0)
                def _():
                    for r, t in zip(acc_refs, a_tiles, strict=True):
                        r[..., cols] = t

                @pl.when(i > 0)
                def _():
                    for r, t in zip(acc_refs, a_tiles, strict=True):
                        r[..., cols] += t

        if chunk is not None:
            a_tile = a_ref[...].astype(BF16)
            for c0 in range(0, tn, chunk):
                cols = slice(c0, c0 + chunk)
                b_part = b_ref[cols, :] if tb else b_ref[:, cols]
                finish(_dot(a_tile, b_part.astype(BF16), ca, cb), cols)
            if comm is not None:
                pl.when(last_step)(lambda: comm.finish(ci_refs, co_refs, sem_refs))
            return

        part = _dot(a_ref[...].astype(BF16), b_ref[...].astype(BF16), ca, cb)
        if nk == 1:
            finish(part)
        else:
            acc_ref = scratch[0]

            @pl.when(kk == 0)
            def _():
                acc_ref[...] = part

            @pl.when(kk > 0)
            def _():
                acc_ref[...] += part

            @pl.when(kk == nk - 1)
            def _():
                finish(acc_ref[...])

        if comm is not None:
            pl.when(last_step)(lambda: comm.finish(ci_refs, co_refs, sem_refs))

    res = pl.pallas_call(
        body, name=name, grid=grid, in_specs=in_specs, out_specs=out_specs, out_shape=out_shape,
        scratch_shapes=comm_scratch + ([pltpu.VMEM((tm, tn), F32)] if nk > 1 else []),
        compiler_params=_cparams(),
    )(a, b, *[x for x, _, _ in ins], *comm_ins)
    return res


def _grid_ends(grid):
    ids = [pl.program_id(ax) for ax in range(len(grid))]
    first = functools.reduce(jnp.logical_and, [p == 0 for p in ids])
    last = functools.reduce(jnp.logical_and, [p == n - 1 for p, n in zip(ids, grid, strict=True)])
    return first, last


def _tile(tm, tn, dj=0):
    return (tm, tn), (lambda i, j: (i, j + dj))


def _rowvec(tn, dj=0):
    return (1, tn), (lambda i, j: (0, j + dj))


def _plain(acc, tiles, i, j):
    return [acc], []


def _ew(name, fn, ins, outs, rows, tr):
    assert rows % tr == 0, (name, rows, tr)
    in_specs = []
    for x in ins:
        if x.shape[0] == rows:
            in_specs.append(pl.BlockSpec((tr, x.shape[1]), lambda i: (i, 0)))
        else:
            in_specs.append(pl.BlockSpec(x.shape, lambda i: (0, 0)))
    n_in = len(ins)

    def body(*refs):
        res = fn(*[r[...] for r in refs[:n_in]])
        for r, t in zip(refs[n_in:], res, strict=True):
            r[...] = t.astype(r.dtype)

    return pl.pallas_call(
        body, name=name, grid=(rows // tr,), in_specs=in_specs,
        out_specs=[pl.BlockSpec((tr, w), lambda i: (i, 0)) for w, _ in outs],
        out_shape=[jax.ShapeDtypeStruct((rows, w), dt) for w, dt in outs],
        compiler_params=_cparams(),
    )(*ins)


def _cast_bf16(name, x):
    rows = x.shape[0]
    tr = next(t for t in (512, 256, 64) if rows % t == 0)
    return _ew(name, lambda v: (v,), [x], [(x.shape[1], BF16)], rows, tr)[0]


def _rope_tables(s):
    half = RET_QK // 2
    inv = 1.0 / (ROPE_BASE ** (jnp.arange(half, dtype=F32) / half))
    inv2 = jnp.concatenate([inv, inv]).reshape(1, RET_QK)
    sign = jnp.concatenate([-jnp.ones((half,), F32), jnp.ones((half,), F32)]).reshape(1, RET_QK)
    tr = ROW_TILE

    def body(inv_ref, sign_ref, cos_ref, sin_ref):
        i = pl.program_id(0)
        pos = (lax.broadcasted_iota(jnp.int32, (tr, RET_QK), 0) + i * tr).astype(F32)
        ang = pos * inv_ref[...]
        cos_ref[...] = jnp.cos(ang)
        sin_ref[...] = jnp.sin(ang) * sign_ref[...]

    vec = pl.BlockSpec((1, RET_QK), lambda i: (0, 0))
    blk = pl.BlockSpec((tr, RET_QK), lambda i: (i, 0))
    return pl.pallas_call(
        body, name="rope_tables", grid=(s // tr,), in_specs=[vec, vec], out_specs=[blk, blk],
        out_shape=[jax.ShapeDtypeStruct((s, RET_QK), F32)] * 2, compiler_params=_cparams(),
    )(inv2, sign)


def _swap_halves(x):
    return pltpu.roll(x, RET_QK // 2, 1)


def _norm(u):
    mu = jnp.mean(u, axis=-1, keepdims=True)
    d = u - mu
    var = jnp.mean(d * d, axis=-1, keepdims=True)
    rstd = lax.rsqrt(var + LN_EPS)
    return d * rstd, rstd


def _norm_bwd(dxh, xhat, rstd):
    m1 = jnp.mean(dxh, axis=-1, keepdims=True)
    m2 = jnp.mean(dxh * xhat, axis=-1, keepdims=True)
    return rstd * (dxh - m1 - xhat * m2)


def _colsum(t):
    return jnp.sum(t, axis=0, keepdims=True)


def _split_mm(t, tri):
    hi = t.astype(BF16)
    lo = (t - hi.astype(F32)).astype(BF16)
    return _dot(hi, tri, 1, 0) + _dot(lo, tri, 1, 0)


def _sb_masks():
    t = SB_BLOCK
    lane = lax.broadcasted_iota(jnp.int32, (1, LANES), 1)
    first = lane < SB_DIM
    m0 = jnp.where(first, 1.0, 0.0).astype(BF16)
    m1 = jnp.where(first, 0.0, 1.0).astype(BF16)
    row = lax.broadcasted_iota(jnp.int32, (t, t), 0)
    col = lax.broadcasted_iota(jnp.int32, (t, t), 1)
    return first, (m0, m1), row, col


def _sb_logits(qh, k, causal):
    z = _dot(qh, k, 1, 1)
    lp = jnp.log(1.0 + jnp.exp(-jnp.abs(z)))
    a = jnp.minimum(z, 0.0) - lp
    r = jnp.minimum(-z, 0.0) - lp
    if causal is not None:
        r = jnp.where(causal, r, 0.0)
    return a, r


def _sb_walk(i, blocks, l_ref, causal):
    pl.when(i == 0)(lambda: blocks([(i, causal)]))
    pl.when(i > 0)(lambda: blocks([(i, causal), (i - 1, None)]))

    def alive():
        top = jnp.max(jnp.maximum(l_ref[0], l_ref[1]))
        return jnp.where(top > SB_DEAD, 1, 0)

    def cond(c):
        return jnp.logical_and(c[0] < i, c[1] > 0)

    def step(c):
        blocks([(i - 1 - c[0], None)])
        return c[0] + 1, alive()

    lax.while_loop(cond, step, (jnp.int32(1), alive()))


def _host(comm, n_in, n_out):
    if comm is None:
        return [], [], [], [], [], lambda refs: (refs[:n_in], refs[n_in:n_in + n_out], refs[n_in + n_out:], None)
    in_specs, out_specs = comm.specs
    n_ci, n_co, n_sem = len(comm.ins), len(comm.out_shape), len(comm.scratch)

    def split(refs):
        ins, ci = refs[:n_in], refs[n_in:n_in + n_ci]
        rest = refs[n_in + n_ci:]
        outs, co = rest[:n_out], rest[n_out:n_out + n_co]
        sems, scratch = rest[n_out + n_co:n_out + n_co + n_sem], rest[n_out + n_co + n_sem:]
        return ins, outs, scratch, (ci, co, sems)

    return in_specs, out_specs, list(comm.out_shape), list(comm.scratch), list(comm.ins), split


def _sb_qkv_specs(s):
    pairs = SB_HEADS // 2
    return [pl.BlockSpec((None, SB_BLOCK, LANES), lambda p, i: (p, i, 0)),
            pl.BlockSpec((None, s, LANES), lambda p, i: (pairs + p, 0, 0)),
            pl.BlockSpec((None, s, LANES), lambda p, i: (2 * pairs + p, 0, 0))]


def _sb_fwd(qkv, s, comm=None):
    t = SB_BLOCK
    nq = s // t
    grid = (SB_HEADS // 2, nq)
    c_in_specs, c_out_specs, c_out_shape, c_scratch, c_ins, split = _host(comm, 3, 2)

    def body(*refs):
        (q_ref, k_ref, v_ref), (o_ref, of_ref), (l_ref, acc_ref), riding = split(refs)
        i = pl.program_id(1)
        if comm is not None:
            first_step, last_step = _grid_ends(grid)
            pl.when(first_step)(lambda: comm.start(*riding))
        first, hmask, row, col = _sb_masks()
        after = jnp.where(row > col, 1.0, 0.0).astype(BF16)
        causal = col < row
        q = q_ref[...]
        qh = (q * hmask[0], q * hmask[1])
        l_ref[...] = jnp.zeros_like(l_ref)
        acc_ref[...] = jnp.zeros_like(acc_ref)

        def blocks(todo):
            chains = [(b, h) for b in range(len(todo)) for h in range(2)]
            starts = [pl.multiple_of(kb * t, t) for kb, _ in todo]
            ks = [k_ref[pl.ds(st, t), :] for st in starts]
            vs = [v_ref[pl.ds(st, t), :] for st in starts]
            ar = {(b, h): _sb_logits(qh[h], ks[b], todo[b][1]) for b, h in chains}
            later = {bh: _split_mm(ar[bh][1], after) for bh in chains}
            carry = [l_ref[0], l_ref[1]]
            w = {}
            for b, (_, mask) in enumerate(todo):
                for h in range(2):
                    wbh = jnp.exp(ar[b, h][0] + later[b, h] + carry[h])
                    w[b, h] = wbh if mask is None else jnp.where(mask, wbh, 0.0)
                carry = [carry[h] + jnp.sum(ar[b, h][1], axis=1, keepdims=True) for h in range(2)]
            pv = {(b, h): _dot(w[b, h].astype(BF16), vs[b], 1, 0) for b, h in chains}
            acc = acc_ref[...]
            for b in range(len(todo)):
                acc = acc + jnp.where(first, pv[b, 0], pv[b, 1])
            acc_ref[...] = acc
            l_ref[0], l_ref[1] = carry

        _sb_walk(i, blocks, l_ref, causal)
        o_ref[...] = acc_ref[...].astype(o_ref.dtype)
        of_ref[...] = acc_ref[...]
        if comm is not None:
            pl.when(last_step)(lambda: comm.finish(*riding))

    blk = pl.BlockSpec((t, LANES), lambda p, i: (i, p))
    return pl.pallas_call(
        body, name="sb_fwd", grid=grid,
        in_specs=_sb_qkv_specs(s) + c_in_specs,
        out_specs=[blk, blk] + c_out_specs,
        out_shape=[jax.ShapeDtypeStruct((s, SB_WIDTH), BF16), jax.ShapeDtypeStruct((s, SB_WIDTH), F32)] + c_out_shape,
        scratch_shapes=c_scratch + [pltpu.VMEM((2, t, 1), F32), pltpu.VMEM((t, LANES), F32)],
        compiler_params=_cparams(),
    )(qkv, qkv, qkv, *c_ins)


def _sb_bwd(qkv, o, do, s, comm=None):
    t = SB_BLOCK
    nq = s // t
    grid = (SB_HEADS // 2, nq)
    c_in_specs, c_out_specs, c_out_shape, c_scratch, c_ins, split = _host(comm, 5, 3)

    def body(*refs):
        ((q_ref, k_ref, v_ref, o_ref, do_ref), (dq_ref, dk_ref, dv_ref),
         (l_ref, e_ref, dq_acc, dk_acc, dv_acc), riding) = split(refs)
        i = pl.program_id(1)
        if comm is not None:
            first_step, last_step = _grid_ends(grid)
            pl.when(first_step)(lambda: comm.start(*riding))
        first, hmask, row, col = _sb_masks()
        after = jnp.where(row > col, 1.0, 0.0).astype(BF16)
        from_here = jnp.where(row >= col, 1.0, 0.0).astype(BF16)
        causal = col < row

        @pl.when(i == 0)
        def _():
            dk_acc[...] = jnp.zeros_like(dk_acc)
            dv_acc[...] = jnp.zeros_like(dv_acc)

        q = q_ref[...]
        do_ = do_ref[...]
        qh = (q * hmask[0], q * hmask[1])
        doh = (do_ * hmask[0], do_ * hmask[1])
        prod = do_.astype(F32) * o_ref[...]
        total = (jnp.sum(jnp.where(first, prod, 0.0), axis=1, keepdims=True),
                 jnp.sum(jnp.where(first, 0.0, prod), axis=1, keepdims=True))
        l_ref[...] = jnp.zeros_like(l_ref)
        e_ref[...] = jnp.zeros_like(e_ref)
        dq_acc[...] = jnp.zeros_like(dq_acc)

        def blocks(todo):
            chains = [(b, h) for b in range(len(todo)) for h in range(2)]
            starts = [pl.multiple_of(kb * t, t) for kb, _ in todo]
            ks = [k_ref[pl.ds(st, t), :] for st in starts]
            vs = [v_ref[pl.ds(st, t), :] for st in starts]
            ar = {(b, h): _sb_logits(qh[h], ks[b], todo[b][1]) for b, h in chains}
            dw = {(b, h): _dot(doh[h], vs[b], 1, 1) for b, h in chains}
            later = {bh: _split_mm(ar[bh][1], after) for bh in chains}
            carry = [l_ref[0], l_ref[1]]
            wb = {}
            for b, (_, mask) in enumerate(todo):
                for h in range(2):
                    wbh = jnp.exp(ar[b, h][0] + later[b, h] + carry[h])
                    wb[b, h] = (wbh if mask is None else jnp.where(mask, wbh, 0.0)).astype(BF16)
                carry = [carry[h] + jnp.sum(ar[b, h][1], axis=1, keepdims=True) for h in range(2)]
            dvs = {(b, h): _dot(wb[b, h], do_, 0, 0) for b, h in chains}
            e = {bh: dw[bh] * wb[bh].astype(F32) for bh in chains}
            suffix = {bh: _split_mm(e[bh], from_here) for bh in chains}
            e_carry = [e_ref[0], e_ref[1]]
            dz = {}
            for b, (_, mask) in enumerate(todo):
                for h in range(2):
                    before = total[h] - (suffix[b, h] + e_carry[h])
                    dzh = e[b, h] - jnp.exp(ar[b, h][0]) * (e[b, h] + before)
                    dz[b, h] = (dzh if mask is None else jnp.where(mask, dzh, 0.0)).astype(BF16)
                e_carry = [e_carry[h] + jnp.sum(e[b, h], axis=1, keepdims=True) for h in range(2)]
            dqs = {(b, h): _dot(dz[b, h], ks[b], 1, 0) for b, h in chains}
            dks = {(b, h): _dot(dz[b, h], q, 0, 0) for b, h in chains}
            dq = dq_acc[...]
            for b, st in enumerate(starts):
                dq = dq + jnp.where(first, dqs[b, 0], dqs[b, 1])
                dk_acc[pl.ds(st, t), :] += jnp.where(first, dks[b, 0], dks[b, 1])
                dv_acc[pl.ds(st, t), :] += jnp.where(first, dvs[b, 0], dvs[b, 1])
            dq_acc[...] = dq
            l_ref[0], l_ref[1] = carry
            e_ref[0], e_ref[1] = e_carry

        _sb_walk(i, blocks, l_ref, causal)
        dq_ref[...] = (dq_acc[...] * SB_SCALE).astype(dq_ref.dtype)

        @pl.when(i == nq - 1)
        def _():
            dk_ref[...] = dk_acc[...].astype(dk_ref.dtype)
            dv_ref[...] = dv_acc[...].astype(dv_ref.dtype)

        if comm is not None:
            pl.when(last_step)(lambda: comm.finish(*riding))

    blk = pl.BlockSpec((t, LANES), lambda p, i: (i, p))
    col_blk = pl.BlockSpec((s, LANES), lambda p, i: (0, p))
    sds = jax.ShapeDtypeStruct((s, SB_WIDTH), BF16)
    return pl.pallas_call(
        body, name="sb_bwd", grid=grid,
        in_specs=_sb_qkv_specs(s) + [blk, blk] + c_in_specs,
        out_specs=[blk, col_blk, col_blk] + c_out_specs,
        out_shape=[sds, sds, sds] + c_out_shape,
        scratch_shapes=c_scratch + [pltpu.VMEM((2, t, 1), F32), pltpu.VMEM((2, t, 1), F32),
                                    pltpu.VMEM((t, LANES), F32), pltpu.VMEM((s, LANES), F32),
                                    pltpu.VMEM((s, LANES), F32)],
        compiler_params=_cparams(),
    )(qkv, qkv, qkv, o, do, *c_ins)


def _ret_log_gamma():
    lg = np.log1p(-np.exp2(-5.0 - np.arange(RET_HEADS, dtype=np.float32))).astype(np.float32)
    return jnp.asarray(np.broadcast_to(lg[:, None, None], (RET_HEADS, 8, LANES)).copy())


RET_SCRATCH = [pltpu.VMEM((RET_HEADS, RET_QK, RET_V), F32),
               pltpu.VMEM((RET_HEADS, RET_BLOCK, RET_BLOCK), F32),
               pltpu.VMEM((RET_HEADS, RET_BLOCK, 1), F32),
               pltpu.VMEM((RET_HEADS, RET_BLOCK, 1), F32)]


def _ret_begin(n, lg_ref, state, within, q_dec, k_dec):
    @pl.when(n == 0)
    def _():
        c = RET_BLOCK
        state[...] = jnp.zeros_like(state)
        row = lax.broadcasted_iota(jnp.int32, (c, c), 0)
        col = lax.broadcasted_iota(jnp.int32, (c, c), 1)
        rel = jnp.maximum(row - col, 0).astype(F32)
        idx = lax.broadcasted_iota(jnp.int32, (c, 1), 0).astype(F32)
        for h in range(RET_HEADS):
            lg = lg_ref[h, 0:1, 0:1]
            within[h] = jnp.where(row >= col, jnp.exp(lg * rel), 0.0)
            q_dec[h] = jnp.exp(lg * (idx + 1.0))
            k_dec[h] = jnp.exp(lg * (c - 1.0 - idx))


def _chunk_decay(lg_ref, h):
    return jnp.exp(lg_ref[h, 0:1, 0:1] * float(RET_BLOCK))


def _ret_heads(x, width):
    return [x[:, h * width:(h + 1) * width] for h in range(RET_HEADS)]


def _ret_specs(s, reverse=False):
    c = RET_BLOCK
    nc = s // c
    pos = (lambda n: nc - 1 - n) if reverse else (lambda n: n)
    q_spec = pl.BlockSpec((c, RET_QK_WIDTH), lambda n: (pos(n), 0))
    k_spec = pl.BlockSpec((c, RET_QK_WIDTH), lambda n: (pos(n), 1))
    v_spec = pl.BlockSpec((c, RET_V_WIDTH), lambda n: (pos(n), 0))
    lg_spec = pl.BlockSpec((RET_HEADS, 8, LANES), lambda n: (0, 0, 0))
    rope_spec = pl.BlockSpec((c, RET_QK), lambda n: (pos(n), 0))
    return nc, q_spec, k_spec, v_spec, lg_spec, rope_spec


def _ret_fwd(rqk, rvg, s):
    nc, q_spec, k_spec, v_spec, lg_spec, _ = _ret_specs(s)
    g_spec = pl.BlockSpec((RET_BLOCK, RET_V_WIDTH), lambda n: (n, 1))
    heads = range(RET_HEADS)

    def body(q_ref, k_ref, v_ref, g_ref, lg_ref, r_ref, y_ref, state, within, q_dec, k_dec):
        n = pl.program_id(0)
        _ret_begin(n, lg_ref, state, within, q_dec, k_dec)
        q, k = _ret_heads(q_ref[...], RET_QK), _ret_heads(k_ref[...], RET_QK)
        v, g = _ret_heads(v_ref[...], RET_V), _ret_heads(g_ref[...], RET_V)
        scores = [_dot(q[h].astype(BF16), k[h].astype(BF16), 1, 1) * within[h] for h in heads]
        cross = [_dot((q[h] * q_dec[h]).astype(BF16), state[h].astype(BF16), 1, 0) for h in heads]
        out = [_dot(scores[h].astype(BF16), v[h], 1, 0) + cross[h] for h in heads]
        grown = [_dot((k[h] * k_dec[h]).astype(BF16), v[h], 0, 0) for h in heads]
        for h in heads:
            sl = slice(h * RET_V, (h + 1) * RET_V)
            r_ref[:, sl] = out[h]
            xhat, _ = _norm(out[h])
            gh = g[h].astype(F32)
            y_ref[:, sl] = (gh * _sigmoid(gh) * xhat).astype(y_ref.dtype)
            state[h] = state[h] * _chunk_decay(lg_ref, h) + grown[h]

    return pl.pallas_call(
        body, name="ret_fwd", grid=(nc,),
        in_specs=[q_spec, k_spec, v_spec, g_spec, lg_spec],
        out_specs=[v_spec, v_spec],
        out_shape=[jax.ShapeDtypeStruct((s, RET_V_WIDTH), F32), jax.ShapeDtypeStruct((s, RET_V_WIDTH), BF16)],
        scratch_shapes=RET_SCRATCH,
        compiler_params=_cparams(),
    )(rqk, rqk, rvg, rvg, _ret_log_gamma())


def _rope_bwd(d, cos, sin):
    return d * cos + _swap_halves(d * sin)


def _ret_bwd_q(rqk, rv, d_out, cos2, sin2, s):
    nc, q_spec, k_spec, v_spec, lg_spec, rope_spec = _ret_specs(s)
    heads = range(RET_HEADS)

    def body(k_ref, v_ref, d_ref, lg_ref, cos_ref, sin_ref, dq_ref, state, within, q_dec, k_dec):
        n = pl.program_id(0)
        _ret_begin(n, lg_ref, state, within, q_dec, k_dec)
        k = _ret_heads(k_ref[...], RET_QK)
        v, d = _ret_heads(v_ref[...], RET_V), _ret_heads(d_ref[...], RET_V)
        cos, sin = cos_ref[...], sin_ref[...]
        d_scores = [_dot(d[h], v[h], 1, 1) * within[h] for h in heads]
        cross = [q_dec[h] * _dot(d[h], state[h].astype(BF16), 1, 1) for h in heads]
        dq = [_dot(d_scores[h].astype(BF16), k[h].astype(BF16), 1, 0) + cross[h] for h in heads]
        grown = [_dot((k[h] * k_dec[h]).astype(BF16), v[h], 0, 0) for h in heads]
        for h in heads:
            sl = slice(h * RET_QK, (h + 1) * RET_QK)
            dq_ref[:, sl] = (_rope_bwd(dq[h], cos, sin) * RET_SCALE).astype(dq_ref.dtype)
            state[h] = state[h] * _chunk_decay(lg_ref, h) + grown[h]

    return pl.pallas_call(
        body, name="ret_bwd_q", grid=(nc,),
        in_specs=[k_spec, v_spec, v_spec, lg_spec, rope_spec, rope_spec],
        out_specs=q_spec,
        out_shape=jax.ShapeDtypeStruct((s, RET_QK_WIDTH), BF16),
        scratch_shapes=RET_SCRATCH,
        compiler_params=_cparams(),
    )(rqk, rv, d_out, _ret_log_gamma(), cos2, sin2)


def _ret_bwd_kv(rqk, rv, d_out, cos2, sin2, s):
    nc, q_spec, k_spec, v_spec, lg_spec, rope_spec = _ret_specs(s, reverse=True)
    heads = range(RET_HEADS)

    def body(q_ref, k_ref, v_ref, d_ref, lg_ref, cos_ref, sin_ref, dk_ref, dv_ref, state, within, q_dec, k_dec):
        n = pl.program_id(0)
        _ret_begin(n, lg_ref, state, within, q_dec, k_dec)
        q, k = _ret_heads(q_ref[...], RET_QK), _ret_heads(k_ref[...], RET_QK)
        v, d = _ret_heads(v_ref[...], RET_V), _ret_heads(d_ref[...], RET_V)
        cos, sin = cos_ref[...], sin_ref[...]
        qb, kb = [q[h].astype(BF16) for h in heads], [k[h].astype(BF16) for h in heads]
        st = [state[h].astype(BF16) for h in heads]
        scores = [_dot(qb[h], kb[h], 1, 1) * within[h] for h in heads]
        d_scores = [_dot(d[h], v[h], 1, 1) * within[h] for h in heads]
        dk = [_dot(d_scores[h].astype(BF16), qb[h], 0, 0) + k_dec[h] * _dot(v[h], st[h], 1, 1) for h in heads]
        dv = [_dot(scores[h].astype(BF16), d[h], 0, 0) + k_dec[h] * _dot(kb[h], st[h], 1, 0) for h in heads]
        grown = [_dot((q[h] * q_dec[h]).astype(BF16), d[h], 0, 0) for h in heads]
        for h in heads:
            dk_ref[:, h * RET_QK:(h + 1) * RET_QK] = _rope_bwd(dk[h], cos, sin).astype(dk_ref.dtype)
            dv_ref[:, h * RET_V:(h + 1) * RET_V] = dv[h].astype(dv_ref.dtype)
            state[h] = state[h] * _chunk_decay(lg_ref, h) + grown[h]

    return pl.pallas_call(
        body, name="ret_bwd_kv", grid=(nc,),
        in_specs=[q_spec, k_spec, v_spec, v_spec, lg_spec, rope_spec, rope_spec],
        out_specs=[q_spec, v_spec],
        out_shape=[jax.ShapeDtypeStruct((s, RET_QK_WIDTH), BF16), jax.ShapeDtypeStruct((s, RET_V_WIDTH), BF16)],
        scratch_shapes=RET_SCRATCH,
        compiler_params=_cparams(),
    )(rqk, rqk, rv, d_out, _ret_log_gamma(), cos2, sin2)


def _xattn_probs(q, k):
    sc = _dot(q, k, 1, 1)
    sc = sc - jnp.max(sc, axis=-1, keepdims=True)
    p = jnp.exp(sc)
    return p / jnp.sum(p, axis=-1, keepdims=True)


def _xattn_fwd(qm, kv, s):
    tq = XATTN_ROWS

    def body(q_ref, kv_ref, o_ref):
        for h in range(MEM_HEADS):
            sl = slice(h * MEM_DIM, (h + 1) * MEM_DIM)
            sv = slice(D_MODEL + h * MEM_DIM, D_MODEL + (h + 1) * MEM_DIM)
            p = _xattn_probs(q_ref[:, sl], kv_ref[:, sl])
            o_ref[:, sl] = _dot(p.astype(BF16), kv_ref[:, sv], 1, 0).astype(o_ref.dtype)

    return pl.pallas_call(
        body, name="xattn_fwd", grid=(s // tq,),
        in_specs=[pl.BlockSpec((tq, D_MODEL), lambda i: (i, 0)),
                  pl.BlockSpec((MEM_LEN, 2 * D_MODEL), lambda i: (0, 0))],
        out_specs=pl.BlockSpec((tq, D_MODEL), lambda i: (i, 0)),
        out_shape=jax.ShapeDtypeStruct((s, D_MODEL), BF16),
        compiler_params=_cparams(),
    )(qm, kv)


def _xattn_bwd(qm, kv, do, s):
    tq = XATTN_ROWS

    def body(q_ref, kv_ref, do_ref, dq_ref, dkv_ref):
        i = pl.program_id(0)

        @pl.when(i == 0)
        def _():
            dkv_ref[...] = jnp.zeros_like(dkv_ref)

        for h in range(MEM_HEADS):
            sl = slice(h * MEM_DIM, (h + 1) * MEM_DIM)
            sv = slice(D_MODEL + h * MEM_DIM, D_MODEL + (h + 1) * MEM_DIM)
            q, k, v, d = q_ref[:, sl], kv_ref[:, sl], kv_ref[:, sv], do_ref[:, sl]
            p = _xattn_probs(q, k)
            dp = _dot(d, v, 1, 1)
            ds = (p * (dp - jnp.sum(p * dp, axis=-1, keepdims=True))).astype(BF16)
            dq_ref[:, sl] = (_dot(ds, k, 1, 0) * MEM_SCALE).astype(dq_ref.dtype)
            dkv_ref[:, sl] += _dot(ds, q, 0, 0)
            dkv_ref[:, sv] += _dot(p.astype(BF16), d, 0, 0)

    row_blk = pl.BlockSpec((tq, D_MODEL), lambda i: (i, 0))
    kv_blk = pl.BlockSpec((MEM_LEN, 2 * D_MODEL), lambda i: (0, 0))
    return pl.pallas_call(
        body, name="xattn_bwd", grid=(s // tq,),
        in_specs=[row_blk, kv_blk, row_blk],
        out_specs=[row_blk, kv_blk],
        out_shape=[jax.ShapeDtypeStruct((s, D_MODEL), BF16), jax.ShapeDtypeStruct((MEM_LEN, 2 * D_MODEL), F32)],
        compiler_params=_cparams(),
    )(qm, kv, do)


def _place():
    x, y, c = lax.axis_index("x"), lax.axis_index("y"), lax.axis_index("c")
    others = [(1 - x, y), (x, 1 - y), (1 - x, 1 - y)]
    return x, y, c, others


def _slab(ref, axis, chip, size):
    start = pl.multiple_of(chip * size, LANES if axis == 1 else 16)
    if axis == 0:
        return ref.at[pl.ds(start, size), :]
    return ref.at[:, pl.ds(start, size)]


class _CommPlan:
    def __init__(self, ins, out_shape, scratch, start, finish):
        self.ins, self.out_shape, self.scratch, self.start, self.finish = ins, out_shape, scratch, start, finish

    @property
    def specs(self):
        any_spec = pl.BlockSpec(memory_space=pl.ANY)
        return [any_spec] * len(self.ins), [any_spec] * len(self.out_shape)

    def split(self, refs):
        n_in, n_out = len(self.ins), len(self.out_shape)
        return refs[:n_in], refs[n_in:n_in + n_out], refs[n_in + n_out:]


def _run_plan(name, plan):
    def body(*refs):
        plan.start(*plan.split(refs))
        plan.finish(*plan.split(refs))

    in_specs, out_specs = plan.specs
    return pl.pallas_call(body, name=name, in_specs=in_specs, out_specs=out_specs, out_shape=plan.out_shape,
                          scratch_shapes=plan.scratch)(*plan.ins)


def _gather_plan(names, shards):
    spec = {name: (shape, axis) for name, shape, axis in BIG}
    nw = len(names)

    def shard_half(ref, c):
        rows = ref.shape[0] // 2
        return ref.at[pl.ds(pl.multiple_of(c * rows, 16), rows), :]

    def region(ref, w, chip, c):
        shape, axis = spec[names[w]]
        size = shape[axis] // N_CHIPS
        if axis == 0:
            rows = size // 2
            return ref.at[pl.ds(pl.multiple_of(chip * size + c * rows, 16), rows), :]
        rows = shape[0] // 2
        return ref.at[pl.ds(pl.multiple_of(c * rows, 16), rows), pl.ds(pl.multiple_of(chip * size, LANES), size)]

    def ops(shard, full, sems):
        ici_send, ici_recv, d2d_send, d2d_recv, local_sems = sems
        x, y, c, others = _place()
        mine, sibling = 2 * x + y, (x, y, 1 - c)
        local, over_ici, arrived, passed_on, from_sibling = [], [], [], [], []
        for w in range(nw):
            shape, axis = spec[names[w]]
            local.append(pltpu.make_async_copy(shard[w], _slab(full[w], axis, mine, shape[axis] // N_CHIPS),
                                               local_sems.at[w]))
            for t, (qx, qy) in enumerate(others):
                n, theirs = 3 * w + t, 2 * qx + qy
                over_ici.append(pltpu.make_async_remote_copy(
                    src_ref=shard_half(shard[w], c), dst_ref=region(full[w], w, mine, c),
                    send_sem=ici_send.at[n], recv_sem=ici_recv.at[n], device_id=(qx, qy, c), device_id_type=MESH))
                arrived.append(pltpu.make_async_remote_copy(
                    src_ref=shard_half(shard[w], c), dst_ref=region(full[w], w, theirs, c),
                    send_sem=ici_send.at[n], recv_sem=ici_recv.at[n], device_id=(qx, qy, c), device_id_type=MESH))
                passed_on.append(pltpu.make_async_remote_copy(
                    src_ref=region(full[w], w, theirs, c), dst_ref=region(full[w], w, theirs, c),
                    send_sem=d2d_send.at[n], recv_sem=d2d_recv.at[n], device_id=sibling, device_id_type=MESH))
                from_sibling.append(pltpu.make_async_remote_copy(
                    src_ref=region(full[w], w, theirs, c), dst_ref=region(full[w], w, theirs, 1 - c),
                    send_sem=d2d_send.at[n], recv_sem=d2d_recv.at[n], device_id=sibling, device_id_type=MESH))
        return local, over_ici, arrived, passed_on, from_sibling

    def start(shard, full, sems):
        local, over_ici, _, _, _ = ops(shard, full, sems)
        for cp in local + over_ici:
            cp.start()

    def finish(shard, full, sems):
        local, over_ici, arrived, passed_on, from_sibling = ops(shard, full, sems)
        for got, onward in zip(arrived, passed_on, strict=True):
            got.wait_recv()
            onward.start()
        for got in from_sibling:
            got.wait_recv()
        for cp in over_ici + passed_on:
            cp.wait_send()
        for cp in local:
            cp.wait()

    dma = pltpu.SemaphoreType.DMA
    return _CommPlan(
        ins=[shards[name] for name in names],
        out_shape=[jax.ShapeDtypeStruct(spec[name][0], BF16) for name in names],
        scratch=[dma((3 * nw,)), dma((3 * nw,)), dma((3 * nw,)), dma((3 * nw,)), dma((nw,))],
        start=start, finish=finish)


def _shard_shape(shape, axis):
    return tuple(d // N_CHIPS if a == axis else d for a, d in enumerate(shape))


def _exchange_plan(names, grads):
    spec = {name: (shape, axis) for name, shape, axis in BIG}
    nw = len(names)

    def ops(grad, stack, sems):
        send_sems, recv_sems, local_sems = sems
        x, y, c, others = _place()
        mine = 2 * x + y
        me, sibling = (x, y, c), (x, y, 1 - c)

        def dev(px, py, pc):
            return 4 * px + 2 * py + pc

        def copy(w, n, src, slot, to):
            return pltpu.make_async_remote_copy(
                src_ref=src, dst_ref=stack[w].at[slot], send_sem=send_sems.at[7 * w + n],
                recv_sem=recv_sems.at[7 * w + n], device_id=to, device_id_type=MESH)

        local, first, arrived, passed_on, from_sibling = [], [], [], [], []
        for w in range(nw):
            shape, axis = spec[names[w]]
            size = shape[axis] // N_CHIPS
            own = _slab(grad[w], axis, mine, size)
            local.append(pltpu.make_async_copy(own, stack[w].at[dev(*me)], local_sems.at[w]))
            first.append(copy(w, 0, own, dev(*me), sibling))
            from_sibling.append(copy(w, 0, own, dev(*sibling), me))
            for t, (qx, qy) in enumerate(others):
                got = stack[w].at[dev(qx, qy, c)]
                first.append(copy(w, 1 + t, _slab(grad[w], axis, 2 * qx + qy, size), dev(*me), (qx, qy, c)))
                arrived.append(copy(w, 1 + t, got, dev(qx, qy, c), me))
                passed_on.append(copy(w, 4 + t, got, dev(qx, qy, c), sibling))
                from_sibling.append(copy(w, 4 + t, got, dev(qx, qy, 1 - c), me))
        return local, first, arrived, passed_on, from_sibling

    def start(grad, stack, sems):
        local, first, _, _, _ = ops(grad, stack, sems)
        for cp in local + first:
            cp.start()

    def finish(grad, stack, sems):
        local, first, arrived, passed_on, from_sibling = ops(grad, stack, sems)
        for got, onward in zip(arrived, passed_on, strict=True):
            got.wait_recv()
            onward.start()
        for got in from_sibling:
            got.wait_recv()
        for cp in first + passed_on:
            cp.wait_send()
        for cp in local:
            cp.wait()

    dma = pltpu.SemaphoreType.DMA
    return _CommPlan(
        ins=[grads[name] for name in names],
        out_shape=[jax.ShapeDtypeStruct((N_DEV,) + _shard_shape(*spec[name]), BF16) for name in names],
        scratch=[dma((7 * nw,)), dma((7 * nw,)), dma((nw,))],
        start=start, finish=finish)


def _adamw(w, g, m, v):
    m = ADAM_B1 * m + (1.0 - ADAM_B1) * g
    v = ADAM_B2 * v + (1.0 - ADAM_B2) * (g * g)
    m_hat = m / (1.0 - ADAM_B1 ** ADAM_STEP)
    v_hat = v / (1.0 - ADAM_B2 ** ADAM_STEP)
    delta = -ADAM_LR * (m_hat / (jnp.sqrt(v_hat) + ADAM_EPS) + ADAM_WD * w)
    return delta, m, v


def _reduce_adamw(name, stack, w, m, v):
    rows, cols = w.shape
    tr = next(t for t in (256, 128, 64) if rows % t == 0)

    def body(s_ref, w_ref, m_ref, v_ref, g_ref, d_ref, nm_ref, nv_ref):
        g = s_ref[0].astype(F32)
        for d in range(1, N_DEV):
            g = g + s_ref[d].astype(F32)
        g_ref[...] = g
        d_ref[...], nm_ref[...], nv_ref[...] = _adamw(w_ref[...], g, m_ref[...], v_ref[...])

    blk = pl.BlockSpec((tr, cols), lambda i: (i, 0))
    return pl.pallas_call(
        body, name=name, grid=(rows // tr,),
        in_specs=[pl.BlockSpec((N_DEV, tr, cols), lambda i: (0, i, 0)), blk, blk, blk],
        out_specs=[blk] * 4, out_shape=[jax.ShapeDtypeStruct((rows, cols), F32)] * 4,
        compiler_params=_cparams(),
    )(stack, w, m, v)


def _small_step(pack, w, m, v):
    def body(p_ref, w_ref, m_ref, v_ref, g_ref, d_ref, nm_ref, nv_ref, loss_ref, all_ref, send_sems, recv_sems):
        x, y, c, _ = _place()
        me = 4 * x + 2 * y + c
        all_ref[me] = p_ref[...]
        sent = []
        for n in range(1, N_DEV):
            peer = me ^ n
            cp = pltpu.make_async_remote_copy(
                src_ref=p_ref, dst_ref=all_ref.at[me], send_sem=send_sems.at[n - 1], recv_sem=recv_sems.at[n - 1],
                device_id=(peer // 4, (peer // 2) % 2, peer % 2), device_id_type=MESH)
            cp.start()
            sent.append(cp)
        for n in range(1, N_DEV):
            peer = me ^ n
            pltpu.make_async_remote_copy(
                src_ref=p_ref, dst_ref=all_ref.at[peer], send_sem=send_sems.at[n - 1], recv_sem=recv_sems.at[n - 1],
                device_id=(peer // 4, (peer // 2) % 2, peer % 2), device_id_type=MESH).wait_recv()
        for cp in sent:
            cp.wait_send()
        tot = all_ref[0]
        for d in range(1, N_DEV):
            tot = tot + all_ref[d]
        g = tot[:SMALL_ROWS]
        g_ref[...] = g
        d_ref[...], nm_ref[...], nv_ref[...] = _adamw(w_ref[...], g, m_ref[...], v_ref[...])
        loss_ref[...] = jnp.sum(jnp.sum(tot[SMALL_ROWS:], axis=1, keepdims=True), axis=0, keepdims=True)

    vm = pl.BlockSpec(memory_space=pltpu.VMEM)
    small = jax.ShapeDtypeStruct((SMALL_ROWS, LANES), F32)
    return pl.pallas_call(
        body, name="small_step",
        in_specs=[vm] * 4, out_specs=[vm] * 5,
        out_shape=[small] * 4 + [jax.ShapeDtypeStruct((1, 1), F32)],
        scratch_shapes=[pltpu.VMEM((N_DEV, PACK_ROWS, LANES), F32),
                        pltpu.SemaphoreType.DMA((N_DEV - 1,)), pltpu.SemaphoreType.DMA((N_DEV - 1,))],
    )(pack, w, m, v)


LATER_WEIGHTS = tuple(name for name, _, _ in BIG if name != "w_in")


def _layer_step(x, mem, tgt, w_in, shards, vec):
    s = x.shape[0]
    d = D_MODEL
    tm = min(ROW_TILE, s)
    tl = min(WIDE_TILE, s)
    cos2, sin2 = _rope_tables(s)
    xb = _cast_bf16("cast_x", x)
    bf = lambda w: ((s, w), BF16)
    f32 = lambda w: ((s, w), F32)

    w_sb, w_rqk = w_in[:, :OFF_RET_Q], w_in[:, OFF_RET_Q:OFF_RET_V]
    w_rvg, w_gate = w_in[:, OFF_RET_V:OFF_GATE], w_in[:, OFF_GATE:]
    q_scale = lambda width, q_width, scale: jnp.concatenate(
        [jnp.full((1, q_width), scale, F32), jnp.ones((1, width - q_width), F32)], axis=1)
    n_groups = 3 * SB_WIDTH // LANES

    def sb_epi(acc, t, i, j):
        scaled = acc * t[0]
        return [jnp.stack([scaled[:, g * LANES:(g + 1) * LANES] for g in range(n_groups)])], []

    (sb_qkv,) = _mm(
        "in_sb", xb, w_sb, s, 3 * SB_WIDTH, d, tm=tm, tn=3 * SB_WIDTH, tk=d, epi=sb_epi,
        ins=[(q_scale(3 * SB_WIDTH, SB_WIDTH, SB_SCALE), *_rowvec(3 * SB_WIDTH))],
        outs=[((n_groups, s, LANES), BF16, (n_groups, tm, LANES), lambda i, j: (0, i, 0))])

    def rope_epi(acc, t, i, j):
        cos, sin, scale = t
        parts = []
        for g in range(acc.shape[1] // RET_QK):
            xg = acc[:, g * RET_QK:(g + 1) * RET_QK]
            parts.append(xg * cos + _swap_halves(xg) * sin)
        return [jnp.concatenate(parts, axis=1) * scale], []

    rope_in = ((tm, RET_QK), lambda i, j: (i, 0))
    (rqk,) = _mm("in_rqk", xb, w_rqk, s, 2 * RET_QK_WIDTH, d, tm=tm, tn=2 * RET_QK_WIDTH, tk=d, epi=rope_epi,
                 chunk=MXU_COLS,
                 ins=[(cos2, *rope_in), (sin2, *rope_in),
                      (q_scale(2 * RET_QK_WIDTH, RET_QK_WIDTH, RET_SCALE), *_rowvec(2 * RET_QK_WIDTH))],
                 outs=[(*f32(2 * RET_QK_WIDTH), *_tile(tm, 2 * RET_QK_WIDTH))])
    (rvg,) = _mm("in_rvg", xb, w_rvg, s, 2 * RET_V_WIDTH, d, tm=tm, tn=2 * RET_V_WIDTH, tk=d,
                 epi=_plain, outs=[(*bf(2 * RET_V_WIDTH), *_tile(tm, 2 * RET_V_WIDTH))])
    (gates,) = _mm("in_gate", xb, w_gate, s, 2 * d, d, tm=tm, tn=2 * d, tk=d, chunk=MXU_COLS,
                   epi=lambda acc, t, i, j: ([_sigmoid(acc + t[0])], []),
                   ins=[(vec["b_gate"], *_rowvec(2 * d))], outs=[(*bf(2 * d), *_tile(tm, 2 * d))])

    sb_out, sb_out_f32, *gathered = _sb_fwd(sb_qkv, s, comm=_gather_plan(LATER_WEIGHTS, shards))
    wt = dict(zip(LATER_WEIGHTS, gathered, strict=True))
    ret, gated = _ret_fwd(rqk, rvg, s)
    (y_sb,) = _mm("sb_o", sb_out, wt["w_sb_o"], s, d, SB_WIDTH, tm=tl, tn=d, tk=SB_WIDTH, epi=_plain,
                  outs=[(*bf(d), *_tile(tl, d))])
    y_ret, mixin = _mm(
        "ret_o", gated, wt["w_ret_o"], s, d, RET_V_WIDTH, tm=tl, tn=d, tk=RET_V_WIDTH, chunk=MXU_COLS,
        epi=lambda acc, t, i, j: ([acc, t[0].astype(F32) * t[2].astype(F32) + t[1].astype(F32) * acc], []),
        ins=[(gates, *_tile(tl, d)), (gates, *_tile(tl, d, 1)), (y_sb, *_tile(tl, d))],
        outs=[(*bf(d), *_tile(tl, d)), (*bf(d), *_tile(tl, d))])

    def ln_epi(acc, t, i, j):
        *res, g, b = t
        prev = res[0] if len(res) == 1 else res[0] * res[1] + res[2]
        xhat, rstd = _norm(DN_ALPHA * prev + acc)
        return [xhat * g + b, xhat, rstd], []

    full = _tile(tm, d)
    col1 = ((tm, 1), lambda i, j: (i, 0))
    vec_in = lambda name: (vec[name], *_rowvec(d))
    ln_outs = [(*bf(d), *full), (*f32(d), *full), ((s, 1), F32, *col1)]
    x1b, xhat1, rstd1 = _mm(
        "mix_o", mixin, wt["w_mix_o"], s, d, d, tm=tm, tn=d, tk=d, epi=ln_epi,
        ins=[(x, *full), vec_in("ln1_g"), vec_in("ln1_b")], outs=ln_outs)

    (qm,) = _mm("mem_q", x1b, wt["w_mem_q"], s, d, d, tm=tl, tn=d, tk=d,
                epi=lambda acc, t, i, j: ([acc * MEM_SCALE], []), outs=[(*bf(d), *_tile(tl, d))])
    (kv,) = _mm("mem_kv", mem, wt["w_mem_kv"], MEM_LEN, 2 * d, d, tm=MEM_LEN, tn=d, tk=d, epi=_plain,
                outs=[((MEM_LEN, 2 * d), BF16, *_tile(MEM_LEN, d))])
    att = _xattn_fwd(qm, kv, s)
    x2b, xhat2, rstd2 = _mm(
        "mem_o", att, wt["w_mem_o"], s, d, d, tm=tm, tn=d, tk=d, epi=ln_epi,
        ins=[(xhat1, *full), vec_in("ln1_g"), vec_in("ln1_b"), vec_in("ln2_g"), vec_in("ln2_b")], outs=ln_outs)

    fh = FFN_HIDDEN
    tf = fh // 2
    (f1,) = _mm("ffn_in1", x2b, wt["w_ffn_in"], s, fh, d, tm=tl, tn=tf, tk=d, epi=_plain, j_outer=True,
                outs=[(*bf(fh), *_tile(tl, tf))])

    def swiglu_epi(acc, t, i, j):
        a = t[0].astype(F32)
        return [acc, a * _sigmoid(a) * acc], []

    f2, act = _mm(
        "ffn_in2", x2b, wt["w_ffn_in"], s, fh, d, tm=tm, tn=fh, tk=d, b_off=(0, 1), epi=swiglu_epi, chunk=MXU_COLS,
        ins=[(f1, *_tile(tm, fh))], outs=[(*bf(fh), *_tile(tm, fh)), (*bf(fh), *_tile(tm, fh))])

    def head_epi(acc, t, i, j):
        prev_hat, prev_g, prev_b, g, b, target = t
        xhat, rstd = _norm(DN_ALPHA * (prev_hat * prev_g + prev_b) + acc)
        err = xhat * g + b - target
        dy = err * (1.0 / d)
        du = _norm_bwd(dy * g, xhat, rstd)
        return [du, du], [_colsum(dy * xhat), _colsum(dy), _colsum(err * err) * (0.5 / d)]

    vec_acc = ((1, d), F32)
    du3, du3b, dg3, db3, loss_cols = _mm(
        "ffn_out", act, wt["w_ffn_out"], s, d, fh, tm=tm, tn=d, tk=fh, epi=head_epi,
        ins=[(xhat2, *full), vec_in("ln2_g"), vec_in("ln2_b"), vec_in("ln3_g"), vec_in("ln3_b"), (tgt, *full)],
        outs=[(*f32(d), *full), (*bf(d), *full)], accs=[vec_acc] * 3)

    grads = {}
    ts = min(SEQ_TILE, s)

    def wgrad(name, a, b, m, n, tm_, tn_, tk_=None):
        (g,) = _mm(name, a, b, m, n, a.shape[0], tm=tm_, tn=tn_, tk=tk_ or ts, ta=True, epi=_plain,
                   outs=[((m, n), BF16, *_tile(tm_, tn_))])
        return g

    def ffn_bwd_epi(acc, t, i, j):
        a, b = t[0].astype(F32), t[1].astype(F32)
        sg = _sigmoid(a)
        return [acc * b * (sg * (1.0 + a * (1.0 - sg))), acc * (a * sg)], []

    df1, df2 = _mm(
        "ffn_out_t", du3b, wt["w_ffn_out"], s, fh, d, tm=tm, tn=fh, tk=d, tb=True, epi=ffn_bwd_epi, chunk=MXU_COLS,
        ins=[(f1, *_tile(tm, fh)), (f2, *_tile(tm, fh))],
        outs=[(*bf(fh), *_tile(tm, fh)), (*bf(fh), *_tile(tm, fh))])
    grads["w_ffn_out"] = wgrad("g_ffn_out", act, du3b, fh, d, tf, d)
    grads["w_ffn_in"] = jnp.concatenate(
        [wgrad("g_ffn_in1", x2b, df1, d, fh, d, tf), wgrad("g_ffn_in2", x2b, df2, d, fh, d, tf)], axis=1)
    (dx2a,) = _mm("ffn_in1_t", df1, wt["w_ffn_in"], s, d, fh, tm=tm, tn=d, tk=fh, tb=True, epi=_plain,
                  outs=[(*f32(d), *full)])

    def ln_bwd(name, a, b, k, tk, b_off, more, scales, xhat, rstd, g):
        def epi(acc, t, i, j):
            *extra, xh, rs, gg = t
            dy = acc
            for e, sc in zip(extra, scales, strict=True):
                dy = dy + e * sc
            du = _norm_bwd(dy * gg, xh, rs)
            return [du, du], [_colsum(dy * xh), _colsum(dy)]

        return _mm(name, a, b, s, d, k, tm=tm, tn=d, tk=tk, tb=True, b_off=b_off, epi=epi,
                   ins=[(e, *full) for e in more] + [(xhat, *full), (rstd, *col1), (g, *_rowvec(d))],
                   outs=[(*f32(d), *full), (*bf(d), *full)], accs=[vec_acc] * 2)

    du2, du2b, dg2, db2 = ln_bwd("ffn_in2_t", df2, wt["w_ffn_in"], fh, fh, (0, 1), [dx2a, du3], [1.0, DN_ALPHA],
                                 xhat2, rstd2, vec["ln2_g"])

    (datt,) = _mm("mem_o_t", du2b, wt["w_mem_o"], s, d, d, tm=tl, tn=d, tk=d, tb=True, epi=_plain,
                  outs=[(*bf(d), *_tile(tl, d))])
    grads["w_mem_o"] = wgrad("g_mem_o", att, du2b, d, d, d, d)
    dqm, dkv = _xattn_bwd(qm, kv, datt, s)
    grads["w_mem_q"] = wgrad("g_mem_q", x1b, dqm, d, d, d, d)
    grads["w_mem_kv"] = wgrad("g_mem_kv", mem, dkv, d, 2 * d, d, d, MEM_LEN)
    du1, du1b, dg1, db1 = ln_bwd("mem_q_t", dqm, wt["w_mem_q"], d, d, (0, 0), [du2], [DN_ALPHA],
                                 xhat1, rstd1, vec["ln1_g"])

    def merge_bwd_epi(acc, t, i, j):
        g0, g1, ysb, yret = (v.astype(F32) for v in t)
        dgate0 = acc * ysb * (g0 * (1.0 - g0))
        dgate1 = acc * yret * (g1 * (1.0 - g1))
        return [dgate0, dgate1, acc * g0, acc * g1], [_colsum(dgate0), _colsum(dgate1)]

    dgate0, dgate1, dy_sb, dy_ret, dbg0, dbg1 = _mm(
        "mix_o_t", du1b, wt["w_mix_o"], s, d, d, tm=tm, tn=d, tk=d, tb=True, epi=merge_bwd_epi, chunk=MXU_COLS,
        ins=[(gates, *full), (gates, *_tile(tm, d, 1)), (y_sb, *full), (y_ret, *full)],
        outs=[(*bf(d), *full)] * 4, accs=[vec_acc] * 2)
    grads["w_mix_o"] = wgrad("g_mix_o", mixin, du1b, d, d, d, d)
    grads["w_sb_o"] = wgrad("g_sb_o", sb_out, dy_sb, SB_WIDTH, d, SB_WIDTH, d)
    grads["w_ret_o"] = wgrad("g_ret_o", gated, dy_ret, RET_V_WIDTH, d, RET_V_WIDTH, d)
    (dsb_out,) = _mm("sb_o_t", dy_sb, wt["w_sb_o"], s, SB_WIDTH, d, tm=tl, tn=SB_WIDTH, tk=d, tb=True, epi=_plain,
                     outs=[(*bf(SB_WIDTH), *_tile(tl, SB_WIDTH))])

    def gate_norm_bwd_epi(acc, t, i, j):
        r, g = t[0], t[1].astype(F32)
        drg, dret = [], []
        for h in range(acc.shape[1] // RET_V):
            sl = slice(h * RET_V, (h + 1) * RET_V)
            xhat, rstd = _norm(r[:, sl])
            gg, dd = g[:, sl], acc[:, sl]
            sg = _sigmoid(gg)
            drg.append(dd * xhat * (sg * (1.0 + gg * (1.0 - sg))))
            dret.append(_norm_bwd(dd * (gg * sg), xhat, rstd))
        return [jnp.concatenate(drg, axis=1), jnp.concatenate(dret, axis=1)], []

    drg, dret = _mm(
        "ret_o_t", dy_ret, wt["w_ret_o"], s, RET_V_WIDTH, d, tm=tm, tn=d, tk=d, tb=True, epi=gate_norm_bwd_epi,
        chunk=MXU_COLS,
        ins=[(ret, *full), (rvg, *_tile(tm, d, 1))],
        outs=[(*bf(RET_V_WIDTH), *full)] * 2)

    drq = _ret_bwd_q(rqk, rvg, dret, cos2, sin2, s)
    drk, drv = _ret_bwd_kv(rqk, rvg, dret, cos2, sin2, s)
    dsq, dsk, dsv, *stacked = _sb_bwd(sb_qkv, sb_out_f32, dsb_out, s, comm=_exchange_plan(LATER_WEIGHTS, grads))
    stacks = dict(zip(LATER_WEIGHTS, stacked, strict=True))

    dh = jnp.concatenate([dsq, dsk, dsv, drq, drk, drv, drg, dgate0, dgate1], axis=1)
    grads["w_in"] = wgrad("g_in", xb, dh, d, IN_WIDTH, d, IN_WIDTH // N_CHIPS)
    grad_x, stacks["w_in"] = _mm(
        "in_t", dh, w_in, s, d, IN_WIDTH, tm=tl, tn=d, tk=IN_WIDTH // N_CHIPS, tb=True,
        epi=lambda acc, t, i, j: ([acc + DN_ALPHA * t[0]], []),
        ins=[(du1, *_tile(tl, d))], outs=[(*f32(d), *_tile(tl, d))], comm=_exchange_plan(("w_in",), grads))

    small = {"b_gate": jnp.concatenate([dbg0, dbg1], axis=1), "ln1_g": dg1, "ln1_b": db1, "ln2_g": dg2,
             "ln2_b": db2, "ln3_g": dg3, "ln3_b": db3}
    return grad_x, stacks, small, loss_cols


def kernel(x, mem, w_in, b_gate, w_sb_o, w_ret_o, w_mix_o, ln1_g, ln1_b, w_mem_q, w_mem_kv, w_mem_o, ln2_g, ln2_b, w_ffn_in, w_ffn_out, ln3_g, ln3_b, loss_target, m_w_in, m_b_gate, m_w_sb_o, m_w_ret_o, m_w_mix_o, m_ln1_g, m_ln1_b, m_w_mem_q, m_w_mem_kv, m_w_mem_o, m_ln2_g, m_ln2_b, m_w_ffn_in, m_w_ffn_out, m_ln3_g, m_ln3_b, v_w_in, v_b_gate, v_w_sb_o, v_w_ret_o, v_w_mix_o, v_ln1_g, v_ln1_b, v_w_mem_q, v_w_mem_kv, v_w_mem_o, v_ln2_g, v_ln2_b, v_w_ffn_in, v_w_ffn_out, v_ln3_g, v_ln3_b):
    given = dict(locals())
    s = x.shape[1]
    x2d = x.reshape(s, D_MODEL)
    tgt = loss_target.reshape(s, D_MODEL)
    mem2d = mem.reshape(MEM_LEN, D_MODEL)
    shard = {name: given[name].reshape(_shard_shape(shape, axis)) for name, shape, axis in BIG}
    vec = {name: given[name] for name in SMALL}

    shards_bf = {name: _cast_bf16("cast_" + name, shard[name]) for name, _, _ in BIG}
    (w_in_full,) = _run_plan("gather_w_in", _gather_plan(("w_in",), shards_bf))

    grad_x, stacks, small, loss_cols = _layer_step(x2d, mem2d, tgt, w_in_full, shards_bf, vec)

    out = {}
    for name, shape, axis in BIG:
        stack = stacks[name]
        shp = given[name].shape
        res = _reduce_adamw("adamw_" + name, stack, shard[name], given["m_" + name].reshape(stack.shape[1:]),
                            given["v_" + name].reshape(stack.shape[1:]))
        out[name] = [r.reshape(shp) for r in res]

    pack = jnp.concatenate([small[name] for name in SMALL] + [loss_cols], axis=1).reshape(PACK_ROWS, LANES)
    cat = lambda pre: jnp.concatenate([given[pre + name] for name in SMALL], axis=1).reshape(SMALL_ROWS, LANES)
    *res, loss = _small_step(pack, cat(""), cat("m_"), cat("v_"))
    flat = [r.reshape(1, SMALL_LEN) for r in res]
    off = 0
    for name in SMALL:
        n = given[name].shape[1]
        out[name] = [r[:, off:off + n] for r in flat]
        off += n

    return (loss.reshape(()), grad_x.reshape(x.shape),
            *[out[name][0] for name in WEIGHT_ORDER], *[out[name][1] for name in WEIGHT_ORDER],
            *[out[name][2] for name in WEIGHT_ORDER], *[out[name][3] for name in WEIGHT_ORDER])
```

```python
import functools

import jax
import jax.numpy as jnp
import numpy as np
from jax import lax
from jax.experimental import pallas as pl
from jax.experimental.pallas import tpu as pltpu

F32, BF16 = jnp.float32, jnp.bfloat16
MESH = pl.DeviceIdType.MESH

D_MODEL = 1024
MEM_LEN = 256
SB_HEADS, SB_DIM, SB_WIDTH = 8, 64, 512
RET_HEADS, RET_QK, RET_V = 4, 128, 256
RET_QK_WIDTH, RET_V_WIDTH = 512, 1024
ROPE_BASE = 10000.0
MEM_HEADS, MEM_DIM = 4, 256
FFN_HIDDEN = 2816
IN_WIDTH = 6656
OFF_RET_Q, OFF_RET_V, OFF_RET_G, OFF_GATE = 1536, 2560, 3584, 4608
DN_ALPHA = 2.0 ** 0.25
LN_EPS = 1e-5
SB_SCALE = SB_DIM ** -0.5
SB_DEAD = -110.0
RET_SCALE = RET_QK ** -0.5
MEM_SCALE = MEM_DIM ** -0.5
ADAM_LR, ADAM_B1, ADAM_B2, ADAM_EPS, ADAM_WD, ADAM_STEP = 0.001, 0.9, 0.999, 1e-08, 0.01, 10

N_DEV, N_CHIPS = 8, 4

LANES = 128
MXU_COLS = 256
VMEM_LIMIT_BYTES = 52 * 2 ** 20
ROW_TILE = 512
WIDE_TILE = 1024
SEQ_TILE = 1024
SB_BLOCK = 256
RET_BLOCK = 256
XATTN_ROWS = 512

BIG = (
    ("w_in", (D_MODEL, IN_WIDTH), 1),
    ("w_sb_o", (SB_WIDTH, D_MODEL), 1),
    ("w_ret_o", (RET_V_WIDTH, D_MODEL), 0),
    ("w_mix_o", (D_MODEL, D_MODEL), 0),
    ("w_mem_q", (D_MODEL, D_MODEL), 0),
    ("w_mem_kv", (D_MODEL, 2 * D_MODEL), 1),
    ("w_mem_o", (D_MODEL, D_MODEL), 0),
    ("w_ffn_in", (D_MODEL, 2 * FFN_HIDDEN), 1),
    ("w_ffn_out", (FFN_HIDDEN, D_MODEL), 0),
)
SMALL = ("b_gate", "ln1_g", "ln1_b", "ln2_g", "ln2_b", "ln3_g", "ln3_b")
SMALL_LEN = 2 * D_MODEL + 6 * D_MODEL
SMALL_ROWS = SMALL_LEN // LANES
PACK_ROWS = SMALL_ROWS + D_MODEL // LANES
WEIGHT_ORDER = ("w_in", "b_gate", "w_sb_o", "w_ret_o", "w_mix_o", "ln1_g", "ln1_b", "w_mem_q", "w_mem_kv",
                "w_mem_o", "ln2_g", "ln2_b", "w_ffn_in", "w_ffn_out", "ln3_g", "ln3_b")


def _cparams():
    return pltpu.CompilerParams(vmem_limit_bytes=VMEM_LIMIT_BYTES)


def _dot(a, b, ca, cb):
    return lax.dot_general(a, b, (((ca,), (cb,)), ((), ())), preferred_element_type=F32)


def _sigmoid(x):
    return 1.0 / (1.0 + jnp.exp(-x))


def _mm(name, a, b, m, n, k, *, tm, tn, tk, epi, outs, ins=(), accs=(), ta=False, tb=False,
        a_off=(0, 0), b_off=(0, 0), j_outer=False, comm=None, chunk=None):
    assert m % tm == 0 and n % tn == 0 and k % tk == 0, (name, m, n, k, tm, tn, tk)
    assert chunk is None or (k == tk and tn % chunk == 0), name
    ni, nj, nk = m // tm, n // tn, k // tk
    assert not accs or nj == 1, name
    ij = (lambda g0, g1: (g1, g0)) if j_outer else (lambda g0, g1: (g0, g1))

    def spec(block, index):
        return pl.BlockSpec(block, lambda g0, g1, kk: index(*ij(g0, g1), kk))

    if ta:
        a_spec = spec((tk, tm), lambda i, j, kk: (kk + a_off[0], i + a_off[1]))
    else:
        a_spec = spec((tm, tk), lambda i, j, kk: (i + a_off[0], kk + a_off[1]))
    if tb:
        b_spec = spec((tn, tk), lambda i, j, kk: (j + b_off[0], kk + b_off[1]))
    else:
        b_spec = spec((tk, tn), lambda i, j, kk: (kk + b_off[0], j + b_off[1]))
    in_specs = [a_spec, b_spec]
    for _, bs, im in ins:
        in_specs.append(spec(bs, lambda i, j, kk, im=im: im(i, j)))
    out_specs, out_shape = [], []
    for shape, dtype, bs, im in outs:
        out_specs.append(spec(bs, lambda i, j, kk, im=im: im(i, j)))
        out_shape.append(jax.ShapeDtypeStruct(shape, dtype))
    for shape, dtype in accs:
        out_specs.append(spec(shape, lambda i, j, kk, nd=len(shape): (0,) * nd))
        out_shape.append(jax.ShapeDtypeStruct(shape, dtype))
    n_in, n_out, n_acc = len(ins), len(outs), len(accs)
    ca, cb = (0 if ta else 1), (1 if tb else 0)
    grid = (*ij(ni, nj), nk)
    comm_ins, comm_outs, comm_scratch = [], [], []
    if comm is not None:
        comm_in_specs, comm_out_specs = comm.specs
        comm_ins, comm_outs, comm_scratch = list(comm.ins), list(comm.out_shape), list(comm.scratch)
        in_specs += comm_in_specs
        out_specs += comm_out_specs
        out_shape += comm_outs
    n_ci, n_co = len(comm_ins), len(comm_outs)

    def body(*refs):
        a_ref, b_ref = refs[:2]
        in_refs = refs[2:2 + n_in]
        ci_refs = refs[2 + n_in:2 + n_in + n_ci]
        rest = refs[2 + n_in + n_ci:]
        out_refs, acc_refs = rest[:n_out], rest[n_out:n_out + n_acc]
        co_refs = rest[n_out + n_acc:n_out + n_acc + n_co]
        scratch = rest[n_out + n_acc + n_co:]
        sem_refs, scratch = scratch[:len(comm_scratch)], scratch[len(comm_scratch):]
        (i, j), kk = ij(pl.program_id(0), pl.program_id(1)), pl.program_id(2)
        if comm is not None:
            first_step, last_step = _grid_ends(grid)
            pl.when(first_step)(lambda: comm.start(ci_refs, co_refs, sem_refs))
        def finish(acc, cols=slice(None)):
            def of(r):
                return r[..., cols] if r.shape[-1] == tn else r[...]

            o_tiles, a_tiles = epi(acc, [of(r) for r in in_refs], i, j)
            for r, t in zip(out_refs, o_tiles, strict=True):
                r[..., cols] = t.astype(r.dtype)
            if n_acc:
                @pl.when(i == 0)
                def _():
                    for r, t in zip(acc_refs, a_tiles, strict=True):
                        r[..., cols] = t

                @pl.when(i > 0)
                def _():
                    for r, t in zip(acc_refs, a_tiles, strict=True):
                        r[..., cols] += t

        if chunk is not None:
            a_tile = a_ref[...].astype(BF16)
            for c0 in range(0, tn, chunk):
                cols = slice(c0, c0 + chunk)
                b_part = b_ref[cols, :] if tb else b_ref[:, cols]
                finish(_dot(a_tile, b_part.astype(BF16), ca, cb), cols)
            if comm is not None:
                pl.when(last_step)(lambda: comm.finish(ci_refs, co_refs, sem_refs))
            return

        part = _dot(a_ref[...].astype(BF16), b_ref[...].astype(BF16), ca, cb)
        if nk == 1:
            finish(part)
        else:
            acc_ref = scratch[0]

            @pl.when(kk == 0)
            def _():
                acc_ref[...] = part

            @pl.when(kk > 0)
            def _():
                acc_ref[...] += part

            @pl.when(kk == nk - 1)
            def _():
                finish(acc_ref[...])

        if comm is not None:
            pl.when(last_step)(lambda: comm.finish(ci_refs, co_refs, sem_refs))

    res = pl.pallas_call(
        body, name=name, grid=grid, in_specs=in_specs, out_specs=out_specs, out_shape=out_shape,
        scratch_shapes=comm_scratch + ([pltpu.VMEM((tm, tn), F32)] if nk > 1 else []),
        compiler_params=_cparams(),
    )(a, b, *[x for x, _, _ in ins], *comm_ins)
    return res


def _grid_ends(grid):
    ids = [pl.program_id(ax) for ax in range(len(grid))]
    first = functools.reduce(jnp.logical_and, [p == 0 for p in ids])
    last = functools.reduce(jnp.logical_and, [p == n - 1 for p, n in zip(ids, grid, strict=True)])
    return first, last


def _tile(tm, tn, dj=0):
    return (tm, tn), (lambda i, j: (i, j + dj))


def _rowvec(tn, dj=0):
    return (1, tn), (lambda i, j: (0, j + dj))


def _plain(acc, tiles, i, j):
    return [acc], []


def _ew(name, fn, ins, outs, rows, tr):
    assert rows % tr == 0, (name, rows, tr)
    in_specs = []
    for x in ins:
        if x.shape[0] == rows:
            in_specs.append(pl.BlockSpec((tr, x.shape[1]), lambda i: (i, 0)))
        else:
            in_specs.append(pl.BlockSpec(x.shape, lambda i: (0, 0)))
    n_in = len(ins)

    def body(*refs):
        res = fn(*[r[...] for r in refs[:n_in]])
        for r, t in zip(refs[n_in:], res, strict=True):
            r[...] = t.astype(r.dtype)

    return pl.pallas_call(
        body, name=name, grid=(rows // tr,), in_specs=in_specs,
        out_specs=[pl.BlockSpec((tr, w), lambda i: (i, 0)) for w, _ in outs],
        out_shape=[jax.ShapeDtypeStruct((rows, w), dt) for w, dt in outs],
        compiler_params=_cparams(),
    )(*ins)


def _cast_bf16(name, x):
    rows = x.shape[0]
    tr = next(t for t in (512, 256, 64) if rows % t == 0)
    return _ew(name, lambda v: (v,), [x], [(x.shape[1], BF16)], rows, tr)[0]


def _prep(x, comm):
    s = x.shape[0]
    half = RET_QK // 2
    inv = 1.0 / (ROPE_BASE ** (jnp.arange(half, dtype=F32) / half))
    inv2 = jnp.concatenate([inv, inv]).reshape(1, RET_QK)
    sign = jnp.concatenate([-jnp.ones((half,), F32), jnp.ones((half,), F32)]).reshape(1, RET_QK)
    tr = min(ROW_TILE, s)
    grid = (s // tr,)
    c_in_specs, c_out_specs, c_out_shape, c_scratch, c_ins, split = _host(comm, 3, 3)

    def body(*refs):
        (x_ref, inv_ref, sign_ref), (xb_ref, cos_ref, sin_ref), _, riding = split(refs)
        i = pl.program_id(0)
        first_step, last_step = _grid_ends(grid)
        pl.when(first_step)(lambda: comm.start(*riding))
        xb_ref[...] = x_ref[...].astype(BF16)
        pos = (lax.broadcasted_iota(jnp.int32, (tr, RET_QK), 0) + i * tr).astype(F32)
        ang = pos * inv_ref[...]
        cos_ref[...] = jnp.cos(ang)
        sin_ref[...] = jnp.sin(ang) * sign_ref[...]
        pl.when(last_step)(lambda: comm.finish(*riding))

    vec = pl.BlockSpec((1, RET_QK), lambda i: (0, 0))
    row = lambda w: pl.BlockSpec((tr, w), lambda i: (i, 0))
    return pl.pallas_call(
        body, name="prep", grid=grid,
        in_specs=[row(D_MODEL), vec, vec] + c_in_specs,
        out_specs=[row(D_MODEL), row(RET_QK), row(RET_QK)] + c_out_specs,
        out_shape=[jax.ShapeDtypeStruct((s, D_MODEL), BF16), jax.ShapeDtypeStruct((s, RET_QK), F32),
                   jax.ShapeDtypeStruct((s, RET_QK), F32)] + c_out_shape,
        scratch_shapes=c_scratch, compiler_params=_cparams(),
    )(x, inv2, sign, *c_ins)


def _swap_halves(x):
    return pltpu.roll(x, RET_QK // 2, 1)


def _norm(u):
    mu = jnp.mean(u, axis=-1, keepdims=True)
    d = u - mu
    var = jnp.mean(d * d, axis=-1, keepdims=True)
    rstd = lax.rsqrt(var + LN_EPS)
    return d * rstd, rstd


def _norm_bwd(dxh, xhat, rstd):
    m1 = jnp.mean(dxh, axis=-1, keepdims=True)
    m2 = jnp.mean(dxh * xhat, axis=-1, keepdims=True)
    return rstd * (dxh - m1 - xhat * m2)


def _colsum(t):
    return jnp.sum(t, axis=0, keepdims=True)


def _split_mm(t, tri):
    hi = t.astype(BF16)
    lo = (t - hi.astype(F32)).astype(BF16)
    return _dot(hi, tri, 1, 0) + _dot(lo, tri, 1, 0)


def _sb_masks():
    t = SB_BLOCK
    lane = lax.broadcasted_iota(jnp.int32, (1, LANES), 1)
    first = lane < SB_DIM
    m0 = jnp.where(first, 1.0, 0.0).astype(BF16)
    m1 = jnp.where(first, 0.0, 1.0).astype(BF16)
    row = lax.broadcasted_iota(jnp.int32, (t, t), 0)
    col = lax.broadcasted_iota(jnp.int32, (t, t), 1)
    return first, (m0, m1), row, col


def _sb_logits(qh, k, causal):
    z = _dot(qh, k, 1, 1)
    lp = jnp.log(1.0 + jnp.exp(-jnp.abs(z)))
    a = jnp.minimum(z, 0.0) - lp
    r = jnp.minimum(-z, 0.0) - lp
    if causal is not None:
        r = jnp.where(causal, r, 0.0)
    return a, r


def _sb_walk(i, blocks, l_ref, causal):
    pl.when(i == 0)(lambda: blocks([(i, causal)]))
    pl.when(i > 0)(lambda: blocks([(i, causal), (i - 1, None)]))

    def alive():
        top = jnp.max(jnp.maximum(l_ref[0], l_ref[1]))
        return jnp.where(top > SB_DEAD, 1, 0)

    def cond(c):
        return jnp.logical_and(c[0] < i, c[1] > 0)

    def step(c):
        blocks([(i - 1 - c[0], None)])
        return c[0] + 1, alive()

    lax.while_loop(cond, step, (jnp.int32(1), alive()))


def _host(comm, n_in, n_out):
    if comm is None:
        return [], [], [], [], [], lambda refs: (refs[:n_in], refs[n_in:n_in + n_out], refs[n_in + n_out:], None)
    in_specs, out_specs = comm.specs
    n_ci, n_co, n_sem = len(comm.ins), len(comm.out_shape), len(comm.scratch)

    def split(refs):
        ins, ci = refs[:n_in], refs[n_in:n_in + n_ci]
        rest = refs[n_in + n_ci:]
        outs, co = rest[:n_out], rest[n_out:n_out + n_co]
        sems, scratch = rest[n_out + n_co:n_out + n_co + n_sem], rest[n_out + n_co + n_sem:]
        return ins, outs, scratch, (ci, co, sems)

    return in_specs, out_specs, list(comm.out_shape), list(comm.scratch), list(comm.ins), split


def _sb_qkv_specs(s):
    pairs = SB_HEADS // 2
    return [pl.BlockSpec((None, SB_BLOCK, LANES), lambda p, i: (p, i, 0)),
            pl.BlockSpec((None, s, LANES), lambda p, i: (pairs + p, 0, 0)),
            pl.BlockSpec((None, s, LANES), lambda p, i: (2 * pairs + p, 0, 0))]


def _sb_fwd(qkv, s, comm=None):
    t = SB_BLOCK
    nq = s // t
    grid = (SB_HEADS // 2, nq)
    c_in_specs, c_out_specs, c_out_shape, c_scratch, c_ins, split = _host(comm, 3, 2)

    def body(*refs):
        (q_ref, k_ref, v_ref), (o_ref, of_ref), (l_ref, acc_ref), riding = split(refs)
        i = pl.program_id(1)
        if comm is not None:
            first_step, last_step = _grid_ends(grid)
            pl.when(first_step)(lambda: comm.start(*riding))
        first, hmask, row, col = _sb_masks()
        after = jnp.where(row > col, 1.0, 0.0).astype(BF16)
        causal = col < row
        q = q_ref[...]
        qh = (q * hmask[0], q * hmask[1])
        l_ref[...] = jnp.zeros_like(l_ref)
        acc_ref[...] = jnp.zeros_like(acc_ref)

        def blocks(todo):
            chains = [(b, h) for b in range(len(todo)) for h in range(2)]
            starts = [pl.multiple_of(kb * t, t) for kb, _ in todo]
            ks = [k_ref[pl.ds(st, t), :] for st in starts]
            vs = [v_ref[pl.ds(st, t), :] for st in starts]
            ar = {(b, h): _sb_logits(qh[h], ks[b], todo[b][1]) for b, h in chains}
            later = {bh: _split_mm(ar[bh][1], after) for bh in chains}
            carry = [l_ref[0], l_ref[1]]
            w = {}
            for b, (_, mask) in enumerate(todo):
                for h in range(2):
                    wbh = jnp.exp(ar[b, h][0] + later[b, h] + carry[h])
                    w[b, h] = wbh if mask is None else jnp.where(mask, wbh, 0.0)
                carry = [carry[h] + jnp.sum(ar[b, h][1], axis=1, keepdims=True) for h in range(2)]
            pv = {(b, h): _dot(w[b, h].astype(BF16), vs[b], 1, 0) for b, h in chains}
            acc = acc_ref[...]
            for b in range(len(todo)):
                acc = acc + jnp.where(first, pv[b, 0], pv[b, 1])
            acc_ref[...] = acc
            l_ref[0], l_ref[1] = carry

        _sb_walk(i, blocks, l_ref, causal)
        o_ref[...] = acc_ref[...].astype(o_ref.dtype)
        of_ref[...] = acc_ref[...]
        if comm is not None:
            pl.when(last_step)(lambda: comm.finish(*riding))

    blk = pl.BlockSpec((t, LANES), lambda p, i: (i, p))
    return pl.pallas_call(
        body, name="sb_fwd", grid=grid,
        in_specs=_sb_qkv_specs(s) + c_in_specs,
        out_specs=[blk, blk] + c_out_specs,
        out_shape=[jax.ShapeDtypeStruct((s, SB_WIDTH), BF16), jax.ShapeDtypeStruct((s, SB_WIDTH), F32)] + c_out_shape,
        scratch_shapes=c_scratch + [pltpu.VMEM((2, t, 1), F32), pltpu.VMEM((t, LANES), F32)],
        compiler_params=_cparams(),
    )(qkv, qkv, qkv, *c_ins)


def _sb_bwd(qkv, o, do, s, comm=None):
    t = SB_BLOCK
    nq = s // t
    grid = (SB_HEADS // 2, nq)
    c_in_specs, c_out_specs, c_out_shape, c_scratch, c_ins, split = _host(comm, 5, 3)

    def body(*refs):
        ((q_ref, k_ref, v_ref, o_ref, do_ref), (dq_ref, dk_ref, dv_ref),
         (l_ref, e_ref, dq_acc, dk_acc, dv_acc), riding) = split(refs)
        i = pl.program_id(1)
        if comm is not None:
            first_step, last_step = _grid_ends(grid)
            pl.when(first_step)(lambda: comm.start(*riding))
        first, hmask, row, col = _sb_masks()
        after = jnp.where(row > col, 1.0, 0.0).astype(BF16)
        from_here = jnp.where(row >= col, 1.0, 0.0).astype(BF16)
        causal = col < row

        @pl.when(i == 0)
        def _():
            dk_acc[...] = jnp.zeros_like(dk_acc)
            dv_acc[...] = jnp.zeros_like(dv_acc)

        q = q_ref[...]
        do_ = do_ref[...]
        qh = (q * hmask[0], q * hmask[1])
        doh = (do_ * hmask[0], do_ * hmask[1])
        prod = do_.astype(F32) * o_ref[...]
        total = (jnp.sum(jnp.where(first, prod, 0.0), axis=1, keepdims=True),
                 jnp.sum(jnp.where(first, 0.0, prod), axis=1, keepdims=True))
        l_ref[...] = jnp.zeros_like(l_ref)
        e_ref[...] = jnp.zeros_like(e_ref)
        dq_acc[...] = jnp.zeros_like(dq_acc)

        def blocks(todo):
            chains = [(b, h) for b in range(len(todo)) for h in range(2)]
            starts = [pl.multiple_of(kb * t, t) for kb, _ in todo]
            ks = [k_ref[pl.ds(st, t), :] for st in starts]
            vs = [v_ref[pl.ds(st, t), :] for st in starts]
            ar = {(b, h): _sb_logits(qh[h], ks[b], todo[b][1]) for b, h in chains}
            dw = {(b, h): _dot(doh[h], vs[b], 1, 1) for b, h in chains}
            later = {bh: _split_mm(ar[bh][1], after) for bh in chains}
            carry = [l_ref[0], l_ref[1]]
            wb = {}
            for b, (_, mask) in enumerate(todo):
                for h in range(2):
                    wbh = jnp.exp(ar[b, h][0] + later[b, h] + carry[h])
                    wb[b, h] = (wbh if mask is None else jnp.where(mask, wbh, 0.0)).astype(BF16)
                carry = [carry[h] + jnp.sum(ar[b, h][1], axis=1, keepdims=True) for h in range(2)]
            dvs = {(b, h): _dot(wb[b, h], do_, 0, 0) for b, h in chains}
            e = {bh: dw[bh] * wb[bh].astype(F32) for bh in chains}
            suffix = {bh: _split_mm(e[bh], from_here) for bh in chains}
            e_carry = [e_ref[0], e_ref[1]]
            dz = {}
            for b, (_, mask) in enumerate(todo):
                for h in range(2):
                    before = total[h] - (suffix[b, h] + e_carry[h])
                    dzh = e[b, h] - jnp.exp(ar[b, h][0]) * (e[b, h] + before)
                    dz[b, h] = (dzh if mask is None else jnp.where(mask, dzh, 0.0)).astype(BF16)
                e_carry = [e_carry[h] + jnp.sum(e[b, h], axis=1, keepdims=True) for h in range(2)]
            dqs = {(b, h): _dot(dz[b, h], ks[b], 1, 0) for b, h in chains}
            dks = {(b, h): _dot(dz[b, h], q, 0, 0) for b, h in chains}
            dq = dq_acc[...]
            for b, st in enumerate(starts):
                dq = dq + jnp.where(first, dqs[b, 0], dqs[b, 1])
                dk_acc[pl.ds(st, t), :] += jnp.where(first, dks[b, 0], dks[b, 1])
                dv_acc[pl.ds(st, t), :] += jnp.where(first, dvs[b, 0], dvs[b, 1])
            dq_acc[...] = dq
            l_ref[0], l_ref[1] = carry
            e_ref[0], e_ref[1] = e_carry

        _sb_walk(i, blocks, l_ref, causal)
        dq_ref[...] = (dq_acc[...] * SB_SCALE).astype(dq_ref.dtype)

        @pl.when(i == nq - 1)
        def _():
            dk_ref[...] = dk_acc[...].astype(dk_ref.dtype)
            dv_ref[...] = dv_acc[...].astype(dv_ref.dtype)

        if comm is not None:
            pl.when(last_step)(lambda: comm.finish(*riding))

    blk = pl.BlockSpec((t, LANES), lambda p, i: (i, p))
    col_blk = pl.BlockSpec((s, LANES), lambda p, i: (0, p))
    sds = jax.ShapeDtypeStruct((s, SB_WIDTH), BF16)
    return pl.pallas_call(
        body, name="sb_bwd", grid=grid,
        in_specs=_sb_qkv_specs(s) + [blk, blk] + c_in_specs,
        out_specs=[blk, col_blk, col_blk] + c_out_specs,
        out_shape=[sds, sds, sds] + c_out_shape,
        scratch_shapes=c_scratch + [pltpu.VMEM((2, t, 1), F32), pltpu.VMEM((2, t, 1), F32),
                                    pltpu.VMEM((t, LANES), F32), pltpu.VMEM((s, LANES), F32),
                                    pltpu.VMEM((s, LANES), F32)],
        compiler_params=_cparams(),
    )(qkv, qkv, qkv, o, do, *c_ins)


def _ret_log_gamma():
    lg = np.log1p(-np.exp2(-5.0 - np.arange(RET_HEADS, dtype=np.float32))).astype(np.float32)
    return jnp.asarray(np.broadcast_to(lg[:, None, None], (RET_HEADS, 8, LANES)).copy())


RET_SCRATCH = [pltpu.VMEM((RET_HEADS, RET_QK, RET_V), F32),
               pltpu.VMEM((RET_HEADS, RET_BLOCK, RET_BLOCK), F32),
               pltpu.VMEM((RET_HEADS, RET_BLOCK, 1), F32),
               pltpu.VMEM((RET_HEADS, RET_BLOCK, 1), F32)]


def _ret_begin(n, lg_ref, state, within, q_dec, k_dec):
    @pl.when(n == 0)
    def _():
        c = RET_BLOCK
        state[...] = jnp.zeros_like(state)
        row = lax.broadcasted_iota(jnp.int32, (c, c), 0)
        col = lax.broadcasted_iota(jnp.int32, (c, c), 1)
        rel = jnp.maximum(row - col, 0).astype(F32)
        idx = lax.broadcasted_iota(jnp.int32, (c, 1), 0).astype(F32)
        for h in range(RET_HEADS):
            lg = lg_ref[h, 0:1, 0:1]
            within[h] = jnp.where(row >= col, jnp.exp(lg * rel), 0.0)
            q_dec[h] = jnp.exp(lg * (idx + 1.0))
            k_dec[h] = jnp.exp(lg * (c - 1.0 - idx))


def _chunk_decay(lg_ref, h):
    return jnp.exp(lg_ref[h, 0:1, 0:1] * float(RET_BLOCK))


def _ret_heads(x, width):
    return [x[:, h * width:(h + 1) * width] for h in range(RET_HEADS)]


def _ret_specs(s, reverse=False):
    c = RET_BLOCK
    nc = s // c
    pos = (lambda n: nc - 1 - n) if reverse else (lambda n: n)
    q_spec = pl.BlockSpec((c, RET_QK_WIDTH), lambda n: (pos(n), 0))
    k_spec = pl.BlockSpec((c, RET_QK_WIDTH), lambda n: (pos(n), 1))
    v_spec = pl.BlockSpec((c, RET_V_WIDTH), lambda n: (pos(n), 0))
    lg_spec = pl.BlockSpec((RET_HEADS, 8, LANES), lambda n: (0, 0, 0))
    rope_spec = pl.BlockSpec((c, RET_QK), lambda n: (pos(n), 0))
    return nc, q_spec, k_spec, v_spec, lg_spec, rope_spec


def _ret_fwd(rqk, rvg, s):
    nc, q_spec, k_spec, v_spec, lg_spec, _ = _ret_specs(s)
    g_spec = pl.BlockSpec((RET_BLOCK, RET_V_WIDTH), lambda n: (n, 1))
    heads = range(RET_HEADS)

    def body(q_ref, k_ref, v_ref, g_ref, lg_ref, r_ref, y_ref, state, within, q_dec, k_dec):
        n = pl.program_id(0)
        _ret_begin(n, lg_ref, state, within, q_dec, k_dec)
        q, k = _ret_heads(q_ref[...], RET_QK), _ret_heads(k_ref[...], RET_QK)
        v, g = _ret_heads(v_ref[...], RET_V), _ret_heads(g_ref[...], RET_V)
        scores = [_dot(q[h].astype(BF16), k[h].astype(BF16), 1, 1) * within[h] for h in heads]
        cross = [_dot((q[h] * q_dec[h]).astype(BF16), state[h].astype(BF16), 1, 0) for h in heads]
        out = [_dot(scores[h].astype(BF16), v[h], 1, 0) + cross[h] for h in heads]
        grown = [_dot((k[h] * k_dec[h]).astype(BF16), v[h], 0, 0) for h in heads]
        for h in heads:
            sl = slice(h * RET_V, (h + 1) * RET_V)
            r_ref[:, sl] = out[h]
            xhat, _ = _norm(out[h])
            gh = g[h].astype(F32)
            y_ref[:, sl] = (gh * _sigmoid(gh) * xhat).astype(y_ref.dtype)
            state[h] = state[h] * _chunk_decay(lg_ref, h) + grown[h]

    return pl.pallas_call(
        body, name="ret_fwd", grid=(nc,),
        in_specs=[q_spec, k_spec, v_spec, g_spec, lg_spec],
        out_specs=[v_spec, v_spec],
        out_shape=[jax.ShapeDtypeStruct((s, RET_V_WIDTH), F32), jax.ShapeDtypeStruct((s, RET_V_WIDTH), BF16)],
        scratch_shapes=RET_SCRATCH,
        compiler_params=_cparams(),
    )(rqk, rqk, rvg, rvg, _ret_log_gamma())


def _rope_bwd(d, cos, sin):
    return d * cos + _swap_halves(d * sin)


def _ret_bwd_q(rqk, rv, d_out, cos2, sin2, s):
    nc, q_spec, k_spec, v_spec, lg_spec, rope_spec = _ret_specs(s)
    heads = range(RET_HEADS)

    def body(k_ref, v_ref, d_ref, lg_ref, cos_ref, sin_ref, dq_ref, state, within, q_dec, k_dec):
        n = pl.program_id(0)
        _ret_begin(n, lg_ref, state, within, q_dec, k_dec)
        k = _ret_heads(k_ref[...], RET_QK)
        v, d = _ret_heads(v_ref[...], RET_V), _ret_heads(d_ref[...], RET_V)
        cos, sin = cos_ref[...], sin_ref[...]
        d_scores = [_dot(d[h], v[h], 1, 1) * within[h] for h in heads]
        cross = [q_dec[h] * _dot(d[h], state[h].astype(BF16), 1, 1) for h in heads]
        dq = [_dot(d_scores[h].astype(BF16), k[h].astype(BF16), 1, 0) + cross[h] for h in heads]
        grown = [_dot((k[h] * k_dec[h]).astype(BF16), v[h], 0, 0) for h in heads]
        for h in heads:
            sl = slice(h * RET_QK, (h + 1) * RET_QK)
            dq_ref[:, sl] = (_rope_bwd(dq[h], cos, sin) * RET_SCALE).astype(dq_ref.dtype)
            state[h] = state[h] * _chunk_decay(lg_ref, h) + grown[h]

    return pl.pallas_call(
        body, name="ret_bwd_q", grid=(nc,),
        in_specs=[k_spec, v_spec, v_spec, lg_spec, rope_spec, rope_spec],
        out_specs=q_spec,
        out_shape=jax.ShapeDtypeStruct((s, RET_QK_WIDTH), BF16),
        scratch_shapes=RET_SCRATCH,
        compiler_params=_cparams(),
    )(rqk, rv, d_out, _ret_log_gamma(), cos2, sin2)


def _ret_bwd_kv(rqk, rv, d_out, cos2, sin2, s):
    nc, q_spec, k_spec, v_spec, lg_spec, rope_spec = _ret_specs(s, reverse=True)
    heads = range(RET_HEADS)

    def body(q_ref, k_ref, v_ref, d_ref, lg_ref, cos_ref, sin_ref, dk_ref, dv_ref, state, within, q_dec, k_dec):
        n = pl.program_id(0)
        _ret_begin(n, lg_ref, state, within, q_dec, k_dec)
        q, k = _ret_heads(q_ref[...], RET_QK), _ret_heads(k_ref[...], RET_QK)
        v, d = _ret_heads(v_ref[...], RET_V), _ret_heads(d_ref[...], RET_V)
        cos, sin = cos_ref[...], sin_ref[...]
        qb, kb = [q[h].astype(BF16) for h in heads], [k[h].astype(BF16) for h in heads]
        st = [state[h].astype(BF16) for h in heads]
        scores = [_dot(qb[h], kb[h], 1, 1) * within[h] for h in heads]
        d_scores = [_dot(d[h], v[h], 1, 1) * within[h] for h in heads]
        dk = [_dot(d_scores[h].astype(BF16), qb[h], 0, 0) + k_dec[h] * _dot(v[h], st[h], 1, 1) for h in heads]
        dv = [_dot(scores[h].astype(BF16), d[h], 0, 0) + k_dec[h] * _dot(kb[h], st[h], 1, 0) for h in heads]
        grown = [_dot((q[h] * q_dec[h]).astype(BF16), d[h], 0, 0) for h in heads]
        for h in heads:
            dk_ref[:, h * RET_QK:(h + 1) * RET_QK] = _rope_bwd(dk[h], cos, sin).astype(dk_ref.dtype)
            dv_ref[:, h * RET_V:(h + 1) * RET_V] = dv[h].astype(dv_ref.dtype)
            state[h] = state[h] * _chunk_decay(lg_ref, h) + grown[h]

    return pl.pallas_call(
        body, name="ret_bwd_kv", grid=(nc,),
        in_specs=[q_spec, k_spec, v_spec, v_spec, lg_spec, rope_spec, rope_spec],
        out_specs=[q_spec, v_spec],
        out_shape=[jax.ShapeDtypeStruct((s, RET_QK_WIDTH), BF16), jax.ShapeDtypeStruct((s, RET_V_WIDTH), BF16)],
        scratch_shapes=RET_SCRATCH,
        compiler_params=_cparams(),
    )(rqk, rqk, rv, d_out, _ret_log_gamma(), cos2, sin2)


def _xattn_probs(q, k):
    sc = _dot(q, k, 1, 1)
    sc = sc - jnp.max(sc, axis=-1, keepdims=True)
    p = jnp.exp(sc)
    return p / jnp.sum(p, axis=-1, keepdims=True)


def _xattn_fwd(qm, kv, s):
    tq = XATTN_ROWS

    def body(q_ref, kv_ref, o_ref):
        for h in range(MEM_HEADS):
            sl = slice(h * MEM_DIM, (h + 1) * MEM_DIM)
            sv = slice(D_MODEL + h * MEM_DIM, D_MODEL + (h + 1) * MEM_DIM)
            p = _xattn_probs(q_ref[:, sl], kv_ref[:, sl])
            o_ref[:, sl] = _dot(p.astype(BF16), kv_ref[:, sv], 1, 0).astype(o_ref.dtype)

    return pl.pallas_call(
        body, name="xattn_fwd", grid=(s // tq,),
        in_specs=[pl.BlockSpec((tq, D_MODEL), lambda i: (i, 0)),
                  pl.BlockSpec((MEM_LEN, 2 * D_MODEL), lambda i: (0, 0))],
        out_specs=pl.BlockSpec((tq, D_MODEL), lambda i: (i, 0)),
        out_shape=jax.ShapeDtypeStruct((s, D_MODEL), BF16),
        compiler_params=_cparams(),
    )(qm, kv)


def _xattn_bwd(qm, kv, do, s):
    tq = XATTN_ROWS

    def body(q_ref, kv_ref, do_ref, dq_ref, dkv_ref):
        i = pl.program_id(0)

        @pl.when(i == 0)
        def _():
            dkv_ref[...] = jnp.zeros_like(dkv_ref)

        for h in range(MEM_HEADS):
            sl = slice(h * MEM_DIM, (h + 1) * MEM_DIM)
            sv = slice(D_MODEL + h * MEM_DIM, D_MODEL + (h + 1) * MEM_DIM)
            q, k, v, d = q_ref[:, sl], kv_ref[:, sl], kv_ref[:, sv], do_ref[:, sl]
            p = _xattn_probs(q, k)
            dp = _dot(d, v, 1, 1)
            ds = (p * (dp - jnp.sum(p * dp, axis=-1, keepdims=True))).astype(BF16)
            dq_ref[:, sl] = (_dot(ds, k, 1, 0) * MEM_SCALE).astype(dq_ref.dtype)
            dkv_ref[:, sl] += _dot(ds, q, 0, 0)
            dkv_ref[:, sv] += _dot(p.astype(BF16), d, 0, 0)

    row_blk = pl.BlockSpec((tq, D_MODEL), lambda i: (i, 0))
    kv_blk = pl.BlockSpec((MEM_LEN, 2 * D_MODEL), lambda i: (0, 0))
    return pl.pallas_call(
        body, name="xattn_bwd", grid=(s // tq,),
        in_specs=[row_blk, kv_blk, row_blk],
        out_specs=[row_blk, kv_blk],
        out_shape=[jax.ShapeDtypeStruct((s, D_MODEL), BF16), jax.ShapeDtypeStruct((MEM_LEN, 2 * D_MODEL), F32)],
        compiler_params=_cparams(),
    )(qm, kv, do)


def _place():
    x, y, c = lax.axis_index("x"), lax.axis_index("y"), lax.axis_index("c")
    others = [(1 - x, y), (x, 1 - y), (1 - x, 1 - y)]
    return x, y, c, others


def _slab(ref, axis, chip, size):
    start = pl.multiple_of(chip * size, LANES if axis == 1 else 16)
    if axis == 0:
        return ref.at[pl.ds(start, size), :]
    return ref.at[:, pl.ds(start, size)]


class _CommPlan:
    def __init__(self, ins, out_shape, scratch, start, finish):
        self.ins, self.out_shape, self.scratch, self.start, self.finish = ins, out_shape, scratch, start, finish

    @property
    def specs(self):
        any_spec = pl.BlockSpec(memory_space=pl.ANY)
        return [any_spec] * len(self.ins), [any_spec] * len(self.out_shape)

    def split(self, refs):
        n_in, n_out = len(self.ins), len(self.out_shape)
        return refs[:n_in], refs[n_in:n_in + n_out], refs[n_in + n_out:]


def _gather_plan(names, shards):
    spec = {name: (shape, axis) for name, shape, axis in BIG}
    nw = len(names)

    def shard_half(ref, c):
        rows = ref.shape[0] // 2
        return ref.at[pl.ds(pl.multiple_of(c * rows, 16), rows), :]

    def region(ref, w, chip, c):
        shape, axis = spec[names[w]]
        size = shape[axis] // N_CHIPS
        if axis == 0:
            rows = size // 2
            return ref.at[pl.ds(pl.multiple_of(chip * size + c * rows, 16), rows), :]
        rows = shape[0] // 2
        return ref.at[pl.ds(pl.multiple_of(c * rows, 16), rows), pl.ds(pl.multiple_of(chip * size, LANES), size)]

    def ops(shard, full, sems):
        ici_send, ici_recv, d2d_send, d2d_recv, local_sems = sems
        x, y, c, others = _place()
        mine, sibling = 2 * x + y, (x, y, 1 - c)
        local, over_ici, arrived, passed_on, from_sibling = [], [], [], [], []
        for w in range(nw):
            shape, axis = spec[names[w]]
            local.append(pltpu.make_async_copy(shard[w], _slab(full[w], axis, mine, shape[axis] // N_CHIPS),
                                               local_sems.at[w]))
            for t, (qx, qy) in enumerate(others):
                n, theirs = 3 * w + t, 2 * qx + qy
                over_ici.append(pltpu.make_async_remote_copy(
                    src_ref=shard_half(shard[w], c), dst_ref=region(full[w], w, mine, c),
                    send_sem=ici_send.at[n], recv_sem=ici_recv.at[n], device_id=(qx, qy, c), device_id_type=MESH))
                arrived.append(pltpu.make_async_remote_copy(
                    src_ref=shard_half(shard[w], c), dst_ref=region(full[w], w, theirs, c),
                    send_sem=ici_send.at[n], recv_sem=ici_recv.at[n], device_id=(qx, qy, c), device_id_type=MESH))
                passed_on.append(pltpu.make_async_remote_copy(
                    src_ref=region(full[w], w, theirs, c), dst_ref=region(full[w], w, theirs, c),
                    send_sem=d2d_send.at[n], recv_sem=d2d_recv.at[n], device_id=sibling, device_id_type=MESH))
                from_sibling.append(pltpu.make_async_remote_copy(
                    src_ref=region(full[w], w, theirs, c), dst_ref=region(full[w], w, theirs, 1 - c),
                    send_sem=d2d_send.at[n], recv_sem=d2d_recv.at[n], device_id=sibling, device_id_type=MESH))
        return local, over_ici, arrived, passed_on, from_sibling

    def start(shard, full, sems):
        local, over_ici, _, _, _ = ops(shard, full, sems)
        for cp in local + over_ici:
            cp.start()

    def finish(shard, full, sems):
        local, over_ici, arrived, passed_on, from_sibling = ops(shard, full, sems)
        for got, onward in zip(arrived, passed_on, strict=True):
            got.wait_recv()
            onward.start()
        for got in from_sibling:
            got.wait_recv()
        for cp in over_ici + passed_on:
            cp.wait_send()
        for cp in local:
            cp.wait()

    dma = pltpu.SemaphoreType.DMA
    return _CommPlan(
        ins=[shards[name] for name in names],
        out_shape=[jax.ShapeDtypeStruct(spec[name][0], BF16) for name in names],
        scratch=[dma((3 * nw,)), dma((3 * nw,)), dma((3 * nw,)), dma((3 * nw,)), dma((nw,))],
        start=start, finish=finish)


def _shard_shape(shape, axis):
    return tuple(d // N_CHIPS if a == axis else d for a, d in enumerate(shape))


def _exchange_plan(names, grads):
    spec = {name: (shape, axis) for name, shape, axis in BIG}
    nw = len(names)

    def ops(grad, stack, sems):
        send_sems, recv_sems, local_sems = sems
        x, y, c, others = _place()
        mine = 2 * x + y
        me, sibling = (x, y, c), (x, y, 1 - c)

        def dev(px, py, pc):
            return 4 * px + 2 * py + pc

        def copy(w, n, src, slot, to):
            return pltpu.make_async_remote_copy(
                src_ref=src, dst_ref=stack[w].at[slot], send_sem=send_sems.at[7 * w + n],
                recv_sem=recv_sems.at[7 * w + n], device_id=to, device_id_type=MESH)

        local, first, arrived, passed_on, from_sibling = [], [], [], [], []
        for w in range(nw):
            shape, axis = spec[names[w]]
            size = shape[axis] // N_CHIPS
            own = _slab(grad[w], axis, mine, size)
            local.append(pltpu.make_async_copy(own, stack[w].at[dev(*me)], local_sems.at[w]))
            first.append(copy(w, 0, own, dev(*me), sibling))
            from_sibling.append(copy(w, 0, own, dev(*sibling), me))
            for t, (qx, qy) in enumerate(others):
                got = stack[w].at[dev(qx, qy, c)]
                first.append(copy(w, 1 + t, _slab(grad[w], axis, 2 * qx + qy, size), dev(*me), (qx, qy, c)))
                arrived.append(copy(w, 1 + t, got, dev(qx, qy, c), me))
                passed_on.append(copy(w, 4 + t, got, dev(qx, qy, c), sibling))
                from_sibling.append(copy(w, 4 + t, got, dev(qx, qy, 1 - c), me))
        return local, first, arrived, passed_on, from_sibling

    def start(grad, stack, sems):
        local, first, _, _, _ = ops(grad, stack, sems)
        for cp in local + first:
            cp.start()

    def finish(grad, stack, sems):
        local, first, arrived, passed_on, from_sibling = ops(grad, stack, sems)
        for got, onward in zip(arrived, passed_on, strict=True):
            got.wait_recv()
            onward.start()
        for got in from_sibling:
            got.wait_recv()
        for cp in first + passed_on:
            cp.wait_send()
        for cp in local:
            cp.wait()

    dma = pltpu.SemaphoreType.DMA
    return _CommPlan(
        ins=[grads[name] for name in names],
        out_shape=[jax.ShapeDtypeStruct((N_DEV,) + _shard_shape(*spec[name]), BF16) for name in names],
        scratch=[dma((7 * nw,)), dma((7 * nw,)), dma((nw,))],
        start=start, finish=finish)


def _adamw(w, g, m, v):
    m = ADAM_B1 * m + (1.0 - ADAM_B1) * g
    v = ADAM_B2 * v + (1.0 - ADAM_B2) * (g * g)
    m_hat = m / (1.0 - ADAM_B1 ** ADAM_STEP)
    v_hat = v / (1.0 - ADAM_B2 ** ADAM_STEP)
    delta = -ADAM_LR * (m_hat / (jnp.sqrt(v_hat) + ADAM_EPS) + ADAM_WD * w)
    return delta, m, v


def _reduce_adamw(name, stack, w, m, v):
    rows, cols = w.shape
    tr = next(t for t in (256, 128, 64) if rows % t == 0)

    def body(s_ref, w_ref, m_ref, v_ref, g_ref, d_ref, nm_ref, nv_ref):
        g = s_ref[0].astype(F32)
        for d in range(1, N_DEV):
            g = g + s_ref[d].astype(F32)
        g_ref[...] = g
        d_ref[...], nm_ref[...], nv_ref[...] = _adamw(w_ref[...], g, m_ref[...], v_ref[...])

    blk = pl.BlockSpec((tr, cols), lambda i: (i, 0))
    return pl.pallas_call(
        body, name=name, grid=(rows // tr,),
        in_specs=[pl.BlockSpec((N_DEV, tr, cols), lambda i: (0, i, 0)), blk, blk, blk],
        out_specs=[blk] * 4, out_shape=[jax.ShapeDtypeStruct((rows, cols), F32)] * 4,
        compiler_params=_cparams(),
    )(stack, w, m, v)


def _small_step(pack, w, m, v):
    def body(p_ref, w_ref, m_ref, v_ref, g_ref, d_ref, nm_ref, nv_ref, loss_ref, all_ref, send_sems, recv_sems):
        x, y, c, _ = _place()
        me = 4 * x + 2 * y + c
        all_ref[me] = p_ref[...]
        sent = []
        for n in range(1, N_DEV):
            peer = me ^ n
            cp = pltpu.make_async_remote_copy(
                src_ref=p_ref, dst_ref=all_ref.at[me], send_sem=send_sems.at[n - 1], recv_sem=recv_sems.at[n - 1],
                device_id=(peer // 4, (peer // 2) % 2, peer % 2), device_id_type=MESH)
            cp.start()
            sent.append(cp)
        for n in range(1, N_DEV):
            peer = me ^ n
            pltpu.make_async_remote_copy(
                src_ref=p_ref, dst_ref=all_ref.at[peer], send_sem=send_sems.at[n - 1], recv_sem=recv_sems.at[n - 1],
                device_id=(peer // 4, (peer // 2) % 2, peer % 2), device_id_type=MESH).wait_recv()
        for cp in sent:
            cp.wait_send()
        tot = all_ref[0]
        for d in range(1, N_DEV):
            tot = tot + all_ref[d]
        g = tot[:SMALL_ROWS]
        g_ref[...] = g
        d_ref[...], nm_ref[...], nv_ref[...] = _adamw(w_ref[...], g, m_ref[...], v_ref[...])
        loss_ref[...] = jnp.sum(jnp.sum(tot[SMALL_ROWS:], axis=1, keepdims=True), axis=0, keepdims=True)

    vm = pl.BlockSpec(memory_space=pltpu.VMEM)
    small = jax.ShapeDtypeStruct((SMALL_ROWS, LANES), F32)
    return pl.pallas_call(
        body, name="small_step",
        in_specs=[vm] * 4, out_specs=[vm] * 5,
        out_shape=[small] * 4 + [jax.ShapeDtypeStruct((1, 1), F32)],
        scratch_shapes=[pltpu.VMEM((N_DEV, PACK_ROWS, LANES), F32),
                        pltpu.SemaphoreType.DMA((N_DEV - 1,)), pltpu.SemaphoreType.DMA((N_DEV - 1,))],
    )(pack, w, m, v)


LATER_WEIGHTS = tuple(name for name, _, _ in BIG if name != "w_in")


def _layer_step(x, mem, tgt, shards, vec):
    s = x.shape[0]
    d = D_MODEL
    tm = min(ROW_TILE, s)
    tl = min(WIDE_TILE, s)
    xb, cos2, sin2, w_in = _prep(x, _gather_plan(("w_in",), shards))
    bf = lambda w: ((s, w), BF16)
    f32 = lambda w: ((s, w), F32)

    w_sb, w_rqk = w_in[:, :OFF_RET_Q], w_in[:, OFF_RET_Q:OFF_RET_V]
    w_rvg, w_gate = w_in[:, OFF_RET_V:OFF_GATE], w_in[:, OFF_GATE:]
    q_scale = lambda width, q_width, scale: jnp.concatenate(
        [jnp.full((1, q_width), scale, F32), jnp.ones((1, width - q_width), F32)], axis=1)
    n_groups = 3 * SB_WIDTH // LANES

    def sb_epi(acc, t, i, j):
        scaled = acc * t[0]
        return [jnp.stack([scaled[:, g * LANES:(g + 1) * LANES] for g in range(n_groups)])], []

    (sb_qkv,) = _mm(
        "in_sb", xb, w_sb, s, 3 * SB_WIDTH, d, tm=tm, tn=3 * SB_WIDTH, tk=d, epi=sb_epi,
        ins=[(q_scale(3 * SB_WIDTH, SB_WIDTH, SB_SCALE), *_rowvec(3 * SB_WIDTH))],
        outs=[((n_groups, s, LANES), BF16, (n_groups, tm, LANES), lambda i, j: (0, i, 0))])

    def rope_epi(acc, t, i, j):
        cos, sin, scale = t
        parts = []
        for g in range(acc.shape[1] // RET_QK):
            xg = acc[:, g * RET_QK:(g + 1) * RET_QK]
            parts.append(xg * cos + _swap_halves(xg) * sin)
        return [jnp.concatenate(parts, axis=1) * scale], []

    rope_in = ((tm, RET_QK), lambda i, j: (i, 0))
    (rqk,) = _mm("in_rqk", xb, w_rqk, s, 2 * RET_QK_WIDTH, d, tm=tm, tn=2 * RET_QK_WIDTH, tk=d, epi=rope_epi,
                 chunk=MXU_COLS,
                 ins=[(cos2, *rope_in), (sin2, *rope_in),
                      (q_scale(2 * RET_QK_WIDTH, RET_QK_WIDTH, RET_SCALE), *_rowvec(2 * RET_QK_WIDTH))],
                 outs=[(*f32(2 * RET_QK_WIDTH), *_tile(tm, 2 * RET_QK_WIDTH))])
    (rvg,) = _mm("in_rvg", xb, w_rvg, s, 2 * RET_V_WIDTH, d, tm=tm, tn=2 * RET_V_WIDTH, tk=d,
                 epi=_plain, outs=[(*bf(2 * RET_V_WIDTH), *_tile(tm, 2 * RET_V_WIDTH))])
    (gates,) = _mm("in_gate", xb, w_gate, s, 2 * d, d, tm=tm, tn=2 * d, tk=d, chunk=MXU_COLS,
                   epi=lambda acc, t, i, j: ([_sigmoid(acc + t[0])], []),
                   ins=[(vec["b_gate"], *_rowvec(2 * d))], outs=[(*bf(2 * d), *_tile(tm, 2 * d))])

    sb_out, sb_out_f32, *gathered = _sb_fwd(sb_qkv, s, comm=_gather_plan(LATER_WEIGHTS, shards))
    wt = dict(zip(LATER_WEIGHTS, gathered, strict=True))
    ret, gated = _ret_fwd(rqk, rvg, s)
    (y_sb,) = _mm("sb_o", sb_out, wt["w_sb_o"], s, d, SB_WIDTH, tm=tl, tn=d, tk=SB_WIDTH, epi=_plain,
                  outs=[(*bf(d), *_tile(tl, d))])
    y_ret, mixin = _mm(
        "ret_o", gated, wt["w_ret_o"], s, d, RET_V_WIDTH, tm=tl, tn=d, tk=RET_V_WIDTH, chunk=MXU_COLS,
        epi=lambda acc, t, i, j: ([acc, t[0].astype(F32) * t[2].astype(F32) + t[1].astype(F32) * acc], []),
        ins=[(gates, *_tile(tl, d)), (gates, *_tile(tl, d, 1)), (y_sb, *_tile(tl, d))],
        outs=[(*bf(d), *_tile(tl, d)), (*bf(d), *_tile(tl, d))])

    def ln_epi(acc, t, i, j):
        *res, g, b = t
        prev = res[0] if len(res) == 1 else res[0] * res[1] + res[2]
        xhat, rstd = _norm(DN_ALPHA * prev + acc)
        return [xhat * g + b, xhat, rstd], []

    full = _tile(tm, d)
    col1 = ((tm, 1), lambda i, j: (i, 0))
    vec_in = lambda name: (vec[name], *_rowvec(d))
    ln_outs = [(*bf(d), *full), (*f32(d), *full), ((s, 1), F32, *col1)]
    x1b, xhat1, rstd1 = _mm(
        "mix_o", mixin, wt["w_mix_o"], s, d, d, tm=tm, tn=d, tk=d, epi=ln_epi,
        ins=[(x, *full), vec_in("ln1_g"), vec_in("ln1_b")], outs=ln_outs)

    (qm,) = _mm("mem_q", x1b, wt["w_mem_q"], s, d, d, tm=tl, tn=d, tk=d,
                epi=lambda acc, t, i, j: ([acc * MEM_SCALE], []), outs=[(*bf(d), *_tile(tl, d))])
    (kv,) = _mm("mem_kv", mem, wt["w_mem_kv"], MEM_LEN, 2 * d, d, tm=MEM_LEN, tn=d, tk=d, epi=_plain,
                outs=[((MEM_LEN, 2 * d), BF16, *_tile(MEM_LEN, d))])
    att = _xattn_fwd(qm, kv, s)
    x2b, xhat2, rstd2 = _mm(
        "mem_o", att, wt["w_mem_o"], s, d, d, tm=tm, tn=d, tk=d, epi=ln_epi,
        ins=[(xhat1, *full), vec_in("ln1_g"), vec_in("ln1_b"), vec_in("ln2_g"), vec_in("ln2_b")], outs=ln_outs)

    fh = FFN_HIDDEN
    tf = fh // 2
    (f1,) = _mm("ffn_in1", x2b, wt["w_ffn_in"], s, fh, d, tm=tl, tn=tf, tk=d, epi=_plain, j_outer=True,
                outs=[(*bf(fh), *_tile(tl, tf))])

    def swiglu_epi(acc, t, i, j):
        a = t[0].astype(F32)
        return [acc, a * _sigmoid(a) * acc], []

    f2, act = _mm(
        "ffn_in2", x2b, wt["w_ffn_in"], s, fh, d, tm=tm, tn=fh, tk=d, b_off=(0, 1), epi=swiglu_epi, chunk=MXU_COLS,
        ins=[(f1, *_tile(tm, fh))], outs=[(*bf(fh), *_tile(tm, fh)), (*bf(fh), *_tile(tm, fh))])

    def head_epi(acc, t, i, j):
        prev_hat, prev_g, prev_b, g, b, target = t
        xhat, rstd = _norm(DN_ALPHA * (prev_hat * prev_g + prev_b) + acc)
        err = xhat * g + b - target
        dy = err * (1.0 / d)
        du = _norm_bwd(dy * g, xhat, rstd)
        return [du], [_colsum(dy * xhat), _colsum(dy), _colsum(err * err) * (0.5 / d)]

    vec_acc = ((1, d), F32)
    du3b, dg3, db3, loss_cols = _mm(
        "ffn_out", act, wt["w_ffn_out"], s, d, fh, tm=tm, tn=d, tk=fh, epi=head_epi,
        ins=[(xhat2, *full), vec_in("ln2_g"), vec_in("ln2_b"), vec_in("ln3_g"), vec_in("ln3_b"), (tgt, *full)],
        outs=[(*bf(d), *full)], accs=[vec_acc] * 3)

    grads = {}
    ts = min(SEQ_TILE, s)

    def wgrad(name, a, b, m, n, tm_, tn_, tk_=None):
        (g,) = _mm(name, a, b, m, n, a.shape[0], tm=tm_, tn=tn_, tk=tk_ or ts, ta=True, epi=_plain,
                   outs=[((m, n), BF16, *_tile(tm_, tn_))])
        return g

    def ffn_bwd_epi(acc, t, i, j):
        a, b = t[0].astype(F32), t[1].astype(F32)
        sg = _sigmoid(a)
        return [acc * b * (sg * (1.0 + a * (1.0 - sg))), acc * (a * sg)], []

    df1, df2 = _mm(
        "ffn_out_t", du3b, wt["w_ffn_out"], s, fh, d, tm=tm, tn=fh, tk=d, tb=True, epi=ffn_bwd_epi, chunk=MXU_COLS,
        ins=[(f1, *_tile(tm, fh)), (f2, *_tile(tm, fh))],
        outs=[(*bf(fh), *_tile(tm, fh)), (*bf(fh), *_tile(tm, fh))])
    grads["w_ffn_out"] = wgrad("g_ffn_out", act, du3b, fh, d, tf, d)
    grads["w_ffn_in"] = jnp.concatenate(
        [wgrad("g_ffn_in1", x2b, df1, d, fh, d, tf), wgrad("g_ffn_in2", x2b, df2, d, fh, d, tf)], axis=1)
    (dx2a,) = _mm("ffn_in1_t", df1, wt["w_ffn_in"], s, d, fh, tm=tm, tn=d, tk=fh, tb=True, epi=_plain,
                  outs=[(*f32(d), *full)])

    def ln_bwd(name, a, b, k, tk, b_off, more, scales, xhat, rstd, g):
        def epi(acc, t, i, j):
            *extra, xh, rs, gg = t
            dy = acc
            for e, sc in zip(extra, scales, strict=True):
                dy = dy + e.astype(F32) * sc
            return [_norm_bwd(dy * gg, xh, rs)], [_colsum(dy * xh), _colsum(dy)]

        return _mm(name, a, b, s, d, k, tm=tm, tn=d, tk=tk, tb=True, b_off=b_off, epi=epi,
                   ins=[(e, *full) for e in more] + [(xhat, *full), (rstd, *col1), (g, *_rowvec(d))],
                   outs=[(*bf(d), *full)], accs=[vec_acc] * 2)

    du2b, dg2, db2 = ln_bwd("ffn_in2_t", df2, wt["w_ffn_in"], fh, fh, (0, 1), [dx2a, du3b], [1.0, DN_ALPHA],
                            xhat2, rstd2, vec["ln2_g"])

    (datt,) = _mm("mem_o_t", du2b, wt["w_mem_o"], s, d, d, tm=tl, tn=d, tk=d, tb=True, epi=_plain,
                  outs=[(*bf(d), *_tile(tl, d))])
    grads["w_mem_o"] = wgrad("g_mem_o", att, du2b, d, d, d, d)
    dqm, dkv = _xattn_bwd(qm, kv, datt, s)
    grads["w_mem_q"] = wgrad("g_mem_q", x1b, dqm, d, d, d, d)
    grads["w_mem_kv"] = wgrad("g_mem_kv", mem, dkv, d, 2 * d, d, d, MEM_LEN)
    du1b, dg1, db1 = ln_bwd("mem_q_t", dqm, wt["w_mem_q"], d, d, (0, 0), [du2b], [DN_ALPHA],
                            xhat1, rstd1, vec["ln1_g"])

    def merge_bwd_epi(acc, t, i, j):
        g0, g1, ysb, yret = (v.astype(F32) for v in t)
        dgate0 = acc * ysb * (g0 * (1.0 - g0))
        dgate1 = acc * yret * (g1 * (1.0 - g1))
        return [dgate0, dgate1, acc * g0, acc * g1], [_colsum(dgate0), _colsum(dgate1)]

    dgate0, dgate1, dy_sb, dy_ret, dbg0, dbg1 = _mm(
        "mix_o_t", du1b, wt["w_mix_o"], s, d, d, tm=tm, tn=d, tk=d, tb=True, epi=merge_bwd_epi,
        ins=[(gates, *full), (gates, *_tile(tm, d, 1)), (y_sb, *full), (y_ret, *full)],
        outs=[(*bf(d), *full)] * 4, accs=[vec_acc] * 2)
    grads["w_mix_o"] = wgrad("g_mix_o", mixin, du1b, d, d, d, d)
    grads["w_sb_o"] = wgrad("g_sb_o", sb_out, dy_sb, SB_WIDTH, d, SB_WIDTH, d)
    grads["w_ret_o"] = wgrad("g_ret_o", gated, dy_ret, RET_V_WIDTH, d, RET_V_WIDTH, d)
    (dsb_out,) = _mm("sb_o_t", dy_sb, wt["w_sb_o"], s, SB_WIDTH, d, tm=tl, tn=SB_WIDTH, tk=d, tb=True, epi=_plain,
                     outs=[(*bf(SB_WIDTH), *_tile(tl, SB_WIDTH))])

    def gate_norm_bwd_epi(acc, t, i, j):
        r, g = t[0], t[1].astype(F32)
        drg, dret = [], []
        for h in range(acc.shape[1] // RET_V):
            sl = slice(h * RET_V, (h + 1) * RET_V)
            xhat, rstd = _norm(r[:, sl])
            gg, dd = g[:, sl], acc[:, sl]
            sg = _sigmoid(gg)
            drg.append(dd * xhat * (sg * (1.0 + gg * (1.0 - sg))))
            dret.append(_norm_bwd(dd * (gg * sg), xhat, rstd))
        return [jnp.concatenate(drg, axis=1), jnp.concatenate(dret, axis=1)], []

    drg, dret = _mm(
        "ret_o_t", dy_ret, wt["w_ret_o"], s, RET_V_WIDTH, d, tm=tm, tn=d, tk=d, tb=True, epi=gate_norm_bwd_epi,
        chunk=MXU_COLS,
        ins=[(ret, *full), (rvg, *_tile(tm, d, 1))],
        outs=[(*bf(RET_V_WIDTH), *full)] * 2)

    drq = _ret_bwd_q(rqk, rvg, dret, cos2, sin2, s)
    drk, drv = _ret_bwd_kv(rqk, rvg, dret, cos2, sin2, s)
    dsq, dsk, dsv, *stacked = _sb_bwd(sb_qkv, sb_out_f32, dsb_out, s, comm=_exchange_plan(LATER_WEIGHTS, grads))
    stacks = dict(zip(LATER_WEIGHTS, stacked, strict=True))

    dh = jnp.concatenate([dsq, dsk, dsv, drq, drk, drv, drg, dgate0, dgate1], axis=1)
    grads["w_in"] = wgrad("g_in", xb, dh, d, IN_WIDTH, d, IN_WIDTH // N_CHIPS)
    grad_x, stacks["w_in"] = _mm(
        "in_t", dh, w_in, s, d, IN_WIDTH, tm=tl, tn=d, tk=IN_WIDTH // N_CHIPS, tb=True,
        epi=lambda acc, t, i, j: ([acc + DN_ALPHA * t[0].astype(F32)], []),
        ins=[(du1b, *_tile(tl, d))], outs=[(*f32(d), *_tile(tl, d))], comm=_exchange_plan(("w_in",), grads))

    small = {"b_gate": jnp.concatenate([dbg0, dbg1], axis=1), "ln1_g": dg1, "ln1_b": db1, "ln2_g": dg2,
             "ln2_b": db2, "ln3_g": dg3, "ln3_b": db3}
    return grad_x, stacks, small, loss_cols


def kernel(x, mem, w_in, b_gate, w_sb_o, w_ret_o, w_mix_o, ln1_g, ln1_b, w_mem_q, w_mem_kv, w_mem_o, ln2_g, ln2_b, w_ffn_in, w_ffn_out, ln3_g, ln3_b, loss_target, m_w_in, m_b_gate, m_w_sb_o, m_w_ret_o, m_w_mix_o, m_ln1_g, m_ln1_b, m_w_mem_q, m_w_mem_kv, m_w_mem_o, m_ln2_g, m_ln2_b, m_w_ffn_in, m_w_ffn_out, m_ln3_g, m_ln3_b, v_w_in, v_b_gate, v_w_sb_o, v_w_ret_o, v_w_mix_o, v_ln1_g, v_ln1_b, v_w_mem_q, v_w_mem_kv, v_w_mem_o, v_ln2_g, v_ln2_b, v_w_ffn_in, v_w_ffn_out, v_ln3_g, v_ln3_b):
    given = dict(locals())
    s = x.shape[1]
    x2d = x.reshape(s, D_MODEL)
    tgt = loss_target.reshape(s, D_MODEL)
    mem2d = mem.reshape(MEM_LEN, D_MODEL)
    shard = {name: given[name].reshape(_shard_shape(shape, axis)) for name, shape, axis in BIG}
    vec = {name: given[name] for name in SMALL}

    shards_bf = {name: _cast_bf16("cast_" + name, shard[name]) for name, _, _ in BIG}

    grad_x, stacks, small, loss_cols = _layer_step(x2d, mem2d, tgt, shards_bf, vec)

    out = {}
    for name, shape, axis in BIG:
        stack = stacks[name]
        shp = given[name].shape
        res = _reduce_adamw("adamw_" + name, stack, shard[name], given["m_" + name].reshape(stack.shape[1:]),
                            given["v_" + name].reshape(stack.shape[1:]))
        out[name] = [r.reshape(shp) for r in res]

    pack = jnp.concatenate([small[name] for name in SMALL] + [loss_cols], axis=1).reshape(PACK_ROWS, LANES)
    cat = lambda pre: jnp.concatenate([given[pre + name] for name in SMALL], axis=1).reshape(SMALL_ROWS, LANES)
    *res, loss = _small_step(pack, cat(""), cat("m_"), cat("v_"))
    flat = [r.reshape(1, SMALL_LEN) for r in res]
    off = 0
    for name in SMALL:
        n = given[name].shape[1]
        out[name] = [r[:, off:off + n] for r in flat]
        off += n

    return (loss.reshape(()), grad_x.reshape(x.shape),
            *[out[name][0] for name in WEIGHT_ORDER], *[out[name][1] for name in WEIGHT_ORDER],
            *[out[name][2] for name in WEIGHT_ORDER], *[out[name][3] for name in WEIGHT_ORDER])
```

```python
import functools

import jax
import jax.numpy as jnp
import numpy as np
from jax import lax
from jax.experimental import pallas as pl
from jax.experimental.pallas import tpu as pltpu

F32, BF16 = jnp.float32, jnp.bfloat16
MESH = pl.DeviceIdType.MESH

D_MODEL = 1024
MEM_LEN = 256
SB_HEADS, SB_DIM, SB_WIDTH = 8, 64, 512
RET_HEADS, RET_QK, RET_V = 4, 128, 256
RET_QK_WIDTH, RET_V_WIDTH = 512, 1024
ROPE_BASE = 10000.0
MEM_HEADS, MEM_DIM = 4, 256
FFN_HIDDEN = 2816
IN_WIDTH = 6656
OFF_RET_Q, OFF_RET_V, OFF_RET_G, OFF_GATE = 1536, 2560, 3584, 4608
DN_ALPHA = 2.0 ** 0.25
LN_EPS = 1e-5
SB_SCALE = SB_DIM ** -0.5
SB_DEAD = -110.0
RET_SCALE = RET_QK ** -0.5
MEM_SCALE = MEM_DIM ** -0.5
ADAM_LR, ADAM_B1, ADAM_B2, ADAM_EPS, ADAM_WD, ADAM_STEP = 0.001, 0.9, 0.999, 1e-08, 0.01, 10

N_DEV, N_CHIPS = 8, 4

LANES = 128
MXU_COLS = 256
VMEM_LIMIT_BYTES = 52 * 2 ** 20
ROW_TILE = 512
WIDE_TILE = 1024
SEQ_TILE = 2048
SB_BLOCK = 256
RET_BLOCK = 256
XATTN_ROWS = 512

BIG = (
    ("w_in", (D_MODEL, IN_WIDTH), 1),
    ("w_sb_o", (SB_WIDTH, D_MODEL), 1),
    ("w_ret_o", (RET_V_WIDTH, D_MODEL), 0),
    ("w_mix_o", (D_MODEL, D_MODEL), 0),
    ("w_mem_q", (D_MODEL, D_MODEL), 0),
    ("w_mem_kv", (D_MODEL, 2 * D_MODEL), 1),
    ("w_mem_o", (D_MODEL, D_MODEL), 0),
    ("w_ffn_in", (D_MODEL, 2 * FFN_HIDDEN), 1),
    ("w_ffn_out", (FFN_HIDDEN, D_MODEL), 0),
)
SMALL = ("b_gate", "ln1_g", "ln1_b", "ln2_g", "ln2_b", "ln3_g", "ln3_b")
SMALL_LEN = 2 * D_MODEL + 6 * D_MODEL
SMALL_ROWS = SMALL_LEN // LANES
PACK_ROWS = SMALL_ROWS + D_MODEL // LANES
WEIGHT_ORDER = ("w_in", "b_gate", "w_sb_o", "w_ret_o", "w_mix_o", "ln1_g", "ln1_b", "w_mem_q", "w_mem_kv",
                "w_mem_o", "ln2_g", "ln2_b", "w_ffn_in", "w_ffn_out", "ln3_g", "ln3_b")


def _cparams():
    return pltpu.CompilerParams(vmem_limit_bytes=VMEM_LIMIT_BYTES)


def _dot(a, b, ca, cb):
    return lax.dot_general(a, b, (((ca,), (cb,)), ((), ())), preferred_element_type=F32)


def _sigmoid(x):
    return 1.0 / (1.0 + jnp.exp(-x))


def _mm(name, a, b, m, n, k, *, tm, tn, tk, epi, outs, ins=(), accs=(), ta=False, tb=False,
        a_off=(0, 0), b_off=(0, 0), j_outer=False, comm=None, chunk=None):
    assert m % tm == 0 and n % tn == 0 and k % tk == 0, (name, m, n, k, tm, tn, tk)
    assert chunk is None or (k == tk and tn % chunk == 0), name
    ni, nj, nk = m // tm, n // tn, k // tk
    assert not accs or nj == 1, name
    ij = (lambda g0, g1: (g1, g0)) if j_outer else (lambda g0, g1: (g0, g1))

    def spec(block, index):
        return pl.BlockSpec(block, lambda g0, g1, kk: index(*ij(g0, g1), kk))

    if ta:
        a_spec = spec((tk, tm), lambda i, j, kk: (kk + a_off[0], i + a_off[1]))
    else:
        a_spec = spec((tm, tk), lambda i, j, kk: (i + a_off[0], kk + a_off[1]))
    if tb:
        b_spec = spec((tn, tk), lambda i, j, kk: (j + b_off[0], kk + b_off[1]))
    else:
        b_spec = spec((tk, tn), lambda i, j, kk: (kk + b_off[0], j + b_off[1]))
    in_specs = [a_spec, b_spec]
    for _, bs, im in ins:
        in_specs.append(spec(bs, lambda i, j, kk, im=im: im(i, j)))
    out_specs, out_shape = [], []
    for shape, dtype, bs, im in outs:
        out_specs.append(spec(bs, lambda i, j, kk, im=im: im(i, j)))
        out_shape.append(jax.ShapeDtypeStruct(shape, dtype))
    for shape, dtype in accs:
        out_specs.append(spec(shape, lambda i, j, kk, nd=len(shape): (0,) * nd))
        out_shape.append(jax.ShapeDtypeStruct(shape, dtype))
    n_in, n_out, n_acc = len(ins), len(outs), len(accs)
    ca, cb = (0 if ta else 1), (1 if tb else 0)
    grid = (*ij(ni, nj), nk)
    comm_ins, comm_outs, comm_scratch = [], [], []
    if comm is not None:
        comm_in_specs, comm_out_specs = comm.specs
        comm_ins, comm_outs, comm_scratch = list(comm.ins), list(comm.out_shape), list(comm.scratch)
        in_specs += comm_in_specs
        out_specs += comm_out_specs
        out_shape += comm_outs
    n_ci, n_co = len(comm_ins), len(comm_outs)

    def body(*refs):
        a_ref, b_ref = refs[:2]
        in_refs = refs[2:2 + n_in]
        ci_refs = refs[2 + n_in:2 + n_in + n_ci]
        rest = refs[2 + n_in + n_ci:]
        out_refs, acc_refs = rest[:n_out], rest[n_out:n_out + n_acc]
        co_refs = rest[n_out + n_acc:n_out + n_acc + n_co]
        scratch = rest[n_out + n_acc + n_co:]
        sem_refs, scratch = scratch[:len(comm_scratch)], scratch[len(comm_scratch):]
        (i, j), kk = ij(pl.program_id(0), pl.program_id(1)), pl.program_id(2)
        if comm is not None:
            first_step, last_step = _grid_ends(grid)
            pl.when(first_step)(lambda: comm.start(ci_refs, co_refs, sem_refs))
        def finish(acc, cols=slice(None)):
            def of(r):
                return r[..., cols] if r.shape[-1] == tn else r[...]

            o_tiles, a_tiles = epi(acc, [of(r) for r in in_refs], i, j)
            for r, t in zip(out_refs, o_tiles, strict=True):
                r[..., cols] = t.astype(r.dtype)
            if n_acc:
                @pl.when(i == 0)
                def _():
                    for r, t in zip(acc_refs, a_tiles, strict=True):
                        r[..., cols] = t

                @pl.when(i > 0)
                def _():
                    for r, t in zip(acc_refs, a_tiles, strict=True):
                        r[..., cols] += t

        if chunk is not None:
            a_tile = a_ref[...].astype(BF16)
            for c0 in range(0, tn, chunk):
                cols = slice(c0, c0 + chunk)
                b_part = b_ref[cols, :] if tb else b_ref[:, cols]
                finish(_dot(a_tile, b_part.astype(BF16), ca, cb), cols)
            if comm is not None:
                pl.when(last_step)(lambda: comm.finish(ci_refs, co_refs, sem_refs))
            return

        part = _dot(a_ref[...].astype(BF16), b_ref[...].astype(BF16), ca, cb)
        if nk == 1:
            finish(part)
        else:
            acc_ref = scratch[0]

            @pl.when(kk == 0)
            def _():
                acc_ref[...] = part

            @pl.when(kk > 0)
            def _():
                acc_ref[...] += part

            @pl.when(kk == nk - 1)
            def _():
                finish(acc_ref[...])

        if comm is not None:
            pl.when(last_step)(lambda: comm.finish(ci_refs, co_refs, sem_refs))

    res = pl.pallas_call(
        body, name=name, grid=grid, in_specs=in_specs, out_specs=out_specs, out_shape=out_shape,
        scratch_shapes=comm_scratch + ([pltpu.VMEM((tm, tn), F32)] if nk > 1 else []),
        compiler_params=_cparams(),
    )(a, b, *[x for x, _, _ in ins], *comm_ins)
    return res


def _grid_ends(grid):
    ids = [pl.program_id(ax) for ax in range(len(grid))]
    first = functools.reduce(jnp.logical_and, [p == 0 for p in ids])
    last = functools.reduce(jnp.logical_and, [p == n - 1 for p, n in zip(ids, grid, strict=True)])
    return first, last


def _tile(tm, tn, dj=0):
    return (tm, tn), (lambda i, j: (i, j + dj))


def _rowvec(tn, dj=0):
    return (1, tn), (lambda i, j: (0, j + dj))


def _plain(acc, tiles, i, j):
    return [acc], []


def _ew(name, fn, ins, outs, rows, tr):
    assert rows % tr == 0, (name, rows, tr)
    in_specs = []
    for x in ins:
        if x.shape[0] == rows:
            in_specs.append(pl.BlockSpec((tr, x.shape[1]), lambda i: (i, 0)))
        else:
            in_specs.append(pl.BlockSpec(x.shape, lambda i: (0, 0)))
    n_in = len(ins)

    def body(*refs):
        res = fn(*[r[...] for r in refs[:n_in]])
        for r, t in zip(refs[n_in:], res, strict=True):
            r[...] = t.astype(r.dtype)

    return pl.pallas_call(
        body, name=name, grid=(rows // tr,), in_specs=in_specs,
        out_specs=[pl.BlockSpec((tr, w), lambda i: (i, 0)) for w, _ in outs],
        out_shape=[jax.ShapeDtypeStruct((rows, w), dt) for w, dt in outs],
        compiler_params=_cparams(),
    )(*ins)


def _cast_bf16(name, x):
    rows = x.shape[0]
    tr = next(t for t in (512, 256, 64) if rows % t == 0)
    return _ew(name, lambda v: (v,), [x], [(x.shape[1], BF16)], rows, tr)[0]


def _prep(x, comm):
    s = x.shape[0]
    half = RET_QK // 2
    inv = 1.0 / (ROPE_BASE ** (jnp.arange(half, dtype=F32) / half))
    inv2 = jnp.concatenate([inv, inv]).reshape(1, RET_QK)
    sign = jnp.concatenate([-jnp.ones((half,), F32), jnp.ones((half,), F32)]).reshape(1, RET_QK)
    tr = min(ROW_TILE, s)
    grid = (s // tr,)
    c_in_specs, c_out_specs, c_out_shape, c_scratch, c_ins, split = _host(comm, 3, 3)

    def body(*refs):
        (x_ref, inv_ref, sign_ref), (xb_ref, cos_ref, sin_ref), _, riding = split(refs)
        i = pl.program_id(0)
        first_step, last_step = _grid_ends(grid)
        pl.when(first_step)(lambda: comm.start(*riding))
        xb_ref[...] = x_ref[...].astype(BF16)
        pos = (lax.broadcasted_iota(jnp.int32, (tr, RET_QK), 0) + i * tr).astype(F32)
        ang = pos * inv_ref[...]
        cos_ref[...] = jnp.cos(ang)
        sin_ref[...] = jnp.sin(ang) * sign_ref[...]
        pl.when(last_step)(lambda: comm.finish(*riding))

    vec = pl.BlockSpec((1, RET_QK), lambda i: (0, 0))
    row = lambda w: pl.BlockSpec((tr, w), lambda i: (i, 0))
    return pl.pallas_call(
        body, name="prep", grid=grid,
        in_specs=[row(D_MODEL), vec, vec] + c_in_specs,
        out_specs=[row(D_MODEL), row(RET_QK), row(RET_QK)] + c_out_specs,
        out_shape=[jax.ShapeDtypeStruct((s, D_MODEL), BF16), jax.ShapeDtypeStruct((s, RET_QK), F32),
                   jax.ShapeDtypeStruct((s, RET_QK), F32)] + c_out_shape,
        scratch_shapes=c_scratch, compiler_params=_cparams(),
    )(x, inv2, sign, *c_ins)


def _swap_halves(x):
    return pltpu.roll(x, RET_QK // 2, 1)


def _norm(u):
    mu = jnp.mean(u, axis=-1, keepdims=True)
    d = u - mu
    var = jnp.mean(d * d, axis=-1, keepdims=True)
    rstd = lax.rsqrt(var + LN_EPS)
    return d * rstd, rstd


def _norm_bwd(dxh, xhat, rstd):
    m1 = jnp.mean(dxh, axis=-1, keepdims=True)
    m2 = jnp.mean(dxh * xhat, axis=-1, keepdims=True)
    return rstd * (dxh - m1 - xhat * m2)


def _colsum(t):
    return jnp.sum(t, axis=0, keepdims=True)


def _split_mm(t, tri):
    hi = t.astype(BF16)
    lo = (t - hi.astype(F32)).astype(BF16)
    return _dot(hi, tri, 1, 0) + _dot(lo, tri, 1, 0)


def _sb_masks():
    t = SB_BLOCK
    lane = lax.broadcasted_iota(jnp.int32, (1, LANES), 1)
    first = lane < SB_DIM
    m0 = jnp.where(first, 1.0, 0.0).astype(BF16)
    m1 = jnp.where(first, 0.0, 1.0).astype(BF16)
    row = lax.broadcasted_iota(jnp.int32, (t, t), 0)
    col = lax.broadcasted_iota(jnp.int32, (t, t), 1)
    return first, (m0, m1), row, col


def _sb_logits(qh, k, causal):
    z = _dot(qh, k, 1, 1)
    lp = jnp.log(1.0 + jnp.exp(-jnp.abs(z)))
    a = jnp.minimum(z, 0.0) - lp
    r = jnp.minimum(-z, 0.0) - lp
    if causal is not None:
        r = jnp.where(causal, r, 0.0)
    return a, r


def _sb_walk(i, blocks, l_ref, causal):
    pl.when(i == 0)(lambda: blocks([(i, causal)]))
    pl.when(i > 0)(lambda: blocks([(i, causal), (i - 1, None)]))

    def alive():
        top = jnp.max(functools.reduce(jnp.maximum, [l_ref[c] for c in range(l_ref.shape[0])]))
        return jnp.where(top > SB_DEAD, 1, 0)

    def cond(c):
        return jnp.logical_and(c[0] < i, c[1] > 0)

    def step(c):
        blocks([(i - 1 - c[0], None)])
        return c[0] + 1, alive()

    lax.while_loop(cond, step, (jnp.int32(1), alive()))


def _host(comm, n_in, n_out):
    if comm is None:
        return [], [], [], [], [], lambda refs: (refs[:n_in], refs[n_in:n_in + n_out], refs[n_in + n_out:], None)
    in_specs, out_specs = comm.specs
    n_ci, n_co, n_sem = len(comm.ins), len(comm.out_shape), len(comm.scratch)

    def split(refs):
        ins, ci = refs[:n_in], refs[n_in:n_in + n_ci]
        rest = refs[n_in + n_ci:]
        outs, co = rest[:n_out], rest[n_out:n_out + n_co]
        sems, scratch = rest[n_out + n_co:n_out + n_co + n_sem], rest[n_out + n_co + n_sem:]
        return ins, outs, scratch, (ci, co, sems)

    return in_specs, out_specs, list(comm.out_shape), list(comm.scratch), list(comm.ins), split


def _sb_qkv_specs(s, g):
    groups = SB_HEADS // 2 // g
    return [pl.BlockSpec((g, SB_BLOCK, LANES), lambda p, i: (p, i, 0)),
            pl.BlockSpec((g, s, LANES), lambda p, i: (groups + p, 0, 0)),
            pl.BlockSpec((g, s, LANES), lambda p, i: (2 * groups + p, 0, 0))]


def _sb_fwd(qkv, s, comm=None):
    t = SB_BLOCK
    g = 2
    nq = s // t
    grid = (SB_HEADS // 2 // g, nq)
    c_in_specs, c_out_specs, c_out_shape, c_scratch, c_ins, split = _host(comm, 3, 2)

    def body(*refs):
        (q_ref, k_ref, v_ref), (o_ref, of_ref), (l_ref, acc_ref), riding = split(refs)
        i = pl.program_id(1)
        if comm is not None:
            first_step, last_step = _grid_ends(grid)
            pl.when(first_step)(lambda: comm.start(*riding))
        first, hmask, row, col = _sb_masks()
        after = jnp.where(row > col, 1.0, 0.0).astype(BF16)
        causal = col < row
        heads = [(p, h) for p in range(g) for h in range(2)]
        qh = {(p, h): q_ref[p] * hmask[h] for p, h in heads}
        l_ref[...] = jnp.zeros_like(l_ref)
        acc_ref[...] = jnp.zeros_like(acc_ref)

        def blocks(todo):
            chains = [(b, p, h) for b in range(len(todo)) for p, h in heads]
            starts = [pl.multiple_of(kb * t, t) for kb, _ in todo]
            ks = {(b, p): k_ref[p, pl.ds(st, t), :] for b, st in enumerate(starts) for p in range(g)}
            vs = {(b, p): v_ref[p, pl.ds(st, t), :] for b, st in enumerate(starts) for p in range(g)}
            ar = {(b, p, h): _sb_logits(qh[p, h], ks[b, p], todo[b][1]) for b, p, h in chains}
            later = {c: _split_mm(ar[c][1], after) for c in chains}
            carry = {(p, h): l_ref[2 * p + h] for p, h in heads}
            w = {}
            for b, (_, mask) in enumerate(todo):
                for p, h in heads:
                    wc = jnp.exp(ar[b, p, h][0] + later[b, p, h] + carry[p, h])
                    w[b, p, h] = wc if mask is None else jnp.where(mask, wc, 0.0)
                carry = {(p, h): carry[p, h] + jnp.sum(ar[b, p, h][1], axis=1, keepdims=True) for p, h in heads}
            pv = {(b, p, h): _dot(w[b, p, h].astype(BF16), vs[b, p], 1, 0) for b, p, h in chains}
            for p in range(g):
                lanes = slice(p * LANES, (p + 1) * LANES)
                acc = acc_ref[:, lanes]
                for b in range(len(todo)):
                    acc = acc + jnp.where(first, pv[b, p, 0], pv[b, p, 1])
                acc_ref[:, lanes] = acc
            for p, h in heads:
                l_ref[2 * p + h] = carry[p, h]

        _sb_walk(i, blocks, l_ref, causal)
        o_ref[...] = acc_ref[...].astype(o_ref.dtype)
        of_ref[...] = acc_ref[...]
        if comm is not None:
            pl.when(last_step)(lambda: comm.finish(*riding))

    blk = pl.BlockSpec((t, g * LANES), lambda p, i: (i, p))
    return pl.pallas_call(
        body, name="sb_fwd", grid=grid,
        in_specs=_sb_qkv_specs(s, g) + c_in_specs,
        out_specs=[blk, blk] + c_out_specs,
        out_shape=[jax.ShapeDtypeStruct((s, SB_WIDTH), BF16), jax.ShapeDtypeStruct((s, SB_WIDTH), F32)] + c_out_shape,
        scratch_shapes=c_scratch + [pltpu.VMEM((2 * g, t, 1), F32), pltpu.VMEM((t, g * LANES), F32)],
        compiler_params=_cparams(),
    )(qkv, qkv, qkv, *c_ins)


def _sb_bwd(qkv, o, do, s, comm=None):
    t = SB_BLOCK
    nq = s // t
    grid = (SB_HEADS // 2, nq)
    c_in_specs, c_out_specs, c_out_shape, c_scratch, c_ins, split = _host(comm, 5, 3)

    def body(*refs):
        ((q_ref, k_ref, v_ref, o_ref, do_ref), (dq_ref, dk_ref, dv_ref),
         (l_ref, e_ref, dq_acc, dk_acc, dv_acc), riding) = split(refs)
        i = pl.program_id(1)
        if comm is not None:
            first_step, last_step = _grid_ends(grid)
            pl.when(first_step)(lambda: comm.start(*riding))
        first, hmask, row, col = _sb_masks()
        after = jnp.where(row > col, 1.0, 0.0).astype(BF16)
        from_here = jnp.where(row >= col, 1.0, 0.0).astype(BF16)
        causal = col < row

        @pl.when(i == 0)
        def _():
            dk_acc[...] = jnp.zeros_like(dk_acc)
            dv_acc[...] = jnp.zeros_like(dv_acc)

        q = q_ref[0]
        do_ = do_ref[...]
        qh = (q * hmask[0], q * hmask[1])
        doh = (do_ * hmask[0], do_ * hmask[1])
        prod = do_.astype(F32) * o_ref[...]
        total = (jnp.sum(jnp.where(first, prod, 0.0), axis=1, keepdims=True),
                 jnp.sum(jnp.where(first, 0.0, prod), axis=1, keepdims=True))
        l_ref[...] = jnp.zeros_like(l_ref)
        e_ref[...] = jnp.zeros_like(e_ref)
        dq_acc[...] = jnp.zeros_like(dq_acc)

        def blocks(todo):
            chains = [(b, h) for b in range(len(todo)) for h in range(2)]
            starts = [pl.multiple_of(kb * t, t) for kb, _ in todo]
            ks = [k_ref[0, pl.ds(st, t), :] for st in starts]
            vs = [v_ref[0, pl.ds(st, t), :] for st in starts]
            ar = {(b, h): _sb_logits(qh[h], ks[b], todo[b][1]) for b, h in chains}
            dw = {(b, h): _dot(doh[h], vs[b], 1, 1) for b, h in chains}
            later = {bh: _split_mm(ar[bh][1], after) for bh in chains}
            carry = [l_ref[0], l_ref[1]]
            wb = {}
            for b, (_, mask) in enumerate(todo):
                for h in range(2):
                    wbh = jnp.exp(ar[b, h][0] + later[b, h] + carry[h])
                    wb[b, h] = (wbh if mask is None else jnp.where(mask, wbh, 0.0)).astype(BF16)
                carry = [carry[h] + jnp.sum(ar[b, h][1], axis=1, keepdims=True) for h in range(2)]
            dvs = {(b, h): _dot(wb[b, h], do_, 0, 0) for b, h in chains}
            e = {bh: dw[bh] * wb[bh].astype(F32) for bh in chains}
            suffix = {bh: _split_mm(e[bh], from_here) for bh in chains}
            e_carry = [e_ref[0], e_ref[1]]
            dz = {}
            for b, (_, mask) in enumerate(todo):
                for h in range(2):
                    before = total[h] - (suffix[b, h] + e_carry[h])
                    dzh = e[b, h] - jnp.exp(ar[b, h][0]) * (e[b, h] + before)
                    dz[b, h] = (dzh if mask is None else jnp.where(mask, dzh, 0.0)).astype(BF16)
                e_carry = [e_carry[h] + jnp.sum(e[b, h], axis=1, keepdims=True) for h in range(2)]
            dqs = {(b, h): _dot(dz[b, h], ks[b], 1, 0) for b, h in chains}
            dks = {(b, h): _dot(dz[b, h], q, 0, 0) for b, h in chains}
            dq = dq_acc[...]
            for b, st in enumerate(starts):
                dq = dq + jnp.where(first, dqs[b, 0], dqs[b, 1])
                dk_acc[pl.ds(st, t), :] += jnp.where(first, dks[b, 0], dks[b, 1])
                dv_acc[pl.ds(st, t), :] += jnp.where(first, dvs[b, 0], dvs[b, 1])
            dq_acc[...] = dq
            l_ref[0], l_ref[1] = carry
            e_ref[0], e_ref[1] = e_carry

        _sb_walk(i, blocks, l_ref, causal)
        dq_ref[...] = (dq_acc[...] * SB_SCALE).astype(dq_ref.dtype)

        @pl.when(i == nq - 1)
        def _():
            dk_ref[...] = dk_acc[...].astype(dk_ref.dtype)
            dv_ref[...] = dv_acc[...].astype(dv_ref.dtype)

        if comm is not None:
            pl.when(last_step)(lambda: comm.finish(*riding))

    blk = pl.BlockSpec((t, LANES), lambda p, i: (i, p))
    col_blk = pl.BlockSpec((s, LANES), lambda p, i: (0, p))
    sds = jax.ShapeDtypeStruct((s, SB_WIDTH), BF16)
    return pl.pallas_call(
        body, name="sb_bwd", grid=grid,
        in_specs=_sb_qkv_specs(s, 1) + [blk, blk] + c_in_specs,
        out_specs=[blk, col_blk, col_blk] + c_out_specs,
        out_shape=[sds, sds, sds] + c_out_shape,
        scratch_shapes=c_scratch + [pltpu.VMEM((2, t, 1), F32), pltpu.VMEM((2, t, 1), F32),
                                    pltpu.VMEM((t, LANES), F32), pltpu.VMEM((s, LANES), F32),
                                    pltpu.VMEM((s, LANES), F32)],
        compiler_params=_cparams(),
    )(qkv, qkv, qkv, o, do, *c_ins)


def _ret_log_gamma():
    lg = np.log1p(-np.exp2(-5.0 - np.arange(RET_HEADS, dtype=np.float32))).astype(np.float32)
    return jnp.asarray(np.broadcast_to(lg[:, None, None], (RET_HEADS, 8, LANES)).copy())


RET_SCRATCH = [pltpu.VMEM((RET_HEADS, RET_QK, RET_V), F32),
               pltpu.VMEM((RET_HEADS, RET_BLOCK, RET_BLOCK), F32),
               pltpu.VMEM((RET_HEADS, RET_BLOCK, 1), F32),
               pltpu.VMEM((RET_HEADS, RET_BLOCK, 1), F32)]


def _ret_begin(n, lg_ref, state, within, q_dec, k_dec):
    @pl.when(n == 0)
    def _():
        c = RET_BLOCK
        state[...] = jnp.zeros_like(state)
        row = lax.broadcasted_iota(jnp.int32, (c, c), 0)
        col = lax.broadcasted_iota(jnp.int32, (c, c), 1)
        rel = jnp.maximum(row - col, 0).astype(F32)
        idx = lax.broadcasted_iota(jnp.int32, (c, 1), 0).astype(F32)
        for h in range(RET_HEADS):
            lg = lg_ref[h, 0:1, 0:1]
            within[h] = jnp.where(row >= col, jnp.exp(lg * rel), 0.0)
            q_dec[h] = jnp.exp(lg * (idx + 1.0))
            k_dec[h] = jnp.exp(lg * (c - 1.0 - idx))


def _chunk_decay(lg_ref, h):
    return jnp.exp(lg_ref[h, 0:1, 0:1] * float(RET_BLOCK))


def _ret_heads(x, width):
    return [x[:, h * width:(h + 1) * width] for h in range(RET_HEADS)]


def _ret_specs(s, reverse=False):
    c = RET_BLOCK
    nc = s // c
    pos = (lambda n: nc - 1 - n) if reverse else (lambda n: n)
    q_spec = pl.BlockSpec((c, RET_QK_WIDTH), lambda n: (pos(n), 0))
    k_spec = pl.BlockSpec((c, RET_QK_WIDTH), lambda n: (pos(n), 1))
    v_spec = pl.BlockSpec((c, RET_V_WIDTH), lambda n: (pos(n), 0))
    lg_spec = pl.BlockSpec((RET_HEADS, 8, LANES), lambda n: (0, 0, 0))
    rope_spec = pl.BlockSpec((c, RET_QK), lambda n: (pos(n), 0))
    return nc, q_spec, k_spec, v_spec, lg_spec, rope_spec


def _ret_fwd(rqk, rvg, s):
    nc, q_spec, k_spec, v_spec, lg_spec, _ = _ret_specs(s)
    g_spec = pl.BlockSpec((RET_BLOCK, RET_V_WIDTH), lambda n: (n, 1))
    heads = range(RET_HEADS)

    def body(q_ref, k_ref, v_ref, g_ref, lg_ref, r_ref, y_ref, state, within, q_dec, k_dec):
        n = pl.program_id(0)
        _ret_begin(n, lg_ref, state, within, q_dec, k_dec)
        q, k = _ret_heads(q_ref[...], RET_QK), _ret_heads(k_ref[...], RET_QK)
        v, g = _ret_heads(v_ref[...], RET_V), _ret_heads(g_ref[...], RET_V)
        scores = [_dot(q[h].astype(BF16), k[h].astype(BF16), 1, 1) * within[h] for h in heads]
        cross = [_dot((q[h] * q_dec[h]).astype(BF16), state[h].astype(BF16), 1, 0) for h in heads]
        out = [_dot(scores[h].astype(BF16), v[h], 1, 0) + cross[h] for h in heads]
        grown = [_dot((k[h] * k_dec[h]).astype(BF16), v[h], 0, 0) for h in heads]
        for h in heads:
            sl = slice(h * RET_V, (h + 1) * RET_V)
            r_ref[:, sl] = out[h]
            xhat, _ = _norm(out[h])
            gh = g[h].astype(F32)
            y_ref[:, sl] = (gh * _sigmoid(gh) * xhat).astype(y_ref.dtype)
            state[h] = state[h] * _chunk_decay(lg_ref, h) + grown[h]

    return pl.pallas_call(
        body, name="ret_fwd", grid=(nc,),
        in_specs=[q_spec, k_spec, v_spec, g_spec, lg_spec],
        out_specs=[v_spec, v_spec],
        out_shape=[jax.ShapeDtypeStruct((s, RET_V_WIDTH), F32), jax.ShapeDtypeStruct((s, RET_V_WIDTH), BF16)],
        scratch_shapes=RET_SCRATCH,
        compiler_params=_cparams(),
    )(rqk, rqk, rvg, rvg, _ret_log_gamma())


def _rope_bwd(d, cos, sin):
    return d * cos + _swap_halves(d * sin)


def _ret_bwd_q(rqk, rv, d_out, cos2, sin2, s):
    nc, q_spec, k_spec, v_spec, lg_spec, rope_spec = _ret_specs(s)
    heads = range(RET_HEADS)

    def body(k_ref, v_ref, d_ref, lg_ref, cos_ref, sin_ref, dq_ref, state, within, q_dec, k_dec):
        n = pl.program_id(0)
        _ret_begin(n, lg_ref, state, within, q_dec, k_dec)
        k = _ret_heads(k_ref[...], RET_QK)
        v, d = _ret_heads(v_ref[...], RET_V), _ret_heads(d_ref[...], RET_V)
        cos, sin = cos_ref[...], sin_ref[...]
        d_scores = [_dot(d[h], v[h], 1, 1) * within[h] for h in heads]
        cross = [q_dec[h] * _dot(d[h], state[h].astype(BF16), 1, 1) for h in heads]
        dq = [_dot(d_scores[h].astype(BF16), k[h].astype(BF16), 1, 0) + cross[h] for h in heads]
        grown = [_dot((k[h] * k_dec[h]).astype(BF16), v[h], 0, 0) for h in heads]
        for h in heads:
            sl = slice(h * RET_QK, (h + 1) * RET_QK)
            dq_ref[:, sl] = (_rope_bwd(dq[h], cos, sin) * RET_SCALE).astype(dq_ref.dtype)
            state[h] = state[h] * _chunk_decay(lg_ref, h) + grown[h]

    return pl.pallas_call(
        body, name="ret_bwd_q", grid=(nc,),
        in_specs=[k_spec, v_spec, v_spec, lg_spec, rope_spec, rope_spec],
        out_specs=q_spec,
        out_shape=jax.ShapeDtypeStruct((s, RET_QK_WIDTH), BF16),
        scratch_shapes=RET_SCRATCH,
        compiler_params=_cparams(),
    )(rqk, rv, d_out, _ret_log_gamma(), cos2, sin2)


def _ret_bwd_kv(rqk, rv, d_out, cos2, sin2, s):
    nc, q_spec, k_spec, v_spec, lg_spec, rope_spec = _ret_specs(s, reverse=True)
    heads = range(RET_HEADS)

    def body(q_ref, k_ref, v_ref, d_ref, lg_ref, cos_ref, sin_ref, dk_ref, dv_ref, state, within, q_dec, k_dec):
        n = pl.program_id(0)
        _ret_begin(n, lg_ref, state, within, q_dec, k_dec)
        q, k = _ret_heads(q_ref[...], RET_QK), _ret_heads(k_ref[...], RET_QK)
        v, d = _ret_heads(v_ref[...], RET_V), _ret_heads(d_ref[...], RET_V)
        cos, sin = cos_ref[...], sin_ref[...]
        qb, kb = [q[h].astype(BF16) for h in heads], [k[h].astype(BF16) for h in heads]
        st = [state[h].astype(BF16) for h in heads]
        scores = [_dot(qb[h], kb[h], 1, 1) * within[h] for h in heads]
        d_scores = [_dot(d[h], v[h], 1, 1) * within[h] for h in heads]
        dk = [_dot(d_scores[h].astype(BF16), qb[h], 0, 0) + k_dec[h] * _dot(v[h], st[h], 1, 1) for h in heads]
        dv = [_dot(scores[h].astype(BF16), d[h], 0, 0) + k_dec[h] * _dot(kb[h], st[h], 1, 0) for h in heads]
        grown = [_dot((q[h] * q_dec[h]).astype(BF16), d[h], 0, 0) for h in heads]
        for h in heads:
            dk_ref[:, h * RET_QK:(h + 1) * RET_QK] = _rope_bwd(dk[h], cos, sin).astype(dk_ref.dtype)
            dv_ref[:, h * RET_V:(h + 1) * RET_V] = dv[h].astype(dv_ref.dtype)
            state[h] = state[h] * _chunk_decay(lg_ref, h) + grown[h]

    return pl.pallas_call(
        body, name="ret_bwd_kv", grid=(nc,),
        in_specs=[q_spec, k_spec, v_spec, v_spec, lg_spec, rope_spec, rope_spec],
        out_specs=[q_spec, v_spec],
        out_shape=[jax.ShapeDtypeStruct((s, RET_QK_WIDTH), BF16), jax.ShapeDtypeStruct((s, RET_V_WIDTH), BF16)],
        scratch_shapes=RET_SCRATCH,
        compiler_params=_cparams(),
    )(rqk, rqk, rv, d_out, _ret_log_gamma(), cos2, sin2)


def _xattn_probs(q, k):
    sc = _dot(q, k, 1, 1)
    sc = sc - jnp.max(sc, axis=-1, keepdims=True)
    p = jnp.exp(sc)
    return p / jnp.sum(p, axis=-1, keepdims=True)


def _xattn_fwd(qm, kv, s):
    tq = XATTN_ROWS

    def body(q_ref, kv_ref, o_ref):
        for h in range(MEM_HEADS):
            sl = slice(h * MEM_DIM, (h + 1) * MEM_DIM)
            sv = slice(D_MODEL + h * MEM_DIM, D_MODEL + (h + 1) * MEM_DIM)
            p = _xattn_probs(q_ref[:, sl], kv_ref[:, sl])
            o_ref[:, sl] = _dot(p.astype(BF16), kv_ref[:, sv], 1, 0).astype(o_ref.dtype)

    return pl.pallas_call(
        body, name="xattn_fwd", grid=(s // tq,),
        in_specs=[pl.BlockSpec((tq, D_MODEL), lambda i: (i, 0)),
                  pl.BlockSpec((MEM_LEN, 2 * D_MODEL), lambda i: (0, 0))],
        out_specs=pl.BlockSpec((tq, D_MODEL), lambda i: (i, 0)),
        out_shape=jax.ShapeDtypeStruct((s, D_MODEL), BF16),
        compiler_params=_cparams(),
    )(qm, kv)


def _xattn_bwd(qm, kv, do, s):
    tq = XATTN_ROWS

    def body(q_ref, kv_ref, do_ref, dq_ref, dkv_ref):
        i = pl.program_id(0)

        @pl.when(i == 0)
        def _():
            dkv_ref[...] = jnp.zeros_like(dkv_ref)

        for h in range(MEM_HEADS):
            sl = slice(h * MEM_DIM, (h + 1) * MEM_DIM)
            sv = slice(D_MODEL + h * MEM_DIM, D_MODEL + (h + 1) * MEM_DIM)
            q, k, v, d = q_ref[:, sl], kv_ref[:, sl], kv_ref[:, sv], do_ref[:, sl]
            p = _xattn_probs(q, k)
            dp = _dot(d, v, 1, 1)
            ds = (p * (dp - jnp.sum(p * dp, axis=-1, keepdims=True))).astype(BF16)
            dq_ref[:, sl] = (_dot(ds, k, 1, 0) * MEM_SCALE).astype(dq_ref.dtype)
            dkv_ref[:, sl] += _dot(ds, q, 0, 0)
            dkv_ref[:, sv] += _dot(p.astype(BF16), d, 0, 0)

    row_blk = pl.BlockSpec((tq, D_MODEL), lambda i: (i, 0))
    kv_blk = pl.BlockSpec((MEM_LEN, 2 * D_MODEL), lambda i: (0, 0))
    return pl.pallas_call(
        body, name="xattn_bwd", grid=(s // tq,),
        in_specs=[row_blk, kv_blk, row_blk],
        out_specs=[row_blk, kv_blk],
        out_shape=[jax.ShapeDtypeStruct((s, D_MODEL), BF16), jax.ShapeDtypeStruct((MEM_LEN, 2 * D_MODEL), F32)],
        compiler_params=_cparams(),
    )(qm, kv, do)


def _place():
    x, y, c = lax.axis_index("x"), lax.axis_index("y"), lax.axis_index("c")
    others = [(1 - x, y), (x, 1 - y), (1 - x, 1 - y)]
    return x, y, c, others


def _slab(ref, axis, chip, size):
    start = pl.multiple_of(chip * size, LANES if axis == 1 else 16)
    if axis == 0:
        return ref.at[pl.ds(start, size), :]
    return ref.at[:, pl.ds(start, size)]


class _CommPlan:
    def __init__(self, ins, out_shape, scratch, start, finish):
        self.ins, self.out_shape, self.scratch, self.start, self.finish = ins, out_shape, scratch, start, finish

    @property
    def specs(self):
        any_spec = pl.BlockSpec(memory_space=pl.ANY)
        return [any_spec] * len(self.ins), [any_spec] * len(self.out_shape)

    def split(self, refs):
        n_in, n_out = len(self.ins), len(self.out_shape)
        return refs[:n_in], refs[n_in:n_in + n_out], refs[n_in + n_out:]


def _gather_plan(names, shards):
    spec = {name: (shape, axis) for name, shape, axis in BIG}
    nw = len(names)

    def shard_half(ref, c):
        rows = ref.shape[0] // 2
        return ref.at[pl.ds(pl.multiple_of(c * rows, 16), rows), :]

    def region(ref, w, chip, c):
        shape, axis = spec[names[w]]
        size = shape[axis] // N_CHIPS
        if axis == 0:
            rows = size // 2
            return ref.at[pl.ds(pl.multiple_of(chip * size + c * rows, 16), rows), :]
        rows = shape[0] // 2
        return ref.at[pl.ds(pl.multiple_of(c * rows, 16), rows), pl.ds(pl.multiple_of(chip * size, LANES), size)]

    def ops(shard, full, sems):
        ici_send, ici_recv, d2d_send, d2d_recv, local_sems = sems
        x, y, c, others = _place()
        mine, sibling = 2 * x + y, (x, y, 1 - c)
        local, over_ici, arrived, passed_on, from_sibling = [], [], [], [], []
        for w in range(nw):
            shape, axis = spec[names[w]]
            local.append(pltpu.make_async_copy(shard[w], _slab(full[w], axis, mine, shape[axis] // N_CHIPS),
                                               local_sems.at[w]))
            for t, (qx, qy) in enumerate(others):
                n, theirs = 3 * w + t, 2 * qx + qy
                over_ici.append(pltpu.make_async_remote_copy(
                    src_ref=shard_half(shard[w], c), dst_ref=region(full[w], w, mine, c),
                    send_sem=ici_send.at[n], recv_sem=ici_recv.at[n], device_id=(qx, qy, c), device_id_type=MESH))
                arrived.append(pltpu.make_async_remote_copy(
                    src_ref=shard_half(shard[w], c), dst_ref=region(full[w], w, theirs, c),
                    send_sem=ici_send.at[n], recv_sem=ici_recv.at[n], device_id=(qx, qy, c), device_id_type=MESH))
                passed_on.append(pltpu.make_async_remote_copy(
                    src_ref=region(full[w], w, theirs, c), dst_ref=region(full[w], w, theirs, c),
                    send_sem=d2d_send.at[n], recv_sem=d2d_recv.at[n], device_id=sibling, device_id_type=MESH))
                from_sibling.append(pltpu.make_async_remote_copy(
                    src_ref=region(full[w], w, theirs, c), dst_ref=region(full[w], w, theirs, 1 - c),
                    send_sem=d2d_send.at[n], recv_sem=d2d_recv.at[n], device_id=sibling, device_id_type=MESH))
        return local, over_ici, arrived, passed_on, from_sibling

    def start(shard, full, sems):
        local, over_ici, _, _, _ = ops(shard, full, sems)
        for cp in local + over_ici:
            cp.start()

    def finish(shard, full, sems):
        local, over_ici, arrived, passed_on, from_sibling = ops(shard, full, sems)
        for got, onward in zip(arrived, passed_on, strict=True):
            got.wait_recv()
            onward.start()
        for got in from_sibling:
            got.wait_recv()
        for cp in over_ici + passed_on:
            cp.wait_send()
        for cp in local:
            cp.wait()

    dma = pltpu.SemaphoreType.DMA
    return _CommPlan(
        ins=[shards[name] for name in names],
        out_shape=[jax.ShapeDtypeStruct(spec[name][0], BF16) for name in names],
        scratch=[dma((3 * nw,)), dma((3 * nw,)), dma((3 * nw,)), dma((3 * nw,)), dma((nw,))],
        start=start, finish=finish)


def _shard_shape(shape, axis):
    return tuple(d // N_CHIPS if a == axis else d for a, d in enumerate(shape))


def _exchange_plan(names, grads):
    spec = {name: (shape, axis) for name, shape, axis in BIG}
    nw = len(names)

    def ops(grad, stack, sems):
        send_sems, recv_sems, local_sems = sems
        x, y, c, others = _place()
        mine = 2 * x + y
        me, sibling = (x, y, c), (x, y, 1 - c)

        def dev(px, py, pc):
            return 4 * px + 2 * py + pc

        def copy(w, n, src, slot, to):
            return pltpu.make_async_remote_copy(
                src_ref=src, dst_ref=stack[w].at[slot], send_sem=send_sems.at[7 * w + n],
                recv_sem=recv_sems.at[7 * w + n], device_id=to, device_id_type=MESH)

        local, first, arrived, passed_on, from_sibling = [], [], [], [], []
        for w in range(nw):
            shape, axis = spec[names[w]]
            size = shape[axis] // N_CHIPS
            own = _slab(grad[w], axis, mine, size)
            local.append(pltpu.make_async_copy(own, stack[w].at[dev(*me)], local_sems.at[w]))
            first.append(copy(w, 0, own, dev(*me), sibling))
            from_sibling.append(copy(w, 0, own, dev(*sibling), me))
            for t, (qx, qy) in enumerate(others):
                got = stack[w].at[dev(qx, qy, c)]
                first.append(copy(w, 1 + t, _slab(grad[w], axis, 2 * qx + qy, size), dev(*me), (qx, qy, c)))
                arrived.append(copy(w, 1 + t, got, dev(qx, qy, c), me))
                passed_on.append(copy(w, 4 + t, got, dev(qx, qy, c), sibling))
                from_sibling.append(copy(w, 4 + t, got, dev(qx, qy, 1 - c), me))
        return local, first, arrived, passed_on, from_sibling

    def start(grad, stack, sems):
        local, first, _, _, _ = ops(grad, stack, sems)
        for cp in local + first:
            cp.start()

    def finish(grad, stack, sems):
        local, first, arrived, passed_on, from_sibling = ops(grad, stack, sems)
        for got, onward in zip(arrived, passed_on, strict=True):
            got.wait_recv()
            onward.start()
        for got in from_sibling:
            got.wait_recv()
        for cp in first + passed_on:
            cp.wait_send()
        for cp in local:
            cp.wait()

    dma = pltpu.SemaphoreType.DMA
    return _CommPlan(
        ins=[grads[name] for name in names],
        out_shape=[jax.ShapeDtypeStruct((N_DEV,) + _shard_shape(*spec[name]), BF16) for name in names],
        scratch=[dma((7 * nw,)), dma((7 * nw,)), dma((nw,))],
        start=start, finish=finish)


def _adamw(w, g, m, v):
    m = ADAM_B1 * m + (1.0 - ADAM_B1) * g
    v = ADAM_B2 * v + (1.0 - ADAM_B2) * (g * g)
    m_hat = m / (1.0 - ADAM_B1 ** ADAM_STEP)
    v_hat = v / (1.0 - ADAM_B2 ** ADAM_STEP)
    delta = -ADAM_LR * (m_hat / (jnp.sqrt(v_hat) + ADAM_EPS) + ADAM_WD * w)
    return delta, m, v


def _reduce_adamw(name, stack, w, m, v):
    rows, cols = w.shape
    tr = next(t for t in (256, 128, 64) if rows % t == 0)

    def body(s_ref, w_ref, m_ref, v_ref, g_ref, d_ref, nm_ref, nv_ref):
        g = s_ref[0].astype(F32)
        for d in range(1, N_DEV):
            g = g + s_ref[d].astype(F32)
        g_ref[...] = g
        d_ref[...], nm_ref[...], nv_ref[...] = _adamw(w_ref[...], g, m_ref[...], v_ref[...])

    blk = pl.BlockSpec((tr, cols), lambda i: (i, 0))
    return pl.pallas_call(
        body, name=name, grid=(rows // tr,),
        in_specs=[pl.BlockSpec((N_DEV, tr, cols), lambda i: (0, i, 0)), blk, blk, blk],
        out_specs=[blk] * 4, out_shape=[jax.ShapeDtypeStruct((rows, cols), F32)] * 4,
        compiler_params=_cparams(),
    )(stack, w, m, v)


def _small_step(pack, w, m, v):
    def body(p_ref, w_ref, m_ref, v_ref, g_ref, d_ref, nm_ref, nv_ref, loss_ref, all_ref, send_sems, recv_sems):
        x, y, c, _ = _place()
        me = 4 * x + 2 * y + c
        all_ref[me] = p_ref[...]
        sent = []
        for n in range(1, N_DEV):
            peer = me ^ n
            cp = pltpu.make_async_remote_copy(
                src_ref=p_ref, dst_ref=all_ref.at[me], send_sem=send_sems.at[n - 1], recv_sem=recv_sems.at[n - 1],
                device_id=(peer // 4, (peer // 2) % 2, peer % 2), device_id_type=MESH)
            cp.start()
            sent.append(cp)
        for n in range(1, N_DEV):
            peer = me ^ n
            pltpu.make_async_remote_copy(
                src_ref=p_ref, dst_ref=all_ref.at[peer], send_sem=send_sems.at[n - 1], recv_sem=recv_sems.at[n - 1],
                device_id=(peer // 4, (peer // 2) % 2, peer % 2), device_id_type=MESH).wait_recv()
        for cp in sent:
            cp.wait_send()
        tot = all_ref[0]
        for d in range(1, N_DEV):
            tot = tot + all_ref[d]
        g = tot[:SMALL_ROWS]
        g_ref[...] = g
        d_ref[...], nm_ref[...], nv_ref[...] = _adamw(w_ref[...], g, m_ref[...], v_ref[...])
        loss_ref[...] = jnp.sum(jnp.sum(tot[SMALL_ROWS:], axis=1, keepdims=True), axis=0, keepdims=True)

    vm = pl.BlockSpec(memory_space=pltpu.VMEM)
    small = jax.ShapeDtypeStruct((SMALL_ROWS, LANES), F32)
    return pl.pallas_call(
        body, name="small_step",
        in_specs=[vm] * 4, out_specs=[vm] * 5,
        out_shape=[small] * 4 + [jax.ShapeDtypeStruct((1, 1), F32)],
        scratch_shapes=[pltpu.VMEM((N_DEV, PACK_ROWS, LANES), F32),
                        pltpu.SemaphoreType.DMA((N_DEV - 1,)), pltpu.SemaphoreType.DMA((N_DEV - 1,))],
    )(pack, w, m, v)


LATER_WEIGHTS = tuple(name for name, _, _ in BIG if name != "w_in")


def _layer_step(x, mem, tgt, shards, vec):
    s = x.shape[0]
    d = D_MODEL
    tm = min(ROW_TILE, s)
    tl = min(WIDE_TILE, s)
    xb, cos2, sin2, w_in = _prep(x, _gather_plan(("w_in",), shards))
    bf = lambda w: ((s, w), BF16)
    f32 = lambda w: ((s, w), F32)

    w_sb, w_rqk = w_in[:, :OFF_RET_Q], w_in[:, OFF_RET_Q:OFF_RET_V]
    w_rvg, w_gate = w_in[:, OFF_RET_V:OFF_GATE], w_in[:, OFF_GATE:]
    q_scale = lambda width, q_width, scale: jnp.concatenate(
        [jnp.full((1, q_width), scale, F32), jnp.ones((1, width - q_width), F32)], axis=1)
    n_groups = 3 * SB_WIDTH // LANES

    def sb_epi(acc, t, i, j):
        scaled = acc * t[0]
        return [jnp.stack([scaled[:, g * LANES:(g + 1) * LANES] for g in range(n_groups)])], []

    (sb_qkv,) = _mm(
        "in_sb", xb, w_sb, s, 3 * SB_WIDTH, d, tm=tl, tn=3 * SB_WIDTH, tk=d, epi=sb_epi,
        ins=[(q_scale(3 * SB_WIDTH, SB_WIDTH, SB_SCALE), *_rowvec(3 * SB_WIDTH))],
        outs=[((n_groups, s, LANES), BF16, (n_groups, tl, LANES), lambda i, j: (0, i, 0))])

    def rope_epi(acc, t, i, j):
        cos, sin, scale = t
        parts = []
        for g in range(acc.shape[1] // RET_QK):
            xg = acc[:, g * RET_QK:(g + 1) * RET_QK]
            parts.append(xg * cos + _swap_halves(xg) * sin)
        return [jnp.concatenate(parts, axis=1) * scale], []

    rope_in = ((tl, RET_QK), lambda i, j: (i, 0))
    (rqk,) = _mm("in_rqk", xb, w_rqk, s, 2 * RET_QK_WIDTH, d, tm=tl, tn=2 * RET_QK_WIDTH, tk=d, epi=rope_epi,
                 chunk=MXU_COLS,
                 ins=[(cos2, *rope_in), (sin2, *rope_in),
                      (q_scale(2 * RET_QK_WIDTH, RET_QK_WIDTH, RET_SCALE), *_rowvec(2 * RET_QK_WIDTH))],
                 outs=[(*f32(2 * RET_QK_WIDTH), *_tile(tl, 2 * RET_QK_WIDTH))])
    (rvg,) = _mm("in_rvg", xb, w_rvg, s, 2 * RET_V_WIDTH, d, tm=tl, tn=2 * RET_V_WIDTH, tk=d, chunk=MXU_COLS,
                 epi=_plain, outs=[(*bf(2 * RET_V_WIDTH), *_tile(tl, 2 * RET_V_WIDTH))])
    (gates,) = _mm("in_gate", xb, w_gate, s, 2 * d, d, tm=tl, tn=2 * d, tk=d, chunk=MXU_COLS,
                   epi=lambda acc, t, i, j: ([_sigmoid(acc + t[0])], []),
                   ins=[(vec["b_gate"], *_rowvec(2 * d))], outs=[(*bf(2 * d), *_tile(tl, 2 * d))])

    sb_out, sb_out_f32, *gathered = _sb_fwd(sb_qkv, s, comm=_gather_plan(LATER_WEIGHTS, shards))
    wt = dict(zip(LATER_WEIGHTS, gathered, strict=True))
    ret, gated = _ret_fwd(rqk, rvg, s)
    (y_sb,) = _mm("sb_o", sb_out, wt["w_sb_o"], s, d, SB_WIDTH, tm=tl, tn=d, tk=SB_WIDTH, epi=_plain,
                  outs=[(*bf(d), *_tile(tl, d))])
    y_ret, mixin = _mm(
        "ret_o", gated, wt["w_ret_o"], s, d, RET_V_WIDTH, tm=tl, tn=d, tk=RET_V_WIDTH, chunk=MXU_COLS,
        epi=lambda acc, t, i, j: ([acc, t[0].astype(F32) * t[2].astype(F32) + t[1].astype(F32) * acc], []),
        ins=[(gates, *_tile(tl, d)), (gates, *_tile(tl, d, 1)), (y_sb, *_tile(tl, d))],
        outs=[(*bf(d), *_tile(tl, d)), (*bf(d), *_tile(tl, d))])

    def ln_epi(acc, t, i, j):
        *res, g, b = t
        prev = res[0] if len(res) == 1 else res[0] * res[1] + res[2]
        xhat, rstd = _norm(DN_ALPHA * prev + acc)
        return [xhat * g + b, xhat, rstd], []

    full = _tile(tm, d)
    col1 = ((tm, 1), lambda i, j: (i, 0))
    vec_in = lambda name: (vec[name], *_rowvec(d))
    ln_outs = [(*bf(d), *full), (*f32(d), *full), ((s, 1), F32, *col1)]
    x1b, xhat1, rstd1 = _mm(
        "mix_o", mixin, wt["w_mix_o"], s, d, d, tm=tm, tn=d, tk=d, epi=ln_epi,
        ins=[(x, *full), vec_in("ln1_g"), vec_in("ln1_b")], outs=ln_outs)

    (qm,) = _mm("mem_q", x1b, wt["w_mem_q"], s, d, d, tm=tl, tn=d, tk=d,
                epi=lambda acc, t, i, j: ([acc * MEM_SCALE], []), outs=[(*bf(d), *_tile(tl, d))])
    (kv,) = _mm("mem_kv", mem, wt["w_mem_kv"], MEM_LEN, 2 * d, d, tm=MEM_LEN, tn=d, tk=d, epi=_plain,
                outs=[((MEM_LEN, 2 * d), BF16, *_tile(MEM_LEN, d))])
    att = _xattn_fwd(qm, kv, s)
    x2b, xhat2, rstd2 = _mm(
        "mem_o", att, wt["w_mem_o"], s, d, d, tm=tm, tn=d, tk=d, epi=ln_epi,
        ins=[(xhat1, *full), vec_in("ln1_g"), vec_in("ln1_b"), vec_in("ln2_g"), vec_in("ln2_b")], outs=ln_outs)

    fh = FFN_HIDDEN
    tf = fh // 2
    (f1,) = _mm("ffn_in1", x2b, wt["w_ffn_in"], s, fh, d, tm=tl, tn=tf, tk=d, epi=_plain, j_outer=True,
                outs=[(*bf(fh), *_tile(tl, tf))])

    def swiglu_epi(acc, t, i, j):
        a = t[0].astype(F32)
        return [acc, a * _sigmoid(a) * acc], []

    f2, act = _mm(
        "ffn_in2", x2b, wt["w_ffn_in"], s, fh, d, tm=tm, tn=fh, tk=d, b_off=(0, 1), epi=swiglu_epi, chunk=MXU_COLS,
        ins=[(f1, *_tile(tm, fh))], outs=[(*bf(fh), *_tile(tm, fh)), (*bf(fh), *_tile(tm, fh))])

    def head_epi(acc, t, i, j):
        prev_hat, prev_g, prev_b, g, b, target = t
        xhat, rstd = _norm(DN_ALPHA * (prev_hat * prev_g + prev_b) + acc)
        err = xhat * g + b - target
        dy = err * (1.0 / d)
        du = _norm_bwd(dy * g, xhat, rstd)
        return [du], [_colsum(dy * xhat), _colsum(dy), _colsum(err * err) * (0.5 / d)]

    vec_acc = ((1, d), F32)
    du3b, dg3, db3, loss_cols = _mm(
        "ffn_out", act, wt["w_ffn_out"], s, d, fh, tm=tm, tn=d, tk=fh, epi=head_epi,
        ins=[(xhat2, *full), vec_in("ln2_g"), vec_in("ln2_b"), vec_in("ln3_g"), vec_in("ln3_b"), (tgt, *full)],
        outs=[(*bf(d), *full)], accs=[vec_acc] * 3)

    grads = {}
    ts = min(SEQ_TILE, s)

    def wgrad(name, a, b, m, n, tm_, tn_, tk_=None):
        (g,) = _mm(name, a, b, m, n, a.shape[0], tm=tm_, tn=tn_, tk=tk_ or ts, ta=True, epi=_plain,
                   outs=[((m, n), BF16, *_tile(tm_, tn_))])
        return g

    def ffn_bwd_epi(acc, t, i, j):
        a, b = t[0].astype(F32), t[1].astype(F32)
        sg = _sigmoid(a)
        return [acc * b * (sg * (1.0 + a * (1.0 - sg))), acc * (a * sg)], []

    df1, df2 = _mm(
        "ffn_out_t", du3b, wt["w_ffn_out"], s, fh, d, tm=tm, tn=fh, tk=d, tb=True, epi=ffn_bwd_epi, chunk=MXU_COLS,
        ins=[(f1, *_tile(tm, fh)), (f2, *_tile(tm, fh))],
        outs=[(*bf(fh), *_tile(tm, fh)), (*bf(fh), *_tile(tm, fh))])
    grads["w_ffn_out"] = wgrad("g_ffn_out", act, du3b, fh, d, tf, d)
    grads["w_ffn_in"] = jnp.concatenate(
        [wgrad("g_ffn_in1", x2b, df1, d, fh, d, tf), wgrad("g_ffn_in2", x2b, df2, d, fh, d, tf)], axis=1)
    (dx2a,) = _mm("ffn_in1_t", df1, wt["w_ffn_in"], s, d, fh, tm=tm, tn=d, tk=fh, tb=True, epi=_plain,
                  outs=[(*f32(d), *full)])

    def ln_bwd(name, a, b, k, tk, b_off, more, scales, xhat, rstd, g):
        def epi(acc, t, i, j):
            *extra, xh, rs, gg = t
            dy = acc
            for e, sc in zip(extra, scales, strict=True):
                dy = dy + e.astype(F32) * sc
            return [_norm_bwd(dy * gg, xh, rs)], [_colsum(dy * xh), _colsum(dy)]

        return _mm(name, a, b, s, d, k, tm=tm, tn=d, tk=tk, tb=True, b_off=b_off, epi=epi,
                   ins=[(e, *full) for e in more] + [(xhat, *full), (rstd, *col1), (g, *_rowvec(d))],
                   outs=[(*bf(d), *full)], accs=[vec_acc] * 2)

    du2b, dg2, db2 = ln_bwd("ffn_in2_t", df2, wt["w_ffn_in"], fh, fh, (0, 1), [dx2a, du3b], [1.0, DN_ALPHA],
                            xhat2, rstd2, vec["ln2_g"])

    (datt,) = _mm("mem_o_t", du2b, wt["w_mem_o"], s, d, d, tm=tl, tn=d, tk=d, tb=True, epi=_plain,
                  outs=[(*bf(d), *_tile(tl, d))])
    grads["w_mem_o"] = wgrad("g_mem_o", att, du2b, d, d, d, d)
    dqm, dkv = _xattn_bwd(qm, kv, datt, s)
    grads["w_mem_q"] = wgrad("g_mem_q", x1b, dqm, d, d, d, d)
    grads["w_mem_kv"] = wgrad("g_mem_kv", mem, dkv, d, 2 * d, d, d, MEM_LEN)
    du1b, dg1, db1 = ln_bwd("mem_q_t", dqm, wt["w_mem_q"], d, d, (0, 0), [du2b], [DN_ALPHA],
                            xhat1, rstd1, vec["ln1_g"])

    def merge_bwd_epi(acc, t, i, j):
        g0, g1, ysb, yret = (v.astype(F32) for v in t)
        dgate0 = acc * ysb * (g0 * (1.0 - g0))
        dgate1 = acc * yret * (g1 * (1.0 - g1))
        return [dgate0, dgate1, acc * g0, acc * g1], [_colsum(dgate0), _colsum(dgate1)]

    dgate0, dgate1, dy_sb, dy_ret, dbg0, dbg1 = _mm(
        "mix_o_t", du1b, wt["w_mix_o"], s, d, d, tm=tm, tn=d, tk=d, tb=True, epi=merge_bwd_epi,
        ins=[(gates, *full), (gates, *_tile(tm, d, 1)), (y_sb, *full), (y_ret, *full)],
        outs=[(*bf(d), *full)] * 4, accs=[vec_acc] * 2)
    grads["w_mix_o"] = wgrad("g_mix_o", mixin, du1b, d, d, d, d)
    grads["w_sb_o"] = wgrad("g_sb_o", sb_out, dy_sb, SB_WIDTH, d, SB_WIDTH, d)
    grads["w_ret_o"] = wgrad("g_ret_o", gated, dy_ret, RET_V_WIDTH, d, RET_V_WIDTH, d)
    (dsb_out,) = _mm("sb_o_t", dy_sb, wt["w_sb_o"], s, SB_WIDTH, d, tm=tl, tn=SB_WIDTH, tk=d, tb=True, epi=_plain,
                     outs=[(*bf(SB_WIDTH), *_tile(tl, SB_WIDTH))])

    def gate_norm_bwd_epi(acc, t, i, j):
        r, g = t[0], t[1].astype(F32)
        drg, dret = [], []
        for h in range(acc.shape[1] // RET_V):
            sl = slice(h * RET_V, (h + 1) * RET_V)
            xhat, rstd = _norm(r[:, sl])
            gg, dd = g[:, sl], acc[:, sl]
            sg = _sigmoid(gg)
            drg.append(dd * xhat * (sg * (1.0 + gg * (1.0 - sg))))
            dret.append(_norm_bwd(dd * (gg * sg), xhat, rstd))
        return [jnp.concatenate(drg, axis=1), jnp.concatenate(dret, axis=1)], []

    drg, dret = _mm(
        "ret_o_t", dy_ret, wt["w_ret_o"], s, RET_V_WIDTH, d, tm=tm, tn=d, tk=d, tb=True, epi=gate_norm_bwd_epi,
        chunk=MXU_COLS,
        ins=[(ret, *full), (rvg, *_tile(tm, d, 1))],
        outs=[(*bf(RET_V_WIDTH), *full)] * 2)

    drq = _ret_bwd_q(rqk, rvg, dret, cos2, sin2, s)
    drk, drv = _ret_bwd_kv(rqk, rvg, dret, cos2, sin2, s)
    dsq, dsk, dsv, *stacked = _sb_bwd(sb_qkv, sb_out_f32, dsb_out, s, comm=_exchange_plan(LATER_WEIGHTS, grads))
    stacks = dict(zip(LATER_WEIGHTS, stacked, strict=True))

    dh = jnp.concatenate([dsq, dsk, dsv, drq, drk, drv, drg, dgate0, dgate1], axis=1)
    grads["w_in"] = wgrad("g_in", xb, dh, d, IN_WIDTH, d, IN_WIDTH // N_CHIPS)
    grad_x, stacks["w_in"] = _mm(
        "in_t", dh, w_in, s, d, IN_WIDTH, tm=tl, tn=d, tk=IN_WIDTH // N_CHIPS, tb=True,
        epi=lambda acc, t, i, j: ([acc + DN_ALPHA * t[0].astype(F32)], []),
        ins=[(du1b, *_tile(tl, d))], outs=[(*f32(d), *_tile(tl, d))], comm=_exchange_plan(("w_in",), grads))

    small = {"b_gate": jnp.concatenate([dbg0, dbg1], axis=1), "ln1_g": dg1, "ln1_b": db1, "ln2_g": dg2,
             "ln2_b": db2, "ln3_g": dg3, "ln3_b": db3}
    return grad_x, stacks, small, loss_cols


def kernel(x, mem, w_in, b_gate, w_sb_o, w_ret_o, w_mix_o, ln1_g, ln1_b, w_mem_q, w_mem_kv, w_mem_o, ln2_g, ln2_b, w_ffn_in, w_ffn_out, ln3_g, ln3_b, loss_target, m_w_in, m_b_gate, m_w_sb_o, m_w_ret_o, m_w_mix_o, m_ln1_g, m_ln1_b, m_w_mem_q, m_w_mem_kv, m_w_mem_o, m_ln2_g, m_ln2_b, m_w_ffn_in, m_w_ffn_out, m_ln3_g, m_ln3_b, v_w_in, v_b_gate, v_w_sb_o, v_w_ret_o, v_w_mix_o, v_ln1_g, v_ln1_b, v_w_mem_q, v_w_mem_kv, v_w_mem_o, v_ln2_g, v_ln2_b, v_w_ffn_in, v_w_ffn_out, v_ln3_g, v_ln3_b):
    given = dict(locals())
    s = x.shape[1]
    x2d = x.reshape(s, D_MODEL)
    tgt = loss_target.reshape(s, D_MODEL)
    mem2d = mem.reshape(MEM_LEN, D_MODEL)
    shard = {name: given[name].reshape(_shard_shape(shape, axis)) for name, shape, axis in BIG}
    vec = {name: given[name] for name in SMALL}

    shards_bf = {name: _cast_bf16("cast_" + name, shard[name]) for name, _, _ in BIG}

    grad_x, stacks, small, loss_cols = _layer_step(x2d, mem2d, tgt, shards_bf, vec)

    out = {}
    for name, shape, axis in BIG:
        stack = stacks[name]
        shp = given[name].shape
        res = _reduce_adamw("adamw_" + name, stack, shard[name], given["m_" + name].reshape(stack.shape[1:]),
                            given["v_" + name].reshape(stack.shape[1:]))
        out[name] = [r.reshape(shp) for r in res]

    pack = jnp.concatenate([small[name] for name in SMALL] + [loss_cols], axis=1).reshape(PACK_ROWS, LANES)
    cat = lambda pre: jnp.concatenate([given[pre + name] for name in SMALL], axis=1).reshape(SMALL_ROWS, LANES)
    *res, loss = _small_step(pack, cat(""), cat("m_"), cat("v_"))
    flat = [r.reshape(1, SMALL_LEN) for r in res]
    off = 0
    for name in SMALL:
        n = given[name].shape[1]
        out[name] = [r[:, off:off + n] for r in flat]
        off += n

    return (loss.reshape(()), grad_x.reshape(x.shape),
            *[out[name][0] for name in WEIGHT_ORDER], *[out[name][1] for name in WEIGHT_ORDER],
            *[out[name][2] for name in WEIGHT_ORDER], *[out[name][3] for name in WEIGHT_ORDER])
```

```python
import functools

import jax
import jax.numpy as jnp
import numpy as np
from jax import lax
from jax.experimental import pallas as pl
from jax.experimental.pallas import tpu as pltpu

F32, BF16 = jnp.float32, jnp.bfloat16
MESH = pl.DeviceIdType.MESH

D_MODEL = 1024
MEM_LEN = 256
SB_HEADS, SB_DIM, SB_WIDTH = 8, 64, 512
RET_HEADS, RET_QK, RET_V = 4, 128, 256
RET_QK_WIDTH, RET_V_WIDTH = 512, 1024
ROPE_BASE = 10000.0
MEM_HEADS, MEM_DIM = 4, 256
FFN_HIDDEN = 2816
IN_WIDTH = 6656
OFF_RET_Q, OFF_RET_V, OFF_RET_G, OFF_GATE = 1536, 2560, 3584, 4608
DN_ALPHA = 2.0 ** 0.25
LN_EPS = 1e-5
SB_SCALE = SB_DIM ** -0.5
SB_DEAD = -110.0
RET_SCALE = RET_QK ** -0.5
MEM_SCALE = MEM_DIM ** -0.5
ADAM_LR, ADAM_B1, ADAM_B2, ADAM_EPS, ADAM_WD, ADAM_STEP = 0.001, 0.9, 0.999, 1e-08, 0.01, 10

N_DEV, N_CHIPS = 8, 4

LANES = 128
MXU_COLS = 256
VMEM_LIMIT_BYTES = 52 * 2 ** 20
ROW_TILE = 512
WIDE_TILE = 1024
SEQ_TILE = 2048
SB_BLOCK = 256
RET_BLOCK = 256
XATTN_ROWS = 512

BIG = (
    ("w_in", (D_MODEL, IN_WIDTH), 1),
    ("w_sb_o", (SB_WIDTH, D_MODEL), 1),
    ("w_ret_o", (RET_V_WIDTH, D_MODEL), 0),
    ("w_mix_o", (D_MODEL, D_MODEL), 0),
    ("w_mem_q", (D_MODEL, D_MODEL), 0),
    ("w_mem_kv", (D_MODEL, 2 * D_MODEL), 1),
    ("w_mem_o", (D_MODEL, D_MODEL), 0),
    ("w_ffn_in", (D_MODEL, 2 * FFN_HIDDEN), 1),
    ("w_ffn_out", (FFN_HIDDEN, D_MODEL), 0),
)
SMALL = ("b_gate", "ln1_g", "ln1_b", "ln2_g", "ln2_b", "ln3_g", "ln3_b")
SMALL_LEN = 2 * D_MODEL + 6 * D_MODEL
SMALL_ROWS = SMALL_LEN // LANES
PACK_ROWS = SMALL_ROWS + D_MODEL // LANES
WEIGHT_ORDER = ("w_in", "b_gate", "w_sb_o", "w_ret_o", "w_mix_o", "ln1_g", "ln1_b", "w_mem_q", "w_mem_kv",
                "w_mem_o", "ln2_g", "ln2_b", "w_ffn_in", "w_ffn_out", "ln3_g", "ln3_b")


def _cparams():
    return pltpu.CompilerParams(vmem_limit_bytes=VMEM_LIMIT_BYTES)


def _dot(a, b, ca, cb):
    return lax.dot_general(a, b, (((ca,), (cb,)), ((), ())), preferred_element_type=F32)


def _sigmoid(x):
    return 1.0 / (1.0 + jnp.exp(-x))


def _mm(name, a, b, m, n, k, *, tm, tn, tk, epi, outs, ins=(), accs=(), ta=False, tb=False,
        a_off=(0, 0), b_off=(0, 0), j_outer=False, comm=None, chunk=None, skew=False):
    assert m % tm == 0 and n % tn == 0 and k % tk == 0, (name, m, n, k, tm, tn, tk)
    assert chunk is None or (k == tk and tn % chunk == 0), name
    ni, nj, nk = m // tm, n // tn, k // tk
    assert not accs or nj == 1, name
    assert not skew or (nk == 1 and nj == 1 and chunk is None and not j_outer and not ta), name
    ij = (lambda g0, g1: (g1, g0)) if j_outer else (lambda g0, g1: (g0, g1))
    lead = (lambda i: jnp.minimum(i, ni - 1)) if skew else (lambda i: i)
    lag = (lambda i: jnp.maximum(i - 1, 0)) if skew else (lambda i: i)

    def spec(block, index):
        return pl.BlockSpec(block, lambda g0, g1, kk: index(*ij(g0, g1), kk))

    if ta:
        a_spec = spec((tk, tm), lambda i, j, kk: (kk + a_off[0], i + a_off[1]))
    else:
        a_spec = spec((tm, tk), lambda i, j, kk: (lead(i) + a_off[0], kk + a_off[1]))
    if tb:
        b_spec = spec((tn, tk), lambda i, j, kk: (j + b_off[0], kk + b_off[1]))
    else:
        b_spec = spec((tk, tn), lambda i, j, kk: (kk + b_off[0], j + b_off[1]))
    in_specs = [a_spec, b_spec]
    for _, bs, im in ins:
        in_specs.append(spec(bs, lambda i, j, kk, im=im: im(lag(i), j)))
    out_specs, out_shape = [], []
    for shape, dtype, bs, im in outs:
        out_specs.append(spec(bs, lambda i, j, kk, im=im: im(lag(i), j)))
        out_shape.append(jax.ShapeDtypeStruct(shape, dtype))
    for shape, dtype in accs:
        out_specs.append(spec(shape, lambda i, j, kk, nd=len(shape): (0,) * nd))
        out_shape.append(jax.ShapeDtypeStruct(shape, dtype))
    n_in, n_out, n_acc = len(ins), len(outs), len(accs)
    ca, cb = (0 if ta else 1), (1 if tb else 0)
    grid = (ni + 1, 1, 1) if skew else (*ij(ni, nj), nk)
    first_tile = 1 if skew else 0
    comm_ins, comm_outs, comm_scratch = [], [], []
    if comm is not None:
        comm_in_specs, comm_out_specs = comm.specs
        comm_ins, comm_outs, comm_scratch = list(comm.ins), list(comm.out_shape), list(comm.scratch)
        in_specs += comm_in_specs
        out_specs += comm_out_specs
        out_shape += comm_outs
    n_ci, n_co = len(comm_ins), len(comm_outs)

    def body(*refs):
        a_ref, b_ref = refs[:2]
        in_refs = refs[2:2 + n_in]
        ci_refs = refs[2 + n_in:2 + n_in + n_ci]
        rest = refs[2 + n_in + n_ci:]
        out_refs, acc_refs = rest[:n_out], rest[n_out:n_out + n_acc]
        co_refs = rest[n_out + n_acc:n_out + n_acc + n_co]
        scratch = rest[n_out + n_acc + n_co:]
        sem_refs, scratch = scratch[:len(comm_scratch)], scratch[len(comm_scratch):]
        (i, j), kk = ij(pl.program_id(0), pl.program_id(1)), pl.program_id(2)
        if comm is not None:
            first_step, last_step = _grid_ends(grid)
            pl.when(first_step)(lambda: comm.start(ci_refs, co_refs, sem_refs))
        def finish(acc, cols=slice(None)):
            def of(r):
                return r[..., cols] if r.shape[-1] == tn else r[...]

            o_tiles, a_tiles = epi(acc, [of(r) for r in in_refs], lag(i), j)
            for r, t in zip(out_refs, o_tiles, strict=True):
                r[..., cols] = t.astype(r.dtype)
            if n_acc:
                @pl.when(i == first_tile)
                def _():
                    for r, t in zip(acc_refs, a_tiles, strict=True):
                        r[..., cols] = t

                @pl.when(i > first_tile)
                def _():
                    for r, t in zip(acc_refs, a_tiles, strict=True):
                        r[..., cols] += t

        if skew:
            bufs = scratch[:2]

            @pl.when(i == 0)
            def _():
                bufs[1][...] = jnp.zeros_like(bufs[1])

            def half(mine, other):
                mine[...] = _dot(a_ref[...].astype(BF16), b_ref[...].astype(BF16), ca, cb)
                finish(other[...])

            pl.when(i % 2 == 0)(lambda: half(bufs[0], bufs[1]))
            pl.when(i % 2 == 1)(lambda: half(bufs[1], bufs[0]))
            if comm is not None:
                pl.when(last_step)(lambda: comm.finish(ci_refs, co_refs, sem_refs))
            return

        if chunk is not None:
            a_tile = a_ref[...].astype(BF16)
            for c0 in range(0, tn, chunk):
                cols = slice(c0, c0 + chunk)
                b_part = b_ref[cols, :] if tb else b_ref[:, cols]
                finish(_dot(a_tile, b_part.astype(BF16), ca, cb), cols)
            if comm is not None:
                pl.when(last_step)(lambda: comm.finish(ci_refs, co_refs, sem_refs))
            return

        part = _dot(a_ref[...].astype(BF16), b_ref[...].astype(BF16), ca, cb)
        if nk == 1:
            finish(part)
        else:
            acc_ref = scratch[0]

            @pl.when(kk == 0)
            def _():
                acc_ref[...] = part

            @pl.when(kk > 0)
            def _():
                acc_ref[...] += part

            @pl.when(kk == nk - 1)
            def _():
                finish(acc_ref[...])

        if comm is not None:
            pl.when(last_step)(lambda: comm.finish(ci_refs, co_refs, sem_refs))

    res = pl.pallas_call(
        body, name=name, grid=grid, in_specs=in_specs, out_specs=out_specs, out_shape=out_shape,
        scratch_shapes=comm_scratch + [pltpu.VMEM((tm, tn), F32)] * (2 if skew else 1 if nk > 1 else 0),
        compiler_params=_cparams(),
    )(a, b, *[x for x, _, _ in ins], *comm_ins)
    return res


def _grid_ends(grid):
    ids = [pl.program_id(ax) for ax in range(len(grid))]
    first = functools.reduce(jnp.logical_and, [p == 0 for p in ids])
    last = functools.reduce(jnp.logical_and, [p == n - 1 for p, n in zip(ids, grid, strict=True)])
    return first, last


def _tile(tm, tn, dj=0):
    return (tm, tn), (lambda i, j: (i, j + dj))


def _rowvec(tn, dj=0):
    return (1, tn), (lambda i, j: (0, j + dj))


def _plain(acc, tiles, i, j):
    return [acc], []


def _ew(name, fn, ins, outs, rows, tr):
    assert rows % tr == 0, (name, rows, tr)
    in_specs = []
    for x in ins:
        if x.shape[0] == rows:
            in_specs.append(pl.BlockSpec((tr, x.shape[1]), lambda i: (i, 0)))
        else:
            in_specs.append(pl.BlockSpec(x.shape, lambda i: (0, 0)))
    n_in = len(ins)

    def body(*refs):
        res = fn(*[r[...] for r in refs[:n_in]])
        for r, t in zip(refs[n_in:], res, strict=True):
            r[...] = t.astype(r.dtype)

    return pl.pallas_call(
        body, name=name, grid=(rows // tr,), in_specs=in_specs,
        out_specs=[pl.BlockSpec((tr, w), lambda i: (i, 0)) for w, _ in outs],
        out_shape=[jax.ShapeDtypeStruct((rows, w), dt) for w, dt in outs],
        compiler_params=_cparams(),
    )(*ins)


def _cast_bf16(name, x):
    rows = x.shape[0]
    tr = next(t for t in (512, 256, 64) if rows % t == 0)
    return _ew(name, lambda v: (v,), [x], [(x.shape[1], BF16)], rows, tr)[0]


def _prep(x, comm):
    s = x.shape[0]
    half = RET_QK // 2
    inv = 1.0 / (ROPE_BASE ** (jnp.arange(half, dtype=F32) / half))
    inv2 = jnp.concatenate([inv, inv]).reshape(1, RET_QK)
    sign = jnp.concatenate([-jnp.ones((half,), F32), jnp.ones((half,), F32)]).reshape(1, RET_QK)
    tr = min(ROW_TILE, s)
    grid = (s // tr,)
    c_in_specs, c_out_specs, c_out_shape, c_scratch, c_ins, split = _host(comm, 3, 3)

    def body(*refs):
        (x_ref, inv_ref, sign_ref), (xb_ref, cos_ref, sin_ref), _, riding = split(refs)
        i = pl.program_id(0)
        first_step, last_step = _grid_ends(grid)
        pl.when(first_step)(lambda: comm.start(*riding))
        xb_ref[...] = x_ref[...].astype(BF16)
        pos = (lax.broadcasted_iota(jnp.int32, (tr, RET_QK), 0) + i * tr).astype(F32)
        ang = pos * inv_ref[...]
        cos_ref[...] = jnp.cos(ang)
        sin_ref[...] = jnp.sin(ang) * sign_ref[...]
        pl.when(last_step)(lambda: comm.finish(*riding))

    vec = pl.BlockSpec((1, RET_QK), lambda i: (0, 0))
    row = lambda w: pl.BlockSpec((tr, w), lambda i: (i, 0))
    return pl.pallas_call(
        body, name="prep", grid=grid,
        in_specs=[row(D_MODEL), vec, vec] + c_in_specs,
        out_specs=[row(D_MODEL), row(RET_QK), row(RET_QK)] + c_out_specs,
        out_shape=[jax.ShapeDtypeStruct((s, D_MODEL), BF16), jax.ShapeDtypeStruct((s, RET_QK), F32),
                   jax.ShapeDtypeStruct((s, RET_QK), F32)] + c_out_shape,
        scratch_shapes=c_scratch, compiler_params=_cparams(),
    )(x, inv2, sign, *c_ins)


def _swap_halves(x):
    return pltpu.roll(x, RET_QK // 2, 1)


def _norm(u):
    mu = jnp.mean(u, axis=-1, keepdims=True)
    d = u - mu
    var = jnp.mean(d * d, axis=-1, keepdims=True)
    rstd = lax.rsqrt(var + LN_EPS)
    return d * rstd, rstd


def _norm_bwd(dxh, xhat, rstd):
    m1 = jnp.mean(dxh, axis=-1, keepdims=True)
    m2 = jnp.mean(dxh * xhat, axis=-1, keepdims=True)
    return rstd * (dxh - m1 - xhat * m2)


def _colsum(t):
    return jnp.sum(t, axis=0, keepdims=True)


def _split_mm(t, tri):
    hi = t.astype(BF16)
    lo = (t - hi.astype(F32)).astype(BF16)
    return _dot(hi, tri, 1, 0) + _dot(lo, tri, 1, 0)


def _sb_masks():
    t = SB_BLOCK
    lane = lax.broadcasted_iota(jnp.int32, (1, LANES), 1)
    first = lane < SB_DIM
    m0 = jnp.where(first, 1.0, 0.0).astype(BF16)
    m1 = jnp.where(first, 0.0, 1.0).astype(BF16)
    row = lax.broadcasted_iota(jnp.int32, (t, t), 0)
    col = lax.broadcasted_iota(jnp.int32, (t, t), 1)
    return first, (m0, m1), row, col


def _sb_logits(qh, k, causal):
    z = _dot(qh, k, 1, 1)
    lp = jnp.log(1.0 + jnp.exp(-jnp.abs(z)))
    a = jnp.minimum(z, 0.0) - lp
    r = jnp.minimum(-z, 0.0) - lp
    if causal is not None:
        r = jnp.where(causal, r, 0.0)
    return a, r


def _sb_walk(i, blocks, l_ref, causal):
    pl.when(i == 0)(lambda: blocks([(i, causal)]))
    pl.when(i > 0)(lambda: blocks([(i, causal), (i - 1, None)]))

    def alive():
        top = jnp.max(functools.reduce(jnp.maximum, [l_ref[c] for c in range(l_ref.shape[0])]))
        return jnp.where(top > SB_DEAD, 1, 0)

    def cond(c):
        return jnp.logical_and(c[0] < i, c[1] > 0)

    def step(c):
        blocks([(i - 1 - c[0], None)])
        return c[0] + 1, alive()

    lax.while_loop(cond, step, (jnp.int32(1), alive()))


def _host(comm, n_in, n_out):
    if comm is None:
        return [], [], [], [], [], lambda refs: (refs[:n_in], refs[n_in:n_in + n_out], refs[n_in + n_out:], None)
    in_specs, out_specs = comm.specs
    n_ci, n_co, n_sem = len(comm.ins), len(comm.out_shape), len(comm.scratch)

    def split(refs):
        ins, ci = refs[:n_in], refs[n_in:n_in + n_ci]
        rest = refs[n_in + n_ci:]
        outs, co = rest[:n_out], rest[n_out:n_out + n_co]
        sems, scratch = rest[n_out + n_co:n_out + n_co + n_sem], rest[n_out + n_co + n_sem:]
        return ins, outs, scratch, (ci, co, sems)

    return in_specs, out_specs, list(comm.out_shape), list(comm.scratch), list(comm.ins), split


def _sb_qkv_specs(s, g):
    groups = SB_HEADS // 2 // g
    return [pl.BlockSpec((g, SB_BLOCK, LANES), lambda p, i: (p, i, 0)),
            pl.BlockSpec((g, s, LANES), lambda p, i: (groups + p, 0, 0)),
            pl.BlockSpec((g, s, LANES), lambda p, i: (2 * groups + p, 0, 0))]


def _sb_fwd(qkv, s, comm=None):
    t = SB_BLOCK
    g = 2
    nq = s // t
    grid = (SB_HEADS // 2 // g, nq)
    c_in_specs, c_out_specs, c_out_shape, c_scratch, c_ins, split = _host(comm, 3, 2)

    def body(*refs):
        (q_ref, k_ref, v_ref), (o_ref, of_ref), (l_ref, acc_ref), riding = split(refs)
        i = pl.program_id(1)
        if comm is not None:
            first_step, last_step = _grid_ends(grid)
            pl.when(first_step)(lambda: comm.start(*riding))
        first, hmask, row, col = _sb_masks()
        after = jnp.where(row > col, 1.0, 0.0).astype(BF16)
        causal = col < row
        heads = [(p, h) for p in range(g) for h in range(2)]
        qh = {(p, h): q_ref[p] * hmask[h] for p, h in heads}
        l_ref[...] = jnp.zeros_like(l_ref)
        acc_ref[...] = jnp.zeros_like(acc_ref)

        def blocks(todo):
            chains = [(b, p, h) for b in range(len(todo)) for p, h in heads]
            starts = [pl.multiple_of(kb * t, t) for kb, _ in todo]
            ks = {(b, p): k_ref[p, pl.ds(st, t), :] for b, st in enumerate(starts) for p in range(g)}
            vs = {(b, p): v_ref[p, pl.ds(st, t), :] for b, st in enumerate(starts) for p in range(g)}
            ar = {(b, p, h): _sb_logits(qh[p, h], ks[b, p], todo[b][1]) for b, p, h in chains}
            later = {c: _split_mm(ar[c][1], after) for c in chains}
            carry = {(p, h): l_ref[2 * p + h] for p, h in heads}
            w = {}
            for b, (_, mask) in enumerate(todo):
                for p, h in heads:
                    wc = jnp.exp(ar[b, p, h][0] + later[b, p, h] + carry[p, h])
                    w[b, p, h] = wc if mask is None else jnp.where(mask, wc, 0.0)
                carry = {(p, h): carry[p, h] + jnp.sum(ar[b, p, h][1], axis=1, keepdims=True) for p, h in heads}
            pv = {(b, p, h): _dot(w[b, p, h].astype(BF16), vs[b, p], 1, 0) for b, p, h in chains}
            for p in range(g):
                lanes = slice(p * LANES, (p + 1) * LANES)
                acc = acc_ref[:, lanes]
                for b in range(len(todo)):
                    acc = acc + jnp.where(first, pv[b, p, 0], pv[b, p, 1])
                acc_ref[:, lanes] = acc
            for p, h in heads:
                l_ref[2 * p + h] = carry[p, h]

        _sb_walk(i, blocks, l_ref, causal)
        o_ref[...] = acc_ref[...].astype(o_ref.dtype)
        of_ref[...] = acc_ref[...]
        if comm is not None:
            pl.when(last_step)(lambda: comm.finish(*riding))

    blk = pl.BlockSpec((t, g * LANES), lambda p, i: (i, p))
    return pl.pallas_call(
        body, name="sb_fwd", grid=grid,
        in_specs=_sb_qkv_specs(s, g) + c_in_specs,
        out_specs=[blk, blk] + c_out_specs,
        out_shape=[jax.ShapeDtypeStruct((s, SB_WIDTH), BF16), jax.ShapeDtypeStruct((s, SB_WIDTH), F32)] + c_out_shape,
        scratch_shapes=c_scratch + [pltpu.VMEM((2 * g, t, 1), F32), pltpu.VMEM((t, g * LANES), F32)],
        compiler_params=_cparams(),
    )(qkv, qkv, qkv, *c_ins)


def _sb_bwd(qkv, o, do, s, comm=None):
    t = SB_BLOCK
    nq = s // t
    grid = (SB_HEADS // 2, nq)
    c_in_specs, c_out_specs, c_out_shape, c_scratch, c_ins, split = _host(comm, 5, 3)

    def body(*refs):
        ((q_ref, k_ref, v_ref, o_ref, do_ref), (dq_ref, dk_ref, dv_ref),
         (l_ref, e_ref, dq_acc, dk_acc, dv_acc), riding) = split(refs)
        i = pl.program_id(1)
        if comm is not None:
            first_step, last_step = _grid_ends(grid)
            pl.when(first_step)(lambda: comm.start(*riding))
        first, hmask, row, col = _sb_masks()
        after = jnp.where(row > col, 1.0, 0.0).astype(BF16)
        from_here = jnp.where(row >= col, 1.0, 0.0).astype(BF16)
        causal = col < row

        @pl.when(i == 0)
        def _():
            dk_acc[...] = jnp.zeros_like(dk_acc)
            dv_acc[...] = jnp.zeros_like(dv_acc)

        q = q_ref[0]
        do_ = do_ref[...]
        qh = (q * hmask[0], q * hmask[1])
        doh = (do_ * hmask[0], do_ * hmask[1])
        prod = do_.astype(F32) * o_ref[...]
        total = (jnp.sum(jnp.where(first, prod, 0.0), axis=1, keepdims=True),
                 jnp.sum(jnp.where(first, 0.0, prod), axis=1, keepdims=True))
        l_ref[...] = jnp.zeros_like(l_ref)
        e_ref[...] = jnp.zeros_like(e_ref)
        dq_acc[...] = jnp.zeros_like(dq_acc)

        def blocks(todo):
            chains = [(b, h) for b in range(len(todo)) for h in range(2)]
            starts = [pl.multiple_of(kb * t, t) for kb, _ in todo]
            ks = [k_ref[0, pl.ds(st, t), :] for st in starts]
            vs = [v_ref[0, pl.ds(st, t), :] for st in starts]
            ar = {(b, h): _sb_logits(qh[h], ks[b], todo[b][1]) for b, h in chains}
            dw = {(b, h): _dot(doh[h], vs[b], 1, 1) for b, h in chains}
            later = {bh: _split_mm(ar[bh][1], after) for bh in chains}
            carry = [l_ref[0], l_ref[1]]
            wb = {}
            for b, (_, mask) in enumerate(todo):
                for h in range(2):
                    wbh = jnp.exp(ar[b, h][0] + later[b, h] + carry[h])
                    wb[b, h] = (wbh if mask is None else jnp.where(mask, wbh, 0.0)).astype(BF16)
                carry = [carry[h] + jnp.sum(ar[b, h][1], axis=1, keepdims=True) for h in range(2)]
            dvs = {(b, h): _dot(wb[b, h], do_, 0, 0) for b, h in chains}
            e = {bh: dw[bh] * wb[bh].astype(F32) for bh in chains}
            suffix = {bh: _split_mm(e[bh], from_here) for bh in chains}
            e_carry = [e_ref[0], e_ref[1]]
            dz = {}
            for b, (_, mask) in enumerate(todo):
                for h in range(2):
                    before = total[h] - (suffix[b, h] + e_carry[h])
                    dzh = e[b, h] - jnp.exp(ar[b, h][0]) * (e[b, h] + before)
                    dz[b, h] = (dzh if mask is None else jnp.where(mask, dzh, 0.0)).astype(BF16)
                e_carry = [e_carry[h] + jnp.sum(e[b, h], axis=1, keepdims=True) for h in range(2)]
            dqs = {(b, h): _dot(dz[b, h], ks[b], 1, 0) for b, h in chains}
            dks = {(b, h): _dot(dz[b, h], q, 0, 0) for b, h in chains}
            dq = dq_acc[...]
            for b, st in enumerate(starts):
                dq = dq + jnp.where(first, dqs[b, 0], dqs[b, 1])
                dk_acc[pl.ds(st, t), :] += jnp.where(first, dks[b, 0], dks[b, 1])
                dv_acc[pl.ds(st, t), :] += jnp.where(first, dvs[b, 0], dvs[b, 1])
            dq_acc[...] = dq
            l_ref[0], l_ref[1] = carry
            e_ref[0], e_ref[1] = e_carry

        _sb_walk(i, blocks, l_ref, causal)
        dq_ref[...] = (dq_acc[...] * SB_SCALE).astype(dq_ref.dtype)

        @pl.when(i == nq - 1)
        def _():
            dk_ref[...] = dk_acc[...].astype(dk_ref.dtype)
            dv_ref[...] = dv_acc[...].astype(dv_ref.dtype)

        if comm is not None:
            pl.when(last_step)(lambda: comm.finish(*riding))

    blk = pl.BlockSpec((t, LANES), lambda p, i: (i, p))
    col_blk = pl.BlockSpec((s, LANES), lambda p, i: (0, p))
    sds = jax.ShapeDtypeStruct((s, SB_WIDTH), BF16)
    return pl.pallas_call(
        body, name="sb_bwd", grid=grid,
        in_specs=_sb_qkv_specs(s, 1) + [blk, blk] + c_in_specs,
        out_specs=[blk, col_blk, col_blk] + c_out_specs,
        out_shape=[sds, sds, sds] + c_out_shape,
        scratch_shapes=c_scratch + [pltpu.VMEM((2, t, 1), F32), pltpu.VMEM((2, t, 1), F32),
                                    pltpu.VMEM((t, LANES), F32), pltpu.VMEM((s, LANES), F32),
                                    pltpu.VMEM((s, LANES), F32)],
        compiler_params=_cparams(),
    )(qkv, qkv, qkv, o, do, *c_ins)


def _ret_log_gamma():
    lg = np.log1p(-np.exp2(-5.0 - np.arange(RET_HEADS, dtype=np.float32))).astype(np.float32)
    return jnp.asarray(np.broadcast_to(lg[:, None, None], (RET_HEADS, 8, LANES)).copy())


RET_SCRATCH = [pltpu.VMEM((RET_HEADS, RET_QK, RET_V), F32),
               pltpu.VMEM((RET_HEADS, RET_BLOCK, RET_BLOCK), F32),
               pltpu.VMEM((RET_HEADS, RET_BLOCK, 1), F32),
               pltpu.VMEM((RET_HEADS, RET_BLOCK, 1), F32)]


def _ret_begin(n, lg_ref, state, within, q_dec, k_dec):
    @pl.when(n == 0)
    def _():
        c = RET_BLOCK
        state[...] = jnp.zeros_like(state)
        row = lax.broadcasted_iota(jnp.int32, (c, c), 0)
        col = lax.broadcasted_iota(jnp.int32, (c, c), 1)
        rel = jnp.maximum(row - col, 0).astype(F32)
        idx = lax.broadcasted_iota(jnp.int32, (c, 1), 0).astype(F32)
        for h in range(RET_HEADS):
            lg = lg_ref[h, 0:1, 0:1]
            within[h] = jnp.where(row >= col, jnp.exp(lg * rel), 0.0)
            q_dec[h] = jnp.exp(lg * (idx + 1.0))
            k_dec[h] = jnp.exp(lg * (c - 1.0 - idx))


def _chunk_decay(lg_ref, h):
    return jnp.exp(lg_ref[h, 0:1, 0:1] * float(RET_BLOCK))


def _ret_heads(x, width):
    return [x[:, h * width:(h + 1) * width] for h in range(RET_HEADS)]


def _ret_specs(s, reverse=False):
    c = RET_BLOCK
    nc = s // c
    pos = (lambda n: nc - 1 - n) if reverse else (lambda n: n)
    q_spec = pl.BlockSpec((c, RET_QK_WIDTH), lambda n: (pos(n), 0))
    k_spec = pl.BlockSpec((c, RET_QK_WIDTH), lambda n: (pos(n), 1))
    v_spec = pl.BlockSpec((c, RET_V_WIDTH), lambda n: (pos(n), 0))
    lg_spec = pl.BlockSpec((RET_HEADS, 8, LANES), lambda n: (0, 0, 0))
    rope_spec = pl.BlockSpec((c, RET_QK), lambda n: (pos(n), 0))
    return nc, q_spec, k_spec, v_spec, lg_spec, rope_spec


def _ret_fwd(rqk, rvg, s):
    nc, q_spec, k_spec, v_spec, lg_spec, _ = _ret_specs(s)
    g_spec = pl.BlockSpec((RET_BLOCK, RET_V_WIDTH), lambda n: (n, 1))
    heads = range(RET_HEADS)

    def body(q_ref, k_ref, v_ref, g_ref, lg_ref, r_ref, y_ref, state, within, q_dec, k_dec):
        n = pl.program_id(0)
        _ret_begin(n, lg_ref, state, within, q_dec, k_dec)
        q, k = _ret_heads(q_ref[...], RET_QK), _ret_heads(k_ref[...], RET_QK)
        v, g = _ret_heads(v_ref[...], RET_V), _ret_heads(g_ref[...], RET_V)
        scores = [_dot(q[h].astype(BF16), k[h].astype(BF16), 1, 1) * within[h] for h in heads]
        cross = [_dot((q[h] * q_dec[h]).astype(BF16), state[h].astype(BF16), 1, 0) for h in heads]
        out = [_dot(scores[h].astype(BF16), v[h], 1, 0) + cross[h] for h in heads]
        grown = [_dot((k[h] * k_dec[h]).astype(BF16), v[h], 0, 0) for h in heads]
        for h in heads:
            sl = slice(h * RET_V, (h + 1) * RET_V)
            r_ref[:, sl] = out[h]
            xhat, _ = _norm(out[h])
            gh = g[h].astype(F32)
            y_ref[:, sl] = (gh * _sigmoid(gh) * xhat).astype(y_ref.dtype)
            state[h] = state[h] * _chunk_decay(lg_ref, h) + grown[h]

    return pl.pallas_call(
        body, name="ret_fwd", grid=(nc,),
        in_specs=[q_spec, k_spec, v_spec, g_spec, lg_spec],
        out_specs=[v_spec, v_spec],
        out_shape=[jax.ShapeDtypeStruct((s, RET_V_WIDTH), F32), jax.ShapeDtypeStruct((s, RET_V_WIDTH), BF16)],
        scratch_shapes=RET_SCRATCH,
        compiler_params=_cparams(),
    )(rqk, rqk, rvg, rvg, _ret_log_gamma())


def _rope_bwd(d, cos, sin):
    return d * cos + _swap_halves(d * sin)


def _ret_bwd_q(rqk, rv, d_out, cos2, sin2, s):
    nc, q_spec, k_spec, v_spec, lg_spec, rope_spec = _ret_specs(s)
    heads = range(RET_HEADS)

    def body(k_ref, v_ref, d_ref, lg_ref, cos_ref, sin_ref, dq_ref, state, within, q_dec, k_dec):
        n = pl.program_id(0)
        _ret_begin(n, lg_ref, state, within, q_dec, k_dec)
        k = _ret_heads(k_ref[...], RET_QK)
        v, d = _ret_heads(v_ref[...], RET_V), _ret_heads(d_ref[...], RET_V)
        cos, sin = cos_ref[...], sin_ref[...]
        d_scores = [_dot(d[h], v[h], 1, 1) * within[h] for h in heads]
        cross = [q_dec[h] * _dot(d[h], state[h].astype(BF16), 1, 1) for h in heads]
        dq = [_dot(d_scores[h].astype(BF16), k[h].astype(BF16), 1, 0) + cross[h] for h in heads]
        grown = [_dot((k[h] * k_dec[h]).astype(BF16), v[h], 0, 0) for h in heads]
        for h in heads:
            sl = slice(h * RET_QK, (h + 1) * RET_QK)
            dq_ref[:, sl] = (_rope_bwd(dq[h], cos, sin) * RET_SCALE).astype(dq_ref.dtype)
            state[h] = state[h] * _chunk_decay(lg_ref, h) + grown[h]

    return pl.pallas_call(
        body, name="ret_bwd_q", grid=(nc,),
        in_specs=[k_spec, v_spec, v_spec, lg_spec, rope_spec, rope_spec],
        out_specs=q_spec,
        out_shape=jax.ShapeDtypeStruct((s, RET_QK_WIDTH), BF16),
        scratch_shapes=RET_SCRATCH,
        compiler_params=_cparams(),
    )(rqk, rv, d_out, _ret_log_gamma(), cos2, sin2)


def _ret_bwd_kv(rqk, rv, d_out, cos2, sin2, s):
    nc, q_spec, k_spec, v_spec, lg_spec, rope_spec = _ret_specs(s, reverse=True)
    heads = range(RET_HEADS)

    def body(q_ref, k_ref, v_ref, d_ref, lg_ref, cos_ref, sin_ref, dk_ref, dv_ref, state, within, q_dec, k_dec):
        n = pl.program_id(0)
        _ret_begin(n, lg_ref, state, within, q_dec, k_dec)
        q, k = _ret_heads(q_ref[...], RET_QK), _ret_heads(k_ref[...], RET_QK)
        v, d = _ret_heads(v_ref[...], RET_V), _ret_heads(d_ref[...], RET_V)
        cos, sin = cos_ref[...], sin_ref[...]
        qb, kb = [q[h].astype(BF16) for h in heads], [k[h].astype(BF16) for h in heads]
        st = [state[h].astype(BF16) for h in heads]
        scores = [_dot(qb[h], kb[h], 1, 1) * within[h] for h in heads]
        d_scores = [_dot(d[h], v[h], 1, 1) * within[h] for h in heads]
        dk = [_dot(d_scores[h].astype(BF16), qb[h], 0, 0) + k_dec[h] * _dot(v[h], st[h], 1, 1) for h in heads]
        dv = [_dot(scores[h].astype(BF16), d[h], 0, 0) + k_dec[h] * _dot(kb[h], st[h], 1, 0) for h in heads]
        grown = [_dot((q[h] * q_dec[h]).astype(BF16), d[h], 0, 0) for h in heads]
        for h in heads:
            dk_ref[:, h * RET_QK:(h + 1) * RET_QK] = _rope_bwd(dk[h], cos, sin).astype(dk_ref.dtype)
            dv_ref[:, h * RET_V:(h + 1) * RET_V] = dv[h].astype(dv_ref.dtype)
            state[h] = state[h] * _chunk_decay(lg_ref, h) + grown[h]

    return pl.pallas_call(
        body, name="ret_bwd_kv", grid=(nc,),
        in_specs=[q_spec, k_spec, v_spec, v_spec, lg_spec, rope_spec, rope_spec],
        out_specs=[q_spec, v_spec],
        out_shape=[jax.ShapeDtypeStruct((s, RET_QK_WIDTH), BF16), jax.ShapeDtypeStruct((s, RET_V_WIDTH), BF16)],
        scratch_shapes=RET_SCRATCH,
        compiler_params=_cparams(),
    )(rqk, rqk, rv, d_out, _ret_log_gamma(), cos2, sin2)


def _xattn_probs(q, k):
    sc = _dot(q, k, 1, 1)
    sc = sc - jnp.max(sc, axis=-1, keepdims=True)
    p = jnp.exp(sc)
    return p / jnp.sum(p, axis=-1, keepdims=True)


def _xattn_fwd(qm, kv, s):
    tq = XATTN_ROWS

    def body(q_ref, kv_ref, o_ref):
        for h in range(MEM_HEADS):
            sl = slice(h * MEM_DIM, (h + 1) * MEM_DIM)
            sv = slice(D_MODEL + h * MEM_DIM, D_MODEL + (h + 1) * MEM_DIM)
            p = _xattn_probs(q_ref[:, sl], kv_ref[:, sl])
            o_ref[:, sl] = _dot(p.astype(BF16), kv_ref[:, sv], 1, 0).astype(o_ref.dtype)

    return pl.pallas_call(
        body, name="xattn_fwd", grid=(s // tq,),
        in_specs=[pl.BlockSpec((tq, D_MODEL), lambda i: (i, 0)),
                  pl.BlockSpec((MEM_LEN, 2 * D_MODEL), lambda i: (0, 0))],
        out_specs=pl.BlockSpec((tq, D_MODEL), lambda i: (i, 0)),
        out_shape=jax.ShapeDtypeStruct((s, D_MODEL), BF16),
        compiler_params=_cparams(),
    )(qm, kv)


def _xattn_bwd(qm, kv, do, s):
    tq = XATTN_ROWS

    def body(q_ref, kv_ref, do_ref, dq_ref, dkv_ref):
        i = pl.program_id(0)

        @pl.when(i == 0)
        def _():
            dkv_ref[...] = jnp.zeros_like(dkv_ref)

        for h in range(MEM_HEADS):
            sl = slice(h * MEM_DIM, (h + 1) * MEM_DIM)
            sv = slice(D_MODEL + h * MEM_DIM, D_MODEL + (h + 1) * MEM_DIM)
            q, k, v, d = q_ref[:, sl], kv_ref[:, sl], kv_ref[:, sv], do_ref[:, sl]
            p = _xattn_probs(q, k)
            dp = _dot(d, v, 1, 1)
            ds = (p * (dp - jnp.sum(p * dp, axis=-1, keepdims=True))).astype(BF16)
            dq_ref[:, sl] = (_dot(ds, k, 1, 0) * MEM_SCALE).astype(dq_ref.dtype)
            dkv_ref[:, sl] += _dot(ds, q, 0, 0)
            dkv_ref[:, sv] += _dot(p.astype(BF16), d, 0, 0)

    row_blk = pl.BlockSpec((tq, D_MODEL), lambda i: (i, 0))
    kv_blk = pl.BlockSpec((MEM_LEN, 2 * D_MODEL), lambda i: (0, 0))
    return pl.pallas_call(
        body, name="xattn_bwd", grid=(s // tq,),
        in_specs=[row_blk, kv_blk, row_blk],
        out_specs=[row_blk, kv_blk],
        out_shape=[jax.ShapeDtypeStruct((s, D_MODEL), BF16), jax.ShapeDtypeStruct((MEM_LEN, 2 * D_MODEL), F32)],
        compiler_params=_cparams(),
    )(qm, kv, do)


def _place():
    x, y, c = lax.axis_index("x"), lax.axis_index("y"), lax.axis_index("c")
    others = [(1 - x, y), (x, 1 - y), (1 - x, 1 - y)]
    return x, y, c, others


def _slab(ref, axis, chip, size):
    start = pl.multiple_of(chip * size, LANES if axis == 1 else 16)
    if axis == 0:
        return ref.at[pl.ds(start, size), :]
    return ref.at[:, pl.ds(start, size)]


class _CommPlan:
    def __init__(self, ins, out_shape, scratch, start, finish):
        self.ins, self.out_shape, self.scratch, self.start, self.finish = ins, out_shape, scratch, start, finish

    @property
    def specs(self):
        any_spec = pl.BlockSpec(memory_space=pl.ANY)
        return [any_spec] * len(self.ins), [any_spec] * len(self.out_shape)

    def split(self, refs):
        n_in, n_out = len(self.ins), len(self.out_shape)
        return refs[:n_in], refs[n_in:n_in + n_out], refs[n_in + n_out:]


def _gather_plan(names, shards):
    spec = {name: (shape, axis) for name, shape, axis in BIG}
    nw = len(names)

    def shard_half(ref, c):
        rows = ref.shape[0] // 2
        return ref.at[pl.ds(pl.multiple_of(c * rows, 16), rows), :]

    def region(ref, w, chip, c):
        shape, axis = spec[names[w]]
        size = shape[axis] // N_CHIPS
        if axis == 0:
            rows = size // 2
            return ref.at[pl.ds(pl.multiple_of(chip * size + c * rows, 16), rows), :]
        rows = shape[0] // 2
        return ref.at[pl.ds(pl.multiple_of(c * rows, 16), rows), pl.ds(pl.multiple_of(chip * size, LANES), size)]

    def ops(shard, full, sems):
        ici_send, ici_recv, d2d_send, d2d_recv, local_sems = sems
        x, y, c, others = _place()
        mine, sibling = 2 * x + y, (x, y, 1 - c)
        local, over_ici, arrived, passed_on, from_sibling = [], [], [], [], []
        for w in range(nw):
            shape, axis = spec[names[w]]
            local.append(pltpu.make_async_copy(shard[w], _slab(full[w], axis, mine, shape[axis] // N_CHIPS),
                                               local_sems.at[w]))
            for t, (qx, qy) in enumerate(others):
                n, theirs = 3 * w + t, 2 * qx + qy
                over_ici.append(pltpu.make_async_remote_copy(
                    src_ref=shard_half(shard[w], c), dst_ref=region(full[w], w, mine, c),
                    send_sem=ici_send.at[n], recv_sem=ici_recv.at[n], device_id=(qx, qy, c), device_id_type=MESH))
                arrived.append(pltpu.make_async_remote_copy(
                    src_ref=shard_half(shard[w], c), dst_ref=region(full[w], w, theirs, c),
                    send_sem=ici_send.at[n], recv_sem=ici_recv.at[n], device_id=(qx, qy, c), device_id_type=MESH))
                passed_on.append(pltpu.make_async_remote_copy(
                    src_ref=region(full[w], w, theirs, c), dst_ref=region(full[w], w, theirs, c),
                    send_sem=d2d_send.at[n], recv_sem=d2d_recv.at[n], device_id=sibling, device_id_type=MESH))
                from_sibling.append(pltpu.make_async_remote_copy(
                    src_ref=region(full[w], w, theirs, c), dst_ref=region(full[w], w, theirs, 1 - c),
                    send_sem=d2d_send.at[n], recv_sem=d2d_recv.at[n], device_id=sibling, device_id_type=MESH))
        return local, over_ici, arrived, passed_on, from_sibling

    def start(shard, full, sems):
        local, over_ici, _, _, _ = ops(shard, full, sems)
        for cp in local + over_ici:
            cp.start()

    def finish(shard, full, sems):
        local, over_ici, arrived, passed_on, from_sibling = ops(shard, full, sems)
        for got, onward in zip(arrived, passed_on, strict=True):
            got.wait_recv()
            onward.start()
        for got in from_sibling:
            got.wait_recv()
        for cp in over_ici + passed_on:
            cp.wait_send()
        for cp in local:
            cp.wait()

    dma = pltpu.SemaphoreType.DMA
    return _CommPlan(
        ins=[shards[name] for name in names],
        out_shape=[jax.ShapeDtypeStruct(spec[name][0], BF16) for name in names],
        scratch=[dma((3 * nw,)), dma((3 * nw,)), dma((3 * nw,)), dma((3 * nw,)), dma((nw,))],
        start=start, finish=finish)


def _shard_shape(shape, axis):
    return tuple(d // N_CHIPS if a == axis else d for a, d in enumerate(shape))


def _exchange_plan(names, grads):
    spec = {name: (shape, axis) for name, shape, axis in BIG}
    nw = len(names)

    def ops(grad, stack, sems):
        send_sems, recv_sems, local_sems = sems
        x, y, c, others = _place()
        mine = 2 * x + y
        me, sibling = (x, y, c), (x, y, 1 - c)

        def dev(px, py, pc):
            return 4 * px + 2 * py + pc

        def copy(w, n, src, slot, to):
            return pltpu.make_async_remote_copy(
                src_ref=src, dst_ref=stack[w].at[slot], send_sem=send_sems.at[7 * w + n],
                recv_sem=recv_sems.at[7 * w + n], device_id=to, device_id_type=MESH)

        local, first, arrived, passed_on, from_sibling = [], [], [], [], []
        for w in range(nw):
            shape, axis = spec[names[w]]
            size = shape[axis] // N_CHIPS
            own = _slab(grad[w], axis, mine, size)
            local.append(pltpu.make_async_copy(own, stack[w].at[dev(*me)], local_sems.at[w]))
            first.append(copy(w, 0, own, dev(*me), sibling))
            from_sibling.append(copy(w, 0, own, dev(*sibling), me))
            for t, (qx, qy) in enumerate(others):
                got = stack[w].at[dev(qx, qy, c)]
                first.append(copy(w, 1 + t, _slab(grad[w], axis, 2 * qx + qy, size), dev(*me), (qx, qy, c)))
                arrived.append(copy(w, 1 + t, got, dev(qx, qy, c), me))
                passed_on.append(copy(w, 4 + t, got, dev(qx, qy, c), sibling))
                from_sibling.append(copy(w, 4 + t, got, dev(qx, qy, 1 - c), me))
        return local, first, arrived, passed_on, from_sibling

    def start(grad, stack, sems):
        local, first, _, _, _ = ops(grad, stack, sems)
        for cp in local + first:
            cp.start()

    def finish(grad, stack, sems):
        local, first, arrived, passed_on, from_sibling = ops(grad, stack, sems)
        for got, onward in zip(arrived, passed_on, strict=True):
            got.wait_recv()
            onward.start()
        for got in from_sibling:
            got.wait_recv()
        for cp in first + passed_on:
            cp.wait_send()
        for cp in local:
            cp.wait()

    dma = pltpu.SemaphoreType.DMA
    return _CommPlan(
        ins=[grads[name] for name in names],
        out_shape=[jax.ShapeDtypeStruct((N_DEV,) + _shard_shape(*spec[name]), BF16) for name in names],
        scratch=[dma((7 * nw,)), dma((7 * nw,)), dma((nw,))],
        start=start, finish=finish)


def _adamw(w, g, m, v):
    m = ADAM_B1 * m + (1.0 - ADAM_B1) * g
    v = ADAM_B2 * v + (1.0 - ADAM_B2) * (g * g)
    m_hat = m / (1.0 - ADAM_B1 ** ADAM_STEP)
    v_hat = v / (1.0 - ADAM_B2 ** ADAM_STEP)
    delta = -ADAM_LR * (m_hat / (jnp.sqrt(v_hat) + ADAM_EPS) + ADAM_WD * w)
    return delta, m, v


def _reduce_adamw(name, stack, w, m, v):
    rows, cols = w.shape
    tr = next(t for t in (256, 128, 64) if rows % t == 0)

    def body(s_ref, w_ref, m_ref, v_ref, g_ref, d_ref, nm_ref, nv_ref):
        g = s_ref[0].astype(F32)
        for d in range(1, N_DEV):
            g = g + s_ref[d].astype(F32)
        g_ref[...] = g
        d_ref[...], nm_ref[...], nv_ref[...] = _adamw(w_ref[...], g, m_ref[...], v_ref[...])

    blk = pl.BlockSpec((tr, cols), lambda i: (i, 0))
    return pl.pallas_call(
        body, name=name, grid=(rows // tr,),
        in_specs=[pl.BlockSpec((N_DEV, tr, cols), lambda i: (0, i, 0)), blk, blk, blk],
        out_specs=[blk] * 4, out_shape=[jax.ShapeDtypeStruct((rows, cols), F32)] * 4,
        compiler_params=_cparams(),
    )(stack, w, m, v)


def _small_step(pack, w, m, v):
    def body(p_ref, w_ref, m_ref, v_ref, g_ref, d_ref, nm_ref, nv_ref, loss_ref, all_ref, send_sems, recv_sems):
        x, y, c, _ = _place()
        me = 4 * x + 2 * y + c
        all_ref[me] = p_ref[...]
        sent = []
        for n in range(1, N_DEV):
            peer = me ^ n
            cp = pltpu.make_async_remote_copy(
                src_ref=p_ref, dst_ref=all_ref.at[me], send_sem=send_sems.at[n - 1], recv_sem=recv_sems.at[n - 1],
                device_id=(peer // 4, (peer // 2) % 2, peer % 2), device_id_type=MESH)
            cp.start()
            sent.append(cp)
        for n in range(1, N_DEV):
            peer = me ^ n
            pltpu.make_async_remote_copy(
                src_ref=p_ref, dst_ref=all_ref.at[peer], send_sem=send_sems.at[n - 1], recv_sem=recv_sems.at[n - 1],
                device_id=(peer // 4, (peer // 2) % 2, peer % 2), device_id_type=MESH).wait_recv()
        for cp in sent:
            cp.wait_send()
        tot = all_ref[0]
        for d in range(1, N_DEV):
            tot = tot + all_ref[d]
        g = tot[:SMALL_ROWS]
        g_ref[...] = g
        d_ref[...], nm_ref[...], nv_ref[...] = _adamw(w_ref[...], g, m_ref[...], v_ref[...])
        loss_ref[...] = jnp.sum(jnp.sum(tot[SMALL_ROWS:], axis=1, keepdims=True), axis=0, keepdims=True)

    vm = pl.BlockSpec(memory_space=pltpu.VMEM)
    small = jax.ShapeDtypeStruct((SMALL_ROWS, LANES), F32)
    return pl.pallas_call(
        body, name="small_step",
        in_specs=[vm] * 4, out_specs=[vm] * 5,
        out_shape=[small] * 4 + [jax.ShapeDtypeStruct((1, 1), F32)],
        scratch_shapes=[pltpu.VMEM((N_DEV, PACK_ROWS, LANES), F32),
                        pltpu.SemaphoreType.DMA((N_DEV - 1,)), pltpu.SemaphoreType.DMA((N_DEV - 1,))],
    )(pack, w, m, v)


LATER_WEIGHTS = tuple(name for name, _, _ in BIG if name != "w_in")


def _layer_step(x, mem, tgt, shards, vec):
    s = x.shape[0]
    d = D_MODEL
    tm = min(ROW_TILE, s)
    tl = min(WIDE_TILE, s)
    xb, cos2, sin2, w_in = _prep(x, _gather_plan(("w_in",), shards))
    bf = lambda w: ((s, w), BF16)
    f32 = lambda w: ((s, w), F32)

    w_sb, w_rqk = w_in[:, :OFF_RET_Q], w_in[:, OFF_RET_Q:OFF_RET_V]
    w_rvg, w_gate = w_in[:, OFF_RET_V:OFF_GATE], w_in[:, OFF_GATE:]
    q_scale = lambda width, q_width, scale: jnp.concatenate(
        [jnp.full((1, q_width), scale, F32), jnp.ones((1, width - q_width), F32)], axis=1)
    n_groups = 3 * SB_WIDTH // LANES

    def sb_epi(acc, t, i, j):
        scaled = acc * t[0]
        return [jnp.stack([scaled[:, g * LANES:(g + 1) * LANES] for g in range(n_groups)])], []

    (sb_qkv,) = _mm(
        "in_sb", xb, w_sb, s, 3 * SB_WIDTH, d, tm=tl, tn=3 * SB_WIDTH, tk=d, epi=sb_epi,
        ins=[(q_scale(3 * SB_WIDTH, SB_WIDTH, SB_SCALE), *_rowvec(3 * SB_WIDTH))],
        outs=[((n_groups, s, LANES), BF16, (n_groups, tl, LANES), lambda i, j: (0, i, 0))])

    def rope_epi(acc, t, i, j):
        cos, sin, scale = t
        parts = []
        for g in range(acc.shape[1] // RET_QK):
            xg = acc[:, g * RET_QK:(g + 1) * RET_QK]
            parts.append(xg * cos + _swap_halves(xg) * sin)
        return [jnp.concatenate(parts, axis=1) * scale], []

    rope_in = ((tl, RET_QK), lambda i, j: (i, 0))
    (rqk,) = _mm("in_rqk", xb, w_rqk, s, 2 * RET_QK_WIDTH, d, tm=tl, tn=2 * RET_QK_WIDTH, tk=d, epi=rope_epi,
                 chunk=MXU_COLS,
                 ins=[(cos2, *rope_in), (sin2, *rope_in),
                      (q_scale(2 * RET_QK_WIDTH, RET_QK_WIDTH, RET_SCALE), *_rowvec(2 * RET_QK_WIDTH))],
                 outs=[(*f32(2 * RET_QK_WIDTH), *_tile(tl, 2 * RET_QK_WIDTH))])
    (rvg,) = _mm("in_rvg", xb, w_rvg, s, 2 * RET_V_WIDTH, d, tm=tl, tn=2 * RET_V_WIDTH, tk=d, chunk=MXU_COLS,
                 epi=_plain, outs=[(*bf(2 * RET_V_WIDTH), *_tile(tl, 2 * RET_V_WIDTH))])
    (gates,) = _mm("in_gate", xb, w_gate, s, 2 * d, d, tm=tl, tn=2 * d, tk=d, chunk=MXU_COLS,
                   epi=lambda acc, t, i, j: ([_sigmoid(acc + t[0])], []),
                   ins=[(vec["b_gate"], *_rowvec(2 * d))], outs=[(*bf(2 * d), *_tile(tl, 2 * d))])

    sb_out, sb_out_f32, *gathered = _sb_fwd(sb_qkv, s, comm=_gather_plan(LATER_WEIGHTS, shards))
    wt = dict(zip(LATER_WEIGHTS, gathered, strict=True))
    ret, gated = _ret_fwd(rqk, rvg, s)
    (y_sb,) = _mm("sb_o", sb_out, wt["w_sb_o"], s, d, SB_WIDTH, tm=tl, tn=d, tk=SB_WIDTH, epi=_plain,
                  outs=[(*bf(d), *_tile(tl, d))])
    y_ret, mixin = _mm(
        "ret_o", gated, wt["w_ret_o"], s, d, RET_V_WIDTH, tm=tl, tn=d, tk=RET_V_WIDTH, chunk=MXU_COLS,
        epi=lambda acc, t, i, j: ([acc, t[0].astype(F32) * t[2].astype(F32) + t[1].astype(F32) * acc], []),
        ins=[(gates, *_tile(tl, d)), (gates, *_tile(tl, d, 1)), (y_sb, *_tile(tl, d))],
        outs=[(*bf(d), *_tile(tl, d)), (*bf(d), *_tile(tl, d))])

    def ln_epi(acc, t, i, j):
        *res, g, b = t
        prev = res[0] if len(res) == 1 else res[0] * res[1] + res[2]
        xhat, rstd = _norm(DN_ALPHA * prev + acc)
        return [xhat * g + b, xhat, rstd], []

    full = _tile(tm, d)
    col1 = ((tm, 1), lambda i, j: (i, 0))
    vec_in = lambda name: (vec[name], *_rowvec(d))
    ln_outs = [(*bf(d), *full), (*f32(d), *full), ((s, 1), F32, *col1)]
    x1b, xhat1, rstd1 = _mm(
        "mix_o", mixin, wt["w_mix_o"], s, d, d, tm=tm, tn=d, tk=d, epi=ln_epi, skew=True,
        ins=[(x, *full), vec_in("ln1_g"), vec_in("ln1_b")], outs=ln_outs)

    (qm,) = _mm("mem_q", x1b, wt["w_mem_q"], s, d, d, tm=tl, tn=d, tk=d,
                epi=lambda acc, t, i, j: ([acc * MEM_SCALE], []), outs=[(*bf(d), *_tile(tl, d))])
    (kv,) = _mm("mem_kv", mem, wt["w_mem_kv"], MEM_LEN, 2 * d, d, tm=MEM_LEN, tn=d, tk=d, epi=_plain,
                outs=[((MEM_LEN, 2 * d), BF16, *_tile(MEM_LEN, d))])
    att = _xattn_fwd(qm, kv, s)
    x2b, xhat2, rstd2 = _mm(
        "mem_o", att, wt["w_mem_o"], s, d, d, tm=tm, tn=d, tk=d, epi=ln_epi, skew=True,
        ins=[(xhat1, *full), vec_in("ln1_g"), vec_in("ln1_b"), vec_in("ln2_g"), vec_in("ln2_b")], outs=ln_outs)

    fh = FFN_HIDDEN
    tf = fh // 2
    (f1,) = _mm("ffn_in1", x2b, wt["w_ffn_in"], s, fh, d, tm=tl, tn=tf, tk=d, epi=_plain, j_outer=True,
                outs=[(*bf(fh), *_tile(tl, tf))])

    def swiglu_epi(acc, t, i, j):
        a = t[0].astype(F32)
        return [acc, a * _sigmoid(a) * acc], []

    f2, act = _mm(
        "ffn_in2", x2b, wt["w_ffn_in"], s, fh, d, tm=tm, tn=fh, tk=d, b_off=(0, 1), epi=swiglu_epi, chunk=MXU_COLS,
        ins=[(f1, *_tile(tm, fh))], outs=[(*bf(fh), *_tile(tm, fh)), (*bf(fh), *_tile(tm, fh))])

    def head_epi(acc, t, i, j):
        prev_hat, prev_g, prev_b, g, b, target = t
        xhat, rstd = _norm(DN_ALPHA * (prev_hat * prev_g + prev_b) + acc)
        err = xhat * g + b - target
        dy = err * (1.0 / d)
        du = _norm_bwd(dy * g, xhat, rstd)
        return [du], [_colsum(dy * xhat), _colsum(dy), _colsum(err * err) * (0.5 / d)]

    vec_acc = ((1, d), F32)
    du3b, dg3, db3, loss_cols = _mm(
        "ffn_out", act, wt["w_ffn_out"], s, d, fh, tm=tm, tn=d, tk=fh, epi=head_epi, skew=True,
        ins=[(xhat2, *full), vec_in("ln2_g"), vec_in("ln2_b"), vec_in("ln3_g"), vec_in("ln3_b"), (tgt, *full)],
        outs=[(*bf(d), *full)], accs=[vec_acc] * 3)

    grads = {}
    ts = min(SEQ_TILE, s)

    def wgrad(name, a, b, m, n, tm_, tn_, tk_=None):
        (g,) = _mm(name, a, b, m, n, a.shape[0], tm=tm_, tn=tn_, tk=tk_ or ts, ta=True, epi=_plain,
                   outs=[((m, n), BF16, *_tile(tm_, tn_))])
        return g

    def ffn_bwd_epi(acc, t, i, j):
        a, b = t[0].astype(F32), t[1].astype(F32)
        sg = _sigmoid(a)
        return [acc * b * (sg * (1.0 + a * (1.0 - sg))), acc * (a * sg)], []

    df1, df2 = _mm(
        "ffn_out_t", du3b, wt["w_ffn_out"], s, fh, d, tm=tm, tn=fh, tk=d, tb=True, epi=ffn_bwd_epi, chunk=MXU_COLS,
        ins=[(f1, *_tile(tm, fh)), (f2, *_tile(tm, fh))],
        outs=[(*bf(fh), *_tile(tm, fh)), (*bf(fh), *_tile(tm, fh))])
    grads["w_ffn_out"] = wgrad("g_ffn_out", act, du3b, fh, d, tf, d)
    grads["w_ffn_in"] = jnp.concatenate(
        [wgrad("g_ffn_in1", x2b, df1, d, fh, d, tf), wgrad("g_ffn_in2", x2b, df2, d, fh, d, tf)], axis=1)
    (dx2a,) = _mm("ffn_in1_t", df1, wt["w_ffn_in"], s, d, fh, tm=tm, tn=d, tk=fh, tb=True, epi=_plain,
                  outs=[(*f32(d), *full)])

    def ln_bwd(name, a, b, k, tk, b_off, more, scales, xhat, rstd, g):
        def epi(acc, t, i, j):
            *extra, xh, rs, gg = t
            dy = acc
            for e, sc in zip(extra, scales, strict=True):
                dy = dy + e.astype(F32) * sc
            return [_norm_bwd(dy * gg, xh, rs)], [_colsum(dy * xh), _colsum(dy)]

        return _mm(name, a, b, s, d, k, tm=tm, tn=d, tk=tk, tb=True, b_off=b_off, epi=epi, skew=True,
                   ins=[(e, *full) for e in more] + [(xhat, *full), (rstd, *col1), (g, *_rowvec(d))],
                   outs=[(*bf(d), *full)], accs=[vec_acc] * 2)

    du2b, dg2, db2 = ln_bwd("ffn_in2_t", df2, wt["w_ffn_in"], fh, fh, (0, 1), [dx2a, du3b], [1.0, DN_ALPHA],
                            xhat2, rstd2, vec["ln2_g"])

    (datt,) = _mm("mem_o_t", du2b, wt["w_mem_o"], s, d, d, tm=tl, tn=d, tk=d, tb=True, epi=_plain,
                  outs=[(*bf(d), *_tile(tl, d))])
    grads["w_mem_o"] = wgrad("g_mem_o", att, du2b, d, d, d, d)
    dqm, dkv = _xattn_bwd(qm, kv, datt, s)
    grads["w_mem_q"] = wgrad("g_mem_q", x1b, dqm, d, d, d, d)
    grads["w_mem_kv"] = wgrad("g_mem_kv", mem, dkv, d, 2 * d, d, d, MEM_LEN)
    du1b, dg1, db1 = ln_bwd("mem_q_t", dqm, wt["w_mem_q"], d, d, (0, 0), [du2b], [DN_ALPHA],
                            xhat1, rstd1, vec["ln1_g"])

    def merge_bwd_epi(acc, t, i, j):
        g0, g1, ysb, yret = (v.astype(F32) for v in t)
        dgate0 = acc * ysb * (g0 * (1.0 - g0))
        dgate1 = acc * yret * (g1 * (1.0 - g1))
        return [dgate0, dgate1, acc * g0, acc * g1], [_colsum(dgate0), _colsum(dgate1)]

    dgate0, dgate1, dy_sb, dy_ret, dbg0, dbg1 = _mm(
        "mix_o_t", du1b, wt["w_mix_o"], s, d, d, tm=tm, tn=d, tk=d, tb=True, epi=merge_bwd_epi, skew=True,
        ins=[(gates, *full), (gates, *_tile(tm, d, 1)), (y_sb, *full), (y_ret, *full)],
        outs=[(*bf(d), *full)] * 4, accs=[vec_acc] * 2)
    grads["w_mix_o"] = wgrad("g_mix_o", mixin, du1b, d, d, d, d)
    grads["w_sb_o"] = wgrad("g_sb_o", sb_out, dy_sb, SB_WIDTH, d, SB_WIDTH, d)
    grads["w_ret_o"] = wgrad("g_ret_o", gated, dy_ret, RET_V_WIDTH, d, RET_V_WIDTH, d)
    (dsb_out,) = _mm("sb_o_t", dy_sb, wt["w_sb_o"], s, SB_WIDTH, d, tm=tl, tn=SB_WIDTH, tk=d, tb=True, epi=_plain,
                     outs=[(*bf(SB_WIDTH), *_tile(tl, SB_WIDTH))])

    def gate_norm_bwd_epi(acc, t, i, j):
        r, g = t[0], t[1].astype(F32)
        drg, dret = [], []
        for h in range(acc.shape[1] // RET_V):
            sl = slice(h * RET_V, (h + 1) * RET_V)
            xhat, rstd = _norm(r[:, sl])
            gg, dd = g[:, sl], acc[:, sl]
            sg = _sigmoid(gg)
            drg.append(dd * xhat * (sg * (1.0 + gg * (1.0 - sg))))
            dret.append(_norm_bwd(dd * (gg * sg), xhat, rstd))
        return [jnp.concatenate(drg, axis=1), jnp.concatenate(dret, axis=1)], []

    drg, dret = _mm(
        "ret_o_t", dy_ret, wt["w_ret_o"], s, RET_V_WIDTH, d, tm=tm, tn=d, tk=d, tb=True, epi=gate_norm_bwd_epi,
        chunk=MXU_COLS,
        ins=[(ret, *full), (rvg, *_tile(tm, d, 1))],
        outs=[(*bf(RET_V_WIDTH), *full)] * 2)

    drq = _ret_bwd_q(rqk, rvg, dret, cos2, sin2, s)
    drk, drv = _ret_bwd_kv(rqk, rvg, dret, cos2, sin2, s)
    dsq, dsk, dsv, *stacked = _sb_bwd(sb_qkv, sb_out_f32, dsb_out, s, comm=_exchange_plan(LATER_WEIGHTS, grads))
    stacks = dict(zip(LATER_WEIGHTS, stacked, strict=True))

    dh = jnp.concatenate([dsq, dsk, dsv, drq, drk, drv, drg, dgate0, dgate1], axis=1)
    grads["w_in"] = wgrad("g_in", xb, dh, d, IN_WIDTH, d, IN_WIDTH // N_CHIPS)
    grad_x, stacks["w_in"] = _mm(
        "in_t", dh, w_in, s, d, IN_WIDTH, tm=tl, tn=d, tk=IN_WIDTH // N_CHIPS, tb=True,
        epi=lambda acc, t, i, j: ([acc + DN_ALPHA * t[0].astype(F32)], []),
        ins=[(du1b, *_tile(tl, d))], outs=[(*f32(d), *_tile(tl, d))], comm=_exchange_plan(("w_in",), grads))

    small = {"b_gate": jnp.concatenate([dbg0, dbg1], axis=1), "ln1_g": dg1, "ln1_b": db1, "ln2_g": dg2,
             "ln2_b": db2, "ln3_g": dg3, "ln3_b": db3}
    return grad_x, stacks, small, loss_cols


def kernel(x, mem, w_in, b_gate, w_sb_o, w_ret_o, w_mix_o, ln1_g, ln1_b, w_mem_q, w_mem_kv, w_mem_o, ln2_g, ln2_b, w_ffn_in, w_ffn_out, ln3_g, ln3_b, loss_target, m_w_in, m_b_gate, m_w_sb_o, m_w_ret_o, m_w_mix_o, m_ln1_g, m_ln1_b, m_w_mem_q, m_w_mem_kv, m_w_mem_o, m_ln2_g, m_ln2_b, m_w_ffn_in, m_w_ffn_out, m_ln3_g, m_ln3_b, v_w_in, v_b_gate, v_w_sb_o, v_w_ret_o, v_w_mix_o, v_ln1_g, v_ln1_b, v_w_mem_q, v_w_mem_kv, v_w_mem_o, v_ln2_g, v_ln2_b, v_w_ffn_in, v_w_ffn_out, v_ln3_g, v_ln3_b):
    given = dict(locals())
    s = x.shape[1]
    x2d = x.reshape(s, D_MODEL)
    tgt = loss_target.reshape(s, D_MODEL)
    mem2d = mem.reshape(MEM_LEN, D_MODEL)
    shard = {name: given[name].reshape(_shard_shape(shape, axis)) for name, shape, axis in BIG}
    vec = {name: given[name] for name in SMALL}

    shards_bf = {name: _cast_bf16("cast_" + name, shard[name]) for name, _, _ in BIG}

    grad_x, stacks, small, loss_cols = _layer_step(x2d, mem2d, tgt, shards_bf, vec)

    out = {}
    for name, shape, axis in BIG:
        stack = stacks[name]
        shp = given[name].shape
        res = _reduce_adamw("adamw_" + name, stack, shard[name], given["m_" + name].reshape(stack.shape[1:]),
                            given["v_" + name].reshape(stack.shape[1:]))
        out[name] = [r.reshape(shp) for r in res]

    pack = jnp.concatenate([small[name] for name in SMALL] + [loss_cols], axis=1).reshape(PACK_ROWS, LANES)
    cat = lambda pre: jnp.concatenate([given[pre + name] for name in SMALL], axis=1).reshape(SMALL_ROWS, LANES)
    *res, loss = _small_step(pack, cat(""), cat("m_"), cat("v_"))
    flat = [r.reshape(1, SMALL_LEN) for r in res]
    off = 0
    for name in SMALL:
        n = given[name].shape[1]
        out[name] = [r[:, off:off + n] for r in flat]
        off += n

    return (loss.reshape(()), grad_x.reshape(x.shape),
            *[out[name][0] for name in WEIGHT_ORDER], *[out[name][1] for name in WEIGHT_ORDER],
            *[out[name][2] for name in WEIGHT_ORDER], *[out[name][3] for name in WEIGHT_ORDER])
```

```python
import functools

import jax
import jax.numpy as jnp
import numpy as np
from jax import lax
from jax.experimental import pallas as pl
from jax.experimental.pallas import tpu as pltpu

F32, BF16 = jnp.float32, jnp.bfloat16
MESH = pl.DeviceIdType.MESH

D_MODEL = 1024
MEM_LEN = 256
SB_HEADS, SB_DIM, SB_WIDTH = 8, 64, 512
RET_HEADS, RET_QK, RET_V = 4, 128, 256
RET_QK_WIDTH, RET_V_WIDTH = 512, 1024
ROPE_BASE = 10000.0
MEM_HEADS, MEM_DIM = 4, 256
FFN_HIDDEN = 2816
IN_WIDTH = 6656
OFF_RET_Q, OFF_RET_V, OFF_RET_G, OFF_GATE = 1536, 2560, 3584, 4608
DN_ALPHA = 2.0 ** 0.25
LN_EPS = 1e-5
SB_SCALE = SB_DIM ** -0.5
SB_DEAD = -110.0
RET_SCALE = RET_QK ** -0.5
MEM_SCALE = MEM_DIM ** -0.5
ADAM_LR, ADAM_B1, ADAM_B2, ADAM_EPS, ADAM_WD, ADAM_STEP = 0.001, 0.9, 0.999, 1e-08, 0.01, 10

N_DEV, N_CHIPS = 8, 4

LANES = 128
MXU_COLS = 256
VMEM_LIMIT_BYTES = 52 * 2 ** 20
ROW_TILE = 512
WIDE_TILE = 1024
SEQ_TILE = 2048
SB_BLOCK = 256
RET_BLOCK = 256
XATTN_ROWS = 512

BIG = (
    ("w_in", (D_MODEL, IN_WIDTH), 1),
    ("w_sb_o", (SB_WIDTH, D_MODEL), 1),
    ("w_ret_o", (RET_V_WIDTH, D_MODEL), 0),
    ("w_mix_o", (D_MODEL, D_MODEL), 0),
    ("w_mem_q", (D_MODEL, D_MODEL), 0),
    ("w_mem_kv", (D_MODEL, 2 * D_MODEL), 1),
    ("w_mem_o", (D_MODEL, D_MODEL), 0),
    ("w_ffn_in", (D_MODEL, 2 * FFN_HIDDEN), 1),
    ("w_ffn_out", (FFN_HIDDEN, D_MODEL), 0),
)
SMALL = ("b_gate", "ln1_g", "ln1_b", "ln2_g", "ln2_b", "ln3_g", "ln3_b")
SMALL_LEN = 2 * D_MODEL + 6 * D_MODEL
SMALL_ROWS = SMALL_LEN // LANES
PACK_ROWS = SMALL_ROWS + D_MODEL // LANES
WEIGHT_ORDER = ("w_in", "b_gate", "w_sb_o", "w_ret_o", "w_mix_o", "ln1_g", "ln1_b", "w_mem_q", "w_mem_kv",
                "w_mem_o", "ln2_g", "ln2_b", "w_ffn_in", "w_ffn_out", "ln3_g", "ln3_b")


def _cparams():
    return pltpu.CompilerParams(vmem_limit_bytes=VMEM_LIMIT_BYTES)


def _dot(a, b, ca, cb):
    return lax.dot_general(a, b, (((ca,), (cb,)), ((), ())), preferred_element_type=F32)


def _sigmoid(x):
    return 1.0 / (1.0 + jnp.exp(-x))


def _mm(name, a, b, m, n, k, *, tm, tn, tk, epi, outs, ins=(), accs=(), ta=False, tb=False,
        a_off=(0, 0), b_off=(0, 0), j_outer=False, comm=None, chunk=None):
    assert m % tm == 0 and n % tn == 0 and k % tk == 0, (name, m, n, k, tm, tn, tk)
    assert chunk is None or (k == tk and tn % chunk == 0), name
    ni, nj, nk = m // tm, n // tn, k // tk
    assert not accs or nj == 1, name
    ij = (lambda g0, g1: (g1, g0)) if j_outer else (lambda g0, g1: (g0, g1))

    def spec(block, index):
        return pl.BlockSpec(block, lambda g0, g1, kk: index(*ij(g0, g1), kk))

    if ta:
        a_spec = spec((tk, tm), lambda i, j, kk: (kk + a_off[0], i + a_off[1]))
    else:
        a_spec = spec((tm, tk), lambda i, j, kk: (i + a_off[0], kk + a_off[1]))
    if tb:
        b_spec = spec((tn, tk), lambda i, j, kk: (j + b_off[0], kk + b_off[1]))
    else:
        b_spec = spec((tk, tn), lambda i, j, kk: (kk + b_off[0], j + b_off[1]))
    in_specs = [a_spec, b_spec]
    for _, bs, im in ins:
        in_specs.append(spec(bs, lambda i, j, kk, im=im: im(i, j)))
    out_specs, out_shape = [], []
    for shape, dtype, bs, im in outs:
        out_specs.append(spec(bs, lambda i, j, kk, im=im: im(i, j)))
        out_shape.append(jax.ShapeDtypeStruct(shape, dtype))
    for shape, dtype in accs:
        out_specs.append(spec(shape, lambda i, j, kk, nd=len(shape): (0,) * nd))
        out_shape.append(jax.ShapeDtypeStruct(shape, dtype))
    n_in, n_out, n_acc = len(ins), len(outs), len(accs)
    ca, cb = (0 if ta else 1), (1 if tb else 0)
    grid = (*ij(ni, nj), nk)
    comm_ins, comm_outs, comm_scratch = [], [], []
    if comm is not None:
        comm_in_specs, comm_out_specs = comm.specs
        comm_ins, comm_outs, comm_scratch = list(comm.ins), list(comm.out_shape), list(comm.scratch)
        in_specs += comm_in_specs
        out_specs += comm_out_specs
        out_shape += comm_outs
    n_ci, n_co = len(comm_ins), len(comm_outs)

    def body(*refs):
        a_ref, b_ref = refs[:2]
        in_refs = refs[2:2 + n_in]
        ci_refs = refs[2 + n_in:2 + n_in + n_ci]
        rest = refs[2 + n_in + n_ci:]
        out_refs, acc_refs = rest[:n_out], rest[n_out:n_out + n_acc]
        co_refs = rest[n_out + n_acc:n_out + n_acc + n_co]
        scratch = rest[n_out + n_acc + n_co:]
        sem_refs, scratch = scratch[:len(comm_scratch)], scratch[len(comm_scratch):]
        (i, j), kk = ij(pl.program_id(0), pl.program_id(1)), pl.program_id(2)
        if comm is not None:
            first_step, last_step = _grid_ends(grid)
            pl.when(first_step)(lambda: comm.start(ci_refs, co_refs, sem_refs))
        def finish(acc, cols=slice(None)):
            def of(r):
                return r[..., cols] if r.shape[-1] == tn else r[...]

            o_tiles, a_tiles = epi(acc, [of(r) for r in in_refs], i, j)
            for r, t in zip(out_refs, o_tiles, strict=True):
                r[..., cols] = t.astype(r.dtype)
            if n_acc:
                @pl.when(i == 0)
                def _():
                    for r, t in zip(acc_refs, a_tiles, strict=True):
                        r[..., cols] = t

                @pl.when(i > 0)
                def _():
                    for r, t in zip(acc_refs, a_tiles, strict=True):
                        r[..., cols] += t

        if chunk is not None:
            a_tile = a_ref[...].astype(BF16)
            for c0 in range(0, tn, chunk):
                cols = slice(c0, c0 + chunk)
                b_part = b_ref[cols, :] if tb else b_ref[:, cols]
                finish(_dot(a_tile, b_part.astype(BF16), ca, cb), cols)
            if comm is not None:
                pl.when(last_step)(lambda: comm.finish(ci_refs, co_refs, sem_refs))
            return

        part = _dot(a_ref[...].astype(BF16), b_ref[...].astype(BF16), ca, cb)
        if nk == 1:
            finish(part)
        else:
            acc_ref = scratch[0]

            @pl.when(kk == 0)
            def _():
                acc_ref[...] = part

            @pl.when(kk > 0)
            def _():
                acc_ref[...] += part

            @pl.when(kk == nk - 1)
            def _():
                finish(acc_ref[...])

        if comm is not None:
            pl.when(last_step)(lambda: comm.finish(ci_refs, co_refs, sem_refs))

    res = pl.pallas_call(
        body, name=name, grid=grid, in_specs=in_specs, out_specs=out_specs, out_shape=out_shape,
        scratch_shapes=comm_scratch + ([pltpu.VMEM((tm, tn), F32)] if nk > 1 else []),
        compiler_params=_cparams(),
    )(a, b, *[x for x, _, _ in ins], *comm_ins)
    return res


def _grid_ends(grid):
    ids = [pl.program_id(ax) for ax in range(len(grid))]
    first = functools.reduce(jnp.logical_and, [p == 0 for p in ids])
    last = functools.reduce(jnp.logical_and, [p == n - 1 for p, n in zip(ids, grid, strict=True)])
    return first, last


def _tile(tm, tn, dj=0):
    return (tm, tn), (lambda i, j: (i, j + dj))


def _rowvec(tn, dj=0):
    return (1, tn), (lambda i, j: (0, j + dj))


def _plain(acc, tiles, i, j):
    return [acc], []


def _ew(name, fn, ins, outs, rows, tr):
    assert rows % tr == 0, (name, rows, tr)
    in_specs = []
    for x in ins:
        if x.shape[0] == rows:
            in_specs.append(pl.BlockSpec((tr, x.shape[1]), lambda i: (i, 0)))
        else:
            in_specs.append(pl.BlockSpec(x.shape, lambda i: (0, 0)))
    n_in = len(ins)

    def body(*refs):
        res = fn(*[r[...] for r in refs[:n_in]])
        for r, t in zip(refs[n_in:], res, strict=True):
            r[...] = t.astype(r.dtype)

    return pl.pallas_call(
        body, name=name, grid=(rows // tr,), in_specs=in_specs,
        out_specs=[pl.BlockSpec((tr, w), lambda i: (i, 0)) for w, _ in outs],
        out_shape=[jax.ShapeDtypeStruct((rows, w), dt) for w, dt in outs],
        compiler_params=_cparams(),
    )(*ins)


def _cast_bf16(name, x):
    rows = x.shape[0]
    tr = next(t for t in (512, 256, 64) if rows % t == 0)
    return _ew(name, lambda v: (v,), [x], [(x.shape[1], BF16)], rows, tr)[0]


def _prep(x, comm):
    s = x.shape[0]
    half = RET_QK // 2
    inv = 1.0 / (ROPE_BASE ** (jnp.arange(half, dtype=F32) / half))
    inv2 = jnp.concatenate([inv, inv]).reshape(1, RET_QK)
    sign = jnp.concatenate([-jnp.ones((half,), F32), jnp.ones((half,), F32)]).reshape(1, RET_QK)
    tr = min(ROW_TILE, s)
    grid = (s // tr,)
    c_in_specs, c_out_specs, c_out_shape, c_scratch, c_ins, split = _host(comm, 3, 3)

    def body(*refs):
        (x_ref, inv_ref, sign_ref), (xb_ref, cos_ref, sin_ref), _, riding = split(refs)
        i = pl.program_id(0)
        first_step, last_step = _grid_ends(grid)
        pl.when(first_step)(lambda: comm.start(*riding))
        xb_ref[...] = x_ref[...].astype(BF16)
        pos = (lax.broadcasted_iota(jnp.int32, (tr, RET_QK), 0) + i * tr).astype(F32)
        ang = pos * inv_ref[...]
        cos_ref[...] = jnp.cos(ang)
        sin_ref[...] = jnp.sin(ang) * sign_ref[...]
        pl.when(last_step)(lambda: comm.finish(*riding))

    vec = pl.BlockSpec((1, RET_QK), lambda i: (0, 0))
    row = lambda w: pl.BlockSpec((tr, w), lambda i: (i, 0))
    return pl.pallas_call(
        body, name="prep", grid=grid,
        in_specs=[row(D_MODEL), vec, vec] + c_in_specs,
        out_specs=[row(D_MODEL), row(RET_QK), row(RET_QK)] + c_out_specs,
        out_shape=[jax.ShapeDtypeStruct((s, D_MODEL), BF16), jax.ShapeDtypeStruct((s, RET_QK), F32),
                   jax.ShapeDtypeStruct((s, RET_QK), F32)] + c_out_shape,
        scratch_shapes=c_scratch, compiler_params=_cparams(),
    )(x, inv2, sign, *c_ins)


def _swap_halves(x):
    return pltpu.roll(x, RET_QK // 2, 1)


def _norm(u):
    mu = jnp.mean(u, axis=-1, keepdims=True)
    d = u - mu
    var = jnp.mean(d * d, axis=-1, keepdims=True)
    rstd = lax.rsqrt(var + LN_EPS)
    return d * rstd, rstd


def _norm_bwd(dxh, xhat, rstd):
    m1 = jnp.mean(dxh, axis=-1, keepdims=True)
    m2 = jnp.mean(dxh * xhat, axis=-1, keepdims=True)
    return rstd * (dxh - m1 - xhat * m2)


def _colsum(t):
    return jnp.sum(t, axis=0, keepdims=True)


def _split_mm(t, tri):
    hi = t.astype(BF16)
    lo = (t - hi.astype(F32)).astype(BF16)
    return _dot(hi, tri, 1, 0) + _dot(lo, tri, 1, 0)


def _sb_masks():
    t = SB_BLOCK
    lane = lax.broadcasted_iota(jnp.int32, (1, LANES), 1)
    first = lane < SB_DIM
    m0 = jnp.where(first, 1.0, 0.0).astype(BF16)
    m1 = jnp.where(first, 0.0, 1.0).astype(BF16)
    row = lax.broadcasted_iota(jnp.int32, (t, t), 0)
    col = lax.broadcasted_iota(jnp.int32, (t, t), 1)
    return first, (m0, m1), row, col


def _sb_logits(qh, k, causal):
    z = _dot(qh, k, 1, 1)
    lp = jnp.log(1.0 + jnp.exp(-jnp.abs(z)))
    a = jnp.minimum(z, 0.0) - lp
    r = jnp.minimum(-z, 0.0) - lp
    if causal is not None:
        r = jnp.where(causal, r, 0.0)
    return a, r


def _sb_walk(i, blocks, l_ref, causal):
    pl.when(i == 0)(lambda: blocks([(i, causal)]))
    pl.when(i > 0)(lambda: blocks([(i, causal), (i - 1, None)]))

    def alive():
        top = jnp.max(functools.reduce(jnp.maximum, [l_ref[c] for c in range(l_ref.shape[0])]))
        return jnp.where(top > SB_DEAD, 1, 0)

    def cond(c):
        return jnp.logical_and(c[0] < i, c[1] > 0)

    def step(c):
        blocks([(i - 1 - c[0], None)])
        return c[0] + 1, alive()

    lax.while_loop(cond, step, (jnp.int32(1), alive()))


def _host(comm, n_in, n_out):
    if comm is None:
        return [], [], [], [], [], lambda refs: (refs[:n_in], refs[n_in:n_in + n_out], refs[n_in + n_out:], None)
    in_specs, out_specs = comm.specs
    n_ci, n_co, n_sem = len(comm.ins), len(comm.out_shape), len(comm.scratch)

    def split(refs):
        ins, ci = refs[:n_in], refs[n_in:n_in + n_ci]
        rest = refs[n_in + n_ci:]
        outs, co = rest[:n_out], rest[n_out:n_out + n_co]
        sems, scratch = rest[n_out + n_co:n_out + n_co + n_sem], rest[n_out + n_co + n_sem:]
        return ins, outs, scratch, (ci, co, sems)

    return in_specs, out_specs, list(comm.out_shape), list(comm.scratch), list(comm.ins), split


def _sb_qkv_specs(s, g):
    groups = SB_HEADS // 2 // g
    return [pl.BlockSpec((g, SB_BLOCK, LANES), lambda p, i: (p, i, 0)),
            pl.BlockSpec((g, s, LANES), lambda p, i: (groups + p, 0, 0)),
            pl.BlockSpec((g, s, LANES), lambda p, i: (2 * groups + p, 0, 0))]


def _sb_fwd(qkv, s, comm=None):
    t = SB_BLOCK
    g = 2
    nq = s // t
    grid = (SB_HEADS // 2 // g, nq)
    c_in_specs, c_out_specs, c_out_shape, c_scratch, c_ins, split = _host(comm, 3, 2)

    def body(*refs):
        (q_ref, k_ref, v_ref), (o_ref, of_ref), (l_ref, acc_ref), riding = split(refs)
        i = pl.program_id(1)
        if comm is not None:
            first_step, last_step = _grid_ends(grid)
            pl.when(first_step)(lambda: comm.start(*riding))
        first, hmask, row, col = _sb_masks()
        after = jnp.where(row > col, 1.0, 0.0).astype(BF16)
        causal = col < row
        heads = [(p, h) for p in range(g) for h in range(2)]
        qh = {(p, h): q_ref[p] * hmask[h] for p, h in heads}
        l_ref[...] = jnp.zeros_like(l_ref)
        acc_ref[...] = jnp.zeros_like(acc_ref)

        def blocks(todo):
            chains = [(b, p, h) for b in range(len(todo)) for p, h in heads]
            starts = [pl.multiple_of(kb * t, t) for kb, _ in todo]
            ks = {(b, p): k_ref[p, pl.ds(st, t), :] for b, st in enumerate(starts) for p in range(g)}
            vs = {(b, p): v_ref[p, pl.ds(st, t), :] for b, st in enumerate(starts) for p in range(g)}
            ar = {(b, p, h): _sb_logits(qh[p, h], ks[b, p], todo[b][1]) for b, p, h in chains}
            later = {c: _split_mm(ar[c][1], after) for c in chains}
            carry = {(p, h): l_ref[2 * p + h] for p, h in heads}
            w = {}
            for b, (_, mask) in enumerate(todo):
                for p, h in heads:
                    wc = jnp.exp(ar[b, p, h][0] + later[b, p, h] + carry[p, h])
                    w[b, p, h] = wc if mask is None else jnp.where(mask, wc, 0.0)
                carry = {(p, h): carry[p, h] + jnp.sum(ar[b, p, h][1], axis=1, keepdims=True) for p, h in heads}
            pv = {(b, p, h): _dot(w[b, p, h].astype(BF16), vs[b, p], 1, 0) for b, p, h in chains}
            for p in range(g):
                lanes = slice(p * LANES, (p + 1) * LANES)
                acc = acc_ref[:, lanes]
                for b in range(len(todo)):
                    acc = acc + jnp.where(first, pv[b, p, 0], pv[b, p, 1])
                acc_ref[:, lanes] = acc
            for p, h in heads:
                l_ref[2 * p + h] = carry[p, h]

        _sb_walk(i, blocks, l_ref, causal)
        o_ref[...] = acc_ref[...].astype(o_ref.dtype)
        of_ref[...] = acc_ref[...]
        if comm is not None:
            pl.when(last_step)(lambda: comm.finish(*riding))

    blk = pl.BlockSpec((t, g * LANES), lambda p, i: (i, p))
    return pl.pallas_call(
        body, name="sb_fwd", grid=grid,
        in_specs=_sb_qkv_specs(s, g) + c_in_specs,
        out_specs=[blk, blk] + c_out_specs,
        out_shape=[jax.ShapeDtypeStruct((s, SB_WIDTH), BF16), jax.ShapeDtypeStruct((s, SB_WIDTH), F32)] + c_out_shape,
        scratch_shapes=c_scratch + [pltpu.VMEM((2 * g, t, 1), F32), pltpu.VMEM((t, g * LANES), F32)],
        compiler_params=_cparams(),
    )(qkv, qkv, qkv, *c_ins)


def _sb_bwd(qkv, o, do, s, comm=None):
    t = SB_BLOCK
    nq = s // t
    grid = (SB_HEADS // 2, nq)
    c_in_specs, c_out_specs, c_out_shape, c_scratch, c_ins, split = _host(comm, 5, 3)

    def body(*refs):
        ((q_ref, k_ref, v_ref, o_ref, do_ref), (dq_ref, dk_ref, dv_ref),
         (l_ref, e_ref, dq_acc, dk_acc, dv_acc), riding) = split(refs)
        i = pl.program_id(1)
        if comm is not None:
            first_step, last_step = _grid_ends(grid)
            pl.when(first_step)(lambda: comm.start(*riding))
        first, hmask, row, col = _sb_masks()
        after = jnp.where(row > col, 1.0, 0.0).astype(BF16)
        from_here = jnp.where(row >= col, 1.0, 0.0).astype(BF16)
        causal = col < row

        @pl.when(i == 0)
        def _():
            dk_acc[...] = jnp.zeros_like(dk_acc)
            dv_acc[...] = jnp.zeros_like(dv_acc)

        q = q_ref[0]
        do_ = do_ref[...]
        qh = (q * hmask[0], q * hmask[1])
        doh = (do_ * hmask[0], do_ * hmask[1])
        prod = do_.astype(F32) * o_ref[...]
        total = (jnp.sum(jnp.where(first, prod, 0.0), axis=1, keepdims=True),
                 jnp.sum(jnp.where(first, 0.0, prod), axis=1, keepdims=True))
        l_ref[...] = jnp.zeros_like(l_ref)
        e_ref[...] = jnp.zeros_like(e_ref)
        dq_acc[...] = jnp.zeros_like(dq_acc)

        def blocks(todo):
            chains = [(b, h) for b in range(len(todo)) for h in range(2)]
            starts = [pl.multiple_of(kb * t, t) for kb, _ in todo]
            ks = [k_ref[0, pl.ds(st, t), :] for st in starts]
            vs = [v_ref[0, pl.ds(st, t), :] for st in starts]
            ar = {(b, h): _sb_logits(qh[h], ks[b], todo[b][1]) for b, h in chains}
            dw = {(b, h): _dot(doh[h], vs[b], 1, 1) for b, h in chains}
            later = {bh: _split_mm(ar[bh][1], after) for bh in chains}
            carry = [l_ref[0], l_ref[1]]
            wb = {}
            for b, (_, mask) in enumerate(todo):
                for h in range(2):
                    wbh = jnp.exp(ar[b, h][0] + later[b, h] + carry[h])
                    wb[b, h] = (wbh if mask is None else jnp.where(mask, wbh, 0.0)).astype(BF16)
                carry = [carry[h] + jnp.sum(ar[b, h][1], axis=1, keepdims=True) for h in range(2)]
            dvs = {(b, h): _dot(wb[b, h], do_, 0, 0) for b, h in chains}
            e = {bh: dw[bh] * wb[bh].astype(F32) for bh in chains}
            suffix = {bh: _split_mm(e[bh], from_here) for bh in chains}
            e_carry = [e_ref[0], e_ref[1]]
            dz = {}
            for b, (_, mask) in enumerate(todo):
                for h in range(2):
                    before = total[h] - (suffix[b, h] + e_carry[h])
                    dzh = e[b, h] - jnp.exp(ar[b, h][0]) * (e[b, h] + before)
                    dz[b, h] = (dzh if mask is None else jnp.where(mask, dzh, 0.0)).astype(BF16)
                e_carry = [e_carry[h] + jnp.sum(e[b, h], axis=1, keepdims=True) for h in range(2)]
            dqs = {(b, h): _dot(dz[b, h], ks[b], 1, 0) for b, h in chains}
            dks = {(b, h): _dot(dz[b, h], q, 0, 0) for b, h in chains}
            dq = dq_acc[...]
            for b, st in enumerate(starts):
                dq = dq + jnp.where(first, dqs[b, 0], dqs[b, 1])
                dk_acc[pl.ds(st, t), :] += jnp.where(first, dks[b, 0], dks[b, 1])
                dv_acc[pl.ds(st, t), :] += jnp.where(first, dvs[b, 0], dvs[b, 1])
            dq_acc[...] = dq
            l_ref[0], l_ref[1] = carry
            e_ref[0], e_ref[1] = e_carry

        _sb_walk(i, blocks, l_ref, causal)
        dq_ref[...] = (dq_acc[...] * SB_SCALE).astype(dq_ref.dtype)

        @pl.when(i == nq - 1)
        def _():
            dk_ref[...] = dk_acc[...].astype(dk_ref.dtype)
            dv_ref[...] = dv_acc[...].astype(dv_ref.dtype)

        if comm is not None:
            pl.when(last_step)(lambda: comm.finish(*riding))

    blk = pl.BlockSpec((t, LANES), lambda p, i: (i, p))
    col_blk = pl.BlockSpec((s, LANES), lambda p, i: (0, p))
    sds = jax.ShapeDtypeStruct((s, SB_WIDTH), BF16)
    return pl.pallas_call(
        body, name="sb_bwd", grid=grid,
        in_specs=_sb_qkv_specs(s, 1) + [blk, blk] + c_in_specs,
        out_specs=[blk, col_blk, col_blk] + c_out_specs,
        out_shape=[sds, sds, sds] + c_out_shape,
        scratch_shapes=c_scratch + [pltpu.VMEM((2, t, 1), F32), pltpu.VMEM((2, t, 1), F32),
                                    pltpu.VMEM((t, LANES), F32), pltpu.VMEM((s, LANES), F32),
                                    pltpu.VMEM((s, LANES), F32)],
        compiler_params=_cparams(),
    )(qkv, qkv, qkv, o, do, *c_ins)


def _ret_log_gamma():
    lg = np.log1p(-np.exp2(-5.0 - np.arange(RET_HEADS, dtype=np.float32))).astype(np.float32)
    return jnp.asarray(np.broadcast_to(lg[:, None, None], (RET_HEADS, 8, LANES)).copy())


RET_SCRATCH = [pltpu.VMEM((RET_HEADS, RET_QK, RET_V), F32),
               pltpu.VMEM((RET_HEADS, RET_BLOCK, RET_BLOCK), F32),
               pltpu.VMEM((RET_HEADS, RET_BLOCK, 1), F32),
               pltpu.VMEM((RET_HEADS, RET_BLOCK, 1), F32)]


def _ret_begin(n, lg_ref, state, within, q_dec, k_dec):
    @pl.when(n == 0)
    def _():
        c = RET_BLOCK
        state[...] = jnp.zeros_like(state)
        row = lax.broadcasted_iota(jnp.int32, (c, c), 0)
        col = lax.broadcasted_iota(jnp.int32, (c, c), 1)
        rel = jnp.maximum(row - col, 0).astype(F32)
        idx = lax.broadcasted_iota(jnp.int32, (c, 1), 0).astype(F32)
        for h in range(RET_HEADS):
            lg = lg_ref[h, 0:1, 0:1]
            within[h] = jnp.where(row >= col, jnp.exp(lg * rel), 0.0)
            q_dec[h] = jnp.exp(lg * (idx + 1.0))
            k_dec[h] = jnp.exp(lg * (c - 1.0 - idx))


def _chunk_decay(lg_ref, h):
    return jnp.exp(lg_ref[h, 0:1, 0:1] * float(RET_BLOCK))


def _ret_heads(x, width):
    return [x[:, h * width:(h + 1) * width] for h in range(RET_HEADS)]


def _ret_specs(s, reverse=False):
    c = RET_BLOCK
    nc = s // c
    pos = (lambda n: nc - 1 - n) if reverse else (lambda n: n)
    q_spec = pl.BlockSpec((c, RET_QK_WIDTH), lambda n: (pos(n), 0))
    k_spec = pl.BlockSpec((c, RET_QK_WIDTH), lambda n: (pos(n), 1))
    v_spec = pl.BlockSpec((c, RET_V_WIDTH), lambda n: (pos(n), 0))
    lg_spec = pl.BlockSpec((RET_HEADS, 8, LANES), lambda n: (0, 0, 0))
    rope_spec = pl.BlockSpec((c, RET_QK), lambda n: (pos(n), 0))
    return nc, q_spec, k_spec, v_spec, lg_spec, rope_spec


def _ret_fwd(rqk, rvg, s):
    nc, q_spec, k_spec, v_spec, lg_spec, _ = _ret_specs(s)
    g_spec = pl.BlockSpec((RET_BLOCK, RET_V_WIDTH), lambda n: (n, 1))
    heads = range(RET_HEADS)

    def body(q_ref, k_ref, v_ref, g_ref, lg_ref, r_ref, y_ref, state, within, q_dec, k_dec):
        n = pl.program_id(0)
        _ret_begin(n, lg_ref, state, within, q_dec, k_dec)
        q, k = _ret_heads(q_ref[...], RET_QK), _ret_heads(k_ref[...], RET_QK)
        v, g = _ret_heads(v_ref[...], RET_V), _ret_heads(g_ref[...], RET_V)
        scores = [_dot(q[h].astype(BF16), k[h].astype(BF16), 1, 1) * within[h] for h in heads]
        cross = [_dot((q[h] * q_dec[h]).astype(BF16), state[h].astype(BF16), 1, 0) for h in heads]
        out = [_dot(scores[h].astype(BF16), v[h], 1, 0) + cross[h] for h in heads]
        grown = [_dot((k[h] * k_dec[h]).astype(BF16), v[h], 0, 0) for h in heads]
        for h in heads:
            sl = slice(h * RET_V, (h + 1) * RET_V)
            r_ref[:, sl] = out[h]
            xhat, _ = _norm(out[h])
            gh = g[h].astype(F32)
            y_ref[:, sl] = (gh * _sigmoid(gh) * xhat).astype(y_ref.dtype)
            state[h] = state[h] * _chunk_decay(lg_ref, h) + grown[h]

    return pl.pallas_call(
        body, name="ret_fwd", grid=(nc,),
        in_specs=[q_spec, k_spec, v_spec, g_spec, lg_spec],
        out_specs=[v_spec, v_spec],
        out_shape=[jax.ShapeDtypeStruct((s, RET_V_WIDTH), F32), jax.ShapeDtypeStruct((s, RET_V_WIDTH), BF16)],
        scratch_shapes=RET_SCRATCH,
        compiler_params=_cparams(),
    )(rqk, rqk, rvg, rvg, _ret_log_gamma())


def _rope_bwd(d, cos, sin):
    return d * cos + _swap_halves(d * sin)


def _ret_bwd_q(rqk, rv, d_out, cos2, sin2, s):
    nc, q_spec, k_spec, v_spec, lg_spec, rope_spec = _ret_specs(s)
    heads = range(RET_HEADS)

    def body(k_ref, v_ref, d_ref, lg_ref, cos_ref, sin_ref, dq_ref, state, within, q_dec, k_dec):
        n = pl.program_id(0)
        _ret_begin(n, lg_ref, state, within, q_dec, k_dec)
        k = _ret_heads(k_ref[...], RET_QK)
        v, d = _ret_heads(v_ref[...], RET_V), _ret_heads(d_ref[...], RET_V)
        cos, sin = cos_ref[...], sin_ref[...]
        d_scores = [_dot(d[h], v[h], 1, 1) * within[h] for h in heads]
        cross = [q_dec[h] * _dot(d[h], state[h].astype(BF16), 1, 1) for h in heads]
        dq = [_dot(d_scores[h].astype(BF16), k[h].astype(BF16), 1, 0) + cross[h] for h in heads]
        grown = [_dot((k[h] * k_dec[h]).astype(BF16), v[h], 0, 0) for h in heads]
        for h in heads:
            sl = slice(h * RET_QK, (h + 1) * RET_QK)
            dq_ref[:, sl] = (_rope_bwd(dq[h], cos, sin) * RET_SCALE).astype(dq_ref.dtype)
            state[h] = state[h] * _chunk_decay(lg_ref, h) + grown[h]

    return pl.pallas_call(
        body, name="ret_bwd_q", grid=(nc,),
        in_specs=[k_spec, v_spec, v_spec, lg_spec, rope_spec, rope_spec],
        out_specs=q_spec,
        out_shape=jax.ShapeDtypeStruct((s, RET_QK_WIDTH), BF16),
        scratch_shapes=RET_SCRATCH,
        compiler_params=_cparams(),
    )(rqk, rv, d_out, _ret_log_gamma(), cos2, sin2)


def _ret_bwd_kv(rqk, rv, d_out, cos2, sin2, s):
    nc, q_spec, k_spec, v_spec, lg_spec, rope_spec = _ret_specs(s, reverse=True)
    heads = range(RET_HEADS)

    def body(q_ref, k_ref, v_ref, d_ref, lg_ref, cos_ref, sin_ref, dk_ref, dv_ref, state, within, q_dec, k_dec):
        n = pl.program_id(0)
        _ret_begin(n, lg_ref, state, within, q_dec, k_dec)
        q, k = _ret_heads(q_ref[...], RET_QK), _ret_heads(k_ref[...], RET_QK)
        v, d = _ret_heads(v_ref[...], RET_V), _ret_heads(d_ref[...], RET_V)
        cos, sin = cos_ref[...], sin_ref[...]
        qb, kb = [q[h].astype(BF16) for h in heads], [k[h].astype(BF16) for h in heads]
        st = [state[h].astype(BF16) for h in heads]
        scores = [_dot(qb[h], kb[h], 1, 1) * within[h] for h in heads]
        d_scores = [_dot(d[h], v[h], 1, 1) * within[h] for h in heads]
        dk = [_dot(d_scores[h].astype(BF16), qb[h], 0, 0) + k_dec[h] * _dot(v[h], st[h], 1, 1) for h in heads]
        dv = [_dot(scores[h].astype(BF16), d[h], 0, 0) + k_dec[h] * _dot(kb[h], st[h], 1, 0) for h in heads]
        grown = [_dot((q[h] * q_dec[h]).astype(BF16), d[h], 0, 0) for h in heads]
        for h in heads:
            dk_ref[:, h * RET_QK:(h + 1) * RET_QK] = _rope_bwd(dk[h], cos, sin).astype(dk_ref.dtype)
            dv_ref[:, h * RET_V:(h + 1) * RET_V] = dv[h].astype(dv_ref.dtype)
            state[h] = state[h] * _chunk_decay(lg_ref, h) + grown[h]

    return pl.pallas_call(
        body, name="ret_bwd_kv", grid=(nc,),
        in_specs=[q_spec, k_spec, v_spec, v_spec, lg_spec, rope_spec, rope_spec],
        out_specs=[q_spec, v_spec],
        out_shape=[jax.ShapeDtypeStruct((s, RET_QK_WIDTH), BF16), jax.ShapeDtypeStruct((s, RET_V_WIDTH), BF16)],
        scratch_shapes=RET_SCRATCH,
        compiler_params=_cparams(),
    )(rqk, rqk, rv, d_out, _ret_log_gamma(), cos2, sin2)


def _xattn_probs(scores):
    sc = scores - jnp.max(scores, axis=-1, keepdims=True)
    p = jnp.exp(sc)
    return p / jnp.sum(p, axis=-1, keepdims=True)


def _xattn_heads(q_ref, kv_ref):
    sls = [slice(h * MEM_DIM, (h + 1) * MEM_DIM) for h in range(MEM_HEADS)]
    q = [q_ref[:, sl] for sl in sls]
    k = [kv_ref[:, sl] for sl in sls]
    v = [kv_ref[:, D_MODEL + h * MEM_DIM:D_MODEL + (h + 1) * MEM_DIM] for h in range(MEM_HEADS)]
    return sls, q, k, v


def _xattn_fwd(qm, kv, s):
    tq = XATTN_ROWS
    heads = range(MEM_HEADS)

    def body(q_ref, kv_ref, o_ref):
        sls, q, k, v = _xattn_heads(q_ref, kv_ref)
        scores = [_dot(q[h], k[h], 1, 1) for h in heads]
        p = [_xattn_probs(scores[h]).astype(BF16) for h in heads]
        out = [_dot(p[h], v[h], 1, 0) for h in heads]
        for h in heads:
            o_ref[:, sls[h]] = out[h].astype(o_ref.dtype)

    return pl.pallas_call(
        body, name="xattn_fwd", grid=(s // tq,),
        in_specs=[pl.BlockSpec((tq, D_MODEL), lambda i: (i, 0)),
                  pl.BlockSpec((MEM_LEN, 2 * D_MODEL), lambda i: (0, 0))],
        out_specs=pl.BlockSpec((tq, D_MODEL), lambda i: (i, 0)),
        out_shape=jax.ShapeDtypeStruct((s, D_MODEL), BF16),
        compiler_params=_cparams(),
    )(qm, kv)


def _xattn_bwd(qm, kv, do, s):
    tq = XATTN_ROWS

    def body(q_ref, kv_ref, do_ref, dq_ref, dkv_ref):
        i = pl.program_id(0)

        @pl.when(i == 0)
        def _():
            dkv_ref[...] = jnp.zeros_like(dkv_ref)

        heads = range(MEM_HEADS)
        sls, q, k, v = _xattn_heads(q_ref, kv_ref)
        d = [do_ref[:, sl] for sl in sls]
        scores = [_dot(q[h], k[h], 1, 1) for h in heads]
        dp = [_dot(d[h], v[h], 1, 1) for h in heads]
        p = [_xattn_probs(scores[h]) for h in heads]
        ds = [(p[h] * (dp[h] - jnp.sum(p[h] * dp[h], axis=-1, keepdims=True))).astype(BF16) for h in heads]
        dq = [_dot(ds[h], k[h], 1, 0) for h in heads]
        dk = [_dot(ds[h], q[h], 0, 0) for h in heads]
        dv = [_dot(p[h].astype(BF16), d[h], 0, 0) for h in heads]
        for h in heads:
            dq_ref[:, sls[h]] = (dq[h] * MEM_SCALE).astype(dq_ref.dtype)
            dkv_ref[:, sls[h]] += dk[h]
            dkv_ref[:, D_MODEL + h * MEM_DIM:D_MODEL + (h + 1) * MEM_DIM] += dv[h]

    row_blk = pl.BlockSpec((tq, D_MODEL), lambda i: (i, 0))
    kv_blk = pl.BlockSpec((MEM_LEN, 2 * D_MODEL), lambda i: (0, 0))
    return pl.pallas_call(
        body, name="xattn_bwd", grid=(s // tq,),
        in_specs=[row_blk, kv_blk, row_blk],
        out_specs=[row_blk, kv_blk],
        out_shape=[jax.ShapeDtypeStruct((s, D_MODEL), BF16), jax.ShapeDtypeStruct((MEM_LEN, 2 * D_MODEL), F32)],
        compiler_params=_cparams(),
    )(qm, kv, do)


def _place():
    x, y, c = lax.axis_index("x"), lax.axis_index("y"), lax.axis_index("c")
    others = [(1 - x, y), (x, 1 - y), (1 - x, 1 - y)]
    return x, y, c, others


def _slab(ref, axis, chip, size):
    start = pl.multiple_of(chip * size, LANES if axis == 1 else 16)
    if axis == 0:
        return ref.at[pl.ds(start, size), :]
    return ref.at[:, pl.ds(start, size)]


class _CommPlan:
    def __init__(self, ins, out_shape, scratch, start, finish):
        self.ins, self.out_shape, self.scratch, self.start, self.finish = ins, out_shape, scratch, start, finish

    @property
    def specs(self):
        any_spec = pl.BlockSpec(memory_space=pl.ANY)
        return [any_spec] * len(self.ins), [any_spec] * len(self.out_shape)

    def split(self, refs):
        n_in, n_out = len(self.ins), len(self.out_shape)
        return refs[:n_in], refs[n_in:n_in + n_out], refs[n_in + n_out:]


def _gather_plan(names, shards):
    spec = {name: (shape, axis) for name, shape, axis in BIG}
    nw = len(names)

    def shard_half(ref, c):
        rows = ref.shape[0] // 2
        return ref.at[pl.ds(pl.multiple_of(c * rows, 16), rows), :]

    def region(ref, w, chip, c):
        shape, axis = spec[names[w]]
        size = shape[axis] // N_CHIPS
        if axis == 0:
            rows = size // 2
            return ref.at[pl.ds(pl.multiple_of(chip * size + c * rows, 16), rows), :]
        rows = shape[0] // 2
        return ref.at[pl.ds(pl.multiple_of(c * rows, 16), rows), pl.ds(pl.multiple_of(chip * size, LANES), size)]

    def ops(shard, full, sems):
        ici_send, ici_recv, d2d_send, d2d_recv, local_sems = sems
        x, y, c, others = _place()
        mine, sibling = 2 * x + y, (x, y, 1 - c)
        local, over_ici, arrived, passed_on, from_sibling = [], [], [], [], []
        for w in range(nw):
            shape, axis = spec[names[w]]
            local.append(pltpu.make_async_copy(shard[w], _slab(full[w], axis, mine, shape[axis] // N_CHIPS),
                                               local_sems.at[w]))
            for t, (qx, qy) in enumerate(others):
                n, theirs = 3 * w + t, 2 * qx + qy
                over_ici.append(pltpu.make_async_remote_copy(
                    src_ref=shard_half(shard[w], c), dst_ref=region(full[w], w, mine, c),
                    send_sem=ici_send.at[n], recv_sem=ici_recv.at[n], device_id=(qx, qy, c), device_id_type=MESH))
                arrived.append(pltpu.make_async_remote_copy(
                    src_ref=shard_half(shard[w], c), dst_ref=region(full[w], w, theirs, c),
                    send_sem=ici_send.at[n], recv_sem=ici_recv.at[n], device_id=(qx, qy, c), device_id_type=MESH))
                passed_on.append(pltpu.make_async_remote_copy(
                    src_ref=region(full[w], w, theirs, c), dst_ref=region(full[w], w, theirs, c),
                    send_sem=d2d_send.at[n], recv_sem=d2d_recv.at[n], device_id=sibling, device_id_type=MESH))
                from_sibling.append(pltpu.make_async_remote_copy(
                    src_ref=region(full[w], w, theirs, c), dst_ref=region(full[w], w, theirs, 1 - c),
                    send_sem=d2d_send.at[n], recv_sem=d2d_recv.at[n], device_id=sibling, device_id_type=MESH))
        return local, over_ici, arrived, passed_on, from_sibling

    def start(shard, full, sems):
        local, over_ici, _, _, _ = ops(shard, full, sems)
        for cp in local + over_ici:
            cp.start()

    def finish(shard, full, sems):
        local, over_ici, arrived, passed_on, from_sibling = ops(shard, full, sems)
        for got, onward in zip(arrived, passed_on, strict=True):
            got.wait_recv()
            onward.start()
        for got in from_sibling:
            got.wait_recv()
        for cp in over_ici + passed_on:
            cp.wait_send()
        for cp in local:
            cp.wait()

    dma = pltpu.SemaphoreType.DMA
    return _CommPlan(
        ins=[shards[name] for name in names],
        out_shape=[jax.ShapeDtypeStruct(spec[name][0], BF16) for name in names],
        scratch=[dma((3 * nw,)), dma((3 * nw,)), dma((3 * nw,)), dma((3 * nw,)), dma((nw,))],
        start=start, finish=finish)


def _shard_shape(shape, axis):
    return tuple(d // N_CHIPS if a == axis else d for a, d in enumerate(shape))


def _exchange_plan(names, grads):
    spec = {name: (shape, axis) for name, shape, axis in BIG}
    nw = len(names)

    def ops(grad, stack, sems):
        send_sems, recv_sems, local_sems = sems
        x, y, c, others = _place()
        mine = 2 * x + y
        me, sibling = (x, y, c), (x, y, 1 - c)

        def dev(px, py, pc):
            return 4 * px + 2 * py + pc

        def copy(w, n, src, slot, to):
            return pltpu.make_async_remote_copy(
                src_ref=src, dst_ref=stack[w].at[slot], send_sem=send_sems.at[7 * w + n],
                recv_sem=recv_sems.at[7 * w + n], device_id=to, device_id_type=MESH)

        local, first, arrived, passed_on, from_sibling = [], [], [], [], []
        for w in range(nw):
            shape, axis = spec[names[w]]
            size = shape[axis] // N_CHIPS
            own = _slab(grad[w], axis, mine, size)
            local.append(pltpu.make_async_copy(own, stack[w].at[dev(*me)], local_sems.at[w]))
            first.append(copy(w, 0, own, dev(*me), sibling))
            from_sibling.append(copy(w, 0, own, dev(*sibling), me))
            for t, (qx, qy) in enumerate(others):
                got = stack[w].at[dev(qx, qy, c)]
                first.append(copy(w, 1 + t, _slab(grad[w], axis, 2 * qx + qy, size), dev(*me), (qx, qy, c)))
                arrived.append(copy(w, 1 + t, got, dev(qx, qy, c), me))
                passed_on.append(copy(w, 4 + t, got, dev(qx, qy, c), sibling))
                from_sibling.append(copy(w, 4 + t, got, dev(qx, qy, 1 - c), me))
        return local, first, arrived, passed_on, from_sibling

    def start(grad, stack, sems):
        local, first, _, _, _ = ops(grad, stack, sems)
        for cp in local + first:
            cp.start()

    def finish(grad, stack, sems):
        local, first, arrived, passed_on, from_sibling = ops(grad, stack, sems)
        for got, onward in zip(arrived, passed_on, strict=True):
            got.wait_recv()
            onward.start()
        for got in from_sibling:
            got.wait_recv()
        for cp in first + passed_on:
            cp.wait_send()
        for cp in local:
            cp.wait()

    dma = pltpu.SemaphoreType.DMA
    return _CommPlan(
        ins=[grads[name] for name in names],
        out_shape=[jax.ShapeDtypeStruct((N_DEV,) + _shard_shape(*spec[name]), BF16) for name in names],
        scratch=[dma((7 * nw,)), dma((7 * nw,)), dma((nw,))],
        start=start, finish=finish)


def _adamw(w, g, m, v):
    m = ADAM_B1 * m + (1.0 - ADAM_B1) * g
    v = ADAM_B2 * v + (1.0 - ADAM_B2) * (g * g)
    m_hat = m / (1.0 - ADAM_B1 ** ADAM_STEP)
    v_hat = v / (1.0 - ADAM_B2 ** ADAM_STEP)
    delta = -ADAM_LR * (m_hat / (jnp.sqrt(v_hat) + ADAM_EPS) + ADAM_WD * w)
    return delta, m, v


def _reduce_adamw(name, stack, w, m, v):
    rows, cols = w.shape
    tr = next(t for t in (256, 128, 64) if rows % t == 0)

    def body(s_ref, w_ref, m_ref, v_ref, g_ref, d_ref, nm_ref, nv_ref):
        g = s_ref[0].astype(F32)
        for d in range(1, N_DEV):
            g = g + s_ref[d].astype(F32)
        g_ref[...] = g
        d_ref[...], nm_ref[...], nv_ref[...] = _adamw(w_ref[...], g, m_ref[...], v_ref[...])

    blk = pl.BlockSpec((tr, cols), lambda i: (i, 0))
    return pl.pallas_call(
        body, name=name, grid=(rows // tr,),
        in_specs=[pl.BlockSpec((N_DEV, tr, cols), lambda i: (0, i, 0)), blk, blk, blk],
        out_specs=[blk] * 4, out_shape=[jax.ShapeDtypeStruct((rows, cols), F32)] * 4,
        compiler_params=_cparams(),
    )(stack, w, m, v)


def _small_step(pack, w, m, v):
    def body(p_ref, w_ref, m_ref, v_ref, g_ref, d_ref, nm_ref, nv_ref, loss_ref, all_ref, send_sems, recv_sems):
        x, y, c, _ = _place()
        me = 4 * x + 2 * y + c
        all_ref[me] = p_ref[...]
        sent = []
        for n in range(1, N_DEV):
            peer = me ^ n
            cp = pltpu.make_async_remote_copy(
                src_ref=p_ref, dst_ref=all_ref.at[me], send_sem=send_sems.at[n - 1], recv_sem=recv_sems.at[n - 1],
                device_id=(peer // 4, (peer // 2) % 2, peer % 2), device_id_type=MESH)
            cp.start()
            sent.append(cp)
        for n in range(1, N_DEV):
            peer = me ^ n
            pltpu.make_async_remote_copy(
                src_ref=p_ref, dst_ref=all_ref.at[peer], send_sem=send_sems.at[n - 1], recv_sem=recv_sems.at[n - 1],
                device_id=(peer // 4, (peer // 2) % 2, peer % 2), device_id_type=MESH).wait_recv()
        for cp in sent:
            cp.wait_send()
        tot = all_ref[0]
        for d in range(1, N_DEV):
            tot = tot + all_ref[d]
        g = tot[:SMALL_ROWS]
        g_ref[...] = g
        d_ref[...], nm_ref[...], nv_ref[...] = _adamw(w_ref[...], g, m_ref[...], v_ref[...])
        loss_ref[...] = jnp.sum(jnp.sum(tot[SMALL_ROWS:], axis=1, keepdims=True), axis=0, keepdims=True)

    vm = pl.BlockSpec(memory_space=pltpu.VMEM)
    small = jax.ShapeDtypeStruct((SMALL_ROWS, LANES), F32)
    return pl.pallas_call(
        body, name="small_step",
        in_specs=[vm] * 4, out_specs=[vm] * 5,
        out_shape=[small] * 4 + [jax.ShapeDtypeStruct((1, 1), F32)],
        scratch_shapes=[pltpu.VMEM((N_DEV, PACK_ROWS, LANES), F32),
                        pltpu.SemaphoreType.DMA((N_DEV - 1,)), pltpu.SemaphoreType.DMA((N_DEV - 1,))],
    )(pack, w, m, v)


LATER_WEIGHTS = tuple(name for name, _, _ in BIG if name != "w_in")


def _layer_step(x, mem, tgt, shards, vec):
    s = x.shape[0]
    d = D_MODEL
    tm = min(ROW_TILE, s)
    tl = min(WIDE_TILE, s)
    xb, cos2, sin2, w_in = _prep(x, _gather_plan(("w_in",), shards))
    bf = lambda w: ((s, w), BF16)
    f32 = lambda w: ((s, w), F32)

    w_sb, w_rqk = w_in[:, :OFF_RET_Q], w_in[:, OFF_RET_Q:OFF_RET_V]
    w_rvg, w_gate = w_in[:, OFF_RET_V:OFF_GATE], w_in[:, OFF_GATE:]
    q_scale = lambda width, q_width, scale: jnp.concatenate(
        [jnp.full((1, q_width), scale, F32), jnp.ones((1, width - q_width), F32)], axis=1)
    n_groups = 3 * SB_WIDTH // LANES

    def sb_epi(acc, t, i, j):
        scaled = acc * t[0]
        return [jnp.stack([scaled[:, g * LANES:(g + 1) * LANES] for g in range(n_groups)])], []

    (sb_qkv,) = _mm(
        "in_sb", xb, w_sb, s, 3 * SB_WIDTH, d, tm=tl, tn=3 * SB_WIDTH, tk=d, epi=sb_epi,
        ins=[(q_scale(3 * SB_WIDTH, SB_WIDTH, SB_SCALE), *_rowvec(3 * SB_WIDTH))],
        outs=[((n_groups, s, LANES), BF16, (n_groups, tl, LANES), lambda i, j: (0, i, 0))])

    def rope_epi(acc, t, i, j):
        cos, sin, scale = t
        parts = []
        for g in range(acc.shape[1] // RET_QK):
            xg = acc[:, g * RET_QK:(g + 1) * RET_QK]
            parts.append(xg * cos + _swap_halves(xg) * sin)
        return [jnp.concatenate(parts, axis=1) * scale], []

    rope_in = ((tl, RET_QK), lambda i, j: (i, 0))
    (rqk,) = _mm("in_rqk", xb, w_rqk, s, 2 * RET_QK_WIDTH, d, tm=tl, tn=2 * RET_QK_WIDTH, tk=d, epi=rope_epi,
                 chunk=MXU_COLS,
                 ins=[(cos2, *rope_in), (sin2, *rope_in),
                      (q_scale(2 * RET_QK_WIDTH, RET_QK_WIDTH, RET_SCALE), *_rowvec(2 * RET_QK_WIDTH))],
                 outs=[(*f32(2 * RET_QK_WIDTH), *_tile(tl, 2 * RET_QK_WIDTH))])
    (rvg,) = _mm("in_rvg", xb, w_rvg, s, 2 * RET_V_WIDTH, d, tm=tl, tn=2 * RET_V_WIDTH, tk=d, chunk=MXU_COLS,
                 epi=_plain, outs=[(*bf(2 * RET_V_WIDTH), *_tile(tl, 2 * RET_V_WIDTH))])
    (gates,) = _mm("in_gate", xb, w_gate, s, 2 * d, d, tm=tl, tn=2 * d, tk=d, chunk=MXU_COLS,
                   epi=lambda acc, t, i, j: ([_sigmoid(acc + t[0])], []),
                   ins=[(vec["b_gate"], *_rowvec(2 * d))], outs=[(*bf(2 * d), *_tile(tl, 2 * d))])

    sb_out, sb_out_f32, *gathered = _sb_fwd(sb_qkv, s, comm=_gather_plan(LATER_WEIGHTS, shards))
    wt = dict(zip(LATER_WEIGHTS, gathered, strict=True))
    ret, gated = _ret_fwd(rqk, rvg, s)
    (y_sb,) = _mm("sb_o", sb_out, wt["w_sb_o"], s, d, SB_WIDTH, tm=tl, tn=d, tk=SB_WIDTH, epi=_plain,
                  outs=[(*bf(d), *_tile(tl, d))])
    y_ret, mixin = _mm(
        "ret_o", gated, wt["w_ret_o"], s, d, RET_V_WIDTH, tm=tl, tn=d, tk=RET_V_WIDTH, chunk=MXU_COLS,
        epi=lambda acc, t, i, j: ([acc, t[0].astype(F32) * t[2].astype(F32) + t[1].astype(F32) * acc], []),
        ins=[(gates, *_tile(tl, d)), (gates, *_tile(tl, d, 1)), (y_sb, *_tile(tl, d))],
        outs=[(*bf(d), *_tile(tl, d)), (*bf(d), *_tile(tl, d))])

    def ln_epi(acc, t, i, j):
        *res, g, b = t
        prev = res[0] if len(res) == 1 else res[0] * res[1] + res[2]
        xhat, rstd = _norm(DN_ALPHA * prev + acc)
        return [xhat * g + b, xhat, rstd], []

    full = _tile(tm, d)
    col1 = ((tm, 1), lambda i, j: (i, 0))
    vec_in = lambda name: (vec[name], *_rowvec(d))
    ln_outs = [(*bf(d), *full), (*f32(d), *full), ((s, 1), F32, *col1)]
    x1b, xhat1, rstd1 = _mm(
        "mix_o", mixin, wt["w_mix_o"], s, d, d, tm=tm, tn=d, tk=d, epi=ln_epi,
        ins=[(x, *full), vec_in("ln1_g"), vec_in("ln1_b")], outs=ln_outs)

    (qm,) = _mm("mem_q", x1b, wt["w_mem_q"], s, d, d, tm=tl, tn=d, tk=d,
                epi=lambda acc, t, i, j: ([acc * MEM_SCALE], []), outs=[(*bf(d), *_tile(tl, d))])
    (kv,) = _mm("mem_kv", mem, wt["w_mem_kv"], MEM_LEN, 2 * d, d, tm=MEM_LEN, tn=d, tk=d, epi=_plain,
                outs=[((MEM_LEN, 2 * d), BF16, *_tile(MEM_LEN, d))])
    att = _xattn_fwd(qm, kv, s)
    x2b, xhat2, rstd2 = _mm(
        "mem_o", att, wt["w_mem_o"], s, d, d, tm=tm, tn=d, tk=d, epi=ln_epi,
        ins=[(xhat1, *full), vec_in("ln1_g"), vec_in("ln1_b"), vec_in("ln2_g"), vec_in("ln2_b")], outs=ln_outs)

    fh = FFN_HIDDEN
    tf = fh // 2
    (f1,) = _mm("ffn_in1", x2b, wt["w_ffn_in"], s, fh, d, tm=tl, tn=tf, tk=d, epi=_plain, j_outer=True,
                outs=[(*bf(fh), *_tile(tl, tf))])

    def swiglu_epi(acc, t, i, j):
        a = t[0].astype(F32)
        return [acc, a * _sigmoid(a) * acc], []

    f2, act = _mm(
        "ffn_in2", x2b, wt["w_ffn_in"], s, fh, d, tm=tm, tn=fh, tk=d, b_off=(0, 1), epi=swiglu_epi, chunk=MXU_COLS,
        ins=[(f1, *_tile(tm, fh))], outs=[(*bf(fh), *_tile(tm, fh)), (*bf(fh), *_tile(tm, fh))])

    def head_epi(acc, t, i, j):
        prev_hat, prev_g, prev_b, g, b, target = t
        xhat, rstd = _norm(DN_ALPHA * (prev_hat * prev_g + prev_b) + acc)
        err = xhat * g + b - target
        dy = err * (1.0 / d)
        du = _norm_bwd(dy * g, xhat, rstd)
        return [du], [_colsum(dy * xhat), _colsum(dy), _colsum(err * err) * (0.5 / d)]

    vec_acc = ((1, d), F32)
    du3b, dg3, db3, loss_cols = _mm(
        "ffn_out", act, wt["w_ffn_out"], s, d, fh, tm=tm, tn=d, tk=fh, epi=head_epi,
        ins=[(xhat2, *full), vec_in("ln2_g"), vec_in("ln2_b"), vec_in("ln3_g"), vec_in("ln3_b"), (tgt, *full)],
        outs=[(*bf(d), *full)], accs=[vec_acc] * 3)

    grads = {}
    ts = min(SEQ_TILE, s)

    def wgrad(name, a, b, m, n, tm_, tn_, tk_=None):
        (g,) = _mm(name, a, b, m, n, a.shape[0], tm=tm_, tn=tn_, tk=tk_ or ts, ta=True, epi=_plain,
                   outs=[((m, n), BF16, *_tile(tm_, tn_))])
        return g

    def ffn_bwd_epi(acc, t, i, j):
        a, b = t[0].astype(F32), t[1].astype(F32)
        sg = _sigmoid(a)
        return [acc * b * (sg * (1.0 + a * (1.0 - sg))), acc * (a * sg)], []

    df1, df2 = _mm(
        "ffn_out_t", du3b, wt["w_ffn_out"], s, fh, d, tm=tm, tn=fh, tk=d, tb=True, epi=ffn_bwd_epi, chunk=MXU_COLS,
        ins=[(f1, *_tile(tm, fh)), (f2, *_tile(tm, fh))],
        outs=[(*bf(fh), *_tile(tm, fh)), (*bf(fh), *_tile(tm, fh))])
    grads["w_ffn_out"] = wgrad("g_ffn_out", act, du3b, fh, d, tf, d)
    grads["w_ffn_in"] = jnp.concatenate(
        [wgrad("g_ffn_in1", x2b, df1, d, fh, d, tf), wgrad("g_ffn_in2", x2b, df2, d, fh, d, tf)], axis=1)
    (dx2a,) = _mm("ffn_in1_t", df1, wt["w_ffn_in"], s, d, fh, tm=tm, tn=d, tk=fh, tb=True, epi=_plain,
                  outs=[(*f32(d), *full)])

    def ln_bwd(name, a, b, k, tk, b_off, more, scales, xhat, rstd, g):
        def epi(acc, t, i, j):
            *extra, xh, rs, gg = t
            dy = acc
            for e, sc in zip(extra, scales, strict=True):
                dy = dy + e.astype(F32) * sc
            return [_norm_bwd(dy * gg, xh, rs)], [_colsum(dy * xh), _colsum(dy)]

        return _mm(name, a, b, s, d, k, tm=tm, tn=d, tk=tk, tb=True, b_off=b_off, epi=epi,
                   ins=[(e, *full) for e in more] + [(xhat, *full), (rstd, *col1), (g, *_rowvec(d))],
                   outs=[(*bf(d), *full)], accs=[vec_acc] * 2)

    du2b, dg2, db2 = ln_bwd("ffn_in2_t", df2, wt["w_ffn_in"], fh, fh, (0, 1), [dx2a, du3b], [1.0, DN_ALPHA],
                            xhat2, rstd2, vec["ln2_g"])

    (datt,) = _mm("mem_o_t", du2b, wt["w_mem_o"], s, d, d, tm=tl, tn=d, tk=d, tb=True, epi=_plain,
                  outs=[(*bf(d), *_tile(tl, d))])
    grads["w_mem_o"] = wgrad("g_mem_o", att, du2b, d, d, d, d)
    dqm, dkv = _xattn_bwd(qm, kv, datt, s)
    grads["w_mem_q"] = wgrad("g_mem_q", x1b, dqm, d, d, d, d)
    grads["w_mem_kv"] = wgrad("g_mem_kv", mem, dkv, d, 2 * d, d, d, MEM_LEN)
    du1b, dg1, db1 = ln_bwd("mem_q_t", dqm, wt["w_mem_q"], d, d, (0, 0), [du2b], [DN_ALPHA],
                            xhat1, rstd1, vec["ln1_g"])

    def merge_bwd_epi(acc, t, i, j):
        g0, g1, ysb, yret = (v.astype(F32) for v in t)
        dgate0 = acc * ysb * (g0 * (1.0 - g0))
        dgate1 = acc * yret * (g1 * (1.0 - g1))
        return [dgate0, dgate1, acc * g0, acc * g1], [_colsum(dgate0), _colsum(dgate1)]

    dgate0, dgate1, dy_sb, dy_ret, dbg0, dbg1 = _mm(
        "mix_o_t", du1b, wt["w_mix_o"], s, d, d, tm=tm, tn=d, tk=d, tb=True, epi=merge_bwd_epi,
        ins=[(gates, *full), (gates, *_tile(tm, d, 1)), (y_sb, *full), (y_ret, *full)],
        outs=[(*bf(d), *full)] * 4, accs=[vec_acc] * 2)
    grads["w_mix_o"] = wgrad("g_mix_o", mixin, du1b, d, d, d, d)
    grads["w_sb_o"] = wgrad("g_sb_o", sb_out, dy_sb, SB_WIDTH, d, SB_WIDTH, d)
    grads["w_ret_o"] = wgrad("g_ret_o", gated, dy_ret, RET_V_WIDTH, d, RET_V_WIDTH, d)
    (dsb_out,) = _mm("sb_o_t", dy_sb, wt["w_sb_o"], s, SB_WIDTH, d, tm=tl, tn=SB_WIDTH, tk=d, tb=True, epi=_plain,
                     outs=[(*bf(SB_WIDTH), *_tile(tl, SB_WIDTH))])

    def gate_norm_bwd_epi(acc, t, i, j):
        r, g = t[0], t[1].astype(F32)
        drg, dret = [], []
        for h in range(acc.shape[1] // RET_V):
            sl = slice(h * RET_V, (h + 1) * RET_V)
            xhat, rstd = _norm(r[:, sl])
            gg, dd = g[:, sl], acc[:, sl]
            sg = _sigmoid(gg)
            drg.append(dd * xhat * (sg * (1.0 + gg * (1.0 - sg))))
            dret.append(_norm_bwd(dd * (gg * sg), xhat, rstd))
        return [jnp.concatenate(drg, axis=1), jnp.concatenate(dret, axis=1)], []

    drg, dret = _mm(
        "ret_o_t", dy_ret, wt["w_ret_o"], s, RET_V_WIDTH, d, tm=tm, tn=d, tk=d, tb=True, epi=gate_norm_bwd_epi,
        chunk=MXU_COLS,
        ins=[(ret, *full), (rvg, *_tile(tm, d, 1))],
        outs=[(*bf(RET_V_WIDTH), *full)] * 2)

    drq = _ret_bwd_q(rqk, rvg, dret, cos2, sin2, s)
    drk, drv = _ret_bwd_kv(rqk, rvg, dret, cos2, sin2, s)
    dsq, dsk, dsv, *stacked = _sb_bwd(sb_qkv, sb_out_f32, dsb_out, s, comm=_exchange_plan(LATER_WEIGHTS, grads))
    stacks = dict(zip(LATER_WEIGHTS, stacked, strict=True))

    dh = jnp.concatenate([dsq, dsk, dsv, drq, drk, drv, drg, dgate0, dgate1], axis=1)
    grads["w_in"] = wgrad("g_in", xb, dh, d, IN_WIDTH, d, IN_WIDTH // N_CHIPS)
    grad_x, stacks["w_in"] = _mm(
        "in_t", dh, w_in, s, d, IN_WIDTH, tm=tl, tn=d, tk=IN_WIDTH // N_CHIPS, tb=True,
        epi=lambda acc, t, i, j: ([acc + DN_ALPHA * t[0].astype(F32)], []),
        ins=[(du1b, *_tile(tl, d))], outs=[(*f32(d), *_tile(tl, d))], comm=_exchange_plan(("w_in",), grads))

    small = {"b_gate": jnp.concatenate([dbg0, dbg1], axis=1), "ln1_g": dg1, "ln1_b": db1, "ln2_g": dg2,
             "ln2_b": db2, "ln3_g": dg3, "ln3_b": db3}
    return grad_x, stacks, small, loss_cols


def kernel(x, mem, w_in, b_gate, w_sb_o, w_ret_o, w_mix_o, ln1_g, ln1_b, w_mem_q, w_mem_kv, w_mem_o, ln2_g, ln2_b, w_ffn_in, w_ffn_out, ln3_g, ln3_b, loss_target, m_w_in, m_b_gate, m_w_sb_o, m_w_ret_o, m_w_mix_o, m_ln1_g, m_ln1_b, m_w_mem_q, m_w_mem_kv, m_w_mem_o, m_ln2_g, m_ln2_b, m_w_ffn_in, m_w_ffn_out, m_ln3_g, m_ln3_b, v_w_in, v_b_gate, v_w_sb_o, v_w_ret_o, v_w_mix_o, v_ln1_g, v_ln1_b, v_w_mem_q, v_w_mem_kv, v_w_mem_o, v_ln2_g, v_ln2_b, v_w_ffn_in, v_w_ffn_out, v_ln3_g, v_ln3_b):
    given = dict(locals())
    s = x.shape[1]
    x2d = x.reshape(s, D_MODEL)
    tgt = loss_target.reshape(s, D_MODEL)
    mem2d = mem.reshape(MEM_LEN, D_MODEL)
    shard = {name: given[name].reshape(_shard_shape(shape, axis)) for name, shape, axis in BIG}
    vec = {name: given[name] for name in SMALL}

    shards_bf = {name: _cast_bf16("cast_" + name, shard[name]) for name, _, _ in BIG}

    grad_x, stacks, small, loss_cols = _layer_step(x2d, mem2d, tgt, shards_bf, vec)

    out = {}
    for name, shape, axis in BIG:
        stack = stacks[name]
        shp = given[name].shape
        res = _reduce_adamw("adamw_" + name, stack, shard[name], given["m_" + name].reshape(stack.shape[1:]),
                            given["v_" + name].reshape(stack.shape[1:]))
        out[name] = [r.reshape(shp) for r in res]

    pack = jnp.concatenate([small[name] for name in SMALL] + [loss_cols], axis=1).reshape(PACK_ROWS, LANES)
    cat = lambda pre: jnp.concatenate([given[pre + name] for name in SMALL], axis=1).reshape(SMALL_ROWS, LANES)
    *res, loss = _small_step(pack, cat(""), cat("m_"), cat("v_"))
    flat = [r.reshape(1, SMALL_LEN) for r in res]
    off = 0
    for name in SMALL:
        n = given[name].shape[1]
        out[name] = [r[:, off:off + n] for r in flat]
        off += n

    return (loss.reshape(()), grad_x.reshape(x.shape),
            *[out[name][0] for name in WEIGHT_ORDER], *[out[name][1] for name in WEIGHT_ORDER],
            *[out[name][2] for name in WEIGHT_ORDER], *[out[name][3] for name in WEIGHT_ORDER])
```

```python
import functools

import jax
import jax.numpy as jnp
import numpy as np
from jax import lax
from jax.experimental import pallas as pl
from jax.experimental.pallas import tpu as pltpu

F32, BF16 = jnp.float32, jnp.bfloat16
MESH = pl.DeviceIdType.MESH

D_MODEL = 1024
MEM_LEN = 256
SB_HEADS, SB_DIM, SB_WIDTH = 8, 64, 512
RET_HEADS, RET_QK, RET_V = 4, 128, 256
RET_QK_WIDTH, RET_V_WIDTH = 512, 1024
ROPE_BASE = 10000.0
MEM_HEADS, MEM_DIM = 4, 256
FFN_HIDDEN = 2816
IN_WIDTH = 6656
OFF_RET_Q, OFF_RET_V, OFF_RET_G, OFF_GATE = 1536, 2560, 3584, 4608
DN_ALPHA = 2.0 ** 0.25
LN_EPS = 1e-5
SB_SCALE = SB_DIM ** -0.5
SB_DEAD = -110.0
RET_SCALE = RET_QK ** -0.5
MEM_SCALE = MEM_DIM ** -0.5
ADAM_LR, ADAM_B1, ADAM_B2, ADAM_EPS, ADAM_WD, ADAM_STEP = 0.001, 0.9, 0.999, 1e-08, 0.01, 10

N_DEV, N_CHIPS = 8, 4

LANES = 128
MXU_COLS = 256
VMEM_LIMIT_BYTES = 52 * 2 ** 20
ROW_TILE = 512
WIDE_TILE = 1024
SEQ_TILE = 2048
SB_BLOCK = 256
RET_BLOCK = 256
XATTN_ROWS = 512

BIG = (
    ("w_in", (D_MODEL, IN_WIDTH), 1),
    ("w_sb_o", (SB_WIDTH, D_MODEL), 1),
    ("w_ret_o", (RET_V_WIDTH, D_MODEL), 0),
    ("w_mix_o", (D_MODEL, D_MODEL), 0),
    ("w_mem_q", (D_MODEL, D_MODEL), 0),
    ("w_mem_kv", (D_MODEL, 2 * D_MODEL), 1),
    ("w_mem_o", (D_MODEL, D_MODEL), 0),
    ("w_ffn_in", (D_MODEL, 2 * FFN_HIDDEN), 1),
    ("w_ffn_out", (FFN_HIDDEN, D_MODEL), 0),
)
SMALL = ("b_gate", "ln1_g", "ln1_b", "ln2_g", "ln2_b", "ln3_g", "ln3_b")
SMALL_LEN = 2 * D_MODEL + 6 * D_MODEL
SMALL_ROWS = SMALL_LEN // LANES
PACK_ROWS = SMALL_ROWS + D_MODEL // LANES
WEIGHT_ORDER = ("w_in", "b_gate", "w_sb_o", "w_ret_o", "w_mix_o", "ln1_g", "ln1_b", "w_mem_q", "w_mem_kv",
                "w_mem_o", "ln2_g", "ln2_b", "w_ffn_in", "w_ffn_out", "ln3_g", "ln3_b")


def _cparams():
    return pltpu.CompilerParams(vmem_limit_bytes=VMEM_LIMIT_BYTES)


def _dot(a, b, ca, cb):
    return lax.dot_general(a, b, (((ca,), (cb,)), ((), ())), preferred_element_type=F32)


def _sigmoid(x):
    return 1.0 / (1.0 + jnp.exp(-x))


def _mm(name, a, b, m, n, k, *, tm, tn, tk, epi, outs, ins=(), accs=(), ta=False, tb=False,
        a_off=(0, 0), b_off=(0, 0), j_outer=False, comm=None, chunk=None):
    assert m % tm == 0 and n % tn == 0 and k % tk == 0, (name, m, n, k, tm, tn, tk)
    assert chunk is None or (k == tk and tn % chunk == 0), name
    ni, nj, nk = m // tm, n // tn, k // tk
    assert not accs or nj == 1, name
    ij = (lambda g0, g1: (g1, g0)) if j_outer else (lambda g0, g1: (g0, g1))

    def spec(block, index):
        return pl.BlockSpec(block, lambda g0, g1, kk: index(*ij(g0, g1), kk))

    if ta:
        a_spec = spec((tk, tm), lambda i, j, kk: (kk + a_off[0], i + a_off[1]))
    else:
        a_spec = spec((tm, tk), lambda i, j, kk: (i + a_off[0], kk + a_off[1]))
    if tb:
        b_spec = spec((tn, tk), lambda i, j, kk: (j + b_off[0], kk + b_off[1]))
    else:
        b_spec = spec((tk, tn), lambda i, j, kk: (kk + b_off[0], j + b_off[1]))
    in_specs = [a_spec, b_spec]
    for _, bs, im in ins:
        in_specs.append(spec(bs, lambda i, j, kk, im=im: im(i, j)))
    out_specs, out_shape = [], []
    for shape, dtype, bs, im in outs:
        out_specs.append(spec(bs, lambda i, j, kk, im=im: im(i, j)))
        out_shape.append(jax.ShapeDtypeStruct(shape, dtype))
    for shape, dtype in accs:
        out_specs.append(spec(shape, lambda i, j, kk, nd=len(shape): (0,) * nd))
        out_shape.append(jax.ShapeDtypeStruct(shape, dtype))
    n_in, n_out, n_acc = len(ins), len(outs), len(accs)
    ca, cb = (0 if ta else 1), (1 if tb else 0)
    grid = (*ij(ni, nj), nk)
    comm_ins, comm_outs, comm_scratch = [], [], []
    if comm is not None:
        comm_in_specs, comm_out_specs = comm.specs
        comm_ins, comm_outs, comm_scratch = list(comm.ins), list(comm.out_shape), list(comm.scratch)
        in_specs += comm_in_specs
        out_specs += comm_out_specs
        out_shape += comm_outs
    n_ci, n_co = len(comm_ins), len(comm_outs)

    def body(*refs):
        a_ref, b_ref = refs[:2]
        in_refs = refs[2:2 + n_in]
        ci_refs = refs[2 + n_in:2 + n_in + n_ci]
        rest = refs[2 + n_in + n_ci:]
        out_refs, acc_refs = rest[:n_out], rest[n_out:n_out + n_acc]
        co_refs = rest[n_out + n_acc:n_out + n_acc + n_co]
        scratch = rest[n_out + n_acc + n_co:]
        sem_refs, scratch = scratch[:len(comm_scratch)], scratch[len(comm_scratch):]
        (i, j), kk = ij(pl.program_id(0), pl.program_id(1)), pl.program_id(2)
        if comm is not None:
            first_step, last_step = _grid_ends(grid)
            pl.when(first_step)(lambda: comm.start(ci_refs, co_refs, sem_refs))
        def finish(acc, cols=slice(None)):
            def of(r):
                return r[..., cols] if r.shape[-1] == tn else r[...]

            o_tiles, a_tiles = epi(acc, [of(r) for r in in_refs], i, j)
            for r, t in zip(out_refs, o_tiles, strict=True):
                r[..., cols] = t.astype(r.dtype)
            if n_acc:
                @pl.when(i == 0)
                def _():
                    for r, t in zip(acc_refs, a_tiles, strict=True):
                        r[..., cols] = t

                @pl.when(i > 0)
                def _():
                    for r, t in zip(acc_refs, a_tiles, strict=True):
                        r[..., cols] += t

        if chunk is not None:
            a_tile = a_ref[...].astype(BF16)
            for c0 in range(0, tn, chunk):
                cols = slice(c0, c0 + chunk)
                b_part = b_ref[cols, :] if tb else b_ref[:, cols]
                finish(_dot(a_tile, b_part.astype(BF16), ca, cb), cols)
            if comm is not None:
                pl.when(last_step)(lambda: comm.finish(ci_refs, co_refs, sem_refs))
            return

        part = _dot(a_ref[...].astype(BF16), b_ref[...].astype(BF16), ca, cb)
        if nk == 1:
            finish(part)
        else:
            acc_ref = scratch[0]

            @pl.when(kk == 0)
            def _():
                acc_ref[...] = part

            @pl.when(kk > 0)
            def _():
                acc_ref[...] += part

            @pl.when(kk == nk - 1)
            def _():
                finish(acc_ref[...])

        if comm is not None:
            pl.when(last_step)(lambda: comm.finish(ci_refs, co_refs, sem_refs))

    res = pl.pallas_call(
        body, name=name, grid=grid, in_specs=in_specs, out_specs=out_specs, out_shape=out_shape,
        scratch_shapes=comm_scratch + ([pltpu.VMEM((tm, tn), F32)] if nk > 1 else []),
        compiler_params=_cparams(),
    )(a, b, *[x for x, _, _ in ins], *comm_ins)
    return res


def _grid_ends(grid):
    ids = [pl.program_id(ax) for ax in range(len(grid))]
    first = functools.reduce(jnp.logical_and, [p == 0 for p in ids])
    last = functools.reduce(jnp.logical_and, [p == n - 1 for p, n in zip(ids, grid, strict=True)])
    return first, last


def _tile(tm, tn, dj=0):
    return (tm, tn), (lambda i, j: (i, j + dj))


def _rowvec(tn, dj=0):
    return (1, tn), (lambda i, j: (0, j + dj))


def _plain(acc, tiles, i, j):
    return [acc], []


def _ew(name, fn, ins, outs, rows, tr):
    assert rows % tr == 0, (name, rows, tr)
    in_specs = []
    for x in ins:
        if x.shape[0] == rows:
            in_specs.append(pl.BlockSpec((tr, x.shape[1]), lambda i: (i, 0)))
        else:
            in_specs.append(pl.BlockSpec(x.shape, lambda i: (0, 0)))
    n_in = len(ins)

    def body(*refs):
        res = fn(*[r[...] for r in refs[:n_in]])
        for r, t in zip(refs[n_in:], res, strict=True):
            r[...] = t.astype(r.dtype)

    return pl.pallas_call(
        body, name=name, grid=(rows // tr,), in_specs=in_specs,
        out_specs=[pl.BlockSpec((tr, w), lambda i: (i, 0)) for w, _ in outs],
        out_shape=[jax.ShapeDtypeStruct((rows, w), dt) for w, dt in outs],
        compiler_params=_cparams(),
    )(*ins)


def _cast_bf16(name, x):
    rows = x.shape[0]
    tr = next(t for t in (512, 256, 64) if rows % t == 0)
    return _ew(name, lambda v: (v,), [x], [(x.shape[1], BF16)], rows, tr)[0]


def _prep(x, comm):
    s = x.shape[0]
    half = RET_QK // 2
    inv = 1.0 / (ROPE_BASE ** (jnp.arange(half, dtype=F32) / half))
    inv2 = jnp.concatenate([inv, inv]).reshape(1, RET_QK)
    sign = jnp.concatenate([-jnp.ones((half,), F32), jnp.ones((half,), F32)]).reshape(1, RET_QK)
    tr = min(ROW_TILE, s)
    grid = (s // tr,)
    c_in_specs, c_out_specs, c_out_shape, c_scratch, c_ins, split = _host(comm, 3, 3)

    def body(*refs):
        (x_ref, inv_ref, sign_ref), (xb_ref, cos_ref, sin_ref), _, riding = split(refs)
        i = pl.program_id(0)
        first_step, last_step = _grid_ends(grid)
        pl.when(first_step)(lambda: comm.start(*riding))
        xb_ref[...] = x_ref[...].astype(BF16)
        pos = (lax.broadcasted_iota(jnp.int32, (tr, RET_QK), 0) + i * tr).astype(F32)
        ang = pos * inv_ref[...]
        cos_ref[...] = jnp.cos(ang)
        sin_ref[...] = jnp.sin(ang) * sign_ref[...]
        pl.when(last_step)(lambda: comm.finish(*riding))

    vec = pl.BlockSpec((1, RET_QK), lambda i: (0, 0))
    row = lambda w: pl.BlockSpec((tr, w), lambda i: (i, 0))
    return pl.pallas_call(
        body, name="prep", grid=grid,
        in_specs=[row(D_MODEL), vec, vec] + c_in_specs,
        out_specs=[row(D_MODEL), row(RET_QK), row(RET_QK)] + c_out_specs,
        out_shape=[jax.ShapeDtypeStruct((s, D_MODEL), BF16), jax.ShapeDtypeStruct((s, RET_QK), F32),
                   jax.ShapeDtypeStruct((s, RET_QK), F32)] + c_out_shape,
        scratch_shapes=c_scratch, compiler_params=_cparams(),
    )(x, inv2, sign, *c_ins)


def _swap_halves(x):
    return pltpu.roll(x, RET_QK // 2, 1)


def _norm(u):
    mu = jnp.mean(u, axis=-1, keepdims=True)
    d = u - mu
    var = jnp.mean(d * d, axis=-1, keepdims=True)
    rstd = lax.rsqrt(var + LN_EPS)
    return d * rstd, rstd


def _norm_bwd(dxh, xhat, rstd):
    m1 = jnp.mean(dxh, axis=-1, keepdims=True)
    m2 = jnp.mean(dxh * xhat, axis=-1, keepdims=True)
    return rstd * (dxh - m1 - xhat * m2)


def _colsum(t):
    return jnp.sum(t, axis=0, keepdims=True)


def _split_mm(t, tri):
    hi = t.astype(BF16)
    lo = (t - hi.astype(F32)).astype(BF16)
    return _dot(hi, tri, 1, 0) + _dot(lo, tri, 1, 0)


def _sb_masks():
    t = SB_BLOCK
    lane = lax.broadcasted_iota(jnp.int32, (1, LANES), 1)
    first = lane < SB_DIM
    m0 = jnp.where(first, 1.0, 0.0).astype(BF16)
    m1 = jnp.where(first, 0.0, 1.0).astype(BF16)
    row = lax.broadcasted_iota(jnp.int32, (t, t), 0)
    col = lax.broadcasted_iota(jnp.int32, (t, t), 1)
    return first, (m0, m1), row, col


def _sb_logits(qh, k, causal):
    z = _dot(qh, k, 1, 1)
    lp = jnp.log(1.0 + jnp.exp(-jnp.abs(z)))
    a = jnp.minimum(z, 0.0) - lp
    r = jnp.minimum(-z, 0.0) - lp
    if causal is not None:
        r = jnp.where(causal, r, 0.0)
    return a, r


def _sb_walk(i, blocks, l_ref, causal):
    pl.when(i == 0)(lambda: blocks([(i, causal)]))
    pl.when(i > 0)(lambda: blocks([(i, causal), (i - 1, None)]))

    def alive():
        top = jnp.max(functools.reduce(jnp.maximum, [l_ref[c] for c in range(l_ref.shape[0])]))
        return jnp.where(top > SB_DEAD, 1, 0)

    def cond(c):
        return jnp.logical_and(c[0] < i, c[1] > 0)

    def step(c):
        blocks([(i - 1 - c[0], None)])
        return c[0] + 1, alive()

    lax.while_loop(cond, step, (jnp.int32(1), alive()))


def _host(comm, n_in, n_out):
    if comm is None:
        return [], [], [], [], [], lambda refs: (refs[:n_in], refs[n_in:n_in + n_out], refs[n_in + n_out:], None)
    in_specs, out_specs = comm.specs
    n_ci, n_co, n_sem = len(comm.ins), len(comm.out_shape), len(comm.scratch)

    def split(refs):
        ins, ci = refs[:n_in], refs[n_in:n_in + n_ci]
        rest = refs[n_in + n_ci:]
        outs, co = rest[:n_out], rest[n_out:n_out + n_co]
        sems, scratch = rest[n_out + n_co:n_out + n_co + n_sem], rest[n_out + n_co + n_sem:]
        return ins, outs, scratch, (ci, co, sems)

    return in_specs, out_specs, list(comm.out_shape), list(comm.scratch), list(comm.ins), split


def _sb_qkv_specs(s, g):
    groups = SB_HEADS // 2 // g
    return [pl.BlockSpec((g, SB_BLOCK, LANES), lambda p, i: (p, i, 0)),
            pl.BlockSpec((g, s, LANES), lambda p, i: (groups + p, 0, 0)),
            pl.BlockSpec((g, s, LANES), lambda p, i: (2 * groups + p, 0, 0))]


def _sb_fwd(qkv, s, comm=None):
    t = SB_BLOCK
    g = 2
    nq = s // t
    grid = (SB_HEADS // 2 // g, nq)
    c_in_specs, c_out_specs, c_out_shape, c_scratch, c_ins, split = _host(comm, 3, 2)

    def body(*refs):
        (q_ref, k_ref, v_ref), (o_ref, of_ref), (l_ref, acc_ref), riding = split(refs)
        i = pl.program_id(1)
        if comm is not None:
            first_step, last_step = _grid_ends(grid)
            pl.when(first_step)(lambda: comm.start(*riding))
        first, hmask, row, col = _sb_masks()
        after = jnp.where(row > col, 1.0, 0.0).astype(BF16)
        causal = col < row
        heads = [(p, h) for p in range(g) for h in range(2)]
        qh = {(p, h): q_ref[p] * hmask[h] for p, h in heads}
        l_ref[...] = jnp.zeros_like(l_ref)
        acc_ref[...] = jnp.zeros_like(acc_ref)

        def blocks(todo):
            chains = [(b, p, h) for b in range(len(todo)) for p, h in heads]
            starts = [pl.multiple_of(kb * t, t) for kb, _ in todo]
            ks = {(b, p): k_ref[p, pl.ds(st, t), :] for b, st in enumerate(starts) for p in range(g)}
            vs = {(b, p): v_ref[p, pl.ds(st, t), :] for b, st in enumerate(starts) for p in range(g)}
            ar = {(b, p, h): _sb_logits(qh[p, h], ks[b, p], todo[b][1]) for b, p, h in chains}
            later = {c: _split_mm(ar[c][1], after) for c in chains}
            carry = {(p, h): l_ref[2 * p + h] for p, h in heads}
            w = {}
            for b, (_, mask) in enumerate(todo):
                for p, h in heads:
                    wc = jnp.exp(ar[b, p, h][0] + later[b, p, h] + carry[p, h])
                    w[b, p, h] = wc if mask is None else jnp.where(mask, wc, 0.0)
                carry = {(p, h): carry[p, h] + jnp.sum(ar[b, p, h][1], axis=1, keepdims=True) for p, h in heads}
            pv = {(b, p, h): _dot(w[b, p, h].astype(BF16), vs[b, p], 1, 0) for b, p, h in chains}
            for p in range(g):
                lanes = slice(p * LANES, (p + 1) * LANES)
                acc = acc_ref[:, lanes]
                for b in range(len(todo)):
                    acc = acc + jnp.where(first, pv[b, p, 0], pv[b, p, 1])
                acc_ref[:, lanes] = acc
            for p, h in heads:
                l_ref[2 * p + h] = carry[p, h]

        _sb_walk(i, blocks, l_ref, causal)
        o_ref[...] = acc_ref[...].astype(o_ref.dtype)
        of_ref[...] = acc_ref[...]
        if comm is not None:
            pl.when(last_step)(lambda: comm.finish(*riding))

    blk = pl.BlockSpec((t, g * LANES), lambda p, i: (i, p))
    return pl.pallas_call(
        body, name="sb_fwd", grid=grid,
        in_specs=_sb_qkv_specs(s, g) + c_in_specs,
        out_specs=[blk, blk] + c_out_specs,
        out_shape=[jax.ShapeDtypeStruct((s, SB_WIDTH), BF16), jax.ShapeDtypeStruct((s, SB_WIDTH), F32)] + c_out_shape,
        scratch_shapes=c_scratch + [pltpu.VMEM((2 * g, t, 1), F32), pltpu.VMEM((t, g * LANES), F32)],
        compiler_params=_cparams(),
    )(qkv, qkv, qkv, *c_ins)


def _sb_bwd(qkv, o, do, s, comm=None):
    t = SB_BLOCK
    g = 2
    nq = s // t
    grid = (SB_HEADS // 2 // g, nq)
    c_in_specs, c_out_specs, c_out_shape, c_scratch, c_ins, split = _host(comm, 5, 3)

    def body(*refs):
        ((q_ref, k_ref, v_ref, o_ref, do_ref), (dq_ref, dk_ref, dv_ref),
         (l_ref, e_ref, dq_acc, dk_acc, dv_acc), riding) = split(refs)
        i = pl.program_id(1)
        if comm is not None:
            first_step, last_step = _grid_ends(grid)
            pl.when(first_step)(lambda: comm.start(*riding))
        first, hmask, row, col = _sb_masks()
        after = jnp.where(row > col, 1.0, 0.0).astype(BF16)
        from_here = jnp.where(row >= col, 1.0, 0.0).astype(BF16)
        causal = col < row

        @pl.when(i == 0)
        def _():
            dk_acc[...] = jnp.zeros_like(dk_acc)
            dv_acc[...] = jnp.zeros_like(dv_acc)

        heads = [(p, h) for p in range(g) for h in range(2)]
        lanes = [slice(p * LANES, (p + 1) * LANES) for p in range(g)]
        q = [q_ref[p] for p in range(g)]
        do_ = [do_ref[:, lanes[p]] for p in range(g)]
        qh = {(p, h): q[p] * hmask[h] for p, h in heads}
        doh = {(p, h): do_[p] * hmask[h] for p, h in heads}
        total = {}
        for p in range(g):
            prod = do_[p].astype(F32) * o_ref[:, lanes[p]]
            total[p, 0] = jnp.sum(jnp.where(first, prod, 0.0), axis=1, keepdims=True)
            total[p, 1] = jnp.sum(jnp.where(first, 0.0, prod), axis=1, keepdims=True)
        l_ref[...] = jnp.zeros_like(l_ref)
        e_ref[...] = jnp.zeros_like(e_ref)
        dq_acc[...] = jnp.zeros_like(dq_acc)

        def blocks(todo):
            chains = [(b, p, h) for b in range(len(todo)) for p, h in heads]
            starts = [pl.multiple_of(kb * t, t) for kb, _ in todo]
            ks = {(b, p): k_ref[p, pl.ds(st, t), :] for b, st in enumerate(starts) for p in range(g)}
            vs = {(b, p): v_ref[p, pl.ds(st, t), :] for b, st in enumerate(starts) for p in range(g)}
            ar = {(b, p, h): _sb_logits(qh[p, h], ks[b, p], todo[b][1]) for b, p, h in chains}
            dw = {(b, p, h): _dot(doh[p, h], vs[b, p], 1, 1) for b, p, h in chains}
            later = {c: _split_mm(ar[c][1], after) for c in chains}
            carry = {(p, h): l_ref[2 * p + h] for p, h in heads}
            wb = {}
            for b, (_, mask) in enumerate(todo):
                for p, h in heads:
                    wc = jnp.exp(ar[b, p, h][0] + later[b, p, h] + carry[p, h])
                    wb[b, p, h] = (wc if mask is None else jnp.where(mask, wc, 0.0)).astype(BF16)
                carry = {(p, h): carry[p, h] + jnp.sum(ar[b, p, h][1], axis=1, keepdims=True) for p, h in heads}
            dvs = {(b, p, h): _dot(wb[b, p, h], do_[p], 0, 0) for b, p, h in chains}
            e = {c: dw[c] * wb[c].astype(F32) for c in chains}
            suffix = {c: _split_mm(e[c], from_here) for c in chains}
            e_carry = {(p, h): e_ref[2 * p + h] for p, h in heads}
            dz = {}
            for b, (_, mask) in enumerate(todo):
                for p, h in heads:
                    before = total[p, h] - (suffix[b, p, h] + e_carry[p, h])
                    dzc = e[b, p, h] - jnp.exp(ar[b, p, h][0]) * (e[b, p, h] + before)
                    dz[b, p, h] = (dzc if mask is None else jnp.where(mask, dzc, 0.0)).astype(BF16)
                e_carry = {(p, h): e_carry[p, h] + jnp.sum(e[b, p, h], axis=1, keepdims=True) for p, h in heads}
            dqs = {(b, p, h): _dot(dz[b, p, h], ks[b, p], 1, 0) for b, p, h in chains}
            dks = {(b, p, h): _dot(dz[b, p, h], q[p], 0, 0) for b, p, h in chains}
            for p in range(g):
                dq = dq_acc[:, lanes[p]]
                for b, st in enumerate(starts):
                    dq = dq + jnp.where(first, dqs[b, p, 0], dqs[b, p, 1])
                    dk_acc[pl.ds(st, t), lanes[p]] += jnp.where(first, dks[b, p, 0], dks[b, p, 1])
                    dv_acc[pl.ds(st, t), lanes[p]] += jnp.where(first, dvs[b, p, 0], dvs[b, p, 1])
                dq_acc[:, lanes[p]] = dq
            for p, h in heads:
                l_ref[2 * p + h] = carry[p, h]
                e_ref[2 * p + h] = e_carry[p, h]

        _sb_walk(i, blocks, l_ref, causal)
        dq_ref[...] = (dq_acc[...] * SB_SCALE).astype(dq_ref.dtype)

        @pl.when(i == nq - 1)
        def _():
            dk_ref[...] = dk_acc[...].astype(dk_ref.dtype)
            dv_ref[...] = dv_acc[...].astype(dv_ref.dtype)

        if comm is not None:
            pl.when(last_step)(lambda: comm.finish(*riding))

    once = pl.Buffered(1)
    q_spec, k_spec, v_spec = _sb_qkv_specs(s, g)
    k_spec = pl.BlockSpec(k_spec.block_shape, k_spec.index_map, pipeline_mode=once)
    v_spec = pl.BlockSpec(v_spec.block_shape, v_spec.index_map, pipeline_mode=once)
    blk = pl.BlockSpec((t, g * LANES), lambda p, i: (i, p))
    col_blk = pl.BlockSpec((s, g * LANES), lambda p, i: (0, p), pipeline_mode=once)
    sds = jax.ShapeDtypeStruct((s, SB_WIDTH), BF16)
    return pl.pallas_call(
        body, name="sb_bwd", grid=grid,
        in_specs=[q_spec, k_spec, v_spec, blk, blk] + c_in_specs,
        out_specs=[blk, col_blk, col_blk] + c_out_specs,
        out_shape=[sds, sds, sds] + c_out_shape,
        scratch_shapes=c_scratch + [pltpu.VMEM((2 * g, t, 1), F32), pltpu.VMEM((2 * g, t, 1), F32),
                                    pltpu.VMEM((t, g * LANES), F32), pltpu.VMEM((s, g * LANES), F32),
                                    pltpu.VMEM((s, g * LANES), F32)],
        compiler_params=_cparams(),
    )(qkv, qkv, qkv, o, do, *c_ins)


def _ret_log_gamma():
    lg = np.log1p(-np.exp2(-5.0 - np.arange(RET_HEADS, dtype=np.float32))).astype(np.float32)
    return jnp.asarray(np.broadcast_to(lg[:, None, None], (RET_HEADS, 8, LANES)).copy())


RET_SCRATCH = [pltpu.VMEM((RET_HEADS, RET_QK, RET_V), F32),
               pltpu.VMEM((RET_HEADS, RET_BLOCK, RET_BLOCK), F32),
               pltpu.VMEM((RET_HEADS, RET_BLOCK, 1), F32),
               pltpu.VMEM((RET_HEADS, RET_BLOCK, 1), F32)]


def _ret_begin(n, lg_ref, state, within, q_dec, k_dec):
    @pl.when(n == 0)
    def _():
        c = RET_BLOCK
        state[...] = jnp.zeros_like(state)
        row = lax.broadcasted_iota(jnp.int32, (c, c), 0)
        col = lax.broadcasted_iota(jnp.int32, (c, c), 1)
        rel = jnp.maximum(row - col, 0).astype(F32)
        idx = lax.broadcasted_iota(jnp.int32, (c, 1), 0).astype(F32)
        for h in range(RET_HEADS):
            lg = lg_ref[h, 0:1, 0:1]
            within[h] = jnp.where(row >= col, jnp.exp(lg * rel), 0.0)
            q_dec[h] = jnp.exp(lg * (idx + 1.0))
            k_dec[h] = jnp.exp(lg * (c - 1.0 - idx))


def _chunk_decay(lg_ref, h):
    return jnp.exp(lg_ref[h, 0:1, 0:1] * float(RET_BLOCK))


def _ret_heads(x, width):
    return [x[:, h * width:(h + 1) * width] for h in range(RET_HEADS)]


def _ret_specs(s, reverse=False):
    c = RET_BLOCK
    nc = s // c
    pos = (lambda n: nc - 1 - n) if reverse else (lambda n: n)
    q_spec = pl.BlockSpec((c, RET_QK_WIDTH), lambda n: (pos(n), 0))
    k_spec = pl.BlockSpec((c, RET_QK_WIDTH), lambda n: (pos(n), 1))
    v_spec = pl.BlockSpec((c, RET_V_WIDTH), lambda n: (pos(n), 0))
    lg_spec = pl.BlockSpec((RET_HEADS, 8, LANES), lambda n: (0, 0, 0))
    rope_spec = pl.BlockSpec((c, RET_QK), lambda n: (pos(n), 0))
    return nc, q_spec, k_spec, v_spec, lg_spec, rope_spec


def _ret_fwd(rqk, rvg, s):
    nc, q_spec, k_spec, v_spec, lg_spec, _ = _ret_specs(s)
    g_spec = pl.BlockSpec((RET_BLOCK, RET_V_WIDTH), lambda n: (n, 1))
    heads = range(RET_HEADS)

    def body(q_ref, k_ref, v_ref, g_ref, lg_ref, r_ref, y_ref, state, within, q_dec, k_dec):
        n = pl.program_id(0)
        _ret_begin(n, lg_ref, state, within, q_dec, k_dec)
        q, k = _ret_heads(q_ref[...], RET_QK), _ret_heads(k_ref[...], RET_QK)
        v, g = _ret_heads(v_ref[...], RET_V), _ret_heads(g_ref[...], RET_V)
        scores = [_dot(q[h].astype(BF16), k[h].astype(BF16), 1, 1) * within[h] for h in heads]
        cross = [_dot((q[h] * q_dec[h]).astype(BF16), state[h].astype(BF16), 1, 0) for h in heads]
        out = [_dot(scores[h].astype(BF16), v[h], 1, 0) + cross[h] for h in heads]
        grown = [_dot((k[h] * k_dec[h]).astype(BF16), v[h], 0, 0) for h in heads]
        for h in heads:
            sl = slice(h * RET_V, (h + 1) * RET_V)
            r_ref[:, sl] = out[h]
            xhat, _ = _norm(out[h])
            gh = g[h].astype(F32)
            y_ref[:, sl] = (gh * _sigmoid(gh) * xhat).astype(y_ref.dtype)
            state[h] = state[h] * _chunk_decay(lg_ref, h) + grown[h]

    return pl.pallas_call(
        body, name="ret_fwd", grid=(nc,),
        in_specs=[q_spec, k_spec, v_spec, g_spec, lg_spec],
        out_specs=[v_spec, v_spec],
        out_shape=[jax.ShapeDtypeStruct((s, RET_V_WIDTH), F32), jax.ShapeDtypeStruct((s, RET_V_WIDTH), BF16)],
        scratch_shapes=RET_SCRATCH,
        compiler_params=_cparams(),
    )(rqk, rqk, rvg, rvg, _ret_log_gamma())


def _rope_bwd(d, cos, sin):
    return d * cos + _swap_halves(d * sin)


def _ret_bwd_q(rqk, rv, d_out, cos2, sin2, s):
    nc, q_spec, k_spec, v_spec, lg_spec, rope_spec = _ret_specs(s)
    heads = range(RET_HEADS)

    def body(k_ref, v_ref, d_ref, lg_ref, cos_ref, sin_ref, dq_ref, state, within, q_dec, k_dec):
        n = pl.program_id(0)
        _ret_begin(n, lg_ref, state, within, q_dec, k_dec)
        k = _ret_heads(k_ref[...], RET_QK)
        v, d = _ret_heads(v_ref[...], RET_V), _ret_heads(d_ref[...], RET_V)
        cos, sin = cos_ref[...], sin_ref[...]
        d_scores = [_dot(d[h], v[h], 1, 1) * within[h] for h in heads]
        cross = [q_dec[h] * _dot(d[h], state[h].astype(BF16), 1, 1) for h in heads]
        dq = [_dot(d_scores[h].astype(BF16), k[h].astype(BF16), 1, 0) + cross[h] for h in heads]
        grown = [_dot((k[h] * k_dec[h]).astype(BF16), v[h], 0, 0) for h in heads]
        for h in heads:
            sl = slice(h * RET_QK, (h + 1) * RET_QK)
            dq_ref[:, sl] = (_rope_bwd(dq[h], cos, sin) * RET_SCALE).astype(dq_ref.dtype)
            state[h] = state[h] * _chunk_decay(lg_ref, h) + grown[h]

    return pl.pallas_call(
        body, name="ret_bwd_q", grid=(nc,),
        in_specs=[k_spec, v_spec, v_spec, lg_spec, rope_spec, rope_spec],
        out_specs=q_spec,
        out_shape=jax.ShapeDtypeStruct((s, RET_QK_WIDTH), BF16),
        scratch_shapes=RET_SCRATCH,
        compiler_params=_cparams(),
    )(rqk, rv, d_out, _ret_log_gamma(), cos2, sin2)


def _ret_bwd_kv(rqk, rv, d_out, cos2, sin2, s):
    nc, q_spec, k_spec, v_spec, lg_spec, rope_spec = _ret_specs(s, reverse=True)
    heads = range(RET_HEADS)

    def body(q_ref, k_ref, v_ref, d_ref, lg_ref, cos_ref, sin_ref, dk_ref, dv_ref, state, within, q_dec, k_dec):
        n = pl.program_id(0)
        _ret_begin(n, lg_ref, state, within, q_dec, k_dec)
        q, k = _ret_heads(q_ref[...], RET_QK), _ret_heads(k_ref[...], RET_QK)
        v, d = _ret_heads(v_ref[...], RET_V), _ret_heads(d_ref[...], RET_V)
        cos, sin = cos_ref[...], sin_ref[...]
        qb, kb = [q[h].astype(BF16) for h in heads], [k[h].astype(BF16) for h in heads]
        st = [state[h].astype(BF16) for h in heads]
        scores = [_dot(qb[h], kb[h], 1, 1) * within[h] for h in heads]
        d_scores = [_dot(d[h], v[h], 1, 1) * within[h] for h in heads]
        dk = [_dot(d_scores[h].astype(BF16), qb[h], 0, 0) + k_dec[h] * _dot(v[h], st[h], 1, 1) for h in heads]
        dv = [_dot(scores[h].astype(BF16), d[h], 0, 0) + k_dec[h] * _dot(kb[h], st[h], 1, 0) for h in heads]
        grown = [_dot((q[h] * q_dec[h]).astype(BF16), d[h], 0, 0) for h in heads]
        for h in heads:
            dk_ref[:, h * RET_QK:(h + 1) * RET_QK] = _rope_bwd(dk[h], cos, sin).astype(dk_ref.dtype)
            dv_ref[:, h * RET_V:(h + 1) * RET_V] = dv[h].astype(dv_ref.dtype)
            state[h] = state[h] * _chunk_decay(lg_ref, h) + grown[h]

    return pl.pallas_call(
        body, name="ret_bwd_kv", grid=(nc,),
        in_specs=[q_spec, k_spec, v_spec, v_spec, lg_spec, rope_spec, rope_spec],
        out_specs=[q_spec, v_spec],
        out_shape=[jax.ShapeDtypeStruct((s, RET_QK_WIDTH), BF16), jax.ShapeDtypeStruct((s, RET_V_WIDTH), BF16)],
        scratch_shapes=RET_SCRATCH,
        compiler_params=_cparams(),
    )(rqk, rqk, rv, d_out, _ret_log_gamma(), cos2, sin2)


def _xattn_probs(scores):
    sc = scores - jnp.max(scores, axis=-1, keepdims=True)
    p = jnp.exp(sc)
    return p / jnp.sum(p, axis=-1, keepdims=True)


def _xattn_heads(q_ref, kv_ref):
    sls = [slice(h * MEM_DIM, (h + 1) * MEM_DIM) for h in range(MEM_HEADS)]
    q = [q_ref[:, sl] for sl in sls]
    k = [kv_ref[:, sl] for sl in sls]
    v = [kv_ref[:, D_MODEL + h * MEM_DIM:D_MODEL + (h + 1) * MEM_DIM] for h in range(MEM_HEADS)]
    return sls, q, k, v


def _xattn_fwd(qm, kv, s):
    tq = XATTN_ROWS
    heads = range(MEM_HEADS)

    def body(q_ref, kv_ref, o_ref):
        sls, q, k, v = _xattn_heads(q_ref, kv_ref)
        scores = [_dot(q[h], k[h], 1, 1) for h in heads]
        p = [_xattn_probs(scores[h]).astype(BF16) for h in heads]
        out = [_dot(p[h], v[h], 1, 0) for h in heads]
        for h in heads:
            o_ref[:, sls[h]] = out[h].astype(o_ref.dtype)

    return pl.pallas_call(
        body, name="xattn_fwd", grid=(s // tq,),
        in_specs=[pl.BlockSpec((tq, D_MODEL), lambda i: (i, 0)),
                  pl.BlockSpec((MEM_LEN, 2 * D_MODEL), lambda i: (0, 0))],
        out_specs=pl.BlockSpec((tq, D_MODEL), lambda i: (i, 0)),
        out_shape=jax.ShapeDtypeStruct((s, D_MODEL), BF16),
        compiler_params=_cparams(),
    )(qm, kv)


def _xattn_bwd(qm, kv, do, s):
    tq = XATTN_ROWS

    def body(q_ref, kv_ref, do_ref, dq_ref, dkv_ref):
        i = pl.program_id(0)

        @pl.when(i == 0)
        def _():
            dkv_ref[...] = jnp.zeros_like(dkv_ref)

        heads = range(MEM_HEADS)
        sls, q, k, v = _xattn_heads(q_ref, kv_ref)
        d = [do_ref[:, sl] for sl in sls]
        scores = [_dot(q[h], k[h], 1, 1) for h in heads]
        dp = [_dot(d[h], v[h], 1, 1) for h in heads]
        p = [_xattn_probs(scores[h]) for h in heads]
        ds = [(p[h] * (dp[h] - jnp.sum(p[h] * dp[h], axis=-1, keepdims=True))).astype(BF16) for h in heads]
        dq = [_dot(ds[h], k[h], 1, 0) for h in heads]
        dk = [_dot(ds[h], q[h], 0, 0) for h in heads]
        dv = [_dot(p[h].astype(BF16), d[h], 0, 0) for h in heads]
        for h in heads:
            dq_ref[:, sls[h]] = (dq[h] * MEM_SCALE).astype(dq_ref.dtype)
            dkv_ref[:, sls[h]] += dk[h]
            dkv_ref[:, D_MODEL + h * MEM_DIM:D_MODEL + (h + 1) * MEM_DIM] += dv[h]

    row_blk = pl.BlockSpec((tq, D_MODEL), lambda i: (i, 0))
    kv_blk = pl.BlockSpec((MEM_LEN, 2 * D_MODEL), lambda i: (0, 0))
    return pl.pallas_call(
        body, name="xattn_bwd", grid=(s // tq,),
        in_specs=[row_blk, kv_blk, row_blk],
        out_specs=[row_blk, kv_blk],
        out_shape=[jax.ShapeDtypeStruct((s, D_MODEL), BF16), jax.ShapeDtypeStruct((MEM_LEN, 2 * D_MODEL), F32)],
        compiler_params=_cparams(),
    )(qm, kv, do)


def _place():
    x, y, c = lax.axis_index("x"), lax.axis_index("y"), lax.axis_index("c")
    others = [(1 - x, y), (x, 1 - y), (1 - x, 1 - y)]
    return x, y, c, others


def _slab(ref, axis, chip, size):
    start = pl.multiple_of(chip * size, LANES if axis == 1 else 16)
    if axis == 0:
        return ref.at[pl.ds(start, size), :]
    return ref.at[:, pl.ds(start, size)]


class _CommPlan:
    def __init__(self, ins, out_shape, scratch, start, finish):
        self.ins, self.out_shape, self.scratch, self.start, self.finish = ins, out_shape, scratch, start, finish

    @property
    def specs(self):
        any_spec = pl.BlockSpec(memory_space=pl.ANY)
        return [any_spec] * len(self.ins), [any_spec] * len(self.out_shape)

    def split(self, refs):
        n_in, n_out = len(self.ins), len(self.out_shape)
        return refs[:n_in], refs[n_in:n_in + n_out], refs[n_in + n_out:]


def _gather_plan(names, shards):
    spec = {name: (shape, axis) for name, shape, axis in BIG}
    nw = len(names)

    def shard_half(ref, c):
        rows = ref.shape[0] // 2
        return ref.at[pl.ds(pl.multiple_of(c * rows, 16), rows), :]

    def region(ref, w, chip, c):
        shape, axis = spec[names[w]]
        size = shape[axis] // N_CHIPS
        if axis == 0:
            rows = size // 2
            return ref.at[pl.ds(pl.multiple_of(chip * size + c * rows, 16), rows), :]
        rows = shape[0] // 2
        return ref.at[pl.ds(pl.multiple_of(c * rows, 16), rows), pl.ds(pl.multiple_of(chip * size, LANES), size)]

    def ops(shard, full, sems):
        ici_send, ici_recv, d2d_send, d2d_recv, local_sems = sems
        x, y, c, others = _place()
        mine, sibling = 2 * x + y, (x, y, 1 - c)
        local, over_ici, arrived, passed_on, from_sibling = [], [], [], [], []
        for w in range(nw):
            shape, axis = spec[names[w]]
            local.append(pltpu.make_async_copy(shard[w], _slab(full[w], axis, mine, shape[axis] // N_CHIPS),
                                               local_sems.at[w]))
            for t, (qx, qy) in enumerate(others):
                n, theirs = 3 * w + t, 2 * qx + qy
                over_ici.append(pltpu.make_async_remote_copy(
                    src_ref=shard_half(shard[w], c), dst_ref=region(full[w], w, mine, c),
                    send_sem=ici_send.at[n], recv_sem=ici_recv.at[n], device_id=(qx, qy, c), device_id_type=MESH))
                arrived.append(pltpu.make_async_remote_copy(
                    src_ref=shard_half(shard[w], c), dst_ref=region(full[w], w, theirs, c),
                    send_sem=ici_send.at[n], recv_sem=ici_recv.at[n], device_id=(qx, qy, c), device_id_type=MESH))
                passed_on.append(pltpu.make_async_remote_copy(
                    src_ref=region(full[w], w, theirs, c), dst_ref=region(full[w], w, theirs, c),
                    send_sem=d2d_send.at[n], recv_sem=d2d_recv.at[n], device_id=sibling, device_id_type=MESH))
                from_sibling.append(pltpu.make_async_remote_copy(
                    src_ref=region(full[w], w, theirs, c), dst_ref=region(full[w], w, theirs, 1 - c),
                    send_sem=d2d_send.at[n], recv_sem=d2d_recv.at[n], device_id=sibling, device_id_type=MESH))
        return local, over_ici, arrived, passed_on, from_sibling

    def start(shard, full, sems):
        local, over_ici, _, _, _ = ops(shard, full, sems)
        for cp in local + over_ici:
            cp.start()

    def finish(shard, full, sems):
        local, over_ici, arrived, passed_on, from_sibling = ops(shard, full, sems)
        for got, onward in zip(arrived, passed_on, strict=True):
            got.wait_recv()
            onward.start()
        for got in from_sibling:
            got.wait_recv()
        for cp in over_ici + passed_on:
            cp.wait_send()
        for cp in local:
            cp.wait()

    dma = pltpu.SemaphoreType.DMA
    return _CommPlan(
        ins=[shards[name] for name in names],
        out_shape=[jax.ShapeDtypeStruct(spec[name][0], BF16) for name in names],
        scratch=[dma((3 * nw,)), dma((3 * nw,)), dma((3 * nw,)), dma((3 * nw,)), dma((nw,))],
        start=start, finish=finish)


def _shard_shape(shape, axis):
    return tuple(d // N_CHIPS if a == axis else d for a, d in enumerate(shape))


def _exchange_plan(names, grads):
    spec = {name: (shape, axis) for name, shape, axis in BIG}
    nw = len(names)

    def ops(grad, stack, sems):
        send_sems, recv_sems, local_sems = sems
        x, y, c, others = _place()
        mine = 2 * x + y
        me, sibling = (x, y, c), (x, y, 1 - c)

        def dev(px, py, pc):
            return 4 * px + 2 * py + pc

        def copy(w, n, src, slot, to):
            return pltpu.make_async_remote_copy(
                src_ref=src, dst_ref=stack[w].at[slot], send_sem=send_sems.at[7 * w + n],
                recv_sem=recv_sems.at[7 * w + n], device_id=to, device_id_type=MESH)

        local, first, arrived, passed_on, from_sibling = [], [], [], [], []
        for w in range(nw):
            shape, axis = spec[names[w]]
            size = shape[axis] // N_CHIPS
            own = _slab(grad[w], axis, mine, size)
            local.append(pltpu.make_async_copy(own, stack[w].at[dev(*me)], local_sems.at[w]))
            first.append(copy(w, 0, own, dev(*me), sibling))
            from_sibling.append(copy(w, 0, own, dev(*sibling), me))
            for t, (qx, qy) in enumerate(others):
                got = stack[w].at[dev(qx, qy, c)]
                first.append(copy(w, 1 + t, _slab(grad[w], axis, 2 * qx + qy, size), dev(*me), (qx, qy, c)))
                arrived.append(copy(w, 1 + t, got, dev(qx, qy, c), me))
                passed_on.append(copy(w, 4 + t, got, dev(qx, qy, c), sibling))
                from_sibling.append(copy(w, 4 + t, got, dev(qx, qy, 1 - c), me))
        return local, first, arrived, passed_on, from_sibling

    def start(grad, stack, sems):
        local, first, _, _, _ = ops(grad, stack, sems)
        for cp in local + first:
            cp.start()

    def finish(grad, stack, sems):
        local, first, arrived, passed_on, from_sibling = ops(grad, stack, sems)
        for got, onward in zip(arrived, passed_on, strict=True):
            got.wait_recv()
            onward.start()
        for got in from_sibling:
            got.wait_recv()
        for cp in first + passed_on:
            cp.wait_send()
        for cp in local:
            cp.wait()

    dma = pltpu.SemaphoreType.DMA
    return _CommPlan(
        ins=[grads[name] for name in names],
        out_shape=[jax.ShapeDtypeStruct((N_DEV,) + _shard_shape(*spec[name]), BF16) for name in names],
        scratch=[dma((7 * nw,)), dma((7 * nw,)), dma((nw,))],
        start=start, finish=finish)


def _adamw(w, g, m, v):
    m = ADAM_B1 * m + (1.0 - ADAM_B1) * g
    v = ADAM_B2 * v + (1.0 - ADAM_B2) * (g * g)
    m_hat = m / (1.0 - ADAM_B1 ** ADAM_STEP)
    v_hat = v / (1.0 - ADAM_B2 ** ADAM_STEP)
    delta = -ADAM_LR * (m_hat / (jnp.sqrt(v_hat) + ADAM_EPS) + ADAM_WD * w)
    return delta, m, v


def _reduce_adamw(name, stack, w, m, v):
    rows, cols = w.shape
    tr = next(t for t in (256, 128, 64) if rows % t == 0)

    def body(s_ref, w_ref, m_ref, v_ref, g_ref, d_ref, nm_ref, nv_ref):
        g = s_ref[0].astype(F32)
        for d in range(1, N_DEV):
            g = g + s_ref[d].astype(F32)
        g_ref[...] = g
        d_ref[...], nm_ref[...], nv_ref[...] = _adamw(w_ref[...], g, m_ref[...], v_ref[...])

    blk = pl.BlockSpec((tr, cols), lambda i: (i, 0))
    return pl.pallas_call(
        body, name=name, grid=(rows // tr,),
        in_specs=[pl.BlockSpec((N_DEV, tr, cols), lambda i: (0, i, 0)), blk, blk, blk],
        out_specs=[blk] * 4, out_shape=[jax.ShapeDtypeStruct((rows, cols), F32)] * 4,
        compiler_params=_cparams(),
    )(stack, w, m, v)


def _small_step(pack, w, m, v):
    def body(p_ref, w_ref, m_ref, v_ref, g_ref, d_ref, nm_ref, nv_ref, loss_ref, all_ref, send_sems, recv_sems):
        x, y, c, _ = _place()
        me = 4 * x + 2 * y + c
        all_ref[me] = p_ref[...]
        sent = []
        for n in range(1, N_DEV):
            peer = me ^ n
            cp = pltpu.make_async_remote_copy(
                src_ref=p_ref, dst_ref=all_ref.at[me], send_sem=send_sems.at[n - 1], recv_sem=recv_sems.at[n - 1],
                device_id=(peer // 4, (peer // 2) % 2, peer % 2), device_id_type=MESH)
            cp.start()
            sent.append(cp)
        for n in range(1, N_DEV):
            peer = me ^ n
            pltpu.make_async_remote_copy(
                src_ref=p_ref, dst_ref=all_ref.at[peer], send_sem=send_sems.at[n - 1], recv_sem=recv_sems.at[n - 1],
                device_id=(peer // 4, (peer // 2) % 2, peer % 2), device_id_type=MESH).wait_recv()
        for cp in sent:
            cp.wait_send()
        tot = all_ref[0]
        for d in range(1, N_DEV):
            tot = tot + all_ref[d]
        g = tot[:SMALL_ROWS]
        g_ref[...] = g
        d_ref[...], nm_ref[...], nv_ref[...] = _adamw(w_ref[...], g, m_ref[...], v_ref[...])
        loss_ref[...] = jnp.sum(jnp.sum(tot[SMALL_ROWS:], axis=1, keepdims=True), axis=0, keepdims=True)

    vm = pl.BlockSpec(memory_space=pltpu.VMEM)
    small = jax.ShapeDtypeStruct((SMALL_ROWS, LANES), F32)
    return pl.pallas_call(
        body, name="small_step",
        in_specs=[vm] * 4, out_specs=[vm] * 5,
        out_shape=[small] * 4 + [jax.ShapeDtypeStruct((1, 1), F32)],
        scratch_shapes=[pltpu.VMEM((N_DEV, PACK_ROWS, LANES), F32),
                        pltpu.SemaphoreType.DMA((N_DEV - 1,)), pltpu.SemaphoreType.DMA((N_DEV - 1,))],
    )(pack, w, m, v)


LATER_WEIGHTS = tuple(name for name, _, _ in BIG if name != "w_in")


def _layer_step(x, mem, tgt, shards, vec):
    s = x.shape[0]
    d = D_MODEL
    tm = min(ROW_TILE, s)
    tl = min(WIDE_TILE, s)
    xb, cos2, sin2, w_in = _prep(x, _gather_plan(("w_in",), shards))
    bf = lambda w: ((s, w), BF16)
    f32 = lambda w: ((s, w), F32)

    w_sb, w_rqk = w_in[:, :OFF_RET_Q], w_in[:, OFF_RET_Q:OFF_RET_V]
    w_rvg, w_gate = w_in[:, OFF_RET_V:OFF_GATE], w_in[:, OFF_GATE:]
    q_scale = lambda width, q_width, scale: jnp.concatenate(
        [jnp.full((1, q_width), scale, F32), jnp.ones((1, width - q_width), F32)], axis=1)
    n_groups = 3 * SB_WIDTH // LANES

    def sb_epi(acc, t, i, j):
        scaled = acc * t[0]
        return [jnp.stack([scaled[:, g * LANES:(g + 1) * LANES] for g in range(n_groups)])], []

    (sb_qkv,) = _mm(
        "in_sb", xb, w_sb, s, 3 * SB_WIDTH, d, tm=tl, tn=3 * SB_WIDTH, tk=d, epi=sb_epi,
        ins=[(q_scale(3 * SB_WIDTH, SB_WIDTH, SB_SCALE), *_rowvec(3 * SB_WIDTH))],
        outs=[((n_groups, s, LANES), BF16, (n_groups, tl, LANES), lambda i, j: (0, i, 0))])

    def rope_epi(acc, t, i, j):
        cos, sin, scale = t
        parts = []
        for g in range(acc.shape[1] // RET_QK):
            xg = acc[:, g * RET_QK:(g + 1) * RET_QK]
            parts.append(xg * cos + _swap_halves(xg) * sin)
        return [jnp.concatenate(parts, axis=1) * scale], []

    rope_in = ((tl, RET_QK), lambda i, j: (i, 0))
    (rqk,) = _mm("in_rqk", xb, w_rqk, s, 2 * RET_QK_WIDTH, d, tm=tl, tn=2 * RET_QK_WIDTH, tk=d, epi=rope_epi,
                 chunk=MXU_COLS,
                 ins=[(cos2, *rope_in), (sin2, *rope_in),
                      (q_scale(2 * RET_QK_WIDTH, RET_QK_WIDTH, RET_SCALE), *_rowvec(2 * RET_QK_WIDTH))],
                 outs=[(*f32(2 * RET_QK_WIDTH), *_tile(tl, 2 * RET_QK_WIDTH))])
    (rvg,) = _mm("in_rvg", xb, w_rvg, s, 2 * RET_V_WIDTH, d, tm=tl, tn=2 * RET_V_WIDTH, tk=d, chunk=MXU_COLS,
                 epi=_plain, outs=[(*bf(2 * RET_V_WIDTH), *_tile(tl, 2 * RET_V_WIDTH))])
    (gates,) = _mm("in_gate", xb, w_gate, s, 2 * d, d, tm=tl, tn=2 * d, tk=d, chunk=MXU_COLS,
                   epi=lambda acc, t, i, j: ([_sigmoid(acc + t[0])], []),
                   ins=[(vec["b_gate"], *_rowvec(2 * d))], outs=[(*bf(2 * d), *_tile(tl, 2 * d))])

    sb_out, sb_out_f32, *gathered = _sb_fwd(sb_qkv, s, comm=_gather_plan(LATER_WEIGHTS, shards))
    wt = dict(zip(LATER_WEIGHTS, gathered, strict=True))
    ret, gated = _ret_fwd(rqk, rvg, s)
    (y_sb,) = _mm("sb_o", sb_out, wt["w_sb_o"], s, d, SB_WIDTH, tm=tl, tn=d, tk=SB_WIDTH, epi=_plain,
                  outs=[(*bf(d), *_tile(tl, d))])
    y_ret, mixin = _mm(
        "ret_o", gated, wt["w_ret_o"], s, d, RET_V_WIDTH, tm=tl, tn=d, tk=RET_V_WIDTH, chunk=MXU_COLS,
        epi=lambda acc, t, i, j: ([acc, t[0].astype(F32) * t[2].astype(F32) + t[1].astype(F32) * acc], []),
        ins=[(gates, *_tile(tl, d)), (gates, *_tile(tl, d, 1)), (y_sb, *_tile(tl, d))],
        outs=[(*bf(d), *_tile(tl, d)), (*bf(d), *_tile(tl, d))])

    def ln_epi(acc, t, i, j):
        *res, g, b = t
        prev = res[0] if len(res) == 1 else res[0] * res[1] + res[2]
        xhat, rstd = _norm(DN_ALPHA * prev + acc)
        return [xhat * g + b, xhat, rstd], []

    full = _tile(tm, d)
    col1 = ((tm, 1), lambda i, j: (i, 0))
    vec_in = lambda name: (vec[name], *_rowvec(d))
    ln_outs = [(*bf(d), *full), (*f32(d), *full), ((s, 1), F32, *col1)]
    x1b, xhat1, rstd1 = _mm(
        "mix_o", mixin, wt["w_mix_o"], s, d, d, tm=tm, tn=d, tk=d, epi=ln_epi,
        ins=[(x, *full), vec_in("ln1_g"), vec_in("ln1_b")], outs=ln_outs)

    (qm,) = _mm("mem_q", x1b, wt["w_mem_q"], s, d, d, tm=tl, tn=d, tk=d,
                epi=lambda acc, t, i, j: ([acc * MEM_SCALE], []), outs=[(*bf(d), *_tile(tl, d))])
    (kv,) = _mm("mem_kv", mem, wt["w_mem_kv"], MEM_LEN, 2 * d, d, tm=MEM_LEN, tn=d, tk=d, epi=_plain,
                outs=[((MEM_LEN, 2 * d), BF16, *_tile(MEM_LEN, d))])
    att = _xattn_fwd(qm, kv, s)
    x2b, xhat2, rstd2 = _mm(
        "mem_o", att, wt["w_mem_o"], s, d, d, tm=tm, tn=d, tk=d, epi=ln_epi,
        ins=[(xhat1, *full), vec_in("ln1_g"), vec_in("ln1_b"), vec_in("ln2_g"), vec_in("ln2_b")], outs=ln_outs)

    fh = FFN_HIDDEN
    tf = fh // 2
    (f1,) = _mm("ffn_in1", x2b, wt["w_ffn_in"], s, fh, d, tm=tl, tn=tf, tk=d, epi=_plain, j_outer=True,
                outs=[(*bf(fh), *_tile(tl, tf))])

    def swiglu_epi(acc, t, i, j):
        a = t[0].astype(F32)
        return [acc, a * _sigmoid(a) * acc], []

    f2, act = _mm(
        "ffn_in2", x2b, wt["w_ffn_in"], s, fh, d, tm=tm, tn=fh, tk=d, b_off=(0, 1), epi=swiglu_epi, chunk=MXU_COLS,
        ins=[(f1, *_tile(tm, fh))], outs=[(*bf(fh), *_tile(tm, fh)), (*bf(fh), *_tile(tm, fh))])

    def head_epi(acc, t, i, j):
        prev_hat, prev_g, prev_b, g, b, target = t
        xhat, rstd = _norm(DN_ALPHA * (prev_hat * prev_g + prev_b) + acc)
        err = xhat * g + b - target
        dy = err * (1.0 / d)
        du = _norm_bwd(dy * g, xhat, rstd)
        return [du], [_colsum(dy * xhat), _colsum(dy), _colsum(err * err) * (0.5 / d)]

    vec_acc = ((1, d), F32)
    du3b, dg3, db3, loss_cols = _mm(
        "ffn_out", act, wt["w_ffn_out"], s, d, fh, tm=tm, tn=d, tk=fh, epi=head_epi,
        ins=[(xhat2, *full), vec_in("ln2_g"), vec_in("ln2_b"), vec_in("ln3_g"), vec_in("ln3_b"), (tgt, *full)],
        outs=[(*bf(d), *full)], accs=[vec_acc] * 3)

    grads = {}
    ts = min(SEQ_TILE, s)

    def wgrad(name, a, b, m, n, tm_, tn_, tk_=None):
        (g,) = _mm(name, a, b, m, n, a.shape[0], tm=tm_, tn=tn_, tk=tk_ or ts, ta=True, epi=_plain,
                   outs=[((m, n), BF16, *_tile(tm_, tn_))])
        return g

    def ffn_bwd_epi(acc, t, i, j):
        a, b = t[0].astype(F32), t[1].astype(F32)
        sg = _sigmoid(a)
        return [acc * b * (sg * (1.0 + a * (1.0 - sg))), acc * (a * sg)], []

    df1, df2 = _mm(
        "ffn_out_t", du3b, wt["w_ffn_out"], s, fh, d, tm=tm, tn=fh, tk=d, tb=True, epi=ffn_bwd_epi, chunk=MXU_COLS,
        ins=[(f1, *_tile(tm, fh)), (f2, *_tile(tm, fh))],
        outs=[(*bf(fh), *_tile(tm, fh)), (*bf(fh), *_tile(tm, fh))])
    grads["w_ffn_out"] = wgrad("g_ffn_out", act, du3b, fh, d, tf, d)
    grads["w_ffn_in"] = jnp.concatenate(
        [wgrad("g_ffn_in1", x2b, df1, d, fh, d, tf), wgrad("g_ffn_in2", x2b, df2, d, fh, d, tf)], axis=1)
    (dx2a,) = _mm("ffn_in1_t", df1, wt["w_ffn_in"], s, d, fh, tm=tm, tn=d, tk=fh, tb=True, epi=_plain,
                  outs=[(*f32(d), *full)])

    def ln_bwd(name, a, b, k, tk, b_off, more, scales, xhat, rstd, g):
        def epi(acc, t, i, j):
            *extra, xh, rs, gg = t
            dy = acc
            for e, sc in zip(extra, scales, strict=True):
                dy = dy + e.astype(F32) * sc
            return [_norm_bwd(dy * gg, xh, rs)], [_colsum(dy * xh), _colsum(dy)]

        return _mm(name, a, b, s, d, k, tm=tm, tn=d, tk=tk, tb=True, b_off=b_off, epi=epi,
                   ins=[(e, *full) for e in more] + [(xhat, *full), (rstd, *col1), (g, *_rowvec(d))],
                   outs=[(*bf(d), *full)], accs=[vec_acc] * 2)

    du2b, dg2, db2 = ln_bwd("ffn_in2_t", df2, wt["w_ffn_in"], fh, fh, (0, 1), [dx2a, du3b], [1.0, DN_ALPHA],
                            xhat2, rstd2, vec["ln2_g"])

    (datt,) = _mm("mem_o_t", du2b, wt["w_mem_o"], s, d, d, tm=tl, tn=d, tk=d, tb=True, epi=_plain,
                  outs=[(*bf(d), *_tile(tl, d))])
    grads["w_mem_o"] = wgrad("g_mem_o", att, du2b, d, d, d, d)
    dqm, dkv = _xattn_bwd(qm, kv, datt, s)
    grads["w_mem_q"] = wgrad("g_mem_q", x1b, dqm, d, d, d, d)
    grads["w_mem_kv"] = wgrad("g_mem_kv", mem, dkv, d, 2 * d, d, d, MEM_LEN)
    du1b, dg1, db1 = ln_bwd("mem_q_t", dqm, wt["w_mem_q"], d, d, (0, 0), [du2b], [DN_ALPHA],
                            xhat1, rstd1, vec["ln1_g"])

    def merge_bwd_epi(acc, t, i, j):
        g0, g1, ysb, yret = (v.astype(F32) for v in t)
        dgate0 = acc * ysb * (g0 * (1.0 - g0))
        dgate1 = acc * yret * (g1 * (1.0 - g1))
        return [dgate0, dgate1, acc * g0, acc * g1], [_colsum(dgate0), _colsum(dgate1)]

    dgate0, dgate1, dy_sb, dy_ret, dbg0, dbg1 = _mm(
        "mix_o_t", du1b, wt["w_mix_o"], s, d, d, tm=tm, tn=d, tk=d, tb=True, epi=merge_bwd_epi,
        ins=[(gates, *full), (gates, *_tile(tm, d, 1)), (y_sb, *full), (y_ret, *full)],
        outs=[(*bf(d), *full)] * 4, accs=[vec_acc] * 2)
    grads["w_mix_o"] = wgrad("g_mix_o", mixin, du1b, d, d, d, d)
    grads["w_sb_o"] = wgrad("g_sb_o", sb_out, dy_sb, SB_WIDTH, d, SB_WIDTH, d)
    grads["w_ret_o"] = wgrad("g_ret_o", gated, dy_ret, RET_V_WIDTH, d, RET_V_WIDTH, d)
    (dsb_out,) = _mm("sb_o_t", dy_sb, wt["w_sb_o"], s, SB_WIDTH, d, tm=tl, tn=SB_WIDTH, tk=d, tb=True, epi=_plain,
                     outs=[(*bf(SB_WIDTH), *_tile(tl, SB_WIDTH))])

    def gate_norm_bwd_epi(acc, t, i, j):
        r, g = t[0], t[1].astype(F32)
        drg, dret = [], []
        for h in range(acc.shape[1] // RET_V):
            sl = slice(h * RET_V, (h + 1) * RET_V)
            xhat, rstd = _norm(r[:, sl])
            gg, dd = g[:, sl], acc[:, sl]
            sg = _sigmoid(gg)
            drg.append(dd * xhat * (sg * (1.0 + gg * (1.0 - sg))))
            dret.append(_norm_bwd(dd * (gg * sg), xhat, rstd))
        return [jnp.concatenate(drg, axis=1), jnp.concatenate(dret, axis=1)], []

    drg, dret = _mm(
        "ret_o_t", dy_ret, wt["w_ret_o"], s, RET_V_WIDTH, d, tm=tm, tn=d, tk=d, tb=True, epi=gate_norm_bwd_epi,
        chunk=MXU_COLS,
        ins=[(ret, *full), (rvg, *_tile(tm, d, 1))],
        outs=[(*bf(RET_V_WIDTH), *full)] * 2)

    drq = _ret_bwd_q(rqk, rvg, dret, cos2, sin2, s)
    drk, drv = _ret_bwd_kv(rqk, rvg, dret, cos2, sin2, s)
    dsq, dsk, dsv, *stacked = _sb_bwd(sb_qkv, sb_out_f32, dsb_out, s, comm=_exchange_plan(LATER_WEIGHTS, grads))
    stacks = dict(zip(LATER_WEIGHTS, stacked, strict=True))

    dh = jnp.concatenate([dsq, dsk, dsv, drq, drk, drv, drg, dgate0, dgate1], axis=1)
    grads["w_in"] = wgrad("g_in", xb, dh, d, IN_WIDTH, d, IN_WIDTH // N_CHIPS)
    grad_x, stacks["w_in"] = _mm(
        "in_t", dh, w_in, s, d, IN_WIDTH, tm=tl, tn=d, tk=IN_WIDTH // N_CHIPS, tb=True,
        epi=lambda acc, t, i, j: ([acc + DN_ALPHA * t[0].astype(F32)], []),
        ins=[(du1b, *_tile(tl, d))], outs=[(*f32(d), *_tile(tl, d))], comm=_exchange_plan(("w_in",), grads))

    small = {"b_gate": jnp.concatenate([dbg0, dbg1], axis=1), "ln1_g": dg1, "ln1_b": db1, "ln2_g": dg2,
             "ln2_b": db2, "ln3_g": dg3, "ln3_b": db3}
    return grad_x, stacks, small, loss_cols


def kernel(x, mem, w_in, b_gate, w_sb_o, w_ret_o, w_mix_o, ln1_g, ln1_b, w_mem_q, w_mem_kv, w_mem_o, ln2_g, ln2_b, w_ffn_in, w_ffn_out, ln3_g, ln3_b, loss_target, m_w_in, m_b_gate, m_w_sb_o, m_w_ret_o, m_w_mix_o, m_ln1_g, m_ln1_b, m_w_mem_q, m_w_mem_kv, m_w_mem_o, m_ln2_g, m_ln2_b, m_w_ffn_in, m_w_ffn_out, m_ln3_g, m_ln3_b, v_w_in, v_b_gate, v_w_sb_o, v_w_ret_o, v_w_mix_o, v_ln1_g, v_ln1_b, v_w_mem_q, v_w_mem_kv, v_w_mem_o, v_ln2_g, v_ln2_b, v_w_ffn_in, v_w_ffn_out, v_ln3_g, v_ln3_b):
    given = dict(locals())
    s = x.shape[1]
    x2d = x.reshape(s, D_MODEL)
    tgt = loss_target.reshape(s, D_MODEL)
    mem2d = mem.reshape(MEM_LEN, D_MODEL)
    shard = {name: given[name].reshape(_shard_shape(shape, axis)) for name, shape, axis in BIG}
    vec = {name: given[name] for name in SMALL}

    shards_bf = {name: _cast_bf16("cast_" + name, shard[name]) for name, _, _ in BIG}

    grad_x, stacks, small, loss_cols = _layer_step(x2d, mem2d, tgt, shards_bf, vec)

    out = {}
    for name, shape, axis in BIG:
        stack = stacks[name]
        shp = given[name].shape
        res = _reduce_adamw("adamw_" + name, stack, shard[name], given["m_" + name].reshape(stack.shape[1:]),
                            given["v_" + name].reshape(stack.shape[1:]))
        out[name] = [r.reshape(shp) for r in res]

    pack = jnp.concatenate([small[name] for name in SMALL] + [loss_cols], axis=1).reshape(PACK_ROWS, LANES)
    cat = lambda pre: jnp.concatenate([given[pre + name] for name in SMALL], axis=1).reshape(SMALL_ROWS, LANES)
    *res, loss = _small_step(pack, cat(""), cat("m_"), cat("v_"))
    flat = [r.reshape(1, SMALL_LEN) for r in res]
    off = 0
    for name in SMALL:
        n = given[name].shape[1]
        out[name] = [r[:, off:off + n] for r in flat]
        off += n

    return (loss.reshape(()), grad_x.reshape(x.shape),
            *[out[name][0] for name in WEIGHT_ORDER], *[out[name][1] for name in WEIGHT_ORDER],
            *[out[name][2] for name in WEIGHT_ORDER], *[out[name][3] for name in WEIGHT_ORDER])
```

```python
import functools

import jax
import jax.numpy as jnp
import numpy as np
from jax import lax
from jax.experimental import pallas as pl
from jax.experimental.pallas import tpu as pltpu

F32, BF16 = jnp.float32, jnp.bfloat16
MESH = pl.DeviceIdType.MESH

D_MODEL = 1024
MEM_LEN = 256
SB_HEADS, SB_DIM, SB_WIDTH = 8, 64, 512
RET_HEADS, RET_QK, RET_V = 4, 128, 256
RET_QK_WIDTH, RET_V_WIDTH = 512, 1024
ROPE_BASE = 10000.0
MEM_HEADS, MEM_DIM = 4, 256
FFN_HIDDEN = 2816
IN_WIDTH = 6656
OFF_RET_Q, OFF_RET_V, OFF_RET_G, OFF_GATE = 1536, 2560, 3584, 4608
DN_ALPHA = 2.0 ** 0.25
LN_EPS = 1e-5
SB_SCALE = SB_DIM ** -0.5
SB_DEAD = -110.0
RET_SCALE = RET_QK ** -0.5
MEM_SCALE = MEM_DIM ** -0.5
ADAM_LR, ADAM_B1, ADAM_B2, ADAM_EPS, ADAM_WD, ADAM_STEP = 0.001, 0.9, 0.999, 1e-08, 0.01, 10

N_DEV, N_CHIPS = 8, 4

LANES = 128
MXU_COLS = 256
VMEM_LIMIT_BYTES = 52 * 2 ** 20
ROW_TILE = 512
WIDE_TILE = 1024
SEQ_TILE = 2048
SB_BLOCK = 256
RET_BLOCK = 256
XATTN_ROWS = 512

BIG = (
    ("w_in", (D_MODEL, IN_WIDTH), 1),
    ("w_sb_o", (SB_WIDTH, D_MODEL), 1),
    ("w_ret_o", (RET_V_WIDTH, D_MODEL), 0),
    ("w_mix_o", (D_MODEL, D_MODEL), 0),
    ("w_mem_q", (D_MODEL, D_MODEL), 0),
    ("w_mem_kv", (D_MODEL, 2 * D_MODEL), 1),
    ("w_mem_o", (D_MODEL, D_MODEL), 0),
    ("w_ffn_in", (D_MODEL, 2 * FFN_HIDDEN), 1),
    ("w_ffn_out", (FFN_HIDDEN, D_MODEL), 0),
)
SMALL = ("b_gate", "ln1_g", "ln1_b", "ln2_g", "ln2_b", "ln3_g", "ln3_b")
SMALL_LEN = 2 * D_MODEL + 6 * D_MODEL
SMALL_ROWS = SMALL_LEN // LANES
PACK_ROWS = SMALL_ROWS + D_MODEL // LANES
WEIGHT_ORDER = ("w_in", "b_gate", "w_sb_o", "w_ret_o", "w_mix_o", "ln1_g", "ln1_b", "w_mem_q", "w_mem_kv",
                "w_mem_o", "ln2_g", "ln2_b", "w_ffn_in", "w_ffn_out", "ln3_g", "ln3_b")


def _cparams():
    return pltpu.CompilerParams(vmem_limit_bytes=VMEM_LIMIT_BYTES)


def _dot(a, b, ca, cb):
    return lax.dot_general(a, b, (((ca,), (cb,)), ((), ())), preferred_element_type=F32)


def _sigmoid(x):
    return 1.0 / (1.0 + jnp.exp(-x))


def _mm(name, a, b, m, n, k, *, tm, tn, tk, epi, outs, ins=(), accs=(), ta=False, tb=False,
        a_off=(0, 0), b_off=(0, 0), j_outer=False, comm=None, chunk=None):
    assert m % tm == 0 and n % tn == 0 and k % tk == 0, (name, m, n, k, tm, tn, tk)
    assert chunk is None or (k == tk and tn % chunk == 0), name
    ni, nj, nk = m // tm, n // tn, k // tk
    assert not accs or nj == 1, name
    ij = (lambda g0, g1: (g1, g0)) if j_outer else (lambda g0, g1: (g0, g1))

    def spec(block, index):
        return pl.BlockSpec(block, lambda g0, g1, kk: index(*ij(g0, g1), kk))

    if ta:
        a_spec = spec((tk, tm), lambda i, j, kk: (kk + a_off[0], i + a_off[1]))
    else:
        a_spec = spec((tm, tk), lambda i, j, kk: (i + a_off[0], kk + a_off[1]))
    if tb:
        b_spec = spec((tn, tk), lambda i, j, kk: (j + b_off[0], kk + b_off[1]))
    else:
        b_spec = spec((tk, tn), lambda i, j, kk: (kk + b_off[0], j + b_off[1]))
    in_specs = [a_spec, b_spec]
    for _, bs, im in ins:
        in_specs.append(spec(bs, lambda i, j, kk, im=im: im(i, j)))
    out_specs, out_shape = [], []
    for shape, dtype, bs, im in outs:
        out_specs.append(spec(bs, lambda i, j, kk, im=im: im(i, j)))
        out_shape.append(jax.ShapeDtypeStruct(shape, dtype))
    for shape, dtype in accs:
        out_specs.append(spec(shape, lambda i, j, kk, nd=len(shape): (0,) * nd))
        out_shape.append(jax.ShapeDtypeStruct(shape, dtype))
    n_in, n_out, n_acc = len(ins), len(outs), len(accs)
    ca, cb = (0 if ta else 1), (1 if tb else 0)
    grid = (*ij(ni, nj), nk)
    comm_ins, comm_outs, comm_scratch = [], [], []
    if comm is not None:
        comm_in_specs, comm_out_specs = comm.specs
        comm_ins, comm_outs, comm_scratch = list(comm.ins), list(comm.out_shape), list(comm.scratch)
        in_specs += comm_in_specs
        out_specs += comm_out_specs
        out_shape += comm_outs
    n_ci, n_co = len(comm_ins), len(comm_outs)

    def body(*refs):
        a_ref, b_ref = refs[:2]
        in_refs = refs[2:2 + n_in]
        ci_refs = refs[2 + n_in:2 + n_in + n_ci]
        rest = refs[2 + n_in + n_ci:]
        out_refs, acc_refs = rest[:n_out], rest[n_out:n_out + n_acc]
        co_refs = rest[n_out + n_acc:n_out + n_acc + n_co]
        scratch = rest[n_out + n_acc + n_co:]
        sem_refs, scratch = scratch[:len(comm_scratch)], scratch[len(comm_scratch):]
        (i, j), kk = ij(pl.program_id(0), pl.program_id(1)), pl.program_id(2)
        if comm is not None:
            first_step, last_step = _grid_ends(grid)
            pl.when(first_step)(lambda: comm.start(ci_refs, co_refs, sem_refs))
        def finish(acc, cols=slice(None)):
            def of(r):
                return r[..., cols] if r.shape[-1] == tn else r[...]

            o_tiles, a_tiles = epi(acc, [of(r) for r in in_refs], i, j)
            for r, t in zip(out_refs, o_tiles, strict=True):
                r[..., cols] = t.astype(r.dtype)
            if n_acc:
                @pl.when(i == 0)
                def _():
                    for r, t in zip(acc_refs, a_tiles, strict=True):
                        r[..., cols] = t

                @pl.when(i > 0)
                def _():
                    for r, t in zip(acc_refs, a_tiles, strict=True):
                        r[..., cols] += t

        if chunk is not None:
            a_tile = a_ref[...].astype(BF16)
            for c0 in range(0, tn, chunk):
                cols = slice(c0, c0 + chunk)
                b_part = b_ref[cols, :] if tb else b_ref[:, cols]
                finish(_dot(a_tile, b_part.astype(BF16), ca, cb), cols)
            if comm is not None:
                pl.when(last_step)(lambda: comm.finish(ci_refs, co_refs, sem_refs))
            return

        part = _dot(a_ref[...].astype(BF16), b_ref[...].astype(BF16), ca, cb)
        if nk == 1:
            finish(part)
        else:
            acc_ref = scratch[0]

            @pl.when(kk == 0)
            def _():
                acc_ref[...] = part

            @pl.when(kk > 0)
            def _():
                acc_ref[...] += part

            @pl.when(kk == nk - 1)
            def _():
                finish(acc_ref[...])

        if comm is not None:
            pl.when(last_step)(lambda: comm.finish(ci_refs, co_refs, sem_refs))

    res = pl.pallas_call(
        body, name=name, grid=grid, in_specs=in_specs, out_specs=out_specs, out_shape=out_shape,
        scratch_shapes=comm_scratch + ([pltpu.VMEM((tm, tn), F32)] if nk > 1 else []),
        compiler_params=_cparams(),
    )(a, b, *[x for x, _, _ in ins], *comm_ins)
    return res


def _grid_ends(grid):
    ids = [pl.program_id(ax) for ax in range(len(grid))]
    first = functools.reduce(jnp.logical_and, [p == 0 for p in ids])
    last = functools.reduce(jnp.logical_and, [p == n - 1 for p, n in zip(ids, grid, strict=True)])
    return first, last


def _tile(tm, tn, dj=0):
    return (tm, tn), (lambda i, j: (i, j + dj))


def _rowvec(tn, dj=0):
    return (1, tn), (lambda i, j: (0, j + dj))


def _plain(acc, tiles, i, j):
    return [acc], []


def _ew(name, fn, ins, outs, rows, tr):
    assert rows % tr == 0, (name, rows, tr)
    in_specs = []
    for x in ins:
        if x.shape[0] == rows:
            in_specs.append(pl.BlockSpec((tr, x.shape[1]), lambda i: (i, 0)))
        else:
            in_specs.append(pl.BlockSpec(x.shape, lambda i: (0, 0)))
    n_in = len(ins)

    def body(*refs):
        res = fn(*[r[...] for r in refs[:n_in]])
        for r, t in zip(refs[n_in:], res, strict=True):
            r[...] = t.astype(r.dtype)

    return pl.pallas_call(
        body, name=name, grid=(rows // tr,), in_specs=in_specs,
        out_specs=[pl.BlockSpec((tr, w), lambda i: (i, 0)) for w, _ in outs],
        out_shape=[jax.ShapeDtypeStruct((rows, w), dt) for w, dt in outs],
        compiler_params=_cparams(),
    )(*ins)


def _cast_bf16(name, x):
    rows = x.shape[0]
    tr = next(t for t in (512, 256, 64) if rows % t == 0)
    return _ew(name, lambda v: (v,), [x], [(x.shape[1], BF16)], rows, tr)[0]


def _prep(x, comm):
    s = x.shape[0]
    half = RET_QK // 2
    inv = 1.0 / (ROPE_BASE ** (jnp.arange(half, dtype=F32) / half))
    inv2 = jnp.concatenate([inv, inv]).reshape(1, RET_QK)
    sign = jnp.concatenate([-jnp.ones((half,), F32), jnp.ones((half,), F32)]).reshape(1, RET_QK)
    tr = min(ROW_TILE, s)
    grid = (s // tr,)
    c_in_specs, c_out_specs, c_out_shape, c_scratch, c_ins, split = _host(comm, 3, 3)

    def body(*refs):
        (x_ref, inv_ref, sign_ref), (xb_ref, cos_ref, sin_ref), _, riding = split(refs)
        i = pl.program_id(0)
        first_step, last_step = _grid_ends(grid)
        pl.when(first_step)(lambda: comm.start(*riding))
        xb_ref[...] = x_ref[...].astype(BF16)
        pos = (lax.broadcasted_iota(jnp.int32, (tr, RET_QK), 0) + i * tr).astype(F32)
        ang = pos * inv_ref[...]
        cos_ref[...] = jnp.cos(ang)
        sin_ref[...] = jnp.sin(ang) * sign_ref[...]
        pl.when(last_step)(lambda: comm.finish(*riding))

    vec = pl.BlockSpec((1, RET_QK), lambda i: (0, 0))
    row = lambda w: pl.BlockSpec((tr, w), lambda i: (i, 0))
    return pl.pallas_call(
        body, name="prep", grid=grid,
        in_specs=[row(D_MODEL), vec, vec] + c_in_specs,
        out_specs=[row(D_MODEL), row(RET_QK), row(RET_QK)] + c_out_specs,
        out_shape=[jax.ShapeDtypeStruct((s, D_MODEL), BF16), jax.ShapeDtypeStruct((s, RET_QK), F32),
                   jax.ShapeDtypeStruct((s, RET_QK), F32)] + c_out_shape,
        scratch_shapes=c_scratch, compiler_params=_cparams(),
    )(x, inv2, sign, *c_ins)


def _swap_halves(x):
    return pltpu.roll(x, RET_QK // 2, 1)


def _norm(u):
    mu = jnp.mean(u, axis=-1, keepdims=True)
    d = u - mu
    var = jnp.mean(d * d, axis=-1, keepdims=True)
    rstd = lax.rsqrt(var + LN_EPS)
    return d * rstd, rstd


def _norm_bwd(dxh, xhat, rstd):
    m1 = jnp.mean(dxh, axis=-1, keepdims=True)
    m2 = jnp.mean(dxh * xhat, axis=-1, keepdims=True)
    return rstd * (dxh - m1 - xhat * m2)


def _colsum(t):
    return jnp.sum(t, axis=0, keepdims=True)


def _split_mm(t, tri):
    hi = t.astype(BF16)
    lo = (t - hi.astype(F32)).astype(BF16)
    return _dot(hi, tri, 1, 0) + _dot(lo, tri, 1, 0)


def _sb_masks():
    t = SB_BLOCK
    lane = lax.broadcasted_iota(jnp.int32, (1, LANES), 1)
    first = lane < SB_DIM
    m0 = jnp.where(first, 1.0, 0.0).astype(BF16)
    m1 = jnp.where(first, 0.0, 1.0).astype(BF16)
    row = lax.broadcasted_iota(jnp.int32, (t, t), 0)
    col = lax.broadcasted_iota(jnp.int32, (t, t), 1)
    return first, (m0, m1), row, col


def _sb_logits(qh, k, causal):
    z = _dot(qh, k, 1, 1)
    lp = jnp.log(1.0 + jnp.exp(-jnp.abs(z)))
    a = jnp.minimum(z, 0.0) - lp
    r = jnp.minimum(-z, 0.0) - lp
    if causal is not None:
        r = jnp.where(causal, r, 0.0)
    return a, r


def _sb_walk(i, blocks, l_ref, causal):
    pl.when(i == 0)(lambda: blocks([(i, causal)]))
    pl.when(i > 0)(lambda: blocks([(i, causal), (i - 1, None)]))

    def alive():
        top = jnp.max(functools.reduce(jnp.maximum, [l_ref[c] for c in range(l_ref.shape[0])]))
        return jnp.where(top > SB_DEAD, 1, 0)

    def cond(c):
        return jnp.logical_and(c[0] < i, c[1] > 0)

    def step(c):
        blocks([(i - 1 - c[0], None)])
        return c[0] + 1, alive()

    lax.while_loop(cond, step, (jnp.int32(1), alive()))


def _host(comm, n_in, n_out):
    if comm is None:
        return [], [], [], [], [], lambda refs: (refs[:n_in], refs[n_in:n_in + n_out], refs[n_in + n_out:], None)
    in_specs, out_specs = comm.specs
    n_ci, n_co, n_sem = len(comm.ins), len(comm.out_shape), len(comm.scratch)

    def split(refs):
        ins, ci = refs[:n_in], refs[n_in:n_in + n_ci]
        rest = refs[n_in + n_ci:]
        outs, co = rest[:n_out], rest[n_out:n_out + n_co]
        sems, scratch = rest[n_out + n_co:n_out + n_co + n_sem], rest[n_out + n_co + n_sem:]
        return ins, outs, scratch, (ci, co, sems)

    return in_specs, out_specs, list(comm.out_shape), list(comm.scratch), list(comm.ins), split


def _sb_qkv_specs(s, g):
    groups = SB_HEADS // 2 // g
    return [pl.BlockSpec((g, SB_BLOCK, LANES), lambda p, i: (p, i, 0)),
            pl.BlockSpec((g, s, LANES), lambda p, i: (groups + p, 0, 0)),
            pl.BlockSpec((g, s, LANES), lambda p, i: (2 * groups + p, 0, 0))]


def _sb_fwd(qkv, s, comm=None):
    t = SB_BLOCK
    g = 2
    nq = s // t
    grid = (SB_HEADS // 2 // g, nq)
    c_in_specs, c_out_specs, c_out_shape, c_scratch, c_ins, split = _host(comm, 3, 2)

    def body(*refs):
        (q_ref, k_ref, v_ref), (o_ref, of_ref), (l_ref, acc_ref), riding = split(refs)
        i = pl.program_id(1)
        if comm is not None:
            first_step, last_step = _grid_ends(grid)
            pl.when(first_step)(lambda: comm.start(*riding))
        first, hmask, row, col = _sb_masks()
        after = jnp.where(row > col, 1.0, 0.0).astype(BF16)
        causal = col < row
        heads = [(p, h) for p in range(g) for h in range(2)]
        qh = {(p, h): q_ref[p] * hmask[h] for p, h in heads}
        l_ref[...] = jnp.zeros_like(l_ref)
        acc_ref[...] = jnp.zeros_like(acc_ref)

        def blocks(todo):
            chains = [(b, p, h) for b in range(len(todo)) for p, h in heads]
            starts = [pl.multiple_of(kb * t, t) for kb, _ in todo]
            ks = {(b, p): k_ref[p, pl.ds(st, t), :] for b, st in enumerate(starts) for p in range(g)}
            vs = {(b, p): v_ref[p, pl.ds(st, t), :] for b, st in enumerate(starts) for p in range(g)}
            ar = {(b, p, h): _sb_logits(qh[p, h], ks[b, p], todo[b][1]) for b, p, h in chains}
            later = {c: _split_mm(ar[c][1], after) for c in chains}
            carry = {(p, h): l_ref[2 * p + h] for p, h in heads}
            w = {}
            for b, (_, mask) in enumerate(todo):
                for p, h in heads:
                    wc = jnp.exp(ar[b, p, h][0] + later[b, p, h] + carry[p, h])
                    w[b, p, h] = wc if mask is None else jnp.where(mask, wc, 0.0)
                carry = {(p, h): carry[p, h] + jnp.sum(ar[b, p, h][1], axis=1, keepdims=True) for p, h in heads}
            pv = {(b, p, h): _dot(w[b, p, h].astype(BF16), vs[b, p], 1, 0) for b, p, h in chains}
            for p in range(g):
                lanes = slice(p * LANES, (p + 1) * LANES)
                acc = acc_ref[:, lanes]
                for b in range(len(todo)):
                    acc = acc + jnp.where(first, pv[b, p, 0], pv[b, p, 1])
                acc_ref[:, lanes] = acc
            for p, h in heads:
                l_ref[2 * p + h] = carry[p, h]

        _sb_walk(i, blocks, l_ref, causal)
        o_ref[...] = acc_ref[...].astype(o_ref.dtype)
        of_ref[...] = acc_ref[...]
        if comm is not None:
            pl.when(last_step)(lambda: comm.finish(*riding))

    blk = pl.BlockSpec((t, g * LANES), lambda p, i: (i, p))
    return pl.pallas_call(
        body, name="sb_fwd", grid=grid,
        in_specs=_sb_qkv_specs(s, g) + c_in_specs,
        out_specs=[blk, blk] + c_out_specs,
        out_shape=[jax.ShapeDtypeStruct((s, SB_WIDTH), BF16), jax.ShapeDtypeStruct((s, SB_WIDTH), F32)] + c_out_shape,
        scratch_shapes=c_scratch + [pltpu.VMEM((2 * g, t, 1), F32), pltpu.VMEM((t, g * LANES), F32)],
        compiler_params=_cparams(),
    )(qkv, qkv, qkv, *c_ins)


def _sb_bwd(qkv, o, do, s, comm=None):
    t = SB_BLOCK
    g = 2
    nq = s // t
    grid = (SB_HEADS // 2 // g, nq)
    c_in_specs, c_out_specs, c_out_shape, c_scratch, c_ins, split = _host(comm, 5, 3)

    def body(*refs):
        ((q_ref, k_ref, v_ref, o_ref, do_ref), (dq_ref, dk_ref, dv_ref),
         (l_ref, e_ref, dq_acc, dk_acc, dv_acc), riding) = split(refs)
        i = pl.program_id(1)
        if comm is not None:
            first_step, last_step = _grid_ends(grid)
            pl.when(first_step)(lambda: comm.start(*riding))
        first, hmask, row, col = _sb_masks()
        after = jnp.where(row > col, 1.0, 0.0).astype(BF16)
        from_here = jnp.where(row >= col, 1.0, 0.0).astype(BF16)
        causal = col < row

        @pl.when(i == 0)
        def _():
            dk_acc[...] = jnp.zeros_like(dk_acc)
            dv_acc[...] = jnp.zeros_like(dv_acc)

        heads = [(p, h) for p in range(g) for h in range(2)]
        lanes = [slice(p * LANES, (p + 1) * LANES) for p in range(g)]
        q = [q_ref[p] for p in range(g)]
        do_ = [do_ref[:, lanes[p]] for p in range(g)]
        qh = {(p, h): q[p] * hmask[h] for p, h in heads}
        doh = {(p, h): do_[p] * hmask[h] for p, h in heads}
        total = {}
        for p in range(g):
            prod = do_[p].astype(F32) * o_ref[:, lanes[p]]
            total[p, 0] = jnp.sum(jnp.where(first, prod, 0.0), axis=1, keepdims=True)
            total[p, 1] = jnp.sum(jnp.where(first, 0.0, prod), axis=1, keepdims=True)
        l_ref[...] = jnp.zeros_like(l_ref)
        e_ref[...] = jnp.zeros_like(e_ref)
        dq_acc[...] = jnp.zeros_like(dq_acc)

        def blocks(todo):
            chains = [(b, p, h) for b in range(len(todo)) for p, h in heads]
            starts = [pl.multiple_of(kb * t, t) for kb, _ in todo]
            ks = {(b, p): k_ref[p, pl.ds(st, t), :] for b, st in enumerate(starts) for p in range(g)}
            vs = {(b, p): v_ref[p, pl.ds(st, t), :] for b, st in enumerate(starts) for p in range(g)}
            ar = {(b, p, h): _sb_logits(qh[p, h], ks[b, p], todo[b][1]) for b, p, h in chains}
            dw = {(b, p, h): _dot(doh[p, h], vs[b, p], 1, 1) for b, p, h in chains}
            later = {c: _split_mm(ar[c][1], after) for c in chains}
            carry = {(p, h): l_ref[2 * p + h] for p, h in heads}
            wb = {}
            for b, (_, mask) in enumerate(todo):
                for p, h in heads:
                    wc = jnp.exp(ar[b, p, h][0] + later[b, p, h] + carry[p, h])
                    wb[b, p, h] = (wc if mask is None else jnp.where(mask, wc, 0.0)).astype(BF16)
                carry = {(p, h): carry[p, h] + jnp.sum(ar[b, p, h][1], axis=1, keepdims=True) for p, h in heads}
            dvs = {(b, p, h): _dot(wb[b, p, h], do_[p], 0, 0) for b, p, h in chains}
            e = {c: dw[c] * wb[c].astype(F32) for c in chains}
            suffix = {c: _split_mm(e[c], from_here) for c in chains}
            e_carry = {(p, h): e_ref[2 * p + h] for p, h in heads}
            dz = {}
            for b, (_, mask) in enumerate(todo):
                for p, h in heads:
                    before = total[p, h] - (suffix[b, p, h] + e_carry[p, h])
                    dzc = e[b, p, h] - jnp.exp(ar[b, p, h][0]) * (e[b, p, h] + before)
                    dz[b, p, h] = (dzc if mask is None else jnp.where(mask, dzc, 0.0)).astype(BF16)
                e_carry = {(p, h): e_carry[p, h] + jnp.sum(e[b, p, h], axis=1, keepdims=True) for p, h in heads}
            dqs = {(b, p, h): _dot(dz[b, p, h], ks[b, p], 1, 0) for b, p, h in chains}
            dks = {(b, p, h): _dot(dz[b, p, h], q[p], 0, 0) for b, p, h in chains}
            for p in range(g):
                dq = dq_acc[:, lanes[p]]
                for b, st in enumerate(starts):
                    dq = dq + jnp.where(first, dqs[b, p, 0], dqs[b, p, 1])
                    dk_acc[pl.ds(st, t), lanes[p]] += jnp.where(first, dks[b, p, 0], dks[b, p, 1])
                    dv_acc[pl.ds(st, t), lanes[p]] += jnp.where(first, dvs[b, p, 0], dvs[b, p, 1])
                dq_acc[:, lanes[p]] = dq
            for p, h in heads:
                l_ref[2 * p + h] = carry[p, h]
                e_ref[2 * p + h] = e_carry[p, h]

        _sb_walk(i, blocks, l_ref, causal)
        dq_ref[...] = (dq_acc[...] * SB_SCALE).astype(dq_ref.dtype)

        @pl.when(i == nq - 1)
        def _():
            dk_ref[...] = dk_acc[...].astype(dk_ref.dtype)
            dv_ref[...] = dv_acc[...].astype(dv_ref.dtype)

        if comm is not None:
            pl.when(last_step)(lambda: comm.finish(*riding))

    once = pl.Buffered(1)
    q_spec, k_spec, v_spec = _sb_qkv_specs(s, g)
    k_spec = pl.BlockSpec(k_spec.block_shape, k_spec.index_map, pipeline_mode=once)
    v_spec = pl.BlockSpec(v_spec.block_shape, v_spec.index_map, pipeline_mode=once)
    blk = pl.BlockSpec((t, g * LANES), lambda p, i: (i, p))
    col_blk = pl.BlockSpec((s, g * LANES), lambda p, i: (0, p), pipeline_mode=once)
    sds = jax.ShapeDtypeStruct((s, SB_WIDTH), BF16)
    return pl.pallas_call(
        body, name="sb_bwd", grid=grid,
        in_specs=[q_spec, k_spec, v_spec, blk, blk] + c_in_specs,
        out_specs=[blk, col_blk, col_blk] + c_out_specs,
        out_shape=[sds, sds, sds] + c_out_shape,
        scratch_shapes=c_scratch + [pltpu.VMEM((2 * g, t, 1), F32), pltpu.VMEM((2 * g, t, 1), F32),
                                    pltpu.VMEM((t, g * LANES), F32), pltpu.VMEM((s, g * LANES), F32),
                                    pltpu.VMEM((s, g * LANES), F32)],
        compiler_params=_cparams(),
    )(qkv, qkv, qkv, o, do, *c_ins)


def _ret_log_gamma():
    lg = np.log1p(-np.exp2(-5.0 - np.arange(RET_HEADS, dtype=np.float32))).astype(np.float32)
    return jnp.asarray(np.broadcast_to(lg[:, None, None], (RET_HEADS, 8, LANES)).copy())


RET_SCRATCH = [pltpu.VMEM((RET_HEADS, RET_QK, RET_V), F32),
               pltpu.VMEM((RET_HEADS, RET_BLOCK, RET_BLOCK), F32),
               pltpu.VMEM((RET_HEADS, RET_BLOCK, 1), F32),
               pltpu.VMEM((RET_HEADS, RET_BLOCK, 1), F32)]


def _ret_begin(n, lg_ref, state, within, q_dec, k_dec):
    @pl.when(n == 0)
    def _():
        c = RET_BLOCK
        state[...] = jnp.zeros_like(state)
        row = lax.broadcasted_iota(jnp.int32, (c, c), 0)
        col = lax.broadcasted_iota(jnp.int32, (c, c), 1)
        rel = jnp.maximum(row - col, 0).astype(F32)
        idx = lax.broadcasted_iota(jnp.int32, (c, 1), 0).astype(F32)
        for h in range(RET_HEADS):
            lg = lg_ref[h, 0:1, 0:1]
            within[h] = jnp.where(row >= col, jnp.exp(lg * rel), 0.0)
            q_dec[h] = jnp.exp(lg * (idx + 1.0))
            k_dec[h] = jnp.exp(lg * (c - 1.0 - idx))


def _chunk_decay(lg_ref, h):
    return jnp.exp(lg_ref[h, 0:1, 0:1] * float(RET_BLOCK))


def _ret_heads(x, width):
    return [x[:, h * width:(h + 1) * width] for h in range(RET_HEADS)]


def _ret_specs(s, reverse=False):
    c = RET_BLOCK
    nc = s // c
    pos = (lambda n: nc - 1 - n) if reverse else (lambda n: n)
    q_spec = pl.BlockSpec((c, RET_QK_WIDTH), lambda n: (pos(n), 0))
    k_spec = pl.BlockSpec((c, RET_QK_WIDTH), lambda n: (pos(n), 1))
    v_spec = pl.BlockSpec((c, RET_V_WIDTH), lambda n: (pos(n), 0))
    lg_spec = pl.BlockSpec((RET_HEADS, 8, LANES), lambda n: (0, 0, 0))
    rope_spec = pl.BlockSpec((c, RET_QK), lambda n: (pos(n), 0))
    return nc, q_spec, k_spec, v_spec, lg_spec, rope_spec


def _ret_fwd(rqk, rvg, s):
    nc, q_spec, k_spec, v_spec, lg_spec, _ = _ret_specs(s)
    g_spec = pl.BlockSpec((RET_BLOCK, RET_V_WIDTH), lambda n: (n, 1))
    heads = range(RET_HEADS)

    def body(q_ref, k_ref, v_ref, g_ref, lg_ref, r_ref, y_ref, state, within, q_dec, k_dec):
        n = pl.program_id(0)
        _ret_begin(n, lg_ref, state, within, q_dec, k_dec)
        q, k = _ret_heads(q_ref[...], RET_QK), _ret_heads(k_ref[...], RET_QK)
        v, g = _ret_heads(v_ref[...], RET_V), _ret_heads(g_ref[...], RET_V)
        scores = [_dot(q[h].astype(BF16), k[h].astype(BF16), 1, 1) * within[h] for h in heads]
        cross = [_dot((q[h] * q_dec[h]).astype(BF16), state[h].astype(BF16), 1, 0) for h in heads]
        out = [_dot(scores[h].astype(BF16), v[h], 1, 0) + cross[h] for h in heads]
        grown = [_dot((k[h] * k_dec[h]).astype(BF16), v[h], 0, 0) for h in heads]
        for h in heads:
            sl = slice(h * RET_V, (h + 1) * RET_V)
            r_ref[:, sl] = out[h]
            xhat, _ = _norm(out[h])
            gh = g[h].astype(F32)
            y_ref[:, sl] = (gh * _sigmoid(gh) * xhat).astype(y_ref.dtype)
            state[h] = state[h] * _chunk_decay(lg_ref, h) + grown[h]

    return pl.pallas_call(
        body, name="ret_fwd", grid=(nc,),
        in_specs=[q_spec, k_spec, v_spec, g_spec, lg_spec],
        out_specs=[v_spec, v_spec],
        out_shape=[jax.ShapeDtypeStruct((s, RET_V_WIDTH), F32), jax.ShapeDtypeStruct((s, RET_V_WIDTH), BF16)],
        scratch_shapes=RET_SCRATCH,
        compiler_params=_cparams(),
    )(rqk, rqk, rvg, rvg, _ret_log_gamma())


def _rope_bwd(d, cos, sin):
    return d * cos + _swap_halves(d * sin)


def _ret_bwd_q(rqk, rv, d_out, cos2, sin2, s):
    nc, q_spec, k_spec, v_spec, lg_spec, rope_spec = _ret_specs(s)
    heads = range(RET_HEADS)

    def body(k_ref, v_ref, d_ref, lg_ref, cos_ref, sin_ref, dq_ref, state, within, q_dec, k_dec):
        n = pl.program_id(0)
        _ret_begin(n, lg_ref, state, within, q_dec, k_dec)
        k = _ret_heads(k_ref[...], RET_QK)
        v, d = _ret_heads(v_ref[...], RET_V), _ret_heads(d_ref[...], RET_V)
        cos, sin = cos_ref[...], sin_ref[...]
        d_scores = [_dot(d[h], v[h], 1, 1) * within[h] for h in heads]
        cross = [q_dec[h] * _dot(d[h], state[h].astype(BF16), 1, 1) for h in heads]
        dq = [_dot(d_scores[h].astype(BF16), k[h].astype(BF16), 1, 0) + cross[h] for h in heads]
        grown = [_dot((k[h] * k_dec[h]).astype(BF16), v[h], 0, 0) for h in heads]
        for h in heads:
            sl = slice(h * RET_QK, (h + 1) * RET_QK)
            dq_ref[:, sl] = (_rope_bwd(dq[h], cos, sin) * RET_SCALE).astype(dq_ref.dtype)
            state[h] = state[h] * _chunk_decay(lg_ref, h) + grown[h]

    return pl.pallas_call(
        body, name="ret_bwd_q", grid=(nc,),
        in_specs=[k_spec, v_spec, v_spec, lg_spec, rope_spec, rope_spec],
        out_specs=q_spec,
        out_shape=jax.ShapeDtypeStruct((s, RET_QK_WIDTH), BF16),
        scratch_shapes=RET_SCRATCH,
        compiler_params=_cparams(),
    )(rqk, rv, d_out, _ret_log_gamma(), cos2, sin2)


def _ret_bwd_kv(rqk, rv, d_out, cos2, sin2, s):
    nc, q_spec, k_spec, v_spec, lg_spec, rope_spec = _ret_specs(s, reverse=True)
    heads = range(RET_HEADS)

    def body(q_ref, k_ref, v_ref, d_ref, lg_ref, cos_ref, sin_ref, dk_ref, dv_ref, state, within, q_dec, k_dec):
        n = pl.program_id(0)
        _ret_begin(n, lg_ref, state, within, q_dec, k_dec)
        q, k = _ret_heads(q_ref[...], RET_QK), _ret_heads(k_ref[...], RET_QK)
        v, d = _ret_heads(v_ref[...], RET_V), _ret_heads(d_ref[...], RET_V)
        cos, sin = cos_ref[...], sin_ref[...]
        qb, kb = [q[h].astype(BF16) for h in heads], [k[h].astype(BF16) for h in heads]
        st = [state[h].astype(BF16) for h in heads]
        scores = [_dot(qb[h], kb[h], 1, 1) * within[h] for h in heads]
        d_scores = [_dot(d[h], v[h], 1, 1) * within[h] for h in heads]
        dk = [_dot(d_scores[h].astype(BF16), qb[h], 0, 0) + k_dec[h] * _dot(v[h], st[h], 1, 1) for h in heads]
        dv = [_dot(scores[h].astype(BF16), d[h], 0, 0) + k_dec[h] * _dot(kb[h], st[h], 1, 0) for h in heads]
        grown = [_dot((q[h] * q_dec[h]).astype(BF16), d[h], 0, 0) for h in heads]
        for h in heads:
            dk_ref[:, h * RET_QK:(h + 1) * RET_QK] = _rope_bwd(dk[h], cos, sin).astype(dk_ref.dtype)
            dv_ref[:, h * RET_V:(h + 1) * RET_V] = dv[h].astype(dv_ref.dtype)
            state[h] = state[h] * _chunk_decay(lg_ref, h) + grown[h]

    return pl.pallas_call(
        body, name="ret_bwd_kv", grid=(nc,),
        in_specs=[q_spec, k_spec, v_spec, v_spec, lg_spec, rope_spec, rope_spec],
        out_specs=[q_spec, v_spec],
        out_shape=[jax.ShapeDtypeStruct((s, RET_QK_WIDTH), BF16), jax.ShapeDtypeStruct((s, RET_V_WIDTH), BF16)],
        scratch_shapes=RET_SCRATCH,
        compiler_params=_cparams(),
    )(rqk, rqk, rv, d_out, _ret_log_gamma(), cos2, sin2)


def _xattn_probs(scores):
    sc = scores - jnp.max(scores, axis=-1, keepdims=True)
    p = jnp.exp(sc)
    return p / jnp.sum(p, axis=-1, keepdims=True)


def _xattn_heads(q_ref, kv_ref):
    sls = [slice(h * MEM_DIM, (h + 1) * MEM_DIM) for h in range(MEM_HEADS)]
    q = [q_ref[:, sl] for sl in sls]
    k = [kv_ref[:, sl] for sl in sls]
    v = [kv_ref[:, D_MODEL + h * MEM_DIM:D_MODEL + (h + 1) * MEM_DIM] for h in range(MEM_HEADS)]
    return sls, q, k, v


def _xattn_fwd(qm, kv, s):
    tq = XATTN_ROWS
    heads = range(MEM_HEADS)

    def body(q_ref, kv_ref, o_ref):
        sls, q, k, v = _xattn_heads(q_ref, kv_ref)
        scores = [_dot(q[h], k[h], 1, 1) for h in heads]
        p = [_xattn_probs(scores[h]).astype(BF16) for h in heads]
        out = [_dot(p[h], v[h], 1, 0) for h in heads]
        for h in heads:
            o_ref[:, sls[h]] = out[h].astype(o_ref.dtype)

    return pl.pallas_call(
        body, name="xattn_fwd", grid=(s // tq,),
        in_specs=[pl.BlockSpec((tq, D_MODEL), lambda i: (i, 0)),
                  pl.BlockSpec((MEM_LEN, 2 * D_MODEL), lambda i: (0, 0))],
        out_specs=pl.BlockSpec((tq, D_MODEL), lambda i: (i, 0)),
        out_shape=jax.ShapeDtypeStruct((s, D_MODEL), BF16),
        compiler_params=_cparams(),
    )(qm, kv)


def _xattn_bwd(qm, kv, do, s):
    tq = XATTN_ROWS

    def body(q_ref, kv_ref, do_ref, dq_ref, dkv_ref):
        i = pl.program_id(0)

        @pl.when(i == 0)
        def _():
            dkv_ref[...] = jnp.zeros_like(dkv_ref)

        heads = range(MEM_HEADS)
        sls, q, k, v = _xattn_heads(q_ref, kv_ref)
        d = [do_ref[:, sl] for sl in sls]
        scores = [_dot(q[h], k[h], 1, 1) for h in heads]
        dp = [_dot(d[h], v[h], 1, 1) for h in heads]
        p = [_xattn_probs(scores[h]) for h in heads]
        ds = [(p[h] * (dp[h] - jnp.sum(p[h] * dp[h], axis=-1, keepdims=True))).astype(BF16) for h in heads]
        dq = [_dot(ds[h], k[h], 1, 0) for h in heads]
        dk = [_dot(ds[h], q[h], 0, 0) for h in heads]
        dv = [_dot(p[h].astype(BF16), d[h], 0, 0) for h in heads]
        for h in heads:
            dq_ref[:, sls[h]] = (dq[h] * MEM_SCALE).astype(dq_ref.dtype)
            dkv_ref[:, sls[h]] += dk[h]
            dkv_ref[:, D_MODEL + h * MEM_DIM:D_MODEL + (h + 1) * MEM_DIM] += dv[h]

    row_blk = pl.BlockSpec((tq, D_MODEL), lambda i: (i, 0))
    kv_blk = pl.BlockSpec((MEM_LEN, 2 * D_MODEL), lambda i: (0, 0))
    return pl.pallas_call(
        body, name="xattn_bwd", grid=(s // tq,),
        in_specs=[row_blk, kv_blk, row_blk],
        out_specs=[row_blk, kv_blk],
        out_shape=[jax.ShapeDtypeStruct((s, D_MODEL), BF16), jax.ShapeDtypeStruct((MEM_LEN, 2 * D_MODEL), F32)],
        compiler_params=_cparams(),
    )(qm, kv, do)


def _place():
    x, y, c = lax.axis_index("x"), lax.axis_index("y"), lax.axis_index("c")
    others = [(1 - x, y), (x, 1 - y), (1 - x, 1 - y)]
    return x, y, c, others


def _slab(ref, axis, chip, size):
    start = pl.multiple_of(chip * size, LANES if axis == 1 else 16)
    if axis == 0:
        return ref.at[pl.ds(start, size), :]
    return ref.at[:, pl.ds(start, size)]


class _CommPlan:
    def __init__(self, ins, out_shape, scratch, start, finish):
        self.ins, self.out_shape, self.scratch, self.start, self.finish = ins, out_shape, scratch, start, finish

    @property
    def specs(self):
        any_spec = pl.BlockSpec(memory_space=pl.ANY)
        return [any_spec] * len(self.ins), [any_spec] * len(self.out_shape)

    def split(self, refs):
        n_in, n_out = len(self.ins), len(self.out_shape)
        return refs[:n_in], refs[n_in:n_in + n_out], refs[n_in + n_out:]


def _gather_plan(names, shards):
    spec = {name: (shape, axis) for name, shape, axis in BIG}
    nw = len(names)

    def shard_half(ref, c):
        rows = ref.shape[0] // 2
        return ref.at[pl.ds(pl.multiple_of(c * rows, 16), rows), :]

    def region(ref, w, chip, c):
        shape, axis = spec[names[w]]
        size = shape[axis] // N_CHIPS
        if axis == 0:
            rows = size // 2
            return ref.at[pl.ds(pl.multiple_of(chip * size + c * rows, 16), rows), :]
        rows = shape[0] // 2
        return ref.at[pl.ds(pl.multiple_of(c * rows, 16), rows), pl.ds(pl.multiple_of(chip * size, LANES), size)]

    def ops(shard, full, sems):
        ici_send, ici_recv, d2d_send, d2d_recv, local_sems = sems
        x, y, c, others = _place()
        mine, sibling = 2 * x + y, (x, y, 1 - c)
        local, over_ici, arrived, passed_on, from_sibling = [], [], [], [], []
        for w in range(nw):
            shape, axis = spec[names[w]]
            local.append(pltpu.make_async_copy(shard[w], _slab(full[w], axis, mine, shape[axis] // N_CHIPS),
                                               local_sems.at[w]))
            for t, (qx, qy) in enumerate(others):
                n, theirs = 3 * w + t, 2 * qx + qy
                over_ici.append(pltpu.make_async_remote_copy(
                    src_ref=shard_half(shard[w], c), dst_ref=region(full[w], w, mine, c),
                    send_sem=ici_send.at[n], recv_sem=ici_recv.at[n], device_id=(qx, qy, c), device_id_type=MESH))
                arrived.append(pltpu.make_async_remote_copy(
                    src_ref=shard_half(shard[w], c), dst_ref=region(full[w], w, theirs, c),
                    send_sem=ici_send.at[n], recv_sem=ici_recv.at[n], device_id=(qx, qy, c), device_id_type=MESH))
                passed_on.append(pltpu.make_async_remote_copy(
                    src_ref=region(full[w], w, theirs, c), dst_ref=region(full[w], w, theirs, c),
                    send_sem=d2d_send.at[n], recv_sem=d2d_recv.at[n], device_id=sibling, device_id_type=MESH))
                from_sibling.append(pltpu.make_async_remote_copy(
                    src_ref=region(full[w], w, theirs, c), dst_ref=region(full[w], w, theirs, 1 - c),
                    send_sem=d2d_send.at[n], recv_sem=d2d_recv.at[n], device_id=sibling, device_id_type=MESH))
        return local, over_ici, arrived, passed_on, from_sibling

    def start(shard, full, sems):
        local, over_ici, _, _, _ = ops(shard, full, sems)
        for cp in local + over_ici:
            cp.start()

    def finish(shard, full, sems):
        local, over_ici, arrived, passed_on, from_sibling = ops(shard, full, sems)
        for got, onward in zip(arrived, passed_on, strict=True):
            got.wait_recv()
            onward.start()
        for got in from_sibling:
            got.wait_recv()
        for cp in over_ici + passed_on:
            cp.wait_send()
        for cp in local:
            cp.wait()

    dma = pltpu.SemaphoreType.DMA
    return _CommPlan(
        ins=[shards[name] for name in names],
        out_shape=[jax.ShapeDtypeStruct(spec[name][0], BF16) for name in names],
        scratch=[dma((3 * nw,)), dma((3 * nw,)), dma((3 * nw,)), dma((3 * nw,)), dma((nw,))],
        start=start, finish=finish)


def _shard_shape(shape, axis):
    return tuple(d // N_CHIPS if a == axis else d for a, d in enumerate(shape))


def _exchange_plan(names, grads):
    spec = {name: (shape, axis) for name, shape, axis in BIG}
    nw = len(names)

    def ops(grad, stack, sems):
        send_sems, recv_sems, local_sems = sems
        x, y, c, others = _place()
        mine = 2 * x + y
        me, sibling = (x, y, c), (x, y, 1 - c)

        def dev(px, py, pc):
            return 4 * px + 2 * py + pc

        def copy(w, n, src, slot, to):
            return pltpu.make_async_remote_copy(
                src_ref=src, dst_ref=stack[w].at[slot], send_sem=send_sems.at[7 * w + n],
                recv_sem=recv_sems.at[7 * w + n], device_id=to, device_id_type=MESH)

        local, first, arrived, passed_on, from_sibling = [], [], [], [], []
        for w in range(nw):
            shape, axis = spec[names[w]]
            size = shape[axis] // N_CHIPS
            own = _slab(grad[w], axis, mine, size)
            local.append(pltpu.make_async_copy(own, stack[w].at[dev(*me)], local_sems.at[w]))
            first.append(copy(w, 0, own, dev(*me), sibling))
            from_sibling.append(copy(w, 0, own, dev(*sibling), me))
            for t, (qx, qy) in enumerate(others):
                got = stack[w].at[dev(qx, qy, c)]
                first.append(copy(w, 1 + t, _slab(grad[w], axis, 2 * qx + qy, size), dev(*me), (qx, qy, c)))
                arrived.append(copy(w, 1 + t, got, dev(qx, qy, c), me))
                passed_on.append(copy(w, 4 + t, got, dev(qx, qy, c), sibling))
                from_sibling.append(copy(w, 4 + t, got, dev(qx, qy, 1 - c), me))
        return local, first, arrived, passed_on, from_sibling

    def start(grad, stack, sems):
        local, first, _, _, _ = ops(grad, stack, sems)
        for cp in local + first:
            cp.start()

    def finish(grad, stack, sems):
        local, first, arrived, passed_on, from_sibling = ops(grad, stack, sems)
        for got, onward in zip(arrived, passed_on, strict=True):
            got.wait_recv()
            onward.start()
        for got in from_sibling:
            got.wait_recv()
        for cp in first + passed_on:
            cp.wait_send()
        for cp in local:
            cp.wait()

    dma = pltpu.SemaphoreType.DMA
    return _CommPlan(
        ins=[grads[name] for name in names],
        out_shape=[jax.ShapeDtypeStruct((N_DEV,) + _shard_shape(*spec[name]), BF16) for name in names],
        scratch=[dma((7 * nw,)), dma((7 * nw,)), dma((nw,))],
        start=start, finish=finish)


def _adamw(w, g, m, v):
    m = ADAM_B1 * m + (1.0 - ADAM_B1) * g
    v = ADAM_B2 * v + (1.0 - ADAM_B2) * (g * g)
    m_hat = m / (1.0 - ADAM_B1 ** ADAM_STEP)
    v_hat = v / (1.0 - ADAM_B2 ** ADAM_STEP)
    delta = -ADAM_LR * (m_hat / (jnp.sqrt(v_hat) + ADAM_EPS) + ADAM_WD * w)
    return delta, m, v


def _reduce_adamw(name, stack, w, m, v):
    rows, cols = w.shape
    tr = next(t for t in (256, 128, 64) if rows % t == 0)

    def body(s_ref, w_ref, m_ref, v_ref, g_ref, d_ref, nm_ref, nv_ref):
        g = s_ref[0].astype(F32)
        for d in range(1, N_DEV):
            g = g + s_ref[d].astype(F32)
        g_ref[...] = g
        d_ref[...], nm_ref[...], nv_ref[...] = _adamw(w_ref[...], g, m_ref[...], v_ref[...])

    blk = pl.BlockSpec((tr, cols), lambda i: (i, 0))
    return pl.pallas_call(
        body, name=name, grid=(rows // tr,),
        in_specs=[pl.BlockSpec((N_DEV, tr, cols), lambda i: (0, i, 0)), blk, blk, blk],
        out_specs=[blk] * 4, out_shape=[jax.ShapeDtypeStruct((rows, cols), F32)] * 4,
        compiler_params=_cparams(),
    )(stack, w, m, v)


def _small_step(pack, w, m, v):
    def body(p_ref, w_ref, m_ref, v_ref, g_ref, d_ref, nm_ref, nv_ref, loss_ref, all_ref, send_sems, recv_sems):
        x, y, c, _ = _place()
        me = 4 * x + 2 * y + c
        all_ref[me] = p_ref[...]
        sent = []
        for n in range(1, N_DEV):
            peer = me ^ n
            cp = pltpu.make_async_remote_copy(
                src_ref=p_ref, dst_ref=all_ref.at[me], send_sem=send_sems.at[n - 1], recv_sem=recv_sems.at[n - 1],
                device_id=(peer // 4, (peer // 2) % 2, peer % 2), device_id_type=MESH)
            cp.start()
            sent.append(cp)
        for n in range(1, N_DEV):
            peer = me ^ n
            pltpu.make_async_remote_copy(
                src_ref=p_ref, dst_ref=all_ref.at[peer], send_sem=send_sems.at[n - 1], recv_sem=recv_sems.at[n - 1],
                device_id=(peer // 4, (peer // 2) % 2, peer % 2), device_id_type=MESH).wait_recv()
        for cp in sent:
            cp.wait_send()
        tot = all_ref[0]
        for d in range(1, N_DEV):
            tot = tot + all_ref[d]
        g = tot[:SMALL_ROWS]
        g_ref[...] = g
        d_ref[...], nm_ref[...], nv_ref[...] = _adamw(w_ref[...], g, m_ref[...], v_ref[...])
        loss_ref[...] = jnp.sum(jnp.sum(tot[SMALL_ROWS:], axis=1, keepdims=True), axis=0, keepdims=True)

    vm = pl.BlockSpec(memory_space=pltpu.VMEM)
    small = jax.ShapeDtypeStruct((SMALL_ROWS, LANES), F32)
    return pl.pallas_call(
        body, name="small_step",
        in_specs=[vm] * 4, out_specs=[vm] * 5,
        out_shape=[small] * 4 + [jax.ShapeDtypeStruct((1, 1), F32)],
        scratch_shapes=[pltpu.VMEM((N_DEV, PACK_ROWS, LANES), F32),
                        pltpu.SemaphoreType.DMA((N_DEV - 1,)), pltpu.SemaphoreType.DMA((N_DEV - 1,))],
    )(pack, w, m, v)


LATER_WEIGHTS = tuple(name for name, _, _ in BIG if name != "w_in")


def _layer_step(x, mem, tgt, shards, vec):
    s = x.shape[0]
    d = D_MODEL
    tm = min(ROW_TILE, s)
    tl = min(WIDE_TILE, s)
    xb, cos2, sin2, w_in = _prep(x, _gather_plan(("w_in",), shards))
    bf = lambda w: ((s, w), BF16)
    f32 = lambda w: ((s, w), F32)

    w_sb, w_rqk = w_in[:, :OFF_RET_Q], w_in[:, OFF_RET_Q:OFF_RET_V]
    w_rvg, w_gate = w_in[:, OFF_RET_V:OFF_GATE], w_in[:, OFF_GATE:]
    q_scale = lambda width, q_width, scale: jnp.concatenate(
        [jnp.full((1, q_width), scale, F32), jnp.ones((1, width - q_width), F32)], axis=1)
    n_groups = 3 * SB_WIDTH // LANES

    def sb_epi(acc, t, i, j):
        scaled = acc * t[0]
        return [jnp.stack([scaled[:, g * LANES:(g + 1) * LANES] for g in range(n_groups)])], []

    (sb_qkv,) = _mm(
        "in_sb", xb, w_sb, s, 3 * SB_WIDTH, d, tm=tl, tn=3 * SB_WIDTH, tk=d, epi=sb_epi,
        ins=[(q_scale(3 * SB_WIDTH, SB_WIDTH, SB_SCALE), *_rowvec(3 * SB_WIDTH))],
        outs=[((n_groups, s, LANES), BF16, (n_groups, tl, LANES), lambda i, j: (0, i, 0))])

    def rope_epi(acc, t, i, j):
        cos, sin, scale = t
        parts = []
        for g in range(acc.shape[1] // RET_QK):
            xg = acc[:, g * RET_QK:(g + 1) * RET_QK]
            parts.append(xg * cos + _swap_halves(xg) * sin)
        return [jnp.concatenate(parts, axis=1) * scale], []

    rope_in = ((tl, RET_QK), lambda i, j: (i, 0))
    (rqk,) = _mm("in_rqk", xb, w_rqk, s, 2 * RET_QK_WIDTH, d, tm=tl, tn=2 * RET_QK_WIDTH, tk=d, epi=rope_epi,
                 chunk=MXU_COLS,
                 ins=[(cos2, *rope_in), (sin2, *rope_in),
                      (q_scale(2 * RET_QK_WIDTH, RET_QK_WIDTH, RET_SCALE), *_rowvec(2 * RET_QK_WIDTH))],
                 outs=[(*f32(2 * RET_QK_WIDTH), *_tile(tl, 2 * RET_QK_WIDTH))])
    (rvg,) = _mm("in_rvg", xb, w_rvg, s, 2 * RET_V_WIDTH, d, tm=tl, tn=2 * RET_V_WIDTH, tk=d, chunk=MXU_COLS,
                 epi=_plain, outs=[(*bf(2 * RET_V_WIDTH), *_tile(tl, 2 * RET_V_WIDTH))])
    (gates,) = _mm("in_gate", xb, w_gate, s, 2 * d, d, tm=tl, tn=2 * d, tk=d, chunk=MXU_COLS,
                   epi=lambda acc, t, i, j: ([_sigmoid(acc + t[0])], []),
                   ins=[(vec["b_gate"], *_rowvec(2 * d))], outs=[(*bf(2 * d), *_tile(tl, 2 * d))])

    sb_out, sb_out_f32, *gathered = _sb_fwd(sb_qkv, s, comm=_gather_plan(LATER_WEIGHTS, shards))
    wt = dict(zip(LATER_WEIGHTS, gathered, strict=True))
    ret, gated = _ret_fwd(rqk, rvg, s)
    (y_sb,) = _mm("sb_o", sb_out, wt["w_sb_o"], s, d, SB_WIDTH, tm=tl, tn=d, tk=SB_WIDTH, epi=_plain,
                  outs=[(*bf(d), *_tile(tl, d))])
    y_ret, mixin = _mm(
        "ret_o", gated, wt["w_ret_o"], s, d, RET_V_WIDTH, tm=tl, tn=d, tk=RET_V_WIDTH, chunk=MXU_COLS,
        epi=lambda acc, t, i, j: ([acc, t[0].astype(F32) * t[2].astype(F32) + t[1].astype(F32) * acc], []),
        ins=[(gates, *_tile(tl, d)), (gates, *_tile(tl, d, 1)), (y_sb, *_tile(tl, d))],
        outs=[(*bf(d), *_tile(tl, d)), (*bf(d), *_tile(tl, d))])

    def ln_epi(acc, t, i, j):
        *res, g, b = t
        prev = res[0] if len(res) == 1 else res[0] * res[1] + res[2]
        xhat, rstd = _norm(DN_ALPHA * prev + acc)
        return [xhat * g + b, xhat, rstd], []

    full = _tile(tm, d)
    col1 = ((tm, 1), lambda i, j: (i, 0))
    vec_in = lambda name: (vec[name], *_rowvec(d))
    ln_outs = [(*bf(d), *full), (*f32(d), *full), ((s, 1), F32, *col1)]
    x1b, xhat1, rstd1 = _mm(
        "mix_o", mixin, wt["w_mix_o"], s, d, d, tm=tm, tn=d, tk=d, epi=ln_epi,
        ins=[(x, *full), vec_in("ln1_g"), vec_in("ln1_b")], outs=ln_outs)

    (qm,) = _mm("mem_q", x1b, wt["w_mem_q"], s, d, d, tm=tl, tn=d, tk=d,
                epi=lambda acc, t, i, j: ([acc * MEM_SCALE], []), outs=[(*bf(d), *_tile(tl, d))])
    (kv,) = _mm("mem_kv", mem, wt["w_mem_kv"], MEM_LEN, 2 * d, d, tm=MEM_LEN, tn=d, tk=d, epi=_plain,
                outs=[((MEM_LEN, 2 * d), BF16, *_tile(MEM_LEN, d))])
    att = _xattn_fwd(qm, kv, s)
    x2b, xhat2, rstd2 = _mm(
        "mem_o", att, wt["w_mem_o"], s, d, d, tm=tm, tn=d, tk=d, epi=ln_epi,
        ins=[(xhat1, *full), vec_in("ln1_g"), vec_in("ln1_b"), vec_in("ln2_g"), vec_in("ln2_b")], outs=ln_outs)

    fh = FFN_HIDDEN
    tf = fh // 2
    (f1,) = _mm("ffn_in1", x2b, wt["w_ffn_in"], s, fh, d, tm=tl, tn=tf, tk=d, epi=_plain, j_outer=True,
                outs=[(*bf(fh), *_tile(tl, tf))])

    def swiglu_epi(acc, t, i, j):
        a = t[0].astype(F32)
        return [acc, a * _sigmoid(a) * acc], []

    f2, act = _mm(
        "ffn_in2", x2b, wt["w_ffn_in"], s, fh, d, tm=tm, tn=fh, tk=d, b_off=(0, 1), epi=swiglu_epi, chunk=MXU_COLS,
        ins=[(f1, *_tile(tm, fh))], outs=[(*bf(fh), *_tile(tm, fh)), (*bf(fh), *_tile(tm, fh))])

    def head_epi(acc, t, i, j):
        prev_hat, prev_g, prev_b, g, b, target = t
        xhat, rstd = _norm(DN_ALPHA * (prev_hat * prev_g + prev_b) + acc)
        err = xhat * g + b - target
        dy = err * (1.0 / d)
        du = _norm_bwd(dy * g, xhat, rstd)
        return [du], [_colsum(dy * xhat), _colsum(dy), _colsum(err * err) * (0.5 / d)]

    vec_acc = ((1, d), F32)
    du3b, dg3, db3, loss_cols = _mm(
        "ffn_out", act, wt["w_ffn_out"], s, d, fh, tm=tm, tn=d, tk=fh, epi=head_epi,
        ins=[(xhat2, *full), vec_in("ln2_g"), vec_in("ln2_b"), vec_in("ln3_g"), vec_in("ln3_b"), (tgt, *full)],
        outs=[(*bf(d), *full)], accs=[vec_acc] * 3)

    grads, stacks = {}, {}
    ts = min(SEQ_TILE, s)

    def wgrad(name, a, b, m, n, tm_, tn_, tk_=None, carry=()):
        g, *stacked = _mm(name, a, b, m, n, a.shape[0], tm=tm_, tn=tn_, tk=tk_ or ts, ta=True, epi=_plain,
                          outs=[((m, n), BF16, *_tile(tm_, tn_))],
                          comm=_exchange_plan(carry, grads) if carry else None)
        stacks.update(zip(carry, stacked, strict=True))
        return g

    def ffn_bwd_epi(acc, t, i, j):
        a, b = t[0].astype(F32), t[1].astype(F32)
        sg = _sigmoid(a)
        return [acc * b * (sg * (1.0 + a * (1.0 - sg))), acc * (a * sg)], []

    df1, df2 = _mm(
        "ffn_out_t", du3b, wt["w_ffn_out"], s, fh, d, tm=tm, tn=fh, tk=d, tb=True, epi=ffn_bwd_epi, chunk=MXU_COLS,
        ins=[(f1, *_tile(tm, fh)), (f2, *_tile(tm, fh))],
        outs=[(*bf(fh), *_tile(tm, fh)), (*bf(fh), *_tile(tm, fh))])
    grads["w_ffn_out"] = wgrad("g_ffn_out", act, du3b, fh, d, tf, d)
    grads["w_ffn_in"] = jnp.concatenate(
        [wgrad("g_ffn_in1", x2b, df1, d, fh, d, tf, carry=("w_ffn_out",)),
         wgrad("g_ffn_in2", x2b, df2, d, fh, d, tf)], axis=1)
    (dx2a,) = _mm("ffn_in1_t", df1, wt["w_ffn_in"], s, d, fh, tm=tm, tn=d, tk=fh, tb=True, epi=_plain,
                  outs=[(*f32(d), *full)])

    def ln_bwd(name, a, b, k, tk, b_off, more, scales, xhat, rstd, g):
        def epi(acc, t, i, j):
            *extra, xh, rs, gg = t
            dy = acc
            for e, sc in zip(extra, scales, strict=True):
                dy = dy + e.astype(F32) * sc
            return [_norm_bwd(dy * gg, xh, rs)], [_colsum(dy * xh), _colsum(dy)]

        return _mm(name, a, b, s, d, k, tm=tm, tn=d, tk=tk, tb=True, b_off=b_off, epi=epi,
                   ins=[(e, *full) for e in more] + [(xhat, *full), (rstd, *col1), (g, *_rowvec(d))],
                   outs=[(*bf(d), *full)], accs=[vec_acc] * 2)

    du2b, dg2, db2 = ln_bwd("ffn_in2_t", df2, wt["w_ffn_in"], fh, fh, (0, 1), [dx2a, du3b], [1.0, DN_ALPHA],
                            xhat2, rstd2, vec["ln2_g"])

    (datt,) = _mm("mem_o_t", du2b, wt["w_mem_o"], s, d, d, tm=tl, tn=d, tk=d, tb=True, epi=_plain,
                  outs=[(*bf(d), *_tile(tl, d))])
    grads["w_mem_o"] = wgrad("g_mem_o", att, du2b, d, d, d, d)
    dqm, dkv = _xattn_bwd(qm, kv, datt, s)
    grads["w_mem_q"] = wgrad("g_mem_q", x1b, dqm, d, d, d, d)
    grads["w_mem_kv"] = wgrad("g_mem_kv", mem, dkv, d, 2 * d, d, d, MEM_LEN)
    du1b, dg1, db1 = ln_bwd("mem_q_t", dqm, wt["w_mem_q"], d, d, (0, 0), [du2b], [DN_ALPHA],
                            xhat1, rstd1, vec["ln1_g"])

    def merge_bwd_epi(acc, t, i, j):
        g0, g1, ysb, yret = (v.astype(F32) for v in t)
        dgate0 = acc * ysb * (g0 * (1.0 - g0))
        dgate1 = acc * yret * (g1 * (1.0 - g1))
        return [dgate0, dgate1, acc * g0, acc * g1], [_colsum(dgate0), _colsum(dgate1)]

    dgate0, dgate1, dy_sb, dy_ret, dbg0, dbg1, stacks["w_mem_kv"] = _mm(
        "mix_o_t", du1b, wt["w_mix_o"], s, d, d, tm=tm, tn=d, tk=d, tb=True, epi=merge_bwd_epi,
        ins=[(gates, *full), (gates, *_tile(tm, d, 1)), (y_sb, *full), (y_ret, *full)],
        outs=[(*bf(d), *full)] * 4, accs=[vec_acc] * 2, comm=_exchange_plan(("w_mem_kv",), grads))
    grads["w_mix_o"] = wgrad("g_mix_o", mixin, du1b, d, d, d, d)
    grads["w_sb_o"] = wgrad("g_sb_o", sb_out, dy_sb, SB_WIDTH, d, SB_WIDTH, d)
    grads["w_ret_o"] = wgrad("g_ret_o", gated, dy_ret, RET_V_WIDTH, d, RET_V_WIDTH, d)
    (dsb_out,) = _mm("sb_o_t", dy_sb, wt["w_sb_o"], s, SB_WIDTH, d, tm=tl, tn=SB_WIDTH, tk=d, tb=True, epi=_plain,
                     outs=[(*bf(SB_WIDTH), *_tile(tl, SB_WIDTH))])

    def gate_norm_bwd_epi(acc, t, i, j):
        r, g = t[0], t[1].astype(F32)
        drg, dret = [], []
        for h in range(acc.shape[1] // RET_V):
            sl = slice(h * RET_V, (h + 1) * RET_V)
            xhat, rstd = _norm(r[:, sl])
            gg, dd = g[:, sl], acc[:, sl]
            sg = _sigmoid(gg)
            drg.append(dd * xhat * (sg * (1.0 + gg * (1.0 - sg))))
            dret.append(_norm_bwd(dd * (gg * sg), xhat, rstd))
        return [jnp.concatenate(drg, axis=1), jnp.concatenate(dret, axis=1)], []

    drg, dret, stacks["w_mem_o"], stacks["w_mem_q"] = _mm(
        "ret_o_t", dy_ret, wt["w_ret_o"], s, RET_V_WIDTH, d, tm=tm, tn=d, tk=d, tb=True, epi=gate_norm_bwd_epi,
        chunk=MXU_COLS,
        ins=[(ret, *full), (rvg, *_tile(tm, d, 1))],
        outs=[(*bf(RET_V_WIDTH), *full)] * 2, comm=_exchange_plan(("w_mem_o", "w_mem_q"), grads))

    drq = _ret_bwd_q(rqk, rvg, dret, cos2, sin2, s)
    drk, drv = _ret_bwd_kv(rqk, rvg, dret, cos2, sin2, s)
    rest = tuple(name for name in LATER_WEIGHTS if name not in stacks)
    dsq, dsk, dsv, *stacked = _sb_bwd(sb_qkv, sb_out_f32, dsb_out, s, comm=_exchange_plan(rest, grads))
    stacks.update(zip(rest, stacked, strict=True))

    dh = jnp.concatenate([dsq, dsk, dsv, drq, drk, drv, drg, dgate0, dgate1], axis=1)
    grads["w_in"] = wgrad("g_in", xb, dh, d, IN_WIDTH, d, IN_WIDTH // N_CHIPS)
    grad_x, stacks["w_in"] = _mm(
        "in_t", dh, w_in, s, d, IN_WIDTH, tm=tl, tn=d, tk=IN_WIDTH // N_CHIPS, tb=True,
        epi=lambda acc, t, i, j: ([acc + DN_ALPHA * t[0].astype(F32)], []),
        ins=[(du1b, *_tile(tl, d))], outs=[(*f32(d), *_tile(tl, d))], comm=_exchange_plan(("w_in",), grads))

    small = {"b_gate": jnp.concatenate([dbg0, dbg1], axis=1), "ln1_g": dg1, "ln1_b": db1, "ln2_g": dg2,
             "ln2_b": db2, "ln3_g": dg3, "ln3_b": db3}
    return grad_x, stacks, small, loss_cols


def kernel(x, mem, w_in, b_gate, w_sb_o, w_ret_o, w_mix_o, ln1_g, ln1_b, w_mem_q, w_mem_kv, w_mem_o, ln2_g, ln2_b, w_ffn_in, w_ffn_out, ln3_g, ln3_b, loss_target, m_w_in, m_b_gate, m_w_sb_o, m_w_ret_o, m_w_mix_o, m_ln1_g, m_ln1_b, m_w_mem_q, m_w_mem_kv, m_w_mem_o, m_ln2_g, m_ln2_b, m_w_ffn_in, m_w_ffn_out, m_ln3_g, m_ln3_b, v_w_in, v_b_gate, v_w_sb_o, v_w_ret_o, v_w_mix_o, v_ln1_g, v_ln1_b, v_w_mem_q, v_w_mem_kv, v_w_mem_o, v_ln2_g, v_ln2_b, v_w_ffn_in, v_w_ffn_out, v_ln3_g, v_ln3_b):
    given = dict(locals())
    s = x.shape[1]
    x2d = x.reshape(s, D_MODEL)
    tgt = loss_target.reshape(s, D_MODEL)
    mem2d = mem.reshape(MEM_LEN, D_MODEL)
    shard = {name: given[name].reshape(_shard_shape(shape, axis)) for name, shape, axis in BIG}
    vec = {name: given[name] for name in SMALL}

    shards_bf = {name: _cast_bf16("cast_" + name, shard[name]) for name, _, _ in BIG}

    grad_x, stacks, small, loss_cols = _layer_step(x2d, mem2d, tgt, shards_bf, vec)

    out = {}
    for name, shape, axis in BIG:
        stack = stacks[name]
        shp = given[name].shape
        res = _reduce_adamw("adamw_" + name, stack, shard[name], given["m_" + name].reshape(stack.shape[1:]),
                            given["v_" + name].reshape(stack.shape[1:]))
        out[name] = [r.reshape(shp) for r in res]

    pack = jnp.concatenate([small[name] for name in SMALL] + [loss_cols], axis=1).reshape(PACK_ROWS, LANES)
    cat = lambda pre: jnp.concatenate([given[pre + name] for name in SMALL], axis=1).reshape(SMALL_ROWS, LANES)
    *res, loss = _small_step(pack, cat(""), cat("m_"), cat("v_"))
    flat = [r.reshape(1, SMALL_LEN) for r in res]
    off = 0
    for name in SMALL:
        n = given[name].shape[1]
        out[name] = [r[:, off:off + n] for r in flat]
        off += n

    return (loss.reshape(()), grad_x.reshape(x.shape),
            *[out[name][0] for name in WEIGHT_ORDER], *[out[name][1] for name in WEIGHT_ORDER],
            *[out[name][2] for name in WEIGHT_ORDER], *[out[name][3] for name in WEIGHT_ORDER])
```

```python
import functools

import jax
import jax.numpy as jnp
import numpy as np
from jax import lax
from jax.experimental import pallas as pl
from jax.experimental.pallas import tpu as pltpu

F32, BF16 = jnp.float32, jnp.bfloat16
MESH = pl.DeviceIdType.MESH

D_MODEL = 1024
MEM_LEN = 256
SB_HEADS, SB_DIM, SB_WIDTH = 8, 64, 512
RET_HEADS, RET_QK, RET_V = 4, 128, 256
RET_QK_WIDTH, RET_V_WIDTH = 512, 1024
ROPE_BASE = 10000.0
MEM_HEADS, MEM_DIM = 4, 256
FFN_HIDDEN = 2816
IN_WIDTH = 6656
OFF_RET_Q, OFF_RET_V, OFF_RET_G, OFF_GATE = 1536, 2560, 3584, 4608
DN_ALPHA = 2.0 ** 0.25
LN_EPS = 1e-5
SB_SCALE = SB_DIM ** -0.5
SB_DEAD = -110.0
RET_SCALE = RET_QK ** -0.5
MEM_SCALE = MEM_DIM ** -0.5
ADAM_LR, ADAM_B1, ADAM_B2, ADAM_EPS, ADAM_WD, ADAM_STEP = 0.001, 0.9, 0.999, 1e-08, 0.01, 10

N_DEV, N_CHIPS = 8, 4

LANES = 128
MXU_COLS = 256
VMEM_LIMIT_BYTES = 52 * 2 ** 20
ROW_TILE = 512
WIDE_TILE = 1024
SEQ_TILE = 2048
SB_BLOCK = 256
RET_BLOCK = 256
RET_CHUNKS_PER_STEP = 2
XATTN_ROWS = 1024

BIG = (
    ("w_in", (D_MODEL, IN_WIDTH), 1),
    ("w_sb_o", (SB_WIDTH, D_MODEL), 1),
    ("w_ret_o", (RET_V_WIDTH, D_MODEL), 0),
    ("w_mix_o", (D_MODEL, D_MODEL), 0),
    ("w_mem_q", (D_MODEL, D_MODEL), 0),
    ("w_mem_kv", (D_MODEL, 2 * D_MODEL), 1),
    ("w_mem_o", (D_MODEL, D_MODEL), 0),
    ("w_ffn_in", (D_MODEL, 2 * FFN_HIDDEN), 1),
    ("w_ffn_out", (FFN_HIDDEN, D_MODEL), 0),
)
SMALL = ("b_gate", "ln1_g", "ln1_b", "ln2_g", "ln2_b", "ln3_g", "ln3_b")
SMALL_LEN = 2 * D_MODEL + 6 * D_MODEL
SMALL_ROWS = SMALL_LEN // LANES
PACK_ROWS = SMALL_ROWS + D_MODEL // LANES
WEIGHT_ORDER = ("w_in", "b_gate", "w_sb_o", "w_ret_o", "w_mix_o", "ln1_g", "ln1_b", "w_mem_q", "w_mem_kv",
                "w_mem_o", "ln2_g", "ln2_b", "w_ffn_in", "w_ffn_out", "ln3_g", "ln3_b")


def _cparams():
    return pltpu.CompilerParams(vmem_limit_bytes=VMEM_LIMIT_BYTES)


def _dot(a, b, ca, cb):
    return lax.dot_general(a, b, (((ca,), (cb,)), ((), ())), preferred_element_type=F32)


def _sigmoid(x):
    return 1.0 / (1.0 + jnp.exp(-x))


def _mm(name, a, b, m, n, k, *, tm, tn, tk, epi, outs, ins=(), accs=(), ta=False, tb=False,
        a_off=(0, 0), b_off=(0, 0), j_outer=False, comm=None, chunk=None):
    assert m % tm == 0 and n % tn == 0 and k % tk == 0, (name, m, n, k, tm, tn, tk)
    assert chunk is None or (k == tk and tn % chunk == 0), name
    ni, nj, nk = m // tm, n // tn, k // tk
    assert not accs or nj == 1, name
    ij = (lambda g0, g1: (g1, g0)) if j_outer else (lambda g0, g1: (g0, g1))

    def spec(block, index):
        return pl.BlockSpec(block, lambda g0, g1, kk: index(*ij(g0, g1), kk))

    if ta:
        a_spec = spec((tk, tm), lambda i, j, kk: (kk + a_off[0], i + a_off[1]))
    else:
        a_spec = spec((tm, tk), lambda i, j, kk: (i + a_off[0], kk + a_off[1]))
    if tb:
        b_spec = spec((tn, tk), lambda i, j, kk: (j + b_off[0], kk + b_off[1]))
    else:
        b_spec = spec((tk, tn), lambda i, j, kk: (kk + b_off[0], j + b_off[1]))
    in_specs = [a_spec, b_spec]
    for _, bs, im in ins:
        in_specs.append(spec(bs, lambda i, j, kk, im=im: im(i, j)))
    out_specs, out_shape = [], []
    for shape, dtype, bs, im in outs:
        out_specs.append(spec(bs, lambda i, j, kk, im=im: im(i, j)))
        out_shape.append(jax.ShapeDtypeStruct(shape, dtype))
    for shape, dtype in accs:
        out_specs.append(spec(shape, lambda i, j, kk, nd=len(shape): (0,) * nd))
        out_shape.append(jax.ShapeDtypeStruct(shape, dtype))
    n_in, n_out, n_acc = len(ins), len(outs), len(accs)
    ca, cb = (0 if ta else 1), (1 if tb else 0)
    grid = (*ij(ni, nj), nk)
    comm_ins, comm_outs, comm_scratch = [], [], []
    if comm is not None:
        comm_in_specs, comm_out_specs = comm.specs
        comm_ins, comm_outs, comm_scratch = list(comm.ins), list(comm.out_shape), list(comm.scratch)
        in_specs += comm_in_specs
        out_specs += comm_out_specs
        out_shape += comm_outs
    n_ci, n_co = len(comm_ins), len(comm_outs)

    def body(*refs):
        a_ref, b_ref = refs[:2]
        in_refs = refs[2:2 + n_in]
        ci_refs = refs[2 + n_in:2 + n_in + n_ci]
        rest = refs[2 + n_in + n_ci:]
        out_refs, acc_refs = rest[:n_out], rest[n_out:n_out + n_acc]
        co_refs = rest[n_out + n_acc:n_out + n_acc + n_co]
        scratch = rest[n_out + n_acc + n_co:]
        sem_refs, scratch = scratch[:len(comm_scratch)], scratch[len(comm_scratch):]
        (i, j), kk = ij(pl.program_id(0), pl.program_id(1)), pl.program_id(2)
        if comm is not None:
            first_step, last_step = _grid_ends(grid)
            pl.when(first_step)(lambda: comm.start(ci_refs, co_refs, sem_refs))
        def finish(acc, cols=slice(None)):
            def of(r):
                return r[..., cols] if r.shape[-1] == tn else r[...]

            o_tiles, a_tiles = epi(acc, [of(r) for r in in_refs], i, j)
            for r, t in zip(out_refs, o_tiles, strict=True):
                r[..., cols] = t.astype(r.dtype)
            if n_acc:
                @pl.when(i == 0)
                def _():
                    for r, t in zip(acc_refs, a_tiles, strict=True):
                        r[..., cols] = t

                @pl.when(i > 0)
                def _():
                    for r, t in zip(acc_refs, a_tiles, strict=True):
                        r[..., cols] += t

        if chunk is not None:
            a_tile = a_ref[...].astype(BF16)
            for c0 in range(0, tn, chunk):
                cols = slice(c0, c0 + chunk)
                b_part = b_ref[cols, :] if tb else b_ref[:, cols]
                finish(_dot(a_tile, b_part.astype(BF16), ca, cb), cols)
            if comm is not None:
                pl.when(last_step)(lambda: comm.finish(ci_refs, co_refs, sem_refs))
            return

        part = _dot(a_ref[...].astype(BF16), b_ref[...].astype(BF16), ca, cb)
        if nk == 1:
            finish(part)
        else:
            acc_ref = scratch[0]

            @pl.when(kk == 0)
            def _():
                acc_ref[...] = part

            @pl.when(kk > 0)
            def _():
                acc_ref[...] += part

            @pl.when(kk == nk - 1)
            def _():
                finish(acc_ref[...])

        if comm is not None:
            pl.when(last_step)(lambda: comm.finish(ci_refs, co_refs, sem_refs))

    res = pl.pallas_call(
        body, name=name, grid=grid, in_specs=in_specs, out_specs=out_specs, out_shape=out_shape,
        scratch_shapes=comm_scratch + ([pltpu.VMEM((tm, tn), F32)] if nk > 1 else []),
        compiler_params=_cparams(),
    )(a, b, *[x for x, _, _ in ins], *comm_ins)
    return res


def _grid_ends(grid):
    ids = [pl.program_id(ax) for ax in range(len(grid))]
    first = functools.reduce(jnp.logical_and, [p == 0 for p in ids])
    last = functools.reduce(jnp.logical_and, [p == n - 1 for p, n in zip(ids, grid, strict=True)])
    return first, last


def _tile(tm, tn, dj=0):
    return (tm, tn), (lambda i, j: (i, j + dj))


def _rowvec(tn, dj=0):
    return (1, tn), (lambda i, j: (0, j + dj))


def _plain(acc, tiles, i, j):
    return [acc], []


def _ew(name, fn, ins, outs, rows, tr):
    assert rows % tr == 0, (name, rows, tr)
    in_specs = []
    for x in ins:
        if x.shape[0] == rows:
            in_specs.append(pl.BlockSpec((tr, x.shape[1]), lambda i: (i, 0)))
        else:
            in_specs.append(pl.BlockSpec(x.shape, lambda i: (0, 0)))
    n_in = len(ins)

    def body(*refs):
        res = fn(*[r[...] for r in refs[:n_in]])
        for r, t in zip(refs[n_in:], res, strict=True):
            r[...] = t.astype(r.dtype)

    return pl.pallas_call(
        body, name=name, grid=(rows // tr,), in_specs=in_specs,
        out_specs=[pl.BlockSpec((tr, w), lambda i: (i, 0)) for w, _ in outs],
        out_shape=[jax.ShapeDtypeStruct((rows, w), dt) for w, dt in outs],
        compiler_params=_cparams(),
    )(*ins)


def _cast_bf16(name, x):
    rows = x.shape[0]
    tr = next(t for t in (512, 256, 64) if rows % t == 0)
    return _ew(name, lambda v: (v,), [x], [(x.shape[1], BF16)], rows, tr)[0]


def _prep(x, comm):
    s = x.shape[0]
    half = RET_QK // 2
    inv = 1.0 / (ROPE_BASE ** (jnp.arange(half, dtype=F32) / half))
    inv2 = jnp.concatenate([inv, inv]).reshape(1, RET_QK)
    sign = jnp.concatenate([-jnp.ones((half,), F32), jnp.ones((half,), F32)]).reshape(1, RET_QK)
    tr = min(ROW_TILE, s)
    grid = (s // tr,)
    c_in_specs, c_out_specs, c_out_shape, c_scratch, c_ins, split = _host(comm, 3, 3)

    def body(*refs):
        (x_ref, inv_ref, sign_ref), (xb_ref, cos_ref, sin_ref), _, riding = split(refs)
        i = pl.program_id(0)
        first_step, last_step = _grid_ends(grid)
        pl.when(first_step)(lambda: comm.start(*riding))
        xb_ref[...] = x_ref[...].astype(BF16)
        pos = (lax.broadcasted_iota(jnp.int32, (tr, RET_QK), 0) + i * tr).astype(F32)
        ang = pos * inv_ref[...]
        cos_ref[...] = jnp.cos(ang)
        sin_ref[...] = jnp.sin(ang) * sign_ref[...]
        pl.when(last_step)(lambda: comm.finish(*riding))

    vec = pl.BlockSpec((1, RET_QK), lambda i: (0, 0))
    row = lambda w: pl.BlockSpec((tr, w), lambda i: (i, 0))
    return pl.pallas_call(
        body, name="prep", grid=grid,
        in_specs=[row(D_MODEL), vec, vec] + c_in_specs,
        out_specs=[row(D_MODEL), row(RET_QK), row(RET_QK)] + c_out_specs,
        out_shape=[jax.ShapeDtypeStruct((s, D_MODEL), BF16), jax.ShapeDtypeStruct((s, RET_QK), F32),
                   jax.ShapeDtypeStruct((s, RET_QK), F32)] + c_out_shape,
        scratch_shapes=c_scratch, compiler_params=_cparams(),
    )(x, inv2, sign, *c_ins)


def _swap_halves(x):
    return pltpu.roll(x, RET_QK // 2, 1)


def _norm(u):
    mu = jnp.mean(u, axis=-1, keepdims=True)
    d = u - mu
    var = jnp.mean(d * d, axis=-1, keepdims=True)
    rstd = lax.rsqrt(var + LN_EPS)
    return d * rstd, rstd


def _norm_bwd(dxh, xhat, rstd):
    m1 = jnp.mean(dxh, axis=-1, keepdims=True)
    m2 = jnp.mean(dxh * xhat, axis=-1, keepdims=True)
    return rstd * (dxh - m1 - xhat * m2)


def _colsum(t):
    return jnp.sum(t, axis=0, keepdims=True)


def _split_mm(t, tri):
    hi = t.astype(BF16)
    lo = (t - hi.astype(F32)).astype(BF16)
    return _dot(hi, tri, 1, 0) + _dot(lo, tri, 1, 0)


def _sb_masks():
    t = SB_BLOCK
    lane = lax.broadcasted_iota(jnp.int32, (1, LANES), 1)
    first = lane < SB_DIM
    m0 = jnp.where(first, 1.0, 0.0).astype(BF16)
    m1 = jnp.where(first, 0.0, 1.0).astype(BF16)
    row = lax.broadcasted_iota(jnp.int32, (t, t), 0)
    col = lax.broadcasted_iota(jnp.int32, (t, t), 1)
    return first, (m0, m1), row, col


def _sb_logits(qh, k, causal):
    z = _dot(qh, k, 1, 1)
    lp = jnp.log(1.0 + jnp.exp(-jnp.abs(z)))
    a = jnp.minimum(z, 0.0) - lp
    r = jnp.minimum(-z, 0.0) - lp
    if causal is not None:
        r = jnp.where(causal, r, 0.0)
    return a, r


def _sb_walk(i, blocks, l_ref, causal):
    pl.when(i == 0)(lambda: blocks([(i, causal)]))
    pl.when(i > 0)(lambda: blocks([(i, causal), (i - 1, None)]))

    def alive():
        top = jnp.max(functools.reduce(jnp.maximum, [l_ref[c] for c in range(l_ref.shape[0])]))
        return jnp.where(top > SB_DEAD, 1, 0)

    def cond(c):
        return jnp.logical_and(c[0] < i, c[1] > 0)

    def step(c):
        blocks([(i - 1 - c[0], None)])
        return c[0] + 1, alive()

    lax.while_loop(cond, step, (jnp.int32(1), alive()))


def _host(comm, n_in, n_out):
    if comm is None:
        return [], [], [], [], [], lambda refs: (refs[:n_in], refs[n_in:n_in + n_out], refs[n_in + n_out:], None)
    in_specs, out_specs = comm.specs
    n_ci, n_co, n_sem = len(comm.ins), len(comm.out_shape), len(comm.scratch)

    def split(refs):
        ins, ci = refs[:n_in], refs[n_in:n_in + n_ci]
        rest = refs[n_in + n_ci:]
        outs, co = rest[:n_out], rest[n_out:n_out + n_co]
        sems, scratch = rest[n_out + n_co:n_out + n_co + n_sem], rest[n_out + n_co + n_sem:]
        return ins, outs, scratch, (ci, co, sems)

    return in_specs, out_specs, list(comm.out_shape), list(comm.scratch), list(comm.ins), split


def _sb_qkv_specs(s, g):
    groups = SB_HEADS // 2 // g
    return [pl.BlockSpec((g, SB_BLOCK, LANES), lambda p, i: (p, i, 0)),
            pl.BlockSpec((g, s, LANES), lambda p, i: (groups + p, 0, 0)),
            pl.BlockSpec((g, s, LANES), lambda p, i: (2 * groups + p, 0, 0))]


def _sb_fwd(qkv, s, comm=None):
    t = SB_BLOCK
    g = 2
    nq = s // t
    grid = (SB_HEADS // 2 // g, nq)
    c_in_specs, c_out_specs, c_out_shape, c_scratch, c_ins, split = _host(comm, 3, 2)

    def body(*refs):
        (q_ref, k_ref, v_ref), (o_ref, of_ref), (l_ref, acc_ref), riding = split(refs)
        i = pl.program_id(1)
        if comm is not None:
            first_step, last_step = _grid_ends(grid)
            pl.when(first_step)(lambda: comm.start(*riding))
        first, hmask, row, col = _sb_masks()
        after = jnp.where(row > col, 1.0, 0.0).astype(BF16)
        causal = col < row
        heads = [(p, h) for p in range(g) for h in range(2)]
        qh = {(p, h): q_ref[p] * hmask[h] for p, h in heads}
        l_ref[...] = jnp.zeros_like(l_ref)
        acc_ref[...] = jnp.zeros_like(acc_ref)

        def blocks(todo):
            chains = [(b, p, h) for b in range(len(todo)) for p, h in heads]
            starts = [pl.multiple_of(kb * t, t) for kb, _ in todo]
            ks = {(b, p): k_ref[p, pl.ds(st, t), :] for b, st in enumerate(starts) for p in range(g)}
            vs = {(b, p): v_ref[p, pl.ds(st, t), :] for b, st in enumerate(starts) for p in range(g)}
            ar = {(b, p, h): _sb_logits(qh[p, h], ks[b, p], todo[b][1]) for b, p, h in chains}
            later = {c: _split_mm(ar[c][1], after) for c in chains}
            carry = {(p, h): l_ref[2 * p + h] for p, h in heads}
            w = {}
            for b, (_, mask) in enumerate(todo):
                for p, h in heads:
                    wc = jnp.exp(ar[b, p, h][0] + later[b, p, h] + carry[p, h])
                    w[b, p, h] = wc if mask is None else jnp.where(mask, wc, 0.0)
                carry = {(p, h): carry[p, h] + jnp.sum(ar[b, p, h][1], axis=1, keepdims=True) for p, h in heads}
            pv = {(b, p, h): _dot(w[b, p, h].astype(BF16), vs[b, p], 1, 0) for b, p, h in chains}
            for p in range(g):
                lanes = slice(p * LANES, (p + 1) * LANES)
                acc = acc_ref[:, lanes]
                for b in range(len(todo)):
                    acc = acc + jnp.where(first, pv[b, p, 0], pv[b, p, 1])
                acc_ref[:, lanes] = acc
            for p, h in heads:
                l_ref[2 * p + h] = carry[p, h]

        _sb_walk(i, blocks, l_ref, causal)
        o_ref[...] = acc_ref[...].astype(o_ref.dtype)
        of_ref[...] = acc_ref[...]
        if comm is not None:
            pl.when(last_step)(lambda: comm.finish(*riding))

    blk = pl.BlockSpec((t, g * LANES), lambda p, i: (i, p))
    return pl.pallas_call(
        body, name="sb_fwd", grid=grid,
        in_specs=_sb_qkv_specs(s, g) + c_in_specs,
        out_specs=[blk, blk] + c_out_specs,
        out_shape=[jax.ShapeDtypeStruct((s, SB_WIDTH), BF16), jax.ShapeDtypeStruct((s, SB_WIDTH), F32)] + c_out_shape,
        scratch_shapes=c_scratch + [pltpu.VMEM((2 * g, t, 1), F32), pltpu.VMEM((t, g * LANES), F32)],
        compiler_params=_cparams(),
    )(qkv, qkv, qkv, *c_ins)


def _sb_bwd(qkv, o, do, s, comm=None):
    t = SB_BLOCK
    g = 2
    nq = s // t
    grid = (SB_HEADS // 2 // g, nq)
    c_in_specs, c_out_specs, c_out_shape, c_scratch, c_ins, split = _host(comm, 5, 3)

    def body(*refs):
        ((q_ref, k_ref, v_ref, o_ref, do_ref), (dq_ref, dk_ref, dv_ref),
         (l_ref, e_ref, dq_acc, dk_acc, dv_acc), riding) = split(refs)
        i = pl.program_id(1)
        if comm is not None:
            first_step, last_step = _grid_ends(grid)
            pl.when(first_step)(lambda: comm.start(*riding))
        first, hmask, row, col = _sb_masks()
        after = jnp.where(row > col, 1.0, 0.0).astype(BF16)
        from_here = jnp.where(row >= col, 1.0, 0.0).astype(BF16)
        causal = col < row

        @pl.when(i == 0)
        def _():
            dk_acc[...] = jnp.zeros_like(dk_acc)
            dv_acc[...] = jnp.zeros_like(dv_acc)

        heads = [(p, h) for p in range(g) for h in range(2)]
        lanes = [slice(p * LANES, (p + 1) * LANES) for p in range(g)]
        q = [q_ref[p] for p in range(g)]
        do_ = [do_ref[:, lanes[p]] for p in range(g)]
        qh = {(p, h): q[p] * hmask[h] for p, h in heads}
        doh = {(p, h): do_[p] * hmask[h] for p, h in heads}
        total = {}
        for p in range(g):
            prod = do_[p].astype(F32) * o_ref[:, lanes[p]]
            total[p, 0] = jnp.sum(jnp.where(first, prod, 0.0), axis=1, keepdims=True)
            total[p, 1] = jnp.sum(jnp.where(first, 0.0, prod), axis=1, keepdims=True)
        l_ref[...] = jnp.zeros_like(l_ref)
        e_ref[...] = jnp.zeros_like(e_ref)
        dq_acc[...] = jnp.zeros_like(dq_acc)

        def blocks(todo):
            chains = [(b, p, h) for b in range(len(todo)) for p, h in heads]
            starts = [pl.multiple_of(kb * t, t) for kb, _ in todo]
            ks = {(b, p): k_ref[p, pl.ds(st, t), :] for b, st in enumerate(starts) for p in range(g)}
            vs = {(b, p): v_ref[p, pl.ds(st, t), :] for b, st in enumerate(starts) for p in range(g)}
            ar = {(b, p, h): _sb_logits(qh[p, h], ks[b, p], todo[b][1]) for b, p, h in chains}
            dw = {(b, p, h): _dot(doh[p, h], vs[b, p], 1, 1) for b, p, h in chains}
            later = {c: _split_mm(ar[c][1], after) for c in chains}
            carry = {(p, h): l_ref[2 * p + h] for p, h in heads}
            wb = {}
            for b, (_, mask) in enumerate(todo):
                for p, h in heads:
                    wc = jnp.exp(ar[b, p, h][0] + later[b, p, h] + carry[p, h])
                    wb[b, p, h] = (wc if mask is None else jnp.where(mask, wc, 0.0)).astype(BF16)
                carry = {(p, h): carry[p, h] + jnp.sum(ar[b, p, h][1], axis=1, keepdims=True) for p, h in heads}
            dvs = {(b, p, h): _dot(wb[b, p, h], do_[p], 0, 0) for b, p, h in chains}
            e = {c: dw[c] * wb[c].astype(F32) for c in chains}
            suffix = {c: _split_mm(e[c], from_here) for c in chains}
            e_carry = {(p, h): e_ref[2 * p + h] for p, h in heads}
            dz = {}
            for b, (_, mask) in enumerate(todo):
                for p, h in heads:
                    before = total[p, h] - (suffix[b, p, h] + e_carry[p, h])
                    dzc = e[b, p, h] - jnp.exp(ar[b, p, h][0]) * (e[b, p, h] + before)
                    dz[b, p, h] = (dzc if mask is None else jnp.where(mask, dzc, 0.0)).astype(BF16)
                e_carry = {(p, h): e_carry[p, h] + jnp.sum(e[b, p, h], axis=1, keepdims=True) for p, h in heads}
            dqs = {(b, p, h): _dot(dz[b, p, h], ks[b, p], 1, 0) for b, p, h in chains}
            dks = {(b, p, h): _dot(dz[b, p, h], q[p], 0, 0) for b, p, h in chains}
            for p in range(g):
                dq = dq_acc[:, lanes[p]]
                for b, st in enumerate(starts):
                    dq = dq + jnp.where(first, dqs[b, p, 0], dqs[b, p, 1])
                    dk_acc[pl.ds(st, t), lanes[p]] += jnp.where(first, dks[b, p, 0], dks[b, p, 1])
                    dv_acc[pl.ds(st, t), lanes[p]] += jnp.where(first, dvs[b, p, 0], dvs[b, p, 1])
                dq_acc[:, lanes[p]] = dq
            for p, h in heads:
                l_ref[2 * p + h] = carry[p, h]
                e_ref[2 * p + h] = e_carry[p, h]

        _sb_walk(i, blocks, l_ref, causal)
        dq_ref[...] = (dq_acc[...] * SB_SCALE).astype(dq_ref.dtype)

        @pl.when(i == nq - 1)
        def _():
            dk_ref[...] = dk_acc[...].astype(dk_ref.dtype)
            dv_ref[...] = dv_acc[...].astype(dv_ref.dtype)

        if comm is not None:
            pl.when(last_step)(lambda: comm.finish(*riding))

    once = pl.Buffered(1)
    q_spec, k_spec, v_spec = _sb_qkv_specs(s, g)
    k_spec = pl.BlockSpec(k_spec.block_shape, k_spec.index_map, pipeline_mode=once)
    v_spec = pl.BlockSpec(v_spec.block_shape, v_spec.index_map, pipeline_mode=once)
    blk = pl.BlockSpec((t, g * LANES), lambda p, i: (i, p))
    col_blk = pl.BlockSpec((s, g * LANES), lambda p, i: (0, p), pipeline_mode=once)
    sds = jax.ShapeDtypeStruct((s, SB_WIDTH), BF16)
    return pl.pallas_call(
        body, name="sb_bwd", grid=grid,
        in_specs=[q_spec, k_spec, v_spec, blk, blk] + c_in_specs,
        out_specs=[blk, col_blk, col_blk] + c_out_specs,
        out_shape=[sds, sds, sds] + c_out_shape,
        scratch_shapes=c_scratch + [pltpu.VMEM((2 * g, t, 1), F32), pltpu.VMEM((2 * g, t, 1), F32),
                                    pltpu.VMEM((t, g * LANES), F32), pltpu.VMEM((s, g * LANES), F32),
                                    pltpu.VMEM((s, g * LANES), F32)],
        compiler_params=_cparams(),
    )(qkv, qkv, qkv, o, do, *c_ins)


def _ret_log_gamma():
    lg = np.log1p(-np.exp2(-5.0 - np.arange(RET_HEADS, dtype=np.float32))).astype(np.float32)
    return jnp.asarray(np.broadcast_to(lg[:, None, None], (RET_HEADS, 8, LANES)).copy())


RET_SCRATCH = [pltpu.VMEM((RET_HEADS, RET_QK, RET_V), F32),
               pltpu.VMEM((RET_HEADS, RET_BLOCK, RET_BLOCK), F32),
               pltpu.VMEM((RET_HEADS, RET_BLOCK, 1), F32),
               pltpu.VMEM((RET_HEADS, RET_BLOCK, 1), F32)]


def _ret_begin(n, lg_ref, state, within, q_dec, k_dec):
    @pl.when(n == 0)
    def _():
        c = RET_BLOCK
        state[...] = jnp.zeros_like(state)
        row = lax.broadcasted_iota(jnp.int32, (c, c), 0)
        col = lax.broadcasted_iota(jnp.int32, (c, c), 1)
        rel = jnp.maximum(row - col, 0).astype(F32)
        idx = lax.broadcasted_iota(jnp.int32, (c, 1), 0).astype(F32)
        for h in range(RET_HEADS):
            lg = lg_ref[h, 0:1, 0:1]
            within[h] = jnp.where(row >= col, jnp.exp(lg * rel), 0.0)
            q_dec[h] = jnp.exp(lg * (idx + 1.0))
            k_dec[h] = jnp.exp(lg * (c - 1.0 - idx))


def _chunk_decay(lg_ref, h):
    return jnp.exp(lg_ref[h, 0:1, 0:1] * float(RET_BLOCK))


def _ret_heads(x, width):
    return [x[:, h * width:(h + 1) * width] for h in range(RET_HEADS)]


def _ret_specs(s, reverse=False):
    c = RET_BLOCK
    rows = c * RET_CHUNKS_PER_STEP
    nc = s // rows
    pos = (lambda n: nc - 1 - n) if reverse else (lambda n: n)
    chunks = [slice(u * c, (u + 1) * c) for u in range(RET_CHUNKS_PER_STEP)]
    q_spec = pl.BlockSpec((rows, RET_QK_WIDTH), lambda n: (pos(n), 0))
    k_spec = pl.BlockSpec((rows, RET_QK_WIDTH), lambda n: (pos(n), 1))
    v_spec = pl.BlockSpec((rows, RET_V_WIDTH), lambda n: (pos(n), 0))
    lg_spec = pl.BlockSpec((RET_HEADS, 8, LANES), lambda n: (0, 0, 0))
    rope_spec = pl.BlockSpec((rows, RET_QK), lambda n: (pos(n), 0))
    return nc, chunks[::-1] if reverse else chunks, q_spec, k_spec, v_spec, lg_spec, rope_spec


def _ret_fwd(rqk, rvg, s):
    nc, chunks, q_spec, k_spec, v_spec, lg_spec, _ = _ret_specs(s)
    g_spec = pl.BlockSpec(v_spec.block_shape, lambda n: (n, 1))
    heads = range(RET_HEADS)

    def body(q_ref, k_ref, v_ref, g_ref, lg_ref, r_ref, y_ref, state, within, q_dec, k_dec):
        n = pl.program_id(0)
        _ret_begin(n, lg_ref, state, within, q_dec, k_dec)
        for rows in chunks:
            q, k = _ret_heads(q_ref[rows], RET_QK), _ret_heads(k_ref[rows], RET_QK)
            v, g = _ret_heads(v_ref[rows], RET_V), _ret_heads(g_ref[rows], RET_V)
            scores = [_dot(q[h].astype(BF16), k[h].astype(BF16), 1, 1) * within[h] for h in heads]
            cross = [_dot((q[h] * q_dec[h]).astype(BF16), state[h].astype(BF16), 1, 0) for h in heads]
            out = [_dot(scores[h].astype(BF16), v[h], 1, 0) + cross[h] for h in heads]
            grown = [_dot((k[h] * k_dec[h]).astype(BF16), v[h], 0, 0) for h in heads]
            for h in heads:
                sl = slice(h * RET_V, (h + 1) * RET_V)
                r_ref[rows, sl] = out[h]
                xhat, _ = _norm(out[h])
                gh = g[h].astype(F32)
                y_ref[rows, sl] = (gh * _sigmoid(gh) * xhat).astype(y_ref.dtype)
                state[h] = state[h] * _chunk_decay(lg_ref, h) + grown[h]

    return pl.pallas_call(
        body, name="ret_fwd", grid=(nc,),
        in_specs=[q_spec, k_spec, v_spec, g_spec, lg_spec],
        out_specs=[v_spec, v_spec],
        out_shape=[jax.ShapeDtypeStruct((s, RET_V_WIDTH), F32), jax.ShapeDtypeStruct((s, RET_V_WIDTH), BF16)],
        scratch_shapes=RET_SCRATCH,
        compiler_params=_cparams(),
    )(rqk, rqk, rvg, rvg, _ret_log_gamma())


def _rope_bwd(d, cos, sin):
    return d * cos + _swap_halves(d * sin)


def _ret_bwd_q(rqk, rv, d_out, cos2, sin2, s):
    nc, chunks, q_spec, k_spec, v_spec, lg_spec, rope_spec = _ret_specs(s)
    heads = range(RET_HEADS)

    def body(k_ref, v_ref, d_ref, lg_ref, cos_ref, sin_ref, dq_ref, state, within, q_dec, k_dec):
        n = pl.program_id(0)
        _ret_begin(n, lg_ref, state, within, q_dec, k_dec)
        for rows in chunks:
            k = _ret_heads(k_ref[rows], RET_QK)
            v, d = _ret_heads(v_ref[rows], RET_V), _ret_heads(d_ref[rows], RET_V)
            cos, sin = cos_ref[rows], sin_ref[rows]
            d_scores = [_dot(d[h], v[h], 1, 1) * within[h] for h in heads]
            cross = [q_dec[h] * _dot(d[h], state[h].astype(BF16), 1, 1) for h in heads]
            dq = [_dot(d_scores[h].astype(BF16), k[h].astype(BF16), 1, 0) + cross[h] for h in heads]
            grown = [_dot((k[h] * k_dec[h]).astype(BF16), v[h], 0, 0) for h in heads]
            for h in heads:
                sl = slice(h * RET_QK, (h + 1) * RET_QK)
                dq_ref[rows, sl] = (_rope_bwd(dq[h], cos, sin) * RET_SCALE).astype(dq_ref.dtype)
                state[h] = state[h] * _chunk_decay(lg_ref, h) + grown[h]

    return pl.pallas_call(
        body, name="ret_bwd_q", grid=(nc,),
        in_specs=[k_spec, v_spec, v_spec, lg_spec, rope_spec, rope_spec],
        out_specs=q_spec,
        out_shape=jax.ShapeDtypeStruct((s, RET_QK_WIDTH), BF16),
        scratch_shapes=RET_SCRATCH,
        compiler_params=_cparams(),
    )(rqk, rv, d_out, _ret_log_gamma(), cos2, sin2)


def _ret_bwd_kv(rqk, rv, d_out, cos2, sin2, s):
    nc, chunks, q_spec, k_spec, v_spec, lg_spec, rope_spec = _ret_specs(s, reverse=True)
    heads = range(RET_HEADS)

    def body(q_ref, k_ref, v_ref, d_ref, lg_ref, cos_ref, sin_ref, dk_ref, dv_ref, state, within, q_dec, k_dec):
        n = pl.program_id(0)
        _ret_begin(n, lg_ref, state, within, q_dec, k_dec)
        for rows in chunks:
            q, k = _ret_heads(q_ref[rows], RET_QK), _ret_heads(k_ref[rows], RET_QK)
            v, d = _ret_heads(v_ref[rows], RET_V), _ret_heads(d_ref[rows], RET_V)
            cos, sin = cos_ref[rows], sin_ref[rows]
            qb, kb = [q[h].astype(BF16) for h in heads], [k[h].astype(BF16) for h in heads]
            st = [state[h].astype(BF16) for h in heads]
            scores = [_dot(qb[h], kb[h], 1, 1) * within[h] for h in heads]
            d_scores = [_dot(d[h], v[h], 1, 1) * within[h] for h in heads]
            dk = [_dot(d_scores[h].astype(BF16), qb[h], 0, 0) + k_dec[h] * _dot(v[h], st[h], 1, 1) for h in heads]
            dv = [_dot(scores[h].astype(BF16), d[h], 0, 0) + k_dec[h] * _dot(kb[h], st[h], 1, 0) for h in heads]
            grown = [_dot((q[h] * q_dec[h]).astype(BF16), d[h], 0, 0) for h in heads]
            for h in heads:
                dk_ref[rows, h * RET_QK:(h + 1) * RET_QK] = _rope_bwd(dk[h], cos, sin).astype(dk_ref.dtype)
                dv_ref[rows, h * RET_V:(h + 1) * RET_V] = dv[h].astype(dv_ref.dtype)
                state[h] = state[h] * _chunk_decay(lg_ref, h) + grown[h]

    return pl.pallas_call(
        body, name="ret_bwd_kv", grid=(nc,),
        in_specs=[q_spec, k_spec, v_spec, v_spec, lg_spec, rope_spec, rope_spec],
        out_specs=[q_spec, v_spec],
        out_shape=[jax.ShapeDtypeStruct((s, RET_QK_WIDTH), BF16), jax.ShapeDtypeStruct((s, RET_V_WIDTH), BF16)],
        scratch_shapes=RET_SCRATCH,
        compiler_params=_cparams(),
    )(rqk, rqk, rv, d_out, _ret_log_gamma(), cos2, sin2)


def _xattn_probs(scores):
    sc = scores - jnp.max(scores, axis=-1, keepdims=True)
    p = jnp.exp(sc)
    return p / jnp.sum(p, axis=-1, keepdims=True)


def _xattn_heads(q_ref, kv_ref):
    sls = [slice(h * MEM_DIM, (h + 1) * MEM_DIM) for h in range(MEM_HEADS)]
    q = [q_ref[:, sl] for sl in sls]
    k = [kv_ref[:, sl] for sl in sls]
    v = [kv_ref[:, D_MODEL + h * MEM_DIM:D_MODEL + (h + 1) * MEM_DIM] for h in range(MEM_HEADS)]
    return sls, q, k, v


def _xattn_fwd(qm, kv, s):
    tq = min(XATTN_ROWS, s)
    heads = range(MEM_HEADS)

    def body(q_ref, kv_ref, o_ref):
        sls, q, k, v = _xattn_heads(q_ref, kv_ref)
        scores = [_dot(q[h], k[h], 1, 1) for h in heads]
        p = [_xattn_probs(scores[h]).astype(BF16) for h in heads]
        out = [_dot(p[h], v[h], 1, 0) for h in heads]
        for h in heads:
            o_ref[:, sls[h]] = out[h].astype(o_ref.dtype)

    return pl.pallas_call(
        body, name="xattn_fwd", grid=(s // tq,),
        in_specs=[pl.BlockSpec((tq, D_MODEL), lambda i: (i, 0)),
                  pl.BlockSpec((MEM_LEN, 2 * D_MODEL), lambda i: (0, 0))],
        out_specs=pl.BlockSpec((tq, D_MODEL), lambda i: (i, 0)),
        out_shape=jax.ShapeDtypeStruct((s, D_MODEL), BF16),
        compiler_params=_cparams(),
    )(qm, kv)


def _xattn_bwd(qm, kv, do, s):
    tq = min(XATTN_ROWS, s)

    def body(q_ref, kv_ref, do_ref, dq_ref, dkv_ref):
        i = pl.program_id(0)

        @pl.when(i == 0)
        def _():
            dkv_ref[...] = jnp.zeros_like(dkv_ref)

        heads = range(MEM_HEADS)
        sls, q, k, v = _xattn_heads(q_ref, kv_ref)
        d = [do_ref[:, sl] for sl in sls]
        scores = [_dot(q[h], k[h], 1, 1) for h in heads]
        dp = [_dot(d[h], v[h], 1, 1) for h in heads]
        p = [_xattn_probs(scores[h]) for h in heads]
        ds = [(p[h] * (dp[h] - jnp.sum(p[h] * dp[h], axis=-1, keepdims=True))).astype(BF16) for h in heads]
        dq = [_dot(ds[h], k[h], 1, 0) for h in heads]
        dk = [_dot(ds[h], q[h], 0, 0) for h in heads]
        dv = [_dot(p[h].astype(BF16), d[h], 0, 0) for h in heads]
        for h in heads:
            dq_ref[:, sls[h]] = (dq[h] * MEM_SCALE).astype(dq_ref.dtype)
            dkv_ref[:, sls[h]] += dk[h]
            dkv_ref[:, D_MODEL + h * MEM_DIM:D_MODEL + (h + 1) * MEM_DIM] += dv[h]

    row_blk = pl.BlockSpec((tq, D_MODEL), lambda i: (i, 0))
    kv_blk = pl.BlockSpec((MEM_LEN, 2 * D_MODEL), lambda i: (0, 0))
    return pl.pallas_call(
        body, name="xattn_bwd", grid=(s // tq,),
        in_specs=[row_blk, kv_blk, row_blk],
        out_specs=[row_blk, kv_blk],
        out_shape=[jax.ShapeDtypeStruct((s, D_MODEL), BF16), jax.ShapeDtypeStruct((MEM_LEN, 2 * D_MODEL), F32)],
        compiler_params=_cparams(),
    )(qm, kv, do)


def _place():
    x, y, c = lax.axis_index("x"), lax.axis_index("y"), lax.axis_index("c")
    others = [(1 - x, y), (x, 1 - y), (1 - x, 1 - y)]
    return x, y, c, others


def _slab(ref, axis, chip, size):
    start = pl.multiple_of(chip * size, LANES if axis == 1 else 16)
    if axis == 0:
        return ref.at[pl.ds(start, size), :]
    return ref.at[:, pl.ds(start, size)]


class _CommPlan:
    def __init__(self, ins, out_shape, scratch, start, finish):
        self.ins, self.out_shape, self.scratch, self.start, self.finish = ins, out_shape, scratch, start, finish

    @property
    def specs(self):
        any_spec = pl.BlockSpec(memory_space=pl.ANY)
        return [any_spec] * len(self.ins), [any_spec] * len(self.out_shape)


def _gather_plan(names, shards):
    spec = {name: (shape, axis) for name, shape, axis in BIG}
    nw = len(names)

    def shard_half(ref, c):
        rows = ref.shape[0] // 2
        return ref.at[pl.ds(pl.multiple_of(c * rows, 16), rows), :]

    def region(ref, w, chip, c):
        shape, axis = spec[names[w]]
        size = shape[axis] // N_CHIPS
        if axis == 0:
            rows = size // 2
            return ref.at[pl.ds(pl.multiple_of(chip * size + c * rows, 16), rows), :]
        rows = shape[0] // 2
        return ref.at[pl.ds(pl.multiple_of(c * rows, 16), rows), pl.ds(pl.multiple_of(chip * size, LANES), size)]

    def ops(shard, full, sems):
        ici_send, ici_recv, d2d_send, d2d_recv, local_sems = sems
        x, y, c, others = _place()
        mine, sibling = 2 * x + y, (x, y, 1 - c)
        local, over_ici, arrived, passed_on, from_sibling = [], [], [], [], []
        for w in range(nw):
            shape, axis = spec[names[w]]
            local.append(pltpu.make_async_copy(shard[w], _slab(full[w], axis, mine, shape[axis] // N_CHIPS),
                                               local_sems.at[w]))
            for t, (qx, qy) in enumerate(others):
                n, theirs = 3 * w + t, 2 * qx + qy
                over_ici.append(pltpu.make_async_remote_copy(
                    src_ref=shard_half(shard[w], c), dst_ref=region(full[w], w, mine, c),
                    send_sem=ici_send.at[n], recv_sem=ici_recv.at[n], device_id=(qx, qy, c), device_id_type=MESH))
                arrived.append(pltpu.make_async_remote_copy(
                    src_ref=shard_half(shard[w], c), dst_ref=region(full[w], w, theirs, c),
                    send_sem=ici_send.at[n], recv_sem=ici_recv.at[n], device_id=(qx, qy, c), device_id_type=MESH))
                passed_on.append(pltpu.make_async_remote_copy(
                    src_ref=region(full[w], w, theirs, c), dst_ref=region(full[w], w, theirs, c),
                    send_sem=d2d_send.at[n], recv_sem=d2d_recv.at[n], device_id=sibling, device_id_type=MESH))
                from_sibling.append(pltpu.make_async_remote_copy(
                    src_ref=region(full[w], w, theirs, c), dst_ref=region(full[w], w, theirs, 1 - c),
                    send_sem=d2d_send.at[n], recv_sem=d2d_recv.at[n], device_id=sibling, device_id_type=MESH))
        return local, over_ici, arrived, passed_on, from_sibling

    def start(shard, full, sems):
        local, over_ici, _, _, _ = ops(shard, full, sems)
        for cp in local + over_ici:
            cp.start()

    def finish(shard, full, sems):
        local, over_ici, arrived, passed_on, from_sibling = ops(shard, full, sems)
        for got, onward in zip(arrived, passed_on, strict=True):
            got.wait_recv()
            onward.start()
        for got in from_sibling:
            got.wait_recv()
        for cp in over_ici + passed_on:
            cp.wait_send()
        for cp in local:
            cp.wait()

    dma = pltpu.SemaphoreType.DMA
    return _CommPlan(
        ins=[shards[name] for name in names],
        out_shape=[jax.ShapeDtypeStruct(spec[name][0], BF16) for name in names],
        scratch=[dma((3 * nw,)), dma((3 * nw,)), dma((3 * nw,)), dma((3 * nw,)), dma((nw,))],
        start=start, finish=finish)


def _shard_shape(shape, axis):
    return tuple(d // N_CHIPS if a == axis else d for a, d in enumerate(shape))


def _exchange_plan(names, grads):
    spec = {name: (shape, axis) for name, shape, axis in BIG}
    nw = len(names)

    def ops(grad, stack, sems):
        send_sems, recv_sems, local_sems = sems
        x, y, c, others = _place()
        mine = 2 * x + y
        me, sibling = (x, y, c), (x, y, 1 - c)

        def dev(px, py, pc):
            return 4 * px + 2 * py + pc

        def copy(w, n, src, slot, to):
            return pltpu.make_async_remote_copy(
                src_ref=src, dst_ref=stack[w].at[slot], send_sem=send_sems.at[7 * w + n],
                recv_sem=recv_sems.at[7 * w + n], device_id=to, device_id_type=MESH)

        local, first, arrived, passed_on, from_sibling = [], [], [], [], []
        for w in range(nw):
            shape, axis = spec[names[w]]
            size = shape[axis] // N_CHIPS
            own = _slab(grad[w], axis, mine, size)
            local.append(pltpu.make_async_copy(own, stack[w].at[dev(*me)], local_sems.at[w]))
            first.append(copy(w, 0, own, dev(*me), sibling))
            from_sibling.append(copy(w, 0, own, dev(*sibling), me))
            for t, (qx, qy) in enumerate(others):
                got = stack[w].at[dev(qx, qy, c)]
                first.append(copy(w, 1 + t, _slab(grad[w], axis, 2 * qx + qy, size), dev(*me), (qx, qy, c)))
                arrived.append(copy(w, 1 + t, got, dev(qx, qy, c), me))
                passed_on.append(copy(w, 4 + t, got, dev(qx, qy, c), sibling))
                from_sibling.append(copy(w, 4 + t, got, dev(qx, qy, 1 - c), me))
        return local, first, arrived, passed_on, from_sibling

    def start(grad, stack, sems):
        local, first, _, _, _ = ops(grad, stack, sems)
        for cp in local + first:
            cp.start()

    def finish(grad, stack, sems):
        local, first, arrived, passed_on, from_sibling = ops(grad, stack, sems)
        for got, onward in zip(arrived, passed_on, strict=True):
            got.wait_recv()
            onward.start()
        for got in from_sibling:
            got.wait_recv()
        for cp in first + passed_on:
            cp.wait_send()
        for cp in local:
            cp.wait()

    dma = pltpu.SemaphoreType.DMA
    return _CommPlan(
        ins=[grads[name] for name in names],
        out_shape=[jax.ShapeDtypeStruct((N_DEV,) + _shard_shape(*spec[name]), BF16) for name in names],
        scratch=[dma((7 * nw,)), dma((7 * nw,)), dma((nw,))],
        start=start, finish=finish)


def _adamw(w, g, m, v):
    m = ADAM_B1 * m + (1.0 - ADAM_B1) * g
    v = ADAM_B2 * v + (1.0 - ADAM_B2) * (g * g)
    m_hat = m / (1.0 - ADAM_B1 ** ADAM_STEP)
    v_hat = v / (1.0 - ADAM_B2 ** ADAM_STEP)
    delta = -ADAM_LR * (m_hat / (jnp.sqrt(v_hat) + ADAM_EPS) + ADAM_WD * w)
    return delta, m, v


def _reduce_adamw(name, stack, w, m, v):
    rows, cols = w.shape
    tr = next(t for t in (256, 128, 64) if rows % t == 0)

    def body(s_ref, w_ref, m_ref, v_ref, g_ref, d_ref, nm_ref, nv_ref):
        g = s_ref[0].astype(F32)
        for d in range(1, N_DEV):
            g = g + s_ref[d].astype(F32)
        g_ref[...] = g
        d_ref[...], nm_ref[...], nv_ref[...] = _adamw(w_ref[...], g, m_ref[...], v_ref[...])

    blk = pl.BlockSpec((tr, cols), lambda i: (i, 0))
    return pl.pallas_call(
        body, name=name, grid=(rows // tr,),
        in_specs=[pl.BlockSpec((N_DEV, tr, cols), lambda i: (0, i, 0)), blk, blk, blk],
        out_specs=[blk] * 4, out_shape=[jax.ShapeDtypeStruct((rows, cols), F32)] * 4,
        compiler_params=_cparams(),
    )(stack, w, m, v)


def _small_step(pack, w, m, v):
    def body(p_ref, w_ref, m_ref, v_ref, g_ref, d_ref, nm_ref, nv_ref, loss_ref, all_ref, send_sems, recv_sems):
        x, y, c, _ = _place()
        me = 4 * x + 2 * y + c
        all_ref[me] = p_ref[...]
        sent = []
        for n in range(1, N_DEV):
            peer = me ^ n
            cp = pltpu.make_async_remote_copy(
                src_ref=p_ref, dst_ref=all_ref.at[me], send_sem=send_sems.at[n - 1], recv_sem=recv_sems.at[n - 1],
                device_id=(peer // 4, (peer // 2) % 2, peer % 2), device_id_type=MESH)
            cp.start()
            sent.append(cp)
        for n in range(1, N_DEV):
            peer = me ^ n
            pltpu.make_async_remote_copy(
                src_ref=p_ref, dst_ref=all_ref.at[peer], send_sem=send_sems.at[n - 1], recv_sem=recv_sems.at[n - 1],
                device_id=(peer // 4, (peer // 2) % 2, peer % 2), device_id_type=MESH).wait_recv()
        for cp in sent:
            cp.wait_send()
        tot = all_ref[0]
        for d in range(1, N_DEV):
            tot = tot + all_ref[d]
        g = tot[:SMALL_ROWS]
        g_ref[...] = g
        d_ref[...], nm_ref[...], nv_ref[...] = _adamw(w_ref[...], g, m_ref[...], v_ref[...])
        loss_ref[...] = jnp.sum(jnp.sum(tot[SMALL_ROWS:], axis=1, keepdims=True), axis=0, keepdims=True)

    vm = pl.BlockSpec(memory_space=pltpu.VMEM)
    small = jax.ShapeDtypeStruct((SMALL_ROWS, LANES), F32)
    return pl.pallas_call(
        body, name="small_step",
        in_specs=[vm] * 4, out_specs=[vm] * 5,
        out_shape=[small] * 4 + [jax.ShapeDtypeStruct((1, 1), F32)],
        scratch_shapes=[pltpu.VMEM((N_DEV, PACK_ROWS, LANES), F32),
                        pltpu.SemaphoreType.DMA((N_DEV - 1,)), pltpu.SemaphoreType.DMA((N_DEV - 1,))],
    )(pack, w, m, v)


LATER_WEIGHTS = tuple(name for name, _, _ in BIG if name != "w_in")


def _layer_step(x, mem, tgt, shards, vec):
    s = x.shape[0]
    d = D_MODEL
    tm = min(ROW_TILE, s)
    tl = min(WIDE_TILE, s)
    xb, cos2, sin2, w_in = _prep(x, _gather_plan(("w_in",), shards))
    bf = lambda w: ((s, w), BF16)
    f32 = lambda w: ((s, w), F32)

    w_sb, w_rqk = w_in[:, :OFF_RET_Q], w_in[:, OFF_RET_Q:OFF_RET_V]
    w_rvg, w_gate = w_in[:, OFF_RET_V:OFF_GATE], w_in[:, OFF_GATE:]
    q_scale = lambda width, q_width, scale: jnp.concatenate(
        [jnp.full((1, q_width), scale, F32), jnp.ones((1, width - q_width), F32)], axis=1)
    n_groups = 3 * SB_WIDTH // LANES

    def sb_epi(acc, t, i, j):
        scaled = acc * t[0]
        return [jnp.stack([scaled[:, g * LANES:(g + 1) * LANES] for g in range(n_groups)])], []

    (sb_qkv,) = _mm(
        "in_sb", xb, w_sb, s, 3 * SB_WIDTH, d, tm=tl, tn=3 * SB_WIDTH, tk=d, epi=sb_epi,
        ins=[(q_scale(3 * SB_WIDTH, SB_WIDTH, SB_SCALE), *_rowvec(3 * SB_WIDTH))],
        outs=[((n_groups, s, LANES), BF16, (n_groups, tl, LANES), lambda i, j: (0, i, 0))])

    def rope_epi(acc, t, i, j):
        cos, sin, scale = t
        parts = []
        for g in range(acc.shape[1] // RET_QK):
            xg = acc[:, g * RET_QK:(g + 1) * RET_QK]
            parts.append(xg * cos + _swap_halves(xg) * sin)
        return [jnp.concatenate(parts, axis=1) * scale], []

    rope_in = ((tl, RET_QK), lambda i, j: (i, 0))
    (rqk,) = _mm("in_rqk", xb, w_rqk, s, 2 * RET_QK_WIDTH, d, tm=tl, tn=2 * RET_QK_WIDTH, tk=d, epi=rope_epi,
                 chunk=MXU_COLS,
                 ins=[(cos2, *rope_in), (sin2, *rope_in),
                      (q_scale(2 * RET_QK_WIDTH, RET_QK_WIDTH, RET_SCALE), *_rowvec(2 * RET_QK_WIDTH))],
                 outs=[(*f32(2 * RET_QK_WIDTH), *_tile(tl, 2 * RET_QK_WIDTH))])
    (rvg,) = _mm("in_rvg", xb, w_rvg, s, 2 * RET_V_WIDTH, d, tm=tl, tn=2 * RET_V_WIDTH, tk=d, chunk=MXU_COLS,
                 epi=_plain, outs=[(*bf(2 * RET_V_WIDTH), *_tile(tl, 2 * RET_V_WIDTH))])
    (gates,) = _mm("in_gate", xb, w_gate, s, 2 * d, d, tm=tl, tn=2 * d, tk=d, chunk=MXU_COLS,
                   epi=lambda acc, t, i, j: ([_sigmoid(acc + t[0])], []),
                   ins=[(vec["b_gate"], *_rowvec(2 * d))], outs=[(*bf(2 * d), *_tile(tl, 2 * d))])

    sb_out, sb_out_f32, *gathered = _sb_fwd(sb_qkv, s, comm=_gather_plan(LATER_WEIGHTS, shards))
    wt = dict(zip(LATER_WEIGHTS, gathered, strict=True))
    ret, gated = _ret_fwd(rqk, rvg, s)
    (y_sb,) = _mm("sb_o", sb_out, wt["w_sb_o"], s, d, SB_WIDTH, tm=tl, tn=d, tk=SB_WIDTH, epi=_plain,
                  outs=[(*bf(d), *_tile(tl, d))])
    y_ret, mixin = _mm(
        "ret_o", gated, wt["w_ret_o"], s, d, RET_V_WIDTH, tm=tl, tn=d, tk=RET_V_WIDTH, chunk=MXU_COLS,
        epi=lambda acc, t, i, j: ([acc, t[0].astype(F32) * t[2].astype(F32) + t[1].astype(F32) * acc], []),
        ins=[(gates, *_tile(tl, d)), (gates, *_tile(tl, d, 1)), (y_sb, *_tile(tl, d))],
        outs=[(*bf(d), *_tile(tl, d)), (*bf(d), *_tile(tl, d))])

    def ln_epi(acc, t, i, j):
        *res, g, b = t
        prev = res[0] if len(res) == 1 else res[0] * res[1] + res[2]
        xhat, rstd = _norm(DN_ALPHA * prev + acc)
        return [xhat * g + b, xhat, rstd], []

    full = _tile(tm, d)
    col1 = ((tm, 1), lambda i, j: (i, 0))
    vec_in = lambda name: (vec[name], *_rowvec(d))
    ln_outs = [(*bf(d), *full), (*f32(d), *full), ((s, 1), F32, *col1)]
    x1b, xhat1, rstd1 = _mm(
        "mix_o", mixin, wt["w_mix_o"], s, d, d, tm=tm, tn=d, tk=d, epi=ln_epi,
        ins=[(x, *full), vec_in("ln1_g"), vec_in("ln1_b")], outs=ln_outs)

    (qm,) = _mm("mem_q", x1b, wt["w_mem_q"], s, d, d, tm=tl, tn=d, tk=d,
                epi=lambda acc, t, i, j: ([acc * MEM_SCALE], []), outs=[(*bf(d), *_tile(tl, d))])
    (kv,) = _mm("mem_kv", mem, wt["w_mem_kv"], MEM_LEN, 2 * d, d, tm=MEM_LEN, tn=d, tk=d, epi=_plain,
                outs=[((MEM_LEN, 2 * d), BF16, *_tile(MEM_LEN, d))])
    att = _xattn_fwd(qm, kv, s)
    x2b, xhat2, rstd2 = _mm(
        "mem_o", att, wt["w_mem_o"], s, d, d, tm=tm, tn=d, tk=d, epi=ln_epi,
        ins=[(xhat1, *full), vec_in("ln1_g"), vec_in("ln1_b"), vec_in("ln2_g"), vec_in("ln2_b")], outs=ln_outs)

    fh = FFN_HIDDEN
    tf = fh // 2
    (f1,) = _mm("ffn_in1", x2b, wt["w_ffn_in"], s, fh, d, tm=tl, tn=tf, tk=d, epi=_plain, j_outer=True,
                outs=[(*bf(fh), *_tile(tl, tf))])

    def swiglu_epi(acc, t, i, j):
        a = t[0].astype(F32)
        return [acc, a * _sigmoid(a) * acc], []

    f2, act = _mm(
        "ffn_in2", x2b, wt["w_ffn_in"], s, fh, d, tm=tm, tn=fh, tk=d, b_off=(0, 1), epi=swiglu_epi, chunk=MXU_COLS,
        ins=[(f1, *_tile(tm, fh))], outs=[(*bf(fh), *_tile(tm, fh)), (*bf(fh), *_tile(tm, fh))])

    def head_epi(acc, t, i, j):
        prev_hat, prev_g, prev_b, g, b, target = t
        xhat, rstd = _norm(DN_ALPHA * (prev_hat * prev_g + prev_b) + acc)
        err = xhat * g + b - target
        dy = err * (1.0 / d)
        du = _norm_bwd(dy * g, xhat, rstd)
        return [du], [_colsum(dy * xhat), _colsum(dy), _colsum(err * err) * (0.5 / d)]

    vec_acc = ((1, d), F32)
    du3b, dg3, db3, loss_cols = _mm(
        "ffn_out", act, wt["w_ffn_out"], s, d, fh, tm=tm, tn=d, tk=fh, epi=head_epi,
        ins=[(xhat2, *full), vec_in("ln2_g"), vec_in("ln2_b"), vec_in("ln3_g"), vec_in("ln3_b"), (tgt, *full)],
        outs=[(*bf(d), *full)], accs=[vec_acc] * 3)

    grads = {}
    ts = min(SEQ_TILE, s)

    def wgrad(name, a, b, m, n, tm_, tn_, tk_=None):
        (g,) = _mm(name, a, b, m, n, a.shape[0], tm=tm_, tn=tn_, tk=tk_ or ts, ta=True, epi=_plain,
                   outs=[((m, n), BF16, *_tile(tm_, tn_))])
        return g

    def ffn_bwd_epi(acc, t, i, j):
        a, b = t[0].astype(F32), t[1].astype(F32)
        sg = _sigmoid(a)
        return [acc * b * (sg * (1.0 + a * (1.0 - sg))), acc * (a * sg)], []

    df1, df2 = _mm(
        "ffn_out_t", du3b, wt["w_ffn_out"], s, fh, d, tm=tm, tn=fh, tk=d, tb=True, epi=ffn_bwd_epi, chunk=MXU_COLS,
        ins=[(f1, *_tile(tm, fh)), (f2, *_tile(tm, fh))],
        outs=[(*bf(fh), *_tile(tm, fh)), (*bf(fh), *_tile(tm, fh))])
    grads["w_ffn_out"] = wgrad("g_ffn_out", act, du3b, fh, d, tf, d)
    grads["w_ffn_in"] = jnp.concatenate(
        [wgrad("g_ffn_in1", x2b, df1, d, fh, d, tf), wgrad("g_ffn_in2", x2b, df2, d, fh, d, tf)], axis=1)
    (dx2a,) = _mm("ffn_in1_t", df1, wt["w_ffn_in"], s, d, fh, tm=tm, tn=d, tk=fh, tb=True, epi=_plain,
                  outs=[(*f32(d), *full)])

    def ln_bwd(name, a, b, k, tk, b_off, more, scales, xhat, rstd, g):
        def epi(acc, t, i, j):
            *extra, xh, rs, gg = t
            dy = acc
            for e, sc in zip(extra, scales, strict=True):
                dy = dy + e.astype(F32) * sc
            return [_norm_bwd(dy * gg, xh, rs)], [_colsum(dy * xh), _colsum(dy)]

        return _mm(name, a, b, s, d, k, tm=tm, tn=d, tk=tk, tb=True, b_off=b_off, epi=epi,
                   ins=[(e, *full) for e in more] + [(xhat, *full), (rstd, *col1), (g, *_rowvec(d))],
                   outs=[(*bf(d), *full)], accs=[vec_acc] * 2)

    du2b, dg2, db2 = ln_bwd("ffn_in2_t", df2, wt["w_ffn_in"], fh, fh, (0, 1), [dx2a, du3b], [1.0, DN_ALPHA],
                            xhat2, rstd2, vec["ln2_g"])

    (datt,) = _mm("mem_o_t", du2b, wt["w_mem_o"], s, d, d, tm=tl, tn=d, tk=d, tb=True, epi=_plain,
                  outs=[(*bf(d), *_tile(tl, d))])
    grads["w_mem_o"] = wgrad("g_mem_o", att, du2b, d, d, d, d)
    dqm, dkv = _xattn_bwd(qm, kv, datt, s)
    grads["w_mem_q"] = wgrad("g_mem_q", x1b, dqm, d, d, d, d)
    grads["w_mem_kv"] = wgrad("g_mem_kv", mem, dkv, d, 2 * d, d, d, MEM_LEN)
    du1b, dg1, db1 = ln_bwd("mem_q_t", dqm, wt["w_mem_q"], d, d, (0, 0), [du2b], [DN_ALPHA],
                            xhat1, rstd1, vec["ln1_g"])

    def merge_bwd_epi(acc, t, i, j):
        g0, g1, ysb, yret = (v.astype(F32) for v in t)
        dgate0 = acc * ysb * (g0 * (1.0 - g0))
        dgate1 = acc * yret * (g1 * (1.0 - g1))
        return [dgate0, dgate1, acc * g0, acc * g1], [_colsum(dgate0), _colsum(dgate1)]

    dgate0, dgate1, dy_sb, dy_ret, dbg0, dbg1 = _mm(
        "mix_o_t", du1b, wt["w_mix_o"], s, d, d, tm=tm, tn=d, tk=d, tb=True, epi=merge_bwd_epi,
        ins=[(gates, *full), (gates, *_tile(tm, d, 1)), (y_sb, *full), (y_ret, *full)],
        outs=[(*bf(d), *full)] * 4, accs=[vec_acc] * 2)
    grads["w_mix_o"] = wgrad("g_mix_o", mixin, du1b, d, d, d, d)
    grads["w_sb_o"] = wgrad("g_sb_o", sb_out, dy_sb, SB_WIDTH, d, SB_WIDTH, d)
    grads["w_ret_o"] = wgrad("g_ret_o", gated, dy_ret, RET_V_WIDTH, d, RET_V_WIDTH, d)
    (dsb_out,) = _mm("sb_o_t", dy_sb, wt["w_sb_o"], s, SB_WIDTH, d, tm=tl, tn=SB_WIDTH, tk=d, tb=True, epi=_plain,
                     outs=[(*bf(SB_WIDTH), *_tile(tl, SB_WIDTH))])

    def gate_norm_bwd_epi(acc, t, i, j):
        r, g = t[0], t[1].astype(F32)
        drg, dret = [], []
        for h in range(acc.shape[1] // RET_V):
            sl = slice(h * RET_V, (h + 1) * RET_V)
            xhat, rstd = _norm(r[:, sl])
            gg, dd = g[:, sl], acc[:, sl]
            sg = _sigmoid(gg)
            drg.append(dd * xhat * (sg * (1.0 + gg * (1.0 - sg))))
            dret.append(_norm_bwd(dd * (gg * sg), xhat, rstd))
        return [jnp.concatenate(drg, axis=1), jnp.concatenate(dret, axis=1)], []

    drg, dret = _mm(
        "ret_o_t", dy_ret, wt["w_ret_o"], s, RET_V_WIDTH, d, tm=tm, tn=d, tk=d, tb=True, epi=gate_norm_bwd_epi,
        chunk=MXU_COLS,
        ins=[(ret, *full), (rvg, *_tile(tm, d, 1))],
        outs=[(*bf(RET_V_WIDTH), *full)] * 2)

    drq = _ret_bwd_q(rqk, rvg, dret, cos2, sin2, s)
    drk, drv = _ret_bwd_kv(rqk, rvg, dret, cos2, sin2, s)
    dsq, dsk, dsv, *stacked = _sb_bwd(sb_qkv, sb_out_f32, dsb_out, s, comm=_exchange_plan(LATER_WEIGHTS, grads))
    stacks = dict(zip(LATER_WEIGHTS, stacked, strict=True))

    dh = jnp.concatenate([dsq, dsk, dsv, drq, drk, drv, drg, dgate0, dgate1], axis=1)
    grads["w_in"] = wgrad("g_in", xb, dh, d, IN_WIDTH, d, IN_WIDTH // N_CHIPS)
    grad_x, stacks["w_in"] = _mm(
        "in_t", dh, w_in, s, d, IN_WIDTH, tm=tl, tn=d, tk=IN_WIDTH // N_CHIPS, tb=True,
        epi=lambda acc, t, i, j: ([acc + DN_ALPHA * t[0].astype(F32)], []),
        ins=[(du1b, *_tile(tl, d))], outs=[(*f32(d), *_tile(tl, d))], comm=_exchange_plan(("w_in",), grads))

    small = {"b_gate": jnp.concatenate([dbg0, dbg1], axis=1), "ln1_g": dg1, "ln1_b": db1, "ln2_g": dg2,
             "ln2_b": db2, "ln3_g": dg3, "ln3_b": db3}
    return grad_x, stacks, small, loss_cols


def kernel(x, mem, w_in, b_gate, w_sb_o, w_ret_o, w_mix_o, ln1_g, ln1_b, w_mem_q, w_mem_kv, w_mem_o, ln2_g, ln2_b, w_ffn_in, w_ffn_out, ln3_g, ln3_b, loss_target, m_w_in, m_b_gate, m_w_sb_o, m_w_ret_o, m_w_mix_o, m_ln1_g, m_ln1_b, m_w_mem_q, m_w_mem_kv, m_w_mem_o, m_ln2_g, m_ln2_b, m_w_ffn_in, m_w_ffn_out, m_ln3_g, m_ln3_b, v_w_in, v_b_gate, v_w_sb_o, v_w_ret_o, v_w_mix_o, v_ln1_g, v_ln1_b, v_w_mem_q, v_w_mem_kv, v_w_mem_o, v_ln2_g, v_ln2_b, v_w_ffn_in, v_w_ffn_out, v_ln3_g, v_ln3_b):
    given = dict(locals())
    s = x.shape[1]
    x2d = x.reshape(s, D_MODEL)
    tgt = loss_target.reshape(s, D_MODEL)
    mem2d = mem.reshape(MEM_LEN, D_MODEL)
    shard = {name: given[name].reshape(_shard_shape(shape, axis)) for name, shape, axis in BIG}
    vec = {name: given[name] for name in SMALL}

    shards_bf = {name: _cast_bf16("cast_" + name, shard[name]) for name, _, _ in BIG}

    grad_x, stacks, small, loss_cols = _layer_step(x2d, mem2d, tgt, shards_bf, vec)

    out = {}
    for name, shape, axis in BIG:
        stack = stacks[name]
        shp = given[name].shape
        res = _reduce_adamw("adamw_" + name, stack, shard[name], given["m_" + name].reshape(stack.shape[1:]),
                            given["v_" + name].reshape(stack.shape[1:]))
        out[name] = [r.reshape(shp) for r in res]

    pack = jnp.concatenate([small[name] for name in SMALL] + [loss_cols], axis=1).reshape(PACK_ROWS, LANES)
    cat = lambda pre: jnp.concatenate([given[pre + name] for name in SMALL], axis=1).reshape(SMALL_ROWS, LANES)
    *res, loss = _small_step(pack, cat(""), cat("m_"), cat("v_"))
    flat = [r.reshape(1, SMALL_LEN) for r in res]
    off = 0
    for name in SMALL:
        n = given[name].shape[1]
        out[name] = [r[:, off:off + n] for r in flat]
        off += n

    return (loss.reshape(()), grad_x.reshape(x.shape),
            *[out[name][0] for name in WEIGHT_ORDER], *[out[name][1] for name in WEIGHT_ORDER],
            *[out[name][2] for name in WEIGHT_ORDER], *[out[name][3] for name in WEIGHT_ORDER])
```

```python
import functools

import jax
import jax.numpy as jnp
import numpy as np
from jax import lax
from jax.experimental import pallas as pl
from jax.experimental.pallas import tpu as pltpu

F32, BF16 = jnp.float32, jnp.bfloat16
MESH = pl.DeviceIdType.MESH

D_MODEL = 1024
MEM_LEN = 256
SB_HEADS, SB_DIM, SB_WIDTH = 8, 64, 512
RET_HEADS, RET_QK, RET_V = 4, 128, 256
RET_QK_WIDTH, RET_V_WIDTH = 512, 1024
ROPE_BASE = 10000.0
MEM_HEADS, MEM_DIM = 4, 256
FFN_HIDDEN = 2816
IN_WIDTH = 6656
OFF_RET_Q, OFF_RET_V, OFF_RET_G, OFF_GATE = 1536, 2560, 3584, 4608
DN_ALPHA = 2.0 ** 0.25
LN_EPS = 1e-5
SB_SCALE = SB_DIM ** -0.5
SB_DEAD = -110.0
RET_SCALE = RET_QK ** -0.5
MEM_SCALE = MEM_DIM ** -0.5
ADAM_LR, ADAM_B1, ADAM_B2, ADAM_EPS, ADAM_WD, ADAM_STEP = 0.001, 0.9, 0.999, 1e-08, 0.01, 10

N_DEV, N_CHIPS = 8, 4

LANES = 128
MXU_COLS = 256
VMEM_LIMIT_BYTES = 52 * 2 ** 20
ROW_TILE = 512
WIDE_TILE = 1024
SEQ_TILE = 2048
SB_BLOCK = 256
RET_BLOCK = 256
RET_CHUNKS_PER_STEP = 4
XATTN_ROWS = 1024

BIG = (
    ("w_in", (D_MODEL, IN_WIDTH), 1),
    ("w_sb_o", (SB_WIDTH, D_MODEL), 1),
    ("w_ret_o", (RET_V_WIDTH, D_MODEL), 0),
    ("w_mix_o", (D_MODEL, D_MODEL), 0),
    ("w_mem_q", (D_MODEL, D_MODEL), 0),
    ("w_mem_kv", (D_MODEL, 2 * D_MODEL), 1),
    ("w_mem_o", (D_MODEL, D_MODEL), 0),
    ("w_ffn_in", (D_MODEL, 2 * FFN_HIDDEN), 1),
    ("w_ffn_out", (FFN_HIDDEN, D_MODEL), 0),
)
SMALL = ("b_gate", "ln1_g", "ln1_b", "ln2_g", "ln2_b", "ln3_g", "ln3_b")
SMALL_LEN = 2 * D_MODEL + 6 * D_MODEL
SMALL_ROWS = SMALL_LEN // LANES
PACK_ROWS = SMALL_ROWS + D_MODEL // LANES
WEIGHT_ORDER = ("w_in", "b_gate", "w_sb_o", "w_ret_o", "w_mix_o", "ln1_g", "ln1_b", "w_mem_q", "w_mem_kv",
                "w_mem_o", "ln2_g", "ln2_b", "w_ffn_in", "w_ffn_out", "ln3_g", "ln3_b")


def _cparams():
    return pltpu.CompilerParams(vmem_limit_bytes=VMEM_LIMIT_BYTES)


def _dot(a, b, ca, cb):
    return lax.dot_general(a, b, (((ca,), (cb,)), ((), ())), preferred_element_type=F32)


def _sigmoid(x):
    return 1.0 / (1.0 + jnp.exp(-x))


def _mm(name, a, b, m, n, k, *, tm, tn, tk, epi, outs, ins=(), accs=(), ta=False, tb=False,
        a_off=(0, 0), b_off=(0, 0), j_outer=False, comm=None, chunk=None):
    assert m % tm == 0 and n % tn == 0 and k % tk == 0, (name, m, n, k, tm, tn, tk)
    assert chunk is None or (k == tk and tn % chunk == 0), name
    ni, nj, nk = m // tm, n // tn, k // tk
    assert not accs or nj == 1, name
    ij = (lambda g0, g1: (g1, g0)) if j_outer else (lambda g0, g1: (g0, g1))

    def spec(block, index):
        return pl.BlockSpec(block, lambda g0, g1, kk: index(*ij(g0, g1), kk))

    if ta:
        a_spec = spec((tk, tm), lambda i, j, kk: (kk + a_off[0], i + a_off[1]))
    else:
        a_spec = spec((tm, tk), lambda i, j, kk: (i + a_off[0], kk + a_off[1]))
    if tb:
        b_spec = spec((tn, tk), lambda i, j, kk: (j + b_off[0], kk + b_off[1]))
    else:
        b_spec = spec((tk, tn), lambda i, j, kk: (kk + b_off[0], j + b_off[1]))
    in_specs = [a_spec, b_spec]
    for _, bs, im in ins:
        in_specs.append(spec(bs, lambda i, j, kk, im=im: im(i, j)))
    out_specs, out_shape = [], []
    for shape, dtype, bs, im in outs:
        out_specs.append(spec(bs, lambda i, j, kk, im=im: im(i, j)))
        out_shape.append(jax.ShapeDtypeStruct(shape, dtype))
    for shape, dtype in accs:
        out_specs.append(spec(shape, lambda i, j, kk, nd=len(shape): (0,) * nd))
        out_shape.append(jax.ShapeDtypeStruct(shape, dtype))
    n_in, n_out, n_acc = len(ins), len(outs), len(accs)
    ca, cb = (0 if ta else 1), (1 if tb else 0)
    grid = (*ij(ni, nj), nk)
    comm_ins, comm_outs, comm_scratch = [], [], []
    if comm is not None:
        comm_in_specs, comm_out_specs = comm.specs
        comm_ins, comm_outs, comm_scratch = list(comm.ins), list(comm.out_shape), list(comm.scratch)
        in_specs += comm_in_specs
        out_specs += comm_out_specs
        out_shape += comm_outs
    n_ci, n_co = len(comm_ins), len(comm_outs)

    def body(*refs):
        a_ref, b_ref = refs[:2]
        in_refs = refs[2:2 + n_in]
        ci_refs = refs[2 + n_in:2 + n_in + n_ci]
        rest = refs[2 + n_in + n_ci:]
        out_refs, acc_refs = rest[:n_out], rest[n_out:n_out + n_acc]
        co_refs = rest[n_out + n_acc:n_out + n_acc + n_co]
        scratch = rest[n_out + n_acc + n_co:]
        sem_refs, scratch = scratch[:len(comm_scratch)], scratch[len(comm_scratch):]
        (i, j), kk = ij(pl.program_id(0), pl.program_id(1)), pl.program_id(2)
        if comm is not None:
            first_step, last_step = _grid_ends(grid)
            pl.when(first_step)(lambda: comm.start(ci_refs, co_refs, sem_refs))
        def finish(acc, cols=slice(None)):
            def of(r):
                return r[..., cols] if r.shape[-1] == tn else r[...]

            o_tiles, a_tiles = epi(acc, [of(r) for r in in_refs], i, j)
            for r, t in zip(out_refs, o_tiles, strict=True):
                r[..., cols] = t.astype(r.dtype)
            if n_acc:
                @pl.when(i == 0)
                def _():
                    for r, t in zip(acc_refs, a_tiles, strict=True):
                        r[..., cols] = t

                @pl.when(i > 0)
                def _():
                    for r, t in zip(acc_refs, a_tiles, strict=True):
                        r[..., cols] += t

        if chunk is not None:
            a_tile = a_ref[...].astype(BF16)
            for c0 in range(0, tn, chunk):
                cols = slice(c0, c0 + chunk)
                b_part = b_ref[cols, :] if tb else b_ref[:, cols]
                finish(_dot(a_tile, b_part.astype(BF16), ca, cb), cols)
            if comm is not None:
                pl.when(last_step)(lambda: comm.finish(ci_refs, co_refs, sem_refs))
            return

        part = _dot(a_ref[...].astype(BF16), b_ref[...].astype(BF16), ca, cb)
        if nk == 1:
            finish(part)
        else:
            acc_ref = scratch[0]

            @pl.when(kk == 0)
            def _():
                acc_ref[...] = part

            @pl.when(kk > 0)
            def _():
                acc_ref[...] += part

            @pl.when(kk == nk - 1)
            def _():
                finish(acc_ref[...])

        if comm is not None:
            pl.when(last_step)(lambda: comm.finish(ci_refs, co_refs, sem_refs))

    res = pl.pallas_call(
        body, name=name, grid=grid, in_specs=in_specs, out_specs=out_specs, out_shape=out_shape,
        scratch_shapes=comm_scratch + ([pltpu.VMEM((tm, tn), F32)] if nk > 1 else []),
        compiler_params=_cparams(),
    )(a, b, *[x for x, _, _ in ins], *comm_ins)
    return res


def _grid_ends(grid):
    ids = [pl.program_id(ax) for ax in range(len(grid))]
    first = functools.reduce(jnp.logical_and, [p == 0 for p in ids])
    last = functools.reduce(jnp.logical_and, [p == n - 1 for p, n in zip(ids, grid, strict=True)])
    return first, last


def _tile(tm, tn, dj=0):
    return (tm, tn), (lambda i, j: (i, j + dj))


def _rowvec(tn, dj=0):
    return (1, tn), (lambda i, j: (0, j + dj))


def _plain(acc, tiles, i, j):
    return [acc], []


def _ew(name, fn, ins, outs, rows, tr):
    assert rows % tr == 0, (name, rows, tr)
    in_specs = []
    for x in ins:
        if x.shape[0] == rows:
            in_specs.append(pl.BlockSpec((tr, x.shape[1]), lambda i: (i, 0)))
        else:
            in_specs.append(pl.BlockSpec(x.shape, lambda i: (0, 0)))
    n_in = len(ins)

    def body(*refs):
        res = fn(*[r[...] for r in refs[:n_in]])
        for r, t in zip(refs[n_in:], res, strict=True):
            r[...] = t.astype(r.dtype)

    return pl.pallas_call(
        body, name=name, grid=(rows // tr,), in_specs=in_specs,
        out_specs=[pl.BlockSpec((tr, w), lambda i: (i, 0)) for w, _ in outs],
        out_shape=[jax.ShapeDtypeStruct((rows, w), dt) for w, dt in outs],
        compiler_params=_cparams(),
    )(*ins)


def _cast_bf16(name, x):
    rows = x.shape[0]
    tr = next(t for t in (512, 256, 64) if rows % t == 0)
    return _ew(name, lambda v: (v,), [x], [(x.shape[1], BF16)], rows, tr)[0]


def _prep(x, comm):
    s = x.shape[0]
    half = RET_QK // 2
    inv = 1.0 / (ROPE_BASE ** (jnp.arange(half, dtype=F32) / half))
    inv2 = jnp.concatenate([inv, inv]).reshape(1, RET_QK)
    sign = jnp.concatenate([-jnp.ones((half,), F32), jnp.ones((half,), F32)]).reshape(1, RET_QK)
    tr = min(ROW_TILE, s)
    grid = (s // tr,)
    c_in_specs, c_out_specs, c_out_shape, c_scratch, c_ins, split = _host(comm, 3, 3)

    def body(*refs):
        (x_ref, inv_ref, sign_ref), (xb_ref, cos_ref, sin_ref), _, riding = split(refs)
        i = pl.program_id(0)
        first_step, last_step = _grid_ends(grid)
        pl.when(first_step)(lambda: comm.start(*riding))
        xb_ref[...] = x_ref[...].astype(BF16)
        pos = (lax.broadcasted_iota(jnp.int32, (tr, RET_QK), 0) + i * tr).astype(F32)
        ang = pos * inv_ref[...]
        cos_ref[...] = jnp.cos(ang)
        sin_ref[...] = jnp.sin(ang) * sign_ref[...]
        pl.when(last_step)(lambda: comm.finish(*riding))

    vec = pl.BlockSpec((1, RET_QK), lambda i: (0, 0))
    row = lambda w: pl.BlockSpec((tr, w), lambda i: (i, 0))
    return pl.pallas_call(
        body, name="prep", grid=grid,
        in_specs=[row(D_MODEL), vec, vec] + c_in_specs,
        out_specs=[row(D_MODEL), row(RET_QK), row(RET_QK)] + c_out_specs,
        out_shape=[jax.ShapeDtypeStruct((s, D_MODEL), BF16), jax.ShapeDtypeStruct((s, RET_QK), F32),
                   jax.ShapeDtypeStruct((s, RET_QK), F32)] + c_out_shape,
        scratch_shapes=c_scratch, compiler_params=_cparams(),
    )(x, inv2, sign, *c_ins)


def _swap_halves(x):
    return pltpu.roll(x, RET_QK // 2, 1)


def _norm(u):
    mu = jnp.mean(u, axis=-1, keepdims=True)
    d = u - mu
    var = jnp.mean(d * d, axis=-1, keepdims=True)
    rstd = lax.rsqrt(var + LN_EPS)
    return d * rstd, rstd


def _norm_bwd(dxh, xhat, rstd):
    m1 = jnp.mean(dxh, axis=-1, keepdims=True)
    m2 = jnp.mean(dxh * xhat, axis=-1, keepdims=True)
    return rstd * (dxh - m1 - xhat * m2)


def _colsum(t):
    return jnp.sum(t, axis=0, keepdims=True)


def _split_mm(t, tri):
    hi = t.astype(BF16)
    lo = (t - hi.astype(F32)).astype(BF16)
    return _dot(hi, tri, 1, 0) + _dot(lo, tri, 1, 0)


def _sb_masks():
    t = SB_BLOCK
    lane = lax.broadcasted_iota(jnp.int32, (1, LANES), 1)
    first = lane < SB_DIM
    m0 = jnp.where(first, 1.0, 0.0).astype(BF16)
    m1 = jnp.where(first, 0.0, 1.0).astype(BF16)
    row = lax.broadcasted_iota(jnp.int32, (t, t), 0)
    col = lax.broadcasted_iota(jnp.int32, (t, t), 1)
    return first, (m0, m1), row, col


def _sb_logits(qh, k, causal):
    z = _dot(qh, k, 1, 1)
    lp = jnp.log(1.0 + jnp.exp(-jnp.abs(z)))
    a = jnp.minimum(z, 0.0) - lp
    r = jnp.minimum(-z, 0.0) - lp
    if causal is not None:
        r = jnp.where(causal, r, 0.0)
    return a, r


def _sb_walk(i, blocks, l_ref, causal):
    pl.when(i == 0)(lambda: blocks([(i, causal)]))
    pl.when(i > 0)(lambda: blocks([(i, causal), (i - 1, None)]))

    def alive():
        top = jnp.max(functools.reduce(jnp.maximum, [l_ref[c] for c in range(l_ref.shape[0])]))
        return jnp.where(top > SB_DEAD, 1, 0)

    def cond(c):
        return jnp.logical_and(c[0] < i, c[1] > 0)

    def step(c):
        blocks([(i - 1 - c[0], None)])
        return c[0] + 1, alive()

    lax.while_loop(cond, step, (jnp.int32(1), alive()))


def _host(comm, n_in, n_out):
    if comm is None:
        return [], [], [], [], [], lambda refs: (refs[:n_in], refs[n_in:n_in + n_out], refs[n_in + n_out:], None)
    in_specs, out_specs = comm.specs
    n_ci, n_co, n_sem = len(comm.ins), len(comm.out_shape), len(comm.scratch)

    def split(refs):
        ins, ci = refs[:n_in], refs[n_in:n_in + n_ci]
        rest = refs[n_in + n_ci:]
        outs, co = rest[:n_out], rest[n_out:n_out + n_co]
        sems, scratch = rest[n_out + n_co:n_out + n_co + n_sem], rest[n_out + n_co + n_sem:]
        return ins, outs, scratch, (ci, co, sems)

    return in_specs, out_specs, list(comm.out_shape), list(comm.scratch), list(comm.ins), split


def _sb_qkv_specs(s, g):
    groups = SB_HEADS // 2 // g
    return [pl.BlockSpec((g, SB_BLOCK, LANES), lambda p, i: (p, i, 0)),
            pl.BlockSpec((g, s, LANES), lambda p, i: (groups + p, 0, 0)),
            pl.BlockSpec((g, s, LANES), lambda p, i: (2 * groups + p, 0, 0))]


def _sb_fwd(qkv, s, comm=None):
    t = SB_BLOCK
    g = 2
    nq = s // t
    grid = (SB_HEADS // 2 // g, nq)
    c_in_specs, c_out_specs, c_out_shape, c_scratch, c_ins, split = _host(comm, 3, 2)

    def body(*refs):
        (q_ref, k_ref, v_ref), (o_ref, of_ref), (l_ref, acc_ref), riding = split(refs)
        i = pl.program_id(1)
        if comm is not None:
            first_step, last_step = _grid_ends(grid)
            pl.when(first_step)(lambda: comm.start(*riding))
        first, hmask, row, col = _sb_masks()
        after = jnp.where(row > col, 1.0, 0.0).astype(BF16)
        causal = col < row
        heads = [(p, h) for p in range(g) for h in range(2)]
        qh = {(p, h): q_ref[p] * hmask[h] for p, h in heads}
        l_ref[...] = jnp.zeros_like(l_ref)
        acc_ref[...] = jnp.zeros_like(acc_ref)

        def blocks(todo):
            chains = [(b, p, h) for b in range(len(todo)) for p, h in heads]
            starts = [pl.multiple_of(kb * t, t) for kb, _ in todo]
            ks = {(b, p): k_ref[p, pl.ds(st, t), :] for b, st in enumerate(starts) for p in range(g)}
            vs = {(b, p): v_ref[p, pl.ds(st, t), :] for b, st in enumerate(starts) for p in range(g)}
            ar = {(b, p, h): _sb_logits(qh[p, h], ks[b, p], todo[b][1]) for b, p, h in chains}
            later = {c: _split_mm(ar[c][1], after) for c in chains}
            carry = {(p, h): l_ref[2 * p + h] for p, h in heads}
            w = {}
            for b, (_, mask) in enumerate(todo):
                for p, h in heads:
                    wc = jnp.exp(ar[b, p, h][0] + later[b, p, h] + carry[p, h])
                    w[b, p, h] = wc if mask is None else jnp.where(mask, wc, 0.0)
                carry = {(p, h): carry[p, h] + jnp.sum(ar[b, p, h][1], axis=1, keepdims=True) for p, h in heads}
            pv = {(b, p, h): _dot(w[b, p, h].astype(BF16), vs[b, p], 1, 0) for b, p, h in chains}
            for p in range(g):
                lanes = slice(p * LANES, (p + 1) * LANES)
                acc = acc_ref[:, lanes]
                for b in range(len(todo)):
                    acc = acc + jnp.where(first, pv[b, p, 0], pv[b, p, 1])
                acc_ref[:, lanes] = acc
            for p, h in heads:
                l_ref[2 * p + h] = carry[p, h]

        _sb_walk(i, blocks, l_ref, causal)
        o_ref[...] = acc_ref[...].astype(o_ref.dtype)
        of_ref[...] = acc_ref[...]
        if comm is not None:
            pl.when(last_step)(lambda: comm.finish(*riding))

    blk = pl.BlockSpec((t, g * LANES), lambda p, i: (i, p))
    return pl.pallas_call(
        body, name="sb_fwd", grid=grid,
        in_specs=_sb_qkv_specs(s, g) + c_in_specs,
        out_specs=[blk, blk] + c_out_specs,
        out_shape=[jax.ShapeDtypeStruct((s, SB_WIDTH), BF16), jax.ShapeDtypeStruct((s, SB_WIDTH), F32)] + c_out_shape,
        scratch_shapes=c_scratch + [pltpu.VMEM((2 * g, t, 1), F32), pltpu.VMEM((t, g * LANES), F32)],
        compiler_params=_cparams(),
    )(qkv, qkv, qkv, *c_ins)


def _sb_bwd(qkv, o, do, s, comm=None):
    t = SB_BLOCK
    g = 2
    nq = s // t
    grid = (SB_HEADS // 2 // g, nq)
    c_in_specs, c_out_specs, c_out_shape, c_scratch, c_ins, split = _host(comm, 5, 3)

    def body(*refs):
        ((q_ref, k_ref, v_ref, o_ref, do_ref), (dq_ref, dk_ref, dv_ref),
         (l_ref, e_ref, dq_acc, dk_acc, dv_acc), riding) = split(refs)
        i = pl.program_id(1)
        if comm is not None:
            first_step, last_step = _grid_ends(grid)
            pl.when(first_step)(lambda: comm.start(*riding))
        first, hmask, row, col = _sb_masks()
        after = jnp.where(row > col, 1.0, 0.0).astype(BF16)
        from_here = jnp.where(row >= col, 1.0, 0.0).astype(BF16)
        causal = col < row

        @pl.when(i == 0)
        def _():
            dk_acc[...] = jnp.zeros_like(dk_acc)
            dv_acc[...] = jnp.zeros_like(dv_acc)

        heads = [(p, h) for p in range(g) for h in range(2)]
        lanes = [slice(p * LANES, (p + 1) * LANES) for p in range(g)]
        q = [q_ref[p] for p in range(g)]
        do_ = [do_ref[:, lanes[p]] for p in range(g)]
        qh = {(p, h): q[p] * hmask[h] for p, h in heads}
        doh = {(p, h): do_[p] * hmask[h] for p, h in heads}
        total = {}
        for p in range(g):
            prod = do_[p].astype(F32) * o_ref[:, lanes[p]]
            total[p, 0] = jnp.sum(jnp.where(first, prod, 0.0), axis=1, keepdims=True)
            total[p, 1] = jnp.sum(jnp.where(first, 0.0, prod), axis=1, keepdims=True)
        l_ref[...] = jnp.zeros_like(l_ref)
        e_ref[...] = jnp.zeros_like(e_ref)
        dq_acc[...] = jnp.zeros_like(dq_acc)

        def blocks(todo):
            chains = [(b, p, h) for b in range(len(todo)) for p, h in heads]
            starts = [pl.multiple_of(kb * t, t) for kb, _ in todo]
            ks = {(b, p): k_ref[p, pl.ds(st, t), :] for b, st in enumerate(starts) for p in range(g)}
            vs = {(b, p): v_ref[p, pl.ds(st, t), :] for b, st in enumerate(starts) for p in range(g)}
            ar = {(b, p, h): _sb_logits(qh[p, h], ks[b, p], todo[b][1]) for b, p, h in chains}
            dw = {(b, p, h): _dot(doh[p, h], vs[b, p], 1, 1) for b, p, h in chains}
            later = {c: _split_mm(ar[c][1], after) for c in chains}
            carry = {(p, h): l_ref[2 * p + h] for p, h in heads}
            wb = {}
            for b, (_, mask) in enumerate(todo):
                for p, h in heads:
                    wc = jnp.exp(ar[b, p, h][0] + later[b, p, h] + carry[p, h])
                    wb[b, p, h] = (wc if mask is None else jnp.where(mask, wc, 0.0)).astype(BF16)
                carry = {(p, h): carry[p, h] + jnp.sum(ar[b, p, h][1], axis=1, keepdims=True) for p, h in heads}
            dvs = {(b, p, h): _dot(wb[b, p, h], do_[p], 0, 0) for b, p, h in chains}
            e = {c: dw[c] * wb[c].astype(F32) for c in chains}
            suffix = {c: _split_mm(e[c], from_here) for c in chains}
            e_carry = {(p, h): e_ref[2 * p + h] for p, h in heads}
            dz = {}
            for b, (_, mask) in enumerate(todo):
                for p, h in heads:
                    before = total[p, h] - (suffix[b, p, h] + e_carry[p, h])
                    dzc = e[b, p, h] - jnp.exp(ar[b, p, h][0]) * (e[b, p, h] + before)
                    dz[b, p, h] = (dzc if mask is None else jnp.where(mask, dzc, 0.0)).astype(BF16)
                e_carry = {(p, h): e_carry[p, h] + jnp.sum(e[b, p, h], axis=1, keepdims=True) for p, h in heads}
            dqs = {(b, p, h): _dot(dz[b, p, h], ks[b, p], 1, 0) for b, p, h in chains}
            dks = {(b, p, h): _dot(dz[b, p, h], q[p], 0, 0) for b, p, h in chains}
            for p in range(g):
                dq = dq_acc[:, lanes[p]]
                for b, st in enumerate(starts):
                    dq = dq + jnp.where(first, dqs[b, p, 0], dqs[b, p, 1])
                    dk_acc[pl.ds(st, t), lanes[p]] += jnp.where(first, dks[b, p, 0], dks[b, p, 1])
                    dv_acc[pl.ds(st, t), lanes[p]] += jnp.where(first, dvs[b, p, 0], dvs[b, p, 1])
                dq_acc[:, lanes[p]] = dq
            for p, h in heads:
                l_ref[2 * p + h] = carry[p, h]
                e_ref[2 * p + h] = e_carry[p, h]

        _sb_walk(i, blocks, l_ref, causal)
        dq_ref[...] = (dq_acc[...] * SB_SCALE).astype(dq_ref.dtype)

        @pl.when(i == nq - 1)
        def _():
            dk_ref[...] = dk_acc[...].astype(dk_ref.dtype)
            dv_ref[...] = dv_acc[...].astype(dv_ref.dtype)

        if comm is not None:
            pl.when(last_step)(lambda: comm.finish(*riding))

    once = pl.Buffered(1)
    q_spec, k_spec, v_spec = _sb_qkv_specs(s, g)
    k_spec = pl.BlockSpec(k_spec.block_shape, k_spec.index_map, pipeline_mode=once)
    v_spec = pl.BlockSpec(v_spec.block_shape, v_spec.index_map, pipeline_mode=once)
    blk = pl.BlockSpec((t, g * LANES), lambda p, i: (i, p))
    col_blk = pl.BlockSpec((s, g * LANES), lambda p, i: (0, p), pipeline_mode=once)
    sds = jax.ShapeDtypeStruct((s, SB_WIDTH), BF16)
    return pl.pallas_call(
        body, name="sb_bwd", grid=grid,
        in_specs=[q_spec, k_spec, v_spec, blk, blk] + c_in_specs,
        out_specs=[blk, col_blk, col_blk] + c_out_specs,
        out_shape=[sds, sds, sds] + c_out_shape,
        scratch_shapes=c_scratch + [pltpu.VMEM((2 * g, t, 1), F32), pltpu.VMEM((2 * g, t, 1), F32),
                                    pltpu.VMEM((t, g * LANES), F32), pltpu.VMEM((s, g * LANES), F32),
                                    pltpu.VMEM((s, g * LANES), F32)],
        compiler_params=_cparams(),
    )(qkv, qkv, qkv, o, do, *c_ins)


def _ret_log_gamma():
    lg = np.log1p(-np.exp2(-5.0 - np.arange(RET_HEADS, dtype=np.float32))).astype(np.float32)
    return jnp.asarray(np.broadcast_to(lg[:, None, None], (RET_HEADS, 8, LANES)).copy())


RET_SCRATCH = [pltpu.VMEM((RET_HEADS, RET_QK, RET_V), F32),
               pltpu.VMEM((RET_HEADS, RET_BLOCK, RET_BLOCK), F32),
               pltpu.VMEM((RET_HEADS, RET_BLOCK, 1), F32),
               pltpu.VMEM((RET_HEADS, RET_BLOCK, 1), F32)]


def _ret_begin(n, lg_ref, state, within, q_dec, k_dec):
    @pl.when(n == 0)
    def _():
        c = RET_BLOCK
        state[...] = jnp.zeros_like(state)
        row = lax.broadcasted_iota(jnp.int32, (c, c), 0)
        col = lax.broadcasted_iota(jnp.int32, (c, c), 1)
        rel = jnp.maximum(row - col, 0).astype(F32)
        idx = lax.broadcasted_iota(jnp.int32, (c, 1), 0).astype(F32)
        for h in range(RET_HEADS):
            lg = lg_ref[h, 0:1, 0:1]
            within[h] = jnp.where(row >= col, jnp.exp(lg * rel), 0.0)
            q_dec[h] = jnp.exp(lg * (idx + 1.0))
            k_dec[h] = jnp.exp(lg * (c - 1.0 - idx))


def _chunk_decay(lg_ref, h):
    return jnp.exp(lg_ref[h, 0:1, 0:1] * float(RET_BLOCK))


def _ret_heads(x, width):
    return [x[:, h * width:(h + 1) * width] for h in range(RET_HEADS)]


def _ret_specs(s, reverse=False):
    c = RET_BLOCK
    per_step = min(RET_CHUNKS_PER_STEP, s // c)
    rows = c * per_step
    nc = s // rows
    pos = (lambda n: nc - 1 - n) if reverse else (lambda n: n)
    chunks = [slice(u * c, (u + 1) * c) for u in range(per_step)]
    q_spec = pl.BlockSpec((rows, RET_QK_WIDTH), lambda n: (pos(n), 0))
    k_spec = pl.BlockSpec((rows, RET_QK_WIDTH), lambda n: (pos(n), 1))
    v_spec = pl.BlockSpec((rows, RET_V_WIDTH), lambda n: (pos(n), 0))
    lg_spec = pl.BlockSpec((RET_HEADS, 8, LANES), lambda n: (0, 0, 0))
    rope_spec = pl.BlockSpec((rows, RET_QK), lambda n: (pos(n), 0))
    return nc, chunks[::-1] if reverse else chunks, q_spec, k_spec, v_spec, lg_spec, rope_spec


def _ret_fwd(rqk, rvg, s):
    nc, chunks, q_spec, k_spec, v_spec, lg_spec, _ = _ret_specs(s)
    g_spec = pl.BlockSpec(v_spec.block_shape, lambda n: (n, 1))
    heads = range(RET_HEADS)

    def body(q_ref, k_ref, v_ref, g_ref, lg_ref, r_ref, y_ref, state, within, q_dec, k_dec):
        n = pl.program_id(0)
        _ret_begin(n, lg_ref, state, within, q_dec, k_dec)
        for rows in chunks:
            q, k = _ret_heads(q_ref[rows], RET_QK), _ret_heads(k_ref[rows], RET_QK)
            v, g = _ret_heads(v_ref[rows], RET_V), _ret_heads(g_ref[rows], RET_V)
            scores = [_dot(q[h].astype(BF16), k[h].astype(BF16), 1, 1) * within[h] for h in heads]
            cross = [_dot((q[h] * q_dec[h]).astype(BF16), state[h].astype(BF16), 1, 0) for h in heads]
            out = [_dot(scores[h].astype(BF16), v[h], 1, 0) + cross[h] for h in heads]
            grown = [_dot((k[h] * k_dec[h]).astype(BF16), v[h], 0, 0) for h in heads]
            for h in heads:
                sl = slice(h * RET_V, (h + 1) * RET_V)
                r_ref[rows, sl] = out[h]
                xhat, _ = _norm(out[h])
                gh = g[h].astype(F32)
                y_ref[rows, sl] = (gh * _sigmoid(gh) * xhat).astype(y_ref.dtype)
                state[h] = state[h] * _chunk_decay(lg_ref, h) + grown[h]

    return pl.pallas_call(
        body, name="ret_fwd", grid=(nc,),
        in_specs=[q_spec, k_spec, v_spec, g_spec, lg_spec],
        out_specs=[v_spec, v_spec],
        out_shape=[jax.ShapeDtypeStruct((s, RET_V_WIDTH), F32), jax.ShapeDtypeStruct((s, RET_V_WIDTH), BF16)],
        scratch_shapes=RET_SCRATCH,
        compiler_params=_cparams(),
    )(rqk, rqk, rvg, rvg, _ret_log_gamma())


def _rope_bwd(d, cos, sin):
    return d * cos + _swap_halves(d * sin)


def _ret_bwd_q(rqk, rv, d_out, cos2, sin2, s):
    nc, chunks, q_spec, k_spec, v_spec, lg_spec, rope_spec = _ret_specs(s)
    heads = range(RET_HEADS)

    def body(k_ref, v_ref, d_ref, lg_ref, cos_ref, sin_ref, dq_ref, state, within, q_dec, k_dec):
        n = pl.program_id(0)
        _ret_begin(n, lg_ref, state, within, q_dec, k_dec)
        for rows in chunks:
            k = _ret_heads(k_ref[rows], RET_QK)
            v, d = _ret_heads(v_ref[rows], RET_V), _ret_heads(d_ref[rows], RET_V)
            cos, sin = cos_ref[rows], sin_ref[rows]
            d_scores = [_dot(d[h], v[h], 1, 1) * within[h] for h in heads]
            cross = [q_dec[h] * _dot(d[h], state[h].astype(BF16), 1, 1) for h in heads]
            dq = [_dot(d_scores[h].astype(BF16), k[h].astype(BF16), 1, 0) + cross[h] for h in heads]
            grown = [_dot((k[h] * k_dec[h]).astype(BF16), v[h], 0, 0) for h in heads]
            for h in heads:
                sl = slice(h * RET_QK, (h + 1) * RET_QK)
                dq_ref[rows, sl] = (_rope_bwd(dq[h], cos, sin) * RET_SCALE).astype(dq_ref.dtype)
                state[h] = state[h] * _chunk_decay(lg_ref, h) + grown[h]

    return pl.pallas_call(
        body, name="ret_bwd_q", grid=(nc,),
        in_specs=[k_spec, v_spec, v_spec, lg_spec, rope_spec, rope_spec],
        out_specs=q_spec,
        out_shape=jax.ShapeDtypeStruct((s, RET_QK_WIDTH), BF16),
        scratch_shapes=RET_SCRATCH,
        compiler_params=_cparams(),
    )(rqk, rv, d_out, _ret_log_gamma(), cos2, sin2)


def _ret_bwd_kv(rqk, rv, d_out, cos2, sin2, s):
    nc, chunks, q_spec, k_spec, v_spec, lg_spec, rope_spec = _ret_specs(s, reverse=True)
    heads = range(RET_HEADS)

    def body(q_ref, k_ref, v_ref, d_ref, lg_ref, cos_ref, sin_ref, dk_ref, dv_ref, state, within, q_dec, k_dec):
        n = pl.program_id(0)
        _ret_begin(n, lg_ref, state, within, q_dec, k_dec)
        for rows in chunks:
            q, k = _ret_heads(q_ref[rows], RET_QK), _ret_heads(k_ref[rows], RET_QK)
            v, d = _ret_heads(v_ref[rows], RET_V), _ret_heads(d_ref[rows], RET_V)
            cos, sin = cos_ref[rows], sin_ref[rows]
            qb, kb = [q[h].astype(BF16) for h in heads], [k[h].astype(BF16) for h in heads]
            st = [state[h].astype(BF16) for h in heads]
            scores = [_dot(qb[h], kb[h], 1, 1) * within[h] for h in heads]
            d_scores = [_dot(d[h], v[h], 1, 1) * within[h] for h in heads]
            dk = [_dot(d_scores[h].astype(BF16), qb[h], 0, 0) + k_dec[h] * _dot(v[h], st[h], 1, 1) for h in heads]
            dv = [_dot(scores[h].astype(BF16), d[h], 0, 0) + k_dec[h] * _dot(kb[h], st[h], 1, 0) for h in heads]
            grown = [_dot((q[h] * q_dec[h]).astype(BF16), d[h], 0, 0) for h in heads]
            for h in heads:
                dk_ref[rows, h * RET_QK:(h + 1) * RET_QK] = _rope_bwd(dk[h], cos, sin).astype(dk_ref.dtype)
                dv_ref[rows, h * RET_V:(h + 1) * RET_V] = dv[h].astype(dv_ref.dtype)
                state[h] = state[h] * _chunk_decay(lg_ref, h) + grown[h]

    return pl.pallas_call(
        body, name="ret_bwd_kv", grid=(nc,),
        in_specs=[q_spec, k_spec, v_spec, v_spec, lg_spec, rope_spec, rope_spec],
        out_specs=[q_spec, v_spec],
        out_shape=[jax.ShapeDtypeStruct((s, RET_QK_WIDTH), BF16), jax.ShapeDtypeStruct((s, RET_V_WIDTH), BF16)],
        scratch_shapes=RET_SCRATCH,
        compiler_params=_cparams(),
    )(rqk, rqk, rv, d_out, _ret_log_gamma(), cos2, sin2)


def _xattn_probs(scores):
    sc = scores - jnp.max(scores, axis=-1, keepdims=True)
    p = jnp.exp(sc)
    return p / jnp.sum(p, axis=-1, keepdims=True)


def _xattn_heads(q_ref, kv_ref):
    sls = [slice(h * MEM_DIM, (h + 1) * MEM_DIM) for h in range(MEM_HEADS)]
    q = [q_ref[:, sl] for sl in sls]
    k = [kv_ref[:, sl] for sl in sls]
    v = [kv_ref[:, D_MODEL + h * MEM_DIM:D_MODEL + (h + 1) * MEM_DIM] for h in range(MEM_HEADS)]
    return sls, q, k, v


def _xattn_fwd(qm, kv, s):
    tq = min(XATTN_ROWS, s)
    heads = range(MEM_HEADS)

    def body(q_ref, kv_ref, o_ref):
        sls, q, k, v = _xattn_heads(q_ref, kv_ref)
        scores = [_dot(q[h], k[h], 1, 1) for h in heads]
        p = [_xattn_probs(scores[h]).astype(BF16) for h in heads]
        out = [_dot(p[h], v[h], 1, 0) for h in heads]
        for h in heads:
            o_ref[:, sls[h]] = out[h].astype(o_ref.dtype)

    return pl.pallas_call(
        body, name="xattn_fwd", grid=(s // tq,),
        in_specs=[pl.BlockSpec((tq, D_MODEL), lambda i: (i, 0)),
                  pl.BlockSpec((MEM_LEN, 2 * D_MODEL), lambda i: (0, 0))],
        out_specs=pl.BlockSpec((tq, D_MODEL), lambda i: (i, 0)),
        out_shape=jax.ShapeDtypeStruct((s, D_MODEL), BF16),
        compiler_params=_cparams(),
    )(qm, kv)


def _xattn_bwd(qm, kv, do, s):
    tq = min(XATTN_ROWS, s)

    def body(q_ref, kv_ref, do_ref, dq_ref, dkv_ref):
        i = pl.program_id(0)

        @pl.when(i == 0)
        def _():
            dkv_ref[...] = jnp.zeros_like(dkv_ref)

        heads = range(MEM_HEADS)
        sls, q, k, v = _xattn_heads(q_ref, kv_ref)
        d = [do_ref[:, sl] for sl in sls]
        scores = [_dot(q[h], k[h], 1, 1) for h in heads]
        dp = [_dot(d[h], v[h], 1, 1) for h in heads]
        p = [_xattn_probs(scores[h]) for h in heads]
        ds = [(p[h] * (dp[h] - jnp.sum(p[h] * dp[h], axis=-1, keepdims=True))).astype(BF16) for h in heads]
        dq = [_dot(ds[h], k[h], 1, 0) for h in heads]
        dk = [_dot(ds[h], q[h], 0, 0) for h in heads]
        dv = [_dot(p[h].astype(BF16), d[h], 0, 0) for h in heads]
        for h in heads:
            dq_ref[:, sls[h]] = (dq[h] * MEM_SCALE).astype(dq_ref.dtype)
            dkv_ref[:, sls[h]] += dk[h]
            dkv_ref[:, D_MODEL + h * MEM_DIM:D_MODEL + (h + 1) * MEM_DIM] += dv[h]

    row_blk = pl.BlockSpec((tq, D_MODEL), lambda i: (i, 0))
    kv_blk = pl.BlockSpec((MEM_LEN, 2 * D_MODEL), lambda i: (0, 0))
    return pl.pallas_call(
        body, name="xattn_bwd", grid=(s // tq,),
        in_specs=[row_blk, kv_blk, row_blk],
        out_specs=[row_blk, kv_blk],
        out_shape=[jax.ShapeDtypeStruct((s, D_MODEL), BF16), jax.ShapeDtypeStruct((MEM_LEN, 2 * D_MODEL), F32)],
        compiler_params=_cparams(),
    )(qm, kv, do)


def _place():
    x, y, c = lax.axis_index("x"), lax.axis_index("y"), lax.axis_index("c")
    others = [(1 - x, y), (x, 1 - y), (1 - x, 1 - y)]
    return x, y, c, others


def _slab(ref, axis, chip, size):
    start = pl.multiple_of(chip * size, LANES if axis == 1 else 16)
    if axis == 0:
        return ref.at[pl.ds(start, size), :]
    return ref.at[:, pl.ds(start, size)]


class _CommPlan:
    def __init__(self, ins, out_shape, scratch, start, finish):
        self.ins, self.out_shape, self.scratch, self.start, self.finish = ins, out_shape, scratch, start, finish

    @property
    def specs(self):
        any_spec = pl.BlockSpec(memory_space=pl.ANY)
        return [any_spec] * len(self.ins), [any_spec] * len(self.out_shape)


def _gather_plan(names, shards):
    spec = {name: (shape, axis) for name, shape, axis in BIG}
    nw = len(names)

    def shard_half(ref, c):
        rows = ref.shape[0] // 2
        return ref.at[pl.ds(pl.multiple_of(c * rows, 16), rows), :]

    def region(ref, w, chip, c):
        shape, axis = spec[names[w]]
        size = shape[axis] // N_CHIPS
        if axis == 0:
            rows = size // 2
            return ref.at[pl.ds(pl.multiple_of(chip * size + c * rows, 16), rows), :]
        rows = shape[0] // 2
        return ref.at[pl.ds(pl.multiple_of(c * rows, 16), rows), pl.ds(pl.multiple_of(chip * size, LANES), size)]

    def ops(shard, full, sems):
        ici_send, ici_recv, d2d_send, d2d_recv, local_sems = sems
        x, y, c, others = _place()
        mine, sibling = 2 * x + y, (x, y, 1 - c)
        local, over_ici, arrived, passed_on, from_sibling = [], [], [], [], []
        for w in range(nw):
            shape, axis = spec[names[w]]
            local.append(pltpu.make_async_copy(shard[w], _slab(full[w], axis, mine, shape[axis] // N_CHIPS),
                                               local_sems.at[w]))
            for t, (qx, qy) in enumerate(others):
                n, theirs = 3 * w + t, 2 * qx + qy
                over_ici.append(pltpu.make_async_remote_copy(
                    src_ref=shard_half(shard[w], c), dst_ref=region(full[w], w, mine, c),
                    send_sem=ici_send.at[n], recv_sem=ici_recv.at[n], device_id=(qx, qy, c), device_id_type=MESH))
                arrived.append(pltpu.make_async_remote_copy(
                    src_ref=shard_half(shard[w], c), dst_ref=region(full[w], w, theirs, c),
                    send_sem=ici_send.at[n], recv_sem=ici_recv.at[n], device_id=(qx, qy, c), device_id_type=MESH))
                passed_on.append(pltpu.make_async_remote_copy(
                    src_ref=region(full[w], w, theirs, c), dst_ref=region(full[w], w, theirs, c),
                    send_sem=d2d_send.at[n], recv_sem=d2d_recv.at[n], device_id=sibling, device_id_type=MESH))
                from_sibling.append(pltpu.make_async_remote_copy(
                    src_ref=region(full[w], w, theirs, c), dst_ref=region(full[w], w, theirs, 1 - c),
                    send_sem=d2d_send.at[n], recv_sem=d2d_recv.at[n], device_id=sibling, device_id_type=MESH))
        return local, over_ici, arrived, passed_on, from_sibling

    def start(shard, full, sems):
        local, over_ici, _, _, _ = ops(shard, full, sems)
        for cp in local + over_ici:
            cp.start()

    def finish(shard, full, sems):
        local, over_ici, arrived, passed_on, from_sibling = ops(shard, full, sems)
        for got, onward in zip(arrived, passed_on, strict=True):
            got.wait_recv()
            onward.start()
        for got in from_sibling:
            got.wait_recv()
        for cp in over_ici + passed_on:
            cp.wait_send()
        for cp in local:
            cp.wait()

    dma = pltpu.SemaphoreType.DMA
    return _CommPlan(
        ins=[shards[name] for name in names],
        out_shape=[jax.ShapeDtypeStruct(spec[name][0], BF16) for name in names],
        scratch=[dma((3 * nw,)), dma((3 * nw,)), dma((3 * nw,)), dma((3 * nw,)), dma((nw,))],
        start=start, finish=finish)


def _shard_shape(shape, axis):
    return tuple(d // N_CHIPS if a == axis else d for a, d in enumerate(shape))


def _exchange_plan(names, grads):
    spec = {name: (shape, axis) for name, shape, axis in BIG}
    nw = len(names)

    def ops(grad, stack, sems):
        send_sems, recv_sems, local_sems = sems
        x, y, c, others = _place()
        mine = 2 * x + y
        me, sibling = (x, y, c), (x, y, 1 - c)

        def dev(px, py, pc):
            return 4 * px + 2 * py + pc

        def copy(w, n, src, slot, to):
            return pltpu.make_async_remote_copy(
                src_ref=src, dst_ref=stack[w].at[slot], send_sem=send_sems.at[7 * w + n],
                recv_sem=recv_sems.at[7 * w + n], device_id=to, device_id_type=MESH)

        local, first, arrived, passed_on, from_sibling = [], [], [], [], []
        for w in range(nw):
            shape, axis = spec[names[w]]
            size = shape[axis] // N_CHIPS
            own = _slab(grad[w], axis, mine, size)
            local.append(pltpu.make_async_copy(own, stack[w].at[dev(*me)], local_sems.at[w]))
            first.append(copy(w, 0, own, dev(*me), sibling))
            from_sibling.append(copy(w, 0, own, dev(*sibling), me))
            for t, (qx, qy) in enumerate(others):
                got = stack[w].at[dev(qx, qy, c)]
                first.append(copy(w, 1 + t, _slab(grad[w], axis, 2 * qx + qy, size), dev(*me), (qx, qy, c)))
                arrived.append(copy(w, 1 + t, got, dev(qx, qy, c), me))
                passed_on.append(copy(w, 4 + t, got, dev(qx, qy, c), sibling))
                from_sibling.append(copy(w, 4 + t, got, dev(qx, qy, 1 - c), me))
        return local, first, arrived, passed_on, from_sibling

    def start(grad, stack, sems):
        local, first, _, _, _ = ops(grad, stack, sems)
        for cp in local + first:
            cp.start()

    def finish(grad, stack, sems):
        local, first, arrived, passed_on, from_sibling = ops(grad, stack, sems)
        for got, onward in zip(arrived, passed_on, strict=True):
            got.wait_recv()
            onward.start()
        for got in from_sibling:
            got.wait_recv()
        for cp in first + passed_on:
            cp.wait_send()
        for cp in local:
            cp.wait()

    dma = pltpu.SemaphoreType.DMA
    return _CommPlan(
        ins=[grads[name] for name in names],
        out_shape=[jax.ShapeDtypeStruct((N_DEV,) + _shard_shape(*spec[name]), BF16) for name in names],
        scratch=[dma((7 * nw,)), dma((7 * nw,)), dma((nw,))],
        start=start, finish=finish)


def _adamw(w, g, m, v):
    m = ADAM_B1 * m + (1.0 - ADAM_B1) * g
    v = ADAM_B2 * v + (1.0 - ADAM_B2) * (g * g)
    m_hat = m / (1.0 - ADAM_B1 ** ADAM_STEP)
    v_hat = v / (1.0 - ADAM_B2 ** ADAM_STEP)
    delta = -ADAM_LR * (m_hat / (jnp.sqrt(v_hat) + ADAM_EPS) + ADAM_WD * w)
    return delta, m, v


def _reduce_adamw(name, stack, w, m, v):
    rows, cols = w.shape
    tr = next(t for t in (256, 128, 64) if rows % t == 0)

    def body(s_ref, w_ref, m_ref, v_ref, g_ref, d_ref, nm_ref, nv_ref):
        g = s_ref[0].astype(F32)
        for d in range(1, N_DEV):
            g = g + s_ref[d].astype(F32)
        g_ref[...] = g
        d_ref[...], nm_ref[...], nv_ref[...] = _adamw(w_ref[...], g, m_ref[...], v_ref[...])

    blk = pl.BlockSpec((tr, cols), lambda i: (i, 0))
    return pl.pallas_call(
        body, name=name, grid=(rows // tr,),
        in_specs=[pl.BlockSpec((N_DEV, tr, cols), lambda i: (0, i, 0)), blk, blk, blk],
        out_specs=[blk] * 4, out_shape=[jax.ShapeDtypeStruct((rows, cols), F32)] * 4,
        compiler_params=_cparams(),
    )(stack, w, m, v)


def _small_step(pack, w, m, v):
    def body(p_ref, w_ref, m_ref, v_ref, g_ref, d_ref, nm_ref, nv_ref, loss_ref, all_ref, send_sems, recv_sems):
        x, y, c, _ = _place()
        me = 4 * x + 2 * y + c
        all_ref[me] = p_ref[...]
        sent = []
        for n in range(1, N_DEV):
            peer = me ^ n
            cp = pltpu.make_async_remote_copy(
                src_ref=p_ref, dst_ref=all_ref.at[me], send_sem=send_sems.at[n - 1], recv_sem=recv_sems.at[n - 1],
                device_id=(peer // 4, (peer // 2) % 2, peer % 2), device_id_type=MESH)
            cp.start()
            sent.append(cp)
        for n in range(1, N_DEV):
            peer = me ^ n
            pltpu.make_async_remote_copy(
                src_ref=p_ref, dst_ref=all_ref.at[peer], send_sem=send_sems.at[n - 1], recv_sem=recv_sems.at[n - 1],
                device_id=(peer // 4, (peer // 2) % 2, peer % 2), device_id_type=MESH).wait_recv()
        for cp in sent:
            cp.wait_send()
        tot = all_ref[0]
        for d in range(1, N_DEV):
            tot = tot + all_ref[d]
        g = tot[:SMALL_ROWS]
        g_ref[...] = g
        d_ref[...], nm_ref[...], nv_ref[...] = _adamw(w_ref[...], g, m_ref[...], v_ref[...])
        loss_ref[...] = jnp.sum(jnp.sum(tot[SMALL_ROWS:], axis=1, keepdims=True), axis=0, keepdims=True)

    vm = pl.BlockSpec(memory_space=pltpu.VMEM)
    small = jax.ShapeDtypeStruct((SMALL_ROWS, LANES), F32)
    return pl.pallas_call(
        body, name="small_step",
        in_specs=[vm] * 4, out_specs=[vm] * 5,
        out_shape=[small] * 4 + [jax.ShapeDtypeStruct((1, 1), F32)],
        scratch_shapes=[pltpu.VMEM((N_DEV, PACK_ROWS, LANES), F32),
                        pltpu.SemaphoreType.DMA((N_DEV - 1,)), pltpu.SemaphoreType.DMA((N_DEV - 1,))],
    )(pack, w, m, v)


LATER_WEIGHTS = tuple(name for name, _, _ in BIG if name != "w_in")


def _layer_step(x, mem, tgt, shards, vec):
    s = x.shape[0]
    d = D_MODEL
    tm = min(ROW_TILE, s)
    tl = min(WIDE_TILE, s)
    xb, cos2, sin2, w_in = _prep(x, _gather_plan(("w_in",), shards))
    bf = lambda w: ((s, w), BF16)
    f32 = lambda w: ((s, w), F32)

    w_sb, w_rqk = w_in[:, :OFF_RET_Q], w_in[:, OFF_RET_Q:OFF_RET_V]
    w_rvg, w_gate = w_in[:, OFF_RET_V:OFF_GATE], w_in[:, OFF_GATE:]
    q_scale = lambda width, q_width, scale: jnp.concatenate(
        [jnp.full((1, q_width), scale, F32), jnp.ones((1, width - q_width), F32)], axis=1)
    n_groups = 3 * SB_WIDTH // LANES

    def sb_epi(acc, t, i, j):
        scaled = acc * t[0]
        return [jnp.stack([scaled[:, g * LANES:(g + 1) * LANES] for g in range(n_groups)])], []

    (sb_qkv,) = _mm(
        "in_sb", xb, w_sb, s, 3 * SB_WIDTH, d, tm=tl, tn=3 * SB_WIDTH, tk=d, epi=sb_epi,
        ins=[(q_scale(3 * SB_WIDTH, SB_WIDTH, SB_SCALE), *_rowvec(3 * SB_WIDTH))],
        outs=[((n_groups, s, LANES), BF16, (n_groups, tl, LANES), lambda i, j: (0, i, 0))])

    def rope_epi(acc, t, i, j):
        cos, sin, scale = t
        parts = []
        for g in range(acc.shape[1] // RET_QK):
            xg = acc[:, g * RET_QK:(g + 1) * RET_QK]
            parts.append(xg * cos + _swap_halves(xg) * sin)
        return [jnp.concatenate(parts, axis=1) * scale], []

    rope_in = ((tl, RET_QK), lambda i, j: (i, 0))
    (rqk,) = _mm("in_rqk", xb, w_rqk, s, 2 * RET_QK_WIDTH, d, tm=tl, tn=2 * RET_QK_WIDTH, tk=d, epi=rope_epi,
                 chunk=MXU_COLS,
                 ins=[(cos2, *rope_in), (sin2, *rope_in),
                      (q_scale(2 * RET_QK_WIDTH, RET_QK_WIDTH, RET_SCALE), *_rowvec(2 * RET_QK_WIDTH))],
                 outs=[(*f32(2 * RET_QK_WIDTH), *_tile(tl, 2 * RET_QK_WIDTH))])
    (rvg,) = _mm("in_rvg", xb, w_rvg, s, 2 * RET_V_WIDTH, d, tm=tl, tn=2 * RET_V_WIDTH, tk=d, chunk=MXU_COLS,
                 epi=_plain, outs=[(*bf(2 * RET_V_WIDTH), *_tile(tl, 2 * RET_V_WIDTH))])
    (gates,) = _mm("in_gate", xb, w_gate, s, 2 * d, d, tm=tl, tn=2 * d, tk=d, chunk=MXU_COLS,
                   epi=lambda acc, t, i, j: ([_sigmoid(acc + t[0])], []),
                   ins=[(vec["b_gate"], *_rowvec(2 * d))], outs=[(*bf(2 * d), *_tile(tl, 2 * d))])

    sb_out, sb_out_f32, *gathered = _sb_fwd(sb_qkv, s, comm=_gather_plan(LATER_WEIGHTS, shards))
    wt = dict(zip(LATER_WEIGHTS, gathered, strict=True))
    ret, gated = _ret_fwd(rqk, rvg, s)
    (y_sb,) = _mm("sb_o", sb_out, wt["w_sb_o"], s, d, SB_WIDTH, tm=tl, tn=d, tk=SB_WIDTH, epi=_plain,
                  outs=[(*bf(d), *_tile(tl, d))])
    y_ret, mixin = _mm(
        "ret_o", gated, wt["w_ret_o"], s, d, RET_V_WIDTH, tm=tl, tn=d, tk=RET_V_WIDTH, chunk=MXU_COLS,
        epi=lambda acc, t, i, j: ([acc, t[0].astype(F32) * t[2].astype(F32) + t[1].astype(F32) * acc], []),
        ins=[(gates, *_tile(tl, d)), (gates, *_tile(tl, d, 1)), (y_sb, *_tile(tl, d))],
        outs=[(*bf(d), *_tile(tl, d)), (*bf(d), *_tile(tl, d))])

    def ln_epi(acc, t, i, j):
        *res, g, b = t
        prev = res[0] if len(res) == 1 else res[0] * res[1] + res[2]
        xhat, rstd = _norm(DN_ALPHA * prev + acc)
        return [xhat * g + b, xhat, rstd], []

    full = _tile(tm, d)
    col1 = ((tm, 1), lambda i, j: (i, 0))
    vec_in = lambda name: (vec[name], *_rowvec(d))
    ln_outs = [(*bf(d), *full), (*f32(d), *full), ((s, 1), F32, *col1)]
    x1b, xhat1, rstd1 = _mm(
        "mix_o", mixin, wt["w_mix_o"], s, d, d, tm=tm, tn=d, tk=d, epi=ln_epi,
        ins=[(x, *full), vec_in("ln1_g"), vec_in("ln1_b")], outs=ln_outs)

    (qm,) = _mm("mem_q", x1b, wt["w_mem_q"], s, d, d, tm=tl, tn=d, tk=d,
                epi=lambda acc, t, i, j: ([acc * MEM_SCALE], []), outs=[(*bf(d), *_tile(tl, d))])
    (kv,) = _mm("mem_kv", mem, wt["w_mem_kv"], MEM_LEN, 2 * d, d, tm=MEM_LEN, tn=d, tk=d, epi=_plain,
                outs=[((MEM_LEN, 2 * d), BF16, *_tile(MEM_LEN, d))])
    att = _xattn_fwd(qm, kv, s)
    x2b, xhat2, rstd2 = _mm(
        "mem_o", att, wt["w_mem_o"], s, d, d, tm=tm, tn=d, tk=d, epi=ln_epi,
        ins=[(xhat1, *full), vec_in("ln1_g"), vec_in("ln1_b"), vec_in("ln2_g"), vec_in("ln2_b")], outs=ln_outs)

    fh = FFN_HIDDEN
    tf = fh // 2
    (f1,) = _mm("ffn_in1", x2b, wt["w_ffn_in"], s, fh, d, tm=tl, tn=tf, tk=d, epi=_plain, j_outer=True,
                outs=[(*bf(fh), *_tile(tl, tf))])

    def swiglu_epi(acc, t, i, j):
        a = t[0].astype(F32)
        return [acc, a * _sigmoid(a) * acc], []

    f2, act = _mm(
        "ffn_in2", x2b, wt["w_ffn_in"], s, fh, d, tm=tm, tn=fh, tk=d, b_off=(0, 1), epi=swiglu_epi, chunk=MXU_COLS,
        ins=[(f1, *_tile(tm, fh))], outs=[(*bf(fh), *_tile(tm, fh)), (*bf(fh), *_tile(tm, fh))])

    def head_epi(acc, t, i, j):
        prev_hat, prev_g, prev_b, g, b, target = t
        xhat, rstd = _norm(DN_ALPHA * (prev_hat * prev_g + prev_b) + acc)
        err = xhat * g + b - target
        dy = err * (1.0 / d)
        du = _norm_bwd(dy * g, xhat, rstd)
        return [du], [_colsum(dy * xhat), _colsum(dy), _colsum(err * err) * (0.5 / d)]

    vec_acc = ((1, d), F32)
    du3b, dg3, db3, loss_cols = _mm(
        "ffn_out", act, wt["w_ffn_out"], s, d, fh, tm=tm, tn=d, tk=fh, epi=head_epi,
        ins=[(xhat2, *full), vec_in("ln2_g"), vec_in("ln2_b"), vec_in("ln3_g"), vec_in("ln3_b"), (tgt, *full)],
        outs=[(*bf(d), *full)], accs=[vec_acc] * 3)

    grads = {}
    ts = min(SEQ_TILE, s)

    def wgrad(name, a, b, m, n, tm_, tn_, tk_=None):
        (g,) = _mm(name, a, b, m, n, a.shape[0], tm=tm_, tn=tn_, tk=tk_ or ts, ta=True, epi=_plain,
                   outs=[((m, n), BF16, *_tile(tm_, tn_))])
        return g

    def ffn_bwd_epi(acc, t, i, j):
        a, b = t[0].astype(F32), t[1].astype(F32)
        sg = _sigmoid(a)
        return [acc * b * (sg * (1.0 + a * (1.0 - sg))), acc * (a * sg)], []

    df1, df2 = _mm(
        "ffn_out_t", du3b, wt["w_ffn_out"], s, fh, d, tm=tm, tn=fh, tk=d, tb=True, epi=ffn_bwd_epi, chunk=MXU_COLS,
        ins=[(f1, *_tile(tm, fh)), (f2, *_tile(tm, fh))],
        outs=[(*bf(fh), *_tile(tm, fh)), (*bf(fh), *_tile(tm, fh))])
    grads["w_ffn_out"] = wgrad("g_ffn_out", act, du3b, fh, d, tf, d)
    grads["w_ffn_in"] = jnp.concatenate(
        [wgrad("g_ffn_in1", x2b, df1, d, fh, d, tf), wgrad("g_ffn_in2", x2b, df2, d, fh, d, tf)], axis=1)
    (dx2a,) = _mm("ffn_in1_t", df1, wt["w_ffn_in"], s, d, fh, tm=tl, tn=d, tk=fh, tb=True, epi=_plain,
                  outs=[(*f32(d), *_tile(tl, d))])

    def ln_bwd(name, a, b, k, tk, b_off, more, scales, xhat, rstd, g):
        def epi(acc, t, i, j):
            *extra, xh, rs, gg = t
            dy = acc
            for e, sc in zip(extra, scales, strict=True):
                dy = dy + e.astype(F32) * sc
            return [_norm_bwd(dy * gg, xh, rs)], [_colsum(dy * xh), _colsum(dy)]

        return _mm(name, a, b, s, d, k, tm=tm, tn=d, tk=tk, tb=True, b_off=b_off, epi=epi,
                   ins=[(e, *full) for e in more] + [(xhat, *full), (rstd, *col1), (g, *_rowvec(d))],
                   outs=[(*bf(d), *full)], accs=[vec_acc] * 2)

    du2b, dg2, db2 = ln_bwd("ffn_in2_t", df2, wt["w_ffn_in"], fh, fh, (0, 1), [dx2a, du3b], [1.0, DN_ALPHA],
                            xhat2, rstd2, vec["ln2_g"])

    (datt,) = _mm("mem_o_t", du2b, wt["w_mem_o"], s, d, d, tm=tl, tn=d, tk=d, tb=True, epi=_plain,
                  outs=[(*bf(d), *_tile(tl, d))])
    grads["w_mem_o"] = wgrad("g_mem_o", att, du2b, d, d, d, d)
    dqm, dkv = _xattn_bwd(qm, kv, datt, s)
    grads["w_mem_q"] = wgrad("g_mem_q", x1b, dqm, d, d, d, d)
    grads["w_mem_kv"] = wgrad("g_mem_kv", mem, dkv, d, 2 * d, d, d, MEM_LEN)
    du1b, dg1, db1 = ln_bwd("mem_q_t", dqm, wt["w_mem_q"], d, d, (0, 0), [du2b], [DN_ALPHA],
                            xhat1, rstd1, vec["ln1_g"])

    def merge_bwd_epi(acc, t, i, j):
        g0, g1, ysb, yret = (v.astype(F32) for v in t)
        dgate0 = acc * ysb * (g0 * (1.0 - g0))
        dgate1 = acc * yret * (g1 * (1.0 - g1))
        return [dgate0, dgate1, acc * g0, acc * g1], [_colsum(dgate0), _colsum(dgate1)]

    dgate0, dgate1, dy_sb, dy_ret, dbg0, dbg1 = _mm(
        "mix_o_t", du1b, wt["w_mix_o"], s, d, d, tm=tm, tn=d, tk=d, tb=True, epi=merge_bwd_epi,
        ins=[(gates, *full), (gates, *_tile(tm, d, 1)), (y_sb, *full), (y_ret, *full)],
        outs=[(*bf(d), *full)] * 4, accs=[vec_acc] * 2)
    grads["w_mix_o"] = wgrad("g_mix_o", mixin, du1b, d, d, d, d)
    grads["w_sb_o"] = wgrad("g_sb_o", sb_out, dy_sb, SB_WIDTH, d, SB_WIDTH, d)
    grads["w_ret_o"] = wgrad("g_ret_o", gated, dy_ret, RET_V_WIDTH, d, RET_V_WIDTH, d)
    (dsb_out,) = _mm("sb_o_t", dy_sb, wt["w_sb_o"], s, SB_WIDTH, d, tm=tl, tn=SB_WIDTH, tk=d, tb=True, epi=_plain,
                     outs=[(*bf(SB_WIDTH), *_tile(tl, SB_WIDTH))])

    def gate_norm_bwd_epi(acc, t, i, j):
        r, g = t[0], t[1].astype(F32)
        drg, dret = [], []
        for h in range(acc.shape[1] // RET_V):
            sl = slice(h * RET_V, (h + 1) * RET_V)
            xhat, rstd = _norm(r[:, sl])
            gg, dd = g[:, sl], acc[:, sl]
            sg = _sigmoid(gg)
            drg.append(dd * xhat * (sg * (1.0 + gg * (1.0 - sg))))
            dret.append(_norm_bwd(dd * (gg * sg), xhat, rstd))
        return [jnp.concatenate(drg, axis=1), jnp.concatenate(dret, axis=1)], []

    drg, dret = _mm(
        "ret_o_t", dy_ret, wt["w_ret_o"], s, RET_V_WIDTH, d, tm=tm, tn=d, tk=d, tb=True, epi=gate_norm_bwd_epi,
        chunk=MXU_COLS,
        ins=[(ret, *full), (rvg, *_tile(tm, d, 1))],
        outs=[(*bf(RET_V_WIDTH), *full)] * 2)

    drq = _ret_bwd_q(rqk, rvg, dret, cos2, sin2, s)
    drk, drv = _ret_bwd_kv(rqk, rvg, dret, cos2, sin2, s)
    dsq, dsk, dsv, *stacked = _sb_bwd(sb_qkv, sb_out_f32, dsb_out, s, comm=_exchange_plan(LATER_WEIGHTS, grads))
    stacks = dict(zip(LATER_WEIGHTS, stacked, strict=True))

    dh = jnp.concatenate([dsq, dsk, dsv, drq, drk, drv, drg, dgate0, dgate1], axis=1)
    grads["w_in"] = wgrad("g_in", xb, dh, d, IN_WIDTH, d, IN_WIDTH // N_CHIPS)
    grad_x, stacks["w_in"] = _mm(
        "in_t", dh, w_in, s, d, IN_WIDTH, tm=tl, tn=d, tk=IN_WIDTH // N_CHIPS, tb=True,
        epi=lambda acc, t, i, j: ([acc + DN_ALPHA * t[0].astype(F32)], []),
        ins=[(du1b, *_tile(tl, d))], outs=[(*f32(d), *_tile(tl, d))], comm=_exchange_plan(("w_in",), grads))

    small = {"b_gate": jnp.concatenate([dbg0, dbg1], axis=1), "ln1_g": dg1, "ln1_b": db1, "ln2_g": dg2,
             "ln2_b": db2, "ln3_g": dg3, "ln3_b": db3}
    return grad_x, stacks, small, loss_cols


def kernel(x, mem, w_in, b_gate, w_sb_o, w_ret_o, w_mix_o, ln1_g, ln1_b, w_mem_q, w_mem_kv, w_mem_o, ln2_g, ln2_b, w_ffn_in, w_ffn_out, ln3_g, ln3_b, loss_target, m_w_in, m_b_gate, m_w_sb_o, m_w_ret_o, m_w_mix_o, m_ln1_g, m_ln1_b, m_w_mem_q, m_w_mem_kv, m_w_mem_o, m_ln2_g, m_ln2_b, m_w_ffn_in, m_w_ffn_out, m_ln3_g, m_ln3_b, v_w_in, v_b_gate, v_w_sb_o, v_w_ret_o, v_w_mix_o, v_ln1_g, v_ln1_b, v_w_mem_q, v_w_mem_kv, v_w_mem_o, v_ln2_g, v_ln2_b, v_w_ffn_in, v_w_ffn_out, v_ln3_g, v_ln3_b):
    given = dict(locals())
    s = x.shape[1]
    x2d = x.reshape(s, D_MODEL)
    tgt = loss_target.reshape(s, D_MODEL)
    mem2d = mem.reshape(MEM_LEN, D_MODEL)
    shard = {name: given[name].reshape(_shard_shape(shape, axis)) for name, shape, axis in BIG}
    vec = {name: given[name] for name in SMALL}

    shards_bf = {name: _cast_bf16("cast_" + name, shard[name]) for name, _, _ in BIG}

    grad_x, stacks, small, loss_cols = _layer_step(x2d, mem2d, tgt, shards_bf, vec)

    out = {}
    for name, shape, axis in BIG:
        stack = stacks[name]
        shp = given[name].shape
        res = _reduce_adamw("adamw_" + name, stack, shard[name], given["m_" + name].reshape(stack.shape[1:]),
                            given["v_" + name].reshape(stack.shape[1:]))
        out[name] = [r.reshape(shp) for r in res]

    pack = jnp.concatenate([small[name] for name in SMALL] + [loss_cols], axis=1).reshape(PACK_ROWS, LANES)
    cat = lambda pre: jnp.concatenate([given[pre + name] for name in SMALL], axis=1).reshape(SMALL_ROWS, LANES)
    *res, loss = _small_step(pack, cat(""), cat("m_"), cat("v_"))
    flat = [r.reshape(1, SMALL_LEN) for r in res]
    off = 0
    for name in SMALL:
        n = given[name].shape[1]
        out[name] = [r[:, off:off + n] for r in flat]
        off += n

    return (loss.reshape(()), grad_x.reshape(x.shape),
            *[out[name][0] for name in WEIGHT_ORDER], *[out[name][1] for name in WEIGHT_ORDER],
            *[out[name][2] for name in WEIGHT_ORDER], *[out[name][3] for name in WEIGHT_ORDER])
```

```python
import functools

import jax
import jax.numpy as jnp
import numpy as np
from jax import lax
from jax.experimental import pallas as pl
from jax.experimental.pallas import tpu as pltpu

F32, BF16 = jnp.float32, jnp.bfloat16
MESH = pl.DeviceIdType.MESH

D_MODEL = 1024
MEM_LEN = 256
SB_HEADS, SB_DIM, SB_WIDTH = 8, 64, 512
RET_HEADS, RET_QK, RET_V = 4, 128, 256
RET_QK_WIDTH, RET_V_WIDTH = 512, 1024
ROPE_BASE = 10000.0
MEM_HEADS, MEM_DIM = 4, 256
FFN_HIDDEN = 2816
IN_WIDTH = 6656
OFF_RET_Q, OFF_RET_V, OFF_RET_G, OFF_GATE = 1536, 2560, 3584, 4608
DN_ALPHA = 2.0 ** 0.25
LN_EPS = 1e-5
SB_SCALE = SB_DIM ** -0.5
SB_DEAD = -110.0
RET_SCALE = RET_QK ** -0.5
MEM_SCALE = MEM_DIM ** -0.5
ADAM_LR, ADAM_B1, ADAM_B2, ADAM_EPS, ADAM_WD, ADAM_STEP = 0.001, 0.9, 0.999, 1e-08, 0.01, 10

N_DEV, N_CHIPS = 8, 4

LANES = 128
MXU_COLS = 256
VMEM_LIMIT_BYTES = 52 * 2 ** 20
ROW_TILE = 512
WIDE_TILE = 1024
SEQ_TILE = 2048
SB_BLOCK = 256
RET_BLOCK = 256
RET_CHUNKS_PER_STEP = 4
XATTN_ROWS = 1024

BIG = (
    ("w_in", (D_MODEL, IN_WIDTH), 1),
    ("w_sb_o", (SB_WIDTH, D_MODEL), 1),
    ("w_ret_o", (RET_V_WIDTH, D_MODEL), 0),
    ("w_mix_o", (D_MODEL, D_MODEL), 0),
    ("w_mem_q", (D_MODEL, D_MODEL), 0),
    ("w_mem_kv", (D_MODEL, 2 * D_MODEL), 1),
    ("w_mem_o", (D_MODEL, D_MODEL), 0),
    ("w_ffn_in", (D_MODEL, 2 * FFN_HIDDEN), 1),
    ("w_ffn_out", (FFN_HIDDEN, D_MODEL), 0),
)
SMALL = ("b_gate", "ln1_g", "ln1_b", "ln2_g", "ln2_b", "ln3_g", "ln3_b")
SMALL_LEN = 2 * D_MODEL + 6 * D_MODEL
SMALL_ROWS = SMALL_LEN // LANES
PACK_ROWS = SMALL_ROWS + D_MODEL // LANES
WEIGHT_ORDER = ("w_in", "b_gate", "w_sb_o", "w_ret_o", "w_mix_o", "ln1_g", "ln1_b", "w_mem_q", "w_mem_kv",
                "w_mem_o", "ln2_g", "ln2_b", "w_ffn_in", "w_ffn_out", "ln3_g", "ln3_b")


def _cparams():
    return pltpu.CompilerParams(vmem_limit_bytes=VMEM_LIMIT_BYTES)


def _dot(a, b, ca, cb):
    return lax.dot_general(a, b, (((ca,), (cb,)), ((), ())), preferred_element_type=F32)


def _sigmoid(x):
    return 1.0 / (1.0 + jnp.exp(-x))


def _mm(name, a, b, m, n, k, *, tm, tn, tk, epi, outs, ins=(), accs=(), ta=False, tb=False,
        a_off=(0, 0), b_off=(0, 0), j_outer=False, comm=None, chunk=None):
    assert m % tm == 0 and n % tn == 0 and k % tk == 0, (name, m, n, k, tm, tn, tk)
    assert chunk is None or (k == tk and tn % chunk == 0), name
    ni, nj, nk = m // tm, n // tn, k // tk
    assert not accs or nj == 1, name
    ij = (lambda g0, g1: (g1, g0)) if j_outer else (lambda g0, g1: (g0, g1))

    def spec(block, index):
        return pl.BlockSpec(block, lambda g0, g1, kk: index(*ij(g0, g1), kk))

    if ta:
        a_spec = spec((tk, tm), lambda i, j, kk: (kk + a_off[0], i + a_off[1]))
    else:
        a_spec = spec((tm, tk), lambda i, j, kk: (i + a_off[0], kk + a_off[1]))
    if tb:
        b_spec = spec((tn, tk), lambda i, j, kk: (j + b_off[0], kk + b_off[1]))
    else:
        b_spec = spec((tk, tn), lambda i, j, kk: (kk + b_off[0], j + b_off[1]))
    in_specs = [a_spec, b_spec]
    for _, bs, im in ins:
        in_specs.append(spec(bs, lambda i, j, kk, im=im: im(i, j)))
    out_specs, out_shape = [], []
    for shape, dtype, bs, im in outs:
        out_specs.append(spec(bs, lambda i, j, kk, im=im: im(i, j)))
        out_shape.append(jax.ShapeDtypeStruct(shape, dtype))
    for shape, dtype in accs:
        out_specs.append(spec(shape, lambda i, j, kk, nd=len(shape): (0,) * nd))
        out_shape.append(jax.ShapeDtypeStruct(shape, dtype))
    n_in, n_out, n_acc = len(ins), len(outs), len(accs)
    ca, cb = (0 if ta else 1), (1 if tb else 0)
    grid = (*ij(ni, nj), nk)
    comm_ins, comm_outs, comm_scratch = [], [], []
    if comm is not None:
        comm_in_specs, comm_out_specs = comm.specs
        comm_ins, comm_outs, comm_scratch = list(comm.ins), list(comm.out_shape), list(comm.scratch)
        in_specs += comm_in_specs
        out_specs += comm_out_specs
        out_shape += comm_outs
    n_ci, n_co = len(comm_ins), len(comm_outs)

    def body(*refs):
        a_ref, b_ref = refs[:2]
        in_refs = refs[2:2 + n_in]
        ci_refs = refs[2 + n_in:2 + n_in + n_ci]
        rest = refs[2 + n_in + n_ci:]
        out_refs, acc_refs = rest[:n_out], rest[n_out:n_out + n_acc]
        co_refs = rest[n_out + n_acc:n_out + n_acc + n_co]
        scratch = rest[n_out + n_acc + n_co:]
        sem_refs, scratch = scratch[:len(comm_scratch)], scratch[len(comm_scratch):]
        (i, j), kk = ij(pl.program_id(0), pl.program_id(1)), pl.program_id(2)
        if comm is not None:
            first_step, last_step = _grid_ends(grid)
            pl.when(first_step)(lambda: comm.start(ci_refs, co_refs, sem_refs))
        def finish(acc, cols=slice(None)):
            def of(r):
                return r[..., cols] if r.shape[-1] == tn else r[...]

            o_tiles, a_tiles = epi(acc, [of(r) for r in in_refs], i, j)
            for r, t in zip(out_refs, o_tiles, strict=True):
                r[..., cols] = t.astype(r.dtype)
            if n_acc:
                @pl.when(i == 0)
                def _():
                    for r, t in zip(acc_refs, a_tiles, strict=True):
                        r[..., cols] = t

                @pl.when(i > 0)
                def _():
                    for r, t in zip(acc_refs, a_tiles, strict=True):
                        r[..., cols] += t

        if chunk is not None:
            a_tile = a_ref[...].astype(BF16)
            for c0 in range(0, tn, chunk):
                cols = slice(c0, c0 + chunk)
                b_part = b_ref[cols, :] if tb else b_ref[:, cols]
                finish(_dot(a_tile, b_part.astype(BF16), ca, cb), cols)
            if comm is not None:
                pl.when(last_step)(lambda: comm.finish(ci_refs, co_refs, sem_refs))
            return

        part = _dot(a_ref[...].astype(BF16), b_ref[...].astype(BF16), ca, cb)
        if nk == 1:
            finish(part)
        else:
            acc_ref = scratch[0]

            @pl.when(kk == 0)
            def _():
                acc_ref[...] = part

            @pl.when(kk > 0)
            def _():
                acc_ref[...] += part

            @pl.when(kk == nk - 1)
            def _():
                finish(acc_ref[...])

        if comm is not None:
            pl.when(last_step)(lambda: comm.finish(ci_refs, co_refs, sem_refs))

    res = pl.pallas_call(
        body, name=name, grid=grid, in_specs=in_specs, out_specs=out_specs, out_shape=out_shape,
        scratch_shapes=comm_scratch + ([pltpu.VMEM((tm, tn), F32)] if nk > 1 else []),
        compiler_params=_cparams(),
    )(a, b, *[x for x, _, _ in ins], *comm_ins)
    return res


def _grid_ends(grid):
    ids = [pl.program_id(ax) for ax in range(len(grid))]
    first = functools.reduce(jnp.logical_and, [p == 0 for p in ids])
    last = functools.reduce(jnp.logical_and, [p == n - 1 for p, n in zip(ids, grid, strict=True)])
    return first, last


def _tile(tm, tn, dj=0):
    return (tm, tn), (lambda i, j: (i, j + dj))


def _rowvec(tn, dj=0):
    return (1, tn), (lambda i, j: (0, j + dj))


def _plain(acc, tiles, i, j):
    return [acc], []


def _ew(name, fn, ins, outs, rows, tr):
    assert rows % tr == 0, (name, rows, tr)
    in_specs = []
    for x in ins:
        if x.shape[0] == rows:
            in_specs.append(pl.BlockSpec((tr, x.shape[1]), lambda i: (i, 0)))
        else:
            in_specs.append(pl.BlockSpec(x.shape, lambda i: (0, 0)))
    n_in = len(ins)

    def body(*refs):
        res = fn(*[r[...] for r in refs[:n_in]])
        for r, t in zip(refs[n_in:], res, strict=True):
            r[...] = t.astype(r.dtype)

    return pl.pallas_call(
        body, name=name, grid=(rows // tr,), in_specs=in_specs,
        out_specs=[pl.BlockSpec((tr, w), lambda i: (i, 0)) for w, _ in outs],
        out_shape=[jax.ShapeDtypeStruct((rows, w), dt) for w, dt in outs],
        compiler_params=_cparams(),
    )(*ins)


def _cast_bf16(name, x):
    rows = x.shape[0]
    tr = next(t for t in (512, 256, 64) if rows % t == 0)
    return _ew(name, lambda v: (v,), [x], [(x.shape[1], BF16)], rows, tr)[0]


def _prep(x, comm):
    s = x.shape[0]
    half = RET_QK // 2
    inv = 1.0 / (ROPE_BASE ** (jnp.arange(half, dtype=F32) / half))
    inv2 = jnp.concatenate([inv, inv]).reshape(1, RET_QK)
    sign = jnp.concatenate([-jnp.ones((half,), F32), jnp.ones((half,), F32)]).reshape(1, RET_QK)
    tr = min(ROW_TILE, s)
    grid = (s // tr,)
    c_in_specs, c_out_specs, c_out_shape, c_scratch, c_ins, split = _host(comm, 3, 3)

    def body(*refs):
        (x_ref, inv_ref, sign_ref), (xb_ref, cos_ref, sin_ref), _, riding = split(refs)
        i = pl.program_id(0)
        first_step, last_step = _grid_ends(grid)
        pl.when(first_step)(lambda: comm.start(*riding))
        xb_ref[...] = x_ref[...].astype(BF16)
        pos = (lax.broadcasted_iota(jnp.int32, (tr, RET_QK), 0) + i * tr).astype(F32)
        ang = pos * inv_ref[...]
        cos_ref[...] = jnp.cos(ang)
        sin_ref[...] = jnp.sin(ang) * sign_ref[...]
        pl.when(last_step)(lambda: comm.finish(*riding))

    vec = pl.BlockSpec((1, RET_QK), lambda i: (0, 0))
    row = lambda w: pl.BlockSpec((tr, w), lambda i: (i, 0))
    return pl.pallas_call(
        body, name="prep", grid=grid,
        in_specs=[row(D_MODEL), vec, vec] + c_in_specs,
        out_specs=[row(D_MODEL), row(RET_QK), row(RET_QK)] + c_out_specs,
        out_shape=[jax.ShapeDtypeStruct((s, D_MODEL), BF16), jax.ShapeDtypeStruct((s, RET_QK), F32),
                   jax.ShapeDtypeStruct((s, RET_QK), F32)] + c_out_shape,
        scratch_shapes=c_scratch, compiler_params=_cparams(),
    )(x, inv2, sign, *c_ins)


def _swap_halves(x):
    return pltpu.roll(x, RET_QK // 2, 1)


def _norm(u):
    mu = jnp.mean(u, axis=-1, keepdims=True)
    d = u - mu
    var = jnp.mean(d * d, axis=-1, keepdims=True)
    rstd = lax.rsqrt(var + LN_EPS)
    return d * rstd, rstd


def _norm_bwd(dxh, xhat, rstd):
    m1 = jnp.mean(dxh, axis=-1, keepdims=True)
    m2 = jnp.mean(dxh * xhat, axis=-1, keepdims=True)
    return rstd * (dxh - m1 - xhat * m2)


def _colsum(t):
    return jnp.sum(t, axis=0, keepdims=True)


def _split_mm(t, tri):
    hi = t.astype(BF16)
    lo = (t - hi.astype(F32)).astype(BF16)
    return _dot(hi, tri, 1, 0) + _dot(lo, tri, 1, 0)


def _sb_masks():
    t = SB_BLOCK
    lane = lax.broadcasted_iota(jnp.int32, (1, LANES), 1)
    first = lane < SB_DIM
    m0 = jnp.where(first, 1.0, 0.0).astype(BF16)
    m1 = jnp.where(first, 0.0, 1.0).astype(BF16)
    row = lax.broadcasted_iota(jnp.int32, (t, t), 0)
    col = lax.broadcasted_iota(jnp.int32, (t, t), 1)
    return first, (m0, m1), row, col


def _sb_logits(qh, k, causal):
    z = _dot(qh, k, 1, 1)
    lp = jnp.log(1.0 + jnp.exp(-jnp.abs(z)))
    a = jnp.minimum(z, 0.0) - lp
    r = jnp.minimum(-z, 0.0) - lp
    if causal is not None:
        r = jnp.where(causal, r, 0.0)
    return a, r


def _sb_walk(i, blocks, l_ref, causal):
    pl.when(i == 0)(lambda: blocks([(i, causal)]))
    pl.when(i > 0)(lambda: blocks([(i, causal), (i - 1, None)]))

    def alive():
        top = jnp.max(functools.reduce(jnp.maximum, [l_ref[c] for c in range(l_ref.shape[0])]))
        return jnp.where(top > SB_DEAD, 1, 0)

    def cond(c):
        return jnp.logical_and(c[0] < i, c[1] > 0)

    def step(c):
        blocks([(i - 1 - c[0], None)])
        return c[0] + 1, alive()

    lax.while_loop(cond, step, (jnp.int32(1), alive()))


def _host(comm, n_in, n_out):
    if comm is None:
        return [], [], [], [], [], lambda refs: (refs[:n_in], refs[n_in:n_in + n_out], refs[n_in + n_out:], None)
    in_specs, out_specs = comm.specs
    n_ci, n_co, n_sem = len(comm.ins), len(comm.out_shape), len(comm.scratch)

    def split(refs):
        ins, ci = refs[:n_in], refs[n_in:n_in + n_ci]
        rest = refs[n_in + n_ci:]
        outs, co = rest[:n_out], rest[n_out:n_out + n_co]
        sems, scratch = rest[n_out + n_co:n_out + n_co + n_sem], rest[n_out + n_co + n_sem:]
        return ins, outs, scratch, (ci, co, sems)

    return in_specs, out_specs, list(comm.out_shape), list(comm.scratch), list(comm.ins), split


def _sb_qkv_specs(s, g):
    groups = SB_HEADS // 2 // g
    return [pl.BlockSpec((g, SB_BLOCK, LANES), lambda p, i: (p, i, 0)),
            pl.BlockSpec((g, s, LANES), lambda p, i: (groups + p, 0, 0)),
            pl.BlockSpec((g, s, LANES), lambda p, i: (2 * groups + p, 0, 0))]


def _sb_fwd(qkv, s, comm=None):
    t = SB_BLOCK
    g = 2
    nq = s // t
    grid = (SB_HEADS // 2 // g, nq)
    c_in_specs, c_out_specs, c_out_shape, c_scratch, c_ins, split = _host(comm, 3, 2)

    def body(*refs):
        (q_ref, k_ref, v_ref), (o_ref, of_ref), (l_ref, acc_ref), riding = split(refs)
        i = pl.program_id(1)
        if comm is not None:
            first_step, last_step = _grid_ends(grid)
            pl.when(first_step)(lambda: comm.start(*riding))
        first, hmask, row, col = _sb_masks()
        after = jnp.where(row > col, 1.0, 0.0).astype(BF16)
        causal = col < row
        heads = [(p, h) for p in range(g) for h in range(2)]
        qh = {(p, h): q_ref[p] * hmask[h] for p, h in heads}
        l_ref[...] = jnp.zeros_like(l_ref)
        acc_ref[...] = jnp.zeros_like(acc_ref)

        def blocks(todo):
            chains = [(b, p, h) for b in range(len(todo)) for p, h in heads]
            starts = [pl.multiple_of(kb * t, t) for kb, _ in todo]
            ks = {(b, p): k_ref[p, pl.ds(st, t), :] for b, st in enumerate(starts) for p in range(g)}
            vs = {(b, p): v_ref[p, pl.ds(st, t), :] for b, st in enumerate(starts) for p in range(g)}
            ar = {(b, p, h): _sb_logits(qh[p, h], ks[b, p], todo[b][1]) for b, p, h in chains}
            later = {c: _split_mm(ar[c][1], after) for c in chains}
            carry = {(p, h): l_ref[2 * p + h] for p, h in heads}
            w = {}
            for b, (_, mask) in enumerate(todo):
                for p, h in heads:
                    wc = jnp.exp(ar[b, p, h][0] + later[b, p, h] + carry[p, h])
                    w[b, p, h] = wc if mask is None else jnp.where(mask, wc, 0.0)
                carry = {(p, h): carry[p, h] + jnp.sum(ar[b, p, h][1], axis=1, keepdims=True) for p, h in heads}
            pv = {(b, p, h): _dot(w[b, p, h].astype(BF16), vs[b, p], 1, 0) for b, p, h in chains}
            for p in range(g):
                lanes = slice(p * LANES, (p + 1) * LANES)
                acc = acc_ref[:, lanes]
                for b in range(len(todo)):
                    acc = acc + jnp.where(first, pv[b, p, 0], pv[b, p, 1])
                acc_ref[:, lanes] = acc
            for p, h in heads:
                l_ref[2 * p + h] = carry[p, h]

        _sb_walk(i, blocks, l_ref, causal)
        o_ref[...] = acc_ref[...].astype(o_ref.dtype)
        of_ref[...] = acc_ref[...]
        if comm is not None:
            pl.when(last_step)(lambda: comm.finish(*riding))

    blk = pl.BlockSpec((t, g * LANES), lambda p, i: (i, p))
    return pl.pallas_call(
        body, name="sb_fwd", grid=grid,
        in_specs=_sb_qkv_specs(s, g) + c_in_specs,
        out_specs=[blk, blk] + c_out_specs,
        out_shape=[jax.ShapeDtypeStruct((s, SB_WIDTH), BF16), jax.ShapeDtypeStruct((s, SB_WIDTH), F32)] + c_out_shape,
        scratch_shapes=c_scratch + [pltpu.VMEM((2 * g, t, 1), F32), pltpu.VMEM((t, g * LANES), F32)],
        compiler_params=_cparams(),
    )(qkv, qkv, qkv, *c_ins)


def _sb_bwd(qkv, o, do, s, comm=None):
    t = SB_BLOCK
    g = 2
    nq = s // t
    grid = (SB_HEADS // 2 // g, nq)
    c_in_specs, c_out_specs, c_out_shape, c_scratch, c_ins, split = _host(comm, 5, 3)

    def body(*refs):
        ((q_ref, k_ref, v_ref, o_ref, do_ref), (dq_ref, dk_ref, dv_ref),
         (l_ref, e_ref, dq_acc, dk_acc, dv_acc), riding) = split(refs)
        i = pl.program_id(1)
        if comm is not None:
            first_step, last_step = _grid_ends(grid)
            pl.when(first_step)(lambda: comm.start(*riding))
        first, hmask, row, col = _sb_masks()
        after = jnp.where(row > col, 1.0, 0.0).astype(BF16)
        from_here = jnp.where(row >= col, 1.0, 0.0).astype(BF16)
        causal = col < row

        @pl.when(i == 0)
        def _():
            dk_acc[...] = jnp.zeros_like(dk_acc)
            dv_acc[...] = jnp.zeros_like(dv_acc)

        heads = [(p, h) for p in range(g) for h in range(2)]
        lanes = [slice(p * LANES, (p + 1) * LANES) for p in range(g)]
        q = [q_ref[p] for p in range(g)]
        do_ = [do_ref[:, lanes[p]] for p in range(g)]
        qh = {(p, h): q[p] * hmask[h] for p, h in heads}
        doh = {(p, h): do_[p] * hmask[h] for p, h in heads}
        total = {}
        for p in range(g):
            prod = do_[p].astype(F32) * o_ref[:, lanes[p]]
            total[p, 0] = jnp.sum(jnp.where(first, prod, 0.0), axis=1, keepdims=True)
            total[p, 1] = jnp.sum(jnp.where(first, 0.0, prod), axis=1, keepdims=True)
        l_ref[...] = jnp.zeros_like(l_ref)
        e_ref[...] = jnp.zeros_like(e_ref)
        dq_acc[...] = jnp.zeros_like(dq_acc)

        def blocks(todo):
            chains = [(b, p, h) for b in range(len(todo)) for p, h in heads]
            starts = [pl.multiple_of(kb * t, t) for kb, _ in todo]
            ks = {(b, p): k_ref[p, pl.ds(st, t), :] for b, st in enumerate(starts) for p in range(g)}
            vs = {(b, p): v_ref[p, pl.ds(st, t), :] for b, st in enumerate(starts) for p in range(g)}
            ar = {(b, p, h): _sb_logits(qh[p, h], ks[b, p], todo[b][1]) for b, p, h in chains}
            dw = {(b, p, h): _dot(doh[p, h], vs[b, p], 1, 1) for b, p, h in chains}
            later = {c: _split_mm(ar[c][1], after) for c in chains}
            carry = {(p, h): l_ref[2 * p + h] for p, h in heads}
            wb = {}
            for b, (_, mask) in enumerate(todo):
                for p, h in heads:
                    wc = jnp.exp(ar[b, p, h][0] + later[b, p, h] + carry[p, h])
                    wb[b, p, h] = (wc if mask is None else jnp.where(mask, wc, 0.0)).astype(BF16)
                carry = {(p, h): carry[p, h] + jnp.sum(ar[b, p, h][1], axis=1, keepdims=True) for p, h in heads}
            dvs = {(b, p, h): _dot(wb[b, p, h], do_[p], 0, 0) for b, p, h in chains}
            e = {c: dw[c] * wb[c].astype(F32) for c in chains}
            suffix = {c: _split_mm(e[c], from_here) for c in chains}
            e_carry = {(p, h): e_ref[2 * p + h] for p, h in heads}
            dz = {}
            for b, (_, mask) in enumerate(todo):
                for p, h in heads:
                    before = total[p, h] - (suffix[b, p, h] + e_carry[p, h])
                    dzc = e[b, p, h] - jnp.exp(ar[b, p, h][0]) * (e[b, p, h] + before)
                    dz[b, p, h] = (dzc if mask is None else jnp.where(mask, dzc, 0.0)).astype(BF16)
                e_carry = {(p, h): e_carry[p, h] + jnp.sum(e[b, p, h], axis=1, keepdims=True) for p, h in heads}
            dqs = {(b, p, h): _dot(dz[b, p, h], ks[b, p], 1, 0) for b, p, h in chains}
            dks = {(b, p, h): _dot(dz[b, p, h], q[p], 0, 0) for b, p, h in chains}
            for p in range(g):
                dq = dq_acc[:, lanes[p]]
                for b, st in enumerate(starts):
                    dq = dq + jnp.where(first, dqs[b, p, 0], dqs[b, p, 1])
                    dk_acc[pl.ds(st, t), lanes[p]] += jnp.where(first, dks[b, p, 0], dks[b, p, 1])
                    dv_acc[pl.ds(st, t), lanes[p]] += jnp.where(first, dvs[b, p, 0], dvs[b, p, 1])
                dq_acc[:, lanes[p]] = dq
            for p, h in heads:
                l_ref[2 * p + h] = carry[p, h]
                e_ref[2 * p + h] = e_carry[p, h]

        _sb_walk(i, blocks, l_ref, causal)
        dq_ref[...] = (dq_acc[...] * SB_SCALE).astype(dq_ref.dtype)

        @pl.when(i == nq - 1)
        def _():
            dk_ref[...] = dk_acc[...].astype(dk_ref.dtype)
            dv_ref[...] = dv_acc[...].astype(dv_ref.dtype)

        if comm is not None:
            pl.when(last_step)(lambda: comm.finish(*riding))

    once = pl.Buffered(1)
    q_spec, k_spec, v_spec = _sb_qkv_specs(s, g)
    k_spec = pl.BlockSpec(k_spec.block_shape, k_spec.index_map, pipeline_mode=once)
    v_spec = pl.BlockSpec(v_spec.block_shape, v_spec.index_map, pipeline_mode=once)
    blk = pl.BlockSpec((t, g * LANES), lambda p, i: (i, p))
    col_blk = pl.BlockSpec((s, g * LANES), lambda p, i: (0, p), pipeline_mode=once)
    sds = jax.ShapeDtypeStruct((s, SB_WIDTH), BF16)
    return pl.pallas_call(
        body, name="sb_bwd", grid=grid,
        in_specs=[q_spec, k_spec, v_spec, blk, blk] + c_in_specs,
        out_specs=[blk, col_blk, col_blk] + c_out_specs,
        out_shape=[sds, sds, sds] + c_out_shape,
        scratch_shapes=c_scratch + [pltpu.VMEM((2 * g, t, 1), F32), pltpu.VMEM((2 * g, t, 1), F32),
                                    pltpu.VMEM((t, g * LANES), F32), pltpu.VMEM((s, g * LANES), F32),
                                    pltpu.VMEM((s, g * LANES), F32)],
        compiler_params=_cparams(),
    )(qkv, qkv, qkv, o, do, *c_ins)


def _ret_log_gamma():
    lg = np.log1p(-np.exp2(-5.0 - np.arange(RET_HEADS, dtype=np.float32))).astype(np.float32)
    return jnp.asarray(np.broadcast_to(lg[:, None, None], (RET_HEADS, 8, LANES)).copy())


RET_SCRATCH = [pltpu.VMEM((RET_HEADS, RET_QK, RET_V), F32),
               pltpu.VMEM((RET_HEADS, RET_BLOCK, RET_BLOCK), F32),
               pltpu.VMEM((RET_HEADS, RET_BLOCK, 1), F32),
               pltpu.VMEM((RET_HEADS, RET_BLOCK, 1), F32)]


def _ret_begin(n, lg_ref, state, within, q_dec, k_dec):
    @pl.when(n == 0)
    def _():
        c = RET_BLOCK
        state[...] = jnp.zeros_like(state)
        row = lax.broadcasted_iota(jnp.int32, (c, c), 0)
        col = lax.broadcasted_iota(jnp.int32, (c, c), 1)
        rel = jnp.maximum(row - col, 0).astype(F32)
        idx = lax.broadcasted_iota(jnp.int32, (c, 1), 0).astype(F32)
        for h in range(RET_HEADS):
            lg = lg_ref[h, 0:1, 0:1]
            within[h] = jnp.where(row >= col, jnp.exp(lg * rel), 0.0)
            q_dec[h] = jnp.exp(lg * (idx + 1.0))
            k_dec[h] = jnp.exp(lg * (c - 1.0 - idx))


def _chunk_decay(lg_ref, h):
    return jnp.exp(lg_ref[h, 0:1, 0:1] * float(RET_BLOCK))


def _ret_heads(x, width):
    return [x[:, h * width:(h + 1) * width] for h in range(RET_HEADS)]


def _ret_specs(s, reverse=False):
    c = RET_BLOCK
    per_step = min(RET_CHUNKS_PER_STEP, s // c)
    rows = c * per_step
    nc = s // rows
    pos = (lambda n: nc - 1 - n) if reverse else (lambda n: n)
    chunks = [slice(u * c, (u + 1) * c) for u in range(per_step)]
    q_spec = pl.BlockSpec((rows, RET_QK_WIDTH), lambda n: (pos(n), 0))
    k_spec = pl.BlockSpec((rows, RET_QK_WIDTH), lambda n: (pos(n), 1))
    v_spec = pl.BlockSpec((rows, RET_V_WIDTH), lambda n: (pos(n), 0))
    lg_spec = pl.BlockSpec((RET_HEADS, 8, LANES), lambda n: (0, 0, 0))
    rope_spec = pl.BlockSpec((rows, RET_QK), lambda n: (pos(n), 0))
    return nc, chunks[::-1] if reverse else chunks, q_spec, k_spec, v_spec, lg_spec, rope_spec


def _ret_fwd(rqk, rvg, s):
    nc, chunks, q_spec, k_spec, v_spec, lg_spec, _ = _ret_specs(s)
    g_spec = pl.BlockSpec(v_spec.block_shape, lambda n: (n, 1))
    heads = range(RET_HEADS)

    def body(q_ref, k_ref, v_ref, g_ref, lg_ref, r_ref, y_ref, state, within, q_dec, k_dec):
        n = pl.program_id(0)
        _ret_begin(n, lg_ref, state, within, q_dec, k_dec)
        for rows in chunks:
            q, k = _ret_heads(q_ref[rows], RET_QK), _ret_heads(k_ref[rows], RET_QK)
            v, g = _ret_heads(v_ref[rows], RET_V), _ret_heads(g_ref[rows], RET_V)
            scores = [_dot(q[h].astype(BF16), k[h].astype(BF16), 1, 1) * within[h] for h in heads]
            cross = [_dot((q[h] * q_dec[h]).astype(BF16), state[h].astype(BF16), 1, 0) for h in heads]
            out = [_dot(scores[h].astype(BF16), v[h], 1, 0) + cross[h] for h in heads]
            grown = [_dot((k[h] * k_dec[h]).astype(BF16), v[h], 0, 0) for h in heads]
            for h in heads:
                sl = slice(h * RET_V, (h + 1) * RET_V)
                r_ref[rows, sl] = out[h]
                xhat, _ = _norm(out[h])
                gh = g[h].astype(F32)
                y_ref[rows, sl] = (gh * _sigmoid(gh) * xhat).astype(y_ref.dtype)
                state[h] = state[h] * _chunk_decay(lg_ref, h) + grown[h]

    return pl.pallas_call(
        body, name="ret_fwd", grid=(nc,),
        in_specs=[q_spec, k_spec, v_spec, g_spec, lg_spec],
        out_specs=[v_spec, v_spec],
        out_shape=[jax.ShapeDtypeStruct((s, RET_V_WIDTH), F32), jax.ShapeDtypeStruct((s, RET_V_WIDTH), BF16)],
        scratch_shapes=RET_SCRATCH,
        compiler_params=_cparams(),
    )(rqk, rqk, rvg, rvg, _ret_log_gamma())


def _rope_bwd(d, cos, sin):
    return d * cos + _swap_halves(d * sin)


def _ret_bwd_q(rqk, rv, d_out, cos2, sin2, s):
    nc, chunks, q_spec, k_spec, v_spec, lg_spec, rope_spec = _ret_specs(s)
    heads = range(RET_HEADS)

    def body(k_ref, v_ref, d_ref, lg_ref, cos_ref, sin_ref, dq_ref, state, within, q_dec, k_dec):
        n = pl.program_id(0)
        _ret_begin(n, lg_ref, state, within, q_dec, k_dec)
        for rows in chunks:
            k = _ret_heads(k_ref[rows], RET_QK)
            v, d = _ret_heads(v_ref[rows], RET_V), _ret_heads(d_ref[rows], RET_V)
            cos, sin = cos_ref[rows], sin_ref[rows]
            d_scores = [_dot(d[h], v[h], 1, 1) * within[h] for h in heads]
            cross = [q_dec[h] * _dot(d[h], state[h].astype(BF16), 1, 1) for h in heads]
            dq = [_dot(d_scores[h].astype(BF16), k[h].astype(BF16), 1, 0) + cross[h] for h in heads]
            grown = [_dot((k[h] * k_dec[h]).astype(BF16), v[h], 0, 0) for h in heads]
            for h in heads:
                sl = slice(h * RET_QK, (h + 1) * RET_QK)
                dq_ref[rows, sl] = (_rope_bwd(dq[h], cos, sin) * RET_SCALE).astype(dq_ref.dtype)
                state[h] = state[h] * _chunk_decay(lg_ref, h) + grown[h]

    return pl.pallas_call(
        body, name="ret_bwd_q", grid=(nc,),
        in_specs=[k_spec, v_spec, v_spec, lg_spec, rope_spec, rope_spec],
        out_specs=q_spec,
        out_shape=jax.ShapeDtypeStruct((s, RET_QK_WIDTH), BF16),
        scratch_shapes=RET_SCRATCH,
        compiler_params=_cparams(),
    )(rqk, rv, d_out, _ret_log_gamma(), cos2, sin2)


def _ret_bwd_kv(rqk, rv, d_out, cos2, sin2, s):
    nc, chunks, q_spec, k_spec, v_spec, lg_spec, rope_spec = _ret_specs(s, reverse=True)
    heads = range(RET_HEADS)

    def body(q_ref, k_ref, v_ref, d_ref, lg_ref, cos_ref, sin_ref, dk_ref, dv_ref, state, within, q_dec, k_dec):
        n = pl.program_id(0)
        _ret_begin(n, lg_ref, state, within, q_dec, k_dec)
        for rows in chunks:
            q, k = _ret_heads(q_ref[rows], RET_QK), _ret_heads(k_ref[rows], RET_QK)
            v, d = _ret_heads(v_ref[rows], RET_V), _ret_heads(d_ref[rows], RET_V)
            cos, sin = cos_ref[rows], sin_ref[rows]
            qb, kb = [q[h].astype(BF16) for h in heads], [k[h].astype(BF16) for h in heads]
            st = [state[h].astype(BF16) for h in heads]
            scores = [_dot(qb[h], kb[h], 1, 1) * within[h] for h in heads]
            d_scores = [_dot(d[h], v[h], 1, 1) * within[h] for h in heads]
            dk = [_dot(d_scores[h].astype(BF16), qb[h], 0, 0) + k_dec[h] * _dot(v[h], st[h], 1, 1) for h in heads]
            dv = [_dot(scores[h].astype(BF16), d[h], 0, 0) + k_dec[h] * _dot(kb[h], st[h], 1, 0) for h in heads]
            grown = [_dot((q[h] * q_dec[h]).astype(BF16), d[h], 0, 0) for h in heads]
            for h in heads:
                dk_ref[rows, h * RET_QK:(h + 1) * RET_QK] = _rope_bwd(dk[h], cos, sin).astype(dk_ref.dtype)
                dv_ref[rows, h * RET_V:(h + 1) * RET_V] = dv[h].astype(dv_ref.dtype)
                state[h] = state[h] * _chunk_decay(lg_ref, h) + grown[h]

    return pl.pallas_call(
        body, name="ret_bwd_kv", grid=(nc,),
        in_specs=[q_spec, k_spec, v_spec, v_spec, lg_spec, rope_spec, rope_spec],
        out_specs=[q_spec, v_spec],
        out_shape=[jax.ShapeDtypeStruct((s, RET_QK_WIDTH), BF16), jax.ShapeDtypeStruct((s, RET_V_WIDTH), BF16)],
        scratch_shapes=RET_SCRATCH,
        compiler_params=_cparams(),
    )(rqk, rqk, rv, d_out, _ret_log_gamma(), cos2, sin2)


def _xattn_probs(scores):
    sc = scores - jnp.max(scores, axis=-1, keepdims=True)
    p = jnp.exp(sc)
    return p / jnp.sum(p, axis=-1, keepdims=True)


def _xattn_heads(q_ref, kv_ref):
    sls = [slice(h * MEM_DIM, (h + 1) * MEM_DIM) for h in range(MEM_HEADS)]
    q = [q_ref[:, sl] for sl in sls]
    k = [kv_ref[:, sl] for sl in sls]
    v = [kv_ref[:, D_MODEL + h * MEM_DIM:D_MODEL + (h + 1) * MEM_DIM] for h in range(MEM_HEADS)]
    return sls, q, k, v


def _xattn_fwd(qm, kv, s):
    tq = min(XATTN_ROWS, s)
    heads = range(MEM_HEADS)

    def body(q_ref, kv_ref, o_ref):
        sls, q, k, v = _xattn_heads(q_ref, kv_ref)
        scores = [_dot(q[h], k[h], 1, 1) for h in heads]
        p = [_xattn_probs(scores[h]).astype(BF16) for h in heads]
        out = [_dot(p[h], v[h], 1, 0) for h in heads]
        for h in heads:
            o_ref[:, sls[h]] = out[h].astype(o_ref.dtype)

    return pl.pallas_call(
        body, name="xattn_fwd", grid=(s // tq,),
        in_specs=[pl.BlockSpec((tq, D_MODEL), lambda i: (i, 0)),
                  pl.BlockSpec((MEM_LEN, 2 * D_MODEL), lambda i: (0, 0))],
        out_specs=pl.BlockSpec((tq, D_MODEL), lambda i: (i, 0)),
        out_shape=jax.ShapeDtypeStruct((s, D_MODEL), BF16),
        compiler_params=_cparams(),
    )(qm, kv)


def _xattn_bwd(qm, kv, do, s):
    tq = min(XATTN_ROWS, s)

    def body(q_ref, kv_ref, do_ref, dq_ref, dkv_ref):
        i = pl.program_id(0)

        @pl.when(i == 0)
        def _():
            dkv_ref[...] = jnp.zeros_like(dkv_ref)

        heads = range(MEM_HEADS)
        sls, q, k, v = _xattn_heads(q_ref, kv_ref)
        d = [do_ref[:, sl] for sl in sls]
        scores = [_dot(q[h], k[h], 1, 1) for h in heads]
        dp = [_dot(d[h], v[h], 1, 1) for h in heads]
        p = [_xattn_probs(scores[h]) for h in heads]
        ds = [(p[h] * (dp[h] - jnp.sum(p[h] * dp[h], axis=-1, keepdims=True))).astype(BF16) for h in heads]
        dq = [_dot(ds[h], k[h], 1, 0) for h in heads]
        dk = [_dot(ds[h], q[h], 0, 0) for h in heads]
        dv = [_dot(p[h].astype(BF16), d[h], 0, 0) for h in heads]
        for h in heads:
            dq_ref[:, sls[h]] = (dq[h] * MEM_SCALE).astype(dq_ref.dtype)
            dkv_ref[:, sls[h]] += dk[h]
            dkv_ref[:, D_MODEL + h * MEM_DIM:D_MODEL + (h + 1) * MEM_DIM] += dv[h]

    row_blk = pl.BlockSpec((tq, D_MODEL), lambda i: (i, 0))
    kv_blk = pl.BlockSpec((MEM_LEN, 2 * D_MODEL), lambda i: (0, 0))
    return pl.pallas_call(
        body, name="xattn_bwd", grid=(s // tq,),
        in_specs=[row_blk, kv_blk, row_blk],
        out_specs=[row_blk, kv_blk],
        out_shape=[jax.ShapeDtypeStruct((s, D_MODEL), BF16), jax.ShapeDtypeStruct((MEM_LEN, 2 * D_MODEL), F32)],
        compiler_params=_cparams(),
    )(qm, kv, do)


def _place():
    x, y, c = lax.axis_index("x"), lax.axis_index("y"), lax.axis_index("c")
    others = [(1 - x, y), (x, 1 - y), (1 - x, 1 - y)]
    return x, y, c, others


def _slab(ref, axis, chip, size):
    start = pl.multiple_of(chip * size, LANES if axis == 1 else 16)
    if axis == 0:
        return ref.at[pl.ds(start, size), :]
    return ref.at[:, pl.ds(start, size)]


class _CommPlan:
    def __init__(self, ins, out_shape, scratch, start, finish):
        self.ins, self.out_shape, self.scratch, self.start, self.finish = ins, out_shape, scratch, start, finish

    @property
    def specs(self):
        any_spec = pl.BlockSpec(memory_space=pl.ANY)
        return [any_spec] * len(self.ins), [any_spec] * len(self.out_shape)

    def beside(self, other):
        cut = len(self.ins), len(self.out_shape), len(self.scratch)

        def parts(ins, outs, sems):
            return ((ins[:cut[0]], outs[:cut[1]], sems[:cut[2]]), (ins[cut[0]:], outs[cut[1]:], sems[cut[2]:]))

        def start(*refs):
            mine, theirs = parts(*refs)
            self.start(*mine)
            other.start(*theirs)

        def finish(*refs):
            mine, theirs = parts(*refs)
            self.finish(*mine)
            other.finish(*theirs)

        return _CommPlan(list(self.ins) + list(other.ins), list(self.out_shape) + list(other.out_shape),
                         list(self.scratch) + list(other.scratch), start, finish)


def _column_copies(pieces, offsets, dst, sems):
    return [pltpu.make_async_copy(src, dst.at[:, pl.ds(off, src.shape[1])], sems.at[n])
            for n, (src, off) in enumerate(zip(pieces, offsets, strict=True))]


def _assemble_plan(pieces, offsets, width):
    def start(ins, outs, sems):
        for cp in _column_copies(ins, offsets, outs[0], sems[0]):
            cp.start()

    def finish(ins, outs, sems):
        for cp in _column_copies(ins, offsets, outs[0], sems[0]):
            cp.wait()

    return _CommPlan(list(pieces), [jax.ShapeDtypeStruct((pieces[0].shape[0], width), BF16)],
                     [pltpu.SemaphoreType.DMA((len(pieces),))], start, finish)


def _fill_columns(buf, pieces, offsets):
    n = len(pieces)
    any_spec = pl.BlockSpec(memory_space=pl.ANY)

    def body(*refs):
        srcs, dst, sems = refs[1:1 + n], refs[1 + n], refs[2 + n]
        copies = _column_copies(srcs, offsets, dst, sems)
        for cp in copies:
            cp.start()
        for cp in copies:
            cp.wait()

    return pl.pallas_call(
        body, name="fill_columns", in_specs=[any_spec] * (1 + n), out_specs=any_spec,
        out_shape=jax.ShapeDtypeStruct(buf.shape, buf.dtype), input_output_aliases={0: 0},
        scratch_shapes=[pltpu.SemaphoreType.DMA((n,))],
    )(buf, *pieces)


def _gather_plan(names, shards):
    spec = {name: (shape, axis) for name, shape, axis in BIG}
    nw = len(names)

    def shard_half(ref, c):
        rows = ref.shape[0] // 2
        return ref.at[pl.ds(pl.multiple_of(c * rows, 16), rows), :]

    def region(ref, w, chip, c):
        shape, axis = spec[names[w]]
        size = shape[axis] // N_CHIPS
        if axis == 0:
            rows = size // 2
            return ref.at[pl.ds(pl.multiple_of(chip * size + c * rows, 16), rows), :]
        rows = shape[0] // 2
        return ref.at[pl.ds(pl.multiple_of(c * rows, 16), rows), pl.ds(pl.multiple_of(chip * size, LANES), size)]

    def ops(shard, full, sems):
        ici_send, ici_recv, d2d_send, d2d_recv, local_sems = sems
        x, y, c, others = _place()
        mine, sibling = 2 * x + y, (x, y, 1 - c)
        local, over_ici, arrived, passed_on, from_sibling = [], [], [], [], []
        for w in range(nw):
            shape, axis = spec[names[w]]
            local.append(pltpu.make_async_copy(shard[w], _slab(full[w], axis, mine, shape[axis] // N_CHIPS),
                                               local_sems.at[w]))
            for t, (qx, qy) in enumerate(others):
                n, theirs = 3 * w + t, 2 * qx + qy
                over_ici.append(pltpu.make_async_remote_copy(
                    src_ref=shard_half(shard[w], c), dst_ref=region(full[w], w, mine, c),
                    send_sem=ici_send.at[n], recv_sem=ici_recv.at[n], device_id=(qx, qy, c), device_id_type=MESH))
                arrived.append(pltpu.make_async_remote_copy(
                    src_ref=shard_half(shard[w], c), dst_ref=region(full[w], w, theirs, c),
                    send_sem=ici_send.at[n], recv_sem=ici_recv.at[n], device_id=(qx, qy, c), device_id_type=MESH))
                passed_on.append(pltpu.make_async_remote_copy(
                    src_ref=region(full[w], w, theirs, c), dst_ref=region(full[w], w, theirs, c),
                    send_sem=d2d_send.at[n], recv_sem=d2d_recv.at[n], device_id=sibling, device_id_type=MESH))
                from_sibling.append(pltpu.make_async_remote_copy(
                    src_ref=region(full[w], w, theirs, c), dst_ref=region(full[w], w, theirs, 1 - c),
                    send_sem=d2d_send.at[n], recv_sem=d2d_recv.at[n], device_id=sibling, device_id_type=MESH))
        return local, over_ici, arrived, passed_on, from_sibling

    def start(shard, full, sems):
        local, over_ici, _, _, _ = ops(shard, full, sems)
        for cp in local + over_ici:
            cp.start()

    def finish(shard, full, sems):
        local, over_ici, arrived, passed_on, from_sibling = ops(shard, full, sems)
        for got, onward in zip(arrived, passed_on, strict=True):
            got.wait_recv()
            onward.start()
        for got in from_sibling:
            got.wait_recv()
        for cp in over_ici + passed_on:
            cp.wait_send()
        for cp in local:
            cp.wait()

    dma = pltpu.SemaphoreType.DMA
    return _CommPlan(
        ins=[shards[name] for name in names],
        out_shape=[jax.ShapeDtypeStruct(spec[name][0], BF16) for name in names],
        scratch=[dma((3 * nw,)), dma((3 * nw,)), dma((3 * nw,)), dma((3 * nw,)), dma((nw,))],
        start=start, finish=finish)


def _shard_shape(shape, axis):
    return tuple(d // N_CHIPS if a == axis else d for a, d in enumerate(shape))


def _exchange_plan(names, grads):
    spec = {name: (shape, axis) for name, shape, axis in BIG}
    nw = len(names)

    def ops(grad, stack, sems):
        send_sems, recv_sems, local_sems = sems
        x, y, c, others = _place()
        mine = 2 * x + y
        me, sibling = (x, y, c), (x, y, 1 - c)

        def dev(px, py, pc):
            return 4 * px + 2 * py + pc

        def copy(w, n, src, slot, to):
            return pltpu.make_async_remote_copy(
                src_ref=src, dst_ref=stack[w].at[slot], send_sem=send_sems.at[7 * w + n],
                recv_sem=recv_sems.at[7 * w + n], device_id=to, device_id_type=MESH)

        local, first, arrived, passed_on, from_sibling = [], [], [], [], []
        for w in range(nw):
            shape, axis = spec[names[w]]
            size = shape[axis] // N_CHIPS
            own = _slab(grad[w], axis, mine, size)
            local.append(pltpu.make_async_copy(own, stack[w].at[dev(*me)], local_sems.at[w]))
            first.append(copy(w, 0, own, dev(*me), sibling))
            from_sibling.append(copy(w, 0, own, dev(*sibling), me))
            for t, (qx, qy) in enumerate(others):
                got = stack[w].at[dev(qx, qy, c)]
                first.append(copy(w, 1 + t, _slab(grad[w], axis, 2 * qx + qy, size), dev(*me), (qx, qy, c)))
                arrived.append(copy(w, 1 + t, got, dev(qx, qy, c), me))
                passed_on.append(copy(w, 4 + t, got, dev(qx, qy, c), sibling))
                from_sibling.append(copy(w, 4 + t, got, dev(qx, qy, 1 - c), me))
        return local, first, arrived, passed_on, from_sibling

    def start(grad, stack, sems):
        local, first, _, _, _ = ops(grad, stack, sems)
        for cp in local + first:
            cp.start()

    def finish(grad, stack, sems):
        local, first, arrived, passed_on, from_sibling = ops(grad, stack, sems)
        for got, onward in zip(arrived, passed_on, strict=True):
            got.wait_recv()
            onward.start()
        for got in from_sibling:
            got.wait_recv()
        for cp in first + passed_on:
            cp.wait_send()
        for cp in local:
            cp.wait()

    dma = pltpu.SemaphoreType.DMA
    return _CommPlan(
        ins=[grads[name] for name in names],
        out_shape=[jax.ShapeDtypeStruct((N_DEV,) + _shard_shape(*spec[name]), BF16) for name in names],
        scratch=[dma((7 * nw,)), dma((7 * nw,)), dma((nw,))],
        start=start, finish=finish)


def _adamw(w, g, m, v):
    m = ADAM_B1 * m + (1.0 - ADAM_B1) * g
    v = ADAM_B2 * v + (1.0 - ADAM_B2) * (g * g)
    m_hat = m / (1.0 - ADAM_B1 ** ADAM_STEP)
    v_hat = v / (1.0 - ADAM_B2 ** ADAM_STEP)
    delta = -ADAM_LR * (m_hat / (jnp.sqrt(v_hat) + ADAM_EPS) + ADAM_WD * w)
    return delta, m, v


def _reduce_adamw(name, stack, w, m, v):
    rows, cols = w.shape
    tr = next(t for t in (256, 128, 64) if rows % t == 0)

    def body(s_ref, w_ref, m_ref, v_ref, g_ref, d_ref, nm_ref, nv_ref):
        g = s_ref[0].astype(F32)
        for d in range(1, N_DEV):
            g = g + s_ref[d].astype(F32)
        g_ref[...] = g
        d_ref[...], nm_ref[...], nv_ref[...] = _adamw(w_ref[...], g, m_ref[...], v_ref[...])

    blk = pl.BlockSpec((tr, cols), lambda i: (i, 0))
    return pl.pallas_call(
        body, name=name, grid=(rows // tr,),
        in_specs=[pl.BlockSpec((N_DEV, tr, cols), lambda i: (0, i, 0)), blk, blk, blk],
        out_specs=[blk] * 4, out_shape=[jax.ShapeDtypeStruct((rows, cols), F32)] * 4,
        compiler_params=_cparams(),
    )(stack, w, m, v)


def _small_step(pack, w, m, v):
    def body(p_ref, w_ref, m_ref, v_ref, g_ref, d_ref, nm_ref, nv_ref, loss_ref, all_ref, send_sems, recv_sems):
        x, y, c, _ = _place()
        me = 4 * x + 2 * y + c
        all_ref[me] = p_ref[...]
        sent = []
        for n in range(1, N_DEV):
            peer = me ^ n
            cp = pltpu.make_async_remote_copy(
                src_ref=p_ref, dst_ref=all_ref.at[me], send_sem=send_sems.at[n - 1], recv_sem=recv_sems.at[n - 1],
                device_id=(peer // 4, (peer // 2) % 2, peer % 2), device_id_type=MESH)
            cp.start()
            sent.append(cp)
        for n in range(1, N_DEV):
            peer = me ^ n
            pltpu.make_async_remote_copy(
                src_ref=p_ref, dst_ref=all_ref.at[peer], send_sem=send_sems.at[n - 1], recv_sem=recv_sems.at[n - 1],
                device_id=(peer // 4, (peer // 2) % 2, peer % 2), device_id_type=MESH).wait_recv()
        for cp in sent:
            cp.wait_send()
        tot = all_ref[0]
        for d in range(1, N_DEV):
            tot = tot + all_ref[d]
        g = tot[:SMALL_ROWS]
        g_ref[...] = g
        d_ref[...], nm_ref[...], nv_ref[...] = _adamw(w_ref[...], g, m_ref[...], v_ref[...])
        loss_ref[...] = jnp.sum(jnp.sum(tot[SMALL_ROWS:], axis=1, keepdims=True), axis=0, keepdims=True)

    vm = pl.BlockSpec(memory_space=pltpu.VMEM)
    small = jax.ShapeDtypeStruct((SMALL_ROWS, LANES), F32)
    return pl.pallas_call(
        body, name="small_step",
        in_specs=[vm] * 4, out_specs=[vm] * 5,
        out_shape=[small] * 4 + [jax.ShapeDtypeStruct((1, 1), F32)],
        scratch_shapes=[pltpu.VMEM((N_DEV, PACK_ROWS, LANES), F32),
                        pltpu.SemaphoreType.DMA((N_DEV - 1,)), pltpu.SemaphoreType.DMA((N_DEV - 1,))],
    )(pack, w, m, v)


LATER_WEIGHTS = tuple(name for name, _, _ in BIG if name != "w_in")


def _layer_step(x, mem, tgt, shards, vec):
    s = x.shape[0]
    d = D_MODEL
    tm = min(ROW_TILE, s)
    tl = min(WIDE_TILE, s)
    xb, cos2, sin2, w_in = _prep(x, _gather_plan(("w_in",), shards))
    bf = lambda w: ((s, w), BF16)
    f32 = lambda w: ((s, w), F32)

    w_sb, w_rqk = w_in[:, :OFF_RET_Q], w_in[:, OFF_RET_Q:OFF_RET_V]
    w_rvg, w_gate = w_in[:, OFF_RET_V:OFF_GATE], w_in[:, OFF_GATE:]
    q_scale = lambda width, q_width, scale: jnp.concatenate(
        [jnp.full((1, q_width), scale, F32), jnp.ones((1, width - q_width), F32)], axis=1)
    n_groups = 3 * SB_WIDTH // LANES

    def sb_epi(acc, t, i, j):
        scaled = acc * t[0]
        return [jnp.stack([scaled[:, g * LANES:(g + 1) * LANES] for g in range(n_groups)])], []

    (sb_qkv,) = _mm(
        "in_sb", xb, w_sb, s, 3 * SB_WIDTH, d, tm=tl, tn=3 * SB_WIDTH, tk=d, epi=sb_epi,
        ins=[(q_scale(3 * SB_WIDTH, SB_WIDTH, SB_SCALE), *_rowvec(3 * SB_WIDTH))],
        outs=[((n_groups, s, LANES), BF16, (n_groups, tl, LANES), lambda i, j: (0, i, 0))])

    def rope_epi(acc, t, i, j):
        cos, sin, scale = t
        parts = []
        for g in range(acc.shape[1] // RET_QK):
            xg = acc[:, g * RET_QK:(g + 1) * RET_QK]
            parts.append(xg * cos + _swap_halves(xg) * sin)
        return [jnp.concatenate(parts, axis=1) * scale], []

    rope_in = ((tl, RET_QK), lambda i, j: (i, 0))
    (rqk,) = _mm("in_rqk", xb, w_rqk, s, 2 * RET_QK_WIDTH, d, tm=tl, tn=2 * RET_QK_WIDTH, tk=d, epi=rope_epi,
                 chunk=MXU_COLS,
                 ins=[(cos2, *rope_in), (sin2, *rope_in),
                      (q_scale(2 * RET_QK_WIDTH, RET_QK_WIDTH, RET_SCALE), *_rowvec(2 * RET_QK_WIDTH))],
                 outs=[(*f32(2 * RET_QK_WIDTH), *_tile(tl, 2 * RET_QK_WIDTH))])
    (rvg,) = _mm("in_rvg", xb, w_rvg, s, 2 * RET_V_WIDTH, d, tm=tl, tn=2 * RET_V_WIDTH, tk=d, chunk=MXU_COLS,
                 epi=_plain, outs=[(*bf(2 * RET_V_WIDTH), *_tile(tl, 2 * RET_V_WIDTH))])
    (gates,) = _mm("in_gate", xb, w_gate, s, 2 * d, d, tm=tl, tn=2 * d, tk=d, chunk=MXU_COLS,
                   epi=lambda acc, t, i, j: ([_sigmoid(acc + t[0])], []),
                   ins=[(vec["b_gate"], *_rowvec(2 * d))], outs=[(*bf(2 * d), *_tile(tl, 2 * d))])

    sb_out, sb_out_f32, *gathered = _sb_fwd(sb_qkv, s, comm=_gather_plan(LATER_WEIGHTS, shards))
    wt = dict(zip(LATER_WEIGHTS, gathered, strict=True))
    ret, gated = _ret_fwd(rqk, rvg, s)
    (y_sb,) = _mm("sb_o", sb_out, wt["w_sb_o"], s, d, SB_WIDTH, tm=tl, tn=d, tk=SB_WIDTH, epi=_plain,
                  outs=[(*bf(d), *_tile(tl, d))])
    y_ret, mixin = _mm(
        "ret_o", gated, wt["w_ret_o"], s, d, RET_V_WIDTH, tm=tl, tn=d, tk=RET_V_WIDTH, chunk=MXU_COLS,
        epi=lambda acc, t, i, j: ([acc, t[0].astype(F32) * t[2].astype(F32) + t[1].astype(F32) * acc], []),
        ins=[(gates, *_tile(tl, d)), (gates, *_tile(tl, d, 1)), (y_sb, *_tile(tl, d))],
        outs=[(*bf(d), *_tile(tl, d)), (*bf(d), *_tile(tl, d))])

    def ln_epi(acc, t, i, j):
        *res, g, b = t
        prev = res[0] if len(res) == 1 else res[0] * res[1] + res[2]
        xhat, rstd = _norm(DN_ALPHA * prev + acc)
        return [xhat * g + b, xhat, rstd], []

    full = _tile(tm, d)
    col1 = ((tm, 1), lambda i, j: (i, 0))
    vec_in = lambda name: (vec[name], *_rowvec(d))
    ln_outs = [(*bf(d), *full), (*f32(d), *full), ((s, 1), F32, *col1)]
    x1b, xhat1, rstd1 = _mm(
        "mix_o", mixin, wt["w_mix_o"], s, d, d, tm=tm, tn=d, tk=d, epi=ln_epi,
        ins=[(x, *full), vec_in("ln1_g"), vec_in("ln1_b")], outs=ln_outs)

    (qm,) = _mm("mem_q", x1b, wt["w_mem_q"], s, d, d, tm=tl, tn=d, tk=d,
                epi=lambda acc, t, i, j: ([acc * MEM_SCALE], []), outs=[(*bf(d), *_tile(tl, d))])
    (kv,) = _mm("mem_kv", mem, wt["w_mem_kv"], MEM_LEN, 2 * d, d, tm=MEM_LEN, tn=d, tk=d, epi=_plain,
                outs=[((MEM_LEN, 2 * d), BF16, *_tile(MEM_LEN, d))])
    att = _xattn_fwd(qm, kv, s)
    x2b, xhat2, rstd2 = _mm(
        "mem_o", att, wt["w_mem_o"], s, d, d, tm=tm, tn=d, tk=d, epi=ln_epi,
        ins=[(xhat1, *full), vec_in("ln1_g"), vec_in("ln1_b"), vec_in("ln2_g"), vec_in("ln2_b")], outs=ln_outs)

    fh = FFN_HIDDEN
    tf = fh // 2
    (f1,) = _mm("ffn_in1", x2b, wt["w_ffn_in"], s, fh, d, tm=tl, tn=tf, tk=d, epi=_plain, j_outer=True,
                outs=[(*bf(fh), *_tile(tl, tf))])

    def swiglu_epi(acc, t, i, j):
        a = t[0].astype(F32)
        return [acc, a * _sigmoid(a) * acc], []

    f2, act = _mm(
        "ffn_in2", x2b, wt["w_ffn_in"], s, fh, d, tm=tm, tn=fh, tk=d, b_off=(0, 1), epi=swiglu_epi, chunk=MXU_COLS,
        ins=[(f1, *_tile(tm, fh))], outs=[(*bf(fh), *_tile(tm, fh)), (*bf(fh), *_tile(tm, fh))])

    def head_epi(acc, t, i, j):
        prev_hat, prev_g, prev_b, g, b, target = t
        xhat, rstd = _norm(DN_ALPHA * (prev_hat * prev_g + prev_b) + acc)
        err = xhat * g + b - target
        dy = err * (1.0 / d)
        du = _norm_bwd(dy * g, xhat, rstd)
        return [du], [_colsum(dy * xhat), _colsum(dy), _colsum(err * err) * (0.5 / d)]

    vec_acc = ((1, d), F32)
    du3b, dg3, db3, loss_cols = _mm(
        "ffn_out", act, wt["w_ffn_out"], s, d, fh, tm=tm, tn=d, tk=fh, epi=head_epi,
        ins=[(xhat2, *full), vec_in("ln2_g"), vec_in("ln2_b"), vec_in("ln3_g"), vec_in("ln3_b"), (tgt, *full)],
        outs=[(*bf(d), *full)], accs=[vec_acc] * 3)

    grads = {}
    ts = min(SEQ_TILE, s)

    def wgrad(name, a, b, m, n, tm_, tn_, tk_=None):
        (g,) = _mm(name, a, b, m, n, a.shape[0], tm=tm_, tn=tn_, tk=tk_ or ts, ta=True, epi=_plain,
                   outs=[((m, n), BF16, *_tile(tm_, tn_))])
        return g

    def ffn_bwd_epi(acc, t, i, j):
        a, b = t[0].astype(F32), t[1].astype(F32)
        sg = _sigmoid(a)
        return [acc * b * (sg * (1.0 + a * (1.0 - sg))), acc * (a * sg)], []

    df1, df2 = _mm(
        "ffn_out_t", du3b, wt["w_ffn_out"], s, fh, d, tm=tm, tn=fh, tk=d, tb=True, epi=ffn_bwd_epi, chunk=MXU_COLS,
        ins=[(f1, *_tile(tm, fh)), (f2, *_tile(tm, fh))],
        outs=[(*bf(fh), *_tile(tm, fh)), (*bf(fh), *_tile(tm, fh))])
    grads["w_ffn_out"] = wgrad("g_ffn_out", act, du3b, fh, d, tf, d)
    grads["w_ffn_in"] = jnp.concatenate(
        [wgrad("g_ffn_in1", x2b, df1, d, fh, d, tf), wgrad("g_ffn_in2", x2b, df2, d, fh, d, tf)], axis=1)
    (dx2a,) = _mm("ffn_in1_t", df1, wt["w_ffn_in"], s, d, fh, tm=tl, tn=d, tk=fh, tb=True, epi=_plain,
                  outs=[(*f32(d), *_tile(tl, d))])

    def ln_bwd(name, a, b, k, tk, b_off, more, scales, xhat, rstd, g):
        def epi(acc, t, i, j):
            *extra, xh, rs, gg = t
            dy = acc
            for e, sc in zip(extra, scales, strict=True):
                dy = dy + e.astype(F32) * sc
            return [_norm_bwd(dy * gg, xh, rs)], [_colsum(dy * xh), _colsum(dy)]

        return _mm(name, a, b, s, d, k, tm=tm, tn=d, tk=tk, tb=True, b_off=b_off, epi=epi,
                   ins=[(e, *full) for e in more] + [(xhat, *full), (rstd, *col1), (g, *_rowvec(d))],
                   outs=[(*bf(d), *full)], accs=[vec_acc] * 2)

    du2b, dg2, db2 = ln_bwd("ffn_in2_t", df2, wt["w_ffn_in"], fh, fh, (0, 1), [dx2a, du3b], [1.0, DN_ALPHA],
                            xhat2, rstd2, vec["ln2_g"])

    (datt,) = _mm("mem_o_t", du2b, wt["w_mem_o"], s, d, d, tm=tl, tn=d, tk=d, tb=True, epi=_plain,
                  outs=[(*bf(d), *_tile(tl, d))])
    grads["w_mem_o"] = wgrad("g_mem_o", att, du2b, d, d, d, d)
    dqm, dkv = _xattn_bwd(qm, kv, datt, s)
    grads["w_mem_q"] = wgrad("g_mem_q", x1b, dqm, d, d, d, d)
    grads["w_mem_kv"] = wgrad("g_mem_kv", mem, dkv, d, 2 * d, d, d, MEM_LEN)
    du1b, dg1, db1 = ln_bwd("mem_q_t", dqm, wt["w_mem_q"], d, d, (0, 0), [du2b], [DN_ALPHA],
                            xhat1, rstd1, vec["ln1_g"])

    def merge_bwd_epi(acc, t, i, j):
        g0, g1, ysb, yret = (v.astype(F32) for v in t)
        dgate0 = acc * ysb * (g0 * (1.0 - g0))
        dgate1 = acc * yret * (g1 * (1.0 - g1))
        return [dgate0, dgate1, acc * g0, acc * g1], [_colsum(dgate0), _colsum(dgate1)]

    dgate0, dgate1, dy_sb, dy_ret, dbg0, dbg1 = _mm(
        "mix_o_t", du1b, wt["w_mix_o"], s, d, d, tm=tm, tn=d, tk=d, tb=True, epi=merge_bwd_epi,
        ins=[(gates, *full), (gates, *_tile(tm, d, 1)), (y_sb, *full), (y_ret, *full)],
        outs=[(*bf(d), *full)] * 4, accs=[vec_acc] * 2)
    grads["w_mix_o"] = wgrad("g_mix_o", mixin, du1b, d, d, d, d)
    grads["w_sb_o"] = wgrad("g_sb_o", sb_out, dy_sb, SB_WIDTH, d, SB_WIDTH, d)
    grads["w_ret_o"] = wgrad("g_ret_o", gated, dy_ret, RET_V_WIDTH, d, RET_V_WIDTH, d)
    (dsb_out,) = _mm("sb_o_t", dy_sb, wt["w_sb_o"], s, SB_WIDTH, d, tm=tl, tn=SB_WIDTH, tk=d, tb=True, epi=_plain,
                     outs=[(*bf(SB_WIDTH), *_tile(tl, SB_WIDTH))])

    def gate_norm_bwd_epi(acc, t, i, j):
        r, g = t[0], t[1].astype(F32)
        drg, dret = [], []
        for h in range(acc.shape[1] // RET_V):
            sl = slice(h * RET_V, (h + 1) * RET_V)
            xhat, rstd = _norm(r[:, sl])
            gg, dd = g[:, sl], acc[:, sl]
            sg = _sigmoid(gg)
            drg.append(dd * xhat * (sg * (1.0 + gg * (1.0 - sg))))
            dret.append(_norm_bwd(dd * (gg * sg), xhat, rstd))
        return [jnp.concatenate(drg, axis=1), jnp.concatenate(dret, axis=1)], []

    drg, dret = _mm(
        "ret_o_t", dy_ret, wt["w_ret_o"], s, RET_V_WIDTH, d, tm=tm, tn=d, tk=d, tb=True, epi=gate_norm_bwd_epi,
        chunk=MXU_COLS,
        ins=[(ret, *full), (rvg, *_tile(tm, d, 1))],
        outs=[(*bf(RET_V_WIDTH), *full)] * 2)

    drq = _ret_bwd_q(rqk, rvg, dret, cos2, sin2, s)
    drk, drv = _ret_bwd_kv(rqk, rvg, dret, cos2, sin2, s)
    ready = [drq, drk, drv, drg, dgate0, dgate1]
    offsets = [OFF_RET_Q, OFF_RET_Q + RET_QK_WIDTH, OFF_RET_V, OFF_RET_G, OFF_GATE, OFF_GATE + d]
    riding = _assemble_plan(ready, offsets, IN_WIDTH).beside(_exchange_plan(LATER_WEIGHTS, grads))
    dsq, dsk, dsv, dh, *stacked = _sb_bwd(sb_qkv, sb_out_f32, dsb_out, s, comm=riding)
    stacks = dict(zip(LATER_WEIGHTS, stacked, strict=True))
    dh = _fill_columns(dh, [dsq, dsk, dsv], [0, SB_WIDTH, 2 * SB_WIDTH])

    grads["w_in"] = wgrad("g_in", xb, dh, d, IN_WIDTH, d, IN_WIDTH // N_CHIPS)
    grad_x, stacks["w_in"] = _mm(
        "in_t", dh, w_in, s, d, IN_WIDTH, tm=tl, tn=d, tk=IN_WIDTH // N_CHIPS, tb=True,
        epi=lambda acc, t, i, j: ([acc + DN_ALPHA * t[0].astype(F32)], []),
        ins=[(du1b, *_tile(tl, d))], outs=[(*f32(d), *_tile(tl, d))], comm=_exchange_plan(("w_in",), grads))

    small = {"b_gate": jnp.concatenate([dbg0, dbg1], axis=1), "ln1_g": dg1, "ln1_b": db1, "ln2_g": dg2,
             "ln2_b": db2, "ln3_g": dg3, "ln3_b": db3}
    return grad_x, stacks, small, loss_cols


def kernel(x, mem, w_in, b_gate, w_sb_o, w_ret_o, w_mix_o, ln1_g, ln1_b, w_mem_q, w_mem_kv, w_mem_o, ln2_g, ln2_b, w_ffn_in, w_ffn_out, ln3_g, ln3_b, loss_target, m_w_in, m_b_gate, m_w_sb_o, m_w_ret_o, m_w_mix_o, m_ln1_g, m_ln1_b, m_w_mem_q, m_w_mem_kv, m_w_mem_o, m_ln2_g, m_ln2_b, m_w_ffn_in, m_w_ffn_out, m_ln3_g, m_ln3_b, v_w_in, v_b_gate, v_w_sb_o, v_w_ret_o, v_w_mix_o, v_ln1_g, v_ln1_b, v_w_mem_q, v_w_mem_kv, v_w_mem_o, v_ln2_g, v_ln2_b, v_w_ffn_in, v_w_ffn_out, v_ln3_g, v_ln3_b):
    given = dict(locals())
    s = x.shape[1]
    x2d = x.reshape(s, D_MODEL)
    tgt = loss_target.reshape(s, D_MODEL)
    mem2d = mem.reshape(MEM_LEN, D_MODEL)
    shard = {name: given[name].reshape(_shard_shape(shape, axis)) for name, shape, axis in BIG}
    vec = {name: given[name] for name in SMALL}

    shards_bf = {name: _cast_bf16("cast_" + name, shard[name]) for name, _, _ in BIG}

    grad_x, stacks, small, loss_cols = _layer_step(x2d, mem2d, tgt, shards_bf, vec)

    out = {}
    for name, shape, axis in BIG:
        stack = stacks[name]
        shp = given[name].shape
        res = _reduce_adamw("adamw_" + name, stack, shard[name], given["m_" + name].reshape(stack.shape[1:]),
                            given["v_" + name].reshape(stack.shape[1:]))
        out[name] = [r.reshape(shp) for r in res]

    pack = jnp.concatenate([small[name] for name in SMALL] + [loss_cols], axis=1).reshape(PACK_ROWS, LANES)
    cat = lambda pre: jnp.concatenate([given[pre + name] for name in SMALL], axis=1).reshape(SMALL_ROWS, LANES)
    *res, loss = _small_step(pack, cat(""), cat("m_"), cat("v_"))
    flat = [r.reshape(1, SMALL_LEN) for r in res]
    off = 0
    for name in SMALL:
        n = given[name].shape[1]
        out[name] = [r[:, off:off + n] for r in flat]
        off += n

    return (loss.reshape(()), grad_x.reshape(x.shape),
            *[out[name][0] for name in WEIGHT_ORDER], *[out[name][1] for name in WEIGHT_ORDER],
            *[out[name][2] for name in WEIGHT_ORDER], *[out[name][3] for name in WEIGHT_ORDER])
```

```python
import functools

import jax
import jax.numpy as jnp
import numpy as np
from jax import lax
from jax.experimental import pallas as pl
from jax.experimental.pallas import tpu as pltpu

F32, BF16 = jnp.float32, jnp.bfloat16
MESH = pl.DeviceIdType.MESH

D_MODEL = 1024
MEM_LEN = 256
SB_HEADS, SB_DIM, SB_WIDTH = 8, 64, 512
RET_HEADS, RET_QK, RET_V = 4, 128, 256
RET_QK_WIDTH, RET_V_WIDTH = 512, 1024
ROPE_BASE = 10000.0
MEM_HEADS, MEM_DIM = 4, 256
FFN_HIDDEN = 2816
IN_WIDTH = 6656
OFF_RET_Q, OFF_RET_V, OFF_RET_G, OFF_GATE = 1536, 2560, 3584, 4608
DN_ALPHA = 2.0 ** 0.25
LN_EPS = 1e-5
SB_SCALE = SB_DIM ** -0.5
SB_DEAD = -110.0
RET_SCALE = RET_QK ** -0.5
MEM_SCALE = MEM_DIM ** -0.5
ADAM_LR, ADAM_B1, ADAM_B2, ADAM_EPS, ADAM_WD, ADAM_STEP = 0.001, 0.9, 0.999, 1e-08, 0.01, 10

N_DEV, N_CHIPS = 8, 4

LANES = 128
MXU_COLS = 256
VMEM_LIMIT_BYTES = 52 * 2 ** 20
ROW_TILE = 512
WIDE_TILE = 1024
SEQ_TILE = 2048
SB_BLOCK = 256
RET_BLOCK = 256
RET_CHUNKS_PER_STEP = 4
XATTN_ROWS = 1024

BIG = (
    ("w_in", (D_MODEL, IN_WIDTH), 1),
    ("w_sb_o", (SB_WIDTH, D_MODEL), 1),
    ("w_ret_o", (RET_V_WIDTH, D_MODEL), 0),
    ("w_mix_o", (D_MODEL, D_MODEL), 0),
    ("w_mem_q", (D_MODEL, D_MODEL), 0),
    ("w_mem_kv", (D_MODEL, 2 * D_MODEL), 1),
    ("w_mem_o", (D_MODEL, D_MODEL), 0),
    ("w_ffn_in", (D_MODEL, 2 * FFN_HIDDEN), 1),
    ("w_ffn_out", (FFN_HIDDEN, D_MODEL), 0),
)
CHIP_MAJOR = ("w_in",)
SMALL = ("b_gate", "ln1_g", "ln1_b", "ln2_g", "ln2_b", "ln3_g", "ln3_b")
SMALL_LEN = 2 * D_MODEL + 6 * D_MODEL
SMALL_ROWS = SMALL_LEN // LANES
PACK_ROWS = SMALL_ROWS + D_MODEL // LANES
WEIGHT_ORDER = ("w_in", "b_gate", "w_sb_o", "w_ret_o", "w_mix_o", "ln1_g", "ln1_b", "w_mem_q", "w_mem_kv",
                "w_mem_o", "ln2_g", "ln2_b", "w_ffn_in", "w_ffn_out", "ln3_g", "ln3_b")


def _cparams():
    return pltpu.CompilerParams(vmem_limit_bytes=VMEM_LIMIT_BYTES)


def _dot(a, b, ca, cb):
    return lax.dot_general(a, b, (((ca,), (cb,)), ((), ())), preferred_element_type=F32)


def _sigmoid(x):
    return 1.0 / (1.0 + jnp.exp(-x))


def _mm(name, a, b, m, n, k, *, tm, tn, tk, epi, outs, ins=(), accs=(), ta=False, tb=False,
        a_off=(0, 0), b_off=(0, 0), j_outer=False, comm=None, chunk=None):
    assert m % tm == 0 and n % tn == 0 and k % tk == 0, (name, m, n, k, tm, tn, tk)
    assert chunk is None or (k == tk and tn % chunk == 0), name
    ni, nj, nk = m // tm, n // tn, k // tk
    assert not accs or nj == 1, name
    ij = (lambda g0, g1: (g1, g0)) if j_outer else (lambda g0, g1: (g0, g1))

    def spec(block, index):
        return pl.BlockSpec(block, lambda g0, g1, kk: index(*ij(g0, g1), kk))

    if ta:
        a_spec = spec((tk, tm), lambda i, j, kk: (kk + a_off[0], i + a_off[1]))
    else:
        a_spec = spec((tm, tk), lambda i, j, kk: (i + a_off[0], kk + a_off[1]))
    if tb and len(b.shape) == 3:
        assert b.shape[2] == tk and not any(b_off), name
        b_spec = spec((None, tn, tk), lambda i, j, kk: (kk, j, 0))
    elif tb:
        b_spec = spec((tn, tk), lambda i, j, kk: (j + b_off[0], kk + b_off[1]))
    else:
        b_spec = spec((tk, tn), lambda i, j, kk: (kk + b_off[0], j + b_off[1]))
    in_specs = [a_spec, b_spec]
    for _, bs, im in ins:
        in_specs.append(spec(bs, lambda i, j, kk, im=im: im(i, j)))
    out_specs, out_shape = [], []
    for shape, dtype, bs, im in outs:
        out_specs.append(spec(bs, lambda i, j, kk, im=im: im(i, j)))
        out_shape.append(jax.ShapeDtypeStruct(shape, dtype))
    for shape, dtype in accs:
        out_specs.append(spec(shape, lambda i, j, kk, nd=len(shape): (0,) * nd))
        out_shape.append(jax.ShapeDtypeStruct(shape, dtype))
    n_in, n_out, n_acc = len(ins), len(outs), len(accs)
    ca, cb = (0 if ta else 1), (1 if tb else 0)
    grid = (*ij(ni, nj), nk)
    comm_ins, comm_outs, comm_scratch = [], [], []
    if comm is not None:
        comm_in_specs, comm_out_specs = comm.specs
        comm_ins, comm_outs, comm_scratch = list(comm.ins), list(comm.out_shape), list(comm.scratch)
        in_specs += comm_in_specs
        out_specs += comm_out_specs
        out_shape += comm_outs
    n_ci, n_co = len(comm_ins), len(comm_outs)

    def body(*refs):
        a_ref, b_ref = refs[:2]
        in_refs = refs[2:2 + n_in]
        ci_refs = refs[2 + n_in:2 + n_in + n_ci]
        rest = refs[2 + n_in + n_ci:]
        out_refs, acc_refs = rest[:n_out], rest[n_out:n_out + n_acc]
        co_refs = rest[n_out + n_acc:n_out + n_acc + n_co]
        scratch = rest[n_out + n_acc + n_co:]
        sem_refs, scratch = scratch[:len(comm_scratch)], scratch[len(comm_scratch):]
        (i, j), kk = ij(pl.program_id(0), pl.program_id(1)), pl.program_id(2)
        if comm is not None:
            first_step, last_step = _grid_ends(grid)
            pl.when(first_step)(lambda: comm.start(ci_refs, co_refs, sem_refs))
        def finish(acc, cols=slice(None)):
            def of(r):
                return r[..., cols] if r.shape[-1] == tn else r[...]

            o_tiles, a_tiles = epi(acc, [of(r) for r in in_refs], i, j)
            for r, t in zip(out_refs, o_tiles, strict=True):
                r[..., cols] = t.astype(r.dtype)
            if n_acc:
                @pl.when(i == 0)
                def _():
                    for r, t in zip(acc_refs, a_tiles, strict=True):
                        r[..., cols] = t

                @pl.when(i > 0)
                def _():
                    for r, t in zip(acc_refs, a_tiles, strict=True):
                        r[..., cols] += t

        if chunk is not None:
            a_tile = a_ref[...].astype(BF16)
            for c0 in range(0, tn, chunk):
                cols = slice(c0, c0 + chunk)
                b_part = b_ref[cols, :] if tb else b_ref[:, cols]
                finish(_dot(a_tile, b_part.astype(BF16), ca, cb), cols)
            if comm is not None:
                pl.when(last_step)(lambda: comm.finish(ci_refs, co_refs, sem_refs))
            return

        part = _dot(a_ref[...].astype(BF16), b_ref[...].astype(BF16), ca, cb)
        if nk == 1:
            finish(part)
        else:
            acc_ref = scratch[0]

            @pl.when(kk == 0)
            def _():
                acc_ref[...] = part

            @pl.when(kk > 0)
            def _():
                acc_ref[...] += part

            @pl.when(kk == nk - 1)
            def _():
                finish(acc_ref[...])

        if comm is not None:
            pl.when(last_step)(lambda: comm.finish(ci_refs, co_refs, sem_refs))

    res = pl.pallas_call(
        body, name=name, grid=grid, in_specs=in_specs, out_specs=out_specs, out_shape=out_shape,
        scratch_shapes=comm_scratch + ([pltpu.VMEM((tm, tn), F32)] if nk > 1 else []),
        compiler_params=_cparams(),
    )(a, b, *[x for x, _, _ in ins], *comm_ins)
    return res


def _grid_ends(grid):
    ids = [pl.program_id(ax) for ax in range(len(grid))]
    first = functools.reduce(jnp.logical_and, [p == 0 for p in ids])
    last = functools.reduce(jnp.logical_and, [p == n - 1 for p, n in zip(ids, grid, strict=True)])
    return first, last


def _tile(tm, tn, dj=0):
    return (tm, tn), (lambda i, j: (i, j + dj))


def _rowvec(tn, dj=0):
    return (1, tn), (lambda i, j: (0, j + dj))


def _plain(acc, tiles, i, j):
    return [acc], []


def _ew(name, fn, ins, outs, rows, tr):
    assert rows % tr == 0, (name, rows, tr)
    in_specs = []
    for x in ins:
        if x.shape[0] == rows:
            in_specs.append(pl.BlockSpec((tr, x.shape[1]), lambda i: (i, 0)))
        else:
            in_specs.append(pl.BlockSpec(x.shape, lambda i: (0, 0)))
    n_in = len(ins)

    def body(*refs):
        res = fn(*[r[...] for r in refs[:n_in]])
        for r, t in zip(refs[n_in:], res, strict=True):
            r[...] = t.astype(r.dtype)

    return pl.pallas_call(
        body, name=name, grid=(rows // tr,), in_specs=in_specs,
        out_specs=[pl.BlockSpec((tr, w), lambda i: (i, 0)) for w, _ in outs],
        out_shape=[jax.ShapeDtypeStruct((rows, w), dt) for w, dt in outs],
        compiler_params=_cparams(),
    )(*ins)


def _cast_bf16(name, x):
    rows = x.shape[0]
    tr = next(t for t in (512, 256, 64) if rows % t == 0)
    return _ew(name, lambda v: (v,), [x], [(x.shape[1], BF16)], rows, tr)[0]


def _prep(x, comm):
    s = x.shape[0]
    half = RET_QK // 2
    inv = 1.0 / (ROPE_BASE ** (jnp.arange(half, dtype=F32) / half))
    inv2 = jnp.concatenate([inv, inv]).reshape(1, RET_QK)
    sign = jnp.concatenate([-jnp.ones((half,), F32), jnp.ones((half,), F32)]).reshape(1, RET_QK)
    tr = min(ROW_TILE, s)
    grid = (s // tr,)
    c_in_specs, c_out_specs, c_out_shape, c_scratch, c_ins, split = _host(comm, 3, 3)

    def body(*refs):
        (x_ref, inv_ref, sign_ref), (xb_ref, cos_ref, sin_ref), _, riding = split(refs)
        i = pl.program_id(0)
        first_step, last_step = _grid_ends(grid)
        pl.when(first_step)(lambda: comm.start(*riding))
        xb_ref[...] = x_ref[...].astype(BF16)
        pos = (lax.broadcasted_iota(jnp.int32, (tr, RET_QK), 0) + i * tr).astype(F32)
        ang = pos * inv_ref[...]
        cos_ref[...] = jnp.cos(ang)
        sin_ref[...] = jnp.sin(ang) * sign_ref[...]
        pl.when(last_step)(lambda: comm.finish(*riding))

    vec = pl.BlockSpec((1, RET_QK), lambda i: (0, 0))
    row = lambda w: pl.BlockSpec((tr, w), lambda i: (i, 0))
    return pl.pallas_call(
        body, name="prep", grid=grid,
        in_specs=[row(D_MODEL), vec, vec] + c_in_specs,
        out_specs=[row(D_MODEL), row(RET_QK), row(RET_QK)] + c_out_specs,
        out_shape=[jax.ShapeDtypeStruct((s, D_MODEL), BF16), jax.ShapeDtypeStruct((s, RET_QK), F32),
                   jax.ShapeDtypeStruct((s, RET_QK), F32)] + c_out_shape,
        scratch_shapes=c_scratch, compiler_params=_cparams(),
    )(x, inv2, sign, *c_ins)


def _swap_halves(x):
    return pltpu.roll(x, RET_QK // 2, 1)


def _norm(u):
    mu = jnp.mean(u, axis=-1, keepdims=True)
    d = u - mu
    var = jnp.mean(d * d, axis=-1, keepdims=True)
    rstd = lax.rsqrt(var + LN_EPS)
    return d * rstd, rstd


def _norm_bwd(dxh, xhat, rstd):
    m1 = jnp.mean(dxh, axis=-1, keepdims=True)
    m2 = jnp.mean(dxh * xhat, axis=-1, keepdims=True)
    return rstd * (dxh - m1 - xhat * m2)


def _colsum(t):
    return jnp.sum(t, axis=0, keepdims=True)


def _split_mm(t, tri):
    hi = t.astype(BF16)
    lo = (t - hi.astype(F32)).astype(BF16)
    return _dot(hi, tri, 1, 0) + _dot(lo, tri, 1, 0)


def _sb_masks():
    t = SB_BLOCK
    lane = lax.broadcasted_iota(jnp.int32, (1, LANES), 1)
    first = lane < SB_DIM
    m0 = jnp.where(first, 1.0, 0.0).astype(BF16)
    m1 = jnp.where(first, 0.0, 1.0).astype(BF16)
    row = lax.broadcasted_iota(jnp.int32, (t, t), 0)
    col = lax.broadcasted_iota(jnp.int32, (t, t), 1)
    return first, (m0, m1), row, col


def _sb_logits(qh, k, causal):
    z = _dot(qh, k, 1, 1)
    lp = jnp.log(1.0 + jnp.exp(-jnp.abs(z)))
    a = jnp.minimum(z, 0.0) - lp
    r = jnp.minimum(-z, 0.0) - lp
    if causal is not None:
        r = jnp.where(causal, r, 0.0)
    return a, r


def _sb_walk(i, blocks, l_ref, causal):
    pl.when(i == 0)(lambda: blocks([(i, causal)]))
    pl.when(i > 0)(lambda: blocks([(i, causal), (i - 1, None)]))

    def alive():
        top = jnp.max(functools.reduce(jnp.maximum, [l_ref[c] for c in range(l_ref.shape[0])]))
        return jnp.where(top > SB_DEAD, 1, 0)

    def cond(c):
        return jnp.logical_and(c[0] < i, c[1] > 0)

    def step(c):
        blocks([(i - 1 - c[0], None)])
        return c[0] + 1, alive()

    lax.while_loop(cond, step, (jnp.int32(1), alive()))


def _host(comm, n_in, n_out):
    if comm is None:
        return [], [], [], [], [], lambda refs: (refs[:n_in], refs[n_in:n_in + n_out], refs[n_in + n_out:], None)
    in_specs, out_specs = comm.specs
    n_ci, n_co, n_sem = len(comm.ins), len(comm.out_shape), len(comm.scratch)

    def split(refs):
        ins, ci = refs[:n_in], refs[n_in:n_in + n_ci]
        rest = refs[n_in + n_ci:]
        outs, co = rest[:n_out], rest[n_out:n_out + n_co]
        sems, scratch = rest[n_out + n_co:n_out + n_co + n_sem], rest[n_out + n_co + n_sem:]
        return ins, outs, scratch, (ci, co, sems)

    return in_specs, out_specs, list(comm.out_shape), list(comm.scratch), list(comm.ins), split


def _sb_qkv_specs(s, g):
    groups = SB_HEADS // 2 // g
    return [pl.BlockSpec((g, SB_BLOCK, LANES), lambda p, i: (p, i, 0)),
            pl.BlockSpec((g, s, LANES), lambda p, i: (groups + p, 0, 0)),
            pl.BlockSpec((g, s, LANES), lambda p, i: (2 * groups + p, 0, 0))]


def _sb_fwd(qkv, s, comm=None):
    t = SB_BLOCK
    g = 2
    nq = s // t
    grid = (SB_HEADS // 2 // g, nq)
    c_in_specs, c_out_specs, c_out_shape, c_scratch, c_ins, split = _host(comm, 3, 2)

    def body(*refs):
        (q_ref, k_ref, v_ref), (o_ref, of_ref), (l_ref, acc_ref), riding = split(refs)
        i = pl.program_id(1)
        if comm is not None:
            first_step, last_step = _grid_ends(grid)
            pl.when(first_step)(lambda: comm.start(*riding))
        first, hmask, row, col = _sb_masks()
        after = jnp.where(row > col, 1.0, 0.0).astype(BF16)
        causal = col < row
        heads = [(p, h) for p in range(g) for h in range(2)]
        qh = {(p, h): q_ref[p] * hmask[h] for p, h in heads}
        l_ref[...] = jnp.zeros_like(l_ref)
        acc_ref[...] = jnp.zeros_like(acc_ref)

        def blocks(todo):
            chains = [(b, p, h) for b in range(len(todo)) for p, h in heads]
            starts = [pl.multiple_of(kb * t, t) for kb, _ in todo]
            ks = {(b, p): k_ref[p, pl.ds(st, t), :] for b, st in enumerate(starts) for p in range(g)}
            vs = {(b, p): v_ref[p, pl.ds(st, t), :] for b, st in enumerate(starts) for p in range(g)}
            ar = {(b, p, h): _sb_logits(qh[p, h], ks[b, p], todo[b][1]) for b, p, h in chains}
            later = {c: _split_mm(ar[c][1], after) for c in chains}
            carry = {(p, h): l_ref[2 * p + h] for p, h in heads}
            w = {}
            for b, (_, mask) in enumerate(todo):
                for p, h in heads:
                    wc = jnp.exp(ar[b, p, h][0] + later[b, p, h] + carry[p, h])
                    w[b, p, h] = wc if mask is None else jnp.where(mask, wc, 0.0)
                carry = {(p, h): carry[p, h] + jnp.sum(ar[b, p, h][1], axis=1, keepdims=True) for p, h in heads}
            pv = {(b, p, h): _dot(w[b, p, h].astype(BF16), vs[b, p], 1, 0) for b, p, h in chains}
            for p in range(g):
                lanes = slice(p * LANES, (p + 1) * LANES)
                acc = acc_ref[:, lanes]
                for b in range(len(todo)):
                    acc = acc + jnp.where(first, pv[b, p, 0], pv[b, p, 1])
                acc_ref[:, lanes] = acc
            for p, h in heads:
                l_ref[2 * p + h] = carry[p, h]

        _sb_walk(i, blocks, l_ref, causal)
        o_ref[...] = acc_ref[...].astype(o_ref.dtype)
        of_ref[...] = acc_ref[...]
        if comm is not None:
            pl.when(last_step)(lambda: comm.finish(*riding))

    blk = pl.BlockSpec((t, g * LANES), lambda p, i: (i, p))
    return pl.pallas_call(
        body, name="sb_fwd", grid=grid,
        in_specs=_sb_qkv_specs(s, g) + c_in_specs,
        out_specs=[blk, blk] + c_out_specs,
        out_shape=[jax.ShapeDtypeStruct((s, SB_WIDTH), BF16), jax.ShapeDtypeStruct((s, SB_WIDTH), F32)] + c_out_shape,
        scratch_shapes=c_scratch + [pltpu.VMEM((2 * g, t, 1), F32), pltpu.VMEM((t, g * LANES), F32)],
        compiler_params=_cparams(),
    )(qkv, qkv, qkv, *c_ins)


def _sb_bwd(qkv, o, do, s, comm=None):
    t = SB_BLOCK
    g = 2
    nq = s // t
    grid = (SB_HEADS // 2 // g, nq)
    c_in_specs, c_out_specs, c_out_shape, c_scratch, c_ins, split = _host(comm, 5, 3)

    def body(*refs):
        ((q_ref, k_ref, v_ref, o_ref, do_ref), (dq_ref, dk_ref, dv_ref),
         (l_ref, e_ref, dq_acc, dk_acc, dv_acc), riding) = split(refs)
        i = pl.program_id(1)
        if comm is not None:
            first_step, last_step = _grid_ends(grid)
            pl.when(first_step)(lambda: comm.start(*riding))
        first, hmask, row, col = _sb_masks()
        after = jnp.where(row > col, 1.0, 0.0).astype(BF16)
        from_here = jnp.where(row >= col, 1.0, 0.0).astype(BF16)
        causal = col < row

        @pl.when(i == 0)
        def _():
            dk_acc[...] = jnp.zeros_like(dk_acc)
            dv_acc[...] = jnp.zeros_like(dv_acc)

        heads = [(p, h) for p in range(g) for h in range(2)]
        lanes = [slice(p * LANES, (p + 1) * LANES) for p in range(g)]
        q = [q_ref[p] for p in range(g)]
        do_ = [do_ref[:, lanes[p]] for p in range(g)]
        qh = {(p, h): q[p] * hmask[h] for p, h in heads}
        doh = {(p, h): do_[p] * hmask[h] for p, h in heads}
        total = {}
        for p in range(g):
            prod = do_[p].astype(F32) * o_ref[:, lanes[p]]
            total[p, 0] = jnp.sum(jnp.where(first, prod, 0.0), axis=1, keepdims=True)
            total[p, 1] = jnp.sum(jnp.where(first, 0.0, prod), axis=1, keepdims=True)
        l_ref[...] = jnp.zeros_like(l_ref)
        e_ref[...] = jnp.zeros_like(e_ref)
        dq_acc[...] = jnp.zeros_like(dq_acc)

        def blocks(todo):
            chains = [(b, p, h) for b in range(len(todo)) for p, h in heads]
            starts = [pl.multiple_of(kb * t, t) for kb, _ in todo]
            ks = {(b, p): k_ref[p, pl.ds(st, t), :] for b, st in enumerate(starts) for p in range(g)}
            vs = {(b, p): v_ref[p, pl.ds(st, t), :] for b, st in enumerate(starts) for p in range(g)}
            ar = {(b, p, h): _sb_logits(qh[p, h], ks[b, p], todo[b][1]) for b, p, h in chains}
            dw = {(b, p, h): _dot(doh[p, h], vs[b, p], 1, 1) for b, p, h in chains}
            later = {c: _split_mm(ar[c][1], after) for c in chains}
            carry = {(p, h): l_ref[2 * p + h] for p, h in heads}
            wb = {}
            for b, (_, mask) in enumerate(todo):
                for p, h in heads:
                    wc = jnp.exp(ar[b, p, h][0] + later[b, p, h] + carry[p, h])
                    wb[b, p, h] = (wc if mask is None else jnp.where(mask, wc, 0.0)).astype(BF16)
                carry = {(p, h): carry[p, h] + jnp.sum(ar[b, p, h][1], axis=1, keepdims=True) for p, h in heads}
            dvs = {(b, p, h): _dot(wb[b, p, h], do_[p], 0, 0) for b, p, h in chains}
            e = {c: dw[c] * wb[c].astype(F32) for c in chains}
            suffix = {c: _split_mm(e[c], from_here) for c in chains}
            e_carry = {(p, h): e_ref[2 * p + h] for p, h in heads}
            dz = {}
            for b, (_, mask) in enumerate(todo):
                for p, h in heads:
                    before = total[p, h] - (suffix[b, p, h] + e_carry[p, h])
                    dzc = e[b, p, h] - jnp.exp(ar[b, p, h][0]) * (e[b, p, h] + before)
                    dz[b, p, h] = (dzc if mask is None else jnp.where(mask, dzc, 0.0)).astype(BF16)
                e_carry = {(p, h): e_carry[p, h] + jnp.sum(e[b, p, h], axis=1, keepdims=True) for p, h in heads}
            dqs = {(b, p, h): _dot(dz[b, p, h], ks[b, p], 1, 0) for b, p, h in chains}
            dks = {(b, p, h): _dot(dz[b, p, h], q[p], 0, 0) for b, p, h in chains}
            for p in range(g):
                dq = dq_acc[:, lanes[p]]
                for b, st in enumerate(starts):
                    dq = dq + jnp.where(first, dqs[b, p, 0], dqs[b, p, 1])
                    dk_acc[pl.ds(st, t), lanes[p]] += jnp.where(first, dks[b, p, 0], dks[b, p, 1])
                    dv_acc[pl.ds(st, t), lanes[p]] += jnp.where(first, dvs[b, p, 0], dvs[b, p, 1])
                dq_acc[:, lanes[p]] = dq
            for p, h in heads:
                l_ref[2 * p + h] = carry[p, h]
                e_ref[2 * p + h] = e_carry[p, h]

        _sb_walk(i, blocks, l_ref, causal)
        dq_ref[...] = (dq_acc[...] * SB_SCALE).astype(dq_ref.dtype)

        @pl.when(i == nq - 1)
        def _():
            dk_ref[...] = dk_acc[...].astype(dk_ref.dtype)
            dv_ref[...] = dv_acc[...].astype(dv_ref.dtype)

        if comm is not None:
            pl.when(last_step)(lambda: comm.finish(*riding))

    once = pl.Buffered(1)
    q_spec, k_spec, v_spec = _sb_qkv_specs(s, g)
    k_spec = pl.BlockSpec(k_spec.block_shape, k_spec.index_map, pipeline_mode=once)
    v_spec = pl.BlockSpec(v_spec.block_shape, v_spec.index_map, pipeline_mode=once)
    blk = pl.BlockSpec((t, g * LANES), lambda p, i: (i, p))
    col_blk = pl.BlockSpec((s, g * LANES), lambda p, i: (0, p), pipeline_mode=once)
    sds = jax.ShapeDtypeStruct((s, SB_WIDTH), BF16)
    return pl.pallas_call(
        body, name="sb_bwd", grid=grid,
        in_specs=[q_spec, k_spec, v_spec, blk, blk] + c_in_specs,
        out_specs=[blk, col_blk, col_blk] + c_out_specs,
        out_shape=[sds, sds, sds] + c_out_shape,
        scratch_shapes=c_scratch + [pltpu.VMEM((2 * g, t, 1), F32), pltpu.VMEM((2 * g, t, 1), F32),
                                    pltpu.VMEM((t, g * LANES), F32), pltpu.VMEM((s, g * LANES), F32),
                                    pltpu.VMEM((s, g * LANES), F32)],
        compiler_params=_cparams(),
    )(qkv, qkv, qkv, o, do, *c_ins)


def _ret_log_gamma():
    lg = np.log1p(-np.exp2(-5.0 - np.arange(RET_HEADS, dtype=np.float32))).astype(np.float32)
    return jnp.asarray(np.broadcast_to(lg[:, None, None], (RET_HEADS, 8, LANES)).copy())


RET_SCRATCH = [pltpu.VMEM((RET_HEADS, RET_QK, RET_V), F32),
               pltpu.VMEM((RET_HEADS, RET_BLOCK, RET_BLOCK), F32),
               pltpu.VMEM((RET_HEADS, RET_BLOCK, 1), F32),
               pltpu.VMEM((RET_HEADS, RET_BLOCK, 1), F32)]


def _ret_begin(n, lg_ref, state, within, q_dec, k_dec):
    @pl.when(n == 0)
    def _():
        c = RET_BLOCK
        state[...] = jnp.zeros_like(state)
        row = lax.broadcasted_iota(jnp.int32, (c, c), 0)
        col = lax.broadcasted_iota(jnp.int32, (c, c), 1)
        rel = jnp.maximum(row - col, 0).astype(F32)
        idx = lax.broadcasted_iota(jnp.int32, (c, 1), 0).astype(F32)
        for h in range(RET_HEADS):
            lg = lg_ref[h, 0:1, 0:1]
            within[h] = jnp.where(row >= col, jnp.exp(lg * rel), 0.0)
            q_dec[h] = jnp.exp(lg * (idx + 1.0))
            k_dec[h] = jnp.exp(lg * (c - 1.0 - idx))


def _chunk_decay(lg_ref, h):
    return jnp.exp(lg_ref[h, 0:1, 0:1] * float(RET_BLOCK))


def _ret_heads(x, width):
    return [x[:, h * width:(h + 1) * width] for h in range(RET_HEADS)]


def _ret_specs(s, reverse=False):
    c = RET_BLOCK
    per_step = min(RET_CHUNKS_PER_STEP, s // c)
    rows = c * per_step
    nc = s // rows
    pos = (lambda n: nc - 1 - n) if reverse else (lambda n: n)
    chunks = [slice(u * c, (u + 1) * c) for u in range(per_step)]
    q_spec = pl.BlockSpec((rows, RET_QK_WIDTH), lambda n: (pos(n), 0))
    k_spec = pl.BlockSpec((rows, RET_QK_WIDTH), lambda n: (pos(n), 1))
    v_spec = pl.BlockSpec((rows, RET_V_WIDTH), lambda n: (pos(n), 0))
    lg_spec = pl.BlockSpec((RET_HEADS, 8, LANES), lambda n: (0, 0, 0))
    rope_spec = pl.BlockSpec((rows, RET_QK), lambda n: (pos(n), 0))
    return nc, chunks[::-1] if reverse else chunks, q_spec, k_spec, v_spec, lg_spec, rope_spec


def _ret_fwd(rqk, rvg, s):
    nc, chunks, q_spec, k_spec, v_spec, lg_spec, _ = _ret_specs(s)
    g_spec = pl.BlockSpec(v_spec.block_shape, lambda n: (n, 1))
    heads = range(RET_HEADS)

    def body(q_ref, k_ref, v_ref, g_ref, lg_ref, r_ref, y_ref, state, within, q_dec, k_dec):
        n = pl.program_id(0)
        _ret_begin(n, lg_ref, state, within, q_dec, k_dec)
        for rows in chunks:
            q, k = _ret_heads(q_ref[rows], RET_QK), _ret_heads(k_ref[rows], RET_QK)
            v, g = _ret_heads(v_ref[rows], RET_V), _ret_heads(g_ref[rows], RET_V)
            scores = [_dot(q[h].astype(BF16), k[h].astype(BF16), 1, 1) * within[h] for h in heads]
            cross = [_dot((q[h] * q_dec[h]).astype(BF16), state[h].astype(BF16), 1, 0) for h in heads]
            out = [_dot(scores[h].astype(BF16), v[h], 1, 0) + cross[h] for h in heads]
            grown = [_dot((k[h] * k_dec[h]).astype(BF16), v[h], 0, 0) for h in heads]
            for h in heads:
                sl = slice(h * RET_V, (h + 1) * RET_V)
                r_ref[rows, sl] = out[h]
                xhat, _ = _norm(out[h])
                gh = g[h].astype(F32)
                y_ref[rows, sl] = (gh * _sigmoid(gh) * xhat).astype(y_ref.dtype)
                state[h] = state[h] * _chunk_decay(lg_ref, h) + grown[h]

    return pl.pallas_call(
        body, name="ret_fwd", grid=(nc,),
        in_specs=[q_spec, k_spec, v_spec, g_spec, lg_spec],
        out_specs=[v_spec, v_spec],
        out_shape=[jax.ShapeDtypeStruct((s, RET_V_WIDTH), F32), jax.ShapeDtypeStruct((s, RET_V_WIDTH), BF16)],
        scratch_shapes=RET_SCRATCH,
        compiler_params=_cparams(),
    )(rqk, rqk, rvg, rvg, _ret_log_gamma())


def _rope_bwd(d, cos, sin):
    return d * cos + _swap_halves(d * sin)


def _ret_bwd_q(rqk, rv, d_out, cos2, sin2, s):
    nc, chunks, q_spec, k_spec, v_spec, lg_spec, rope_spec = _ret_specs(s)
    heads = range(RET_HEADS)

    def body(k_ref, v_ref, d_ref, lg_ref, cos_ref, sin_ref, dq_ref, state, within, q_dec, k_dec):
        n = pl.program_id(0)
        _ret_begin(n, lg_ref, state, within, q_dec, k_dec)
        for rows in chunks:
            k = _ret_heads(k_ref[rows], RET_QK)
            v, d = _ret_heads(v_ref[rows], RET_V), _ret_heads(d_ref[rows], RET_V)
            cos, sin = cos_ref[rows], sin_ref[rows]
            d_scores = [_dot(d[h], v[h], 1, 1) * within[h] for h in heads]
            cross = [q_dec[h] * _dot(d[h], state[h].astype(BF16), 1, 1) for h in heads]
            dq = [_dot(d_scores[h].astype(BF16), k[h].astype(BF16), 1, 0) + cross[h] for h in heads]
            grown = [_dot((k[h] * k_dec[h]).astype(BF16), v[h], 0, 0) for h in heads]
            for h in heads:
                sl = slice(h * RET_QK, (h + 1) * RET_QK)
                dq_ref[rows, sl] = (_rope_bwd(dq[h], cos, sin) * RET_SCALE).astype(dq_ref.dtype)
                state[h] = state[h] * _chunk_decay(lg_ref, h) + grown[h]

    return pl.pallas_call(
        body, name="ret_bwd_q", grid=(nc,),
        in_specs=[k_spec, v_spec, v_spec, lg_spec, rope_spec, rope_spec],
        out_specs=q_spec,
        out_shape=jax.ShapeDtypeStruct((s, RET_QK_WIDTH), BF16),
        scratch_shapes=RET_SCRATCH,
        compiler_params=_cparams(),
    )(rqk, rv, d_out, _ret_log_gamma(), cos2, sin2)


def _ret_bwd_kv(rqk, rv, d_out, cos2, sin2, s):
    nc, chunks, q_spec, k_spec, v_spec, lg_spec, rope_spec = _ret_specs(s, reverse=True)
    heads = range(RET_HEADS)

    def body(q_ref, k_ref, v_ref, d_ref, lg_ref, cos_ref, sin_ref, dk_ref, dv_ref, state, within, q_dec, k_dec):
        n = pl.program_id(0)
        _ret_begin(n, lg_ref, state, within, q_dec, k_dec)
        for rows in chunks:
            q, k = _ret_heads(q_ref[rows], RET_QK), _ret_heads(k_ref[rows], RET_QK)
            v, d = _ret_heads(v_ref[rows], RET_V), _ret_heads(d_ref[rows], RET_V)
            cos, sin = cos_ref[rows], sin_ref[rows]
            qb, kb = [q[h].astype(BF16) for h in heads], [k[h].astype(BF16) for h in heads]
            st = [state[h].astype(BF16) for h in heads]
            scores = [_dot(qb[h], kb[h], 1, 1) * within[h] for h in heads]
            d_scores = [_dot(d[h], v[h], 1, 1) * within[h] for h in heads]
            dk = [_dot(d_scores[h].astype(BF16), qb[h], 0, 0) + k_dec[h] * _dot(v[h], st[h], 1, 1) for h in heads]
            dv = [_dot(scores[h].astype(BF16), d[h], 0, 0) + k_dec[h] * _dot(kb[h], st[h], 1, 0) for h in heads]
            grown = [_dot((q[h] * q_dec[h]).astype(BF16), d[h], 0, 0) for h in heads]
            for h in heads:
                dk_ref[rows, h * RET_QK:(h + 1) * RET_QK] = _rope_bwd(dk[h], cos, sin).astype(dk_ref.dtype)
                dv_ref[rows, h * RET_V:(h + 1) * RET_V] = dv[h].astype(dv_ref.dtype)
                state[h] = state[h] * _chunk_decay(lg_ref, h) + grown[h]

    return pl.pallas_call(
        body, name="ret_bwd_kv", grid=(nc,),
        in_specs=[q_spec, k_spec, v_spec, v_spec, lg_spec, rope_spec, rope_spec],
        out_specs=[q_spec, v_spec],
        out_shape=[jax.ShapeDtypeStruct((s, RET_QK_WIDTH), BF16), jax.ShapeDtypeStruct((s, RET_V_WIDTH), BF16)],
        scratch_shapes=RET_SCRATCH,
        compiler_params=_cparams(),
    )(rqk, rqk, rv, d_out, _ret_log_gamma(), cos2, sin2)


def _xattn_probs(scores):
    sc = scores - jnp.max(scores, axis=-1, keepdims=True)
    p = jnp.exp(sc)
    return p / jnp.sum(p, axis=-1, keepdims=True)


def _xattn_heads(q_ref, kv_ref):
    sls = [slice(h * MEM_DIM, (h + 1) * MEM_DIM) for h in range(MEM_HEADS)]
    q = [q_ref[:, sl] for sl in sls]
    k = [kv_ref[:, sl] for sl in sls]
    v = [kv_ref[:, D_MODEL + h * MEM_DIM:D_MODEL + (h + 1) * MEM_DIM] for h in range(MEM_HEADS)]
    return sls, q, k, v


def _xattn_fwd(qm, kv, s):
    tq = min(XATTN_ROWS, s)
    heads = range(MEM_HEADS)

    def body(q_ref, kv_ref, o_ref):
        sls, q, k, v = _xattn_heads(q_ref, kv_ref)
        scores = [_dot(q[h], k[h], 1, 1) for h in heads]
        p = [_xattn_probs(scores[h]).astype(BF16) for h in heads]
        out = [_dot(p[h], v[h], 1, 0) for h in heads]
        for h in heads:
            o_ref[:, sls[h]] = out[h].astype(o_ref.dtype)

    return pl.pallas_call(
        body, name="xattn_fwd", grid=(s // tq,),
        in_specs=[pl.BlockSpec((tq, D_MODEL), lambda i: (i, 0)),
                  pl.BlockSpec((MEM_LEN, 2 * D_MODEL), lambda i: (0, 0))],
        out_specs=pl.BlockSpec((tq, D_MODEL), lambda i: (i, 0)),
        out_shape=jax.ShapeDtypeStruct((s, D_MODEL), BF16),
        compiler_params=_cparams(),
    )(qm, kv)


def _xattn_bwd(qm, kv, do, s):
    tq = min(XATTN_ROWS, s)

    def body(q_ref, kv_ref, do_ref, dq_ref, dkv_ref):
        i = pl.program_id(0)

        @pl.when(i == 0)
        def _():
            dkv_ref[...] = jnp.zeros_like(dkv_ref)

        heads = range(MEM_HEADS)
        sls, q, k, v = _xattn_heads(q_ref, kv_ref)
        d = [do_ref[:, sl] for sl in sls]
        scores = [_dot(q[h], k[h], 1, 1) for h in heads]
        dp = [_dot(d[h], v[h], 1, 1) for h in heads]
        p = [_xattn_probs(scores[h]) for h in heads]
        ds = [(p[h] * (dp[h] - jnp.sum(p[h] * dp[h], axis=-1, keepdims=True))).astype(BF16) for h in heads]
        dq = [_dot(ds[h], k[h], 1, 0) for h in heads]
        dk = [_dot(ds[h], q[h], 0, 0) for h in heads]
        dv = [_dot(p[h].astype(BF16), d[h], 0, 0) for h in heads]
        for h in heads:
            dq_ref[:, sls[h]] = (dq[h] * MEM_SCALE).astype(dq_ref.dtype)
            dkv_ref[:, sls[h]] += dk[h]
            dkv_ref[:, D_MODEL + h * MEM_DIM:D_MODEL + (h + 1) * MEM_DIM] += dv[h]

    row_blk = pl.BlockSpec((tq, D_MODEL), lambda i: (i, 0))
    kv_blk = pl.BlockSpec((MEM_LEN, 2 * D_MODEL), lambda i: (0, 0))
    return pl.pallas_call(
        body, name="xattn_bwd", grid=(s // tq,),
        in_specs=[row_blk, kv_blk, row_blk],
        out_specs=[row_blk, kv_blk],
        out_shape=[jax.ShapeDtypeStruct((s, D_MODEL), BF16), jax.ShapeDtypeStruct((MEM_LEN, 2 * D_MODEL), F32)],
        compiler_params=_cparams(),
    )(qm, kv, do)


def _place():
    x, y, c = lax.axis_index("x"), lax.axis_index("y"), lax.axis_index("c")
    others = [(1 - x, y), (x, 1 - y), (1 - x, 1 - y)]
    return x, y, c, others


def _slab(ref, axis, chip, size):
    if len(ref.shape) == 3:
        return ref.at[chip]
    start = pl.multiple_of(chip * size, LANES if axis == 1 else 16)
    if axis == 0:
        return ref.at[pl.ds(start, size), :]
    return ref.at[:, pl.ds(start, size)]


class _CommPlan:
    def __init__(self, ins, out_shape, scratch, start, finish):
        self.ins, self.out_shape, self.scratch, self.start, self.finish = ins, out_shape, scratch, start, finish

    @property
    def specs(self):
        any_spec = pl.BlockSpec(memory_space=pl.ANY)
        return [any_spec] * len(self.ins), [any_spec] * len(self.out_shape)


def _gather_plan(names, shards):
    spec = {name: (shape, axis) for name, shape, axis in BIG}
    nw = len(names)

    def shard_half(ref, c):
        rows = ref.shape[0] // 2
        return ref.at[pl.ds(pl.multiple_of(c * rows, 16), rows), :]

    def region(ref, w, chip, c):
        shape, axis = spec[names[w]]
        size = shape[axis] // N_CHIPS
        if names[w] in CHIP_MAJOR:
            rows = shape[0] // 2
            return ref.at[chip, pl.ds(pl.multiple_of(c * rows, 16), rows), :]
        if axis == 0:
            rows = size // 2
            return ref.at[pl.ds(pl.multiple_of(chip * size + c * rows, 16), rows), :]
        rows = shape[0] // 2
        return ref.at[pl.ds(pl.multiple_of(c * rows, 16), rows), pl.ds(pl.multiple_of(chip * size, LANES), size)]

    def ops(shard, full, sems):
        ici_send, ici_recv, d2d_send, d2d_recv, local_sems = sems
        x, y, c, others = _place()
        mine, sibling = 2 * x + y, (x, y, 1 - c)
        local, over_ici, arrived, passed_on, from_sibling = [], [], [], [], []
        for w in range(nw):
            shape, axis = spec[names[w]]
            local.append(pltpu.make_async_copy(shard[w], _slab(full[w], axis, mine, shape[axis] // N_CHIPS),
                                               local_sems.at[w]))
            for t, (qx, qy) in enumerate(others):
                n, theirs = 3 * w + t, 2 * qx + qy
                over_ici.append(pltpu.make_async_remote_copy(
                    src_ref=shard_half(shard[w], c), dst_ref=region(full[w], w, mine, c),
                    send_sem=ici_send.at[n], recv_sem=ici_recv.at[n], device_id=(qx, qy, c), device_id_type=MESH))
                arrived.append(pltpu.make_async_remote_copy(
                    src_ref=shard_half(shard[w], c), dst_ref=region(full[w], w, theirs, c),
                    send_sem=ici_send.at[n], recv_sem=ici_recv.at[n], device_id=(qx, qy, c), device_id_type=MESH))
                passed_on.append(pltpu.make_async_remote_copy(
                    src_ref=region(full[w], w, theirs, c), dst_ref=region(full[w], w, theirs, c),
                    send_sem=d2d_send.at[n], recv_sem=d2d_recv.at[n], device_id=sibling, device_id_type=MESH))
                from_sibling.append(pltpu.make_async_remote_copy(
                    src_ref=region(full[w], w, theirs, c), dst_ref=region(full[w], w, theirs, 1 - c),
                    send_sem=d2d_send.at[n], recv_sem=d2d_recv.at[n], device_id=sibling, device_id_type=MESH))
        return local, over_ici, arrived, passed_on, from_sibling

    def start(shard, full, sems):
        local, over_ici, _, _, _ = ops(shard, full, sems)
        for cp in local + over_ici:
            cp.start()

    def finish(shard, full, sems):
        local, over_ici, arrived, passed_on, from_sibling = ops(shard, full, sems)
        for got, onward in zip(arrived, passed_on, strict=True):
            got.wait_recv()
            onward.start()
        for got in from_sibling:
            got.wait_recv()
        for cp in over_ici + passed_on:
            cp.wait_send()
        for cp in local:
            cp.wait()

    dma = pltpu.SemaphoreType.DMA
    return _CommPlan(
        ins=[shards[name] for name in names],
        out_shape=[jax.ShapeDtypeStruct((N_CHIPS,) + _shard_shape(*spec[name]) if name in CHIP_MAJOR
                                        else spec[name][0], BF16) for name in names],
        scratch=[dma((3 * nw,)), dma((3 * nw,)), dma((3 * nw,)), dma((3 * nw,)), dma((nw,))],
        start=start, finish=finish)


def _shard_shape(shape, axis):
    return tuple(d // N_CHIPS if a == axis else d for a, d in enumerate(shape))


def _exchange_plan(names, grads):
    spec = {name: (shape, axis) for name, shape, axis in BIG}
    nw = len(names)

    def ops(grad, stack, sems):
        send_sems, recv_sems, local_sems = sems
        x, y, c, others = _place()
        mine = 2 * x + y
        me, sibling = (x, y, c), (x, y, 1 - c)

        def dev(px, py, pc):
            return 4 * px + 2 * py + pc

        def copy(w, n, src, slot, to):
            return pltpu.make_async_remote_copy(
                src_ref=src, dst_ref=stack[w].at[slot], send_sem=send_sems.at[7 * w + n],
                recv_sem=recv_sems.at[7 * w + n], device_id=to, device_id_type=MESH)

        local, first, arrived, passed_on, from_sibling = [], [], [], [], []
        for w in range(nw):
            shape, axis = spec[names[w]]
            size = shape[axis] // N_CHIPS
            own = _slab(grad[w], axis, mine, size)
            local.append(pltpu.make_async_copy(own, stack[w].at[dev(*me)], local_sems.at[w]))
            first.append(copy(w, 0, own, dev(*me), sibling))
            from_sibling.append(copy(w, 0, own, dev(*sibling), me))
            for t, (qx, qy) in enumerate(others):
                got = stack[w].at[dev(qx, qy, c)]
                first.append(copy(w, 1 + t, _slab(grad[w], axis, 2 * qx + qy, size), dev(*me), (qx, qy, c)))
                arrived.append(copy(w, 1 + t, got, dev(qx, qy, c), me))
                passed_on.append(copy(w, 4 + t, got, dev(qx, qy, c), sibling))
                from_sibling.append(copy(w, 4 + t, got, dev(qx, qy, 1 - c), me))
        return local, first, arrived, passed_on, from_sibling

    def start(grad, stack, sems):
        local, first, _, _, _ = ops(grad, stack, sems)
        for cp in local + first:
            cp.start()

    def finish(grad, stack, sems):
        local, first, arrived, passed_on, from_sibling = ops(grad, stack, sems)
        for got, onward in zip(arrived, passed_on, strict=True):
            got.wait_recv()
            onward.start()
        for got in from_sibling:
            got.wait_recv()
        for cp in first + passed_on:
            cp.wait_send()
        for cp in local:
            cp.wait()

    dma = pltpu.SemaphoreType.DMA
    return _CommPlan(
        ins=[grads[name] for name in names],
        out_shape=[jax.ShapeDtypeStruct((N_DEV,) + _shard_shape(*spec[name]), BF16) for name in names],
        scratch=[dma((7 * nw,)), dma((7 * nw,)), dma((nw,))],
        start=start, finish=finish)


def _adamw(w, g, m, v):
    m = ADAM_B1 * m + (1.0 - ADAM_B1) * g
    v = ADAM_B2 * v + (1.0 - ADAM_B2) * (g * g)
    m_hat = m / (1.0 - ADAM_B1 ** ADAM_STEP)
    v_hat = v / (1.0 - ADAM_B2 ** ADAM_STEP)
    delta = -ADAM_LR * (m_hat / (jnp.sqrt(v_hat) + ADAM_EPS) + ADAM_WD * w)
    return delta, m, v


def _reduce_adamw(name, stack, w, m, v):
    rows, cols = w.shape
    tr = next(t for t in (256, 128, 64) if rows % t == 0)

    def body(s_ref, w_ref, m_ref, v_ref, g_ref, d_ref, nm_ref, nv_ref):
        g = s_ref[0].astype(F32)
        for d in range(1, N_DEV):
            g = g + s_ref[d].astype(F32)
        g_ref[...] = g
        d_ref[...], nm_ref[...], nv_ref[...] = _adamw(w_ref[...], g, m_ref[...], v_ref[...])

    blk = pl.BlockSpec((tr, cols), lambda i: (i, 0))
    return pl.pallas_call(
        body, name=name, grid=(rows // tr,),
        in_specs=[pl.BlockSpec((N_DEV, tr, cols), lambda i: (0, i, 0)), blk, blk, blk],
        out_specs=[blk] * 4, out_shape=[jax.ShapeDtypeStruct((rows, cols), F32)] * 4,
        compiler_params=_cparams(),
    )(stack, w, m, v)


def _small_step(pack, w, m, v):
    def body(p_ref, w_ref, m_ref, v_ref, g_ref, d_ref, nm_ref, nv_ref, loss_ref, all_ref, send_sems, recv_sems):
        x, y, c, _ = _place()
        me = 4 * x + 2 * y + c
        all_ref[me] = p_ref[...]
        sent = []
        for n in range(1, N_DEV):
            peer = me ^ n
            cp = pltpu.make_async_remote_copy(
                src_ref=p_ref, dst_ref=all_ref.at[me], send_sem=send_sems.at[n - 1], recv_sem=recv_sems.at[n - 1],
                device_id=(peer // 4, (peer // 2) % 2, peer % 2), device_id_type=MESH)
            cp.start()
            sent.append(cp)
        for n in range(1, N_DEV):
            peer = me ^ n
            pltpu.make_async_remote_copy(
                src_ref=p_ref, dst_ref=all_ref.at[peer], send_sem=send_sems.at[n - 1], recv_sem=recv_sems.at[n - 1],
                device_id=(peer // 4, (peer // 2) % 2, peer % 2), device_id_type=MESH).wait_recv()
        for cp in sent:
            cp.wait_send()
        tot = all_ref[0]
        for d in range(1, N_DEV):
            tot = tot + all_ref[d]
        g = tot[:SMALL_ROWS]
        g_ref[...] = g
        d_ref[...], nm_ref[...], nv_ref[...] = _adamw(w_ref[...], g, m_ref[...], v_ref[...])
        loss_ref[...] = jnp.sum(jnp.sum(tot[SMALL_ROWS:], axis=1, keepdims=True), axis=0, keepdims=True)

    vm = pl.BlockSpec(memory_space=pltpu.VMEM)
    small = jax.ShapeDtypeStruct((SMALL_ROWS, LANES), F32)
    return pl.pallas_call(
        body, name="small_step",
        in_specs=[vm] * 4, out_specs=[vm] * 5,
        out_shape=[small] * 4 + [jax.ShapeDtypeStruct((1, 1), F32)],
        scratch_shapes=[pltpu.VMEM((N_DEV, PACK_ROWS, LANES), F32),
                        pltpu.SemaphoreType.DMA((N_DEV - 1,)), pltpu.SemaphoreType.DMA((N_DEV - 1,))],
    )(pack, w, m, v)


LATER_WEIGHTS = tuple(name for name, _, _ in BIG if name != "w_in")


def _layer_step(x, mem, tgt, shards, vec):
    s = x.shape[0]
    d = D_MODEL
    tm = min(ROW_TILE, s)
    tl = min(WIDE_TILE, s)
    xb, cos2, sin2, w_in = _prep(x, _gather_plan(("w_in",), shards))
    bf = lambda w: ((s, w), BF16)
    f32 = lambda w: ((s, w), F32)

    w_flat = jnp.concatenate(list(w_in), axis=1)
    w_sb, w_rqk = w_flat[:, :OFF_RET_Q], w_flat[:, OFF_RET_Q:OFF_RET_V]
    w_rvg, w_gate = w_flat[:, OFF_RET_V:OFF_GATE], w_flat[:, OFF_GATE:]
    q_scale = lambda width, q_width, scale: jnp.concatenate(
        [jnp.full((1, q_width), scale, F32), jnp.ones((1, width - q_width), F32)], axis=1)
    n_groups = 3 * SB_WIDTH // LANES

    def sb_epi(acc, t, i, j):
        scaled = acc * t[0]
        return [jnp.stack([scaled[:, g * LANES:(g + 1) * LANES] for g in range(n_groups)])], []

    (sb_qkv,) = _mm(
        "in_sb", xb, w_sb, s, 3 * SB_WIDTH, d, tm=tl, tn=3 * SB_WIDTH, tk=d, epi=sb_epi,
        ins=[(q_scale(3 * SB_WIDTH, SB_WIDTH, SB_SCALE), *_rowvec(3 * SB_WIDTH))],
        outs=[((n_groups, s, LANES), BF16, (n_groups, tl, LANES), lambda i, j: (0, i, 0))])

    def rope_epi(acc, t, i, j):
        cos, sin, scale = t
        parts = []
        for g in range(acc.shape[1] // RET_QK):
            xg = acc[:, g * RET_QK:(g + 1) * RET_QK]
            parts.append(xg * cos + _swap_halves(xg) * sin)
        return [jnp.concatenate(parts, axis=1) * scale], []

    rope_in = ((tl, RET_QK), lambda i, j: (i, 0))
    (rqk,) = _mm("in_rqk", xb, w_rqk, s, 2 * RET_QK_WIDTH, d, tm=tl, tn=2 * RET_QK_WIDTH, tk=d, epi=rope_epi,
                 chunk=MXU_COLS,
                 ins=[(cos2, *rope_in), (sin2, *rope_in),
                      (q_scale(2 * RET_QK_WIDTH, RET_QK_WIDTH, RET_SCALE), *_rowvec(2 * RET_QK_WIDTH))],
                 outs=[(*f32(2 * RET_QK_WIDTH), *_tile(tl, 2 * RET_QK_WIDTH))])
    (rvg,) = _mm("in_rvg", xb, w_rvg, s, 2 * RET_V_WIDTH, d, tm=tl, tn=2 * RET_V_WIDTH, tk=d, chunk=MXU_COLS,
                 epi=_plain, outs=[(*bf(2 * RET_V_WIDTH), *_tile(tl, 2 * RET_V_WIDTH))])
    (gates,) = _mm("in_gate", xb, w_gate, s, 2 * d, d, tm=tl, tn=2 * d, tk=d, chunk=MXU_COLS,
                   epi=lambda acc, t, i, j: ([_sigmoid(acc + t[0])], []),
                   ins=[(vec["b_gate"], *_rowvec(2 * d))], outs=[(*bf(2 * d), *_tile(tl, 2 * d))])

    sb_out, sb_out_f32, *gathered = _sb_fwd(sb_qkv, s, comm=_gather_plan(LATER_WEIGHTS, shards))
    wt = dict(zip(LATER_WEIGHTS, gathered, strict=True))
    ret, gated = _ret_fwd(rqk, rvg, s)
    (y_sb,) = _mm("sb_o", sb_out, wt["w_sb_o"], s, d, SB_WIDTH, tm=tl, tn=d, tk=SB_WIDTH, epi=_plain,
                  outs=[(*bf(d), *_tile(tl, d))])
    y_ret, mixin = _mm(
        "ret_o", gated, wt["w_ret_o"], s, d, RET_V_WIDTH, tm=tl, tn=d, tk=RET_V_WIDTH, chunk=MXU_COLS,
        epi=lambda acc, t, i, j: ([acc, t[0].astype(F32) * t[2].astype(F32) + t[1].astype(F32) * acc], []),
        ins=[(gates, *_tile(tl, d)), (gates, *_tile(tl, d, 1)), (y_sb, *_tile(tl, d))],
        outs=[(*bf(d), *_tile(tl, d)), (*bf(d), *_tile(tl, d))])

    def ln_epi(acc, t, i, j):
        *res, g, b = t
        prev = res[0] if len(res) == 1 else res[0] * res[1] + res[2]
        xhat, rstd = _norm(DN_ALPHA * prev + acc)
        return [xhat * g + b, xhat, rstd], []

    full = _tile(tm, d)
    col1 = ((tm, 1), lambda i, j: (i, 0))
    vec_in = lambda name: (vec[name], *_rowvec(d))
    ln_outs = [(*bf(d), *full), (*f32(d), *full), ((s, 1), F32, *col1)]
    x1b, xhat1, rstd1 = _mm(
        "mix_o", mixin, wt["w_mix_o"], s, d, d, tm=tm, tn=d, tk=d, epi=ln_epi,
        ins=[(x, *full), vec_in("ln1_g"), vec_in("ln1_b")], outs=ln_outs)

    (qm,) = _mm("mem_q", x1b, wt["w_mem_q"], s, d, d, tm=tl, tn=d, tk=d,
                epi=lambda acc, t, i, j: ([acc * MEM_SCALE], []), outs=[(*bf(d), *_tile(tl, d))])
    (kv,) = _mm("mem_kv", mem, wt["w_mem_kv"], MEM_LEN, 2 * d, d, tm=MEM_LEN, tn=d, tk=d, epi=_plain,
                outs=[((MEM_LEN, 2 * d), BF16, *_tile(MEM_LEN, d))])
    att = _xattn_fwd(qm, kv, s)
    x2b, xhat2, rstd2 = _mm(
        "mem_o", att, wt["w_mem_o"], s, d, d, tm=tm, tn=d, tk=d, epi=ln_epi,
        ins=[(xhat1, *full), vec_in("ln1_g"), vec_in("ln1_b"), vec_in("ln2_g"), vec_in("ln2_b")], outs=ln_outs)

    fh = FFN_HIDDEN
    tf = fh // 2
    (f1,) = _mm("ffn_in1", x2b, wt["w_ffn_in"], s, fh, d, tm=tl, tn=tf, tk=d, epi=_plain, j_outer=True,
                outs=[(*bf(fh), *_tile(tl, tf))])

    def swiglu_epi(acc, t, i, j):
        a = t[0].astype(F32)
        return [acc, a * _sigmoid(a) * acc], []

    f2, act = _mm(
        "ffn_in2", x2b, wt["w_ffn_in"], s, fh, d, tm=tm, tn=fh, tk=d, b_off=(0, 1), epi=swiglu_epi, chunk=MXU_COLS,
        ins=[(f1, *_tile(tm, fh))], outs=[(*bf(fh), *_tile(tm, fh)), (*bf(fh), *_tile(tm, fh))])

    def head_epi(acc, t, i, j):
        prev_hat, prev_g, prev_b, g, b, target = t
        xhat, rstd = _norm(DN_ALPHA * (prev_hat * prev_g + prev_b) + acc)
        err = xhat * g + b - target
        dy = err * (1.0 / d)
        du = _norm_bwd(dy * g, xhat, rstd)
        return [du], [_colsum(dy * xhat), _colsum(dy), _colsum(err * err) * (0.5 / d)]

    vec_acc = ((1, d), F32)
    du3b, dg3, db3, loss_cols = _mm(
        "ffn_out", act, wt["w_ffn_out"], s, d, fh, tm=tm, tn=d, tk=fh, epi=head_epi,
        ins=[(xhat2, *full), vec_in("ln2_g"), vec_in("ln2_b"), vec_in("ln3_g"), vec_in("ln3_b"), (tgt, *full)],
        outs=[(*bf(d), *full)], accs=[vec_acc] * 3)

    grads = {}
    ts = min(SEQ_TILE, s)

    def wgrad(name, a, b, m, n, tm_, tn_, tk_=None):
        (g,) = _mm(name, a, b, m, n, a.shape[0], tm=tm_, tn=tn_, tk=tk_ or ts, ta=True, epi=_plain,
                   outs=[((m, n), BF16, *_tile(tm_, tn_))])
        return g

    def ffn_bwd_epi(acc, t, i, j):
        a, b = t[0].astype(F32), t[1].astype(F32)
        sg = _sigmoid(a)
        return [acc * b * (sg * (1.0 + a * (1.0 - sg))), acc * (a * sg)], []

    df1, df2 = _mm(
        "ffn_out_t", du3b, wt["w_ffn_out"], s, fh, d, tm=tm, tn=fh, tk=d, tb=True, epi=ffn_bwd_epi, chunk=MXU_COLS,
        ins=[(f1, *_tile(tm, fh)), (f2, *_tile(tm, fh))],
        outs=[(*bf(fh), *_tile(tm, fh)), (*bf(fh), *_tile(tm, fh))])
    grads["w_ffn_out"] = wgrad("g_ffn_out", act, du3b, fh, d, tf, d)
    grads["w_ffn_in"] = jnp.concatenate(
        [wgrad("g_ffn_in1", x2b, df1, d, fh, d, tf), wgrad("g_ffn_in2", x2b, df2, d, fh, d, tf)], axis=1)
    (dx2a,) = _mm("ffn_in1_t", df1, wt["w_ffn_in"], s, d, fh, tm=tl, tn=d, tk=fh, tb=True, epi=_plain,
                  outs=[(*f32(d), *_tile(tl, d))])

    def ln_bwd(name, a, b, k, tk, b_off, more, scales, xhat, rstd, g):
        def epi(acc, t, i, j):
            *extra, xh, rs, gg = t
            dy = acc
            for e, sc in zip(extra, scales, strict=True):
                dy = dy + e.astype(F32) * sc
            return [_norm_bwd(dy * gg, xh, rs)], [_colsum(dy * xh), _colsum(dy)]

        return _mm(name, a, b, s, d, k, tm=tm, tn=d, tk=tk, tb=True, b_off=b_off, epi=epi,
                   ins=[(e, *full) for e in more] + [(xhat, *full), (rstd, *col1), (g, *_rowvec(d))],
                   outs=[(*bf(d), *full)], accs=[vec_acc] * 2)

    du2b, dg2, db2 = ln_bwd("ffn_in2_t", df2, wt["w_ffn_in"], fh, fh, (0, 1), [dx2a, du3b], [1.0, DN_ALPHA],
                            xhat2, rstd2, vec["ln2_g"])

    (datt,) = _mm("mem_o_t", du2b, wt["w_mem_o"], s, d, d, tm=tl, tn=d, tk=d, tb=True, epi=_plain,
                  outs=[(*bf(d), *_tile(tl, d))])
    grads["w_mem_o"] = wgrad("g_mem_o", att, du2b, d, d, d, d)
    dqm, dkv = _xattn_bwd(qm, kv, datt, s)
    grads["w_mem_q"] = wgrad("g_mem_q", x1b, dqm, d, d, d, d)
    grads["w_mem_kv"] = wgrad("g_mem_kv", mem, dkv, d, 2 * d, d, d, MEM_LEN)
    du1b, dg1, db1 = ln_bwd("mem_q_t", dqm, wt["w_mem_q"], d, d, (0, 0), [du2b], [DN_ALPHA],
                            xhat1, rstd1, vec["ln1_g"])

    def merge_bwd_epi(acc, t, i, j):
        g0, g1, ysb, yret = (v.astype(F32) for v in t)
        dgate0 = acc * ysb * (g0 * (1.0 - g0))
        dgate1 = acc * yret * (g1 * (1.0 - g1))
        return [dgate0, dgate1, acc * g0, acc * g1], [_colsum(dgate0), _colsum(dgate1)]

    dgate0, dgate1, dy_sb, dy_ret, dbg0, dbg1 = _mm(
        "mix_o_t", du1b, wt["w_mix_o"], s, d, d, tm=tm, tn=d, tk=d, tb=True, epi=merge_bwd_epi,
        ins=[(gates, *full), (gates, *_tile(tm, d, 1)), (y_sb, *full), (y_ret, *full)],
        outs=[(*bf(d), *full)] * 4, accs=[vec_acc] * 2)
    grads["w_mix_o"] = wgrad("g_mix_o", mixin, du1b, d, d, d, d)
    grads["w_sb_o"] = wgrad("g_sb_o", sb_out, dy_sb, SB_WIDTH, d, SB_WIDTH, d)
    grads["w_ret_o"] = wgrad("g_ret_o", gated, dy_ret, RET_V_WIDTH, d, RET_V_WIDTH, d)
    (dsb_out,) = _mm("sb_o_t", dy_sb, wt["w_sb_o"], s, SB_WIDTH, d, tm=tl, tn=SB_WIDTH, tk=d, tb=True, epi=_plain,
                     outs=[(*bf(SB_WIDTH), *_tile(tl, SB_WIDTH))])

    def gate_norm_bwd_epi(acc, t, i, j):
        r, g = t[0], t[1].astype(F32)
        drg, dret = [], []
        for h in range(acc.shape[1] // RET_V):
            sl = slice(h * RET_V, (h + 1) * RET_V)
            xhat, rstd = _norm(r[:, sl])
            gg, dd = g[:, sl], acc[:, sl]
            sg = _sigmoid(gg)
            drg.append(dd * xhat * (sg * (1.0 + gg * (1.0 - sg))))
            dret.append(_norm_bwd(dd * (gg * sg), xhat, rstd))
        return [jnp.concatenate(drg, axis=1), jnp.concatenate(dret, axis=1)], []

    drg, dret = _mm(
        "ret_o_t", dy_ret, wt["w_ret_o"], s, RET_V_WIDTH, d, tm=tm, tn=d, tk=d, tb=True, epi=gate_norm_bwd_epi,
        chunk=MXU_COLS,
        ins=[(ret, *full), (rvg, *_tile(tm, d, 1))],
        outs=[(*bf(RET_V_WIDTH), *full)] * 2)

    drq = _ret_bwd_q(rqk, rvg, dret, cos2, sin2, s)
    drk, drv = _ret_bwd_kv(rqk, rvg, dret, cos2, sin2, s)
    dsq, dsk, dsv, *stacked = _sb_bwd(sb_qkv, sb_out_f32, dsb_out, s, comm=_exchange_plan(LATER_WEIGHTS, grads))
    stacks = dict(zip(LATER_WEIGHTS, stacked, strict=True))

    dh = jnp.concatenate([dsq, dsk, dsv, drq, drk, drv, drg, dgate0, dgate1], axis=1)
    slab = IN_WIDTH // N_CHIPS
    (grads["w_in"],) = _mm("g_in", xb, dh, d, IN_WIDTH, s, tm=d, tn=slab, tk=ts, ta=True, epi=_plain,
                           outs=[((N_CHIPS, d, slab), BF16, (None, d, slab), lambda i, j: (j, 0, 0))])
    grad_x, stacks["w_in"] = _mm(
        "in_t", dh, w_in, s, d, IN_WIDTH, tm=tl, tn=d, tk=IN_WIDTH // N_CHIPS, tb=True,
        epi=lambda acc, t, i, j: ([acc + DN_ALPHA * t[0].astype(F32)], []),
        ins=[(du1b, *_tile(tl, d))], outs=[(*f32(d), *_tile(tl, d))], comm=_exchange_plan(("w_in",), grads))

    small = {"b_gate": jnp.concatenate([dbg0, dbg1], axis=1), "ln1_g": dg1, "ln1_b": db1, "ln2_g": dg2,
             "ln2_b": db2, "ln3_g": dg3, "ln3_b": db3}
    return grad_x, stacks, small, loss_cols


def kernel(x, mem, w_in, b_gate, w_sb_o, w_ret_o, w_mix_o, ln1_g, ln1_b, w_mem_q, w_mem_kv, w_mem_o, ln2_g, ln2_b, w_ffn_in, w_ffn_out, ln3_g, ln3_b, loss_target, m_w_in, m_b_gate, m_w_sb_o, m_w_ret_o, m_w_mix_o, m_ln1_g, m_ln1_b, m_w_mem_q, m_w_mem_kv, m_w_mem_o, m_ln2_g, m_ln2_b, m_w_ffn_in, m_w_ffn_out, m_ln3_g, m_ln3_b, v_w_in, v_b_gate, v_w_sb_o, v_w_ret_o, v_w_mix_o, v_ln1_g, v_ln1_b, v_w_mem_q, v_w_mem_kv, v_w_mem_o, v_ln2_g, v_ln2_b, v_w_ffn_in, v_w_ffn_out, v_ln3_g, v_ln3_b):
    given = dict(locals())
    s = x.shape[1]
    x2d = x.reshape(s, D_MODEL)
    tgt = loss_target.reshape(s, D_MODEL)
    mem2d = mem.reshape(MEM_LEN, D_MODEL)
    shard = {name: given[name].reshape(_shard_shape(shape, axis)) for name, shape, axis in BIG}
    vec = {name: given[name] for name in SMALL}

    shards_bf = {name: _cast_bf16("cast_" + name, shard[name]) for name, _, _ in BIG}

    grad_x, stacks, small, loss_cols = _layer_step(x2d, mem2d, tgt, shards_bf, vec)

    out = {}
    for name, shape, axis in BIG:
        stack = stacks[name]
        shp = given[name].shape
        res = _reduce_adamw("adamw_" + name, stack, shard[name], given["m_" + name].reshape(stack.shape[1:]),
                            given["v_" + name].reshape(stack.shape[1:]))
        out[name] = [r.reshape(shp) for r in res]

    pack = jnp.concatenate([small[name] for name in SMALL] + [loss_cols], axis=1).reshape(PACK_ROWS, LANES)
    cat = lambda pre: jnp.concatenate([given[pre + name] for name in SMALL], axis=1).reshape(SMALL_ROWS, LANES)
    *res, loss = _small_step(pack, cat(""), cat("m_"), cat("v_"))
    flat = [r.reshape(1, SMALL_LEN) for r in res]
    off = 0
    for name in SMALL:
        n = given[name].shape[1]
        out[name] = [r[:, off:off + n] for r in flat]
        off += n

    return (loss.reshape(()), grad_x.reshape(x.shape),
            *[out[name][0] for name in WEIGHT_ORDER], *[out[name][1] for name in WEIGHT_ORDER],
            *[out[name][2] for name in WEIGHT_ORDER], *[out[name][3] for name in WEIGHT_ORDER])
```

```python
import functools

import jax
import jax.numpy as jnp
import numpy as np
from jax import lax
from jax.experimental import pallas as pl
from jax.experimental.pallas import tpu as pltpu

F32, BF16 = jnp.float32, jnp.bfloat16
MESH = pl.DeviceIdType.MESH

D_MODEL = 1024
MEM_LEN = 256
SB_HEADS, SB_DIM, SB_WIDTH = 8, 64, 512
RET_HEADS, RET_QK, RET_V = 4, 128, 256
RET_QK_WIDTH, RET_V_WIDTH = 512, 1024
ROPE_BASE = 10000.0
MEM_HEADS, MEM_DIM = 4, 256
FFN_HIDDEN = 2816
IN_WIDTH = 6656
OFF_RET_Q, OFF_RET_V, OFF_RET_G, OFF_GATE = 1536, 2560, 3584, 4608
DN_ALPHA = 2.0 ** 0.25
LN_EPS = 1e-5
SB_SCALE = SB_DIM ** -0.5
SB_DEAD = -110.0
RET_SCALE = RET_QK ** -0.5
MEM_SCALE = MEM_DIM ** -0.5
ADAM_LR, ADAM_B1, ADAM_B2, ADAM_EPS, ADAM_WD, ADAM_STEP = 0.001, 0.9, 0.999, 1e-08, 0.01, 10

N_DEV, N_CHIPS = 8, 4

LANES = 128
MXU_COLS = 256
VMEM_LIMIT_BYTES = 52 * 2 ** 20
ROW_TILE = 512
WIDE_TILE = 1024
SEQ_TILE = 2048
SB_BLOCK = 256
RET_BLOCK = 256
RET_CHUNKS_PER_STEP = 4
XATTN_ROWS = 1024

BIG = (
    ("w_in", (D_MODEL, IN_WIDTH), 1),
    ("w_sb_o", (SB_WIDTH, D_MODEL), 1),
    ("w_ret_o", (RET_V_WIDTH, D_MODEL), 0),
    ("w_mix_o", (D_MODEL, D_MODEL), 0),
    ("w_mem_q", (D_MODEL, D_MODEL), 0),
    ("w_mem_kv", (D_MODEL, 2 * D_MODEL), 1),
    ("w_mem_o", (D_MODEL, D_MODEL), 0),
    ("w_ffn_in", (D_MODEL, 2 * FFN_HIDDEN), 1),
    ("w_ffn_out", (FFN_HIDDEN, D_MODEL), 0),
)
SMALL = ("b_gate", "ln1_g", "ln1_b", "ln2_g", "ln2_b", "ln3_g", "ln3_b")
SMALL_LEN = 2 * D_MODEL + 6 * D_MODEL
SMALL_ROWS = SMALL_LEN // LANES
PACK_ROWS = SMALL_ROWS + D_MODEL // LANES
WEIGHT_ORDER = ("w_in", "b_gate", "w_sb_o", "w_ret_o", "w_mix_o", "ln1_g", "ln1_b", "w_mem_q", "w_mem_kv",
                "w_mem_o", "ln2_g", "ln2_b", "w_ffn_in", "w_ffn_out", "ln3_g", "ln3_b")


def _cparams():
    return pltpu.CompilerParams(vmem_limit_bytes=VMEM_LIMIT_BYTES)


def _dot(a, b, ca, cb):
    return lax.dot_general(a, b, (((ca,), (cb,)), ((), ())), preferred_element_type=F32)


def _sigmoid(x):
    return 1.0 / (1.0 + jnp.exp(-x))


def _mm(name, a, b, m, n, k, *, tm, tn, tk, epi, outs, ins=(), accs=(), ta=False, tb=False,
        a_off=(0, 0), b_off=(0, 0), j_outer=False, comm=None, chunk=None):
    assert m % tm == 0 and n % tn == 0 and k % tk == 0, (name, m, n, k, tm, tn, tk)
    assert chunk is None or (k == tk and tn % chunk == 0), name
    ni, nj, nk = m // tm, n // tn, k // tk
    assert not accs or nj == 1, name
    ij = (lambda g0, g1: (g1, g0)) if j_outer else (lambda g0, g1: (g0, g1))

    def spec(block, index):
        return pl.BlockSpec(block, lambda g0, g1, kk: index(*ij(g0, g1), kk))

    a_list = list(a) if isinstance(a, (list, tuple)) else [a]
    n_a = len(a_list)
    if n_a > 1:
        assert not ta and nk == 1 and chunk is None and not any(a_off), name
        assert sum(p.shape[1] for p in a_list) == k, name
        a_specs = [spec((tm, p.shape[1]), lambda i, j, kk: (i, 0)) for p in a_list]
    elif ta:
        a_specs = [spec((tk, tm), lambda i, j, kk: (kk + a_off[0], i + a_off[1]))]
    else:
        a_specs = [spec((tm, tk), lambda i, j, kk: (i + a_off[0], kk + a_off[1]))]
    if tb:
        b_spec = spec((tn, tk), lambda i, j, kk: (j + b_off[0], kk + b_off[1]))
    else:
        b_spec = spec((tk, tn), lambda i, j, kk: (kk + b_off[0], j + b_off[1]))
    if n_a > 1 and nj == 1:
        b_spec = pl.BlockSpec(b_spec.block_shape, b_spec.index_map, pipeline_mode=pl.Buffered(1))
    in_specs = [*a_specs, b_spec]
    for _, bs, im in ins:
        in_specs.append(spec(bs, lambda i, j, kk, im=im: im(i, j)))
    out_specs, out_shape = [], []
    for shape, dtype, bs, im in outs:
        out_specs.append(spec(bs, lambda i, j, kk, im=im: im(i, j)))
        out_shape.append(jax.ShapeDtypeStruct(shape, dtype))
    for shape, dtype in accs:
        out_specs.append(spec(shape, lambda i, j, kk, nd=len(shape): (0,) * nd))
        out_shape.append(jax.ShapeDtypeStruct(shape, dtype))
    n_in, n_out, n_acc = len(ins), len(outs), len(accs)
    ca, cb = (0 if ta else 1), (1 if tb else 0)
    grid = (*ij(ni, nj), nk)
    comm_ins, comm_outs, comm_scratch = [], [], []
    if comm is not None:
        comm_in_specs, comm_out_specs = comm.specs
        comm_ins, comm_outs, comm_scratch = list(comm.ins), list(comm.out_shape), list(comm.scratch)
        in_specs += comm_in_specs
        out_specs += comm_out_specs
        out_shape += comm_outs
    n_ci, n_co = len(comm_ins), len(comm_outs)

    def body(*refs):
        a_refs, b_ref, refs = refs[:n_a], refs[n_a], refs[n_a + 1:]
        a_ref = a_refs[0]
        in_refs = refs[:n_in]
        ci_refs = refs[n_in:n_in + n_ci]
        rest = refs[n_in + n_ci:]
        out_refs, acc_refs = rest[:n_out], rest[n_out:n_out + n_acc]
        co_refs = rest[n_out + n_acc:n_out + n_acc + n_co]
        scratch = rest[n_out + n_acc + n_co:]
        sem_refs, scratch = scratch[:len(comm_scratch)], scratch[len(comm_scratch):]
        (i, j), kk = ij(pl.program_id(0), pl.program_id(1)), pl.program_id(2)
        if comm is not None:
            first_step, last_step = _grid_ends(grid)
            pl.when(first_step)(lambda: comm.start(ci_refs, co_refs, sem_refs))
        def finish(acc, cols=slice(None)):
            def of(r):
                return r[..., cols] if r.shape[-1] == tn else r[...]

            o_tiles, a_tiles = epi(acc, [of(r) for r in in_refs], i, j)
            for r, t in zip(out_refs, o_tiles, strict=True):
                r[..., cols] = t.astype(r.dtype)
            if n_acc:
                @pl.when(i == 0)
                def _():
                    for r, t in zip(acc_refs, a_tiles, strict=True):
                        r[..., cols] = t

                @pl.when(i > 0)
                def _():
                    for r, t in zip(acc_refs, a_tiles, strict=True):
                        r[..., cols] += t

        if chunk is not None:
            a_tile = a_ref[...].astype(BF16)
            for c0 in range(0, tn, chunk):
                cols = slice(c0, c0 + chunk)
                b_part = b_ref[cols, :] if tb else b_ref[:, cols]
                finish(_dot(a_tile, b_part.astype(BF16), ca, cb), cols)
            if comm is not None:
                pl.when(last_step)(lambda: comm.finish(ci_refs, co_refs, sem_refs))
            return

        if n_a > 1:
            part, c0 = None, 0
            for r in a_refs:
                c1 = c0 + r.shape[1]
                b_part = b_ref[:, c0:c1] if tb else b_ref[c0:c1, :]
                term = _dot(r[...].astype(BF16), b_part.astype(BF16), ca, cb)
                part, c0 = (term if part is None else part + term), c1
        else:
            part = _dot(a_ref[...].astype(BF16), b_ref[...].astype(BF16), ca, cb)
        if nk == 1:
            finish(part)
        else:
            acc_ref = scratch[0]

            @pl.when(kk == 0)
            def _():
                acc_ref[...] = part

            @pl.when(kk > 0)
            def _():
                acc_ref[...] += part

            @pl.when(kk == nk - 1)
            def _():
                finish(acc_ref[...])

        if comm is not None:
            pl.when(last_step)(lambda: comm.finish(ci_refs, co_refs, sem_refs))

    res = pl.pallas_call(
        body, name=name, grid=grid, in_specs=in_specs, out_specs=out_specs, out_shape=out_shape,
        scratch_shapes=comm_scratch + ([pltpu.VMEM((tm, tn), F32)] if nk > 1 else []),
        compiler_params=_cparams(),
    )(*a_list, b, *[x for x, _, _ in ins], *comm_ins)
    return res


def _grid_ends(grid):
    ids = [pl.program_id(ax) for ax in range(len(grid))]
    first = functools.reduce(jnp.logical_and, [p == 0 for p in ids])
    last = functools.reduce(jnp.logical_and, [p == n - 1 for p, n in zip(ids, grid, strict=True)])
    return first, last


def _tile(tm, tn, dj=0):
    return (tm, tn), (lambda i, j: (i, j + dj))


def _rowvec(tn, dj=0):
    return (1, tn), (lambda i, j: (0, j + dj))


def _plain(acc, tiles, i, j):
    return [acc], []


def _ew(name, fn, ins, outs, rows, tr):
    assert rows % tr == 0, (name, rows, tr)
    in_specs = []
    for x in ins:
        if x.shape[0] == rows:
            in_specs.append(pl.BlockSpec((tr, x.shape[1]), lambda i: (i, 0)))
        else:
            in_specs.append(pl.BlockSpec(x.shape, lambda i: (0, 0)))
    n_in = len(ins)

    def body(*refs):
        res = fn(*[r[...] for r in refs[:n_in]])
        for r, t in zip(refs[n_in:], res, strict=True):
            r[...] = t.astype(r.dtype)

    return pl.pallas_call(
        body, name=name, grid=(rows // tr,), in_specs=in_specs,
        out_specs=[pl.BlockSpec((tr, w), lambda i: (i, 0)) for w, _ in outs],
        out_shape=[jax.ShapeDtypeStruct((rows, w), dt) for w, dt in outs],
        compiler_params=_cparams(),
    )(*ins)


def _cast_bf16(name, x):
    rows = x.shape[0]
    tr = next(t for t in (512, 256, 64) if rows % t == 0)
    return _ew(name, lambda v: (v,), [x], [(x.shape[1], BF16)], rows, tr)[0]


def _prep(x, comm):
    s = x.shape[0]
    half = RET_QK // 2
    inv = 1.0 / (ROPE_BASE ** (jnp.arange(half, dtype=F32) / half))
    inv2 = jnp.concatenate([inv, inv]).reshape(1, RET_QK)
    sign = jnp.concatenate([-jnp.ones((half,), F32), jnp.ones((half,), F32)]).reshape(1, RET_QK)
    tr = min(ROW_TILE, s)
    grid = (s // tr,)
    c_in_specs, c_out_specs, c_out_shape, c_scratch, c_ins, split = _host(comm, 3, 3)

    def body(*refs):
        (x_ref, inv_ref, sign_ref), (xb_ref, cos_ref, sin_ref), _, riding = split(refs)
        i = pl.program_id(0)
        first_step, last_step = _grid_ends(grid)
        pl.when(first_step)(lambda: comm.start(*riding))
        xb_ref[...] = x_ref[...].astype(BF16)
        pos = (lax.broadcasted_iota(jnp.int32, (tr, RET_QK), 0) + i * tr).astype(F32)
        ang = pos * inv_ref[...]
        cos_ref[...] = jnp.cos(ang)
        sin_ref[...] = jnp.sin(ang) * sign_ref[...]
        pl.when(last_step)(lambda: comm.finish(*riding))

    vec = pl.BlockSpec((1, RET_QK), lambda i: (0, 0))
    row = lambda w: pl.BlockSpec((tr, w), lambda i: (i, 0))
    return pl.pallas_call(
        body, name="prep", grid=grid,
        in_specs=[row(D_MODEL), vec, vec] + c_in_specs,
        out_specs=[row(D_MODEL), row(RET_QK), row(RET_QK)] + c_out_specs,
        out_shape=[jax.ShapeDtypeStruct((s, D_MODEL), BF16), jax.ShapeDtypeStruct((s, RET_QK), F32),
                   jax.ShapeDtypeStruct((s, RET_QK), F32)] + c_out_shape,
        scratch_shapes=c_scratch, compiler_params=_cparams(),
    )(x, inv2, sign, *c_ins)


def _swap_halves(x):
    return pltpu.roll(x, RET_QK // 2, 1)


def _norm(u):
    mu = jnp.mean(u, axis=-1, keepdims=True)
    d = u - mu
    var = jnp.mean(d * d, axis=-1, keepdims=True)
    rstd = lax.rsqrt(var + LN_EPS)
    return d * rstd, rstd


def _norm_bwd(dxh, xhat, rstd):
    m1 = jnp.mean(dxh, axis=-1, keepdims=True)
    m2 = jnp.mean(dxh * xhat, axis=-1, keepdims=True)
    return rstd * (dxh - m1 - xhat * m2)


def _colsum(t):
    return jnp.sum(t, axis=0, keepdims=True)


def _split_mm(t, tri):
    hi = t.astype(BF16)
    lo = (t - hi.astype(F32)).astype(BF16)
    return _dot(hi, tri, 1, 0) + _dot(lo, tri, 1, 0)


def _sb_masks():
    t = SB_BLOCK
    lane = lax.broadcasted_iota(jnp.int32, (1, LANES), 1)
    first = lane < SB_DIM
    m0 = jnp.where(first, 1.0, 0.0).astype(BF16)
    m1 = jnp.where(first, 0.0, 1.0).astype(BF16)
    row = lax.broadcasted_iota(jnp.int32, (t, t), 0)
    col = lax.broadcasted_iota(jnp.int32, (t, t), 1)
    return first, (m0, m1), row, col


def _sb_logits(qh, k, causal):
    z = _dot(qh, k, 1, 1)
    lp = jnp.log(1.0 + jnp.exp(-jnp.abs(z)))
    a = jnp.minimum(z, 0.0) - lp
    r = jnp.minimum(-z, 0.0) - lp
    if causal is not None:
        r = jnp.where(causal, r, 0.0)
    return a, r


def _sb_walk(i, blocks, l_ref, causal):
    pl.when(i == 0)(lambda: blocks([(i, causal)]))
    pl.when(i > 0)(lambda: blocks([(i, causal), (i - 1, None)]))

    def alive():
        top = jnp.max(functools.reduce(jnp.maximum, [l_ref[c] for c in range(l_ref.shape[0])]))
        return jnp.where(top > SB_DEAD, 1, 0)

    def cond(c):
        return jnp.logical_and(c[0] < i, c[1] > 0)

    def step(c):
        blocks([(i - 1 - c[0], None)])
        return c[0] + 1, alive()

    lax.while_loop(cond, step, (jnp.int32(1), alive()))


def _host(comm, n_in, n_out):
    if comm is None:
        return [], [], [], [], [], lambda refs: (refs[:n_in], refs[n_in:n_in + n_out], refs[n_in + n_out:], None)
    in_specs, out_specs = comm.specs
    n_ci, n_co, n_sem = len(comm.ins), len(comm.out_shape), len(comm.scratch)

    def split(refs):
        ins, ci = refs[:n_in], refs[n_in:n_in + n_ci]
        rest = refs[n_in + n_ci:]
        outs, co = rest[:n_out], rest[n_out:n_out + n_co]
        sems, scratch = rest[n_out + n_co:n_out + n_co + n_sem], rest[n_out + n_co + n_sem:]
        return ins, outs, scratch, (ci, co, sems)

    return in_specs, out_specs, list(comm.out_shape), list(comm.scratch), list(comm.ins), split


def _sb_qkv_specs(s, g):
    groups = SB_HEADS // 2 // g
    return [pl.BlockSpec((g, SB_BLOCK, LANES), lambda p, i: (p, i, 0)),
            pl.BlockSpec((g, s, LANES), lambda p, i: (groups + p, 0, 0)),
            pl.BlockSpec((g, s, LANES), lambda p, i: (2 * groups + p, 0, 0))]


def _sb_fwd(qkv, s, comm=None):
    t = SB_BLOCK
    g = 2
    nq = s // t
    grid = (SB_HEADS // 2 // g, nq)
    c_in_specs, c_out_specs, c_out_shape, c_scratch, c_ins, split = _host(comm, 3, 2)

    def body(*refs):
        (q_ref, k_ref, v_ref), (o_ref, of_ref), (l_ref, acc_ref), riding = split(refs)
        i = pl.program_id(1)
        if comm is not None:
            first_step, last_step = _grid_ends(grid)
            pl.when(first_step)(lambda: comm.start(*riding))
        first, hmask, row, col = _sb_masks()
        after = jnp.where(row > col, 1.0, 0.0).astype(BF16)
        causal = col < row
        heads = [(p, h) for p in range(g) for h in range(2)]
        qh = {(p, h): q_ref[p] * hmask[h] for p, h in heads}
        l_ref[...] = jnp.zeros_like(l_ref)
        acc_ref[...] = jnp.zeros_like(acc_ref)

        def blocks(todo):
            chains = [(b, p, h) for b in range(len(todo)) for p, h in heads]
            starts = [pl.multiple_of(kb * t, t) for kb, _ in todo]
            ks = {(b, p): k_ref[p, pl.ds(st, t), :] for b, st in enumerate(starts) for p in range(g)}
            vs = {(b, p): v_ref[p, pl.ds(st, t), :] for b, st in enumerate(starts) for p in range(g)}
            ar = {(b, p, h): _sb_logits(qh[p, h], ks[b, p], todo[b][1]) for b, p, h in chains}
            later = {c: _split_mm(ar[c][1], after) for c in chains}
            carry = {(p, h): l_ref[2 * p + h] for p, h in heads}
            w = {}
            for b, (_, mask) in enumerate(todo):
                for p, h in heads:
                    wc = jnp.exp(ar[b, p, h][0] + later[b, p, h] + carry[p, h])
                    w[b, p, h] = wc if mask is None else jnp.where(mask, wc, 0.0)
                carry = {(p, h): carry[p, h] + jnp.sum(ar[b, p, h][1], axis=1, keepdims=True) for p, h in heads}
            pv = {(b, p, h): _dot(w[b, p, h].astype(BF16), vs[b, p], 1, 0) for b, p, h in chains}
            for p in range(g):
                lanes = slice(p * LANES, (p + 1) * LANES)
                acc = acc_ref[:, lanes]
                for b in range(len(todo)):
                    acc = acc + jnp.where(first, pv[b, p, 0], pv[b, p, 1])
                acc_ref[:, lanes] = acc
            for p, h in heads:
                l_ref[2 * p + h] = carry[p, h]

        _sb_walk(i, blocks, l_ref, causal)
        o_ref[...] = acc_ref[...].astype(o_ref.dtype)
        of_ref[...] = acc_ref[...]
        if comm is not None:
            pl.when(last_step)(lambda: comm.finish(*riding))

    blk = pl.BlockSpec((t, g * LANES), lambda p, i: (i, p))
    return pl.pallas_call(
        body, name="sb_fwd", grid=grid,
        in_specs=_sb_qkv_specs(s, g) + c_in_specs,
        out_specs=[blk, blk] + c_out_specs,
        out_shape=[jax.ShapeDtypeStruct((s, SB_WIDTH), BF16), jax.ShapeDtypeStruct((s, SB_WIDTH), F32)] + c_out_shape,
        scratch_shapes=c_scratch + [pltpu.VMEM((2 * g, t, 1), F32), pltpu.VMEM((t, g * LANES), F32)],
        compiler_params=_cparams(),
    )(qkv, qkv, qkv, *c_ins)


def _sb_bwd(qkv, o, do, s, comm=None):
    t = SB_BLOCK
    g = 2
    nq = s // t
    grid = (SB_HEADS // 2 // g, nq)
    c_in_specs, c_out_specs, c_out_shape, c_scratch, c_ins, split = _host(comm, 5, 3)

    def body(*refs):
        ((q_ref, k_ref, v_ref, o_ref, do_ref), (dq_ref, dk_ref, dv_ref),
         (l_ref, e_ref, dq_acc, dk_acc, dv_acc), riding) = split(refs)
        i = pl.program_id(1)
        if comm is not None:
            first_step, last_step = _grid_ends(grid)
            pl.when(first_step)(lambda: comm.start(*riding))
        first, hmask, row, col = _sb_masks()
        after = jnp.where(row > col, 1.0, 0.0).astype(BF16)
        from_here = jnp.where(row >= col, 1.0, 0.0).astype(BF16)
        causal = col < row

        @pl.when(i == 0)
        def _():
            dk_acc[...] = jnp.zeros_like(dk_acc)
            dv_acc[...] = jnp.zeros_like(dv_acc)

        heads = [(p, h) for p in range(g) for h in range(2)]
        lanes = [slice(p * LANES, (p + 1) * LANES) for p in range(g)]
        q = [q_ref[p] for p in range(g)]
        do_ = [do_ref[:, lanes[p]] for p in range(g)]
        qh = {(p, h): q[p] * hmask[h] for p, h in heads}
        doh = {(p, h): do_[p] * hmask[h] for p, h in heads}
        total = {}
        for p in range(g):
            prod = do_[p].astype(F32) * o_ref[:, lanes[p]]
            total[p, 0] = jnp.sum(jnp.where(first, prod, 0.0), axis=1, keepdims=True)
            total[p, 1] = jnp.sum(jnp.where(first, 0.0, prod), axis=1, keepdims=True)
        l_ref[...] = jnp.zeros_like(l_ref)
        e_ref[...] = jnp.zeros_like(e_ref)
        dq_acc[...] = jnp.zeros_like(dq_acc)

        def blocks(todo):
            chains = [(b, p, h) for b in range(len(todo)) for p, h in heads]
            starts = [pl.multiple_of(kb * t, t) for kb, _ in todo]
            ks = {(b, p): k_ref[p, pl.ds(st, t), :] for b, st in enumerate(starts) for p in range(g)}
            vs = {(b, p): v_ref[p, pl.ds(st, t), :] for b, st in enumerate(starts) for p in range(g)}
            ar = {(b, p, h): _sb_logits(qh[p, h], ks[b, p], todo[b][1]) for b, p, h in chains}
            dw = {(b, p, h): _dot(doh[p, h], vs[b, p], 1, 1) for b, p, h in chains}
            later = {c: _split_mm(ar[c][1], after) for c in chains}
            carry = {(p, h): l_ref[2 * p + h] for p, h in heads}
            wb = {}
            for b, (_, mask) in enumerate(todo):
                for p, h in heads:
                    wc = jnp.exp(ar[b, p, h][0] + later[b, p, h] + carry[p, h])
                    wb[b, p, h] = (wc if mask is None else jnp.where(mask, wc, 0.0)).astype(BF16)
                carry = {(p, h): carry[p, h] + jnp.sum(ar[b, p, h][1], axis=1, keepdims=True) for p, h in heads}
            dvs = {(b, p, h): _dot(wb[b, p, h], do_[p], 0, 0) for b, p, h in chains}
            e = {c: dw[c] * wb[c].astype(F32) for c in chains}
            suffix = {c: _split_mm(e[c], from_here) for c in chains}
            e_carry = {(p, h): e_ref[2 * p + h] for p, h in heads}
            dz = {}
            for b, (_, mask) in enumerate(todo):
                for p, h in heads:
                    before = total[p, h] - (suffix[b, p, h] + e_carry[p, h])
                    dzc = e[b, p, h] - jnp.exp(ar[b, p, h][0]) * (e[b, p, h] + before)
                    dz[b, p, h] = (dzc if mask is None else jnp.where(mask, dzc, 0.0)).astype(BF16)
                e_carry = {(p, h): e_carry[p, h] + jnp.sum(e[b, p, h], axis=1, keepdims=True) for p, h in heads}
            dqs = {(b, p, h): _dot(dz[b, p, h], ks[b, p], 1, 0) for b, p, h in chains}
            dks = {(b, p, h): _dot(dz[b, p, h], q[p], 0, 0) for b, p, h in chains}
            for p in range(g):
                dq = dq_acc[:, lanes[p]]
                for b, st in enumerate(starts):
                    dq = dq + jnp.where(first, dqs[b, p, 0], dqs[b, p, 1])
                    dk_acc[pl.ds(st, t), lanes[p]] += jnp.where(first, dks[b, p, 0], dks[b, p, 1])
                    dv_acc[pl.ds(st, t), lanes[p]] += jnp.where(first, dvs[b, p, 0], dvs[b, p, 1])
                dq_acc[:, lanes[p]] = dq
            for p, h in heads:
                l_ref[2 * p + h] = carry[p, h]
                e_ref[2 * p + h] = e_carry[p, h]

        _sb_walk(i, blocks, l_ref, causal)
        dq_ref[...] = (dq_acc[...] * SB_SCALE).astype(dq_ref.dtype)

        @pl.when(i == nq - 1)
        def _():
            dk_ref[...] = dk_acc[...].astype(dk_ref.dtype)
            dv_ref[...] = dv_acc[...].astype(dv_ref.dtype)

        if comm is not None:
            pl.when(last_step)(lambda: comm.finish(*riding))

    once = pl.Buffered(1)
    q_spec, k_spec, v_spec = _sb_qkv_specs(s, g)
    k_spec = pl.BlockSpec(k_spec.block_shape, k_spec.index_map, pipeline_mode=once)
    v_spec = pl.BlockSpec(v_spec.block_shape, v_spec.index_map, pipeline_mode=once)
    blk = pl.BlockSpec((t, g * LANES), lambda p, i: (i, p))
    col_blk = pl.BlockSpec((s, g * LANES), lambda p, i: (0, p), pipeline_mode=once)
    sds = jax.ShapeDtypeStruct((s, SB_WIDTH), BF16)
    return pl.pallas_call(
        body, name="sb_bwd", grid=grid,
        in_specs=[q_spec, k_spec, v_spec, blk, blk] + c_in_specs,
        out_specs=[blk, col_blk, col_blk] + c_out_specs,
        out_shape=[sds, sds, sds] + c_out_shape,
        scratch_shapes=c_scratch + [pltpu.VMEM((2 * g, t, 1), F32), pltpu.VMEM((2 * g, t, 1), F32),
                                    pltpu.VMEM((t, g * LANES), F32), pltpu.VMEM((s, g * LANES), F32),
                                    pltpu.VMEM((s, g * LANES), F32)],
        compiler_params=_cparams(),
    )(qkv, qkv, qkv, o, do, *c_ins)


def _ret_log_gamma():
    lg = np.log1p(-np.exp2(-5.0 - np.arange(RET_HEADS, dtype=np.float32))).astype(np.float32)
    return jnp.asarray(np.broadcast_to(lg[:, None, None], (RET_HEADS, 8, LANES)).copy())


RET_SCRATCH = [pltpu.VMEM((RET_HEADS, RET_QK, RET_V), F32),
               pltpu.VMEM((RET_HEADS, RET_BLOCK, RET_BLOCK), F32),
               pltpu.VMEM((RET_HEADS, RET_BLOCK, 1), F32),
               pltpu.VMEM((RET_HEADS, RET_BLOCK, 1), F32)]


def _ret_begin(n, lg_ref, state, within, q_dec, k_dec):
    @pl.when(n == 0)
    def _():
        c = RET_BLOCK
        state[...] = jnp.zeros_like(state)
        row = lax.broadcasted_iota(jnp.int32, (c, c), 0)
        col = lax.broadcasted_iota(jnp.int32, (c, c), 1)
        rel = jnp.maximum(row - col, 0).astype(F32)
        idx = lax.broadcasted_iota(jnp.int32, (c, 1), 0).astype(F32)
        for h in range(RET_HEADS):
            lg = lg_ref[h, 0:1, 0:1]
            within[h] = jnp.where(row >= col, jnp.exp(lg * rel), 0.0)
            q_dec[h] = jnp.exp(lg * (idx + 1.0))
            k_dec[h] = jnp.exp(lg * (c - 1.0 - idx))


def _chunk_decay(lg_ref, h):
    return jnp.exp(lg_ref[h, 0:1, 0:1] * float(RET_BLOCK))


def _ret_heads(x, width):
    return [x[:, h * width:(h + 1) * width] for h in range(RET_HEADS)]


def _ret_specs(s, reverse=False):
    c = RET_BLOCK
    per_step = min(RET_CHUNKS_PER_STEP, s // c)
    rows = c * per_step
    nc = s // rows
    pos = (lambda n: nc - 1 - n) if reverse else (lambda n: n)
    chunks = [slice(u * c, (u + 1) * c) for u in range(per_step)]
    q_spec = pl.BlockSpec((rows, RET_QK_WIDTH), lambda n: (pos(n), 0))
    k_spec = pl.BlockSpec((rows, RET_QK_WIDTH), lambda n: (pos(n), 1))
    v_spec = pl.BlockSpec((rows, RET_V_WIDTH), lambda n: (pos(n), 0))
    lg_spec = pl.BlockSpec((RET_HEADS, 8, LANES), lambda n: (0, 0, 0))
    rope_spec = pl.BlockSpec((rows, RET_QK), lambda n: (pos(n), 0))
    return nc, chunks[::-1] if reverse else chunks, q_spec, k_spec, v_spec, lg_spec, rope_spec


def _ret_fwd(rqk, rvg, s):
    nc, chunks, q_spec, k_spec, v_spec, lg_spec, _ = _ret_specs(s)
    g_spec = pl.BlockSpec(v_spec.block_shape, lambda n: (n, 1))
    heads = range(RET_HEADS)

    def body(q_ref, k_ref, v_ref, g_ref, lg_ref, r_ref, y_ref, state, within, q_dec, k_dec):
        n = pl.program_id(0)
        _ret_begin(n, lg_ref, state, within, q_dec, k_dec)
        for rows in chunks:
            q, k = _ret_heads(q_ref[rows], RET_QK), _ret_heads(k_ref[rows], RET_QK)
            v, g = _ret_heads(v_ref[rows], RET_V), _ret_heads(g_ref[rows], RET_V)
            scores = [_dot(q[h].astype(BF16), k[h].astype(BF16), 1, 1) * within[h] for h in heads]
            cross = [_dot((q[h] * q_dec[h]).astype(BF16), state[h].astype(BF16), 1, 0) for h in heads]
            out = [_dot(scores[h].astype(BF16), v[h], 1, 0) + cross[h] for h in heads]
            grown = [_dot((k[h] * k_dec[h]).astype(BF16), v[h], 0, 0) for h in heads]
            for h in heads:
                sl = slice(h * RET_V, (h + 1) * RET_V)
                r_ref[rows, sl] = out[h]
                xhat, _ = _norm(out[h])
                gh = g[h].astype(F32)
                y_ref[rows, sl] = (gh * _sigmoid(gh) * xhat).astype(y_ref.dtype)
                state[h] = state[h] * _chunk_decay(lg_ref, h) + grown[h]

    return pl.pallas_call(
        body, name="ret_fwd", grid=(nc,),
        in_specs=[q_spec, k_spec, v_spec, g_spec, lg_spec],
        out_specs=[v_spec, v_spec],
        out_shape=[jax.ShapeDtypeStruct((s, RET_V_WIDTH), F32), jax.ShapeDtypeStruct((s, RET_V_WIDTH), BF16)],
        scratch_shapes=RET_SCRATCH,
        compiler_params=_cparams(),
    )(rqk, rqk, rvg, rvg, _ret_log_gamma())


def _rope_bwd(d, cos, sin):
    return d * cos + _swap_halves(d * sin)


def _ret_bwd_q(rqk, rv, d_out, cos2, sin2, s):
    nc, chunks, q_spec, k_spec, v_spec, lg_spec, rope_spec = _ret_specs(s)
    heads = range(RET_HEADS)

    def body(k_ref, v_ref, d_ref, lg_ref, cos_ref, sin_ref, dq_ref, state, within, q_dec, k_dec):
        n = pl.program_id(0)
        _ret_begin(n, lg_ref, state, within, q_dec, k_dec)
        for rows in chunks:
            k = _ret_heads(k_ref[rows], RET_QK)
            v, d = _ret_heads(v_ref[rows], RET_V), _ret_heads(d_ref[rows], RET_V)
            cos, sin = cos_ref[rows], sin_ref[rows]
            d_scores = [_dot(d[h], v[h], 1, 1) * within[h] for h in heads]
            cross = [q_dec[h] * _dot(d[h], state[h].astype(BF16), 1, 1) for h in heads]
            dq = [_dot(d_scores[h].astype(BF16), k[h].astype(BF16), 1, 0) + cross[h] for h in heads]
            grown = [_dot((k[h] * k_dec[h]).astype(BF16), v[h], 0, 0) for h in heads]
            for h in heads:
                sl = slice(h * RET_QK, (h + 1) * RET_QK)
                dq_ref[rows, sl] = (_rope_bwd(dq[h], cos, sin) * RET_SCALE).astype(dq_ref.dtype)
                state[h] = state[h] * _chunk_decay(lg_ref, h) + grown[h]

    return pl.pallas_call(
        body, name="ret_bwd_q", grid=(nc,),
        in_specs=[k_spec, v_spec, v_spec, lg_spec, rope_spec, rope_spec],
        out_specs=q_spec,
        out_shape=jax.ShapeDtypeStruct((s, RET_QK_WIDTH), BF16),
        scratch_shapes=RET_SCRATCH,
        compiler_params=_cparams(),
    )(rqk, rv, d_out, _ret_log_gamma(), cos2, sin2)


def _ret_bwd_kv(rqk, rv, d_out, cos2, sin2, s):
    nc, chunks, q_spec, k_spec, v_spec, lg_spec, rope_spec = _ret_specs(s, reverse=True)
    heads = range(RET_HEADS)

    def body(q_ref, k_ref, v_ref, d_ref, lg_ref, cos_ref, sin_ref, dk_ref, dv_ref, state, within, q_dec, k_dec):
        n = pl.program_id(0)
        _ret_begin(n, lg_ref, state, within, q_dec, k_dec)
        for rows in chunks:
            q, k = _ret_heads(q_ref[rows], RET_QK), _ret_heads(k_ref[rows], RET_QK)
            v, d = _ret_heads(v_ref[rows], RET_V), _ret_heads(d_ref[rows], RET_V)
            cos, sin = cos_ref[rows], sin_ref[rows]
            qb, kb = [q[h].astype(BF16) for h in heads], [k[h].astype(BF16) for h in heads]
            st = [state[h].astype(BF16) for h in heads]
            scores = [_dot(qb[h], kb[h], 1, 1) * within[h] for h in heads]
            d_scores = [_dot(d[h], v[h], 1, 1) * within[h] for h in heads]
            dk = [_dot(d_scores[h].astype(BF16), qb[h], 0, 0) + k_dec[h] * _dot(v[h], st[h], 1, 1) for h in heads]
            dv = [_dot(scores[h].astype(BF16), d[h], 0, 0) + k_dec[h] * _dot(kb[h], st[h], 1, 0) for h in heads]
            grown = [_dot((q[h] * q_dec[h]).astype(BF16), d[h], 0, 0) for h in heads]
            for h in heads:
                dk_ref[rows, h * RET_QK:(h + 1) * RET_QK] = _rope_bwd(dk[h], cos, sin).astype(dk_ref.dtype)
                dv_ref[rows, h * RET_V:(h + 1) * RET_V] = dv[h].astype(dv_ref.dtype)
                state[h] = state[h] * _chunk_decay(lg_ref, h) + grown[h]

    return pl.pallas_call(
        body, name="ret_bwd_kv", grid=(nc,),
        in_specs=[q_spec, k_spec, v_spec, v_spec, lg_spec, rope_spec, rope_spec],
        out_specs=[q_spec, v_spec],
        out_shape=[jax.ShapeDtypeStruct((s, RET_QK_WIDTH), BF16), jax.ShapeDtypeStruct((s, RET_V_WIDTH), BF16)],
        scratch_shapes=RET_SCRATCH,
        compiler_params=_cparams(),
    )(rqk, rqk, rv, d_out, _ret_log_gamma(), cos2, sin2)


def _xattn_probs(scores):
    sc = scores - jnp.max(scores, axis=-1, keepdims=True)
    p = jnp.exp(sc)
    return p / jnp.sum(p, axis=-1, keepdims=True)


def _xattn_heads(q_ref, kv_ref):
    sls = [slice(h * MEM_DIM, (h + 1) * MEM_DIM) for h in range(MEM_HEADS)]
    q = [q_ref[:, sl] for sl in sls]
    k = [kv_ref[:, sl] for sl in sls]
    v = [kv_ref[:, D_MODEL + h * MEM_DIM:D_MODEL + (h + 1) * MEM_DIM] for h in range(MEM_HEADS)]
    return sls, q, k, v


def _xattn_fwd(qm, kv, s):
    tq = min(XATTN_ROWS, s)
    heads = range(MEM_HEADS)

    def body(q_ref, kv_ref, o_ref):
        sls, q, k, v = _xattn_heads(q_ref, kv_ref)
        scores = [_dot(q[h], k[h], 1, 1) for h in heads]
        p = [_xattn_probs(scores[h]).astype(BF16) for h in heads]
        out = [_dot(p[h], v[h], 1, 0) for h in heads]
        for h in heads:
            o_ref[:, sls[h]] = out[h].astype(o_ref.dtype)

    return pl.pallas_call(
        body, name="xattn_fwd", grid=(s // tq,),
        in_specs=[pl.BlockSpec((tq, D_MODEL), lambda i: (i, 0)),
                  pl.BlockSpec((MEM_LEN, 2 * D_MODEL), lambda i: (0, 0))],
        out_specs=pl.BlockSpec((tq, D_MODEL), lambda i: (i, 0)),
        out_shape=jax.ShapeDtypeStruct((s, D_MODEL), BF16),
        compiler_params=_cparams(),
    )(qm, kv)


def _xattn_bwd(qm, kv, do, s):
    tq = min(XATTN_ROWS, s)

    def body(q_ref, kv_ref, do_ref, dq_ref, dkv_ref):
        i = pl.program_id(0)

        @pl.when(i == 0)
        def _():
            dkv_ref[...] = jnp.zeros_like(dkv_ref)

        heads = range(MEM_HEADS)
        sls, q, k, v = _xattn_heads(q_ref, kv_ref)
        d = [do_ref[:, sl] for sl in sls]
        scores = [_dot(q[h], k[h], 1, 1) for h in heads]
        dp = [_dot(d[h], v[h], 1, 1) for h in heads]
        p = [_xattn_probs(scores[h]) for h in heads]
        ds = [(p[h] * (dp[h] - jnp.sum(p[h] * dp[h], axis=-1, keepdims=True))).astype(BF16) for h in heads]
        dq = [_dot(ds[h], k[h], 1, 0) for h in heads]
        dk = [_dot(ds[h], q[h], 0, 0) for h in heads]
        dv = [_dot(p[h].astype(BF16), d[h], 0, 0) for h in heads]
        for h in heads:
            dq_ref[:, sls[h]] = (dq[h] * MEM_SCALE).astype(dq_ref.dtype)
            dkv_ref[:, sls[h]] += dk[h]
            dkv_ref[:, D_MODEL + h * MEM_DIM:D_MODEL + (h + 1) * MEM_DIM] += dv[h]

    row_blk = pl.BlockSpec((tq, D_MODEL), lambda i: (i, 0))
    kv_blk = pl.BlockSpec((MEM_LEN, 2 * D_MODEL), lambda i: (0, 0))
    return pl.pallas_call(
        body, name="xattn_bwd", grid=(s // tq,),
        in_specs=[row_blk, kv_blk, row_blk],
        out_specs=[row_blk, kv_blk],
        out_shape=[jax.ShapeDtypeStruct((s, D_MODEL), BF16), jax.ShapeDtypeStruct((MEM_LEN, 2 * D_MODEL), F32)],
        compiler_params=_cparams(),
    )(qm, kv, do)


def _place():
    x, y, c = lax.axis_index("x"), lax.axis_index("y"), lax.axis_index("c")
    others = [(1 - x, y), (x, 1 - y), (1 - x, 1 - y)]
    return x, y, c, others


def _slab(ref, axis, chip, size):
    start = pl.multiple_of(chip * size, LANES if axis == 1 else 16)
    if axis == 0:
        return ref.at[pl.ds(start, size), :]
    return ref.at[:, pl.ds(start, size)]


class _CommPlan:
    def __init__(self, ins, out_shape, scratch, start, finish):
        self.ins, self.out_shape, self.scratch, self.start, self.finish = ins, out_shape, scratch, start, finish

    @property
    def specs(self):
        any_spec = pl.BlockSpec(memory_space=pl.ANY)
        return [any_spec] * len(self.ins), [any_spec] * len(self.out_shape)


def _gather_plan(names, shards):
    spec = {name: (shape, axis) for name, shape, axis in BIG}
    nw = len(names)

    def shard_half(ref, c):
        rows = ref.shape[0] // 2
        return ref.at[pl.ds(pl.multiple_of(c * rows, 16), rows), :]

    def region(ref, w, chip, c):
        shape, axis = spec[names[w]]
        size = shape[axis] // N_CHIPS
        if axis == 0:
            rows = size // 2
            return ref.at[pl.ds(pl.multiple_of(chip * size + c * rows, 16), rows), :]
        rows = shape[0] // 2
        return ref.at[pl.ds(pl.multiple_of(c * rows, 16), rows), pl.ds(pl.multiple_of(chip * size, LANES), size)]

    def ops(shard, full, sems):
        ici_send, ici_recv, d2d_send, d2d_recv, local_sems = sems
        x, y, c, others = _place()
        mine, sibling = 2 * x + y, (x, y, 1 - c)
        local, over_ici, arrived, passed_on, from_sibling = [], [], [], [], []
        for w in range(nw):
            shape, axis = spec[names[w]]
            local.append(pltpu.make_async_copy(shard[w], _slab(full[w], axis, mine, shape[axis] // N_CHIPS),
                                               local_sems.at[w]))
            for t, (qx, qy) in enumerate(others):
                n, theirs = 3 * w + t, 2 * qx + qy
                over_ici.append(pltpu.make_async_remote_copy(
                    src_ref=shard_half(shard[w], c), dst_ref=region(full[w], w, mine, c),
                    send_sem=ici_send.at[n], recv_sem=ici_recv.at[n], device_id=(qx, qy, c), device_id_type=MESH))
                arrived.append(pltpu.make_async_remote_copy(
                    src_ref=shard_half(shard[w], c), dst_ref=region(full[w], w, theirs, c),
                    send_sem=ici_send.at[n], recv_sem=ici_recv.at[n], device_id=(qx, qy, c), device_id_type=MESH))
                passed_on.append(pltpu.make_async_remote_copy(
                    src_ref=region(full[w], w, theirs, c), dst_ref=region(full[w], w, theirs, c),
                    send_sem=d2d_send.at[n], recv_sem=d2d_recv.at[n], device_id=sibling, device_id_type=MESH))
                from_sibling.append(pltpu.make_async_remote_copy(
                    src_ref=region(full[w], w, theirs, c), dst_ref=region(full[w], w, theirs, 1 - c),
                    send_sem=d2d_send.at[n], recv_sem=d2d_recv.at[n], device_id=sibling, device_id_type=MESH))
        return local, over_ici, arrived, passed_on, from_sibling

    def start(shard, full, sems):
        local, over_ici, _, _, _ = ops(shard, full, sems)
        for cp in local + over_ici:
            cp.start()

    def finish(shard, full, sems):
        local, over_ici, arrived, passed_on, from_sibling = ops(shard, full, sems)
        for got, onward in zip(arrived, passed_on, strict=True):
            got.wait_recv()
            onward.start()
        for got in from_sibling:
            got.wait_recv()
        for cp in over_ici + passed_on:
            cp.wait_send()
        for cp in local:
            cp.wait()

    dma = pltpu.SemaphoreType.DMA
    return _CommPlan(
        ins=[shards[name] for name in names],
        out_shape=[jax.ShapeDtypeStruct(spec[name][0], BF16) for name in names],
        scratch=[dma((3 * nw,)), dma((3 * nw,)), dma((3 * nw,)), dma((3 * nw,)), dma((nw,))],
        start=start, finish=finish)


def _shard_shape(shape, axis):
    return tuple(d // N_CHIPS if a == axis else d for a, d in enumerate(shape))


def _exchange_plan(names, grads):
    spec = {name: (shape, axis) for name, shape, axis in BIG}
    nw = len(names)

    def ops(grad, stack, sems):
        send_sems, recv_sems, local_sems = sems
        x, y, c, others = _place()
        mine = 2 * x + y
        me, sibling = (x, y, c), (x, y, 1 - c)

        def dev(px, py, pc):
            return 4 * px + 2 * py + pc

        def copy(w, n, src, slot, to):
            return pltpu.make_async_remote_copy(
                src_ref=src, dst_ref=stack[w].at[slot], send_sem=send_sems.at[7 * w + n],
                recv_sem=recv_sems.at[7 * w + n], device_id=to, device_id_type=MESH)

        local, first, arrived, passed_on, from_sibling = [], [], [], [], []
        for w in range(nw):
            shape, axis = spec[names[w]]
            size = shape[axis] // N_CHIPS
            own = _slab(grad[w], axis, mine, size)
            local.append(pltpu.make_async_copy(own, stack[w].at[dev(*me)], local_sems.at[w]))
            first.append(copy(w, 0, own, dev(*me), sibling))
            from_sibling.append(copy(w, 0, own, dev(*sibling), me))
            for t, (qx, qy) in enumerate(others):
                got = stack[w].at[dev(qx, qy, c)]
                first.append(copy(w, 1 + t, _slab(grad[w], axis, 2 * qx + qy, size), dev(*me), (qx, qy, c)))
                arrived.append(copy(w, 1 + t, got, dev(qx, qy, c), me))
                passed_on.append(copy(w, 4 + t, got, dev(qx, qy, c), sibling))
                from_sibling.append(copy(w, 4 + t, got, dev(qx, qy, 1 - c), me))
        return local, first, arrived, passed_on, from_sibling

    def start(grad, stack, sems):
        local, first, _, _, _ = ops(grad, stack, sems)
        for cp in local + first:
            cp.start()

    def finish(grad, stack, sems):
        local, first, arrived, passed_on, from_sibling = ops(grad, stack, sems)
        for got, onward in zip(arrived, passed_on, strict=True):
            got.wait_recv()
            onward.start()
        for got in from_sibling:
            got.wait_recv()
        for cp in first + passed_on:
            cp.wait_send()
        for cp in local:
            cp.wait()

    dma = pltpu.SemaphoreType.DMA
    return _CommPlan(
        ins=[grads[name] for name in names],
        out_shape=[jax.ShapeDtypeStruct((N_DEV,) + _shard_shape(*spec[name]), BF16) for name in names],
        scratch=[dma((7 * nw,)), dma((7 * nw,)), dma((nw,))],
        start=start, finish=finish)


def _adamw(w, g, m, v):
    m = ADAM_B1 * m + (1.0 - ADAM_B1) * g
    v = ADAM_B2 * v + (1.0 - ADAM_B2) * (g * g)
    m_hat = m / (1.0 - ADAM_B1 ** ADAM_STEP)
    v_hat = v / (1.0 - ADAM_B2 ** ADAM_STEP)
    delta = -ADAM_LR * (m_hat / (jnp.sqrt(v_hat) + ADAM_EPS) + ADAM_WD * w)
    return delta, m, v


def _reduce_adamw(name, stack, w, m, v):
    rows, cols = w.shape
    tr = next(t for t in (256, 128, 64) if rows % t == 0)

    def body(s_ref, w_ref, m_ref, v_ref, g_ref, d_ref, nm_ref, nv_ref):
        g = s_ref[0].astype(F32)
        for d in range(1, N_DEV):
            g = g + s_ref[d].astype(F32)
        g_ref[...] = g
        d_ref[...], nm_ref[...], nv_ref[...] = _adamw(w_ref[...], g, m_ref[...], v_ref[...])

    blk = pl.BlockSpec((tr, cols), lambda i: (i, 0))
    return pl.pallas_call(
        body, name=name, grid=(rows // tr,),
        in_specs=[pl.BlockSpec((N_DEV, tr, cols), lambda i: (0, i, 0)), blk, blk, blk],
        out_specs=[blk] * 4, out_shape=[jax.ShapeDtypeStruct((rows, cols), F32)] * 4,
        compiler_params=_cparams(),
    )(stack, w, m, v)


def _small_step(pack, w, m, v):
    def body(p_ref, w_ref, m_ref, v_ref, g_ref, d_ref, nm_ref, nv_ref, loss_ref, all_ref, send_sems, recv_sems):
        x, y, c, _ = _place()
        me = 4 * x + 2 * y + c
        all_ref[me] = p_ref[...]
        sent = []
        for n in range(1, N_DEV):
            peer = me ^ n
            cp = pltpu.make_async_remote_copy(
                src_ref=p_ref, dst_ref=all_ref.at[me], send_sem=send_sems.at[n - 1], recv_sem=recv_sems.at[n - 1],
                device_id=(peer // 4, (peer // 2) % 2, peer % 2), device_id_type=MESH)
            cp.start()
            sent.append(cp)
        for n in range(1, N_DEV):
            peer = me ^ n
            pltpu.make_async_remote_copy(
                src_ref=p_ref, dst_ref=all_ref.at[peer], send_sem=send_sems.at[n - 1], recv_sem=recv_sems.at[n - 1],
                device_id=(peer // 4, (peer // 2) % 2, peer % 2), device_id_type=MESH).wait_recv()
        for cp in sent:
            cp.wait_send()
        tot = all_ref[0]
        for d in range(1, N_DEV):
            tot = tot + all_ref[d]
        g = tot[:SMALL_ROWS]
        g_ref[...] = g
        d_ref[...], nm_ref[...], nv_ref[...] = _adamw(w_ref[...], g, m_ref[...], v_ref[...])
        loss_ref[...] = jnp.sum(jnp.sum(tot[SMALL_ROWS:], axis=1, keepdims=True), axis=0, keepdims=True)

    vm = pl.BlockSpec(memory_space=pltpu.VMEM)
    small = jax.ShapeDtypeStruct((SMALL_ROWS, LANES), F32)
    return pl.pallas_call(
        body, name="small_step",
        in_specs=[vm] * 4, out_specs=[vm] * 5,
        out_shape=[small] * 4 + [jax.ShapeDtypeStruct((1, 1), F32)],
        scratch_shapes=[pltpu.VMEM((N_DEV, PACK_ROWS, LANES), F32),
                        pltpu.SemaphoreType.DMA((N_DEV - 1,)), pltpu.SemaphoreType.DMA((N_DEV - 1,))],
    )(pack, w, m, v)


LATER_WEIGHTS = tuple(name for name, _, _ in BIG if name != "w_in")


def _layer_step(x, mem, tgt, shards, vec):
    s = x.shape[0]
    d = D_MODEL
    tm = min(ROW_TILE, s)
    tl = min(WIDE_TILE, s)
    xb, cos2, sin2, w_in = _prep(x, _gather_plan(("w_in",), shards))
    bf = lambda w: ((s, w), BF16)
    f32 = lambda w: ((s, w), F32)

    w_sb, w_rqk = w_in[:, :OFF_RET_Q], w_in[:, OFF_RET_Q:OFF_RET_V]
    w_rvg, w_gate = w_in[:, OFF_RET_V:OFF_GATE], w_in[:, OFF_GATE:]
    q_scale = lambda width, q_width, scale: jnp.concatenate(
        [jnp.full((1, q_width), scale, F32), jnp.ones((1, width - q_width), F32)], axis=1)
    n_groups = 3 * SB_WIDTH // LANES

    def sb_epi(acc, t, i, j):
        scaled = acc * t[0]
        return [jnp.stack([scaled[:, g * LANES:(g + 1) * LANES] for g in range(n_groups)])], []

    (sb_qkv,) = _mm(
        "in_sb", xb, w_sb, s, 3 * SB_WIDTH, d, tm=tl, tn=3 * SB_WIDTH, tk=d, epi=sb_epi,
        ins=[(q_scale(3 * SB_WIDTH, SB_WIDTH, SB_SCALE), *_rowvec(3 * SB_WIDTH))],
        outs=[((n_groups, s, LANES), BF16, (n_groups, tl, LANES), lambda i, j: (0, i, 0))])

    def rope_epi(acc, t, i, j):
        cos, sin, scale = t
        parts = []
        for g in range(acc.shape[1] // RET_QK):
            xg = acc[:, g * RET_QK:(g + 1) * RET_QK]
            parts.append(xg * cos + _swap_halves(xg) * sin)
        return [jnp.concatenate(parts, axis=1) * scale], []

    rope_in = ((tl, RET_QK), lambda i, j: (i, 0))
    (rqk,) = _mm("in_rqk", xb, w_rqk, s, 2 * RET_QK_WIDTH, d, tm=tl, tn=2 * RET_QK_WIDTH, tk=d, epi=rope_epi,
                 chunk=MXU_COLS,
                 ins=[(cos2, *rope_in), (sin2, *rope_in),
                      (q_scale(2 * RET_QK_WIDTH, RET_QK_WIDTH, RET_SCALE), *_rowvec(2 * RET_QK_WIDTH))],
                 outs=[(*f32(2 * RET_QK_WIDTH), *_tile(tl, 2 * RET_QK_WIDTH))])
    (rvg,) = _mm("in_rvg", xb, w_rvg, s, 2 * RET_V_WIDTH, d, tm=tl, tn=2 * RET_V_WIDTH, tk=d, chunk=MXU_COLS,
                 epi=_plain, outs=[(*bf(2 * RET_V_WIDTH), *_tile(tl, 2 * RET_V_WIDTH))])
    (gates,) = _mm("in_gate", xb, w_gate, s, 2 * d, d, tm=tl, tn=2 * d, tk=d, chunk=MXU_COLS,
                   epi=lambda acc, t, i, j: ([_sigmoid(acc + t[0])], []),
                   ins=[(vec["b_gate"], *_rowvec(2 * d))], outs=[(*bf(2 * d), *_tile(tl, 2 * d))])

    sb_out, sb_out_f32, *gathered = _sb_fwd(sb_qkv, s, comm=_gather_plan(LATER_WEIGHTS, shards))
    wt = dict(zip(LATER_WEIGHTS, gathered, strict=True))
    ret, gated = _ret_fwd(rqk, rvg, s)
    (y_sb,) = _mm("sb_o", sb_out, wt["w_sb_o"], s, d, SB_WIDTH, tm=tl, tn=d, tk=SB_WIDTH, epi=_plain,
                  outs=[(*bf(d), *_tile(tl, d))])
    y_ret, mixin = _mm(
        "ret_o", gated, wt["w_ret_o"], s, d, RET_V_WIDTH, tm=tl, tn=d, tk=RET_V_WIDTH, chunk=MXU_COLS,
        epi=lambda acc, t, i, j: ([acc, t[0].astype(F32) * t[2].astype(F32) + t[1].astype(F32) * acc], []),
        ins=[(gates, *_tile(tl, d)), (gates, *_tile(tl, d, 1)), (y_sb, *_tile(tl, d))],
        outs=[(*bf(d), *_tile(tl, d)), (*bf(d), *_tile(tl, d))])

    def ln_epi(acc, t, i, j):
        *res, g, b = t
        prev = res[0] if len(res) == 1 else res[0] * res[1] + res[2]
        xhat, rstd = _norm(DN_ALPHA * prev + acc)
        return [xhat * g + b, xhat, rstd], []

    full = _tile(tm, d)
    col1 = ((tm, 1), lambda i, j: (i, 0))
    vec_in = lambda name: (vec[name], *_rowvec(d))
    ln_outs = [(*bf(d), *full), (*f32(d), *full), ((s, 1), F32, *col1)]
    x1b, xhat1, rstd1 = _mm(
        "mix_o", mixin, wt["w_mix_o"], s, d, d, tm=tm, tn=d, tk=d, epi=ln_epi,
        ins=[(x, *full), vec_in("ln1_g"), vec_in("ln1_b")], outs=ln_outs)

    (qm,) = _mm("mem_q", x1b, wt["w_mem_q"], s, d, d, tm=tl, tn=d, tk=d,
                epi=lambda acc, t, i, j: ([acc * MEM_SCALE], []), outs=[(*bf(d), *_tile(tl, d))])
    (kv,) = _mm("mem_kv", mem, wt["w_mem_kv"], MEM_LEN, 2 * d, d, tm=MEM_LEN, tn=d, tk=d, epi=_plain,
                outs=[((MEM_LEN, 2 * d), BF16, *_tile(MEM_LEN, d))])
    att = _xattn_fwd(qm, kv, s)
    x2b, xhat2, rstd2 = _mm(
        "mem_o", att, wt["w_mem_o"], s, d, d, tm=tm, tn=d, tk=d, epi=ln_epi,
        ins=[(xhat1, *full), vec_in("ln1_g"), vec_in("ln1_b"), vec_in("ln2_g"), vec_in("ln2_b")], outs=ln_outs)

    fh = FFN_HIDDEN
    tf = fh // 2
    (f1,) = _mm("ffn_in1", x2b, wt["w_ffn_in"], s, fh, d, tm=tl, tn=tf, tk=d, epi=_plain, j_outer=True,
                outs=[(*bf(fh), *_tile(tl, tf))])

    def swiglu_epi(acc, t, i, j):
        a = t[0].astype(F32)
        return [acc, a * _sigmoid(a) * acc], []

    f2, act = _mm(
        "ffn_in2", x2b, wt["w_ffn_in"], s, fh, d, tm=tm, tn=fh, tk=d, b_off=(0, 1), epi=swiglu_epi, chunk=MXU_COLS,
        ins=[(f1, *_tile(tm, fh))], outs=[(*bf(fh), *_tile(tm, fh)), (*bf(fh), *_tile(tm, fh))])

    def head_epi(acc, t, i, j):
        prev_hat, prev_g, prev_b, g, b, target = t
        xhat, rstd = _norm(DN_ALPHA * (prev_hat * prev_g + prev_b) + acc)
        err = xhat * g + b - target
        dy = err * (1.0 / d)
        du = _norm_bwd(dy * g, xhat, rstd)
        return [du], [_colsum(dy * xhat), _colsum(dy), _colsum(err * err) * (0.5 / d)]

    vec_acc = ((1, d), F32)
    du3b, dg3, db3, loss_cols = _mm(
        "ffn_out", act, wt["w_ffn_out"], s, d, fh, tm=tm, tn=d, tk=fh, epi=head_epi,
        ins=[(xhat2, *full), vec_in("ln2_g"), vec_in("ln2_b"), vec_in("ln3_g"), vec_in("ln3_b"), (tgt, *full)],
        outs=[(*bf(d), *full)], accs=[vec_acc] * 3)

    grads = {}
    ts = min(SEQ_TILE, s)

    def wgrad(name, a, b, m, n, tm_, tn_, tk_=None):
        (g,) = _mm(name, a, b, m, n, a.shape[0], tm=tm_, tn=tn_, tk=tk_ or ts, ta=True, epi=_plain,
                   outs=[((m, n), BF16, *_tile(tm_, tn_))])
        return g

    def ffn_bwd_epi(acc, t, i, j):
        a, b = t[0].astype(F32), t[1].astype(F32)
        sg = _sigmoid(a)
        return [acc * b * (sg * (1.0 + a * (1.0 - sg))), acc * (a * sg)], []

    df1, df2 = _mm(
        "ffn_out_t", du3b, wt["w_ffn_out"], s, fh, d, tm=tm, tn=fh, tk=d, tb=True, epi=ffn_bwd_epi, chunk=MXU_COLS,
        ins=[(f1, *_tile(tm, fh)), (f2, *_tile(tm, fh))],
        outs=[(*bf(fh), *_tile(tm, fh)), (*bf(fh), *_tile(tm, fh))])
    grads["w_ffn_out"] = wgrad("g_ffn_out", act, du3b, fh, d, tf, d)
    grads["w_ffn_in"] = jnp.concatenate(
        [wgrad("g_ffn_in1", x2b, df1, d, fh, d, tf), wgrad("g_ffn_in2", x2b, df2, d, fh, d, tf)], axis=1)
    (dx2a,) = _mm("ffn_in1_t", df1, wt["w_ffn_in"], s, d, fh, tm=tl, tn=d, tk=fh, tb=True, epi=_plain,
                  outs=[(*f32(d), *_tile(tl, d))])

    def ln_bwd(name, a, b, k, tk, b_off, more, scales, xhat, rstd, g):
        def epi(acc, t, i, j):
            *extra, xh, rs, gg = t
            dy = acc
            for e, sc in zip(extra, scales, strict=True):
                dy = dy + e.astype(F32) * sc
            return [_norm_bwd(dy * gg, xh, rs)], [_colsum(dy * xh), _colsum(dy)]

        return _mm(name, a, b, s, d, k, tm=tm, tn=d, tk=tk, tb=True, b_off=b_off, epi=epi,
                   ins=[(e, *full) for e in more] + [(xhat, *full), (rstd, *col1), (g, *_rowvec(d))],
                   outs=[(*bf(d), *full)], accs=[vec_acc] * 2)

    du2b, dg2, db2 = ln_bwd("ffn_in2_t", df2, wt["w_ffn_in"], fh, fh, (0, 1), [dx2a, du3b], [1.0, DN_ALPHA],
                            xhat2, rstd2, vec["ln2_g"])

    (datt,) = _mm("mem_o_t", du2b, wt["w_mem_o"], s, d, d, tm=tl, tn=d, tk=d, tb=True, epi=_plain,
                  outs=[(*bf(d), *_tile(tl, d))])
    grads["w_mem_o"] = wgrad("g_mem_o", att, du2b, d, d, d, d)
    dqm, dkv = _xattn_bwd(qm, kv, datt, s)
    grads["w_mem_q"] = wgrad("g_mem_q", x1b, dqm, d, d, d, d)
    grads["w_mem_kv"] = wgrad("g_mem_kv", mem, dkv, d, 2 * d, d, d, MEM_LEN)
    du1b, dg1, db1 = ln_bwd("mem_q_t", dqm, wt["w_mem_q"], d, d, (0, 0), [du2b], [DN_ALPHA],
                            xhat1, rstd1, vec["ln1_g"])

    def merge_bwd_epi(acc, t, i, j):
        g0, g1, ysb, yret = (v.astype(F32) for v in t)
        dgate0 = acc * ysb * (g0 * (1.0 - g0))
        dgate1 = acc * yret * (g1 * (1.0 - g1))
        return [dgate0, dgate1, acc * g0, acc * g1], [_colsum(dgate0), _colsum(dgate1)]

    dgate0, dgate1, dy_sb, dy_ret, dbg0, dbg1 = _mm(
        "mix_o_t", du1b, wt["w_mix_o"], s, d, d, tm=tm, tn=d, tk=d, tb=True, epi=merge_bwd_epi,
        ins=[(gates, *full), (gates, *_tile(tm, d, 1)), (y_sb, *full), (y_ret, *full)],
        outs=[(*bf(d), *full)] * 4, accs=[vec_acc] * 2)
    grads["w_mix_o"] = wgrad("g_mix_o", mixin, du1b, d, d, d, d)
    grads["w_sb_o"] = wgrad("g_sb_o", sb_out, dy_sb, SB_WIDTH, d, SB_WIDTH, d)
    grads["w_ret_o"] = wgrad("g_ret_o", gated, dy_ret, RET_V_WIDTH, d, RET_V_WIDTH, d)
    (dsb_out,) = _mm("sb_o_t", dy_sb, wt["w_sb_o"], s, SB_WIDTH, d, tm=tl, tn=SB_WIDTH, tk=d, tb=True, epi=_plain,
                     outs=[(*bf(SB_WIDTH), *_tile(tl, SB_WIDTH))])

    def gate_norm_bwd_epi(acc, t, i, j):
        r, g = t[0], t[1].astype(F32)
        drg, dret = [], []
        for h in range(acc.shape[1] // RET_V):
            sl = slice(h * RET_V, (h + 1) * RET_V)
            xhat, rstd = _norm(r[:, sl])
            gg, dd = g[:, sl], acc[:, sl]
            sg = _sigmoid(gg)
            drg.append(dd * xhat * (sg * (1.0 + gg * (1.0 - sg))))
            dret.append(_norm_bwd(dd * (gg * sg), xhat, rstd))
        return [jnp.concatenate(drg, axis=1), jnp.concatenate(dret, axis=1)], []

    drg, dret = _mm(
        "ret_o_t", dy_ret, wt["w_ret_o"], s, RET_V_WIDTH, d, tm=tm, tn=d, tk=d, tb=True, epi=gate_norm_bwd_epi,
        chunk=MXU_COLS,
        ins=[(ret, *full), (rvg, *_tile(tm, d, 1))],
        outs=[(*bf(RET_V_WIDTH), *full)] * 2)

    drq = _ret_bwd_q(rqk, rvg, dret, cos2, sin2, s)
    drk, drv = _ret_bwd_kv(rqk, rvg, dret, cos2, sin2, s)
    dsq, dsk, dsv, *stacked = _sb_bwd(sb_qkv, sb_out_f32, dsb_out, s, comm=_exchange_plan(LATER_WEIGHTS, grads))
    stacks = dict(zip(LATER_WEIGHTS, stacked, strict=True))

    dh = {"sq": dsq, "sk": dsk, "sv": dsv, "rq": drq, "rk": drk, "rv": drv, "rg": drg, "gate0": dgate0,
          "gate1": dgate1}
    grads["w_in"] = jnp.concatenate(
        [wgrad("g_in_" + name, xb, piece, d, piece.shape[1], d, piece.shape[1]) for name, piece in dh.items()],
        axis=1)
    grad_x, stacks["w_in"] = _mm(
        "in_t", list(dh.values()), w_in, s, d, IN_WIDTH, tm=tm, tn=d, tk=IN_WIDTH, tb=True,
        epi=lambda acc, t, i, j: ([acc + DN_ALPHA * t[0].astype(F32)], []),
        ins=[(du1b, *full)], outs=[(*f32(d), *full)], comm=_exchange_plan(("w_in",), grads))

    small = {"b_gate": jnp.concatenate([dbg0, dbg1], axis=1), "ln1_g": dg1, "ln1_b": db1, "ln2_g": dg2,
             "ln2_b": db2, "ln3_g": dg3, "ln3_b": db3}
    return grad_x, stacks, small, loss_cols


def kernel(x, mem, w_in, b_gate, w_sb_o, w_ret_o, w_mix_o, ln1_g, ln1_b, w_mem_q, w_mem_kv, w_mem_o, ln2_g, ln2_b, w_ffn_in, w_ffn_out, ln3_g, ln3_b, loss_target, m_w_in, m_b_gate, m_w_sb_o, m_w_ret_o, m_w_mix_o, m_ln1_g, m_ln1_b, m_w_mem_q, m_w_mem_kv, m_w_mem_o, m_ln2_g, m_ln2_b, m_w_ffn_in, m_w_ffn_out, m_ln3_g, m_ln3_b, v_w_in, v_b_gate, v_w_sb_o, v_w_ret_o, v_w_mix_o, v_ln1_g, v_ln1_b, v_w_mem_q, v_w_mem_kv, v_w_mem_o, v_ln2_g, v_ln2_b, v_w_ffn_in, v_w_ffn_out, v_ln3_g, v_ln3_b):
    given = dict(locals())
    s = x.shape[1]
    x2d = x.reshape(s, D_MODEL)
    tgt = loss_target.reshape(s, D_MODEL)
    mem2d = mem.reshape(MEM_LEN, D_MODEL)
    shard = {name: given[name].reshape(_shard_shape(shape, axis)) for name, shape, axis in BIG}
    vec = {name: given[name] for name in SMALL}

    shards_bf = {name: _cast_bf16("cast_" + name, shard[name]) for name, _, _ in BIG}

    grad_x, stacks, small, loss_cols = _layer_step(x2d, mem2d, tgt, shards_bf, vec)

    out = {}
    for name, shape, axis in BIG:
        stack = stacks[name]
        shp = given[name].shape
        res = _reduce_adamw("adamw_" + name, stack, shard[name], given["m_" + name].reshape(stack.shape[1:]),
                            given["v_" + name].reshape(stack.shape[1:]))
        out[name] = [r.reshape(shp) for r in res]

    pack = jnp.concatenate([small[name] for name in SMALL] + [loss_cols], axis=1).reshape(PACK_ROWS, LANES)
    cat = lambda pre: jnp.concatenate([given[pre + name] for name in SMALL], axis=1).reshape(SMALL_ROWS, LANES)
    *res, loss = _small_step(pack, cat(""), cat("m_"), cat("v_"))
    flat = [r.reshape(1, SMALL_LEN) for r in res]
    off = 0
    for name in SMALL:
        n = given[name].shape[1]
        out[name] = [r[:, off:off + n] for r in flat]
        off += n

    return (loss.reshape(()), grad_x.reshape(x.shape),
            *[out[name][0] for name in WEIGHT_ORDER], *[out[name][1] for name in WEIGHT_ORDER],
            *[out[name][2] for name in WEIGHT_ORDER], *[out[name][3] for name in WEIGHT_ORDER])
```

```python
import functools

import jax
import jax.numpy as jnp
import numpy as np
from jax import lax
from jax.experimental import pallas as pl
from jax.experimental.pallas import tpu as pltpu

F32, BF16 = jnp.float32, jnp.bfloat16
MESH = pl.DeviceIdType.MESH

D_MODEL = 1024
MEM_LEN = 256
SB_HEADS, SB_DIM, SB_WIDTH = 8, 64, 512
RET_HEADS, RET_QK, RET_V = 4, 128, 256
RET_QK_WIDTH, RET_V_WIDTH = 512, 1024
ROPE_BASE = 10000.0
MEM_HEADS, MEM_DIM = 4, 256
FFN_HIDDEN = 2816
IN_WIDTH = 6656
OFF_RET_Q, OFF_RET_V, OFF_RET_G, OFF_GATE = 1536, 2560, 3584, 4608
DN_ALPHA = 2.0 ** 0.25
LN_EPS = 1e-5
SB_SCALE = SB_DIM ** -0.5
SB_DEAD = -110.0
RET_SCALE = RET_QK ** -0.5
MEM_SCALE = MEM_DIM ** -0.5
ADAM_LR, ADAM_B1, ADAM_B2, ADAM_EPS, ADAM_WD, ADAM_STEP = 0.001, 0.9, 0.999, 1e-08, 0.01, 10

N_DEV, N_CHIPS = 8, 4

LANES = 128
MXU_COLS = 256
VMEM_LIMIT_BYTES = 52 * 2 ** 20
ROW_TILE = 512
WIDE_TILE = 1024
SEQ_TILE = 2048
SB_BLOCK = 256
RET_BLOCK = 256
RET_CHUNKS_PER_STEP = 4
XATTN_ROWS = 1024

BIG = (
    ("w_in", (D_MODEL, IN_WIDTH), 1),
    ("w_sb_o", (SB_WIDTH, D_MODEL), 1),
    ("w_ret_o", (RET_V_WIDTH, D_MODEL), 0),
    ("w_mix_o", (D_MODEL, D_MODEL), 0),
    ("w_mem_q", (D_MODEL, D_MODEL), 0),
    ("w_mem_kv", (D_MODEL, 2 * D_MODEL), 1),
    ("w_mem_o", (D_MODEL, D_MODEL), 0),
    ("w_ffn_in", (D_MODEL, 2 * FFN_HIDDEN), 1),
    ("w_ffn_out", (FFN_HIDDEN, D_MODEL), 0),
)
SMALL = ("b_gate", "ln1_g", "ln1_b", "ln2_g", "ln2_b", "ln3_g", "ln3_b")
SMALL_LEN = 2 * D_MODEL + 6 * D_MODEL
SMALL_ROWS = SMALL_LEN // LANES
PACK_ROWS = SMALL_ROWS + D_MODEL // LANES
WEIGHT_ORDER = ("w_in", "b_gate", "w_sb_o", "w_ret_o", "w_mix_o", "ln1_g", "ln1_b", "w_mem_q", "w_mem_kv",
                "w_mem_o", "ln2_g", "ln2_b", "w_ffn_in", "w_ffn_out", "ln3_g", "ln3_b")


def _cparams():
    return pltpu.CompilerParams(vmem_limit_bytes=VMEM_LIMIT_BYTES)


def _dot(a, b, ca, cb):
    return lax.dot_general(a, b, (((ca,), (cb,)), ((), ())), preferred_element_type=F32)


def _sigmoid(x):
    return 1.0 / (1.0 + jnp.exp(-x))


def _mm(name, a, b, m, n, k, *, tm, tn, tk, epi, outs, ins=(), accs=(), ta=False, tb=False,
        a_off=(0, 0), b_off=(0, 0), j_outer=False, comm=None, chunk=None):
    assert m % tm == 0 and n % tn == 0 and k % tk == 0, (name, m, n, k, tm, tn, tk)
    assert chunk is None or (k == tk and tn % chunk == 0), name
    ni, nj, nk = m // tm, n // tn, k // tk
    assert not accs or nj == 1, name
    ij = (lambda g0, g1: (g1, g0)) if j_outer else (lambda g0, g1: (g0, g1))

    def spec(block, index):
        return pl.BlockSpec(block, lambda g0, g1, kk: index(*ij(g0, g1), kk))

    a_list = list(a) if isinstance(a, (list, tuple)) else [a]
    n_a = len(a_list)
    if n_a > 1:
        assert not ta and nk == 1 and chunk is None and not any(a_off), name
        assert sum(p.shape[1] for p in a_list) == k, name
        a_specs = [spec((tm, p.shape[1]), lambda i, j, kk: (i, 0)) for p in a_list]
    elif ta:
        a_specs = [spec((tk, tm), lambda i, j, kk: (kk + a_off[0], i + a_off[1]))]
    else:
        a_specs = [spec((tm, tk), lambda i, j, kk: (i + a_off[0], kk + a_off[1]))]
    b_list = list(b) if isinstance(b, (list, tuple)) else [b]
    n_b = len(b_list)
    if n_b > 1:
        assert not tb and nj == 1 and nk > 1 and n_a == 1 and chunk is None and not any(b_off), name
        assert sum(p.shape[1] for p in b_list) == n, name
        b_specs = [spec((tk, p.shape[1]), lambda i, j, kk: (kk, 0)) for p in b_list]
    elif tb:
        b_specs = [spec((tn, tk), lambda i, j, kk: (j + b_off[0], kk + b_off[1]))]
    else:
        b_specs = [spec((tk, tn), lambda i, j, kk: (kk + b_off[0], j + b_off[1]))]
    if n_a > 1 and nj == 1:
        b_specs = [pl.BlockSpec(b_specs[0].block_shape, b_specs[0].index_map, pipeline_mode=pl.Buffered(1))]
    in_specs = [*a_specs, *b_specs]
    for _, bs, im in ins:
        in_specs.append(spec(bs, lambda i, j, kk, im=im: im(i, j)))
    out_specs, out_shape = [], []
    for shape, dtype, bs, im in outs:
        out_specs.append(spec(bs, lambda i, j, kk, im=im: im(i, j)))
        out_shape.append(jax.ShapeDtypeStruct(shape, dtype))
    for shape, dtype in accs:
        out_specs.append(spec(shape, lambda i, j, kk, nd=len(shape): (0,) * nd))
        out_shape.append(jax.ShapeDtypeStruct(shape, dtype))
    n_in, n_out, n_acc = len(ins), len(outs), len(accs)
    ca, cb = (0 if ta else 1), (1 if tb else 0)
    grid = (*ij(ni, nj), nk)
    comm_ins, comm_outs, comm_scratch = [], [], []
    if comm is not None:
        comm_in_specs, comm_out_specs = comm.specs
        comm_ins, comm_outs, comm_scratch = list(comm.ins), list(comm.out_shape), list(comm.scratch)
        in_specs += comm_in_specs
        out_specs += comm_out_specs
        out_shape += comm_outs
    n_ci, n_co = len(comm_ins), len(comm_outs)

    def body(*refs):
        a_refs, b_refs, refs = refs[:n_a], refs[n_a:n_a + n_b], refs[n_a + n_b:]
        a_ref, b_ref = a_refs[0], b_refs[0]
        in_refs = refs[:n_in]
        ci_refs = refs[n_in:n_in + n_ci]
        rest = refs[n_in + n_ci:]
        out_refs, acc_refs = rest[:n_out], rest[n_out:n_out + n_acc]
        co_refs = rest[n_out + n_acc:n_out + n_acc + n_co]
        scratch = rest[n_out + n_acc + n_co:]
        sem_refs, scratch = scratch[:len(comm_scratch)], scratch[len(comm_scratch):]
        (i, j), kk = ij(pl.program_id(0), pl.program_id(1)), pl.program_id(2)
        if comm is not None:
            first_step, last_step = _grid_ends(grid)
            pl.when(first_step)(lambda: comm.start(ci_refs, co_refs, sem_refs))
        def finish(acc, cols=slice(None)):
            def of(r):
                return r[..., cols] if r.shape[-1] == tn else r[...]

            o_tiles, a_tiles = epi(acc, [of(r) for r in in_refs], i, j)
            for r, t in zip(out_refs, o_tiles, strict=True):
                r[..., cols] = t.astype(r.dtype)
            if n_acc:
                @pl.when(i == 0)
                def _():
                    for r, t in zip(acc_refs, a_tiles, strict=True):
                        r[..., cols] = t

                @pl.when(i > 0)
                def _():
                    for r, t in zip(acc_refs, a_tiles, strict=True):
                        r[..., cols] += t

        if chunk is not None:
            a_tile = a_ref[...].astype(BF16)
            for c0 in range(0, tn, chunk):
                cols = slice(c0, c0 + chunk)
                b_part = b_ref[cols, :] if tb else b_ref[:, cols]
                finish(_dot(a_tile, b_part.astype(BF16), ca, cb), cols)
            if comm is not None:
                pl.when(last_step)(lambda: comm.finish(ci_refs, co_refs, sem_refs))
            return

        if n_b > 1:
            acc_ref = scratch[0]

            def accumulate(first):
                a_tile, c0 = a_ref[...].astype(BF16), 0
                for r in b_refs:
                    c1 = c0 + r.shape[1]
                    term = _dot(a_tile, r[...].astype(BF16), ca, cb)
                    acc_ref[:, c0:c1] = term if first else acc_ref[:, c0:c1] + term
                    c0 = c1

            pl.when(kk == 0)(lambda: accumulate(True))
            pl.when(kk > 0)(lambda: accumulate(False))
            pl.when(kk == nk - 1)(lambda: finish(acc_ref[...]))
            if comm is not None:
                pl.when(last_step)(lambda: comm.finish(ci_refs, co_refs, sem_refs))
            return

        if n_a > 1:
            part, c0 = None, 0
            for r in a_refs:
                c1 = c0 + r.shape[1]
                b_part = b_ref[:, c0:c1] if tb else b_ref[c0:c1, :]
                term = _dot(r[...].astype(BF16), b_part.astype(BF16), ca, cb)
                part, c0 = (term if part is None else part + term), c1
        else:
            part = _dot(a_ref[...].astype(BF16), b_ref[...].astype(BF16), ca, cb)
        if nk == 1:
            finish(part)
        else:
            acc_ref = scratch[0]

            @pl.when(kk == 0)
            def _():
                acc_ref[...] = part

            @pl.when(kk > 0)
            def _():
                acc_ref[...] += part

            @pl.when(kk == nk - 1)
            def _():
                finish(acc_ref[...])

        if comm is not None:
            pl.when(last_step)(lambda: comm.finish(ci_refs, co_refs, sem_refs))

    res = pl.pallas_call(
        body, name=name, grid=grid, in_specs=in_specs, out_specs=out_specs, out_shape=out_shape,
        scratch_shapes=comm_scratch + ([pltpu.VMEM((tm, tn), F32)] if nk > 1 else []),
        compiler_params=_cparams(),
    )(*a_list, *b_list, *[x for x, _, _ in ins], *comm_ins)
    return res


def _grid_ends(grid):
    ids = [pl.program_id(ax) for ax in range(len(grid))]
    first = functools.reduce(jnp.logical_and, [p == 0 for p in ids])
    last = functools.reduce(jnp.logical_and, [p == n - 1 for p, n in zip(ids, grid, strict=True)])
    return first, last


def _tile(tm, tn, dj=0):
    return (tm, tn), (lambda i, j: (i, j + dj))


def _rowvec(tn, dj=0):
    return (1, tn), (lambda i, j: (0, j + dj))


def _plain(acc, tiles, i, j):
    return [acc], []


def _ew(name, fn, ins, outs, rows, tr):
    assert rows % tr == 0, (name, rows, tr)
    in_specs = []
    for x in ins:
        if x.shape[0] == rows:
            in_specs.append(pl.BlockSpec((tr, x.shape[1]), lambda i: (i, 0)))
        else:
            in_specs.append(pl.BlockSpec(x.shape, lambda i: (0, 0)))
    n_in = len(ins)

    def body(*refs):
        res = fn(*[r[...] for r in refs[:n_in]])
        for r, t in zip(refs[n_in:], res, strict=True):
            r[...] = t.astype(r.dtype)

    return pl.pallas_call(
        body, name=name, grid=(rows // tr,), in_specs=in_specs,
        out_specs=[pl.BlockSpec((tr, w), lambda i: (i, 0)) for w, _ in outs],
        out_shape=[jax.ShapeDtypeStruct((rows, w), dt) for w, dt in outs],
        compiler_params=_cparams(),
    )(*ins)


def _cast_bf16(name, x):
    rows = x.shape[0]
    tr = next(t for t in (512, 256, 64) if rows % t == 0)
    return _ew(name, lambda v: (v,), [x], [(x.shape[1], BF16)], rows, tr)[0]


def _prep(x, comm):
    s = x.shape[0]
    half = RET_QK // 2
    inv = 1.0 / (ROPE_BASE ** (jnp.arange(half, dtype=F32) / half))
    inv2 = jnp.concatenate([inv, inv]).reshape(1, RET_QK)
    sign = jnp.concatenate([-jnp.ones((half,), F32), jnp.ones((half,), F32)]).reshape(1, RET_QK)
    tr = min(ROW_TILE, s)
    grid = (s // tr,)
    c_in_specs, c_out_specs, c_out_shape, c_scratch, c_ins, split = _host(comm, 3, 3)

    def body(*refs):
        (x_ref, inv_ref, sign_ref), (xb_ref, cos_ref, sin_ref), _, riding = split(refs)
        i = pl.program_id(0)
        first_step, last_step = _grid_ends(grid)
        pl.when(first_step)(lambda: comm.start(*riding))
        xb_ref[...] = x_ref[...].astype(BF16)
        pos = (lax.broadcasted_iota(jnp.int32, (tr, RET_QK), 0) + i * tr).astype(F32)
        ang = pos * inv_ref[...]
        cos_ref[...] = jnp.cos(ang)
        sin_ref[...] = jnp.sin(ang) * sign_ref[...]
        pl.when(last_step)(lambda: comm.finish(*riding))

    vec = pl.BlockSpec((1, RET_QK), lambda i: (0, 0))
    row = lambda w: pl.BlockSpec((tr, w), lambda i: (i, 0))
    return pl.pallas_call(
        body, name="prep", grid=grid,
        in_specs=[row(D_MODEL), vec, vec] + c_in_specs,
        out_specs=[row(D_MODEL), row(RET_QK), row(RET_QK)] + c_out_specs,
        out_shape=[jax.ShapeDtypeStruct((s, D_MODEL), BF16), jax.ShapeDtypeStruct((s, RET_QK), F32),
                   jax.ShapeDtypeStruct((s, RET_QK), F32)] + c_out_shape,
        scratch_shapes=c_scratch, compiler_params=_cparams(),
    )(x, inv2, sign, *c_ins)


def _swap_halves(x):
    return pltpu.roll(x, RET_QK // 2, 1)


def _norm(u):
    mu = jnp.mean(u, axis=-1, keepdims=True)
    d = u - mu
    var = jnp.mean(d * d, axis=-1, keepdims=True)
    rstd = lax.rsqrt(var + LN_EPS)
    return d * rstd, rstd


def _norm_bwd(dxh, xhat, rstd):
    m1 = jnp.mean(dxh, axis=-1, keepdims=True)
    m2 = jnp.mean(dxh * xhat, axis=-1, keepdims=True)
    return rstd * (dxh - m1 - xhat * m2)


def _colsum(t):
    return jnp.sum(t, axis=0, keepdims=True)


def _split_mm(t, tri):
    hi = t.astype(BF16)
    lo = (t - hi.astype(F32)).astype(BF16)
    return _dot(hi, tri, 1, 0) + _dot(lo, tri, 1, 0)


def _sb_masks():
    t = SB_BLOCK
    lane = lax.broadcasted_iota(jnp.int32, (1, LANES), 1)
    first = lane < SB_DIM
    m0 = jnp.where(first, 1.0, 0.0).astype(BF16)
    m1 = jnp.where(first, 0.0, 1.0).astype(BF16)
    row = lax.broadcasted_iota(jnp.int32, (t, t), 0)
    col = lax.broadcasted_iota(jnp.int32, (t, t), 1)
    return first, (m0, m1), row, col


def _sb_logits(qh, k, causal):
    z = _dot(qh, k, 1, 1)
    lp = jnp.log(1.0 + jnp.exp(-jnp.abs(z)))
    a = jnp.minimum(z, 0.0) - lp
    r = jnp.minimum(-z, 0.0) - lp
    if causal is not None:
        r = jnp.where(causal, r, 0.0)
    return a, r


def _sb_walk(i, blocks, l_ref, causal):
    pl.when(i == 0)(lambda: blocks([(i, causal)]))
    pl.when(i > 0)(lambda: blocks([(i, causal), (i - 1, None)]))

    def alive():
        top = jnp.max(functools.reduce(jnp.maximum, [l_ref[c] for c in range(l_ref.shape[0])]))
        return jnp.where(top > SB_DEAD, 1, 0)

    def cond(c):
        return jnp.logical_and(c[0] < i, c[1] > 0)

    def step(c):
        blocks([(i - 1 - c[0], None)])
        return c[0] + 1, alive()

    lax.while_loop(cond, step, (jnp.int32(1), alive()))


def _host(comm, n_in, n_out):
    if comm is None:
        return [], [], [], [], [], lambda refs: (refs[:n_in], refs[n_in:n_in + n_out], refs[n_in + n_out:], None)
    in_specs, out_specs = comm.specs
    n_ci, n_co, n_sem = len(comm.ins), len(comm.out_shape), len(comm.scratch)

    def split(refs):
        ins, ci = refs[:n_in], refs[n_in:n_in + n_ci]
        rest = refs[n_in + n_ci:]
        outs, co = rest[:n_out], rest[n_out:n_out + n_co]
        sems, scratch = rest[n_out + n_co:n_out + n_co + n_sem], rest[n_out + n_co + n_sem:]
        return ins, outs, scratch, (ci, co, sems)

    return in_specs, out_specs, list(comm.out_shape), list(comm.scratch), list(comm.ins), split


def _sb_qkv_specs(s, g):
    groups = SB_HEADS // 2 // g
    return [pl.BlockSpec((g, SB_BLOCK, LANES), lambda p, i: (p, i, 0)),
            pl.BlockSpec((g, s, LANES), lambda p, i: (groups + p, 0, 0)),
            pl.BlockSpec((g, s, LANES), lambda p, i: (2 * groups + p, 0, 0))]


def _sb_fwd(qkv, s, comm=None):
    t = SB_BLOCK
    g = 2
    nq = s // t
    grid = (SB_HEADS // 2 // g, nq)
    c_in_specs, c_out_specs, c_out_shape, c_scratch, c_ins, split = _host(comm, 3, 2)

    def body(*refs):
        (q_ref, k_ref, v_ref), (o_ref, of_ref), (l_ref, acc_ref), riding = split(refs)
        i = pl.program_id(1)
        if comm is not None:
            first_step, last_step = _grid_ends(grid)
            pl.when(first_step)(lambda: comm.start(*riding))
        first, hmask, row, col = _sb_masks()
        after = jnp.where(row > col, 1.0, 0.0).astype(BF16)
        causal = col < row
        heads = [(p, h) for p in range(g) for h in range(2)]
        qh = {(p, h): q_ref[p] * hmask[h] for p, h in heads}
        l_ref[...] = jnp.zeros_like(l_ref)
        acc_ref[...] = jnp.zeros_like(acc_ref)

        def blocks(todo):
            chains = [(b, p, h) for b in range(len(todo)) for p, h in heads]
            starts = [pl.multiple_of(kb * t, t) for kb, _ in todo]
            ks = {(b, p): k_ref[p, pl.ds(st, t), :] for b, st in enumerate(starts) for p in range(g)}
            vs = {(b, p): v_ref[p, pl.ds(st, t), :] for b, st in enumerate(starts) for p in range(g)}
            ar = {(b, p, h): _sb_logits(qh[p, h], ks[b, p], todo[b][1]) for b, p, h in chains}
            later = {c: _split_mm(ar[c][1], after) for c in chains}
            carry = {(p, h): l_ref[2 * p + h] for p, h in heads}
            w = {}
            for b, (_, mask) in enumerate(todo):
                for p, h in heads:
                    wc = jnp.exp(ar[b, p, h][0] + later[b, p, h] + carry[p, h])
                    w[b, p, h] = wc if mask is None else jnp.where(mask, wc, 0.0)
                carry = {(p, h): carry[p, h] + jnp.sum(ar[b, p, h][1], axis=1, keepdims=True) for p, h in heads}
            pv = {(b, p, h): _dot(w[b, p, h].astype(BF16), vs[b, p], 1, 0) for b, p, h in chains}
            for p in range(g):
                lanes = slice(p * LANES, (p + 1) * LANES)
                acc = acc_ref[:, lanes]
                for b in range(len(todo)):
                    acc = acc + jnp.where(first, pv[b, p, 0], pv[b, p, 1])
                acc_ref[:, lanes] = acc
            for p, h in heads:
                l_ref[2 * p + h] = carry[p, h]

        _sb_walk(i, blocks, l_ref, causal)
        o_ref[...] = acc_ref[...].astype(o_ref.dtype)
        of_ref[...] = acc_ref[...]
        if comm is not None:
            pl.when(last_step)(lambda: comm.finish(*riding))

    blk = pl.BlockSpec((t, g * LANES), lambda p, i: (i, p))
    return pl.pallas_call(
        body, name="sb_fwd", grid=grid,
        in_specs=_sb_qkv_specs(s, g) + c_in_specs,
        out_specs=[blk, blk] + c_out_specs,
        out_shape=[jax.ShapeDtypeStruct((s, SB_WIDTH), BF16), jax.ShapeDtypeStruct((s, SB_WIDTH), F32)] + c_out_shape,
        scratch_shapes=c_scratch + [pltpu.VMEM((2 * g, t, 1), F32), pltpu.VMEM((t, g * LANES), F32)],
        compiler_params=_cparams(),
    )(qkv, qkv, qkv, *c_ins)


def _sb_bwd(qkv, o, do, s, comm=None):
    t = SB_BLOCK
    g = 2
    nq = s // t
    grid = (SB_HEADS // 2 // g, nq)
    c_in_specs, c_out_specs, c_out_shape, c_scratch, c_ins, split = _host(comm, 5, 3)

    def body(*refs):
        ((q_ref, k_ref, v_ref, o_ref, do_ref), (dq_ref, dk_ref, dv_ref),
         (l_ref, e_ref, dq_acc, dk_acc, dv_acc), riding) = split(refs)
        i = pl.program_id(1)
        if comm is not None:
            first_step, last_step = _grid_ends(grid)
            pl.when(first_step)(lambda: comm.start(*riding))
        first, hmask, row, col = _sb_masks()
        after = jnp.where(row > col, 1.0, 0.0).astype(BF16)
        from_here = jnp.where(row >= col, 1.0, 0.0).astype(BF16)
        causal = col < row

        @pl.when(i == 0)
        def _():
            dk_acc[...] = jnp.zeros_like(dk_acc)
            dv_acc[...] = jnp.zeros_like(dv_acc)

        heads = [(p, h) for p in range(g) for h in range(2)]
        lanes = [slice(p * LANES, (p + 1) * LANES) for p in range(g)]
        q = [q_ref[p] for p in range(g)]
        do_ = [do_ref[:, lanes[p]] for p in range(g)]
        qh = {(p, h): q[p] * hmask[h] for p, h in heads}
        doh = {(p, h): do_[p] * hmask[h] for p, h in heads}
        total = {}
        for p in range(g):
            prod = do_[p].astype(F32) * o_ref[:, lanes[p]]
            total[p, 0] = jnp.sum(jnp.where(first, prod, 0.0), axis=1, keepdims=True)
            total[p, 1] = jnp.sum(jnp.where(first, 0.0, prod), axis=1, keepdims=True)
        l_ref[...] = jnp.zeros_like(l_ref)
        e_ref[...] = jnp.zeros_like(e_ref)
        dq_acc[...] = jnp.zeros_like(dq_acc)

        def blocks(todo):
            chains = [(b, p, h) for b in range(len(todo)) for p, h in heads]
            starts = [pl.multiple_of(kb * t, t) for kb, _ in todo]
            ks = {(b, p): k_ref[p, pl.ds(st, t), :] for b, st in enumerate(starts) for p in range(g)}
            vs = {(b, p): v_ref[p, pl.ds(st, t), :] for b, st in enumerate(starts) for p in range(g)}
            ar = {(b, p, h): _sb_logits(qh[p, h], ks[b, p], todo[b][1]) for b, p, h in chains}
            dw = {(b, p, h): _dot(doh[p, h], vs[b, p], 1, 1) for b, p, h in chains}
            later = {c: _split_mm(ar[c][1], after) for c in chains}
            carry = {(p, h): l_ref[2 * p + h] for p, h in heads}
            wb = {}
            for b, (_, mask) in enumerate(todo):
                for p, h in heads:
                    wc = jnp.exp(ar[b, p, h][0] + later[b, p, h] + carry[p, h])
                    wb[b, p, h] = (wc if mask is None else jnp.where(mask, wc, 0.0)).astype(BF16)
                carry = {(p, h): carry[p, h] + jnp.sum(ar[b, p, h][1], axis=1, keepdims=True) for p, h in heads}
            dvs = {(b, p, h): _dot(wb[b, p, h], do_[p], 0, 0) for b, p, h in chains}
            e = {c: dw[c] * wb[c].astype(F32) for c in chains}
            suffix = {c: _split_mm(e[c], from_here) for c in chains}
            e_carry = {(p, h): e_ref[2 * p + h] for p, h in heads}
            dz = {}
            for b, (_, mask) in enumerate(todo):
                for p, h in heads:
                    before = total[p, h] - (suffix[b, p, h] + e_carry[p, h])
                    dzc = e[b, p, h] - jnp.exp(ar[b, p, h][0]) * (e[b, p, h] + before)
                    dz[b, p, h] = (dzc if mask is None else jnp.where(mask, dzc, 0.0)).astype(BF16)
                e_carry = {(p, h): e_carry[p, h] + jnp.sum(e[b, p, h], axis=1, keepdims=True) for p, h in heads}
            dqs = {(b, p, h): _dot(dz[b, p, h], ks[b, p], 1, 0) for b, p, h in chains}
            dks = {(b, p, h): _dot(dz[b, p, h], q[p], 0, 0) for b, p, h in chains}
            for p in range(g):
                dq = dq_acc[:, lanes[p]]
                for b, st in enumerate(starts):
                    dq = dq + jnp.where(first, dqs[b, p, 0], dqs[b, p, 1])
                    dk_acc[pl.ds(st, t), lanes[p]] += jnp.where(first, dks[b, p, 0], dks[b, p, 1])
                    dv_acc[pl.ds(st, t), lanes[p]] += jnp.where(first, dvs[b, p, 0], dvs[b, p, 1])
                dq_acc[:, lanes[p]] = dq
            for p, h in heads:
                l_ref[2 * p + h] = carry[p, h]
                e_ref[2 * p + h] = e_carry[p, h]

        _sb_walk(i, blocks, l_ref, causal)
        dq_ref[...] = (dq_acc[...] * SB_SCALE).astype(dq_ref.dtype)

        @pl.when(i == nq - 1)
        def _():
            dk_ref[...] = dk_acc[...].astype(dk_ref.dtype)
            dv_ref[...] = dv_acc[...].astype(dv_ref.dtype)

        if comm is not None:
            pl.when(last_step)(lambda: comm.finish(*riding))

    once = pl.Buffered(1)
    q_spec, k_spec, v_spec = _sb_qkv_specs(s, g)
    k_spec = pl.BlockSpec(k_spec.block_shape, k_spec.index_map, pipeline_mode=once)
    v_spec = pl.BlockSpec(v_spec.block_shape, v_spec.index_map, pipeline_mode=once)
    blk = pl.BlockSpec((t, g * LANES), lambda p, i: (i, p))
    col_blk = pl.BlockSpec((s, g * LANES), lambda p, i: (0, p), pipeline_mode=once)
    sds = jax.ShapeDtypeStruct((s, SB_WIDTH), BF16)
    return pl.pallas_call(
        body, name="sb_bwd", grid=grid,
        in_specs=[q_spec, k_spec, v_spec, blk, blk] + c_in_specs,
        out_specs=[blk, col_blk, col_blk] + c_out_specs,
        out_shape=[sds, sds, sds] + c_out_shape,
        scratch_shapes=c_scratch + [pltpu.VMEM((2 * g, t, 1), F32), pltpu.VMEM((2 * g, t, 1), F32),
                                    pltpu.VMEM((t, g * LANES), F32), pltpu.VMEM((s, g * LANES), F32),
                                    pltpu.VMEM((s, g * LANES), F32)],
        compiler_params=_cparams(),
    )(qkv, qkv, qkv, o, do, *c_ins)


def _ret_log_gamma():
    lg = np.log1p(-np.exp2(-5.0 - np.arange(RET_HEADS, dtype=np.float32))).astype(np.float32)
    return jnp.asarray(np.broadcast_to(lg[:, None, None], (RET_HEADS, 8, LANES)).copy())


RET_SCRATCH = [pltpu.VMEM((RET_HEADS, RET_QK, RET_V), F32),
               pltpu.VMEM((RET_HEADS, RET_BLOCK, RET_BLOCK), F32),
               pltpu.VMEM((RET_HEADS, RET_BLOCK, 1), F32),
               pltpu.VMEM((RET_HEADS, RET_BLOCK, 1), F32)]


def _ret_begin(n, lg_ref, state, within, q_dec, k_dec):
    @pl.when(n == 0)
    def _():
        c = RET_BLOCK
        state[...] = jnp.zeros_like(state)
        row = lax.broadcasted_iota(jnp.int32, (c, c), 0)
        col = lax.broadcasted_iota(jnp.int32, (c, c), 1)
        rel = jnp.maximum(row - col, 0).astype(F32)
        idx = lax.broadcasted_iota(jnp.int32, (c, 1), 0).astype(F32)
        for h in range(RET_HEADS):
            lg = lg_ref[h, 0:1, 0:1]
            within[h] = jnp.where(row >= col, jnp.exp(lg * rel), 0.0)
            q_dec[h] = jnp.exp(lg * (idx + 1.0))
            k_dec[h] = jnp.exp(lg * (c - 1.0 - idx))


def _chunk_decay(lg_ref, h):
    return jnp.exp(lg_ref[h, 0:1, 0:1] * float(RET_BLOCK))


def _ret_heads(x, width):
    return [x[:, h * width:(h + 1) * width] for h in range(RET_HEADS)]


def _ret_specs(s, reverse=False):
    c = RET_BLOCK
    per_step = min(RET_CHUNKS_PER_STEP, s // c)
    rows = c * per_step
    nc = s // rows
    pos = (lambda n: nc - 1 - n) if reverse else (lambda n: n)
    chunks = [slice(u * c, (u + 1) * c) for u in range(per_step)]
    q_spec = pl.BlockSpec((rows, RET_QK_WIDTH), lambda n: (pos(n), 0))
    k_spec = pl.BlockSpec((rows, RET_QK_WIDTH), lambda n: (pos(n), 1))
    v_spec = pl.BlockSpec((rows, RET_V_WIDTH), lambda n: (pos(n), 0))
    lg_spec = pl.BlockSpec((RET_HEADS, 8, LANES), lambda n: (0, 0, 0))
    rope_spec = pl.BlockSpec((rows, RET_QK), lambda n: (pos(n), 0))
    return nc, chunks[::-1] if reverse else chunks, q_spec, k_spec, v_spec, lg_spec, rope_spec


def _ret_fwd(rqk, rvg, s):
    nc, chunks, q_spec, k_spec, v_spec, lg_spec, _ = _ret_specs(s)
    g_spec = pl.BlockSpec(v_spec.block_shape, lambda n: (n, 1))
    heads = range(RET_HEADS)

    def body(q_ref, k_ref, v_ref, g_ref, lg_ref, r_ref, y_ref, state, within, q_dec, k_dec):
        n = pl.program_id(0)
        _ret_begin(n, lg_ref, state, within, q_dec, k_dec)
        for rows in chunks:
            q, k = _ret_heads(q_ref[rows], RET_QK), _ret_heads(k_ref[rows], RET_QK)
            v, g = _ret_heads(v_ref[rows], RET_V), _ret_heads(g_ref[rows], RET_V)
            scores = [_dot(q[h].astype(BF16), k[h].astype(BF16), 1, 1) * within[h] for h in heads]
            cross = [_dot((q[h] * q_dec[h]).astype(BF16), state[h].astype(BF16), 1, 0) for h in heads]
            out = [_dot(scores[h].astype(BF16), v[h], 1, 0) + cross[h] for h in heads]
            grown = [_dot((k[h] * k_dec[h]).astype(BF16), v[h], 0, 0) for h in heads]
            for h in heads:
                sl = slice(h * RET_V, (h + 1) * RET_V)
                r_ref[rows, sl] = out[h]
                xhat, _ = _norm(out[h])
                gh = g[h].astype(F32)
                y_ref[rows, sl] = (gh * _sigmoid(gh) * xhat).astype(y_ref.dtype)
                state[h] = state[h] * _chunk_decay(lg_ref, h) + grown[h]

    return pl.pallas_call(
        body, name="ret_fwd", grid=(nc,),
        in_specs=[q_spec, k_spec, v_spec, g_spec, lg_spec],
        out_specs=[v_spec, v_spec],
        out_shape=[jax.ShapeDtypeStruct((s, RET_V_WIDTH), F32), jax.ShapeDtypeStruct((s, RET_V_WIDTH), BF16)],
        scratch_shapes=RET_SCRATCH,
        compiler_params=_cparams(),
    )(rqk, rqk, rvg, rvg, _ret_log_gamma())


def _rope_bwd(d, cos, sin):
    return d * cos + _swap_halves(d * sin)


def _ret_bwd_q(rqk, rv, d_out, cos2, sin2, s):
    nc, chunks, q_spec, k_spec, v_spec, lg_spec, rope_spec = _ret_specs(s)
    heads = range(RET_HEADS)

    def body(k_ref, v_ref, d_ref, lg_ref, cos_ref, sin_ref, dq_ref, state, within, q_dec, k_dec):
        n = pl.program_id(0)
        _ret_begin(n, lg_ref, state, within, q_dec, k_dec)
        for rows in chunks:
            k = _ret_heads(k_ref[rows], RET_QK)
            v, d = _ret_heads(v_ref[rows], RET_V), _ret_heads(d_ref[rows], RET_V)
            cos, sin = cos_ref[rows], sin_ref[rows]
            d_scores = [_dot(d[h], v[h], 1, 1) * within[h] for h in heads]
            cross = [q_dec[h] * _dot(d[h], state[h].astype(BF16), 1, 1) for h in heads]
            dq = [_dot(d_scores[h].astype(BF16), k[h].astype(BF16), 1, 0) + cross[h] for h in heads]
            grown = [_dot((k[h] * k_dec[h]).astype(BF16), v[h], 0, 0) for h in heads]
            for h in heads:
                sl = slice(h * RET_QK, (h + 1) * RET_QK)
                dq_ref[rows, sl] = (_rope_bwd(dq[h], cos, sin) * RET_SCALE).astype(dq_ref.dtype)
                state[h] = state[h] * _chunk_decay(lg_ref, h) + grown[h]

    return pl.pallas_call(
        body, name="ret_bwd_q", grid=(nc,),
        in_specs=[k_spec, v_spec, v_spec, lg_spec, rope_spec, rope_spec],
        out_specs=q_spec,
        out_shape=jax.ShapeDtypeStruct((s, RET_QK_WIDTH), BF16),
        scratch_shapes=RET_SCRATCH,
        compiler_params=_cparams(),
    )(rqk, rv, d_out, _ret_log_gamma(), cos2, sin2)


def _ret_bwd_kv(rqk, rv, d_out, cos2, sin2, s):
    nc, chunks, q_spec, k_spec, v_spec, lg_spec, rope_spec = _ret_specs(s, reverse=True)
    heads = range(RET_HEADS)

    def body(q_ref, k_ref, v_ref, d_ref, lg_ref, cos_ref, sin_ref, dk_ref, dv_ref, state, within, q_dec, k_dec):
        n = pl.program_id(0)
        _ret_begin(n, lg_ref, state, within, q_dec, k_dec)
        for rows in chunks:
            q, k = _ret_heads(q_ref[rows], RET_QK), _ret_heads(k_ref[rows], RET_QK)
            v, d = _ret_heads(v_ref[rows], RET_V), _ret_heads(d_ref[rows], RET_V)
            cos, sin = cos_ref[rows], sin_ref[rows]
            qb, kb = [q[h].astype(BF16) for h in heads], [k[h].astype(BF16) for h in heads]
            st = [state[h].astype(BF16) for h in heads]
            scores = [_dot(qb[h], kb[h], 1, 1) * within[h] for h in heads]
            d_scores = [_dot(d[h], v[h], 1, 1) * within[h] for h in heads]
            dk = [_dot(d_scores[h].astype(BF16), qb[h], 0, 0) + k_dec[h] * _dot(v[h], st[h], 1, 1) for h in heads]
            dv = [_dot(scores[h].astype(BF16), d[h], 0, 0) + k_dec[h] * _dot(kb[h], st[h], 1, 0) for h in heads]
            grown = [_dot((q[h] * q_dec[h]).astype(BF16), d[h], 0, 0) for h in heads]
            for h in heads:
                dk_ref[rows, h * RET_QK:(h + 1) * RET_QK] = _rope_bwd(dk[h], cos, sin).astype(dk_ref.dtype)
                dv_ref[rows, h * RET_V:(h + 1) * RET_V] = dv[h].astype(dv_ref.dtype)
                state[h] = state[h] * _chunk_decay(lg_ref, h) + grown[h]

    return pl.pallas_call(
        body, name="ret_bwd_kv", grid=(nc,),
        in_specs=[q_spec, k_spec, v_spec, v_spec, lg_spec, rope_spec, rope_spec],
        out_specs=[q_spec, v_spec],
        out_shape=[jax.ShapeDtypeStruct((s, RET_QK_WIDTH), BF16), jax.ShapeDtypeStruct((s, RET_V_WIDTH), BF16)],
        scratch_shapes=RET_SCRATCH,
        compiler_params=_cparams(),
    )(rqk, rqk, rv, d_out, _ret_log_gamma(), cos2, sin2)


def _xattn_probs(scores):
    sc = scores - jnp.max(scores, axis=-1, keepdims=True)
    p = jnp.exp(sc)
    return p / jnp.sum(p, axis=-1, keepdims=True)


def _xattn_heads(q_ref, kv_ref):
    sls = [slice(h * MEM_DIM, (h + 1) * MEM_DIM) for h in range(MEM_HEADS)]
    q = [q_ref[:, sl] for sl in sls]
    k = [kv_ref[:, sl] for sl in sls]
    v = [kv_ref[:, D_MODEL + h * MEM_DIM:D_MODEL + (h + 1) * MEM_DIM] for h in range(MEM_HEADS)]
    return sls, q, k, v


def _xattn_fwd(qm, kv, s):
    tq = min(XATTN_ROWS, s)
    heads = range(MEM_HEADS)

    def body(q_ref, kv_ref, o_ref):
        sls, q, k, v = _xattn_heads(q_ref, kv_ref)
        scores = [_dot(q[h], k[h], 1, 1) for h in heads]
        p = [_xattn_probs(scores[h]).astype(BF16) for h in heads]
        out = [_dot(p[h], v[h], 1, 0) for h in heads]
        for h in heads:
            o_ref[:, sls[h]] = out[h].astype(o_ref.dtype)

    return pl.pallas_call(
        body, name="xattn_fwd", grid=(s // tq,),
        in_specs=[pl.BlockSpec((tq, D_MODEL), lambda i: (i, 0)),
                  pl.BlockSpec((MEM_LEN, 2 * D_MODEL), lambda i: (0, 0))],
        out_specs=pl.BlockSpec((tq, D_MODEL), lambda i: (i, 0)),
        out_shape=jax.ShapeDtypeStruct((s, D_MODEL), BF16),
        compiler_params=_cparams(),
    )(qm, kv)


def _xattn_bwd(qm, kv, do, s):
    tq = min(XATTN_ROWS, s)

    def body(q_ref, kv_ref, do_ref, dq_ref, dkv_ref):
        i = pl.program_id(0)

        @pl.when(i == 0)
        def _():
            dkv_ref[...] = jnp.zeros_like(dkv_ref)

        heads = range(MEM_HEADS)
        sls, q, k, v = _xattn_heads(q_ref, kv_ref)
        d = [do_ref[:, sl] for sl in sls]
        scores = [_dot(q[h], k[h], 1, 1) for h in heads]
        dp = [_dot(d[h], v[h], 1, 1) for h in heads]
        p = [_xattn_probs(scores[h]) for h in heads]
        ds = [(p[h] * (dp[h] - jnp.sum(p[h] * dp[h], axis=-1, keepdims=True))).astype(BF16) for h in heads]
        dq = [_dot(ds[h], k[h], 1, 0) for h in heads]
        dk = [_dot(ds[h], q[h], 0, 0) for h in heads]
        dv = [_dot(p[h].astype(BF16), d[h], 0, 0) for h in heads]
        for h in heads:
            dq_ref[:, sls[h]] = (dq[h] * MEM_SCALE).astype(dq_ref.dtype)
            dkv_ref[:, sls[h]] += dk[h]
            dkv_ref[:, D_MODEL + h * MEM_DIM:D_MODEL + (h + 1) * MEM_DIM] += dv[h]

    row_blk = pl.BlockSpec((tq, D_MODEL), lambda i: (i, 0))
    kv_blk = pl.BlockSpec((MEM_LEN, 2 * D_MODEL), lambda i: (0, 0))
    return pl.pallas_call(
        body, name="xattn_bwd", grid=(s // tq,),
        in_specs=[row_blk, kv_blk, row_blk],
        out_specs=[row_blk, kv_blk],
        out_shape=[jax.ShapeDtypeStruct((s, D_MODEL), BF16), jax.ShapeDtypeStruct((MEM_LEN, 2 * D_MODEL), F32)],
        compiler_params=_cparams(),
    )(qm, kv, do)


def _place():
    x, y, c = lax.axis_index("x"), lax.axis_index("y"), lax.axis_index("c")
    others = [(1 - x, y), (x, 1 - y), (1 - x, 1 - y)]
    return x, y, c, others


def _slab(ref, axis, chip, size):
    start = pl.multiple_of(chip * size, LANES if axis == 1 else 16)
    if axis == 0:
        return ref.at[pl.ds(start, size), :]
    return ref.at[:, pl.ds(start, size)]


class _CommPlan:
    def __init__(self, ins, out_shape, scratch, start, finish):
        self.ins, self.out_shape, self.scratch, self.start, self.finish = ins, out_shape, scratch, start, finish

    @property
    def specs(self):
        any_spec = pl.BlockSpec(memory_space=pl.ANY)
        return [any_spec] * len(self.ins), [any_spec] * len(self.out_shape)


def _gather_plan(names, shards):
    spec = {name: (shape, axis) for name, shape, axis in BIG}
    nw = len(names)

    def shard_half(ref, c):
        rows = ref.shape[0] // 2
        return ref.at[pl.ds(pl.multiple_of(c * rows, 16), rows), :]

    def region(ref, w, chip, c):
        shape, axis = spec[names[w]]
        size = shape[axis] // N_CHIPS
        if axis == 0:
            rows = size // 2
            return ref.at[pl.ds(pl.multiple_of(chip * size + c * rows, 16), rows), :]
        rows = shape[0] // 2
        return ref.at[pl.ds(pl.multiple_of(c * rows, 16), rows), pl.ds(pl.multiple_of(chip * size, LANES), size)]

    def ops(shard, full, sems):
        ici_send, ici_recv, d2d_send, d2d_recv, local_sems = sems
        x, y, c, others = _place()
        mine, sibling = 2 * x + y, (x, y, 1 - c)
        local, over_ici, arrived, passed_on, from_sibling = [], [], [], [], []
        for w in range(nw):
            shape, axis = spec[names[w]]
            local.append(pltpu.make_async_copy(shard[w], _slab(full[w], axis, mine, shape[axis] // N_CHIPS),
                                               local_sems.at[w]))
            for t, (qx, qy) in enumerate(others):
                n, theirs = 3 * w + t, 2 * qx + qy
                over_ici.append(pltpu.make_async_remote_copy(
                    src_ref=shard_half(shard[w], c), dst_ref=region(full[w], w, mine, c),
                    send_sem=ici_send.at[n], recv_sem=ici_recv.at[n], device_id=(qx, qy, c), device_id_type=MESH))
                arrived.append(pltpu.make_async_remote_copy(
                    src_ref=shard_half(shard[w], c), dst_ref=region(full[w], w, theirs, c),
                    send_sem=ici_send.at[n], recv_sem=ici_recv.at[n], device_id=(qx, qy, c), device_id_type=MESH))
                passed_on.append(pltpu.make_async_remote_copy(
                    src_ref=region(full[w], w, theirs, c), dst_ref=region(full[w], w, theirs, c),
                    send_sem=d2d_send.at[n], recv_sem=d2d_recv.at[n], device_id=sibling, device_id_type=MESH))
                from_sibling.append(pltpu.make_async_remote_copy(
                    src_ref=region(full[w], w, theirs, c), dst_ref=region(full[w], w, theirs, 1 - c),
                    send_sem=d2d_send.at[n], recv_sem=d2d_recv.at[n], device_id=sibling, device_id_type=MESH))
        return local, over_ici, arrived, passed_on, from_sibling

    def start(shard, full, sems):
        local, over_ici, _, _, _ = ops(shard, full, sems)
        for cp in local + over_ici:
            cp.start()

    def finish(shard, full, sems):
        local, over_ici, arrived, passed_on, from_sibling = ops(shard, full, sems)
        for got, onward in zip(arrived, passed_on, strict=True):
            got.wait_recv()
            onward.start()
        for got in from_sibling:
            got.wait_recv()
        for cp in over_ici + passed_on:
            cp.wait_send()
        for cp in local:
            cp.wait()

    dma = pltpu.SemaphoreType.DMA
    return _CommPlan(
        ins=[shards[name] for name in names],
        out_shape=[jax.ShapeDtypeStruct(spec[name][0], BF16) for name in names],
        scratch=[dma((3 * nw,)), dma((3 * nw,)), dma((3 * nw,)), dma((3 * nw,)), dma((nw,))],
        start=start, finish=finish)


def _shard_shape(shape, axis):
    return tuple(d // N_CHIPS if a == axis else d for a, d in enumerate(shape))


def _exchange_plan(names, grads):
    spec = {name: (shape, axis) for name, shape, axis in BIG}
    nw = len(names)

    def ops(grad, stack, sems):
        send_sems, recv_sems, local_sems = sems
        x, y, c, others = _place()
        mine = 2 * x + y
        me, sibling = (x, y, c), (x, y, 1 - c)

        def dev(px, py, pc):
            return 4 * px + 2 * py + pc

        def copy(w, n, src, slot, to):
            return pltpu.make_async_remote_copy(
                src_ref=src, dst_ref=stack[w].at[slot], send_sem=send_sems.at[7 * w + n],
                recv_sem=recv_sems.at[7 * w + n], device_id=to, device_id_type=MESH)

        local, first, arrived, passed_on, from_sibling = [], [], [], [], []
        for w in range(nw):
            shape, axis = spec[names[w]]
            size = shape[axis] // N_CHIPS
            own = _slab(grad[w], axis, mine, size)
            local.append(pltpu.make_async_copy(own, stack[w].at[dev(*me)], local_sems.at[w]))
            first.append(copy(w, 0, own, dev(*me), sibling))
            from_sibling.append(copy(w, 0, own, dev(*sibling), me))
            for t, (qx, qy) in enumerate(others):
                got = stack[w].at[dev(qx, qy, c)]
                first.append(copy(w, 1 + t, _slab(grad[w], axis, 2 * qx + qy, size), dev(*me), (qx, qy, c)))
                arrived.append(copy(w, 1 + t, got, dev(qx, qy, c), me))
                passed_on.append(copy(w, 4 + t, got, dev(qx, qy, c), sibling))
                from_sibling.append(copy(w, 4 + t, got, dev(qx, qy, 1 - c), me))
        return local, first, arrived, passed_on, from_sibling

    def start(grad, stack, sems):
        local, first, _, _, _ = ops(grad, stack, sems)
        for cp in local + first:
            cp.start()

    def finish(grad, stack, sems):
        local, first, arrived, passed_on, from_sibling = ops(grad, stack, sems)
        for got, onward in zip(arrived, passed_on, strict=True):
            got.wait_recv()
            onward.start()
        for got in from_sibling:
            got.wait_recv()
        for cp in first + passed_on:
            cp.wait_send()
        for cp in local:
            cp.wait()

    dma = pltpu.SemaphoreType.DMA
    return _CommPlan(
        ins=[grads[name] for name in names],
        out_shape=[jax.ShapeDtypeStruct((N_DEV,) + _shard_shape(*spec[name]), BF16) for name in names],
        scratch=[dma((7 * nw,)), dma((7 * nw,)), dma((nw,))],
        start=start, finish=finish)


def _adamw(w, g, m, v):
    m = ADAM_B1 * m + (1.0 - ADAM_B1) * g
    v = ADAM_B2 * v + (1.0 - ADAM_B2) * (g * g)
    m_hat = m / (1.0 - ADAM_B1 ** ADAM_STEP)
    v_hat = v / (1.0 - ADAM_B2 ** ADAM_STEP)
    delta = -ADAM_LR * (m_hat / (jnp.sqrt(v_hat) + ADAM_EPS) + ADAM_WD * w)
    return delta, m, v


def _reduce_adamw(name, stack, w, m, v):
    rows, cols = w.shape
    tr = next(t for t in (256, 128, 64) if rows % t == 0)

    def body(s_ref, w_ref, m_ref, v_ref, g_ref, d_ref, nm_ref, nv_ref):
        g = s_ref[0].astype(F32)
        for d in range(1, N_DEV):
            g = g + s_ref[d].astype(F32)
        g_ref[...] = g
        d_ref[...], nm_ref[...], nv_ref[...] = _adamw(w_ref[...], g, m_ref[...], v_ref[...])

    blk = pl.BlockSpec((tr, cols), lambda i: (i, 0))
    return pl.pallas_call(
        body, name=name, grid=(rows // tr,),
        in_specs=[pl.BlockSpec((N_DEV, tr, cols), lambda i: (0, i, 0)), blk, blk, blk],
        out_specs=[blk] * 4, out_shape=[jax.ShapeDtypeStruct((rows, cols), F32)] * 4,
        compiler_params=_cparams(),
    )(stack, w, m, v)


def _small_step(pack, w, m, v):
    def body(p_ref, w_ref, m_ref, v_ref, g_ref, d_ref, nm_ref, nv_ref, loss_ref, all_ref, send_sems, recv_sems):
        x, y, c, _ = _place()
        me = 4 * x + 2 * y + c
        all_ref[me] = p_ref[...]
        sent = []
        for n in range(1, N_DEV):
            peer = me ^ n
            cp = pltpu.make_async_remote_copy(
                src_ref=p_ref, dst_ref=all_ref.at[me], send_sem=send_sems.at[n - 1], recv_sem=recv_sems.at[n - 1],
                device_id=(peer // 4, (peer // 2) % 2, peer % 2), device_id_type=MESH)
            cp.start()
            sent.append(cp)
        for n in range(1, N_DEV):
            peer = me ^ n
            pltpu.make_async_remote_copy(
                src_ref=p_ref, dst_ref=all_ref.at[peer], send_sem=send_sems.at[n - 1], recv_sem=recv_sems.at[n - 1],
                device_id=(peer // 4, (peer // 2) % 2, peer % 2), device_id_type=MESH).wait_recv()
        for cp in sent:
            cp.wait_send()
        tot = all_ref[0]
        for d in range(1, N_DEV):
            tot = tot + all_ref[d]
        g = tot[:SMALL_ROWS]
        g_ref[...] = g
        d_ref[...], nm_ref[...], nv_ref[...] = _adamw(w_ref[...], g, m_ref[...], v_ref[...])
        loss_ref[...] = jnp.sum(jnp.sum(tot[SMALL_ROWS:], axis=1, keepdims=True), axis=0, keepdims=True)

    vm = pl.BlockSpec(memory_space=pltpu.VMEM)
    small = jax.ShapeDtypeStruct((SMALL_ROWS, LANES), F32)
    return pl.pallas_call(
        body, name="small_step",
        in_specs=[vm] * 4, out_specs=[vm] * 5,
        out_shape=[small] * 4 + [jax.ShapeDtypeStruct((1, 1), F32)],
        scratch_shapes=[pltpu.VMEM((N_DEV, PACK_ROWS, LANES), F32),
                        pltpu.SemaphoreType.DMA((N_DEV - 1,)), pltpu.SemaphoreType.DMA((N_DEV - 1,))],
    )(pack, w, m, v)


LATER_WEIGHTS = tuple(name for name, _, _ in BIG if name != "w_in")


def _layer_step(x, mem, tgt, shards, vec):
    s = x.shape[0]
    d = D_MODEL
    tm = min(ROW_TILE, s)
    tl = min(WIDE_TILE, s)
    xb, cos2, sin2, w_in = _prep(x, _gather_plan(("w_in",), shards))
    bf = lambda w: ((s, w), BF16)
    f32 = lambda w: ((s, w), F32)

    w_sb, w_rqk = w_in[:, :OFF_RET_Q], w_in[:, OFF_RET_Q:OFF_RET_V]
    w_rvg, w_gate = w_in[:, OFF_RET_V:OFF_GATE], w_in[:, OFF_GATE:]
    q_scale = lambda width, q_width, scale: jnp.concatenate(
        [jnp.full((1, q_width), scale, F32), jnp.ones((1, width - q_width), F32)], axis=1)
    n_groups = 3 * SB_WIDTH // LANES

    def sb_epi(acc, t, i, j):
        scaled = acc * t[0]
        return [jnp.stack([scaled[:, g * LANES:(g + 1) * LANES] for g in range(n_groups)])], []

    (sb_qkv,) = _mm(
        "in_sb", xb, w_sb, s, 3 * SB_WIDTH, d, tm=tl, tn=3 * SB_WIDTH, tk=d, epi=sb_epi,
        ins=[(q_scale(3 * SB_WIDTH, SB_WIDTH, SB_SCALE), *_rowvec(3 * SB_WIDTH))],
        outs=[((n_groups, s, LANES), BF16, (n_groups, tl, LANES), lambda i, j: (0, i, 0))])

    def rope_epi(acc, t, i, j):
        cos, sin, scale = t
        parts = []
        for g in range(acc.shape[1] // RET_QK):
            xg = acc[:, g * RET_QK:(g + 1) * RET_QK]
            parts.append(xg * cos + _swap_halves(xg) * sin)
        return [jnp.concatenate(parts, axis=1) * scale], []

    rope_in = ((tl, RET_QK), lambda i, j: (i, 0))
    (rqk,) = _mm("in_rqk", xb, w_rqk, s, 2 * RET_QK_WIDTH, d, tm=tl, tn=2 * RET_QK_WIDTH, tk=d, epi=rope_epi,
                 chunk=MXU_COLS,
                 ins=[(cos2, *rope_in), (sin2, *rope_in),
                      (q_scale(2 * RET_QK_WIDTH, RET_QK_WIDTH, RET_SCALE), *_rowvec(2 * RET_QK_WIDTH))],
                 outs=[(*f32(2 * RET_QK_WIDTH), *_tile(tl, 2 * RET_QK_WIDTH))])
    (rvg,) = _mm("in_rvg", xb, w_rvg, s, 2 * RET_V_WIDTH, d, tm=tl, tn=2 * RET_V_WIDTH, tk=d, chunk=MXU_COLS,
                 epi=_plain, outs=[(*bf(2 * RET_V_WIDTH), *_tile(tl, 2 * RET_V_WIDTH))])
    (gates,) = _mm("in_gate", xb, w_gate, s, 2 * d, d, tm=tl, tn=2 * d, tk=d, chunk=MXU_COLS,
                   epi=lambda acc, t, i, j: ([_sigmoid(acc + t[0])], []),
                   ins=[(vec["b_gate"], *_rowvec(2 * d))], outs=[(*bf(2 * d), *_tile(tl, 2 * d))])

    sb_out, sb_out_f32, *gathered = _sb_fwd(sb_qkv, s, comm=_gather_plan(LATER_WEIGHTS, shards))
    wt = dict(zip(LATER_WEIGHTS, gathered, strict=True))
    ret, gated = _ret_fwd(rqk, rvg, s)
    (y_sb,) = _mm("sb_o", sb_out, wt["w_sb_o"], s, d, SB_WIDTH, tm=tl, tn=d, tk=SB_WIDTH, epi=_plain,
                  outs=[(*bf(d), *_tile(tl, d))])
    y_ret, mixin = _mm(
        "ret_o", gated, wt["w_ret_o"], s, d, RET_V_WIDTH, tm=tl, tn=d, tk=RET_V_WIDTH, chunk=MXU_COLS,
        epi=lambda acc, t, i, j: ([acc, t[0].astype(F32) * t[2].astype(F32) + t[1].astype(F32) * acc], []),
        ins=[(gates, *_tile(tl, d)), (gates, *_tile(tl, d, 1)), (y_sb, *_tile(tl, d))],
        outs=[(*bf(d), *_tile(tl, d)), (*bf(d), *_tile(tl, d))])

    def ln_epi(acc, t, i, j):
        *res, g, b = t
        prev = res[0] if len(res) == 1 else res[0] * res[1] + res[2]
        xhat, rstd = _norm(DN_ALPHA * prev + acc)
        return [xhat * g + b, xhat, rstd], []

    full = _tile(tm, d)
    col1 = ((tm, 1), lambda i, j: (i, 0))
    vec_in = lambda name: (vec[name], *_rowvec(d))
    ln_outs = [(*bf(d), *full), (*f32(d), *full), ((s, 1), F32, *col1)]
    x1b, xhat1, rstd1 = _mm(
        "mix_o", mixin, wt["w_mix_o"], s, d, d, tm=tm, tn=d, tk=d, epi=ln_epi,
        ins=[(x, *full), vec_in("ln1_g"), vec_in("ln1_b")], outs=ln_outs)

    (qm,) = _mm("mem_q", x1b, wt["w_mem_q"], s, d, d, tm=tl, tn=d, tk=d,
                epi=lambda acc, t, i, j: ([acc * MEM_SCALE], []), outs=[(*bf(d), *_tile(tl, d))])
    (kv,) = _mm("mem_kv", mem, wt["w_mem_kv"], MEM_LEN, 2 * d, d, tm=MEM_LEN, tn=d, tk=d, epi=_plain,
                outs=[((MEM_LEN, 2 * d), BF16, *_tile(MEM_LEN, d))])
    att = _xattn_fwd(qm, kv, s)
    x2b, xhat2, rstd2 = _mm(
        "mem_o", att, wt["w_mem_o"], s, d, d, tm=tm, tn=d, tk=d, epi=ln_epi,
        ins=[(xhat1, *full), vec_in("ln1_g"), vec_in("ln1_b"), vec_in("ln2_g"), vec_in("ln2_b")], outs=ln_outs)

    fh = FFN_HIDDEN
    tf = fh // 2
    (f1,) = _mm("ffn_in1", x2b, wt["w_ffn_in"], s, fh, d, tm=tl, tn=tf, tk=d, epi=_plain, j_outer=True,
                outs=[(*bf(fh), *_tile(tl, tf))])

    def swiglu_epi(acc, t, i, j):
        a = t[0].astype(F32)
        return [acc, a * _sigmoid(a) * acc], []

    f2, act = _mm(
        "ffn_in2", x2b, wt["w_ffn_in"], s, fh, d, tm=tm, tn=fh, tk=d, b_off=(0, 1), epi=swiglu_epi, chunk=MXU_COLS,
        ins=[(f1, *_tile(tm, fh))], outs=[(*bf(fh), *_tile(tm, fh)), (*bf(fh), *_tile(tm, fh))])

    def head_epi(acc, t, i, j):
        prev_hat, prev_g, prev_b, g, b, target = t
        xhat, rstd = _norm(DN_ALPHA * (prev_hat * prev_g + prev_b) + acc)
        err = xhat * g + b - target
        dy = err * (1.0 / d)
        du = _norm_bwd(dy * g, xhat, rstd)
        return [du], [_colsum(dy * xhat), _colsum(dy), _colsum(err * err) * (0.5 / d)]

    vec_acc = ((1, d), F32)
    du3b, dg3, db3, loss_cols = _mm(
        "ffn_out", act, wt["w_ffn_out"], s, d, fh, tm=tm, tn=d, tk=fh, epi=head_epi,
        ins=[(xhat2, *full), vec_in("ln2_g"), vec_in("ln2_b"), vec_in("ln3_g"), vec_in("ln3_b"), (tgt, *full)],
        outs=[(*bf(d), *full)], accs=[vec_acc] * 3)

    grads = {}
    ts = min(SEQ_TILE, s)

    def wgrad(name, a, b, m, n, tm_, tn_, tk_=None):
        (g,) = _mm(name, a, b, m, n, a.shape[0], tm=tm_, tn=tn_, tk=tk_ or ts, ta=True, epi=_plain,
                   outs=[((m, n), BF16, *_tile(tm_, tn_))])
        return g

    def ffn_bwd_epi(acc, t, i, j):
        a, b = t[0].astype(F32), t[1].astype(F32)
        sg = _sigmoid(a)
        return [acc * b * (sg * (1.0 + a * (1.0 - sg))), acc * (a * sg)], []

    df1, df2 = _mm(
        "ffn_out_t", du3b, wt["w_ffn_out"], s, fh, d, tm=tm, tn=fh, tk=d, tb=True, epi=ffn_bwd_epi, chunk=MXU_COLS,
        ins=[(f1, *_tile(tm, fh)), (f2, *_tile(tm, fh))],
        outs=[(*bf(fh), *_tile(tm, fh)), (*bf(fh), *_tile(tm, fh))])
    grads["w_ffn_out"] = wgrad("g_ffn_out", act, du3b, fh, d, tf, d)
    grads["w_ffn_in"] = jnp.concatenate(
        [wgrad("g_ffn_in1", x2b, df1, d, fh, d, tf), wgrad("g_ffn_in2", x2b, df2, d, fh, d, tf)], axis=1)
    (dx2a,) = _mm("ffn_in1_t", df1, wt["w_ffn_in"], s, d, fh, tm=tl, tn=d, tk=fh, tb=True, epi=_plain,
                  outs=[(*f32(d), *_tile(tl, d))])

    def ln_bwd(name, a, b, k, tk, b_off, more, scales, xhat, rstd, g):
        def epi(acc, t, i, j):
            *extra, xh, rs, gg = t
            dy = acc
            for e, sc in zip(extra, scales, strict=True):
                dy = dy + e.astype(F32) * sc
            return [_norm_bwd(dy * gg, xh, rs)], [_colsum(dy * xh), _colsum(dy)]

        return _mm(name, a, b, s, d, k, tm=tm, tn=d, tk=tk, tb=True, b_off=b_off, epi=epi,
                   ins=[(e, *full) for e in more] + [(xhat, *full), (rstd, *col1), (g, *_rowvec(d))],
                   outs=[(*bf(d), *full)], accs=[vec_acc] * 2)

    du2b, dg2, db2 = ln_bwd("ffn_in2_t", df2, wt["w_ffn_in"], fh, fh, (0, 1), [dx2a, du3b], [1.0, DN_ALPHA],
                            xhat2, rstd2, vec["ln2_g"])

    (datt,) = _mm("mem_o_t", du2b, wt["w_mem_o"], s, d, d, tm=tl, tn=d, tk=d, tb=True, epi=_plain,
                  outs=[(*bf(d), *_tile(tl, d))])
    grads["w_mem_o"] = wgrad("g_mem_o", att, du2b, d, d, d, d)
    dqm, dkv = _xattn_bwd(qm, kv, datt, s)
    grads["w_mem_q"] = wgrad("g_mem_q", x1b, dqm, d, d, d, d)
    grads["w_mem_kv"] = wgrad("g_mem_kv", mem, dkv, d, 2 * d, d, d, MEM_LEN)
    du1b, dg1, db1 = ln_bwd("mem_q_t", dqm, wt["w_mem_q"], d, d, (0, 0), [du2b], [DN_ALPHA],
                            xhat1, rstd1, vec["ln1_g"])

    def merge_bwd_epi(acc, t, i, j):
        g0, g1, ysb, yret = (v.astype(F32) for v in t)
        dgate0 = acc * ysb * (g0 * (1.0 - g0))
        dgate1 = acc * yret * (g1 * (1.0 - g1))
        return [dgate0, dgate1, acc * g0, acc * g1], [_colsum(dgate0), _colsum(dgate1)]

    dgate0, dgate1, dy_sb, dy_ret, dbg0, dbg1 = _mm(
        "mix_o_t", du1b, wt["w_mix_o"], s, d, d, tm=tm, tn=d, tk=d, tb=True, epi=merge_bwd_epi,
        ins=[(gates, *full), (gates, *_tile(tm, d, 1)), (y_sb, *full), (y_ret, *full)],
        outs=[(*bf(d), *full)] * 4, accs=[vec_acc] * 2)
    grads["w_mix_o"] = wgrad("g_mix_o", mixin, du1b, d, d, d, d)
    grads["w_sb_o"] = wgrad("g_sb_o", sb_out, dy_sb, SB_WIDTH, d, SB_WIDTH, d)
    grads["w_ret_o"] = wgrad("g_ret_o", gated, dy_ret, RET_V_WIDTH, d, RET_V_WIDTH, d)
    (dsb_out,) = _mm("sb_o_t", dy_sb, wt["w_sb_o"], s, SB_WIDTH, d, tm=tl, tn=SB_WIDTH, tk=d, tb=True, epi=_plain,
                     outs=[(*bf(SB_WIDTH), *_tile(tl, SB_WIDTH))])

    def gate_norm_bwd_epi(acc, t, i, j):
        r, g = t[0], t[1].astype(F32)
        drg, dret = [], []
        for h in range(acc.shape[1] // RET_V):
            sl = slice(h * RET_V, (h + 1) * RET_V)
            xhat, rstd = _norm(r[:, sl])
            gg, dd = g[:, sl], acc[:, sl]
            sg = _sigmoid(gg)
            drg.append(dd * xhat * (sg * (1.0 + gg * (1.0 - sg))))
            dret.append(_norm_bwd(dd * (gg * sg), xhat, rstd))
        return [jnp.concatenate(drg, axis=1), jnp.concatenate(dret, axis=1)], []

    drg, dret = _mm(
        "ret_o_t", dy_ret, wt["w_ret_o"], s, RET_V_WIDTH, d, tm=tm, tn=d, tk=d, tb=True, epi=gate_norm_bwd_epi,
        chunk=MXU_COLS,
        ins=[(ret, *full), (rvg, *_tile(tm, d, 1))],
        outs=[(*bf(RET_V_WIDTH), *full)] * 2)

    drq = _ret_bwd_q(rqk, rvg, dret, cos2, sin2, s)
    drk, drv = _ret_bwd_kv(rqk, rvg, dret, cos2, sin2, s)
    dsq, dsk, dsv, *stacked = _sb_bwd(sb_qkv, sb_out_f32, dsb_out, s, comm=_exchange_plan(LATER_WEIGHTS, grads))
    stacks = dict(zip(LATER_WEIGHTS, stacked, strict=True))

    dh = {"sq": dsq, "sk": dsk, "sv": dsv, "rq": drq, "rk": drk, "rv": drv, "rg": drg, "gate0": dgate0,
          "gate1": dgate1}
    def wgrad_in(name, pieces):
        width = sum(p.shape[1] for p in pieces)
        (g,) = _mm(name, xb, pieces, d, width, s, tm=d, tn=width, tk=min(ROW_TILE, s // 2), ta=True, epi=_plain,
                   outs=[((d, width), BF16, *_tile(d, width))])
        return g

    grads["w_in"] = jnp.concatenate(
        [wgrad_in("g_in_mixers", [dsq, dsk, dsv, drq, drk, drv]), wgrad_in("g_in_gates", [drg, dgate0, dgate1])],
        axis=1)
    grad_x, stacks["w_in"] = _mm(
        "in_t", list(dh.values()), w_in, s, d, IN_WIDTH, tm=tm, tn=d, tk=IN_WIDTH, tb=True,
        epi=lambda acc, t, i, j: ([acc + DN_ALPHA * t[0].astype(F32)], []),
        ins=[(du1b, *full)], outs=[(*f32(d), *full)], comm=_exchange_plan(("w_in",), grads))

    small = {"b_gate": jnp.concatenate([dbg0, dbg1], axis=1), "ln1_g": dg1, "ln1_b": db1, "ln2_g": dg2,
             "ln2_b": db2, "ln3_g": dg3, "ln3_b": db3}
    return grad_x, stacks, small, loss_cols


def kernel(x, mem, w_in, b_gate, w_sb_o, w_ret_o, w_mix_o, ln1_g, ln1_b, w_mem_q, w_mem_kv, w_mem_o, ln2_g, ln2_b, w_ffn_in, w_ffn_out, ln3_g, ln3_b, loss_target, m_w_in, m_b_gate, m_w_sb_o, m_w_ret_o, m_w_mix_o, m_ln1_g, m_ln1_b, m_w_mem_q, m_w_mem_kv, m_w_mem_o, m_ln2_g, m_ln2_b, m_w_ffn_in, m_w_ffn_out, m_ln3_g, m_ln3_b, v_w_in, v_b_gate, v_w_sb_o, v_w_ret_o, v_w_mix_o, v_ln1_g, v_ln1_b, v_w_mem_q, v_w_mem_kv, v_w_mem_o, v_ln2_g, v_ln2_b, v_w_ffn_in, v_w_ffn_out, v_ln3_g, v_ln3_b):
    given = dict(locals())
    s = x.shape[1]
    x2d = x.reshape(s, D_MODEL)
    tgt = loss_target.reshape(s, D_MODEL)
    mem2d = mem.reshape(MEM_LEN, D_MODEL)
    shard = {name: given[name].reshape(_shard_shape(shape, axis)) for name, shape, axis in BIG}
    vec = {name: given[name] for name in SMALL}

    shards_bf = {name: _cast_bf16("cast_" + name, shard[name]) for name, _, _ in BIG}

    grad_x, stacks, small, loss_cols = _layer_step(x2d, mem2d, tgt, shards_bf, vec)

    out = {}
    for name, shape, axis in BIG:
        stack = stacks[name]
        shp = given[name].shape
        res = _reduce_adamw("adamw_" + name, stack, shard[name], given["m_" + name].reshape(stack.shape[1:]),
                            given["v_" + name].reshape(stack.shape[1:]))
        out[name] = [r.reshape(shp) for r in res]

    pack = jnp.concatenate([small[name] for name in SMALL] + [loss_cols], axis=1).reshape(PACK_ROWS, LANES)
    cat = lambda pre: jnp.concatenate([given[pre + name] for name in SMALL], axis=1).reshape(SMALL_ROWS, LANES)
    *res, loss = _small_step(pack, cat(""), cat("m_"), cat("v_"))
    flat = [r.reshape(1, SMALL_LEN) for r in res]
    off = 0
    for name in SMALL:
        n = given[name].shape[1]
        out[name] = [r[:, off:off + n] for r in flat]
        off += n

    return (loss.reshape(()), grad_x.reshape(x.shape),
            *[out[name][0] for name in WEIGHT_ORDER], *[out[name][1] for name in WEIGHT_ORDER],
            *[out[name][2] for name in WEIGHT_ORDER], *[out[name][3] for name in WEIGHT_ORDER])
```

```python
import functools

import jax
import jax.numpy as jnp
import numpy as np
from jax import lax
from jax.experimental import pallas as pl
from jax.experimental.pallas import tpu as pltpu

F32, BF16 = jnp.float32, jnp.bfloat16
MESH = pl.DeviceIdType.MESH

D_MODEL = 1024
MEM_LEN = 256
SB_HEADS, SB_DIM, SB_WIDTH = 8, 64, 512
RET_HEADS, RET_QK, RET_V = 4, 128, 256
RET_QK_WIDTH, RET_V_WIDTH = 512, 1024
ROPE_BASE = 10000.0
MEM_HEADS, MEM_DIM = 4, 256
FFN_HIDDEN = 2816
IN_WIDTH = 6656
OFF_RET_Q, OFF_RET_V, OFF_RET_G, OFF_GATE = 1536, 2560, 3584, 4608
DN_ALPHA = 2.0 ** 0.25
LN_EPS = 1e-5
SB_SCALE = SB_DIM ** -0.5
SB_DEAD = -110.0
RET_SCALE = RET_QK ** -0.5
MEM_SCALE = MEM_DIM ** -0.5
ADAM_LR, ADAM_B1, ADAM_B2, ADAM_EPS, ADAM_WD, ADAM_STEP = 0.001, 0.9, 0.999, 1e-08, 0.01, 10

N_DEV, N_CHIPS = 8, 4

LANES = 128
MXU_COLS = 256
VMEM_LIMIT_BYTES = 52 * 2 ** 20
ROW_TILE = 512
WIDE_TILE = 1024
SEQ_TILE = 2048
SB_BLOCK = 256
RET_BLOCK = 256
RET_CHUNKS_PER_STEP = 4
XATTN_ROWS = 1024

BIG = (
    ("w_in", (D_MODEL, IN_WIDTH), 1),
    ("w_sb_o", (SB_WIDTH, D_MODEL), 1),
    ("w_ret_o", (RET_V_WIDTH, D_MODEL), 0),
    ("w_mix_o", (D_MODEL, D_MODEL), 0),
    ("w_mem_q", (D_MODEL, D_MODEL), 0),
    ("w_mem_kv", (D_MODEL, 2 * D_MODEL), 1),
    ("w_mem_o", (D_MODEL, D_MODEL), 0),
    ("w_ffn_in", (D_MODEL, 2 * FFN_HIDDEN), 1),
    ("w_ffn_out", (FFN_HIDDEN, D_MODEL), 0),
)
SMALL = ("b_gate", "ln1_g", "ln1_b", "ln2_g", "ln2_b", "ln3_g", "ln3_b")
SMALL_LEN = 2 * D_MODEL + 6 * D_MODEL
SMALL_ROWS = SMALL_LEN // LANES
PACK_ROWS = SMALL_ROWS + D_MODEL // LANES
WEIGHT_ORDER = ("w_in", "b_gate", "w_sb_o", "w_ret_o", "w_mix_o", "ln1_g", "ln1_b", "w_mem_q", "w_mem_kv",
                "w_mem_o", "ln2_g", "ln2_b", "w_ffn_in", "w_ffn_out", "ln3_g", "ln3_b")


def _cparams():
    return pltpu.CompilerParams(vmem_limit_bytes=VMEM_LIMIT_BYTES)


def _dot(a, b, ca, cb):
    return lax.dot_general(a, b, (((ca,), (cb,)), ((), ())), preferred_element_type=F32)


def _sigmoid(x):
    return 1.0 / (1.0 + jnp.exp(-x))


def _mm(name, a, b, m, n, k, *, tm, tn, tk, epi, outs, ins=(), accs=(), ta=False, tb=False,
        a_off=(0, 0), b_off=(0, 0), j_outer=False, comm=None, chunk=None, halves=False):
    assert not halves or (chunk is not None and (tn // 2) % chunk == 0 and not ins), name
    assert m % tm == 0 and n % tn == 0 and k % tk == 0, (name, m, n, k, tm, tn, tk)
    assert chunk is None or (k == tk and tn % chunk == 0), name
    ni, nj, nk = m // tm, n // tn, k // tk
    assert not accs or nj == 1, name
    ij = (lambda g0, g1: (g1, g0)) if j_outer else (lambda g0, g1: (g0, g1))

    def spec(block, index):
        return pl.BlockSpec(block, lambda g0, g1, kk: index(*ij(g0, g1), kk))

    a_list = list(a) if isinstance(a, (list, tuple)) else [a]
    n_a = len(a_list)
    if n_a > 1:
        assert not ta and nk == 1 and chunk is None and not any(a_off), name
        assert sum(p.shape[1] for p in a_list) == k, name
        a_specs = [spec((tm, p.shape[1]), lambda i, j, kk: (i, 0)) for p in a_list]
    elif ta:
        a_specs = [spec((tk, tm), lambda i, j, kk: (kk + a_off[0], i + a_off[1]))]
    else:
        a_specs = [spec((tm, tk), lambda i, j, kk: (i + a_off[0], kk + a_off[1]))]
    b_list = list(b) if isinstance(b, (list, tuple)) else [b]
    n_b = len(b_list)
    if n_b > 1:
        assert not tb and nj == 1 and nk > 1 and n_a == 1 and chunk is None and not any(b_off), name
        assert sum(p.shape[1] for p in b_list) == n, name
        b_specs = [spec((tk, p.shape[1]), lambda i, j, kk: (kk, 0)) for p in b_list]
    elif tb:
        b_specs = [spec((tn, tk), lambda i, j, kk: (j + b_off[0], kk + b_off[1]))]
    else:
        b_specs = [spec((tk, tn), lambda i, j, kk: (kk + b_off[0], j + b_off[1]))]
    if n_a > 1 and nj == 1:
        b_specs = [pl.BlockSpec(b_specs[0].block_shape, b_specs[0].index_map, pipeline_mode=pl.Buffered(1))]
    in_specs = [*a_specs, *b_specs]
    for _, bs, im in ins:
        in_specs.append(spec(bs, lambda i, j, kk, im=im: im(i, j)))
    out_specs, out_shape = [], []
    for shape, dtype, bs, im in outs:
        out_specs.append(spec(bs, lambda i, j, kk, im=im: im(i, j)))
        out_shape.append(jax.ShapeDtypeStruct(shape, dtype))
    for shape, dtype in accs:
        out_specs.append(spec(shape, lambda i, j, kk, nd=len(shape): (0,) * nd))
        out_shape.append(jax.ShapeDtypeStruct(shape, dtype))
    n_in, n_out, n_acc = len(ins), len(outs), len(accs)
    ca, cb = (0 if ta else 1), (1 if tb else 0)
    grid = (*ij(ni, nj), nk)
    comm_ins, comm_outs, comm_scratch = [], [], []
    if comm is not None:
        comm_in_specs, comm_out_specs = comm.specs
        comm_ins, comm_outs, comm_scratch = list(comm.ins), list(comm.out_shape), list(comm.scratch)
        in_specs += comm_in_specs
        out_specs += comm_out_specs
        out_shape += comm_outs
    n_ci, n_co = len(comm_ins), len(comm_outs)

    def body(*refs):
        a_refs, b_refs, refs = refs[:n_a], refs[n_a:n_a + n_b], refs[n_a + n_b:]
        a_ref, b_ref = a_refs[0], b_refs[0]
        in_refs = refs[:n_in]
        ci_refs = refs[n_in:n_in + n_ci]
        rest = refs[n_in + n_ci:]
        out_refs, acc_refs = rest[:n_out], rest[n_out:n_out + n_acc]
        co_refs = rest[n_out + n_acc:n_out + n_acc + n_co]
        scratch = rest[n_out + n_acc + n_co:]
        sem_refs, scratch = scratch[:len(comm_scratch)], scratch[len(comm_scratch):]
        (i, j), kk = ij(pl.program_id(0), pl.program_id(1)), pl.program_id(2)
        if comm is not None:
            first_step, last_step = _grid_ends(grid)
            pl.when(first_step)(lambda: comm.start(ci_refs, co_refs, sem_refs))
        def finish(acc, cols=slice(None)):
            def of(r):
                return r[..., cols] if r.shape[-1] == tn else r[...]

            o_tiles, a_tiles = epi(acc, [of(r) for r in in_refs], i, j)
            for r, t in zip(out_refs, o_tiles, strict=True):
                r[..., cols] = t.astype(r.dtype)
            if n_acc:
                @pl.when(i == 0)
                def _():
                    for r, t in zip(acc_refs, a_tiles, strict=True):
                        r[..., cols] = t

                @pl.when(i > 0)
                def _():
                    for r, t in zip(acc_refs, a_tiles, strict=True):
                        r[..., cols] += t

        if chunk is not None:
            a_tile = a_ref[...].astype(BF16)

            def product(c0):
                b_part = b_ref[c0:c0 + chunk, :] if tb else b_ref[:, c0:c0 + chunk]
                return _dot(a_tile, b_part.astype(BF16), ca, cb)

            for c0 in range(0, tn // 2 if halves else tn, chunk):
                acc = (product(c0), product(tn // 2 + c0)) if halves else product(c0)
                finish(acc, slice(c0, c0 + chunk))
            if comm is not None:
                pl.when(last_step)(lambda: comm.finish(ci_refs, co_refs, sem_refs))
            return

        if n_b > 1:
            acc_ref = scratch[0]

            def accumulate(first):
                a_tile, c0 = a_ref[...].astype(BF16), 0
                for r in b_refs:
                    c1 = c0 + r.shape[1]
                    term = _dot(a_tile, r[...].astype(BF16), ca, cb)
                    acc_ref[:, c0:c1] = term if first else acc_ref[:, c0:c1] + term
                    c0 = c1

            pl.when(kk == 0)(lambda: accumulate(True))
            pl.when(kk > 0)(lambda: accumulate(False))
            pl.when(kk == nk - 1)(lambda: finish(acc_ref[...]))
            if comm is not None:
                pl.when(last_step)(lambda: comm.finish(ci_refs, co_refs, sem_refs))
            return

        if n_a > 1:
            part, c0 = None, 0
            for r in a_refs:
                c1 = c0 + r.shape[1]
                b_part = b_ref[:, c0:c1] if tb else b_ref[c0:c1, :]
                term = _dot(r[...].astype(BF16), b_part.astype(BF16), ca, cb)
                part, c0 = (term if part is None else part + term), c1
        else:
            part = _dot(a_ref[...].astype(BF16), b_ref[...].astype(BF16), ca, cb)
        if nk == 1:
            finish(part)
        else:
            acc_ref = scratch[0]

            @pl.when(kk == 0)
            def _():
                acc_ref[...] = part

            @pl.when(kk > 0)
            def _():
                acc_ref[...] += part

            @pl.when(kk == nk - 1)
            def _():
                finish(acc_ref[...])

        if comm is not None:
            pl.when(last_step)(lambda: comm.finish(ci_refs, co_refs, sem_refs))

    res = pl.pallas_call(
        body, name=name, grid=grid, in_specs=in_specs, out_specs=out_specs, out_shape=out_shape,
        scratch_shapes=comm_scratch + ([pltpu.VMEM((tm, tn), F32)] if nk > 1 else []),
        compiler_params=_cparams(),
    )(*a_list, *b_list, *[x for x, _, _ in ins], *comm_ins)
    return res


def _grid_ends(grid):
    ids = [pl.program_id(ax) for ax in range(len(grid))]
    first = functools.reduce(jnp.logical_and, [p == 0 for p in ids])
    last = functools.reduce(jnp.logical_and, [p == n - 1 for p, n in zip(ids, grid, strict=True)])
    return first, last


def _tile(tm, tn, dj=0):
    return (tm, tn), (lambda i, j: (i, j + dj))


def _rowvec(tn, dj=0):
    return (1, tn), (lambda i, j: (0, j + dj))


def _plain(acc, tiles, i, j):
    return [acc], []


def _ew(name, fn, ins, outs, rows, tr):
    assert rows % tr == 0, (name, rows, tr)
    in_specs = []
    for x in ins:
        if x.shape[0] == rows:
            in_specs.append(pl.BlockSpec((tr, x.shape[1]), lambda i: (i, 0)))
        else:
            in_specs.append(pl.BlockSpec(x.shape, lambda i: (0, 0)))
    n_in = len(ins)

    def body(*refs):
        res = fn(*[r[...] for r in refs[:n_in]])
        for r, t in zip(refs[n_in:], res, strict=True):
            r[...] = t.astype(r.dtype)

    return pl.pallas_call(
        body, name=name, grid=(rows // tr,), in_specs=in_specs,
        out_specs=[pl.BlockSpec((tr, w), lambda i: (i, 0)) for w, _ in outs],
        out_shape=[jax.ShapeDtypeStruct((rows, w), dt) for w, dt in outs],
        compiler_params=_cparams(),
    )(*ins)


def _cast_bf16(name, x):
    rows = x.shape[0]
    tr = next(t for t in (512, 256, 64) if rows % t == 0)
    return _ew(name, lambda v: (v,), [x], [(x.shape[1], BF16)], rows, tr)[0]


def _prep(x, comm):
    s = x.shape[0]
    half = RET_QK // 2
    inv = 1.0 / (ROPE_BASE ** (jnp.arange(half, dtype=F32) / half))
    inv2 = jnp.concatenate([inv, inv]).reshape(1, RET_QK)
    sign = jnp.concatenate([-jnp.ones((half,), F32), jnp.ones((half,), F32)]).reshape(1, RET_QK)
    tr = min(ROW_TILE, s)
    grid = (s // tr,)
    c_in_specs, c_out_specs, c_out_shape, c_scratch, c_ins, split = _host(comm, 3, 3)

    def body(*refs):
        (x_ref, inv_ref, sign_ref), (xb_ref, cos_ref, sin_ref), _, riding = split(refs)
        i = pl.program_id(0)
        first_step, last_step = _grid_ends(grid)
        pl.when(first_step)(lambda: comm.start(*riding))
        xb_ref[...] = x_ref[...].astype(BF16)
        pos = (lax.broadcasted_iota(jnp.int32, (tr, RET_QK), 0) + i * tr).astype(F32)
        ang = pos * inv_ref[...]
        cos_ref[...] = jnp.cos(ang)
        sin_ref[...] = jnp.sin(ang) * sign_ref[...]
        pl.when(last_step)(lambda: comm.finish(*riding))

    vec = pl.BlockSpec((1, RET_QK), lambda i: (0, 0))
    row = lambda w: pl.BlockSpec((tr, w), lambda i: (i, 0))
    return pl.pallas_call(
        body, name="prep", grid=grid,
        in_specs=[row(D_MODEL), vec, vec] + c_in_specs,
        out_specs=[row(D_MODEL), row(RET_QK), row(RET_QK)] + c_out_specs,
        out_shape=[jax.ShapeDtypeStruct((s, D_MODEL), BF16), jax.ShapeDtypeStruct((s, RET_QK), F32),
                   jax.ShapeDtypeStruct((s, RET_QK), F32)] + c_out_shape,
        scratch_shapes=c_scratch, compiler_params=_cparams(),
    )(x, inv2, sign, *c_ins)


def _swap_halves(x):
    return pltpu.roll(x, RET_QK // 2, 1)


def _norm(u):
    mu = jnp.mean(u, axis=-1, keepdims=True)
    d = u - mu
    var = jnp.mean(d * d, axis=-1, keepdims=True)
    rstd = lax.rsqrt(var + LN_EPS)
    return d * rstd, rstd


def _norm_bwd(dxh, xhat, rstd):
    m1 = jnp.mean(dxh, axis=-1, keepdims=True)
    m2 = jnp.mean(dxh * xhat, axis=-1, keepdims=True)
    return rstd * (dxh - m1 - xhat * m2)


def _colsum(t):
    return jnp.sum(t, axis=0, keepdims=True)


def _split_mm(t, tri):
    hi = t.astype(BF16)
    lo = (t - hi.astype(F32)).astype(BF16)
    return _dot(hi, tri, 1, 0) + _dot(lo, tri, 1, 0)


def _sb_masks():
    t = SB_BLOCK
    lane = lax.broadcasted_iota(jnp.int32, (1, LANES), 1)
    first = lane < SB_DIM
    m0 = jnp.where(first, 1.0, 0.0).astype(BF16)
    m1 = jnp.where(first, 0.0, 1.0).astype(BF16)
    row = lax.broadcasted_iota(jnp.int32, (t, t), 0)
    col = lax.broadcasted_iota(jnp.int32, (t, t), 1)
    return first, (m0, m1), row, col


def _sb_logits(qh, k, causal):
    z = _dot(qh, k, 1, 1)
    lp = jnp.log(1.0 + jnp.exp(-jnp.abs(z)))
    a = jnp.minimum(z, 0.0) - lp
    r = jnp.minimum(-z, 0.0) - lp
    if causal is not None:
        r = jnp.where(causal, r, 0.0)
    return a, r


def _sb_walk(i, blocks, l_ref, causal):
    pl.when(i == 0)(lambda: blocks([(i, causal)]))
    pl.when(i > 0)(lambda: blocks([(i, causal), (i - 1, None)]))

    def alive():
        top = jnp.max(functools.reduce(jnp.maximum, [l_ref[c] for c in range(l_ref.shape[0])]))
        return jnp.where(top > SB_DEAD, 1, 0)

    def cond(c):
        return jnp.logical_and(c[0] < i, c[1] > 0)

    def step(c):
        blocks([(i - 1 - c[0], None)])
        return c[0] + 1, alive()

    lax.while_loop(cond, step, (jnp.int32(1), alive()))


def _host(comm, n_in, n_out):
    if comm is None:
        return [], [], [], [], [], lambda refs: (refs[:n_in], refs[n_in:n_in + n_out], refs[n_in + n_out:], None)
    in_specs, out_specs = comm.specs
    n_ci, n_co, n_sem = len(comm.ins), len(comm.out_shape), len(comm.scratch)

    def split(refs):
        ins, ci = refs[:n_in], refs[n_in:n_in + n_ci]
        rest = refs[n_in + n_ci:]
        outs, co = rest[:n_out], rest[n_out:n_out + n_co]
        sems, scratch = rest[n_out + n_co:n_out + n_co + n_sem], rest[n_out + n_co + n_sem:]
        return ins, outs, scratch, (ci, co, sems)

    return in_specs, out_specs, list(comm.out_shape), list(comm.scratch), list(comm.ins), split


def _sb_qkv_specs(s, g):
    groups = SB_HEADS // 2 // g
    return [pl.BlockSpec((g, SB_BLOCK, LANES), lambda p, i: (p, i, 0)),
            pl.BlockSpec((g, s, LANES), lambda p, i: (groups + p, 0, 0)),
            pl.BlockSpec((g, s, LANES), lambda p, i: (2 * groups + p, 0, 0))]


def _sb_fwd(qkv, s, comm=None):
    t = SB_BLOCK
    g = 2
    nq = s // t
    grid = (SB_HEADS // 2 // g, nq)
    c_in_specs, c_out_specs, c_out_shape, c_scratch, c_ins, split = _host(comm, 3, 2)

    def body(*refs):
        (q_ref, k_ref, v_ref), (o_ref, of_ref), (l_ref, acc_ref), riding = split(refs)
        i = pl.program_id(1)
        if comm is not None:
            first_step, last_step = _grid_ends(grid)
            pl.when(first_step)(lambda: comm.start(*riding))
        first, hmask, row, col = _sb_masks()
        after = jnp.where(row > col, 1.0, 0.0).astype(BF16)
        causal = col < row
        heads = [(p, h) for p in range(g) for h in range(2)]
        qh = {(p, h): q_ref[p] * hmask[h] for p, h in heads}
        l_ref[...] = jnp.zeros_like(l_ref)
        acc_ref[...] = jnp.zeros_like(acc_ref)

        def blocks(todo):
            chains = [(b, p, h) for b in range(len(todo)) for p, h in heads]
            starts = [pl.multiple_of(kb * t, t) for kb, _ in todo]
            ks = {(b, p): k_ref[p, pl.ds(st, t), :] for b, st in enumerate(starts) for p in range(g)}
            vs = {(b, p): v_ref[p, pl.ds(st, t), :] for b, st in enumerate(starts) for p in range(g)}
            ar = {(b, p, h): _sb_logits(qh[p, h], ks[b, p], todo[b][1]) for b, p, h in chains}
            later = {c: _split_mm(ar[c][1], after) for c in chains}
            carry = {(p, h): l_ref[2 * p + h] for p, h in heads}
            w = {}
            for b, (_, mask) in enumerate(todo):
                for p, h in heads:
                    wc = jnp.exp(ar[b, p, h][0] + later[b, p, h] + carry[p, h])
                    w[b, p, h] = wc if mask is None else jnp.where(mask, wc, 0.0)
                carry = {(p, h): carry[p, h] + jnp.sum(ar[b, p, h][1], axis=1, keepdims=True) for p, h in heads}
            pv = {(b, p, h): _dot(w[b, p, h].astype(BF16), vs[b, p], 1, 0) for b, p, h in chains}
            for p in range(g):
                lanes = slice(p * LANES, (p + 1) * LANES)
                acc = acc_ref[:, lanes]
                for b in range(len(todo)):
                    acc = acc + jnp.where(first, pv[b, p, 0], pv[b, p, 1])
                acc_ref[:, lanes] = acc
            for p, h in heads:
                l_ref[2 * p + h] = carry[p, h]

        _sb_walk(i, blocks, l_ref, causal)
        o_ref[...] = acc_ref[...].astype(o_ref.dtype)
        of_ref[...] = acc_ref[...]
        if comm is not None:
            pl.when(last_step)(lambda: comm.finish(*riding))

    blk = pl.BlockSpec((t, g * LANES), lambda p, i: (i, p))
    return pl.pallas_call(
        body, name="sb_fwd", grid=grid,
        in_specs=_sb_qkv_specs(s, g) + c_in_specs,
        out_specs=[blk, blk] + c_out_specs,
        out_shape=[jax.ShapeDtypeStruct((s, SB_WIDTH), BF16), jax.ShapeDtypeStruct((s, SB_WIDTH), F32)] + c_out_shape,
        scratch_shapes=c_scratch + [pltpu.VMEM((2 * g, t, 1), F32), pltpu.VMEM((t, g * LANES), F32)],
        compiler_params=_cparams(),
    )(qkv, qkv, qkv, *c_ins)


def _sb_bwd(qkv, o, do, s, comm=None):
    t = SB_BLOCK
    g = 2
    nq = s // t
    grid = (SB_HEADS // 2 // g, nq)
    c_in_specs, c_out_specs, c_out_shape, c_scratch, c_ins, split = _host(comm, 5, 3)

    def body(*refs):
        ((q_ref, k_ref, v_ref, o_ref, do_ref), (dq_ref, dk_ref, dv_ref),
         (l_ref, e_ref, dq_acc, dk_acc, dv_acc), riding) = split(refs)
        i = pl.program_id(1)
        if comm is not None:
            first_step, last_step = _grid_ends(grid)
            pl.when(first_step)(lambda: comm.start(*riding))
        first, hmask, row, col = _sb_masks()
        after = jnp.where(row > col, 1.0, 0.0).astype(BF16)
        from_here = jnp.where(row >= col, 1.0, 0.0).astype(BF16)
        causal = col < row

        @pl.when(i == 0)
        def _():
            dk_acc[...] = jnp.zeros_like(dk_acc)
            dv_acc[...] = jnp.zeros_like(dv_acc)

        heads = [(p, h) for p in range(g) for h in range(2)]
        lanes = [slice(p * LANES, (p + 1) * LANES) for p in range(g)]
        q = [q_ref[p] for p in range(g)]
        do_ = [do_ref[:, lanes[p]] for p in range(g)]
        qh = {(p, h): q[p] * hmask[h] for p, h in heads}
        doh = {(p, h): do_[p] * hmask[h] for p, h in heads}
        total = {}
        for p in range(g):
            prod = do_[p].astype(F32) * o_ref[:, lanes[p]]
            total[p, 0] = jnp.sum(jnp.where(first, prod, 0.0), axis=1, keepdims=True)
            total[p, 1] = jnp.sum(jnp.where(first, 0.0, prod), axis=1, keepdims=True)
        l_ref[...] = jnp.zeros_like(l_ref)
        e_ref[...] = jnp.zeros_like(e_ref)
        dq_acc[...] = jnp.zeros_like(dq_acc)

        def blocks(todo):
            chains = [(b, p, h) for b in range(len(todo)) for p, h in heads]
            starts = [pl.multiple_of(kb * t, t) for kb, _ in todo]
            ks = {(b, p): k_ref[p, pl.ds(st, t), :] for b, st in enumerate(starts) for p in range(g)}
            vs = {(b, p): v_ref[p, pl.ds(st, t), :] for b, st in enumerate(starts) for p in range(g)}
            ar = {(b, p, h): _sb_logits(qh[p, h], ks[b, p], todo[b][1]) for b, p, h in chains}
            dw = {(b, p, h): _dot(doh[p, h], vs[b, p], 1, 1) for b, p, h in chains}
            later = {c: _split_mm(ar[c][1], after) for c in chains}
            carry = {(p, h): l_ref[2 * p + h] for p, h in heads}
            wb = {}
            for b, (_, mask) in enumerate(todo):
                for p, h in heads:
                    wc = jnp.exp(ar[b, p, h][0] + later[b, p, h] + carry[p, h])
                    wb[b, p, h] = (wc if mask is None else jnp.where(mask, wc, 0.0)).astype(BF16)
                carry = {(p, h): carry[p, h] + jnp.sum(ar[b, p, h][1], axis=1, keepdims=True) for p, h in heads}
            dvs = {(b, p, h): _dot(wb[b, p, h], do_[p], 0, 0) for b, p, h in chains}
            e = {c: dw[c] * wb[c].astype(F32) for c in chains}
            suffix = {c: _split_mm(e[c], from_here) for c in chains}
            e_carry = {(p, h): e_ref[2 * p + h] for p, h in heads}
            dz = {}
            for b, (_, mask) in enumerate(todo):
                for p, h in heads:
                    before = total[p, h] - (suffix[b, p, h] + e_carry[p, h])
                    dzc = e[b, p, h] - jnp.exp(ar[b, p, h][0]) * (e[b, p, h] + before)
                    dz[b, p, h] = (dzc if mask is None else jnp.where(mask, dzc, 0.0)).astype(BF16)
                e_carry = {(p, h): e_carry[p, h] + jnp.sum(e[b, p, h], axis=1, keepdims=True) for p, h in heads}
            dqs = {(b, p, h): _dot(dz[b, p, h], ks[b, p], 1, 0) for b, p, h in chains}
            dks = {(b, p, h): _dot(dz[b, p, h], q[p], 0, 0) for b, p, h in chains}
            for p in range(g):
                dq = dq_acc[:, lanes[p]]
                for b, st in enumerate(starts):
                    dq = dq + jnp.where(first, dqs[b, p, 0], dqs[b, p, 1])
                    dk_acc[pl.ds(st, t), lanes[p]] += jnp.where(first, dks[b, p, 0], dks[b, p, 1])
                    dv_acc[pl.ds(st, t), lanes[p]] += jnp.where(first, dvs[b, p, 0], dvs[b, p, 1])
                dq_acc[:, lanes[p]] = dq
            for p, h in heads:
                l_ref[2 * p + h] = carry[p, h]
                e_ref[2 * p + h] = e_carry[p, h]

        _sb_walk(i, blocks, l_ref, causal)
        dq_ref[...] = (dq_acc[...] * SB_SCALE).astype(dq_ref.dtype)

        @pl.when(i == nq - 1)
        def _():
            dk_ref[...] = dk_acc[...].astype(dk_ref.dtype)
            dv_ref[...] = dv_acc[...].astype(dv_ref.dtype)

        if comm is not None:
            pl.when(last_step)(lambda: comm.finish(*riding))

    once = pl.Buffered(1)
    q_spec, k_spec, v_spec = _sb_qkv_specs(s, g)
    k_spec = pl.BlockSpec(k_spec.block_shape, k_spec.index_map, pipeline_mode=once)
    v_spec = pl.BlockSpec(v_spec.block_shape, v_spec.index_map, pipeline_mode=once)
    blk = pl.BlockSpec((t, g * LANES), lambda p, i: (i, p))
    col_blk = pl.BlockSpec((s, g * LANES), lambda p, i: (0, p), pipeline_mode=once)
    sds = jax.ShapeDtypeStruct((s, SB_WIDTH), BF16)
    return pl.pallas_call(
        body, name="sb_bwd", grid=grid,
        in_specs=[q_spec, k_spec, v_spec, blk, blk] + c_in_specs,
        out_specs=[blk, col_blk, col_blk] + c_out_specs,
        out_shape=[sds, sds, sds] + c_out_shape,
        scratch_shapes=c_scratch + [pltpu.VMEM((2 * g, t, 1), F32), pltpu.VMEM((2 * g, t, 1), F32),
                                    pltpu.VMEM((t, g * LANES), F32), pltpu.VMEM((s, g * LANES), F32),
                                    pltpu.VMEM((s, g * LANES), F32)],
        compiler_params=_cparams(),
    )(qkv, qkv, qkv, o, do, *c_ins)


def _ret_log_gamma():
    lg = np.log1p(-np.exp2(-5.0 - np.arange(RET_HEADS, dtype=np.float32))).astype(np.float32)
    return jnp.asarray(np.broadcast_to(lg[:, None, None], (RET_HEADS, 8, LANES)).copy())


RET_SCRATCH = [pltpu.VMEM((RET_HEADS, RET_QK, RET_V), F32),
               pltpu.VMEM((RET_HEADS, RET_BLOCK, RET_BLOCK), F32),
               pltpu.VMEM((RET_HEADS, RET_BLOCK, 1), F32),
               pltpu.VMEM((RET_HEADS, RET_BLOCK, 1), F32)]


def _ret_begin(n, lg_ref, state, within, q_dec, k_dec):
    @pl.when(n == 0)
    def _():
        c = RET_BLOCK
        state[...] = jnp.zeros_like(state)
        row = lax.broadcasted_iota(jnp.int32, (c, c), 0)
        col = lax.broadcasted_iota(jnp.int32, (c, c), 1)
        rel = jnp.maximum(row - col, 0).astype(F32)
        idx = lax.broadcasted_iota(jnp.int32, (c, 1), 0).astype(F32)
        for h in range(RET_HEADS):
            lg = lg_ref[h, 0:1, 0:1]
            within[h] = jnp.where(row >= col, jnp.exp(lg * rel), 0.0)
            q_dec[h] = jnp.exp(lg * (idx + 1.0))
            k_dec[h] = jnp.exp(lg * (c - 1.0 - idx))


def _chunk_decay(lg_ref, h):
    return jnp.exp(lg_ref[h, 0:1, 0:1] * float(RET_BLOCK))


def _ret_heads(x, width):
    return [x[:, h * width:(h + 1) * width] for h in range(RET_HEADS)]


def _ret_specs(s, reverse=False):
    c = RET_BLOCK
    per_step = min(RET_CHUNKS_PER_STEP, s // c)
    rows = c * per_step
    nc = s // rows
    pos = (lambda n: nc - 1 - n) if reverse else (lambda n: n)
    chunks = [slice(u * c, (u + 1) * c) for u in range(per_step)]
    q_spec = pl.BlockSpec((rows, RET_QK_WIDTH), lambda n: (pos(n), 0))
    k_spec = pl.BlockSpec((rows, RET_QK_WIDTH), lambda n: (pos(n), 1))
    v_spec = pl.BlockSpec((rows, RET_V_WIDTH), lambda n: (pos(n), 0))
    lg_spec = pl.BlockSpec((RET_HEADS, 8, LANES), lambda n: (0, 0, 0))
    rope_spec = pl.BlockSpec((rows, RET_QK), lambda n: (pos(n), 0))
    return nc, chunks[::-1] if reverse else chunks, q_spec, k_spec, v_spec, lg_spec, rope_spec


def _ret_fwd(rqk, rvg, s):
    nc, chunks, q_spec, k_spec, v_spec, lg_spec, _ = _ret_specs(s)
    g_spec = pl.BlockSpec(v_spec.block_shape, lambda n: (n, 1))
    heads = range(RET_HEADS)

    def body(q_ref, k_ref, v_ref, g_ref, lg_ref, r_ref, y_ref, state, within, q_dec, k_dec):
        n = pl.program_id(0)
        _ret_begin(n, lg_ref, state, within, q_dec, k_dec)
        for rows in chunks:
            q, k = _ret_heads(q_ref[rows], RET_QK), _ret_heads(k_ref[rows], RET_QK)
            v, g = _ret_heads(v_ref[rows], RET_V), _ret_heads(g_ref[rows], RET_V)
            scores = [_dot(q[h].astype(BF16), k[h].astype(BF16), 1, 1) * within[h] for h in heads]
            cross = [_dot((q[h] * q_dec[h]).astype(BF16), state[h].astype(BF16), 1, 0) for h in heads]
            out = [_dot(scores[h].astype(BF16), v[h], 1, 0) + cross[h] for h in heads]
            grown = [_dot((k[h] * k_dec[h]).astype(BF16), v[h], 0, 0) for h in heads]
            for h in heads:
                sl = slice(h * RET_V, (h + 1) * RET_V)
                r_ref[rows, sl] = out[h]
                xhat, _ = _norm(out[h])
                gh = g[h].astype(F32)
                y_ref[rows, sl] = (gh * _sigmoid(gh) * xhat).astype(y_ref.dtype)
                state[h] = state[h] * _chunk_decay(lg_ref, h) + grown[h]

    return pl.pallas_call(
        body, name="ret_fwd", grid=(nc,),
        in_specs=[q_spec, k_spec, v_spec, g_spec, lg_spec],
        out_specs=[v_spec, v_spec],
        out_shape=[jax.ShapeDtypeStruct((s, RET_V_WIDTH), F32), jax.ShapeDtypeStruct((s, RET_V_WIDTH), BF16)],
        scratch_shapes=RET_SCRATCH,
        compiler_params=_cparams(),
    )(rqk, rqk, rvg, rvg, _ret_log_gamma())


def _rope_bwd(d, cos, sin):
    return d * cos + _swap_halves(d * sin)


def _ret_bwd_q(rqk, rv, d_out, cos2, sin2, s):
    nc, chunks, q_spec, k_spec, v_spec, lg_spec, rope_spec = _ret_specs(s)
    heads = range(RET_HEADS)

    def body(k_ref, v_ref, d_ref, lg_ref, cos_ref, sin_ref, dq_ref, state, within, q_dec, k_dec):
        n = pl.program_id(0)
        _ret_begin(n, lg_ref, state, within, q_dec, k_dec)
        for rows in chunks:
            k = _ret_heads(k_ref[rows], RET_QK)
            v, d = _ret_heads(v_ref[rows], RET_V), _ret_heads(d_ref[rows], RET_V)
            cos, sin = cos_ref[rows], sin_ref[rows]
            d_scores = [_dot(d[h], v[h], 1, 1) * within[h] for h in heads]
            cross = [q_dec[h] * _dot(d[h], state[h].astype(BF16), 1, 1) for h in heads]
            dq = [_dot(d_scores[h].astype(BF16), k[h].astype(BF16), 1, 0) + cross[h] for h in heads]
            grown = [_dot((k[h] * k_dec[h]).astype(BF16), v[h], 0, 0) for h in heads]
            for h in heads:
                sl = slice(h * RET_QK, (h + 1) * RET_QK)
                dq_ref[rows, sl] = (_rope_bwd(dq[h], cos, sin) * RET_SCALE).astype(dq_ref.dtype)
                state[h] = state[h] * _chunk_decay(lg_ref, h) + grown[h]

    return pl.pallas_call(
        body, name="ret_bwd_q", grid=(nc,),
        in_specs=[k_spec, v_spec, v_spec, lg_spec, rope_spec, rope_spec],
        out_specs=q_spec,
        out_shape=jax.ShapeDtypeStruct((s, RET_QK_WIDTH), BF16),
        scratch_shapes=RET_SCRATCH,
        compiler_params=_cparams(),
    )(rqk, rv, d_out, _ret_log_gamma(), cos2, sin2)


def _ret_bwd_kv(rqk, rv, d_out, cos2, sin2, s):
    nc, chunks, q_spec, k_spec, v_spec, lg_spec, rope_spec = _ret_specs(s, reverse=True)
    heads = range(RET_HEADS)

    def body(q_ref, k_ref, v_ref, d_ref, lg_ref, cos_ref, sin_ref, dk_ref, dv_ref, state, within, q_dec, k_dec):
        n = pl.program_id(0)
        _ret_begin(n, lg_ref, state, within, q_dec, k_dec)
        for rows in chunks:
            q, k = _ret_heads(q_ref[rows], RET_QK), _ret_heads(k_ref[rows], RET_QK)
            v, d = _ret_heads(v_ref[rows], RET_V), _ret_heads(d_ref[rows], RET_V)
            cos, sin = cos_ref[rows], sin_ref[rows]
            qb, kb = [q[h].astype(BF16) for h in heads], [k[h].astype(BF16) for h in heads]
            st = [state[h].astype(BF16) for h in heads]
            scores = [_dot(qb[h], kb[h], 1, 1) * within[h] for h in heads]
            d_scores = [_dot(d[h], v[h], 1, 1) * within[h] for h in heads]
            dk = [_dot(d_scores[h].astype(BF16), qb[h], 0, 0) + k_dec[h] * _dot(v[h], st[h], 1, 1) for h in heads]
            dv = [_dot(scores[h].astype(BF16), d[h], 0, 0) + k_dec[h] * _dot(kb[h], st[h], 1, 0) for h in heads]
            grown = [_dot((q[h] * q_dec[h]).astype(BF16), d[h], 0, 0) for h in heads]
            for h in heads:
                dk_ref[rows, h * RET_QK:(h + 1) * RET_QK] = _rope_bwd(dk[h], cos, sin).astype(dk_ref.dtype)
                dv_ref[rows, h * RET_V:(h + 1) * RET_V] = dv[h].astype(dv_ref.dtype)
                state[h] = state[h] * _chunk_decay(lg_ref, h) + grown[h]

    return pl.pallas_call(
        body, name="ret_bwd_kv", grid=(nc,),
        in_specs=[q_spec, k_spec, v_spec, v_spec, lg_spec, rope_spec, rope_spec],
        out_specs=[q_spec, v_spec],
        out_shape=[jax.ShapeDtypeStruct((s, RET_QK_WIDTH), BF16), jax.ShapeDtypeStruct((s, RET_V_WIDTH), BF16)],
        scratch_shapes=RET_SCRATCH,
        compiler_params=_cparams(),
    )(rqk, rqk, rv, d_out, _ret_log_gamma(), cos2, sin2)


def _xattn_probs(scores):
    sc = scores - jnp.max(scores, axis=-1, keepdims=True)
    p = jnp.exp(sc)
    return p / jnp.sum(p, axis=-1, keepdims=True)


def _xattn_heads(q_ref, kv_ref):
    sls = [slice(h * MEM_DIM, (h + 1) * MEM_DIM) for h in range(MEM_HEADS)]
    q = [q_ref[:, sl] for sl in sls]
    k = [kv_ref[:, sl] for sl in sls]
    v = [kv_ref[:, D_MODEL + h * MEM_DIM:D_MODEL + (h + 1) * MEM_DIM] for h in range(MEM_HEADS)]
    return sls, q, k, v


def _xattn_fwd(qm, kv, s):
    tq = min(XATTN_ROWS, s)
    heads = range(MEM_HEADS)

    def body(q_ref, kv_ref, o_ref):
        sls, q, k, v = _xattn_heads(q_ref, kv_ref)
        scores = [_dot(q[h], k[h], 1, 1) for h in heads]
        p = [_xattn_probs(scores[h]).astype(BF16) for h in heads]
        out = [_dot(p[h], v[h], 1, 0) for h in heads]
        for h in heads:
            o_ref[:, sls[h]] = out[h].astype(o_ref.dtype)

    return pl.pallas_call(
        body, name="xattn_fwd", grid=(s // tq,),
        in_specs=[pl.BlockSpec((tq, D_MODEL), lambda i: (i, 0)),
                  pl.BlockSpec((MEM_LEN, 2 * D_MODEL), lambda i: (0, 0))],
        out_specs=pl.BlockSpec((tq, D_MODEL), lambda i: (i, 0)),
        out_shape=jax.ShapeDtypeStruct((s, D_MODEL), BF16),
        compiler_params=_cparams(),
    )(qm, kv)


def _xattn_bwd(qm, kv, do, s):
    tq = min(XATTN_ROWS, s)

    def body(q_ref, kv_ref, do_ref, dq_ref, dkv_ref):
        i = pl.program_id(0)

        @pl.when(i == 0)
        def _():
            dkv_ref[...] = jnp.zeros_like(dkv_ref)

        heads = range(MEM_HEADS)
        sls, q, k, v = _xattn_heads(q_ref, kv_ref)
        d = [do_ref[:, sl] for sl in sls]
        scores = [_dot(q[h], k[h], 1, 1) for h in heads]
        dp = [_dot(d[h], v[h], 1, 1) for h in heads]
        p = [_xattn_probs(scores[h]) for h in heads]
        ds = [(p[h] * (dp[h] - jnp.sum(p[h] * dp[h], axis=-1, keepdims=True))).astype(BF16) for h in heads]
        dq = [_dot(ds[h], k[h], 1, 0) for h in heads]
        dk = [_dot(ds[h], q[h], 0, 0) for h in heads]
        dv = [_dot(p[h].astype(BF16), d[h], 0, 0) for h in heads]
        for h in heads:
            dq_ref[:, sls[h]] = (dq[h] * MEM_SCALE).astype(dq_ref.dtype)
            dkv_ref[:, sls[h]] += dk[h]
            dkv_ref[:, D_MODEL + h * MEM_DIM:D_MODEL + (h + 1) * MEM_DIM] += dv[h]

    row_blk = pl.BlockSpec((tq, D_MODEL), lambda i: (i, 0))
    kv_blk = pl.BlockSpec((MEM_LEN, 2 * D_MODEL), lambda i: (0, 0))
    return pl.pallas_call(
        body, name="xattn_bwd", grid=(s // tq,),
        in_specs=[row_blk, kv_blk, row_blk],
        out_specs=[row_blk, kv_blk],
        out_shape=[jax.ShapeDtypeStruct((s, D_MODEL), BF16), jax.ShapeDtypeStruct((MEM_LEN, 2 * D_MODEL), F32)],
        compiler_params=_cparams(),
    )(qm, kv, do)


def _place():
    x, y, c = lax.axis_index("x"), lax.axis_index("y"), lax.axis_index("c")
    others = [(1 - x, y), (x, 1 - y), (1 - x, 1 - y)]
    return x, y, c, others


def _slab(ref, axis, chip, size):
    start = pl.multiple_of(chip * size, LANES if axis == 1 else 16)
    if axis == 0:
        return ref.at[pl.ds(start, size), :]
    return ref.at[:, pl.ds(start, size)]


class _CommPlan:
    def __init__(self, ins, out_shape, scratch, start, finish):
        self.ins, self.out_shape, self.scratch, self.start, self.finish = ins, out_shape, scratch, start, finish

    @property
    def specs(self):
        any_spec = pl.BlockSpec(memory_space=pl.ANY)
        return [any_spec] * len(self.ins), [any_spec] * len(self.out_shape)


def _gather_plan(names, shards):
    spec = {name: (shape, axis) for name, shape, axis in BIG}
    nw = len(names)

    def shard_half(ref, c):
        rows = ref.shape[0] // 2
        return ref.at[pl.ds(pl.multiple_of(c * rows, 16), rows), :]

    def region(ref, w, chip, c):
        shape, axis = spec[names[w]]
        size = shape[axis] // N_CHIPS
        if axis == 0:
            rows = size // 2
            return ref.at[pl.ds(pl.multiple_of(chip * size + c * rows, 16), rows), :]
        rows = shape[0] // 2
        return ref.at[pl.ds(pl.multiple_of(c * rows, 16), rows), pl.ds(pl.multiple_of(chip * size, LANES), size)]

    def ops(shard, full, sems):
        ici_send, ici_recv, d2d_send, d2d_recv, local_sems = sems
        x, y, c, others = _place()
        mine, sibling = 2 * x + y, (x, y, 1 - c)
        local, over_ici, arrived, passed_on, from_sibling = [], [], [], [], []
        for w in range(nw):
            shape, axis = spec[names[w]]
            local.append(pltpu.make_async_copy(shard[w], _slab(full[w], axis, mine, shape[axis] // N_CHIPS),
                                               local_sems.at[w]))
            for t, (qx, qy) in enumerate(others):
                n, theirs = 3 * w + t, 2 * qx + qy
                over_ici.append(pltpu.make_async_remote_copy(
                    src_ref=shard_half(shard[w], c), dst_ref=region(full[w], w, mine, c),
                    send_sem=ici_send.at[n], recv_sem=ici_recv.at[n], device_id=(qx, qy, c), device_id_type=MESH))
                arrived.append(pltpu.make_async_remote_copy(
                    src_ref=shard_half(shard[w], c), dst_ref=region(full[w], w, theirs, c),
                    send_sem=ici_send.at[n], recv_sem=ici_recv.at[n], device_id=(qx, qy, c), device_id_type=MESH))
                passed_on.append(pltpu.make_async_remote_copy(
                    src_ref=region(full[w], w, theirs, c), dst_ref=region(full[w], w, theirs, c),
                    send_sem=d2d_send.at[n], recv_sem=d2d_recv.at[n], device_id=sibling, device_id_type=MESH))
                from_sibling.append(pltpu.make_async_remote_copy(
                    src_ref=region(full[w], w, theirs, c), dst_ref=region(full[w], w, theirs, 1 - c),
                    send_sem=d2d_send.at[n], recv_sem=d2d_recv.at[n], device_id=sibling, device_id_type=MESH))
        return local, over_ici, arrived, passed_on, from_sibling

    def start(shard, full, sems):
        local, over_ici, _, _, _ = ops(shard, full, sems)
        for cp in local + over_ici:
            cp.start()

    def finish(shard, full, sems):
        local, over_ici, arrived, passed_on, from_sibling = ops(shard, full, sems)
        for got, onward in zip(arrived, passed_on, strict=True):
            got.wait_recv()
            onward.start()
        for got in from_sibling:
            got.wait_recv()
        for cp in over_ici + passed_on:
            cp.wait_send()
        for cp in local:
            cp.wait()

    dma = pltpu.SemaphoreType.DMA
    return _CommPlan(
        ins=[shards[name] for name in names],
        out_shape=[jax.ShapeDtypeStruct(spec[name][0], BF16) for name in names],
        scratch=[dma((3 * nw,)), dma((3 * nw,)), dma((3 * nw,)), dma((3 * nw,)), dma((nw,))],
        start=start, finish=finish)


def _shard_shape(shape, axis):
    return tuple(d // N_CHIPS if a == axis else d for a, d in enumerate(shape))


def _exchange_plan(names, grads):
    spec = {name: (shape, axis) for name, shape, axis in BIG}
    nw = len(names)

    def ops(grad, stack, sems):
        send_sems, recv_sems, local_sems = sems
        x, y, c, others = _place()
        mine = 2 * x + y
        me, sibling = (x, y, c), (x, y, 1 - c)

        def dev(px, py, pc):
            return 4 * px + 2 * py + pc

        def copy(w, n, src, slot, to):
            return pltpu.make_async_remote_copy(
                src_ref=src, dst_ref=stack[w].at[slot], send_sem=send_sems.at[7 * w + n],
                recv_sem=recv_sems.at[7 * w + n], device_id=to, device_id_type=MESH)

        local, first, arrived, passed_on, from_sibling = [], [], [], [], []
        for w in range(nw):
            shape, axis = spec[names[w]]
            size = shape[axis] // N_CHIPS
            own = _slab(grad[w], axis, mine, size)
            local.append(pltpu.make_async_copy(own, stack[w].at[dev(*me)], local_sems.at[w]))
            first.append(copy(w, 0, own, dev(*me), sibling))
            from_sibling.append(copy(w, 0, own, dev(*sibling), me))
            for t, (qx, qy) in enumerate(others):
                got = stack[w].at[dev(qx, qy, c)]
                first.append(copy(w, 1 + t, _slab(grad[w], axis, 2 * qx + qy, size), dev(*me), (qx, qy, c)))
                arrived.append(copy(w, 1 + t, got, dev(qx, qy, c), me))
                passed_on.append(copy(w, 4 + t, got, dev(qx, qy, c), sibling))
                from_sibling.append(copy(w, 4 + t, got, dev(qx, qy, 1 - c), me))
        return local, first, arrived, passed_on, from_sibling

    def start(grad, stack, sems):
        local, first, _, _, _ = ops(grad, stack, sems)
        for cp in local + first:
            cp.start()

    def finish(grad, stack, sems):
        local, first, arrived, passed_on, from_sibling = ops(grad, stack, sems)
        for got, onward in zip(arrived, passed_on, strict=True):
            got.wait_recv()
            onward.start()
        for got in from_sibling:
            got.wait_recv()
        for cp in first + passed_on:
            cp.wait_send()
        for cp in local:
            cp.wait()

    dma = pltpu.SemaphoreType.DMA
    return _CommPlan(
        ins=[grads[name] for name in names],
        out_shape=[jax.ShapeDtypeStruct((N_DEV,) + _shard_shape(*spec[name]), BF16) for name in names],
        scratch=[dma((7 * nw,)), dma((7 * nw,)), dma((nw,))],
        start=start, finish=finish)


def _adamw(w, g, m, v):
    m = ADAM_B1 * m + (1.0 - ADAM_B1) * g
    v = ADAM_B2 * v + (1.0 - ADAM_B2) * (g * g)
    m_hat = m / (1.0 - ADAM_B1 ** ADAM_STEP)
    v_hat = v / (1.0 - ADAM_B2 ** ADAM_STEP)
    delta = -ADAM_LR * (m_hat / (jnp.sqrt(v_hat) + ADAM_EPS) + ADAM_WD * w)
    return delta, m, v


def _reduce_adamw(name, stack, w, m, v):
    rows, cols = w.shape
    tr = next(t for t in (256, 128, 64) if rows % t == 0)

    def body(s_ref, w_ref, m_ref, v_ref, g_ref, d_ref, nm_ref, nv_ref):
        g = s_ref[0].astype(F32)
        for d in range(1, N_DEV):
            g = g + s_ref[d].astype(F32)
        g_ref[...] = g
        d_ref[...], nm_ref[...], nv_ref[...] = _adamw(w_ref[...], g, m_ref[...], v_ref[...])

    blk = pl.BlockSpec((tr, cols), lambda i: (i, 0))
    return pl.pallas_call(
        body, name=name, grid=(rows // tr,),
        in_specs=[pl.BlockSpec((N_DEV, tr, cols), lambda i: (0, i, 0)), blk, blk, blk],
        out_specs=[blk] * 4, out_shape=[jax.ShapeDtypeStruct((rows, cols), F32)] * 4,
        compiler_params=_cparams(),
    )(stack, w, m, v)


def _small_step(pack, w, m, v):
    def body(p_ref, w_ref, m_ref, v_ref, g_ref, d_ref, nm_ref, nv_ref, loss_ref, all_ref, send_sems, recv_sems):
        x, y, c, _ = _place()
        me = 4 * x + 2 * y + c
        all_ref[me] = p_ref[...]
        sent = []
        for n in range(1, N_DEV):
            peer = me ^ n
            cp = pltpu.make_async_remote_copy(
                src_ref=p_ref, dst_ref=all_ref.at[me], send_sem=send_sems.at[n - 1], recv_sem=recv_sems.at[n - 1],
                device_id=(peer // 4, (peer // 2) % 2, peer % 2), device_id_type=MESH)
            cp.start()
            sent.append(cp)
        for n in range(1, N_DEV):
            peer = me ^ n
            pltpu.make_async_remote_copy(
                src_ref=p_ref, dst_ref=all_ref.at[peer], send_sem=send_sems.at[n - 1], recv_sem=recv_sems.at[n - 1],
                device_id=(peer // 4, (peer // 2) % 2, peer % 2), device_id_type=MESH).wait_recv()
        for cp in sent:
            cp.wait_send()
        tot = all_ref[0]
        for d in range(1, N_DEV):
            tot = tot + all_ref[d]
        g = tot[:SMALL_ROWS]
        g_ref[...] = g
        d_ref[...], nm_ref[...], nv_ref[...] = _adamw(w_ref[...], g, m_ref[...], v_ref[...])
        loss_ref[...] = jnp.sum(jnp.sum(tot[SMALL_ROWS:], axis=1, keepdims=True), axis=0, keepdims=True)

    vm = pl.BlockSpec(memory_space=pltpu.VMEM)
    small = jax.ShapeDtypeStruct((SMALL_ROWS, LANES), F32)
    return pl.pallas_call(
        body, name="small_step",
        in_specs=[vm] * 4, out_specs=[vm] * 5,
        out_shape=[small] * 4 + [jax.ShapeDtypeStruct((1, 1), F32)],
        scratch_shapes=[pltpu.VMEM((N_DEV, PACK_ROWS, LANES), F32),
                        pltpu.SemaphoreType.DMA((N_DEV - 1,)), pltpu.SemaphoreType.DMA((N_DEV - 1,))],
    )(pack, w, m, v)


LATER_WEIGHTS = tuple(name for name, _, _ in BIG if name != "w_in")


def _layer_step(x, mem, tgt, shards, vec):
    s = x.shape[0]
    d = D_MODEL
    tm = min(ROW_TILE, s)
    tl = min(WIDE_TILE, s)
    xb, cos2, sin2, w_in = _prep(x, _gather_plan(("w_in",), shards))
    bf = lambda w: ((s, w), BF16)
    f32 = lambda w: ((s, w), F32)

    w_sb, w_rqk = w_in[:, :OFF_RET_Q], w_in[:, OFF_RET_Q:OFF_RET_V]
    w_rvg, w_gate = w_in[:, OFF_RET_V:OFF_GATE], w_in[:, OFF_GATE:]
    q_scale = lambda width, q_width, scale: jnp.concatenate(
        [jnp.full((1, q_width), scale, F32), jnp.ones((1, width - q_width), F32)], axis=1)
    n_groups = 3 * SB_WIDTH // LANES

    def sb_epi(acc, t, i, j):
        scaled = acc * t[0]
        return [jnp.stack([scaled[:, g * LANES:(g + 1) * LANES] for g in range(n_groups)])], []

    (sb_qkv,) = _mm(
        "in_sb", xb, w_sb, s, 3 * SB_WIDTH, d, tm=tl, tn=3 * SB_WIDTH, tk=d, epi=sb_epi,
        ins=[(q_scale(3 * SB_WIDTH, SB_WIDTH, SB_SCALE), *_rowvec(3 * SB_WIDTH))],
        outs=[((n_groups, s, LANES), BF16, (n_groups, tl, LANES), lambda i, j: (0, i, 0))])

    def rope_epi(acc, t, i, j):
        cos, sin, scale = t
        parts = []
        for g in range(acc.shape[1] // RET_QK):
            xg = acc[:, g * RET_QK:(g + 1) * RET_QK]
            parts.append(xg * cos + _swap_halves(xg) * sin)
        return [jnp.concatenate(parts, axis=1) * scale], []

    rope_in = ((tl, RET_QK), lambda i, j: (i, 0))
    (rqk,) = _mm("in_rqk", xb, w_rqk, s, 2 * RET_QK_WIDTH, d, tm=tl, tn=2 * RET_QK_WIDTH, tk=d, epi=rope_epi,
                 chunk=MXU_COLS,
                 ins=[(cos2, *rope_in), (sin2, *rope_in),
                      (q_scale(2 * RET_QK_WIDTH, RET_QK_WIDTH, RET_SCALE), *_rowvec(2 * RET_QK_WIDTH))],
                 outs=[(*f32(2 * RET_QK_WIDTH), *_tile(tl, 2 * RET_QK_WIDTH))])
    (rvg,) = _mm("in_rvg", xb, w_rvg, s, 2 * RET_V_WIDTH, d, tm=tl, tn=2 * RET_V_WIDTH, tk=d, chunk=MXU_COLS,
                 epi=_plain, outs=[(*bf(2 * RET_V_WIDTH), *_tile(tl, 2 * RET_V_WIDTH))])
    (gates,) = _mm("in_gate", xb, w_gate, s, 2 * d, d, tm=tl, tn=2 * d, tk=d, chunk=MXU_COLS,
                   epi=lambda acc, t, i, j: ([_sigmoid(acc + t[0])], []),
                   ins=[(vec["b_gate"], *_rowvec(2 * d))], outs=[(*bf(2 * d), *_tile(tl, 2 * d))])

    sb_out, sb_out_f32, *gathered = _sb_fwd(sb_qkv, s, comm=_gather_plan(LATER_WEIGHTS, shards))
    wt = dict(zip(LATER_WEIGHTS, gathered, strict=True))
    ret, gated = _ret_fwd(rqk, rvg, s)
    (y_sb,) = _mm("sb_o", sb_out, wt["w_sb_o"], s, d, SB_WIDTH, tm=tl, tn=d, tk=SB_WIDTH, epi=_plain,
                  outs=[(*bf(d), *_tile(tl, d))])
    y_ret, mixin = _mm(
        "ret_o", gated, wt["w_ret_o"], s, d, RET_V_WIDTH, tm=tl, tn=d, tk=RET_V_WIDTH, chunk=MXU_COLS,
        epi=lambda acc, t, i, j: ([acc, t[0].astype(F32) * t[2].astype(F32) + t[1].astype(F32) * acc], []),
        ins=[(gates, *_tile(tl, d)), (gates, *_tile(tl, d, 1)), (y_sb, *_tile(tl, d))],
        outs=[(*bf(d), *_tile(tl, d)), (*bf(d), *_tile(tl, d))])

    def ln_epi(acc, t, i, j):
        *res, g, b = t
        prev = res[0] if len(res) == 1 else res[0] * res[1] + res[2]
        xhat, rstd = _norm(DN_ALPHA * prev + acc)
        return [xhat * g + b, xhat, rstd], []

    full = _tile(tm, d)
    col1 = ((tm, 1), lambda i, j: (i, 0))
    vec_in = lambda name: (vec[name], *_rowvec(d))
    ln_outs = [(*bf(d), *full), (*f32(d), *full), ((s, 1), F32, *col1)]
    x1b, xhat1, rstd1 = _mm(
        "mix_o", mixin, wt["w_mix_o"], s, d, d, tm=tm, tn=d, tk=d, epi=ln_epi,
        ins=[(x, *full), vec_in("ln1_g"), vec_in("ln1_b")], outs=ln_outs)

    (qm,) = _mm("mem_q", x1b, wt["w_mem_q"], s, d, d, tm=tl, tn=d, tk=d,
                epi=lambda acc, t, i, j: ([acc * MEM_SCALE], []), outs=[(*bf(d), *_tile(tl, d))])
    (kv,) = _mm("mem_kv", mem, wt["w_mem_kv"], MEM_LEN, 2 * d, d, tm=MEM_LEN, tn=d, tk=d, epi=_plain,
                outs=[((MEM_LEN, 2 * d), BF16, *_tile(MEM_LEN, d))])
    att = _xattn_fwd(qm, kv, s)
    x2b, xhat2, rstd2 = _mm(
        "mem_o", att, wt["w_mem_o"], s, d, d, tm=tm, tn=d, tk=d, epi=ln_epi,
        ins=[(xhat1, *full), vec_in("ln1_g"), vec_in("ln1_b"), vec_in("ln2_g"), vec_in("ln2_b")], outs=ln_outs)

    fh = FFN_HIDDEN
    tf = fh // 2
    def swiglu_epi(acc, t, i, j):
        a = acc[0].astype(BF16).astype(F32)
        return [a, acc[1], a * _sigmoid(a) * acc[1]], []

    f1, f2, act = _mm(
        "ffn_in", x2b, wt["w_ffn_in"], s, 2 * fh, d, tm=tm, tn=2 * fh, tk=d, epi=swiglu_epi, chunk=MXU_COLS,
        halves=True, outs=[(*bf(fh), *_tile(tm, fh))] * 3)

    def head_epi(acc, t, i, j):
        prev_hat, prev_g, prev_b, g, b, target = t
        xhat, rstd = _norm(DN_ALPHA * (prev_hat * prev_g + prev_b) + acc)
        err = xhat * g + b - target
        dy = err * (1.0 / d)
        du = _norm_bwd(dy * g, xhat, rstd)
        return [du], [_colsum(dy * xhat), _colsum(dy), _colsum(err * err) * (0.5 / d)]

    vec_acc = ((1, d), F32)
    du3b, dg3, db3, loss_cols = _mm(
        "ffn_out", act, wt["w_ffn_out"], s, d, fh, tm=tm, tn=d, tk=fh, epi=head_epi,
        ins=[(xhat2, *full), vec_in("ln2_g"), vec_in("ln2_b"), vec_in("ln3_g"), vec_in("ln3_b"), (tgt, *full)],
        outs=[(*bf(d), *full)], accs=[vec_acc] * 3)

    grads = {}
    ts = min(SEQ_TILE, s)

    def wgrad(name, a, b, m, n, tm_, tn_, tk_=None):
        (g,) = _mm(name, a, b, m, n, a.shape[0], tm=tm_, tn=tn_, tk=tk_ or ts, ta=True, epi=_plain,
                   outs=[((m, n), BF16, *_tile(tm_, tn_))])
        return g

    def ffn_bwd_epi(acc, t, i, j):
        a, b = t[0].astype(F32), t[1].astype(F32)
        sg = _sigmoid(a)
        return [acc * b * (sg * (1.0 + a * (1.0 - sg))), acc * (a * sg)], []

    df1, df2 = _mm(
        "ffn_out_t", du3b, wt["w_ffn_out"], s, fh, d, tm=tm, tn=fh, tk=d, tb=True, epi=ffn_bwd_epi, chunk=MXU_COLS,
        ins=[(f1, *_tile(tm, fh)), (f2, *_tile(tm, fh))],
        outs=[(*bf(fh), *_tile(tm, fh)), (*bf(fh), *_tile(tm, fh))])
    grads["w_ffn_out"] = wgrad("g_ffn_out", act, du3b, fh, d, tf, d)
    grads["w_ffn_in"] = jnp.concatenate(
        [wgrad("g_ffn_in1", x2b, df1, d, fh, d, tf), wgrad("g_ffn_in2", x2b, df2, d, fh, d, tf)], axis=1)
    (dx2a,) = _mm("ffn_in1_t", df1, wt["w_ffn_in"], s, d, fh, tm=tl, tn=d, tk=fh, tb=True, epi=_plain,
                  outs=[(*f32(d), *_tile(tl, d))])

    def ln_bwd(name, a, b, k, tk, b_off, more, scales, xhat, rstd, g):
        def epi(acc, t, i, j):
            *extra, xh, rs, gg = t
            dy = acc
            for e, sc in zip(extra, scales, strict=True):
                dy = dy + e.astype(F32) * sc
            return [_norm_bwd(dy * gg, xh, rs)], [_colsum(dy * xh), _colsum(dy)]

        return _mm(name, a, b, s, d, k, tm=tm, tn=d, tk=tk, tb=True, b_off=b_off, epi=epi,
                   ins=[(e, *full) for e in more] + [(xhat, *full), (rstd, *col1), (g, *_rowvec(d))],
                   outs=[(*bf(d), *full)], accs=[vec_acc] * 2)

    du2b, dg2, db2 = ln_bwd("ffn_in2_t", df2, wt["w_ffn_in"], fh, fh, (0, 1), [dx2a, du3b], [1.0, DN_ALPHA],
                            xhat2, rstd2, vec["ln2_g"])

    (datt,) = _mm("mem_o_t", du2b, wt["w_mem_o"], s, d, d, tm=tl, tn=d, tk=d, tb=True, epi=_plain,
                  outs=[(*bf(d), *_tile(tl, d))])
    grads["w_mem_o"] = wgrad("g_mem_o", att, du2b, d, d, d, d)
    dqm, dkv = _xattn_bwd(qm, kv, datt, s)
    grads["w_mem_q"] = wgrad("g_mem_q", x1b, dqm, d, d, d, d)
    grads["w_mem_kv"] = wgrad("g_mem_kv", mem, dkv, d, 2 * d, d, d, MEM_LEN)
    du1b, dg1, db1 = ln_bwd("mem_q_t", dqm, wt["w_mem_q"], d, d, (0, 0), [du2b], [DN_ALPHA],
                            xhat1, rstd1, vec["ln1_g"])

    def merge_bwd_epi(acc, t, i, j):
        g0, g1, ysb, yret = (v.astype(F32) for v in t)
        dgate0 = acc * ysb * (g0 * (1.0 - g0))
        dgate1 = acc * yret * (g1 * (1.0 - g1))
        return [dgate0, dgate1, acc * g0, acc * g1], [_colsum(dgate0), _colsum(dgate1)]

    dgate0, dgate1, dy_sb, dy_ret, dbg0, dbg1 = _mm(
        "mix_o_t", du1b, wt["w_mix_o"], s, d, d, tm=tm, tn=d, tk=d, tb=True, epi=merge_bwd_epi,
        ins=[(gates, *full), (gates, *_tile(tm, d, 1)), (y_sb, *full), (y_ret, *full)],
        outs=[(*bf(d), *full)] * 4, accs=[vec_acc] * 2)
    grads["w_mix_o"] = wgrad("g_mix_o", mixin, du1b, d, d, d, d)
    grads["w_sb_o"] = wgrad("g_sb_o", sb_out, dy_sb, SB_WIDTH, d, SB_WIDTH, d)
    grads["w_ret_o"] = wgrad("g_ret_o", gated, dy_ret, RET_V_WIDTH, d, RET_V_WIDTH, d)
    (dsb_out,) = _mm("sb_o_t", dy_sb, wt["w_sb_o"], s, SB_WIDTH, d, tm=tl, tn=SB_WIDTH, tk=d, tb=True, epi=_plain,
                     outs=[(*bf(SB_WIDTH), *_tile(tl, SB_WIDTH))])

    def gate_norm_bwd_epi(acc, t, i, j):
        r, g = t[0], t[1].astype(F32)
        drg, dret = [], []
        for h in range(acc.shape[1] // RET_V):
            sl = slice(h * RET_V, (h + 1) * RET_V)
            xhat, rstd = _norm(r[:, sl])
            gg, dd = g[:, sl], acc[:, sl]
            sg = _sigmoid(gg)
            drg.append(dd * xhat * (sg * (1.0 + gg * (1.0 - sg))))
            dret.append(_norm_bwd(dd * (gg * sg), xhat, rstd))
        return [jnp.concatenate(drg, axis=1), jnp.concatenate(dret, axis=1)], []

    drg, dret = _mm(
        "ret_o_t", dy_ret, wt["w_ret_o"], s, RET_V_WIDTH, d, tm=tm, tn=d, tk=d, tb=True, epi=gate_norm_bwd_epi,
        chunk=MXU_COLS,
        ins=[(ret, *full), (rvg, *_tile(tm, d, 1))],
        outs=[(*bf(RET_V_WIDTH), *full)] * 2)

    drq = _ret_bwd_q(rqk, rvg, dret, cos2, sin2, s)
    drk, drv = _ret_bwd_kv(rqk, rvg, dret, cos2, sin2, s)
    dsq, dsk, dsv, *stacked = _sb_bwd(sb_qkv, sb_out_f32, dsb_out, s, comm=_exchange_plan(LATER_WEIGHTS, grads))
    stacks = dict(zip(LATER_WEIGHTS, stacked, strict=True))

    dh = {"sq": dsq, "sk": dsk, "sv": dsv, "rq": drq, "rk": drk, "rv": drv, "rg": drg, "gate0": dgate0,
          "gate1": dgate1}
    def wgrad_in(name, pieces):
        width = sum(p.shape[1] for p in pieces)
        (g,) = _mm(name, xb, pieces, d, width, s, tm=d, tn=width, tk=min(ROW_TILE, s // 2), ta=True, epi=_plain,
                   outs=[((d, width), BF16, *_tile(d, width))])
        return g

    grads["w_in"] = jnp.concatenate(
        [wgrad_in("g_in_mixers", [dsq, dsk, dsv, drq, drk, drv]), wgrad_in("g_in_gates", [drg, dgate0, dgate1])],
        axis=1)
    grad_x, stacks["w_in"] = _mm(
        "in_t", list(dh.values()), w_in, s, d, IN_WIDTH, tm=tm, tn=d, tk=IN_WIDTH, tb=True,
        epi=lambda acc, t, i, j: ([acc + DN_ALPHA * t[0].astype(F32)], []),
        ins=[(du1b, *full)], outs=[(*f32(d), *full)], comm=_exchange_plan(("w_in",), grads))

    small = {"b_gate": jnp.concatenate([dbg0, dbg1], axis=1), "ln1_g": dg1, "ln1_b": db1, "ln2_g": dg2,
             "ln2_b": db2, "ln3_g": dg3, "ln3_b": db3}
    return grad_x, stacks, small, loss_cols


def kernel(x, mem, w_in, b_gate, w_sb_o, w_ret_o, w_mix_o, ln1_g, ln1_b, w_mem_q, w_mem_kv, w_mem_o, ln2_g, ln2_b, w_ffn_in, w_ffn_out, ln3_g, ln3_b, loss_target, m_w_in, m_b_gate, m_w_sb_o, m_w_ret_o, m_w_mix_o, m_ln1_g, m_ln1_b, m_w_mem_q, m_w_mem_kv, m_w_mem_o, m_ln2_g, m_ln2_b, m_w_ffn_in, m_w_ffn_out, m_ln3_g, m_ln3_b, v_w_in, v_b_gate, v_w_sb_o, v_w_ret_o, v_w_mix_o, v_ln1_g, v_ln1_b, v_w_mem_q, v_w_mem_kv, v_w_mem_o, v_ln2_g, v_ln2_b, v_w_ffn_in, v_w_ffn_out, v_ln3_g, v_ln3_b):
    given = dict(locals())
    s = x.shape[1]
    x2d = x.reshape(s, D_MODEL)
    tgt = loss_target.reshape(s, D_MODEL)
    mem2d = mem.reshape(MEM_LEN, D_MODEL)
    shard = {name: given[name].reshape(_shard_shape(shape, axis)) for name, shape, axis in BIG}
    vec = {name: given[name] for name in SMALL}

    shards_bf = {name: _cast_bf16("cast_" + name, shard[name]) for name, _, _ in BIG}

    grad_x, stacks, small, loss_cols = _layer_step(x2d, mem2d, tgt, shards_bf, vec)

    out = {}
    for name, shape, axis in BIG:
        stack = stacks[name]
        shp = given[name].shape
        res = _reduce_adamw("adamw_" + name, stack, shard[name], given["m_" + name].reshape(stack.shape[1:]),
                            given["v_" + name].reshape(stack.shape[1:]))
        out[name] = [r.reshape(shp) for r in res]

    pack = jnp.concatenate([small[name] for name in SMALL] + [loss_cols], axis=1).reshape(PACK_ROWS, LANES)
    cat = lambda pre: jnp.concatenate([given[pre + name] for name in SMALL], axis=1).reshape(SMALL_ROWS, LANES)
    *res, loss = _small_step(pack, cat(""), cat("m_"), cat("v_"))
    flat = [r.reshape(1, SMALL_LEN) for r in res]
    off = 0
    for name in SMALL:
        n = given[name].shape[1]
        out[name] = [r[:, off:off + n] for r in flat]
        off += n

    return (loss.reshape(()), grad_x.reshape(x.shape),
            *[out[name][0] for name in WEIGHT_ORDER], *[out[name][1] for name in WEIGHT_ORDER],
            *[out[name][2] for name in WEIGHT_ORDER], *[out[name][3] for name in WEIGHT_ORDER])
```

```python
import functools

import jax
import jax.numpy as jnp
import numpy as np
from jax import lax
from jax.experimental import pallas as pl
from jax.experimental.pallas import tpu as pltpu

F32, BF16 = jnp.float32, jnp.bfloat16
MESH = pl.DeviceIdType.MESH

D_MODEL = 1024
MEM_LEN = 256
SB_HEADS, SB_DIM, SB_WIDTH = 8, 64, 512
RET_HEADS, RET_QK, RET_V = 4, 128, 256
RET_QK_WIDTH, RET_V_WIDTH = 512, 1024
ROPE_BASE = 10000.0
MEM_HEADS, MEM_DIM = 4, 256
FFN_HIDDEN = 2816
IN_WIDTH = 6656
OFF_RET_Q, OFF_RET_V, OFF_RET_G, OFF_GATE = 1536, 2560, 3584, 4608
DN_ALPHA = 2.0 ** 0.25
LN_EPS = 1e-5
SB_SCALE = SB_DIM ** -0.5
SB_DEAD = -110.0
RET_SCALE = RET_QK ** -0.5
MEM_SCALE = MEM_DIM ** -0.5
ADAM_LR, ADAM_B1, ADAM_B2, ADAM_EPS, ADAM_WD, ADAM_STEP = 0.001, 0.9, 0.999, 1e-08, 0.01, 10

N_DEV, N_CHIPS = 8, 4

LANES = 128
MXU_COLS = 256
VMEM_LIMIT_BYTES = 52 * 2 ** 20
ROW_TILE = 512
WIDE_TILE = 1024
SEQ_TILE = 2048
SB_BLOCK = 256
RET_BLOCK = 256
RET_CHUNKS_PER_STEP = 4
XATTN_ROWS = 1024

BIG = (
    ("w_in", (D_MODEL, IN_WIDTH), 1),
    ("w_sb_o", (SB_WIDTH, D_MODEL), 1),
    ("w_ret_o", (RET_V_WIDTH, D_MODEL), 0),
    ("w_mix_o", (D_MODEL, D_MODEL), 0),
    ("w_mem_q", (D_MODEL, D_MODEL), 0),
    ("w_mem_kv", (D_MODEL, 2 * D_MODEL), 1),
    ("w_mem_o", (D_MODEL, D_MODEL), 0),
    ("w_ffn_in", (D_MODEL, 2 * FFN_HIDDEN), 1),
    ("w_ffn_out", (FFN_HIDDEN, D_MODEL), 0),
)
SMALL = ("b_gate", "ln1_g", "ln1_b", "ln2_g", "ln2_b", "ln3_g", "ln3_b")
SMALL_LEN = 2 * D_MODEL + 6 * D_MODEL
SMALL_ROWS = SMALL_LEN // LANES
PACK_ROWS = SMALL_ROWS + D_MODEL // LANES
WEIGHT_ORDER = ("w_in", "b_gate", "w_sb_o", "w_ret_o", "w_mix_o", "ln1_g", "ln1_b", "w_mem_q", "w_mem_kv",
                "w_mem_o", "ln2_g", "ln2_b", "w_ffn_in", "w_ffn_out", "ln3_g", "ln3_b")


def _cparams():
    return pltpu.CompilerParams(vmem_limit_bytes=VMEM_LIMIT_BYTES)


def _dot(a, b, ca, cb):
    return lax.dot_general(a, b, (((ca,), (cb,)), ((), ())), preferred_element_type=F32)


def _sigmoid(x):
    return 1.0 / (1.0 + jnp.exp(-x))


def _mm(name, a, b, m, n, k, *, tm, tn, tk, epi, outs, ins=(), accs=(), ta=False, tb=False,
        a_off=(0, 0), b_off=(0, 0), j_outer=False, comm=None, chunk=None, halves=False):
    assert not halves or (chunk is not None and (tn // 2) % chunk == 0 and not ins), name
    assert m % tm == 0 and n % tn == 0 and k % tk == 0, (name, m, n, k, tm, tn, tk)
    assert chunk is None or (k == tk and tn % chunk == 0), name
    ni, nj, nk = m // tm, n // tn, k // tk
    assert not accs or nj == 1, name
    ij = (lambda g0, g1: (g1, g0)) if j_outer else (lambda g0, g1: (g0, g1))

    def spec(block, index):
        return pl.BlockSpec(block, lambda g0, g1, kk: index(*ij(g0, g1), kk))

    a_list = list(a) if isinstance(a, (list, tuple)) else [a]
    n_a = len(a_list)
    if n_a > 1:
        assert not ta and nk == 1 and chunk is None and not any(a_off), name
        assert sum(p.shape[1] for p in a_list) == k, name
        a_specs = [spec((tm, p.shape[1]), lambda i, j, kk: (i, 0)) for p in a_list]
    elif ta:
        a_specs = [spec((tk, tm), lambda i, j, kk: (kk + a_off[0], i + a_off[1]))]
    else:
        a_specs = [spec((tm, tk), lambda i, j, kk: (i + a_off[0], kk + a_off[1]))]
    b_list = list(b) if isinstance(b, (list, tuple)) else [b]
    n_b = len(b_list)
    if n_b > 1:
        assert not tb and nj == 1 and nk > 1 and n_a == 1 and chunk is None and not any(b_off), name
        assert sum(p.shape[1] for p in b_list) == n, name
        b_specs = [spec((tk, p.shape[1]), lambda i, j, kk: (kk, 0)) for p in b_list]
    elif tb:
        b_specs = [spec((tn, tk), lambda i, j, kk: (j + b_off[0], kk + b_off[1]))]
    else:
        b_specs = [spec((tk, tn), lambda i, j, kk: (kk + b_off[0], j + b_off[1]))]
    if n_a > 1 and nj == 1:
        b_specs = [pl.BlockSpec(b_specs[0].block_shape, b_specs[0].index_map, pipeline_mode=pl.Buffered(1))]
    in_specs = [*a_specs, *b_specs]
    for _, bs, im in ins:
        in_specs.append(spec(bs, lambda i, j, kk, im=im: im(i, j)))
    out_specs, out_shape = [], []
    for shape, dtype, bs, im in outs:
        out_specs.append(spec(bs, lambda i, j, kk, im=im: im(i, j)))
        out_shape.append(jax.ShapeDtypeStruct(shape, dtype))
    for shape, dtype in accs:
        out_specs.append(spec(shape, lambda i, j, kk, nd=len(shape): (0,) * nd))
        out_shape.append(jax.ShapeDtypeStruct(shape, dtype))
    n_in, n_out, n_acc = len(ins), len(outs), len(accs)
    ca, cb = (0 if ta else 1), (1 if tb else 0)
    grid = (*ij(ni, nj), nk)
    comm_ins, comm_outs, comm_scratch = [], [], []
    if comm is not None:
        comm_in_specs, comm_out_specs = comm.specs
        comm_ins, comm_outs, comm_scratch = list(comm.ins), list(comm.out_shape), list(comm.scratch)
        in_specs += comm_in_specs
        out_specs += comm_out_specs
        out_shape += comm_outs
    n_ci, n_co = len(comm_ins), len(comm_outs)

    def body(*refs):
        a_refs, b_refs, refs = refs[:n_a], refs[n_a:n_a + n_b], refs[n_a + n_b:]
        a_ref, b_ref = a_refs[0], b_refs[0]
        in_refs = refs[:n_in]
        ci_refs = refs[n_in:n_in + n_ci]
        rest = refs[n_in + n_ci:]
        out_refs, acc_refs = rest[:n_out], rest[n_out:n_out + n_acc]
        co_refs = rest[n_out + n_acc:n_out + n_acc + n_co]
        scratch = rest[n_out + n_acc + n_co:]
        sem_refs, scratch = scratch[:len(comm_scratch)], scratch[len(comm_scratch):]
        (i, j), kk = ij(pl.program_id(0), pl.program_id(1)), pl.program_id(2)
        if comm is not None:
            first_step, last_step = _grid_ends(grid)
            pl.when(first_step)(lambda: comm.start(ci_refs, co_refs, sem_refs))
        def finish(acc, cols=slice(None)):
            def of(r):
                return r[..., cols] if r.shape[-1] == tn else r[...]

            o_tiles, a_tiles = epi(acc, [of(r) for r in in_refs], i, j)
            for r, t in zip(out_refs, o_tiles, strict=True):
                r[..., cols] = t.astype(r.dtype)
            if n_acc:
                @pl.when(i == 0)
                def _():
                    for r, t in zip(acc_refs, a_tiles, strict=True):
                        r[..., cols] = t

                @pl.when(i > 0)
                def _():
                    for r, t in zip(acc_refs, a_tiles, strict=True):
                        r[..., cols] += t

        if chunk is not None:
            a_tile = a_ref[...].astype(BF16)

            def product(c0):
                b_part = b_ref[c0:c0 + chunk, :] if tb else b_ref[:, c0:c0 + chunk]
                return _dot(a_tile, b_part.astype(BF16), ca, cb)

            for c0 in range(0, tn // 2 if halves else tn, chunk):
                acc = (product(c0), product(tn // 2 + c0)) if halves else product(c0)
                finish(acc, slice(c0, c0 + chunk))
            if comm is not None:
                pl.when(last_step)(lambda: comm.finish(ci_refs, co_refs, sem_refs))
            return

        if n_b > 1:
            acc_ref = scratch[0]

            def accumulate(first):
                a_tile, c0 = a_ref[...].astype(BF16), 0
                for r in b_refs:
                    c1 = c0 + r.shape[1]
                    term = _dot(a_tile, r[...].astype(BF16), ca, cb)
                    acc_ref[:, c0:c1] = term if first else acc_ref[:, c0:c1] + term
                    c0 = c1

            pl.when(kk == 0)(lambda: accumulate(True))
            pl.when(kk > 0)(lambda: accumulate(False))
            pl.when(kk == nk - 1)(lambda: finish(acc_ref[...]))
            if comm is not None:
                pl.when(last_step)(lambda: comm.finish(ci_refs, co_refs, sem_refs))
            return

        if n_a > 1:
            part, c0 = None, 0
            for r in a_refs:
                c1 = c0 + r.shape[1]
                b_part = b_ref[:, c0:c1] if tb else b_ref[c0:c1, :]
                term = _dot(r[...].astype(BF16), b_part.astype(BF16), ca, cb)
                part, c0 = (term if part is None else part + term), c1
        else:
            part = _dot(a_ref[...].astype(BF16), b_ref[...].astype(BF16), ca, cb)
        if nk == 1:
            finish(part)
        else:
            acc_ref = scratch[0]

            @pl.when(kk == 0)
            def _():
                acc_ref[...] = part

            @pl.when(kk > 0)
            def _():
                acc_ref[...] += part

            @pl.when(kk == nk - 1)
            def _():
                finish(acc_ref[...])

        if comm is not None:
            pl.when(last_step)(lambda: comm.finish(ci_refs, co_refs, sem_refs))

    res = pl.pallas_call(
        body, name=name, grid=grid, in_specs=in_specs, out_specs=out_specs, out_shape=out_shape,
        scratch_shapes=comm_scratch + ([pltpu.VMEM((tm, tn), F32)] if nk > 1 else []),
        compiler_params=_cparams(),
    )(*a_list, *b_list, *[x for x, _, _ in ins], *comm_ins)
    return res


def _grid_ends(grid):
    ids = [pl.program_id(ax) for ax in range(len(grid))]
    first = functools.reduce(jnp.logical_and, [p == 0 for p in ids])
    last = functools.reduce(jnp.logical_and, [p == n - 1 for p, n in zip(ids, grid, strict=True)])
    return first, last


def _tile(tm, tn, dj=0):
    return (tm, tn), (lambda i, j: (i, j + dj))


def _rowvec(tn, dj=0):
    return (1, tn), (lambda i, j: (0, j + dj))


def _plain(acc, tiles, i, j):
    return [acc], []


def _ew(name, fn, ins, outs, rows, tr):
    assert rows % tr == 0, (name, rows, tr)
    in_specs = []
    for x in ins:
        if x.shape[0] == rows:
            in_specs.append(pl.BlockSpec((tr, x.shape[1]), lambda i: (i, 0)))
        else:
            in_specs.append(pl.BlockSpec(x.shape, lambda i: (0, 0)))
    n_in = len(ins)

    def body(*refs):
        res = fn(*[r[...] for r in refs[:n_in]])
        for r, t in zip(refs[n_in:], res, strict=True):
            r[...] = t.astype(r.dtype)

    return pl.pallas_call(
        body, name=name, grid=(rows // tr,), in_specs=in_specs,
        out_specs=[pl.BlockSpec((tr, w), lambda i: (i, 0)) for w, _ in outs],
        out_shape=[jax.ShapeDtypeStruct((rows, w), dt) for w, dt in outs],
        compiler_params=_cparams(),
    )(*ins)


def _cast_bf16(name, x):
    rows = x.shape[0]
    tr = next(t for t in (512, 256, 64) if rows % t == 0)
    return _ew(name, lambda v: (v,), [x], [(x.shape[1], BF16)], rows, tr)[0]


def _prep(x, comm):
    s = x.shape[0]
    half = RET_QK // 2
    inv = 1.0 / (ROPE_BASE ** (jnp.arange(half, dtype=F32) / half))
    inv2 = jnp.concatenate([inv, inv]).reshape(1, RET_QK)
    sign = jnp.concatenate([-jnp.ones((half,), F32), jnp.ones((half,), F32)]).reshape(1, RET_QK)
    tr = min(ROW_TILE, s)
    grid = (s // tr,)
    c_in_specs, c_out_specs, c_out_shape, c_scratch, c_ins, split = _host(comm, 3, 3)

    def body(*refs):
        (x_ref, inv_ref, sign_ref), (xb_ref, cos_ref, sin_ref), _, riding = split(refs)
        i = pl.program_id(0)
        first_step, last_step = _grid_ends(grid)
        pl.when(first_step)(lambda: comm.start(*riding))
        xb_ref[...] = x_ref[...].astype(BF16)
        pos = (lax.broadcasted_iota(jnp.int32, (tr, RET_QK), 0) + i * tr).astype(F32)
        ang = pos * inv_ref[...]
        cos_ref[...] = jnp.cos(ang)
        sin_ref[...] = jnp.sin(ang) * sign_ref[...]
        pl.when(last_step)(lambda: comm.finish(*riding))

    vec = pl.BlockSpec((1, RET_QK), lambda i: (0, 0))
    row = lambda w: pl.BlockSpec((tr, w), lambda i: (i, 0))
    return pl.pallas_call(
        body, name="prep", grid=grid,
        in_specs=[row(D_MODEL), vec, vec] + c_in_specs,
        out_specs=[row(D_MODEL), row(RET_QK), row(RET_QK)] + c_out_specs,
        out_shape=[jax.ShapeDtypeStruct((s, D_MODEL), BF16), jax.ShapeDtypeStruct((s, RET_QK), F32),
                   jax.ShapeDtypeStruct((s, RET_QK), F32)] + c_out_shape,
        scratch_shapes=c_scratch, compiler_params=_cparams(),
    )(x, inv2, sign, *c_ins)


def _swap_halves(x):
    return pltpu.roll(x, RET_QK // 2, 1)


def _norm(u):
    mu = jnp.mean(u, axis=-1, keepdims=True)
    d = u - mu
    var = jnp.mean(d * d, axis=-1, keepdims=True)
    rstd = lax.rsqrt(var + LN_EPS)
    return d * rstd, rstd


def _norm_bwd(dxh, xhat, rstd):
    m1 = jnp.mean(dxh, axis=-1, keepdims=True)
    m2 = jnp.mean(dxh * xhat, axis=-1, keepdims=True)
    return rstd * (dxh - m1 - xhat * m2)


def _colsum(t):
    return jnp.sum(t, axis=0, keepdims=True)


def _split_mm(t, tri):
    hi = t.astype(BF16)
    lo = (t - hi.astype(F32)).astype(BF16)
    return _dot(hi, tri, 1, 0) + _dot(lo, tri, 1, 0)


def _sb_masks():
    t = SB_BLOCK
    lane = lax.broadcasted_iota(jnp.int32, (1, LANES), 1)
    first = lane < SB_DIM
    m0 = jnp.where(first, 1.0, 0.0).astype(BF16)
    m1 = jnp.where(first, 0.0, 1.0).astype(BF16)
    row = lax.broadcasted_iota(jnp.int32, (t, t), 0)
    col = lax.broadcasted_iota(jnp.int32, (t, t), 1)
    return first, (m0, m1), row, col


def _sb_logits(qh, k, causal):
    z = _dot(qh, k, 1, 1)
    lp = jnp.log(1.0 + jnp.exp(-jnp.abs(z)))
    a = jnp.minimum(z, 0.0) - lp
    r = jnp.minimum(-z, 0.0) - lp
    if causal is not None:
        r = jnp.where(causal, r, 0.0)
    return a, r


def _sb_walk(i, blocks, l_ref, causal):
    pl.when(i == 0)(lambda: blocks([(i, causal)]))
    pl.when(i > 0)(lambda: blocks([(i, causal), (i - 1, None)]))

    def alive():
        top = jnp.max(functools.reduce(jnp.maximum, [l_ref[c] for c in range(l_ref.shape[0])]))
        return jnp.where(top > SB_DEAD, 1, 0)

    def cond(c):
        return jnp.logical_and(c[0] < i, c[1] > 0)

    def step(c):
        blocks([(i - 1 - c[0], None)])
        return c[0] + 1, alive()

    lax.while_loop(cond, step, (jnp.int32(1), alive()))


def _host(comm, n_in, n_out):
    if comm is None:
        return [], [], [], [], [], lambda refs: (refs[:n_in], refs[n_in:n_in + n_out], refs[n_in + n_out:], None)
    in_specs, out_specs = comm.specs
    n_ci, n_co, n_sem = len(comm.ins), len(comm.out_shape), len(comm.scratch)

    def split(refs):
        ins, ci = refs[:n_in], refs[n_in:n_in + n_ci]
        rest = refs[n_in + n_ci:]
        outs, co = rest[:n_out], rest[n_out:n_out + n_co]
        sems, scratch = rest[n_out + n_co:n_out + n_co + n_sem], rest[n_out + n_co + n_sem:]
        return ins, outs, scratch, (ci, co, sems)

    return in_specs, out_specs, list(comm.out_shape), list(comm.scratch), list(comm.ins), split


def _sb_qkv_specs(s, g):
    groups = SB_HEADS // 2 // g
    return [pl.BlockSpec((g, SB_BLOCK, LANES), lambda p, i: (p, i, 0)),
            pl.BlockSpec((g, s, LANES), lambda p, i: (groups + p, 0, 0)),
            pl.BlockSpec((g, s, LANES), lambda p, i: (2 * groups + p, 0, 0))]


def _sb_fwd(qkv, s, comm=None):
    t = SB_BLOCK
    g = 2
    nq = s // t
    grid = (SB_HEADS // 2 // g, nq)
    c_in_specs, c_out_specs, c_out_shape, c_scratch, c_ins, split = _host(comm, 3, 2)

    def body(*refs):
        (q_ref, k_ref, v_ref), (o_ref, of_ref), (l_ref, acc_ref), riding = split(refs)
        i = pl.program_id(1)
        if comm is not None:
            first_step, last_step = _grid_ends(grid)
            pl.when(first_step)(lambda: comm.start(*riding))
        first, hmask, row, col = _sb_masks()
        after = jnp.where(row > col, 1.0, 0.0).astype(BF16)
        causal = col < row
        heads = [(p, h) for p in range(g) for h in range(2)]
        qh = {(p, h): q_ref[p] * hmask[h] for p, h in heads}
        l_ref[...] = jnp.zeros_like(l_ref)
        acc_ref[...] = jnp.zeros_like(acc_ref)

        def blocks(todo):
            chains = [(b, p, h) for b in range(len(todo)) for p, h in heads]
            starts = [pl.multiple_of(kb * t, t) for kb, _ in todo]
            ks = {(b, p): k_ref[p, pl.ds(st, t), :] for b, st in enumerate(starts) for p in range(g)}
            vs = {(b, p): v_ref[p, pl.ds(st, t), :] for b, st in enumerate(starts) for p in range(g)}
            ar = {(b, p, h): _sb_logits(qh[p, h], ks[b, p], todo[b][1]) for b, p, h in chains}
            later = {c: _split_mm(ar[c][1], after) for c in chains}
            carry = {(p, h): l_ref[2 * p + h] for p, h in heads}
            w = {}
            for b, (_, mask) in enumerate(todo):
                for p, h in heads:
                    wc = jnp.exp(ar[b, p, h][0] + later[b, p, h] + carry[p, h])
                    w[b, p, h] = wc if mask is None else jnp.where(mask, wc, 0.0)
                carry = {(p, h): carry[p, h] + jnp.sum(ar[b, p, h][1], axis=1, keepdims=True) for p, h in heads}
            pv = {(b, p, h): _dot(w[b, p, h].astype(BF16), vs[b, p], 1, 0) for b, p, h in chains}
            for p in range(g):
                lanes = slice(p * LANES, (p + 1) * LANES)
                acc = acc_ref[:, lanes]
                for b in range(len(todo)):
                    acc = acc + jnp.where(first, pv[b, p, 0], pv[b, p, 1])
                acc_ref[:, lanes] = acc
            for p, h in heads:
                l_ref[2 * p + h] = carry[p, h]

        _sb_walk(i, blocks, l_ref, causal)
        o_ref[...] = acc_ref[...].astype(o_ref.dtype)
        of_ref[...] = acc_ref[...]
        if comm is not None:
            pl.when(last_step)(lambda: comm.finish(*riding))

    blk = pl.BlockSpec((t, g * LANES), lambda p, i: (i, p))
    return pl.pallas_call(
        body, name="sb_fwd", grid=grid,
        in_specs=_sb_qkv_specs(s, g) + c_in_specs,
        out_specs=[blk, blk] + c_out_specs,
        out_shape=[jax.ShapeDtypeStruct((s, SB_WIDTH), BF16), jax.ShapeDtypeStruct((s, SB_WIDTH), F32)] + c_out_shape,
        scratch_shapes=c_scratch + [pltpu.VMEM((2 * g, t, 1), F32), pltpu.VMEM((t, g * LANES), F32)],
        compiler_params=_cparams(),
    )(qkv, qkv, qkv, *c_ins)


def _sb_bwd(qkv, o, do, s, comm=None):
    t = SB_BLOCK
    g = 2
    nq = s // t
    grid = (SB_HEADS // 2 // g, nq)
    c_in_specs, c_out_specs, c_out_shape, c_scratch, c_ins, split = _host(comm, 5, 3)

    def body(*refs):
        ((q_ref, k_ref, v_ref, o_ref, do_ref), (dq_ref, dk_ref, dv_ref),
         (l_ref, e_ref, dq_acc, dk_acc, dv_acc), riding) = split(refs)
        i = pl.program_id(1)
        if comm is not None:
            first_step, last_step = _grid_ends(grid)
            pl.when(first_step)(lambda: comm.start(*riding))
        first, hmask, row, col = _sb_masks()
        after = jnp.where(row > col, 1.0, 0.0).astype(BF16)
        from_here = jnp.where(row >= col, 1.0, 0.0).astype(BF16)
        causal = col < row

        @pl.when(i == 0)
        def _():
            dk_acc[...] = jnp.zeros_like(dk_acc)
            dv_acc[...] = jnp.zeros_like(dv_acc)

        heads = [(p, h) for p in range(g) for h in range(2)]
        lanes = [slice(p * LANES, (p + 1) * LANES) for p in range(g)]
        q = [q_ref[p] for p in range(g)]
        do_ = [do_ref[:, lanes[p]] for p in range(g)]
        qh = {(p, h): q[p] * hmask[h] for p, h in heads}
        doh = {(p, h): do_[p] * hmask[h] for p, h in heads}
        total = {}
        for p in range(g):
            prod = do_[p].astype(F32) * o_ref[:, lanes[p]]
            total[p, 0] = jnp.sum(jnp.where(first, prod, 0.0), axis=1, keepdims=True)
            total[p, 1] = jnp.sum(jnp.where(first, 0.0, prod), axis=1, keepdims=True)
        l_ref[...] = jnp.zeros_like(l_ref)
        e_ref[...] = jnp.zeros_like(e_ref)
        dq_acc[...] = jnp.zeros_like(dq_acc)

        def blocks(todo):
            chains = [(b, p, h) for b in range(len(todo)) for p, h in heads]
            starts = [pl.multiple_of(kb * t, t) for kb, _ in todo]
            ks = {(b, p): k_ref[p, pl.ds(st, t), :] for b, st in enumerate(starts) for p in range(g)}
            vs = {(b, p): v_ref[p, pl.ds(st, t), :] for b, st in enumerate(starts) for p in range(g)}
            ar = {(b, p, h): _sb_logits(qh[p, h], ks[b, p], todo[b][1]) for b, p, h in chains}
            dw = {(b, p, h): _dot(doh[p, h], vs[b, p], 1, 1) for b, p, h in chains}
            later = {c: _split_mm(ar[c][1], after) for c in chains}
            carry = {(p, h): l_ref[2 * p + h] for p, h in heads}
            wb = {}
            for b, (_, mask) in enumerate(todo):
                for p, h in heads:
                    wc = jnp.exp(ar[b, p, h][0] + later[b, p, h] + carry[p, h])
                    wb[b, p, h] = (wc if mask is None else jnp.where(mask, wc, 0.0)).astype(BF16)
                carry = {(p, h): carry[p, h] + jnp.sum(ar[b, p, h][1], axis=1, keepdims=True) for p, h in heads}
            dvs = {(b, p, h): _dot(wb[b, p, h], do_[p], 0, 0) for b, p, h in chains}
            e = {c: dw[c] * wb[c].astype(F32) for c in chains}
            suffix = {c: _split_mm(e[c], from_here) for c in chains}
            e_carry = {(p, h): e_ref[2 * p + h] for p, h in heads}
            dz = {}
            for b, (_, mask) in enumerate(todo):
                for p, h in heads:
                    before = total[p, h] - (suffix[b, p, h] + e_carry[p, h])
                    dzc = e[b, p, h] - jnp.exp(ar[b, p, h][0]) * (e[b, p, h] + before)
                    dz[b, p, h] = (dzc if mask is None else jnp.where(mask, dzc, 0.0)).astype(BF16)
                e_carry = {(p, h): e_carry[p, h] + jnp.sum(e[b, p, h], axis=1, keepdims=True) for p, h in heads}
            dqs = {(b, p, h): _dot(dz[b, p, h], ks[b, p], 1, 0) for b, p, h in chains}
            dks = {(b, p, h): _dot(dz[b, p, h], q[p], 0, 0) for b, p, h in chains}
            for p in range(g):
                dq = dq_acc[:, lanes[p]]
                for b, st in enumerate(starts):
                    dq = dq + jnp.where(first, dqs[b, p, 0], dqs[b, p, 1])
                    dk_acc[pl.ds(st, t), lanes[p]] += jnp.where(first, dks[b, p, 0], dks[b, p, 1])
                    dv_acc[pl.ds(st, t), lanes[p]] += jnp.where(first, dvs[b, p, 0], dvs[b, p, 1])
                dq_acc[:, lanes[p]] = dq
            for p, h in heads:
                l_ref[2 * p + h] = carry[p, h]
                e_ref[2 * p + h] = e_carry[p, h]

        _sb_walk(i, blocks, l_ref, causal)
        dq_ref[...] = (dq_acc[...] * SB_SCALE).astype(dq_ref.dtype)

        @pl.when(i == nq - 1)
        def _():
            dk_ref[...] = dk_acc[...].astype(dk_ref.dtype)
            dv_ref[...] = dv_acc[...].astype(dv_ref.dtype)

        if comm is not None:
            pl.when(last_step)(lambda: comm.finish(*riding))

    once = pl.Buffered(1)
    q_spec, k_spec, v_spec = _sb_qkv_specs(s, g)
    k_spec = pl.BlockSpec(k_spec.block_shape, k_spec.index_map, pipeline_mode=once)
    v_spec = pl.BlockSpec(v_spec.block_shape, v_spec.index_map, pipeline_mode=once)
    blk = pl.BlockSpec((t, g * LANES), lambda p, i: (i, p))
    col_blk = pl.BlockSpec((s, g * LANES), lambda p, i: (0, p), pipeline_mode=once)
    sds = jax.ShapeDtypeStruct((s, SB_WIDTH), BF16)
    return pl.pallas_call(
        body, name="sb_bwd", grid=grid,
        in_specs=[q_spec, k_spec, v_spec, blk, blk] + c_in_specs,
        out_specs=[blk, col_blk, col_blk] + c_out_specs,
        out_shape=[sds, sds, sds] + c_out_shape,
        scratch_shapes=c_scratch + [pltpu.VMEM((2 * g, t, 1), F32), pltpu.VMEM((2 * g, t, 1), F32),
                                    pltpu.VMEM((t, g * LANES), F32), pltpu.VMEM((s, g * LANES), F32),
                                    pltpu.VMEM((s, g * LANES), F32)],
        compiler_params=_cparams(),
    )(qkv, qkv, qkv, o, do, *c_ins)


def _ret_log_gamma():
    lg = np.log1p(-np.exp2(-5.0 - np.arange(RET_HEADS, dtype=np.float32))).astype(np.float32)
    return jnp.asarray(np.broadcast_to(lg[:, None, None], (RET_HEADS, 8, LANES)).copy())


RET_SCRATCH = [pltpu.VMEM((RET_HEADS, RET_QK, RET_V), F32),
               pltpu.VMEM((RET_HEADS, RET_BLOCK, RET_BLOCK), F32),
               pltpu.VMEM((RET_HEADS, RET_BLOCK, 1), F32),
               pltpu.VMEM((RET_HEADS, RET_BLOCK, 1), F32)]


def _ret_begin(n, lg_ref, state, within, q_dec, k_dec):
    @pl.when(n == 0)
    def _():
        c = RET_BLOCK
        state[...] = jnp.zeros_like(state)
        row = lax.broadcasted_iota(jnp.int32, (c, c), 0)
        col = lax.broadcasted_iota(jnp.int32, (c, c), 1)
        rel = jnp.maximum(row - col, 0).astype(F32)
        idx = lax.broadcasted_iota(jnp.int32, (c, 1), 0).astype(F32)
        for h in range(RET_HEADS):
            lg = lg_ref[h, 0:1, 0:1]
            within[h] = jnp.where(row >= col, jnp.exp(lg * rel), 0.0)
            q_dec[h] = jnp.exp(lg * (idx + 1.0))
            k_dec[h] = jnp.exp(lg * (c - 1.0 - idx))


def _chunk_decay(lg_ref, h):
    return jnp.exp(lg_ref[h, 0:1, 0:1] * float(RET_BLOCK))


def _ret_heads(x, width):
    return [x[:, h * width:(h + 1) * width] for h in range(RET_HEADS)]


def _ret_specs(s, reverse=False):
    c = RET_BLOCK
    per_step = min(RET_CHUNKS_PER_STEP, s // c)
    rows = c * per_step
    nc = s // rows
    pos = (lambda n: nc - 1 - n) if reverse else (lambda n: n)
    chunks = [slice(u * c, (u + 1) * c) for u in range(per_step)]
    q_spec = pl.BlockSpec((rows, RET_QK_WIDTH), lambda n: (pos(n), 0))
    k_spec = pl.BlockSpec((rows, RET_QK_WIDTH), lambda n: (pos(n), 1))
    v_spec = pl.BlockSpec((rows, RET_V_WIDTH), lambda n: (pos(n), 0))
    lg_spec = pl.BlockSpec((RET_HEADS, 8, LANES), lambda n: (0, 0, 0))
    rope_spec = pl.BlockSpec((rows, RET_QK), lambda n: (pos(n), 0))
    return nc, chunks[::-1] if reverse else chunks, q_spec, k_spec, v_spec, lg_spec, rope_spec


def _ret_fwd(rqk, rvg, s):
    nc, chunks, q_spec, k_spec, v_spec, lg_spec, _ = _ret_specs(s)
    g_spec = pl.BlockSpec(v_spec.block_shape, lambda n: (n, 1))
    heads = range(RET_HEADS)

    def body(q_ref, k_ref, v_ref, g_ref, lg_ref, r_ref, y_ref, state, within, q_dec, k_dec):
        n = pl.program_id(0)
        _ret_begin(n, lg_ref, state, within, q_dec, k_dec)
        for rows in chunks:
            q, k = _ret_heads(q_ref[rows], RET_QK), _ret_heads(k_ref[rows], RET_QK)
            v, g = _ret_heads(v_ref[rows], RET_V), _ret_heads(g_ref[rows], RET_V)
            scores = [_dot(q[h].astype(BF16), k[h].astype(BF16), 1, 1) * within[h] for h in heads]
            cross = [_dot((q[h] * q_dec[h]).astype(BF16), state[h].astype(BF16), 1, 0) for h in heads]
            out = [_dot(scores[h].astype(BF16), v[h], 1, 0) + cross[h] for h in heads]
            grown = [_dot((k[h] * k_dec[h]).astype(BF16), v[h], 0, 0) for h in heads]
            for h in heads:
                sl = slice(h * RET_V, (h + 1) * RET_V)
                r_ref[rows, sl] = out[h]
                xhat, _ = _norm(out[h])
                gh = g[h].astype(F32)
                y_ref[rows, sl] = (gh * _sigmoid(gh) * xhat).astype(y_ref.dtype)
                state[h] = state[h] * _chunk_decay(lg_ref, h) + grown[h]

    return pl.pallas_call(
        body, name="ret_fwd", grid=(nc,),
        in_specs=[q_spec, k_spec, v_spec, g_spec, lg_spec],
        out_specs=[v_spec, v_spec],
        out_shape=[jax.ShapeDtypeStruct((s, RET_V_WIDTH), F32), jax.ShapeDtypeStruct((s, RET_V_WIDTH), BF16)],
        scratch_shapes=RET_SCRATCH,
        compiler_params=_cparams(),
    )(rqk, rqk, rvg, rvg, _ret_log_gamma())


def _rope_bwd(d, cos, sin):
    return d * cos + _swap_halves(d * sin)


def _ret_bwd_q(rqk, rv, d_out, cos2, sin2, s):
    nc, chunks, q_spec, k_spec, v_spec, lg_spec, rope_spec = _ret_specs(s)
    heads = range(RET_HEADS)

    def body(k_ref, v_ref, d_ref, lg_ref, cos_ref, sin_ref, dq_ref, state, within, q_dec, k_dec):
        n = pl.program_id(0)
        _ret_begin(n, lg_ref, state, within, q_dec, k_dec)
        for rows in chunks:
            k = _ret_heads(k_ref[rows], RET_QK)
            v, d = _ret_heads(v_ref[rows], RET_V), _ret_heads(d_ref[rows], RET_V)
            cos, sin = cos_ref[rows], sin_ref[rows]
            d_scores = [_dot(d[h], v[h], 1, 1) * within[h] for h in heads]
            cross = [q_dec[h] * _dot(d[h], state[h].astype(BF16), 1, 1) for h in heads]
            dq = [_dot(d_scores[h].astype(BF16), k[h].astype(BF16), 1, 0) + cross[h] for h in heads]
            grown = [_dot((k[h] * k_dec[h]).astype(BF16), v[h], 0, 0) for h in heads]
            for h in heads:
                sl = slice(h * RET_QK, (h + 1) * RET_QK)
                dq_ref[rows, sl] = (_rope_bwd(dq[h], cos, sin) * RET_SCALE).astype(dq_ref.dtype)
                state[h] = state[h] * _chunk_decay(lg_ref, h) + grown[h]

    return pl.pallas_call(
        body, name="ret_bwd_q", grid=(nc,),
        in_specs=[k_spec, v_spec, v_spec, lg_spec, rope_spec, rope_spec],
        out_specs=q_spec,
        out_shape=jax.ShapeDtypeStruct((s, RET_QK_WIDTH), BF16),
        scratch_shapes=RET_SCRATCH,
        compiler_params=_cparams(),
    )(rqk, rv, d_out, _ret_log_gamma(), cos2, sin2)


def _ret_bwd_kv(rqk, rv, d_out, cos2, sin2, s):
    nc, chunks, q_spec, k_spec, v_spec, lg_spec, rope_spec = _ret_specs(s, reverse=True)
    heads = range(RET_HEADS)

    def body(q_ref, k_ref, v_ref, d_ref, lg_ref, cos_ref, sin_ref, dk_ref, dv_ref, state, within, q_dec, k_dec):
        n = pl.program_id(0)
        _ret_begin(n, lg_ref, state, within, q_dec, k_dec)
        for rows in chunks:
            q, k = _ret_heads(q_ref[rows], RET_QK), _ret_heads(k_ref[rows], RET_QK)
            v, d = _ret_heads(v_ref[rows], RET_V), _ret_heads(d_ref[rows], RET_V)
            cos, sin = cos_ref[rows], sin_ref[rows]
            qb, kb = [q[h].astype(BF16) for h in heads], [k[h].astype(BF16) for h in heads]
            st = [state[h].astype(BF16) for h in heads]
            scores = [_dot(qb[h], kb[h], 1, 1) * within[h] for h in heads]
            d_scores = [_dot(d[h], v[h], 1, 1) * within[h] for h in heads]
            dk = [_dot(d_scores[h].astype(BF16), qb[h], 0, 0) + k_dec[h] * _dot(v[h], st[h], 1, 1) for h in heads]
            dv = [_dot(scores[h].astype(BF16), d[h], 0, 0) + k_dec[h] * _dot(kb[h], st[h], 1, 0) for h in heads]
            grown = [_dot((q[h] * q_dec[h]).astype(BF16), d[h], 0, 0) for h in heads]
            for h in heads:
                dk_ref[rows, h * RET_QK:(h + 1) * RET_QK] = _rope_bwd(dk[h], cos, sin).astype(dk_ref.dtype)
                dv_ref[rows, h * RET_V:(h + 1) * RET_V] = dv[h].astype(dv_ref.dtype)
                state[h] = state[h] * _chunk_decay(lg_ref, h) + grown[h]

    return pl.pallas_call(
        body, name="ret_bwd_kv", grid=(nc,),
        in_specs=[q_spec, k_spec, v_spec, v_spec, lg_spec, rope_spec, rope_spec],
        out_specs=[q_spec, v_spec],
        out_shape=[jax.ShapeDtypeStruct((s, RET_QK_WIDTH), BF16), jax.ShapeDtypeStruct((s, RET_V_WIDTH), BF16)],
        scratch_shapes=RET_SCRATCH,
        compiler_params=_cparams(),
    )(rqk, rqk, rv, d_out, _ret_log_gamma(), cos2, sin2)


def _xattn_probs(scores):
    sc = scores - jnp.max(scores, axis=-1, keepdims=True)
    p = jnp.exp(sc)
    return p / jnp.sum(p, axis=-1, keepdims=True)


def _xattn_heads(q_ref, kv_ref):
    sls = [slice(h * MEM_DIM, (h + 1) * MEM_DIM) for h in range(MEM_HEADS)]
    q = [q_ref[:, sl] for sl in sls]
    k = [kv_ref[:, sl] for sl in sls]
    v = [kv_ref[:, D_MODEL + h * MEM_DIM:D_MODEL + (h + 1) * MEM_DIM] for h in range(MEM_HEADS)]
    return sls, q, k, v


def _xattn_fwd(qm, kv, s):
    tq = min(XATTN_ROWS, s)
    heads = range(MEM_HEADS)

    def body(q_ref, kv_ref, o_ref):
        sls, q, k, v = _xattn_heads(q_ref, kv_ref)
        scores = [_dot(q[h], k[h], 1, 1) for h in heads]
        p = [_xattn_probs(scores[h]).astype(BF16) for h in heads]
        out = [_dot(p[h], v[h], 1, 0) for h in heads]
        for h in heads:
            o_ref[:, sls[h]] = out[h].astype(o_ref.dtype)

    return pl.pallas_call(
        body, name="xattn_fwd", grid=(s // tq,),
        in_specs=[pl.BlockSpec((tq, D_MODEL), lambda i: (i, 0)),
                  pl.BlockSpec((MEM_LEN, 2 * D_MODEL), lambda i: (0, 0))],
        out_specs=pl.BlockSpec((tq, D_MODEL), lambda i: (i, 0)),
        out_shape=jax.ShapeDtypeStruct((s, D_MODEL), BF16),
        compiler_params=_cparams(),
    )(qm, kv)


def _xattn_bwd(qm, kv, do, s):
    tq = min(XATTN_ROWS, s)

    def body(q_ref, kv_ref, do_ref, dq_ref, dkv_ref):
        i = pl.program_id(0)

        @pl.when(i == 0)
        def _():
            dkv_ref[...] = jnp.zeros_like(dkv_ref)

        heads = range(MEM_HEADS)
        sls, q, k, v = _xattn_heads(q_ref, kv_ref)
        d = [do_ref[:, sl] for sl in sls]
        scores = [_dot(q[h], k[h], 1, 1) for h in heads]
        dp = [_dot(d[h], v[h], 1, 1) for h in heads]
        p = [_xattn_probs(scores[h]) for h in heads]
        ds = [(p[h] * (dp[h] - jnp.sum(p[h] * dp[h], axis=-1, keepdims=True))).astype(BF16) for h in heads]
        dq = [_dot(ds[h], k[h], 1, 0) for h in heads]
        dk = [_dot(ds[h], q[h], 0, 0) for h in heads]
        dv = [_dot(p[h].astype(BF16), d[h], 0, 0) for h in heads]
        for h in heads:
            dq_ref[:, sls[h]] = (dq[h] * MEM_SCALE).astype(dq_ref.dtype)
            dkv_ref[:, sls[h]] += dk[h]
            dkv_ref[:, D_MODEL + h * MEM_DIM:D_MODEL + (h + 1) * MEM_DIM] += dv[h]

    row_blk = pl.BlockSpec((tq, D_MODEL), lambda i: (i, 0))
    kv_blk = pl.BlockSpec((MEM_LEN, 2 * D_MODEL), lambda i: (0, 0))
    return pl.pallas_call(
        body, name="xattn_bwd", grid=(s // tq,),
        in_specs=[row_blk, kv_blk, row_blk],
        out_specs=[row_blk, kv_blk],
        out_shape=[jax.ShapeDtypeStruct((s, D_MODEL), BF16), jax.ShapeDtypeStruct((MEM_LEN, 2 * D_MODEL), F32)],
        compiler_params=_cparams(),
    )(qm, kv, do)


def _place():
    x, y, c = lax.axis_index("x"), lax.axis_index("y"), lax.axis_index("c")
    others = [(1 - x, y), (x, 1 - y), (1 - x, 1 - y)]
    return x, y, c, others


def _slab(ref, axis, chip, size):
    start = pl.multiple_of(chip * size, LANES if axis == 1 else 16)
    if axis == 0:
        return ref.at[pl.ds(start, size), :]
    return ref.at[:, pl.ds(start, size)]


class _CommPlan:
    def __init__(self, ins, out_shape, scratch, start, finish):
        self.ins, self.out_shape, self.scratch, self.start, self.finish = ins, out_shape, scratch, start, finish

    @property
    def specs(self):
        any_spec = pl.BlockSpec(memory_space=pl.ANY)
        return [any_spec] * len(self.ins), [any_spec] * len(self.out_shape)


def _gather_plan(names, shards):
    spec = {name: (shape, axis) for name, shape, axis in BIG}
    nw = len(names)

    def shard_half(ref, c):
        rows = ref.shape[0] // 2
        return ref.at[pl.ds(pl.multiple_of(c * rows, 16), rows), :]

    def region(ref, w, chip, c):
        shape, axis = spec[names[w]]
        size = shape[axis] // N_CHIPS
        if axis == 0:
            rows = size // 2
            return ref.at[pl.ds(pl.multiple_of(chip * size + c * rows, 16), rows), :]
        rows = shape[0] // 2
        return ref.at[pl.ds(pl.multiple_of(c * rows, 16), rows), pl.ds(pl.multiple_of(chip * size, LANES), size)]

    def ops(shard, full, sems):
        ici_send, ici_recv, d2d_send, d2d_recv, local_sems = sems
        x, y, c, others = _place()
        mine, sibling = 2 * x + y, (x, y, 1 - c)
        local, over_ici, arrived, passed_on, from_sibling = [], [], [], [], []
        for w in range(nw):
            shape, axis = spec[names[w]]
            local.append(pltpu.make_async_copy(shard[w], _slab(full[w], axis, mine, shape[axis] // N_CHIPS),
                                               local_sems.at[w]))
            for t, (qx, qy) in enumerate(others):
                n, theirs = 3 * w + t, 2 * qx + qy
                over_ici.append(pltpu.make_async_remote_copy(
                    src_ref=shard_half(shard[w], c), dst_ref=region(full[w], w, mine, c),
                    send_sem=ici_send.at[n], recv_sem=ici_recv.at[n], device_id=(qx, qy, c), device_id_type=MESH))
                arrived.append(pltpu.make_async_remote_copy(
                    src_ref=shard_half(shard[w], c), dst_ref=region(full[w], w, theirs, c),
                    send_sem=ici_send.at[n], recv_sem=ici_recv.at[n], device_id=(qx, qy, c), device_id_type=MESH))
                passed_on.append(pltpu.make_async_remote_copy(
                    src_ref=region(full[w], w, theirs, c), dst_ref=region(full[w], w, theirs, c),
                    send_sem=d2d_send.at[n], recv_sem=d2d_recv.at[n], device_id=sibling, device_id_type=MESH))
                from_sibling.append(pltpu.make_async_remote_copy(
                    src_ref=region(full[w], w, theirs, c), dst_ref=region(full[w], w, theirs, 1 - c),
                    send_sem=d2d_send.at[n], recv_sem=d2d_recv.at[n], device_id=sibling, device_id_type=MESH))
        return local, over_ici, arrived, passed_on, from_sibling

    def start(shard, full, sems):
        local, over_ici, _, _, _ = ops(shard, full, sems)
        for cp in local + over_ici:
            cp.start()

    def finish(shard, full, sems):
        local, over_ici, arrived, passed_on, from_sibling = ops(shard, full, sems)
        for got, onward in zip(arrived, passed_on, strict=True):
            got.wait_recv()
            onward.start()
        for got in from_sibling:
            got.wait_recv()
        for cp in over_ici + passed_on:
            cp.wait_send()
        for cp in local:
            cp.wait()

    dma = pltpu.SemaphoreType.DMA
    return _CommPlan(
        ins=[shards[name] for name in names],
        out_shape=[jax.ShapeDtypeStruct(spec[name][0], BF16) for name in names],
        scratch=[dma((3 * nw,)), dma((3 * nw,)), dma((3 * nw,)), dma((3 * nw,)), dma((nw,))],
        start=start, finish=finish)


def _shard_shape(shape, axis):
    return tuple(d // N_CHIPS if a == axis else d for a, d in enumerate(shape))


def _exchange_plan(names, grads):
    spec = {name: (shape, axis) for name, shape, axis in BIG}
    nw = len(names)

    def ops(grad, stack, sems):
        send_sems, recv_sems, local_sems = sems
        x, y, c, others = _place()
        mine = 2 * x + y
        me, sibling = (x, y, c), (x, y, 1 - c)

        def dev(px, py, pc):
            return 4 * px + 2 * py + pc

        def copy(w, n, src, slot, to):
            return pltpu.make_async_remote_copy(
                src_ref=src, dst_ref=stack[w].at[slot], send_sem=send_sems.at[7 * w + n],
                recv_sem=recv_sems.at[7 * w + n], device_id=to, device_id_type=MESH)

        local, first, arrived, passed_on, from_sibling = [], [], [], [], []
        for w in range(nw):
            shape, axis = spec[names[w]]
            size = shape[axis] // N_CHIPS
            own = _slab(grad[w], axis, mine, size)
            local.append(pltpu.make_async_copy(own, stack[w].at[dev(*me)], local_sems.at[w]))
            first.append(copy(w, 0, own, dev(*me), sibling))
            from_sibling.append(copy(w, 0, own, dev(*sibling), me))
            for t, (qx, qy) in enumerate(others):
                got = stack[w].at[dev(qx, qy, c)]
                first.append(copy(w, 1 + t, _slab(grad[w], axis, 2 * qx + qy, size), dev(*me), (qx, qy, c)))
                arrived.append(copy(w, 1 + t, got, dev(qx, qy, c), me))
                passed_on.append(copy(w, 4 + t, got, dev(qx, qy, c), sibling))
                from_sibling.append(copy(w, 4 + t, got, dev(qx, qy, 1 - c), me))
        return local, first, arrived, passed_on, from_sibling

    def start(grad, stack, sems):
        local, first, _, _, _ = ops(grad, stack, sems)
        for cp in local + first:
            cp.start()

    def finish(grad, stack, sems):
        local, first, arrived, passed_on, from_sibling = ops(grad, stack, sems)
        for got, onward in zip(arrived, passed_on, strict=True):
            got.wait_recv()
            onward.start()
        for got in from_sibling:
            got.wait_recv()
        for cp in first + passed_on:
            cp.wait_send()
        for cp in local:
            cp.wait()

    dma = pltpu.SemaphoreType.DMA
    return _CommPlan(
        ins=[grads[name] for name in names],
        out_shape=[jax.ShapeDtypeStruct((N_DEV,) + _shard_shape(*spec[name]), BF16) for name in names],
        scratch=[dma((7 * nw,)), dma((7 * nw,)), dma((nw,))],
        start=start, finish=finish)


def _adamw(w, g, m, v):
    m = ADAM_B1 * m + (1.0 - ADAM_B1) * g
    v = ADAM_B2 * v + (1.0 - ADAM_B2) * (g * g)
    m_hat = m / (1.0 - ADAM_B1 ** ADAM_STEP)
    v_hat = v / (1.0 - ADAM_B2 ** ADAM_STEP)
    delta = -ADAM_LR * (m_hat / (jnp.sqrt(v_hat) + ADAM_EPS) + ADAM_WD * w)
    return delta, m, v


def _reduce_adamw(name, stack, w, m, v):
    rows, cols = w.shape
    tr = next(t for t in (256, 128, 64) if rows % t == 0)

    def body(s_ref, w_ref, m_ref, v_ref, g_ref, d_ref, nm_ref, nv_ref):
        g = s_ref[0].astype(F32)
        for d in range(1, N_DEV):
            g = g + s_ref[d].astype(F32)
        g_ref[...] = g
        d_ref[...], nm_ref[...], nv_ref[...] = _adamw(w_ref[...], g, m_ref[...], v_ref[...])

    blk = pl.BlockSpec((tr, cols), lambda i: (i, 0))
    return pl.pallas_call(
        body, name=name, grid=(rows // tr,),
        in_specs=[pl.BlockSpec((N_DEV, tr, cols), lambda i: (0, i, 0)), blk, blk, blk],
        out_specs=[blk] * 4, out_shape=[jax.ShapeDtypeStruct((rows, cols), F32)] * 4,
        compiler_params=_cparams(),
    )(stack, w, m, v)


def _small_step(pack, w, m, v):
    def body(p_ref, w_ref, m_ref, v_ref, g_ref, d_ref, nm_ref, nv_ref, loss_ref, all_ref, send_sems, recv_sems):
        x, y, c, _ = _place()
        me = 4 * x + 2 * y + c
        all_ref[me] = p_ref[...]
        sent = []
        for n in range(1, N_DEV):
            peer = me ^ n
            cp = pltpu.make_async_remote_copy(
                src_ref=p_ref, dst_ref=all_ref.at[me], send_sem=send_sems.at[n - 1], recv_sem=recv_sems.at[n - 1],
                device_id=(peer // 4, (peer // 2) % 2, peer % 2), device_id_type=MESH)
            cp.start()
            sent.append(cp)
        for n in range(1, N_DEV):
            peer = me ^ n
            pltpu.make_async_remote_copy(
                src_ref=p_ref, dst_ref=all_ref.at[peer], send_sem=send_sems.at[n - 1], recv_sem=recv_sems.at[n - 1],
                device_id=(peer // 4, (peer // 2) % 2, peer % 2), device_id_type=MESH).wait_recv()
        for cp in sent:
            cp.wait_send()
        tot = all_ref[0]
        for d in range(1, N_DEV):
            tot = tot + all_ref[d]
        g = tot[:SMALL_ROWS]
        g_ref[...] = g
        d_ref[...], nm_ref[...], nv_ref[...] = _adamw(w_ref[...], g, m_ref[...], v_ref[...])
        loss_ref[...] = jnp.sum(jnp.sum(tot[SMALL_ROWS:], axis=1, keepdims=True), axis=0, keepdims=True)

    vm = pl.BlockSpec(memory_space=pltpu.VMEM)
    small = jax.ShapeDtypeStruct((SMALL_ROWS, LANES), F32)
    return pl.pallas_call(
        body, name="small_step",
        in_specs=[vm] * 4, out_specs=[vm] * 5,
        out_shape=[small] * 4 + [jax.ShapeDtypeStruct((1, 1), F32)],
        scratch_shapes=[pltpu.VMEM((N_DEV, PACK_ROWS, LANES), F32),
                        pltpu.SemaphoreType.DMA((N_DEV - 1,)), pltpu.SemaphoreType.DMA((N_DEV - 1,))],
    )(pack, w, m, v)


LATER_WEIGHTS = tuple(name for name, _, _ in BIG if name != "w_in")


def _layer_step(x, mem, tgt, shards, vec):
    s = x.shape[0]
    d = D_MODEL
    tm = min(ROW_TILE, s)
    tl = min(WIDE_TILE, s)
    xb, cos2, sin2, w_in = _prep(x, _gather_plan(("w_in",), shards))
    bf = lambda w: ((s, w), BF16)
    f32 = lambda w: ((s, w), F32)

    w_sb, w_rqk = w_in[:, :OFF_RET_Q], w_in[:, OFF_RET_Q:OFF_RET_V]
    w_rvg, w_gate = w_in[:, OFF_RET_V:OFF_GATE], w_in[:, OFF_GATE:]
    q_scale = lambda width, q_width, scale: jnp.concatenate(
        [jnp.full((1, q_width), scale, F32), jnp.ones((1, width - q_width), F32)], axis=1)
    n_groups = 3 * SB_WIDTH // LANES

    def sb_epi(acc, t, i, j):
        scaled = acc * t[0]
        return [jnp.stack([scaled[:, g * LANES:(g + 1) * LANES] for g in range(n_groups)])], []

    (sb_qkv,) = _mm(
        "in_sb", xb, w_sb, s, 3 * SB_WIDTH, d, tm=tl, tn=3 * SB_WIDTH, tk=d, epi=sb_epi,
        ins=[(q_scale(3 * SB_WIDTH, SB_WIDTH, SB_SCALE), *_rowvec(3 * SB_WIDTH))],
        outs=[((n_groups, s, LANES), BF16, (n_groups, tl, LANES), lambda i, j: (0, i, 0))])

    def rope_epi(acc, t, i, j):
        cos, sin, scale = t
        parts = []
        for g in range(acc.shape[1] // RET_QK):
            xg = acc[:, g * RET_QK:(g + 1) * RET_QK]
            parts.append(xg * cos + _swap_halves(xg) * sin)
        return [jnp.concatenate(parts, axis=1) * scale], []

    rope_in = ((tl, RET_QK), lambda i, j: (i, 0))
    (rqk,) = _mm("in_rqk", xb, w_rqk, s, 2 * RET_QK_WIDTH, d, tm=tl, tn=2 * RET_QK_WIDTH, tk=d, epi=rope_epi,
                 chunk=MXU_COLS,
                 ins=[(cos2, *rope_in), (sin2, *rope_in),
                      (q_scale(2 * RET_QK_WIDTH, RET_QK_WIDTH, RET_SCALE), *_rowvec(2 * RET_QK_WIDTH))],
                 outs=[(*f32(2 * RET_QK_WIDTH), *_tile(tl, 2 * RET_QK_WIDTH))])
    (rvg,) = _mm("in_rvg", xb, w_rvg, s, 2 * RET_V_WIDTH, d, tm=tl, tn=2 * RET_V_WIDTH, tk=d, chunk=MXU_COLS,
                 epi=_plain, outs=[(*bf(2 * RET_V_WIDTH), *_tile(tl, 2 * RET_V_WIDTH))])
    (gates,) = _mm("in_gate", xb, w_gate, s, 2 * d, d, tm=tl, tn=2 * d, tk=d, chunk=MXU_COLS,
                   epi=lambda acc, t, i, j: ([_sigmoid(acc + t[0])], []),
                   ins=[(vec["b_gate"], *_rowvec(2 * d))], outs=[(*bf(2 * d), *_tile(tl, 2 * d))])

    sb_out, sb_out_f32, *gathered = _sb_fwd(sb_qkv, s, comm=_gather_plan(LATER_WEIGHTS, shards))
    wt = dict(zip(LATER_WEIGHTS, gathered, strict=True))
    ret, gated = _ret_fwd(rqk, rvg, s)
    (y_sb,) = _mm("sb_o", sb_out, wt["w_sb_o"], s, d, SB_WIDTH, tm=tl, tn=d, tk=SB_WIDTH, epi=_plain,
                  outs=[(*bf(d), *_tile(tl, d))])
    y_ret, mixin = _mm(
        "ret_o", gated, wt["w_ret_o"], s, d, RET_V_WIDTH, tm=tl, tn=d, tk=RET_V_WIDTH, chunk=MXU_COLS,
        epi=lambda acc, t, i, j: ([acc, t[0].astype(F32) * t[2].astype(F32) + t[1].astype(F32) * acc], []),
        ins=[(gates, *_tile(tl, d)), (gates, *_tile(tl, d, 1)), (y_sb, *_tile(tl, d))],
        outs=[(*bf(d), *_tile(tl, d)), (*bf(d), *_tile(tl, d))])

    def ln_epi(acc, t, i, j):
        *res, g, b = t
        prev = res[0] if len(res) == 1 else res[0] * res[1] + res[2]
        xhat, rstd = _norm(DN_ALPHA * prev + acc)
        return [xhat * g + b, xhat, rstd], []

    full = _tile(tm, d)
    col1 = ((tm, 1), lambda i, j: (i, 0))
    vec_in = lambda name: (vec[name], *_rowvec(d))
    ln_outs = [(*bf(d), *full), (*f32(d), *full), ((s, 1), F32, *col1)]
    x1b, xhat1, rstd1 = _mm(
        "mix_o", mixin, wt["w_mix_o"], s, d, d, tm=tm, tn=d, tk=d, epi=ln_epi,
        ins=[(x, *full), vec_in("ln1_g"), vec_in("ln1_b")], outs=ln_outs)

    (qm,) = _mm("mem_q", x1b, wt["w_mem_q"], s, d, d, tm=tl, tn=d, tk=d,
                epi=lambda acc, t, i, j: ([acc * MEM_SCALE], []), outs=[(*bf(d), *_tile(tl, d))])
    (kv,) = _mm("mem_kv", mem, wt["w_mem_kv"], MEM_LEN, 2 * d, d, tm=MEM_LEN, tn=d, tk=d, epi=_plain,
                outs=[((MEM_LEN, 2 * d), BF16, *_tile(MEM_LEN, d))])
    att = _xattn_fwd(qm, kv, s)
    x2b, xhat2, rstd2 = _mm(
        "mem_o", att, wt["w_mem_o"], s, d, d, tm=tm, tn=d, tk=d, epi=ln_epi,
        ins=[(xhat1, *full), vec_in("ln1_g"), vec_in("ln1_b"), vec_in("ln2_g"), vec_in("ln2_b")], outs=ln_outs)

    fh = FFN_HIDDEN
    tf = fh // 2
    def swiglu_epi(acc, t, i, j):
        a = acc[0].astype(BF16).astype(F32)
        return [a, acc[1], a * _sigmoid(a) * acc[1]], []

    f1, f2, act = _mm(
        "ffn_in", x2b, wt["w_ffn_in"], s, 2 * fh, d, tm=tm, tn=2 * fh, tk=d, epi=swiglu_epi, chunk=MXU_COLS,
        halves=True, outs=[(*bf(fh), *_tile(tm, fh))] * 3)

    def head_epi(acc, t, i, j):
        prev_hat, prev_g, prev_b, g, b, target = t
        xhat, rstd = _norm(DN_ALPHA * (prev_hat * prev_g + prev_b) + acc)
        err = xhat * g + b - target
        dy = err * (1.0 / d)
        du = _norm_bwd(dy * g, xhat, rstd)
        return [du], [_colsum(dy * xhat), _colsum(dy), _colsum(err * err) * (0.5 / d)]

    vec_acc = ((1, d), F32)
    du3b, dg3, db3, loss_cols = _mm(
        "ffn_out", act, wt["w_ffn_out"], s, d, fh, tm=tm, tn=d, tk=fh, epi=head_epi,
        ins=[(xhat2, *full), vec_in("ln2_g"), vec_in("ln2_b"), vec_in("ln3_g"), vec_in("ln3_b"), (tgt, *full)],
        outs=[(*bf(d), *full)], accs=[vec_acc] * 3)

    grads = {}
    ts = min(SEQ_TILE, s)

    def wgrad(name, a, b, m, n, tm_, tn_, tk_=None):
        (g,) = _mm(name, a, b, m, n, a.shape[0], tm=tm_, tn=tn_, tk=tk_ or ts, ta=True, epi=_plain,
                   outs=[((m, n), BF16, *_tile(tm_, tn_))])
        return g

    def ffn_bwd_epi(acc, t, i, j):
        a, b = t[0].astype(F32), t[1].astype(F32)
        sg = _sigmoid(a)
        return [acc * b * (sg * (1.0 + a * (1.0 - sg))), acc * (a * sg)], []

    df1, df2 = _mm(
        "ffn_out_t", du3b, wt["w_ffn_out"], s, fh, d, tm=tm, tn=fh, tk=d, tb=True, epi=ffn_bwd_epi, chunk=MXU_COLS,
        ins=[(f1, *_tile(tm, fh)), (f2, *_tile(tm, fh))],
        outs=[(*bf(fh), *_tile(tm, fh)), (*bf(fh), *_tile(tm, fh))])
    grads["w_ffn_out"] = wgrad("g_ffn_out", act, du3b, fh, d, tf, d)
    grads["w_ffn_in"] = jnp.concatenate(
        [wgrad("g_ffn_in1", x2b, df1, d, fh, d, tf), wgrad("g_ffn_in2", x2b, df2, d, fh, d, tf)], axis=1)

    def ln_bwd(name, a, b, k, tk, b_off, more, scales, xhat, rstd, g):
        def epi(acc, t, i, j):
            *extra, xh, rs, gg = t
            dy = acc
            for e, sc in zip(extra, scales, strict=True):
                dy = dy + e.astype(F32) * sc
            return [_norm_bwd(dy * gg, xh, rs)], [_colsum(dy * xh), _colsum(dy)]

        return _mm(name, a, b, s, d, k, tm=tm, tn=d, tk=tk, tb=True, b_off=b_off, epi=epi,
                   ins=[(e, *full) for e in more] + [(xhat, *full), (rstd, *col1), (g, *_rowvec(d))],
                   outs=[(*bf(d), *full)], accs=[vec_acc] * 2)

    du2b, dg2, db2 = ln_bwd("ffn_in_t", [df1, df2], wt["w_ffn_in"], 2 * fh, 2 * fh, (0, 0), [du3b], [DN_ALPHA],
                            xhat2, rstd2, vec["ln2_g"])

    (datt,) = _mm("mem_o_t", du2b, wt["w_mem_o"], s, d, d, tm=tl, tn=d, tk=d, tb=True, epi=_plain,
                  outs=[(*bf(d), *_tile(tl, d))])
    grads["w_mem_o"] = wgrad("g_mem_o", att, du2b, d, d, d, d)
    dqm, dkv = _xattn_bwd(qm, kv, datt, s)
    grads["w_mem_q"] = wgrad("g_mem_q", x1b, dqm, d, d, d, d)
    grads["w_mem_kv"] = wgrad("g_mem_kv", mem, dkv, d, 2 * d, d, d, MEM_LEN)
    du1b, dg1, db1 = ln_bwd("mem_q_t", dqm, wt["w_mem_q"], d, d, (0, 0), [du2b], [DN_ALPHA],
                            xhat1, rstd1, vec["ln1_g"])

    def merge_bwd_epi(acc, t, i, j):
        g0, g1, ysb, yret = (v.astype(F32) for v in t)
        dgate0 = acc * ysb * (g0 * (1.0 - g0))
        dgate1 = acc * yret * (g1 * (1.0 - g1))
        return [dgate0, dgate1, acc * g0, acc * g1], [_colsum(dgate0), _colsum(dgate1)]

    dgate0, dgate1, dy_sb, dy_ret, dbg0, dbg1 = _mm(
        "mix_o_t", du1b, wt["w_mix_o"], s, d, d, tm=tm, tn=d, tk=d, tb=True, epi=merge_bwd_epi,
        ins=[(gates, *full), (gates, *_tile(tm, d, 1)), (y_sb, *full), (y_ret, *full)],
        outs=[(*bf(d), *full)] * 4, accs=[vec_acc] * 2)
    grads["w_mix_o"] = wgrad("g_mix_o", mixin, du1b, d, d, d, d)
    grads["w_sb_o"] = wgrad("g_sb_o", sb_out, dy_sb, SB_WIDTH, d, SB_WIDTH, d)
    grads["w_ret_o"] = wgrad("g_ret_o", gated, dy_ret, RET_V_WIDTH, d, RET_V_WIDTH, d)
    (dsb_out,) = _mm("sb_o_t", dy_sb, wt["w_sb_o"], s, SB_WIDTH, d, tm=tl, tn=SB_WIDTH, tk=d, tb=True, epi=_plain,
                     outs=[(*bf(SB_WIDTH), *_tile(tl, SB_WIDTH))])

    def gate_norm_bwd_epi(acc, t, i, j):
        r, g = t[0], t[1].astype(F32)
        drg, dret = [], []
        for h in range(acc.shape[1] // RET_V):
            sl = slice(h * RET_V, (h + 1) * RET_V)
            xhat, rstd = _norm(r[:, sl])
            gg, dd = g[:, sl], acc[:, sl]
            sg = _sigmoid(gg)
            drg.append(dd * xhat * (sg * (1.0 + gg * (1.0 - sg))))
            dret.append(_norm_bwd(dd * (gg * sg), xhat, rstd))
        return [jnp.concatenate(drg, axis=1), jnp.concatenate(dret, axis=1)], []

    drg, dret = _mm(
        "ret_o_t", dy_ret, wt["w_ret_o"], s, RET_V_WIDTH, d, tm=tm, tn=d, tk=d, tb=True, epi=gate_norm_bwd_epi,
        chunk=MXU_COLS,
        ins=[(ret, *full), (rvg, *_tile(tm, d, 1))],
        outs=[(*bf(RET_V_WIDTH), *full)] * 2)

    drq = _ret_bwd_q(rqk, rvg, dret, cos2, sin2, s)
    drk, drv = _ret_bwd_kv(rqk, rvg, dret, cos2, sin2, s)
    dsq, dsk, dsv, *stacked = _sb_bwd(sb_qkv, sb_out_f32, dsb_out, s, comm=_exchange_plan(LATER_WEIGHTS, grads))
    stacks = dict(zip(LATER_WEIGHTS, stacked, strict=True))

    dh = {"sq": dsq, "sk": dsk, "sv": dsv, "rq": drq, "rk": drk, "rv": drv, "rg": drg, "gate0": dgate0,
          "gate1": dgate1}
    def wgrad_in(name, pieces):
        width = sum(p.shape[1] for p in pieces)
        (g,) = _mm(name, xb, pieces, d, width, s, tm=d, tn=width, tk=min(ROW_TILE, s // 2), ta=True, epi=_plain,
                   outs=[((d, width), BF16, *_tile(d, width))])
        return g

    grads["w_in"] = jnp.concatenate(
        [wgrad_in("g_in_mixers", [dsq, dsk, dsv, drq, drk, drv]), wgrad_in("g_in_gates", [drg, dgate0, dgate1])],
        axis=1)
    grad_x, stacks["w_in"] = _mm(
        "in_t", list(dh.values()), w_in, s, d, IN_WIDTH, tm=tm, tn=d, tk=IN_WIDTH, tb=True,
        epi=lambda acc, t, i, j: ([acc + DN_ALPHA * t[0].astype(F32)], []),
        ins=[(du1b, *full)], outs=[(*f32(d), *full)], comm=_exchange_plan(("w_in",), grads))

    small = {"b_gate": jnp.concatenate([dbg0, dbg1], axis=1), "ln1_g": dg1, "ln1_b": db1, "ln2_g": dg2,
             "ln2_b": db2, "ln3_g": dg3, "ln3_b": db3}
    return grad_x, stacks, small, loss_cols


def kernel(x, mem, w_in, b_gate, w_sb_o, w_ret_o, w_mix_o, ln1_g, ln1_b, w_mem_q, w_mem_kv, w_mem_o, ln2_g, ln2_b, w_ffn_in, w_ffn_out, ln3_g, ln3_b, loss_target, m_w_in, m_b_gate, m_w_sb_o, m_w_ret_o, m_w_mix_o, m_ln1_g, m_ln1_b, m_w_mem_q, m_w_mem_kv, m_w_mem_o, m_ln2_g, m_ln2_b, m_w_ffn_in, m_w_ffn_out, m_ln3_g, m_ln3_b, v_w_in, v_b_gate, v_w_sb_o, v_w_ret_o, v_w_mix_o, v_ln1_g, v_ln1_b, v_w_mem_q, v_w_mem_kv, v_w_mem_o, v_ln2_g, v_ln2_b, v_w_ffn_in, v_w_ffn_out, v_ln3_g, v_ln3_b):
    given = dict(locals())
    s = x.shape[1]
    x2d = x.reshape(s, D_MODEL)
    tgt = loss_target.reshape(s, D_MODEL)
    mem2d = mem.reshape(MEM_LEN, D_MODEL)
    shard = {name: given[name].reshape(_shard_shape(shape, axis)) for name, shape, axis in BIG}
    vec = {name: given[name] for name in SMALL}

    shards_bf = {name: _cast_bf16("cast_" + name, shard[name]) for name, _, _ in BIG}

    grad_x, stacks, small, loss_cols = _layer_step(x2d, mem2d, tgt, shards_bf, vec)

    out = {}
    for name, shape, axis in BIG:
        stack = stacks[name]
        shp = given[name].shape
        res = _reduce_adamw("adamw_" + name, stack, shard[name], given["m_" + name].reshape(stack.shape[1:]),
                            given["v_" + name].reshape(stack.shape[1:]))
        out[name] = [r.reshape(shp) for r in res]

    pack = jnp.concatenate([small[name] for name in SMALL] + [loss_cols], axis=1).reshape(PACK_ROWS, LANES)
    cat = lambda pre: jnp.concatenate([given[pre + name] for name in SMALL], axis=1).reshape(SMALL_ROWS, LANES)
    *res, loss = _small_step(pack, cat(""), cat("m_"), cat("v_"))
    flat = [r.reshape(1, SMALL_LEN) for r in res]
    off = 0
    for name in SMALL:
        n = given[name].shape[1]
        out[name] = [r[:, off:off + n] for r in flat]
        off += n

    return (loss.reshape(()), grad_x.reshape(x.shape),
            *[out[name][0] for name in WEIGHT_ORDER], *[out[name][1] for name in WEIGHT_ORDER],
            *[out[name][2] for name in WEIGHT_ORDER], *[out[name][3] for name in WEIGHT_ORDER])
```

```python
import functools

import jax
import jax.numpy as jnp
import numpy as np
from jax import lax
from jax.experimental import pallas as pl
from jax.experimental.pallas import tpu as pltpu

F32, BF16 = jnp.float32, jnp.bfloat16
MESH = pl.DeviceIdType.MESH

D_MODEL = 1024
MEM_LEN = 256
SB_HEADS, SB_DIM, SB_WIDTH = 8, 64, 512
RET_HEADS, RET_QK, RET_V = 4, 128, 256
RET_QK_WIDTH, RET_V_WIDTH = 512, 1024
ROPE_BASE = 10000.0
MEM_HEADS, MEM_DIM = 4, 256
FFN_HIDDEN = 2816
IN_WIDTH = 6656
OFF_RET_Q, OFF_RET_V, OFF_RET_G, OFF_GATE = 1536, 2560, 3584, 4608
DN_ALPHA = 2.0 ** 0.25
LN_EPS = 1e-5
SB_SCALE = SB_DIM ** -0.5
SB_DEAD = -110.0
RET_SCALE = RET_QK ** -0.5
MEM_SCALE = MEM_DIM ** -0.5
ADAM_LR, ADAM_B1, ADAM_B2, ADAM_EPS, ADAM_WD, ADAM_STEP = 0.001, 0.9, 0.999, 1e-08, 0.01, 10

N_DEV, N_CHIPS = 8, 4

LANES = 128
MXU_COLS = 256
VMEM_LIMIT_BYTES = 52 * 2 ** 20
ROW_TILE = 512
WIDE_TILE = 1024
SEQ_TILE = 2048
SB_BLOCK = 256
RET_BLOCK = 256
RET_CHUNKS_PER_STEP = 4
XATTN_ROWS = 1024

BIG = (
    ("w_in", (D_MODEL, IN_WIDTH), 1),
    ("w_sb_o", (SB_WIDTH, D_MODEL), 1),
    ("w_ret_o", (RET_V_WIDTH, D_MODEL), 0),
    ("w_mix_o", (D_MODEL, D_MODEL), 0),
    ("w_mem_q", (D_MODEL, D_MODEL), 0),
    ("w_mem_kv", (D_MODEL, 2 * D_MODEL), 1),
    ("w_mem_o", (D_MODEL, D_MODEL), 0),
    ("w_ffn_in", (D_MODEL, 2 * FFN_HIDDEN), 1),
    ("w_ffn_out", (FFN_HIDDEN, D_MODEL), 0),
)
SMALL = ("b_gate", "ln1_g", "ln1_b", "ln2_g", "ln2_b", "ln3_g", "ln3_b")
SMALL_LEN = 2 * D_MODEL + 6 * D_MODEL
SMALL_ROWS = SMALL_LEN // LANES
PACK_ROWS = SMALL_ROWS + D_MODEL // LANES
WEIGHT_ORDER = ("w_in", "b_gate", "w_sb_o", "w_ret_o", "w_mix_o", "ln1_g", "ln1_b", "w_mem_q", "w_mem_kv",
                "w_mem_o", "ln2_g", "ln2_b", "w_ffn_in", "w_ffn_out", "ln3_g", "ln3_b")


def _cparams():
    return pltpu.CompilerParams(vmem_limit_bytes=VMEM_LIMIT_BYTES)


def _dot(a, b, ca, cb):
    return lax.dot_general(a, b, (((ca,), (cb,)), ((), ())), preferred_element_type=F32)


def _sigmoid(x):
    return 1.0 / (1.0 + jnp.exp(-x))


def _mm(name, a, b, m, n, k, *, tm, tn, tk, epi, outs, ins=(), accs=(), ta=False, tb=False,
        a_off=(0, 0), b_off=(0, 0), j_outer=False, comm=None, chunk=None, halves=False):
    assert not halves or (chunk is not None and (tn // 2) % chunk == 0 and not ins), name
    assert m % tm == 0 and n % tn == 0 and k % tk == 0, (name, m, n, k, tm, tn, tk)
    assert chunk is None or (k == tk and tn % chunk == 0), name
    ni, nj, nk = m // tm, n // tn, k // tk
    assert not accs or nj == 1, name
    ij = (lambda g0, g1: (g1, g0)) if j_outer else (lambda g0, g1: (g0, g1))

    def spec(block, index):
        return pl.BlockSpec(block, lambda g0, g1, kk: index(*ij(g0, g1), kk))

    a_list = list(a) if isinstance(a, (list, tuple)) else [a]
    n_a = len(a_list)
    if n_a > 1:
        assert not ta and nk == 1 and chunk is None and not any(a_off), name
        assert sum(p.shape[1] for p in a_list) == k, name
        a_specs = [spec((tm, p.shape[1]), lambda i, j, kk: (i, 0)) for p in a_list]
    elif ta:
        a_specs = [spec((tk, tm), lambda i, j, kk: (kk + a_off[0], i + a_off[1]))]
    else:
        a_specs = [spec((tm, tk), lambda i, j, kk: (i + a_off[0], kk + a_off[1]))]
    b_list = list(b) if isinstance(b, (list, tuple)) else [b]
    n_b = len(b_list)
    if n_b > 1:
        assert not tb and nj == 1 and nk > 1 and n_a == 1 and chunk is None and not any(b_off), name
        assert sum(p.shape[1] for p in b_list) == n, name
        b_specs = [spec((tk, p.shape[1]), lambda i, j, kk: (kk, 0)) for p in b_list]
    elif tb:
        b_specs = [spec((tn, tk), lambda i, j, kk: (j + b_off[0], kk + b_off[1]))]
    else:
        b_specs = [spec((tk, tn), lambda i, j, kk: (kk + b_off[0], j + b_off[1]))]
    if n_a > 1 and nj == 1:
        b_specs = [pl.BlockSpec(b_specs[0].block_shape, b_specs[0].index_map, pipeline_mode=pl.Buffered(1))]
    in_specs = [*a_specs, *b_specs]
    for _, bs, im in ins:
        in_specs.append(spec(bs, lambda i, j, kk, im=im: im(i, j)))
    out_specs, out_shape = [], []
    for shape, dtype, bs, im in outs:
        out_specs.append(spec(bs, lambda i, j, kk, im=im: im(i, j)))
        out_shape.append(jax.ShapeDtypeStruct(shape, dtype))
    for shape, dtype in accs:
        out_specs.append(spec(shape, lambda i, j, kk, nd=len(shape): (0,) * nd))
        out_shape.append(jax.ShapeDtypeStruct(shape, dtype))
    n_in, n_out, n_acc = len(ins), len(outs), len(accs)
    ca, cb = (0 if ta else 1), (1 if tb else 0)
    grid = (*ij(ni, nj), nk)
    comm_ins, comm_outs, comm_scratch = [], [], []
    if comm is not None:
        comm_in_specs, comm_out_specs = comm.specs
        comm_ins, comm_outs, comm_scratch = list(comm.ins), list(comm.out_shape), list(comm.scratch)
        in_specs += comm_in_specs
        out_specs += comm_out_specs
        out_shape += comm_outs
    n_ci, n_co = len(comm_ins), len(comm_outs)

    def body(*refs):
        a_refs, b_refs, refs = refs[:n_a], refs[n_a:n_a + n_b], refs[n_a + n_b:]
        a_ref, b_ref = a_refs[0], b_refs[0]
        in_refs = refs[:n_in]
        ci_refs = refs[n_in:n_in + n_ci]
        rest = refs[n_in + n_ci:]
        out_refs, acc_refs = rest[:n_out], rest[n_out:n_out + n_acc]
        co_refs = rest[n_out + n_acc:n_out + n_acc + n_co]
        scratch = rest[n_out + n_acc + n_co:]
        sem_refs, scratch = scratch[:len(comm_scratch)], scratch[len(comm_scratch):]
        (i, j), kk = ij(pl.program_id(0), pl.program_id(1)), pl.program_id(2)
        if comm is not None:
            first_step, last_step = _grid_ends(grid)
            pl.when(first_step)(lambda: comm.start(ci_refs, co_refs, sem_refs))
        def finish(acc, cols=slice(None)):
            def of(r):
                return r[..., cols] if r.shape[-1] == tn else r[...]

            o_tiles, a_tiles = epi(acc, [of(r) for r in in_refs], i, j)
            for r, t in zip(out_refs, o_tiles, strict=True):
                r[..., cols] = t.astype(r.dtype)
            if n_acc:
                @pl.when(i == 0)
                def _():
                    for r, t in zip(acc_refs, a_tiles, strict=True):
                        r[..., cols] = t

                @pl.when(i > 0)
                def _():
                    for r, t in zip(acc_refs, a_tiles, strict=True):
                        r[..., cols] += t

        if chunk is not None:
            a_tile = a_ref[...].astype(BF16)

            def product(c0):
                b_part = b_ref[c0:c0 + chunk, :] if tb else b_ref[:, c0:c0 + chunk]
                return _dot(a_tile, b_part.astype(BF16), ca, cb)

            for c0 in range(0, tn // 2 if halves else tn, chunk):
                acc = (product(c0), product(tn // 2 + c0)) if halves else product(c0)
                finish(acc, slice(c0, c0 + chunk))
            if comm is not None:
                pl.when(last_step)(lambda: comm.finish(ci_refs, co_refs, sem_refs))
            return

        if n_b > 1:
            acc_ref = scratch[0]

            def accumulate(first):
                a_tile, c0 = a_ref[...].astype(BF16), 0
                for r in b_refs:
                    c1 = c0 + r.shape[1]
                    term = _dot(a_tile, r[...].astype(BF16), ca, cb)
                    acc_ref[:, c0:c1] = term if first else acc_ref[:, c0:c1] + term
                    c0 = c1

            pl.when(kk == 0)(lambda: accumulate(True))
            pl.when(kk > 0)(lambda: accumulate(False))
            pl.when(kk == nk - 1)(lambda: finish(acc_ref[...]))
            if comm is not None:
                pl.when(last_step)(lambda: comm.finish(ci_refs, co_refs, sem_refs))
            return

        if n_a > 1:
            part, c0 = None, 0
            for r in a_refs:
                c1 = c0 + r.shape[1]
                b_part = b_ref[:, c0:c1] if tb else b_ref[c0:c1, :]
                term = _dot(r[...].astype(BF16), b_part.astype(BF16), ca, cb)
                part, c0 = (term if part is None else part + term), c1
        else:
            part = _dot(a_ref[...].astype(BF16), b_ref[...].astype(BF16), ca, cb)
        if nk == 1:
            finish(part)
        else:
            acc_ref = scratch[0]

            @pl.when(kk == 0)
            def _():
                acc_ref[...] = part

            @pl.when(kk > 0)
            def _():
                acc_ref[...] += part

            @pl.when(kk == nk - 1)
            def _():
                finish(acc_ref[...])

        if comm is not None:
            pl.when(last_step)(lambda: comm.finish(ci_refs, co_refs, sem_refs))

    res = pl.pallas_call(
        body, name=name, grid=grid, in_specs=in_specs, out_specs=out_specs, out_shape=out_shape,
        scratch_shapes=comm_scratch + ([pltpu.VMEM((tm, tn), F32)] if nk > 1 else []),
        compiler_params=_cparams(),
    )(*a_list, *b_list, *[x for x, _, _ in ins], *comm_ins)
    return res


def _grid_ends(grid):
    ids = [pl.program_id(ax) for ax in range(len(grid))]
    first = functools.reduce(jnp.logical_and, [p == 0 for p in ids])
    last = functools.reduce(jnp.logical_and, [p == n - 1 for p, n in zip(ids, grid, strict=True)])
    return first, last


def _tile(tm, tn, dj=0):
    return (tm, tn), (lambda i, j: (i, j + dj))


def _rowvec(tn, dj=0):
    return (1, tn), (lambda i, j: (0, j + dj))


def _plain(acc, tiles, i, j):
    return [acc], []


def _ew(name, fn, ins, outs, rows, tr):
    assert rows % tr == 0, (name, rows, tr)
    in_specs = []
    for x in ins:
        if x.shape[0] == rows:
            in_specs.append(pl.BlockSpec((tr, x.shape[1]), lambda i: (i, 0)))
        else:
            in_specs.append(pl.BlockSpec(x.shape, lambda i: (0, 0)))
    n_in = len(ins)

    def body(*refs):
        res = fn(*[r[...] for r in refs[:n_in]])
        for r, t in zip(refs[n_in:], res, strict=True):
            r[...] = t.astype(r.dtype)

    return pl.pallas_call(
        body, name=name, grid=(rows // tr,), in_specs=in_specs,
        out_specs=[pl.BlockSpec((tr, w), lambda i: (i, 0)) for w, _ in outs],
        out_shape=[jax.ShapeDtypeStruct((rows, w), dt) for w, dt in outs],
        compiler_params=_cparams(),
    )(*ins)


def _cast_bf16(name, x):
    rows = x.shape[0]
    tr = next(t for t in (512, 256, 64) if rows % t == 0)
    return _ew(name, lambda v: (v,), [x], [(x.shape[1], BF16)], rows, tr)[0]


def _prep(x, comm):
    s = x.shape[0]
    half = RET_QK // 2
    inv = 1.0 / (ROPE_BASE ** (jnp.arange(half, dtype=F32) / half))
    inv2 = jnp.concatenate([inv, inv]).reshape(1, RET_QK)
    sign = jnp.concatenate([-jnp.ones((half,), F32), jnp.ones((half,), F32)]).reshape(1, RET_QK)
    tr = min(ROW_TILE, s)
    grid = (s // tr,)
    c_in_specs, c_out_specs, c_out_shape, c_scratch, c_ins, split = _host(comm, 3, 3)

    def body(*refs):
        (x_ref, inv_ref, sign_ref), (xb_ref, cos_ref, sin_ref), _, riding = split(refs)
        i = pl.program_id(0)
        first_step, last_step = _grid_ends(grid)
        pl.when(first_step)(lambda: comm.start(*riding))
        xb_ref[...] = x_ref[...].astype(BF16)
        pos = (lax.broadcasted_iota(jnp.int32, (tr, RET_QK), 0) + i * tr).astype(F32)
        ang = pos * inv_ref[...]
        cos_ref[...] = jnp.cos(ang)
        sin_ref[...] = jnp.sin(ang) * sign_ref[...]
        pl.when(last_step)(lambda: comm.finish(*riding))

    vec = pl.BlockSpec((1, RET_QK), lambda i: (0, 0))
    row = lambda w: pl.BlockSpec((tr, w), lambda i: (i, 0))
    return pl.pallas_call(
        body, name="prep", grid=grid,
        in_specs=[row(D_MODEL), vec, vec] + c_in_specs,
        out_specs=[row(D_MODEL), row(RET_QK), row(RET_QK)] + c_out_specs,
        out_shape=[jax.ShapeDtypeStruct((s, D_MODEL), BF16), jax.ShapeDtypeStruct((s, RET_QK), F32),
                   jax.ShapeDtypeStruct((s, RET_QK), F32)] + c_out_shape,
        scratch_shapes=c_scratch, compiler_params=_cparams(),
    )(x, inv2, sign, *c_ins)


def _swap_halves(x):
    return pltpu.roll(x, RET_QK // 2, 1)


def _norm(u):
    mu = jnp.mean(u, axis=-1, keepdims=True)
    d = u - mu
    var = jnp.mean(d * d, axis=-1, keepdims=True)
    rstd = lax.rsqrt(var + LN_EPS)
    return d * rstd, rstd


def _norm_bwd(dxh, xhat, rstd):
    m1 = jnp.mean(dxh, axis=-1, keepdims=True)
    m2 = jnp.mean(dxh * xhat, axis=-1, keepdims=True)
    return rstd * (dxh - m1 - xhat * m2)


def _colsum(t):
    return jnp.sum(t, axis=0, keepdims=True)


def _split_mm(t, tri):
    hi = t.astype(BF16)
    lo = (t - hi.astype(F32)).astype(BF16)
    return _dot(hi, tri, 1, 0) + _dot(lo, tri, 1, 0)


def _sb_masks():
    t = SB_BLOCK
    lane = lax.broadcasted_iota(jnp.int32, (1, LANES), 1)
    first = lane < SB_DIM
    m0 = jnp.where(first, 1.0, 0.0).astype(BF16)
    m1 = jnp.where(first, 0.0, 1.0).astype(BF16)
    row = lax.broadcasted_iota(jnp.int32, (t, t), 0)
    col = lax.broadcasted_iota(jnp.int32, (t, t), 1)
    return first, (m0, m1), row, col


def _sb_logits(qh, k, causal):
    z = _dot(qh, k, 1, 1)
    lp = jnp.log(1.0 + jnp.exp(-jnp.abs(z)))
    a = jnp.minimum(z, 0.0) - lp
    r = jnp.minimum(-z, 0.0) - lp
    if causal is not None:
        r = jnp.where(causal, r, 0.0)
    return a, r


def _sb_walk(i, blocks, l_ref, causal):
    pl.when(i == 0)(lambda: blocks([(i, causal)]))
    pl.when(i > 0)(lambda: blocks([(i, causal), (i - 1, None)]))

    def alive():
        top = jnp.max(functools.reduce(jnp.maximum, [l_ref[c] for c in range(l_ref.shape[0])]))
        return jnp.where(top > SB_DEAD, 1, 0)

    def cond(c):
        return jnp.logical_and(c[0] < i, c[1] > 0)

    def step(c):
        blocks([(i - 1 - c[0], None)])
        return c[0] + 1, alive()

    lax.while_loop(cond, step, (jnp.int32(1), alive()))


def _host(comm, n_in, n_out):
    if comm is None:
        return [], [], [], [], [], lambda refs: (refs[:n_in], refs[n_in:n_in + n_out], refs[n_in + n_out:], None)
    in_specs, out_specs = comm.specs
    n_ci, n_co, n_sem = len(comm.ins), len(comm.out_shape), len(comm.scratch)

    def split(refs):
        ins, ci = refs[:n_in], refs[n_in:n_in + n_ci]
        rest = refs[n_in + n_ci:]
        outs, co = rest[:n_out], rest[n_out:n_out + n_co]
        sems, scratch = rest[n_out + n_co:n_out + n_co + n_sem], rest[n_out + n_co + n_sem:]
        return ins, outs, scratch, (ci, co, sems)

    return in_specs, out_specs, list(comm.out_shape), list(comm.scratch), list(comm.ins), split


def _sb_qkv_specs(s, g):
    groups = SB_HEADS // 2 // g
    return [pl.BlockSpec((g, SB_BLOCK, LANES), lambda p, i: (p, i, 0)),
            pl.BlockSpec((g, s, LANES), lambda p, i: (groups + p, 0, 0)),
            pl.BlockSpec((g, s, LANES), lambda p, i: (2 * groups + p, 0, 0))]


def _sb_fwd(qkv, s, comm=None):
    t = SB_BLOCK
    g = 2
    nq = s // t
    grid = (SB_HEADS // 2 // g, nq)
    c_in_specs, c_out_specs, c_out_shape, c_scratch, c_ins, split = _host(comm, 3, 2)

    def body(*refs):
        (q_ref, k_ref, v_ref), (o_ref, of_ref), (l_ref, acc_ref), riding = split(refs)
        i = pl.program_id(1)
        if comm is not None:
            first_step, last_step = _grid_ends(grid)
            pl.when(first_step)(lambda: comm.start(*riding))
        first, hmask, row, col = _sb_masks()
        after = jnp.where(row > col, 1.0, 0.0).astype(BF16)
        causal = col < row
        heads = [(p, h) for p in range(g) for h in range(2)]
        qh = {(p, h): q_ref[p] * hmask[h] for p, h in heads}
        l_ref[...] = jnp.zeros_like(l_ref)
        acc_ref[...] = jnp.zeros_like(acc_ref)

        def blocks(todo):
            chains = [(b, p, h) for b in range(len(todo)) for p, h in heads]
            starts = [pl.multiple_of(kb * t, t) for kb, _ in todo]
            ks = {(b, p): k_ref[p, pl.ds(st, t), :] for b, st in enumerate(starts) for p in range(g)}
            vs = {(b, p): v_ref[p, pl.ds(st, t), :] for b, st in enumerate(starts) for p in range(g)}
            ar = {(b, p, h): _sb_logits(qh[p, h], ks[b, p], todo[b][1]) for b, p, h in chains}
            later = {c: _split_mm(ar[c][1], after) for c in chains}
            carry = {(p, h): l_ref[2 * p + h] for p, h in heads}
            w = {}
            for b, (_, mask) in enumerate(todo):
                for p, h in heads:
                    wc = jnp.exp(ar[b, p, h][0] + later[b, p, h] + carry[p, h])
                    w[b, p, h] = wc if mask is None else jnp.where(mask, wc, 0.0)
                carry = {(p, h): carry[p, h] + jnp.sum(ar[b, p, h][1], axis=1, keepdims=True) for p, h in heads}
            pv = {(b, p, h): _dot(w[b, p, h].astype(BF16), vs[b, p], 1, 0) for b, p, h in chains}
            for p in range(g):
                lanes = slice(p * LANES, (p + 1) * LANES)
                acc = acc_ref[:, lanes]
                for b in range(len(todo)):
                    acc = acc + jnp.where(first, pv[b, p, 0], pv[b, p, 1])
                acc_ref[:, lanes] = acc
            for p, h in heads:
                l_ref[2 * p + h] = carry[p, h]

        _sb_walk(i, blocks, l_ref, causal)
        o_ref[...] = acc_ref[...].astype(o_ref.dtype)
        of_ref[...] = acc_ref[...]
        if comm is not None:
            pl.when(last_step)(lambda: comm.finish(*riding))

    blk = pl.BlockSpec((t, g * LANES), lambda p, i: (i, p))
    return pl.pallas_call(
        body, name="sb_fwd", grid=grid,
        in_specs=_sb_qkv_specs(s, g) + c_in_specs,
        out_specs=[blk, blk] + c_out_specs,
        out_shape=[jax.ShapeDtypeStruct((s, SB_WIDTH), BF16), jax.ShapeDtypeStruct((s, SB_WIDTH), F32)] + c_out_shape,
        scratch_shapes=c_scratch + [pltpu.VMEM((2 * g, t, 1), F32), pltpu.VMEM((t, g * LANES), F32)],
        compiler_params=_cparams(),
    )(qkv, qkv, qkv, *c_ins)


def _sb_bwd(qkv, o, do, s, comm=None):
    t = SB_BLOCK
    g = 2
    nq = s // t
    grid = (SB_HEADS // 2 // g, nq)
    c_in_specs, c_out_specs, c_out_shape, c_scratch, c_ins, split = _host(comm, 5, 3)

    def body(*refs):
        ((q_ref, k_ref, v_ref, o_ref, do_ref), (dq_ref, dk_ref, dv_ref),
         (l_ref, e_ref, dq_acc, dk_acc, dv_acc), riding) = split(refs)
        i = pl.program_id(1)
        if comm is not None:
            first_step, last_step = _grid_ends(grid)
            pl.when(first_step)(lambda: comm.start(*riding))
        first, hmask, row, col = _sb_masks()
        after = jnp.where(row > col, 1.0, 0.0).astype(BF16)
        from_here = jnp.where(row >= col, 1.0, 0.0).astype(BF16)
        causal = col < row

        @pl.when(i == 0)
        def _():
            dk_acc[...] = jnp.zeros_like(dk_acc)
            dv_acc[...] = jnp.zeros_like(dv_acc)

        heads = [(p, h) for p in range(g) for h in range(2)]
        lanes = [slice(p * LANES, (p + 1) * LANES) for p in range(g)]
        q = [q_ref[p] for p in range(g)]
        do_ = [do_ref[:, lanes[p]] for p in range(g)]
        qh = {(p, h): q[p] * hmask[h] for p, h in heads}
        doh = {(p, h): do_[p] * hmask[h] for p, h in heads}
        total = {}
        for p in range(g):
            prod = do_[p].astype(F32) * o_ref[:, lanes[p]]
            total[p, 0] = jnp.sum(jnp.where(first, prod, 0.0), axis=1, keepdims=True)
            total[p, 1] = jnp.sum(jnp.where(first, 0.0, prod), axis=1, keepdims=True)
        l_ref[...] = jnp.zeros_like(l_ref)
        e_ref[...] = jnp.zeros_like(e_ref)
        dq_acc[...] = jnp.zeros_like(dq_acc)

        def blocks(todo):
            chains = [(b, p, h) for b in range(len(todo)) for p, h in heads]
            starts = [pl.multiple_of(kb * t, t) for kb, _ in todo]
            ks = {(b, p): k_ref[p, pl.ds(st, t), :] for b, st in enumerate(starts) for p in range(g)}
            vs = {(b, p): v_ref[p, pl.ds(st, t), :] for b, st in enumerate(starts) for p in range(g)}
            ar = {(b, p, h): _sb_logits(qh[p, h], ks[b, p], todo[b][1]) for b, p, h in chains}
            dw = {(b, p, h): _dot(doh[p, h], vs[b, p], 1, 1) for b, p, h in chains}
            later = {c: _split_mm(ar[c][1], after) for c in chains}
            carry = {(p, h): l_ref[2 * p + h] for p, h in heads}
            wb = {}
            for b, (_, mask) in enumerate(todo):
                for p, h in heads:
                    wc = jnp.exp(ar[b, p, h][0] + later[b, p, h] + carry[p, h])
                    wb[b, p, h] = (wc if mask is None else jnp.where(mask, wc, 0.0)).astype(BF16)
                carry = {(p, h): carry[p, h] + jnp.sum(ar[b, p, h][1], axis=1, keepdims=True) for p, h in heads}
            dvs = {(b, p, h): _dot(wb[b, p, h], do_[p], 0, 0) for b, p, h in chains}
            e = {c: dw[c] * wb[c].astype(F32) for c in chains}
            suffix = {c: _split_mm(e[c], from_here) for c in chains}
            e_carry = {(p, h): e_ref[2 * p + h] for p, h in heads}
            dz = {}
            for b, (_, mask) in enumerate(todo):
                for p, h in heads:
                    before = total[p, h] - (suffix[b, p, h] + e_carry[p, h])
                    dzc = e[b, p, h] - jnp.exp(ar[b, p, h][0]) * (e[b, p, h] + before)
                    dz[b, p, h] = (dzc if mask is None else jnp.where(mask, dzc, 0.0)).astype(BF16)
                e_carry = {(p, h): e_carry[p, h] + jnp.sum(e[b, p, h], axis=1, keepdims=True) for p, h in heads}
            dqs = {(b, p, h): _dot(dz[b, p, h], ks[b, p], 1, 0) for b, p, h in chains}
            dks = {(b, p, h): _dot(dz[b, p, h], q[p], 0, 0) for b, p, h in chains}
            for p in range(g):
                dq = dq_acc[:, lanes[p]]
                for b, st in enumerate(starts):
                    dq = dq + jnp.where(first, dqs[b, p, 0], dqs[b, p, 1])
                    dk_acc[pl.ds(st, t), lanes[p]] += jnp.where(first, dks[b, p, 0], dks[b, p, 1])
                    dv_acc[pl.ds(st, t), lanes[p]] += jnp.where(first, dvs[b, p, 0], dvs[b, p, 1])
                dq_acc[:, lanes[p]] = dq
            for p, h in heads:
                l_ref[2 * p + h] = carry[p, h]
                e_ref[2 * p + h] = e_carry[p, h]

        _sb_walk(i, blocks, l_ref, causal)
        dq_ref[...] = (dq_acc[...] * SB_SCALE).astype(dq_ref.dtype)

        @pl.when(i == nq - 1)
        def _():
            dk_ref[...] = dk_acc[...].astype(dk_ref.dtype)
            dv_ref[...] = dv_acc[...].astype(dv_ref.dtype)

        if comm is not None:
            pl.when(last_step)(lambda: comm.finish(*riding))

    once = pl.Buffered(1)
    q_spec, k_spec, v_spec = _sb_qkv_specs(s, g)
    k_spec = pl.BlockSpec(k_spec.block_shape, k_spec.index_map, pipeline_mode=once)
    v_spec = pl.BlockSpec(v_spec.block_shape, v_spec.index_map, pipeline_mode=once)
    blk = pl.BlockSpec((t, g * LANES), lambda p, i: (i, p))
    col_blk = pl.BlockSpec((s, g * LANES), lambda p, i: (0, p), pipeline_mode=once)
    sds = jax.ShapeDtypeStruct((s, SB_WIDTH), BF16)
    return pl.pallas_call(
        body, name="sb_bwd", grid=grid,
        in_specs=[q_spec, k_spec, v_spec, blk, blk] + c_in_specs,
        out_specs=[blk, col_blk, col_blk] + c_out_specs,
        out_shape=[sds, sds, sds] + c_out_shape,
        scratch_shapes=c_scratch + [pltpu.VMEM((2 * g, t, 1), F32), pltpu.VMEM((2 * g, t, 1), F32),
                                    pltpu.VMEM((t, g * LANES), F32), pltpu.VMEM((s, g * LANES), F32),
                                    pltpu.VMEM((s, g * LANES), F32)],
        compiler_params=_cparams(),
    )(qkv, qkv, qkv, o, do, *c_ins)


def _ret_log_gamma():
    lg = np.log1p(-np.exp2(-5.0 - np.arange(RET_HEADS, dtype=np.float32))).astype(np.float32)
    return jnp.asarray(np.broadcast_to(lg[:, None, None], (RET_HEADS, 8, LANES)).copy())


RET_SCRATCH = [pltpu.VMEM((RET_HEADS, RET_QK, RET_V), F32),
               pltpu.VMEM((RET_HEADS, RET_BLOCK, RET_BLOCK), F32),
               pltpu.VMEM((RET_HEADS, RET_BLOCK, 1), F32),
               pltpu.VMEM((RET_HEADS, RET_BLOCK, 1), F32)]


def _ret_begin(n, lg_ref, state, within, q_dec, k_dec):
    @pl.when(n == 0)
    def _():
        c = RET_BLOCK
        state[...] = jnp.zeros_like(state)
        row = lax.broadcasted_iota(jnp.int32, (c, c), 0)
        col = lax.broadcasted_iota(jnp.int32, (c, c), 1)
        rel = jnp.maximum(row - col, 0).astype(F32)
        idx = lax.broadcasted_iota(jnp.int32, (c, 1), 0).astype(F32)
        for h in range(RET_HEADS):
            lg = lg_ref[h, 0:1, 0:1]
            within[h] = jnp.where(row >= col, jnp.exp(lg * rel), 0.0)
            q_dec[h] = jnp.exp(lg * (idx + 1.0))
            k_dec[h] = jnp.exp(lg * (c - 1.0 - idx))


def _chunk_decay(lg_ref, h):
    return jnp.exp(lg_ref[h, 0:1, 0:1] * float(RET_BLOCK))


def _ret_heads(x, width):
    return [x[:, h * width:(h + 1) * width] for h in range(RET_HEADS)]


def _ret_specs(s, reverse=False):
    c = RET_BLOCK
    per_step = min(RET_CHUNKS_PER_STEP, s // c)
    rows = c * per_step
    nc = s // rows
    pos = (lambda n: nc - 1 - n) if reverse else (lambda n: n)
    chunks = [slice(u * c, (u + 1) * c) for u in range(per_step)]
    q_spec = pl.BlockSpec((rows, RET_QK_WIDTH), lambda n: (pos(n), 0))
    k_spec = pl.BlockSpec((rows, RET_QK_WIDTH), lambda n: (pos(n), 1))
    v_spec = pl.BlockSpec((rows, RET_V_WIDTH), lambda n: (pos(n), 0))
    lg_spec = pl.BlockSpec((RET_HEADS, 8, LANES), lambda n: (0, 0, 0))
    rope_spec = pl.BlockSpec((rows, RET_QK), lambda n: (pos(n), 0))
    return nc, chunks[::-1] if reverse else chunks, q_spec, k_spec, v_spec, lg_spec, rope_spec


def _ret_fwd(rqk, rvg, s):
    nc, chunks, q_spec, k_spec, v_spec, lg_spec, _ = _ret_specs(s)
    g_spec = pl.BlockSpec(v_spec.block_shape, lambda n: (n, 1))
    heads = range(RET_HEADS)

    def body(q_ref, k_ref, v_ref, g_ref, lg_ref, r_ref, y_ref, state, within, q_dec, k_dec):
        n = pl.program_id(0)
        _ret_begin(n, lg_ref, state, within, q_dec, k_dec)
        for rows in chunks:
            q, k = _ret_heads(q_ref[rows], RET_QK), _ret_heads(k_ref[rows], RET_QK)
            v, g = _ret_heads(v_ref[rows], RET_V), _ret_heads(g_ref[rows], RET_V)
            scores = [_dot(q[h].astype(BF16), k[h].astype(BF16), 1, 1) * within[h] for h in heads]
            cross = [_dot((q[h] * q_dec[h]).astype(BF16), state[h].astype(BF16), 1, 0) for h in heads]
            out = [_dot(scores[h].astype(BF16), v[h], 1, 0) + cross[h] for h in heads]
            grown = [_dot((k[h] * k_dec[h]).astype(BF16), v[h], 0, 0) for h in heads]
            for h in heads:
                sl = slice(h * RET_V, (h + 1) * RET_V)
                r_ref[rows, sl] = out[h]
                xhat, _ = _norm(out[h])
                gh = g[h].astype(F32)
                y_ref[rows, sl] = (gh * _sigmoid(gh) * xhat).astype(y_ref.dtype)
                state[h] = state[h] * _chunk_decay(lg_ref, h) + grown[h]

    return pl.pallas_call(
        body, name="ret_fwd", grid=(nc,),
        in_specs=[q_spec, k_spec, v_spec, g_spec, lg_spec],
        out_specs=[v_spec, v_spec],
        out_shape=[jax.ShapeDtypeStruct((s, RET_V_WIDTH), F32), jax.ShapeDtypeStruct((s, RET_V_WIDTH), BF16)],
        scratch_shapes=RET_SCRATCH,
        compiler_params=_cparams(),
    )(rqk, rqk, rvg, rvg, _ret_log_gamma())


def _rope_bwd(d, cos, sin):
    return d * cos + _swap_halves(d * sin)


def _ret_bwd_q(rqk, rv, d_out, cos2, sin2, s):
    nc, chunks, q_spec, k_spec, v_spec, lg_spec, rope_spec = _ret_specs(s)
    heads = range(RET_HEADS)

    def body(k_ref, v_ref, d_ref, lg_ref, cos_ref, sin_ref, dq_ref, state, within, q_dec, k_dec):
        n = pl.program_id(0)
        _ret_begin(n, lg_ref, state, within, q_dec, k_dec)
        for rows in chunks:
            k = _ret_heads(k_ref[rows], RET_QK)
            v, d = _ret_heads(v_ref[rows], RET_V), _ret_heads(d_ref[rows], RET_V)
            cos, sin = cos_ref[rows], sin_ref[rows]
            d_scores = [_dot(d[h], v[h], 1, 1) * within[h] for h in heads]
            cross = [q_dec[h] * _dot(d[h], state[h].astype(BF16), 1, 1) for h in heads]
            dq = [_dot(d_scores[h].astype(BF16), k[h].astype(BF16), 1, 0) + cross[h] for h in heads]
            grown = [_dot((k[h] * k_dec[h]).astype(BF16), v[h], 0, 0) for h in heads]
            for h in heads:
                sl = slice(h * RET_QK, (h + 1) * RET_QK)
                dq_ref[rows, sl] = (_rope_bwd(dq[h], cos, sin) * RET_SCALE).astype(dq_ref.dtype)
                state[h] = state[h] * _chunk_decay(lg_ref, h) + grown[h]

    return pl.pallas_call(
        body, name="ret_bwd_q", grid=(nc,),
        in_specs=[k_spec, v_spec, v_spec, lg_spec, rope_spec, rope_spec],
        out_specs=q_spec,
        out_shape=jax.ShapeDtypeStruct((s, RET_QK_WIDTH), BF16),
        scratch_shapes=RET_SCRATCH,
        compiler_params=_cparams(),
    )(rqk, rv, d_out, _ret_log_gamma(), cos2, sin2)


def _ret_bwd_kv(rqk, rv, d_out, cos2, sin2, s):
    nc, chunks, q_spec, k_spec, v_spec, lg_spec, rope_spec = _ret_specs(s, reverse=True)
    heads = range(RET_HEADS)

    def body(q_ref, k_ref, v_ref, d_ref, lg_ref, cos_ref, sin_ref, dk_ref, dv_ref, state, within, q_dec, k_dec):
        n = pl.program_id(0)
        _ret_begin(n, lg_ref, state, within, q_dec, k_dec)
        for rows in chunks:
            q, k = _ret_heads(q_ref[rows], RET_QK), _ret_heads(k_ref[rows], RET_QK)
            v, d = _ret_heads(v_ref[rows], RET_V), _ret_heads(d_ref[rows], RET_V)
            cos, sin = cos_ref[rows], sin_ref[rows]
            qb, kb = [q[h].astype(BF16) for h in heads], [k[h].astype(BF16) for h in heads]
            st = [state[h].astype(BF16) for h in heads]
            scores = [_dot(qb[h], kb[h], 1, 1) * within[h] for h in heads]
            d_scores = [_dot(d[h], v[h], 1, 1) * within[h] for h in heads]
            dk = [_dot(d_scores[h].astype(BF16), qb[h], 0, 0) + k_dec[h] * _dot(v[h], st[h], 1, 1) for h in heads]
            dv = [_dot(scores[h].astype(BF16), d[h], 0, 0) + k_dec[h] * _dot(kb[h], st[h], 1, 0) for h in heads]
            grown = [_dot((q[h] * q_dec[h]).astype(BF16), d[h], 0, 0) for h in heads]
            for h in heads:
                dk_ref[rows, h * RET_QK:(h + 1) * RET_QK] = _rope_bwd(dk[h], cos, sin).astype(dk_ref.dtype)
                dv_ref[rows, h * RET_V:(h + 1) * RET_V] = dv[h].astype(dv_ref.dtype)
                state[h] = state[h] * _chunk_decay(lg_ref, h) + grown[h]

    return pl.pallas_call(
        body, name="ret_bwd_kv", grid=(nc,),
        in_specs=[q_spec, k_spec, v_spec, v_spec, lg_spec, rope_spec, rope_spec],
        out_specs=[q_spec, v_spec],
        out_shape=[jax.ShapeDtypeStruct((s, RET_QK_WIDTH), BF16), jax.ShapeDtypeStruct((s, RET_V_WIDTH), BF16)],
        scratch_shapes=RET_SCRATCH,
        compiler_params=_cparams(),
    )(rqk, rqk, rv, d_out, _ret_log_gamma(), cos2, sin2)


def _xattn_probs(scores):
    sc = scores - jnp.max(scores, axis=-1, keepdims=True)
    p = jnp.exp(sc)
    return p / jnp.sum(p, axis=-1, keepdims=True)


def _xattn_heads(q_ref, kv_ref):
    sls = [slice(h * MEM_DIM, (h + 1) * MEM_DIM) for h in range(MEM_HEADS)]
    q = [q_ref[:, sl] for sl in sls]
    k = [kv_ref[:, sl] for sl in sls]
    v = [kv_ref[:, D_MODEL + h * MEM_DIM:D_MODEL + (h + 1) * MEM_DIM] for h in range(MEM_HEADS)]
    return sls, q, k, v


def _xattn_fwd(qm, kv, s):
    tq = min(XATTN_ROWS, s)
    heads = range(MEM_HEADS)

    def body(q_ref, kv_ref, o_ref):
        sls, q, k, v = _xattn_heads(q_ref, kv_ref)
        scores = [_dot(q[h], k[h], 1, 1) for h in heads]
        p = [_xattn_probs(scores[h]).astype(BF16) for h in heads]
        out = [_dot(p[h], v[h], 1, 0) for h in heads]
        for h in heads:
            o_ref[:, sls[h]] = out[h].astype(o_ref.dtype)

    return pl.pallas_call(
        body, name="xattn_fwd", grid=(s // tq,),
        in_specs=[pl.BlockSpec((tq, D_MODEL), lambda i: (i, 0)),
                  pl.BlockSpec((MEM_LEN, 2 * D_MODEL), lambda i: (0, 0))],
        out_specs=pl.BlockSpec((tq, D_MODEL), lambda i: (i, 0)),
        out_shape=jax.ShapeDtypeStruct((s, D_MODEL), BF16),
        compiler_params=_cparams(),
    )(qm, kv)


def _xattn_bwd(qm, kv, do, s):
    tq = min(XATTN_ROWS, s)

    def body(q_ref, kv_ref, do_ref, dq_ref, dkv_ref):
        i = pl.program_id(0)

        @pl.when(i == 0)
        def _():
            dkv_ref[...] = jnp.zeros_like(dkv_ref)

        heads = range(MEM_HEADS)
        sls, q, k, v = _xattn_heads(q_ref, kv_ref)
        d = [do_ref[:, sl] for sl in sls]
        scores = [_dot(q[h], k[h], 1, 1) for h in heads]
        dp = [_dot(d[h], v[h], 1, 1) for h in heads]
        p = [_xattn_probs(scores[h]) for h in heads]
        ds = [(p[h] * (dp[h] - jnp.sum(p[h] * dp[h], axis=-1, keepdims=True))).astype(BF16) for h in heads]
        dq = [_dot(ds[h], k[h], 1, 0) for h in heads]
        dk = [_dot(ds[h], q[h], 0, 0) for h in heads]
        dv = [_dot(p[h].astype(BF16), d[h], 0, 0) for h in heads]
        for h in heads:
            dq_ref[:, sls[h]] = (dq[h] * MEM_SCALE).astype(dq_ref.dtype)
            dkv_ref[:, sls[h]] += dk[h]
            dkv_ref[:, D_MODEL + h * MEM_DIM:D_MODEL + (h + 1) * MEM_DIM] += dv[h]

    row_blk = pl.BlockSpec((tq, D_MODEL), lambda i: (i, 0))
    kv_blk = pl.BlockSpec((MEM_LEN, 2 * D_MODEL), lambda i: (0, 0))
    return pl.pallas_call(
        body, name="xattn_bwd", grid=(s // tq,),
        in_specs=[row_blk, kv_blk, row_blk],
        out_specs=[row_blk, kv_blk],
        out_shape=[jax.ShapeDtypeStruct((s, D_MODEL), BF16), jax.ShapeDtypeStruct((MEM_LEN, 2 * D_MODEL), F32)],
        compiler_params=_cparams(),
    )(qm, kv, do)


def _place():
    x, y, c = lax.axis_index("x"), lax.axis_index("y"), lax.axis_index("c")
    others = [(1 - x, y), (x, 1 - y), (1 - x, 1 - y)]
    return x, y, c, others


def _slab(ref, axis, chip, size):
    start = pl.multiple_of(chip * size, LANES if axis == 1 else 16)
    if axis == 0:
        return ref.at[pl.ds(start, size), :]
    return ref.at[:, pl.ds(start, size)]


class _CommPlan:
    def __init__(self, ins, out_shape, scratch, start, finish):
        self.ins, self.out_shape, self.scratch, self.start, self.finish = ins, out_shape, scratch, start, finish

    @property
    def specs(self):
        any_spec = pl.BlockSpec(memory_space=pl.ANY)
        return [any_spec] * len(self.ins), [any_spec] * len(self.out_shape)


def _gather_plan(names, shards):
    spec = {name: (shape, axis) for name, shape, axis in BIG}
    nw = len(names)

    def shard_half(ref, c):
        rows = ref.shape[0] // 2
        return ref.at[pl.ds(pl.multiple_of(c * rows, 16), rows), :]

    def region(ref, w, chip, c):
        shape, axis = spec[names[w]]
        size = shape[axis] // N_CHIPS
        if axis == 0:
            rows = size // 2
            return ref.at[pl.ds(pl.multiple_of(chip * size + c * rows, 16), rows), :]
        rows = shape[0] // 2
        return ref.at[pl.ds(pl.multiple_of(c * rows, 16), rows), pl.ds(pl.multiple_of(chip * size, LANES), size)]

    def ops(shard, full, sems):
        ici_send, ici_recv, d2d_send, d2d_recv, local_sems = sems
        x, y, c, others = _place()
        mine, sibling = 2 * x + y, (x, y, 1 - c)
        local, over_ici, arrived, passed_on, from_sibling = [], [], [], [], []
        for w in range(nw):
            shape, axis = spec[names[w]]
            local.append(pltpu.make_async_copy(shard[w], _slab(full[w], axis, mine, shape[axis] // N_CHIPS),
                                               local_sems.at[w]))
            for t, (qx, qy) in enumerate(others):
                n, theirs = 3 * w + t, 2 * qx + qy
                over_ici.append(pltpu.make_async_remote_copy(
                    src_ref=shard_half(shard[w], c), dst_ref=region(full[w], w, mine, c),
                    send_sem=ici_send.at[n], recv_sem=ici_recv.at[n], device_id=(qx, qy, c), device_id_type=MESH))
                arrived.append(pltpu.make_async_remote_copy(
                    src_ref=shard_half(shard[w], c), dst_ref=region(full[w], w, theirs, c),
                    send_sem=ici_send.at[n], recv_sem=ici_recv.at[n], device_id=(qx, qy, c), device_id_type=MESH))
                passed_on.append(pltpu.make_async_remote_copy(
                    src_ref=region(full[w], w, theirs, c), dst_ref=region(full[w], w, theirs, c),
                    send_sem=d2d_send.at[n], recv_sem=d2d_recv.at[n], device_id=sibling, device_id_type=MESH))
                from_sibling.append(pltpu.make_async_remote_copy(
                    src_ref=region(full[w], w, theirs, c), dst_ref=region(full[w], w, theirs, 1 - c),
                    send_sem=d2d_send.at[n], recv_sem=d2d_recv.at[n], device_id=sibling, device_id_type=MESH))
        return local, over_ici, arrived, passed_on, from_sibling

    def start(shard, full, sems):
        local, over_ici, _, _, _ = ops(shard, full, sems)
        for cp in local + over_ici:
            cp.start()

    def finish(shard, full, sems):
        local, over_ici, arrived, passed_on, from_sibling = ops(shard, full, sems)
        for got, onward in zip(arrived, passed_on, strict=True):
            got.wait_recv()
            onward.start()
        for got in from_sibling:
            got.wait_recv()
        for cp in over_ici + passed_on:
            cp.wait_send()
        for cp in local:
            cp.wait()

    dma = pltpu.SemaphoreType.DMA
    return _CommPlan(
        ins=[shards[name] for name in names],
        out_shape=[jax.ShapeDtypeStruct(spec[name][0], BF16) for name in names],
        scratch=[dma((3 * nw,)), dma((3 * nw,)), dma((3 * nw,)), dma((3 * nw,)), dma((nw,))],
        start=start, finish=finish)


def _shard_shape(shape, axis):
    return tuple(d // N_CHIPS if a == axis else d for a, d in enumerate(shape))


def _exchange_plan(names, grads):
    spec = {name: (shape, axis) for name, shape, axis in BIG}
    nw = len(names)

    def ops(grad, stack, sems):
        send_sems, recv_sems, local_sems = sems
        x, y, c, others = _place()
        mine = 2 * x + y
        me, sibling = (x, y, c), (x, y, 1 - c)

        def dev(px, py, pc):
            return 4 * px + 2 * py + pc

        def copy(w, n, src, slot, to):
            return pltpu.make_async_remote_copy(
                src_ref=src, dst_ref=stack[w].at[slot], send_sem=send_sems.at[7 * w + n],
                recv_sem=recv_sems.at[7 * w + n], device_id=to, device_id_type=MESH)

        local, first, arrived, passed_on, from_sibling = [], [], [], [], []
        for w in range(nw):
            shape, axis = spec[names[w]]
            size = shape[axis] // N_CHIPS
            own = _slab(grad[w], axis, mine, size)
            local.append(pltpu.make_async_copy(own, stack[w].at[dev(*me)], local_sems.at[w]))
            first.append(copy(w, 0, own, dev(*me), sibling))
            from_sibling.append(copy(w, 0, own, dev(*sibling), me))
            for t, (qx, qy) in enumerate(others):
                got = stack[w].at[dev(qx, qy, c)]
                first.append(copy(w, 1 + t, _slab(grad[w], axis, 2 * qx + qy, size), dev(*me), (qx, qy, c)))
                arrived.append(copy(w, 1 + t, got, dev(qx, qy, c), me))
                passed_on.append(copy(w, 4 + t, got, dev(qx, qy, c), sibling))
                from_sibling.append(copy(w, 4 + t, got, dev(qx, qy, 1 - c), me))
        return local, first, arrived, passed_on, from_sibling

    def start(grad, stack, sems):
        local, first, _, _, _ = ops(grad, stack, sems)
        for cp in local + first:
            cp.start()

    def finish(grad, stack, sems):
        local, first, arrived, passed_on, from_sibling = ops(grad, stack, sems)
        for got, onward in zip(arrived, passed_on, strict=True):
            got.wait_recv()
            onward.start()
        for got in from_sibling:
            got.wait_recv()
        for cp in first + passed_on:
            cp.wait_send()
        for cp in local:
            cp.wait()

    dma = pltpu.SemaphoreType.DMA
    return _CommPlan(
        ins=[grads[name] for name in names],
        out_shape=[jax.ShapeDtypeStruct((N_DEV,) + _shard_shape(*spec[name]), BF16) for name in names],
        scratch=[dma((7 * nw,)), dma((7 * nw,)), dma((nw,))],
        start=start, finish=finish)


def _adamw(w, g, m, v):
    m = ADAM_B1 * m + (1.0 - ADAM_B1) * g
    v = ADAM_B2 * v + (1.0 - ADAM_B2) * (g * g)
    m_hat = m / (1.0 - ADAM_B1 ** ADAM_STEP)
    v_hat = v / (1.0 - ADAM_B2 ** ADAM_STEP)
    delta = -ADAM_LR * (m_hat / (jnp.sqrt(v_hat) + ADAM_EPS) + ADAM_WD * w)
    return delta, m, v


def _reduce_adamw(name, stack, w, m, v):
    rows, cols = w.shape
    tr = next(t for t in (256, 128, 64) if rows % t == 0)

    def body(s_ref, w_ref, m_ref, v_ref, g_ref, d_ref, nm_ref, nv_ref):
        g = s_ref[0].astype(F32)
        for d in range(1, N_DEV):
            g = g + s_ref[d].astype(F32)
        g_ref[...] = g
        d_ref[...], nm_ref[...], nv_ref[...] = _adamw(w_ref[...], g, m_ref[...], v_ref[...])

    blk = pl.BlockSpec((tr, cols), lambda i: (i, 0))
    return pl.pallas_call(
        body, name=name, grid=(rows // tr,),
        in_specs=[pl.BlockSpec((N_DEV, tr, cols), lambda i: (0, i, 0)), blk, blk, blk],
        out_specs=[blk] * 4, out_shape=[jax.ShapeDtypeStruct((rows, cols), F32)] * 4,
        compiler_params=_cparams(),
    )(stack, w, m, v)


def _small_step(pack, w, m, v):
    def body(p_ref, w_ref, m_ref, v_ref, g_ref, d_ref, nm_ref, nv_ref, loss_ref, all_ref, send_sems, recv_sems):
        x, y, c, _ = _place()
        me = 4 * x + 2 * y + c
        all_ref[me] = p_ref[...]
        sent = []
        for n in range(1, N_DEV):
            peer = me ^ n
            cp = pltpu.make_async_remote_copy(
                src_ref=p_ref, dst_ref=all_ref.at[me], send_sem=send_sems.at[n - 1], recv_sem=recv_sems.at[n - 1],
                device_id=(peer // 4, (peer // 2) % 2, peer % 2), device_id_type=MESH)
            cp.start()
            sent.append(cp)
        for n in range(1, N_DEV):
            peer = me ^ n
            pltpu.make_async_remote_copy(
                src_ref=p_ref, dst_ref=all_ref.at[peer], send_sem=send_sems.at[n - 1], recv_sem=recv_sems.at[n - 1],
                device_id=(peer // 4, (peer // 2) % 2, peer % 2), device_id_type=MESH).wait_recv()
        for cp in sent:
            cp.wait_send()
        tot = all_ref[0]
        for d in range(1, N_DEV):
            tot = tot + all_ref[d]
        g = tot[:SMALL_ROWS]
        g_ref[...] = g
        d_ref[...], nm_ref[...], nv_ref[...] = _adamw(w_ref[...], g, m_ref[...], v_ref[...])
        loss_ref[...] = jnp.sum(jnp.sum(tot[SMALL_ROWS:], axis=1, keepdims=True), axis=0, keepdims=True)

    vm = pl.BlockSpec(memory_space=pltpu.VMEM)
    small = jax.ShapeDtypeStruct((SMALL_ROWS, LANES), F32)
    return pl.pallas_call(
        body, name="small_step",
        in_specs=[vm] * 4, out_specs=[vm] * 5,
        out_shape=[small] * 4 + [jax.ShapeDtypeStruct((1, 1), F32)],
        scratch_shapes=[pltpu.VMEM((N_DEV, PACK_ROWS, LANES), F32),
                        pltpu.SemaphoreType.DMA((N_DEV - 1,)), pltpu.SemaphoreType.DMA((N_DEV - 1,))],
    )(pack, w, m, v)


LATER_WEIGHTS = tuple(name for name, _, _ in BIG if name != "w_in")


def _layer_step(x, mem, tgt, shards, vec):
    s = x.shape[0]
    d = D_MODEL
    tm = min(ROW_TILE, s)
    tl = min(WIDE_TILE, s)
    xb, cos2, sin2, w_in = _prep(x, _gather_plan(("w_in",), shards))
    bf = lambda w: ((s, w), BF16)
    f32 = lambda w: ((s, w), F32)

    w_sb, w_rqk = w_in[:, :OFF_RET_Q], w_in[:, OFF_RET_Q:OFF_RET_V]
    w_rvg, w_gate = w_in[:, OFF_RET_V:OFF_GATE], w_in[:, OFF_GATE:]
    q_scale = lambda width, q_width, scale: jnp.concatenate(
        [jnp.full((1, q_width), scale, F32), jnp.ones((1, width - q_width), F32)], axis=1)
    n_groups = 3 * SB_WIDTH // LANES

    def sb_epi(acc, t, i, j):
        scaled = acc * t[0]
        return [jnp.stack([scaled[:, g * LANES:(g + 1) * LANES] for g in range(n_groups)])], []

    (sb_qkv,) = _mm(
        "in_sb", xb, w_sb, s, 3 * SB_WIDTH, d, tm=tl, tn=3 * SB_WIDTH, tk=d, epi=sb_epi,
        ins=[(q_scale(3 * SB_WIDTH, SB_WIDTH, SB_SCALE), *_rowvec(3 * SB_WIDTH))],
        outs=[((n_groups, s, LANES), BF16, (n_groups, tl, LANES), lambda i, j: (0, i, 0))])

    def rope_epi(acc, t, i, j):
        cos, sin, scale = t
        parts = []
        for g in range(acc.shape[1] // RET_QK):
            xg = acc[:, g * RET_QK:(g + 1) * RET_QK]
            parts.append(xg * cos + _swap_halves(xg) * sin)
        return [jnp.concatenate(parts, axis=1) * scale], []

    rope_in = ((tl, RET_QK), lambda i, j: (i, 0))
    (rqk,) = _mm("in_rqk", xb, w_rqk, s, 2 * RET_QK_WIDTH, d, tm=tl, tn=2 * RET_QK_WIDTH, tk=d, epi=rope_epi,
                 chunk=MXU_COLS,
                 ins=[(cos2, *rope_in), (sin2, *rope_in),
                      (q_scale(2 * RET_QK_WIDTH, RET_QK_WIDTH, RET_SCALE), *_rowvec(2 * RET_QK_WIDTH))],
                 outs=[(*f32(2 * RET_QK_WIDTH), *_tile(tl, 2 * RET_QK_WIDTH))])
    (rvg,) = _mm("in_rvg", xb, w_rvg, s, 2 * RET_V_WIDTH, d, tm=tl, tn=2 * RET_V_WIDTH, tk=d, chunk=MXU_COLS,
                 epi=_plain, outs=[(*bf(2 * RET_V_WIDTH), *_tile(tl, 2 * RET_V_WIDTH))])
    (gates,) = _mm("in_gate", xb, w_gate, s, 2 * d, d, tm=tl, tn=2 * d, tk=d, chunk=MXU_COLS,
                   epi=lambda acc, t, i, j: ([_sigmoid(acc + t[0])], []),
                   ins=[(vec["b_gate"], *_rowvec(2 * d))], outs=[(*bf(2 * d), *_tile(tl, 2 * d))])

    sb_out, sb_out_f32, *gathered = _sb_fwd(sb_qkv, s, comm=_gather_plan(LATER_WEIGHTS, shards))
    wt = dict(zip(LATER_WEIGHTS, gathered, strict=True))
    ret, gated = _ret_fwd(rqk, rvg, s)
    (y_sb,) = _mm("sb_o", sb_out, wt["w_sb_o"], s, d, SB_WIDTH, tm=tl, tn=d, tk=SB_WIDTH, epi=_plain,
                  outs=[(*bf(d), *_tile(tl, d))])
    y_ret, mixin = _mm(
        "ret_o", gated, wt["w_ret_o"], s, d, RET_V_WIDTH, tm=tl, tn=d, tk=RET_V_WIDTH, chunk=MXU_COLS,
        epi=lambda acc, t, i, j: ([acc, t[0].astype(F32) * t[2].astype(F32) + t[1].astype(F32) * acc], []),
        ins=[(gates, *_tile(tl, d)), (gates, *_tile(tl, d, 1)), (y_sb, *_tile(tl, d))],
        outs=[(*bf(d), *_tile(tl, d)), (*bf(d), *_tile(tl, d))])

    def ln_epi(acc, t, i, j):
        *res, g, b = t
        prev = res[0] if len(res) == 1 else res[0] * res[1] + res[2]
        xhat, rstd = _norm(DN_ALPHA * prev + acc)
        return [xhat * g + b, xhat, rstd], []

    full = _tile(tm, d)
    col1 = ((tm, 1), lambda i, j: (i, 0))
    vec_in = lambda name: (vec[name], *_rowvec(d))
    ln_outs = [(*bf(d), *full), (*f32(d), *full), ((s, 1), F32, *col1)]
    x1b, xhat1, rstd1 = _mm(
        "mix_o", mixin, wt["w_mix_o"], s, d, d, tm=tm, tn=d, tk=d, epi=ln_epi,
        ins=[(x, *full), vec_in("ln1_g"), vec_in("ln1_b")], outs=ln_outs)

    (qm,) = _mm("mem_q", x1b, wt["w_mem_q"], s, d, d, tm=tl, tn=d, tk=d,
                epi=lambda acc, t, i, j: ([acc * MEM_SCALE], []), outs=[(*bf(d), *_tile(tl, d))])
    (kv,) = _mm("mem_kv", mem, wt["w_mem_kv"], MEM_LEN, 2 * d, d, tm=MEM_LEN, tn=d, tk=d, epi=_plain,
                outs=[((MEM_LEN, 2 * d), BF16, *_tile(MEM_LEN, d))])
    att = _xattn_fwd(qm, kv, s)
    x2b, xhat2, rstd2 = _mm(
        "mem_o", att, wt["w_mem_o"], s, d, d, tm=tm, tn=d, tk=d, epi=ln_epi,
        ins=[(xhat1, *full), vec_in("ln1_g"), vec_in("ln1_b"), vec_in("ln2_g"), vec_in("ln2_b")], outs=ln_outs)

    fh = FFN_HIDDEN
    tf = fh // 2
    def swiglu_epi(acc, t, i, j):
        a = acc[0].astype(BF16).astype(F32)
        return [a, acc[1], a * _sigmoid(a) * acc[1]], []

    f1, f2, act = _mm(
        "ffn_in", x2b, wt["w_ffn_in"], s, 2 * fh, d, tm=tm, tn=2 * fh, tk=d, epi=swiglu_epi, chunk=MXU_COLS,
        halves=True, outs=[(*bf(fh), *_tile(tm, fh))] * 3)

    def head_epi(acc, t, i, j):
        prev_hat, prev_g, prev_b, g, b, target = t
        xhat, rstd = _norm(DN_ALPHA * (prev_hat * prev_g + prev_b) + acc)
        err = xhat * g + b - target
        dy = err * (1.0 / d)
        du = _norm_bwd(dy * g, xhat, rstd)
        return [du], [_colsum(dy * xhat), _colsum(dy), _colsum(err * err) * (0.5 / d)]

    vec_acc = ((1, d), F32)
    du3b, dg3, db3, loss_cols = _mm(
        "ffn_out", act, wt["w_ffn_out"], s, d, fh, tm=tm, tn=d, tk=fh, epi=head_epi,
        ins=[(xhat2, *full), vec_in("ln2_g"), vec_in("ln2_b"), vec_in("ln3_g"), vec_in("ln3_b"), (tgt, *full)],
        outs=[(*bf(d), *full)], accs=[vec_acc] * 3)

    grads = {}
    ts = min(SEQ_TILE, s)

    def wgrad(name, a, b, m, n, tm_, tn_, tk_=None):
        (g,) = _mm(name, a, b, m, n, a.shape[0], tm=tm_, tn=tn_, tk=tk_ or ts, ta=True, epi=_plain,
                   outs=[((m, n), BF16, *_tile(tm_, tn_))])
        return g

    def wgrad_wide(name, a, pieces, tm_):
        m, width = a.shape[1], sum(p.shape[1] for p in pieces)
        (g,) = _mm(name, a, pieces, m, width, s, tm=tm_, tn=width, tk=min(ROW_TILE, s // 2), ta=True, epi=_plain,
                   outs=[((m, width), BF16, *_tile(tm_, width))])
        return g

    def ffn_bwd_epi(acc, t, i, j):
        a, b = t[0].astype(F32), t[1].astype(F32)
        sg = _sigmoid(a)
        return [acc * b * (sg * (1.0 + a * (1.0 - sg))), acc * (a * sg)], []

    df1, df2 = _mm(
        "ffn_out_t", du3b, wt["w_ffn_out"], s, fh, d, tm=tm, tn=fh, tk=d, tb=True, epi=ffn_bwd_epi, chunk=MXU_COLS,
        ins=[(f1, *_tile(tm, fh)), (f2, *_tile(tm, fh))],
        outs=[(*bf(fh), *_tile(tm, fh)), (*bf(fh), *_tile(tm, fh))])
    grads["w_ffn_out"] = wgrad("g_ffn_out", act, du3b, fh, d, tf, d)
    grads["w_ffn_in"] = wgrad_wide("g_ffn_in", x2b, [df1, df2], d // 2)

    def ln_bwd(name, a, b, k, tk, b_off, more, scales, xhat, rstd, g):
        def epi(acc, t, i, j):
            *extra, xh, rs, gg = t
            dy = acc
            for e, sc in zip(extra, scales, strict=True):
                dy = dy + e.astype(F32) * sc
            return [_norm_bwd(dy * gg, xh, rs)], [_colsum(dy * xh), _colsum(dy)]

        return _mm(name, a, b, s, d, k, tm=tm, tn=d, tk=tk, tb=True, b_off=b_off, epi=epi,
                   ins=[(e, *full) for e in more] + [(xhat, *full), (rstd, *col1), (g, *_rowvec(d))],
                   outs=[(*bf(d), *full)], accs=[vec_acc] * 2)

    du2b, dg2, db2 = ln_bwd("ffn_in_t", [df1, df2], wt["w_ffn_in"], 2 * fh, 2 * fh, (0, 0), [du3b], [DN_ALPHA],
                            xhat2, rstd2, vec["ln2_g"])

    (datt,) = _mm("mem_o_t", du2b, wt["w_mem_o"], s, d, d, tm=tl, tn=d, tk=d, tb=True, epi=_plain,
                  outs=[(*bf(d), *_tile(tl, d))])
    grads["w_mem_o"] = wgrad("g_mem_o", att, du2b, d, d, d, d)
    dqm, dkv = _xattn_bwd(qm, kv, datt, s)
    grads["w_mem_q"] = wgrad("g_mem_q", x1b, dqm, d, d, d, d)
    grads["w_mem_kv"] = wgrad("g_mem_kv", mem, dkv, d, 2 * d, d, d, MEM_LEN)
    du1b, dg1, db1 = ln_bwd("mem_q_t", dqm, wt["w_mem_q"], d, d, (0, 0), [du2b], [DN_ALPHA],
                            xhat1, rstd1, vec["ln1_g"])

    def merge_bwd_epi(acc, t, i, j):
        g0, g1, ysb, yret = (v.astype(F32) for v in t)
        dgate0 = acc * ysb * (g0 * (1.0 - g0))
        dgate1 = acc * yret * (g1 * (1.0 - g1))
        return [dgate0, dgate1, acc * g0, acc * g1], [_colsum(dgate0), _colsum(dgate1)]

    dgate0, dgate1, dy_sb, dy_ret, dbg0, dbg1 = _mm(
        "mix_o_t", du1b, wt["w_mix_o"], s, d, d, tm=tm, tn=d, tk=d, tb=True, epi=merge_bwd_epi,
        ins=[(gates, *full), (gates, *_tile(tm, d, 1)), (y_sb, *full), (y_ret, *full)],
        outs=[(*bf(d), *full)] * 4, accs=[vec_acc] * 2)
    grads["w_mix_o"] = wgrad("g_mix_o", mixin, du1b, d, d, d, d)
    grads["w_sb_o"] = wgrad("g_sb_o", sb_out, dy_sb, SB_WIDTH, d, SB_WIDTH, d)
    grads["w_ret_o"] = wgrad("g_ret_o", gated, dy_ret, RET_V_WIDTH, d, RET_V_WIDTH, d)
    (dsb_out,) = _mm("sb_o_t", dy_sb, wt["w_sb_o"], s, SB_WIDTH, d, tm=tl, tn=SB_WIDTH, tk=d, tb=True, epi=_plain,
                     outs=[(*bf(SB_WIDTH), *_tile(tl, SB_WIDTH))])

    def gate_norm_bwd_epi(acc, t, i, j):
        r, g = t[0], t[1].astype(F32)
        drg, dret = [], []
        for h in range(acc.shape[1] // RET_V):
            sl = slice(h * RET_V, (h + 1) * RET_V)
            xhat, rstd = _norm(r[:, sl])
            gg, dd = g[:, sl], acc[:, sl]
            sg = _sigmoid(gg)
            drg.append(dd * xhat * (sg * (1.0 + gg * (1.0 - sg))))
            dret.append(_norm_bwd(dd * (gg * sg), xhat, rstd))
        return [jnp.concatenate(drg, axis=1), jnp.concatenate(dret, axis=1)], []

    drg, dret = _mm(
        "ret_o_t", dy_ret, wt["w_ret_o"], s, RET_V_WIDTH, d, tm=tm, tn=d, tk=d, tb=True, epi=gate_norm_bwd_epi,
        chunk=MXU_COLS,
        ins=[(ret, *full), (rvg, *_tile(tm, d, 1))],
        outs=[(*bf(RET_V_WIDTH), *full)] * 2)

    drq = _ret_bwd_q(rqk, rvg, dret, cos2, sin2, s)
    drk, drv = _ret_bwd_kv(rqk, rvg, dret, cos2, sin2, s)
    dsq, dsk, dsv, *stacked = _sb_bwd(sb_qkv, sb_out_f32, dsb_out, s, comm=_exchange_plan(LATER_WEIGHTS, grads))
    stacks = dict(zip(LATER_WEIGHTS, stacked, strict=True))

    dh_mixers, dh_gates = [dsq, dsk, dsv, drq, drk, drv], [drg, dgate0, dgate1]
    grads["w_in"] = jnp.concatenate(
        [wgrad_wide("g_in_mixers", xb, dh_mixers, d), wgrad_wide("g_in_gates", xb, dh_gates, d)], axis=1)
    grad_x, stacks["w_in"] = _mm(
        "in_t", dh_mixers + dh_gates, w_in, s, d, IN_WIDTH, tm=tm, tn=d, tk=IN_WIDTH, tb=True,
        epi=lambda acc, t, i, j: ([acc + DN_ALPHA * t[0].astype(F32)], []),
        ins=[(du1b, *full)], outs=[(*f32(d), *full)], comm=_exchange_plan(("w_in",), grads))

    small = {"b_gate": jnp.concatenate([dbg0, dbg1], axis=1), "ln1_g": dg1, "ln1_b": db1, "ln2_g": dg2,
             "ln2_b": db2, "ln3_g": dg3, "ln3_b": db3}
    return grad_x, stacks, small, loss_cols


def kernel(x, mem, w_in, b_gate, w_sb_o, w_ret_o, w_mix_o, ln1_g, ln1_b, w_mem_q, w_mem_kv, w_mem_o, ln2_g, ln2_b, w_ffn_in, w_ffn_out, ln3_g, ln3_b, loss_target, m_w_in, m_b_gate, m_w_sb_o, m_w_ret_o, m_w_mix_o, m_ln1_g, m_ln1_b, m_w_mem_q, m_w_mem_kv, m_w_mem_o, m_ln2_g, m_ln2_b, m_w_ffn_in, m_w_ffn_out, m_ln3_g, m_ln3_b, v_w_in, v_b_gate, v_w_sb_o, v_w_ret_o, v_w_mix_o, v_ln1_g, v_ln1_b, v_w_mem_q, v_w_mem_kv, v_w_mem_o, v_ln2_g, v_ln2_b, v_w_ffn_in, v_w_ffn_out, v_ln3_g, v_ln3_b):
    given = dict(locals())
    s = x.shape[1]
    x2d = x.reshape(s, D_MODEL)
    tgt = loss_target.reshape(s, D_MODEL)
    mem2d = mem.reshape(MEM_LEN, D_MODEL)
    shard = {name: given[name].reshape(_shard_shape(shape, axis)) for name, shape, axis in BIG}
    vec = {name: given[name] for name in SMALL}

    shards_bf = {name: _cast_bf16("cast_" + name, shard[name]) for name, _, _ in BIG}

    grad_x, stacks, small, loss_cols = _layer_step(x2d, mem2d, tgt, shards_bf, vec)

    out = {}
    for name, shape, axis in BIG:
        stack = stacks[name]
        shp = given[name].shape
        res = _reduce_adamw("adamw_" + name, stack, shard[name], given["m_" + name].reshape(stack.shape[1:]),
                            given["v_" + name].reshape(stack.shape[1:]))
        out[name] = [r.reshape(shp) for r in res]

    pack = jnp.concatenate([small[name] for name in SMALL] + [loss_cols], axis=1).reshape(PACK_ROWS, LANES)
    cat = lambda pre: jnp.concatenate([given[pre + name] for name in SMALL], axis=1).reshape(SMALL_ROWS, LANES)
    *res, loss = _small_step(pack, cat(""), cat("m_"), cat("v_"))
    flat = [r.reshape(1, SMALL_LEN) for r in res]
    off = 0
    for name in SMALL:
        n = given[name].shape[1]
        out[name] = [r[:, off:off + n] for r in flat]
        off += n

    return (loss.reshape(()), grad_x.reshape(x.shape),
            *[out[name][0] for name in WEIGHT_ORDER], *[out[name][1] for name in WEIGHT_ORDER],
            *[out[name][2] for name in WEIGHT_ORDER], *[out[name][3] for name in WEIGHT_ORDER])
```

```python
import functools

import jax
import jax.numpy as jnp
import numpy as np
from jax import lax
from jax.experimental import pallas as pl
from jax.experimental.pallas import tpu as pltpu

F32, BF16 = jnp.float32, jnp.bfloat16
MESH = pl.DeviceIdType.MESH

D_MODEL = 1024
MEM_LEN = 256
SB_HEADS, SB_DIM, SB_WIDTH = 8, 64, 512
RET_HEADS, RET_QK, RET_V = 4, 128, 256
RET_QK_WIDTH, RET_V_WIDTH = 512, 1024
ROPE_BASE = 10000.0
MEM_HEADS, MEM_DIM = 4, 256
FFN_HIDDEN = 2816
IN_WIDTH = 6656
OFF_RET_Q, OFF_RET_V, OFF_RET_G, OFF_GATE = 1536, 2560, 3584, 4608
DN_ALPHA = 2.0 ** 0.25
LN_EPS = 1e-5
SB_SCALE = SB_DIM ** -0.5
SB_DEAD = -110.0
RET_SCALE = RET_QK ** -0.5
MEM_SCALE = MEM_DIM ** -0.5
ADAM_LR, ADAM_B1, ADAM_B2, ADAM_EPS, ADAM_WD, ADAM_STEP = 0.001, 0.9, 0.999, 1e-08, 0.01, 10

N_DEV, N_CHIPS = 8, 4

LANES = 128
MXU_COLS = 256
VMEM_LIMIT_BYTES = 52 * 2 ** 20
ROW_TILE = 512
WIDE_TILE = 1024
SEQ_TILE = 2048
SB_BLOCK = 256
RET_BLOCK = 256
RET_CHUNKS_PER_STEP = 4
XATTN_ROWS = 1024

BIG = (
    ("w_in", (D_MODEL, IN_WIDTH), 1),
    ("w_sb_o", (SB_WIDTH, D_MODEL), 1),
    ("w_ret_o", (RET_V_WIDTH, D_MODEL), 0),
    ("w_mix_o", (D_MODEL, D_MODEL), 0),
    ("w_mem_q", (D_MODEL, D_MODEL), 0),
    ("w_mem_kv", (D_MODEL, 2 * D_MODEL), 1),
    ("w_mem_o", (D_MODEL, D_MODEL), 0),
    ("w_ffn_in", (D_MODEL, 2 * FFN_HIDDEN), 1),
    ("w_ffn_out", (FFN_HIDDEN, D_MODEL), 0),
)
SMALL = ("b_gate", "ln1_g", "ln1_b", "ln2_g", "ln2_b", "ln3_g", "ln3_b")
SMALL_LEN = 2 * D_MODEL + 6 * D_MODEL
SMALL_ROWS = SMALL_LEN // LANES
PACK_ROWS = SMALL_ROWS + D_MODEL // LANES
WEIGHT_ORDER = ("w_in", "b_gate", "w_sb_o", "w_ret_o", "w_mix_o", "ln1_g", "ln1_b", "w_mem_q", "w_mem_kv",
                "w_mem_o", "ln2_g", "ln2_b", "w_ffn_in", "w_ffn_out", "ln3_g", "ln3_b")


def _cparams():
    return pltpu.CompilerParams(vmem_limit_bytes=VMEM_LIMIT_BYTES)


def _dot(a, b, ca, cb):
    return lax.dot_general(a, b, (((ca,), (cb,)), ((), ())), preferred_element_type=F32)


def _sigmoid(x):
    return 1.0 / (1.0 + jnp.exp(-x))


def _mm(name, a, b, m, n, k, *, tm, tn, tk, epi, outs, ins=(), accs=(), ta=False, tb=False,
        a_off=(0, 0), b_off=(0, 0), j_outer=False, comm=None, chunk=None, halves=False):
    assert not halves or (chunk is not None and (tn // 2) % chunk == 0 and not ins), name
    assert m % tm == 0 and n % tn == 0 and k % tk == 0, (name, m, n, k, tm, tn, tk)
    assert chunk is None or (k == tk and tn % chunk == 0), name
    ni, nj, nk = m // tm, n // tn, k // tk
    assert not accs or nj == 1, name
    ij = (lambda g0, g1: (g1, g0)) if j_outer else (lambda g0, g1: (g0, g1))

    def spec(block, index):
        return pl.BlockSpec(block, lambda g0, g1, kk: index(*ij(g0, g1), kk))

    a_list = list(a) if isinstance(a, (list, tuple)) else [a]
    n_a = len(a_list)
    if n_a > 1:
        assert not ta and nk == 1 and chunk is None and not any(a_off), name
        assert sum(p.shape[1] for p in a_list) == k, name
        a_specs = [spec((tm, p.shape[1]), lambda i, j, kk: (i, 0)) for p in a_list]
    elif ta:
        a_specs = [spec((tk, tm), lambda i, j, kk: (kk + a_off[0], i + a_off[1]))]
    else:
        a_specs = [spec((tm, tk), lambda i, j, kk: (i + a_off[0], kk + a_off[1]))]
    b_list = list(b) if isinstance(b, (list, tuple)) else [b]
    n_b = len(b_list)
    if n_b > 1:
        assert not tb and nj == 1 and nk > 1 and n_a == 1 and chunk is None and not any(b_off), name
        assert sum(p.shape[1] for p in b_list) == n, name
        b_specs = [spec((tk, p.shape[1]), lambda i, j, kk: (kk, 0)) for p in b_list]
    elif tb:
        b_specs = [spec((tn, tk), lambda i, j, kk: (j + b_off[0], kk + b_off[1]))]
    else:
        b_specs = [spec((tk, tn), lambda i, j, kk: (kk + b_off[0], j + b_off[1]))]
    if n_a > 1 and nj == 1:
        b_specs = [pl.BlockSpec(b_specs[0].block_shape, b_specs[0].index_map, pipeline_mode=pl.Buffered(1))]
    in_specs = [*a_specs, *b_specs]
    for _, bs, im in ins:
        in_specs.append(spec(bs, lambda i, j, kk, im=im: im(i, j)))
    out_specs, out_shape = [], []
    for shape, dtype, bs, im in outs:
        out_specs.append(spec(bs, lambda i, j, kk, im=im: im(i, j)))
        out_shape.append(jax.ShapeDtypeStruct(shape, dtype))
    for shape, dtype in accs:
        out_specs.append(spec(shape, lambda i, j, kk, nd=len(shape): (0,) * nd))
        out_shape.append(jax.ShapeDtypeStruct(shape, dtype))
    n_in, n_out, n_acc = len(ins), len(outs), len(accs)
    ca, cb = (0 if ta else 1), (1 if tb else 0)
    grid = (*ij(ni, nj), nk)
    comm_ins, comm_outs, comm_scratch = [], [], []
    if comm is not None:
        comm_in_specs, comm_out_specs = comm.specs
        comm_ins, comm_outs, comm_scratch = list(comm.ins), list(comm.out_shape), list(comm.scratch)
        in_specs += comm_in_specs
        out_specs += comm_out_specs
        out_shape += comm_outs
    n_ci, n_co = len(comm_ins), len(comm_outs)

    def body(*refs):
        a_refs, b_refs, refs = refs[:n_a], refs[n_a:n_a + n_b], refs[n_a + n_b:]
        a_ref, b_ref = a_refs[0], b_refs[0]
        in_refs = refs[:n_in]
        ci_refs = refs[n_in:n_in + n_ci]
        rest = refs[n_in + n_ci:]
        out_refs, acc_refs = rest[:n_out], rest[n_out:n_out + n_acc]
        co_refs = rest[n_out + n_acc:n_out + n_acc + n_co]
        scratch = rest[n_out + n_acc + n_co:]
        sem_refs, scratch = scratch[:len(comm_scratch)], scratch[len(comm_scratch):]
        (i, j), kk = ij(pl.program_id(0), pl.program_id(1)), pl.program_id(2)
        if comm is not None:
            first_step, last_step = _grid_ends(grid)
            pl.when(first_step)(lambda: comm.start(ci_refs, co_refs, sem_refs))
        def finish(acc, cols=slice(None)):
            def of(r):
                return r[..., cols] if r.shape[-1] == tn else r[...]

            o_tiles, a_tiles = epi(acc, [of(r) for r in in_refs], i, j)
            for r, t in zip(out_refs, o_tiles, strict=True):
                r[..., cols] = t.astype(r.dtype)
            if n_acc:
                @pl.when(i == 0)
                def _():
                    for r, t in zip(acc_refs, a_tiles, strict=True):
                        r[..., cols] = t

                @pl.when(i > 0)
                def _():
                    for r, t in zip(acc_refs, a_tiles, strict=True):
                        r[..., cols] += t

        if chunk is not None:
            a_tile = a_ref[...].astype(BF16)

            def product(c0):
                b_part = b_ref[c0:c0 + chunk, :] if tb else b_ref[:, c0:c0 + chunk]
                return _dot(a_tile, b_part.astype(BF16), ca, cb)

            for c0 in range(0, tn // 2 if halves else tn, chunk):
                acc = (product(c0), product(tn // 2 + c0)) if halves else product(c0)
                finish(acc, slice(c0, c0 + chunk))
            if comm is not None:
                pl.when(last_step)(lambda: comm.finish(ci_refs, co_refs, sem_refs))
            return

        if n_b > 1:
            acc_ref = scratch[0]

            def accumulate(first):
                a_tile, c0 = a_ref[...].astype(BF16), 0
                for r in b_refs:
                    c1 = c0 + r.shape[1]
                    term = _dot(a_tile, r[...].astype(BF16), ca, cb)
                    acc_ref[:, c0:c1] = term if first else acc_ref[:, c0:c1] + term
                    c0 = c1

            pl.when(kk == 0)(lambda: accumulate(True))
            pl.when(kk > 0)(lambda: accumulate(False))
            pl.when(kk == nk - 1)(lambda: finish(acc_ref[...]))
            if comm is not None:
                pl.when(last_step)(lambda: comm.finish(ci_refs, co_refs, sem_refs))
            return

        if n_a > 1:
            part, c0 = None, 0
            for r in a_refs:
                c1 = c0 + r.shape[1]
                b_part = b_ref[:, c0:c1] if tb else b_ref[c0:c1, :]
                term = _dot(r[...].astype(BF16), b_part.astype(BF16), ca, cb)
                part, c0 = (term if part is None else part + term), c1
        else:
            part = _dot(a_ref[...].astype(BF16), b_ref[...].astype(BF16), ca, cb)
        if nk == 1:
            finish(part)
        else:
            acc_ref = scratch[0]

            @pl.when(kk == 0)
            def _():
                acc_ref[...] = part

            @pl.when(kk > 0)
            def _():
                acc_ref[...] += part

            @pl.when(kk == nk - 1)
            def _():
                finish(acc_ref[...])

        if comm is not None:
            pl.when(last_step)(lambda: comm.finish(ci_refs, co_refs, sem_refs))

    res = pl.pallas_call(
        body, name=name, grid=grid, in_specs=in_specs, out_specs=out_specs, out_shape=out_shape,
        scratch_shapes=comm_scratch + ([pltpu.VMEM((tm, tn), F32)] if nk > 1 else []),
        compiler_params=_cparams(),
    )(*a_list, *b_list, *[x for x, _, _ in ins], *comm_ins)
    return res


def _grid_ends(grid):
    ids = [pl.program_id(ax) for ax in range(len(grid))]
    first = functools.reduce(jnp.logical_and, [p == 0 for p in ids])
    last = functools.reduce(jnp.logical_and, [p == n - 1 for p, n in zip(ids, grid, strict=True)])
    return first, last


def _tile(tm, tn, dj=0):
    return (tm, tn), (lambda i, j: (i, j + dj))


def _rowvec(tn, dj=0):
    return (1, tn), (lambda i, j: (0, j + dj))


def _plain(acc, tiles, i, j):
    return [acc], []


def _ew(name, fn, ins, outs, rows, tr):
    assert rows % tr == 0, (name, rows, tr)
    in_specs = []
    for x in ins:
        if x.shape[0] == rows:
            in_specs.append(pl.BlockSpec((tr, x.shape[1]), lambda i: (i, 0)))
        else:
            in_specs.append(pl.BlockSpec(x.shape, lambda i: (0, 0)))
    n_in = len(ins)

    def body(*refs):
        res = fn(*[r[...] for r in refs[:n_in]])
        for r, t in zip(refs[n_in:], res, strict=True):
            r[...] = t.astype(r.dtype)

    return pl.pallas_call(
        body, name=name, grid=(rows // tr,), in_specs=in_specs,
        out_specs=[pl.BlockSpec((tr, w), lambda i: (i, 0)) for w, _ in outs],
        out_shape=[jax.ShapeDtypeStruct((rows, w), dt) for w, dt in outs],
        compiler_params=_cparams(),
    )(*ins)


def _cast_bf16(name, x):
    rows = x.shape[0]
    tr = next(t for t in (512, 256, 64) if rows % t == 0)
    return _ew(name, lambda v: (v,), [x], [(x.shape[1], BF16)], rows, tr)[0]


def _prep(x, comm):
    s = x.shape[0]
    half = RET_QK // 2
    inv = 1.0 / (ROPE_BASE ** (jnp.arange(half, dtype=F32) / half))
    inv2 = jnp.concatenate([inv, inv]).reshape(1, RET_QK)
    sign = jnp.concatenate([-jnp.ones((half,), F32), jnp.ones((half,), F32)]).reshape(1, RET_QK)
    tr = min(ROW_TILE, s)
    grid = (s // tr,)
    c_in_specs, c_out_specs, c_out_shape, c_scratch, c_ins, split = _host(comm, 3, 3)

    def body(*refs):
        (x_ref, inv_ref, sign_ref), (xb_ref, cos_ref, sin_ref), _, riding = split(refs)
        i = pl.program_id(0)
        first_step, last_step = _grid_ends(grid)
        pl.when(first_step)(lambda: comm.start(*riding))
        xb_ref[...] = x_ref[...].astype(BF16)
        pos = (lax.broadcasted_iota(jnp.int32, (tr, RET_QK), 0) + i * tr).astype(F32)
        ang = pos * inv_ref[...]
        cos_ref[...] = jnp.cos(ang)
        sin_ref[...] = jnp.sin(ang) * sign_ref[...]
        pl.when(last_step)(lambda: comm.finish(*riding))

    vec = pl.BlockSpec((1, RET_QK), lambda i: (0, 0))
    row = lambda w: pl.BlockSpec((tr, w), lambda i: (i, 0))
    return pl.pallas_call(
        body, name="prep", grid=grid,
        in_specs=[row(D_MODEL), vec, vec] + c_in_specs,
        out_specs=[row(D_MODEL), row(RET_QK), row(RET_QK)] + c_out_specs,
        out_shape=[jax.ShapeDtypeStruct((s, D_MODEL), BF16), jax.ShapeDtypeStruct((s, RET_QK), F32),
                   jax.ShapeDtypeStruct((s, RET_QK), F32)] + c_out_shape,
        scratch_shapes=c_scratch, compiler_params=_cparams(),
    )(x, inv2, sign, *c_ins)


def _swap_halves(x):
    return pltpu.roll(x, RET_QK // 2, 1)


def _norm(u):
    mu = jnp.mean(u, axis=-1, keepdims=True)
    d = u - mu
    var = jnp.mean(d * d, axis=-1, keepdims=True)
    rstd = lax.rsqrt(var + LN_EPS)
    return d * rstd, rstd


def _norm_bwd(dxh, xhat, rstd):
    m1 = jnp.mean(dxh, axis=-1, keepdims=True)
    m2 = jnp.mean(dxh * xhat, axis=-1, keepdims=True)
    return rstd * (dxh - m1 - xhat * m2)


def _colsum(t):
    return jnp.sum(t, axis=0, keepdims=True)


def _split_mm(t, tri):
    hi = t.astype(BF16)
    lo = (t - hi.astype(F32)).astype(BF16)
    return _dot(hi, tri, 1, 0) + _dot(lo, tri, 1, 0)


def _sb_masks():
    t = SB_BLOCK
    lane = lax.broadcasted_iota(jnp.int32, (1, LANES), 1)
    first = lane < SB_DIM
    m0 = jnp.where(first, 1.0, 0.0).astype(BF16)
    m1 = jnp.where(first, 0.0, 1.0).astype(BF16)
    row = lax.broadcasted_iota(jnp.int32, (t, t), 0)
    col = lax.broadcasted_iota(jnp.int32, (t, t), 1)
    return first, (m0, m1), row, col


def _sb_logits(qh, k, causal):
    z = _dot(qh, k, 1, 1)
    lp = jnp.log(1.0 + jnp.exp(-jnp.abs(z)))
    a = jnp.minimum(z, 0.0) - lp
    r = jnp.minimum(-z, 0.0) - lp
    if causal is not None:
        r = jnp.where(causal, r, 0.0)
    return a, r


def _sb_walk(i, blocks, l_ref, causal):
    pl.when(i == 0)(lambda: blocks([(i, causal)]))
    pl.when(i > 0)(lambda: blocks([(i, causal), (i - 1, None)]))

    def alive():
        top = jnp.max(functools.reduce(jnp.maximum, [l_ref[c] for c in range(l_ref.shape[0])]))
        return jnp.where(top > SB_DEAD, 1, 0)

    def cond(c):
        return jnp.logical_and(c[0] < i, c[1] > 0)

    def step(c):
        blocks([(i - 1 - c[0], None)])
        return c[0] + 1, alive()

    lax.while_loop(cond, step, (jnp.int32(1), alive()))


def _host(comm, n_in, n_out):
    if comm is None:
        return [], [], [], [], [], lambda refs: (refs[:n_in], refs[n_in:n_in + n_out], refs[n_in + n_out:], None)
    in_specs, out_specs = comm.specs
    n_ci, n_co, n_sem = len(comm.ins), len(comm.out_shape), len(comm.scratch)

    def split(refs):
        ins, ci = refs[:n_in], refs[n_in:n_in + n_ci]
        rest = refs[n_in + n_ci:]
        outs, co = rest[:n_out], rest[n_out:n_out + n_co]
        sems, scratch = rest[n_out + n_co:n_out + n_co + n_sem], rest[n_out + n_co + n_sem:]
        return ins, outs, scratch, (ci, co, sems)

    return in_specs, out_specs, list(comm.out_shape), list(comm.scratch), list(comm.ins), split


def _sb_qkv_specs(s, g):
    groups = SB_HEADS // 2 // g
    return [pl.BlockSpec((g, SB_BLOCK, LANES), lambda p, i: (p, i, 0)),
            pl.BlockSpec((g, s, LANES), lambda p, i: (groups + p, 0, 0)),
            pl.BlockSpec((g, s, LANES), lambda p, i: (2 * groups + p, 0, 0))]


def _sb_fwd(qkv, s, comm=None):
    t = SB_BLOCK
    g = 2
    nq = s // t
    grid = (SB_HEADS // 2 // g, nq)
    c_in_specs, c_out_specs, c_out_shape, c_scratch, c_ins, split = _host(comm, 3, 2)

    def body(*refs):
        (q_ref, k_ref, v_ref), (o_ref, of_ref), (l_ref, acc_ref), riding = split(refs)
        i = pl.program_id(1)
        if comm is not None:
            first_step, last_step = _grid_ends(grid)
            pl.when(first_step)(lambda: comm.start(*riding))
        first, hmask, row, col = _sb_masks()
        after = jnp.where(row > col, 1.0, 0.0).astype(BF16)
        causal = col < row
        heads = [(p, h) for p in range(g) for h in range(2)]
        qh = {(p, h): q_ref[p] * hmask[h] for p, h in heads}
        l_ref[...] = jnp.zeros_like(l_ref)
        acc_ref[...] = jnp.zeros_like(acc_ref)

        def blocks(todo):
            chains = [(b, p, h) for b in range(len(todo)) for p, h in heads]
            starts = [pl.multiple_of(kb * t, t) for kb, _ in todo]
            ks = {(b, p): k_ref[p, pl.ds(st, t), :] for b, st in enumerate(starts) for p in range(g)}
            vs = {(b, p): v_ref[p, pl.ds(st, t), :] for b, st in enumerate(starts) for p in range(g)}
            ar = {(b, p, h): _sb_logits(qh[p, h], ks[b, p], todo[b][1]) for b, p, h in chains}
            later = {c: _split_mm(ar[c][1], after) for c in chains}
            carry = {(p, h): l_ref[2 * p + h] for p, h in heads}
            w = {}
            for b, (_, mask) in enumerate(todo):
                for p, h in heads:
                    wc = jnp.exp(ar[b, p, h][0] + later[b, p, h] + carry[p, h])
                    w[b, p, h] = wc if mask is None else jnp.where(mask, wc, 0.0)
                carry = {(p, h): carry[p, h] + jnp.sum(ar[b, p, h][1], axis=1, keepdims=True) for p, h in heads}
            pv = {(b, p, h): _dot(w[b, p, h].astype(BF16), vs[b, p], 1, 0) for b, p, h in chains}
            for p in range(g):
                lanes = slice(p * LANES, (p + 1) * LANES)
                acc = acc_ref[:, lanes]
                for b in range(len(todo)):
                    acc = acc + jnp.where(first, pv[b, p, 0], pv[b, p, 1])
                acc_ref[:, lanes] = acc
            for p, h in heads:
                l_ref[2 * p + h] = carry[p, h]

        _sb_walk(i, blocks, l_ref, causal)
        o_ref[...] = acc_ref[...].astype(o_ref.dtype)
        of_ref[...] = acc_ref[...]
        if comm is not None:
            pl.when(last_step)(lambda: comm.finish(*riding))

    blk = pl.BlockSpec((t, g * LANES), lambda p, i: (i, p))
    return pl.pallas_call(
        body, name="sb_fwd", grid=grid,
        in_specs=_sb_qkv_specs(s, g) + c_in_specs,
        out_specs=[blk, blk] + c_out_specs,
        out_shape=[jax.ShapeDtypeStruct((s, SB_WIDTH), BF16), jax.ShapeDtypeStruct((s, SB_WIDTH), F32)] + c_out_shape,
        scratch_shapes=c_scratch + [pltpu.VMEM((2 * g, t, 1), F32), pltpu.VMEM((t, g * LANES), F32)],
        compiler_params=_cparams(),
    )(qkv, qkv, qkv, *c_ins)


def _sb_bwd(qkv, o, do, s, comm=None):
    t = SB_BLOCK
    g = 2
    nq = s // t
    grid = (SB_HEADS // 2 // g, nq)
    c_in_specs, c_out_specs, c_out_shape, c_scratch, c_ins, split = _host(comm, 5, 3)

    def body(*refs):
        ((q_ref, k_ref, v_ref, o_ref, do_ref), (dq_ref, dk_ref, dv_ref),
         (l_ref, e_ref, dq_acc, dk_acc, dv_acc), riding) = split(refs)
        i = pl.program_id(1)
        if comm is not None:
            first_step, last_step = _grid_ends(grid)
            pl.when(first_step)(lambda: comm.start(*riding))
        first, hmask, row, col = _sb_masks()
        after = jnp.where(row > col, 1.0, 0.0).astype(BF16)
        from_here = jnp.where(row >= col, 1.0, 0.0).astype(BF16)
        causal = col < row

        @pl.when(i == 0)
        def _():
            dk_acc[...] = jnp.zeros_like(dk_acc)
            dv_acc[...] = jnp.zeros_like(dv_acc)

        heads = [(p, h) for p in range(g) for h in range(2)]
        lanes = [slice(p * LANES, (p + 1) * LANES) for p in range(g)]
        q = [q_ref[p] for p in range(g)]
        do_ = [do_ref[:, lanes[p]] for p in range(g)]
        qh = {(p, h): q[p] * hmask[h] for p, h in heads}
        doh = {(p, h): do_[p] * hmask[h] for p, h in heads}
        total = {}
        for p in range(g):
            prod = do_[p].astype(F32) * o_ref[:, lanes[p]]
            total[p, 0] = jnp.sum(jnp.where(first, prod, 0.0), axis=1, keepdims=True)
            total[p, 1] = jnp.sum(jnp.where(first, 0.0, prod), axis=1, keepdims=True)
        l_ref[...] = jnp.zeros_like(l_ref)
        e_ref[...] = jnp.zeros_like(e_ref)
        dq_acc[...] = jnp.zeros_like(dq_acc)

        def blocks(todo):
            chains = [(b, p, h) for b in range(len(todo)) for p, h in heads]
            starts = [pl.multiple_of(kb * t, t) for kb, _ in todo]
            ks = {(b, p): k_ref[p, pl.ds(st, t), :] for b, st in enumerate(starts) for p in range(g)}
            vs = {(b, p): v_ref[p, pl.ds(st, t), :] for b, st in enumerate(starts) for p in range(g)}
            ar = {(b, p, h): _sb_logits(qh[p, h], ks[b, p], todo[b][1]) for b, p, h in chains}
            dw = {(b, p, h): _dot(doh[p, h], vs[b, p], 1, 1) for b, p, h in chains}
            later = {c: _split_mm(ar[c][1], after) for c in chains}
            carry = {(p, h): l_ref[2 * p + h] for p, h in heads}
            wb = {}
            for b, (_, mask) in enumerate(todo):
                for p, h in heads:
                    wc = jnp.exp(ar[b, p, h][0] + later[b, p, h] + carry[p, h])
                    wb[b, p, h] = (wc if mask is None else jnp.where(mask, wc, 0.0)).astype(BF16)
                carry = {(p, h): carry[p, h] + jnp.sum(ar[b, p, h][1], axis=1, keepdims=True) for p, h in heads}
            dvs = {(b, p, h): _dot(wb[b, p, h], do_[p], 0, 0) for b, p, h in chains}
            e = {c: dw[c] * wb[c].astype(F32) for c in chains}
            suffix = {c: _split_mm(e[c], from_here) for c in chains}
            e_carry = {(p, h): e_ref[2 * p + h] for p, h in heads}
            dz = {}
            for b, (_, mask) in enumerate(todo):
                for p, h in heads:
                    before = total[p, h] - (suffix[b, p, h] + e_carry[p, h])
                    dzc = e[b, p, h] - jnp.exp(ar[b, p, h][0]) * (e[b, p, h] + before)
                    dz[b, p, h] = (dzc if mask is None else jnp.where(mask, dzc, 0.0)).astype(BF16)
                e_carry = {(p, h): e_carry[p, h] + jnp.sum(e[b, p, h], axis=1, keepdims=True) for p, h in heads}
            dqs = {(b, p, h): _dot(dz[b, p, h], ks[b, p], 1, 0) for b, p, h in chains}
            dks = {(b, p, h): _dot(dz[b, p, h], q[p], 0, 0) for b, p, h in chains}
            for p in range(g):
                dq = dq_acc[:, lanes[p]]
                for b, st in enumerate(starts):
                    dq = dq + jnp.where(first, dqs[b, p, 0], dqs[b, p, 1])
                    dk_acc[pl.ds(st, t), lanes[p]] += jnp.where(first, dks[b, p, 0], dks[b, p, 1])
                    dv_acc[pl.ds(st, t), lanes[p]] += jnp.where(first, dvs[b, p, 0], dvs[b, p, 1])
                dq_acc[:, lanes[p]] = dq
            for p, h in heads:
                l_ref[2 * p + h] = carry[p, h]
                e_ref[2 * p + h] = e_carry[p, h]

        _sb_walk(i, blocks, l_ref, causal)
        dq_ref[...] = (dq_acc[...] * SB_SCALE).astype(dq_ref.dtype)

        @pl.when(i == nq - 1)
        def _():
            dk_ref[...] = dk_acc[...].astype(dk_ref.dtype)
            dv_ref[...] = dv_acc[...].astype(dv_ref.dtype)

        if comm is not None:
            pl.when(last_step)(lambda: comm.finish(*riding))

    once = pl.Buffered(1)
    q_spec, k_spec, v_spec = _sb_qkv_specs(s, g)
    k_spec = pl.BlockSpec(k_spec.block_shape, k_spec.index_map, pipeline_mode=once)
    v_spec = pl.BlockSpec(v_spec.block_shape, v_spec.index_map, pipeline_mode=once)
    blk = pl.BlockSpec((t, g * LANES), lambda p, i: (i, p))
    col_blk = pl.BlockSpec((s, g * LANES), lambda p, i: (0, p), pipeline_mode=once)
    sds = jax.ShapeDtypeStruct((s, SB_WIDTH), BF16)
    return pl.pallas_call(
        body, name="sb_bwd", grid=grid,
        in_specs=[q_spec, k_spec, v_spec, blk, blk] + c_in_specs,
        out_specs=[blk, col_blk, col_blk] + c_out_specs,
        out_shape=[sds, sds, sds] + c_out_shape,
        scratch_shapes=c_scratch + [pltpu.VMEM((2 * g, t, 1), F32), pltpu.VMEM((2 * g, t, 1), F32),
                                    pltpu.VMEM((t, g * LANES), F32), pltpu.VMEM((s, g * LANES), F32),
                                    pltpu.VMEM((s, g * LANES), F32)],
        compiler_params=_cparams(),
    )(qkv, qkv, qkv, o, do, *c_ins)


def _ret_log_gamma():
    lg = np.log1p(-np.exp2(-5.0 - np.arange(RET_HEADS, dtype=np.float32))).astype(np.float32)
    return jnp.asarray(np.broadcast_to(lg[:, None, None], (RET_HEADS, 8, LANES)).copy())


RET_SCRATCH = [pltpu.VMEM((RET_HEADS, RET_QK, RET_V), F32),
               pltpu.VMEM((RET_HEADS, RET_BLOCK, RET_BLOCK), F32),
               pltpu.VMEM((RET_HEADS, RET_BLOCK, 1), F32),
               pltpu.VMEM((RET_HEADS, RET_BLOCK, 1), F32)]


def _ret_begin(n, lg_ref, state, within, q_dec, k_dec):
    @pl.when(n == 0)
    def _():
        c = RET_BLOCK
        state[...] = jnp.zeros_like(state)
        row = lax.broadcasted_iota(jnp.int32, (c, c), 0)
        col = lax.broadcasted_iota(jnp.int32, (c, c), 1)
        rel = jnp.maximum(row - col, 0).astype(F32)
        idx = lax.broadcasted_iota(jnp.int32, (c, 1), 0).astype(F32)
        for h in range(RET_HEADS):
            lg = lg_ref[h, 0:1, 0:1]
            within[h] = jnp.where(row >= col, jnp.exp(lg * rel), 0.0)
            q_dec[h] = jnp.exp(lg * (idx + 1.0))
            k_dec[h] = jnp.exp(lg * (c - 1.0 - idx))


def _chunk_decay(lg_ref, h):
    return jnp.exp(lg_ref[h, 0:1, 0:1] * float(RET_BLOCK))


def _ret_heads(x, width):
    return [x[:, h * width:(h + 1) * width] for h in range(RET_HEADS)]


def _ret_specs(s, reverse=False):
    c = RET_BLOCK
    per_step = min(RET_CHUNKS_PER_STEP, s // c)
    rows = c * per_step
    nc = s // rows
    pos = (lambda n: nc - 1 - n) if reverse else (lambda n: n)
    chunks = [slice(u * c, (u + 1) * c) for u in range(per_step)]
    q_spec = pl.BlockSpec((rows, RET_QK_WIDTH), lambda n: (pos(n), 0))
    k_spec = pl.BlockSpec((rows, RET_QK_WIDTH), lambda n: (pos(n), 1))
    v_spec = pl.BlockSpec((rows, RET_V_WIDTH), lambda n: (pos(n), 0))
    lg_spec = pl.BlockSpec((RET_HEADS, 8, LANES), lambda n: (0, 0, 0))
    rope_spec = pl.BlockSpec((rows, RET_QK), lambda n: (pos(n), 0))
    return nc, chunks[::-1] if reverse else chunks, q_spec, k_spec, v_spec, lg_spec, rope_spec


def _ret_fwd(rqk, rvg, s):
    nc, chunks, q_spec, k_spec, v_spec, lg_spec, _ = _ret_specs(s)
    g_spec = pl.BlockSpec(v_spec.block_shape, lambda n: (n, 1))
    heads = range(RET_HEADS)

    def body(q_ref, k_ref, v_ref, g_ref, lg_ref, r_ref, y_ref, state, within, q_dec, k_dec):
        n = pl.program_id(0)
        _ret_begin(n, lg_ref, state, within, q_dec, k_dec)
        for rows in chunks:
            q, k = _ret_heads(q_ref[rows], RET_QK), _ret_heads(k_ref[rows], RET_QK)
            v, g = _ret_heads(v_ref[rows], RET_V), _ret_heads(g_ref[rows], RET_V)
            scores = [_dot(q[h].astype(BF16), k[h].astype(BF16), 1, 1) * within[h] for h in heads]
            cross = [_dot((q[h] * q_dec[h]).astype(BF16), state[h].astype(BF16), 1, 0) for h in heads]
            out = [_dot(scores[h].astype(BF16), v[h], 1, 0) + cross[h] for h in heads]
            grown = [_dot((k[h] * k_dec[h]).astype(BF16), v[h], 0, 0) for h in heads]
            for h in heads:
                sl = slice(h * RET_V, (h + 1) * RET_V)
                r_ref[rows, sl] = out[h]
                xhat, _ = _norm(out[h])
                gh = g[h].astype(F32)
                y_ref[rows, sl] = (gh * _sigmoid(gh) * xhat).astype(y_ref.dtype)
                state[h] = state[h] * _chunk_decay(lg_ref, h) + grown[h]

    return pl.pallas_call(
        body, name="ret_fwd", grid=(nc,),
        in_specs=[q_spec, k_spec, v_spec, g_spec, lg_spec],
        out_specs=[v_spec, v_spec],
        out_shape=[jax.ShapeDtypeStruct((s, RET_V_WIDTH), F32), jax.ShapeDtypeStruct((s, RET_V_WIDTH), BF16)],
        scratch_shapes=RET_SCRATCH,
        compiler_params=_cparams(),
    )(rqk, rqk, rvg, rvg, _ret_log_gamma())


def _rope_bwd(d, cos, sin):
    return d * cos + _swap_halves(d * sin)


def _ret_bwd_q(rqk, rv, d_out, cos2, sin2, s):
    nc, chunks, q_spec, k_spec, v_spec, lg_spec, rope_spec = _ret_specs(s)
    heads = range(RET_HEADS)

    def body(k_ref, v_ref, d_ref, lg_ref, cos_ref, sin_ref, dq_ref, state, within, q_dec, k_dec):
        n = pl.program_id(0)
        _ret_begin(n, lg_ref, state, within, q_dec, k_dec)
        for rows in chunks:
            k = _ret_heads(k_ref[rows], RET_QK)
            v, d = _ret_heads(v_ref[rows], RET_V), _ret_heads(d_ref[rows], RET_V)
            cos, sin = cos_ref[rows], sin_ref[rows]
            d_scores = [_dot(d[h], v[h], 1, 1) * within[h] for h in heads]
            cross = [q_dec[h] * _dot(d[h], state[h].astype(BF16), 1, 1) for h in heads]
            dq = [_dot(d_scores[h].astype(BF16), k[h].astype(BF16), 1, 0) + cross[h] for h in heads]
            grown = [_dot((k[h] * k_dec[h]).astype(BF16), v[h], 0, 0) for h in heads]
            for h in heads:
                sl = slice(h * RET_QK, (h + 1) * RET_QK)
                dq_ref[rows, sl] = (_rope_bwd(dq[h], cos, sin) * RET_SCALE).astype(dq_ref.dtype)
                state[h] = state[h] * _chunk_decay(lg_ref, h) + grown[h]

    return pl.pallas_call(
        body, name="ret_bwd_q", grid=(nc,),
        in_specs=[k_spec, v_spec, v_spec, lg_spec, rope_spec, rope_spec],
        out_specs=q_spec,
        out_shape=jax.ShapeDtypeStruct((s, RET_QK_WIDTH), BF16),
        scratch_shapes=RET_SCRATCH,
        compiler_params=_cparams(),
    )(rqk, rv, d_out, _ret_log_gamma(), cos2, sin2)


def _ret_bwd_kv(rqk, rv, d_out, cos2, sin2, s):
    nc, chunks, q_spec, k_spec, v_spec, lg_spec, rope_spec = _ret_specs(s, reverse=True)
    heads = range(RET_HEADS)

    def body(q_ref, k_ref, v_ref, d_ref, lg_ref, cos_ref, sin_ref, dk_ref, dv_ref, state, within, q_dec, k_dec):
        n = pl.program_id(0)
        _ret_begin(n, lg_ref, state, within, q_dec, k_dec)
        for rows in chunks:
            q, k = _ret_heads(q_ref[rows], RET_QK), _ret_heads(k_ref[rows], RET_QK)
            v, d = _ret_heads(v_ref[rows], RET_V), _ret_heads(d_ref[rows], RET_V)
            cos, sin = cos_ref[rows], sin_ref[rows]
            qb, kb = [q[h].astype(BF16) for h in heads], [k[h].astype(BF16) for h in heads]
            st = [state[h].astype(BF16) for h in heads]
            scores = [_dot(qb[h], kb[h], 1, 1) * within[h] for h in heads]
            d_scores = [_dot(d[h], v[h], 1, 1) * within[h] for h in heads]
            dk = [_dot(d_scores[h].astype(BF16), qb[h], 0, 0) + k_dec[h] * _dot(v[h], st[h], 1, 1) for h in heads]
            dv = [_dot(scores[h].astype(BF16), d[h], 0, 0) + k_dec[h] * _dot(kb[h], st[h], 1, 0) for h in heads]
            grown = [_dot((q[h] * q_dec[h]).astype(BF16), d[h], 0, 0) for h in heads]
            for h in heads:
                dk_ref[rows, h * RET_QK:(h + 1) * RET_QK] = _rope_bwd(dk[h], cos, sin).astype(dk_ref.dtype)
                dv_ref[rows, h * RET_V:(h + 1) * RET_V] = dv[h].astype(dv_ref.dtype)
                state[h] = state[h] * _chunk_decay(lg_ref, h) + grown[h]

    return pl.pallas_call(
        body, name="ret_bwd_kv", grid=(nc,),
        in_specs=[q_spec, k_spec, v_spec, v_spec, lg_spec, rope_spec, rope_spec],
        out_specs=[q_spec, v_spec],
        out_shape=[jax.ShapeDtypeStruct((s, RET_QK_WIDTH), BF16), jax.ShapeDtypeStruct((s, RET_V_WIDTH), BF16)],
        scratch_shapes=RET_SCRATCH,
        compiler_params=_cparams(),
    )(rqk, rqk, rv, d_out, _ret_log_gamma(), cos2, sin2)


def _xattn_probs(scores):
    sc = scores - jnp.max(scores, axis=-1, keepdims=True)
    p = jnp.exp(sc)
    return p / jnp.sum(p, axis=-1, keepdims=True)


def _xattn_heads(q_ref, kv_ref):
    sls = [slice(h * MEM_DIM, (h + 1) * MEM_DIM) for h in range(MEM_HEADS)]
    q = [q_ref[:, sl] for sl in sls]
    k = [kv_ref[:, sl] for sl in sls]
    v = [kv_ref[:, D_MODEL + h * MEM_DIM:D_MODEL + (h + 1) * MEM_DIM] for h in range(MEM_HEADS)]
    return sls, q, k, v


def _xattn_fwd(qm, kv, s):
    tq = min(XATTN_ROWS, s)
    heads = range(MEM_HEADS)

    def body(q_ref, kv_ref, o_ref):
        sls, q, k, v = _xattn_heads(q_ref, kv_ref)
        scores = [_dot(q[h], k[h], 1, 1) for h in heads]
        p = [_xattn_probs(scores[h]).astype(BF16) for h in heads]
        out = [_dot(p[h], v[h], 1, 0) for h in heads]
        for h in heads:
            o_ref[:, sls[h]] = out[h].astype(o_ref.dtype)

    return pl.pallas_call(
        body, name="xattn_fwd", grid=(s // tq,),
        in_specs=[pl.BlockSpec((tq, D_MODEL), lambda i: (i, 0)),
                  pl.BlockSpec((MEM_LEN, 2 * D_MODEL), lambda i: (0, 0))],
        out_specs=pl.BlockSpec((tq, D_MODEL), lambda i: (i, 0)),
        out_shape=jax.ShapeDtypeStruct((s, D_MODEL), BF16),
        compiler_params=_cparams(),
    )(qm, kv)


def _xattn_bwd(qm, kv, do, s):
    tq = min(XATTN_ROWS, s)

    def body(q_ref, kv_ref, do_ref, dq_ref, dkv_ref):
        i = pl.program_id(0)

        @pl.when(i == 0)
        def _():
            dkv_ref[...] = jnp.zeros_like(dkv_ref)

        heads = range(MEM_HEADS)
        sls, q, k, v = _xattn_heads(q_ref, kv_ref)
        d = [do_ref[:, sl] for sl in sls]
        scores = [_dot(q[h], k[h], 1, 1) for h in heads]
        dp = [_dot(d[h], v[h], 1, 1) for h in heads]
        p = [_xattn_probs(scores[h]) for h in heads]
        ds = [(p[h] * (dp[h] - jnp.sum(p[h] * dp[h], axis=-1, keepdims=True))).astype(BF16) for h in heads]
        dq = [_dot(ds[h], k[h], 1, 0) for h in heads]
        dk = [_dot(ds[h], q[h], 0, 0) for h in heads]
        dv = [_dot(p[h].astype(BF16), d[h], 0, 0) for h in heads]
        for h in heads:
            dq_ref[:, sls[h]] = (dq[h] * MEM_SCALE).astype(dq_ref.dtype)
            dkv_ref[:, sls[h]] += dk[h]
            dkv_ref[:, D_MODEL + h * MEM_DIM:D_MODEL + (h + 1) * MEM_DIM] += dv[h]

    row_blk = pl.BlockSpec((tq, D_MODEL), lambda i: (i, 0))
    kv_blk = pl.BlockSpec((MEM_LEN, 2 * D_MODEL), lambda i: (0, 0))
    return pl.pallas_call(
        body, name="xattn_bwd", grid=(s // tq,),
        in_specs=[row_blk, kv_blk, row_blk],
        out_specs=[row_blk, kv_blk],
        out_shape=[jax.ShapeDtypeStruct((s, D_MODEL), BF16), jax.ShapeDtypeStruct((MEM_LEN, 2 * D_MODEL), F32)],
        compiler_params=_cparams(),
    )(qm, kv, do)


def _place():
    x, y, c = lax.axis_index("x"), lax.axis_index("y"), lax.axis_index("c")
    others = [(1 - x, y), (x, 1 - y), (1 - x, 1 - y)]
    return x, y, c, others


def _slab(ref, axis, chip, size):
    start = pl.multiple_of(chip * size, LANES if axis == 1 else 16)
    if axis == 0:
        return ref.at[pl.ds(start, size), :]
    return ref.at[:, pl.ds(start, size)]


class _CommPlan:
    def __init__(self, ins, out_shape, scratch, start, finish):
        self.ins, self.out_shape, self.scratch, self.start, self.finish = ins, out_shape, scratch, start, finish

    @property
    def specs(self):
        any_spec = pl.BlockSpec(memory_space=pl.ANY)
        return [any_spec] * len(self.ins), [any_spec] * len(self.out_shape)


def _gather_plan(names, shards):
    spec = {name: (shape, axis) for name, shape, axis in BIG}
    nw = len(names)

    def shard_half(ref, c):
        rows = ref.shape[0] // 2
        return ref.at[pl.ds(pl.multiple_of(c * rows, 16), rows), :]

    def region(ref, w, chip, c):
        shape, axis = spec[names[w]]
        size = shape[axis] // N_CHIPS
        if axis == 0:
            rows = size // 2
            return ref.at[pl.ds(pl.multiple_of(chip * size + c * rows, 16), rows), :]
        rows = shape[0] // 2
        return ref.at[pl.ds(pl.multiple_of(c * rows, 16), rows), pl.ds(pl.multiple_of(chip * size, LANES), size)]

    def ops(shard, full, sems):
        ici_send, ici_recv, d2d_send, d2d_recv, local_sems = sems
        x, y, c, others = _place()
        mine, sibling = 2 * x + y, (x, y, 1 - c)
        local, over_ici, arrived, passed_on, from_sibling = [], [], [], [], []
        for w in range(nw):
            shape, axis = spec[names[w]]
            local.append(pltpu.make_async_copy(shard[w], _slab(full[w], axis, mine, shape[axis] // N_CHIPS),
                                               local_sems.at[w]))
            for t, (qx, qy) in enumerate(others):
                n, theirs = 3 * w + t, 2 * qx + qy
                over_ici.append(pltpu.make_async_remote_copy(
                    src_ref=shard_half(shard[w], c), dst_ref=region(full[w], w, mine, c),
                    send_sem=ici_send.at[n], recv_sem=ici_recv.at[n], device_id=(qx, qy, c), device_id_type=MESH))
                arrived.append(pltpu.make_async_remote_copy(
                    src_ref=shard_half(shard[w], c), dst_ref=region(full[w], w, theirs, c),
                    send_sem=ici_send.at[n], recv_sem=ici_recv.at[n], device_id=(qx, qy, c), device_id_type=MESH))
                passed_on.append(pltpu.make_async_remote_copy(
                    src_ref=region(full[w], w, theirs, c), dst_ref=region(full[w], w, theirs, c),
                    send_sem=d2d_send.at[n], recv_sem=d2d_recv.at[n], device_id=sibling, device_id_type=MESH))
                from_sibling.append(pltpu.make_async_remote_copy(
                    src_ref=region(full[w], w, theirs, c), dst_ref=region(full[w], w, theirs, 1 - c),
                    send_sem=d2d_send.at[n], recv_sem=d2d_recv.at[n], device_id=sibling, device_id_type=MESH))
        return local, over_ici, arrived, passed_on, from_sibling

    def start(shard, full, sems):
        local, over_ici, _, _, _ = ops(shard, full, sems)
        for cp in local + over_ici:
            cp.start()

    def finish(shard, full, sems):
        local, over_ici, arrived, passed_on, from_sibling = ops(shard, full, sems)
        for got, onward in zip(arrived, passed_on, strict=True):
            got.wait_recv()
            onward.start()
        for got in from_sibling:
            got.wait_recv()
        for cp in over_ici + passed_on:
            cp.wait_send()
        for cp in local:
            cp.wait()

    dma = pltpu.SemaphoreType.DMA
    return _CommPlan(
        ins=[shards[name] for name in names],
        out_shape=[jax.ShapeDtypeStruct(spec[name][0], BF16) for name in names],
        scratch=[dma((3 * nw,)), dma((3 * nw,)), dma((3 * nw,)), dma((3 * nw,)), dma((nw,))],
        start=start, finish=finish)


def _shard_shape(shape, axis):
    return tuple(d // N_CHIPS if a == axis else d for a, d in enumerate(shape))


def _exchange_plan(names, grads):
    spec = {name: (shape, axis) for name, shape, axis in BIG}
    nw = len(names)

    def ops(grad, stack, sems):
        send_sems, recv_sems, local_sems = sems
        x, y, c, others = _place()
        mine = 2 * x + y
        me, sibling = (x, y, c), (x, y, 1 - c)

        def dev(px, py, pc):
            return 4 * px + 2 * py + pc

        def copy(w, n, src, slot, to):
            return pltpu.make_async_remote_copy(
                src_ref=src, dst_ref=stack[w].at[slot], send_sem=send_sems.at[7 * w + n],
                recv_sem=recv_sems.at[7 * w + n], device_id=to, device_id_type=MESH)

        local, first, arrived, passed_on, from_sibling = [], [], [], [], []
        for w in range(nw):
            shape, axis = spec[names[w]]
            size = shape[axis] // N_CHIPS
            own = _slab(grad[w], axis, mine, size)
            local.append(pltpu.make_async_copy(own, stack[w].at[dev(*me)], local_sems.at[w]))
            first.append(copy(w, 0, own, dev(*me), sibling))
            from_sibling.append(copy(w, 0, own, dev(*sibling), me))
            for t, (qx, qy) in enumerate(others):
                got = stack[w].at[dev(qx, qy, c)]
                first.append(copy(w, 1 + t, _slab(grad[w], axis, 2 * qx + qy, size), dev(*me), (qx, qy, c)))
                arrived.append(copy(w, 1 + t, got, dev(qx, qy, c), me))
                passed_on.append(copy(w, 4 + t, got, dev(qx, qy, c), sibling))
                from_sibling.append(copy(w, 4 + t, got, dev(qx, qy, 1 - c), me))
        return local, first, arrived, passed_on, from_sibling

    def start(grad, stack, sems):
        local, first, _, _, _ = ops(grad, stack, sems)
        for cp in local + first:
            cp.start()

    def finish(grad, stack, sems):
        local, first, arrived, passed_on, from_sibling = ops(grad, stack, sems)
        for got, onward in zip(arrived, passed_on, strict=True):
            got.wait_recv()
            onward.start()
        for got in from_sibling:
            got.wait_recv()
        for cp in first + passed_on:
            cp.wait_send()
        for cp in local:
            cp.wait()

    dma = pltpu.SemaphoreType.DMA
    return _CommPlan(
        ins=[grads[name] for name in names],
        out_shape=[jax.ShapeDtypeStruct((N_DEV,) + _shard_shape(*spec[name]), BF16) for name in names],
        scratch=[dma((7 * nw,)), dma((7 * nw,)), dma((nw,))],
        start=start, finish=finish)


def _adamw(w, g, m, v):
    m = ADAM_B1 * m + (1.0 - ADAM_B1) * g
    v = ADAM_B2 * v + (1.0 - ADAM_B2) * (g * g)
    m_hat = m / (1.0 - ADAM_B1 ** ADAM_STEP)
    v_hat = v / (1.0 - ADAM_B2 ** ADAM_STEP)
    delta = -ADAM_LR * (m_hat / (jnp.sqrt(v_hat) + ADAM_EPS) + ADAM_WD * w)
    return delta, m, v


def _reduce_adamw(name, stack, w, m, v):
    rows, cols = w.shape
    tr = next(t for t in (256, 128, 64) if rows % t == 0)

    def body(s_ref, w_ref, m_ref, v_ref, g_ref, d_ref, nm_ref, nv_ref):
        g = s_ref[0].astype(F32)
        for d in range(1, N_DEV):
            g = g + s_ref[d].astype(F32)
        g_ref[...] = g
        d_ref[...], nm_ref[...], nv_ref[...] = _adamw(w_ref[...], g, m_ref[...], v_ref[...])

    blk = pl.BlockSpec((tr, cols), lambda i: (i, 0))
    return pl.pallas_call(
        body, name=name, grid=(rows // tr,),
        in_specs=[pl.BlockSpec((N_DEV, tr, cols), lambda i: (0, i, 0)), blk, blk, blk],
        out_specs=[blk] * 4, out_shape=[jax.ShapeDtypeStruct((rows, cols), F32)] * 4,
        compiler_params=_cparams(),
    )(stack, w, m, v)


def _small_step(pack, w, m, v):
    def body(p_ref, w_ref, m_ref, v_ref, g_ref, d_ref, nm_ref, nv_ref, loss_ref, all_ref, send_sems, recv_sems):
        x, y, c, _ = _place()
        me = 4 * x + 2 * y + c
        all_ref[me] = p_ref[...]
        sent = []
        for n in range(1, N_DEV):
            peer = me ^ n
            cp = pltpu.make_async_remote_copy(
                src_ref=p_ref, dst_ref=all_ref.at[me], send_sem=send_sems.at[n - 1], recv_sem=recv_sems.at[n - 1],
                device_id=(peer // 4, (peer // 2) % 2, peer % 2), device_id_type=MESH)
            cp.start()
            sent.append(cp)
        for n in range(1, N_DEV):
            peer = me ^ n
            pltpu.make_async_remote_copy(
                src_ref=p_ref, dst_ref=all_ref.at[peer], send_sem=send_sems.at[n - 1], recv_sem=recv_sems.at[n - 1],
                device_id=(peer // 4, (peer // 2) % 2, peer % 2), device_id_type=MESH).wait_recv()
        for cp in sent:
            cp.wait_send()
        tot = all_ref[0]
        for d in range(1, N_DEV):
            tot = tot + all_ref[d]
        g = tot[:SMALL_ROWS]
        g_ref[...] = g
        d_ref[...], nm_ref[...], nv_ref[...] = _adamw(w_ref[...], g, m_ref[...], v_ref[...])
        loss_ref[...] = jnp.sum(jnp.sum(tot[SMALL_ROWS:], axis=1, keepdims=True), axis=0, keepdims=True)

    vm = pl.BlockSpec(memory_space=pltpu.VMEM)
    small = jax.ShapeDtypeStruct((SMALL_ROWS, LANES), F32)
    return pl.pallas_call(
        body, name="small_step",
        in_specs=[vm] * 4, out_specs=[vm] * 5,
        out_shape=[small] * 4 + [jax.ShapeDtypeStruct((1, 1), F32)],
        scratch_shapes=[pltpu.VMEM((N_DEV, PACK_ROWS, LANES), F32),
                        pltpu.SemaphoreType.DMA((N_DEV - 1,)), pltpu.SemaphoreType.DMA((N_DEV - 1,))],
    )(pack, w, m, v)


LATER_WEIGHTS = tuple(name for name, _, _ in BIG if name != "w_in")


def _layer_step(x, mem, tgt, shards, vec):
    s = x.shape[0]
    d = D_MODEL
    tm = min(ROW_TILE, s)
    tl = min(WIDE_TILE, s)
    xb, cos2, sin2, w_in = _prep(x, _gather_plan(("w_in",), shards))
    bf = lambda w: ((s, w), BF16)
    f32 = lambda w: ((s, w), F32)

    w_sb, w_rqk = w_in[:, :OFF_RET_Q], w_in[:, OFF_RET_Q:OFF_RET_V]
    w_rvg, w_gate = w_in[:, OFF_RET_V:OFF_GATE], w_in[:, OFF_GATE:]
    q_scale = lambda width, q_width, scale: jnp.concatenate(
        [jnp.full((1, q_width), scale, F32), jnp.ones((1, width - q_width), F32)], axis=1)
    n_groups = 3 * SB_WIDTH // LANES

    def sb_epi(acc, t, i, j):
        scaled = acc * t[0]
        return [jnp.stack([scaled[:, g * LANES:(g + 1) * LANES] for g in range(n_groups)])], []

    (sb_qkv,) = _mm(
        "in_sb", xb, w_sb, s, 3 * SB_WIDTH, d, tm=tl, tn=3 * SB_WIDTH, tk=d, epi=sb_epi,
        ins=[(q_scale(3 * SB_WIDTH, SB_WIDTH, SB_SCALE), *_rowvec(3 * SB_WIDTH))],
        outs=[((n_groups, s, LANES), BF16, (n_groups, tl, LANES), lambda i, j: (0, i, 0))])

    def rope_epi(acc, t, i, j):
        cos, sin, scale = t
        parts = []
        for g in range(acc.shape[1] // RET_QK):
            xg = acc[:, g * RET_QK:(g + 1) * RET_QK]
            parts.append(xg * cos + _swap_halves(xg) * sin)
        return [jnp.concatenate(parts, axis=1) * scale], []

    rope_in = ((tl, RET_QK), lambda i, j: (i, 0))
    (rqk,) = _mm("in_rqk", xb, w_rqk, s, 2 * RET_QK_WIDTH, d, tm=tl, tn=2 * RET_QK_WIDTH, tk=d, epi=rope_epi,
                 chunk=MXU_COLS,
                 ins=[(cos2, *rope_in), (sin2, *rope_in),
                      (q_scale(2 * RET_QK_WIDTH, RET_QK_WIDTH, RET_SCALE), *_rowvec(2 * RET_QK_WIDTH))],
                 outs=[(*f32(2 * RET_QK_WIDTH), *_tile(tl, 2 * RET_QK_WIDTH))])
    (rvg,) = _mm("in_rvg", xb, w_rvg, s, 2 * RET_V_WIDTH, d, tm=tl, tn=2 * RET_V_WIDTH, tk=d, chunk=MXU_COLS,
                 epi=_plain, outs=[(*bf(2 * RET_V_WIDTH), *_tile(tl, 2 * RET_V_WIDTH))])
    (gates,) = _mm("in_gate", xb, w_gate, s, 2 * d, d, tm=tl, tn=2 * d, tk=d, chunk=MXU_COLS,
                   epi=lambda acc, t, i, j: ([_sigmoid(acc + t[0])], []),
                   ins=[(vec["b_gate"], *_rowvec(2 * d))], outs=[(*bf(2 * d), *_tile(tl, 2 * d))])

    sb_out, sb_out_f32, *gathered = _sb_fwd(sb_qkv, s, comm=_gather_plan(LATER_WEIGHTS, shards))
    wt = dict(zip(LATER_WEIGHTS, gathered, strict=True))
    ret, gated = _ret_fwd(rqk, rvg, s)
    (y_sb,) = _mm("sb_o", sb_out, wt["w_sb_o"], s, d, SB_WIDTH, tm=tl, tn=d, tk=SB_WIDTH, epi=_plain,
                  outs=[(*bf(d), *_tile(tl, d))])
    y_ret, mixin = _mm(
        "ret_o", gated, wt["w_ret_o"], s, d, RET_V_WIDTH, tm=tl, tn=d, tk=RET_V_WIDTH, chunk=MXU_COLS,
        epi=lambda acc, t, i, j: ([acc, t[0].astype(F32) * t[2].astype(F32) + t[1].astype(F32) * acc], []),
        ins=[(gates, *_tile(tl, d)), (gates, *_tile(tl, d, 1)), (y_sb, *_tile(tl, d))],
        outs=[(*bf(d), *_tile(tl, d)), (*bf(d), *_tile(tl, d))])

    def ln_epi(acc, t, i, j):
        *res, g, b = t
        prev = res[0] if len(res) == 1 else res[0] * res[1] + res[2]
        xhat, rstd = _norm(DN_ALPHA * prev + acc)
        return [xhat * g + b, xhat, rstd], []

    full = _tile(tm, d)
    col1 = ((tm, 1), lambda i, j: (i, 0))
    vec_in = lambda name: (vec[name], *_rowvec(d))
    ln_outs = [(*bf(d), *full), (*f32(d), *full), ((s, 1), F32, *col1)]
    x1b, xhat1, rstd1 = _mm(
        "mix_o", mixin, wt["w_mix_o"], s, d, d, tm=tm, tn=d, tk=d, epi=ln_epi,
        ins=[(x, *full), vec_in("ln1_g"), vec_in("ln1_b")], outs=ln_outs)

    (qm,) = _mm("mem_q", x1b, wt["w_mem_q"], s, d, d, tm=tl, tn=d, tk=d,
                epi=lambda acc, t, i, j: ([acc * MEM_SCALE], []), outs=[(*bf(d), *_tile(tl, d))])
    (kv,) = _mm("mem_kv", mem, wt["w_mem_kv"], MEM_LEN, 2 * d, d, tm=MEM_LEN, tn=d, tk=d, epi=_plain,
                outs=[((MEM_LEN, 2 * d), BF16, *_tile(MEM_LEN, d))])
    att = _xattn_fwd(qm, kv, s)
    x2b, xhat2, rstd2 = _mm(
        "mem_o", att, wt["w_mem_o"], s, d, d, tm=tm, tn=d, tk=d, epi=ln_epi,
        ins=[(xhat1, *full), vec_in("ln1_g"), vec_in("ln1_b"), vec_in("ln2_g"), vec_in("ln2_b")], outs=ln_outs)

    fh = FFN_HIDDEN
    tf = fh // 2
    def swiglu_epi(acc, t, i, j):
        a = acc[0].astype(BF16).astype(F32)
        return [a, acc[1], a * _sigmoid(a) * acc[1]], []

    f1, f2, act = _mm(
        "ffn_in", x2b, wt["w_ffn_in"], s, 2 * fh, d, tm=tm, tn=2 * fh, tk=d, epi=swiglu_epi, chunk=MXU_COLS,
        halves=True, outs=[(*bf(fh), *_tile(tm, fh))] * 3)

    def head_epi(acc, t, i, j):
        prev_hat, prev_g, prev_b, g, b, target = t
        xhat, rstd = _norm(DN_ALPHA * (prev_hat * prev_g + prev_b) + acc)
        err = xhat * g + b - target
        dy = err * (1.0 / d)
        du = _norm_bwd(dy * g, xhat, rstd)
        return [du], [_colsum(dy * xhat), _colsum(dy), _colsum(err * err) * (0.5 / d)]

    vec_acc = ((1, d), F32)
    du3b, dg3, db3, loss_cols = _mm(
        "ffn_out", act, wt["w_ffn_out"], s, d, fh, tm=tm, tn=d, tk=fh, epi=head_epi,
        ins=[(xhat2, *full), vec_in("ln2_g"), vec_in("ln2_b"), vec_in("ln3_g"), vec_in("ln3_b"), (tgt, *full)],
        outs=[(*bf(d), *full)], accs=[vec_acc] * 3)

    grads = {}
    ts = min(SEQ_TILE, s)

    def wgrad(name, a, b, m, n, tm_, tn_, tk_=None):
        (g,) = _mm(name, a, b, m, n, a.shape[0], tm=tm_, tn=tn_, tk=tk_ or ts, ta=True, epi=_plain,
                   outs=[((m, n), BF16, *_tile(tm_, tn_))])
        return g

    def wgrad_wide(name, a, pieces, tm_):
        m, width = a.shape[1], sum(p.shape[1] for p in pieces)
        (g,) = _mm(name, a, pieces, m, width, s, tm=tm_, tn=width, tk=min(ROW_TILE, s // 2), ta=True, epi=_plain,
                   outs=[((m, width), BF16, *_tile(tm_, width))])
        return g

    def ffn_bwd_epi(acc, t, i, j):
        a, b = t[0].astype(F32), t[1].astype(F32)
        sg = _sigmoid(a)
        return [acc * b * (sg * (1.0 + a * (1.0 - sg))), acc * (a * sg)], []

    df1, df2 = _mm(
        "ffn_out_t", du3b, wt["w_ffn_out"], s, fh, d, tm=tm, tn=fh, tk=d, tb=True, epi=ffn_bwd_epi, chunk=MXU_COLS,
        ins=[(f1, *_tile(tm, fh)), (f2, *_tile(tm, fh))],
        outs=[(*bf(fh), *_tile(tm, fh)), (*bf(fh), *_tile(tm, fh))])
    grads["w_ffn_out"] = wgrad("g_ffn_out", act, du3b, fh, d, tf, d)
    grads["w_ffn_in"] = wgrad_wide("g_ffn_in", x2b, [df1, df2], d // 2)

    def ln_bwd(name, a, b, k, tk, b_off, more, scales, xhat, rstd, g):
        def epi(acc, t, i, j):
            *extra, xh, rs, gg = t
            dy = acc
            for e, sc in zip(extra, scales, strict=True):
                dy = dy + e.astype(F32) * sc
            return [_norm_bwd(dy * gg, xh, rs)], [_colsum(dy * xh), _colsum(dy)]

        return _mm(name, a, b, s, d, k, tm=tm, tn=d, tk=tk, tb=True, b_off=b_off, epi=epi,
                   ins=[(e, *full) for e in more] + [(xhat, *full), (rstd, *col1), (g, *_rowvec(d))],
                   outs=[(*bf(d), *full)], accs=[vec_acc] * 2)

    du2b, dg2, db2 = ln_bwd("ffn_in_t", [df1, df2], wt["w_ffn_in"], 2 * fh, 2 * fh, (0, 0), [du3b], [DN_ALPHA],
                            xhat2, rstd2, vec["ln2_g"])

    (datt,) = _mm("mem_o_t", du2b, wt["w_mem_o"], s, d, d, tm=tl, tn=d, tk=d, tb=True, epi=_plain,
                  outs=[(*bf(d), *_tile(tl, d))])
    grads["w_mem_o"] = wgrad("g_mem_o", att, du2b, d, d, d, d)
    dqm, dkv = _xattn_bwd(qm, kv, datt, s)
    grads["w_mem_q"] = wgrad("g_mem_q", x1b, dqm, d, d, d, d)
    grads["w_mem_kv"] = wgrad("g_mem_kv", mem, dkv, d, 2 * d, d, d, MEM_LEN)
    du1b, dg1, db1 = ln_bwd("mem_q_t", dqm, wt["w_mem_q"], d, d, (0, 0), [du2b], [DN_ALPHA],
                            xhat1, rstd1, vec["ln1_g"])

    def merge_bwd_epi(acc, t, i, j):
        g0, g1, ysb, yret = (v.astype(F32) for v in t)
        dgate0 = acc * ysb * (g0 * (1.0 - g0))
        dgate1 = acc * yret * (g1 * (1.0 - g1))
        return [dgate0, dgate1, acc * g0, acc * g1], [_colsum(dgate0), _colsum(dgate1)]

    dgate0, dgate1, dy_sb, dy_ret, dbg0, dbg1 = _mm(
        "mix_o_t", du1b, wt["w_mix_o"], s, d, d, tm=tm, tn=d, tk=d, tb=True, epi=merge_bwd_epi,
        ins=[(gates, *full), (gates, *_tile(tm, d, 1)), (y_sb, *full), (y_ret, *full)],
        outs=[(*bf(d), *full)] * 4, accs=[vec_acc] * 2)
    grads["w_mix_o"] = wgrad("g_mix_o", mixin, du1b, d, d, d, d)
    grads["w_sb_o"] = wgrad("g_sb_o", sb_out, dy_sb, SB_WIDTH, d, SB_WIDTH, d)
    grads["w_ret_o"] = wgrad("g_ret_o", gated, dy_ret, RET_V_WIDTH, d, RET_V_WIDTH, d)
    (dsb_out,) = _mm("sb_o_t", dy_sb, wt["w_sb_o"], s, SB_WIDTH, d, tm=tl, tn=SB_WIDTH, tk=d, tb=True, epi=_plain,
                     outs=[(*bf(SB_WIDTH), *_tile(tl, SB_WIDTH))])

    def gate_norm_bwd_epi(acc, t, i, j):
        r, g = t[0], t[1].astype(F32)
        drg, dret = [], []
        for h in range(acc.shape[1] // RET_V):
            sl = slice(h * RET_V, (h + 1) * RET_V)
            xhat, rstd = _norm(r[:, sl])
            gg, dd = g[:, sl], acc[:, sl]
            sg = _sigmoid(gg)
            drg.append(dd * xhat * (sg * (1.0 + gg * (1.0 - sg))))
            dret.append(_norm_bwd(dd * (gg * sg), xhat, rstd))
        return [jnp.concatenate(drg, axis=1), jnp.concatenate(dret, axis=1)], []

    drg, dret = _mm(
        "ret_o_t", dy_ret, wt["w_ret_o"], s, RET_V_WIDTH, d, tm=tm, tn=d, tk=d, tb=True, epi=gate_norm_bwd_epi,
        chunk=MXU_COLS,
        ins=[(ret, *full), (rvg, *_tile(tm, d, 1))],
        outs=[(*bf(RET_V_WIDTH), *full)] * 2)

    drq = _ret_bwd_q(rqk, rvg, dret, cos2, sin2, s)
    drk, drv = _ret_bwd_kv(rqk, rvg, dret, cos2, sin2, s)
    dsq, dsk, dsv, *stacked = _sb_bwd(sb_qkv, sb_out_f32, dsb_out, s, comm=_exchange_plan(LATER_WEIGHTS, grads))
    stacks = dict(zip(LATER_WEIGHTS, stacked, strict=True))

    dh_mixers, dh_gates = [dsq, dsk, dsv, drq, drk, drv], [drg, dgate0, dgate1]
    grads["w_in"] = wgrad_wide("g_in", xb, dh_mixers + dh_gates, d // 2)
    grad_x, stacks["w_in"] = _mm(
        "in_t", dh_mixers + dh_gates, w_in, s, d, IN_WIDTH, tm=tm, tn=d, tk=IN_WIDTH, tb=True,
        epi=lambda acc, t, i, j: ([acc + DN_ALPHA * t[0].astype(F32)], []),
        ins=[(du1b, *full)], outs=[(*f32(d), *full)], comm=_exchange_plan(("w_in",), grads))

    small = {"b_gate": jnp.concatenate([dbg0, dbg1], axis=1), "ln1_g": dg1, "ln1_b": db1, "ln2_g": dg2,
             "ln2_b": db2, "ln3_g": dg3, "ln3_b": db3}
    return grad_x, stacks, small, loss_cols


def kernel(x, mem, w_in, b_gate, w_sb_o, w_ret_o, w_mix_o, ln1_g, ln1_b, w_mem_q, w_mem_kv, w_mem_o, ln2_g, ln2_b, w_ffn_in, w_ffn_out, ln3_g, ln3_b, loss_target, m_w_in, m_b_gate, m_w_sb_o, m_w_ret_o, m_w_mix_o, m_ln1_g, m_ln1_b, m_w_mem_q, m_w_mem_kv, m_w_mem_o, m_ln2_g, m_ln2_b, m_w_ffn_in, m_w_ffn_out, m_ln3_g, m_ln3_b, v_w_in, v_b_gate, v_w_sb_o, v_w_ret_o, v_w_mix_o, v_ln1_g, v_ln1_b, v_w_mem_q, v_w_mem_kv, v_w_mem_o, v_ln2_g, v_ln2_b, v_w_ffn_in, v_w_ffn_out, v_ln3_g, v_ln3_b):
    given = dict(locals())
    s = x.shape[1]
    x2d = x.reshape(s, D_MODEL)
    tgt = loss_target.reshape(s, D_MODEL)
    mem2d = mem.reshape(MEM_LEN, D_MODEL)
    shard = {name: given[name].reshape(_shard_shape(shape, axis)) for name, shape, axis in BIG}
    vec = {name: given[name] for name in SMALL}

    shards_bf = {name: _cast_bf16("cast_" + name, shard[name]) for name, _, _ in BIG}

    grad_x, stacks, small, loss_cols = _layer_step(x2d, mem2d, tgt, shards_bf, vec)

    out = {}
    for name, shape, axis in BIG:
        stack = stacks[name]
        shp = given[name].shape
        res = _reduce_adamw("adamw_" + name, stack, shard[name], given["m_" + name].reshape(stack.shape[1:]),
                            given["v_" + name].reshape(stack.shape[1:]))
        out[name] = [r.reshape(shp) for r in res]

    pack = jnp.concatenate([small[name] for name in SMALL] + [loss_cols], axis=1).reshape(PACK_ROWS, LANES)
    cat = lambda pre: jnp.concatenate([given[pre + name] for name in SMALL], axis=1).reshape(SMALL_ROWS, LANES)
    *res, loss = _small_step(pack, cat(""), cat("m_"), cat("v_"))
    flat = [r.reshape(1, SMALL_LEN) for r in res]
    off = 0
    for name in SMALL:
        n = given[name].shape[1]
        out[name] = [r[:, off:off + n] for r in flat]
        off += n

    return (loss.reshape(()), grad_x.reshape(x.shape),
            *[out[name][0] for name in WEIGHT_ORDER], *[out[name][1] for name in WEIGHT_ORDER],
            *[out[name][2] for name in WEIGHT_ORDER], *[out[name][3] for name in WEIGHT_ORDER])
```

```python
import functools

import jax
import jax.numpy as jnp
import numpy as np
from jax import lax
from jax.experimental import pallas as pl
from jax.experimental.pallas import tpu as pltpu

F32, BF16 = jnp.float32, jnp.bfloat16
MESH = pl.DeviceIdType.MESH

D_MODEL = 1024
MEM_LEN = 256
SB_HEADS, SB_DIM, SB_WIDTH = 8, 64, 512
RET_HEADS, RET_QK, RET_V = 4, 128, 256
RET_QK_WIDTH, RET_V_WIDTH = 512, 1024
ROPE_BASE = 10000.0
MEM_HEADS, MEM_DIM = 4, 256
FFN_HIDDEN = 2816
IN_WIDTH = 6656
OFF_RET_Q, OFF_RET_V, OFF_RET_G, OFF_GATE = 1536, 2560, 3584, 4608
DN_ALPHA = 2.0 ** 0.25
LN_EPS = 1e-5
SB_SCALE = SB_DIM ** -0.5
SB_DEAD = -110.0
RET_SCALE = RET_QK ** -0.5
MEM_SCALE = MEM_DIM ** -0.5
ADAM_LR, ADAM_B1, ADAM_B2, ADAM_EPS, ADAM_WD, ADAM_STEP = 0.001, 0.9, 0.999, 1e-08, 0.01, 10

N_DEV, N_CHIPS = 8, 4

LANES = 128
MXU_COLS = 256
VMEM_LIMIT_BYTES = 52 * 2 ** 20
ROW_TILE = 512
WIDE_TILE = 1024
SEQ_TILE = 2048
SB_BLOCK = 256
RET_BLOCK = 256
RET_CHUNKS_PER_STEP = 4
XATTN_ROWS = 1024

BIG = (
    ("w_in", (D_MODEL, IN_WIDTH), 1),
    ("w_sb_o", (SB_WIDTH, D_MODEL), 1),
    ("w_ret_o", (RET_V_WIDTH, D_MODEL), 0),
    ("w_mix_o", (D_MODEL, D_MODEL), 0),
    ("w_mem_q", (D_MODEL, D_MODEL), 0),
    ("w_mem_kv", (D_MODEL, 2 * D_MODEL), 1),
    ("w_mem_o", (D_MODEL, D_MODEL), 0),
    ("w_ffn_in", (D_MODEL, 2 * FFN_HIDDEN), 1),
    ("w_ffn_out", (FFN_HIDDEN, D_MODEL), 0),
)
SMALL = ("b_gate", "ln1_g", "ln1_b", "ln2_g", "ln2_b", "ln3_g", "ln3_b")
SMALL_LEN = 2 * D_MODEL + 6 * D_MODEL
SMALL_ROWS = SMALL_LEN // LANES
PACK_ROWS = SMALL_ROWS + D_MODEL // LANES
WEIGHT_ORDER = ("w_in", "b_gate", "w_sb_o", "w_ret_o", "w_mix_o", "ln1_g", "ln1_b", "w_mem_q", "w_mem_kv",
                "w_mem_o", "ln2_g", "ln2_b", "w_ffn_in", "w_ffn_out", "ln3_g", "ln3_b")


def _cparams():
    return pltpu.CompilerParams(vmem_limit_bytes=VMEM_LIMIT_BYTES)


def _dot(a, b, ca, cb):
    return lax.dot_general(a, b, (((ca,), (cb,)), ((), ())), preferred_element_type=F32)


def _sigmoid(x):
    return 1.0 / (1.0 + jnp.exp(-x))


def _mm(name, a, b, m, n, k, *, tm, tn, tk, epi, outs, ins=(), accs=(), ta=False, tb=False,
        a_off=(0, 0), b_off=(0, 0), j_outer=False, comm=None, chunk=None, halves=False):
    assert not halves or (chunk is not None and (tn // 2) % chunk == 0 and not ins), name
    assert m % tm == 0 and n % tn == 0 and k % tk == 0, (name, m, n, k, tm, tn, tk)
    assert chunk is None or (k == tk and tn % chunk == 0), name
    ni, nj, nk = m // tm, n // tn, k // tk
    assert not accs or nj == 1, name
    ij = (lambda g0, g1: (g1, g0)) if j_outer else (lambda g0, g1: (g0, g1))

    def spec(block, index):
        return pl.BlockSpec(block, lambda g0, g1, kk: index(*ij(g0, g1), kk))

    a_list = list(a) if isinstance(a, (list, tuple)) else [a]
    n_a = len(a_list)
    if n_a > 1:
        assert not ta and nk == 1 and chunk is None and not any(a_off), name
        assert sum(p.shape[1] for p in a_list) == k, name
        a_specs = [spec((tm, p.shape[1]), lambda i, j, kk: (i, 0)) for p in a_list]
    elif ta:
        a_specs = [spec((tk, tm), lambda i, j, kk: (kk + a_off[0], i + a_off[1]))]
    else:
        a_specs = [spec((tm, tk), lambda i, j, kk: (i + a_off[0], kk + a_off[1]))]
    b_list = list(b) if isinstance(b, (list, tuple)) else [b]
    n_b = len(b_list)
    if n_b > 1:
        assert not tb and nj == 1 and nk > 1 and n_a == 1 and chunk is None and not any(b_off), name
        assert sum(p.shape[1] for p in b_list) == n, name
        b_specs = [spec((tk, p.shape[1]), lambda i, j, kk: (kk, 0)) for p in b_list]
    elif tb:
        b_specs = [spec((tn, tk), lambda i, j, kk: (j + b_off[0], kk + b_off[1]))]
    else:
        b_specs = [spec((tk, tn), lambda i, j, kk: (kk + b_off[0], j + b_off[1]))]
    if n_a > 1 and nj == 1:
        b_specs = [pl.BlockSpec(b_specs[0].block_shape, b_specs[0].index_map, pipeline_mode=pl.Buffered(1))]
    in_specs = [*a_specs, *b_specs]
    for _, bs, im in ins:
        in_specs.append(spec(bs, lambda i, j, kk, im=im: im(i, j)))
    out_specs, out_shape = [], []
    for shape, dtype, bs, im in outs:
        out_specs.append(spec(bs, lambda i, j, kk, im=im: im(i, j)))
        out_shape.append(jax.ShapeDtypeStruct(shape, dtype))
    for shape, dtype in accs:
        out_specs.append(spec(shape, lambda i, j, kk, nd=len(shape): (0,) * nd))
        out_shape.append(jax.ShapeDtypeStruct(shape, dtype))
    n_in, n_out, n_acc = len(ins), len(outs), len(accs)
    ca, cb = (0 if ta else 1), (1 if tb else 0)
    grid = (*ij(ni, nj), nk)
    comm_ins, comm_outs, comm_scratch = [], [], []
    if comm is not None:
        comm_in_specs, comm_out_specs = comm.specs
        comm_ins, comm_outs, comm_scratch = list(comm.ins), list(comm.out_shape), list(comm.scratch)
        in_specs += comm_in_specs
        out_specs += comm_out_specs
        out_shape += comm_outs
    n_ci, n_co = len(comm_ins), len(comm_outs)

    def body(*refs):
        a_refs, b_refs, refs = refs[:n_a], refs[n_a:n_a + n_b], refs[n_a + n_b:]
        a_ref, b_ref = a_refs[0], b_refs[0]
        in_refs = refs[:n_in]
        ci_refs = refs[n_in:n_in + n_ci]
        rest = refs[n_in + n_ci:]
        out_refs, acc_refs = rest[:n_out], rest[n_out:n_out + n_acc]
        co_refs = rest[n_out + n_acc:n_out + n_acc + n_co]
        scratch = rest[n_out + n_acc + n_co:]
        sem_refs, scratch = scratch[:len(comm_scratch)], scratch[len(comm_scratch):]
        (i, j), kk = ij(pl.program_id(0), pl.program_id(1)), pl.program_id(2)
        if comm is not None:
            first_step, last_step = _grid_ends(grid)
            pl.when(first_step)(lambda: comm.start(ci_refs, co_refs, sem_refs))
        def finish(acc, cols=slice(None)):
            def of(r):
                return r[..., cols] if r.shape[-1] == tn else r[...]

            o_tiles, a_tiles = epi(acc, [of(r) for r in in_refs], i, j)
            for r, t in zip(out_refs, o_tiles, strict=True):
                r[..., cols] = t.astype(r.dtype)
            if n_acc:
                @pl.when(i == 0)
                def _():
                    for r, t in zip(acc_refs, a_tiles, strict=True):
                        r[..., cols] = t

                @pl.when(i > 0)
                def _():
                    for r, t in zip(acc_refs, a_tiles, strict=True):
                        r[..., cols] += t

        if chunk is not None:
            a_tile = a_ref[...].astype(BF16)

            def product(c0):
                b_part = b_ref[c0:c0 + chunk, :] if tb else b_ref[:, c0:c0 + chunk]
                return _dot(a_tile, b_part.astype(BF16), ca, cb)

            for c0 in range(0, tn // 2 if halves else tn, chunk):
                acc = (product(c0), product(tn // 2 + c0)) if halves else product(c0)
                finish(acc, slice(c0, c0 + chunk))
            if comm is not None:
                pl.when(last_step)(lambda: comm.finish(ci_refs, co_refs, sem_refs))
            return

        if n_b > 1:
            acc_ref = scratch[0]

            def accumulate(first):
                a_tile, c0 = a_ref[...].astype(BF16), 0
                for r in b_refs:
                    c1 = c0 + r.shape[1]
                    term = _dot(a_tile, r[...].astype(BF16), ca, cb)
                    acc_ref[:, c0:c1] = term if first else acc_ref[:, c0:c1] + term
                    c0 = c1

            pl.when(kk == 0)(lambda: accumulate(True))
            pl.when(kk > 0)(lambda: accumulate(False))
            pl.when(kk == nk - 1)(lambda: finish(acc_ref[...]))
            if comm is not None:
                pl.when(last_step)(lambda: comm.finish(ci_refs, co_refs, sem_refs))
            return

        if n_a > 1:
            part, c0 = None, 0
            for r in a_refs:
                c1 = c0 + r.shape[1]
                b_part = b_ref[:, c0:c1] if tb else b_ref[c0:c1, :]
                term = _dot(r[...].astype(BF16), b_part.astype(BF16), ca, cb)
                part, c0 = (term if part is None else part + term), c1
        else:
            part = _dot(a_ref[...].astype(BF16), b_ref[...].astype(BF16), ca, cb)
        if nk == 1:
            finish(part)
        else:
            acc_ref = scratch[0]

            @pl.when(kk == 0)
            def _():
                acc_ref[...] = part

            @pl.when(kk > 0)
            def _():
                acc_ref[...] += part

            @pl.when(kk == nk - 1)
            def _():
                finish(acc_ref[...])

        if comm is not None:
            pl.when(last_step)(lambda: comm.finish(ci_refs, co_refs, sem_refs))

    res = pl.pallas_call(
        body, name=name, grid=grid, in_specs=in_specs, out_specs=out_specs, out_shape=out_shape,
        scratch_shapes=comm_scratch + ([pltpu.VMEM((tm, tn), F32)] if nk > 1 else []),
        compiler_params=_cparams(),
    )(*a_list, *b_list, *[x for x, _, _ in ins], *comm_ins)
    return res


def _grid_ends(grid):
    ids = [pl.program_id(ax) for ax in range(len(grid))]
    first = functools.reduce(jnp.logical_and, [p == 0 for p in ids])
    last = functools.reduce(jnp.logical_and, [p == n - 1 for p, n in zip(ids, grid, strict=True)])
    return first, last


def _tile(tm, tn, dj=0):
    return (tm, tn), (lambda i, j: (i, j + dj))


def _rowvec(tn, dj=0):
    return (1, tn), (lambda i, j: (0, j + dj))


def _plain(acc, tiles, i, j):
    return [acc], []


def _ew(name, fn, ins, outs, rows, tr):
    assert rows % tr == 0, (name, rows, tr)
    in_specs = []
    for x in ins:
        if x.shape[0] == rows:
            in_specs.append(pl.BlockSpec((tr, x.shape[1]), lambda i: (i, 0)))
        else:
            in_specs.append(pl.BlockSpec(x.shape, lambda i: (0, 0)))
    n_in = len(ins)

    def body(*refs):
        res = fn(*[r[...] for r in refs[:n_in]])
        for r, t in zip(refs[n_in:], res, strict=True):
            r[...] = t.astype(r.dtype)

    return pl.pallas_call(
        body, name=name, grid=(rows // tr,), in_specs=in_specs,
        out_specs=[pl.BlockSpec((tr, w), lambda i: (i, 0)) for w, _ in outs],
        out_shape=[jax.ShapeDtypeStruct((rows, w), dt) for w, dt in outs],
        compiler_params=_cparams(),
    )(*ins)


def _cast_bf16(name, x):
    rows = x.shape[0]
    tr = next(t for t in (512, 256, 64) if rows % t == 0)
    return _ew(name, lambda v: (v,), [x], [(x.shape[1], BF16)], rows, tr)[0]


def _prep(x, comm):
    s = x.shape[0]
    half = RET_QK // 2
    inv = 1.0 / (ROPE_BASE ** (jnp.arange(half, dtype=F32) / half))
    inv2 = jnp.concatenate([inv, inv]).reshape(1, RET_QK)
    sign = jnp.concatenate([-jnp.ones((half,), F32), jnp.ones((half,), F32)]).reshape(1, RET_QK)
    tr = min(ROW_TILE, s)
    grid = (s // tr,)
    c_in_specs, c_out_specs, c_out_shape, c_scratch, c_ins, split = _host(comm, 3, 3)

    def body(*refs):
        (x_ref, inv_ref, sign_ref), (xb_ref, cos_ref, sin_ref), _, riding = split(refs)
        i = pl.program_id(0)
        first_step, last_step = _grid_ends(grid)
        pl.when(first_step)(lambda: comm.start(*riding))
        xb_ref[...] = x_ref[...].astype(BF16)
        pos = (lax.broadcasted_iota(jnp.int32, (tr, RET_QK), 0) + i * tr).astype(F32)
        ang = pos * inv_ref[...]
        cos_ref[...] = jnp.cos(ang)
        sin_ref[...] = jnp.sin(ang) * sign_ref[...]
        pl.when(last_step)(lambda: comm.finish(*riding))

    vec = pl.BlockSpec((1, RET_QK), lambda i: (0, 0))
    row = lambda w: pl.BlockSpec((tr, w), lambda i: (i, 0))
    return pl.pallas_call(
        body, name="prep", grid=grid,
        in_specs=[row(D_MODEL), vec, vec] + c_in_specs,
        out_specs=[row(D_MODEL), row(RET_QK), row(RET_QK)] + c_out_specs,
        out_shape=[jax.ShapeDtypeStruct((s, D_MODEL), BF16), jax.ShapeDtypeStruct((s, RET_QK), F32),
                   jax.ShapeDtypeStruct((s, RET_QK), F32)] + c_out_shape,
        scratch_shapes=c_scratch, compiler_params=_cparams(),
    )(x, inv2, sign, *c_ins)


def _swap_halves(x):
    return pltpu.roll(x, RET_QK // 2, 1)


def _norm(u):
    mu = jnp.mean(u, axis=-1, keepdims=True)
    d = u - mu
    var = jnp.mean(d * d, axis=-1, keepdims=True)
    rstd = lax.rsqrt(var + LN_EPS)
    return d * rstd, rstd


def _norm_bwd(dxh, xhat, rstd):
    m1 = jnp.mean(dxh, axis=-1, keepdims=True)
    m2 = jnp.mean(dxh * xhat, axis=-1, keepdims=True)
    return rstd * (dxh - m1 - xhat * m2)


def _colsum(t):
    return jnp.sum(t, axis=0, keepdims=True)


def _split_mm(t, tri):
    hi = t.astype(BF16)
    lo = (t - hi.astype(F32)).astype(BF16)
    return _dot(hi, tri, 1, 0) + _dot(lo, tri, 1, 0)


def _sb_masks():
    t = SB_BLOCK
    lane = lax.broadcasted_iota(jnp.int32, (1, LANES), 1)
    first = lane < SB_DIM
    m0 = jnp.where(first, 1.0, 0.0).astype(BF16)
    m1 = jnp.where(first, 0.0, 1.0).astype(BF16)
    row = lax.broadcasted_iota(jnp.int32, (t, t), 0)
    col = lax.broadcasted_iota(jnp.int32, (t, t), 1)
    return first, (m0, m1), row, col


def _sb_logits(qh, k, causal):
    z = _dot(qh, k, 1, 1)
    lp = jnp.log(1.0 + jnp.exp(-jnp.abs(z)))
    a = jnp.minimum(z, 0.0) - lp
    r = jnp.minimum(-z, 0.0) - lp
    if causal is not None:
        r = jnp.where(causal, r, 0.0)
    return a, r


def _sb_walk(i, blocks, l_ref, causal):
    pl.when(i == 0)(lambda: blocks([(i, causal)]))
    pl.when(i > 0)(lambda: blocks([(i, causal), (i - 1, None)]))

    def alive():
        top = jnp.max(functools.reduce(jnp.maximum, [l_ref[c] for c in range(l_ref.shape[0])]))
        return jnp.where(top > SB_DEAD, 1, 0)

    def cond(c):
        return jnp.logical_and(c[0] < i, c[1] > 0)

    def step(c):
        blocks([(i - 1 - c[0], None)])
        return c[0] + 1, alive()

    lax.while_loop(cond, step, (jnp.int32(1), alive()))


def _host(comm, n_in, n_out):
    if comm is None:
        return [], [], [], [], [], lambda refs: (refs[:n_in], refs[n_in:n_in + n_out], refs[n_in + n_out:], None)
    in_specs, out_specs = comm.specs
    n_ci, n_co, n_sem = len(comm.ins), len(comm.out_shape), len(comm.scratch)

    def split(refs):
        ins, ci = refs[:n_in], refs[n_in:n_in + n_ci]
        rest = refs[n_in + n_ci:]
        outs, co = rest[:n_out], rest[n_out:n_out + n_co]
        sems, scratch = rest[n_out + n_co:n_out + n_co + n_sem], rest[n_out + n_co + n_sem:]
        return ins, outs, scratch, (ci, co, sems)

    return in_specs, out_specs, list(comm.out_shape), list(comm.scratch), list(comm.ins), split


def _sb_qkv_specs(s, g):
    groups = SB_HEADS // 2 // g
    return [pl.BlockSpec((g, SB_BLOCK, LANES), lambda p, i: (p, i, 0)),
            pl.BlockSpec((g, s, LANES), lambda p, i: (groups + p, 0, 0)),
            pl.BlockSpec((g, s, LANES), lambda p, i: (2 * groups + p, 0, 0))]


def _sb_fwd(qkv, s, comm=None):
    t = SB_BLOCK
    g = 2
    nq = s // t
    grid = (SB_HEADS // 2 // g, nq)
    c_in_specs, c_out_specs, c_out_shape, c_scratch, c_ins, split = _host(comm, 3, 2)

    def body(*refs):
        (q_ref, k_ref, v_ref), (o_ref, of_ref), (l_ref, acc_ref), riding = split(refs)
        i = pl.program_id(1)
        if comm is not None:
            first_step, last_step = _grid_ends(grid)
            pl.when(first_step)(lambda: comm.start(*riding))
        first, hmask, row, col = _sb_masks()
        after = jnp.where(row > col, 1.0, 0.0).astype(BF16)
        causal = col < row
        heads = [(p, h) for p in range(g) for h in range(2)]
        qh = {(p, h): q_ref[p] * hmask[h] for p, h in heads}
        l_ref[...] = jnp.zeros_like(l_ref)
        acc_ref[...] = jnp.zeros_like(acc_ref)

        def blocks(todo):
            chains = [(b, p, h) for b in range(len(todo)) for p, h in heads]
            starts = [pl.multiple_of(kb * t, t) for kb, _ in todo]
            ks = {(b, p): k_ref[p, pl.ds(st, t), :] for b, st in enumerate(starts) for p in range(g)}
            vs = {(b, p): v_ref[p, pl.ds(st, t), :] for b, st in enumerate(starts) for p in range(g)}
            ar = {(b, p, h): _sb_logits(qh[p, h], ks[b, p], todo[b][1]) for b, p, h in chains}
            later = {c: _split_mm(ar[c][1], after) for c in chains}
            carry = {(p, h): l_ref[2 * p + h] for p, h in heads}
            w = {}
            for b, (_, mask) in enumerate(todo):
                for p, h in heads:
                    wc = jnp.exp(ar[b, p, h][0] + later[b, p, h] + carry[p, h])
                    w[b, p, h] = wc if mask is None else jnp.where(mask, wc, 0.0)
                carry = {(p, h): carry[p, h] + jnp.sum(ar[b, p, h][1], axis=1, keepdims=True) for p, h in heads}
            pv = {(b, p, h): _dot(w[b, p, h].astype(BF16), vs[b, p], 1, 0) for b, p, h in chains}
            for p in range(g):
                lanes = slice(p * LANES, (p + 1) * LANES)
                acc = acc_ref[:, lanes]
                for b in range(len(todo)):
                    acc = acc + jnp.where(first, pv[b, p, 0], pv[b, p, 1])
                acc_ref[:, lanes] = acc
            for p, h in heads:
                l_ref[2 * p + h] = carry[p, h]

        _sb_walk(i, blocks, l_ref, causal)
        o_ref[...] = acc_ref[...].astype(o_ref.dtype)
        of_ref[...] = acc_ref[...]
        if comm is not None:
            pl.when(last_step)(lambda: comm.finish(*riding))

    blk = pl.BlockSpec((t, g * LANES), lambda p, i: (i, p))
    return pl.pallas_call(
        body, name="sb_fwd", grid=grid,
        in_specs=_sb_qkv_specs(s, g) + c_in_specs,
        out_specs=[blk, blk] + c_out_specs,
        out_shape=[jax.ShapeDtypeStruct((s, SB_WIDTH), BF16), jax.ShapeDtypeStruct((s, SB_WIDTH), F32)] + c_out_shape,
        scratch_shapes=c_scratch + [pltpu.VMEM((2 * g, t, 1), F32), pltpu.VMEM((t, g * LANES), F32)],
        compiler_params=_cparams(),
    )(qkv, qkv, qkv, *c_ins)


def _sb_bwd(qkv, o, do, s, comm=None):
    t = SB_BLOCK
    g = 2
    nq = s // t
    grid = (SB_HEADS // 2 // g, nq)
    c_in_specs, c_out_specs, c_out_shape, c_scratch, c_ins, split = _host(comm, 5, 3)

    def body(*refs):
        ((q_ref, k_ref, v_ref, o_ref, do_ref), (dq_ref, dk_ref, dv_ref),
         (l_ref, e_ref, dq_acc, dk_acc, dv_acc), riding) = split(refs)
        i = pl.program_id(1)
        if comm is not None:
            first_step, last_step = _grid_ends(grid)
            pl.when(first_step)(lambda: comm.start(*riding))
        first, hmask, row, col = _sb_masks()
        after = jnp.where(row > col, 1.0, 0.0).astype(BF16)
        from_here = jnp.where(row >= col, 1.0, 0.0).astype(BF16)
        causal = col < row

        @pl.when(i == 0)
        def _():
            dk_acc[...] = jnp.zeros_like(dk_acc)
            dv_acc[...] = jnp.zeros_like(dv_acc)

        heads = [(p, h) for p in range(g) for h in range(2)]
        lanes = [slice(p * LANES, (p + 1) * LANES) for p in range(g)]
        q = [q_ref[p] for p in range(g)]
        do_ = [do_ref[:, lanes[p]] for p in range(g)]
        qh = {(p, h): q[p] * hmask[h] for p, h in heads}
        doh = {(p, h): do_[p] * hmask[h] for p, h in heads}
        total = {}
        for p in range(g):
            prod = do_[p].astype(F32) * o_ref[:, lanes[p]]
            total[p, 0] = jnp.sum(jnp.where(first, prod, 0.0), axis=1, keepdims=True)
            total[p, 1] = jnp.sum(jnp.where(first, 0.0, prod), axis=1, keepdims=True)
        l_ref[...] = jnp.zeros_like(l_ref)
        e_ref[...] = jnp.zeros_like(e_ref)
        dq_acc[...] = jnp.zeros_like(dq_acc)

        def blocks(todo):
            chains = [(b, p, h) for b in range(len(todo)) for p, h in heads]
            starts = [pl.multiple_of(kb * t, t) for kb, _ in todo]
            ks = {(b, p): k_ref[p, pl.ds(st, t), :] for b, st in enumerate(starts) for p in range(g)}
            vs = {(b, p): v_ref[p, pl.ds(st, t), :] for b, st in enumerate(starts) for p in range(g)}
            ar = {(b, p, h): _sb_logits(qh[p, h], ks[b, p], todo[b][1]) for b, p, h in chains}
            dw = {(b, p, h): _dot(doh[p, h], vs[b, p], 1, 1) for b, p, h in chains}
            later = {c: _split_mm(ar[c][1], after) for c in chains}
            carry = {(p, h): l_ref[2 * p + h] for p, h in heads}
            wb = {}
            for b, (_, mask) in enumerate(todo):
                for p, h in heads:
                    wc = jnp.exp(ar[b, p, h][0] + later[b, p, h] + carry[p, h])
                    wb[b, p, h] = (wc if mask is None else jnp.where(mask, wc, 0.0)).astype(BF16)
                carry = {(p, h): carry[p, h] + jnp.sum(ar[b, p, h][1], axis=1, keepdims=True) for p, h in heads}
            dvs = {(b, p, h): _dot(wb[b, p, h], do_[p], 0, 0) for b, p, h in chains}
            e = {c: dw[c] * wb[c].astype(F32) for c in chains}
            suffix = {c: _split_mm(e[c], from_here) for c in chains}
            e_carry = {(p, h): e_ref[2 * p + h] for p, h in heads}
            dz = {}
            for b, (_, mask) in enumerate(todo):
                for p, h in heads:
                    before = total[p, h] - (suffix[b, p, h] + e_carry[p, h])
                    dzc = e[b, p, h] - jnp.exp(ar[b, p, h][0]) * (e[b, p, h] + before)
                    dz[b, p, h] = (dzc if mask is None else jnp.where(mask, dzc, 0.0)).astype(BF16)
                e_carry = {(p, h): e_carry[p, h] + jnp.sum(e[b, p, h], axis=1, keepdims=True) for p, h in heads}
            dqs = {(b, p, h): _dot(dz[b, p, h], ks[b, p], 1, 0) for b, p, h in chains}
            dks = {(b, p, h): _dot(dz[b, p, h], q[p], 0, 0) for b, p, h in chains}
            for p in range(g):
                dq = dq_acc[:, lanes[p]]
                for b, st in enumerate(starts):
                    dq = dq + jnp.where(first, dqs[b, p, 0], dqs[b, p, 1])
                    dk_acc[pl.ds(st, t), lanes[p]] += jnp.where(first, dks[b, p, 0], dks[b, p, 1])
                    dv_acc[pl.ds(st, t), lanes[p]] += jnp.where(first, dvs[b, p, 0], dvs[b, p, 1])
                dq_acc[:, lanes[p]] = dq
            for p, h in heads:
                l_ref[2 * p + h] = carry[p, h]
                e_ref[2 * p + h] = e_carry[p, h]

        _sb_walk(i, blocks, l_ref, causal)
        dq_ref[...] = (dq_acc[...] * SB_SCALE).astype(dq_ref.dtype)

        @pl.when(i == nq - 1)
        def _():
            dk_ref[...] = dk_acc[...].astype(dk_ref.dtype)
            dv_ref[...] = dv_acc[...].astype(dv_ref.dtype)

        if comm is not None:
            pl.when(last_step)(lambda: comm.finish(*riding))

    once = pl.Buffered(1)
    q_spec, k_spec, v_spec = _sb_qkv_specs(s, g)
    k_spec = pl.BlockSpec(k_spec.block_shape, k_spec.index_map, pipeline_mode=once)
    v_spec = pl.BlockSpec(v_spec.block_shape, v_spec.index_map, pipeline_mode=once)
    blk = pl.BlockSpec((t, g * LANES), lambda p, i: (i, p))
    col_blk = pl.BlockSpec((s, g * LANES), lambda p, i: (0, p), pipeline_mode=once)
    sds = jax.ShapeDtypeStruct((s, SB_WIDTH), BF16)
    return pl.pallas_call(
        body, name="sb_bwd", grid=grid,
        in_specs=[q_spec, k_spec, v_spec, blk, blk] + c_in_specs,
        out_specs=[blk, col_blk, col_blk] + c_out_specs,
        out_shape=[sds, sds, sds] + c_out_shape,
        scratch_shapes=c_scratch + [pltpu.VMEM((2 * g, t, 1), F32), pltpu.VMEM((2 * g, t, 1), F32),
                                    pltpu.VMEM((t, g * LANES), F32), pltpu.VMEM((s, g * LANES), F32),
                                    pltpu.VMEM((s, g * LANES), F32)],
        compiler_params=_cparams(),
    )(qkv, qkv, qkv, o, do, *c_ins)


def _ret_log_gamma():
    lg = np.log1p(-np.exp2(-5.0 - np.arange(RET_HEADS, dtype=np.float32))).astype(np.float32)
    return jnp.asarray(np.broadcast_to(lg[:, None, None], (RET_HEADS, 8, LANES)).copy())


RET_SCRATCH = [pltpu.VMEM((RET_HEADS, RET_QK, RET_V), F32),
               pltpu.VMEM((RET_HEADS, RET_BLOCK, RET_BLOCK), F32),
               pltpu.VMEM((RET_HEADS, RET_BLOCK, 1), F32),
               pltpu.VMEM((RET_HEADS, RET_BLOCK, 1), F32)]


def _ret_begin(n, lg_ref, state, within, q_dec, k_dec):
    @pl.when(n == 0)
    def _():
        c = RET_BLOCK
        state[...] = jnp.zeros_like(state)
        row = lax.broadcasted_iota(jnp.int32, (c, c), 0)
        col = lax.broadcasted_iota(jnp.int32, (c, c), 1)
        rel = jnp.maximum(row - col, 0).astype(F32)
        idx = lax.broadcasted_iota(jnp.int32, (c, 1), 0).astype(F32)
        for h in range(RET_HEADS):
            lg = lg_ref[h, 0:1, 0:1]
            within[h] = jnp.where(row >= col, jnp.exp(lg * rel), 0.0)
            q_dec[h] = jnp.exp(lg * (idx + 1.0))
            k_dec[h] = jnp.exp(lg * (c - 1.0 - idx))


def _chunk_decay(lg_ref, h):
    return jnp.exp(lg_ref[h, 0:1, 0:1] * float(RET_BLOCK))


def _ret_heads(x, width):
    return [x[:, h * width:(h + 1) * width] for h in range(RET_HEADS)]


def _ret_specs(s, reverse=False):
    c = RET_BLOCK
    per_step = min(RET_CHUNKS_PER_STEP, s // c)
    rows = c * per_step
    nc = s // rows
    pos = (lambda n: nc - 1 - n) if reverse else (lambda n: n)
    chunks = [slice(u * c, (u + 1) * c) for u in range(per_step)]
    q_spec = pl.BlockSpec((rows, RET_QK_WIDTH), lambda n: (pos(n), 0))
    k_spec = pl.BlockSpec((rows, RET_QK_WIDTH), lambda n: (pos(n), 1))
    v_spec = pl.BlockSpec((rows, RET_V_WIDTH), lambda n: (pos(n), 0))
    lg_spec = pl.BlockSpec((RET_HEADS, 8, LANES), lambda n: (0, 0, 0))
    rope_spec = pl.BlockSpec((rows, RET_QK), lambda n: (pos(n), 0))
    return nc, chunks[::-1] if reverse else chunks, q_spec, k_spec, v_spec, lg_spec, rope_spec


def _ret_fwd(rqk, rvg, s):
    nc, chunks, q_spec, k_spec, v_spec, lg_spec, _ = _ret_specs(s)
    g_spec = pl.BlockSpec(v_spec.block_shape, lambda n: (n, 1))
    heads = range(RET_HEADS)

    def body(q_ref, k_ref, v_ref, g_ref, lg_ref, r_ref, y_ref, state, within, q_dec, k_dec):
        n = pl.program_id(0)
        _ret_begin(n, lg_ref, state, within, q_dec, k_dec)
        for rows in chunks:
            q, k = _ret_heads(q_ref[rows], RET_QK), _ret_heads(k_ref[rows], RET_QK)
            v, g = _ret_heads(v_ref[rows], RET_V), _ret_heads(g_ref[rows], RET_V)
            scores = [_dot(q[h].astype(BF16), k[h].astype(BF16), 1, 1) * within[h] for h in heads]
            cross = [_dot((q[h] * q_dec[h]).astype(BF16), state[h].astype(BF16), 1, 0) for h in heads]
            out = [_dot(scores[h].astype(BF16), v[h], 1, 0) + cross[h] for h in heads]
            grown = [_dot((k[h] * k_dec[h]).astype(BF16), v[h], 0, 0) for h in heads]
            for h in heads:
                sl = slice(h * RET_V, (h + 1) * RET_V)
                r_ref[rows, sl] = out[h]
                xhat, _ = _norm(out[h])
                gh = g[h].astype(F32)
                y_ref[rows, sl] = (gh * _sigmoid(gh) * xhat).astype(y_ref.dtype)
                state[h] = state[h] * _chunk_decay(lg_ref, h) + grown[h]

    return pl.pallas_call(
        body, name="ret_fwd", grid=(nc,),
        in_specs=[q_spec, k_spec, v_spec, g_spec, lg_spec],
        out_specs=[v_spec, v_spec],
        out_shape=[jax.ShapeDtypeStruct((s, RET_V_WIDTH), F32), jax.ShapeDtypeStruct((s, RET_V_WIDTH), BF16)],
        scratch_shapes=RET_SCRATCH,
        compiler_params=_cparams(),
    )(rqk, rqk, rvg, rvg, _ret_log_gamma())


def _rope_bwd(d, cos, sin):
    return d * cos + _swap_halves(d * sin)


def _ret_bwd_q(rqk, rv, d_out, cos2, sin2, s):
    nc, chunks, q_spec, k_spec, v_spec, lg_spec, rope_spec = _ret_specs(s)
    heads = range(RET_HEADS)

    def body(k_ref, v_ref, d_ref, lg_ref, cos_ref, sin_ref, dq_ref, state, within, q_dec, k_dec):
        n = pl.program_id(0)
        _ret_begin(n, lg_ref, state, within, q_dec, k_dec)
        for rows in chunks:
            k = _ret_heads(k_ref[rows], RET_QK)
            v, d = _ret_heads(v_ref[rows], RET_V), _ret_heads(d_ref[rows], RET_V)
            cos, sin = cos_ref[rows], sin_ref[rows]
            d_scores = [_dot(d[h], v[h], 1, 1) * within[h] for h in heads]
            cross = [q_dec[h] * _dot(d[h], state[h].astype(BF16), 1, 1) for h in heads]
            dq = [_dot(d_scores[h].astype(BF16), k[h].astype(BF16), 1, 0) + cross[h] for h in heads]
            grown = [_dot((k[h] * k_dec[h]).astype(BF16), v[h], 0, 0) for h in heads]
            for h in heads:
                sl = slice(h * RET_QK, (h + 1) * RET_QK)
                dq_ref[rows, sl] = (_rope_bwd(dq[h], cos, sin) * RET_SCALE).astype(dq_ref.dtype)
                state[h] = state[h] * _chunk_decay(lg_ref, h) + grown[h]

    return pl.pallas_call(
        body, name="ret_bwd_q", grid=(nc,),
        in_specs=[k_spec, v_spec, v_spec, lg_spec, rope_spec, rope_spec],
        out_specs=q_spec,
        out_shape=jax.ShapeDtypeStruct((s, RET_QK_WIDTH), BF16),
        scratch_shapes=RET_SCRATCH,
        compiler_params=_cparams(),
    )(rqk, rv, d_out, _ret_log_gamma(), cos2, sin2)


def _ret_bwd_kv(rqk, rv, d_out, cos2, sin2, s):
    nc, chunks, q_spec, k_spec, v_spec, lg_spec, rope_spec = _ret_specs(s, reverse=True)
    heads = range(RET_HEADS)

    def body(q_ref, k_ref, v_ref, d_ref, lg_ref, cos_ref, sin_ref, dk_ref, dv_ref, state, within, q_dec, k_dec):
        n = pl.program_id(0)
        _ret_begin(n, lg_ref, state, within, q_dec, k_dec)
        for rows in chunks:
            q, k = _ret_heads(q_ref[rows], RET_QK), _ret_heads(k_ref[rows], RET_QK)
            v, d = _ret_heads(v_ref[rows], RET_V), _ret_heads(d_ref[rows], RET_V)
            cos, sin = cos_ref[rows], sin_ref[rows]
            qb, kb = [q[h].astype(BF16) for h in heads], [k[h].astype(BF16) for h in heads]
            st = [state[h].astype(BF16) for h in heads]
            scores = [_dot(qb[h], kb[h], 1, 1) * within[h] for h in heads]
            d_scores = [_dot(d[h], v[h], 1, 1) * within[h] for h in heads]
            dk = [_dot(d_scores[h].astype(BF16), qb[h], 0, 0) + k_dec[h] * _dot(v[h], st[h], 1, 1) for h in heads]
            dv = [_dot(scores[h].astype(BF16), d[h], 0, 0) + k_dec[h] * _dot(kb[h], st[h], 1, 0) for h in heads]
            grown = [_dot((q[h] * q_dec[h]).astype(BF16), d[h], 0, 0) for h in heads]
            for h in heads:
                dk_ref[rows, h * RET_QK:(h + 1) * RET_QK] = _rope_bwd(dk[h], cos, sin).astype(dk_ref.dtype)
                dv_ref[rows, h * RET_V:(h + 1) * RET_V] = dv[h].astype(dv_ref.dtype)
                state[h] = state[h] * _chunk_decay(lg_ref, h) + grown[h]

    return pl.pallas_call(
        body, name="ret_bwd_kv", grid=(nc,),
        in_specs=[q_spec, k_spec, v_spec, v_spec, lg_spec, rope_spec, rope_spec],
        out_specs=[q_spec, v_spec],
        out_shape=[jax.ShapeDtypeStruct((s, RET_QK_WIDTH), BF16), jax.ShapeDtypeStruct((s, RET_V_WIDTH), BF16)],
        scratch_shapes=RET_SCRATCH,
        compiler_params=_cparams(),
    )(rqk, rqk, rv, d_out, _ret_log_gamma(), cos2, sin2)


def _xattn_probs(scores):
    sc = scores - jnp.max(scores, axis=-1, keepdims=True)
    p = jnp.exp(sc)
    return p / jnp.sum(p, axis=-1, keepdims=True)


def _xattn_heads(q_ref, kv_ref):
    sls = [slice(h * MEM_DIM, (h + 1) * MEM_DIM) for h in range(MEM_HEADS)]
    q = [q_ref[:, sl] for sl in sls]
    k = [kv_ref[:, sl] for sl in sls]
    v = [kv_ref[:, D_MODEL + h * MEM_DIM:D_MODEL + (h + 1) * MEM_DIM] for h in range(MEM_HEADS)]
    return sls, q, k, v


def _xattn_fwd(qm, kv, s):
    tq = min(XATTN_ROWS, s)
    heads = range(MEM_HEADS)

    def body(q_ref, kv_ref, o_ref):
        sls, q, k, v = _xattn_heads(q_ref, kv_ref)
        scores = [_dot(q[h], k[h], 1, 1) for h in heads]
        p = [_xattn_probs(scores[h]).astype(BF16) for h in heads]
        out = [_dot(p[h], v[h], 1, 0) for h in heads]
        for h in heads:
            o_ref[:, sls[h]] = out[h].astype(o_ref.dtype)

    return pl.pallas_call(
        body, name="xattn_fwd", grid=(s // tq,),
        in_specs=[pl.BlockSpec((tq, D_MODEL), lambda i: (i, 0)),
                  pl.BlockSpec((MEM_LEN, 2 * D_MODEL), lambda i: (0, 0))],
        out_specs=pl.BlockSpec((tq, D_MODEL), lambda i: (i, 0)),
        out_shape=jax.ShapeDtypeStruct((s, D_MODEL), BF16),
        compiler_params=_cparams(),
    )(qm, kv)


def _xattn_bwd(qm, kv, do, s):
    tq = min(XATTN_ROWS, s)

    def body(q_ref, kv_ref, do_ref, dq_ref, dkv_ref):
        i = pl.program_id(0)

        @pl.when(i == 0)
        def _():
            dkv_ref[...] = jnp.zeros_like(dkv_ref)

        heads = range(MEM_HEADS)
        sls, q, k, v = _xattn_heads(q_ref, kv_ref)
        d = [do_ref[:, sl] for sl in sls]
        scores = [_dot(q[h], k[h], 1, 1) for h in heads]
        dp = [_dot(d[h], v[h], 1, 1) for h in heads]
        p = [_xattn_probs(scores[h]) for h in heads]
        ds = [(p[h] * (dp[h] - jnp.sum(p[h] * dp[h], axis=-1, keepdims=True))).astype(BF16) for h in heads]
        dq = [_dot(ds[h], k[h], 1, 0) for h in heads]
        dk = [_dot(ds[h], q[h], 0, 0) for h in heads]
        dv = [_dot(p[h].astype(BF16), d[h], 0, 0) for h in heads]
        for h in heads:
            dq_ref[:, sls[h]] = (dq[h] * MEM_SCALE).astype(dq_ref.dtype)
            dkv_ref[:, sls[h]] += dk[h]
            dkv_ref[:, D_MODEL + h * MEM_DIM:D_MODEL + (h + 1) * MEM_DIM] += dv[h]

    row_blk = pl.BlockSpec((tq, D_MODEL), lambda i: (i, 0))
    kv_blk = pl.BlockSpec((MEM_LEN, 2 * D_MODEL), lambda i: (0, 0))
    return pl.pallas_call(
        body, name="xattn_bwd", grid=(s // tq,),
        in_specs=[row_blk, kv_blk, row_blk],
        out_specs=[row_blk, kv_blk],
        out_shape=[jax.ShapeDtypeStruct((s, D_MODEL), BF16), jax.ShapeDtypeStruct((MEM_LEN, 2 * D_MODEL), F32)],
        compiler_params=_cparams(),
    )(qm, kv, do)


def _place():
    x, y, c = lax.axis_index("x"), lax.axis_index("y"), lax.axis_index("c")
    others = [(1 - x, y), (x, 1 - y), (1 - x, 1 - y)]
    return x, y, c, others


def _slab(ref, axis, chip, size):
    start = pl.multiple_of(chip * size, LANES if axis == 1 else 16)
    if axis == 0:
        return ref.at[pl.ds(start, size), :]
    return ref.at[:, pl.ds(start, size)]


class _CommPlan:
    def __init__(self, ins, out_shape, scratch, start, finish):
        self.ins, self.out_shape, self.scratch, self.start, self.finish = ins, out_shape, scratch, start, finish

    @property
    def specs(self):
        any_spec = pl.BlockSpec(memory_space=pl.ANY)
        return [any_spec] * len(self.ins), [any_spec] * len(self.out_shape)


def _gather_plan(names, shards):
    spec = {name: (shape, axis) for name, shape, axis in BIG}
    nw = len(names)

    def shard_half(ref, c):
        rows = ref.shape[0] // 2
        return ref.at[pl.ds(pl.multiple_of(c * rows, 16), rows), :]

    def region(ref, w, chip, c):
        shape, axis = spec[names[w]]
        size = shape[axis] // N_CHIPS
        if axis == 0:
            rows = size // 2
            return ref.at[pl.ds(pl.multiple_of(chip * size + c * rows, 16), rows), :]
        rows = shape[0] // 2
        return ref.at[pl.ds(pl.multiple_of(c * rows, 16), rows), pl.ds(pl.multiple_of(chip * size, LANES), size)]

    def ops(shard, full, sems):
        ici_send, ici_recv, d2d_send, d2d_recv, local_sems = sems
        x, y, c, others = _place()
        mine, sibling = 2 * x + y, (x, y, 1 - c)
        local, over_ici, arrived, passed_on, from_sibling = [], [], [], [], []
        for w in range(nw):
            shape, axis = spec[names[w]]
            local.append(pltpu.make_async_copy(shard[w], _slab(full[w], axis, mine, shape[axis] // N_CHIPS),
                                               local_sems.at[w]))
            for t, (qx, qy) in enumerate(others):
                n, theirs = 3 * w + t, 2 * qx + qy
                over_ici.append(pltpu.make_async_remote_copy(
                    src_ref=shard_half(shard[w], c), dst_ref=region(full[w], w, mine, c),
                    send_sem=ici_send.at[n], recv_sem=ici_recv.at[n], device_id=(qx, qy, c), device_id_type=MESH))
                arrived.append(pltpu.make_async_remote_copy(
                    src_ref=shard_half(shard[w], c), dst_ref=region(full[w], w, theirs, c),
                    send_sem=ici_send.at[n], recv_sem=ici_recv.at[n], device_id=(qx, qy, c), device_id_type=MESH))
                passed_on.append(pltpu.make_async_remote_copy(
                    src_ref=region(full[w], w, theirs, c), dst_ref=region(full[w], w, theirs, c),
                    send_sem=d2d_send.at[n], recv_sem=d2d_recv.at[n], device_id=sibling, device_id_type=MESH))
                from_sibling.append(pltpu.make_async_remote_copy(
                    src_ref=region(full[w], w, theirs, c), dst_ref=region(full[w], w, theirs, 1 - c),
                    send_sem=d2d_send.at[n], recv_sem=d2d_recv.at[n], device_id=sibling, device_id_type=MESH))
        return local, over_ici, arrived, passed_on, from_sibling

    def start(shard, full, sems):
        local, over_ici, _, _, _ = ops(shard, full, sems)
        for cp in local + over_ici:
            cp.start()

    def finish(shard, full, sems):
        local, over_ici, arrived, passed_on, from_sibling = ops(shard, full, sems)
        for got, onward in zip(arrived, passed_on, strict=True):
            got.wait_recv()
            onward.start()
        for got in from_sibling:
            got.wait_recv()
        for cp in over_ici + passed_on:
            cp.wait_send()
        for cp in local:
            cp.wait()

    dma = pltpu.SemaphoreType.DMA
    return _CommPlan(
        ins=[shards[name] for name in names],
        out_shape=[jax.ShapeDtypeStruct(spec[name][0], BF16) for name in names],
        scratch=[dma((3 * nw,)), dma((3 * nw,)), dma((3 * nw,)), dma((3 * nw,)), dma((nw,))],
        start=start, finish=finish)


def _shard_shape(shape, axis):
    return tuple(d // N_CHIPS if a == axis else d for a, d in enumerate(shape))


def _exchange_plan(names, grads):
    spec = {name: (shape, axis) for name, shape, axis in BIG}
    nw = len(names)

    def ops(grad, stack, sems):
        send_sems, recv_sems, local_sems = sems
        x, y, c, others = _place()
        mine = 2 * x + y
        me, sibling = (x, y, c), (x, y, 1 - c)

        def dev(px, py, pc):
            return 4 * px + 2 * py + pc

        def copy(w, n, src, slot, to):
            return pltpu.make_async_remote_copy(
                src_ref=src, dst_ref=stack[w].at[slot], send_sem=send_sems.at[7 * w + n],
                recv_sem=recv_sems.at[7 * w + n], device_id=to, device_id_type=MESH)

        local, first, arrived, passed_on, from_sibling = [], [], [], [], []
        for w in range(nw):
            shape, axis = spec[names[w]]
            size = shape[axis] // N_CHIPS
            own = _slab(grad[w], axis, mine, size)
            local.append(pltpu.make_async_copy(own, stack[w].at[dev(*me)], local_sems.at[w]))
            first.append(copy(w, 0, own, dev(*me), sibling))
            from_sibling.append(copy(w, 0, own, dev(*sibling), me))
            for t, (qx, qy) in enumerate(others):
                got = stack[w].at[dev(qx, qy, c)]
                first.append(copy(w, 1 + t, _slab(grad[w], axis, 2 * qx + qy, size), dev(*me), (qx, qy, c)))
                arrived.append(copy(w, 1 + t, got, dev(qx, qy, c), me))
                passed_on.append(copy(w, 4 + t, got, dev(qx, qy, c), sibling))
                from_sibling.append(copy(w, 4 + t, got, dev(qx, qy, 1 - c), me))
        return local, first, arrived, passed_on, from_sibling

    def start(grad, stack, sems):
        local, first, _, _, _ = ops(grad, stack, sems)
        for cp in local + first:
            cp.start()

    def finish(grad, stack, sems):
        local, first, arrived, passed_on, from_sibling = ops(grad, stack, sems)
        for got, onward in zip(arrived, passed_on, strict=True):
            got.wait_recv()
            onward.start()
        for got in from_sibling:
            got.wait_recv()
        for cp in first + passed_on:
            cp.wait_send()
        for cp in local:
            cp.wait()

    dma = pltpu.SemaphoreType.DMA
    return _CommPlan(
        ins=[grads[name] for name in names],
        out_shape=[jax.ShapeDtypeStruct((N_DEV,) + _shard_shape(*spec[name]), BF16) for name in names],
        scratch=[dma((7 * nw,)), dma((7 * nw,)), dma((nw,))],
        start=start, finish=finish)


def _adamw(w, g, m, v):
    m = ADAM_B1 * m + (1.0 - ADAM_B1) * g
    v = ADAM_B2 * v + (1.0 - ADAM_B2) * (g * g)
    m_hat = m / (1.0 - ADAM_B1 ** ADAM_STEP)
    v_hat = v / (1.0 - ADAM_B2 ** ADAM_STEP)
    delta = -ADAM_LR * (m_hat / (jnp.sqrt(v_hat) + ADAM_EPS) + ADAM_WD * w)
    return delta, m, v


def _reduce_adamw(name, stack, w, m, v):
    rows, cols = w.shape
    tr = next(t for t in (256, 128, 64) if rows % t == 0)

    def body(s_ref, w_ref, m_ref, v_ref, g_ref, d_ref, nm_ref, nv_ref):
        g = s_ref[0].astype(F32)
        for d in range(1, N_DEV):
            g = g + s_ref[d].astype(F32)
        g_ref[...] = g
        d_ref[...], nm_ref[...], nv_ref[...] = _adamw(w_ref[...], g, m_ref[...], v_ref[...])

    blk = pl.BlockSpec((tr, cols), lambda i: (i, 0))
    return pl.pallas_call(
        body, name=name, grid=(rows // tr,),
        in_specs=[pl.BlockSpec((N_DEV, tr, cols), lambda i: (0, i, 0)), blk, blk, blk],
        out_specs=[blk] * 4, out_shape=[jax.ShapeDtypeStruct((rows, cols), F32)] * 4,
        compiler_params=_cparams(),
    )(stack, w, m, v)


def _small_step(pack, w, m, v):
    def body(p_ref, w_ref, m_ref, v_ref, g_ref, d_ref, nm_ref, nv_ref, loss_ref, all_ref, send_sems, recv_sems):
        x, y, c, _ = _place()
        me = 4 * x + 2 * y + c
        all_ref[me] = p_ref[...]
        sent = []
        for n in range(1, N_DEV):
            peer = me ^ n
            cp = pltpu.make_async_remote_copy(
                src_ref=p_ref, dst_ref=all_ref.at[me], send_sem=send_sems.at[n - 1], recv_sem=recv_sems.at[n - 1],
                device_id=(peer // 4, (peer // 2) % 2, peer % 2), device_id_type=MESH)
            cp.start()
            sent.append(cp)
        for n in range(1, N_DEV):
            peer = me ^ n
            pltpu.make_async_remote_copy(
                src_ref=p_ref, dst_ref=all_ref.at[peer], send_sem=send_sems.at[n - 1], recv_sem=recv_sems.at[n - 1],
                device_id=(peer // 4, (peer // 2) % 2, peer % 2), device_id_type=MESH).wait_recv()
        for cp in sent:
            cp.wait_send()
        tot = all_ref[0]
        for d in range(1, N_DEV):
            tot = tot + all_ref[d]
        g = tot[:SMALL_ROWS]
        g_ref[...] = g
        d_ref[...], nm_ref[...], nv_ref[...] = _adamw(w_ref[...], g, m_ref[...], v_ref[...])
        loss_ref[...] = jnp.sum(jnp.sum(tot[SMALL_ROWS:], axis=1, keepdims=True), axis=0, keepdims=True)

    vm = pl.BlockSpec(memory_space=pltpu.VMEM)
    small = jax.ShapeDtypeStruct((SMALL_ROWS, LANES), F32)
    return pl.pallas_call(
        body, name="small_step",
        in_specs=[vm] * 4, out_specs=[vm] * 5,
        out_shape=[small] * 4 + [jax.ShapeDtypeStruct((1, 1), F32)],
        scratch_shapes=[pltpu.VMEM((N_DEV, PACK_ROWS, LANES), F32),
                        pltpu.SemaphoreType.DMA((N_DEV - 1,)), pltpu.SemaphoreType.DMA((N_DEV - 1,))],
    )(pack, w, m, v)


LATER_WEIGHTS = tuple(name for name, _, _ in BIG if name != "w_in")


def _layer_step(x, mem, tgt, shards, vec):
    s = x.shape[0]
    d = D_MODEL
    tm = min(ROW_TILE, s)
    tl = min(WIDE_TILE, s)
    xb, cos2, sin2, w_in = _prep(x, _gather_plan(("w_in",), shards))
    bf = lambda w: ((s, w), BF16)
    f32 = lambda w: ((s, w), F32)

    w_sb, w_rqk = w_in[:, :OFF_RET_Q], w_in[:, OFF_RET_Q:OFF_RET_V]
    w_rvg, w_gate = w_in[:, OFF_RET_V:OFF_GATE], w_in[:, OFF_GATE:]
    q_scale = lambda width, q_width, scale: jnp.concatenate(
        [jnp.full((1, q_width), scale, F32), jnp.ones((1, width - q_width), F32)], axis=1)
    n_groups = 3 * SB_WIDTH // LANES

    def sb_epi(acc, t, i, j):
        scaled = acc * t[0]
        return [jnp.stack([scaled[:, g * LANES:(g + 1) * LANES] for g in range(n_groups)])], []

    (sb_qkv,) = _mm(
        "in_sb", xb, w_sb, s, 3 * SB_WIDTH, d, tm=tl, tn=3 * SB_WIDTH, tk=d, epi=sb_epi,
        ins=[(q_scale(3 * SB_WIDTH, SB_WIDTH, SB_SCALE), *_rowvec(3 * SB_WIDTH))],
        outs=[((n_groups, s, LANES), BF16, (n_groups, tl, LANES), lambda i, j: (0, i, 0))])

    def rope_epi(acc, t, i, j):
        cos, sin, scale = t
        parts = []
        for g in range(acc.shape[1] // RET_QK):
            xg = acc[:, g * RET_QK:(g + 1) * RET_QK]
            parts.append(xg * cos + _swap_halves(xg) * sin)
        return [jnp.concatenate(parts, axis=1) * scale], []

    rope_in = ((tl, RET_QK), lambda i, j: (i, 0))
    (rqk,) = _mm("in_rqk", xb, w_rqk, s, 2 * RET_QK_WIDTH, d, tm=tl, tn=2 * RET_QK_WIDTH, tk=d, epi=rope_epi,
                 chunk=MXU_COLS,
                 ins=[(cos2, *rope_in), (sin2, *rope_in),
                      (q_scale(2 * RET_QK_WIDTH, RET_QK_WIDTH, RET_SCALE), *_rowvec(2 * RET_QK_WIDTH))],
                 outs=[(*f32(2 * RET_QK_WIDTH), *_tile(tl, 2 * RET_QK_WIDTH))])
    (rvg,) = _mm("in_rvg", xb, w_rvg, s, 2 * RET_V_WIDTH, d, tm=tl, tn=2 * RET_V_WIDTH, tk=d, chunk=MXU_COLS,
                 epi=_plain, outs=[(*bf(2 * RET_V_WIDTH), *_tile(tl, 2 * RET_V_WIDTH))])
    (gates,) = _mm("in_gate", xb, w_gate, s, 2 * d, d, tm=tl, tn=2 * d, tk=d, chunk=MXU_COLS,
                   epi=lambda acc, t, i, j: ([_sigmoid(acc + t[0])], []),
                   ins=[(vec["b_gate"], *_rowvec(2 * d))], outs=[(*bf(2 * d), *_tile(tl, 2 * d))])

    sb_out, sb_out_f32, *gathered = _sb_fwd(sb_qkv, s, comm=_gather_plan(LATER_WEIGHTS, shards))
    wt = dict(zip(LATER_WEIGHTS, gathered, strict=True))
    ret, gated = _ret_fwd(rqk, rvg, s)
    (y_sb,) = _mm("sb_o", sb_out, wt["w_sb_o"], s, d, SB_WIDTH, tm=tl, tn=d, tk=SB_WIDTH, epi=_plain,
                  outs=[(*bf(d), *_tile(tl, d))])
    y_ret, mixin = _mm(
        "ret_o", gated, wt["w_ret_o"], s, d, RET_V_WIDTH, tm=tl, tn=d, tk=RET_V_WIDTH, chunk=MXU_COLS,
        epi=lambda acc, t, i, j: ([acc, t[0].astype(F32) * t[2].astype(F32) + t[1].astype(F32) * acc], []),
        ins=[(gates, *_tile(tl, d)), (gates, *_tile(tl, d, 1)), (y_sb, *_tile(tl, d))],
        outs=[(*bf(d), *_tile(tl, d)), (*bf(d), *_tile(tl, d))])

    def ln_epi(acc, t, i, j):
        *res, g, b = t
        prev = res[0] if len(res) == 1 else res[0] * res[1] + res[2]
        xhat, rstd = _norm(DN_ALPHA * prev + acc)
        return [xhat * g + b, xhat, rstd], []

    full = _tile(tm, d)
    col1 = ((tm, 1), lambda i, j: (i, 0))
    vec_in = lambda name: (vec[name], *_rowvec(d))
    ln_outs = [(*bf(d), *full), (*f32(d), *full), ((s, 1), F32, *col1)]
    def ln_query_epi(acc, t, i, j):
        *rest, w_query = t
        (x1, xhat, rstd), _ = ln_epi(acc, rest, i, j)
        x1 = x1.astype(BF16)
        return [x1, xhat, rstd, _dot(x1, w_query, 1, 0) * MEM_SCALE], []

    x1b, xhat1, rstd1, qm = _mm(
        "mix_o", mixin, wt["w_mix_o"], s, d, d, tm=tm, tn=d, tk=d, epi=ln_query_epi,
        ins=[(x, *full), vec_in("ln1_g"), vec_in("ln1_b"), (wt["w_mem_q"], (d, d), lambda i, j: (0, 0))],
        outs=ln_outs + [(*bf(d), *full)])
    (kv,) = _mm("mem_kv", mem, wt["w_mem_kv"], MEM_LEN, 2 * d, d, tm=MEM_LEN, tn=d, tk=d, epi=_plain,
                outs=[((MEM_LEN, 2 * d), BF16, *_tile(MEM_LEN, d))])
    att = _xattn_fwd(qm, kv, s)
    x2b, xhat2, rstd2 = _mm(
        "mem_o", att, wt["w_mem_o"], s, d, d, tm=tm, tn=d, tk=d, epi=ln_epi,
        ins=[(xhat1, *full), vec_in("ln1_g"), vec_in("ln1_b"), vec_in("ln2_g"), vec_in("ln2_b")], outs=ln_outs)

    fh = FFN_HIDDEN
    tf = fh // 2
    def swiglu_epi(acc, t, i, j):
        a = acc[0].astype(BF16).astype(F32)
        return [a, acc[1], a * _sigmoid(a) * acc[1]], []

    f1, f2, act = _mm(
        "ffn_in", x2b, wt["w_ffn_in"], s, 2 * fh, d, tm=tm, tn=2 * fh, tk=d, epi=swiglu_epi, chunk=MXU_COLS,
        halves=True, outs=[(*bf(fh), *_tile(tm, fh))] * 3)

    def head_epi(acc, t, i, j):
        prev_hat, prev_g, prev_b, g, b, target = t
        xhat, rstd = _norm(DN_ALPHA * (prev_hat * prev_g + prev_b) + acc)
        err = xhat * g + b - target
        dy = err * (1.0 / d)
        du = _norm_bwd(dy * g, xhat, rstd)
        return [du], [_colsum(dy * xhat), _colsum(dy), _colsum(err * err) * (0.5 / d)]

    vec_acc = ((1, d), F32)
    du3b, dg3, db3, loss_cols = _mm(
        "ffn_out", act, wt["w_ffn_out"], s, d, fh, tm=tm, tn=d, tk=fh, epi=head_epi,
        ins=[(xhat2, *full), vec_in("ln2_g"), vec_in("ln2_b"), vec_in("ln3_g"), vec_in("ln3_b"), (tgt, *full)],
        outs=[(*bf(d), *full)], accs=[vec_acc] * 3)

    grads = {}
    ts = min(SEQ_TILE, s)

    def wgrad(name, a, b, m, n, tm_, tn_, tk_=None):
        (g,) = _mm(name, a, b, m, n, a.shape[0], tm=tm_, tn=tn_, tk=tk_ or ts, ta=True, epi=_plain,
                   outs=[((m, n), BF16, *_tile(tm_, tn_))])
        return g

    def wgrad_wide(name, a, pieces, tm_):
        m, width = a.shape[1], sum(p.shape[1] for p in pieces)
        (g,) = _mm(name, a, pieces, m, width, s, tm=tm_, tn=width, tk=min(ROW_TILE, s // 2), ta=True, epi=_plain,
                   outs=[((m, width), BF16, *_tile(tm_, width))])
        return g

    def ffn_bwd_epi(acc, t, i, j):
        a, b = t[0].astype(F32), t[1].astype(F32)
        sg = _sigmoid(a)
        return [acc * b * (sg * (1.0 + a * (1.0 - sg))), acc * (a * sg)], []

    df1, df2 = _mm(
        "ffn_out_t", du3b, wt["w_ffn_out"], s, fh, d, tm=tm, tn=fh, tk=d, tb=True, epi=ffn_bwd_epi, chunk=MXU_COLS,
        ins=[(f1, *_tile(tm, fh)), (f2, *_tile(tm, fh))],
        outs=[(*bf(fh), *_tile(tm, fh)), (*bf(fh), *_tile(tm, fh))])
    grads["w_ffn_out"] = wgrad("g_ffn_out", act, du3b, fh, d, tf, d)
    grads["w_ffn_in"] = wgrad_wide("g_ffn_in", x2b, [df1, df2], d // 2)

    def ln_bwd(name, a, b, k, tk, b_off, more, scales, xhat, rstd, g):
        def epi(acc, t, i, j):
            *extra, xh, rs, gg = t
            dy = acc
            for e, sc in zip(extra, scales, strict=True):
                dy = dy + e.astype(F32) * sc
            return [_norm_bwd(dy * gg, xh, rs)], [_colsum(dy * xh), _colsum(dy)]

        return _mm(name, a, b, s, d, k, tm=tm, tn=d, tk=tk, tb=True, b_off=b_off, epi=epi,
                   ins=[(e, *full) for e in more] + [(xhat, *full), (rstd, *col1), (g, *_rowvec(d))],
                   outs=[(*bf(d), *full)], accs=[vec_acc] * 2)

    du2b, dg2, db2 = ln_bwd("ffn_in_t", [df1, df2], wt["w_ffn_in"], 2 * fh, 2 * fh, (0, 0), [du3b], [DN_ALPHA],
                            xhat2, rstd2, vec["ln2_g"])

    (datt,) = _mm("mem_o_t", du2b, wt["w_mem_o"], s, d, d, tm=tl, tn=d, tk=d, tb=True, epi=_plain,
                  outs=[(*bf(d), *_tile(tl, d))])
    grads["w_mem_o"] = wgrad("g_mem_o", att, du2b, d, d, d, d)
    dqm, dkv = _xattn_bwd(qm, kv, datt, s)
    grads["w_mem_q"] = wgrad("g_mem_q", x1b, dqm, d, d, d, d)
    grads["w_mem_kv"] = wgrad("g_mem_kv", mem, dkv, d, 2 * d, d, d, MEM_LEN)
    du1b, dg1, db1 = ln_bwd("mem_q_t", dqm, wt["w_mem_q"], d, d, (0, 0), [du2b], [DN_ALPHA],
                            xhat1, rstd1, vec["ln1_g"])

    def merge_bwd_epi(acc, t, i, j):
        g0, g1, ysb, yret = (v.astype(F32) for v in t)
        dgate0 = acc * ysb * (g0 * (1.0 - g0))
        dgate1 = acc * yret * (g1 * (1.0 - g1))
        return [dgate0, dgate1, acc * g0, acc * g1], [_colsum(dgate0), _colsum(dgate1)]

    dgate0, dgate1, dy_sb, dy_ret, dbg0, dbg1 = _mm(
        "mix_o_t", du1b, wt["w_mix_o"], s, d, d, tm=tm, tn=d, tk=d, tb=True, epi=merge_bwd_epi,
        ins=[(gates, *full), (gates, *_tile(tm, d, 1)), (y_sb, *full), (y_ret, *full)],
        outs=[(*bf(d), *full)] * 4, accs=[vec_acc] * 2)
    grads["w_mix_o"] = wgrad("g_mix_o", mixin, du1b, d, d, d, d)
    grads["w_sb_o"] = wgrad("g_sb_o", sb_out, dy_sb, SB_WIDTH, d, SB_WIDTH, d)
    grads["w_ret_o"] = wgrad("g_ret_o", gated, dy_ret, RET_V_WIDTH, d, RET_V_WIDTH, d)
    (dsb_out,) = _mm("sb_o_t", dy_sb, wt["w_sb_o"], s, SB_WIDTH, d, tm=tl, tn=SB_WIDTH, tk=d, tb=True, epi=_plain,
                     outs=[(*bf(SB_WIDTH), *_tile(tl, SB_WIDTH))])

    def gate_norm_bwd_epi(acc, t, i, j):
        r, g = t[0], t[1].astype(F32)
        drg, dret = [], []
        for h in range(acc.shape[1] // RET_V):
            sl = slice(h * RET_V, (h + 1) * RET_V)
            xhat, rstd = _norm(r[:, sl])
            gg, dd = g[:, sl], acc[:, sl]
            sg = _sigmoid(gg)
            drg.append(dd * xhat * (sg * (1.0 + gg * (1.0 - sg))))
            dret.append(_norm_bwd(dd * (gg * sg), xhat, rstd))
        return [jnp.concatenate(drg, axis=1), jnp.concatenate(dret, axis=1)], []

    drg, dret = _mm(
        "ret_o_t", dy_ret, wt["w_ret_o"], s, RET_V_WIDTH, d, tm=tm, tn=d, tk=d, tb=True, epi=gate_norm_bwd_epi,
        chunk=MXU_COLS,
        ins=[(ret, *full), (rvg, *_tile(tm, d, 1))],
        outs=[(*bf(RET_V_WIDTH), *full)] * 2)

    drq = _ret_bwd_q(rqk, rvg, dret, cos2, sin2, s)
    drk, drv = _ret_bwd_kv(rqk, rvg, dret, cos2, sin2, s)
    dsq, dsk, dsv, *stacked = _sb_bwd(sb_qkv, sb_out_f32, dsb_out, s, comm=_exchange_plan(LATER_WEIGHTS, grads))
    stacks = dict(zip(LATER_WEIGHTS, stacked, strict=True))

    dh_mixers, dh_gates = [dsq, dsk, dsv, drq, drk, drv], [drg, dgate0, dgate1]
    grads["w_in"] = wgrad_wide("g_in", xb, dh_mixers + dh_gates, d // 2)
    grad_x, stacks["w_in"] = _mm(
        "in_t", dh_mixers + dh_gates, w_in, s, d, IN_WIDTH, tm=tm, tn=d, tk=IN_WIDTH, tb=True,
        epi=lambda acc, t, i, j: ([acc + DN_ALPHA * t[0].astype(F32)], []),
        ins=[(du1b, *full)], outs=[(*f32(d), *full)], comm=_exchange_plan(("w_in",), grads))

    small = {"b_gate": jnp.concatenate([dbg0, dbg1], axis=1), "ln1_g": dg1, "ln1_b": db1, "ln2_g": dg2,
             "ln2_b": db2, "ln3_g": dg3, "ln3_b": db3}
    return grad_x, stacks, small, loss_cols


def kernel(x, mem, w_in, b_gate, w_sb_o, w_ret_o, w_mix_o, ln1_g, ln1_b, w_mem_q, w_mem_kv, w_mem_o, ln2_g, ln2_b, w_ffn_in, w_ffn_out, ln3_g, ln3_b, loss_target, m_w_in, m_b_gate, m_w_sb_o, m_w_ret_o, m_w_mix_o, m_ln1_g, m_ln1_b, m_w_mem_q, m_w_mem_kv, m_w_mem_o, m_ln2_g, m_ln2_b, m_w_ffn_in, m_w_ffn_out, m_ln3_g, m_ln3_b, v_w_in, v_b_gate, v_w_sb_o, v_w_ret_o, v_w_mix_o, v_ln1_g, v_ln1_b, v_w_mem_q, v_w_mem_kv, v_w_mem_o, v_ln2_g, v_ln2_b, v_w_ffn_in, v_w_ffn_out, v_ln3_g, v_ln3_b):
    given = dict(locals())
    s = x.shape[1]
    x2d = x.reshape(s, D_MODEL)
    tgt = loss_target.reshape(s, D_MODEL)
    mem2d = mem.reshape(MEM_LEN, D_MODEL)
    shard = {name: given[name].reshape(_shard_shape(shape, axis)) for name, shape, axis in BIG}
    vec = {name: given[name] for name in SMALL}

    shards_bf = {name: _cast_bf16("cast_" + name, shard[name]) for name, _, _ in BIG}

    grad_x, stacks, small, loss_cols = _layer_step(x2d, mem2d, tgt, shards_bf, vec)

    out = {}
    for name, shape, axis in BIG:
        stack = stacks[name]
        shp = given[name].shape
        res = _reduce_adamw("adamw_" + name, stack, shard[name], given["m_" + name].reshape(stack.shape[1:]),
                            given["v_" + name].reshape(stack.shape[1:]))
        out[name] = [r.reshape(shp) for r in res]

    pack = jnp.concatenate([small[name] for name in SMALL] + [loss_cols], axis=1).reshape(PACK_ROWS, LANES)
    cat = lambda pre: jnp.concatenate([given[pre + name] for name in SMALL], axis=1).reshape(SMALL_ROWS, LANES)
    *res, loss = _small_step(pack, cat(""), cat("m_"), cat("v_"))
    flat = [r.reshape(1, SMALL_LEN) for r in res]
    off = 0
    for name in SMALL:
        n = given[name].shape[1]
        out[name] = [r[:, off:off + n] for r in flat]
        off += n

    return (loss.reshape(()), grad_x.reshape(x.shape),
            *[out[name][0] for name in WEIGHT_ORDER], *[out[name][1] for name in WEIGHT_ORDER],
            *[out[name][2] for name in WEIGHT_ORDER], *[out[name][3] for name in WEIGHT_ORDER])
```

```python
import functools

import jax
import jax.numpy as jnp
import numpy as np
from jax import lax
from jax.experimental import pallas as pl
from jax.experimental.pallas import tpu as pltpu

F32, BF16 = jnp.float32, jnp.bfloat16
MESH = pl.DeviceIdType.MESH

D_MODEL = 1024
MEM_LEN = 256
SB_HEADS, SB_DIM, SB_WIDTH = 8, 64, 512
RET_HEADS, RET_QK, RET_V = 4, 128, 256
RET_QK_WIDTH, RET_V_WIDTH = 512, 1024
ROPE_BASE = 10000.0
MEM_HEADS, MEM_DIM = 4, 256
FFN_HIDDEN = 2816
IN_WIDTH = 6656
OFF_RET_Q, OFF_RET_V, OFF_RET_G, OFF_GATE = 1536, 2560, 3584, 4608
DN_ALPHA = 2.0 ** 0.25
LN_EPS = 1e-5
SB_SCALE = SB_DIM ** -0.5
SB_DEAD = -110.0
RET_SCALE = RET_QK ** -0.5
MEM_SCALE = MEM_DIM ** -0.5
ADAM_LR, ADAM_B1, ADAM_B2, ADAM_EPS, ADAM_WD, ADAM_STEP = 0.001, 0.9, 0.999, 1e-08, 0.01, 10

N_DEV, N_CHIPS = 8, 4

LANES = 128
MXU_COLS = 256
VMEM_LIMIT_BYTES = 52 * 2 ** 20
ROW_TILE = 512
WIDE_TILE = 1024
SEQ_TILE = 2048
SB_BLOCK = 256
RET_BLOCK = 256
RET_CHUNKS_PER_STEP = 4
XATTN_ROWS = 1024

BIG = (
    ("w_in", (D_MODEL, IN_WIDTH), 1),
    ("w_sb_o", (SB_WIDTH, D_MODEL), 1),
    ("w_ret_o", (RET_V_WIDTH, D_MODEL), 0),
    ("w_mix_o", (D_MODEL, D_MODEL), 0),
    ("w_mem_q", (D_MODEL, D_MODEL), 0),
    ("w_mem_kv", (D_MODEL, 2 * D_MODEL), 1),
    ("w_mem_o", (D_MODEL, D_MODEL), 0),
    ("w_ffn_in", (D_MODEL, 2 * FFN_HIDDEN), 1),
    ("w_ffn_out", (FFN_HIDDEN, D_MODEL), 0),
)
SMALL = ("b_gate", "ln1_g", "ln1_b", "ln2_g", "ln2_b", "ln3_g", "ln3_b")
SMALL_LEN = 2 * D_MODEL + 6 * D_MODEL
SMALL_ROWS = SMALL_LEN // LANES
PACK_ROWS = SMALL_ROWS + D_MODEL // LANES
WEIGHT_ORDER = ("w_in", "b_gate", "w_sb_o", "w_ret_o", "w_mix_o", "ln1_g", "ln1_b", "w_mem_q", "w_mem_kv",
                "w_mem_o", "ln2_g", "ln2_b", "w_ffn_in", "w_ffn_out", "ln3_g", "ln3_b")


def _cparams():
    return pltpu.CompilerParams(vmem_limit_bytes=VMEM_LIMIT_BYTES)


def _dot(a, b, ca, cb):
    return lax.dot_general(a, b, (((ca,), (cb,)), ((), ())), preferred_element_type=F32)


def _sigmoid(x):
    return 1.0 / (1.0 + jnp.exp(-x))


def _mm(name, a, b, m, n, k, *, tm, tn, tk, epi, outs, ins=(), accs=(), ta=False, tb=False,
        a_off=(0, 0), b_off=(0, 0), j_outer=False, comm=None, chunk=None, halves=False):
    assert not halves or (chunk is not None and (tn // 2) % chunk == 0 and not ins), name
    assert m % tm == 0 and n % tn == 0 and k % tk == 0, (name, m, n, k, tm, tn, tk)
    assert chunk is None or (k == tk and tn % chunk == 0), name
    ni, nj, nk = m // tm, n // tn, k // tk
    assert not accs or nj == 1, name
    ij = (lambda g0, g1: (g1, g0)) if j_outer else (lambda g0, g1: (g0, g1))

    def spec(block, index):
        return pl.BlockSpec(block, lambda g0, g1, kk: index(*ij(g0, g1), kk))

    a_list = list(a) if isinstance(a, (list, tuple)) else [a]
    n_a = len(a_list)
    if n_a > 1:
        assert not ta and nk == 1 and chunk is None and not any(a_off), name
        assert sum(p.shape[1] for p in a_list) == k, name
        a_specs = [spec((tm, p.shape[1]), lambda i, j, kk: (i, 0)) for p in a_list]
    elif ta:
        a_specs = [spec((tk, tm), lambda i, j, kk: (kk + a_off[0], i + a_off[1]))]
    else:
        a_specs = [spec((tm, tk), lambda i, j, kk: (i + a_off[0], kk + a_off[1]))]
    b_list = list(b) if isinstance(b, (list, tuple)) else [b]
    n_b = len(b_list)
    if n_b > 1:
        assert not tb and nj == 1 and nk > 1 and n_a == 1 and chunk is None and not any(b_off), name
        assert sum(p.shape[1] for p in b_list) == n, name
        b_specs = [spec((tk, p.shape[1]), lambda i, j, kk: (kk, 0)) for p in b_list]
    elif tb:
        b_specs = [spec((tn, tk), lambda i, j, kk: (j + b_off[0], kk + b_off[1]))]
    else:
        b_specs = [spec((tk, tn), lambda i, j, kk: (kk + b_off[0], j + b_off[1]))]
    if n_a > 1 and nj == 1:
        b_specs = [pl.BlockSpec(b_specs[0].block_shape, b_specs[0].index_map, pipeline_mode=pl.Buffered(1))]
    in_specs = [*a_specs, *b_specs]
    for _, bs, im in ins:
        in_specs.append(spec(bs, lambda i, j, kk, im=im: im(i, j)))
    out_specs, out_shape = [], []
    for shape, dtype, bs, im in outs:
        out_specs.append(spec(bs, lambda i, j, kk, im=im: im(i, j)))
        out_shape.append(jax.ShapeDtypeStruct(shape, dtype))
    for shape, dtype in accs:
        out_specs.append(spec(shape, lambda i, j, kk, nd=len(shape): (0,) * nd))
        out_shape.append(jax.ShapeDtypeStruct(shape, dtype))
    n_in, n_out, n_acc = len(ins), len(outs), len(accs)
    ca, cb = (0 if ta else 1), (1 if tb else 0)
    grid = (*ij(ni, nj), nk)
    comm_ins, comm_outs, comm_scratch = [], [], []
    if comm is not None:
        comm_in_specs, comm_out_specs = comm.specs
        comm_ins, comm_outs, comm_scratch = list(comm.ins), list(comm.out_shape), list(comm.scratch)
        in_specs += comm_in_specs
        out_specs += comm_out_specs
        out_shape += comm_outs
    n_ci, n_co = len(comm_ins), len(comm_outs)

    def body(*refs):
        a_refs, b_refs, refs = refs[:n_a], refs[n_a:n_a + n_b], refs[n_a + n_b:]
        a_ref, b_ref = a_refs[0], b_refs[0]
        in_refs = refs[:n_in]
        ci_refs = refs[n_in:n_in + n_ci]
        rest = refs[n_in + n_ci:]
        out_refs, acc_refs = rest[:n_out], rest[n_out:n_out + n_acc]
        co_refs = rest[n_out + n_acc:n_out + n_acc + n_co]
        scratch = rest[n_out + n_acc + n_co:]
        sem_refs, scratch = scratch[:len(comm_scratch)], scratch[len(comm_scratch):]
        (i, j), kk = ij(pl.program_id(0), pl.program_id(1)), pl.program_id(2)
        if comm is not None:
            first_step, last_step = _grid_ends(grid)
            pl.when(first_step)(lambda: comm.start(ci_refs, co_refs, sem_refs))
        def finish(acc, cols=slice(None)):
            def of(r):
                return r[..., cols] if r.shape[-1] == tn else r[...]

            o_tiles, a_tiles = epi(acc, [of(r) for r in in_refs], i, j)
            for r, t in zip(out_refs, o_tiles, strict=True):
                r[..., cols] = t.astype(r.dtype)
            if n_acc:
                @pl.when(i == 0)
                def _():
                    for r, t in zip(acc_refs, a_tiles, strict=True):
                        r[..., cols] = t

                @pl.when(i > 0)
                def _():
                    for r, t in zip(acc_refs, a_tiles, strict=True):
                        r[..., cols] += t

        if chunk is not None:
            a_tile = a_ref[...].astype(BF16)

            def product(c0):
                b_part = b_ref[c0:c0 + chunk, :] if tb else b_ref[:, c0:c0 + chunk]
                return _dot(a_tile, b_part.astype(BF16), ca, cb)

            for c0 in range(0, tn // 2 if halves else tn, chunk):
                acc = (product(c0), product(tn // 2 + c0)) if halves else product(c0)
                finish(acc, slice(c0, c0 + chunk))
            if comm is not None:
                pl.when(last_step)(lambda: comm.finish(ci_refs, co_refs, sem_refs))
            return

        if n_b > 1:
            acc_ref = scratch[0]

            def accumulate(first):
                a_tile, c0 = a_ref[...].astype(BF16), 0
                for r in b_refs:
                    c1 = c0 + r.shape[1]
                    term = _dot(a_tile, r[...].astype(BF16), ca, cb)
                    acc_ref[:, c0:c1] = term if first else acc_ref[:, c0:c1] + term
                    c0 = c1

            pl.when(kk == 0)(lambda: accumulate(True))
            pl.when(kk > 0)(lambda: accumulate(False))
            pl.when(kk == nk - 1)(lambda: finish(acc_ref[...]))
            if comm is not None:
                pl.when(last_step)(lambda: comm.finish(ci_refs, co_refs, sem_refs))
            return

        if n_a > 1:
            part, c0 = None, 0
            for r in a_refs:
                c1 = c0 + r.shape[1]
                b_part = b_ref[:, c0:c1] if tb else b_ref[c0:c1, :]
                term = _dot(r[...].astype(BF16), b_part.astype(BF16), ca, cb)
                part, c0 = (term if part is None else part + term), c1
        else:
            part = _dot(a_ref[...].astype(BF16), b_ref[...].astype(BF16), ca, cb)
        if nk == 1:
            finish(part)
        else:
            acc_ref = scratch[0]

            @pl.when(kk == 0)
            def _():
                acc_ref[...] = part

            @pl.when(kk > 0)
            def _():
                acc_ref[...] += part

            @pl.when(kk == nk - 1)
            def _():
                finish(acc_ref[...])

        if comm is not None:
            pl.when(last_step)(lambda: comm.finish(ci_refs, co_refs, sem_refs))

    res = pl.pallas_call(
        body, name=name, grid=grid, in_specs=in_specs, out_specs=out_specs, out_shape=out_shape,
        scratch_shapes=comm_scratch + ([pltpu.VMEM((tm, tn), F32)] if nk > 1 else []),
        compiler_params=_cparams(),
    )(*a_list, *b_list, *[x for x, _, _ in ins], *comm_ins)
    return res


def _grid_ends(grid):
    ids = [pl.program_id(ax) for ax in range(len(grid))]
    first = functools.reduce(jnp.logical_and, [p == 0 for p in ids])
    last = functools.reduce(jnp.logical_and, [p == n - 1 for p, n in zip(ids, grid, strict=True)])
    return first, last


def _tile(tm, tn, dj=0):
    return (tm, tn), (lambda i, j: (i, j + dj))


def _rowvec(tn, dj=0):
    return (1, tn), (lambda i, j: (0, j + dj))


def _plain(acc, tiles, i, j):
    return [acc], []


def _ew(name, fn, ins, outs, rows, tr):
    assert rows % tr == 0, (name, rows, tr)
    in_specs = []
    for x in ins:
        if x.shape[0] == rows:
            in_specs.append(pl.BlockSpec((tr, x.shape[1]), lambda i: (i, 0)))
        else:
            in_specs.append(pl.BlockSpec(x.shape, lambda i: (0, 0)))
    n_in = len(ins)

    def body(*refs):
        res = fn(*[r[...] for r in refs[:n_in]])
        for r, t in zip(refs[n_in:], res, strict=True):
            r[...] = t.astype(r.dtype)

    return pl.pallas_call(
        body, name=name, grid=(rows // tr,), in_specs=in_specs,
        out_specs=[pl.BlockSpec((tr, w), lambda i: (i, 0)) for w, _ in outs],
        out_shape=[jax.ShapeDtypeStruct((rows, w), dt) for w, dt in outs],
        compiler_params=_cparams(),
    )(*ins)


def _cast_bf16(name, x):
    rows = x.shape[0]
    tr = next(t for t in (512, 256, 64) if rows % t == 0)
    return _ew(name, lambda v: (v,), [x], [(x.shape[1], BF16)], rows, tr)[0]


def _prep(x, comm):
    s = x.shape[0]
    half = RET_QK // 2
    inv = 1.0 / (ROPE_BASE ** (jnp.arange(half, dtype=F32) / half))
    inv2 = jnp.concatenate([inv, inv]).reshape(1, RET_QK)
    sign = jnp.concatenate([-jnp.ones((half,), F32), jnp.ones((half,), F32)]).reshape(1, RET_QK)
    tr = min(ROW_TILE, s)
    grid = (s // tr,)
    c_in_specs, c_out_specs, c_out_shape, c_scratch, c_ins, split = _host(comm, 3, 3)

    def body(*refs):
        (x_ref, inv_ref, sign_ref), (xb_ref, cos_ref, sin_ref), _, riding = split(refs)
        i = pl.program_id(0)
        first_step, last_step = _grid_ends(grid)
        pl.when(first_step)(lambda: comm.start(*riding))
        xb_ref[...] = x_ref[...].astype(BF16)
        pos = (lax.broadcasted_iota(jnp.int32, (tr, RET_QK), 0) + i * tr).astype(F32)
        ang = pos * inv_ref[...]
        cos_ref[...] = jnp.cos(ang)
        sin_ref[...] = jnp.sin(ang) * sign_ref[...]
        pl.when(last_step)(lambda: comm.finish(*riding))

    vec = pl.BlockSpec((1, RET_QK), lambda i: (0, 0))
    row = lambda w: pl.BlockSpec((tr, w), lambda i: (i, 0))
    return pl.pallas_call(
        body, name="prep", grid=grid,
        in_specs=[row(D_MODEL), vec, vec] + c_in_specs,
        out_specs=[row(D_MODEL), row(RET_QK), row(RET_QK)] + c_out_specs,
        out_shape=[jax.ShapeDtypeStruct((s, D_MODEL), BF16), jax.ShapeDtypeStruct((s, RET_QK), F32),
                   jax.ShapeDtypeStruct((s, RET_QK), F32)] + c_out_shape,
        scratch_shapes=c_scratch, compiler_params=_cparams(),
    )(x, inv2, sign, *c_ins)


def _swap_halves(x):
    return pltpu.roll(x, RET_QK // 2, 1)


def _norm(u):
    mu = jnp.mean(u, axis=-1, keepdims=True)
    d = u - mu
    var = jnp.mean(d * d, axis=-1, keepdims=True)
    rstd = lax.rsqrt(var + LN_EPS)
    return d * rstd, rstd


def _norm_bwd(dxh, xhat, rstd):
    m1 = jnp.mean(dxh, axis=-1, keepdims=True)
    m2 = jnp.mean(dxh * xhat, axis=-1, keepdims=True)
    return rstd * (dxh - m1 - xhat * m2)


def _colsum(t):
    return jnp.sum(t, axis=0, keepdims=True)


def _split_mm(t, tri):
    hi = t.astype(BF16)
    lo = (t - hi.astype(F32)).astype(BF16)
    return _dot(hi, tri, 1, 0) + _dot(lo, tri, 1, 0)


def _sb_masks():
    t = SB_BLOCK
    lane = lax.broadcasted_iota(jnp.int32, (1, LANES), 1)
    first = lane < SB_DIM
    m0 = jnp.where(first, 1.0, 0.0).astype(BF16)
    m1 = jnp.where(first, 0.0, 1.0).astype(BF16)
    row = lax.broadcasted_iota(jnp.int32, (t, t), 0)
    col = lax.broadcasted_iota(jnp.int32, (t, t), 1)
    return first, (m0, m1), row, col


def _sb_logits(qh, k, causal):
    z = _dot(qh, k, 1, 1)
    lp = jnp.log(1.0 + jnp.exp(-jnp.abs(z)))
    a = jnp.minimum(z, 0.0) - lp
    r = jnp.minimum(-z, 0.0) - lp
    if causal is not None:
        r = jnp.where(causal, r, 0.0)
    return a, r


def _sb_walk(i, blocks, l_ref, causal):
    pl.when(i == 0)(lambda: blocks([(i, causal)]))
    pl.when(i > 0)(lambda: blocks([(i, causal), (i - 1, None)]))

    def alive():
        top = jnp.max(functools.reduce(jnp.maximum, [l_ref[c] for c in range(l_ref.shape[0])]))
        return jnp.where(top > SB_DEAD, 1, 0)

    def cond(c):
        return jnp.logical_and(c[0] < i, c[1] > 0)

    def step(c):
        blocks([(i - 1 - c[0], None)])
        return c[0] + 1, alive()

    lax.while_loop(cond, step, (jnp.int32(1), alive()))


def _host(comm, n_in, n_out):
    if comm is None:
        return [], [], [], [], [], lambda refs: (refs[:n_in], refs[n_in:n_in + n_out], refs[n_in + n_out:], None)
    in_specs, out_specs = comm.specs
    n_ci, n_co, n_sem = len(comm.ins), len(comm.out_shape), len(comm.scratch)

    def split(refs):
        ins, ci = refs[:n_in], refs[n_in:n_in + n_ci]
        rest = refs[n_in + n_ci:]
        outs, co = rest[:n_out], rest[n_out:n_out + n_co]
        sems, scratch = rest[n_out + n_co:n_out + n_co + n_sem], rest[n_out + n_co + n_sem:]
        return ins, outs, scratch, (ci, co, sems)

    return in_specs, out_specs, list(comm.out_shape), list(comm.scratch), list(comm.ins), split


def _sb_qkv_specs(s, g):
    groups = SB_HEADS // 2 // g
    return [pl.BlockSpec((g, SB_BLOCK, LANES), lambda p, i: (p, i, 0)),
            pl.BlockSpec((g, s, LANES), lambda p, i: (groups + p, 0, 0)),
            pl.BlockSpec((g, s, LANES), lambda p, i: (2 * groups + p, 0, 0))]


def _sb_fwd(qkv, s, comm=None):
    t = SB_BLOCK
    g = 2
    nq = s // t
    grid = (SB_HEADS // 2 // g, nq)
    c_in_specs, c_out_specs, c_out_shape, c_scratch, c_ins, split = _host(comm, 3, 2)

    def body(*refs):
        (q_ref, k_ref, v_ref), (o_ref, of_ref), (l_ref, acc_ref), riding = split(refs)
        i = pl.program_id(1)
        if comm is not None:
            first_step, last_step = _grid_ends(grid)
            pl.when(first_step)(lambda: comm.start(*riding))
        first, hmask, row, col = _sb_masks()
        after = jnp.where(row > col, 1.0, 0.0).astype(BF16)
        causal = col < row
        heads = [(p, h) for p in range(g) for h in range(2)]
        qh = {(p, h): q_ref[p] * hmask[h] for p, h in heads}
        l_ref[...] = jnp.zeros_like(l_ref)
        acc_ref[...] = jnp.zeros_like(acc_ref)

        def blocks(todo):
            chains = [(b, p, h) for b in range(len(todo)) for p, h in heads]
            starts = [pl.multiple_of(kb * t, t) for kb, _ in todo]
            ks = {(b, p): k_ref[p, pl.ds(st, t), :] for b, st in enumerate(starts) for p in range(g)}
            vs = {(b, p): v_ref[p, pl.ds(st, t), :] for b, st in enumerate(starts) for p in range(g)}
            ar = {(b, p, h): _sb_logits(qh[p, h], ks[b, p], todo[b][1]) for b, p, h in chains}
            later = {c: _split_mm(ar[c][1], after) for c in chains}
            carry = {(p, h): l_ref[2 * p + h] for p, h in heads}
            w = {}
            for b, (_, mask) in enumerate(todo):
                for p, h in heads:
                    wc = jnp.exp(ar[b, p, h][0] + later[b, p, h] + carry[p, h])
                    w[b, p, h] = wc if mask is None else jnp.where(mask, wc, 0.0)
                carry = {(p, h): carry[p, h] + jnp.sum(ar[b, p, h][1], axis=1, keepdims=True) for p, h in heads}
            pv = {(b, p, h): _dot(w[b, p, h].astype(BF16), vs[b, p], 1, 0) for b, p, h in chains}
            for p in range(g):
                lanes = slice(p * LANES, (p + 1) * LANES)
                acc = acc_ref[:, lanes]
                for b in range(len(todo)):
                    acc = acc + jnp.where(first, pv[b, p, 0], pv[b, p, 1])
                acc_ref[:, lanes] = acc
            for p, h in heads:
                l_ref[2 * p + h] = carry[p, h]

        _sb_walk(i, blocks, l_ref, causal)
        o_ref[...] = acc_ref[...].astype(o_ref.dtype)
        of_ref[...] = acc_ref[...]
        if comm is not None:
            pl.when(last_step)(lambda: comm.finish(*riding))

    blk = pl.BlockSpec((t, g * LANES), lambda p, i: (i, p))
    return pl.pallas_call(
        body, name="sb_fwd", grid=grid,
        in_specs=_sb_qkv_specs(s, g) + c_in_specs,
        out_specs=[blk, blk] + c_out_specs,
        out_shape=[jax.ShapeDtypeStruct((s, SB_WIDTH), BF16), jax.ShapeDtypeStruct((s, SB_WIDTH), F32)] + c_out_shape,
        scratch_shapes=c_scratch + [pltpu.VMEM((2 * g, t, 1), F32), pltpu.VMEM((t, g * LANES), F32)],
        compiler_params=_cparams(),
    )(qkv, qkv, qkv, *c_ins)


def _sb_bwd(qkv, o, do, s, comm=None):
    t = SB_BLOCK
    g = 2
    nq = s // t
    grid = (SB_HEADS // 2 // g, nq)
    c_in_specs, c_out_specs, c_out_shape, c_scratch, c_ins, split = _host(comm, 5, 3)

    def body(*refs):
        ((q_ref, k_ref, v_ref, o_ref, do_ref), (dq_ref, dk_ref, dv_ref),
         (l_ref, e_ref, dq_acc, dk_acc, dv_acc), riding) = split(refs)
        i = pl.program_id(1)
        if comm is not None:
            first_step, last_step = _grid_ends(grid)
            pl.when(first_step)(lambda: comm.start(*riding))
        first, hmask, row, col = _sb_masks()
        after = jnp.where(row > col, 1.0, 0.0).astype(BF16)
        from_here = jnp.where(row >= col, 1.0, 0.0).astype(BF16)
        causal = col < row

        @pl.when(i == 0)
        def _():
            dk_acc[...] = jnp.zeros_like(dk_acc)
            dv_acc[...] = jnp.zeros_like(dv_acc)

        heads = [(p, h) for p in range(g) for h in range(2)]
        lanes = [slice(p * LANES, (p + 1) * LANES) for p in range(g)]
        q = [q_ref[p] for p in range(g)]
        do_ = [do_ref[:, lanes[p]] for p in range(g)]
        qh = {(p, h): q[p] * hmask[h] for p, h in heads}
        doh = {(p, h): do_[p] * hmask[h] for p, h in heads}
        total = {}
        for p in range(g):
            prod = do_[p].astype(F32) * o_ref[:, lanes[p]]
            total[p, 0] = jnp.sum(jnp.where(first, prod, 0.0), axis=1, keepdims=True)
            total[p, 1] = jnp.sum(jnp.where(first, 0.0, prod), axis=1, keepdims=True)
        l_ref[...] = jnp.zeros_like(l_ref)
        e_ref[...] = jnp.zeros_like(e_ref)
        dq_acc[...] = jnp.zeros_like(dq_acc)

        def blocks(todo):
            chains = [(b, p, h) for b in range(len(todo)) for p, h in heads]
            starts = [pl.multiple_of(kb * t, t) for kb, _ in todo]
            ks = {(b, p): k_ref[p, pl.ds(st, t), :] for b, st in enumerate(starts) for p in range(g)}
            vs = {(b, p): v_ref[p, pl.ds(st, t), :] for b, st in enumerate(starts) for p in range(g)}
            ar = {(b, p, h): _sb_logits(qh[p, h], ks[b, p], todo[b][1]) for b, p, h in chains}
            dw = {(b, p, h): _dot(doh[p, h], vs[b, p], 1, 1) for b, p, h in chains}
            later = {c: _split_mm(ar[c][1], after) for c in chains}
            carry = {(p, h): l_ref[2 * p + h] for p, h in heads}
            wb = {}
            for b, (_, mask) in enumerate(todo):
                for p, h in heads:
                    wc = jnp.exp(ar[b, p, h][0] + later[b, p, h] + carry[p, h])
                    wb[b, p, h] = (wc if mask is None else jnp.where(mask, wc, 0.0)).astype(BF16)
                carry = {(p, h): carry[p, h] + jnp.sum(ar[b, p, h][1], axis=1, keepdims=True) for p, h in heads}
            dvs = {(b, p, h): _dot(wb[b, p, h], do_[p], 0, 0) for b, p, h in chains}
            e = {c: dw[c] * wb[c].astype(F32) for c in chains}
            suffix = {c: _split_mm(e[c], from_here) for c in chains}
            e_carry = {(p, h): e_ref[2 * p + h] for p, h in heads}
            dz = {}
            for b, (_, mask) in enumerate(todo):
                for p, h in heads:
                    before = total[p, h] - (suffix[b, p, h] + e_carry[p, h])
                    dzc = e[b, p, h] - jnp.exp(ar[b, p, h][0]) * (e[b, p, h] + before)
                    dz[b, p, h] = (dzc if mask is None else jnp.where(mask, dzc, 0.0)).astype(BF16)
                e_carry = {(p, h): e_carry[p, h] + jnp.sum(e[b, p, h], axis=1, keepdims=True) for p, h in heads}
            dqs = {(b, p, h): _dot(dz[b, p, h], ks[b, p], 1, 0) for b, p, h in chains}
            dks = {(b, p, h): _dot(dz[b, p, h], q[p], 0, 0) for b, p, h in chains}
            for p in range(g):
                dq = dq_acc[:, lanes[p]]
                for b, st in enumerate(starts):
                    dq = dq + jnp.where(first, dqs[b, p, 0], dqs[b, p, 1])
                    dk_acc[pl.ds(st, t), lanes[p]] += jnp.where(first, dks[b, p, 0], dks[b, p, 1])
                    dv_acc[pl.ds(st, t), lanes[p]] += jnp.where(first, dvs[b, p, 0], dvs[b, p, 1])
                dq_acc[:, lanes[p]] = dq
            for p, h in heads:
                l_ref[2 * p + h] = carry[p, h]
                e_ref[2 * p + h] = e_carry[p, h]

        _sb_walk(i, blocks, l_ref, causal)
        dq_ref[...] = (dq_acc[...] * SB_SCALE).astype(dq_ref.dtype)

        @pl.when(i == nq - 1)
        def _():
            dk_ref[...] = dk_acc[...].astype(dk_ref.dtype)
            dv_ref[...] = dv_acc[...].astype(dv_ref.dtype)

        if comm is not None:
            pl.when(last_step)(lambda: comm.finish(*riding))

    once = pl.Buffered(1)
    q_spec, k_spec, v_spec = _sb_qkv_specs(s, g)
    k_spec = pl.BlockSpec(k_spec.block_shape, k_spec.index_map, pipeline_mode=once)
    v_spec = pl.BlockSpec(v_spec.block_shape, v_spec.index_map, pipeline_mode=once)
    blk = pl.BlockSpec((t, g * LANES), lambda p, i: (i, p))
    col_blk = pl.BlockSpec((s, g * LANES), lambda p, i: (0, p), pipeline_mode=once)
    sds = jax.ShapeDtypeStruct((s, SB_WIDTH), BF16)
    return pl.pallas_call(
        body, name="sb_bwd", grid=grid,
        in_specs=[q_spec, k_spec, v_spec, blk, blk] + c_in_specs,
        out_specs=[blk, col_blk, col_blk] + c_out_specs,
        out_shape=[sds, sds, sds] + c_out_shape,
        scratch_shapes=c_scratch + [pltpu.VMEM((2 * g, t, 1), F32), pltpu.VMEM((2 * g, t, 1), F32),
                                    pltpu.VMEM((t, g * LANES), F32), pltpu.VMEM((s, g * LANES), F32),
                                    pltpu.VMEM((s, g * LANES), F32)],
        compiler_params=_cparams(),
    )(qkv, qkv, qkv, o, do, *c_ins)


def _ret_log_gamma():
    lg = np.log1p(-np.exp2(-5.0 - np.arange(RET_HEADS, dtype=np.float32))).astype(np.float32)
    return jnp.asarray(np.broadcast_to(lg[:, None, None], (RET_HEADS, 8, LANES)).copy())


RET_SCRATCH = [pltpu.VMEM((RET_HEADS, RET_QK, RET_V), F32),
               pltpu.VMEM((RET_HEADS, RET_BLOCK, RET_BLOCK), F32),
               pltpu.VMEM((RET_HEADS, RET_BLOCK, 1), F32),
               pltpu.VMEM((RET_HEADS, RET_BLOCK, 1), F32)]


def _ret_begin(n, lg_ref, state, within, q_dec, k_dec):
    @pl.when(n == 0)
    def _():
        c = RET_BLOCK
        state[...] = jnp.zeros_like(state)
        row = lax.broadcasted_iota(jnp.int32, (c, c), 0)
        col = lax.broadcasted_iota(jnp.int32, (c, c), 1)
        rel = jnp.maximum(row - col, 0).astype(F32)
        idx = lax.broadcasted_iota(jnp.int32, (c, 1), 0).astype(F32)
        for h in range(RET_HEADS):
            lg = lg_ref[h, 0:1, 0:1]
            within[h] = jnp.where(row >= col, jnp.exp(lg * rel), 0.0)
            q_dec[h] = jnp.exp(lg * (idx + 1.0))
            k_dec[h] = jnp.exp(lg * (c - 1.0 - idx))


def _chunk_decay(lg_ref, h):
    return jnp.exp(lg_ref[h, 0:1, 0:1] * float(RET_BLOCK))


def _ret_heads(x, width):
    return [x[:, h * width:(h + 1) * width] for h in range(RET_HEADS)]


def _ret_specs(s, reverse=False):
    c = RET_BLOCK
    per_step = min(RET_CHUNKS_PER_STEP, s // c)
    rows = c * per_step
    nc = s // rows
    pos = (lambda n: nc - 1 - n) if reverse else (lambda n: n)
    chunks = [slice(u * c, (u + 1) * c) for u in range(per_step)]
    q_spec = pl.BlockSpec((rows, RET_QK_WIDTH), lambda n: (pos(n), 0))
    k_spec = pl.BlockSpec((rows, RET_QK_WIDTH), lambda n: (pos(n), 1))
    v_spec = pl.BlockSpec((rows, RET_V_WIDTH), lambda n: (pos(n), 0))
    lg_spec = pl.BlockSpec((RET_HEADS, 8, LANES), lambda n: (0, 0, 0))
    rope_spec = pl.BlockSpec((rows, RET_QK), lambda n: (pos(n), 0))
    return nc, chunks[::-1] if reverse else chunks, q_spec, k_spec, v_spec, lg_spec, rope_spec


def _ret_fwd(rqk, rvg, s):
    nc, chunks, q_spec, k_spec, v_spec, lg_spec, _ = _ret_specs(s)
    g_spec = pl.BlockSpec(v_spec.block_shape, lambda n: (n, 1))
    heads = range(RET_HEADS)

    def body(q_ref, k_ref, v_ref, g_ref, lg_ref, r_ref, y_ref, state, within, q_dec, k_dec):
        n = pl.program_id(0)
        _ret_begin(n, lg_ref, state, within, q_dec, k_dec)
        for rows in chunks:
            q, k = _ret_heads(q_ref[rows], RET_QK), _ret_heads(k_ref[rows], RET_QK)
            v, g = _ret_heads(v_ref[rows], RET_V), _ret_heads(g_ref[rows], RET_V)
            scores = [_dot(q[h].astype(BF16), k[h].astype(BF16), 1, 1) * within[h] for h in heads]
            cross = [_dot((q[h] * q_dec[h]).astype(BF16), state[h].astype(BF16), 1, 0) for h in heads]
            out = [_dot(scores[h].astype(BF16), v[h], 1, 0) + cross[h] for h in heads]
            grown = [_dot((k[h] * k_dec[h]).astype(BF16), v[h], 0, 0) for h in heads]
            for h in heads:
                sl = slice(h * RET_V, (h + 1) * RET_V)
                r_ref[rows, sl] = out[h]
                xhat, _ = _norm(out[h])
                gh = g[h].astype(F32)
                y_ref[rows, sl] = (gh * _sigmoid(gh) * xhat).astype(y_ref.dtype)
                state[h] = state[h] * _chunk_decay(lg_ref, h) + grown[h]

    return pl.pallas_call(
        body, name="ret_fwd", grid=(nc,),
        in_specs=[q_spec, k_spec, v_spec, g_spec, lg_spec],
        out_specs=[v_spec, v_spec],
        out_shape=[jax.ShapeDtypeStruct((s, RET_V_WIDTH), F32), jax.ShapeDtypeStruct((s, RET_V_WIDTH), BF16)],
        scratch_shapes=RET_SCRATCH,
        compiler_params=_cparams(),
    )(rqk, rqk, rvg, rvg, _ret_log_gamma())


def _rope_bwd(d, cos, sin):
    return d * cos + _swap_halves(d * sin)


def _ret_bwd_q(rqk, rv, d_out, cos2, sin2, s):
    nc, chunks, q_spec, k_spec, v_spec, lg_spec, rope_spec = _ret_specs(s)
    heads = range(RET_HEADS)

    def body(k_ref, v_ref, d_ref, lg_ref, cos_ref, sin_ref, dq_ref, state, within, q_dec, k_dec):
        n = pl.program_id(0)
        _ret_begin(n, lg_ref, state, within, q_dec, k_dec)
        for rows in chunks:
            k = _ret_heads(k_ref[rows], RET_QK)
            v, d = _ret_heads(v_ref[rows], RET_V), _ret_heads(d_ref[rows], RET_V)
            cos, sin = cos_ref[rows], sin_ref[rows]
            d_scores = [_dot(d[h], v[h], 1, 1) * within[h] for h in heads]
            cross = [q_dec[h] * _dot(d[h], state[h].astype(BF16), 1, 1) for h in heads]
            dq = [_dot(d_scores[h].astype(BF16), k[h].astype(BF16), 1, 0) + cross[h] for h in heads]
            grown = [_dot((k[h] * k_dec[h]).astype(BF16), v[h], 0, 0) for h in heads]
            for h in heads:
                sl = slice(h * RET_QK, (h + 1) * RET_QK)
                dq_ref[rows, sl] = (_rope_bwd(dq[h], cos, sin) * RET_SCALE).astype(dq_ref.dtype)
                state[h] = state[h] * _chunk_decay(lg_ref, h) + grown[h]

    return pl.pallas_call(
        body, name="ret_bwd_q", grid=(nc,),
        in_specs=[k_spec, v_spec, v_spec, lg_spec, rope_spec, rope_spec],
        out_specs=q_spec,
        out_shape=jax.ShapeDtypeStruct((s, RET_QK_WIDTH), BF16),
        scratch_shapes=RET_SCRATCH,
        compiler_params=_cparams(),
    )(rqk, rv, d_out, _ret_log_gamma(), cos2, sin2)


def _ret_bwd_kv(rqk, rv, d_out, cos2, sin2, s):
    nc, chunks, q_spec, k_spec, v_spec, lg_spec, rope_spec = _ret_specs(s, reverse=True)
    heads = range(RET_HEADS)

    def body(q_ref, k_ref, v_ref, d_ref, lg_ref, cos_ref, sin_ref, dk_ref, dv_ref, state, within, q_dec, k_dec):
        n = pl.program_id(0)
        _ret_begin(n, lg_ref, state, within, q_dec, k_dec)
        for rows in chunks:
            q, k = _ret_heads(q_ref[rows], RET_QK), _ret_heads(k_ref[rows], RET_QK)
            v, d = _ret_heads(v_ref[rows], RET_V), _ret_heads(d_ref[rows], RET_V)
            cos, sin = cos_ref[rows], sin_ref[rows]
            qb, kb = [q[h].astype(BF16) for h in heads], [k[h].astype(BF16) for h in heads]
            st = [state[h].astype(BF16) for h in heads]
            scores = [_dot(qb[h], kb[h], 1, 1) * within[h] for h in heads]
            d_scores = [_dot(d[h], v[h], 1, 1) * within[h] for h in heads]
            dk = [_dot(d_scores[h].astype(BF16), qb[h], 0, 0) + k_dec[h] * _dot(v[h], st[h], 1, 1) for h in heads]
            dv = [_dot(scores[h].astype(BF16), d[h], 0, 0) + k_dec[h] * _dot(kb[h], st[h], 1, 0) for h in heads]
            grown = [_dot((q[h] * q_dec[h]).astype(BF16), d[h], 0, 0) for h in heads]
            for h in heads:
                dk_ref[rows, h * RET_QK:(h + 1) * RET_QK] = _rope_bwd(dk[h], cos, sin).astype(dk_ref.dtype)
                dv_ref[rows, h * RET_V:(h + 1) * RET_V] = dv[h].astype(dv_ref.dtype)
                state[h] = state[h] * _chunk_decay(lg_ref, h) + grown[h]

    return pl.pallas_call(
        body, name="ret_bwd_kv", grid=(nc,),
        in_specs=[q_spec, k_spec, v_spec, v_spec, lg_spec, rope_spec, rope_spec],
        out_specs=[q_spec, v_spec],
        out_shape=[jax.ShapeDtypeStruct((s, RET_QK_WIDTH), BF16), jax.ShapeDtypeStruct((s, RET_V_WIDTH), BF16)],
        scratch_shapes=RET_SCRATCH,
        compiler_params=_cparams(),
    )(rqk, rqk, rv, d_out, _ret_log_gamma(), cos2, sin2)


def _xattn_probs(scores):
    sc = scores - jnp.max(scores, axis=-1, keepdims=True)
    p = jnp.exp(sc)
    return p / jnp.sum(p, axis=-1, keepdims=True)


def _xattn_heads(q_ref, kv_ref):
    sls = [slice(h * MEM_DIM, (h + 1) * MEM_DIM) for h in range(MEM_HEADS)]
    q = [q_ref[:, sl] for sl in sls]
    k = [kv_ref[:, sl] for sl in sls]
    v = [kv_ref[:, D_MODEL + h * MEM_DIM:D_MODEL + (h + 1) * MEM_DIM] for h in range(MEM_HEADS)]
    return sls, q, k, v


def _xattn_fwd(qm, kv, s):
    tq = min(XATTN_ROWS, s)
    heads = range(MEM_HEADS)

    def body(q_ref, kv_ref, o_ref):
        sls, q, k, v = _xattn_heads(q_ref, kv_ref)
        scores = [_dot(q[h], k[h], 1, 1) for h in heads]
        p = [_xattn_probs(scores[h]).astype(BF16) for h in heads]
        out = [_dot(p[h], v[h], 1, 0) for h in heads]
        for h in heads:
            o_ref[:, sls[h]] = out[h].astype(o_ref.dtype)

    return pl.pallas_call(
        body, name="xattn_fwd", grid=(s // tq,),
        in_specs=[pl.BlockSpec((tq, D_MODEL), lambda i: (i, 0)),
                  pl.BlockSpec((MEM_LEN, 2 * D_MODEL), lambda i: (0, 0))],
        out_specs=pl.BlockSpec((tq, D_MODEL), lambda i: (i, 0)),
        out_shape=jax.ShapeDtypeStruct((s, D_MODEL), BF16),
        compiler_params=_cparams(),
    )(qm, kv)


def _xattn_bwd(qm, kv, do, s):
    tq = min(XATTN_ROWS, s)

    def body(q_ref, kv_ref, do_ref, dq_ref, dkv_ref):
        i = pl.program_id(0)

        @pl.when(i == 0)
        def _():
            dkv_ref[...] = jnp.zeros_like(dkv_ref)

        heads = range(MEM_HEADS)
        sls, q, k, v = _xattn_heads(q_ref, kv_ref)
        d = [do_ref[:, sl] for sl in sls]
        scores = [_dot(q[h], k[h], 1, 1) for h in heads]
        dp = [_dot(d[h], v[h], 1, 1) for h in heads]
        p = [_xattn_probs(scores[h]) for h in heads]
        ds = [(p[h] * (dp[h] - jnp.sum(p[h] * dp[h], axis=-1, keepdims=True))).astype(BF16) for h in heads]
        dq = [_dot(ds[h], k[h], 1, 0) for h in heads]
        dk = [_dot(ds[h], q[h], 0, 0) for h in heads]
        dv = [_dot(p[h].astype(BF16), d[h], 0, 0) for h in heads]
        for h in heads:
            dq_ref[:, sls[h]] = (dq[h] * MEM_SCALE).astype(dq_ref.dtype)
            dkv_ref[:, sls[h]] += dk[h]
            dkv_ref[:, D_MODEL + h * MEM_DIM:D_MODEL + (h + 1) * MEM_DIM] += dv[h]

    row_blk = pl.BlockSpec((tq, D_MODEL), lambda i: (i, 0))
    kv_blk = pl.BlockSpec((MEM_LEN, 2 * D_MODEL), lambda i: (0, 0))
    return pl.pallas_call(
        body, name="xattn_bwd", grid=(s // tq,),
        in_specs=[row_blk, kv_blk, row_blk],
        out_specs=[row_blk, kv_blk],
        out_shape=[jax.ShapeDtypeStruct((s, D_MODEL), BF16), jax.ShapeDtypeStruct((MEM_LEN, 2 * D_MODEL), F32)],
        compiler_params=_cparams(),
    )(qm, kv, do)


def _place():
    x, y, c = lax.axis_index("x"), lax.axis_index("y"), lax.axis_index("c")
    others = [(1 - x, y), (x, 1 - y), (1 - x, 1 - y)]
    return x, y, c, others


def _slab(ref, axis, chip, size):
    start = pl.multiple_of(chip * size, LANES if axis == 1 else 16)
    if axis == 0:
        return ref.at[pl.ds(start, size), :]
    return ref.at[:, pl.ds(start, size)]


class _CommPlan:
    def __init__(self, ins, out_shape, scratch, start, finish):
        self.ins, self.out_shape, self.scratch, self.start, self.finish = ins, out_shape, scratch, start, finish

    @property
    def specs(self):
        any_spec = pl.BlockSpec(memory_space=pl.ANY)
        return [any_spec] * len(self.ins), [any_spec] * len(self.out_shape)


def _gather_plan(names, shards):
    spec = {name: (shape, axis) for name, shape, axis in BIG}
    nw = len(names)

    def shard_half(ref, c):
        rows = ref.shape[0] // 2
        return ref.at[pl.ds(pl.multiple_of(c * rows, 16), rows), :]

    def region(ref, w, chip, c):
        shape, axis = spec[names[w]]
        size = shape[axis] // N_CHIPS
        if axis == 0:
            rows = size // 2
            return ref.at[pl.ds(pl.multiple_of(chip * size + c * rows, 16), rows), :]
        rows = shape[0] // 2
        return ref.at[pl.ds(pl.multiple_of(c * rows, 16), rows), pl.ds(pl.multiple_of(chip * size, LANES), size)]

    def ops(shard, full, sems):
        ici_send, ici_recv, d2d_send, d2d_recv, local_sems = sems
        x, y, c, others = _place()
        mine, sibling = 2 * x + y, (x, y, 1 - c)
        local, over_ici, arrived, passed_on, from_sibling = [], [], [], [], []
        for w in range(nw):
            shape, axis = spec[names[w]]
            local.append(pltpu.make_async_copy(shard[w], _slab(full[w], axis, mine, shape[axis] // N_CHIPS),
                                               local_sems.at[w]))
            for t, (qx, qy) in enumerate(others):
                n, theirs = 3 * w + t, 2 * qx + qy
                over_ici.append(pltpu.make_async_remote_copy(
                    src_ref=shard_half(shard[w], c), dst_ref=region(full[w], w, mine, c),
                    send_sem=ici_send.at[n], recv_sem=ici_recv.at[n], device_id=(qx, qy, c), device_id_type=MESH))
                arrived.append(pltpu.make_async_remote_copy(
                    src_ref=shard_half(shard[w], c), dst_ref=region(full[w], w, theirs, c),
                    send_sem=ici_send.at[n], recv_sem=ici_recv.at[n], device_id=(qx, qy, c), device_id_type=MESH))
                passed_on.append(pltpu.make_async_remote_copy(
                    src_ref=region(full[w], w, theirs, c), dst_ref=region(full[w], w, theirs, c),
                    send_sem=d2d_send.at[n], recv_sem=d2d_recv.at[n], device_id=sibling, device_id_type=MESH))
                from_sibling.append(pltpu.make_async_remote_copy(
                    src_ref=region(full[w], w, theirs, c), dst_ref=region(full[w], w, theirs, 1 - c),
                    send_sem=d2d_send.at[n], recv_sem=d2d_recv.at[n], device_id=sibling, device_id_type=MESH))
        return local, over_ici, arrived, passed_on, from_sibling

    def start(shard, full, sems):
        local, over_ici, _, _, _ = ops(shard, full, sems)
        for cp in local + over_ici:
            cp.start()

    def finish(shard, full, sems):
        local, over_ici, arrived, passed_on, from_sibling = ops(shard, full, sems)
        for got, onward in zip(arrived, passed_on, strict=True):
            got.wait_recv()
            onward.start()
        for got in from_sibling:
            got.wait_recv()
        for cp in over_ici + passed_on:
            cp.wait_send()
        for cp in local:
            cp.wait()

    dma = pltpu.SemaphoreType.DMA
    return _CommPlan(
        ins=[shards[name] for name in names],
        out_shape=[jax.ShapeDtypeStruct(spec[name][0], BF16) for name in names],
        scratch=[dma((3 * nw,)), dma((3 * nw,)), dma((3 * nw,)), dma((3 * nw,)), dma((nw,))],
        start=start, finish=finish)


def _shard_shape(shape, axis):
    return tuple(d // N_CHIPS if a == axis else d for a, d in enumerate(shape))


def _exchange_plan(names, grads):
    spec = {name: (shape, axis) for name, shape, axis in BIG}
    nw = len(names)

    def ops(grad, stack, sems):
        send_sems, recv_sems, local_sems = sems
        x, y, c, others = _place()
        mine = 2 * x + y
        me, sibling = (x, y, c), (x, y, 1 - c)

        def dev(px, py, pc):
            return 4 * px + 2 * py + pc

        def copy(w, n, src, slot, to):
            return pltpu.make_async_remote_copy(
                src_ref=src, dst_ref=stack[w].at[slot], send_sem=send_sems.at[7 * w + n],
                recv_sem=recv_sems.at[7 * w + n], device_id=to, device_id_type=MESH)

        local, first, arrived, passed_on, from_sibling = [], [], [], [], []
        for w in range(nw):
            shape, axis = spec[names[w]]
            size = shape[axis] // N_CHIPS
            own = _slab(grad[w], axis, mine, size)
            local.append(pltpu.make_async_copy(own, stack[w].at[dev(*me)], local_sems.at[w]))
            first.append(copy(w, 0, own, dev(*me), sibling))
            from_sibling.append(copy(w, 0, own, dev(*sibling), me))
            for t, (qx, qy) in enumerate(others):
                got = stack[w].at[dev(qx, qy, c)]
                first.append(copy(w, 1 + t, _slab(grad[w], axis, 2 * qx + qy, size), dev(*me), (qx, qy, c)))
                arrived.append(copy(w, 1 + t, got, dev(qx, qy, c), me))
                passed_on.append(copy(w, 4 + t, got, dev(qx, qy, c), sibling))
                from_sibling.append(copy(w, 4 + t, got, dev(qx, qy, 1 - c), me))
        return local, first, arrived, passed_on, from_sibling

    def start(grad, stack, sems):
        local, first, _, _, _ = ops(grad, stack, sems)
        for cp in local + first:
            cp.start()

    def finish(grad, stack, sems):
        local, first, arrived, passed_on, from_sibling = ops(grad, stack, sems)
        for got, onward in zip(arrived, passed_on, strict=True):
            got.wait_recv()
            onward.start()
        for got in from_sibling:
            got.wait_recv()
        for cp in first + passed_on:
            cp.wait_send()
        for cp in local:
            cp.wait()

    dma = pltpu.SemaphoreType.DMA
    return _CommPlan(
        ins=[grads[name] for name in names],
        out_shape=[jax.ShapeDtypeStruct((N_DEV,) + _shard_shape(*spec[name]), BF16) for name in names],
        scratch=[dma((7 * nw,)), dma((7 * nw,)), dma((nw,))],
        start=start, finish=finish)


def _adamw(w, g, m, v):
    m = ADAM_B1 * m + (1.0 - ADAM_B1) * g
    v = ADAM_B2 * v + (1.0 - ADAM_B2) * (g * g)
    m_hat = m / (1.0 - ADAM_B1 ** ADAM_STEP)
    v_hat = v / (1.0 - ADAM_B2 ** ADAM_STEP)
    delta = -ADAM_LR * (m_hat / (jnp.sqrt(v_hat) + ADAM_EPS) + ADAM_WD * w)
    return delta, m, v


def _reduce_adamw(name, stack, w, m, v):
    rows, cols = w.shape
    tr = next(t for t in (256, 128, 64) if rows % t == 0)

    def body(s_ref, w_ref, m_ref, v_ref, g_ref, d_ref, nm_ref, nv_ref):
        g = s_ref[0].astype(F32)
        for d in range(1, N_DEV):
            g = g + s_ref[d].astype(F32)
        g_ref[...] = g
        d_ref[...], nm_ref[...], nv_ref[...] = _adamw(w_ref[...], g, m_ref[...], v_ref[...])

    blk = pl.BlockSpec((tr, cols), lambda i: (i, 0))
    return pl.pallas_call(
        body, name=name, grid=(rows // tr,),
        in_specs=[pl.BlockSpec((N_DEV, tr, cols), lambda i: (0, i, 0)), blk, blk, blk],
        out_specs=[blk] * 4, out_shape=[jax.ShapeDtypeStruct((rows, cols), F32)] * 4,
        compiler_params=_cparams(),
    )(stack, w, m, v)


def _small_step(pack, w, m, v):
    def body(p_ref, w_ref, m_ref, v_ref, g_ref, d_ref, nm_ref, nv_ref, loss_ref, all_ref, send_sems, recv_sems):
        x, y, c, _ = _place()
        me = 4 * x + 2 * y + c
        all_ref[me] = p_ref[...]
        sent = []
        for n in range(1, N_DEV):
            peer = me ^ n
            cp = pltpu.make_async_remote_copy(
                src_ref=p_ref, dst_ref=all_ref.at[me], send_sem=send_sems.at[n - 1], recv_sem=recv_sems.at[n - 1],
                device_id=(peer // 4, (peer // 2) % 2, peer % 2), device_id_type=MESH)
            cp.start()
            sent.append(cp)
        for n in range(1, N_DEV):
            peer = me ^ n
            pltpu.make_async_remote_copy(
                src_ref=p_ref, dst_ref=all_ref.at[peer], send_sem=send_sems.at[n - 1], recv_sem=recv_sems.at[n - 1],
                device_id=(peer // 4, (peer // 2) % 2, peer % 2), device_id_type=MESH).wait_recv()
        for cp in sent:
            cp.wait_send()
        tot = all_ref[0]
        for d in range(1, N_DEV):
            tot = tot + all_ref[d]
        g = tot[:SMALL_ROWS]
        g_ref[...] = g
        d_ref[...], nm_ref[...], nv_ref[...] = _adamw(w_ref[...], g, m_ref[...], v_ref[...])
        loss_ref[...] = jnp.sum(jnp.sum(tot[SMALL_ROWS:], axis=1, keepdims=True), axis=0, keepdims=True)

    vm = pl.BlockSpec(memory_space=pltpu.VMEM)
    small = jax.ShapeDtypeStruct((SMALL_ROWS, LANES), F32)
    return pl.pallas_call(
        body, name="small_step",
        in_specs=[vm] * 4, out_specs=[vm] * 5,
        out_shape=[small] * 4 + [jax.ShapeDtypeStruct((1, 1), F32)],
        scratch_shapes=[pltpu.VMEM((N_DEV, PACK_ROWS, LANES), F32),
                        pltpu.SemaphoreType.DMA((N_DEV - 1,)), pltpu.SemaphoreType.DMA((N_DEV - 1,))],
    )(pack, w, m, v)


LATER_WEIGHTS = tuple(name for name, _, _ in BIG if name != "w_in")


def _layer_step(x, mem, tgt, shards, vec):
    s = x.shape[0]
    d = D_MODEL
    tm = min(ROW_TILE, s)
    tl = min(WIDE_TILE, s)
    xb, cos2, sin2, w_in = _prep(x, _gather_plan(("w_in",), shards))
    bf = lambda w: ((s, w), BF16)
    f32 = lambda w: ((s, w), F32)

    w_sb, w_rqk = w_in[:, :OFF_RET_Q], w_in[:, OFF_RET_Q:OFF_RET_V]
    w_rvg, w_gate = w_in[:, OFF_RET_V:OFF_GATE], w_in[:, OFF_GATE:]
    q_scale = lambda width, q_width, scale: jnp.concatenate(
        [jnp.full((1, q_width), scale, F32), jnp.ones((1, width - q_width), F32)], axis=1)
    n_groups = 3 * SB_WIDTH // LANES

    def sb_epi(acc, t, i, j):
        scaled = acc * t[0]
        return [jnp.stack([scaled[:, g * LANES:(g + 1) * LANES] for g in range(n_groups)])], []

    (sb_qkv,) = _mm(
        "in_sb", xb, w_sb, s, 3 * SB_WIDTH, d, tm=tl, tn=3 * SB_WIDTH, tk=d, epi=sb_epi,
        ins=[(q_scale(3 * SB_WIDTH, SB_WIDTH, SB_SCALE), *_rowvec(3 * SB_WIDTH))],
        outs=[((n_groups, s, LANES), BF16, (n_groups, tl, LANES), lambda i, j: (0, i, 0))])

    def rope_epi(acc, t, i, j):
        cos, sin, scale = t
        parts = []
        for g in range(acc.shape[1] // RET_QK):
            xg = acc[:, g * RET_QK:(g + 1) * RET_QK]
            parts.append(xg * cos + _swap_halves(xg) * sin)
        return [jnp.concatenate(parts, axis=1) * scale], []

    rope_in = ((tl, RET_QK), lambda i, j: (i, 0))
    (rqk,) = _mm("in_rqk", xb, w_rqk, s, 2 * RET_QK_WIDTH, d, tm=tl, tn=2 * RET_QK_WIDTH, tk=d, epi=rope_epi,
                 chunk=MXU_COLS,
                 ins=[(cos2, *rope_in), (sin2, *rope_in),
                      (q_scale(2 * RET_QK_WIDTH, RET_QK_WIDTH, RET_SCALE), *_rowvec(2 * RET_QK_WIDTH))],
                 outs=[(*f32(2 * RET_QK_WIDTH), *_tile(tl, 2 * RET_QK_WIDTH))])
    (rvg,) = _mm("in_rvg", xb, w_rvg, s, 2 * RET_V_WIDTH, d, tm=tl, tn=2 * RET_V_WIDTH, tk=d, chunk=MXU_COLS,
                 epi=_plain, outs=[(*bf(2 * RET_V_WIDTH), *_tile(tl, 2 * RET_V_WIDTH))])
    (gates,) = _mm("in_gate", xb, w_gate, s, 2 * d, d, tm=tl, tn=2 * d, tk=d, chunk=MXU_COLS,
                   epi=lambda acc, t, i, j: ([_sigmoid(acc + t[0])], []),
                   ins=[(vec["b_gate"], *_rowvec(2 * d))], outs=[(*bf(2 * d), *_tile(tl, 2 * d))])

    sb_out, sb_out_f32, *gathered = _sb_fwd(sb_qkv, s, comm=_gather_plan(LATER_WEIGHTS, shards))
    wt = dict(zip(LATER_WEIGHTS, gathered, strict=True))
    ret, gated = _ret_fwd(rqk, rvg, s)
    (y_sb,) = _mm("sb_o", sb_out, wt["w_sb_o"], s, d, SB_WIDTH, tm=tl, tn=d, tk=SB_WIDTH, epi=_plain,
                  outs=[(*bf(d), *_tile(tl, d))])
    y_ret, mixin = _mm(
        "ret_o", gated, wt["w_ret_o"], s, d, RET_V_WIDTH, tm=tl, tn=d, tk=RET_V_WIDTH, chunk=MXU_COLS,
        epi=lambda acc, t, i, j: ([acc, t[0].astype(F32) * t[2].astype(F32) + t[1].astype(F32) * acc], []),
        ins=[(gates, *_tile(tl, d)), (gates, *_tile(tl, d, 1)), (y_sb, *_tile(tl, d))],
        outs=[(*bf(d), *_tile(tl, d)), (*bf(d), *_tile(tl, d))])

    def ln_epi(acc, t, i, j):
        *res, g, b = t
        prev = res[0] if len(res) == 1 else res[0] * res[1] + res[2]
        xhat, rstd = _norm(DN_ALPHA * prev + acc)
        return [xhat * g + b, xhat, rstd], []

    full = _tile(tm, d)
    col1 = ((tm, 1), lambda i, j: (i, 0))
    vec_in = lambda name: (vec[name], *_rowvec(d))
    ln_outs = [(*bf(d), *full), (*f32(d), *full), ((s, 1), F32, *col1)]
    def ln_query_epi(acc, t, i, j):
        *rest, w_query = t
        (x1, xhat, rstd), _ = ln_epi(acc, rest, i, j)
        x1 = x1.astype(BF16)
        return [x1, xhat, rstd, _dot(x1, w_query, 1, 0) * MEM_SCALE], []

    x1b, xhat1, rstd1, qm = _mm(
        "mix_o", mixin, wt["w_mix_o"], s, d, d, tm=tm, tn=d, tk=d, epi=ln_query_epi,
        ins=[(x, *full), vec_in("ln1_g"), vec_in("ln1_b"), (wt["w_mem_q"], (d, d), lambda i, j: (0, 0))],
        outs=ln_outs + [(*bf(d), *full)])
    (kv,) = _mm("mem_kv", mem, wt["w_mem_kv"], MEM_LEN, 2 * d, d, tm=MEM_LEN, tn=d, tk=d, epi=_plain,
                outs=[((MEM_LEN, 2 * d), BF16, *_tile(MEM_LEN, d))])
    att = _xattn_fwd(qm, kv, s)
    x2b, xhat2, rstd2 = _mm(
        "mem_o", att, wt["w_mem_o"], s, d, d, tm=tm, tn=d, tk=d, epi=ln_epi,
        ins=[(xhat1, *full), vec_in("ln1_g"), vec_in("ln1_b"), vec_in("ln2_g"), vec_in("ln2_b")], outs=ln_outs)

    fh = FFN_HIDDEN
    tf = fh // 2
    def swiglu_epi(acc, t, i, j):
        a = acc[0].astype(BF16).astype(F32)
        return [a, acc[1], a * _sigmoid(a) * acc[1]], []

    f1, f2, act = _mm(
        "ffn_in", x2b, wt["w_ffn_in"], s, 2 * fh, d, tm=tm, tn=2 * fh, tk=d, epi=swiglu_epi, chunk=MXU_COLS,
        halves=True, outs=[(*bf(fh), *_tile(tm, fh))] * 3)

    def head_epi(acc, t, i, j):
        prev_hat, prev_g, prev_b, g, b, target = t
        xhat, rstd = _norm(DN_ALPHA * (prev_hat * prev_g + prev_b) + acc)
        err = xhat * g + b - target
        dy = err * (1.0 / d)
        du = _norm_bwd(dy * g, xhat, rstd)
        return [du], [_colsum(dy * xhat), _colsum(dy), _colsum(err * err) * (0.5 / d)]

    vec_acc = ((1, d), F32)
    du3b, dg3, db3, loss_cols = _mm(
        "ffn_out", act, wt["w_ffn_out"], s, d, fh, tm=tm, tn=d, tk=fh, epi=head_epi,
        ins=[(xhat2, *full), vec_in("ln2_g"), vec_in("ln2_b"), vec_in("ln3_g"), vec_in("ln3_b"), (tgt, *full)],
        outs=[(*bf(d), *full)], accs=[vec_acc] * 3)

    grads = {}
    ts = min(SEQ_TILE, s)

    def wgrad(name, a, b, m, n, tm_, tn_, tk_=None):
        (g,) = _mm(name, a, b, m, n, a.shape[0], tm=tm_, tn=tn_, tk=tk_ or ts, ta=True, epi=_plain,
                   outs=[((m, n), BF16, *_tile(tm_, tn_))])
        return g

    def wgrad_wide(name, a, pieces, tm_):
        m, width = a.shape[1], sum(p.shape[1] for p in pieces)
        (g,) = _mm(name, a, pieces, m, width, s, tm=tm_, tn=width, tk=min(ROW_TILE, s // 2), ta=True, epi=_plain,
                   outs=[((m, width), BF16, *_tile(tm_, width))])
        return g

    def ffn_bwd_epi(acc, t, i, j):
        a, b = t[0].astype(F32), t[1].astype(F32)
        sg = _sigmoid(a)
        return [acc * b * (sg * (1.0 + a * (1.0 - sg))), acc * (a * sg)], []

    df1, df2 = _mm(
        "ffn_out_t", du3b, wt["w_ffn_out"], s, fh, d, tm=tm, tn=fh, tk=d, tb=True, epi=ffn_bwd_epi, chunk=MXU_COLS,
        ins=[(f1, *_tile(tm, fh)), (f2, *_tile(tm, fh))],
        outs=[(*bf(fh), *_tile(tm, fh)), (*bf(fh), *_tile(tm, fh))])
    grads["w_ffn_out"] = wgrad("g_ffn_out", act, du3b, fh, d, tf, d)
    grads["w_ffn_in"] = wgrad_wide("g_ffn_in", x2b, [df1, df2], d // 2)

    def ln_bwd(name, a, b, k, tk, b_off, more, scales, xhat, rstd, g):
        def epi(acc, t, i, j):
            *extra, xh, rs, gg = t
            dy = acc
            for e, sc in zip(extra, scales, strict=True):
                dy = dy + e.astype(F32) * sc
            return [_norm_bwd(dy * gg, xh, rs)], [_colsum(dy * xh), _colsum(dy)]

        return _mm(name, a, b, s, d, k, tm=tm, tn=d, tk=tk, tb=True, b_off=b_off, epi=epi,
                   ins=[(e, *full) for e in more] + [(xhat, *full), (rstd, *col1), (g, *_rowvec(d))],
                   outs=[(*bf(d), *full)], accs=[vec_acc] * 2)

    du2b, dg2, db2 = ln_bwd("ffn_in_t", [df1, df2], wt["w_ffn_in"], 2 * fh, 2 * fh, (0, 0), [du3b], [DN_ALPHA],
                            xhat2, rstd2, vec["ln2_g"])

    (datt,) = _mm("mem_o_t", du2b, wt["w_mem_o"], s, d, d, tm=tl, tn=d, tk=d, tb=True, epi=_plain,
                  outs=[(*bf(d), *_tile(tl, d))])
    grads["w_mem_o"] = wgrad("g_mem_o", att, du2b, d, d, d, d)
    dqm, dkv = _xattn_bwd(qm, kv, datt, s)
    grads["w_mem_q"] = wgrad("g_mem_q", x1b, dqm, d, d, d, d)
    grads["w_mem_kv"] = wgrad("g_mem_kv", mem, dkv, d, 2 * d, d, d, MEM_LEN)
    du1b, dg1, db1 = ln_bwd("mem_q_t", dqm, wt["w_mem_q"], d, d, (0, 0), [du2b], [DN_ALPHA],
                            xhat1, rstd1, vec["ln1_g"])

    def merge_bwd_epi(acc, t, i, j):
        g0, g1, ysb, yret = (v.astype(F32) for v in t[:4])
        dgate0 = acc * ysb * (g0 * (1.0 - g0))
        dgate1 = acc * yret * (g1 * (1.0 - g1))
        dy_sb = (acc * g0).astype(BF16)
        return [dgate0, dgate1, dy_sb, acc * g1, _dot(dy_sb, t[4], 1, 1)], [_colsum(dgate0), _colsum(dgate1)]

    dgate0, dgate1, dy_sb, dy_ret, dsb_out, dbg0, dbg1 = _mm(
        "mix_o_t", du1b, wt["w_mix_o"], s, d, d, tm=tm, tn=d, tk=d, tb=True, epi=merge_bwd_epi,
        ins=[(gates, *full), (gates, *_tile(tm, d, 1)), (y_sb, *full), (y_ret, *full),
             (wt["w_sb_o"], (SB_WIDTH, d), lambda i, j: (0, 0))],
        outs=[(*bf(d), *full)] * 4 + [(*bf(SB_WIDTH), *_tile(tm, SB_WIDTH))], accs=[vec_acc] * 2)
    grads["w_mix_o"] = wgrad("g_mix_o", mixin, du1b, d, d, d, d)
    grads["w_sb_o"] = wgrad("g_sb_o", sb_out, dy_sb, SB_WIDTH, d, SB_WIDTH, d)
    grads["w_ret_o"] = wgrad("g_ret_o", gated, dy_ret, RET_V_WIDTH, d, RET_V_WIDTH, d)

    def gate_norm_bwd_epi(acc, t, i, j):
        r, g = t[0], t[1].astype(F32)
        drg, dret = [], []
        for h in range(acc.shape[1] // RET_V):
            sl = slice(h * RET_V, (h + 1) * RET_V)
            xhat, rstd = _norm(r[:, sl])
            gg, dd = g[:, sl], acc[:, sl]
            sg = _sigmoid(gg)
            drg.append(dd * xhat * (sg * (1.0 + gg * (1.0 - sg))))
            dret.append(_norm_bwd(dd * (gg * sg), xhat, rstd))
        return [jnp.concatenate(drg, axis=1), jnp.concatenate(dret, axis=1)], []

    drg, dret = _mm(
        "ret_o_t", dy_ret, wt["w_ret_o"], s, RET_V_WIDTH, d, tm=tm, tn=d, tk=d, tb=True, epi=gate_norm_bwd_epi,
        chunk=MXU_COLS,
        ins=[(ret, *full), (rvg, *_tile(tm, d, 1))],
        outs=[(*bf(RET_V_WIDTH), *full)] * 2)

    drq = _ret_bwd_q(rqk, rvg, dret, cos2, sin2, s)
    drk, drv = _ret_bwd_kv(rqk, rvg, dret, cos2, sin2, s)
    dsq, dsk, dsv, *stacked = _sb_bwd(sb_qkv, sb_out_f32, dsb_out, s, comm=_exchange_plan(LATER_WEIGHTS, grads))
    stacks = dict(zip(LATER_WEIGHTS, stacked, strict=True))

    dh_mixers, dh_gates = [dsq, dsk, dsv, drq, drk, drv], [drg, dgate0, dgate1]
    grads["w_in"] = wgrad_wide("g_in", xb, dh_mixers + dh_gates, d // 2)
    grad_x, stacks["w_in"] = _mm(
        "in_t", dh_mixers + dh_gates, w_in, s, d, IN_WIDTH, tm=tm, tn=d, tk=IN_WIDTH, tb=True,
        epi=lambda acc, t, i, j: ([acc + DN_ALPHA * t[0].astype(F32)], []),
        ins=[(du1b, *full)], outs=[(*f32(d), *full)], comm=_exchange_plan(("w_in",), grads))

    small = {"b_gate": jnp.concatenate([dbg0, dbg1], axis=1), "ln1_g": dg1, "ln1_b": db1, "ln2_g": dg2,
             "ln2_b": db2, "ln3_g": dg3, "ln3_b": db3}
    return grad_x, stacks, small, loss_cols


def kernel(x, mem, w_in, b_gate, w_sb_o, w_ret_o, w_mix_o, ln1_g, ln1_b, w_mem_q, w_mem_kv, w_mem_o, ln2_g, ln2_b, w_ffn_in, w_ffn_out, ln3_g, ln3_b, loss_target, m_w_in, m_b_gate, m_w_sb_o, m_w_ret_o, m_w_mix_o, m_ln1_g, m_ln1_b, m_w_mem_q, m_w_mem_kv, m_w_mem_o, m_ln2_g, m_ln2_b, m_w_ffn_in, m_w_ffn_out, m_ln3_g, m_ln3_b, v_w_in, v_b_gate, v_w_sb_o, v_w_ret_o, v_w_mix_o, v_ln1_g, v_ln1_b, v_w_mem_q, v_w_mem_kv, v_w_mem_o, v_ln2_g, v_ln2_b, v_w_ffn_in, v_w_ffn_out, v_ln3_g, v_ln3_b):
    given = dict(locals())
    s = x.shape[1]
    x2d = x.reshape(s, D_MODEL)
    tgt = loss_target.reshape(s, D_MODEL)
    mem2d = mem.reshape(MEM_LEN, D_MODEL)
    shard = {name: given[name].reshape(_shard_shape(shape, axis)) for name, shape, axis in BIG}
    vec = {name: given[name] for name in SMALL}

    shards_bf = {name: _cast_bf16("cast_" + name, shard[name]) for name, _, _ in BIG}

    grad_x, stacks, small, loss_cols = _layer_step(x2d, mem2d, tgt, shards_bf, vec)

    out = {}
    for name, shape, axis in BIG:
        stack = stacks[name]
        shp = given[name].shape
        res = _reduce_adamw("adamw_" + name, stack, shard[name], given["m_" + name].reshape(stack.shape[1:]),
                            given["v_" + name].reshape(stack.shape[1:]))
        out[name] = [r.reshape(shp) for r in res]

    pack = jnp.concatenate([small[name] for name in SMALL] + [loss_cols], axis=1).reshape(PACK_ROWS, LANES)
    cat = lambda pre: jnp.concatenate([given[pre + name] for name in SMALL], axis=1).reshape(SMALL_ROWS, LANES)
    *res, loss = _small_step(pack, cat(""), cat("m_"), cat("v_"))
    flat = [r.reshape(1, SMALL_LEN) for r in res]
    off = 0
    for name in SMALL:
        n = given[name].shape[1]
        out[name] = [r[:, off:off + n] for r in flat]
        off += n

    return (loss.reshape(()), grad_x.reshape(x.shape),
            *[out[name][0] for name in WEIGHT_ORDER], *[out[name][1] for name in WEIGHT_ORDER],
            *[out[name][2] for name in WEIGHT_ORDER], *[out[name][3] for name in WEIGHT_ORDER])
```
